```python
import math
import jax, jax.numpy as jnp
from jax import lax
import numpy as np

D_MODEL = 1024
BATCH = 8
SEQ = 2048
DEPTH = 2

N_A_LAYERS = DEPTH // 2
N_B_LAYERS = DEPTH - N_A_LAYERS
S5_GROUP = 16
S5_GROUPS = D_MODEL // S5_GROUP
S5_STATE = 64
DT_MIN = 1e-3
DT_MAX = 1e-1
LAMBDA_RE_MAX = -1e-4
HEAD_DIM = 64
N_Q_HEADS = D_MODEL // HEAD_DIM
N_KV_HEADS = 4
Q_PER_KV = N_Q_HEADS // N_KV_HEADS
WINDOW = 128
BLOCK = 128
D_FF = 4 * D_MODEL
NORM_EPS = 1e-5

kernel_name = "yoco_s5_swa_sink_hybrid"


def rmsnorm(x, g):
    x32 = x.astype(jnp.float32)
    y = x32 * lax.rsqrt(jnp.mean(x32 * x32, axis=-1, keepdims=True) + NORM_EPS)
    return (y * g.astype(jnp.float32)).astype(x.dtype)


def s5_mixer(u, a_re, a_im, log_dt, b_re, b_im, c_re, c_im, d_skip, w_glu, b_glu):
    bsz, seq, _ = u.shape
    f32 = jnp.float32
    u32 = u.astype(f32)
    ug = u32.reshape(bsz, seq, S5_GROUPS, S5_GROUP)
    lam = lax.complex(jnp.minimum(a_re.astype(f32), LAMBDA_RE_MAX), a_im.astype(f32))
    dt = jnp.exp(log_dt.astype(f32))[:, None]
    lam_bar = jnp.exp(lam * dt)
    b_c = lax.complex(b_re.astype(f32), b_im.astype(f32))
    b_bar = ((lam_bar - 1.0) / lam)[:, :, None] * b_c
    bu = lax.complex(jnp.einsum('blgc,gpc->blgp', ug, jnp.real(b_bar)),
                     jnp.einsum('blgc,gpc->blgp', ug, jnp.imag(b_bar)))
    a = jnp.broadcast_to(lam_bar[None, None], (1, seq, S5_GROUPS, S5_STATE))

    def combine(left, right):
        a_l, b_l = left
        a_r, b_r = right
        return a_r * a_l, a_r * b_l + b_r

    _, states = lax.associative_scan(combine, (a, bu), axis=1)
    y = (jnp.einsum('blgp,gcp->blgc', jnp.real(states), c_re.astype(f32))
         - jnp.einsum('blgp,gcp->blgc', jnp.imag(states), c_im.astype(f32)))
    y = y.reshape(bsz, seq, D_MODEL) + d_skip.astype(f32) * u32
    y = jax.nn.gelu(y).astype(u.dtype)
    z = y @ w_glu + b_glu
    val, gate = jnp.split(z, 2, axis=-1)
    return val * jax.nn.sigmoid(gate)


def shared_kv(h, g_kv, w_kv, b_kv):
    bsz, seq, _ = h.shape
    kv = rmsnorm(h, g_kv) @ w_kv + b_kv
    k, v = jnp.split(kv, 2, axis=-1)
    return (k.reshape(bsz, seq, N_KV_HEADS, HEAD_DIM),
            v.reshape(bsz, seq, N_KV_HEADS, HEAD_DIM))


def window_blocks(t, nb):
    bsz = t.shape[0]
    tb = t.reshape(bsz, nb, BLOCK, N_KV_HEADS, HEAD_DIM)
    prev = jnp.concatenate([jnp.zeros_like(tb[:, :1]), tb[:, :-1]], axis=1)
    return jnp.moveaxis(jnp.concatenate([prev, tb], axis=2), 1, 0)


def swa_sink_attention(hn, k, v, w_q, b_q, sinks, w_o, b_o):
    bsz, seq, _ = hn.shape
    nb = seq // BLOCK
    f32 = jnp.float32
    scale = 1.0 / math.sqrt(HEAD_DIM)
    q = (hn @ w_q + b_q).reshape(bsz, nb, BLOCK, N_KV_HEADS, Q_PER_KV, HEAD_DIM)
    q = jnp.moveaxis(q, 1, 0)
    kw = window_blocks(k, nb)
    vw = window_blocks(v, nb)
    qi = jnp.arange(BLOCK)[:, None]
    kj = jnp.arange(2 * BLOCK)[None, :]
    diff = qi + BLOCK - kj
    band = (diff >= 0) & (diff < WINDOW)
    sink = sinks.astype(f32).reshape(N_KV_HEADS, Q_PER_KV)[None, :, :, None]

    def block_fn(args):
        n, qb, kb, vb = args
        s = jnp.einsum('bqkgd,bskd->bkgqs', qb.astype(f32), kb.astype(f32)) * scale
        valid = band & ((n - 1) * BLOCK + kj >= 0)
        s = jnp.where(valid, s, -jnp.inf)
        m = jnp.maximum(jnp.max(s, axis=-1), sink)
        p = jnp.exp(s - m[..., None])
        denom = jnp.sum(p, axis=-1) + jnp.exp(sink - m)
        o = jnp.einsum('bkgqs,bskd->bqkgd', p / denom[..., None], vb.astype(f32))
        return o.astype(hn.dtype)

    o = lax.map(block_fn, (jnp.arange(nb), q, kw, vw))
    o = jnp.moveaxis(o, 0, 1).reshape(bsz, seq, N_Q_HEADS * HEAD_DIM)
    return o @ w_o + b_o


def sq_relu_mlp(h, w_in, w_out):
    return jnp.square(jax.nn.relu(h @ w_in)) @ w_out


def _fwd_setup_inputs(seed: int = 0) -> dict:
    key = jax.random.key(seed)
    ks = jax.random.split(key, 26)
    f32 = jnp.float32
    nrm = lambda k, shape, s: jax.random.normal(k, shape, f32) * s
    G, P, GC, D = S5_GROUPS, S5_STATE, S5_GROUP, D_MODEL
    HQ = N_Q_HEADS * HEAD_DIM
    HKV = N_KV_HEADS * HEAD_DIM
    x = jax.random.normal(ks[0], (BATCH, SEQ, D), f32)
    norm_mix = 1.0 + nrm(ks[1], (DEPTH, D), 0.02)
    norm_mlp = 1.0 + nrm(ks[2], (DEPTH, D), 0.02)
    norm_kv = 1.0 + nrm(ks[3], (D,), 0.02)
    norm_final = 1.0 + nrm(ks[4], (D,), 0.02)
    s5_a_re = -0.5 + nrm(ks[5], (N_A_LAYERS, G, P), 0.01)
    s5_a_im = (jnp.pi * jnp.arange(P, dtype=f32))[None, None, :] + nrm(ks[6], (N_A_LAYERS, G, P), 0.01)
    s5_log_dt = jax.random.uniform(ks[7], (N_A_LAYERS, G), f32, math.log(DT_MIN), math.log(DT_MAX))
    s5_b_re = nrm(ks[8], (N_A_LAYERS, G, P, GC), (2.0 * GC) ** -0.5)
    s5_b_im = nrm(ks[9], (N_A_LAYERS, G, P, GC), (2.0 * GC) ** -0.5)
    s5_c_re = nrm(ks[10], (N_A_LAYERS, G, GC, P), P ** -0.5)
    s5_c_im = nrm(ks[11], (N_A_LAYERS, G, GC, P), P ** -0.5)
    s5_d = nrm(ks[12], (N_A_LAYERS, D), 1.0)
    s5_w_glu = nrm(ks[13], (N_A_LAYERS, D, 2 * D), D ** -0.5)
    s5_b_glu = nrm(ks[14], (N_A_LAYERS, 2 * D), 0.01)
    w_kv = nrm(ks[15], (D, 2 * HKV), D ** -0.5)
    b_kv = nrm(ks[16], (2 * HKV,), 0.01)
    w_q = nrm(ks[17], (N_B_LAYERS, D, HQ), D ** -0.5)
    b_q = nrm(ks[18], (N_B_LAYERS, HQ), 0.01)
    sinks = nrm(ks[19], (N_B_LAYERS, N_Q_HEADS), 0.5)
    w_o = nrm(ks[20], (N_B_LAYERS, HQ, D), HQ ** -0.5)
    b_o = nrm(ks[21], (N_B_LAYERS, D), 0.01)
    w_mlp_in = nrm(ks[22], (DEPTH, D, D_FF), D ** -0.5)
    w_mlp_out = nrm(ks[23], (DEPTH, D_FF, D), D_FF ** -0.5)
    return {"x": x, "norm_mix": norm_mix, "norm_mlp": norm_mlp, "norm_kv": norm_kv,
            "norm_final": norm_final, "s5_a_re": s5_a_re, "s5_a_im": s5_a_im,
            "s5_log_dt": s5_log_dt, "s5_b_re": s5_b_re, "s5_b_im": s5_b_im,
            "s5_c_re": s5_c_re, "s5_c_im": s5_c_im, "s5_d": s5_d, "s5_w_glu": s5_w_glu,
            "s5_b_glu": s5_b_glu, "w_kv": w_kv, "b_kv": b_kv, "w_q": w_q, "b_q": b_q,
            "sinks": sinks, "w_o": w_o, "b_o": b_o, "w_mlp_in": w_mlp_in,
            "w_mlp_out": w_mlp_out}


def _fwd_reference(x, norm_mix, norm_mlp, norm_kv, norm_final, s5_a_re, s5_a_im, s5_log_dt,
              s5_b_re, s5_b_im, s5_c_re, s5_c_im, s5_d, s5_w_glu, s5_b_glu, w_kv, b_kv,
              w_q, b_q, sinks, w_o, b_o, w_mlp_in, w_mlp_out):
    h = x
    k = v = None
    for layer in range(DEPTH):
        if layer < N_A_LAYERS:
            hn = rmsnorm(h, norm_mix[layer])
            h = h + s5_mixer(hn, s5_a_re[layer], s5_a_im[layer], s5_log_dt[layer],
                             s5_b_re[layer], s5_b_im[layer], s5_c_re[layer], s5_c_im[layer],
                             s5_d[layer], s5_w_glu[layer], s5_b_glu[layer])
        else:
            if layer == N_A_LAYERS:
                k, v = shared_kv(h, norm_kv, w_kv, b_kv)
            bi = layer - N_A_LAYERS
            hn = rmsnorm(h, norm_mix[layer])
            h = h + swa_sink_attention(hn, k, v, w_q[bi], b_q[bi], sinks[bi], w_o[bi], b_o[bi])
        h = h + sq_relu_mlp(rmsnorm(h, norm_mlp[layer]), w_mlp_in[layer], w_mlp_out[layer])
    return rmsnorm(h, norm_final)


import jax as _jax
import jax.numpy as _jnp

TWIN_FORMAT = 'train_step'
FWD_PARAMS = ['x', 'norm_mix', 'norm_mlp', 'norm_kv', 'norm_final', 's5_a_re', 's5_a_im', 's5_log_dt', 's5_b_re', 's5_b_im', 's5_c_re', 's5_c_im', 's5_d', 's5_w_glu', 's5_b_glu', 'w_kv', 'b_kv', 'w_q', 'b_q', 'sinks', 'w_o', 'b_o', 'w_mlp_in', 'w_mlp_out']
TWIN_WEIGHTS = ['norm_mix', 'norm_mlp', 'norm_kv', 'norm_final', 's5_a_re', 's5_a_im', 's5_log_dt', 's5_b_re', 's5_b_im', 's5_c_re', 's5_c_im', 's5_d', 's5_w_glu', 's5_b_glu', 'w_kv', 'b_kv', 'w_q', 'b_q', 'sinks', 'w_o', 'b_o', 'w_mlp_in', 'w_mlp_out']
TWIN_DIFF_INPUT = 'x'
TWIN_INPUTS = ['x', 'norm_mix', 'norm_mlp', 'norm_kv', 'norm_final', 's5_a_re', 's5_a_im', 's5_log_dt', 's5_b_re', 's5_b_im', 's5_c_re', 's5_c_im', 's5_d', 's5_w_glu', 's5_b_glu', 'w_kv', 'b_kv', 'w_q', 'b_q', 'sinks', 'w_o', 'b_o', 'w_mlp_in', 'w_mlp_out', 'loss_target', 'm_norm_mix', 'm_norm_mlp', 'm_norm_kv', 'm_norm_final', 'm_s5_a_re', 'm_s5_a_im', 'm_s5_log_dt', 'm_s5_b_re', 'm_s5_b_im', 'm_s5_c_re', 'm_s5_c_im', 'm_s5_d', 'm_s5_w_glu', 'm_s5_b_glu', 'm_w_kv', 'm_b_kv', 'm_w_q', 'm_b_q', 'm_sinks', 'm_w_o', 'm_b_o', 'm_w_mlp_in', 'm_w_mlp_out', 'v_norm_mix', 'v_norm_mlp', 'v_norm_kv', 'v_norm_final', 'v_s5_a_re', 'v_s5_a_im', 'v_s5_log_dt', 'v_s5_b_re', 'v_s5_b_im', 'v_s5_c_re', 'v_s5_c_im', 'v_s5_d', 'v_s5_w_glu', 'v_s5_b_glu', 'v_w_kv', 'v_b_kv', 'v_w_q', 'v_b_q', 'v_sinks', 'v_w_o', 'v_b_o', 'v_w_mlp_in', 'v_w_mlp_out']
TWIN_OUTPUTS = ['loss', 'grad_x', 'grad_norm_mix', 'grad_norm_mlp', 'grad_norm_kv', 'grad_norm_final', 'grad_s5_a_re', 'grad_s5_a_im', 'grad_s5_log_dt', 'grad_s5_b_re', 'grad_s5_b_im', 'grad_s5_c_re', 'grad_s5_c_im', 'grad_s5_d', 'grad_s5_w_glu', 'grad_s5_b_glu', 'grad_w_kv', 'grad_b_kv', 'grad_w_q', 'grad_b_q', 'grad_sinks', 'grad_w_o', 'grad_b_o', 'grad_w_mlp_in', 'grad_w_mlp_out', 'delta_norm_mix', 'delta_norm_mlp', 'delta_norm_kv', 'delta_norm_final', 'delta_s5_a_re', 'delta_s5_a_im', 'delta_s5_log_dt', 'delta_s5_b_re', 'delta_s5_b_im', 'delta_s5_c_re', 'delta_s5_c_im', 'delta_s5_d', 'delta_s5_w_glu', 'delta_s5_b_glu', 'delta_w_kv', 'delta_b_kv', 'delta_w_q', 'delta_b_q', 'delta_sinks', 'delta_w_o', 'delta_b_o', 'delta_w_mlp_in', 'delta_w_mlp_out', 'new_m_norm_mix', 'new_m_norm_mlp', 'new_m_norm_kv', 'new_m_norm_final', 'new_m_s5_a_re', 'new_m_s5_a_im', 'new_m_s5_log_dt', 'new_m_s5_b_re', 'new_m_s5_b_im', 'new_m_s5_c_re', 'new_m_s5_c_im', 'new_m_s5_d', 'new_m_s5_w_glu', 'new_m_s5_b_glu', 'new_m_w_kv', 'new_m_b_kv', 'new_m_w_q', 'new_m_b_q', 'new_m_sinks', 'new_m_w_o', 'new_m_b_o', 'new_m_w_mlp_in', 'new_m_w_mlp_out', 'new_v_norm_mix', 'new_v_norm_mlp', 'new_v_norm_kv', 'new_v_norm_final', 'new_v_s5_a_re', 'new_v_s5_a_im', 'new_v_s5_log_dt', 'new_v_s5_b_re', 'new_v_s5_b_im', 'new_v_s5_c_re', 'new_v_s5_c_im', 'new_v_s5_d', 'new_v_s5_w_glu', 'new_v_s5_b_glu', 'new_v_w_kv', 'new_v_b_kv', 'new_v_w_q', 'new_v_b_q', 'new_v_sinks', 'new_v_w_o', 'new_v_b_o', 'new_v_w_mlp_in', 'new_v_w_mlp_out']
TWIN_LEAF_KINDS = {'loss': 'loss', 'grad_x': 'grad_x', 'grad_norm_mix': 'grad_w', 'grad_norm_mlp': 'grad_w', 'grad_norm_kv': 'grad_w', 'grad_norm_final': 'grad_w', 'grad_s5_a_re': 'grad_w', 'grad_s5_a_im': 'grad_w', 'grad_s5_log_dt': 'grad_w', 'grad_s5_b_re': 'grad_w', 'grad_s5_b_im': 'grad_w', 'grad_s5_c_re': 'grad_w', 'grad_s5_c_im': 'grad_w', 'grad_s5_d': 'grad_w', 'grad_s5_w_glu': 'grad_w', 'grad_s5_b_glu': 'grad_w', 'grad_w_kv': 'grad_w', 'grad_b_kv': 'grad_w', 'grad_w_q': 'grad_w', 'grad_b_q': 'grad_w', 'grad_sinks': 'grad_w', 'grad_w_o': 'grad_w', 'grad_b_o': 'grad_w', 'grad_w_mlp_in': 'grad_w', 'grad_w_mlp_out': 'grad_w', 'delta_norm_mix': 'delta_w', 'delta_norm_mlp': 'delta_w', 'delta_norm_kv': 'delta_w', 'delta_norm_final': 'delta_w', 'delta_s5_a_re': 'delta_w', 'delta_s5_a_im': 'delta_w', 'delta_s5_log_dt': 'delta_w', 'delta_s5_b_re': 'delta_w', 'delta_s5_b_im': 'delta_w', 'delta_s5_c_re': 'delta_w', 'delta_s5_c_im': 'delta_w', 'delta_s5_d': 'delta_w', 'delta_s5_w_glu': 'delta_w', 'delta_s5_b_glu': 'delta_w', 'delta_w_kv': 'delta_w', 'delta_b_kv': 'delta_w', 'delta_w_q': 'delta_w', 'delta_b_q': 'delta_w', 'delta_sinks': 'delta_w', 'delta_w_o': 'delta_w', 'delta_b_o': 'delta_w', 'delta_w_mlp_in': 'delta_w', 'delta_w_mlp_out': 'delta_w', 'new_m_norm_mix': 'new_m', 'new_m_norm_mlp': 'new_m', 'new_m_norm_kv': 'new_m', 'new_m_norm_final': 'new_m', 'new_m_s5_a_re': 'new_m', 'new_m_s5_a_im': 'new_m', 'new_m_s5_log_dt': 'new_m', 'new_m_s5_b_re': 'new_m', 'new_m_s5_b_im': 'new_m', 'new_m_s5_c_re': 'new_m', 'new_m_s5_c_im': 'new_m', 'new_m_s5_d': 'new_m', 'new_m_s5_w_glu': 'new_m', 'new_m_s5_b_glu': 'new_m', 'new_m_w_kv': 'new_m', 'new_m_b_kv': 'new_m', 'new_m_w_q': 'new_m', 'new_m_b_q': 'new_m', 'new_m_sinks': 'new_m', 'new_m_w_o': 'new_m', 'new_m_b_o': 'new_m', 'new_m_w_mlp_in': 'new_m', 'new_m_w_mlp_out': 'new_m', 'new_v_norm_mix': 'new_v', 'new_v_norm_mlp': 'new_v', 'new_v_norm_kv': 'new_v', 'new_v_norm_final': 'new_v', 'new_v_s5_a_re': 'new_v', 'new_v_s5_a_im': 'new_v', 'new_v_s5_log_dt': 'new_v', 'new_v_s5_b_re': 'new_v', 'new_v_s5_b_im': 'new_v', 'new_v_s5_c_re': 'new_v', 'new_v_s5_c_im': 'new_v', 'new_v_s5_d': 'new_v', 'new_v_s5_w_glu': 'new_v', 'new_v_s5_b_glu': 'new_v', 'new_v_w_kv': 'new_v', 'new_v_b_kv': 'new_v', 'new_v_w_q': 'new_v', 'new_v_b_q': 'new_v', 'new_v_sinks': 'new_v', 'new_v_w_o': 'new_v', 'new_v_b_o': 'new_v', 'new_v_w_mlp_in': 'new_v', 'new_v_w_mlp_out': 'new_v'}


def _forward(args):
    return _fwd_reference(*[args[k] for k in FWD_PARAMS])


def _output_shape():
    out = _jax.eval_shape(lambda: _forward(_fwd_setup_inputs(0)))
    return out.shape, out.dtype

N_MICROBATCH = 1
ADAM_LR = 0.001
ADAM_B1 = 0.9
ADAM_B2 = 0.999
ADAM_EPS = 1e-08
ADAM_WD = 0.01
ADAM_STEP = 10
PER_EXAMPLE_BATCH_AXIS = {'x': 0, 'loss_target': 0}
SHARED_INPUTS = []
_WEIGHT_DTYPES = {'norm_mix': _jnp.float32, 'norm_mlp': _jnp.float32, 'norm_kv': _jnp.float32, 'norm_final': _jnp.float32, 's5_a_re': _jnp.float32, 's5_a_im': _jnp.float32, 's5_log_dt': _jnp.float32, 's5_b_re': _jnp.float32, 's5_b_im': _jnp.float32, 's5_c_re': _jnp.float32, 's5_c_im': _jnp.float32, 's5_d': _jnp.float32, 's5_w_glu': _jnp.float32, 's5_b_glu': _jnp.float32, 'w_kv': _jnp.float32, 'b_kv': _jnp.float32, 'w_q': _jnp.float32, 'b_q': _jnp.float32, 'sinks': _jnp.float32, 'w_o': _jnp.float32, 'b_o': _jnp.float32, 'w_mlp_in': _jnp.float32, 'w_mlp_out': _jnp.float32}
MOMENT_SCALE = {'norm_mix': 4.641567e-02, 'norm_mlp': 1.040618e-01, 'norm_kv': 3.686399e-02, 'norm_final': 1.652504e+01, 's5_a_re': 3.334422e-03, 's5_a_im': 3.686216e-03, 's5_log_dt': 1.768181e+00, 's5_b_re': 2.196486e-03, 's5_b_im': 2.196650e-03, 's5_c_re': 3.157099e-03, 's5_c_im': 3.183585e-03, 's5_d': 5.626145e-02, 's5_w_glu': 3.805204e-02, 's5_b_glu': 7.629246e-02, 'w_kv': 5.848630e-02, 'b_kv': 1.788762e-01, 'w_q': 1.233668e-02, 'b_q': 1.223764e-02, 'sinks': 1.492095e-02, 'w_o': 4.506545e-02, 'b_o': 1.319298e-01, 'w_mlp_in': 5.135164e-02, 'w_mlp_out': 1.032441e-01}


def _to_microbatches(a, axis):
    t = _jnp.moveaxis(a, axis, 0)
    t = t.reshape((N_MICROBATCH, t.shape[0] // N_MICROBATCH) + t.shape[1:])
    return _jnp.moveaxis(t, 1, axis + 1)


def setup_inputs(seed: int = 0) -> dict:
    inp = _fwd_setup_inputs(seed)
    key = _jax.random.fold_in(_jax.random.key(seed), 7919)
    shape, _ = _output_shape()
    out = dict(inp)
    out["loss_target"] = _jax.random.normal(_jax.random.fold_in(key, 0), shape, _jnp.float32)
    for i, name in enumerate(TWIN_WEIGHTS):
        w = inp[name].astype(_jnp.float32)
        if MOMENT_SCALE is None:
            s = _jnp.sqrt(_jnp.mean(_jnp.square(w)) + 1e-30)
        else:
            s = MOMENT_SCALE[name]
        km, kv = _jax.random.split(_jax.random.fold_in(key, i + 1))
        out[name] = w
        out["m_" + name] = s * _jax.random.normal(km, w.shape, _jnp.float32)
        out["v_" + name] = (s * s) * _jax.random.uniform(kv, w.shape, _jnp.float32, 0.5, 1.5)
    if N_MICROBATCH > 1:
        for name, axis in PER_EXAMPLE_BATCH_AXIS.items():
            out[name] = _to_microbatches(out[name], axis)
    return {'x': out['x'], 'norm_mix': out['norm_mix'], 'norm_mlp': out['norm_mlp'], 'norm_kv': out['norm_kv'], 'norm_final': out['norm_final'], 's5_a_re': out['s5_a_re'], 's5_a_im': out['s5_a_im'], 's5_log_dt': out['s5_log_dt'], 's5_b_re': out['s5_b_re'], 's5_b_im': out['s5_b_im'], 's5_c_re': out['s5_c_re'], 's5_c_im': out['s5_c_im'], 's5_d': out['s5_d'], 's5_w_glu': out['s5_w_glu'], 's5_b_glu': out['s5_b_glu'], 'w_kv': out['w_kv'], 'b_kv': out['b_kv'], 'w_q': out['w_q'], 'b_q': out['b_q'], 'sinks': out['sinks'], 'w_o': out['w_o'], 'b_o': out['b_o'], 'w_mlp_in': out['w_mlp_in'], 'w_mlp_out': out['w_mlp_out'], 'loss_target': out['loss_target'], 'm_norm_mix': out['m_norm_mix'], 'm_norm_mlp': out['m_norm_mlp'], 'm_norm_kv': out['m_norm_kv'], 'm_norm_final': out['m_norm_final'], 'm_s5_a_re': out['m_s5_a_re'], 'm_s5_a_im': out['m_s5_a_im'], 'm_s5_log_dt': out['m_s5_log_dt'], 'm_s5_b_re': out['m_s5_b_re'], 'm_s5_b_im': out['m_s5_b_im'], 'm_s5_c_re': out['m_s5_c_re'], 'm_s5_c_im': out['m_s5_c_im'], 'm_s5_d': out['m_s5_d'], 'm_s5_w_glu': out['m_s5_w_glu'], 'm_s5_b_glu': out['m_s5_b_glu'], 'm_w_kv': out['m_w_kv'], 'm_b_kv': out['m_b_kv'], 'm_w_q': out['m_w_q'], 'm_b_q': out['m_b_q'], 'm_sinks': out['m_sinks'], 'm_w_o': out['m_w_o'], 'm_b_o': out['m_b_o'], 'm_w_mlp_in': out['m_w_mlp_in'], 'm_w_mlp_out': out['m_w_mlp_out'], 'v_norm_mix': out['v_norm_mix'], 'v_norm_mlp': out['v_norm_mlp'], 'v_norm_kv': out['v_norm_kv'], 'v_norm_final': out['v_norm_final'], 'v_s5_a_re': out['v_s5_a_re'], 'v_s5_a_im': out['v_s5_a_im'], 'v_s5_log_dt': out['v_s5_log_dt'], 'v_s5_b_re': out['v_s5_b_re'], 'v_s5_b_im': out['v_s5_b_im'], 'v_s5_c_re': out['v_s5_c_re'], 'v_s5_c_im': out['v_s5_c_im'], 'v_s5_d': out['v_s5_d'], 'v_s5_w_glu': out['v_s5_w_glu'], 'v_s5_b_glu': out['v_s5_b_glu'], 'v_w_kv': out['v_w_kv'], 'v_b_kv': out['v_b_kv'], 'v_w_q': out['v_w_q'], 'v_b_q': out['v_b_q'], 'v_sinks': out['v_sinks'], 'v_w_o': out['v_w_o'], 'v_b_o': out['v_b_o'], 'v_w_mlp_in': out['v_w_mlp_in'], 'v_w_mlp_out': out['v_w_mlp_out']}


def _loss(weights, diff, rest, loss_target):
    with _jax.named_scope("forward"):
        args = {**rest, TWIN_DIFF_INPUT: diff, **{k: w.astype(_WEIGHT_DTYPES[k]) for k, w in weights.items()}}
        y = _forward(args)
    with _jax.named_scope("loss_head"):
        err = _jnp.square(y.astype(_jnp.float32) - loss_target)
        return 0.5 * _jnp.sum(_jnp.mean(err, axis=-1)) if err.ndim else 0.5 * err


def _adamw(w, g, m, v):
    m = ADAM_B1 * m + (1.0 - ADAM_B1) * g
    v = ADAM_B2 * v + (1.0 - ADAM_B2) * _jnp.square(g)
    m_hat = m / (1.0 - ADAM_B1 ** ADAM_STEP)
    v_hat = v / (1.0 - ADAM_B2 ** ADAM_STEP)
    delta = -ADAM_LR * (m_hat / (_jnp.sqrt(v_hat) + ADAM_EPS) + ADAM_WD * w)
    return delta, m, v


def reference(x, norm_mix, norm_mlp, norm_kv, norm_final, s5_a_re, s5_a_im, s5_log_dt, s5_b_re, s5_b_im, s5_c_re, s5_c_im, s5_d, s5_w_glu, s5_b_glu, w_kv, b_kv, w_q, b_q, sinks, w_o, b_o, w_mlp_in, w_mlp_out, loss_target, m_norm_mix, m_norm_mlp, m_norm_kv, m_norm_final, m_s5_a_re, m_s5_a_im, m_s5_log_dt, m_s5_b_re, m_s5_b_im, m_s5_c_re, m_s5_c_im, m_s5_d, m_s5_w_glu, m_s5_b_glu, m_w_kv, m_b_kv, m_w_q, m_b_q, m_sinks, m_w_o, m_b_o, m_w_mlp_in, m_w_mlp_out, v_norm_mix, v_norm_mlp, v_norm_kv, v_norm_final, v_s5_a_re, v_s5_a_im, v_s5_log_dt, v_s5_b_re, v_s5_b_im, v_s5_c_re, v_s5_c_im, v_s5_d, v_s5_w_glu, v_s5_b_glu, v_w_kv, v_b_kv, v_w_q, v_b_q, v_sinks, v_w_o, v_b_o, v_w_mlp_in, v_w_mlp_out):
    given = dict(x=x, norm_mix=norm_mix, norm_mlp=norm_mlp, norm_kv=norm_kv, norm_final=norm_final, s5_a_re=s5_a_re, s5_a_im=s5_a_im, s5_log_dt=s5_log_dt, s5_b_re=s5_b_re, s5_b_im=s5_b_im, s5_c_re=s5_c_re, s5_c_im=s5_c_im, s5_d=s5_d, s5_w_glu=s5_w_glu, s5_b_glu=s5_b_glu, w_kv=w_kv, b_kv=b_kv, w_q=w_q, b_q=b_q, sinks=sinks, w_o=w_o, b_o=b_o, w_mlp_in=w_mlp_in, w_mlp_out=w_mlp_out, loss_target=loss_target, m_norm_mix=m_norm_mix, m_norm_mlp=m_norm_mlp, m_norm_kv=m_norm_kv, m_norm_final=m_norm_final, m_s5_a_re=m_s5_a_re, m_s5_a_im=m_s5_a_im, m_s5_log_dt=m_s5_log_dt, m_s5_b_re=m_s5_b_re, m_s5_b_im=m_s5_b_im, m_s5_c_re=m_s5_c_re, m_s5_c_im=m_s5_c_im, m_s5_d=m_s5_d, m_s5_w_glu=m_s5_w_glu, m_s5_b_glu=m_s5_b_glu, m_w_kv=m_w_kv, m_b_kv=m_b_kv, m_w_q=m_w_q, m_b_q=m_b_q, m_sinks=m_sinks, m_w_o=m_w_o, m_b_o=m_b_o, m_w_mlp_in=m_w_mlp_in, m_w_mlp_out=m_w_mlp_out, v_norm_mix=v_norm_mix, v_norm_mlp=v_norm_mlp, v_norm_kv=v_norm_kv, v_norm_final=v_norm_final, v_s5_a_re=v_s5_a_re, v_s5_a_im=v_s5_a_im, v_s5_log_dt=v_s5_log_dt, v_s5_b_re=v_s5_b_re, v_s5_b_im=v_s5_b_im, v_s5_c_re=v_s5_c_re, v_s5_c_im=v_s5_c_im, v_s5_d=v_s5_d, v_s5_w_glu=v_s5_w_glu, v_s5_b_glu=v_s5_b_glu, v_w_kv=v_w_kv, v_b_kv=v_b_kv, v_w_q=v_w_q, v_b_q=v_b_q, v_sinks=v_sinks, v_w_o=v_w_o, v_b_o=v_b_o, v_w_mlp_in=v_w_mlp_in, v_w_mlp_out=v_w_mlp_out)
    weights = {n: given[n] for n in TWIN_WEIGHTS}
    shared = {n: given[n] for n in SHARED_INPUTS}
    per_example = {n: given[n] for n in ['x']}
    grad_fn = _jax.value_and_grad(_loss, argnums=(0, 1))

    def one_microbatch(ex, loss_target):
        ex = dict(ex)
        diff = ex.pop(TWIN_DIFF_INPUT)
        return grad_fn(weights, diff, {**shared, **ex}, loss_target)

    if N_MICROBATCH == 1:
        loss, (grad_w, grad_x) = one_microbatch(per_example, given["loss_target"])
    else:
        def body(carry, xs):
            loss_sum, grad_sum = carry
            l_k, (gw_k, gx_k) = one_microbatch(xs[0], xs[1])
            with _jax.named_scope("update"):
                return (loss_sum + l_k, _jax.tree.map(_jnp.add, grad_sum, gw_k)), gx_k

        init = (_jnp.zeros((), _jnp.float32), _jax.tree.map(_jnp.zeros_like, weights))
        (loss, grad_w), grad_x = _jax.lax.scan(body, init, (per_example, given["loss_target"]))
    with _jax.named_scope("update"):
        delta_w, new_m, new_v = {}, {}, {}
        for n in TWIN_WEIGHTS:
            delta_w[n], new_m[n], new_v[n] = _adamw(weights[n], grad_w[n], given["m_" + n], given["v_" + n])
    return (loss, grad_x, *[grad_w[n] for n in TWIN_WEIGHTS], *[delta_w[n] for n in TWIN_WEIGHTS],
            *[new_m[n] for n in TWIN_WEIGHTS], *[new_v[n] for n in TWIN_WEIGHTS])
```

```python
import functools
import math

import jax
import jax.numpy as jnp
from jax import lax
from jax.experimental import pallas as pl
from jax.experimental.pallas import tpu as pltpu

f32 = jnp.float32
bf16 = jnp.bfloat16
SDS = jax.ShapeDtypeStruct

T = 2048
D = 1024
NDEV = 8
NORM_EPS = 1e-5
S5_G, S5_C, S5_P = 64, 16, 64
S5_SUB = 8
S5_CH = 8
S5_STEPS = T // S5_CH
DT_MIN_LAMBDA = -1e-4
HEAD_DIM = 64
N_KV = 4
Q_PER_KV = 4
BLK = 128
D_FF_SHARD = 512
ADAM_LR, ADAM_B1, ADAM_B2, ADAM_EPS, ADAM_WD, ADAM_STEP = 0.001, 0.9, 0.999, 1e-08, 0.01, 10
VMEM_LIMIT = 56 * 1024 * 1024
MESH = pl.DeviceIdType.MESH


def _cp(**kw):
    return pltpu.CompilerParams(vmem_limit_bytes=VMEM_LIMIT, **kw)


def _dot(a, b):
    return jnp.dot(a, b, preferred_element_type=f32)


def _dot_nt(a, b):
    return lax.dot_general(a, b, (((1,), (1,)), ((), ())), preferred_element_type=f32)


def _dot_tn(a, b):
    return lax.dot_general(a, b, (((0,), (0,)), ((), ())), preferred_element_type=f32)


def _rms(x, g):
    r = lax.rsqrt(jnp.mean(x * x, axis=-1, keepdims=True) + NORM_EPS)
    return x * r * g, r


def _rms_bwd(x, g, dy):
    r = lax.rsqrt(jnp.mean(x * x, axis=-1, keepdims=True) + NORM_EPS)
    u = dy * g
    dx = r * u - (r * r * r) * x * jnp.mean(u * x, axis=-1, keepdims=True)
    return dx, dy * x * r


def _colsum8(v):
    s = jnp.sum(v, axis=0, keepdims=True)
    row = lax.broadcasted_iota(jnp.int32, (8, v.shape[1]), 0)
    return jnp.where(row == 0, jnp.broadcast_to(s, (8, v.shape[1])), 0.0)


def _full(shape):
    nd = len(shape)
    return pl.BlockSpec(shape, lambda *_: (0,) * nd, pipeline_mode=pl.Buffered(1))


def s5_discretize(a_re, a_im, log_dt, b_re, b_im, c_re, c_im):
    lam_r = jnp.minimum(a_re, DT_MIN_LAMBDA)
    lam_i = a_im
    dt = jnp.exp(log_dt)[:, None]
    e = jnp.exp(lam_r * dt)
    lbr = e * jnp.cos(lam_i * dt)
    lbi = e * jnp.sin(lam_i * dt)
    den = lam_r * lam_r + lam_i * lam_i
    cf_r = ((lbr - 1.0) * lam_r + lbi * lam_i) / den
    cf_i = (lbi * lam_r - (lbr - 1.0) * lam_i) / den
    bb_r = cf_r[:, :, None] * b_re - cf_i[:, :, None] * b_im
    bb_i = cf_r[:, :, None] * b_im + cf_i[:, :, None] * b_re
    eye = jnp.eye(8, dtype=f32)

    def blk_b(m):
        return jnp.einsum('bgpc,gh->bgchp', m.reshape(8, 8, S5_P, S5_C), eye).reshape(8, 128, 512)

    def blk_c(m):
        return jnp.einsum('bgcp,gh->bgphc', m.reshape(8, 8, S5_C, S5_P), eye).reshape(8, 512, 128)

    bm = jnp.concatenate([blk_b(bb_r), blk_b(bb_i)], axis=-1)
    cm = jnp.concatenate([blk_c(c_re), -blk_c(c_im)], axis=1)
    lam = jnp.stack([lbr.reshape(8, 512), lbi.reshape(8, 512)], axis=1)
    lam = jnp.broadcast_to(lam[:, :, None, :], (8, 2, 8, 512))
    return lam, bm, cm


def _cmul(ar, ai, br, bi):
    return ar * br - ai * bi, ar * bi + ai * br


def _shift_rows(v, k, up):
    row = lax.broadcasted_iota(jnp.int32, v.shape, 0)
    if up:
        return jnp.where(row < 8 - k, pltpu.roll(v, 8 - k, 0), 0.0)
    return jnp.where(row >= k, pltpu.roll(v, k, 0), 0.0)


def _chunk_scan(S, lr, li, reverse, aux=None):
    if reverse:
        li = -li
    z = jnp.zeros((8, 512), f32)

    def idx(i):
        return (S5_STEPS - 1 - i) if reverse else i

    def rec(xr, xi, row):
        br = S[row, 0:512]
        bi = S[row, 512:1024]
        return lr * xr - li * xi + br, lr * xi + li * xr + bi

    def step1(i, c):
        row = pl.ds(pl.multiple_of(idx(i) * 8, 8), 8)
        return rec(c[0], c[1], row)

    er, ei = lax.fori_loop(0, S5_STEPS, step1, (z, z), unroll=8)
    ar, ai = lr, li
    for _ in range(8):
        ar, ai = _cmul(ar, ai, ar, ai)
    cr, ci = _shift_rows(er, 1, reverse), _shift_rows(ei, 1, reverse)
    for k in (1, 2, 4):
        sr, si = _shift_rows(cr, k, reverse), _shift_rows(ci, k, reverse)
        pr, pi_ = _cmul(ar, ai, sr, si)
        cr, ci = cr + pr, ci + pi_
        ar, ai = _cmul(ar, ai, ar, ai)

    if aux is None:
        def step2(i, c):
            row = pl.ds(pl.multiple_of(idx(i) * 8, 8), 8)
            xr, xi = rec(c[0], c[1], row)
            S[row, 0:512] = xr
            S[row, 512:1024] = xi
            return xr, xi

        lax.fori_loop(0, S5_STEPS, step2, (cr, ci), unroll=8)
        return None

    def step2(i, c):
        gr0, gi0, dr, di = c
        s = idx(i)
        row = pl.ds(pl.multiple_of(s * 8, 8), 8)
        gr, gi = rec(gr0, gi0, row)
        S[row, 0:512] = gr
        S[row, 512:1024] = gi
        prow = pl.ds(pl.multiple_of(jnp.maximum(s - 1, 0) * 8, 8), 8)
        xr = aux[prow, 0:512]
        xi = aux[prow, 512:1024]
        dr = dr + gr * xr + gi * xi
        di = di + gi * xr - gr * xi
        return gr, gi, dr, di

    gr, gi, dr, di = lax.fori_loop(0, S5_STEPS - 1, step2, (cr, ci, z, z), unroll=8)
    row0 = pl.ds(0, 8)
    gr, gi = rec(gr, gi, row0)
    S[row0, 0:512] = gr
    S[row0, 512:1024] = gi
    last = pl.ds((S5_STEPS - 1) * 8, 8)
    xr = _shift_rows(aux[last, 0:512], 1, False)
    xi = _shift_rows(aux[last, 512:1024], 1, False)
    dr = dr + gr * xr + gi * xi
    di = di + gi * xr - gr * xi
    return dr, di


_ROWS = 256


def _row_loop(fn):
    def body(r, c):
        fn(pl.ds(pl.multiple_of(r * _ROWS, _ROWS), _ROWS))
        return c
    lax.fori_loop(0, T // _ROWS, body, 0)


def s5_core_fwd(hn, bm, lam, cm):
    def body(u_ref, b_ref, lam_ref, c_ref, ys_ref, S):
        def bu(rows):
            S[rows, :] = _dot(u_ref[rows, :], b_ref[...])
        _row_loop(bu)
        _chunk_scan(S, lam_ref[0], lam_ref[1], False)

        def ys(rows):
            ys_ref[rows, :] = _dot(S[rows, :].astype(bf16), c_ref[...])
        _row_loop(ys)

    return pl.pallas_call(
        body, name="s5_core_fwd", grid=(S5_SUB,),
        in_specs=[pl.BlockSpec((T, 128), lambda b: (0, b)),
                  pl.BlockSpec((None, 128, 1024), lambda b: (b, 0, 0)),
                  pl.BlockSpec((None, 2, 8, 512), lambda b: (b, 0, 0, 0)),
                  pl.BlockSpec((None, 1024, 128), lambda b: (b, 0, 0))],
        out_specs=pl.BlockSpec((T, 128), lambda b: (0, b)),
        out_shape=SDS((T, D), f32),
        scratch_shapes=[pltpu.VMEM((T, 1024), f32)],
        compiler_params=_cp(dimension_semantics=("arbitrary",)),
    )(hn, bm, lam, cm)


def s5_core_bwd(hn, dy, bm, lam, cm):
    def body(u_ref, dy_ref, b_ref, lam_ref, c_ref, du_ref, db_ref, dct_ref, dlam_ref, S1, S2):
        def bu(rows):
            S1[rows, :] = _dot(u_ref[rows, :], b_ref[...])
        _row_loop(bu)
        _chunk_scan(S1, lam_ref[0], lam_ref[1], False)
        dct_ref[...] = jnp.zeros_like(dct_ref)

        def dx(rows):
            dyb = dy_ref[rows, :].astype(bf16)
            S2[rows, :] = _dot_nt(dyb, c_ref[...])
            dct_ref[...] += _dot_tn(dyb, S1[rows, :].astype(bf16))
        _row_loop(dx)
        dr, di = _chunk_scan(S2, lam_ref[0], lam_ref[1], True, aux=S1)
        dlam_ref[0] = dr
        dlam_ref[1] = di
        db_ref[...] = jnp.zeros_like(db_ref)

        def dbu(rows):
            gb = S2[rows, :].astype(bf16)
            db_ref[...] += _dot_tn(u_ref[rows, :], gb)
            du_ref[rows, :] = _dot_nt(gb, b_ref[...])
        _row_loop(dbu)

    return pl.pallas_call(
        body, name="s5_core_bwd", grid=(S5_SUB,),
        in_specs=[pl.BlockSpec((T, 128), lambda b: (0, b)),
                  pl.BlockSpec((T, 128), lambda b: (0, b)),
                  pl.BlockSpec((None, 128, 1024), lambda b: (b, 0, 0)),
                  pl.BlockSpec((None, 2, 8, 512), lambda b: (b, 0, 0, 0)),
                  pl.BlockSpec((None, 1024, 128), lambda b: (b, 0, 0))],
        out_specs=[pl.BlockSpec((T, 128), lambda b: (0, b)),
                   pl.BlockSpec((None, 128, 1024), lambda b: (b, 0, 0)),
                   pl.BlockSpec((None, 128, 1024), lambda b: (b, 0, 0)),
                   pl.BlockSpec((None, 2, 8, 512), lambda b: (b, 0, 0, 0))],
        out_shape=[SDS((T, D), f32), SDS((8, 128, 1024), f32), SDS((8, 128, 1024), f32), SDS((8, 2, 8, 512), f32)],
        scratch_shapes=[pltpu.VMEM((T, 1024), f32), pltpu.VMEM((T, 1024), f32)],
        compiler_params=_cp(dimension_semantics=("arbitrary",)),
    )(hn, dy, bm, lam, cm)


TM = 512
NT = T // TM


def _tile(n=D):
    return pl.BlockSpec((TM, n), lambda i: (i, 0))


def s5_pre(xp, g):
    def body(x_ref, g_ref, hn_ref):
        hn, _ = _rms(x_ref[...], g_ref[...])
        hn_ref[...] = hn.astype(bf16)

    return pl.pallas_call(
        body, name="s5_pre", grid=(NT,), in_specs=[_tile(), _full((1, D))], out_specs=_tile(),
        out_shape=SDS((T, D), bf16), compiler_params=_cp(dimension_semantics=("arbitrary",)),
    )(xp, g)


def _gelu_grad(y):
    c = math.sqrt(2.0 / math.pi)
    t = jnp.tanh(c * (y + 0.044715 * y * y * y))
    return 0.5 * (1.0 + t) + 0.5 * y * (1.0 - t * t) * c * (1.0 + 3.0 * 0.044715 * y * y)


def s5_post(ys, xp, g, d, wglu, bglu):
    def body(ys_ref, x_ref, g_ref, d_ref, w_ref, b_ref, y_ref, z_ref, h_ref):
        x = x_ref[...]
        hn, _ = _rms(x, g_ref[...])
        y = ys_ref[...] + d_ref[...] * hn
        y_ref[...] = y
        yg = jax.nn.gelu(y).astype(bf16)
        for j in range(4):
            cv = slice(j * 256, (j + 1) * 256)
            cg = slice(1024 + j * 256, 1024 + (j + 1) * 256)
            val = _dot(yg, w_ref[j]) + b_ref[:, cv]
            gate = _dot(yg, w_ref[j + 4]) + b_ref[:, cg]
            z_ref[:, cv] = val
            z_ref[:, cg] = gate
            h_ref[:, cv] = x[:, cv] + val * jax.nn.sigmoid(gate)

    return pl.pallas_call(
        body, name="s5_post", grid=(NT,),
        in_specs=[_tile(), _tile(), _full((1, D)), _full((1, D)), _full((8, D, 256)), _full((1, 2 * D))],
        out_specs=[_tile(), _tile(2 * D), _tile()],
        out_shape=[SDS((T, D), f32), SDS((T, 2 * D), f32), SDS((T, D), f32)],
        compiler_params=_cp(dimension_semantics=("arbitrary",)),
    )(ys, xp, g, d, wglu, bglu)


def s5_post_bwd(dh, y, z, wglu):
    def body(dh_ref, y_ref, z_ref, w_ref, dy_ref, dw_ref, db_ref, acc):
        i = pl.program_id(0)

        @pl.when(i == 0)
        def _():
            acc[...] = jnp.zeros_like(acc)
            db_ref[...] = jnp.zeros_like(db_ref)

        dh_ = dh_ref[...]
        y = y_ref[...]
        yg = jax.nn.gelu(y).astype(bf16)
        dyg = jnp.zeros((TM, D), f32)
        for j in range(4):
            cv = slice(j * 256, (j + 1) * 256)
            cg = slice(1024 + j * 256, 1024 + (j + 1) * 256)
            val = z_ref[:, cv]
            sg = jax.nn.sigmoid(z_ref[:, cg])
            dval = dh_[:, cv] * sg
            dgate = dh_[:, cv] * val * sg * (1.0 - sg)
            db_ref[:, cv] += _colsum8(dval)
            db_ref[:, cg] += _colsum8(dgate)
            dvb = dval.astype(bf16)
            dgb = dgate.astype(bf16)
            acc[j] += _dot_tn(yg, dvb)
            acc[j + 4] += _dot_tn(yg, dgb)
            dyg = dyg + _dot_nt(dvb, w_ref[j]) + _dot_nt(dgb, w_ref[j + 4])
        dy_ref[...] = dyg * _gelu_grad(y)

        @pl.when(i == NT - 1)
        def _():
            dw_ref[...] = acc[...].astype(bf16)

    return pl.pallas_call(
        body, name="s5_post_bwd", grid=(NT,),
        in_specs=[_tile(), _tile(), _tile(2 * D), _full((8, D, 256))],
        out_specs=[_tile(), _full((8, D, 256)), _full((8, 2 * D))],
        out_shape=[SDS((T, D), f32), SDS((8, D, 256), bf16), SDS((8, 2 * D), f32)],
        scratch_shapes=[pltpu.VMEM((8, D, 256), f32)],
        compiler_params=_cp(dimension_semantics=("arbitrary",)),
    )(dh, y, z, wglu)


def s5_pre_bwd(xp, g, du, dy, d, dh):
    def body(x_ref, g_ref, du_ref, dy_ref, d_ref, dh_ref, dx_ref, dg_ref, dd_ref):
        i = pl.program_id(0)

        @pl.when(i == 0)
        def _():
            dg_ref[...] = jnp.zeros_like(dg_ref)
            dd_ref[...] = jnp.zeros_like(dd_ref)

        x = x_ref[...]
        g = g_ref[...]
        dy = dy_ref[...]
        hn, _ = _rms(x, g)
        dhn = du_ref[...] + d_ref[...] * dy
        dx, dgt = _rms_bwd(x, g, dhn)
        dx_ref[...] = dh_ref[...] + dx
        dg_ref[...] += _colsum8(dgt)
        dd_ref[...] += _colsum8(dy * hn)

    return pl.pallas_call(
        body, name="s5_pre_bwd", grid=(NT,),
        in_specs=[_tile(), _full((1, D)), _tile(), _tile(), _full((1, D)), _tile()],
        out_specs=[_tile(), _full((8, D)), _full((8, D))],
        out_shape=[SDS((T, D), f32), SDS((8, D), f32), SDS((8, D), f32)],
        compiler_params=_cp(dimension_semantics=("arbitrary",)),
    )(xp, g, du, dy, d, dh)


TMF = 1024


def mlp_fwd(h, g, w_in, w_out, layer):
    def body(h_ref, g_ref, wi_ref, wo_ref, hm_ref, out_ref, acc):
        j = pl.program_id(1)

        @pl.when(j == 0)
        def _():
            hm, _ = _rms(h_ref[...], g_ref[...])
            hm_ref[...] = hm.astype(bf16)
            acc[...] = jnp.zeros_like(acc)

        a = jnp.maximum(_dot(hm_ref[...], wi_ref[...]), 0.0)
        acc[...] += _dot((a * a).astype(bf16), wo_ref[...])

        @pl.when(j == NDEV - 1)
        def _():
            out_ref[...] = h_ref[...] + acc[...]

    return pl.pallas_call(
        body, name=f"mlp_fwd{layer}", grid=(T // TMF, NDEV),
        in_specs=[pl.BlockSpec((TMF, D), lambda i, j: (i, 0)),
                  pl.BlockSpec((1, D), lambda i, j: (0, 0)),
                  pl.BlockSpec((None, None, D, D_FF_SHARD), lambda i, j: (j, layer, 0, 0)),
                  pl.BlockSpec((None, None, D_FF_SHARD, D), lambda i, j: (j, layer, 0, 0))],
        out_specs=[pl.BlockSpec((TMF, D), lambda i, j: (i, 0)), pl.BlockSpec((TMF, D), lambda i, j: (i, 0))],
        out_shape=[SDS((T, D), bf16), SDS((T, D), f32)],
        scratch_shapes=[pltpu.VMEM((TMF, D), f32)],
        compiler_params=_cp(dimension_semantics=("arbitrary", "arbitrary")),
    )(h, g, w_in, w_out)


def mlp_bwd(h, hm, g, dout, w_in, w_out, layer, other=None):
    last = NDEV - 1

    def body(h_ref, hm_ref, g_ref, do_ref, wi_ref, wo_ref, *rest):
        dh_ref, dwi_ref, dwo_ref, dg_ref, dhm, awi, awo = rest[-7:]
        j = pl.program_id(0)
        i = pl.program_id(1)
        rows = pl.ds(pl.multiple_of(i * TM, TM), TM)

        @pl.when(i == 0)
        def _():
            awi[...] = jnp.zeros_like(awi)
            awo[...] = jnp.zeros_like(awo)

        hm_ = hm_ref[...]
        dob = do_ref[...].astype(bf16)
        r = jnp.maximum(_dot(hm_, wi_ref[...]), 0.0)
        dz = (_dot_nt(dob, wo_ref[...]) * (2.0 * r)).astype(bf16)
        awo[...] += _dot_tn((r * r).astype(bf16), dob)
        awi[...] += _dot_tn(hm_, dz)
        part = _dot_nt(dz, wi_ref[...])

        @pl.when(j == 0)
        def _():
            dhm[rows, :] = part

        @pl.when(j > 0)
        def _():
            dhm[rows, :] += part

        @pl.when(i == NT - 1)
        def _():
            dwi_ref[...] = awi[...].astype(bf16)
            dwo_ref[...] = awo[...].astype(bf16)

        @pl.when(j == last)
        def _():
            @pl.when(i == 0)
            def _():
                dg_ref[...] = jnp.zeros_like(dg_ref)
            dx, dgt = _rms_bwd(h_ref[...], g_ref[...], dhm[rows, :])
            dh_ref[...] = do_ref[...] + dx
            dg_ref[...] += _colsum8(dgt)

    late = lambda j, i: (jnp.where(j == last, i, 0), 0)
    anyspec = pl.BlockSpec(memory_space=pl.ANY)
    extra = [] if other is None else list(other)
    return pl.pallas_call(
        body, name=f"mlp_bwd{layer}", grid=(NDEV, NT),
        in_specs=[pl.BlockSpec((TM, D), late),
                  pl.BlockSpec((TM, D), lambda j, i: (i, 0)),
                  pl.BlockSpec((1, D), lambda j, i: (0, 0)),
                  pl.BlockSpec((TM, D), lambda j, i: (i, 0)),
                  pl.BlockSpec((None, None, D, D_FF_SHARD), lambda j, i: (j, layer, 0, 0)),
                  pl.BlockSpec((None, None, D_FF_SHARD, D), lambda j, i: (j, layer, 0, 0))] + [anyspec] * len(extra),
        out_specs=[pl.BlockSpec((TM, D), late),
                   pl.BlockSpec((None, None, D, D_FF_SHARD), lambda j, i: (j, layer, 0, 0)),
                   pl.BlockSpec((None, None, D_FF_SHARD, D), lambda j, i: (j, layer, 0, 0)),
                   pl.BlockSpec((8, D), lambda j, i: (0, 0))],
        out_shape=[SDS((T, D), f32), SDS((NDEV, 2, D, D_FF_SHARD), bf16), SDS((NDEV, 2, D_FF_SHARD, D), bf16),
                   SDS((8, D), f32)],
        scratch_shapes=[pltpu.VMEM((T, D), f32), pltpu.VMEM((D, D_FF_SHARD), f32), pltpu.VMEM((D_FF_SHARD, D), f32)],
        input_output_aliases={} if other is None else {6: 1, 7: 2},
        compiler_params=_cp(dimension_semantics=("arbitrary", "arbitrary")),
    )(h, hm, g, dout, w_in, w_out, *extra)


def attn_pre(h, g_kv, g_mix, wk4, wv4, bk4, bv4, wq, bq):
    def body(h_ref, gkv_ref, gm_ref, wk_ref, wv_ref, bk_ref, bv_ref, wq_ref, bq_ref,
             kvn_ref, hn_ref, k_ref, v_ref, q_ref):
        h_ = h_ref[...]
        kvn = _rms(h_, gkv_ref[...])[0].astype(bf16)
        hn = _rms(h_, gm_ref[...])[0].astype(bf16)
        kvn_ref[...] = kvn
        hn_ref[...] = hn
        k_ref[...] = (_dot(kvn, wk_ref[...]) + bk_ref[...]).astype(bf16)
        v_ref[...] = (_dot(kvn, wv_ref[...]) + bv_ref[...]).astype(bf16)
        q_ref[...] = (_dot(hn, wq_ref[...]) + bq_ref[...]).astype(bf16)

    return pl.pallas_call(
        body, name="attn_pre", grid=(NT,),
        in_specs=[_tile(), _full((1, D)), _full((1, D)), _full((D, D)), _full((D, D)), _full((1, D)), _full((1, D)),
                  _full((D, D)), _full((1, D))],
        out_specs=[_tile()] * 5,
        out_shape=[SDS((T, D), bf16)] * 5,
        compiler_params=_cp(dimension_semantics=("arbitrary",)),
    )(h, g_kv, g_mix, wk4, wv4, bk4, bv4, wq, bq)


def _attn_specs():
    cur = pl.BlockSpec((TM, 256), lambda j, n: (n, j))
    prev = pl.BlockSpec((BLK, 256), lambda j, n: (jnp.maximum(n * (TM // BLK) - 1, 0), j))
    return cur, prev


def _attn_probs(qg, k2, sink, first):
    s = _dot_nt(qg, k2) * (1.0 / math.sqrt(HEAD_DIM))
    qi = lax.broadcasted_iota(jnp.int32, (BLK, 2 * BLK), 0)
    kj = lax.broadcasted_iota(jnp.int32, (BLK, 2 * BLK), 1)
    diff = qi + BLK - kj
    valid = (diff >= 0) & (diff < BLK) & (jnp.logical_not(first) | (kj >= BLK))
    s = jnp.where(valid, s, -jnp.inf)
    m = jnp.maximum(jnp.max(s, axis=-1, keepdims=True), sink)
    p = jnp.exp(s - m)
    ps = jnp.exp(sink - m)
    denom = jnp.sum(p, axis=-1, keepdims=True) + ps
    return p / denom, ps / denom


def _head_mask(g):
    lane = lax.broadcasted_iota(jnp.int32, (1, 256), 1)
    return (lane >= g * HEAD_DIM) & (lane < (g + 1) * HEAD_DIM)


def attn_core_fwd(q, k4, v4, sinks):
    nb = TM // BLK

    def body(sink_ref, q_ref, kc_ref, kp_ref, vc_ref, vp_ref, o_ref):
        j = pl.program_id(0)
        n = pl.program_id(1)
        for b in range(nb):
            qb = q_ref[b * BLK:(b + 1) * BLK, :]
            if b == 0:
                k2 = jnp.concatenate([kp_ref[...], kc_ref[0:BLK, :]], axis=0)
                v2 = jnp.concatenate([vp_ref[...], vc_ref[0:BLK, :]], axis=0)
                first = n == 0
            else:
                k2 = kc_ref[(b - 1) * BLK:(b + 1) * BLK, :]
                v2 = vc_ref[(b - 1) * BLK:(b + 1) * BLK, :]
                first = False
            acc = jnp.zeros((BLK, 256), f32)
            for g in range(Q_PER_KV):
                mk = _head_mask(g)
                a, _ = _attn_probs(jnp.where(mk, qb, 0), k2, sink_ref[j * Q_PER_KV + g], first)
                acc = acc + _dot(a.astype(bf16), jnp.where(mk, v2, 0))
            o_ref[b * BLK:(b + 1) * BLK, :] = acc.astype(bf16)

    cur, prev = _attn_specs()
    return pl.pallas_call(
        body, name="attn_core_fwd", grid=(N_KV, NT),
        in_specs=[pl.BlockSpec(memory_space=pltpu.SMEM), cur, cur, prev, cur, prev],
        out_specs=cur, out_shape=SDS((T, D), bf16),
        compiler_params=_cp(dimension_semantics=("arbitrary", "arbitrary")),
    )(sinks, q, k4, k4, v4, v4)


def attn_post(h, o, wo, bo):
    def body(h_ref, o_ref, w_ref, b_ref, out_ref):
        out_ref[...] = h_ref[...] + _dot(o_ref[...], w_ref[...]) + b_ref[...]

    return pl.pallas_call(
        body, name="attn_post", grid=(NT,), in_specs=[_tile(), _tile(), _full((D, D)), _full((1, D))],
        out_specs=_tile(), out_shape=SDS((T, D), f32), compiler_params=_cp(dimension_semantics=("arbitrary",)),
    )(h, o, wo, bo)


def attn_bwd_pre(dh, o, wo):
    def body(dh_ref, o_ref, w_ref, do_ref, dw_ref, db_ref, acc):
        i = pl.program_id(0)

        @pl.when(i == 0)
        def _():
            acc[...] = jnp.zeros_like(acc)
            db_ref[...] = jnp.zeros_like(db_ref)

        dh_ = dh_ref[...]
        dhb = dh_.astype(bf16)
        do_ref[...] = _dot_nt(dhb, w_ref[...]).astype(bf16)
        acc[...] += _dot_tn(o_ref[...], dhb)
        db_ref[...] += _colsum8(dh_)

        @pl.when(i == NT - 1)
        def _():
            dw_ref[...] = acc[...].astype(bf16)

    return pl.pallas_call(
        body, name="attn_bwd_pre", grid=(NT,), in_specs=[_tile(), _tile(), _full((D, D))],
        out_specs=[_tile(), _full((D, D)), _full((8, D))],
        out_shape=[SDS((T, D), bf16), SDS((D, D), bf16), SDS((8, D), f32)],
        scratch_shapes=[pltpu.VMEM((D, D), f32)],
        compiler_params=_cp(dimension_semantics=("arbitrary",)),
    )(dh, o, wo)


def attn_core_bwd(q, do, k4, v4, sinks):
    nb = TM // BLK

    def body(sink_ref, q_ref, do_ref, kc_ref, kp_ref, vc_ref, vp_ref, dq_ref, dk_ref, dv_ref, ds_ref):
        j = pl.program_id(0)
        n = pl.program_id(1)

        @pl.when(n == 0)
        def _():
            dk_ref[...] = jnp.zeros_like(dk_ref)
            dv_ref[...] = jnp.zeros_like(dv_ref)
            ds_ref[...] = jnp.zeros_like(ds_ref)

        lane8 = lax.broadcasted_iota(jnp.int32, (8, 128), 1)
        row8 = lax.broadcasted_iota(jnp.int32, (8, 128), 0)
        for b in range(nb):
            qb = q_ref[b * BLK:(b + 1) * BLK, :]
            dob = do_ref[b * BLK:(b + 1) * BLK, :]
            if b == 0:
                k2 = jnp.concatenate([kp_ref[...], kc_ref[0:BLK, :]], axis=0)
                v2 = jnp.concatenate([vp_ref[...], vc_ref[0:BLK, :]], axis=0)
                first = n == 0
            else:
                k2 = kc_ref[(b - 1) * BLK:(b + 1) * BLK, :]
                v2 = vc_ref[(b - 1) * BLK:(b + 1) * BLK, :]
                first = False
            dq = jnp.zeros((BLK, 256), f32)
            dk2 = jnp.zeros((2 * BLK, 256), f32)
            dv2 = jnp.zeros((2 * BLK, 256), f32)
            for g in range(Q_PER_KV):
                mk = _head_mask(g)
                qg = jnp.where(mk, qb, 0)
                dog = jnp.where(mk, dob, 0)
                a, asink = _attn_probs(qg, k2, sink_ref[j * Q_PER_KV + g], first)
                dp = _dot_nt(dog, v2)
                dd = jnp.sum(a * dp, axis=-1, keepdims=True)
                dsc = (a * (dp - dd) * (1.0 / math.sqrt(HEAD_DIM))).astype(bf16)
                dsink = -jnp.sum(asink * dd, axis=0, keepdims=True)
                ds_ref[...] += jnp.where((lane8 == g) & (row8 == 0), jnp.broadcast_to(dsink, (8, 128)), 0.0)
                dq = dq + _dot(dsc, jnp.where(mk, k2, 0))
                dk2 = dk2 + _dot_tn(dsc, qg)
                dv2 = dv2 + _dot_tn(a.astype(bf16), dog)
            dq_ref[b * BLK:(b + 1) * BLK, :] = dq
            cur = pl.ds(pl.multiple_of(n * TM + b * BLK, BLK), BLK)
            dk_ref[cur, :] += dk2[BLK:, :]
            dv_ref[cur, :] += dv2[BLK:, :]
            if b == 0:
                @pl.when(n > 0)
                def _():
                    prv = pl.ds(pl.multiple_of(n * TM - BLK, BLK), BLK)
                    dk_ref[prv, :] += dk2[:BLK, :]
                    dv_ref[prv, :] += dv2[:BLK, :]
            else:
                prv = pl.ds(pl.multiple_of(n * TM + (b - 1) * BLK, BLK), BLK)
                dk_ref[prv, :] += dk2[:BLK, :]
                dv_ref[prv, :] += dv2[:BLK, :]

    cur, prev = _attn_specs()
    col = pl.BlockSpec((T, 256), lambda j, n: (0, j))
    return pl.pallas_call(
        body, name="attn_core_bwd", grid=(N_KV, NT),
        in_specs=[pl.BlockSpec(memory_space=pltpu.SMEM), cur, cur, cur, prev, cur, prev],
        out_specs=[cur, col, col, pl.BlockSpec((None, 8, 128), lambda j, n: (j, 0, 0))],
        out_shape=[SDS((T, D), f32), SDS((T, D), f32), SDS((T, D), f32), SDS((N_KV, 8, 128), f32)],
        compiler_params=_cp(dimension_semantics=("arbitrary", "arbitrary")),
    )(sinks, q, do, k4, k4, v4, v4)


def attn_bwd_q(h, dh, dq, hn, g_mix, wq):
    def body(h_ref, dh_ref, dq_ref, hn_ref, gm_ref, wq_ref, out_ref, dwq_ref, dbq_ref, dgm_ref, aq):
        i = pl.program_id(0)

        @pl.when(i == 0)
        def _():
            aq[...] = jnp.zeros_like(aq)
            dbq_ref[...] = jnp.zeros_like(dbq_ref)
            dgm_ref[...] = jnp.zeros_like(dgm_ref)

        dq_ = dq_ref[...]
        dqb = dq_.astype(bf16)
        aq[...] += _dot_tn(hn_ref[...], dqb)
        dbq_ref[...] += _colsum8(dq_)
        dx, dg = _rms_bwd(h_ref[...], gm_ref[...], _dot_nt(dqb, wq_ref[...]))
        out_ref[...] = dh_ref[...] + dx
        dgm_ref[...] += _colsum8(dg)

        @pl.when(i == NT - 1)
        def _():
            dwq_ref[...] = aq[...].astype(bf16)

    vec = _full((8, D))
    mat = _full((D, D))
    return pl.pallas_call(
        body, name="attn_bwd_q", grid=(NT,),
        in_specs=[_tile()] * 4 + [_full((1, D)), mat],
        out_specs=[_tile(), mat, vec, vec],
        out_shape=[SDS((T, D), f32), SDS((D, D), bf16), SDS((8, D), f32), SDS((8, D), f32)],
        scratch_shapes=[pltpu.VMEM((D, D), f32)],
        compiler_params=_cp(dimension_semantics=("arbitrary",)),
    )(h, dh, dq, hn, g_mix, wq)


def attn_bwd_kv(h, dh, dk4, dv4, kvn, g_kv, wk4, wv4):
    def body(h_ref, dh_ref, dk_ref, dv_ref, kvn_ref, gkv_ref, wk_ref, wv_ref,
             out_ref, dwk_ref, dwv_ref, dbk_ref, dbv_ref, dgkv_ref):
        i = pl.program_id(0)

        @pl.when(i == 0)
        def _():
            for r in (dwk_ref, dwv_ref, dbk_ref, dbv_ref, dgkv_ref):
                r[...] = jnp.zeros_like(r)

        dk_ = dk_ref[...]
        dv_ = dv_ref[...]
        dkb, dvb = dk_.astype(bf16), dv_.astype(bf16)
        dkvn = _dot_nt(dkb, wk_ref[...]) + _dot_nt(dvb, wv_ref[...])
        dwk_ref[...] += _dot_tn(kvn_ref[...], dkb)
        dwv_ref[...] += _dot_tn(kvn_ref[...], dvb)
        dbk_ref[...] += _colsum8(dk_)
        dbv_ref[...] += _colsum8(dv_)
        dx, dg = _rms_bwd(h_ref[...], gkv_ref[...], dkvn)
        out_ref[...] = dh_ref[...] + dx
        dgkv_ref[...] += _colsum8(dg)

    vec = _full((8, D))
    mat = _full((D, D))
    return pl.pallas_call(
        body, name="attn_bwd_kv", grid=(NT,),
        in_specs=[_tile()] * 5 + [_full((1, D)), mat, mat],
        out_specs=[_tile(), mat, mat, vec, vec, vec],
        out_shape=[SDS((T, D), f32), SDS((D, D), f32), SDS((D, D), f32)] + [SDS((8, D), f32)] * 3,
        compiler_params=_cp(dimension_semantics=("arbitrary",)),
    )(h, dh, dk4, dv4, kvn, g_kv, wk4, wv4)


def final_loss(h, g, target):
    def body(h_ref, g_ref, t_ref, loss_ref, dh_ref, dg_ref):
        i = pl.program_id(0)

        @pl.when(i == 0)
        def _():
            loss_ref[...] = jnp.zeros_like(loss_ref)
            dg_ref[...] = jnp.zeros_like(dg_ref)

        h_ = h_ref[...]
        g_ = g_ref[...]
        y, _ = _rms(h_, g_)
        diff = y - t_ref[...]
        per_tok = jnp.mean(diff * diff, axis=-1, keepdims=True)
        tot = 0.5 * jnp.sum(per_tok, axis=0, keepdims=True)
        lane = lax.broadcasted_iota(jnp.int32, (8, 128), 1)
        row = lax.broadcasted_iota(jnp.int32, (8, 128), 0)
        loss_ref[...] += jnp.where((lane == 0) & (row == 0), jnp.broadcast_to(tot, (8, 128)), 0.0)
        dx, dgt = _rms_bwd(h_, g_, diff * (1.0 / D))
        dh_ref[...] = dx
        dg_ref[...] += _colsum8(dgt)

    return pl.pallas_call(
        body, name="final_loss", grid=(NT,), in_specs=[_tile(), _full((1, D)), _tile()],
        out_specs=[_full((8, 128)), _tile(), _full((8, D))],
        out_shape=[SDS((8, 128), f32), SDS((T, D), f32), SDS((8, D), f32)],
        compiler_params=_cp(dimension_semantics=("arbitrary",)),
    )(h, g, target)


def _to_chunked(a):
    return a.reshape(S5_CH, S5_STEPS, a.shape[-1]).transpose(1, 0, 2).reshape(T, a.shape[-1])


def _from_chunked(a):
    return a.reshape(S5_STEPS, S5_CH, a.shape[-1]).transpose(1, 0, 2).reshape(T, a.shape[-1])


def _rep4(w):
    return jnp.broadcast_to(w.reshape(w.shape[0], N_KV, 1, HEAD_DIM), (w.shape[0], N_KV, Q_PER_KV, HEAD_DIM)).reshape(
        w.shape[0], N_KV * Q_PER_KV * HEAD_DIM)


def _fold4(w):
    return w.reshape(w.shape[0], N_KV, Q_PER_KV, HEAD_DIM).sum(axis=2).reshape(w.shape[0], N_KV * HEAD_DIM)


def local_step(x, target, p):
    row = lambda v: v.reshape(1, -1)
    (lam, bm, cm), prep_vjp = jax.vjp(s5_discretize, p["s5_a_re"][0], p["s5_a_im"][0], p["s5_log_dt"][0],
                                      p["s5_b_re"][0], p["s5_b_im"][0], p["s5_c_re"][0], p["s5_c_im"][0])
    bmb, cmb = bm.astype(bf16), cm.astype(bf16)
    g_mix0, g_mix1 = row(p["norm_mix"][0]), row(p["norm_mix"][1])
    g_mlp0, g_mlp1 = row(p["norm_mlp"][0]), row(p["norm_mlp"][1])
    g_kv, g_fin = row(p["norm_kv"]), row(p["norm_final"])
    d_skip = p["s5_d"]
    wglu, bglu = p["s5_w_glu"], p["s5_b_glu"]
    w_in, w_out = p["w_mlp_in"], p["w_mlp_out"]
    wkv = p["w_kv"]
    wk4, wv4 = _rep4(wkv[:, :256]), _rep4(wkv[:, 256:])
    bk4, bv4 = _rep4(row(p["b_kv"])[:, :256]), _rep4(row(p["b_kv"])[:, 256:])
    wq, wo = p["w_q"], p["w_o"]
    bq, bo = p["b_q"], p["b_o"]
    sinks = p["sinks"].reshape(16)

    xp = _to_chunked(x)
    hn0 = s5_pre(xp, g_mix0)
    ys = s5_core_fwd(hn0, bmb, lam, cmb)
    y, z, h1 = s5_post(ys, xp, g_mix0, d_skip, wglu, bglu)
    hm0, h2p = mlp_fwd(h1, g_mlp0, w_in, w_out, 0)
    h2 = _from_chunked(h2p)
    kvn, hn1, k4, v4, q = attn_pre(h2, g_kv, g_mix1, wk4, wv4, bk4, bv4, wq, bq)
    o = attn_core_fwd(q, k4, v4, sinks)
    h3 = attn_post(h2, o, wo, bo)
    hm1, h4 = mlp_fwd(h3, g_mlp1, w_in, w_out, 1)
    loss, dh4, dg_fin = final_loss(h4, g_fin, target)

    dh3, dwin, dwout, dg_mlp1 = mlp_bwd(h3, hm1, g_mlp1, dh4, w_in, w_out, 1)
    do, dwo, dbo = attn_bwd_pre(dh3, o, wo)
    dq, dk4, dv4, dsink = attn_core_bwd(q, do, k4, v4, sinks)
    dh2, dwq, dbq, dg_mix1 = attn_bwd_q(h2, dh3, dq, hn1, g_mix1, wq)
    dh2, dwk4, dwv4, dbk4, dbv4, dg_kv = attn_bwd_kv(h2, dh2, dk4, dv4, kvn, g_kv, wk4, wv4)
    dh2p = _to_chunked(dh2)
    dh1, dwin, dwout, dg_mlp0 = mlp_bwd(h1, hm0, g_mlp0, dh2p, w_in, w_out, 0, other=(dwin, dwout))
    dy, dwglu, dbglu = s5_post_bwd(dh1, y, z, wglu)
    du, dbm, dcmt, dlam = s5_core_bwd(hn0, dy, bmb, lam, cmb)
    dxp, dg_mix0, dd = s5_pre_bwd(xp, g_mix0, du, dy, d_skip, dh1)
    grad_x = _from_chunked(dxp)
    da_re, da_im, dlog_dt, db_re, db_im, dc_re, dc_im = prep_vjp((dlam, dbm, dcmt.transpose(0, 2, 1)))

    grads = {
        "norm_mix": jnp.stack([dg_mix0[0], dg_mix1[0]]),
        "norm_mlp": jnp.stack([dg_mlp0[0], dg_mlp1[0]]),
        "norm_kv": dg_kv[0], "norm_final": dg_fin[0],
        "s5_a_re": da_re[None], "s5_a_im": da_im[None], "s5_log_dt": dlog_dt[None],
        "s5_b_re": db_re[None], "s5_b_im": db_im[None], "s5_c_re": dc_re[None], "s5_c_im": dc_im[None],
        "s5_d": dd[0:1], "s5_b_glu": dbglu[0:1],
        "b_kv": jnp.concatenate([_fold4(dbk4[0:1]), _fold4(dbv4[0:1])], axis=1)[0],
        "b_q": dbq[0:1], "sinks": dsink[:, 0, :Q_PER_KV].reshape(1, 16), "b_o": dbo[0:1],
        "s5_w_glu": dwglu,
        "w_kv": jnp.concatenate([_fold4(dwk4), _fold4(dwv4)], axis=1).astype(bf16),
        "w_q": dwq, "w_o": dwo,
        "w_mlp_in": dwin, "w_mlp_out": dwout,
    }
    return loss, grad_x, grads


_ANY = pl.BlockSpec(memory_space=pl.ANY)


def _pos():
    return lax.axis_index("x"), lax.axis_index("y"), lax.axis_index("c")


def _other_chips(x, y):
    return [(1 - x, y), (x, 1 - y), (1 - x, 1 - y)]


def all_gather(arrs):
    n = len(arrs)

    def body(*refs):
        ins, outs = refs[:n], refs[n:2 * n]
        send_sems, recv_sems, local_sems = refs[2 * n:]
        x, y, c = _pos()
        me, sib = (x, y, c), (x, y, 1 - c)
        chips = _other_chips(x, y)

        def copy(a, k, block, to, src=None):
            dst = outs[a].at[4 * block[0] + 2 * block[1] + block[2]]
            return pltpu.make_async_remote_copy(
                src_ref=dst if src is None else src, dst_ref=dst, send_sem=send_sems.at[a, k],
                recv_sem=recv_sems.at[a, k], device_id=to, device_id_type=MESH)

        mine = [pltpu.make_async_copy(ins[a], outs[a].at[4 * x + 2 * y + c], local_sems.at[a]) for a in range(n)]
        for cp in mine:
            cp.start()
        first = []
        for a in range(n):
            first.append(copy(a, 0, me, sib, src=ins[a]))
            first += [copy(a, 1 + j, me, (*chip, c), src=ins[a]) for j, chip in enumerate(chips)]
        for cp in first:
            cp.start()
        passed = []
        for j, chip in enumerate(chips):
            for a in range(n):
                copy(a, 1 + j, (*chip, c), me).wait_recv()
                cp = copy(a, 4 + j, (*chip, c), sib)
                cp.start()
                passed.append(cp)
        for a in range(n):
            copy(a, 0, sib, me).wait_recv()
            for j, chip in enumerate(chips):
                copy(a, 4 + j, (*chip, 1 - c), me).wait_recv()
        for cp in first + passed:
            cp.wait_send()
        for cp in mine:
            cp.wait()

    return pl.pallas_call(
        body, name="all_gather", in_specs=[_ANY] * n, out_specs=[_ANY] * n,
        out_shape=[SDS((NDEV,) + a.shape, a.dtype) for a in arrs],
        scratch_shapes=[pltpu.SemaphoreType.DMA((n, 7)), pltpu.SemaphoreType.DMA((n, 7)),
                        pltpu.SemaphoreType.DMA((n,))],
    )(*arrs)


def rs_pair(grads):
    n = len(grads)

    def body(*refs):
        ins, outs = refs[:n], refs[n:2 * n]
        send_sems, recv_sems = refs[2 * n:]
        x, y, c = _pos()
        cps = []
        for a in range(n):
            for k in range(4):
                cps.append(pltpu.make_async_remote_copy(
                    src_ref=ins[a].at[2 * k + 1 - c], dst_ref=outs[a].at[k], send_sem=send_sems.at[a, k],
                    recv_sem=recv_sems.at[a, k], device_id=(x, y, 1 - c), device_id_type=MESH))
        for cp in cps:
            cp.start()
        for cp in cps:
            cp.wait_recv()
        for cp in cps:
            cp.wait_send()

    return pl.pallas_call(
        body, name="rs_pair", in_specs=[_ANY] * n, out_specs=[_ANY] * n,
        out_shape=[SDS((4,) + g.shape[1:], g.dtype) for g in grads],
        scratch_shapes=[pltpu.SemaphoreType.DMA((n, 4)), pltpu.SemaphoreType.DMA((n, 4))],
    )(*grads)


def rs_chips(parts):
    n = len(parts)

    def body(*refs):
        ins, outs = refs[:n], refs[n:2 * n]
        send_sems, recv_sems = refs[2 * n:]
        x, y, c = _pos()
        cps = []
        for a in range(n):
            for r, (px, py) in enumerate(_other_chips(x, y)):
                cps.append(pltpu.make_async_remote_copy(
                    src_ref=ins[a].at[2 * px + py], dst_ref=outs[a].at[r], send_sem=send_sems.at[a, r],
                    recv_sem=recv_sems.at[a, r], device_id=(px, py, c), device_id_type=MESH))
        for cp in cps:
            cp.start()
        for cp in cps:
            cp.wait_recv()
        for cp in cps:
            cp.wait_send()

    return pl.pallas_call(
        body, name="rs_chips", in_specs=[_ANY] * n, out_specs=[_ANY] * n,
        out_shape=[SDS((3,) + g.shape[1:], g.dtype) for g in parts],
        scratch_shapes=[pltpu.SemaphoreType.DMA((n, 3)), pltpu.SemaphoreType.DMA((n, 3))],
    )(*parts)


def _row_tile(r, c):
    return min(r, max(8, (256 * 1024) // c))


def add_pairs(g, r1, core, name):
    _, R, C = g.shape
    tr = _row_tile(R, C)

    def body(core_ref, g_ref, r_ref, o_ref):
        o_ref[...] = (g_ref[...].astype(f32) + r_ref[...].astype(f32)).astype(bf16)

    return pl.pallas_call(
        body, name=name, out_shape=SDS((4, R, C), bf16),
        grid_spec=pltpu.PrefetchScalarGridSpec(
            num_scalar_prefetch=1, grid=(4, R // tr),
            in_specs=[pl.BlockSpec((None, tr, C), lambda k, i, core: (2 * k + core[0], i, 0)),
                      pl.BlockSpec((None, tr, C), lambda k, i, core: (k, i, 0))],
            out_specs=pl.BlockSpec((None, tr, C), lambda k, i, core: (k, i, 0))),
        compiler_params=_cp(dimension_semantics=("arbitrary", "arbitrary")),
    )(core, g, r1)


def _adamw(w, g, m, v):
    m = ADAM_B1 * m + (1.0 - ADAM_B1) * g
    v = ADAM_B2 * v + (1.0 - ADAM_B2) * (g * g)
    m_hat = m / (1.0 - ADAM_B1 ** ADAM_STEP)
    v_hat = v / (1.0 - ADAM_B2 ** ADAM_STEP)
    delta = -ADAM_LR * (m_hat / (jnp.sqrt(v_hat) + ADAM_EPS) + ADAM_WD * w)
    return delta, m, v


def adam_big(w, m, v, part, r2, chip, name):
    R, C = w.shape
    tr = _row_tile(R, C)

    def body(chip_ref, w_ref, m_ref, v_ref, p_ref, r_ref, g_out, d_out, m_out, v_out):
        g = p_ref[...].astype(f32) + r_ref[0].astype(f32) + r_ref[1].astype(f32) + r_ref[2].astype(f32)
        d, m_, v_ = _adamw(w_ref[...], g, m_ref[...], v_ref[...])
        g_out[...] = g
        d_out[...] = d
        m_out[...] = m_
        v_out[...] = v_

    blk = pl.BlockSpec((tr, C), lambda i, chip: (i, 0))
    return pl.pallas_call(
        body, name=name, out_shape=[SDS((R, C), f32)] * 4,
        grid_spec=pltpu.PrefetchScalarGridSpec(
            num_scalar_prefetch=1, grid=(R // tr,),
            in_specs=[blk, blk, blk,
                      pl.BlockSpec((None, tr, C), lambda i, chip: (chip[0], i, 0)),
                      pl.BlockSpec((3, tr, C), lambda i, chip: (0, i, 0))],
            out_specs=[blk] * 4),
        compiler_params=_cp(dimension_semantics=("arbitrary",)),
    )(chip, w, m, v, part, r2)


def allreduce_small(buf):
    R = buf.shape[0]

    def body(in_ref, out_ref, acc1, acc2, r0, r1, r2, send_sems, recv_sems):
        x, y, c = _pos()
        peers = [(x, y, 1 - c), (1 - x, y, c), (x, 1 - y, c)]
        srcs, rcvs, dsts = [in_ref, acc1, acc2], [r0, r1, r2], [acc1, acc2, out_ref]
        for s in range(3):
            cp = pltpu.make_async_remote_copy(src_ref=srcs[s], dst_ref=rcvs[s], send_sem=send_sems.at[s],
                                              recv_sem=recv_sems.at[s], device_id=peers[s], device_id_type=MESH)
            cp.start()
            cp.wait()
            dsts[s][...] = srcs[s][...] + rcvs[s][...]

    return pl.pallas_call(
        body, name="allreduce_small", out_shape=SDS((R, 128), f32),
        scratch_shapes=[pltpu.VMEM((R, 128), f32)] * 5 + [pltpu.SemaphoreType.DMA((3,)), pltpu.SemaphoreType.DMA((3,))],
    )(buf)


def adam_small(w, g, m, v):
    def body(w_ref, g_ref, m_ref, v_ref, d_out, m_out, v_out):
        d, m_, v_ = _adamw(w_ref[...], g_ref[...], m_ref[...], v_ref[...])
        d_out[...] = d
        m_out[...] = m_
        v_out[...] = v_

    return pl.pallas_call(body, name="adam_small", out_shape=[SDS(w.shape, f32)] * 3)(w, g, m, v)


WEIGHTS = ['norm_mix', 'norm_mlp', 'norm_kv', 'norm_final', 's5_a_re', 's5_a_im', 's5_log_dt', 's5_b_re', 's5_b_im',
           's5_c_re', 's5_c_im', 's5_d', 's5_w_glu', 's5_b_glu', 'w_kv', 'b_kv', 'w_q', 'b_q', 'sinks', 'w_o', 'b_o',
           'w_mlp_in', 'w_mlp_out']
BIG = ['s5_w_glu', 'w_kv', 'w_q', 'w_o', 'w_mlp_in', 'w_mlp_out']
BIG_2D = {'s5_w_glu': (D, 256), 'w_kv': (128, 512), 'w_q': (128, D), 'w_o': (128, D), 'w_mlp_in': (2 * D, 512),
          'w_mlp_out': (2 * 512, D)}
SHARDED_SMALL = {'s5_d': D, 's5_b_glu': 2 * D}
SMALL = [n for n in WEIGHTS if n not in BIG]
SMALL_SIZE = {'norm_mix': 2 * D, 'norm_mlp': 2 * D, 'norm_kv': D, 'norm_final': D, 's5_a_re': 4096, 's5_a_im': 4096,
              's5_log_dt': 64, 's5_b_re': 65536, 's5_b_im': 65536, 's5_c_re': 65536, 's5_c_im': 65536, 's5_d': D,
              's5_b_glu': 2 * D, 'b_kv': 512, 'b_q': D, 'sinks': 16, 'b_o': D}


def _pack(vals):
    parts = []
    for n in SMALL:
        v = vals[n].reshape(-1).astype(f32)
        parts.append(jnp.pad(v, (0, (-v.shape[0]) % 128)))
    flat = jnp.concatenate(parts)
    flat = jnp.pad(flat, (0, (-flat.shape[0]) % 1024))
    return flat.reshape(-1, 128)


def _unpack(buf):
    flat = buf.reshape(-1)
    out, off = {}, 0
    for n in SMALL:
        sz = SMALL_SIZE[n]
        out[n] = flat[off:off + sz]
        off += sz + (-sz) % 128
    return out


def kernel(x, norm_mix, norm_mlp, norm_kv, norm_final, s5_a_re, s5_a_im, s5_log_dt, s5_b_re, s5_b_im, s5_c_re, s5_c_im, s5_d, s5_w_glu, s5_b_glu, w_kv, b_kv, w_q, b_q, sinks, w_o, b_o, w_mlp_in, w_mlp_out, loss_target, m_norm_mix, m_norm_mlp, m_norm_kv, m_norm_final, m_s5_a_re, m_s5_a_im, m_s5_log_dt, m_s5_b_re, m_s5_b_im, m_s5_c_re, m_s5_c_im, m_s5_d, m_s5_w_glu, m_s5_b_glu, m_w_kv, m_b_kv, m_w_q, m_b_q, m_sinks, m_w_o, m_b_o, m_w_mlp_in, m_w_mlp_out, v_norm_mix, v_norm_mlp, v_norm_kv, v_norm_final, v_s5_a_re, v_s5_a_im, v_s5_log_dt, v_s5_b_re, v_s5_b_im, v_s5_c_re, v_s5_c_im, v_s5_d, v_s5_w_glu, v_s5_b_glu, v_w_kv, v_b_kv, v_w_q, v_b_q, v_sinks, v_w_o, v_b_o, v_w_mlp_in, v_w_mlp_out):
    w = dict(norm_mix=norm_mix, norm_mlp=norm_mlp, norm_kv=norm_kv, norm_final=norm_final, s5_a_re=s5_a_re,
             s5_a_im=s5_a_im, s5_log_dt=s5_log_dt, s5_b_re=s5_b_re, s5_b_im=s5_b_im, s5_c_re=s5_c_re, s5_c_im=s5_c_im,
             s5_d=s5_d, s5_w_glu=s5_w_glu, s5_b_glu=s5_b_glu, w_kv=w_kv, b_kv=b_kv, w_q=w_q, b_q=b_q, sinks=sinks,
             w_o=w_o, b_o=b_o, w_mlp_in=w_mlp_in, w_mlp_out=w_mlp_out)
    m = dict(norm_mix=m_norm_mix, norm_mlp=m_norm_mlp, norm_kv=m_norm_kv, norm_final=m_norm_final, s5_a_re=m_s5_a_re,
             s5_a_im=m_s5_a_im, s5_log_dt=m_s5_log_dt, s5_b_re=m_s5_b_re, s5_b_im=m_s5_b_im, s5_c_re=m_s5_c_re,
             s5_c_im=m_s5_c_im, s5_d=m_s5_d, s5_w_glu=m_s5_w_glu, s5_b_glu=m_s5_b_glu, w_kv=m_w_kv, b_kv=m_b_kv,
             w_q=m_w_q, b_q=m_b_q, sinks=m_sinks, w_o=m_w_o, b_o=m_b_o, w_mlp_in=m_w_mlp_in, w_mlp_out=m_w_mlp_out)
    v = dict(norm_mix=v_norm_mix, norm_mlp=v_norm_mlp, norm_kv=v_norm_kv, norm_final=v_norm_final, s5_a_re=v_s5_a_re,
             s5_a_im=v_s5_a_im, s5_log_dt=v_s5_log_dt, s5_b_re=v_s5_b_re, s5_b_im=v_s5_b_im, s5_c_re=v_s5_c_re,
             s5_c_im=v_s5_c_im, s5_d=v_s5_d, s5_w_glu=v_s5_w_glu, s5_b_glu=v_s5_b_glu, w_kv=v_w_kv, b_kv=v_b_kv,
             w_q=v_w_q, b_q=v_b_q, sinks=v_sinks, w_o=v_w_o, b_o=v_b_o, w_mlp_in=v_w_mlp_in, w_mlp_out=v_w_mlp_out)
    xi, yi, ci = _pos()
    dev = 4 * xi + 2 * yi + ci

    vecs = jnp.broadcast_to(jnp.concatenate([s5_d, s5_b_glu], axis=1), (8, 384))
    g_glu, g_kv, g_q, g_o, g_in, g_out, g_vec = all_gather([
        s5_w_glu[0].astype(bf16), w_kv.astype(bf16), w_q[0].astype(bf16), w_o[0].astype(bf16),
        w_mlp_in.astype(bf16), w_mlp_out.astype(bf16), vecs])
    p = {n: w[n] for n in SMALL}
    p["s5_d"] = g_vec[:, 0, :128].reshape(1, D)
    p["s5_b_glu"] = g_vec[:, 0, 128:].reshape(1, 2 * D)
    p.update(s5_w_glu=g_glu, w_kv=g_kv.reshape(D, 512), w_q=g_q.reshape(D, D), w_o=g_o.reshape(D, D),
             w_mlp_in=g_in, w_mlp_out=g_out)

    loss, grad_x, grads = local_step(x[0], loss_target[0], p)
    loss = lax.psum(loss[0, 0], ("x", "y", "c"))

    big = [grads[n].reshape((NDEV,) + BIG_2D[n]) for n in BIG]
    r1 = rs_pair(big)
    core = ci.reshape(1).astype(jnp.int32)
    parts = [add_pairs(g, r, core, f"add_pairs_{n}") for n, g, r in zip(BIG, big, r1)]
    r2 = rs_chips(parts)
    chip = (2 * xi + yi).reshape(1).astype(jnp.int32)
    out_g, out_d, out_m, out_v = {}, {}, {}, {}
    for n, part, rr in zip(BIG, parts, r2):
        shp = w[n].shape
        res = adam_big(w[n].reshape(BIG_2D[n]), m[n].reshape(BIG_2D[n]), v[n].reshape(BIG_2D[n]), part, rr, chip,
                       f"adam_{n}")
        out_g[n], out_d[n], out_m[n], out_v[n] = [r.reshape(shp) for r in res]

    def full_len(d):
        d = dict(d)
        for n, ln in SHARDED_SMALL.items():
            d[n] = lax.dynamic_update_slice(jnp.zeros((ln,), f32), d[n].reshape(-1), (dev * (ln // NDEV),))
        return d

    gsum = allreduce_small(_pack({n: grads[n] for n in SMALL}))
    ds, ms, vs = adam_small(_pack(full_len({n: w[n] for n in SMALL})), gsum,
                            _pack(full_len({n: m[n] for n in SMALL})), _pack(full_len({n: v[n] for n in SMALL})))
    for src, dst in ((gsum, out_g), (ds, out_d), (ms, out_m), (vs, out_v)):
        for n, val in _unpack(src).items():
            if n in SHARDED_SMALL:
                ln = SHARDED_SMALL[n] // NDEV
                val = lax.dynamic_slice(val, (dev * ln,), (ln,))
            dst[n] = val.reshape(w[n].shape)

    return (loss, grad_x[None], *[out_g[n] for n in WEIGHTS], *[out_d[n] for n in WEIGHTS],
            *[out_m[n] for n in WEIGHTS], *[out_v[n] for n in WEIGHTS])
```

```python
import functools
import math

import jax
import jax.numpy as jnp
from jax import lax
from jax.experimental import pallas as pl
from jax.experimental.pallas import tpu as pltpu

f32 = jnp.float32
bf16 = jnp.bfloat16
SDS = jax.ShapeDtypeStruct

T = 2048
D = 1024
NDEV = 8
NORM_EPS = 1e-5
S5_G, S5_C, S5_P = 64, 16, 64
S5_SUB = 8
S5_CH = 8
S5_STEPS = T // S5_CH
DT_MIN_LAMBDA = -1e-4
HEAD_DIM = 64
N_KV = 4
Q_PER_KV = 4
BLK = 128
D_FF_SHARD = 512
ADAM_LR, ADAM_B1, ADAM_B2, ADAM_EPS, ADAM_WD, ADAM_STEP = 0.001, 0.9, 0.999, 1e-08, 0.01, 10
VMEM_LIMIT = 56 * 1024 * 1024
MESH = pl.DeviceIdType.MESH


def _cp(**kw):
    return pltpu.CompilerParams(vmem_limit_bytes=VMEM_LIMIT, **kw)


def _dot(a, b):
    return jnp.dot(a, b, preferred_element_type=f32)


def _dot_nt(a, b):
    return lax.dot_general(a, b, (((1,), (1,)), ((), ())), preferred_element_type=f32)


def _dot_tn(a, b):
    return lax.dot_general(a, b, (((0,), (0,)), ((), ())), preferred_element_type=f32)


def _rms(x, g):
    r = lax.rsqrt(jnp.mean(x * x, axis=-1, keepdims=True) + NORM_EPS)
    return x * r * g, r


def _rms_bwd(x, g, dy):
    r = lax.rsqrt(jnp.mean(x * x, axis=-1, keepdims=True) + NORM_EPS)
    u = dy * g
    dx = r * u - (r * r * r) * x * jnp.mean(u * x, axis=-1, keepdims=True)
    return dx, dy * x * r


def _colsum8(v):
    s = jnp.sum(v, axis=0, keepdims=True)
    row = lax.broadcasted_iota(jnp.int32, (8, v.shape[1]), 0)
    return jnp.where(row == 0, jnp.broadcast_to(s, (8, v.shape[1])), 0.0)


def _full(shape):
    nd = len(shape)
    return pl.BlockSpec(shape, lambda *_: (0,) * nd, pipeline_mode=pl.Buffered(1))


_ANY = pl.BlockSpec(memory_space=pl.ANY)


def _pos():
    return lax.axis_index("x"), lax.axis_index("y"), lax.axis_index("c")


def _other_chips(x, y):
    return [(1 - x, y), (x, 1 - y), (1 - x, 1 - y)]


class BgGather:
    def __init__(self, arrs):
        n = len(arrs)
        self.arrs = list(arrs)
        self.out_shape = [SDS((NDEV,) + a.shape, a.dtype) for a in arrs]
        self.scratch = [pltpu.SemaphoreType.DMA((n, 7)), pltpu.SemaphoreType.DMA((n, 7)),
                        pltpu.SemaphoreType.DMA((n,))]
        self.has_mid = True
        self.result = None

    def _copy(self, ins, outs, sems, a, k, block, to, own=False):
        dst = outs[a].at[4 * block[0] + 2 * block[1] + block[2]]
        return pltpu.make_async_remote_copy(
            src_ref=ins[a] if own else dst, dst_ref=dst, send_sem=sems[0].at[a, k], recv_sem=sems[1].at[a, k],
            device_id=to, device_id_type=MESH)

    def _mine(self, ins, outs, sems):
        x, y, c = _pos()
        return [pltpu.make_async_copy(ins[a], outs[a].at[4 * x + 2 * y + c], sems[2].at[a])
                for a in range(len(self.arrs))]

    def _first(self, ins, outs, sems):
        x, y, c = _pos()
        me = (x, y, c)
        cps = []
        for a in range(len(self.arrs)):
            cps.append(self._copy(ins, outs, sems, a, 0, me, (x, y, 1 - c), own=True))
            cps += [self._copy(ins, outs, sems, a, 1 + j, me, (*chip, c), own=True)
                    for j, chip in enumerate(_other_chips(x, y))]
        return cps

    def _passed(self, ins, outs, sems):
        x, y, c = _pos()
        return [self._copy(ins, outs, sems, a, 4 + j, (*chip, c), (x, y, 1 - c))
                for j, chip in enumerate(_other_chips(x, y)) for a in range(len(self.arrs))]

    def start(self, ins, outs, sems):
        for cp in self._mine(ins, outs, sems) + self._first(ins, outs, sems):
            cp.start()

    def mid(self, ins, outs, sems):
        x, y, c = _pos()
        for j, chip in enumerate(_other_chips(x, y)):
            for a in range(len(self.arrs)):
                self._copy(ins, outs, sems, a, 1 + j, (*chip, c), (x, y, c)).wait_recv()
                self._copy(ins, outs, sems, a, 4 + j, (*chip, c), (x, y, 1 - c)).start()

    def finish(self, ins, outs, sems):
        x, y, c = _pos()
        for a in range(len(self.arrs)):
            self._copy(ins, outs, sems, a, 0, (x, y, 1 - c), (x, y, c)).wait_recv()
            for j, chip in enumerate(_other_chips(x, y)):
                self._copy(ins, outs, sems, a, 4 + j, (*chip, 1 - c), (x, y, c)).wait_recv()
        for cp in self._first(ins, outs, sems) + self._passed(ins, outs, sems):
            cp.wait_send()
        for cp in self._mine(ins, outs, sems):
            cp.wait()


class BgPair:
    def __init__(self, arrs):
        n = len(arrs)
        self.arrs = list(arrs)
        self.out_shape = [SDS((4,) + a.shape[1:], a.dtype) for a in arrs]
        self.scratch = [pltpu.SemaphoreType.DMA((n, 4)), pltpu.SemaphoreType.DMA((n, 4))]
        self.has_mid = False
        self.result = None

    def _copies(self, ins, outs, sems):
        x, y, c = _pos()
        return [pltpu.make_async_remote_copy(
            src_ref=ins[a].at[2 * k + 1 - c], dst_ref=outs[a].at[k], send_sem=sems[0].at[a, k],
            recv_sem=sems[1].at[a, k], device_id=(x, y, 1 - c), device_id_type=MESH)
            for a in range(len(self.arrs)) for k in range(4)]

    def start(self, ins, outs, sems):
        for cp in self._copies(ins, outs, sems):
            cp.start()

    def finish(self, ins, outs, sems):
        cps = self._copies(ins, outs, sems)
        for cp in cps:
            cp.wait_recv()
        for cp in cps:
            cp.wait_send()


class BgChips(BgPair):
    def __init__(self, arrs):
        n = len(arrs)
        self.arrs = list(arrs)
        self.out_shape = [SDS((3,) + a.shape[1:], a.dtype) for a in arrs]
        self.scratch = [pltpu.SemaphoreType.DMA((n, 3)), pltpu.SemaphoreType.DMA((n, 3))]
        self.has_mid = False
        self.result = None

    def _copies(self, ins, outs, sems):
        x, y, c = _pos()
        return [pltpu.make_async_remote_copy(
            src_ref=ins[a].at[2 * px + py], dst_ref=outs[a].at[r], send_sem=sems[0].at[a, r],
            recv_sem=sems[1].at[a, r], device_id=(px, py, c), device_id_type=MESH)
            for a in range(len(self.arrs)) for r, (px, py) in enumerate(_other_chips(x, y))]


def _call(bgs, body, *, name, grid, in_specs, out_specs, out_shape, scratch_shapes=(), compiler_params=None):
    single = not isinstance(out_shape, (list, tuple))
    out_specs_l = [out_specs] if single else list(out_specs)
    out_shape_l = [out_shape] if single else list(out_shape)
    bgs = [b for b in (bgs or []) if b is not None]
    n_in, n_out, n_sc = len(in_specs), len(out_shape_l), len(scratch_shapes)
    nsteps = math.prod(grid)

    def full(*refs):
        pos = [0]

        def take(k):
            r = refs[pos[0]:pos[0] + k]
            pos[0] += k
            return r

        ins = take(n_in)
        b_ins = [take(len(b.arrs)) for b in bgs]
        outs = take(n_out)
        b_outs = [take(len(b.out_shape)) for b in bgs]
        sc = take(n_sc)
        b_sc = [take(len(b.scratch)) for b in bgs]
        if bgs:
            step = pl.program_id(0)
            for d in range(1, len(grid)):
                step = step * grid[d] + pl.program_id(d)

            @pl.when(step == 0)
            def _():
                for b, i_, o_, s_ in zip(bgs, b_ins, b_outs, b_sc):
                    b.start(i_, o_, s_)

        body(*ins, *outs, *sc)
        if bgs:
            for b, i_, o_, s_ in zip(bgs, b_ins, b_outs, b_sc):
                if b.has_mid:
                    @pl.when(step == max(0, (3 * nsteps) // 4 - 1))
                    def _():
                        b.mid(i_, o_, s_)

            @pl.when(step == nsteps - 1)
            def _():
                for b, i_, o_, s_ in zip(bgs, b_ins, b_outs, b_sc):
                    b.finish(i_, o_, s_)

    def run(*args):
        res = pl.pallas_call(
            full, name=name, grid=grid,
            in_specs=list(in_specs) + [_ANY] * sum(len(b.arrs) for b in bgs),
            out_specs=out_specs_l + [_ANY] * sum(len(b.out_shape) for b in bgs),
            out_shape=out_shape_l + [s for b in bgs for s in b.out_shape],
            scratch_shapes=list(scratch_shapes) + [s for b in bgs for s in b.scratch],
            compiler_params=compiler_params,
        )(*args, *[a for b in bgs for a in b.arrs])
        rest = list(res[n_out:])
        for b in bgs:
            b.result, rest = rest[:len(b.out_shape)], rest[len(b.out_shape):]
        return res[0] if single else list(res[:n_out])

    return run


def s5_discretize(a_re, a_im, log_dt, b_re, b_im, c_re, c_im):
    lam_r = jnp.minimum(a_re, DT_MIN_LAMBDA)
    lam_i = a_im
    dt = jnp.exp(log_dt)[:, None]
    e = jnp.exp(lam_r * dt)
    lbr = e * jnp.cos(lam_i * dt)
    lbi = e * jnp.sin(lam_i * dt)
    den = lam_r * lam_r + lam_i * lam_i
    cf_r = ((lbr - 1.0) * lam_r + lbi * lam_i) / den
    cf_i = (lbi * lam_r - (lbr - 1.0) * lam_i) / den
    bb_r = cf_r[:, :, None] * b_re - cf_i[:, :, None] * b_im
    bb_i = cf_r[:, :, None] * b_im + cf_i[:, :, None] * b_re
    eye = jnp.eye(8, dtype=f32)

    def blk_b(m):
        return jnp.einsum('bgpc,gh->bgchp', m.reshape(8, 8, S5_P, S5_C), eye).reshape(8, 128, 512)

    def blk_c(m):
        return jnp.einsum('bgcp,gh->bgphc', m.reshape(8, 8, S5_C, S5_P), eye).reshape(8, 512, 128)

    bm = jnp.concatenate([blk_b(bb_r), blk_b(bb_i)], axis=-1)
    cm = jnp.concatenate([blk_c(c_re), -blk_c(c_im)], axis=1)
    lam = jnp.stack([lbr.reshape(8, 512), lbi.reshape(8, 512)], axis=1)
    lam = jnp.broadcast_to(lam[:, :, None, :], (8, 2, 8, 512))
    return lam, bm, cm


def _cmul(ar, ai, br, bi):
    return ar * br - ai * bi, ar * bi + ai * br


def _shift_rows(v, k, up):
    row = lax.broadcasted_iota(jnp.int32, v.shape, 0)
    if up:
        return jnp.where(row < 8 - k, pltpu.roll(v, 8 - k, 0), 0.0)
    return jnp.where(row >= k, pltpu.roll(v, k, 0), 0.0)


def _chunk_scan(S, lr, li, reverse, aux=None):
    if reverse:
        li = -li
    z = jnp.zeros((8, 512), f32)

    def idx(i):
        return (S5_STEPS - 1 - i) if reverse else i

    def rec(xr, xi, row):
        br = S[row, 0:512]
        bi = S[row, 512:1024]
        return lr * xr - li * xi + br, lr * xi + li * xr + bi

    def step1(i, c):
        row = pl.ds(pl.multiple_of(idx(i) * 8, 8), 8)
        return rec(c[0], c[1], row)

    er, ei = lax.fori_loop(0, S5_STEPS, step1, (z, z), unroll=8)
    ar, ai = lr, li
    for _ in range(8):
        ar, ai = _cmul(ar, ai, ar, ai)
    cr, ci = _shift_rows(er, 1, reverse), _shift_rows(ei, 1, reverse)
    for k in (1, 2, 4):
        sr, si = _shift_rows(cr, k, reverse), _shift_rows(ci, k, reverse)
        pr, pi_ = _cmul(ar, ai, sr, si)
        cr, ci = cr + pr, ci + pi_
        ar, ai = _cmul(ar, ai, ar, ai)

    if aux is None:
        def step2(i, c):
            row = pl.ds(pl.multiple_of(idx(i) * 8, 8), 8)
            xr, xi = rec(c[0], c[1], row)
            S[row, 0:512] = xr
            S[row, 512:1024] = xi
            return xr, xi

        lax.fori_loop(0, S5_STEPS, step2, (cr, ci), unroll=8)
        return None

    def step2(i, c):
        gr0, gi0, dr, di = c
        s = idx(i)
        row = pl.ds(pl.multiple_of(s * 8, 8), 8)
        gr, gi = rec(gr0, gi0, row)
        S[row, 0:512] = gr
        S[row, 512:1024] = gi
        prow = pl.ds(pl.multiple_of(jnp.maximum(s - 1, 0) * 8, 8), 8)
        xr = aux[prow, 0:512]
        xi = aux[prow, 512:1024]
        dr = dr + gr * xr + gi * xi
        di = di + gi * xr - gr * xi
        return gr, gi, dr, di

    gr, gi, dr, di = lax.fori_loop(0, S5_STEPS - 1, step2, (cr, ci, z, z), unroll=8)
    row0 = pl.ds(0, 8)
    gr, gi = rec(gr, gi, row0)
    S[row0, 0:512] = gr
    S[row0, 512:1024] = gi
    last = pl.ds((S5_STEPS - 1) * 8, 8)
    xr = _shift_rows(aux[last, 0:512], 1, False)
    xi = _shift_rows(aux[last, 512:1024], 1, False)
    dr = dr + gr * xr + gi * xi
    di = di + gi * xr - gr * xi
    return dr, di


_ROWS = 256


def _row_loop(fn):
    def body(r, c):
        fn(pl.ds(pl.multiple_of(r * _ROWS, _ROWS), _ROWS))
        return c
    lax.fori_loop(0, T // _ROWS, body, 0)


def s5_core_fwd(hn, bm, lam, cm, bg=()):
    def body(u_ref, b_ref, lam_ref, c_ref, ys_ref, S):
        def bu(rows):
            S[rows, :] = _dot(u_ref[rows, :], b_ref[...])
        _row_loop(bu)
        _chunk_scan(S, lam_ref[0], lam_ref[1], False)

        def ys(rows):
            ys_ref[rows, :] = _dot(S[rows, :].astype(bf16), c_ref[...])
        _row_loop(ys)

    return _call(
        bg, body, name="s5_core_fwd", grid=(S5_SUB,),
        in_specs=[pl.BlockSpec((T, 128), lambda b: (0, b)),
                  pl.BlockSpec((None, 128, 1024), lambda b: (b, 0, 0)),
                  pl.BlockSpec((None, 2, 8, 512), lambda b: (b, 0, 0, 0)),
                  pl.BlockSpec((None, 1024, 128), lambda b: (b, 0, 0))],
        out_specs=pl.BlockSpec((T, 128), lambda b: (0, b)),
        out_shape=SDS((T, D), f32),
        scratch_shapes=[pltpu.VMEM((T, 1024), f32)],
        compiler_params=_cp(dimension_semantics=("arbitrary",)),
    )(hn, bm, lam, cm)


def s5_core_bwd(hn, dy, bm, lam, cm, bg=()):
    def body(u_ref, dy_ref, b_ref, lam_ref, c_ref, du_ref, db_ref, dct_ref, dlam_ref, S1, S2):
        def bu(rows):
            S1[rows, :] = _dot(u_ref[rows, :], b_ref[...])
        _row_loop(bu)
        _chunk_scan(S1, lam_ref[0], lam_ref[1], False)
        dct_ref[...] = jnp.zeros_like(dct_ref)

        def dx(rows):
            dyb = dy_ref[rows, :].astype(bf16)
            S2[rows, :] = _dot_nt(dyb, c_ref[...])
            dct_ref[...] += _dot_tn(dyb, S1[rows, :].astype(bf16))
        _row_loop(dx)
        dr, di = _chunk_scan(S2, lam_ref[0], lam_ref[1], True, aux=S1)
        dlam_ref[0] = dr
        dlam_ref[1] = di
        db_ref[...] = jnp.zeros_like(db_ref)

        def dbu(rows):
            gb = S2[rows, :].astype(bf16)
            db_ref[...] += _dot_tn(u_ref[rows, :], gb)
            du_ref[rows, :] = _dot_nt(gb, b_ref[...])
        _row_loop(dbu)

    return _call(
        bg, body, name="s5_core_bwd", grid=(S5_SUB,),
        in_specs=[pl.BlockSpec((T, 128), lambda b: (0, b)),
                  pl.BlockSpec((T, 128), lambda b: (0, b)),
                  pl.BlockSpec((None, 128, 1024), lambda b: (b, 0, 0)),
                  pl.BlockSpec((None, 2, 8, 512), lambda b: (b, 0, 0, 0)),
                  pl.BlockSpec((None, 1024, 128), lambda b: (b, 0, 0))],
        out_specs=[pl.BlockSpec((T, 128), lambda b: (0, b)),
                   pl.BlockSpec((None, 128, 1024), lambda b: (b, 0, 0)),
                   pl.BlockSpec((None, 128, 1024), lambda b: (b, 0, 0)),
                   pl.BlockSpec((None, 2, 8, 512), lambda b: (b, 0, 0, 0))],
        out_shape=[SDS((T, D), f32), SDS((8, 128, 1024), f32), SDS((8, 128, 1024), f32), SDS((8, 2, 8, 512), f32)],
        scratch_shapes=[pltpu.VMEM((T, 1024), f32), pltpu.VMEM((T, 1024), f32)],
        compiler_params=_cp(dimension_semantics=("arbitrary",)),
    )(hn, dy, bm, lam, cm)


TM = 512
NT = T // TM


def _tile(n=D):
    return pl.BlockSpec((TM, n), lambda i: (i, 0))


def s5_pre(xp, g):
    def body(x_ref, g_ref, hn_ref):
        hn, _ = _rms(x_ref[...], g_ref[...])
        hn_ref[...] = hn.astype(bf16)

    return pl.pallas_call(
        body, name="s5_pre", grid=(NT,), in_specs=[_tile(), _full((1, D))], out_specs=_tile(),
        out_shape=SDS((T, D), bf16), compiler_params=_cp(dimension_semantics=("arbitrary",)),
    )(xp, g)


def _gelu_grad(y):
    c = math.sqrt(2.0 / math.pi)
    t = jnp.tanh(c * (y + 0.044715 * y * y * y))
    return 0.5 * (1.0 + t) + 0.5 * y * (1.0 - t * t) * c * (1.0 + 3.0 * 0.044715 * y * y)


def s5_post(ys, xp, g, d, wglu, bglu, bg=()):
    def body(ys_ref, x_ref, g_ref, d_ref, w_ref, b_ref, y_ref, z_ref, h_ref):
        x = x_ref[...]
        hn, _ = _rms(x, g_ref[...])
        y = ys_ref[...] + d_ref[...] * hn
        y_ref[...] = y
        yg = jax.nn.gelu(y).astype(bf16)
        for j in range(4):
            cv = slice(j * 256, (j + 1) * 256)
            cg = slice(1024 + j * 256, 1024 + (j + 1) * 256)
            val = _dot(yg, w_ref[j]) + b_ref[:, cv]
            gate = _dot(yg, w_ref[j + 4]) + b_ref[:, cg]
            z_ref[:, cv] = val
            z_ref[:, cg] = gate
            h_ref[:, cv] = x[:, cv] + val * jax.nn.sigmoid(gate)

    return _call(
        bg, body, name="s5_post", grid=(NT,),
        in_specs=[_tile(), _tile(), _full((1, D)), _full((1, D)), _full((8, D, 256)), _full((1, 2 * D))],
        out_specs=[_tile(), _tile(2 * D), _tile()],
        out_shape=[SDS((T, D), f32), SDS((T, 2 * D), f32), SDS((T, D), f32)],
        compiler_params=_cp(dimension_semantics=("arbitrary",)),
    )(ys, xp, g, d, wglu, bglu)


def s5_post_bwd(dh, y, z, wglu, bg=()):
    def body(dh_ref, y_ref, z_ref, w_ref, dy_ref, dw_ref, db_ref, acc):
        i = pl.program_id(0)

        @pl.when(i == 0)
        def _():
            acc[...] = jnp.zeros_like(acc)
            db_ref[...] = jnp.zeros_like(db_ref)

        dh_ = dh_ref[...]
        y = y_ref[...]
        yg = jax.nn.gelu(y).astype(bf16)
        dyg = jnp.zeros((TM, D), f32)
        for j in range(4):
            cv = slice(j * 256, (j + 1) * 256)
            cg = slice(1024 + j * 256, 1024 + (j + 1) * 256)
            val = z_ref[:, cv]
            sg = jax.nn.sigmoid(z_ref[:, cg])
            dval = dh_[:, cv] * sg
            dgate = dh_[:, cv] * val * sg * (1.0 - sg)
            db_ref[:, cv] += _colsum8(dval)
            db_ref[:, cg] += _colsum8(dgate)
            dvb = dval.astype(bf16)
            dgb = dgate.astype(bf16)
            acc[j] += _dot_tn(yg, dvb)
            acc[j + 4] += _dot_tn(yg, dgb)
            dyg = dyg + _dot_nt(dvb, w_ref[j]) + _dot_nt(dgb, w_ref[j + 4])
        dy_ref[...] = dyg * _gelu_grad(y)

        @pl.when(i == NT - 1)
        def _():
            dw_ref[...] = acc[...].astype(bf16)

    return _call(
        bg, body, name="s5_post_bwd", grid=(NT,),
        in_specs=[_tile(), _tile(), _tile(2 * D), _full((8, D, 256))],
        out_specs=[_tile(), _full((8, D, 256)), _full((8, 2 * D))],
        out_shape=[SDS((T, D), f32), SDS((8, D, 256), bf16), SDS((8, 2 * D), f32)],
        scratch_shapes=[pltpu.VMEM((8, D, 256), f32)],
        compiler_params=_cp(dimension_semantics=("arbitrary",)),
    )(dh, y, z, wglu)


def s5_pre_bwd(xp, g, du, dy, d, dh, bg=()):
    def body(x_ref, g_ref, du_ref, dy_ref, d_ref, dh_ref, dx_ref, dg_ref, dd_ref):
        i = pl.program_id(0)

        @pl.when(i == 0)
        def _():
            dg_ref[...] = jnp.zeros_like(dg_ref)
            dd_ref[...] = jnp.zeros_like(dd_ref)

        x = x_ref[...]
        g = g_ref[...]
        dy = dy_ref[...]
        hn, _ = _rms(x, g)
        dhn = du_ref[...] + d_ref[...] * dy
        dx, dgt = _rms_bwd(x, g, dhn)
        dx_ref[...] = dh_ref[...] + dx
        dg_ref[...] += _colsum8(dgt)
        dd_ref[...] += _colsum8(dy * hn)

    return _call(
        bg, body, name="s5_pre_bwd", grid=(NT,),
        in_specs=[_tile(), _full((1, D)), _tile(), _tile(), _full((1, D)), _tile()],
        out_specs=[_tile(), _full((8, D)), _full((8, D))],
        out_shape=[SDS((T, D), f32), SDS((8, D), f32), SDS((8, D), f32)],
        compiler_params=_cp(dimension_semantics=("arbitrary",)),
    )(xp, g, du, dy, d, dh)


TMF = 1024


def mlp_fwd(h, g, w_in, w_out, layer, bg=()):
    def body(h_ref, g_ref, wi_ref, wo_ref, hm_ref, out_ref, acc):
        j = pl.program_id(1)

        @pl.when(j == 0)
        def _():
            hm, _ = _rms(h_ref[...], g_ref[...])
            hm_ref[...] = hm.astype(bf16)
            acc[...] = jnp.zeros_like(acc)

        a = jnp.maximum(_dot(hm_ref[...], wi_ref[...]), 0.0)
        acc[...] += _dot((a * a).astype(bf16), wo_ref[...])

        @pl.when(j == NDEV - 1)
        def _():
            out_ref[...] = h_ref[...] + acc[...]

    return _call(
        bg, body, name=f"mlp_fwd{layer}", grid=(T // TMF, NDEV),
        in_specs=[pl.BlockSpec((TMF, D), lambda i, j: (i, 0)),
                  pl.BlockSpec((1, D), lambda i, j: (0, 0)),
                  pl.BlockSpec((None, D, D_FF_SHARD), lambda i, j: (j, 0, 0)),
                  pl.BlockSpec((None, D_FF_SHARD, D), lambda i, j: (j, 0, 0))],
        out_specs=[pl.BlockSpec((TMF, D), lambda i, j: (i, 0)), pl.BlockSpec((TMF, D), lambda i, j: (i, 0))],
        out_shape=[SDS((T, D), bf16), SDS((T, D), f32)],
        scratch_shapes=[pltpu.VMEM((TMF, D), f32)],
        compiler_params=_cp(dimension_semantics=("arbitrary", "arbitrary")),
    )(h, g, w_in, w_out)


def mlp_bwd(h, hm, g, dout, w_in, w_out, layer, bg=()):
    last = NDEV - 1

    def body(h_ref, hm_ref, g_ref, do_ref, wi_ref, wo_ref, dh_ref, dwi_ref, dwo_ref, dg_ref, dhm, awi, awo):
        j = pl.program_id(0)
        i = pl.program_id(1)
        rows = pl.ds(pl.multiple_of(i * TM, TM), TM)

        @pl.when(i == 0)
        def _():
            awi[...] = jnp.zeros_like(awi)
            awo[...] = jnp.zeros_like(awo)

        hm_ = hm_ref[...]
        dob = do_ref[...].astype(bf16)
        r = jnp.maximum(_dot(hm_, wi_ref[...]), 0.0)
        dz = (_dot_nt(dob, wo_ref[...]) * (2.0 * r)).astype(bf16)
        awo[...] += _dot_tn((r * r).astype(bf16), dob)
        awi[...] += _dot_tn(hm_, dz)
        part = _dot_nt(dz, wi_ref[...])

        @pl.when(j == 0)
        def _():
            dhm[rows, :] = part

        @pl.when(j > 0)
        def _():
            dhm[rows, :] += part

        @pl.when(i == NT - 1)
        def _():
            dwi_ref[...] = awi[...].astype(bf16)
            dwo_ref[...] = awo[...].astype(bf16)

        @pl.when(j == last)
        def _():
            @pl.when(i == 0)
            def _():
                dg_ref[...] = jnp.zeros_like(dg_ref)
            dx, dgt = _rms_bwd(h_ref[...], g_ref[...], dhm[rows, :])
            dh_ref[...] = do_ref[...] + dx
            dg_ref[...] += _colsum8(dgt)

    late = lambda j, i: (jnp.where(j == last, i, 0), 0)
    return _call(
        bg, body, name=f"mlp_bwd{layer}", grid=(NDEV, NT),
        in_specs=[pl.BlockSpec((TM, D), late),
                  pl.BlockSpec((TM, D), lambda j, i: (i, 0)),
                  pl.BlockSpec((1, D), lambda j, i: (0, 0)),
                  pl.BlockSpec((TM, D), lambda j, i: (i, 0)),
                  pl.BlockSpec((None, D, D_FF_SHARD), lambda j, i: (j, 0, 0)),
                  pl.BlockSpec((None, D_FF_SHARD, D), lambda j, i: (j, 0, 0))],
        out_specs=[pl.BlockSpec((TM, D), late),
                   pl.BlockSpec((None, D, D_FF_SHARD), lambda j, i: (j, 0, 0)),
                   pl.BlockSpec((None, D_FF_SHARD, D), lambda j, i: (j, 0, 0)),
                   pl.BlockSpec((8, D), lambda j, i: (0, 0))],
        out_shape=[SDS((T, D), f32), SDS((NDEV, D, D_FF_SHARD), bf16), SDS((NDEV, D_FF_SHARD, D), bf16),
                   SDS((8, D), f32)],
        scratch_shapes=[pltpu.VMEM((T, D), f32), pltpu.VMEM((D, D_FF_SHARD), f32), pltpu.VMEM((D_FF_SHARD, D), f32)],
        compiler_params=_cp(dimension_semantics=("arbitrary", "arbitrary")),
    )(h, hm, g, dout, w_in, w_out)


def attn_pre(h, g_kv, g_mix, wk4, wv4, bk4, bv4, wq, bq):
    def body(h_ref, gkv_ref, gm_ref, wk_ref, wv_ref, bk_ref, bv_ref, wq_ref, bq_ref,
             kvn_ref, hn_ref, k_ref, v_ref, q_ref):
        h_ = h_ref[...]
        kvn = _rms(h_, gkv_ref[...])[0].astype(bf16)
        hn = _rms(h_, gm_ref[...])[0].astype(bf16)
        kvn_ref[...] = kvn
        hn_ref[...] = hn
        k_ref[...] = (_dot(kvn, wk_ref[...]) + bk_ref[...]).astype(bf16)
        v_ref[...] = (_dot(kvn, wv_ref[...]) + bv_ref[...]).astype(bf16)
        q_ref[...] = (_dot(hn, wq_ref[...]) + bq_ref[...]).astype(bf16)

    return pl.pallas_call(
        body, name="attn_pre", grid=(NT,),
        in_specs=[_tile(), _full((1, D)), _full((1, D)), _full((D, D)), _full((D, D)), _full((1, D)), _full((1, D)),
                  _full((D, D)), _full((1, D))],
        out_specs=[_tile()] * 5,
        out_shape=[SDS((T, D), bf16)] * 5,
        compiler_params=_cp(dimension_semantics=("arbitrary",)),
    )(h, g_kv, g_mix, wk4, wv4, bk4, bv4, wq, bq)


def _attn_specs():
    cur = pl.BlockSpec((TM, 256), lambda j, n: (n, j))
    prev = pl.BlockSpec((BLK, 256), lambda j, n: (jnp.maximum(n * (TM // BLK) - 1, 0), j))
    return cur, prev


def _attn_probs(qg, k2, sink, first):
    s = _dot_nt(qg, k2) * (1.0 / math.sqrt(HEAD_DIM))
    qi = lax.broadcasted_iota(jnp.int32, (BLK, 2 * BLK), 0)
    kj = lax.broadcasted_iota(jnp.int32, (BLK, 2 * BLK), 1)
    diff = qi + BLK - kj
    valid = (diff >= 0) & (diff < BLK) & (jnp.logical_not(first) | (kj >= BLK))
    s = jnp.where(valid, s, -jnp.inf)
    m = jnp.maximum(jnp.max(s, axis=-1, keepdims=True), sink)
    p = jnp.exp(s - m)
    ps = jnp.exp(sink - m)
    denom = jnp.sum(p, axis=-1, keepdims=True) + ps
    return p / denom, ps / denom


def _head_mask(g):
    lane = lax.broadcasted_iota(jnp.int32, (1, 256), 1)
    return (lane >= g * HEAD_DIM) & (lane < (g + 1) * HEAD_DIM)


def attn_core_fwd(q, k4, v4, sinks, bg=()):
    nb = TM // BLK

    def body(sink_ref, q_ref, kc_ref, kp_ref, vc_ref, vp_ref, o_ref):
        j = pl.program_id(0)
        n = pl.program_id(1)
        for b in range(nb):
            qb = q_ref[b * BLK:(b + 1) * BLK, :]
            if b == 0:
                k2 = jnp.concatenate([kp_ref[...], kc_ref[0:BLK, :]], axis=0)
                v2 = jnp.concatenate([vp_ref[...], vc_ref[0:BLK, :]], axis=0)
                first = n == 0
            else:
                k2 = kc_ref[(b - 1) * BLK:(b + 1) * BLK, :]
                v2 = vc_ref[(b - 1) * BLK:(b + 1) * BLK, :]
                first = False
            acc = jnp.zeros((BLK, 256), f32)
            for g in range(Q_PER_KV):
                mk = _head_mask(g)
                a, _ = _attn_probs(jnp.where(mk, qb, 0), k2, sink_ref[j * Q_PER_KV + g], first)
                acc = acc + _dot(a.astype(bf16), jnp.where(mk, v2, 0))
            o_ref[b * BLK:(b + 1) * BLK, :] = acc.astype(bf16)

    cur, prev = _attn_specs()
    return _call(
        bg, body, name="attn_core_fwd", grid=(N_KV, NT),
        in_specs=[pl.BlockSpec(memory_space=pltpu.SMEM), cur, cur, prev, cur, prev],
        out_specs=cur, out_shape=SDS((T, D), bf16),
        compiler_params=_cp(dimension_semantics=("arbitrary", "arbitrary")),
    )(sinks, q, k4, k4, v4, v4)


def attn_post(h, o, wo, bo):
    def body(h_ref, o_ref, w_ref, b_ref, out_ref):
        out_ref[...] = h_ref[...] + _dot(o_ref[...], w_ref[...]) + b_ref[...]

    return pl.pallas_call(
        body, name="attn_post", grid=(NT,), in_specs=[_tile(), _tile(), _full((D, D)), _full((1, D))],
        out_specs=_tile(), out_shape=SDS((T, D), f32), compiler_params=_cp(dimension_semantics=("arbitrary",)),
    )(h, o, wo, bo)


def attn_bwd_pre(dh, o, wo, bg=()):
    def body(dh_ref, o_ref, w_ref, do_ref, dw_ref, db_ref, acc):
        i = pl.program_id(0)

        @pl.when(i == 0)
        def _():
            acc[...] = jnp.zeros_like(acc)
            db_ref[...] = jnp.zeros_like(db_ref)

        dh_ = dh_ref[...]
        dhb = dh_.astype(bf16)
        do_ref[...] = _dot_nt(dhb, w_ref[...]).astype(bf16)
        acc[...] += _dot_tn(o_ref[...], dhb)
        db_ref[...] += _colsum8(dh_)

        @pl.when(i == NT - 1)
        def _():
            dw_ref[...] = acc[...].astype(bf16)

    return _call(
        bg, body, name="attn_bwd_pre", grid=(NT,), in_specs=[_tile(), _tile(), _full((D, D))],
        out_specs=[_tile(), _full((D, D)), _full((8, D))],
        out_shape=[SDS((T, D), bf16), SDS((D, D), bf16), SDS((8, D), f32)],
        scratch_shapes=[pltpu.VMEM((D, D), f32)],
        compiler_params=_cp(dimension_semantics=("arbitrary",)),
    )(dh, o, wo)


def attn_core_bwd(q, do, k4, v4, sinks, bg=()):
    nb = TM // BLK

    def body(sink_ref, q_ref, do_ref, kc_ref, kp_ref, vc_ref, vp_ref, dq_ref, dk_ref, dv_ref, ds_ref):
        j = pl.program_id(0)
        n = pl.program_id(1)

        @pl.when(n == 0)
        def _():
            dk_ref[...] = jnp.zeros_like(dk_ref)
            dv_ref[...] = jnp.zeros_like(dv_ref)
            ds_ref[...] = jnp.zeros_like(ds_ref)

        lane8 = lax.broadcasted_iota(jnp.int32, (8, 128), 1)
        row8 = lax.broadcasted_iota(jnp.int32, (8, 128), 0)
        for b in range(nb):
            qb = q_ref[b * BLK:(b + 1) * BLK, :]
            dob = do_ref[b * BLK:(b + 1) * BLK, :]
            if b == 0:
                k2 = jnp.concatenate([kp_ref[...], kc_ref[0:BLK, :]], axis=0)
                v2 = jnp.concatenate([vp_ref[...], vc_ref[0:BLK, :]], axis=0)
                first = n == 0
            else:
                k2 = kc_ref[(b - 1) * BLK:(b + 1) * BLK, :]
                v2 = vc_ref[(b - 1) * BLK:(b + 1) * BLK, :]
                first = False
            dq = jnp.zeros((BLK, 256), f32)
            dk2 = jnp.zeros((2 * BLK, 256), f32)
            dv2 = jnp.zeros((2 * BLK, 256), f32)
            for g in range(Q_PER_KV):
                mk = _head_mask(g)
                qg = jnp.where(mk, qb, 0)
                dog = jnp.where(mk, dob, 0)
                a, asink = _attn_probs(qg, k2, sink_ref[j * Q_PER_KV + g], first)
                dp = _dot_nt(dog, v2)
                dd = jnp.sum(a * dp, axis=-1, keepdims=True)
                dsc = (a * (dp - dd) * (1.0 / math.sqrt(HEAD_DIM))).astype(bf16)
                dsink = -jnp.sum(asink * dd, axis=0, keepdims=True)
                ds_ref[...] += jnp.where((lane8 == g) & (row8 == 0), jnp.broadcast_to(dsink, (8, 128)), 0.0)
                dq = dq + _dot(dsc, jnp.where(mk, k2, 0))
                dk2 = dk2 + _dot_tn(dsc, qg)
                dv2 = dv2 + _dot_tn(a.astype(bf16), dog)
            dq_ref[b * BLK:(b + 1) * BLK, :] = dq
            cur = pl.ds(pl.multiple_of(n * TM + b * BLK, BLK), BLK)
            dk_ref[cur, :] += dk2[BLK:, :]
            dv_ref[cur, :] += dv2[BLK:, :]
            if b == 0:
                @pl.when(n > 0)
                def _():
                    prv = pl.ds(pl.multiple_of(n * TM - BLK, BLK), BLK)
                    dk_ref[prv, :] += dk2[:BLK, :]
                    dv_ref[prv, :] += dv2[:BLK, :]
            else:
                prv = pl.ds(pl.multiple_of(n * TM + (b - 1) * BLK, BLK), BLK)
                dk_ref[prv, :] += dk2[:BLK, :]
                dv_ref[prv, :] += dv2[:BLK, :]

    cur, prev = _attn_specs()
    col = pl.BlockSpec((T, 256), lambda j, n: (0, j))
    return _call(
        bg, body, name="attn_core_bwd", grid=(N_KV, NT),
        in_specs=[pl.BlockSpec(memory_space=pltpu.SMEM), cur, cur, cur, prev, cur, prev],
        out_specs=[cur, col, col, pl.BlockSpec((None, 8, 128), lambda j, n: (j, 0, 0))],
        out_shape=[SDS((T, D), f32), SDS((T, D), f32), SDS((T, D), f32), SDS((N_KV, 8, 128), f32)],
        compiler_params=_cp(dimension_semantics=("arbitrary", "arbitrary")),
    )(sinks, q, do, k4, k4, v4, v4)


def attn_bwd_q(h, dh, dq, hn, g_mix, wq):
    def body(h_ref, dh_ref, dq_ref, hn_ref, gm_ref, wq_ref, out_ref, dwq_ref, dbq_ref, dgm_ref, aq):
        i = pl.program_id(0)

        @pl.when(i == 0)
        def _():
            aq[...] = jnp.zeros_like(aq)
            dbq_ref[...] = jnp.zeros_like(dbq_ref)
            dgm_ref[...] = jnp.zeros_like(dgm_ref)

        dq_ = dq_ref[...]
        dqb = dq_.astype(bf16)
        aq[...] += _dot_tn(hn_ref[...], dqb)
        dbq_ref[...] += _colsum8(dq_)
        dx, dg = _rms_bwd(h_ref[...], gm_ref[...], _dot_nt(dqb, wq_ref[...]))
        out_ref[...] = dh_ref[...] + dx
        dgm_ref[...] += _colsum8(dg)

        @pl.when(i == NT - 1)
        def _():
            dwq_ref[...] = aq[...].astype(bf16)

    vec = _full((8, D))
    mat = _full((D, D))
    return pl.pallas_call(
        body, name="attn_bwd_q", grid=(NT,),
        in_specs=[_tile()] * 4 + [_full((1, D)), mat],
        out_specs=[_tile(), mat, vec, vec],
        out_shape=[SDS((T, D), f32), SDS((D, D), bf16), SDS((8, D), f32), SDS((8, D), f32)],
        scratch_shapes=[pltpu.VMEM((D, D), f32)],
        compiler_params=_cp(dimension_semantics=("arbitrary",)),
    )(h, dh, dq, hn, g_mix, wq)


def attn_bwd_kv(h, dh, dk4, dv4, kvn, g_kv, wk4, wv4):
    def body(h_ref, dh_ref, dk_ref, dv_ref, kvn_ref, gkv_ref, wk_ref, wv_ref,
             out_ref, dwk_ref, dwv_ref, dbk_ref, dbv_ref, dgkv_ref):
        i = pl.program_id(0)

        @pl.when(i == 0)
        def _():
            for r in (dwk_ref, dwv_ref, dbk_ref, dbv_ref, dgkv_ref):
                r[...] = jnp.zeros_like(r)

        dk_ = dk_ref[...]
        dv_ = dv_ref[...]
        dkb, dvb = dk_.astype(bf16), dv_.astype(bf16)
        dkvn = _dot_nt(dkb, wk_ref[...]) + _dot_nt(dvb, wv_ref[...])
        dwk_ref[...] += _dot_tn(kvn_ref[...], dkb)
        dwv_ref[...] += _dot_tn(kvn_ref[...], dvb)
        dbk_ref[...] += _colsum8(dk_)
        dbv_ref[...] += _colsum8(dv_)
        dx, dg = _rms_bwd(h_ref[...], gkv_ref[...], dkvn)
        out_ref[...] = dh_ref[...] + dx
        dgkv_ref[...] += _colsum8(dg)

    vec = _full((8, D))
    mat = _full((D, D))
    return pl.pallas_call(
        body, name="attn_bwd_kv", grid=(NT,),
        in_specs=[_tile()] * 5 + [_full((1, D)), mat, mat],
        out_specs=[_tile(), mat, mat, vec, vec, vec],
        out_shape=[SDS((T, D), f32), SDS((D, D), f32), SDS((D, D), f32)] + [SDS((8, D), f32)] * 3,
        compiler_params=_cp(dimension_semantics=("arbitrary",)),
    )(h, dh, dk4, dv4, kvn, g_kv, wk4, wv4)


def final_loss(h, g, target):
    def body(h_ref, g_ref, t_ref, loss_ref, dh_ref, dg_ref):
        i = pl.program_id(0)

        @pl.when(i == 0)
        def _():
            loss_ref[...] = jnp.zeros_like(loss_ref)
            dg_ref[...] = jnp.zeros_like(dg_ref)

        h_ = h_ref[...]
        g_ = g_ref[...]
        y, _ = _rms(h_, g_)
        diff = y - t_ref[...]
        per_tok = jnp.mean(diff * diff, axis=-1, keepdims=True)
        tot = 0.5 * jnp.sum(per_tok, axis=0, keepdims=True)
        lane = lax.broadcasted_iota(jnp.int32, (8, 128), 1)
        row = lax.broadcasted_iota(jnp.int32, (8, 128), 0)
        loss_ref[...] += jnp.where((lane == 0) & (row == 0), jnp.broadcast_to(tot, (8, 128)), 0.0)
        dx, dgt = _rms_bwd(h_, g_, diff * (1.0 / D))
        dh_ref[...] = dx
        dg_ref[...] += _colsum8(dgt)

    return pl.pallas_call(
        body, name="final_loss", grid=(NT,), in_specs=[_tile(), _full((1, D)), _tile()],
        out_specs=[_full((8, 128)), _tile(), _full((8, D))],
        out_shape=[SDS((8, 128), f32), SDS((T, D), f32), SDS((8, D), f32)],
        compiler_params=_cp(dimension_semantics=("arbitrary",)),
    )(h, g, target)


def _to_chunked(a):
    return a.reshape(S5_CH, S5_STEPS, a.shape[-1]).transpose(1, 0, 2).reshape(T, a.shape[-1])


def _from_chunked(a):
    return a.reshape(S5_STEPS, S5_CH, a.shape[-1]).transpose(1, 0, 2).reshape(T, a.shape[-1])


def _rep4(w):
    return jnp.broadcast_to(w.reshape(w.shape[0], N_KV, 1, HEAD_DIM), (w.shape[0], N_KV, Q_PER_KV, HEAD_DIM)).reshape(
        w.shape[0], N_KV * Q_PER_KV * HEAD_DIM)


def _fold4(w):
    return w.reshape(w.shape[0], N_KV, Q_PER_KV, HEAD_DIM).sum(axis=2).reshape(w.shape[0], N_KV * HEAD_DIM)


def fwd_bwd(x, target, p, shards, core):
    row = lambda v: v.reshape(1, -1)
    (lam, bm, cm), prep_vjp = jax.vjp(s5_discretize, p["s5_a_re"][0], p["s5_a_im"][0], p["s5_log_dt"][0],
                                      p["s5_b_re"][0], p["s5_b_im"][0], p["s5_c_re"][0], p["s5_c_im"][0])
    bmb, cmb = bm.astype(bf16), cm.astype(bf16)
    g_mix0, g_mix1 = row(p["norm_mix"][0]), row(p["norm_mix"][1])
    g_mlp0, g_mlp1 = row(p["norm_mlp"][0]), row(p["norm_mlp"][1])
    g_kv, g_fin = row(p["norm_kv"]), row(p["norm_final"])
    bq, bo = p["b_q"], p["b_o"]
    bk4, bv4 = _rep4(row(p["b_kv"])[:, :256]), _rep4(row(p["b_kv"])[:, 256:])
    sinks = p["sinks"].reshape(16)

    def reduce_pairs(names, bg):
        return [add_pairs(g, r, core, f"add_pairs_{n}") for n, g, r in zip(names, bg.arrs, bg.result)]

    xp = _to_chunked(x)
    hn0 = s5_pre(xp, g_mix0)
    ga = BgGather([shards["s5_w_glu"], shards["vecs"], shards["w_in0"]])
    ys = s5_core_fwd(hn0, bmb, lam, cmb, bg=[ga])
    wglu, gvec, win0 = ga.result
    d_skip = gvec[:, 0, :128].reshape(1, D)
    bglu = gvec[:, 0, 128:].reshape(1, 2 * D)
    gb = BgGather([shards["w_out0"]])
    y, z, h1 = s5_post(ys, xp, g_mix0, d_skip, wglu, bglu, bg=[gb])
    wout0, = gb.result
    gc = BgGather([shards["w_kv"], shards["w_q"], shards["w_o"], shards["w_in1"]])
    hm0, h2p = mlp_fwd(h1, g_mlp0, win0, wout0, 0, bg=[gc])
    wkv, wq, wo, win1 = gc.result
    wkv, wq, wo = wkv.reshape(D, 512), wq.reshape(D, D), wo.reshape(D, D)
    wk4, wv4 = _rep4(wkv[:, :256]), _rep4(wkv[:, 256:])
    h2 = _from_chunked(h2p)
    kvn, hn1, k4, v4, q = attn_pre(h2, g_kv, g_mix1, wk4, wv4, bk4, bv4, wq, bq)
    gd = BgGather([shards["w_out1"]])
    o = attn_core_fwd(q, k4, v4, sinks, bg=[gd])
    wout1, = gd.result
    h3 = attn_post(h2, o, wo, bo)
    hm1, h4 = mlp_fwd(h3, g_mlp1, win1, wout1, 1)
    loss, dh4, dg_fin = final_loss(h4, g_fin, target)

    big = {}
    dh3, dwin1, dwout1, dg_mlp1 = mlp_bwd(h3, hm1, g_mlp1, dh4, win1, wout1, 1)
    pa = BgPair([dwin1, dwout1])
    do, dwo, dbo = attn_bwd_pre(dh3, o, wo, bg=[pa])
    ca = BgChips(reduce_pairs(["w_in1", "w_out1"], pa))
    dq, dk4, dv4, dsink = attn_core_bwd(q, do, k4, v4, sinks, bg=[ca])
    big["w_in1"], big["w_out1"] = zip(ca.arrs, ca.result)
    dh2, dwq, dbq, dg_mix1 = attn_bwd_q(h2, dh3, dq, hn1, g_mix1, wq)
    dh2, dwk4, dwv4, dbk4, dbv4, dg_kv = attn_bwd_kv(h2, dh2, dk4, dv4, kvn, g_kv, wk4, wv4)
    dwkv = jnp.concatenate([_fold4(dwk4), _fold4(dwv4)], axis=1).astype(bf16)
    pb = BgPair([dwkv.reshape(NDEV, 128, 512), dwq.reshape(NDEV, 128, D), dwo.reshape(NDEV, 128, D)])
    dh2p = _to_chunked(dh2)
    dh1, dwin0, dwout0, dg_mlp0 = mlp_bwd(h1, hm0, g_mlp0, dh2p, win0, wout0, 0, bg=[pb])
    cb = BgChips(reduce_pairs(["w_kv", "w_q", "w_o"], pb))
    pc = BgPair([dwin0, dwout0])
    dy, dwglu, dbglu = s5_post_bwd(dh1, y, z, wglu, bg=[cb, pc])
    big["w_kv"], big["w_q"], big["w_o"] = zip(cb.arrs, cb.result)
    cc = BgChips(reduce_pairs(["w_in0", "w_out0"], pc))
    pd = BgPair([dwglu])
    du, dbm, dcmt, dlam = s5_core_bwd(hn0, dy, bmb, lam, cmb, bg=[cc, pd])
    big["w_in0"], big["w_out0"] = zip(cc.arrs, cc.result)
    cd = BgChips(reduce_pairs(["s5_w_glu"], pd))
    dxp, dg_mix0, dd = s5_pre_bwd(xp, g_mix0, du, dy, d_skip, dh1, bg=[cd])
    big["s5_w_glu"], = zip(cd.arrs, cd.result)
    grad_x = _from_chunked(dxp)
    da_re, da_im, dlog_dt, db_re, db_im, dc_re, dc_im = prep_vjp((dlam, dbm, dcmt.transpose(0, 2, 1)))

    small = {
        "norm_mix": jnp.stack([dg_mix0[0], dg_mix1[0]]),
        "norm_mlp": jnp.stack([dg_mlp0[0], dg_mlp1[0]]),
        "norm_kv": dg_kv[0], "norm_final": dg_fin[0],
        "s5_a_re": da_re[None], "s5_a_im": da_im[None], "s5_log_dt": dlog_dt[None],
        "s5_b_re": db_re[None], "s5_b_im": db_im[None], "s5_c_re": dc_re[None], "s5_c_im": dc_im[None],
        "s5_d": dd[0:1], "s5_b_glu": dbglu[0:1],
        "b_kv": jnp.concatenate([_fold4(dbk4[0:1]), _fold4(dbv4[0:1])], axis=1)[0],
        "b_q": dbq[0:1], "sinks": dsink[:, 0, :Q_PER_KV].reshape(1, 16), "b_o": dbo[0:1],
    }
    return loss, grad_x, small, big


_ANY = pl.BlockSpec(memory_space=pl.ANY)


def _pos():
    return lax.axis_index("x"), lax.axis_index("y"), lax.axis_index("c")


def _other_chips(x, y):
    return [(1 - x, y), (x, 1 - y), (1 - x, 1 - y)]


def all_gather(arrs):
    n = len(arrs)

    def body(*refs):
        ins, outs = refs[:n], refs[n:2 * n]
        send_sems, recv_sems, local_sems = refs[2 * n:]
        x, y, c = _pos()
        me, sib = (x, y, c), (x, y, 1 - c)
        chips = _other_chips(x, y)

        def copy(a, k, block, to, src=None):
            dst = outs[a].at[4 * block[0] + 2 * block[1] + block[2]]
            return pltpu.make_async_remote_copy(
                src_ref=dst if src is None else src, dst_ref=dst, send_sem=send_sems.at[a, k],
                recv_sem=recv_sems.at[a, k], device_id=to, device_id_type=MESH)

        mine = [pltpu.make_async_copy(ins[a], outs[a].at[4 * x + 2 * y + c], local_sems.at[a]) for a in range(n)]
        for cp in mine:
            cp.start()
        first = []
        for a in range(n):
            first.append(copy(a, 0, me, sib, src=ins[a]))
            first += [copy(a, 1 + j, me, (*chip, c), src=ins[a]) for j, chip in enumerate(chips)]
        for cp in first:
            cp.start()
        passed = []
        for j, chip in enumerate(chips):
            for a in range(n):
                copy(a, 1 + j, (*chip, c), me).wait_recv()
                cp = copy(a, 4 + j, (*chip, c), sib)
                cp.start()
                passed.append(cp)
        for a in range(n):
            copy(a, 0, sib, me).wait_recv()
            for j, chip in enumerate(chips):
                copy(a, 4 + j, (*chip, 1 - c), me).wait_recv()
        for cp in first + passed:
            cp.wait_send()
        for cp in mine:
            cp.wait()

    return pl.pallas_call(
        body, name="all_gather", in_specs=[_ANY] * n, out_specs=[_ANY] * n,
        out_shape=[SDS((NDEV,) + a.shape, a.dtype) for a in arrs],
        scratch_shapes=[pltpu.SemaphoreType.DMA((n, 7)), pltpu.SemaphoreType.DMA((n, 7)),
                        pltpu.SemaphoreType.DMA((n,))],
    )(*arrs)


def rs_pair(grads):
    n = len(grads)

    def body(*refs):
        ins, outs = refs[:n], refs[n:2 * n]
        send_sems, recv_sems = refs[2 * n:]
        x, y, c = _pos()
        cps = []
        for a in range(n):
            for k in range(4):
                cps.append(pltpu.make_async_remote_copy(
                    src_ref=ins[a].at[2 * k + 1 - c], dst_ref=outs[a].at[k], send_sem=send_sems.at[a, k],
                    recv_sem=recv_sems.at[a, k], device_id=(x, y, 1 - c), device_id_type=MESH))
        for cp in cps:
            cp.start()
        for cp in cps:
            cp.wait_recv()
        for cp in cps:
            cp.wait_send()

    return pl.pallas_call(
        body, name="rs_pair", in_specs=[_ANY] * n, out_specs=[_ANY] * n,
        out_shape=[SDS((4,) + g.shape[1:], g.dtype) for g in grads],
        scratch_shapes=[pltpu.SemaphoreType.DMA((n, 4)), pltpu.SemaphoreType.DMA((n, 4))],
    )(*grads)


def rs_chips(parts):
    n = len(parts)

    def body(*refs):
        ins, outs = refs[:n], refs[n:2 * n]
        send_sems, recv_sems = refs[2 * n:]
        x, y, c = _pos()
        cps = []
        for a in range(n):
            for r, (px, py) in enumerate(_other_chips(x, y)):
                cps.append(pltpu.make_async_remote_copy(
                    src_ref=ins[a].at[2 * px + py], dst_ref=outs[a].at[r], send_sem=send_sems.at[a, r],
                    recv_sem=recv_sems.at[a, r], device_id=(px, py, c), device_id_type=MESH))
        for cp in cps:
            cp.start()
        for cp in cps:
            cp.wait_recv()
        for cp in cps:
            cp.wait_send()

    return pl.pallas_call(
        body, name="rs_chips", in_specs=[_ANY] * n, out_specs=[_ANY] * n,
        out_shape=[SDS((3,) + g.shape[1:], g.dtype) for g in parts],
        scratch_shapes=[pltpu.SemaphoreType.DMA((n, 3)), pltpu.SemaphoreType.DMA((n, 3))],
    )(*parts)


def _row_tile(r, c):
    return min(r, max(8, (256 * 1024) // c))


def add_pairs(g, r1, core, name):
    _, R, C = g.shape
    tr = _row_tile(R, C)

    def body(core_ref, g_ref, r_ref, o_ref):
        o_ref[...] = (g_ref[...].astype(f32) + r_ref[...].astype(f32)).astype(bf16)

    return pl.pallas_call(
        body, name=name, out_shape=SDS((4, R, C), bf16),
        grid_spec=pltpu.PrefetchScalarGridSpec(
            num_scalar_prefetch=1, grid=(4, R // tr),
            in_specs=[pl.BlockSpec((None, tr, C), lambda k, i, core: (2 * k + core[0], i, 0)),
                      pl.BlockSpec((None, tr, C), lambda k, i, core: (k, i, 0))],
            out_specs=pl.BlockSpec((None, tr, C), lambda k, i, core: (k, i, 0))),
        compiler_params=_cp(dimension_semantics=("arbitrary", "arbitrary")),
    )(core, g, r1)


def _adamw(w, g, m, v):
    m = ADAM_B1 * m + (1.0 - ADAM_B1) * g
    v = ADAM_B2 * v + (1.0 - ADAM_B2) * (g * g)
    m_hat = m / (1.0 - ADAM_B1 ** ADAM_STEP)
    v_hat = v / (1.0 - ADAM_B2 ** ADAM_STEP)
    delta = -ADAM_LR * (m_hat / (jnp.sqrt(v_hat) + ADAM_EPS) + ADAM_WD * w)
    return delta, m, v


def adam_big(w, m, v, part, r2, chip, name, layer=0, prev=None):
    L, R, C = w.shape
    tr = _row_tile(R, C)

    def body(chip_ref, w_ref, m_ref, v_ref, p_ref, r_ref, *rest):
        g_out, d_out, m_out, v_out = rest[-4:]
        g = p_ref[...].astype(f32) + r_ref[0].astype(f32) + r_ref[1].astype(f32) + r_ref[2].astype(f32)
        d, m_, v_ = _adamw(w_ref[...], g, m_ref[...], v_ref[...])
        g_out[...] = g
        d_out[...] = d
        m_out[...] = m_
        v_out[...] = v_

    blk = pl.BlockSpec((None, tr, C), lambda i, chip: (layer, i, 0))
    extra = [] if prev is None else list(prev)
    return pl.pallas_call(
        body, name=name, out_shape=[SDS((L, R, C), f32)] * 4,
        grid_spec=pltpu.PrefetchScalarGridSpec(
            num_scalar_prefetch=1, grid=(R // tr,),
            in_specs=[blk, blk, blk,
                      pl.BlockSpec((None, tr, C), lambda i, chip: (chip[0], i, 0)),
                      pl.BlockSpec((3, tr, C), lambda i, chip: (0, i, 0))] + [_ANY] * len(extra),
            out_specs=[blk] * 4),
        input_output_aliases={6 + k: k for k in range(len(extra))},
        compiler_params=_cp(dimension_semantics=("arbitrary",)),
    )(chip, w, m, v, part, r2, *extra)


def allreduce_small(buf):
    R = buf.shape[0]

    def body(in_ref, out_ref, acc1, acc2, r0, r1, r2, send_sems, recv_sems):
        x, y, c = _pos()
        peers = [(x, y, 1 - c), (1 - x, y, c), (x, 1 - y, c)]
        srcs, rcvs, dsts = [in_ref, acc1, acc2], [r0, r1, r2], [acc1, acc2, out_ref]
        for s in range(3):
            cp = pltpu.make_async_remote_copy(src_ref=srcs[s], dst_ref=rcvs[s], send_sem=send_sems.at[s],
                                              recv_sem=recv_sems.at[s], device_id=peers[s], device_id_type=MESH)
            cp.start()
            cp.wait()
            dsts[s][...] = srcs[s][...] + rcvs[s][...]

    return pl.pallas_call(
        body, name="allreduce_small", out_shape=SDS((R, 128), f32),
        scratch_shapes=[pltpu.VMEM((R, 128), f32)] * 5 + [pltpu.SemaphoreType.DMA((3,)), pltpu.SemaphoreType.DMA((3,))],
    )(buf)


def adam_small(w, g, m, v):
    def body(w_ref, g_ref, m_ref, v_ref, d_out, m_out, v_out):
        d, m_, v_ = _adamw(w_ref[...], g_ref[...], m_ref[...], v_ref[...])
        d_out[...] = d
        m_out[...] = m_
        v_out[...] = v_

    return pl.pallas_call(body, name="adam_small", out_shape=[SDS(w.shape, f32)] * 3)(w, g, m, v)


WEIGHTS = ['norm_mix', 'norm_mlp', 'norm_kv', 'norm_final', 's5_a_re', 's5_a_im', 's5_log_dt', 's5_b_re', 's5_b_im',
           's5_c_re', 's5_c_im', 's5_d', 's5_w_glu', 's5_b_glu', 'w_kv', 'b_kv', 'w_q', 'b_q', 'sinks', 'w_o', 'b_o',
           'w_mlp_in', 'w_mlp_out']
BIG = ['s5_w_glu', 'w_kv', 'w_q', 'w_o', 'w_mlp_in', 'w_mlp_out']
BIG_2D = {'s5_w_glu': (D, 256), 'w_kv': (128, 512), 'w_q': (128, D), 'w_o': (128, D), 'w_mlp_in': (2 * D, 512),
          'w_mlp_out': (2 * 512, D)}
SHARDED_SMALL = {'s5_d': D, 's5_b_glu': 2 * D}
SMALL = [n for n in WEIGHTS if n not in BIG]
SMALL_SIZE = {'norm_mix': 2 * D, 'norm_mlp': 2 * D, 'norm_kv': D, 'norm_final': D, 's5_a_re': 4096, 's5_a_im': 4096,
              's5_log_dt': 64, 's5_b_re': 65536, 's5_b_im': 65536, 's5_c_re': 65536, 's5_c_im': 65536, 's5_d': D,
              's5_b_glu': 2 * D, 'b_kv': 512, 'b_q': D, 'sinks': 16, 'b_o': D}


def _pack(vals):
    parts = []
    for n in SMALL:
        v = vals[n].reshape(-1).astype(f32)
        parts.append(jnp.pad(v, (0, (-v.shape[0]) % 128)))
    flat = jnp.concatenate(parts)
    flat = jnp.pad(flat, (0, (-flat.shape[0]) % 1024))
    return flat.reshape(-1, 128)


def _unpack(buf):
    flat = buf.reshape(-1)
    out, off = {}, 0
    for n in SMALL:
        sz = SMALL_SIZE[n]
        out[n] = flat[off:off + sz]
        off += sz + (-sz) % 128
    return out


def kernel(x, norm_mix, norm_mlp, norm_kv, norm_final, s5_a_re, s5_a_im, s5_log_dt, s5_b_re, s5_b_im, s5_c_re, s5_c_im, s5_d, s5_w_glu, s5_b_glu, w_kv, b_kv, w_q, b_q, sinks, w_o, b_o, w_mlp_in, w_mlp_out, loss_target, m_norm_mix, m_norm_mlp, m_norm_kv, m_norm_final, m_s5_a_re, m_s5_a_im, m_s5_log_dt, m_s5_b_re, m_s5_b_im, m_s5_c_re, m_s5_c_im, m_s5_d, m_s5_w_glu, m_s5_b_glu, m_w_kv, m_b_kv, m_w_q, m_b_q, m_sinks, m_w_o, m_b_o, m_w_mlp_in, m_w_mlp_out, v_norm_mix, v_norm_mlp, v_norm_kv, v_norm_final, v_s5_a_re, v_s5_a_im, v_s5_log_dt, v_s5_b_re, v_s5_b_im, v_s5_c_re, v_s5_c_im, v_s5_d, v_s5_w_glu, v_s5_b_glu, v_w_kv, v_b_kv, v_w_q, v_b_q, v_sinks, v_w_o, v_b_o, v_w_mlp_in, v_w_mlp_out):
    w = dict(norm_mix=norm_mix, norm_mlp=norm_mlp, norm_kv=norm_kv, norm_final=norm_final, s5_a_re=s5_a_re,
             s5_a_im=s5_a_im, s5_log_dt=s5_log_dt, s5_b_re=s5_b_re, s5_b_im=s5_b_im, s5_c_re=s5_c_re, s5_c_im=s5_c_im,
             s5_d=s5_d, s5_w_glu=s5_w_glu, s5_b_glu=s5_b_glu, w_kv=w_kv, b_kv=b_kv, w_q=w_q, b_q=b_q, sinks=sinks,
             w_o=w_o, b_o=b_o, w_mlp_in=w_mlp_in, w_mlp_out=w_mlp_out)
    m = dict(norm_mix=m_norm_mix, norm_mlp=m_norm_mlp, norm_kv=m_norm_kv, norm_final=m_norm_final, s5_a_re=m_s5_a_re,
             s5_a_im=m_s5_a_im, s5_log_dt=m_s5_log_dt, s5_b_re=m_s5_b_re, s5_b_im=m_s5_b_im, s5_c_re=m_s5_c_re,
             s5_c_im=m_s5_c_im, s5_d=m_s5_d, s5_w_glu=m_s5_w_glu, s5_b_glu=m_s5_b_glu, w_kv=m_w_kv, b_kv=m_b_kv,
             w_q=m_w_q, b_q=m_b_q, sinks=m_sinks, w_o=m_w_o, b_o=m_b_o, w_mlp_in=m_w_mlp_in, w_mlp_out=m_w_mlp_out)
    v = dict(norm_mix=v_norm_mix, norm_mlp=v_norm_mlp, norm_kv=v_norm_kv, norm_final=v_norm_final, s5_a_re=v_s5_a_re,
             s5_a_im=v_s5_a_im, s5_log_dt=v_s5_log_dt, s5_b_re=v_s5_b_re, s5_b_im=v_s5_b_im, s5_c_re=v_s5_c_re,
             s5_c_im=v_s5_c_im, s5_d=v_s5_d, s5_w_glu=v_s5_w_glu, s5_b_glu=v_s5_b_glu, w_kv=v_w_kv, b_kv=v_b_kv,
             w_q=v_w_q, b_q=v_b_q, sinks=v_sinks, w_o=v_w_o, b_o=v_b_o, w_mlp_in=v_w_mlp_in, w_mlp_out=v_w_mlp_out)
    xi, yi, ci = _pos()
    dev = 4 * xi + 2 * yi + ci
    core = ci.reshape(1).astype(jnp.int32)
    chip = (2 * xi + yi).reshape(1).astype(jnp.int32)

    shards = {
        "s5_w_glu": s5_w_glu[0].astype(bf16), "w_kv": w_kv.astype(bf16), "w_q": w_q[0].astype(bf16),
        "w_o": w_o[0].astype(bf16), "w_in0": w_mlp_in[0].astype(bf16), "w_in1": w_mlp_in[1].astype(bf16),
        "w_out0": w_mlp_out[0].astype(bf16), "w_out1": w_mlp_out[1].astype(bf16),
        "vecs": jnp.broadcast_to(jnp.concatenate([s5_d, s5_b_glu], axis=1), (8, 384)),
    }
    loss, grad_x, grads, big = fwd_bwd(x[0], loss_target[0], {n: w[n] for n in SMALL}, shards, core)
    loss = lax.psum(loss[0, 0], ("x", "y", "c"))

    out_g, out_d, out_m, out_v = {}, {}, {}, {}
    for n in ("s5_w_glu", "w_kv", "w_q", "w_o"):
        shp = w[n].shape
        r3 = (1,) + BIG_2D[n]
        res = adam_big(w[n].reshape(r3), m[n].reshape(r3), v[n].reshape(r3), *big[n], chip, f"adam_{n}")
        out_g[n], out_d[n], out_m[n], out_v[n] = [r.reshape(shp) for r in res]
    for n, k in (("w_mlp_in", "w_in"), ("w_mlp_out", "w_out")):
        res = adam_big(w[n], m[n], v[n], *big[k + "1"], chip, f"adam_{k}1", layer=1)
        res = adam_big(w[n], m[n], v[n], *big[k + "0"], chip, f"adam_{k}0", layer=0, prev=res)
        out_g[n], out_d[n], out_m[n], out_v[n] = res

    def full_len(d):
        d = dict(d)
        for n, ln in SHARDED_SMALL.items():
            d[n] = lax.dynamic_update_slice(jnp.zeros((ln,), f32), d[n].reshape(-1), (dev * (ln // NDEV),))
        return d

    gsum = allreduce_small(_pack({n: grads[n] for n in SMALL}))
    ds, ms, vs = adam_small(_pack(full_len({n: w[n] for n in SMALL})), gsum,
                            _pack(full_len({n: m[n] for n in SMALL})), _pack(full_len({n: v[n] for n in SMALL})))
    for src, dst in ((gsum, out_g), (ds, out_d), (ms, out_m), (vs, out_v)):
        for n, val in _unpack(src).items():
            if n in SHARDED_SMALL:
                ln = SHARDED_SMALL[n] // NDEV
                val = lax.dynamic_slice(val, (dev * ln,), (ln,))
            dst[n] = val.reshape(w[n].shape)

    return (loss, grad_x[None], *[out_g[n] for n in WEIGHTS], *[out_d[n] for n in WEIGHTS],
            *[out_m[n] for n in WEIGHTS], *[out_v[n] for n in WEIGHTS])
```

```python
import functools
import math

import jax
import jax.numpy as jnp
from jax import lax
from jax.experimental import pallas as pl
from jax.experimental.pallas import tpu as pltpu

f32 = jnp.float32
bf16 = jnp.bfloat16
SDS = jax.ShapeDtypeStruct

T = 2048
D = 1024
NDEV = 8
NORM_EPS = 1e-5
S5_G, S5_C, S5_P = 64, 16, 64
S5_SUB = 8
S5_CH = 8
S5_STEPS = T // S5_CH
DT_MIN_LAMBDA = -1e-4
HEAD_DIM = 64
N_KV = 4
Q_PER_KV = 4
BLK = 128
D_FF_SHARD = 512
ADAM_LR, ADAM_B1, ADAM_B2, ADAM_EPS, ADAM_WD, ADAM_STEP = 0.001, 0.9, 0.999, 1e-08, 0.01, 10
VMEM_LIMIT = 56 * 1024 * 1024
MESH = pl.DeviceIdType.MESH


def _cp(**kw):
    return pltpu.CompilerParams(vmem_limit_bytes=VMEM_LIMIT, **kw)


def _dot(a, b):
    return jnp.dot(a, b, preferred_element_type=f32)


def _dot_nt(a, b):
    return lax.dot_general(a, b, (((1,), (1,)), ((), ())), preferred_element_type=f32)


def _dot_tn(a, b):
    return lax.dot_general(a, b, (((0,), (0,)), ((), ())), preferred_element_type=f32)


def _rms(x, g):
    r = lax.rsqrt(jnp.mean(x * x, axis=-1, keepdims=True) + NORM_EPS)
    return x * r * g, r


def _rms_bwd(x, g, dy):
    r = lax.rsqrt(jnp.mean(x * x, axis=-1, keepdims=True) + NORM_EPS)
    u = dy * g
    dx = r * u - (r * r * r) * x * jnp.mean(u * x, axis=-1, keepdims=True)
    return dx, dy * x * r


def _colsum8(v):
    s = jnp.sum(v, axis=0, keepdims=True)
    row = lax.broadcasted_iota(jnp.int32, (8, v.shape[1]), 0)
    return jnp.where(row == 0, jnp.broadcast_to(s, (8, v.shape[1])), 0.0)


def _full(shape):
    nd = len(shape)
    return pl.BlockSpec(shape, lambda *_: (0,) * nd, pipeline_mode=pl.Buffered(1))


_ANY = pl.BlockSpec(memory_space=pl.ANY)


def _pos():
    return lax.axis_index("x"), lax.axis_index("y"), lax.axis_index("c")


def _other_chips(x, y):
    return [(1 - x, y), (x, 1 - y), (1 - x, 1 - y)]


class BgGather:
    def __init__(self, arrs):
        n = len(arrs)
        self.arrs = list(arrs)
        self.out_shape = [SDS((NDEV,) + a.shape, a.dtype) for a in arrs]
        self.scratch = [pltpu.SemaphoreType.DMA((n, 7)), pltpu.SemaphoreType.DMA((n, 7)),
                        pltpu.SemaphoreType.DMA((n,))]
        self.has_mid = True
        self.result = None

    def _copy(self, ins, outs, sems, a, k, block, to, own=False):
        dst = outs[a].at[4 * block[0] + 2 * block[1] + block[2]]
        return pltpu.make_async_remote_copy(
            src_ref=ins[a] if own else dst, dst_ref=dst, send_sem=sems[0].at[a, k], recv_sem=sems[1].at[a, k],
            device_id=to, device_id_type=MESH)

    def _mine(self, ins, outs, sems):
        x, y, c = _pos()
        return [pltpu.make_async_copy(ins[a], outs[a].at[4 * x + 2 * y + c], sems[2].at[a])
                for a in range(len(self.arrs))]

    def _first(self, ins, outs, sems):
        x, y, c = _pos()
        me = (x, y, c)
        cps = []
        for a in range(len(self.arrs)):
            cps.append(self._copy(ins, outs, sems, a, 0, me, (x, y, 1 - c), own=True))
            cps += [self._copy(ins, outs, sems, a, 1 + j, me, (*chip, c), own=True)
                    for j, chip in enumerate(_other_chips(x, y))]
        return cps

    def _passed(self, ins, outs, sems):
        x, y, c = _pos()
        return [self._copy(ins, outs, sems, a, 4 + j, (*chip, c), (x, y, 1 - c))
                for j, chip in enumerate(_other_chips(x, y)) for a in range(len(self.arrs))]

    def start(self, ins, outs, sems):
        for cp in self._mine(ins, outs, sems) + self._first(ins, outs, sems):
            cp.start()

    def mid(self, ins, outs, sems):
        x, y, c = _pos()
        for j, chip in enumerate(_other_chips(x, y)):
            for a in range(len(self.arrs)):
                self._copy(ins, outs, sems, a, 1 + j, (*chip, c), (x, y, c)).wait_recv()
                self._copy(ins, outs, sems, a, 4 + j, (*chip, c), (x, y, 1 - c)).start()

    def finish(self, ins, outs, sems):
        x, y, c = _pos()
        for a in range(len(self.arrs)):
            self._copy(ins, outs, sems, a, 0, (x, y, 1 - c), (x, y, c)).wait_recv()
            for j, chip in enumerate(_other_chips(x, y)):
                self._copy(ins, outs, sems, a, 4 + j, (*chip, 1 - c), (x, y, c)).wait_recv()
        for cp in self._first(ins, outs, sems) + self._passed(ins, outs, sems):
            cp.wait_send()
        for cp in self._mine(ins, outs, sems):
            cp.wait()


class BgPair:
    def __init__(self, arrs):
        n = len(arrs)
        self.arrs = list(arrs)
        self.out_shape = [SDS((4,) + a.shape[1:], a.dtype) for a in arrs]
        self.scratch = [pltpu.SemaphoreType.DMA((n, 4)), pltpu.SemaphoreType.DMA((n, 4))]
        self.has_mid = False
        self.result = None

    def _copies(self, ins, outs, sems):
        x, y, c = _pos()
        return [pltpu.make_async_remote_copy(
            src_ref=ins[a].at[2 * k + 1 - c], dst_ref=outs[a].at[k], send_sem=sems[0].at[a, k],
            recv_sem=sems[1].at[a, k], device_id=(x, y, 1 - c), device_id_type=MESH)
            for a in range(len(self.arrs)) for k in range(4)]

    def start(self, ins, outs, sems):
        for cp in self._copies(ins, outs, sems):
            cp.start()

    def finish(self, ins, outs, sems):
        cps = self._copies(ins, outs, sems)
        for cp in cps:
            cp.wait_recv()
        for cp in cps:
            cp.wait_send()


class BgChips(BgPair):
    def __init__(self, arrs):
        n = len(arrs)
        self.arrs = list(arrs)
        self.out_shape = [SDS((3,) + a.shape[1:], a.dtype) for a in arrs]
        self.scratch = [pltpu.SemaphoreType.DMA((n, 3)), pltpu.SemaphoreType.DMA((n, 3))]
        self.has_mid = False
        self.result = None

    def _copies(self, ins, outs, sems):
        x, y, c = _pos()
        return [pltpu.make_async_remote_copy(
            src_ref=ins[a].at[2 * px + py], dst_ref=outs[a].at[r], send_sem=sems[0].at[a, r],
            recv_sem=sems[1].at[a, r], device_id=(px, py, c), device_id_type=MESH)
            for a in range(len(self.arrs)) for r, (px, py) in enumerate(_other_chips(x, y))]


def _call(bgs, body, *, name, grid, in_specs, out_specs, out_shape, scratch_shapes=(), compiler_params=None):
    single = not isinstance(out_shape, (list, tuple))
    out_specs_l = [out_specs] if single else list(out_specs)
    out_shape_l = [out_shape] if single else list(out_shape)
    bgs = [b for b in (bgs or []) if b is not None]
    n_in, n_out, n_sc = len(in_specs), len(out_shape_l), len(scratch_shapes)
    nsteps = math.prod(grid)

    def full(*refs):
        pos = [0]

        def take(k):
            r = refs[pos[0]:pos[0] + k]
            pos[0] += k
            return r

        ins = take(n_in)
        b_ins = [take(len(b.arrs)) for b in bgs]
        outs = take(n_out)
        b_outs = [take(len(b.out_shape)) for b in bgs]
        sc = take(n_sc)
        b_sc = [take(len(b.scratch)) for b in bgs]
        if bgs:
            step = pl.program_id(0)
            for d in range(1, len(grid)):
                step = step * grid[d] + pl.program_id(d)

            @pl.when(step == 0)
            def _():
                for b, i_, o_, s_ in zip(bgs, b_ins, b_outs, b_sc):
                    b.start(i_, o_, s_)

        body(*ins, *outs, *sc)
        if bgs:
            for b, i_, o_, s_ in zip(bgs, b_ins, b_outs, b_sc):
                if b.has_mid:
                    @pl.when(step == max(0, (3 * nsteps) // 4 - 1))
                    def _():
                        b.mid(i_, o_, s_)

            @pl.when(step == nsteps - 1)
            def _():
                for b, i_, o_, s_ in zip(bgs, b_ins, b_outs, b_sc):
                    b.finish(i_, o_, s_)

    def run(*args):
        res = pl.pallas_call(
            full, name=name, grid=grid,
            in_specs=list(in_specs) + [_ANY] * sum(len(b.arrs) for b in bgs),
            out_specs=out_specs_l + [_ANY] * sum(len(b.out_shape) for b in bgs),
            out_shape=out_shape_l + [s for b in bgs for s in b.out_shape],
            scratch_shapes=list(scratch_shapes) + [s for b in bgs for s in b.scratch],
            compiler_params=compiler_params,
        )(*args, *[a for b in bgs for a in b.arrs])
        rest = list(res[n_out:])
        for b in bgs:
            b.result, rest = rest[:len(b.out_shape)], rest[len(b.out_shape):]
        return res[0] if single else list(res[:n_out])

    return run


def s5_discretize(a_re, a_im, log_dt, b_re, b_im, c_re, c_im):
    lam_r = jnp.minimum(a_re, DT_MIN_LAMBDA)
    lam_i = a_im
    dt = jnp.exp(log_dt)[:, None]
    e = jnp.exp(lam_r * dt)
    lbr = e * jnp.cos(lam_i * dt)
    lbi = e * jnp.sin(lam_i * dt)
    den = lam_r * lam_r + lam_i * lam_i
    cf_r = ((lbr - 1.0) * lam_r + lbi * lam_i) / den
    cf_i = (lbi * lam_r - (lbr - 1.0) * lam_i) / den
    bb_r = cf_r[:, :, None] * b_re - cf_i[:, :, None] * b_im
    bb_i = cf_r[:, :, None] * b_im + cf_i[:, :, None] * b_re
    eye = jnp.eye(8, dtype=f32)

    def blk_b(m):
        return jnp.einsum('bgpc,gh->bgchp', m.reshape(8, 8, S5_P, S5_C), eye).reshape(8, 128, 512)

    def blk_c(m):
        return jnp.einsum('bgcp,gh->bgphc', m.reshape(8, 8, S5_C, S5_P), eye).reshape(8, 512, 128)

    bm = jnp.concatenate([blk_b(bb_r), blk_b(bb_i)], axis=-1)
    cm = jnp.concatenate([blk_c(c_re), -blk_c(c_im)], axis=1)
    lam = jnp.stack([lbr.reshape(8, 512), lbi.reshape(8, 512)], axis=1)
    lam = jnp.broadcast_to(lam[:, :, None, :], (8, 2, 8, 512))
    return lam, bm, cm


def _cmul(ar, ai, br, bi):
    return ar * br - ai * bi, ar * bi + ai * br


def _shift_rows(v, k, up):
    row = lax.broadcasted_iota(jnp.int32, v.shape, 0)
    if up:
        return jnp.where(row < 8 - k, pltpu.roll(v, 8 - k, 0), 0.0)
    return jnp.where(row >= k, pltpu.roll(v, k, 0), 0.0)


def _chunk_scan(S, lr, li, reverse, aux=None):
    if reverse:
        li = -li
    z = jnp.zeros((8, 512), f32)

    def idx(i):
        return (S5_STEPS - 1 - i) if reverse else i

    def rec(xr, xi, row):
        br = S[row, 0:512]
        bi = S[row, 512:1024]
        return lr * xr - li * xi + br, lr * xi + li * xr + bi

    def step1(i, c):
        row = pl.ds(pl.multiple_of(idx(i) * 8, 8), 8)
        return rec(c[0], c[1], row)

    er, ei = lax.fori_loop(0, S5_STEPS, step1, (z, z), unroll=8)
    ar, ai = lr, li
    for _ in range(8):
        ar, ai = _cmul(ar, ai, ar, ai)
    cr, ci = _shift_rows(er, 1, reverse), _shift_rows(ei, 1, reverse)
    for k in (1, 2, 4):
        sr, si = _shift_rows(cr, k, reverse), _shift_rows(ci, k, reverse)
        pr, pi_ = _cmul(ar, ai, sr, si)
        cr, ci = cr + pr, ci + pi_
        ar, ai = _cmul(ar, ai, ar, ai)

    if aux is None:
        def step2(i, c):
            row = pl.ds(pl.multiple_of(idx(i) * 8, 8), 8)
            xr, xi = rec(c[0], c[1], row)
            S[row, 0:512] = xr
            S[row, 512:1024] = xi
            return xr, xi

        lax.fori_loop(0, S5_STEPS, step2, (cr, ci), unroll=8)
        return None

    def step2(i, c):
        gr0, gi0, dr, di = c
        s = idx(i)
        row = pl.ds(pl.multiple_of(s * 8, 8), 8)
        gr, gi = rec(gr0, gi0, row)
        S[row, 0:512] = gr
        S[row, 512:1024] = gi
        prow = pl.ds(pl.multiple_of(jnp.maximum(s - 1, 0) * 8, 8), 8)
        xr = aux[prow, 0:512]
        xi = aux[prow, 512:1024]
        dr = dr + gr * xr + gi * xi
        di = di + gi * xr - gr * xi
        return gr, gi, dr, di

    gr, gi, dr, di = lax.fori_loop(0, S5_STEPS - 1, step2, (cr, ci, z, z), unroll=8)
    row0 = pl.ds(0, 8)
    gr, gi = rec(gr, gi, row0)
    S[row0, 0:512] = gr
    S[row0, 512:1024] = gi
    last = pl.ds((S5_STEPS - 1) * 8, 8)
    xr = _shift_rows(aux[last, 0:512], 1, False)
    xi = _shift_rows(aux[last, 512:1024], 1, False)
    dr = dr + gr * xr + gi * xi
    di = di + gi * xr - gr * xi
    return dr, di


_ROWS = 256


def _row_loop(fn):
    def body(r, c):
        fn(pl.ds(pl.multiple_of(r * _ROWS, _ROWS), _ROWS))
        return c
    lax.fori_loop(0, T // _ROWS, body, 0)


def s5_core_fwd(hn, bm, lam, cm, bg=()):
    def body(u_ref, b_ref, lam_ref, c_ref, ys_ref, S):
        def bu(rows):
            S[rows, :] = _dot(u_ref[rows, :], b_ref[...])
        _row_loop(bu)
        _chunk_scan(S, lam_ref[0], lam_ref[1], False)

        def ys(rows):
            ys_ref[rows, :] = _dot(S[rows, :].astype(bf16), c_ref[...])
        _row_loop(ys)

    return _call(
        bg, body, name="s5_core_fwd", grid=(S5_SUB,),
        in_specs=[pl.BlockSpec((T, 128), lambda b: (0, b)),
                  pl.BlockSpec((None, 128, 1024), lambda b: (b, 0, 0)),
                  pl.BlockSpec((None, 2, 8, 512), lambda b: (b, 0, 0, 0)),
                  pl.BlockSpec((None, 1024, 128), lambda b: (b, 0, 0))],
        out_specs=pl.BlockSpec((T, 128), lambda b: (0, b)),
        out_shape=SDS((T, D), f32),
        scratch_shapes=[pltpu.VMEM((T, 1024), f32)],
        compiler_params=_cp(dimension_semantics=("arbitrary",)),
    )(hn, bm, lam, cm)


def s5_core_bwd(hn, dy, bm, lam, cm, bg=()):
    def body(u_ref, dy_ref, b_ref, lam_ref, c_ref, du_ref, db_ref, dct_ref, dlam_ref, S1, S2):
        def bu(rows):
            S1[rows, :] = _dot(u_ref[rows, :], b_ref[...])
        _row_loop(bu)
        _chunk_scan(S1, lam_ref[0], lam_ref[1], False)
        dct_ref[...] = jnp.zeros_like(dct_ref)

        def dx(rows):
            dyb = dy_ref[rows, :].astype(bf16)
            S2[rows, :] = _dot_nt(dyb, c_ref[...])
            dct_ref[...] += _dot_tn(dyb, S1[rows, :].astype(bf16))
        _row_loop(dx)
        dr, di = _chunk_scan(S2, lam_ref[0], lam_ref[1], True, aux=S1)
        dlam_ref[0] = dr
        dlam_ref[1] = di
        db_ref[...] = jnp.zeros_like(db_ref)

        def dbu(rows):
            gb = S2[rows, :].astype(bf16)
            db_ref[...] += _dot_tn(u_ref[rows, :], gb)
            du_ref[rows, :] = _dot_nt(gb, b_ref[...])
        _row_loop(dbu)

    return _call(
        bg, body, name="s5_core_bwd", grid=(S5_SUB,),
        in_specs=[pl.BlockSpec((T, 128), lambda b: (0, b)),
                  pl.BlockSpec((T, 128), lambda b: (0, b)),
                  pl.BlockSpec((None, 128, 1024), lambda b: (b, 0, 0)),
                  pl.BlockSpec((None, 2, 8, 512), lambda b: (b, 0, 0, 0)),
                  pl.BlockSpec((None, 1024, 128), lambda b: (b, 0, 0))],
        out_specs=[pl.BlockSpec((T, 128), lambda b: (0, b)),
                   pl.BlockSpec((None, 128, 1024), lambda b: (b, 0, 0)),
                   pl.BlockSpec((None, 128, 1024), lambda b: (b, 0, 0)),
                   pl.BlockSpec((None, 2, 8, 512), lambda b: (b, 0, 0, 0))],
        out_shape=[SDS((T, D), f32), SDS((8, 128, 1024), f32), SDS((8, 128, 1024), f32), SDS((8, 2, 8, 512), f32)],
        scratch_shapes=[pltpu.VMEM((T, 1024), f32), pltpu.VMEM((T, 1024), f32)],
        compiler_params=_cp(dimension_semantics=("arbitrary",)),
    )(hn, dy, bm, lam, cm)


TM = 512
NT = T // TM


def _tile(n=D):
    return pl.BlockSpec((TM, n), lambda i: (i, 0))


def s5_pre(xp, g):
    def body(x_ref, g_ref, hn_ref):
        hn, _ = _rms(x_ref[...], g_ref[...])
        hn_ref[...] = hn.astype(bf16)

    return pl.pallas_call(
        body, name="s5_pre", grid=(NT,), in_specs=[_tile(), _full((1, D))], out_specs=_tile(),
        out_shape=SDS((T, D), bf16), compiler_params=_cp(dimension_semantics=("arbitrary",)),
    )(xp, g)


def _gelu_grad(y):
    c = math.sqrt(2.0 / math.pi)
    t = jnp.tanh(c * (y + 0.044715 * y * y * y))
    return 0.5 * (1.0 + t) + 0.5 * y * (1.0 - t * t) * c * (1.0 + 3.0 * 0.044715 * y * y)


def s5_post(ys, xp, g, d, wglu, bglu, bg=()):
    def body(ys_ref, x_ref, g_ref, d_ref, w_ref, b_ref, y_ref, z_ref, h_ref):
        x = x_ref[...]
        hn, _ = _rms(x, g_ref[...])
        y = ys_ref[...] + d_ref[...] * hn
        y_ref[...] = y
        yg = jax.nn.gelu(y).astype(bf16)
        for j in range(4):
            cv = slice(j * 256, (j + 1) * 256)
            cg = slice(1024 + j * 256, 1024 + (j + 1) * 256)
            val = _dot(yg, w_ref[j]) + b_ref[:, cv]
            gate = _dot(yg, w_ref[j + 4]) + b_ref[:, cg]
            z_ref[:, cv] = val
            z_ref[:, cg] = gate
            h_ref[:, cv] = x[:, cv] + val * jax.nn.sigmoid(gate)

    return _call(
        bg, body, name="s5_post", grid=(NT,),
        in_specs=[_tile(), _tile(), _full((1, D)), _full((1, D)), _full((8, D, 256)), _full((1, 2 * D))],
        out_specs=[_tile(), _tile(2 * D), _tile()],
        out_shape=[SDS((T, D), f32), SDS((T, 2 * D), f32), SDS((T, D), f32)],
        compiler_params=_cp(dimension_semantics=("arbitrary",)),
    )(ys, xp, g, d, wglu, bglu)


def s5_post_bwd(dh, y, z, wglu, bg=()):
    def body(dh_ref, y_ref, z_ref, w_ref, dy_ref, dw_ref, db_ref, acc):
        i = pl.program_id(0)

        @pl.when(i == 0)
        def _():
            acc[...] = jnp.zeros_like(acc)
            db_ref[...] = jnp.zeros_like(db_ref)

        dh_ = dh_ref[...]
        y = y_ref[...]
        yg = jax.nn.gelu(y).astype(bf16)
        dyg = jnp.zeros((TM, D), f32)
        for j in range(4):
            cv = slice(j * 256, (j + 1) * 256)
            cg = slice(1024 + j * 256, 1024 + (j + 1) * 256)
            val = z_ref[:, cv]
            sg = jax.nn.sigmoid(z_ref[:, cg])
            dval = dh_[:, cv] * sg
            dgate = dh_[:, cv] * val * sg * (1.0 - sg)
            db_ref[:, cv] += _colsum8(dval)
            db_ref[:, cg] += _colsum8(dgate)
            dvb = dval.astype(bf16)
            dgb = dgate.astype(bf16)
            acc[j] += _dot_tn(yg, dvb)
            acc[j + 4] += _dot_tn(yg, dgb)
            dyg = dyg + _dot_nt(dvb, w_ref[j]) + _dot_nt(dgb, w_ref[j + 4])
        dy_ref[...] = dyg * _gelu_grad(y)

        @pl.when(i == NT - 1)
        def _():
            dw_ref[...] = acc[...].astype(bf16)

    return _call(
        bg, body, name="s5_post_bwd", grid=(NT,),
        in_specs=[_tile(), _tile(), _tile(2 * D), _full((8, D, 256))],
        out_specs=[_tile(), _full((8, D, 256)), _full((8, 2 * D))],
        out_shape=[SDS((T, D), f32), SDS((8, D, 256), bf16), SDS((8, 2 * D), f32)],
        scratch_shapes=[pltpu.VMEM((8, D, 256), f32)],
        compiler_params=_cp(dimension_semantics=("arbitrary",)),
    )(dh, y, z, wglu)


def s5_pre_bwd(xp, g, du, dy, d, dh, bg=()):
    def body(x_ref, g_ref, du_ref, dy_ref, d_ref, dh_ref, dx_ref, dg_ref, dd_ref):
        i = pl.program_id(0)

        @pl.when(i == 0)
        def _():
            dg_ref[...] = jnp.zeros_like(dg_ref)
            dd_ref[...] = jnp.zeros_like(dd_ref)

        x = x_ref[...]
        g = g_ref[...]
        dy = dy_ref[...]
        hn, _ = _rms(x, g)
        dhn = du_ref[...] + d_ref[...] * dy
        dx, dgt = _rms_bwd(x, g, dhn)
        dx_ref[...] = dh_ref[...] + dx
        dg_ref[...] += _colsum8(dgt)
        dd_ref[...] += _colsum8(dy * hn)

    return _call(
        bg, body, name="s5_pre_bwd", grid=(NT,),
        in_specs=[_tile(), _full((1, D)), _tile(), _tile(), _full((1, D)), _tile()],
        out_specs=[_tile(), _full((8, D)), _full((8, D))],
        out_shape=[SDS((T, D), f32), SDS((8, D), f32), SDS((8, D), f32)],
        compiler_params=_cp(dimension_semantics=("arbitrary",)),
    )(xp, g, du, dy, d, dh)


TMF = 1024


def mlp_fwd(h, g, w_in, w_out, layer, bg=()):
    def body(h_ref, g_ref, wi_ref, wo_ref, hm_ref, out_ref, acc):
        j = pl.program_id(1)

        @pl.when(j == 0)
        def _():
            hm, _ = _rms(h_ref[...], g_ref[...])
            hm_ref[...] = hm.astype(bf16)
            acc[...] = jnp.zeros_like(acc)

        a = jnp.maximum(_dot(hm_ref[...], wi_ref[...]), 0.0)
        acc[...] += _dot((a * a).astype(bf16), wo_ref[...])

        @pl.when(j == NDEV - 1)
        def _():
            out_ref[...] = h_ref[...] + acc[...]

    return _call(
        bg, body, name=f"mlp_fwd{layer}", grid=(T // TMF, NDEV),
        in_specs=[pl.BlockSpec((TMF, D), lambda i, j: (i, 0)),
                  pl.BlockSpec((1, D), lambda i, j: (0, 0)),
                  pl.BlockSpec((None, D, D_FF_SHARD), lambda i, j: (j, 0, 0)),
                  pl.BlockSpec((None, D_FF_SHARD, D), lambda i, j: (j, 0, 0))],
        out_specs=[pl.BlockSpec((TMF, D), lambda i, j: (i, 0)), pl.BlockSpec((TMF, D), lambda i, j: (i, 0))],
        out_shape=[SDS((T, D), bf16), SDS((T, D), f32)],
        scratch_shapes=[pltpu.VMEM((TMF, D), f32)],
        compiler_params=_cp(dimension_semantics=("arbitrary", "arbitrary")),
    )(h, g, w_in, w_out)


def mlp_bwd(h, hm, g, dout, w_in, w_out, layer, bg=()):
    last = NDEV - 1

    def body(h_ref, hm_ref, g_ref, do_ref, wi_ref, wo_ref, dh_ref, dwi_ref, dwo_ref, dg_ref, dhm, awi, awo):
        j = pl.program_id(0)
        i = pl.program_id(1)
        rows = pl.ds(pl.multiple_of(i * TM, TM), TM)

        @pl.when(i == 0)
        def _():
            awi[...] = jnp.zeros_like(awi)
            awo[...] = jnp.zeros_like(awo)

        hm_ = hm_ref[...]
        dob = do_ref[...].astype(bf16)
        r = jnp.maximum(_dot(hm_, wi_ref[...]), 0.0)
        dz = (_dot_nt(dob, wo_ref[...]) * (2.0 * r)).astype(bf16)
        awo[...] += _dot_tn((r * r).astype(bf16), dob)
        awi[...] += _dot_tn(hm_, dz)
        part = _dot_nt(dz, wi_ref[...])

        @pl.when(j == 0)
        def _():
            dhm[rows, :] = part

        @pl.when(j > 0)
        def _():
            dhm[rows, :] += part

        @pl.when(i == NT - 1)
        def _():
            dwi_ref[...] = awi[...].astype(bf16)
            dwo_ref[...] = awo[...].astype(bf16)

        @pl.when(j == last)
        def _():
            @pl.when(i == 0)
            def _():
                dg_ref[...] = jnp.zeros_like(dg_ref)
            dx, dgt = _rms_bwd(h_ref[...], g_ref[...], dhm[rows, :])
            dh_ref[...] = do_ref[...] + dx
            dg_ref[...] += _colsum8(dgt)

    late = lambda j, i: (jnp.where(j == last, i, 0), 0)
    return _call(
        bg, body, name=f"mlp_bwd{layer}", grid=(NDEV, NT),
        in_specs=[pl.BlockSpec((TM, D), late),
                  pl.BlockSpec((TM, D), lambda j, i: (i, 0)),
                  pl.BlockSpec((1, D), lambda j, i: (0, 0)),
                  pl.BlockSpec((TM, D), lambda j, i: (i, 0)),
                  pl.BlockSpec((None, D, D_FF_SHARD), lambda j, i: (j, 0, 0)),
                  pl.BlockSpec((None, D_FF_SHARD, D), lambda j, i: (j, 0, 0))],
        out_specs=[pl.BlockSpec((TM, D), late),
                   pl.BlockSpec((None, D, D_FF_SHARD), lambda j, i: (j, 0, 0)),
                   pl.BlockSpec((None, D_FF_SHARD, D), lambda j, i: (j, 0, 0)),
                   pl.BlockSpec((8, D), lambda j, i: (0, 0))],
        out_shape=[SDS((T, D), f32), SDS((NDEV, D, D_FF_SHARD), bf16), SDS((NDEV, D_FF_SHARD, D), bf16),
                   SDS((8, D), f32)],
        scratch_shapes=[pltpu.VMEM((T, D), f32), pltpu.VMEM((D, D_FF_SHARD), f32), pltpu.VMEM((D_FF_SHARD, D), f32)],
        compiler_params=_cp(dimension_semantics=("arbitrary", "arbitrary")),
    )(h, hm, g, dout, w_in, w_out)


def attn_pre(h, g_kv, g_mix, wk4, wv4, bk4, bv4, wq, bq):
    def body(h_ref, gkv_ref, gm_ref, wk_ref, wv_ref, bk_ref, bv_ref, wq_ref, bq_ref,
             kvn_ref, hn_ref, k_ref, v_ref, q_ref):
        h_ = h_ref[...]
        kvn = _rms(h_, gkv_ref[...])[0].astype(bf16)
        hn = _rms(h_, gm_ref[...])[0].astype(bf16)
        kvn_ref[...] = kvn
        hn_ref[...] = hn
        k_ref[...] = (_dot(kvn, wk_ref[...]) + bk_ref[...]).astype(bf16)
        v_ref[...] = (_dot(kvn, wv_ref[...]) + bv_ref[...]).astype(bf16)
        q_ref[...] = (_dot(hn, wq_ref[...]) + bq_ref[...]).astype(bf16)

    return pl.pallas_call(
        body, name="attn_pre", grid=(NT,),
        in_specs=[_tile(), _full((1, D)), _full((1, D)), _full((D, D)), _full((D, D)), _full((1, D)), _full((1, D)),
                  _full((D, D)), _full((1, D))],
        out_specs=[_tile()] * 5,
        out_shape=[SDS((T, D), bf16)] * 5,
        compiler_params=_cp(dimension_semantics=("arbitrary",)),
    )(h, g_kv, g_mix, wk4, wv4, bk4, bv4, wq, bq)


def _attn_specs():
    cur = pl.BlockSpec((TM, 256), lambda j, n: (n, j))
    prev = pl.BlockSpec((BLK, 256), lambda j, n: (jnp.maximum(n * (TM // BLK) - 1, 0), j))
    return cur, prev


def _attn_probs(qg, k2, sink, first):
    s = _dot_nt(qg, k2) * (1.0 / math.sqrt(HEAD_DIM))
    qi = lax.broadcasted_iota(jnp.int32, (BLK, 2 * BLK), 0)
    kj = lax.broadcasted_iota(jnp.int32, (BLK, 2 * BLK), 1)
    diff = qi + BLK - kj
    valid = (diff >= 0) & (diff < BLK) & (jnp.logical_not(first) | (kj >= BLK))
    s = jnp.where(valid, s, -jnp.inf)
    m = jnp.maximum(jnp.max(s, axis=-1, keepdims=True), sink)
    p = jnp.exp(s - m)
    ps = jnp.exp(sink - m)
    denom = jnp.sum(p, axis=-1, keepdims=True) + ps
    return p / denom, ps / denom


def _head_mask(g):
    lane = lax.broadcasted_iota(jnp.int32, (1, 256), 1)
    return (lane >= g * HEAD_DIM) & (lane < (g + 1) * HEAD_DIM)


def attn_core_fwd(q, k4, v4, sinks, bg=()):
    nb = TM // BLK

    def body(sink_ref, q_ref, kc_ref, kp_ref, vc_ref, vp_ref, o_ref):
        j = pl.program_id(0)
        n = pl.program_id(1)
        for b in range(nb):
            qb = q_ref[b * BLK:(b + 1) * BLK, :]
            if b == 0:
                k2 = jnp.concatenate([kp_ref[...], kc_ref[0:BLK, :]], axis=0)
                v2 = jnp.concatenate([vp_ref[...], vc_ref[0:BLK, :]], axis=0)
                first = n == 0
            else:
                k2 = kc_ref[(b - 1) * BLK:(b + 1) * BLK, :]
                v2 = vc_ref[(b - 1) * BLK:(b + 1) * BLK, :]
                first = False
            acc = jnp.zeros((BLK, 256), f32)
            for g in range(Q_PER_KV):
                mk = _head_mask(g)
                a, _ = _attn_probs(jnp.where(mk, qb, 0), k2, sink_ref[j * Q_PER_KV + g], first)
                acc = acc + _dot(a.astype(bf16), jnp.where(mk, v2, 0))
            o_ref[b * BLK:(b + 1) * BLK, :] = acc.astype(bf16)

    cur, prev = _attn_specs()
    return _call(
        bg, body, name="attn_core_fwd", grid=(N_KV, NT),
        in_specs=[pl.BlockSpec(memory_space=pltpu.SMEM), cur, cur, prev, cur, prev],
        out_specs=cur, out_shape=SDS((T, D), bf16),
        compiler_params=_cp(dimension_semantics=("arbitrary", "arbitrary")),
    )(sinks, q, k4, k4, v4, v4)


def attn_post(h, o, wo, bo):
    def body(h_ref, o_ref, w_ref, b_ref, out_ref):
        out_ref[...] = h_ref[...] + _dot(o_ref[...], w_ref[...]) + b_ref[...]

    return pl.pallas_call(
        body, name="attn_post", grid=(NT,), in_specs=[_tile(), _tile(), _full((D, D)), _full((1, D))],
        out_specs=_tile(), out_shape=SDS((T, D), f32), compiler_params=_cp(dimension_semantics=("arbitrary",)),
    )(h, o, wo, bo)


def attn_bwd_pre(dh, o, wo, bg=()):
    def body(dh_ref, o_ref, w_ref, do_ref, dw_ref, db_ref, acc):
        i = pl.program_id(0)

        @pl.when(i == 0)
        def _():
            acc[...] = jnp.zeros_like(acc)
            db_ref[...] = jnp.zeros_like(db_ref)

        dh_ = dh_ref[...]
        dhb = dh_.astype(bf16)
        do_ref[...] = _dot_nt(dhb, w_ref[...]).astype(bf16)
        acc[...] += _dot_tn(o_ref[...], dhb)
        db_ref[...] += _colsum8(dh_)

        @pl.when(i == NT - 1)
        def _():
            dw_ref[...] = acc[...].astype(bf16)

    return _call(
        bg, body, name="attn_bwd_pre", grid=(NT,), in_specs=[_tile(), _tile(), _full((D, D))],
        out_specs=[_tile(), _full((D, D)), _full((8, D))],
        out_shape=[SDS((T, D), bf16), SDS((D, D), bf16), SDS((8, D), f32)],
        scratch_shapes=[pltpu.VMEM((D, D), f32)],
        compiler_params=_cp(dimension_semantics=("arbitrary",)),
    )(dh, o, wo)


def attn_core_bwd(q, do, k4, v4, sinks, bg=()):
    nb = TM // BLK

    def body(sink_ref, q_ref, do_ref, kc_ref, kp_ref, vc_ref, vp_ref, dq_ref, dk_ref, dv_ref, ds_ref):
        j = pl.program_id(0)
        n = pl.program_id(1)

        @pl.when(n == 0)
        def _():
            dk_ref[...] = jnp.zeros_like(dk_ref)
            dv_ref[...] = jnp.zeros_like(dv_ref)
            ds_ref[...] = jnp.zeros_like(ds_ref)

        lane8 = lax.broadcasted_iota(jnp.int32, (8, 128), 1)
        row8 = lax.broadcasted_iota(jnp.int32, (8, 128), 0)
        for b in range(nb):
            qb = q_ref[b * BLK:(b + 1) * BLK, :]
            dob = do_ref[b * BLK:(b + 1) * BLK, :]
            if b == 0:
                k2 = jnp.concatenate([kp_ref[...], kc_ref[0:BLK, :]], axis=0)
                v2 = jnp.concatenate([vp_ref[...], vc_ref[0:BLK, :]], axis=0)
                first = n == 0
            else:
                k2 = kc_ref[(b - 1) * BLK:(b + 1) * BLK, :]
                v2 = vc_ref[(b - 1) * BLK:(b + 1) * BLK, :]
                first = False
            dq = jnp.zeros((BLK, 256), f32)
            dk2 = jnp.zeros((2 * BLK, 256), f32)
            dv2 = jnp.zeros((2 * BLK, 256), f32)
            for g in range(Q_PER_KV):
                mk = _head_mask(g)
                qg = jnp.where(mk, qb, 0)
                dog = jnp.where(mk, dob, 0)
                a, asink = _attn_probs(qg, k2, sink_ref[j * Q_PER_KV + g], first)
                dp = _dot_nt(dog, v2)
                dd = jnp.sum(a * dp, axis=-1, keepdims=True)
                dsc = (a * (dp - dd) * (1.0 / math.sqrt(HEAD_DIM))).astype(bf16)
                dsink = -jnp.sum(asink * dd, axis=0, keepdims=True)
                ds_ref[...] += jnp.where((lane8 == g) & (row8 == 0), jnp.broadcast_to(dsink, (8, 128)), 0.0)
                dq = dq + _dot(dsc, jnp.where(mk, k2, 0))
                dk2 = dk2 + _dot_tn(dsc, qg)
                dv2 = dv2 + _dot_tn(a.astype(bf16), dog)
            dq_ref[b * BLK:(b + 1) * BLK, :] = dq
            cur = pl.ds(pl.multiple_of(n * TM + b * BLK, BLK), BLK)
            dk_ref[cur, :] += dk2[BLK:, :]
            dv_ref[cur, :] += dv2[BLK:, :]
            if b == 0:
                @pl.when(n > 0)
                def _():
                    prv = pl.ds(pl.multiple_of(n * TM - BLK, BLK), BLK)
                    dk_ref[prv, :] += dk2[:BLK, :]
                    dv_ref[prv, :] += dv2[:BLK, :]
            else:
                prv = pl.ds(pl.multiple_of(n * TM + (b - 1) * BLK, BLK), BLK)
                dk_ref[prv, :] += dk2[:BLK, :]
                dv_ref[prv, :] += dv2[:BLK, :]

    cur, prev = _attn_specs()
    col = pl.BlockSpec((T, 256), lambda j, n: (0, j))
    return _call(
        bg, body, name="attn_core_bwd", grid=(N_KV, NT),
        in_specs=[pl.BlockSpec(memory_space=pltpu.SMEM), cur, cur, cur, prev, cur, prev],
        out_specs=[cur, col, col, pl.BlockSpec((None, 8, 128), lambda j, n: (j, 0, 0))],
        out_shape=[SDS((T, D), f32), SDS((T, D), f32), SDS((T, D), f32), SDS((N_KV, 8, 128), f32)],
        compiler_params=_cp(dimension_semantics=("arbitrary", "arbitrary")),
    )(sinks, q, do, k4, k4, v4, v4)


def attn_bwd_q(h, dh, dq, hn, g_mix, wq):
    def body(h_ref, dh_ref, dq_ref, hn_ref, gm_ref, wq_ref, out_ref, dwq_ref, dbq_ref, dgm_ref, aq):
        i = pl.program_id(0)

        @pl.when(i == 0)
        def _():
            aq[...] = jnp.zeros_like(aq)
            dbq_ref[...] = jnp.zeros_like(dbq_ref)
            dgm_ref[...] = jnp.zeros_like(dgm_ref)

        dq_ = dq_ref[...]
        dqb = dq_.astype(bf16)
        aq[...] += _dot_tn(hn_ref[...], dqb)
        dbq_ref[...] += _colsum8(dq_)
        dx, dg = _rms_bwd(h_ref[...], gm_ref[...], _dot_nt(dqb, wq_ref[...]))
        out_ref[...] = dh_ref[...] + dx
        dgm_ref[...] += _colsum8(dg)

        @pl.when(i == NT - 1)
        def _():
            dwq_ref[...] = aq[...].astype(bf16)

    vec = _full((8, D))
    mat = _full((D, D))
    return pl.pallas_call(
        body, name="attn_bwd_q", grid=(NT,),
        in_specs=[_tile()] * 4 + [_full((1, D)), mat],
        out_specs=[_tile(), mat, vec, vec],
        out_shape=[SDS((T, D), f32), SDS((D, D), bf16), SDS((8, D), f32), SDS((8, D), f32)],
        scratch_shapes=[pltpu.VMEM((D, D), f32)],
        compiler_params=_cp(dimension_semantics=("arbitrary",)),
    )(h, dh, dq, hn, g_mix, wq)


def attn_bwd_kv(h, dh, dk4, dv4, kvn, g_kv, wk4, wv4):
    def body(h_ref, dh_ref, dk_ref, dv_ref, kvn_ref, gkv_ref, wk_ref, wv_ref,
             out_ref, dwk_ref, dwv_ref, dbk_ref, dbv_ref, dgkv_ref):
        i = pl.program_id(0)

        @pl.when(i == 0)
        def _():
            for r in (dwk_ref, dwv_ref, dbk_ref, dbv_ref, dgkv_ref):
                r[...] = jnp.zeros_like(r)

        dk_ = dk_ref[...]
        dv_ = dv_ref[...]
        dkb, dvb = dk_.astype(bf16), dv_.astype(bf16)
        dkvn = _dot_nt(dkb, wk_ref[...]) + _dot_nt(dvb, wv_ref[...])
        dwk_ref[...] += _dot_tn(kvn_ref[...], dkb)
        dwv_ref[...] += _dot_tn(kvn_ref[...], dvb)
        dbk_ref[...] += _colsum8(dk_)
        dbv_ref[...] += _colsum8(dv_)
        dx, dg = _rms_bwd(h_ref[...], gkv_ref[...], dkvn)
        out_ref[...] = dh_ref[...] + dx
        dgkv_ref[...] += _colsum8(dg)

    vec = _full((8, D))
    mat = _full((D, D))
    return pl.pallas_call(
        body, name="attn_bwd_kv", grid=(NT,),
        in_specs=[_tile()] * 5 + [_full((1, D)), mat, mat],
        out_specs=[_tile(), mat, mat, vec, vec, vec],
        out_shape=[SDS((T, D), f32), SDS((D, D), f32), SDS((D, D), f32)] + [SDS((8, D), f32)] * 3,
        compiler_params=_cp(dimension_semantics=("arbitrary",)),
    )(h, dh, dk4, dv4, kvn, g_kv, wk4, wv4)


def final_loss(h, g, target):
    def body(h_ref, g_ref, t_ref, loss_ref, dh_ref, dg_ref):
        i = pl.program_id(0)

        @pl.when(i == 0)
        def _():
            loss_ref[...] = jnp.zeros_like(loss_ref)
            dg_ref[...] = jnp.zeros_like(dg_ref)

        h_ = h_ref[...]
        g_ = g_ref[...]
        y, _ = _rms(h_, g_)
        diff = y - t_ref[...]
        per_tok = jnp.mean(diff * diff, axis=-1, keepdims=True)
        tot = 0.5 * jnp.sum(per_tok, axis=0, keepdims=True)
        lane = lax.broadcasted_iota(jnp.int32, (8, 128), 1)
        row = lax.broadcasted_iota(jnp.int32, (8, 128), 0)
        loss_ref[...] += jnp.where((lane == 0) & (row == 0), jnp.broadcast_to(tot, (8, 128)), 0.0)
        dx, dgt = _rms_bwd(h_, g_, diff * (1.0 / D))
        dh_ref[...] = dx
        dg_ref[...] += _colsum8(dgt)

    return pl.pallas_call(
        body, name="final_loss", grid=(NT,), in_specs=[_tile(), _full((1, D)), _tile()],
        out_specs=[_full((8, 128)), _tile(), _full((8, D))],
        out_shape=[SDS((8, 128), f32), SDS((T, D), f32), SDS((8, D), f32)],
        compiler_params=_cp(dimension_semantics=("arbitrary",)),
    )(h, g, target)


def _to_chunked(a):
    return a.reshape(S5_CH, S5_STEPS, a.shape[-1]).transpose(1, 0, 2).reshape(T, a.shape[-1])


def _from_chunked(a):
    return a.reshape(S5_STEPS, S5_CH, a.shape[-1]).transpose(1, 0, 2).reshape(T, a.shape[-1])


def _rep4(w):
    return jnp.broadcast_to(w.reshape(w.shape[0], N_KV, 1, HEAD_DIM), (w.shape[0], N_KV, Q_PER_KV, HEAD_DIM)).reshape(
        w.shape[0], N_KV * Q_PER_KV * HEAD_DIM)


def _fold4(w):
    return w.reshape(w.shape[0], N_KV, Q_PER_KV, HEAD_DIM).sum(axis=2).reshape(w.shape[0], N_KV * HEAD_DIM)


def fwd_bwd(x, target, p, shards, core):
    row = lambda v: v.reshape(1, -1)
    (lam, bm, cm), prep_vjp = jax.vjp(s5_discretize, p["s5_a_re"][0], p["s5_a_im"][0], p["s5_log_dt"][0],
                                      p["s5_b_re"][0], p["s5_b_im"][0], p["s5_c_re"][0], p["s5_c_im"][0])
    bmb, cmb = bm.astype(bf16), cm.astype(bf16)
    g_mix0, g_mix1 = row(p["norm_mix"][0]), row(p["norm_mix"][1])
    g_mlp0, g_mlp1 = row(p["norm_mlp"][0]), row(p["norm_mlp"][1])
    g_kv, g_fin = row(p["norm_kv"]), row(p["norm_final"])
    bq, bo = p["b_q"], p["b_o"]
    bk4, bv4 = _rep4(row(p["b_kv"])[:, :256]), _rep4(row(p["b_kv"])[:, 256:])
    sinks = p["sinks"].reshape(16)

    def reduce_pairs(names, bg):
        return [add_pairs(g, r, core, f"add_pairs_{n}") for n, g, r in zip(names, bg.arrs, bg.result)]

    xp = _to_chunked(x)
    hn0 = s5_pre(xp, g_mix0)
    ga = BgGather([shards["s5_w_glu"], shards["vecs"], shards["w_in0"]])
    ys = s5_core_fwd(hn0, bmb, lam, cmb, bg=[ga])
    wglu, gvec, win0 = ga.result
    d_skip = gvec[:, 0, :128].reshape(1, D)
    bglu = gvec[:, 0, 128:].reshape(1, 2 * D)
    gb = BgGather([shards["w_out0"]])
    y, z, h1 = s5_post(ys, xp, g_mix0, d_skip, wglu, bglu, bg=[gb])
    wout0, = gb.result
    gc = BgGather([shards["w_kv"], shards["w_q"], shards["w_o"], shards["w_in1"]])
    hm0, h2p = mlp_fwd(h1, g_mlp0, win0, wout0, 0, bg=[gc])
    wkv, wq, wo, win1 = gc.result
    wkv, wq, wo = wkv.reshape(D, 512), wq.reshape(D, D), wo.reshape(D, D)
    wk4, wv4 = _rep4(wkv[:, :256]), _rep4(wkv[:, 256:])
    h2 = _from_chunked(h2p)
    kvn, hn1, k4, v4, q = attn_pre(h2, g_kv, g_mix1, wk4, wv4, bk4, bv4, wq, bq)
    gd = BgGather([shards["w_out1"]])
    o = attn_core_fwd(q, k4, v4, sinks, bg=[gd])
    wout1, = gd.result
    h3 = attn_post(h2, o, wo, bo)
    hm1, h4 = mlp_fwd(h3, g_mlp1, win1, wout1, 1)
    loss, dh4, dg_fin = final_loss(h4, g_fin, target)

    big = {}
    dh3, dwin1, dwout1, dg_mlp1 = mlp_bwd(h3, hm1, g_mlp1, dh4, win1, wout1, 1)
    pa = BgPair([dwin1, dwout1])
    do, dwo, dbo = attn_bwd_pre(dh3, o, wo, bg=[pa])
    ca = BgChips(reduce_pairs(["w_in1", "w_out1"], pa))
    dq, dk4, dv4, dsink = attn_core_bwd(q, do, k4, v4, sinks, bg=[ca])
    big["w_in1"], big["w_out1"] = zip(ca.arrs, ca.result)
    dh2, dwq, dbq, dg_mix1 = attn_bwd_q(h2, dh3, dq, hn1, g_mix1, wq)
    dh2, dwk4, dwv4, dbk4, dbv4, dg_kv = attn_bwd_kv(h2, dh2, dk4, dv4, kvn, g_kv, wk4, wv4)
    dwkv = jnp.concatenate([_fold4(dwk4), _fold4(dwv4)], axis=1).astype(bf16)
    pb = BgPair([dwkv.reshape(NDEV, 128, 512), dwq.reshape(NDEV, 128, D), dwo.reshape(NDEV, 128, D)])
    dh2p = _to_chunked(dh2)
    dh1, dwin0, dwout0, dg_mlp0 = mlp_bwd(h1, hm0, g_mlp0, dh2p, win0, wout0, 0, bg=[pb])
    cb = BgChips(reduce_pairs(["w_kv", "w_q", "w_o"], pb))
    pc = BgPair([dwin0, dwout0])
    dy, dwglu, dbglu = s5_post_bwd(dh1, y, z, wglu, bg=[cb, pc])
    big["w_kv"], big["w_q"], big["w_o"] = zip(cb.arrs, cb.result)
    cc = BgChips(reduce_pairs(["w_in0", "w_out0"], pc))
    pd = BgPair([dwglu])
    du, dbm, dcmt, dlam = s5_core_bwd(hn0, dy, bmb, lam, cmb, bg=[cc, pd])
    big["w_in0"], big["w_out0"] = zip(cc.arrs, cc.result)
    cd = BgChips(reduce_pairs(["s5_w_glu"], pd))
    dxp, dg_mix0, dd = s5_pre_bwd(xp, g_mix0, du, dy, d_skip, dh1, bg=[cd])
    big["s5_w_glu"], = zip(cd.arrs, cd.result)
    grad_x = _from_chunked(dxp)
    da_re, da_im, dlog_dt, db_re, db_im, dc_re, dc_im = prep_vjp((dlam, dbm, dcmt.transpose(0, 2, 1)))

    def lanes(v_):
        v_ = v_.reshape(1, -1)
        return jnp.pad(v_, ((0, 0), (0, D - v_.shape[1])))

    small = jnp.concatenate([
        dg_mix0[0:1], dg_mix1[0:1], dg_mlp0[0:1], dg_mlp1[0:1], dg_kv[0:1], dg_fin[0:1], dd[0:1], dbq[0:1], dbo[0:1],
        dbglu[0:1].reshape(2, D), lanes(jnp.concatenate([_fold4(dbk4[0:1]), _fold4(dbv4[0:1])], axis=1)),
        lanes(dsink[:, 0, :Q_PER_KV]), lanes(dlog_dt), jnp.zeros((2, D), f32),
        da_re.reshape(4, D), da_im.reshape(4, D), db_re.reshape(64, D), db_im.reshape(64, D),
        dc_re.reshape(64, D), dc_im.reshape(64, D)], axis=0)
    return loss, grad_x, small, big


_ANY = pl.BlockSpec(memory_space=pl.ANY)


def _pos():
    return lax.axis_index("x"), lax.axis_index("y"), lax.axis_index("c")


def _other_chips(x, y):
    return [(1 - x, y), (x, 1 - y), (1 - x, 1 - y)]


def all_gather(arrs):
    n = len(arrs)

    def body(*refs):
        ins, outs = refs[:n], refs[n:2 * n]
        send_sems, recv_sems, local_sems = refs[2 * n:]
        x, y, c = _pos()
        me, sib = (x, y, c), (x, y, 1 - c)
        chips = _other_chips(x, y)

        def copy(a, k, block, to, src=None):
            dst = outs[a].at[4 * block[0] + 2 * block[1] + block[2]]
            return pltpu.make_async_remote_copy(
                src_ref=dst if src is None else src, dst_ref=dst, send_sem=send_sems.at[a, k],
                recv_sem=recv_sems.at[a, k], device_id=to, device_id_type=MESH)

        mine = [pltpu.make_async_copy(ins[a], outs[a].at[4 * x + 2 * y + c], local_sems.at[a]) for a in range(n)]
        for cp in mine:
            cp.start()
        first = []
        for a in range(n):
            first.append(copy(a, 0, me, sib, src=ins[a]))
            first += [copy(a, 1 + j, me, (*chip, c), src=ins[a]) for j, chip in enumerate(chips)]
        for cp in first:
            cp.start()
        passed = []
        for j, chip in enumerate(chips):
            for a in range(n):
                copy(a, 1 + j, (*chip, c), me).wait_recv()
                cp = copy(a, 4 + j, (*chip, c), sib)
                cp.start()
                passed.append(cp)
        for a in range(n):
            copy(a, 0, sib, me).wait_recv()
            for j, chip in enumerate(chips):
                copy(a, 4 + j, (*chip, 1 - c), me).wait_recv()
        for cp in first + passed:
            cp.wait_send()
        for cp in mine:
            cp.wait()

    return pl.pallas_call(
        body, name="all_gather", in_specs=[_ANY] * n, out_specs=[_ANY] * n,
        out_shape=[SDS((NDEV,) + a.shape, a.dtype) for a in arrs],
        scratch_shapes=[pltpu.SemaphoreType.DMA((n, 7)), pltpu.SemaphoreType.DMA((n, 7)),
                        pltpu.SemaphoreType.DMA((n,))],
    )(*arrs)


def rs_pair(grads):
    n = len(grads)

    def body(*refs):
        ins, outs = refs[:n], refs[n:2 * n]
        send_sems, recv_sems = refs[2 * n:]
        x, y, c = _pos()
        cps = []
        for a in range(n):
            for k in range(4):
                cps.append(pltpu.make_async_remote_copy(
                    src_ref=ins[a].at[2 * k + 1 - c], dst_ref=outs[a].at[k], send_sem=send_sems.at[a, k],
                    recv_sem=recv_sems.at[a, k], device_id=(x, y, 1 - c), device_id_type=MESH))
        for cp in cps:
            cp.start()
        for cp in cps:
            cp.wait_recv()
        for cp in cps:
            cp.wait_send()

    return pl.pallas_call(
        body, name="rs_pair", in_specs=[_ANY] * n, out_specs=[_ANY] * n,
        out_shape=[SDS((4,) + g.shape[1:], g.dtype) for g in grads],
        scratch_shapes=[pltpu.SemaphoreType.DMA((n, 4)), pltpu.SemaphoreType.DMA((n, 4))],
    )(*grads)


def rs_chips(parts):
    n = len(parts)

    def body(*refs):
        ins, outs = refs[:n], refs[n:2 * n]
        send_sems, recv_sems = refs[2 * n:]
        x, y, c = _pos()
        cps = []
        for a in range(n):
            for r, (px, py) in enumerate(_other_chips(x, y)):
                cps.append(pltpu.make_async_remote_copy(
                    src_ref=ins[a].at[2 * px + py], dst_ref=outs[a].at[r], send_sem=send_sems.at[a, r],
                    recv_sem=recv_sems.at[a, r], device_id=(px, py, c), device_id_type=MESH))
        for cp in cps:
            cp.start()
        for cp in cps:
            cp.wait_recv()
        for cp in cps:
            cp.wait_send()

    return pl.pallas_call(
        body, name="rs_chips", in_specs=[_ANY] * n, out_specs=[_ANY] * n,
        out_shape=[SDS((3,) + g.shape[1:], g.dtype) for g in parts],
        scratch_shapes=[pltpu.SemaphoreType.DMA((n, 3)), pltpu.SemaphoreType.DMA((n, 3))],
    )(*parts)


def _row_tile(r, c):
    return min(r, max(8, (256 * 1024) // c))


def add_pairs(g, r1, core, name):
    _, R, C = g.shape
    tr = _row_tile(R, C)

    def body(core_ref, g_ref, r_ref, o_ref):
        o_ref[...] = (g_ref[...].astype(f32) + r_ref[...].astype(f32)).astype(bf16)

    return pl.pallas_call(
        body, name=name, out_shape=SDS((4, R, C), bf16),
        grid_spec=pltpu.PrefetchScalarGridSpec(
            num_scalar_prefetch=1, grid=(4, R // tr),
            in_specs=[pl.BlockSpec((None, tr, C), lambda k, i, core: (2 * k + core[0], i, 0)),
                      pl.BlockSpec((None, tr, C), lambda k, i, core: (k, i, 0))],
            out_specs=pl.BlockSpec((None, tr, C), lambda k, i, core: (k, i, 0))),
        compiler_params=_cp(dimension_semantics=("arbitrary", "arbitrary")),
    )(core, g, r1)


def _adamw(w, g, m, v):
    m = ADAM_B1 * m + (1.0 - ADAM_B1) * g
    v = ADAM_B2 * v + (1.0 - ADAM_B2) * (g * g)
    m_hat = m / (1.0 - ADAM_B1 ** ADAM_STEP)
    v_hat = v / (1.0 - ADAM_B2 ** ADAM_STEP)
    delta = -ADAM_LR * (m_hat / (jnp.sqrt(v_hat) + ADAM_EPS) + ADAM_WD * w)
    return delta, m, v


def adam_big(w, m, v, part, r2, chip, name, layer=0, prev=None):
    L, R, C = w.shape
    tr = _row_tile(R, C)

    def body(chip_ref, w_ref, m_ref, v_ref, p_ref, r_ref, *rest):
        g_out, d_out, m_out, v_out = rest[-4:]
        g = p_ref[...].astype(f32) + r_ref[0].astype(f32) + r_ref[1].astype(f32) + r_ref[2].astype(f32)
        d, m_, v_ = _adamw(w_ref[...], g, m_ref[...], v_ref[...])
        g_out[...] = g
        d_out[...] = d
        m_out[...] = m_
        v_out[...] = v_

    blk = pl.BlockSpec((None, tr, C), lambda i, chip: (layer, i, 0))
    extra = [] if prev is None else list(prev)
    return pl.pallas_call(
        body, name=name, out_shape=[SDS((L, R, C), f32)] * 4,
        grid_spec=pltpu.PrefetchScalarGridSpec(
            num_scalar_prefetch=1, grid=(R // tr,),
            in_specs=[blk, blk, blk,
                      pl.BlockSpec((None, tr, C), lambda i, chip: (chip[0], i, 0)),
                      pl.BlockSpec((3, tr, C), lambda i, chip: (0, i, 0))] + [_ANY] * len(extra),
            out_specs=[blk] * 4),
        input_output_aliases={6 + k: k for k in range(len(extra))},
        compiler_params=_cp(dimension_semantics=("arbitrary",)),
    )(chip, w, m, v, part, r2, *extra)


def allreduce_small(buf):
    shp = buf.shape

    def body(in_ref, out_ref, acc1, acc2, r0, r1, r2, send_sems, recv_sems):
        x, y, c = _pos()
        peers = [(x, y, 1 - c), (1 - x, y, c), (x, 1 - y, c)]
        srcs, rcvs, dsts = [in_ref, acc1, acc2], [r0, r1, r2], [acc1, acc2, out_ref]
        for s in range(3):
            cp = pltpu.make_async_remote_copy(src_ref=srcs[s], dst_ref=rcvs[s], send_sem=send_sems.at[s],
                                              recv_sem=recv_sems.at[s], device_id=peers[s], device_id_type=MESH)
            cp.start()
            cp.wait()
            dsts[s][...] = srcs[s][...] + rcvs[s][...]

    return pl.pallas_call(
        body, name="allreduce_small", out_shape=SDS(shp, f32),
        scratch_shapes=[pltpu.VMEM(shp, f32)] * 5 + [pltpu.SemaphoreType.DMA((3,)), pltpu.SemaphoreType.DMA((3,))],
    )(buf)


SMALL_ROWS = {'norm_mix': (0, 2, D), 'norm_mlp': (2, 2, D), 'norm_kv': (4, 1, D), 'norm_final': (5, 1, D),
              's5_d': (6, 1, D), 'b_q': (7, 1, D), 'b_o': (8, 1, D), 's5_b_glu': (9, 2, D), 'b_kv': (11, 1, 512),
              'sinks': (12, 1, 16), 's5_log_dt': (13, 1, 64), 's5_a_re': (16, 4, D), 's5_a_im': (20, 4, D),
              's5_b_re': (24, 64, D), 's5_b_im': (88, 64, D), 's5_c_re': (152, 64, D), 's5_c_im': (216, 64, D)}
ROW_PARAMS = ['norm_mix', 'norm_mlp', 'norm_kv', 'norm_final', 'b_q', 'b_o', 'b_kv', 'sinks', 's5_log_dt']
SHARD_PARAMS = ['s5_d', 's5_b_glu']
S5_PARAMS = ['s5_a_re', 's5_a_im', 's5_b_re', 's5_b_im', 's5_c_re', 's5_c_im']


def adam_small(dev, gsum, s5_grads, w, m, v):
    names = ROW_PARAMS + SHARD_PARAMS + S5_PARAMS
    n_g = len(ROW_PARAMS) + len(SHARD_PARAMS)

    def body(dev_ref, gs_ref, *refs):
        pos = [0]

        def take(k):
            r = refs[pos[0]:pos[0] + k]
            pos[0] += k
            return r

        g5 = take(len(S5_PARAMS))
        wr, mr, vr = take(len(names)), take(len(names)), take(len(names))
        g_out = take(n_g)
        d_out, m_out, v_out = take(len(names)), take(len(names)), take(len(names))
        dv = dev_ref[0]
        for i, n in enumerate(names):
            if n in S5_PARAMS:
                g = g5[S5_PARAMS.index(n)][...]
            elif n in SHARD_PARAMS:
                r0, _, _ = SMALL_ROWS[n]
                ln = wr[i].shape[1]
                g = jnp.zeros((1, ln), f32)
                for k in range(NDEV):
                    off = k * ln
                    piece = gs_ref[r0 + off // D:r0 + off // D + 1, off % D:off % D + ln]
                    g = g + jnp.where(dv == k, piece, 0.0)
                g_out[i][...] = g
            else:
                r0, nr, nl = SMALL_ROWS[n]
                g = gs_ref[r0:r0 + nr, 0:nl]
                g_out[i][...] = g
            d, m_, v_ = _adamw(wr[i][...], g, mr[i][...], vr[i][...])
            d_out[i][...] = d
            m_out[i][...] = m_
            v_out[i][...] = v_

    vm = pl.BlockSpec(memory_space=pltpu.VMEM)
    ins = [s5_grads[n] for n in S5_PARAMS] + [d[n] for d in (w, m, v) for n in names]
    shapes = [SDS(w[n].shape, f32) for n in names]
    res = pl.pallas_call(
        body, name="adam_small", in_specs=[pl.BlockSpec(memory_space=pltpu.SMEM)] + [vm] * (1 + len(ins)),
        out_specs=[vm] * (n_g + 3 * len(names)), out_shape=shapes[:n_g] + shapes * 3,
        compiler_params=_cp(),
    )(dev, gsum, *ins)
    g_o = dict(zip(names[:n_g], res[:n_g]))
    rest = res[n_g:]
    k = len(names)
    return g_o, dict(zip(names, rest[:k])), dict(zip(names, rest[k:2 * k])), dict(zip(names, rest[2 * k:]))


WEIGHTS = ['norm_mix', 'norm_mlp', 'norm_kv', 'norm_final', 's5_a_re', 's5_a_im', 's5_log_dt', 's5_b_re', 's5_b_im',
           's5_c_re', 's5_c_im', 's5_d', 's5_w_glu', 's5_b_glu', 'w_kv', 'b_kv', 'w_q', 'b_q', 'sinks', 'w_o', 'b_o',
           'w_mlp_in', 'w_mlp_out']
BIG = ['s5_w_glu', 'w_kv', 'w_q', 'w_o', 'w_mlp_in', 'w_mlp_out']
BIG_2D = {'s5_w_glu': (D, 256), 'w_kv': (128, 512), 'w_q': (128, D), 'w_o': (128, D), 'w_mlp_in': (2 * D, 512),
          'w_mlp_out': (2 * 512, D)}
SHARDED_SMALL = {'s5_d': D, 's5_b_glu': 2 * D}
SMALL = [n for n in WEIGHTS if n not in BIG]
SMALL_SIZE = {'norm_mix': 2 * D, 'norm_mlp': 2 * D, 'norm_kv': D, 'norm_final': D, 's5_a_re': 4096, 's5_a_im': 4096,
              's5_log_dt': 64, 's5_b_re': 65536, 's5_b_im': 65536, 's5_c_re': 65536, 's5_c_im': 65536, 's5_d': D,
              's5_b_glu': 2 * D, 'b_kv': 512, 'b_q': D, 'sinks': 16, 'b_o': D}


def _pack(vals):
    parts = []
    for n in SMALL:
        v = vals[n].reshape(-1).astype(f32)
        parts.append(jnp.pad(v, (0, (-v.shape[0]) % 128)))
    flat = jnp.concatenate(parts)
    flat = jnp.pad(flat, (0, (-flat.shape[0]) % 1024))
    return flat.reshape(-1, 128)


def _unpack(buf):
    flat = buf.reshape(-1)
    out, off = {}, 0
    for n in SMALL:
        sz = SMALL_SIZE[n]
        out[n] = flat[off:off + sz]
        off += sz + (-sz) % 128
    return out


def kernel(x, norm_mix, norm_mlp, norm_kv, norm_final, s5_a_re, s5_a_im, s5_log_dt, s5_b_re, s5_b_im, s5_c_re, s5_c_im, s5_d, s5_w_glu, s5_b_glu, w_kv, b_kv, w_q, b_q, sinks, w_o, b_o, w_mlp_in, w_mlp_out, loss_target, m_norm_mix, m_norm_mlp, m_norm_kv, m_norm_final, m_s5_a_re, m_s5_a_im, m_s5_log_dt, m_s5_b_re, m_s5_b_im, m_s5_c_re, m_s5_c_im, m_s5_d, m_s5_w_glu, m_s5_b_glu, m_w_kv, m_b_kv, m_w_q, m_b_q, m_sinks, m_w_o, m_b_o, m_w_mlp_in, m_w_mlp_out, v_norm_mix, v_norm_mlp, v_norm_kv, v_norm_final, v_s5_a_re, v_s5_a_im, v_s5_log_dt, v_s5_b_re, v_s5_b_im, v_s5_c_re, v_s5_c_im, v_s5_d, v_s5_w_glu, v_s5_b_glu, v_w_kv, v_b_kv, v_w_q, v_b_q, v_sinks, v_w_o, v_b_o, v_w_mlp_in, v_w_mlp_out):
    w = dict(norm_mix=norm_mix, norm_mlp=norm_mlp, norm_kv=norm_kv, norm_final=norm_final, s5_a_re=s5_a_re,
             s5_a_im=s5_a_im, s5_log_dt=s5_log_dt, s5_b_re=s5_b_re, s5_b_im=s5_b_im, s5_c_re=s5_c_re, s5_c_im=s5_c_im,
             s5_d=s5_d, s5_w_glu=s5_w_glu, s5_b_glu=s5_b_glu, w_kv=w_kv, b_kv=b_kv, w_q=w_q, b_q=b_q, sinks=sinks,
             w_o=w_o, b_o=b_o, w_mlp_in=w_mlp_in, w_mlp_out=w_mlp_out)
    m = dict(norm_mix=m_norm_mix, norm_mlp=m_norm_mlp, norm_kv=m_norm_kv, norm_final=m_norm_final, s5_a_re=m_s5_a_re,
             s5_a_im=m_s5_a_im, s5_log_dt=m_s5_log_dt, s5_b_re=m_s5_b_re, s5_b_im=m_s5_b_im, s5_c_re=m_s5_c_re,
             s5_c_im=m_s5_c_im, s5_d=m_s5_d, s5_w_glu=m_s5_w_glu, s5_b_glu=m_s5_b_glu, w_kv=m_w_kv, b_kv=m_b_kv,
             w_q=m_w_q, b_q=m_b_q, sinks=m_sinks, w_o=m_w_o, b_o=m_b_o, w_mlp_in=m_w_mlp_in, w_mlp_out=m_w_mlp_out)
    v = dict(norm_mix=v_norm_mix, norm_mlp=v_norm_mlp, norm_kv=v_norm_kv, norm_final=v_norm_final, s5_a_re=v_s5_a_re,
             s5_a_im=v_s5_a_im, s5_log_dt=v_s5_log_dt, s5_b_re=v_s5_b_re, s5_b_im=v_s5_b_im, s5_c_re=v_s5_c_re,
             s5_c_im=v_s5_c_im, s5_d=v_s5_d, s5_w_glu=v_s5_w_glu, s5_b_glu=v_s5_b_glu, w_kv=v_w_kv, b_kv=v_b_kv,
             w_q=v_w_q, b_q=v_b_q, sinks=v_sinks, w_o=v_w_o, b_o=v_b_o, w_mlp_in=v_w_mlp_in, w_mlp_out=v_w_mlp_out)
    xi, yi, ci = _pos()
    dev = 4 * xi + 2 * yi + ci
    core = ci.reshape(1).astype(jnp.int32)
    chip = (2 * xi + yi).reshape(1).astype(jnp.int32)

    shards = {
        "s5_w_glu": s5_w_glu[0].astype(bf16), "w_kv": w_kv.astype(bf16), "w_q": w_q[0].astype(bf16),
        "w_o": w_o[0].astype(bf16), "w_in0": w_mlp_in[0].astype(bf16), "w_in1": w_mlp_in[1].astype(bf16),
        "w_out0": w_mlp_out[0].astype(bf16), "w_out1": w_mlp_out[1].astype(bf16),
        "vecs": jnp.broadcast_to(jnp.concatenate([s5_d, s5_b_glu], axis=1), (8, 384)),
    }
    loss, grad_x, grads, big = fwd_bwd(x[0], loss_target[0], {n: w[n] for n in SMALL}, shards, core)
    loss = lax.psum(loss[0, 0], ("x", "y", "c"))

    out_g, out_d, out_m, out_v = {}, {}, {}, {}
    for n in ("s5_w_glu", "w_kv", "w_q", "w_o"):
        shp = w[n].shape
        r3 = (1,) + BIG_2D[n]
        res = adam_big(w[n].reshape(r3), m[n].reshape(r3), v[n].reshape(r3), *big[n], chip, f"adam_{n}")
        out_g[n], out_d[n], out_m[n], out_v[n] = [r.reshape(shp) for r in res]
    for n, k in (("w_mlp_in", "w_in"), ("w_mlp_out", "w_out")):
        res = adam_big(w[n], m[n], v[n], *big[k + "1"], chip, f"adam_{k}1", layer=1)
        res = adam_big(w[n], m[n], v[n], *big[k + "0"], chip, f"adam_{k}0", layer=0, prev=res)
        out_g[n], out_d[n], out_m[n], out_v[n] = res

    gsum = allreduce_small(grads)
    for n in S5_PARAMS:
        r0, nr, _ = SMALL_ROWS[n]
        out_g[n] = gsum[r0:r0 + nr].reshape(w[n].shape)
    as2d = lambda d: {n: (d[n].reshape(1, -1) if d[n].ndim == 1 else d[n]) for n in SMALL}
    g_s, d_s, m_s, v_s = adam_small(dev.reshape(1).astype(jnp.int32), gsum, out_g, as2d(w), as2d(m), as2d(v))
    for src, dst in ((g_s, out_g), (d_s, out_d), (m_s, out_m), (v_s, out_v)):
        for n, val in src.items():
            dst[n] = val.reshape(w[n].shape)

    return (loss, grad_x[None], *[out_g[n] for n in WEIGHTS], *[out_d[n] for n in WEIGHTS],
            *[out_m[n] for n in WEIGHTS], *[out_v[n] for n in WEIGHTS])
```

```python
import functools
import math

import jax
import jax.numpy as jnp
from jax import lax
from jax.experimental import pallas as pl
from jax.experimental.pallas import tpu as pltpu

f32 = jnp.float32
bf16 = jnp.bfloat16
SDS = jax.ShapeDtypeStruct

T = 2048
D = 1024
NDEV = 8
NORM_EPS = 1e-5
S5_G, S5_C, S5_P = 64, 16, 64
S5_SUB = 8
S5_CH = 8
S5_STEPS = T // S5_CH
DT_MIN_LAMBDA = -1e-4
HEAD_DIM = 64
N_KV = 4
Q_PER_KV = 4
BLK = 128
D_FF_SHARD = 512
ADAM_LR, ADAM_B1, ADAM_B2, ADAM_EPS, ADAM_WD, ADAM_STEP = 0.001, 0.9, 0.999, 1e-08, 0.01, 10
VMEM_LIMIT = 56 * 1024 * 1024
MESH = pl.DeviceIdType.MESH


def _cp(**kw):
    return pltpu.CompilerParams(vmem_limit_bytes=VMEM_LIMIT, **kw)


def _dot(a, b):
    return jnp.dot(a, b, preferred_element_type=f32)


def _dot_nt(a, b):
    return lax.dot_general(a, b, (((1,), (1,)), ((), ())), preferred_element_type=f32)


def _dot_tn(a, b):
    return lax.dot_general(a, b, (((0,), (0,)), ((), ())), preferred_element_type=f32)


def _rms(x, g):
    r = lax.rsqrt(jnp.mean(x * x, axis=-1, keepdims=True) + NORM_EPS)
    return x * r * g, r


def _rms_bwd(x, g, dy):
    r = lax.rsqrt(jnp.mean(x * x, axis=-1, keepdims=True) + NORM_EPS)
    u = dy * g
    dx = r * u - (r * r * r) * x * jnp.mean(u * x, axis=-1, keepdims=True)
    return dx, dy * x * r


def _colsum8(v):
    s = jnp.sum(v, axis=0, keepdims=True)
    row = lax.broadcasted_iota(jnp.int32, (8, v.shape[1]), 0)
    return jnp.where(row == 0, jnp.broadcast_to(s, (8, v.shape[1])), 0.0)


def _full(shape):
    nd = len(shape)
    return pl.BlockSpec(shape, lambda *_: (0,) * nd, pipeline_mode=pl.Buffered(1))


_ANY = pl.BlockSpec(memory_space=pl.ANY)


def _pos():
    return lax.axis_index("x"), lax.axis_index("y"), lax.axis_index("c")


def _other_chips(x, y):
    return [(1 - x, y), (x, 1 - y), (1 - x, 1 - y)]


class BgGather:
    def __init__(self, arrs):
        n = len(arrs)
        self.arrs = list(arrs)
        self.out_shape = [SDS((NDEV,) + a.shape, a.dtype) for a in arrs]
        self.scratch = [pltpu.SemaphoreType.DMA((n, 7)), pltpu.SemaphoreType.DMA((n, 7)),
                        pltpu.SemaphoreType.DMA((n,))]
        self.has_mid = True
        self.result = None

    def _copy(self, ins, outs, sems, a, k, block, to, own=False):
        dst = outs[a].at[4 * block[0] + 2 * block[1] + block[2]]
        return pltpu.make_async_remote_copy(
            src_ref=ins[a] if own else dst, dst_ref=dst, send_sem=sems[0].at[a, k], recv_sem=sems[1].at[a, k],
            device_id=to, device_id_type=MESH)

    def _mine(self, ins, outs, sems):
        x, y, c = _pos()
        return [pltpu.make_async_copy(ins[a], outs[a].at[4 * x + 2 * y + c], sems[2].at[a])
                for a in range(len(self.arrs))]

    def _first(self, ins, outs, sems):
        x, y, c = _pos()
        me = (x, y, c)
        cps = []
        for a in range(len(self.arrs)):
            cps.append(self._copy(ins, outs, sems, a, 0, me, (x, y, 1 - c), own=True))
            cps += [self._copy(ins, outs, sems, a, 1 + j, me, (*chip, c), own=True)
                    for j, chip in enumerate(_other_chips(x, y))]
        return cps

    def _passed(self, ins, outs, sems):
        x, y, c = _pos()
        return [self._copy(ins, outs, sems, a, 4 + j, (*chip, c), (x, y, 1 - c))
                for j, chip in enumerate(_other_chips(x, y)) for a in range(len(self.arrs))]

    def start(self, ins, outs, sems):
        for cp in self._mine(ins, outs, sems) + self._first(ins, outs, sems):
            cp.start()

    def mid(self, ins, outs, sems):
        x, y, c = _pos()
        for j, chip in enumerate(_other_chips(x, y)):
            for a in range(len(self.arrs)):
                self._copy(ins, outs, sems, a, 1 + j, (*chip, c), (x, y, c)).wait_recv()
                self._copy(ins, outs, sems, a, 4 + j, (*chip, c), (x, y, 1 - c)).start()

    def finish(self, ins, outs, sems):
        x, y, c = _pos()
        for a in range(len(self.arrs)):
            self._copy(ins, outs, sems, a, 0, (x, y, 1 - c), (x, y, c)).wait_recv()
            for j, chip in enumerate(_other_chips(x, y)):
                self._copy(ins, outs, sems, a, 4 + j, (*chip, 1 - c), (x, y, c)).wait_recv()
        for cp in self._first(ins, outs, sems) + self._passed(ins, outs, sems):
            cp.wait_send()
        for cp in self._mine(ins, outs, sems):
            cp.wait()


class BgPair:
    def __init__(self, arrs):
        n = len(arrs)
        self.arrs = list(arrs)
        self.out_shape = [SDS((4,) + a.shape[1:], a.dtype) for a in arrs]
        self.scratch = [pltpu.SemaphoreType.DMA((n, 4)), pltpu.SemaphoreType.DMA((n, 4))]
        self.has_mid = False
        self.result = None

    def _copies(self, ins, outs, sems):
        x, y, c = _pos()
        return [pltpu.make_async_remote_copy(
            src_ref=ins[a].at[2 * k + 1 - c], dst_ref=outs[a].at[k], send_sem=sems[0].at[a, k],
            recv_sem=sems[1].at[a, k], device_id=(x, y, 1 - c), device_id_type=MESH)
            for a in range(len(self.arrs)) for k in range(4)]

    def start(self, ins, outs, sems):
        for cp in self._copies(ins, outs, sems):
            cp.start()

    def finish(self, ins, outs, sems):
        cps = self._copies(ins, outs, sems)
        for cp in cps:
            cp.wait_recv()
        for cp in cps:
            cp.wait_send()


class BgChips(BgPair):
    def __init__(self, arrs):
        n = len(arrs)
        self.arrs = list(arrs)
        self.out_shape = [SDS((3,) + a.shape[1:], a.dtype) for a in arrs]
        self.scratch = [pltpu.SemaphoreType.DMA((n, 3)), pltpu.SemaphoreType.DMA((n, 3))]
        self.has_mid = False
        self.result = None

    def _copies(self, ins, outs, sems):
        x, y, c = _pos()
        return [pltpu.make_async_remote_copy(
            src_ref=ins[a].at[2 * px + py], dst_ref=outs[a].at[r], send_sem=sems[0].at[a, r],
            recv_sem=sems[1].at[a, r], device_id=(px, py, c), device_id_type=MESH)
            for a in range(len(self.arrs)) for r, (px, py) in enumerate(_other_chips(x, y))]


def _call(bgs, body, *, name, grid, in_specs, out_specs, out_shape, scratch_shapes=(), compiler_params=None):
    single = not isinstance(out_shape, (list, tuple))
    out_specs_l = [out_specs] if single else list(out_specs)
    out_shape_l = [out_shape] if single else list(out_shape)
    bgs = [b for b in (bgs or []) if b is not None]
    n_in, n_out, n_sc = len(in_specs), len(out_shape_l), len(scratch_shapes)
    nsteps = math.prod(grid)

    def full(*refs):
        pos = [0]

        def take(k):
            r = refs[pos[0]:pos[0] + k]
            pos[0] += k
            return r

        ins = take(n_in)
        b_ins = [take(len(b.arrs)) for b in bgs]
        outs = take(n_out)
        b_outs = [take(len(b.out_shape)) for b in bgs]
        sc = take(n_sc)
        b_sc = [take(len(b.scratch)) for b in bgs]
        if bgs:
            step = pl.program_id(0)
            for d in range(1, len(grid)):
                step = step * grid[d] + pl.program_id(d)

            @pl.when(step == 0)
            def _():
                for b, i_, o_, s_ in zip(bgs, b_ins, b_outs, b_sc):
                    b.start(i_, o_, s_)

        body(*ins, *outs, *sc)
        if bgs:
            for b, i_, o_, s_ in zip(bgs, b_ins, b_outs, b_sc):
                if b.has_mid:
                    @pl.when(step == max(0, (3 * nsteps) // 4 - 1))
                    def _():
                        b.mid(i_, o_, s_)

            @pl.when(step == nsteps - 1)
            def _():
                for b, i_, o_, s_ in zip(bgs, b_ins, b_outs, b_sc):
                    b.finish(i_, o_, s_)

    def run(*args):
        res = pl.pallas_call(
            full, name=name, grid=grid,
            in_specs=list(in_specs) + [_ANY] * sum(len(b.arrs) for b in bgs),
            out_specs=out_specs_l + [_ANY] * sum(len(b.out_shape) for b in bgs),
            out_shape=out_shape_l + [s for b in bgs for s in b.out_shape],
            scratch_shapes=list(scratch_shapes) + [s for b in bgs for s in b.scratch],
            compiler_params=compiler_params,
        )(*args, *[a for b in bgs for a in b.arrs])
        rest = list(res[n_out:])
        for b in bgs:
            b.result, rest = rest[:len(b.out_shape)], rest[len(b.out_shape):]
        return res[0] if single else list(res[:n_out])

    return run


def s5_discretize(a_re, a_im, log_dt, b_re, b_im, c_re, c_im):
    lam_r = jnp.minimum(a_re, DT_MIN_LAMBDA)
    lam_i = a_im
    dt = jnp.exp(log_dt)[:, None]
    e = jnp.exp(lam_r * dt)
    lbr = e * jnp.cos(lam_i * dt)
    lbi = e * jnp.sin(lam_i * dt)
    den = lam_r * lam_r + lam_i * lam_i
    cf_r = ((lbr - 1.0) * lam_r + lbi * lam_i) / den
    cf_i = (lbi * lam_r - (lbr - 1.0) * lam_i) / den
    bb_r = cf_r[:, :, None] * b_re - cf_i[:, :, None] * b_im
    bb_i = cf_r[:, :, None] * b_im + cf_i[:, :, None] * b_re
    eye = jnp.eye(8, dtype=f32)

    def blk_b(m):
        return jnp.einsum('bgpc,gh->bgchp', m.reshape(8, 8, S5_P, S5_C), eye).reshape(8, 128, 512)

    def blk_c(m):
        return jnp.einsum('bgcp,gh->bgphc', m.reshape(8, 8, S5_C, S5_P), eye).reshape(8, 512, 128)

    bm = jnp.concatenate([blk_b(bb_r), blk_b(bb_i)], axis=-1)
    cm = jnp.concatenate([blk_c(c_re), -blk_c(c_im)], axis=1)
    lam = jnp.stack([lbr.reshape(8, 512), lbi.reshape(8, 512)], axis=1)
    lam = jnp.broadcast_to(lam[:, :, None, :], (8, 2, 8, 512))
    return lam, bm, cm


def _cmul(ar, ai, br, bi):
    return ar * br - ai * bi, ar * bi + ai * br


def _shift_rows(v, k, up):
    row = lax.broadcasted_iota(jnp.int32, v.shape, 0)
    if up:
        return jnp.where(row < 8 - k, pltpu.roll(v, 8 - k, 0), 0.0)
    return jnp.where(row >= k, pltpu.roll(v, k, 0), 0.0)


def _chunk_scan(S, lr, li, reverse, aux=None):
    if reverse:
        li = -li
    z = jnp.zeros((8, 512), f32)

    def idx(i):
        return (S5_STEPS - 1 - i) if reverse else i

    def rec(xr, xi, row):
        br = S[row, 0:512]
        bi = S[row, 512:1024]
        return lr * xr - li * xi + br, lr * xi + li * xr + bi

    def step1(i, c):
        row = pl.ds(pl.multiple_of(idx(i) * 8, 8), 8)
        return rec(c[0], c[1], row)

    er, ei = lax.fori_loop(0, S5_STEPS, step1, (z, z), unroll=8)
    ar, ai = lr, li
    for _ in range(8):
        ar, ai = _cmul(ar, ai, ar, ai)
    cr, ci = _shift_rows(er, 1, reverse), _shift_rows(ei, 1, reverse)
    for k in (1, 2, 4):
        sr, si = _shift_rows(cr, k, reverse), _shift_rows(ci, k, reverse)
        pr, pi_ = _cmul(ar, ai, sr, si)
        cr, ci = cr + pr, ci + pi_
        ar, ai = _cmul(ar, ai, ar, ai)

    if aux is None:
        def step2(i, c):
            row = pl.ds(pl.multiple_of(idx(i) * 8, 8), 8)
            xr, xi = rec(c[0], c[1], row)
            S[row, 0:512] = xr
            S[row, 512:1024] = xi
            return xr, xi

        lax.fori_loop(0, S5_STEPS, step2, (cr, ci), unroll=8)
        return None

    def step2(i, c):
        gr0, gi0, dr, di = c
        s = idx(i)
        row = pl.ds(pl.multiple_of(s * 8, 8), 8)
        gr, gi = rec(gr0, gi0, row)
        S[row, 0:512] = gr
        S[row, 512:1024] = gi
        prow = pl.ds(pl.multiple_of(jnp.maximum(s - 1, 0) * 8, 8), 8)
        xr = aux[prow, 0:512]
        xi = aux[prow, 512:1024]
        dr = dr + gr * xr + gi * xi
        di = di + gi * xr - gr * xi
        return gr, gi, dr, di

    gr, gi, dr, di = lax.fori_loop(0, S5_STEPS - 1, step2, (cr, ci, z, z), unroll=8)
    row0 = pl.ds(0, 8)
    gr, gi = rec(gr, gi, row0)
    S[row0, 0:512] = gr
    S[row0, 512:1024] = gi
    last = pl.ds((S5_STEPS - 1) * 8, 8)
    xr = _shift_rows(aux[last, 0:512], 1, False)
    xi = _shift_rows(aux[last, 512:1024], 1, False)
    dr = dr + gr * xr + gi * xi
    di = di + gi * xr - gr * xi
    return dr, di


_ROWS = 256


def _row_loop(fn):
    def body(r, c):
        fn(pl.ds(pl.multiple_of(r * _ROWS, _ROWS), _ROWS))
        return c
    lax.fori_loop(0, T // _ROWS, body, 0)


def s5_core_fwd(hn, bm, lam, cm, bg=()):
    def body(u_ref, b_ref, lam_ref, c_ref, ys_ref, S):
        def bu(rows):
            S[rows, :] = _dot(u_ref[rows, :], b_ref[...])
        _row_loop(bu)
        _chunk_scan(S, lam_ref[0], lam_ref[1], False)

        def ys(rows):
            ys_ref[rows, :] = _dot(S[rows, :].astype(bf16), c_ref[...])
        _row_loop(ys)

    return _call(
        bg, body, name="s5_core_fwd", grid=(S5_SUB,),
        in_specs=[pl.BlockSpec((T, 128), lambda b: (0, b)),
                  pl.BlockSpec((None, 128, 1024), lambda b: (b, 0, 0)),
                  pl.BlockSpec((None, 2, 8, 512), lambda b: (b, 0, 0, 0)),
                  pl.BlockSpec((None, 1024, 128), lambda b: (b, 0, 0))],
        out_specs=pl.BlockSpec((T, 128), lambda b: (0, b)),
        out_shape=SDS((T, D), f32),
        scratch_shapes=[pltpu.VMEM((T, 1024), f32)],
        compiler_params=_cp(dimension_semantics=("arbitrary",)),
    )(hn, bm, lam, cm)


def s5_core_bwd(hn, dy, bm, lam, cm, bg=()):
    def body(u_ref, dy_ref, b_ref, lam_ref, c_ref, du_ref, db_ref, dct_ref, dlam_ref, S1, S2):
        def bu(rows):
            S1[rows, :] = _dot(u_ref[rows, :], b_ref[...])
        _row_loop(bu)
        _chunk_scan(S1, lam_ref[0], lam_ref[1], False)
        dct_ref[...] = jnp.zeros_like(dct_ref)

        def dx(rows):
            dyb = dy_ref[rows, :].astype(bf16)
            S2[rows, :] = _dot_nt(dyb, c_ref[...])
            dct_ref[...] += _dot_tn(dyb, S1[rows, :].astype(bf16))
        _row_loop(dx)
        dr, di = _chunk_scan(S2, lam_ref[0], lam_ref[1], True, aux=S1)
        dlam_ref[0] = dr
        dlam_ref[1] = di
        db_ref[...] = jnp.zeros_like(db_ref)

        def dbu(rows):
            gb = S2[rows, :].astype(bf16)
            db_ref[...] += _dot_tn(u_ref[rows, :], gb)
            du_ref[rows, :] = _dot_nt(gb, b_ref[...])
        _row_loop(dbu)

    return _call(
        bg, body, name="s5_core_bwd", grid=(S5_SUB,),
        in_specs=[pl.BlockSpec((T, 128), lambda b: (0, b)),
                  pl.BlockSpec((T, 128), lambda b: (0, b)),
                  pl.BlockSpec((None, 128, 1024), lambda b: (b, 0, 0)),
                  pl.BlockSpec((None, 2, 8, 512), lambda b: (b, 0, 0, 0)),
                  pl.BlockSpec((None, 1024, 128), lambda b: (b, 0, 0))],
        out_specs=[pl.BlockSpec((T, 128), lambda b: (0, b)),
                   pl.BlockSpec((None, 128, 1024), lambda b: (b, 0, 0)),
                   pl.BlockSpec((None, 128, 1024), lambda b: (b, 0, 0)),
                   pl.BlockSpec((None, 2, 8, 512), lambda b: (b, 0, 0, 0))],
        out_shape=[SDS((T, D), f32), SDS((8, 128, 1024), f32), SDS((8, 128, 1024), f32), SDS((8, 2, 8, 512), f32)],
        scratch_shapes=[pltpu.VMEM((T, 1024), f32), pltpu.VMEM((T, 1024), f32)],
        compiler_params=_cp(dimension_semantics=("arbitrary",)),
    )(hn, dy, bm, lam, cm)


TM = 512
NT = T // TM


def _tile(n=D):
    return pl.BlockSpec((TM, n), lambda i: (i, 0))


def s5_pre(xp, g):
    def body(x_ref, g_ref, hn_ref):
        hn, _ = _rms(x_ref[...], g_ref[...])
        hn_ref[...] = hn.astype(bf16)

    return pl.pallas_call(
        body, name="s5_pre", grid=(NT,), in_specs=[_tile(), _full((1, D))], out_specs=_tile(),
        out_shape=SDS((T, D), bf16), compiler_params=_cp(dimension_semantics=("arbitrary",)),
    )(xp, g)


def _gelu_grad(y):
    c = math.sqrt(2.0 / math.pi)
    t = jnp.tanh(c * (y + 0.044715 * y * y * y))
    return 0.5 * (1.0 + t) + 0.5 * y * (1.0 - t * t) * c * (1.0 + 3.0 * 0.044715 * y * y)


def s5_post(ys, xp, g, d, wglu, bglu, bg=()):
    def body(ys_ref, x_ref, g_ref, d_ref, w_ref, b_ref, y_ref, z_ref, h_ref):
        x = x_ref[...]
        hn, _ = _rms(x, g_ref[...])
        y = ys_ref[...] + d_ref[...] * hn
        y_ref[...] = y
        yg = jax.nn.gelu(y).astype(bf16)
        for j in range(4):
            cv = slice(j * 256, (j + 1) * 256)
            cg = slice(1024 + j * 256, 1024 + (j + 1) * 256)
            val = _dot(yg, w_ref[j]) + b_ref[:, cv]
            gate = _dot(yg, w_ref[j + 4]) + b_ref[:, cg]
            z_ref[:, cv] = val
            z_ref[:, cg] = gate
            h_ref[:, cv] = x[:, cv] + val * jax.nn.sigmoid(gate)

    return _call(
        bg, body, name="s5_post", grid=(NT,),
        in_specs=[_tile(), _tile(), _full((1, D)), _full((1, D)), _full((8, D, 256)), _full((1, 2 * D))],
        out_specs=[_tile(), _tile(2 * D), _tile()],
        out_shape=[SDS((T, D), f32), SDS((T, 2 * D), f32), SDS((T, D), f32)],
        compiler_params=_cp(dimension_semantics=("arbitrary",)),
    )(ys, xp, g, d, wglu, bglu)


def s5_post_bwd(dh, y, z, wglu, bg=()):
    def body(dh_ref, y_ref, z_ref, w_ref, dy_ref, dw_ref, db_ref, acc):
        i = pl.program_id(0)

        @pl.when(i == 0)
        def _():
            acc[...] = jnp.zeros_like(acc)
            db_ref[...] = jnp.zeros_like(db_ref)

        dh_ = dh_ref[...]
        y = y_ref[...]
        yg = jax.nn.gelu(y).astype(bf16)
        dyg = jnp.zeros((TM, D), f32)
        for j in range(4):
            cv = slice(j * 256, (j + 1) * 256)
            cg = slice(1024 + j * 256, 1024 + (j + 1) * 256)
            val = z_ref[:, cv]
            sg = jax.nn.sigmoid(z_ref[:, cg])
            dval = dh_[:, cv] * sg
            dgate = dh_[:, cv] * val * sg * (1.0 - sg)
            db_ref[:, cv] += _colsum8(dval)
            db_ref[:, cg] += _colsum8(dgate)
            dvb = dval.astype(bf16)
            dgb = dgate.astype(bf16)
            acc[j] += _dot_tn(yg, dvb)
            acc[j + 4] += _dot_tn(yg, dgb)
            dyg = dyg + _dot_nt(dvb, w_ref[j]) + _dot_nt(dgb, w_ref[j + 4])
        dy_ref[...] = dyg * _gelu_grad(y)

        @pl.when(i == NT - 1)
        def _():
            dw_ref[...] = acc[...].astype(bf16)

    return _call(
        bg, body, name="s5_post_bwd", grid=(NT,),
        in_specs=[_tile(), _tile(), _tile(2 * D), _full((8, D, 256))],
        out_specs=[_tile(), _full((8, D, 256)), _full((8, 2 * D))],
        out_shape=[SDS((T, D), f32), SDS((8, D, 256), bf16), SDS((8, 2 * D), f32)],
        scratch_shapes=[pltpu.VMEM((8, D, 256), f32)],
        compiler_params=_cp(dimension_semantics=("arbitrary",)),
    )(dh, y, z, wglu)


def s5_pre_bwd(xp, g, du, dy, d, dh, bg=()):
    def body(x_ref, g_ref, du_ref, dy_ref, d_ref, dh_ref, dx_ref, dg_ref, dd_ref):
        i = pl.program_id(0)

        @pl.when(i == 0)
        def _():
            dg_ref[...] = jnp.zeros_like(dg_ref)
            dd_ref[...] = jnp.zeros_like(dd_ref)

        x = x_ref[...]
        g = g_ref[...]
        dy = dy_ref[...]
        hn, _ = _rms(x, g)
        dhn = du_ref[...] + d_ref[...] * dy
        dx, dgt = _rms_bwd(x, g, dhn)
        dx_ref[...] = dh_ref[...] + dx
        dg_ref[...] += _colsum8(dgt)
        dd_ref[...] += _colsum8(dy * hn)

    return _call(
        bg, body, name="s5_pre_bwd", grid=(NT,),
        in_specs=[_tile(), _full((1, D)), _tile(), _tile(), _full((1, D)), _tile()],
        out_specs=[_tile(), _full((8, D)), _full((8, D))],
        out_shape=[SDS((T, D), f32), SDS((8, D), f32), SDS((8, D), f32)],
        compiler_params=_cp(dimension_semantics=("arbitrary",)),
    )(xp, g, du, dy, d, dh)


TMF = 1024


def mlp_fwd(h, g, w_in, w_out, layer, bg=()):
    def body(h_ref, g_ref, wi_ref, wo_ref, hm_ref, out_ref, acc):
        j = pl.program_id(1)

        @pl.when(j == 0)
        def _():
            hm, _ = _rms(h_ref[...], g_ref[...])
            hm_ref[...] = hm.astype(bf16)
            acc[...] = jnp.zeros_like(acc)

        a = jnp.maximum(_dot(hm_ref[...], wi_ref[...]), 0.0)
        acc[...] += _dot((a * a).astype(bf16), wo_ref[...])

        @pl.when(j == NDEV - 1)
        def _():
            out_ref[...] = h_ref[...] + acc[...]

    return _call(
        bg, body, name=f"mlp_fwd{layer}", grid=(T // TMF, NDEV),
        in_specs=[pl.BlockSpec((TMF, D), lambda i, j: (i, 0)),
                  pl.BlockSpec((1, D), lambda i, j: (0, 0)),
                  pl.BlockSpec((None, D, D_FF_SHARD), lambda i, j: (j, 0, 0)),
                  pl.BlockSpec((None, D_FF_SHARD, D), lambda i, j: (j, 0, 0))],
        out_specs=[pl.BlockSpec((TMF, D), lambda i, j: (i, 0)), pl.BlockSpec((TMF, D), lambda i, j: (i, 0))],
        out_shape=[SDS((T, D), bf16), SDS((T, D), f32)],
        scratch_shapes=[pltpu.VMEM((TMF, D), f32)],
        compiler_params=_cp(dimension_semantics=("arbitrary", "arbitrary")),
    )(h, g, w_in, w_out)


def mlp_bwd(h, hm, g, dout, w_in, w_out, layer, bg=()):
    last = NDEV - 1

    def body(h_ref, hm_ref, g_ref, do_ref, wi_ref, wo_ref, dh_ref, dwi_ref, dwo_ref, dg_ref, dhm, awi, awo):
        j = pl.program_id(0)
        i = pl.program_id(1)
        rows = pl.ds(pl.multiple_of(i * TM, TM), TM)

        @pl.when(i == 0)
        def _():
            awi[...] = jnp.zeros_like(awi)
            awo[...] = jnp.zeros_like(awo)

        hm_ = hm_ref[...]
        dob = do_ref[...].astype(bf16)
        r = jnp.maximum(_dot(hm_, wi_ref[...]), 0.0)
        dz = (_dot_nt(dob, wo_ref[...]) * (2.0 * r)).astype(bf16)
        awo[...] += _dot_tn((r * r).astype(bf16), dob)
        awi[...] += _dot_tn(hm_, dz)
        part = _dot_nt(dz, wi_ref[...])

        @pl.when(j == 0)
        def _():
            dhm[rows, :] = part

        @pl.when(j > 0)
        def _():
            dhm[rows, :] += part

        @pl.when(i == NT - 1)
        def _():
            dwi_ref[...] = awi[...].astype(bf16)
            dwo_ref[...] = awo[...].astype(bf16)

        @pl.when(j == last)
        def _():
            @pl.when(i == 0)
            def _():
                dg_ref[...] = jnp.zeros_like(dg_ref)
            dx, dgt = _rms_bwd(h_ref[...], g_ref[...], dhm[rows, :])
            dh_ref[...] = do_ref[...] + dx
            dg_ref[...] += _colsum8(dgt)

    late = lambda j, i: (jnp.where(j == last, i, 0), 0)
    return _call(
        bg, body, name=f"mlp_bwd{layer}", grid=(NDEV, NT),
        in_specs=[pl.BlockSpec((TM, D), late),
                  pl.BlockSpec((TM, D), lambda j, i: (i, 0)),
                  pl.BlockSpec((1, D), lambda j, i: (0, 0)),
                  pl.BlockSpec((TM, D), lambda j, i: (i, 0)),
                  pl.BlockSpec((None, D, D_FF_SHARD), lambda j, i: (j, 0, 0)),
                  pl.BlockSpec((None, D_FF_SHARD, D), lambda j, i: (j, 0, 0))],
        out_specs=[pl.BlockSpec((TM, D), late),
                   pl.BlockSpec((None, D, D_FF_SHARD), lambda j, i: (j, 0, 0)),
                   pl.BlockSpec((None, D_FF_SHARD, D), lambda j, i: (j, 0, 0)),
                   pl.BlockSpec((8, D), lambda j, i: (0, 0))],
        out_shape=[SDS((T, D), f32), SDS((NDEV, D, D_FF_SHARD), bf16), SDS((NDEV, D_FF_SHARD, D), bf16),
                   SDS((8, D), f32)],
        scratch_shapes=[pltpu.VMEM((T, D), f32), pltpu.VMEM((D, D_FF_SHARD), f32), pltpu.VMEM((D_FF_SHARD, D), f32)],
        compiler_params=_cp(dimension_semantics=("arbitrary", "arbitrary")),
    )(h, hm, g, dout, w_in, w_out)


def attn_pre(h, g_kv, g_mix, wk4, wv4, bk4, bv4, wq, bq):
    def body(h_ref, gkv_ref, gm_ref, wk_ref, wv_ref, bk_ref, bv_ref, wq_ref, bq_ref,
             kvn_ref, hn_ref, k_ref, v_ref, q_ref):
        h_ = h_ref[...]
        kvn = _rms(h_, gkv_ref[...])[0].astype(bf16)
        hn = _rms(h_, gm_ref[...])[0].astype(bf16)
        kvn_ref[...] = kvn
        hn_ref[...] = hn
        k_ref[...] = (_dot(kvn, wk_ref[...]) + bk_ref[...]).astype(bf16)
        v_ref[...] = (_dot(kvn, wv_ref[...]) + bv_ref[...]).astype(bf16)
        q_ref[...] = (_dot(hn, wq_ref[...]) + bq_ref[...]).astype(bf16)

    return pl.pallas_call(
        body, name="attn_pre", grid=(NT,),
        in_specs=[_tile(), _full((1, D)), _full((1, D)), _full((D, D)), _full((D, D)), _full((1, D)), _full((1, D)),
                  _full((D, D)), _full((1, D))],
        out_specs=[_tile()] * 5,
        out_shape=[SDS((T, D), bf16)] * 5,
        compiler_params=_cp(dimension_semantics=("arbitrary",)),
    )(h, g_kv, g_mix, wk4, wv4, bk4, bv4, wq, bq)


def _attn_specs():
    cur = pl.BlockSpec((TM, 256), lambda j, n: (n, j))
    prev = pl.BlockSpec((BLK, 256), lambda j, n: (jnp.maximum(n * (TM // BLK) - 1, 0), j))
    return cur, prev


def _head_mask(g):
    lane = lax.broadcasted_iota(jnp.int32, (1, 256), 1)
    return (lane >= g * HEAD_DIM) & (lane < (g + 1) * HEAD_DIM)


def _stack_heads(t):
    return jnp.concatenate([jnp.where(_head_mask(g), t, 0) for g in range(Q_PER_KV)], axis=0)


def _unstack_heads(t):
    out = jnp.where(_head_mask(0), t[0:BLK], 0.0)
    for g in range(1, Q_PER_KV):
        out = out + jnp.where(_head_mask(g), t[g * BLK:(g + 1) * BLK], 0.0)
    return out


def _attn_probs(qs, k2, sinks, first):
    rows = Q_PER_KV * BLK
    s = _dot_nt(qs, k2) * (1.0 / math.sqrt(HEAD_DIM))
    qi = jnp.bitwise_and(lax.broadcasted_iota(jnp.int32, (rows, 2 * BLK), 0), BLK - 1)
    kj = lax.broadcasted_iota(jnp.int32, (rows, 2 * BLK), 1)
    diff = qi + BLK - kj
    valid = (diff >= 0) & (diff < BLK) & (jnp.logical_not(first) | (kj >= BLK))
    s = jnp.where(valid, s, -jnp.inf)
    rb = lax.broadcasted_iota(jnp.int32, (rows, 1), 0)
    sink = jnp.where(rb < BLK, sinks[0], jnp.where(rb < 2 * BLK, sinks[1], jnp.where(rb < 3 * BLK, sinks[2], sinks[3])))
    m = jnp.maximum(jnp.max(s, axis=-1, keepdims=True), sink)
    p = jnp.exp(s - m)
    ps = jnp.exp(sink - m)
    denom = jnp.sum(p, axis=-1, keepdims=True) + ps
    return p / denom, ps / denom


def attn_core_fwd(q, k4, v4, sinks, bg=()):
    nb = TM // BLK

    def body(sink_ref, q_ref, kc_ref, kp_ref, vc_ref, vp_ref, o_ref):
        j = pl.program_id(0)
        n = pl.program_id(1)
        sk = [sink_ref[j * Q_PER_KV + g] for g in range(Q_PER_KV)]
        for b in range(nb):
            qb = q_ref[b * BLK:(b + 1) * BLK, :]
            if b == 0:
                k2 = jnp.concatenate([kp_ref[...], kc_ref[0:BLK, :]], axis=0)
                v2 = jnp.concatenate([vp_ref[...], vc_ref[0:BLK, :]], axis=0)
                first = n == 0
            else:
                k2 = kc_ref[(b - 1) * BLK:(b + 1) * BLK, :]
                v2 = vc_ref[(b - 1) * BLK:(b + 1) * BLK, :]
                first = False
            a, _ = _attn_probs(_stack_heads(qb), k2, sk, first)
            o_ref[b * BLK:(b + 1) * BLK, :] = _unstack_heads(_dot(a.astype(bf16), v2)).astype(bf16)

    cur, prev = _attn_specs()
    return _call(
        bg, body, name="attn_core_fwd", grid=(N_KV, NT),
        in_specs=[pl.BlockSpec(memory_space=pltpu.SMEM), cur, cur, prev, cur, prev],
        out_specs=cur, out_shape=SDS((T, D), bf16),
        compiler_params=_cp(dimension_semantics=("arbitrary", "arbitrary")),
    )(sinks, q, k4, k4, v4, v4)


def attn_post(h, o, wo, bo):
    def body(h_ref, o_ref, w_ref, b_ref, out_ref):
        out_ref[...] = h_ref[...] + _dot(o_ref[...], w_ref[...]) + b_ref[...]

    return pl.pallas_call(
        body, name="attn_post", grid=(NT,), in_specs=[_tile(), _tile(), _full((D, D)), _full((1, D))],
        out_specs=_tile(), out_shape=SDS((T, D), f32), compiler_params=_cp(dimension_semantics=("arbitrary",)),
    )(h, o, wo, bo)


def attn_bwd_pre(dh, o, wo, bg=()):
    def body(dh_ref, o_ref, w_ref, do_ref, dw_ref, db_ref, acc):
        i = pl.program_id(0)

        @pl.when(i == 0)
        def _():
            acc[...] = jnp.zeros_like(acc)
            db_ref[...] = jnp.zeros_like(db_ref)

        dh_ = dh_ref[...]
        dhb = dh_.astype(bf16)
        do_ref[...] = _dot_nt(dhb, w_ref[...]).astype(bf16)
        acc[...] += _dot_tn(o_ref[...], dhb)
        db_ref[...] += _colsum8(dh_)

        @pl.when(i == NT - 1)
        def _():
            dw_ref[...] = acc[...].astype(bf16)

    return _call(
        bg, body, name="attn_bwd_pre", grid=(NT,), in_specs=[_tile(), _tile(), _full((D, D))],
        out_specs=[_tile(), _full((D, D)), _full((8, D))],
        out_shape=[SDS((T, D), bf16), SDS((D, D), bf16), SDS((8, D), f32)],
        scratch_shapes=[pltpu.VMEM((D, D), f32)],
        compiler_params=_cp(dimension_semantics=("arbitrary",)),
    )(dh, o, wo)


def attn_core_bwd(q, do, k4, v4, sinks, bg=()):
    nb = TM // BLK

    def body(sink_ref, q_ref, do_ref, kc_ref, kp_ref, vc_ref, vp_ref, dq_ref, dk_ref, dv_ref, ds_ref):
        j = pl.program_id(0)
        n = pl.program_id(1)

        @pl.when(n == 0)
        def _():
            dk_ref[...] = jnp.zeros_like(dk_ref)
            dv_ref[...] = jnp.zeros_like(dv_ref)
            ds_ref[...] = jnp.zeros_like(ds_ref)

        lane8 = lax.broadcasted_iota(jnp.int32, (8, 128), 1)
        row8 = lax.broadcasted_iota(jnp.int32, (8, 128), 0)
        sk = [sink_ref[j * Q_PER_KV + g] for g in range(Q_PER_KV)]
        for b in range(nb):
            qs = _stack_heads(q_ref[b * BLK:(b + 1) * BLK, :])
            dos = _stack_heads(do_ref[b * BLK:(b + 1) * BLK, :])
            if b == 0:
                k2 = jnp.concatenate([kp_ref[...], kc_ref[0:BLK, :]], axis=0)
                v2 = jnp.concatenate([vp_ref[...], vc_ref[0:BLK, :]], axis=0)
                first = n == 0
            else:
                k2 = kc_ref[(b - 1) * BLK:(b + 1) * BLK, :]
                v2 = vc_ref[(b - 1) * BLK:(b + 1) * BLK, :]
                first = False
            a, asink = _attn_probs(qs, k2, sk, first)
            dp = _dot_nt(dos, v2)
            dd = jnp.sum(a * dp, axis=-1, keepdims=True)
            dsc = (a * (dp - dd) * (1.0 / math.sqrt(HEAD_DIM))).astype(bf16)
            t = asink * dd
            for g in range(Q_PER_KV):
                dsink = -jnp.sum(t[g * BLK:(g + 1) * BLK], axis=0, keepdims=True)
                ds_ref[...] += jnp.where((lane8 == g) & (row8 == 0), jnp.broadcast_to(dsink, (8, 128)), 0.0)
            dq_ref[b * BLK:(b + 1) * BLK, :] = _unstack_heads(_dot(dsc, k2))
            dk2 = _dot_tn(dsc, qs)
            dv2 = _dot_tn(a.astype(bf16), dos)
            cur = pl.ds(pl.multiple_of(n * TM + b * BLK, BLK), BLK)
            dk_ref[cur, :] += dk2[BLK:, :]
            dv_ref[cur, :] += dv2[BLK:, :]
            if b == 0:
                @pl.when(n > 0)
                def _():
                    prv = pl.ds(pl.multiple_of(n * TM - BLK, BLK), BLK)
                    dk_ref[prv, :] += dk2[:BLK, :]
                    dv_ref[prv, :] += dv2[:BLK, :]
            else:
                prv = pl.ds(pl.multiple_of(n * TM + (b - 1) * BLK, BLK), BLK)
                dk_ref[prv, :] += dk2[:BLK, :]
                dv_ref[prv, :] += dv2[:BLK, :]

    cur, prev = _attn_specs()
    col = pl.BlockSpec((T, 256), lambda j, n: (0, j))
    return _call(
        bg, body, name="attn_core_bwd", grid=(N_KV, NT),
        in_specs=[pl.BlockSpec(memory_space=pltpu.SMEM), cur, cur, cur, prev, cur, prev],
        out_specs=[cur, col, col, pl.BlockSpec((None, 8, 128), lambda j, n: (j, 0, 0))],
        out_shape=[SDS((T, D), f32), SDS((T, D), f32), SDS((T, D), f32), SDS((N_KV, 8, 128), f32)],
        compiler_params=_cp(dimension_semantics=("arbitrary", "arbitrary")),
    )(sinks, q, do, k4, k4, v4, v4)


def attn_bwd_q(h, dh, dq, hn, g_mix, wq):
    def body(h_ref, dh_ref, dq_ref, hn_ref, gm_ref, wq_ref, out_ref, dwq_ref, dbq_ref, dgm_ref, aq):
        i = pl.program_id(0)

        @pl.when(i == 0)
        def _():
            aq[...] = jnp.zeros_like(aq)
            dbq_ref[...] = jnp.zeros_like(dbq_ref)
            dgm_ref[...] = jnp.zeros_like(dgm_ref)

        dq_ = dq_ref[...]
        dqb = dq_.astype(bf16)
        aq[...] += _dot_tn(hn_ref[...], dqb)
        dbq_ref[...] += _colsum8(dq_)
        dx, dg = _rms_bwd(h_ref[...], gm_ref[...], _dot_nt(dqb, wq_ref[...]))
        out_ref[...] = dh_ref[...] + dx
        dgm_ref[...] += _colsum8(dg)

        @pl.when(i == NT - 1)
        def _():
            dwq_ref[...] = aq[...].astype(bf16)

    vec = _full((8, D))
    mat = _full((D, D))
    return pl.pallas_call(
        body, name="attn_bwd_q", grid=(NT,),
        in_specs=[_tile()] * 4 + [_full((1, D)), mat],
        out_specs=[_tile(), mat, vec, vec],
        out_shape=[SDS((T, D), f32), SDS((D, D), bf16), SDS((8, D), f32), SDS((8, D), f32)],
        scratch_shapes=[pltpu.VMEM((D, D), f32)],
        compiler_params=_cp(dimension_semantics=("arbitrary",)),
    )(h, dh, dq, hn, g_mix, wq)


def attn_bwd_kv(h, dh, dk4, dv4, kvn, g_kv, wk4, wv4):
    def body(h_ref, dh_ref, dk_ref, dv_ref, kvn_ref, gkv_ref, wk_ref, wv_ref,
             out_ref, dwk_ref, dwv_ref, dbk_ref, dbv_ref, dgkv_ref):
        i = pl.program_id(0)

        @pl.when(i == 0)
        def _():
            for r in (dwk_ref, dwv_ref, dbk_ref, dbv_ref, dgkv_ref):
                r[...] = jnp.zeros_like(r)

        dk_ = dk_ref[...]
        dv_ = dv_ref[...]
        dkb, dvb = dk_.astype(bf16), dv_.astype(bf16)
        dkvn = _dot_nt(dkb, wk_ref[...]) + _dot_nt(dvb, wv_ref[...])
        dwk_ref[...] += _dot_tn(kvn_ref[...], dkb)
        dwv_ref[...] += _dot_tn(kvn_ref[...], dvb)
        dbk_ref[...] += _colsum8(dk_)
        dbv_ref[...] += _colsum8(dv_)
        dx, dg = _rms_bwd(h_ref[...], gkv_ref[...], dkvn)
        out_ref[...] = dh_ref[...] + dx
        dgkv_ref[...] += _colsum8(dg)

    vec = _full((8, D))
    mat = _full((D, D))
    return pl.pallas_call(
        body, name="attn_bwd_kv", grid=(NT,),
        in_specs=[_tile()] * 5 + [_full((1, D)), mat, mat],
        out_specs=[_tile(), mat, mat, vec, vec, vec],
        out_shape=[SDS((T, D), f32), SDS((D, D), f32), SDS((D, D), f32)] + [SDS((8, D), f32)] * 3,
        compiler_params=_cp(dimension_semantics=("arbitrary",)),
    )(h, dh, dk4, dv4, kvn, g_kv, wk4, wv4)


def final_loss(h, g, target):
    def body(h_ref, g_ref, t_ref, loss_ref, dh_ref, dg_ref):
        i = pl.program_id(0)

        @pl.when(i == 0)
        def _():
            loss_ref[...] = jnp.zeros_like(loss_ref)
            dg_ref[...] = jnp.zeros_like(dg_ref)

        h_ = h_ref[...]
        g_ = g_ref[...]
        y, _ = _rms(h_, g_)
        diff = y - t_ref[...]
        per_tok = jnp.mean(diff * diff, axis=-1, keepdims=True)
        tot = 0.5 * jnp.sum(per_tok, axis=0, keepdims=True)
        lane = lax.broadcasted_iota(jnp.int32, (8, 128), 1)
        row = lax.broadcasted_iota(jnp.int32, (8, 128), 0)
        loss_ref[...] += jnp.where((lane == 0) & (row == 0), jnp.broadcast_to(tot, (8, 128)), 0.0)
        dx, dgt = _rms_bwd(h_, g_, diff * (1.0 / D))
        dh_ref[...] = dx
        dg_ref[...] += _colsum8(dgt)

    return pl.pallas_call(
        body, name="final_loss", grid=(NT,), in_specs=[_tile(), _full((1, D)), _tile()],
        out_specs=[_full((8, 128)), _tile(), _full((8, D))],
        out_shape=[SDS((8, 128), f32), SDS((T, D), f32), SDS((8, D), f32)],
        compiler_params=_cp(dimension_semantics=("arbitrary",)),
    )(h, g, target)


def _to_chunked(a):
    return a.reshape(S5_CH, S5_STEPS, a.shape[-1]).transpose(1, 0, 2).reshape(T, a.shape[-1])


def _from_chunked(a):
    return a.reshape(S5_STEPS, S5_CH, a.shape[-1]).transpose(1, 0, 2).reshape(T, a.shape[-1])


def _rep4(w):
    return jnp.broadcast_to(w.reshape(w.shape[0], N_KV, 1, HEAD_DIM), (w.shape[0], N_KV, Q_PER_KV, HEAD_DIM)).reshape(
        w.shape[0], N_KV * Q_PER_KV * HEAD_DIM)


def _fold4(w):
    return w.reshape(w.shape[0], N_KV, Q_PER_KV, HEAD_DIM).sum(axis=2).reshape(w.shape[0], N_KV * HEAD_DIM)


def fwd_bwd(x, target, p, shards, core):
    row = lambda v: v.reshape(1, -1)
    (lam, bm, cm), prep_vjp = jax.vjp(s5_discretize, p["s5_a_re"][0], p["s5_a_im"][0], p["s5_log_dt"][0],
                                      p["s5_b_re"][0], p["s5_b_im"][0], p["s5_c_re"][0], p["s5_c_im"][0])
    bmb, cmb = bm.astype(bf16), cm.astype(bf16)
    g_mix0, g_mix1 = row(p["norm_mix"][0]), row(p["norm_mix"][1])
    g_mlp0, g_mlp1 = row(p["norm_mlp"][0]), row(p["norm_mlp"][1])
    g_kv, g_fin = row(p["norm_kv"]), row(p["norm_final"])
    bq, bo = p["b_q"], p["b_o"]
    bk4, bv4 = _rep4(row(p["b_kv"])[:, :256]), _rep4(row(p["b_kv"])[:, 256:])
    sinks = p["sinks"].reshape(16)

    def reduce_pairs(names, bg):
        return [add_pairs(g, r, core, f"add_pairs_{n}") for n, g, r in zip(names, bg.arrs, bg.result)]

    xp = _to_chunked(x)
    hn0 = s5_pre(xp, g_mix0)
    ga = BgGather([shards["s5_w_glu"], shards["vecs"], shards["w_in0"]])
    ys = s5_core_fwd(hn0, bmb, lam, cmb, bg=[ga])
    wglu, gvec, win0 = ga.result
    d_skip = gvec[:, 0, :128].reshape(1, D)
    bglu = gvec[:, 0, 128:].reshape(1, 2 * D)
    gb = BgGather([shards["w_out0"]])
    y, z, h1 = s5_post(ys, xp, g_mix0, d_skip, wglu, bglu, bg=[gb])
    wout0, = gb.result
    gc = BgGather([shards["w_kv"], shards["w_q"], shards["w_o"], shards["w_in1"]])
    hm0, h2p = mlp_fwd(h1, g_mlp0, win0, wout0, 0, bg=[gc])
    wkv, wq, wo, win1 = gc.result
    wkv, wq, wo = wkv.reshape(D, 512), wq.reshape(D, D), wo.reshape(D, D)
    wk4, wv4 = _rep4(wkv[:, :256]), _rep4(wkv[:, 256:])
    h2 = _from_chunked(h2p)
    kvn, hn1, k4, v4, q = attn_pre(h2, g_kv, g_mix1, wk4, wv4, bk4, bv4, wq, bq)
    gd = BgGather([shards["w_out1"]])
    o = attn_core_fwd(q, k4, v4, sinks, bg=[gd])
    wout1, = gd.result
    h3 = attn_post(h2, o, wo, bo)
    hm1, h4 = mlp_fwd(h3, g_mlp1, win1, wout1, 1)
    loss, dh4, dg_fin = final_loss(h4, g_fin, target)

    big = {}
    dh3, dwin1, dwout1, dg_mlp1 = mlp_bwd(h3, hm1, g_mlp1, dh4, win1, wout1, 1)
    pa = BgPair([dwin1, dwout1])
    do, dwo, dbo = attn_bwd_pre(dh3, o, wo, bg=[pa])
    ca = BgChips(reduce_pairs(["w_in1", "w_out1"], pa))
    dq, dk4, dv4, dsink = attn_core_bwd(q, do, k4, v4, sinks, bg=[ca])
    big["w_in1"], big["w_out1"] = zip(ca.arrs, ca.result)
    dh2, dwq, dbq, dg_mix1 = attn_bwd_q(h2, dh3, dq, hn1, g_mix1, wq)
    dh2, dwk4, dwv4, dbk4, dbv4, dg_kv = attn_bwd_kv(h2, dh2, dk4, dv4, kvn, g_kv, wk4, wv4)
    dwkv = jnp.concatenate([_fold4(dwk4), _fold4(dwv4)], axis=1).astype(bf16)
    pb = BgPair([dwkv.reshape(NDEV, 128, 512), dwq.reshape(NDEV, 128, D), dwo.reshape(NDEV, 128, D)])
    dh2p = _to_chunked(dh2)
    dh1, dwin0, dwout0, dg_mlp0 = mlp_bwd(h1, hm0, g_mlp0, dh2p, win0, wout0, 0, bg=[pb])
    cb = BgChips(reduce_pairs(["w_kv", "w_q", "w_o"], pb))
    pc = BgPair([dwin0, dwout0])
    dy, dwglu, dbglu = s5_post_bwd(dh1, y, z, wglu, bg=[cb, pc])
    big["w_kv"], big["w_q"], big["w_o"] = zip(cb.arrs, cb.result)
    cc = BgChips(reduce_pairs(["w_in0", "w_out0"], pc))
    pd = BgPair([dwglu])
    du, dbm, dcmt, dlam = s5_core_bwd(hn0, dy, bmb, lam, cmb, bg=[cc, pd])
    big["w_in0"], big["w_out0"] = zip(cc.arrs, cc.result)
    cd = BgChips(reduce_pairs(["s5_w_glu"], pd))
    dxp, dg_mix0, dd = s5_pre_bwd(xp, g_mix0, du, dy, d_skip, dh1, bg=[cd])
    big["s5_w_glu"], = zip(cd.arrs, cd.result)
    grad_x = _from_chunked(dxp)
    da_re, da_im, dlog_dt, db_re, db_im, dc_re, dc_im = prep_vjp((dlam, dbm, dcmt.transpose(0, 2, 1)))

    def lanes(v_):
        v_ = v_.reshape(1, -1)
        return jnp.pad(v_, ((0, 0), (0, D - v_.shape[1])))

    small = jnp.concatenate([
        dg_mix0[0:1], dg_mix1[0:1], dg_mlp0[0:1], dg_mlp1[0:1], dg_kv[0:1], dg_fin[0:1], dd[0:1], dbq[0:1], dbo[0:1],
        dbglu[0:1].reshape(2, D), lanes(jnp.concatenate([_fold4(dbk4[0:1]), _fold4(dbv4[0:1])], axis=1)),
        lanes(dsink[:, 0, :Q_PER_KV]), lanes(dlog_dt), lanes(loss[0:1, 0:1]), jnp.zeros((1, D), f32),
        da_re.reshape(4, D), da_im.reshape(4, D), db_re.reshape(64, D), db_im.reshape(64, D),
        dc_re.reshape(64, D), dc_im.reshape(64, D)], axis=0)
    return loss, grad_x, small, big


_ANY = pl.BlockSpec(memory_space=pl.ANY)


def _pos():
    return lax.axis_index("x"), lax.axis_index("y"), lax.axis_index("c")


def _other_chips(x, y):
    return [(1 - x, y), (x, 1 - y), (1 - x, 1 - y)]


def all_gather(arrs):
    n = len(arrs)

    def body(*refs):
        ins, outs = refs[:n], refs[n:2 * n]
        send_sems, recv_sems, local_sems = refs[2 * n:]
        x, y, c = _pos()
        me, sib = (x, y, c), (x, y, 1 - c)
        chips = _other_chips(x, y)

        def copy(a, k, block, to, src=None):
            dst = outs[a].at[4 * block[0] + 2 * block[1] + block[2]]
            return pltpu.make_async_remote_copy(
                src_ref=dst if src is None else src, dst_ref=dst, send_sem=send_sems.at[a, k],
                recv_sem=recv_sems.at[a, k], device_id=to, device_id_type=MESH)

        mine = [pltpu.make_async_copy(ins[a], outs[a].at[4 * x + 2 * y + c], local_sems.at[a]) for a in range(n)]
        for cp in mine:
            cp.start()
        first = []
        for a in range(n):
            first.append(copy(a, 0, me, sib, src=ins[a]))
            first += [copy(a, 1 + j, me, (*chip, c), src=ins[a]) for j, chip in enumerate(chips)]
        for cp in first:
            cp.start()
        passed = []
        for j, chip in enumerate(chips):
            for a in range(n):
                copy(a, 1 + j, (*chip, c), me).wait_recv()
                cp = copy(a, 4 + j, (*chip, c), sib)
                cp.start()
                passed.append(cp)
        for a in range(n):
            copy(a, 0, sib, me).wait_recv()
            for j, chip in enumerate(chips):
                copy(a, 4 + j, (*chip, 1 - c), me).wait_recv()
        for cp in first + passed:
            cp.wait_send()
        for cp in mine:
            cp.wait()

    return pl.pallas_call(
        body, name="all_gather", in_specs=[_ANY] * n, out_specs=[_ANY] * n,
        out_shape=[SDS((NDEV,) + a.shape, a.dtype) for a in arrs],
        scratch_shapes=[pltpu.SemaphoreType.DMA((n, 7)), pltpu.SemaphoreType.DMA((n, 7)),
                        pltpu.SemaphoreType.DMA((n,))],
    )(*arrs)


def rs_pair(grads):
    n = len(grads)

    def body(*refs):
        ins, outs = refs[:n], refs[n:2 * n]
        send_sems, recv_sems = refs[2 * n:]
        x, y, c = _pos()
        cps = []
        for a in range(n):
            for k in range(4):
                cps.append(pltpu.make_async_remote_copy(
                    src_ref=ins[a].at[2 * k + 1 - c], dst_ref=outs[a].at[k], send_sem=send_sems.at[a, k],
                    recv_sem=recv_sems.at[a, k], device_id=(x, y, 1 - c), device_id_type=MESH))
        for cp in cps:
            cp.start()
        for cp in cps:
            cp.wait_recv()
        for cp in cps:
            cp.wait_send()

    return pl.pallas_call(
        body, name="rs_pair", in_specs=[_ANY] * n, out_specs=[_ANY] * n,
        out_shape=[SDS((4,) + g.shape[1:], g.dtype) for g in grads],
        scratch_shapes=[pltpu.SemaphoreType.DMA((n, 4)), pltpu.SemaphoreType.DMA((n, 4))],
    )(*grads)


def rs_chips(parts):
    n = len(parts)

    def body(*refs):
        ins, outs = refs[:n], refs[n:2 * n]
        send_sems, recv_sems = refs[2 * n:]
        x, y, c = _pos()
        cps = []
        for a in range(n):
            for r, (px, py) in enumerate(_other_chips(x, y)):
                cps.append(pltpu.make_async_remote_copy(
                    src_ref=ins[a].at[2 * px + py], dst_ref=outs[a].at[r], send_sem=send_sems.at[a, r],
                    recv_sem=recv_sems.at[a, r], device_id=(px, py, c), device_id_type=MESH))
        for cp in cps:
            cp.start()
        for cp in cps:
            cp.wait_recv()
        for cp in cps:
            cp.wait_send()

    return pl.pallas_call(
        body, name="rs_chips", in_specs=[_ANY] * n, out_specs=[_ANY] * n,
        out_shape=[SDS((3,) + g.shape[1:], g.dtype) for g in parts],
        scratch_shapes=[pltpu.SemaphoreType.DMA((n, 3)), pltpu.SemaphoreType.DMA((n, 3))],
    )(*parts)


def _row_tile(r, c):
    return min(r, max(8, (256 * 1024) // c))


def add_pairs(g, r1, core, name):
    _, R, C = g.shape
    tr = _row_tile(R, C)

    def body(core_ref, g_ref, r_ref, o_ref):
        o_ref[...] = (g_ref[...].astype(f32) + r_ref[...].astype(f32)).astype(bf16)

    return pl.pallas_call(
        body, name=name, out_shape=SDS((4, R, C), bf16),
        grid_spec=pltpu.PrefetchScalarGridSpec(
            num_scalar_prefetch=1, grid=(4, R // tr),
            in_specs=[pl.BlockSpec((None, tr, C), lambda k, i, core: (2 * k + core[0], i, 0)),
                      pl.BlockSpec((None, tr, C), lambda k, i, core: (k, i, 0))],
            out_specs=pl.BlockSpec((None, tr, C), lambda k, i, core: (k, i, 0))),
        compiler_params=_cp(dimension_semantics=("arbitrary", "arbitrary")),
    )(core, g, r1)


def _adamw(w, g, m, v):
    m = ADAM_B1 * m + (1.0 - ADAM_B1) * g
    v = ADAM_B2 * v + (1.0 - ADAM_B2) * (g * g)
    m_hat = m / (1.0 - ADAM_B1 ** ADAM_STEP)
    v_hat = v / (1.0 - ADAM_B2 ** ADAM_STEP)
    delta = -ADAM_LR * (m_hat / (jnp.sqrt(v_hat) + ADAM_EPS) + ADAM_WD * w)
    return delta, m, v


def adam_big(w, m, v, part, r2, chip, name, layer=0, prev=None):
    L, R, C = w.shape
    tr = _row_tile(R, C)

    def body(chip_ref, w_ref, m_ref, v_ref, p_ref, r_ref, *rest):
        g_out, d_out, m_out, v_out = rest[-4:]
        g = p_ref[...].astype(f32) + r_ref[0].astype(f32) + r_ref[1].astype(f32) + r_ref[2].astype(f32)
        d, m_, v_ = _adamw(w_ref[...], g, m_ref[...], v_ref[...])
        g_out[...] = g
        d_out[...] = d
        m_out[...] = m_
        v_out[...] = v_

    blk = pl.BlockSpec((None, tr, C), lambda i, chip: (layer, i, 0))
    extra = [] if prev is None else list(prev)
    return pl.pallas_call(
        body, name=name, out_shape=[SDS((L, R, C), f32)] * 4,
        grid_spec=pltpu.PrefetchScalarGridSpec(
            num_scalar_prefetch=1, grid=(R // tr,),
            in_specs=[blk, blk, blk,
                      pl.BlockSpec((None, tr, C), lambda i, chip: (chip[0], i, 0)),
                      pl.BlockSpec((3, tr, C), lambda i, chip: (0, i, 0))] + [_ANY] * len(extra),
            out_specs=[blk] * 4),
        input_output_aliases={6 + k: k for k in range(len(extra))},
        compiler_params=_cp(dimension_semantics=("arbitrary",)),
    )(chip, w, m, v, part, r2, *extra)


def allreduce_small(buf):
    shp = buf.shape
    half = (shp[0] // 16) * 8
    parts = (pl.ds(0, half), pl.ds(half, shp[0] - half))

    def body(in_ref, out_ref, acc1, acc2, r0, r1, r2, send_sems, recv_sems):
        x, y, c = _pos()
        across = [(1 - x, y, c), (x, 1 - y, c)]

        def exchange(src, rcv, dst, copies):
            cps = [pltpu.make_async_remote_copy(
                src_ref=src.at[rows], dst_ref=rcv.at[rows], send_sem=send_sems.at[k], recv_sem=recv_sems.at[k],
                device_id=peer, device_id_type=MESH) for k, rows, peer in copies]
            for cp in cps:
                cp.start()
            for cp in cps:
                cp.wait()
            dst[...] = src[...] + rcv[...]

        exchange(in_ref, r0, acc1, [(0, pl.ds(0, shp[0]), (x, y, 1 - c))])
        exchange(acc1, r1, acc2, [(1, parts[0], across[0]), (2, parts[1], across[1])])
        exchange(acc2, r2, out_ref, [(3, parts[0], across[1]), (4, parts[1], across[0])])

    return pl.pallas_call(
        body, name="allreduce_small", out_shape=SDS(shp, f32),
        scratch_shapes=[pltpu.VMEM(shp, f32)] * 5 + [pltpu.SemaphoreType.DMA((5,)), pltpu.SemaphoreType.DMA((5,))],
    )(buf)


SMALL_ROWS = {'norm_mix': (0, 2, D), 'norm_mlp': (2, 2, D), 'norm_kv': (4, 1, D), 'norm_final': (5, 1, D),
              's5_d': (6, 1, D), 'b_q': (7, 1, D), 'b_o': (8, 1, D), 's5_b_glu': (9, 2, D), 'b_kv': (11, 1, 512),
              'sinks': (12, 1, 16), 's5_log_dt': (13, 1, 64), 's5_a_re': (16, 4, D), 's5_a_im': (20, 4, D),
              's5_b_re': (24, 64, D), 's5_b_im': (88, 64, D), 's5_c_re': (152, 64, D), 's5_c_im': (216, 64, D)}
LOSS_ROW = 14
ROW_PARAMS = ['norm_mix', 'norm_mlp', 'norm_kv', 'norm_final', 'b_q', 'b_o', 'b_kv', 'sinks', 's5_log_dt']
SHARD_PARAMS = ['s5_d', 's5_b_glu']
S5_PARAMS = ['s5_a_re', 's5_a_im', 's5_b_re', 's5_b_im', 's5_c_re', 's5_c_im']


def adam_small(dev, gsum, s5_grads, w, m, v):
    names = ROW_PARAMS + SHARD_PARAMS + S5_PARAMS
    n_g = len(ROW_PARAMS) + len(SHARD_PARAMS)

    def body(dev_ref, gs_ref, *refs):
        pos = [0]

        def take(k):
            r = refs[pos[0]:pos[0] + k]
            pos[0] += k
            return r

        g5 = take(len(S5_PARAMS))
        wr, mr, vr = take(len(names)), take(len(names)), take(len(names))
        g_out = take(n_g)
        d_out, m_out, v_out = take(len(names)), take(len(names)), take(len(names))
        dv = dev_ref[0]
        for i, n in enumerate(names):
            if n in S5_PARAMS:
                g = g5[S5_PARAMS.index(n)][...]
            elif n in SHARD_PARAMS:
                r0, _, _ = SMALL_ROWS[n]
                ln = wr[i].shape[1]
                g = jnp.zeros((1, ln), f32)
                for k in range(NDEV):
                    off = k * ln
                    piece = gs_ref[r0 + off // D:r0 + off // D + 1, off % D:off % D + ln]
                    g = g + jnp.where(dv == k, piece, 0.0)
                g_out[i][...] = g
            else:
                r0, nr, nl = SMALL_ROWS[n]
                g = gs_ref[r0:r0 + nr, 0:nl]
                g_out[i][...] = g
            d, m_, v_ = _adamw(wr[i][...], g, mr[i][...], vr[i][...])
            d_out[i][...] = d
            m_out[i][...] = m_
            v_out[i][...] = v_

    vm = pl.BlockSpec(memory_space=pltpu.VMEM)
    ins = [s5_grads[n] for n in S5_PARAMS] + [d[n] for d in (w, m, v) for n in names]
    shapes = [SDS(w[n].shape, f32) for n in names]
    res = pl.pallas_call(
        body, name="adam_small", in_specs=[pl.BlockSpec(memory_space=pltpu.SMEM)] + [vm] * (1 + len(ins)),
        out_specs=[vm] * (n_g + 3 * len(names)), out_shape=shapes[:n_g] + shapes * 3,
        compiler_params=_cp(),
    )(dev, gsum, *ins)
    g_o = dict(zip(names[:n_g], res[:n_g]))
    rest = res[n_g:]
    k = len(names)
    return g_o, dict(zip(names, rest[:k])), dict(zip(names, rest[k:2 * k])), dict(zip(names, rest[2 * k:]))


WEIGHTS = ['norm_mix', 'norm_mlp', 'norm_kv', 'norm_final', 's5_a_re', 's5_a_im', 's5_log_dt', 's5_b_re', 's5_b_im',
           's5_c_re', 's5_c_im', 's5_d', 's5_w_glu', 's5_b_glu', 'w_kv', 'b_kv', 'w_q', 'b_q', 'sinks', 'w_o', 'b_o',
           'w_mlp_in', 'w_mlp_out']
BIG = ['s5_w_glu', 'w_kv', 'w_q', 'w_o', 'w_mlp_in', 'w_mlp_out']
BIG_2D = {'s5_w_glu': (D, 256), 'w_kv': (128, 512), 'w_q': (128, D), 'w_o': (128, D), 'w_mlp_in': (2 * D, 512),
          'w_mlp_out': (2 * 512, D)}
SHARDED_SMALL = {'s5_d': D, 's5_b_glu': 2 * D}
SMALL = [n for n in WEIGHTS if n not in BIG]
SMALL_SIZE = {'norm_mix': 2 * D, 'norm_mlp': 2 * D, 'norm_kv': D, 'norm_final': D, 's5_a_re': 4096, 's5_a_im': 4096,
              's5_log_dt': 64, 's5_b_re': 65536, 's5_b_im': 65536, 's5_c_re': 65536, 's5_c_im': 65536, 's5_d': D,
              's5_b_glu': 2 * D, 'b_kv': 512, 'b_q': D, 'sinks': 16, 'b_o': D}


def _pack(vals):
    parts = []
    for n in SMALL:
        v = vals[n].reshape(-1).astype(f32)
        parts.append(jnp.pad(v, (0, (-v.shape[0]) % 128)))
    flat = jnp.concatenate(parts)
    flat = jnp.pad(flat, (0, (-flat.shape[0]) % 1024))
    return flat.reshape(-1, 128)


def _unpack(buf):
    flat = buf.reshape(-1)
    out, off = {}, 0
    for n in SMALL:
        sz = SMALL_SIZE[n]
        out[n] = flat[off:off + sz]
        off += sz + (-sz) % 128
    return out


def kernel(x, norm_mix, norm_mlp, norm_kv, norm_final, s5_a_re, s5_a_im, s5_log_dt, s5_b_re, s5_b_im, s5_c_re, s5_c_im, s5_d, s5_w_glu, s5_b_glu, w_kv, b_kv, w_q, b_q, sinks, w_o, b_o, w_mlp_in, w_mlp_out, loss_target, m_norm_mix, m_norm_mlp, m_norm_kv, m_norm_final, m_s5_a_re, m_s5_a_im, m_s5_log_dt, m_s5_b_re, m_s5_b_im, m_s5_c_re, m_s5_c_im, m_s5_d, m_s5_w_glu, m_s5_b_glu, m_w_kv, m_b_kv, m_w_q, m_b_q, m_sinks, m_w_o, m_b_o, m_w_mlp_in, m_w_mlp_out, v_norm_mix, v_norm_mlp, v_norm_kv, v_norm_final, v_s5_a_re, v_s5_a_im, v_s5_log_dt, v_s5_b_re, v_s5_b_im, v_s5_c_re, v_s5_c_im, v_s5_d, v_s5_w_glu, v_s5_b_glu, v_w_kv, v_b_kv, v_w_q, v_b_q, v_sinks, v_w_o, v_b_o, v_w_mlp_in, v_w_mlp_out):
    w = dict(norm_mix=norm_mix, norm_mlp=norm_mlp, norm_kv=norm_kv, norm_final=norm_final, s5_a_re=s5_a_re,
             s5_a_im=s5_a_im, s5_log_dt=s5_log_dt, s5_b_re=s5_b_re, s5_b_im=s5_b_im, s5_c_re=s5_c_re, s5_c_im=s5_c_im,
             s5_d=s5_d, s5_w_glu=s5_w_glu, s5_b_glu=s5_b_glu, w_kv=w_kv, b_kv=b_kv, w_q=w_q, b_q=b_q, sinks=sinks,
             w_o=w_o, b_o=b_o, w_mlp_in=w_mlp_in, w_mlp_out=w_mlp_out)
    m = dict(norm_mix=m_norm_mix, norm_mlp=m_norm_mlp, norm_kv=m_norm_kv, norm_final=m_norm_final, s5_a_re=m_s5_a_re,
             s5_a_im=m_s5_a_im, s5_log_dt=m_s5_log_dt, s5_b_re=m_s5_b_re, s5_b_im=m_s5_b_im, s5_c_re=m_s5_c_re,
             s5_c_im=m_s5_c_im, s5_d=m_s5_d, s5_w_glu=m_s5_w_glu, s5_b_glu=m_s5_b_glu, w_kv=m_w_kv, b_kv=m_b_kv,
             w_q=m_w_q, b_q=m_b_q, sinks=m_sinks, w_o=m_w_o, b_o=m_b_o, w_mlp_in=m_w_mlp_in, w_mlp_out=m_w_mlp_out)
    v = dict(norm_mix=v_norm_mix, norm_mlp=v_norm_mlp, norm_kv=v_norm_kv, norm_final=v_norm_final, s5_a_re=v_s5_a_re,
             s5_a_im=v_s5_a_im, s5_log_dt=v_s5_log_dt, s5_b_re=v_s5_b_re, s5_b_im=v_s5_b_im, s5_c_re=v_s5_c_re,
             s5_c_im=v_s5_c_im, s5_d=v_s5_d, s5_w_glu=v_s5_w_glu, s5_b_glu=v_s5_b_glu, w_kv=v_w_kv, b_kv=v_b_kv,
             w_q=v_w_q, b_q=v_b_q, sinks=v_sinks, w_o=v_w_o, b_o=v_b_o, w_mlp_in=v_w_mlp_in, w_mlp_out=v_w_mlp_out)
    xi, yi, ci = _pos()
    dev = 4 * xi + 2 * yi + ci
    core = ci.reshape(1).astype(jnp.int32)
    chip = (2 * xi + yi).reshape(1).astype(jnp.int32)

    shards = {
        "s5_w_glu": s5_w_glu[0].astype(bf16), "w_kv": w_kv.astype(bf16), "w_q": w_q[0].astype(bf16),
        "w_o": w_o[0].astype(bf16), "w_in0": w_mlp_in[0].astype(bf16), "w_in1": w_mlp_in[1].astype(bf16),
        "w_out0": w_mlp_out[0].astype(bf16), "w_out1": w_mlp_out[1].astype(bf16),
        "vecs": jnp.broadcast_to(jnp.concatenate([s5_d, s5_b_glu], axis=1), (8, 384)),
    }
    _, grad_x, grads, big = fwd_bwd(x[0], loss_target[0], {n: w[n] for n in SMALL}, shards, core)

    out_g, out_d, out_m, out_v = {}, {}, {}, {}
    for n in ("s5_w_glu", "w_kv", "w_q", "w_o"):
        shp = w[n].shape
        r3 = (1,) + BIG_2D[n]
        res = adam_big(w[n].reshape(r3), m[n].reshape(r3), v[n].reshape(r3), *big[n], chip, f"adam_{n}")
        out_g[n], out_d[n], out_m[n], out_v[n] = [r.reshape(shp) for r in res]
    for n, k in (("w_mlp_in", "w_in"), ("w_mlp_out", "w_out")):
        res = adam_big(w[n], m[n], v[n], *big[k + "1"], chip, f"adam_{k}1", layer=1)
        res = adam_big(w[n], m[n], v[n], *big[k + "0"], chip, f"adam_{k}0", layer=0, prev=res)
        out_g[n], out_d[n], out_m[n], out_v[n] = res

    gsum = allreduce_small(grads)
    loss = gsum[LOSS_ROW, 0]
    for n in S5_PARAMS:
        r0, nr, _ = SMALL_ROWS[n]
        out_g[n] = gsum[r0:r0 + nr].reshape(w[n].shape)
    as2d = lambda d: {n: (d[n].reshape(1, -1) if d[n].ndim == 1 else d[n]) for n in SMALL}
    g_s, d_s, m_s, v_s = adam_small(dev.reshape(1).astype(jnp.int32), gsum, out_g, as2d(w), as2d(m), as2d(v))
    for src, dst in ((g_s, out_g), (d_s, out_d), (m_s, out_m), (v_s, out_v)):
        for n, val in src.items():
            dst[n] = val.reshape(w[n].shape)

    return (loss, grad_x[None], *[out_g[n] for n in WEIGHTS], *[out_d[n] for n in WEIGHTS],
            *[out_m[n] for n in WEIGHTS], *[out_v[n] for n in WEIGHTS])
```

```python
import functools
import math

import jax
import jax.numpy as jnp
from jax import lax
from jax.experimental import pallas as pl
from jax.experimental.pallas import tpu as pltpu

f32 = jnp.float32
bf16 = jnp.bfloat16
SDS = jax.ShapeDtypeStruct

T = 2048
D = 1024
NDEV = 8
NORM_EPS = 1e-5
S5_G, S5_C, S5_P = 64, 16, 64
S5_SUB = 8
S5_CH = 8
S5_STEPS = T // S5_CH
DT_MIN_LAMBDA = -1e-4
HEAD_DIM = 64
N_KV = 4
Q_PER_KV = 4
BLK = 128
D_FF_SHARD = 512
ADAM_LR, ADAM_B1, ADAM_B2, ADAM_EPS, ADAM_WD, ADAM_STEP = 0.001, 0.9, 0.999, 1e-08, 0.01, 10
VMEM_LIMIT = 56 * 1024 * 1024
MESH = pl.DeviceIdType.MESH


def _cp(**kw):
    return pltpu.CompilerParams(vmem_limit_bytes=VMEM_LIMIT, **kw)


def _dot(a, b):
    return jnp.dot(a, b, preferred_element_type=f32)


def _dot_nt(a, b):
    return lax.dot_general(a, b, (((1,), (1,)), ((), ())), preferred_element_type=f32)


def _dot_tn(a, b):
    return lax.dot_general(a, b, (((0,), (0,)), ((), ())), preferred_element_type=f32)


def _rms(x, g):
    r = lax.rsqrt(jnp.mean(x * x, axis=-1, keepdims=True) + NORM_EPS)
    return x * r * g, r


def _rms_bwd(x, g, dy):
    r = lax.rsqrt(jnp.mean(x * x, axis=-1, keepdims=True) + NORM_EPS)
    u = dy * g
    dx = r * u - (r * r * r) * x * jnp.mean(u * x, axis=-1, keepdims=True)
    return dx, dy * x * r


def _colsum8(v):
    s = jnp.sum(v, axis=0, keepdims=True)
    row = lax.broadcasted_iota(jnp.int32, (8, v.shape[1]), 0)
    return jnp.where(row == 0, jnp.broadcast_to(s, (8, v.shape[1])), 0.0)


def _full(shape):
    nd = len(shape)
    return pl.BlockSpec(shape, lambda *_: (0,) * nd, pipeline_mode=pl.Buffered(1))


_ANY = pl.BlockSpec(memory_space=pl.ANY)


def _pos():
    return lax.axis_index("x"), lax.axis_index("y"), lax.axis_index("c")


def _other_chips(x, y):
    return [(1 - x, y), (x, 1 - y), (1 - x, 1 - y)]


class BgGather:
    def __init__(self, arrs):
        n = len(arrs)
        self.arrs = list(arrs)
        self.out_shape = [SDS((NDEV,) + a.shape, a.dtype) for a in arrs]
        self.scratch = [pltpu.SemaphoreType.DMA((n, 7)), pltpu.SemaphoreType.DMA((n, 7)),
                        pltpu.SemaphoreType.DMA((n,))]
        self.has_mid = True
        self.result = None

    def _copy(self, ins, outs, sems, a, k, block, to, own=False):
        dst = outs[a].at[4 * block[0] + 2 * block[1] + block[2]]
        return pltpu.make_async_remote_copy(
            src_ref=ins[a] if own else dst, dst_ref=dst, send_sem=sems[0].at[a, k], recv_sem=sems[1].at[a, k],
            device_id=to, device_id_type=MESH)

    def _mine(self, ins, outs, sems):
        x, y, c = _pos()
        return [pltpu.make_async_copy(ins[a], outs[a].at[4 * x + 2 * y + c], sems[2].at[a])
                for a in range(len(self.arrs))]

    def _first(self, ins, outs, sems):
        x, y, c = _pos()
        me = (x, y, c)
        cps = []
        for a in range(len(self.arrs)):
            cps.append(self._copy(ins, outs, sems, a, 0, me, (x, y, 1 - c), own=True))
            cps += [self._copy(ins, outs, sems, a, 1 + j, me, (*chip, c), own=True)
                    for j, chip in enumerate(_other_chips(x, y))]
        return cps

    def _passed(self, ins, outs, sems):
        x, y, c = _pos()
        return [self._copy(ins, outs, sems, a, 4 + j, (*chip, c), (x, y, 1 - c))
                for j, chip in enumerate(_other_chips(x, y)) for a in range(len(self.arrs))]

    def start(self, ins, outs, sems):
        for cp in self._mine(ins, outs, sems) + self._first(ins, outs, sems):
            cp.start()

    def mid(self, ins, outs, sems):
        x, y, c = _pos()
        for j, chip in enumerate(_other_chips(x, y)):
            for a in range(len(self.arrs)):
                self._copy(ins, outs, sems, a, 1 + j, (*chip, c), (x, y, c)).wait_recv()
                self._copy(ins, outs, sems, a, 4 + j, (*chip, c), (x, y, 1 - c)).start()

    def finish(self, ins, outs, sems):
        x, y, c = _pos()
        for a in range(len(self.arrs)):
            self._copy(ins, outs, sems, a, 0, (x, y, 1 - c), (x, y, c)).wait_recv()
            for j, chip in enumerate(_other_chips(x, y)):
                self._copy(ins, outs, sems, a, 4 + j, (*chip, 1 - c), (x, y, c)).wait_recv()
        for cp in self._first(ins, outs, sems) + self._passed(ins, outs, sems):
            cp.wait_send()
        for cp in self._mine(ins, outs, sems):
            cp.wait()


class BgPair:
    def __init__(self, arrs):
        n = len(arrs)
        self.arrs = list(arrs)
        self.out_shape = [SDS((4,) + a.shape[1:], a.dtype) for a in arrs]
        self.scratch = [pltpu.SemaphoreType.DMA((n, 4)), pltpu.SemaphoreType.DMA((n, 4))]
        self.has_mid = False
        self.result = None

    def _copies(self, ins, outs, sems):
        x, y, c = _pos()
        return [pltpu.make_async_remote_copy(
            src_ref=ins[a].at[2 * k + 1 - c], dst_ref=outs[a].at[k], send_sem=sems[0].at[a, k],
            recv_sem=sems[1].at[a, k], device_id=(x, y, 1 - c), device_id_type=MESH)
            for a in range(len(self.arrs)) for k in range(4)]

    def start(self, ins, outs, sems):
        for cp in self._copies(ins, outs, sems):
            cp.start()

    def finish(self, ins, outs, sems):
        cps = self._copies(ins, outs, sems)
        for cp in cps:
            cp.wait_recv()
        for cp in cps:
            cp.wait_send()


class BgChips(BgPair):
    def __init__(self, arrs):
        n = len(arrs)
        self.arrs = list(arrs)
        self.out_shape = [SDS((3,) + a.shape[1:], a.dtype) for a in arrs]
        self.scratch = [pltpu.SemaphoreType.DMA((n, 3)), pltpu.SemaphoreType.DMA((n, 3))]
        self.has_mid = False
        self.result = None

    def _copies(self, ins, outs, sems):
        x, y, c = _pos()
        return [pltpu.make_async_remote_copy(
            src_ref=ins[a].at[2 * px + py], dst_ref=outs[a].at[r], send_sem=sems[0].at[a, r],
            recv_sem=sems[1].at[a, r], device_id=(px, py, c), device_id_type=MESH)
            for a in range(len(self.arrs)) for r, (px, py) in enumerate(_other_chips(x, y))]


def _call(bgs, body, *, name, grid, in_specs, out_specs, out_shape, scratch_shapes=(), compiler_params=None):
    single = not isinstance(out_shape, (list, tuple))
    out_specs_l = [out_specs] if single else list(out_specs)
    out_shape_l = [out_shape] if single else list(out_shape)
    bgs = [b for b in (bgs or []) if b is not None]
    n_in, n_out, n_sc = len(in_specs), len(out_shape_l), len(scratch_shapes)
    nsteps = math.prod(grid)

    def full(*refs):
        pos = [0]

        def take(k):
            r = refs[pos[0]:pos[0] + k]
            pos[0] += k
            return r

        ins = take(n_in)
        b_ins = [take(len(b.arrs)) for b in bgs]
        outs = take(n_out)
        b_outs = [take(len(b.out_shape)) for b in bgs]
        sc = take(n_sc)
        b_sc = [take(len(b.scratch)) for b in bgs]
        if bgs:
            step = pl.program_id(0)
            for d in range(1, len(grid)):
                step = step * grid[d] + pl.program_id(d)

            @pl.when(step == 0)
            def _():
                for b, i_, o_, s_ in zip(bgs, b_ins, b_outs, b_sc):
                    b.start(i_, o_, s_)

        body(*ins, *outs, *sc)
        if bgs:
            for b, i_, o_, s_ in zip(bgs, b_ins, b_outs, b_sc):
                if b.has_mid:
                    @pl.when(step == max(0, (3 * nsteps) // 4 - 1))
                    def _():
                        b.mid(i_, o_, s_)

            @pl.when(step == nsteps - 1)
            def _():
                for b, i_, o_, s_ in zip(bgs, b_ins, b_outs, b_sc):
                    b.finish(i_, o_, s_)

    def run(*args):
        res = pl.pallas_call(
            full, name=name, grid=grid,
            in_specs=list(in_specs) + [_ANY] * sum(len(b.arrs) for b in bgs),
            out_specs=out_specs_l + [_ANY] * sum(len(b.out_shape) for b in bgs),
            out_shape=out_shape_l + [s for b in bgs for s in b.out_shape],
            scratch_shapes=list(scratch_shapes) + [s for b in bgs for s in b.scratch],
            compiler_params=compiler_params,
        )(*args, *[a for b in bgs for a in b.arrs])
        rest = list(res[n_out:])
        for b in bgs:
            b.result, rest = rest[:len(b.out_shape)], rest[len(b.out_shape):]
        return res[0] if single else list(res[:n_out])

    return run


def s5_discretize(a_re, a_im, log_dt, b_re, b_im, c_re, c_im):
    lam_r = jnp.minimum(a_re, DT_MIN_LAMBDA)
    lam_i = a_im
    dt = jnp.exp(log_dt)[:, None]
    e = jnp.exp(lam_r * dt)
    lbr = e * jnp.cos(lam_i * dt)
    lbi = e * jnp.sin(lam_i * dt)
    den = lam_r * lam_r + lam_i * lam_i
    cf_r = ((lbr - 1.0) * lam_r + lbi * lam_i) / den
    cf_i = (lbi * lam_r - (lbr - 1.0) * lam_i) / den
    bb_r = cf_r[:, :, None] * b_re - cf_i[:, :, None] * b_im
    bb_i = cf_r[:, :, None] * b_im + cf_i[:, :, None] * b_re
    eye = jnp.eye(8, dtype=f32)

    def blk_b(m):
        return jnp.einsum('bgpc,gh->bgchp', m.reshape(8, 8, S5_P, S5_C), eye).reshape(8, 128, 512)

    def blk_c(m):
        return jnp.einsum('bgcp,gh->bgphc', m.reshape(8, 8, S5_C, S5_P), eye).reshape(8, 512, 128)

    bm = jnp.concatenate([blk_b(bb_r), blk_b(bb_i)], axis=-1)
    cm = jnp.concatenate([blk_c(c_re), -blk_c(c_im)], axis=1)
    lam = jnp.stack([lbr.reshape(8, 512), lbi.reshape(8, 512)], axis=1)
    lam = jnp.broadcast_to(lam[:, :, None, :], (8, 2, 8, 512))
    return lam, bm, cm


def _cmul(ar, ai, br, bi):
    return ar * br - ai * bi, ar * bi + ai * br


def _shift_rows(v, k, up):
    row = lax.broadcasted_iota(jnp.int32, v.shape, 0)
    if up:
        return jnp.where(row < 8 - k, pltpu.roll(v, 8 - k, 0), 0.0)
    return jnp.where(row >= k, pltpu.roll(v, k, 0), 0.0)


def _chunk_scan(S, lr, li, reverse, aux=None):
    if reverse:
        li = -li
    z = jnp.zeros((8, 512), f32)

    def idx(i):
        return (S5_STEPS - 1 - i) if reverse else i

    def rec(xr, xi, row):
        br = S[row, 0:512]
        bi = S[row, 512:1024]
        return lr * xr - li * xi + br, lr * xi + li * xr + bi

    def step1(i, c):
        row = pl.ds(pl.multiple_of(idx(i) * 8, 8), 8)
        return rec(c[0], c[1], row)

    er, ei = lax.fori_loop(0, S5_STEPS, step1, (z, z), unroll=8)
    ar, ai = lr, li
    for _ in range(8):
        ar, ai = _cmul(ar, ai, ar, ai)
    cr, ci = _shift_rows(er, 1, reverse), _shift_rows(ei, 1, reverse)
    for k in (1, 2, 4):
        sr, si = _shift_rows(cr, k, reverse), _shift_rows(ci, k, reverse)
        pr, pi_ = _cmul(ar, ai, sr, si)
        cr, ci = cr + pr, ci + pi_
        ar, ai = _cmul(ar, ai, ar, ai)

    if aux is None:
        def step2(i, c):
            row = pl.ds(pl.multiple_of(idx(i) * 8, 8), 8)
            xr, xi = rec(c[0], c[1], row)
            S[row, 0:512] = xr
            S[row, 512:1024] = xi
            return xr, xi

        lax.fori_loop(0, S5_STEPS, step2, (cr, ci), unroll=8)
        return None

    def step2(i, c):
        gr0, gi0, dr, di = c
        s = idx(i)
        row = pl.ds(pl.multiple_of(s * 8, 8), 8)
        gr, gi = rec(gr0, gi0, row)
        S[row, 0:512] = gr
        S[row, 512:1024] = gi
        prow = pl.ds(pl.multiple_of(jnp.maximum(s - 1, 0) * 8, 8), 8)
        xr = aux[prow, 0:512]
        xi = aux[prow, 512:1024]
        dr = dr + gr * xr + gi * xi
        di = di + gi * xr - gr * xi
        return gr, gi, dr, di

    gr, gi, dr, di = lax.fori_loop(0, S5_STEPS - 1, step2, (cr, ci, z, z), unroll=8)
    row0 = pl.ds(0, 8)
    gr, gi = rec(gr, gi, row0)
    S[row0, 0:512] = gr
    S[row0, 512:1024] = gi
    last = pl.ds((S5_STEPS - 1) * 8, 8)
    xr = _shift_rows(aux[last, 0:512], 1, False)
    xi = _shift_rows(aux[last, 512:1024], 1, False)
    dr = dr + gr * xr + gi * xi
    di = di + gi * xr - gr * xi
    return dr, di


_ROWS = 256


def _row_loop(fn):
    def body(r, c):
        fn(pl.ds(pl.multiple_of(r * _ROWS, _ROWS), _ROWS))
        return c
    lax.fori_loop(0, T // _ROWS, body, 0)


def s5_core_fwd(hn, bm, lam, cm, bg=()):
    def body(u_ref, b_ref, lam_ref, c_ref, ys_ref, S):
        def bu(rows):
            S[rows, :] = _dot(u_ref[rows, :], b_ref[...])
        _row_loop(bu)
        _chunk_scan(S, lam_ref[0], lam_ref[1], False)

        def ys(rows):
            ys_ref[rows, :] = _dot(S[rows, :].astype(bf16), c_ref[...])
        _row_loop(ys)

    return _call(
        bg, body, name="s5_core_fwd", grid=(S5_SUB,),
        in_specs=[pl.BlockSpec((T, 128), lambda b: (0, b)),
                  pl.BlockSpec((None, 128, 1024), lambda b: (b, 0, 0)),
                  pl.BlockSpec((None, 2, 8, 512), lambda b: (b, 0, 0, 0)),
                  pl.BlockSpec((None, 1024, 128), lambda b: (b, 0, 0))],
        out_specs=pl.BlockSpec((T, 128), lambda b: (0, b)),
        out_shape=SDS((T, D), f32),
        scratch_shapes=[pltpu.VMEM((T, 1024), f32)],
        compiler_params=_cp(dimension_semantics=("arbitrary",)),
    )(hn, bm, lam, cm)


def s5_core_bwd(hn, dy, bm, lam, cm, bg=()):
    def body(u_ref, dy_ref, b_ref, lam_ref, c_ref, du_ref, db_ref, dct_ref, dlam_ref, S1, S2):
        def bu(rows):
            S1[rows, :] = _dot(u_ref[rows, :], b_ref[...])
        _row_loop(bu)
        _chunk_scan(S1, lam_ref[0], lam_ref[1], False)
        dct_ref[...] = jnp.zeros_like(dct_ref)

        def dx(rows):
            dyb = dy_ref[rows, :].astype(bf16)
            S2[rows, :] = _dot_nt(dyb, c_ref[...])
            dct_ref[...] += _dot_tn(dyb, S1[rows, :].astype(bf16))
        _row_loop(dx)
        dr, di = _chunk_scan(S2, lam_ref[0], lam_ref[1], True, aux=S1)
        dlam_ref[0] = dr
        dlam_ref[1] = di
        db_ref[...] = jnp.zeros_like(db_ref)

        def dbu(rows):
            gb = S2[rows, :].astype(bf16)
            db_ref[...] += _dot_tn(u_ref[rows, :], gb)
            du_ref[rows, :] = _dot_nt(gb, b_ref[...])
        _row_loop(dbu)

    return _call(
        bg, body, name="s5_core_bwd", grid=(S5_SUB,),
        in_specs=[pl.BlockSpec((T, 128), lambda b: (0, b)),
                  pl.BlockSpec((T, 128), lambda b: (0, b)),
                  pl.BlockSpec((None, 128, 1024), lambda b: (b, 0, 0)),
                  pl.BlockSpec((None, 2, 8, 512), lambda b: (b, 0, 0, 0)),
                  pl.BlockSpec((None, 1024, 128), lambda b: (b, 0, 0))],
        out_specs=[pl.BlockSpec((T, 128), lambda b: (0, b)),
                   pl.BlockSpec((None, 128, 1024), lambda b: (b, 0, 0)),
                   pl.BlockSpec((None, 128, 1024), lambda b: (b, 0, 0)),
                   pl.BlockSpec((None, 2, 8, 512), lambda b: (b, 0, 0, 0))],
        out_shape=[SDS((T, D), f32), SDS((8, 128, 1024), f32), SDS((8, 128, 1024), f32), SDS((8, 2, 8, 512), f32)],
        scratch_shapes=[pltpu.VMEM((T, 1024), f32), pltpu.VMEM((T, 1024), f32)],
        compiler_params=_cp(dimension_semantics=("arbitrary",)),
    )(hn, dy, bm, lam, cm)


TM = 512
NT = T // TM


def _tile(n=D):
    return pl.BlockSpec((TM, n), lambda i: (i, 0))


def s5_pre(xp, g):
    def body(x_ref, g_ref, hn_ref):
        hn, _ = _rms(x_ref[...], g_ref[...])
        hn_ref[...] = hn.astype(bf16)

    return pl.pallas_call(
        body, name="s5_pre", grid=(NT,), in_specs=[_tile(), _full((1, D))], out_specs=_tile(),
        out_shape=SDS((T, D), bf16), compiler_params=_cp(dimension_semantics=("arbitrary",)),
    )(xp, g)


def _gelu_grad(y):
    c = math.sqrt(2.0 / math.pi)
    t = jnp.tanh(c * (y + 0.044715 * y * y * y))
    return 0.5 * (1.0 + t) + 0.5 * y * (1.0 - t * t) * c * (1.0 + 3.0 * 0.044715 * y * y)


def s5_post(ys, xp, g, d, wglu, bglu, bg=()):
    def body(ys_ref, x_ref, g_ref, d_ref, w_ref, b_ref, y_ref, z_ref, h_ref):
        x = x_ref[...]
        hn, _ = _rms(x, g_ref[...])
        y = ys_ref[...] + d_ref[...] * hn
        y_ref[...] = y
        yg = jax.nn.gelu(y).astype(bf16)
        for j in range(4):
            cv = slice(j * 256, (j + 1) * 256)
            cg = slice(1024 + j * 256, 1024 + (j + 1) * 256)
            val = _dot(yg, w_ref[j]) + b_ref[:, cv]
            gate = _dot(yg, w_ref[j + 4]) + b_ref[:, cg]
            z_ref[:, cv] = val
            z_ref[:, cg] = gate
            h_ref[:, cv] = x[:, cv] + val * jax.nn.sigmoid(gate)

    return _call(
        bg, body, name="s5_post", grid=(NT,),
        in_specs=[_tile(), _tile(), _full((1, D)), _full((1, D)), _full((8, D, 256)), _full((1, 2 * D))],
        out_specs=[_tile(), _tile(2 * D), _tile()],
        out_shape=[SDS((T, D), f32), SDS((T, 2 * D), f32), SDS((T, D), f32)],
        compiler_params=_cp(dimension_semantics=("arbitrary",)),
    )(ys, xp, g, d, wglu, bglu)


def s5_post_bwd(dh, y, z, wglu, bg=()):
    def body(dh_ref, y_ref, z_ref, w_ref, dy_ref, dw_ref, db_ref, acc):
        i = pl.program_id(0)

        @pl.when(i == 0)
        def _():
            acc[...] = jnp.zeros_like(acc)
            db_ref[...] = jnp.zeros_like(db_ref)

        dh_ = dh_ref[...]
        y = y_ref[...]
        yg = jax.nn.gelu(y).astype(bf16)
        dyg = jnp.zeros((TM, D), f32)
        for j in range(4):
            cv = slice(j * 256, (j + 1) * 256)
            cg = slice(1024 + j * 256, 1024 + (j + 1) * 256)
            val = z_ref[:, cv]
            sg = jax.nn.sigmoid(z_ref[:, cg])
            dval = dh_[:, cv] * sg
            dgate = dh_[:, cv] * val * sg * (1.0 - sg)
            db_ref[:, cv] += _colsum8(dval)
            db_ref[:, cg] += _colsum8(dgate)
            dvb = dval.astype(bf16)
            dgb = dgate.astype(bf16)
            acc[j] += _dot_tn(yg, dvb)
            acc[j + 4] += _dot_tn(yg, dgb)
            dyg = dyg + _dot_nt(dvb, w_ref[j]) + _dot_nt(dgb, w_ref[j + 4])
        dy_ref[...] = dyg * _gelu_grad(y)

        @pl.when(i == NT - 1)
        def _():
            dw_ref[...] = acc[...].astype(bf16)

    return _call(
        bg, body, name="s5_post_bwd", grid=(NT,),
        in_specs=[_tile(), _tile(), _tile(2 * D), _full((8, D, 256))],
        out_specs=[_tile(), _full((8, D, 256)), _full((8, 2 * D))],
        out_shape=[SDS((T, D), f32), SDS((8, D, 256), bf16), SDS((8, 2 * D), f32)],
        scratch_shapes=[pltpu.VMEM((8, D, 256), f32)],
        compiler_params=_cp(dimension_semantics=("arbitrary",)),
    )(dh, y, z, wglu)


def s5_pre_bwd(xp, g, du, dy, d, dh, bg=()):
    def body(x_ref, g_ref, du_ref, dy_ref, d_ref, dh_ref, dx_ref, dg_ref, dd_ref):
        i = pl.program_id(0)

        @pl.when(i == 0)
        def _():
            dg_ref[...] = jnp.zeros_like(dg_ref)
            dd_ref[...] = jnp.zeros_like(dd_ref)

        x = x_ref[...]
        g = g_ref[...]
        dy = dy_ref[...]
        hn, _ = _rms(x, g)
        dhn = du_ref[...] + d_ref[...] * dy
        dx, dgt = _rms_bwd(x, g, dhn)
        dx_ref[...] = dh_ref[...] + dx
        dg_ref[...] += _colsum8(dgt)
        dd_ref[...] += _colsum8(dy * hn)

    return _call(
        bg, body, name="s5_pre_bwd", grid=(NT,),
        in_specs=[_tile(), _full((1, D)), _tile(), _tile(), _full((1, D)), _tile()],
        out_specs=[_tile(), _full((8, D)), _full((8, D))],
        out_shape=[SDS((T, D), f32), SDS((8, D), f32), SDS((8, D), f32)],
        compiler_params=_cp(dimension_semantics=("arbitrary",)),
    )(xp, g, du, dy, d, dh)


TMF = 1024


def mlp_fwd(h, g, w_in, w_out, layer, bg=()):
    def body(h_ref, g_ref, wi_ref, wo_ref, hm_ref, out_ref, acc):
        j = pl.program_id(1)

        @pl.when(j == 0)
        def _():
            hm, _ = _rms(h_ref[...], g_ref[...])
            hm_ref[...] = hm.astype(bf16)
            acc[...] = jnp.zeros_like(acc)

        a = jnp.maximum(_dot(hm_ref[...], wi_ref[...]), 0.0)
        acc[...] += _dot((a * a).astype(bf16), wo_ref[...])

        @pl.when(j == NDEV - 1)
        def _():
            out_ref[...] = h_ref[...] + acc[...]

    return _call(
        bg, body, name=f"mlp_fwd{layer}", grid=(T // TMF, NDEV),
        in_specs=[pl.BlockSpec((TMF, D), lambda i, j: (i, 0)),
                  pl.BlockSpec((1, D), lambda i, j: (0, 0)),
                  pl.BlockSpec((None, D, D_FF_SHARD), lambda i, j: (j, 0, 0)),
                  pl.BlockSpec((None, D_FF_SHARD, D), lambda i, j: (j, 0, 0))],
        out_specs=[pl.BlockSpec((TMF, D), lambda i, j: (i, 0)), pl.BlockSpec((TMF, D), lambda i, j: (i, 0))],
        out_shape=[SDS((T, D), bf16), SDS((T, D), f32)],
        scratch_shapes=[pltpu.VMEM((TMF, D), f32)],
        compiler_params=_cp(dimension_semantics=("arbitrary", "arbitrary")),
    )(h, g, w_in, w_out)


def mlp_bwd(h, hm, g, dout, w_in, w_out, layer, bg=()):
    last = NDEV - 1

    def body(h_ref, hm_ref, g_ref, do_ref, wi_ref, wo_ref, dh_ref, dwi_ref, dwo_ref, dg_ref, dhm, awi, awo):
        j = pl.program_id(0)
        i = pl.program_id(1)
        rows = pl.ds(pl.multiple_of(i * TM, TM), TM)

        @pl.when(i == 0)
        def _():
            awi[...] = jnp.zeros_like(awi)
            awo[...] = jnp.zeros_like(awo)

        hm_ = hm_ref[...]
        dob = do_ref[...].astype(bf16)
        r = jnp.maximum(_dot(hm_, wi_ref[...]), 0.0)
        dz = (_dot_nt(dob, wo_ref[...]) * (2.0 * r)).astype(bf16)
        awo[...] += _dot_tn((r * r).astype(bf16), dob)
        awi[...] += _dot_tn(hm_, dz)
        part = _dot_nt(dz, wi_ref[...])

        @pl.when(j == 0)
        def _():
            dhm[rows, :] = part

        @pl.when(j > 0)
        def _():
            dhm[rows, :] += part

        @pl.when(i == NT - 1)
        def _():
            dwi_ref[...] = awi[...].astype(bf16)
            dwo_ref[...] = awo[...].astype(bf16)

        @pl.when(j == last)
        def _():
            @pl.when(i == 0)
            def _():
                dg_ref[...] = jnp.zeros_like(dg_ref)
            dx, dgt = _rms_bwd(h_ref[...], g_ref[...], dhm[rows, :])
            dh_ref[...] = do_ref[...] + dx
            dg_ref[...] += _colsum8(dgt)

    late = lambda j, i: (jnp.where(j == last, i, 0), 0)
    return _call(
        bg, body, name=f"mlp_bwd{layer}", grid=(NDEV, NT),
        in_specs=[pl.BlockSpec((TM, D), late),
                  pl.BlockSpec((TM, D), lambda j, i: (i, 0)),
                  pl.BlockSpec((1, D), lambda j, i: (0, 0)),
                  pl.BlockSpec((TM, D), lambda j, i: (i, 0)),
                  pl.BlockSpec((None, D, D_FF_SHARD), lambda j, i: (j, 0, 0)),
                  pl.BlockSpec((None, D_FF_SHARD, D), lambda j, i: (j, 0, 0))],
        out_specs=[pl.BlockSpec((TM, D), late),
                   pl.BlockSpec((None, D, D_FF_SHARD), lambda j, i: (j, 0, 0)),
                   pl.BlockSpec((None, D_FF_SHARD, D), lambda j, i: (j, 0, 0)),
                   pl.BlockSpec((8, D), lambda j, i: (0, 0))],
        out_shape=[SDS((T, D), f32), SDS((NDEV, D, D_FF_SHARD), bf16), SDS((NDEV, D_FF_SHARD, D), bf16),
                   SDS((8, D), f32)],
        scratch_shapes=[pltpu.VMEM((T, D), f32), pltpu.VMEM((D, D_FF_SHARD), f32), pltpu.VMEM((D_FF_SHARD, D), f32)],
        compiler_params=_cp(dimension_semantics=("arbitrary", "arbitrary")),
    )(h, hm, g, dout, w_in, w_out)


def _spread4():
    r = lax.broadcasted_iota(jnp.int32, (256, D), 0)
    c = lax.broadcasted_iota(jnp.int32, (256, D), 1)
    return ((c // 256 == r // HEAD_DIM) & (c % HEAD_DIM == r % HEAD_DIM)).astype(bf16)


def attn_pre(h, g_kv, g_mix, wkv, bkv, spread, wq, bq):
    def body(h_ref, gkv_ref, gm_ref, wkv_ref, bkv_ref, sp_ref, wq_ref, bq_ref, kvn_ref, hn_ref, k_ref, v_ref, q_ref):
        h_ = h_ref[...]
        kvn = _rms(h_, gkv_ref[...])[0].astype(bf16)
        hn = _rms(h_, gm_ref[...])[0].astype(bf16)
        kvn_ref[...] = kvn
        hn_ref[...] = hn
        kv = (_dot(kvn, wkv_ref[...]) + bkv_ref[...]).astype(bf16)
        k_ref[...] = _dot(kv[:, :256], sp_ref[...]).astype(bf16)
        v_ref[...] = _dot(kv[:, 256:], sp_ref[...]).astype(bf16)
        q_ref[...] = (_dot(hn, wq_ref[...]) + bq_ref[...]).astype(bf16)

    return pl.pallas_call(
        body, name="attn_pre", grid=(NT,),
        in_specs=[_tile(), _full((1, D)), _full((1, D)), _full((D, 512)), _full((1, 512)), _full((256, D)),
                  _full((D, D)), _full((1, D))],
        out_specs=[_tile()] * 5,
        out_shape=[SDS((T, D), bf16)] * 5,
        compiler_params=_cp(dimension_semantics=("arbitrary",)),
    )(h, g_kv, g_mix, wkv, bkv, spread, wq, bq)


def _attn_specs():
    cur = pl.BlockSpec((TM, 256), lambda j, n: (n, j))
    prev = pl.BlockSpec((BLK, 256), lambda j, n: (jnp.maximum(n * (TM // BLK) - 1, 0), j))
    return cur, prev


def _head_mask(g):
    lane = lax.broadcasted_iota(jnp.int32, (1, 256), 1)
    return (lane >= g * HEAD_DIM) & (lane < (g + 1) * HEAD_DIM)


def _stack_heads(t):
    return jnp.concatenate([jnp.where(_head_mask(g), t, 0) for g in range(Q_PER_KV)], axis=0)


def _unstack_heads(t):
    out = jnp.where(_head_mask(0), t[0:BLK], 0.0)
    for g in range(1, Q_PER_KV):
        out = out + jnp.where(_head_mask(g), t[g * BLK:(g + 1) * BLK], 0.0)
    return out


def _attn_probs(qs, k2, sinks, first):
    rows = Q_PER_KV * BLK
    s = _dot_nt(qs, k2) * (1.0 / math.sqrt(HEAD_DIM))
    qi = jnp.bitwise_and(lax.broadcasted_iota(jnp.int32, (rows, 2 * BLK), 0), BLK - 1)
    kj = lax.broadcasted_iota(jnp.int32, (rows, 2 * BLK), 1)
    diff = qi + BLK - kj
    valid = (diff >= 0) & (diff < BLK) & (jnp.logical_not(first) | (kj >= BLK))
    s = jnp.where(valid, s, -jnp.inf)
    rb = lax.broadcasted_iota(jnp.int32, (rows, 1), 0)
    sink = jnp.where(rb < BLK, sinks[0], jnp.where(rb < 2 * BLK, sinks[1], jnp.where(rb < 3 * BLK, sinks[2], sinks[3])))
    m = jnp.maximum(jnp.max(s, axis=-1, keepdims=True), sink)
    p = jnp.exp(s - m)
    ps = jnp.exp(sink - m)
    denom = jnp.sum(p, axis=-1, keepdims=True) + ps
    return p / denom, ps / denom


def attn_core_fwd(q, k4, v4, sinks, bg=()):
    nb = TM // BLK

    def body(sink_ref, q_ref, kc_ref, kp_ref, vc_ref, vp_ref, o_ref):
        j = pl.program_id(0)
        n = pl.program_id(1)
        sk = [sink_ref[j * Q_PER_KV + g] for g in range(Q_PER_KV)]
        for b in range(nb):
            qb = q_ref[b * BLK:(b + 1) * BLK, :]
            if b == 0:
                k2 = jnp.concatenate([kp_ref[...], kc_ref[0:BLK, :]], axis=0)
                v2 = jnp.concatenate([vp_ref[...], vc_ref[0:BLK, :]], axis=0)
                first = n == 0
            else:
                k2 = kc_ref[(b - 1) * BLK:(b + 1) * BLK, :]
                v2 = vc_ref[(b - 1) * BLK:(b + 1) * BLK, :]
                first = False
            a, _ = _attn_probs(_stack_heads(qb), k2, sk, first)
            o_ref[b * BLK:(b + 1) * BLK, :] = _unstack_heads(_dot(a.astype(bf16), v2)).astype(bf16)

    cur, prev = _attn_specs()
    return _call(
        bg, body, name="attn_core_fwd", grid=(N_KV, NT),
        in_specs=[pl.BlockSpec(memory_space=pltpu.SMEM), cur, cur, prev, cur, prev],
        out_specs=cur, out_shape=SDS((T, D), bf16),
        compiler_params=_cp(dimension_semantics=("arbitrary", "arbitrary")),
    )(sinks, q, k4, k4, v4, v4)


def attn_post(h, o, wo, bo):
    def body(h_ref, o_ref, w_ref, b_ref, out_ref):
        out_ref[...] = h_ref[...] + _dot(o_ref[...], w_ref[...]) + b_ref[...]

    return pl.pallas_call(
        body, name="attn_post", grid=(NT,), in_specs=[_tile(), _tile(), _full((D, D)), _full((1, D))],
        out_specs=_tile(), out_shape=SDS((T, D), f32), compiler_params=_cp(dimension_semantics=("arbitrary",)),
    )(h, o, wo, bo)


def attn_bwd_pre(dh, o, wo, bg=()):
    def body(dh_ref, o_ref, w_ref, do_ref, dw_ref, db_ref, acc):
        i = pl.program_id(0)

        @pl.when(i == 0)
        def _():
            acc[...] = jnp.zeros_like(acc)
            db_ref[...] = jnp.zeros_like(db_ref)

        dh_ = dh_ref[...]
        dhb = dh_.astype(bf16)
        do_ref[...] = _dot_nt(dhb, w_ref[...]).astype(bf16)
        acc[...] += _dot_tn(o_ref[...], dhb)
        db_ref[...] += _colsum8(dh_)

        @pl.when(i == NT - 1)
        def _():
            dw_ref[...] = acc[...].astype(bf16)

    return _call(
        bg, body, name="attn_bwd_pre", grid=(NT,), in_specs=[_tile(), _tile(), _full((D, D))],
        out_specs=[_tile(), _full((D, D)), _full((8, D))],
        out_shape=[SDS((T, D), bf16), SDS((D, D), bf16), SDS((8, D), f32)],
        scratch_shapes=[pltpu.VMEM((D, D), f32)],
        compiler_params=_cp(dimension_semantics=("arbitrary",)),
    )(dh, o, wo)


def attn_core_bwd(q, do, k4, v4, sinks, bg=()):
    nb = TM // BLK

    def body(sink_ref, q_ref, do_ref, kc_ref, kp_ref, vc_ref, vp_ref, dq_ref, dk_ref, dv_ref, ds_ref):
        j = pl.program_id(0)
        n = pl.program_id(1)

        @pl.when(n == 0)
        def _():
            dk_ref[...] = jnp.zeros_like(dk_ref)
            dv_ref[...] = jnp.zeros_like(dv_ref)
            ds_ref[...] = jnp.zeros_like(ds_ref)

        lane8 = lax.broadcasted_iota(jnp.int32, (8, 128), 1)
        row8 = lax.broadcasted_iota(jnp.int32, (8, 128), 0)
        sk = [sink_ref[j * Q_PER_KV + g] for g in range(Q_PER_KV)]
        for b in range(nb):
            qs = _stack_heads(q_ref[b * BLK:(b + 1) * BLK, :])
            dos = _stack_heads(do_ref[b * BLK:(b + 1) * BLK, :])
            if b == 0:
                k2 = jnp.concatenate([kp_ref[...], kc_ref[0:BLK, :]], axis=0)
                v2 = jnp.concatenate([vp_ref[...], vc_ref[0:BLK, :]], axis=0)
                first = n == 0
            else:
                k2 = kc_ref[(b - 1) * BLK:(b + 1) * BLK, :]
                v2 = vc_ref[(b - 1) * BLK:(b + 1) * BLK, :]
                first = False
            a, asink = _attn_probs(qs, k2, sk, first)
            dp = _dot_nt(dos, v2)
            dd = jnp.sum(a * dp, axis=-1, keepdims=True)
            dsc = (a * (dp - dd) * (1.0 / math.sqrt(HEAD_DIM))).astype(bf16)
            t = asink * dd
            for g in range(Q_PER_KV):
                dsink = -jnp.sum(t[g * BLK:(g + 1) * BLK], axis=0, keepdims=True)
                ds_ref[...] += jnp.where((lane8 == g) & (row8 == 0), jnp.broadcast_to(dsink, (8, 128)), 0.0)
            dq_ref[b * BLK:(b + 1) * BLK, :] = _unstack_heads(_dot(dsc, k2))
            dk2 = _dot_tn(dsc, qs)
            dv2 = _dot_tn(a.astype(bf16), dos)
            cur = pl.ds(pl.multiple_of(n * TM + b * BLK, BLK), BLK)
            dk_ref[cur, :] += dk2[BLK:, :]
            dv_ref[cur, :] += dv2[BLK:, :]
            if b == 0:
                @pl.when(n > 0)
                def _():
                    prv = pl.ds(pl.multiple_of(n * TM - BLK, BLK), BLK)
                    dk_ref[prv, :] += dk2[:BLK, :]
                    dv_ref[prv, :] += dv2[:BLK, :]
            else:
                prv = pl.ds(pl.multiple_of(n * TM + (b - 1) * BLK, BLK), BLK)
                dk_ref[prv, :] += dk2[:BLK, :]
                dv_ref[prv, :] += dv2[:BLK, :]

    cur, prev = _attn_specs()
    col = pl.BlockSpec((T, 256), lambda j, n: (0, j))
    return _call(
        bg, body, name="attn_core_bwd", grid=(N_KV, NT),
        in_specs=[pl.BlockSpec(memory_space=pltpu.SMEM), cur, cur, cur, prev, cur, prev],
        out_specs=[cur, col, col, pl.BlockSpec((None, 8, 128), lambda j, n: (j, 0, 0))],
        out_shape=[SDS((T, D), f32), SDS((T, D), f32), SDS((T, D), f32), SDS((N_KV, 8, 128), f32)],
        compiler_params=_cp(dimension_semantics=("arbitrary", "arbitrary")),
    )(sinks, q, do, k4, k4, v4, v4)


def attn_bwd_q(h, dh, dq, hn, g_mix, wq):
    def body(h_ref, dh_ref, dq_ref, hn_ref, gm_ref, wq_ref, out_ref, dwq_ref, dbq_ref, dgm_ref, aq):
        i = pl.program_id(0)

        @pl.when(i == 0)
        def _():
            aq[...] = jnp.zeros_like(aq)
            dbq_ref[...] = jnp.zeros_like(dbq_ref)
            dgm_ref[...] = jnp.zeros_like(dgm_ref)

        dq_ = dq_ref[...]
        dqb = dq_.astype(bf16)
        aq[...] += _dot_tn(hn_ref[...], dqb)
        dbq_ref[...] += _colsum8(dq_)
        dx, dg = _rms_bwd(h_ref[...], gm_ref[...], _dot_nt(dqb, wq_ref[...]))
        out_ref[...] = dh_ref[...] + dx
        dgm_ref[...] += _colsum8(dg)

        @pl.when(i == NT - 1)
        def _():
            dwq_ref[...] = aq[...].astype(bf16)

    vec = _full((8, D))
    mat = _full((D, D))
    return pl.pallas_call(
        body, name="attn_bwd_q", grid=(NT,),
        in_specs=[_tile()] * 4 + [_full((1, D)), mat],
        out_specs=[_tile(), mat, vec, vec],
        out_shape=[SDS((T, D), f32), SDS((D, D), bf16), SDS((8, D), f32), SDS((8, D), f32)],
        scratch_shapes=[pltpu.VMEM((D, D), f32)],
        compiler_params=_cp(dimension_semantics=("arbitrary",)),
    )(h, dh, dq, hn, g_mix, wq)


def attn_bwd_kv(h, dh, dk4, dv4, kvn, g_kv, wkv, spread):
    def body(h_ref, dh_ref, dk_ref, dv_ref, kvn_ref, gkv_ref, wkv_ref, sp_ref, out_ref, dw_ref, db_ref, dgkv_ref, acc):
        i = pl.program_id(0)

        @pl.when(i == 0)
        def _():
            for r in (acc, db_ref, dgkv_ref):
                r[...] = jnp.zeros_like(r)

        dkv = jnp.concatenate([_dot_nt(dk_ref[...].astype(bf16), sp_ref[...]),
                               _dot_nt(dv_ref[...].astype(bf16), sp_ref[...])], axis=1)
        dkvb = dkv.astype(bf16)
        acc[...] += _dot_tn(kvn_ref[...], dkvb)
        db_ref[...] += _colsum8(dkv)
        dx, dg = _rms_bwd(h_ref[...], gkv_ref[...], _dot_nt(dkvb, wkv_ref[...]))
        out_ref[...] = dh_ref[...] + dx
        dgkv_ref[...] += _colsum8(dg)

        @pl.when(i == NT - 1)
        def _():
            dw_ref[...] = acc[...].astype(bf16)

    return pl.pallas_call(
        body, name="attn_bwd_kv", grid=(NT,),
        in_specs=[_tile()] * 5 + [_full((1, D)), _full((D, 512)), _full((256, D))],
        out_specs=[_tile(), _full((D, 512)), _full((8, 512)), _full((8, D))],
        out_shape=[SDS((T, D), f32), SDS((D, 512), bf16), SDS((8, 512), f32), SDS((8, D), f32)],
        scratch_shapes=[pltpu.VMEM((D, 512), f32)],
        compiler_params=_cp(dimension_semantics=("arbitrary",)),
    )(h, dh, dk4, dv4, kvn, g_kv, wkv, spread)


def final_loss(h, g, target):
    def body(h_ref, g_ref, t_ref, loss_ref, dh_ref, dg_ref):
        i = pl.program_id(0)

        @pl.when(i == 0)
        def _():
            loss_ref[...] = jnp.zeros_like(loss_ref)
            dg_ref[...] = jnp.zeros_like(dg_ref)

        h_ = h_ref[...]
        g_ = g_ref[...]
        y, _ = _rms(h_, g_)
        diff = y - t_ref[...]
        per_tok = jnp.mean(diff * diff, axis=-1, keepdims=True)
        tot = 0.5 * jnp.sum(per_tok, axis=0, keepdims=True)
        lane = lax.broadcasted_iota(jnp.int32, (8, 128), 1)
        row = lax.broadcasted_iota(jnp.int32, (8, 128), 0)
        loss_ref[...] += jnp.where((lane == 0) & (row == 0), jnp.broadcast_to(tot, (8, 128)), 0.0)
        dx, dgt = _rms_bwd(h_, g_, diff * (1.0 / D))
        dh_ref[...] = dx
        dg_ref[...] += _colsum8(dgt)

    return pl.pallas_call(
        body, name="final_loss", grid=(NT,), in_specs=[_tile(), _full((1, D)), _tile()],
        out_specs=[_full((8, 128)), _tile(), _full((8, D))],
        out_shape=[SDS((8, 128), f32), SDS((T, D), f32), SDS((8, D), f32)],
        compiler_params=_cp(dimension_semantics=("arbitrary",)),
    )(h, g, target)


def _to_chunked(a):
    return a.reshape(S5_CH, S5_STEPS, a.shape[-1]).transpose(1, 0, 2).reshape(T, a.shape[-1])


def _from_chunked(a):
    return a.reshape(S5_STEPS, S5_CH, a.shape[-1]).transpose(1, 0, 2).reshape(T, a.shape[-1])


def _rep4(w):
    return jnp.broadcast_to(w.reshape(w.shape[0], N_KV, 1, HEAD_DIM), (w.shape[0], N_KV, Q_PER_KV, HEAD_DIM)).reshape(
        w.shape[0], N_KV * Q_PER_KV * HEAD_DIM)


def _fold4(w):
    return w.reshape(w.shape[0], N_KV, Q_PER_KV, HEAD_DIM).sum(axis=2).reshape(w.shape[0], N_KV * HEAD_DIM)


def fwd_bwd(x, target, p, shards, core):
    row = lambda v: v.reshape(1, -1)
    (lam, bm, cm), prep_vjp = jax.vjp(s5_discretize, p["s5_a_re"][0], p["s5_a_im"][0], p["s5_log_dt"][0],
                                      p["s5_b_re"][0], p["s5_b_im"][0], p["s5_c_re"][0], p["s5_c_im"][0])
    bmb, cmb = bm.astype(bf16), cm.astype(bf16)
    g_mix0, g_mix1 = row(p["norm_mix"][0]), row(p["norm_mix"][1])
    g_mlp0, g_mlp1 = row(p["norm_mlp"][0]), row(p["norm_mlp"][1])
    g_kv, g_fin = row(p["norm_kv"]), row(p["norm_final"])
    bq, bo = p["b_q"], p["b_o"]
    bkv = row(p["b_kv"])
    spread = _spread4()
    sinks = p["sinks"].reshape(16)

    def reduce_pairs(names, bg):
        return [add_pairs(g, r, core, f"add_pairs_{n}") for n, g, r in zip(names, bg.arrs, bg.result)]

    xp = _to_chunked(x)
    hn0 = s5_pre(xp, g_mix0)
    ga = BgGather([shards["s5_w_glu"], shards["vecs"], shards["w_in0"]])
    ys = s5_core_fwd(hn0, bmb, lam, cmb, bg=[ga])
    wglu, gvec, win0 = ga.result
    d_skip = gvec[:, 0, :128].reshape(1, D)
    bglu = gvec[:, 0, 128:].reshape(1, 2 * D)
    gb = BgGather([shards["w_out0"]])
    y, z, h1 = s5_post(ys, xp, g_mix0, d_skip, wglu, bglu, bg=[gb])
    wout0, = gb.result
    gc = BgGather([shards["w_kv"], shards["w_q"], shards["w_o"], shards["w_in1"]])
    hm0, h2p = mlp_fwd(h1, g_mlp0, win0, wout0, 0, bg=[gc])
    wkv, wq, wo, win1 = gc.result
    wkv, wq, wo = wkv.reshape(D, 512), wq.reshape(D, D), wo.reshape(D, D)
    h2 = _from_chunked(h2p)
    kvn, hn1, k4, v4, q = attn_pre(h2, g_kv, g_mix1, wkv, bkv, spread, wq, bq)
    gd = BgGather([shards["w_out1"]])
    o = attn_core_fwd(q, k4, v4, sinks, bg=[gd])
    wout1, = gd.result
    h3 = attn_post(h2, o, wo, bo)
    hm1, h4 = mlp_fwd(h3, g_mlp1, win1, wout1, 1)
    loss, dh4, dg_fin = final_loss(h4, g_fin, target)

    big = {}
    dh3, dwin1, dwout1, dg_mlp1 = mlp_bwd(h3, hm1, g_mlp1, dh4, win1, wout1, 1)
    pa = BgPair([dwin1, dwout1])
    do, dwo, dbo = attn_bwd_pre(dh3, o, wo, bg=[pa])
    ca = BgChips(reduce_pairs(["w_in1", "w_out1"], pa))
    dq, dk4, dv4, dsink = attn_core_bwd(q, do, k4, v4, sinks, bg=[ca])
    big["w_in1"], big["w_out1"] = zip(ca.arrs, ca.result)
    dh2, dwq, dbq, dg_mix1 = attn_bwd_q(h2, dh3, dq, hn1, g_mix1, wq)
    dh2, dwkv, dbkv, dg_kv = attn_bwd_kv(h2, dh2, dk4, dv4, kvn, g_kv, wkv, spread)
    pb = BgPair([dwkv.reshape(NDEV, 128, 512), dwq.reshape(NDEV, 128, D), dwo.reshape(NDEV, 128, D)])
    dh2p = _to_chunked(dh2)
    dh1, dwin0, dwout0, dg_mlp0 = mlp_bwd(h1, hm0, g_mlp0, dh2p, win0, wout0, 0, bg=[pb])
    cb = BgChips(reduce_pairs(["w_kv", "w_q", "w_o"], pb))
    pc = BgPair([dwin0, dwout0])
    dy, dwglu, dbglu = s5_post_bwd(dh1, y, z, wglu, bg=[cb, pc])
    big["w_kv"], big["w_q"], big["w_o"] = zip(cb.arrs, cb.result)
    cc = BgChips(reduce_pairs(["w_in0", "w_out0"], pc))
    pd = BgPair([dwglu])
    du, dbm, dcmt, dlam = s5_core_bwd(hn0, dy, bmb, lam, cmb, bg=[cc, pd])
    big["w_in0"], big["w_out0"] = zip(cc.arrs, cc.result)
    cd = BgChips(reduce_pairs(["s5_w_glu"], pd))
    dxp, dg_mix0, dd = s5_pre_bwd(xp, g_mix0, du, dy, d_skip, dh1, bg=[cd])
    big["s5_w_glu"], = zip(cd.arrs, cd.result)
    grad_x = _from_chunked(dxp)
    da_re, da_im, dlog_dt, db_re, db_im, dc_re, dc_im = prep_vjp((dlam, dbm, dcmt.transpose(0, 2, 1)))

    def lanes(v_):
        v_ = v_.reshape(1, -1)
        return jnp.pad(v_, ((0, 0), (0, D - v_.shape[1])))

    small = jnp.concatenate([
        dg_mix0[0:1], dg_mix1[0:1], dg_mlp0[0:1], dg_mlp1[0:1], dg_kv[0:1], dg_fin[0:1], dd[0:1], dbq[0:1], dbo[0:1],
        dbglu[0:1].reshape(2, D), lanes(dbkv[0:1]),
        lanes(dsink[:, 0, :Q_PER_KV]), lanes(dlog_dt), lanes(loss[0:1, 0:1]), jnp.zeros((1, D), f32),
        da_re.reshape(4, D), da_im.reshape(4, D),
        db_re.transpose(0, 2, 1).reshape(64, D), db_im.transpose(0, 2, 1).reshape(64, D),
        dc_re.reshape(64, D), dc_im.reshape(64, D)], axis=0)
    return loss, grad_x, small, big


_ANY = pl.BlockSpec(memory_space=pl.ANY)


def _pos():
    return lax.axis_index("x"), lax.axis_index("y"), lax.axis_index("c")


def _other_chips(x, y):
    return [(1 - x, y), (x, 1 - y), (1 - x, 1 - y)]


def all_gather(arrs):
    n = len(arrs)

    def body(*refs):
        ins, outs = refs[:n], refs[n:2 * n]
        send_sems, recv_sems, local_sems = refs[2 * n:]
        x, y, c = _pos()
        me, sib = (x, y, c), (x, y, 1 - c)
        chips = _other_chips(x, y)

        def copy(a, k, block, to, src=None):
            dst = outs[a].at[4 * block[0] + 2 * block[1] + block[2]]
            return pltpu.make_async_remote_copy(
                src_ref=dst if src is None else src, dst_ref=dst, send_sem=send_sems.at[a, k],
                recv_sem=recv_sems.at[a, k], device_id=to, device_id_type=MESH)

        mine = [pltpu.make_async_copy(ins[a], outs[a].at[4 * x + 2 * y + c], local_sems.at[a]) for a in range(n)]
        for cp in mine:
            cp.start()
        first = []
        for a in range(n):
            first.append(copy(a, 0, me, sib, src=ins[a]))
            first += [copy(a, 1 + j, me, (*chip, c), src=ins[a]) for j, chip in enumerate(chips)]
        for cp in first:
            cp.start()
        passed = []
        for j, chip in enumerate(chips):
            for a in range(n):
                copy(a, 1 + j, (*chip, c), me).wait_recv()
                cp = copy(a, 4 + j, (*chip, c), sib)
                cp.start()
                passed.append(cp)
        for a in range(n):
            copy(a, 0, sib, me).wait_recv()
            for j, chip in enumerate(chips):
                copy(a, 4 + j, (*chip, 1 - c), me).wait_recv()
        for cp in first + passed:
            cp.wait_send()
        for cp in mine:
            cp.wait()

    return pl.pallas_call(
        body, name="all_gather", in_specs=[_ANY] * n, out_specs=[_ANY] * n,
        out_shape=[SDS((NDEV,) + a.shape, a.dtype) for a in arrs],
        scratch_shapes=[pltpu.SemaphoreType.DMA((n, 7)), pltpu.SemaphoreType.DMA((n, 7)),
                        pltpu.SemaphoreType.DMA((n,))],
    )(*arrs)


def rs_pair(grads):
    n = len(grads)

    def body(*refs):
        ins, outs = refs[:n], refs[n:2 * n]
        send_sems, recv_sems = refs[2 * n:]
        x, y, c = _pos()
        cps = []
        for a in range(n):
            for k in range(4):
                cps.append(pltpu.make_async_remote_copy(
                    src_ref=ins[a].at[2 * k + 1 - c], dst_ref=outs[a].at[k], send_sem=send_sems.at[a, k],
                    recv_sem=recv_sems.at[a, k], device_id=(x, y, 1 - c), device_id_type=MESH))
        for cp in cps:
            cp.start()
        for cp in cps:
            cp.wait_recv()
        for cp in cps:
            cp.wait_send()

    return pl.pallas_call(
        body, name="rs_pair", in_specs=[_ANY] * n, out_specs=[_ANY] * n,
        out_shape=[SDS((4,) + g.shape[1:], g.dtype) for g in grads],
        scratch_shapes=[pltpu.SemaphoreType.DMA((n, 4)), pltpu.SemaphoreType.DMA((n, 4))],
    )(*grads)


def rs_chips(parts):
    n = len(parts)

    def body(*refs):
        ins, outs = refs[:n], refs[n:2 * n]
        send_sems, recv_sems = refs[2 * n:]
        x, y, c = _pos()
        cps = []
        for a in range(n):
            for r, (px, py) in enumerate(_other_chips(x, y)):
                cps.append(pltpu.make_async_remote_copy(
                    src_ref=ins[a].at[2 * px + py], dst_ref=outs[a].at[r], send_sem=send_sems.at[a, r],
                    recv_sem=recv_sems.at[a, r], device_id=(px, py, c), device_id_type=MESH))
        for cp in cps:
            cp.start()
        for cp in cps:
            cp.wait_recv()
        for cp in cps:
            cp.wait_send()

    return pl.pallas_call(
        body, name="rs_chips", in_specs=[_ANY] * n, out_specs=[_ANY] * n,
        out_shape=[SDS((3,) + g.shape[1:], g.dtype) for g in parts],
        scratch_shapes=[pltpu.SemaphoreType.DMA((n, 3)), pltpu.SemaphoreType.DMA((n, 3))],
    )(*parts)


def _row_tile(r, c):
    return min(r, max(8, (256 * 1024) // c))


def add_pairs(g, r1, core, name):
    _, R, C = g.shape
    tr = _row_tile(R, C)

    def body(core_ref, g_ref, r_ref, o_ref):
        o_ref[...] = (g_ref[...].astype(f32) + r_ref[...].astype(f32)).astype(bf16)

    return pl.pallas_call(
        body, name=name, out_shape=SDS((4, R, C), bf16),
        grid_spec=pltpu.PrefetchScalarGridSpec(
            num_scalar_prefetch=1, grid=(4, R // tr),
            in_specs=[pl.BlockSpec((None, tr, C), lambda k, i, core: (2 * k + core[0], i, 0)),
                      pl.BlockSpec((None, tr, C), lambda k, i, core: (k, i, 0))],
            out_specs=pl.BlockSpec((None, tr, C), lambda k, i, core: (k, i, 0))),
        compiler_params=_cp(dimension_semantics=("arbitrary", "arbitrary")),
    )(core, g, r1)


def _adamw(w, g, m, v):
    m = ADAM_B1 * m + (1.0 - ADAM_B1) * g
    v = ADAM_B2 * v + (1.0 - ADAM_B2) * (g * g)
    m_hat = m / (1.0 - ADAM_B1 ** ADAM_STEP)
    v_hat = v / (1.0 - ADAM_B2 ** ADAM_STEP)
    delta = -ADAM_LR * (m_hat / (jnp.sqrt(v_hat) + ADAM_EPS) + ADAM_WD * w)
    return delta, m, v


def adam_big(w, m, v, part, r2, chip, name, layer=0, prev=None):
    L, R, C = w.shape
    tr = _row_tile(R, C)

    def body(chip_ref, w_ref, m_ref, v_ref, p_ref, r_ref, *rest):
        g_out, d_out, m_out, v_out = rest[-4:]
        g = p_ref[...].astype(f32) + r_ref[0].astype(f32) + r_ref[1].astype(f32) + r_ref[2].astype(f32)
        d, m_, v_ = _adamw(w_ref[...], g, m_ref[...], v_ref[...])
        g_out[...] = g
        d_out[...] = d
        m_out[...] = m_
        v_out[...] = v_

    blk = pl.BlockSpec((None, tr, C), lambda i, chip: (layer, i, 0))
    extra = [] if prev is None else list(prev)
    return pl.pallas_call(
        body, name=name, out_shape=[SDS((L, R, C), f32)] * 4,
        grid_spec=pltpu.PrefetchScalarGridSpec(
            num_scalar_prefetch=1, grid=(R // tr,),
            in_specs=[blk, blk, blk,
                      pl.BlockSpec((None, tr, C), lambda i, chip: (chip[0], i, 0)),
                      pl.BlockSpec((3, tr, C), lambda i, chip: (0, i, 0))] + [_ANY] * len(extra),
            out_specs=[blk] * 4),
        input_output_aliases={6 + k: k for k in range(len(extra))},
        compiler_params=_cp(dimension_semantics=("arbitrary",)),
    )(chip, w, m, v, part, r2, *extra)


def allreduce_small(buf):
    shp = buf.shape
    half = (shp[0] // 16) * 8
    parts = (pl.ds(0, half), pl.ds(half, shp[0] - half))

    def body(in_ref, out_ref, acc1, acc2, r0, r1, r2, send_sems, recv_sems):
        x, y, c = _pos()
        across = [(1 - x, y, c), (x, 1 - y, c)]

        def exchange(src, rcv, dst, copies):
            cps = [pltpu.make_async_remote_copy(
                src_ref=src.at[rows], dst_ref=rcv.at[rows], send_sem=send_sems.at[k], recv_sem=recv_sems.at[k],
                device_id=peer, device_id_type=MESH) for k, rows, peer in copies]
            for cp in cps:
                cp.start()
            for cp in cps:
                cp.wait()
            dst[...] = src[...] + rcv[...]

        exchange(in_ref, r0, acc1, [(0, pl.ds(0, shp[0]), (x, y, 1 - c))])
        exchange(acc1, r1, acc2, [(1, parts[0], across[0]), (2, parts[1], across[1])])
        exchange(acc2, r2, out_ref, [(3, parts[0], across[1]), (4, parts[1], across[0])])

    return pl.pallas_call(
        body, name="allreduce_small", out_shape=SDS(shp, f32),
        scratch_shapes=[pltpu.VMEM(shp, f32)] * 5 + [pltpu.SemaphoreType.DMA((5,)), pltpu.SemaphoreType.DMA((5,))],
    )(buf)


SMALL_ROWS = {'norm_mix': (0, 2, D), 'norm_mlp': (2, 2, D), 'norm_kv': (4, 1, D), 'norm_final': (5, 1, D),
              's5_d': (6, 1, D), 'b_q': (7, 1, D), 'b_o': (8, 1, D), 's5_b_glu': (9, 2, D), 'b_kv': (11, 1, 512),
              'sinks': (12, 1, 16), 's5_log_dt': (13, 1, 64), 's5_a_re': (16, 4, D), 's5_a_im': (20, 4, D),
              's5_b_re': (24, 64, D), 's5_b_im': (88, 64, D), 's5_c_re': (152, 64, D), 's5_c_im': (216, 64, D)}
LOSS_ROW = 14
ROW_PARAMS = ['norm_mix', 'norm_mlp', 'norm_kv', 'norm_final', 'b_q', 'b_o', 'b_kv', 'sinks', 's5_log_dt']
SHARD_PARAMS = ['s5_d', 's5_b_glu']
S5_PARAMS = ['s5_a_re', 's5_a_im', 's5_b_re', 's5_b_im', 's5_c_re', 's5_c_im']


def adam_small(dev, gsum, s5_grads, w, m, v):
    names = ROW_PARAMS + SHARD_PARAMS + S5_PARAMS
    n_g = len(ROW_PARAMS) + len(SHARD_PARAMS)

    def body(dev_ref, gs_ref, *refs):
        pos = [0]

        def take(k):
            r = refs[pos[0]:pos[0] + k]
            pos[0] += k
            return r

        g5 = take(len(S5_PARAMS))
        wr, mr, vr = take(len(names)), take(len(names)), take(len(names))
        g_out = take(n_g)
        d_out, m_out, v_out = take(len(names)), take(len(names)), take(len(names))
        dv = dev_ref[0]
        for i, n in enumerate(names):
            if n in S5_PARAMS:
                g = g5[S5_PARAMS.index(n)][...]
            elif n in SHARD_PARAMS:
                r0, _, _ = SMALL_ROWS[n]
                ln = wr[i].shape[1]
                g = jnp.zeros((1, ln), f32)
                for k in range(NDEV):
                    off = k * ln
                    piece = gs_ref[r0 + off // D:r0 + off // D + 1, off % D:off % D + ln]
                    g = g + jnp.where(dv == k, piece, 0.0)
                g_out[i][...] = g
            else:
                r0, nr, nl = SMALL_ROWS[n]
                g = gs_ref[r0:r0 + nr, 0:nl]
                g_out[i][...] = g
            d, m_, v_ = _adamw(wr[i][...], g, mr[i][...], vr[i][...])
            d_out[i][...] = d
            m_out[i][...] = m_
            v_out[i][...] = v_

    vm = pl.BlockSpec(memory_space=pltpu.VMEM)
    ins = [s5_grads[n] for n in S5_PARAMS] + [d[n] for d in (w, m, v) for n in names]
    shapes = [SDS(w[n].shape, f32) for n in names]
    res = pl.pallas_call(
        body, name="adam_small", in_specs=[pl.BlockSpec(memory_space=pltpu.SMEM)] + [vm] * (1 + len(ins)),
        out_specs=[vm] * (n_g + 3 * len(names)), out_shape=shapes[:n_g] + shapes * 3,
        compiler_params=_cp(),
    )(dev, gsum, *ins)
    g_o = dict(zip(names[:n_g], res[:n_g]))
    rest = res[n_g:]
    k = len(names)
    return g_o, dict(zip(names, rest[:k])), dict(zip(names, rest[k:2 * k])), dict(zip(names, rest[2 * k:]))


WEIGHTS = ['norm_mix', 'norm_mlp', 'norm_kv', 'norm_final', 's5_a_re', 's5_a_im', 's5_log_dt', 's5_b_re', 's5_b_im',
           's5_c_re', 's5_c_im', 's5_d', 's5_w_glu', 's5_b_glu', 'w_kv', 'b_kv', 'w_q', 'b_q', 'sinks', 'w_o', 'b_o',
           'w_mlp_in', 'w_mlp_out']
BIG = ['s5_w_glu', 'w_kv', 'w_q', 'w_o', 'w_mlp_in', 'w_mlp_out']
BIG_2D = {'s5_w_glu': (D, 256), 'w_kv': (128, 512), 'w_q': (128, D), 'w_o': (128, D), 'w_mlp_in': (2 * D, 512),
          'w_mlp_out': (2 * 512, D)}
SHARDED_SMALL = {'s5_d': D, 's5_b_glu': 2 * D}
SMALL = [n for n in WEIGHTS if n not in BIG]
SMALL_SIZE = {'norm_mix': 2 * D, 'norm_mlp': 2 * D, 'norm_kv': D, 'norm_final': D, 's5_a_re': 4096, 's5_a_im': 4096,
              's5_log_dt': 64, 's5_b_re': 65536, 's5_b_im': 65536, 's5_c_re': 65536, 's5_c_im': 65536, 's5_d': D,
              's5_b_glu': 2 * D, 'b_kv': 512, 'b_q': D, 'sinks': 16, 'b_o': D}


def _pack(vals):
    parts = []
    for n in SMALL:
        v = vals[n].reshape(-1).astype(f32)
        parts.append(jnp.pad(v, (0, (-v.shape[0]) % 128)))
    flat = jnp.concatenate(parts)
    flat = jnp.pad(flat, (0, (-flat.shape[0]) % 1024))
    return flat.reshape(-1, 128)


def _unpack(buf):
    flat = buf.reshape(-1)
    out, off = {}, 0
    for n in SMALL:
        sz = SMALL_SIZE[n]
        out[n] = flat[off:off + sz]
        off += sz + (-sz) % 128
    return out


def kernel(x, norm_mix, norm_mlp, norm_kv, norm_final, s5_a_re, s5_a_im, s5_log_dt, s5_b_re, s5_b_im, s5_c_re, s5_c_im, s5_d, s5_w_glu, s5_b_glu, w_kv, b_kv, w_q, b_q, sinks, w_o, b_o, w_mlp_in, w_mlp_out, loss_target, m_norm_mix, m_norm_mlp, m_norm_kv, m_norm_final, m_s5_a_re, m_s5_a_im, m_s5_log_dt, m_s5_b_re, m_s5_b_im, m_s5_c_re, m_s5_c_im, m_s5_d, m_s5_w_glu, m_s5_b_glu, m_w_kv, m_b_kv, m_w_q, m_b_q, m_sinks, m_w_o, m_b_o, m_w_mlp_in, m_w_mlp_out, v_norm_mix, v_norm_mlp, v_norm_kv, v_norm_final, v_s5_a_re, v_s5_a_im, v_s5_log_dt, v_s5_b_re, v_s5_b_im, v_s5_c_re, v_s5_c_im, v_s5_d, v_s5_w_glu, v_s5_b_glu, v_w_kv, v_b_kv, v_w_q, v_b_q, v_sinks, v_w_o, v_b_o, v_w_mlp_in, v_w_mlp_out):
    w = dict(norm_mix=norm_mix, norm_mlp=norm_mlp, norm_kv=norm_kv, norm_final=norm_final, s5_a_re=s5_a_re,
             s5_a_im=s5_a_im, s5_log_dt=s5_log_dt, s5_b_re=s5_b_re, s5_b_im=s5_b_im, s5_c_re=s5_c_re, s5_c_im=s5_c_im,
             s5_d=s5_d, s5_w_glu=s5_w_glu, s5_b_glu=s5_b_glu, w_kv=w_kv, b_kv=b_kv, w_q=w_q, b_q=b_q, sinks=sinks,
             w_o=w_o, b_o=b_o, w_mlp_in=w_mlp_in, w_mlp_out=w_mlp_out)
    m = dict(norm_mix=m_norm_mix, norm_mlp=m_norm_mlp, norm_kv=m_norm_kv, norm_final=m_norm_final, s5_a_re=m_s5_a_re,
             s5_a_im=m_s5_a_im, s5_log_dt=m_s5_log_dt, s5_b_re=m_s5_b_re, s5_b_im=m_s5_b_im, s5_c_re=m_s5_c_re,
             s5_c_im=m_s5_c_im, s5_d=m_s5_d, s5_w_glu=m_s5_w_glu, s5_b_glu=m_s5_b_glu, w_kv=m_w_kv, b_kv=m_b_kv,
             w_q=m_w_q, b_q=m_b_q, sinks=m_sinks, w_o=m_w_o, b_o=m_b_o, w_mlp_in=m_w_mlp_in, w_mlp_out=m_w_mlp_out)
    v = dict(norm_mix=v_norm_mix, norm_mlp=v_norm_mlp, norm_kv=v_norm_kv, norm_final=v_norm_final, s5_a_re=v_s5_a_re,
             s5_a_im=v_s5_a_im, s5_log_dt=v_s5_log_dt, s5_b_re=v_s5_b_re, s5_b_im=v_s5_b_im, s5_c_re=v_s5_c_re,
             s5_c_im=v_s5_c_im, s5_d=v_s5_d, s5_w_glu=v_s5_w_glu, s5_b_glu=v_s5_b_glu, w_kv=v_w_kv, b_kv=v_b_kv,
             w_q=v_w_q, b_q=v_b_q, sinks=v_sinks, w_o=v_w_o, b_o=v_b_o, w_mlp_in=v_w_mlp_in, w_mlp_out=v_w_mlp_out)
    xi, yi, ci = _pos()
    dev = 4 * xi + 2 * yi + ci
    core = ci.reshape(1).astype(jnp.int32)
    chip = (2 * xi + yi).reshape(1).astype(jnp.int32)

    shards = {
        "s5_w_glu": s5_w_glu[0].astype(bf16), "w_kv": w_kv.astype(bf16), "w_q": w_q[0].astype(bf16),
        "w_o": w_o[0].astype(bf16), "w_in0": w_mlp_in[0].astype(bf16), "w_in1": w_mlp_in[1].astype(bf16),
        "w_out0": w_mlp_out[0].astype(bf16), "w_out1": w_mlp_out[1].astype(bf16),
        "vecs": jnp.broadcast_to(jnp.concatenate([s5_d, s5_b_glu], axis=1), (8, 384)),
    }
    _, grad_x, grads, big = fwd_bwd(x[0], loss_target[0], {n: w[n] for n in SMALL}, shards, core)

    out_g, out_d, out_m, out_v = {}, {}, {}, {}
    for n in ("s5_w_glu", "w_kv", "w_q", "w_o"):
        shp = w[n].shape
        r3 = (1,) + BIG_2D[n]
        res = adam_big(w[n].reshape(r3), m[n].reshape(r3), v[n].reshape(r3), *big[n], chip, f"adam_{n}")
        out_g[n], out_d[n], out_m[n], out_v[n] = [r.reshape(shp) for r in res]
    for n, k in (("w_mlp_in", "w_in"), ("w_mlp_out", "w_out")):
        res = adam_big(w[n], m[n], v[n], *big[k + "1"], chip, f"adam_{k}1", layer=1)
        res = adam_big(w[n], m[n], v[n], *big[k + "0"], chip, f"adam_{k}0", layer=0, prev=res)
        out_g[n], out_d[n], out_m[n], out_v[n] = res

    gsum = allreduce_small(grads)
    loss = gsum[LOSS_ROW, 0]
    swapped = ("s5_b_re", "s5_b_im")
    swap = lambda a: a.transpose(0, 1, 3, 2)

    def kernel_side(d):
        d = {n: (d[n].reshape(1, -1) if d[n].ndim == 1 else d[n]) for n in SMALL}
        d.update({n: swap(d[n]) for n in swapped})
        return d

    s5_g = {}
    for n in S5_PARAMS:
        r0, nr, _ = SMALL_ROWS[n]
        s5_g[n] = gsum[r0:r0 + nr].reshape((1, 64, 16, 64) if n in swapped else w[n].shape)
        out_g[n] = s5_g[n]
    g_s, d_s, m_s, v_s = adam_small(dev.reshape(1).astype(jnp.int32), gsum, s5_g, kernel_side(w), kernel_side(m),
                                    kernel_side(v))
    for src, dst in ((g_s, out_g), (d_s, out_d), (m_s, out_m), (v_s, out_v)):
        dst.update(src)
    for dst in (out_g, out_d, out_m, out_v):
        for n in SMALL:
            dst[n] = (swap(dst[n]) if n in swapped else dst[n]).reshape(w[n].shape)

    return (loss, grad_x[None], *[out_g[n] for n in WEIGHTS], *[out_d[n] for n in WEIGHTS],
            *[out_m[n] for n in WEIGHTS], *[out_v[n] for n in WEIGHTS])
```

```python
import functools
import math

import jax
import jax.numpy as jnp
from jax import lax
from jax.experimental import pallas as pl
from jax.experimental.pallas import tpu as pltpu

f32 = jnp.float32
bf16 = jnp.bfloat16
SDS = jax.ShapeDtypeStruct

T = 2048
D = 1024
NDEV = 8
NORM_EPS = 1e-5
S5_G, S5_C, S5_P = 64, 16, 64
S5_SUB = 8
S5_CH = 8
S5_STEPS = T // S5_CH
DT_MIN_LAMBDA = -1e-4
HEAD_DIM = 64
N_KV = 4
Q_PER_KV = 4
BLK = 128
D_FF_SHARD = 512
ADAM_LR, ADAM_B1, ADAM_B2, ADAM_EPS, ADAM_WD, ADAM_STEP = 0.001, 0.9, 0.999, 1e-08, 0.01, 10
VMEM_LIMIT = 56 * 1024 * 1024
MESH = pl.DeviceIdType.MESH


def _cp(**kw):
    return pltpu.CompilerParams(vmem_limit_bytes=VMEM_LIMIT, **kw)


def _dot(a, b):
    return jnp.dot(a, b, preferred_element_type=f32)


def _dot_nt(a, b):
    return lax.dot_general(a, b, (((1,), (1,)), ((), ())), preferred_element_type=f32)


def _dot_tn(a, b):
    return lax.dot_general(a, b, (((0,), (0,)), ((), ())), preferred_element_type=f32)


def _rms(x, g):
    r = lax.rsqrt(jnp.mean(x * x, axis=-1, keepdims=True) + NORM_EPS)
    return x * r * g, r


def _rms_bwd(x, g, dy):
    r = lax.rsqrt(jnp.mean(x * x, axis=-1, keepdims=True) + NORM_EPS)
    u = dy * g
    dx = r * u - (r * r * r) * x * jnp.mean(u * x, axis=-1, keepdims=True)
    return dx, dy * x * r


def _colsum8(v):
    s = jnp.sum(v, axis=0, keepdims=True)
    row = lax.broadcasted_iota(jnp.int32, (8, v.shape[1]), 0)
    return jnp.where(row == 0, jnp.broadcast_to(s, (8, v.shape[1])), 0.0)


def _full(shape):
    nd = len(shape)
    return pl.BlockSpec(shape, lambda *_: (0,) * nd, pipeline_mode=pl.Buffered(1))


_ANY = pl.BlockSpec(memory_space=pl.ANY)


def _pos():
    return lax.axis_index("x"), lax.axis_index("y"), lax.axis_index("c")


def _other_chips(x, y):
    return [(1 - x, y), (x, 1 - y), (1 - x, 1 - y)]


class BgGather:
    SIB, XN, YN, FWD_Y, FWD_X, SIB_X, SIB_Y, SIB_D = range(8)

    def __init__(self, arrs):
        n = len(arrs)
        self.arrs = list(arrs)
        self.out_shape = [SDS((NDEV,) + a.shape, a.dtype) for a in arrs]
        self.scratch = [pltpu.SemaphoreType.DMA((n, 8)), pltpu.SemaphoreType.DMA((n, 8)),
                        pltpu.SemaphoreType.DMA((n,))]
        self.has_mid = True
        self.result = None

    def _halves(self, a):
        rows = self.arrs[a].shape[0]
        cut = rows // 2 if rows >= 32 else rows
        return (0, cut), (cut, rows - cut)

    def _copy(self, ins, outs, sems, a, k, block, to, own=False, part=None):
        slot = 4 * block[0] + 2 * block[1] + block[2]
        rows = pl.ds(0, self.arrs[a].shape[0]) if part is None else pl.ds(*self._halves(a)[part])
        dst = outs[a].at[slot, rows]
        return pltpu.make_async_remote_copy(
            src_ref=ins[a].at[rows] if own else dst, dst_ref=dst, send_sem=sems[0].at[a, k],
            recv_sem=sems[1].at[a, k], device_id=to, device_id_type=MESH)

    def _mine(self, ins, outs, sems):
        x, y, c = _pos()
        return [pltpu.make_async_copy(ins[a], outs[a].at[4 * x + 2 * y + c], sems[2].at[a])
                for a in range(len(self.arrs))]

    def _split(self, a):
        return self._halves(a)[1][1] > 0

    def _sends(self, ins, outs, sems, phase):
        x, y, c = _pos()
        me, sib, xn, yn, dg = (x, y, c), (x, y, 1 - c), (1 - x, y, c), (x, 1 - y, c), (1 - x, 1 - y, c)
        cps = []
        for a in range(len(self.arrs)):
            cp = lambda k, block, to, **kw: self._copy(ins, outs, sems, a, k, block, to, **kw)
            if phase == 0:
                cps += [cp(self.SIB, me, sib, own=True), cp(self.XN, me, xn, own=True), cp(self.YN, me, yn, own=True)]
            elif phase == 1:
                cps.append(cp(self.FWD_Y, xn, yn, part=0))
                if self._split(a):
                    cps.append(cp(self.FWD_X, yn, xn, part=1))
                cps += [cp(self.SIB_X, xn, sib), cp(self.SIB_Y, yn, sib)]
            else:
                cps.append(cp(self.SIB_D, dg, sib))
        return cps

    def _arrivals(self, ins, outs, sems, phase):
        x, y, c = _pos()
        me, xn, yn, dg = (x, y, c), (1 - x, y, c), (x, 1 - y, c), (1 - x, 1 - y, c)
        cps = []
        for a in range(len(self.arrs)):
            cp = lambda k, block, **kw: self._copy(ins, outs, sems, a, k, block, me, **kw)
            if phase == 1:
                cps += [cp(self.XN, xn), cp(self.YN, yn)]
            elif phase == 2:
                cps.append(cp(self.FWD_Y, dg, part=0))
                if self._split(a):
                    cps.append(cp(self.FWD_X, dg, part=1))
            else:
                cps += [cp(self.SIB, (x, y, 1 - c)), cp(self.SIB_X, (1 - x, y, 1 - c)),
                        cp(self.SIB_Y, (x, 1 - y, 1 - c)), cp(self.SIB_D, (1 - x, 1 - y, 1 - c))]
        return cps

    def start(self, ins, outs, sems):
        for cp in self._mine(ins, outs, sems) + self._sends(ins, outs, sems, 0):
            cp.start()

    def mid(self, ins, outs, sems):
        for cp in self._arrivals(ins, outs, sems, 1):
            cp.wait_recv()
        for cp in self._sends(ins, outs, sems, 1):
            cp.start()

    def finish(self, ins, outs, sems):
        for cp in self._arrivals(ins, outs, sems, 2):
            cp.wait_recv()
        for cp in self._sends(ins, outs, sems, 2):
            cp.start()
        for cp in self._arrivals(ins, outs, sems, 3):
            cp.wait_recv()
        for ph in range(3):
            for cp in self._sends(ins, outs, sems, ph):
                cp.wait_send()
        for cp in self._mine(ins, outs, sems):
            cp.wait()


class BgPair:
    def __init__(self, arrs):
        n = len(arrs)
        self.arrs = list(arrs)
        self.out_shape = [SDS((4,) + a.shape[1:], a.dtype) for a in arrs]
        self.scratch = [pltpu.SemaphoreType.DMA((n, 4)), pltpu.SemaphoreType.DMA((n, 4))]
        self.has_mid = False
        self.result = None

    def _copies(self, ins, outs, sems):
        x, y, c = _pos()
        return [pltpu.make_async_remote_copy(
            src_ref=ins[a].at[2 * k + 1 - c], dst_ref=outs[a].at[k], send_sem=sems[0].at[a, k],
            recv_sem=sems[1].at[a, k], device_id=(x, y, 1 - c), device_id_type=MESH)
            for a in range(len(self.arrs)) for k in range(4)]

    def start(self, ins, outs, sems):
        for cp in self._copies(ins, outs, sems):
            cp.start()

    def finish(self, ins, outs, sems):
        cps = self._copies(ins, outs, sems)
        for cp in cps:
            cp.wait_recv()
        for cp in cps:
            cp.wait_send()


class BgChips(BgPair):
    def __init__(self, arrs):
        n = len(arrs)
        self.arrs = list(arrs)
        self.out_shape = [SDS((3,) + a.shape[1:], a.dtype) for a in arrs]
        self.scratch = [pltpu.SemaphoreType.DMA((n, 3)), pltpu.SemaphoreType.DMA((n, 3))]
        self.has_mid = False
        self.result = None

    def _copies(self, ins, outs, sems):
        x, y, c = _pos()
        return [pltpu.make_async_remote_copy(
            src_ref=ins[a].at[2 * px + py], dst_ref=outs[a].at[r], send_sem=sems[0].at[a, r],
            recv_sem=sems[1].at[a, r], device_id=(px, py, c), device_id_type=MESH)
            for a in range(len(self.arrs)) for r, (px, py) in enumerate(_other_chips(x, y))]


def _call(bgs, body, *, name, grid, in_specs, out_specs, out_shape, scratch_shapes=(), compiler_params=None):
    single = not isinstance(out_shape, (list, tuple))
    out_specs_l = [out_specs] if single else list(out_specs)
    out_shape_l = [out_shape] if single else list(out_shape)
    bgs = [b for b in (bgs or []) if b is not None]
    n_in, n_out, n_sc = len(in_specs), len(out_shape_l), len(scratch_shapes)
    nsteps = math.prod(grid)

    def full(*refs):
        pos = [0]

        def take(k):
            r = refs[pos[0]:pos[0] + k]
            pos[0] += k
            return r

        ins = take(n_in)
        b_ins = [take(len(b.arrs)) for b in bgs]
        outs = take(n_out)
        b_outs = [take(len(b.out_shape)) for b in bgs]
        sc = take(n_sc)
        b_sc = [take(len(b.scratch)) for b in bgs]
        if bgs:
            step = pl.program_id(0)
            for d in range(1, len(grid)):
                step = step * grid[d] + pl.program_id(d)

            @pl.when(step == 0)
            def _():
                for b, i_, o_, s_ in zip(bgs, b_ins, b_outs, b_sc):
                    b.start(i_, o_, s_)

        body(*ins, *outs, *sc)
        if bgs:
            for b, i_, o_, s_ in zip(bgs, b_ins, b_outs, b_sc):
                if b.has_mid:
                    @pl.when(step == max(0, (3 * nsteps) // 4 - 1))
                    def _():
                        b.mid(i_, o_, s_)

            @pl.when(step == nsteps - 1)
            def _():
                for b, i_, o_, s_ in zip(bgs, b_ins, b_outs, b_sc):
                    b.finish(i_, o_, s_)

    def run(*args):
        res = pl.pallas_call(
            full, name=name, grid=grid,
            in_specs=list(in_specs) + [_ANY] * sum(len(b.arrs) for b in bgs),
            out_specs=out_specs_l + [_ANY] * sum(len(b.out_shape) for b in bgs),
            out_shape=out_shape_l + [s for b in bgs for s in b.out_shape],
            scratch_shapes=list(scratch_shapes) + [s for b in bgs for s in b.scratch],
            compiler_params=compiler_params,
        )(*args, *[a for b in bgs for a in b.arrs])
        rest = list(res[n_out:])
        for b in bgs:
            b.result, rest = rest[:len(b.out_shape)], rest[len(b.out_shape):]
        return res[0] if single else list(res[:n_out])

    return run


def s5_discretize(a_re, a_im, log_dt, b_re, b_im, c_re, c_im):
    lam_r = jnp.minimum(a_re, DT_MIN_LAMBDA)
    lam_i = a_im
    dt = jnp.exp(log_dt)[:, None]
    e = jnp.exp(lam_r * dt)
    lbr = e * jnp.cos(lam_i * dt)
    lbi = e * jnp.sin(lam_i * dt)
    den = lam_r * lam_r + lam_i * lam_i
    cf_r = ((lbr - 1.0) * lam_r + lbi * lam_i) / den
    cf_i = (lbi * lam_r - (lbr - 1.0) * lam_i) / den
    bb_r = cf_r[:, :, None] * b_re - cf_i[:, :, None] * b_im
    bb_i = cf_r[:, :, None] * b_im + cf_i[:, :, None] * b_re
    eye = jnp.eye(8, dtype=f32)

    def blk_b(m):
        return jnp.einsum('bgpc,gh->bgchp', m.reshape(8, 8, S5_P, S5_C), eye).reshape(8, 128, 512)

    def blk_c(m):
        return jnp.einsum('bgcp,gh->bgphc', m.reshape(8, 8, S5_C, S5_P), eye).reshape(8, 512, 128)

    bm = jnp.concatenate([blk_b(bb_r), blk_b(bb_i)], axis=-1)
    cm = jnp.concatenate([blk_c(c_re), -blk_c(c_im)], axis=1)
    lam = jnp.stack([lbr.reshape(8, 512), lbi.reshape(8, 512)], axis=1)
    lam = jnp.broadcast_to(lam[:, :, None, :], (8, 2, 8, 512))
    return lam, bm, cm


def _cmul(ar, ai, br, bi):
    return ar * br - ai * bi, ar * bi + ai * br


def _shift_rows(v, k, up):
    row = lax.broadcasted_iota(jnp.int32, v.shape, 0)
    if up:
        return jnp.where(row < 8 - k, pltpu.roll(v, 8 - k, 0), 0.0)
    return jnp.where(row >= k, pltpu.roll(v, k, 0), 0.0)


def _chunk_scan(S, lr, li, reverse, aux=None):
    if reverse:
        li = -li
    z = jnp.zeros((8, 512), f32)

    def idx(i):
        return (S5_STEPS - 1 - i) if reverse else i

    def rec(xr, xi, row):
        br = S[row, 0:512]
        bi = S[row, 512:1024]
        return lr * xr - li * xi + br, lr * xi + li * xr + bi

    def step1(i, c):
        row = pl.ds(pl.multiple_of(idx(i) * 8, 8), 8)
        return rec(c[0], c[1], row)

    er, ei = lax.fori_loop(0, S5_STEPS, step1, (z, z), unroll=8)
    ar, ai = lr, li
    for _ in range(8):
        ar, ai = _cmul(ar, ai, ar, ai)
    cr, ci = _shift_rows(er, 1, reverse), _shift_rows(ei, 1, reverse)
    for k in (1, 2, 4):
        sr, si = _shift_rows(cr, k, reverse), _shift_rows(ci, k, reverse)
        pr, pi_ = _cmul(ar, ai, sr, si)
        cr, ci = cr + pr, ci + pi_
        ar, ai = _cmul(ar, ai, ar, ai)

    if aux is None:
        def step2(i, c):
            row = pl.ds(pl.multiple_of(idx(i) * 8, 8), 8)
            xr, xi = rec(c[0], c[1], row)
            S[row, 0:512] = xr
            S[row, 512:1024] = xi
            return xr, xi

        lax.fori_loop(0, S5_STEPS, step2, (cr, ci), unroll=8)
        return None

    def step2(i, c):
        gr0, gi0, dr, di = c
        s = idx(i)
        row = pl.ds(pl.multiple_of(s * 8, 8), 8)
        gr, gi = rec(gr0, gi0, row)
        S[row, 0:512] = gr
        S[row, 512:1024] = gi
        prow = pl.ds(pl.multiple_of(jnp.maximum(s - 1, 0) * 8, 8), 8)
        xr = aux[prow, 0:512]
        xi = aux[prow, 512:1024]
        dr = dr + gr * xr + gi * xi
        di = di + gi * xr - gr * xi
        return gr, gi, dr, di

    gr, gi, dr, di = lax.fori_loop(0, S5_STEPS - 1, step2, (cr, ci, z, z), unroll=8)
    row0 = pl.ds(0, 8)
    gr, gi = rec(gr, gi, row0)
    S[row0, 0:512] = gr
    S[row0, 512:1024] = gi
    last = pl.ds((S5_STEPS - 1) * 8, 8)
    xr = _shift_rows(aux[last, 0:512], 1, False)
    xi = _shift_rows(aux[last, 512:1024], 1, False)
    dr = dr + gr * xr + gi * xi
    di = di + gi * xr - gr * xi
    return dr, di


_ROWS = 256


def _row_loop(fn):
    def body(r, c):
        fn(pl.ds(pl.multiple_of(r * _ROWS, _ROWS), _ROWS))
        return c
    lax.fori_loop(0, T // _ROWS, body, 0)


def s5_core_fwd(hn, bm, lam, cm, bg=()):
    def body(u_ref, b_ref, lam_ref, c_ref, ys_ref, S):
        def bu(rows):
            S[rows, :] = _dot(u_ref[rows, :], b_ref[...])
        _row_loop(bu)
        _chunk_scan(S, lam_ref[0], lam_ref[1], False)

        def ys(rows):
            ys_ref[rows, :] = _dot(S[rows, :].astype(bf16), c_ref[...])
        _row_loop(ys)

    return _call(
        bg, body, name="s5_core_fwd", grid=(S5_SUB,),
        in_specs=[pl.BlockSpec((T, 128), lambda b: (0, b)),
                  pl.BlockSpec((None, 128, 1024), lambda b: (b, 0, 0)),
                  pl.BlockSpec((None, 2, 8, 512), lambda b: (b, 0, 0, 0)),
                  pl.BlockSpec((None, 1024, 128), lambda b: (b, 0, 0))],
        out_specs=pl.BlockSpec((T, 128), lambda b: (0, b)),
        out_shape=SDS((T, D), f32),
        scratch_shapes=[pltpu.VMEM((T, 1024), f32)],
        compiler_params=_cp(dimension_semantics=("arbitrary",)),
    )(hn, bm, lam, cm)


def s5_core_bwd(hn, dy, bm, lam, cm, bg=()):
    def body(u_ref, dy_ref, b_ref, lam_ref, c_ref, du_ref, db_ref, dct_ref, dlam_ref, S1, S2):
        def bu(rows):
            S1[rows, :] = _dot(u_ref[rows, :], b_ref[...])
        _row_loop(bu)
        _chunk_scan(S1, lam_ref[0], lam_ref[1], False)
        dct_ref[...] = jnp.zeros_like(dct_ref)

        def dx(rows):
            dyb = dy_ref[rows, :].astype(bf16)
            S2[rows, :] = _dot_nt(dyb, c_ref[...])
            dct_ref[...] += _dot_tn(dyb, S1[rows, :].astype(bf16))
        _row_loop(dx)
        dr, di = _chunk_scan(S2, lam_ref[0], lam_ref[1], True, aux=S1)
        dlam_ref[0] = dr
        dlam_ref[1] = di
        db_ref[...] = jnp.zeros_like(db_ref)

        def dbu(rows):
            gb = S2[rows, :].astype(bf16)
            db_ref[...] += _dot_tn(u_ref[rows, :], gb)
            du_ref[rows, :] = _dot_nt(gb, b_ref[...])
        _row_loop(dbu)

    return _call(
        bg, body, name="s5_core_bwd", grid=(S5_SUB,),
        in_specs=[pl.BlockSpec((T, 128), lambda b: (0, b)),
                  pl.BlockSpec((T, 128), lambda b: (0, b)),
                  pl.BlockSpec((None, 128, 1024), lambda b: (b, 0, 0)),
                  pl.BlockSpec((None, 2, 8, 512), lambda b: (b, 0, 0, 0)),
                  pl.BlockSpec((None, 1024, 128), lambda b: (b, 0, 0))],
        out_specs=[pl.BlockSpec((T, 128), lambda b: (0, b)),
                   pl.BlockSpec((None, 128, 1024), lambda b: (b, 0, 0)),
                   pl.BlockSpec((None, 128, 1024), lambda b: (b, 0, 0)),
                   pl.BlockSpec((None, 2, 8, 512), lambda b: (b, 0, 0, 0))],
        out_shape=[SDS((T, D), f32), SDS((8, 128, 1024), f32), SDS((8, 128, 1024), f32), SDS((8, 2, 8, 512), f32)],
        scratch_shapes=[pltpu.VMEM((T, 1024), f32), pltpu.VMEM((T, 1024), f32)],
        compiler_params=_cp(dimension_semantics=("arbitrary",)),
    )(hn, dy, bm, lam, cm)


TM = 512
NT = T // TM


def _tile(n=D):
    return pl.BlockSpec((TM, n), lambda i: (i, 0))


def s5_pre(xp, g):
    def body(x_ref, g_ref, hn_ref):
        hn, _ = _rms(x_ref[...], g_ref[...])
        hn_ref[...] = hn.astype(bf16)

    return pl.pallas_call(
        body, name="s5_pre", grid=(NT,), in_specs=[_tile(), _full((1, D))], out_specs=_tile(),
        out_shape=SDS((T, D), bf16), compiler_params=_cp(dimension_semantics=("arbitrary",)),
    )(xp, g)


def _gelu_grad(y):
    c = math.sqrt(2.0 / math.pi)
    t = jnp.tanh(c * (y + 0.044715 * y * y * y))
    return 0.5 * (1.0 + t) + 0.5 * y * (1.0 - t * t) * c * (1.0 + 3.0 * 0.044715 * y * y)


def s5_post(ys, xp, g, d, wglu, bglu, bg=()):
    def body(ys_ref, x_ref, g_ref, d_ref, w_ref, b_ref, y_ref, z_ref, h_ref):
        x = x_ref[...]
        hn, _ = _rms(x, g_ref[...])
        y = ys_ref[...] + d_ref[...] * hn
        y_ref[...] = y
        yg = jax.nn.gelu(y).astype(bf16)
        for j in range(4):
            cv = slice(j * 256, (j + 1) * 256)
            cg = slice(1024 + j * 256, 1024 + (j + 1) * 256)
            val = _dot(yg, w_ref[j]) + b_ref[:, cv]
            gate = _dot(yg, w_ref[j + 4]) + b_ref[:, cg]
            z_ref[:, cv] = val
            z_ref[:, cg] = gate
            h_ref[:, cv] = x[:, cv] + val * jax.nn.sigmoid(gate)

    return _call(
        bg, body, name="s5_post", grid=(NT,),
        in_specs=[_tile(), _tile(), _full((1, D)), _full((1, D)), _full((8, D, 256)), _full((1, 2 * D))],
        out_specs=[_tile(), _tile(2 * D), _tile()],
        out_shape=[SDS((T, D), f32), SDS((T, 2 * D), f32), SDS((T, D), f32)],
        compiler_params=_cp(dimension_semantics=("arbitrary",)),
    )(ys, xp, g, d, wglu, bglu)


def s5_post_bwd(dh, y, z, wglu, bg=()):
    def body(dh_ref, y_ref, z_ref, w_ref, dy_ref, dw_ref, db_ref, acc):
        i = pl.program_id(0)

        @pl.when(i == 0)
        def _():
            acc[...] = jnp.zeros_like(acc)
            db_ref[...] = jnp.zeros_like(db_ref)

        dh_ = dh_ref[...]
        y = y_ref[...]
        yg = jax.nn.gelu(y).astype(bf16)
        dyg = jnp.zeros((TM, D), f32)
        for j in range(4):
            cv = slice(j * 256, (j + 1) * 256)
            cg = slice(1024 + j * 256, 1024 + (j + 1) * 256)
            val = z_ref[:, cv]
            sg = jax.nn.sigmoid(z_ref[:, cg])
            dval = dh_[:, cv] * sg
            dgate = dh_[:, cv] * val * sg * (1.0 - sg)
            db_ref[:, cv] += _colsum8(dval)
            db_ref[:, cg] += _colsum8(dgate)
            dvb = dval.astype(bf16)
            dgb = dgate.astype(bf16)
            acc[j] += _dot_tn(yg, dvb)
            acc[j + 4] += _dot_tn(yg, dgb)
            dyg = dyg + _dot_nt(dvb, w_ref[j]) + _dot_nt(dgb, w_ref[j + 4])
        dy_ref[...] = dyg * _gelu_grad(y)

        @pl.when(i == NT - 1)
        def _():
            dw_ref[...] = acc[...].astype(bf16)

    return _call(
        bg, body, name="s5_post_bwd", grid=(NT,),
        in_specs=[_tile(), _tile(), _tile(2 * D), _full((8, D, 256))],
        out_specs=[_tile(), _full((8, D, 256)), _full((8, 2 * D))],
        out_shape=[SDS((T, D), f32), SDS((8, D, 256), bf16), SDS((8, 2 * D), f32)],
        scratch_shapes=[pltpu.VMEM((8, D, 256), f32)],
        compiler_params=_cp(dimension_semantics=("arbitrary",)),
    )(dh, y, z, wglu)


def s5_pre_bwd(xp, g, du, dy, d, dh, bg=()):
    def body(x_ref, g_ref, du_ref, dy_ref, d_ref, dh_ref, dx_ref, dg_ref, dd_ref):
        i = pl.program_id(0)

        @pl.when(i == 0)
        def _():
            dg_ref[...] = jnp.zeros_like(dg_ref)
            dd_ref[...] = jnp.zeros_like(dd_ref)

        x = x_ref[...]
        g = g_ref[...]
        dy = dy_ref[...]
        hn, _ = _rms(x, g)
        dhn = du_ref[...] + d_ref[...] * dy
        dx, dgt = _rms_bwd(x, g, dhn)
        dx_ref[...] = dh_ref[...] + dx
        dg_ref[...] += _colsum8(dgt)
        dd_ref[...] += _colsum8(dy * hn)

    return _call(
        bg, body, name="s5_pre_bwd", grid=(NT,),
        in_specs=[_tile(), _full((1, D)), _tile(), _tile(), _full((1, D)), _tile()],
        out_specs=[_tile(), _full((8, D)), _full((8, D))],
        out_shape=[SDS((T, D), f32), SDS((8, D), f32), SDS((8, D), f32)],
        compiler_params=_cp(dimension_semantics=("arbitrary",)),
    )(xp, g, du, dy, d, dh)


TMF = 1024


def mlp_fwd(h, g, w_in, w_out, layer, bg=()):
    def body(h_ref, g_ref, wi_ref, wo_ref, hm_ref, out_ref, acc):
        j = pl.program_id(1)

        @pl.when(j == 0)
        def _():
            hm, _ = _rms(h_ref[...], g_ref[...])
            hm_ref[...] = hm.astype(bf16)
            acc[...] = jnp.zeros_like(acc)

        a = jnp.maximum(_dot(hm_ref[...], wi_ref[...]), 0.0)
        acc[...] += _dot((a * a).astype(bf16), wo_ref[...])

        @pl.when(j == NDEV - 1)
        def _():
            out_ref[...] = h_ref[...] + acc[...]

    return _call(
        bg, body, name=f"mlp_fwd{layer}", grid=(T // TMF, NDEV),
        in_specs=[pl.BlockSpec((TMF, D), lambda i, j: (i, 0)),
                  pl.BlockSpec((1, D), lambda i, j: (0, 0)),
                  pl.BlockSpec((None, D, D_FF_SHARD), lambda i, j: (j, 0, 0)),
                  pl.BlockSpec((None, D_FF_SHARD, D), lambda i, j: (j, 0, 0))],
        out_specs=[pl.BlockSpec((TMF, D), lambda i, j: (i, 0)), pl.BlockSpec((TMF, D), lambda i, j: (i, 0))],
        out_shape=[SDS((T, D), bf16), SDS((T, D), f32)],
        scratch_shapes=[pltpu.VMEM((TMF, D), f32)],
        compiler_params=_cp(dimension_semantics=("arbitrary", "arbitrary")),
    )(h, g, w_in, w_out)


def mlp_bwd(h, hm, g, dout, w_in, w_out, layer, bg=()):
    last = NDEV - 1

    def body(h_ref, hm_ref, g_ref, do_ref, wi_ref, wo_ref, dh_ref, dwi_ref, dwo_ref, dg_ref, dhm, awi, awo):
        j = pl.program_id(0)
        i = pl.program_id(1)
        rows = pl.ds(pl.multiple_of(i * TM, TM), TM)

        @pl.when(i == 0)
        def _():
            awi[...] = jnp.zeros_like(awi)
            awo[...] = jnp.zeros_like(awo)

        hm_ = hm_ref[...]
        dob = do_ref[...].astype(bf16)
        r = jnp.maximum(_dot(hm_, wi_ref[...]), 0.0)
        dz = (_dot_nt(dob, wo_ref[...]) * (2.0 * r)).astype(bf16)
        awo[...] += _dot_tn((r * r).astype(bf16), dob)
        awi[...] += _dot_tn(hm_, dz)
        part = _dot_nt(dz, wi_ref[...])

        @pl.when(j == 0)
        def _():
            dhm[rows, :] = part

        @pl.when(j > 0)
        def _():
            dhm[rows, :] += part

        @pl.when(i == NT - 1)
        def _():
            dwi_ref[...] = awi[...].astype(bf16)
            dwo_ref[...] = awo[...].astype(bf16)

        @pl.when(j == last)
        def _():
            @pl.when(i == 0)
            def _():
                dg_ref[...] = jnp.zeros_like(dg_ref)
            dx, dgt = _rms_bwd(h_ref[...], g_ref[...], dhm[rows, :])
            dh_ref[...] = do_ref[...] + dx
            dg_ref[...] += _colsum8(dgt)

    late = lambda j, i: (jnp.where(j == last, i, 0), 0)
    return _call(
        bg, body, name=f"mlp_bwd{layer}", grid=(NDEV, NT),
        in_specs=[pl.BlockSpec((TM, D), late),
                  pl.BlockSpec((TM, D), lambda j, i: (i, 0)),
                  pl.BlockSpec((1, D), lambda j, i: (0, 0)),
                  pl.BlockSpec((TM, D), lambda j, i: (i, 0)),
                  pl.BlockSpec((None, D, D_FF_SHARD), lambda j, i: (j, 0, 0)),
                  pl.BlockSpec((None, D_FF_SHARD, D), lambda j, i: (j, 0, 0))],
        out_specs=[pl.BlockSpec((TM, D), late),
                   pl.BlockSpec((None, D, D_FF_SHARD), lambda j, i: (j, 0, 0)),
                   pl.BlockSpec((None, D_FF_SHARD, D), lambda j, i: (j, 0, 0)),
                   pl.BlockSpec((8, D), lambda j, i: (0, 0))],
        out_shape=[SDS((T, D), f32), SDS((NDEV, D, D_FF_SHARD), bf16), SDS((NDEV, D_FF_SHARD, D), bf16),
                   SDS((8, D), f32)],
        scratch_shapes=[pltpu.VMEM((T, D), f32), pltpu.VMEM((D, D_FF_SHARD), f32), pltpu.VMEM((D_FF_SHARD, D), f32)],
        compiler_params=_cp(dimension_semantics=("arbitrary", "arbitrary")),
    )(h, hm, g, dout, w_in, w_out)


def _spread4():
    r = lax.broadcasted_iota(jnp.int32, (256, D), 0)
    c = lax.broadcasted_iota(jnp.int32, (256, D), 1)
    return ((c // 256 == r // HEAD_DIM) & (c % HEAD_DIM == r % HEAD_DIM)).astype(bf16)


def attn_pre(h, g_kv, g_mix, wkv, bkv, spread, wq, bq):
    def body(h_ref, gkv_ref, gm_ref, wkv_ref, bkv_ref, sp_ref, wq_ref, bq_ref, kvn_ref, hn_ref, k_ref, v_ref, q_ref):
        h_ = h_ref[...]
        kvn = _rms(h_, gkv_ref[...])[0].astype(bf16)
        hn = _rms(h_, gm_ref[...])[0].astype(bf16)
        kvn_ref[...] = kvn
        hn_ref[...] = hn
        kv = (_dot(kvn, wkv_ref[...]) + bkv_ref[...]).astype(bf16)
        k_ref[...] = _dot(kv[:, :256], sp_ref[...]).astype(bf16)
        v_ref[...] = _dot(kv[:, 256:], sp_ref[...]).astype(bf16)
        q_ref[...] = (_dot(hn, wq_ref[...]) + bq_ref[...]).astype(bf16)

    return pl.pallas_call(
        body, name="attn_pre", grid=(NT,),
        in_specs=[_tile(), _full((1, D)), _full((1, D)), _full((D, 512)), _full((1, 512)), _full((256, D)),
                  _full((D, D)), _full((1, D))],
        out_specs=[_tile()] * 5,
        out_shape=[SDS((T, D), bf16)] * 5,
        compiler_params=_cp(dimension_semantics=("arbitrary",)),
    )(h, g_kv, g_mix, wkv, bkv, spread, wq, bq)


def _attn_specs():
    cur = pl.BlockSpec((TM, 256), lambda j, n: (n, j))
    prev = pl.BlockSpec((BLK, 256), lambda j, n: (jnp.maximum(n * (TM // BLK) - 1, 0), j))
    return cur, prev


def _head_mask(g):
    lane = lax.broadcasted_iota(jnp.int32, (1, 256), 1)
    return (lane >= g * HEAD_DIM) & (lane < (g + 1) * HEAD_DIM)


def _stack_heads(t):
    return jnp.concatenate([jnp.where(_head_mask(g), t, 0) for g in range(Q_PER_KV)], axis=0)


def _unstack_heads(t):
    out = jnp.where(_head_mask(0), t[0:BLK], 0.0)
    for g in range(1, Q_PER_KV):
        out = out + jnp.where(_head_mask(g), t[g * BLK:(g + 1) * BLK], 0.0)
    return out


def _attn_probs(qs, k2, sinks, first):
    rows = Q_PER_KV * BLK
    s = _dot_nt(qs, k2) * (1.0 / math.sqrt(HEAD_DIM))
    qi = jnp.bitwise_and(lax.broadcasted_iota(jnp.int32, (rows, 2 * BLK), 0), BLK - 1)
    kj = lax.broadcasted_iota(jnp.int32, (rows, 2 * BLK), 1)
    diff = qi + BLK - kj
    valid = (diff >= 0) & (diff < BLK) & (jnp.logical_not(first) | (kj >= BLK))
    s = jnp.where(valid, s, -jnp.inf)
    rb = lax.broadcasted_iota(jnp.int32, (rows, 1), 0)
    sink = jnp.where(rb < BLK, sinks[0], jnp.where(rb < 2 * BLK, sinks[1], jnp.where(rb < 3 * BLK, sinks[2], sinks[3])))
    m = jnp.maximum(jnp.max(s, axis=-1, keepdims=True), sink)
    p = jnp.exp(s - m)
    ps = jnp.exp(sink - m)
    denom = jnp.sum(p, axis=-1, keepdims=True) + ps
    return p / denom, ps / denom


def attn_core_fwd(q, k4, v4, sinks, bg=()):
    nb = TM // BLK

    def body(sink_ref, q_ref, kc_ref, kp_ref, vc_ref, vp_ref, o_ref):
        j = pl.program_id(0)
        n = pl.program_id(1)
        sk = [sink_ref[j * Q_PER_KV + g] for g in range(Q_PER_KV)]
        for b in range(nb):
            qb = q_ref[b * BLK:(b + 1) * BLK, :]
            if b == 0:
                k2 = jnp.concatenate([kp_ref[...], kc_ref[0:BLK, :]], axis=0)
                v2 = jnp.concatenate([vp_ref[...], vc_ref[0:BLK, :]], axis=0)
                first = n == 0
            else:
                k2 = kc_ref[(b - 1) * BLK:(b + 1) * BLK, :]
                v2 = vc_ref[(b - 1) * BLK:(b + 1) * BLK, :]
                first = False
            a, _ = _attn_probs(_stack_heads(qb), k2, sk, first)
            o_ref[b * BLK:(b + 1) * BLK, :] = _unstack_heads(_dot(a.astype(bf16), v2)).astype(bf16)

    cur, prev = _attn_specs()
    return _call(
        bg, body, name="attn_core_fwd", grid=(N_KV, NT),
        in_specs=[pl.BlockSpec(memory_space=pltpu.SMEM), cur, cur, prev, cur, prev],
        out_specs=cur, out_shape=SDS((T, D), bf16),
        compiler_params=_cp(dimension_semantics=("arbitrary", "arbitrary")),
    )(sinks, q, k4, k4, v4, v4)


def attn_post(h, o, wo, bo):
    def body(h_ref, o_ref, w_ref, b_ref, out_ref):
        out_ref[...] = h_ref[...] + _dot(o_ref[...], w_ref[...]) + b_ref[...]

    return pl.pallas_call(
        body, name="attn_post", grid=(NT,), in_specs=[_tile(), _tile(), _full((D, D)), _full((1, D))],
        out_specs=_tile(), out_shape=SDS((T, D), f32), compiler_params=_cp(dimension_semantics=("arbitrary",)),
    )(h, o, wo, bo)


def attn_bwd_pre(dh, o, wo, bg=()):
    def body(dh_ref, o_ref, w_ref, do_ref, dw_ref, db_ref, acc):
        i = pl.program_id(0)

        @pl.when(i == 0)
        def _():
            acc[...] = jnp.zeros_like(acc)
            db_ref[...] = jnp.zeros_like(db_ref)

        dh_ = dh_ref[...]
        dhb = dh_.astype(bf16)
        do_ref[...] = _dot_nt(dhb, w_ref[...]).astype(bf16)
        acc[...] += _dot_tn(o_ref[...], dhb)
        db_ref[...] += _colsum8(dh_)

        @pl.when(i == NT - 1)
        def _():
            dw_ref[...] = acc[...].astype(bf16)

    return _call(
        bg, body, name="attn_bwd_pre", grid=(NT,), in_specs=[_tile(), _tile(), _full((D, D))],
        out_specs=[_tile(), _full((D, D)), _full((8, D))],
        out_shape=[SDS((T, D), bf16), SDS((D, D), bf16), SDS((8, D), f32)],
        scratch_shapes=[pltpu.VMEM((D, D), f32)],
        compiler_params=_cp(dimension_semantics=("arbitrary",)),
    )(dh, o, wo)


def attn_core_bwd(q, do, k4, v4, sinks, bg=()):
    nb = TM // BLK

    def body(sink_ref, q_ref, do_ref, kc_ref, kp_ref, vc_ref, vp_ref, dq_ref, dk_ref, dv_ref, ds_ref):
        j = pl.program_id(0)
        n = pl.program_id(1)

        @pl.when(n == 0)
        def _():
            dk_ref[...] = jnp.zeros_like(dk_ref)
            dv_ref[...] = jnp.zeros_like(dv_ref)
            ds_ref[...] = jnp.zeros_like(ds_ref)

        lane8 = lax.broadcasted_iota(jnp.int32, (8, 128), 1)
        row8 = lax.broadcasted_iota(jnp.int32, (8, 128), 0)
        sk = [sink_ref[j * Q_PER_KV + g] for g in range(Q_PER_KV)]
        for b in range(nb):
            qs = _stack_heads(q_ref[b * BLK:(b + 1) * BLK, :])
            dos = _stack_heads(do_ref[b * BLK:(b + 1) * BLK, :])
            if b == 0:
                k2 = jnp.concatenate([kp_ref[...], kc_ref[0:BLK, :]], axis=0)
                v2 = jnp.concatenate([vp_ref[...], vc_ref[0:BLK, :]], axis=0)
                first = n == 0
            else:
                k2 = kc_ref[(b - 1) * BLK:(b + 1) * BLK, :]
                v2 = vc_ref[(b - 1) * BLK:(b + 1) * BLK, :]
                first = False
            a, asink = _attn_probs(qs, k2, sk, first)
            dp = _dot_nt(dos, v2)
            dd = jnp.sum(a * dp, axis=-1, keepdims=True)
            dsc = (a * (dp - dd) * (1.0 / math.sqrt(HEAD_DIM))).astype(bf16)
            t = asink * dd
            for g in range(Q_PER_KV):
                dsink = -jnp.sum(t[g * BLK:(g + 1) * BLK], axis=0, keepdims=True)
                ds_ref[...] += jnp.where((lane8 == g) & (row8 == 0), jnp.broadcast_to(dsink, (8, 128)), 0.0)
            dq_ref[b * BLK:(b + 1) * BLK, :] = _unstack_heads(_dot(dsc, k2))
            dk2 = _dot_tn(dsc, qs)
            dv2 = _dot_tn(a.astype(bf16), dos)
            cur = pl.ds(pl.multiple_of(n * TM + b * BLK, BLK), BLK)
            dk_ref[cur, :] += dk2[BLK:, :]
            dv_ref[cur, :] += dv2[BLK:, :]
            if b == 0:
                @pl.when(n > 0)
                def _():
                    prv = pl.ds(pl.multiple_of(n * TM - BLK, BLK), BLK)
                    dk_ref[prv, :] += dk2[:BLK, :]
                    dv_ref[prv, :] += dv2[:BLK, :]
            else:
                prv = pl.ds(pl.multiple_of(n * TM + (b - 1) * BLK, BLK), BLK)
                dk_ref[prv, :] += dk2[:BLK, :]
                dv_ref[prv, :] += dv2[:BLK, :]

    cur, prev = _attn_specs()
    col = pl.BlockSpec((T, 256), lambda j, n: (0, j))
    return _call(
        bg, body, name="attn_core_bwd", grid=(N_KV, NT),
        in_specs=[pl.BlockSpec(memory_space=pltpu.SMEM), cur, cur, cur, prev, cur, prev],
        out_specs=[cur, col, col, pl.BlockSpec((None, 8, 128), lambda j, n: (j, 0, 0))],
        out_shape=[SDS((T, D), f32), SDS((T, D), f32), SDS((T, D), f32), SDS((N_KV, 8, 128), f32)],
        compiler_params=_cp(dimension_semantics=("arbitrary", "arbitrary")),
    )(sinks, q, do, k4, k4, v4, v4)


def attn_bwd_q(h, dh, dq, hn, g_mix, wq):
    def body(h_ref, dh_ref, dq_ref, hn_ref, gm_ref, wq_ref, out_ref, dwq_ref, dbq_ref, dgm_ref, aq):
        i = pl.program_id(0)

        @pl.when(i == 0)
        def _():
            aq[...] = jnp.zeros_like(aq)
            dbq_ref[...] = jnp.zeros_like(dbq_ref)
            dgm_ref[...] = jnp.zeros_like(dgm_ref)

        dq_ = dq_ref[...]
        dqb = dq_.astype(bf16)
        aq[...] += _dot_tn(hn_ref[...], dqb)
        dbq_ref[...] += _colsum8(dq_)
        dx, dg = _rms_bwd(h_ref[...], gm_ref[...], _dot_nt(dqb, wq_ref[...]))
        out_ref[...] = dh_ref[...] + dx
        dgm_ref[...] += _colsum8(dg)

        @pl.when(i == NT - 1)
        def _():
            dwq_ref[...] = aq[...].astype(bf16)

    vec = _full((8, D))
    mat = _full((D, D))
    return pl.pallas_call(
        body, name="attn_bwd_q", grid=(NT,),
        in_specs=[_tile()] * 4 + [_full((1, D)), mat],
        out_specs=[_tile(), mat, vec, vec],
        out_shape=[SDS((T, D), f32), SDS((D, D), bf16), SDS((8, D), f32), SDS((8, D), f32)],
        scratch_shapes=[pltpu.VMEM((D, D), f32)],
        compiler_params=_cp(dimension_semantics=("arbitrary",)),
    )(h, dh, dq, hn, g_mix, wq)


def attn_bwd_kv(h, dh, dk4, dv4, kvn, g_kv, wkv, spread):
    def body(h_ref, dh_ref, dk_ref, dv_ref, kvn_ref, gkv_ref, wkv_ref, sp_ref, out_ref, dw_ref, db_ref, dgkv_ref, acc):
        i = pl.program_id(0)

        @pl.when(i == 0)
        def _():
            for r in (acc, db_ref, dgkv_ref):
                r[...] = jnp.zeros_like(r)

        dkv = jnp.concatenate([_dot_nt(dk_ref[...].astype(bf16), sp_ref[...]),
                               _dot_nt(dv_ref[...].astype(bf16), sp_ref[...])], axis=1)
        dkvb = dkv.astype(bf16)
        acc[...] += _dot_tn(kvn_ref[...], dkvb)
        db_ref[...] += _colsum8(dkv)
        dx, dg = _rms_bwd(h_ref[...], gkv_ref[...], _dot_nt(dkvb, wkv_ref[...]))
        out_ref[...] = dh_ref[...] + dx
        dgkv_ref[...] += _colsum8(dg)

        @pl.when(i == NT - 1)
        def _():
            dw_ref[...] = acc[...].astype(bf16)

    return pl.pallas_call(
        body, name="attn_bwd_kv", grid=(NT,),
        in_specs=[_tile()] * 5 + [_full((1, D)), _full((D, 512)), _full((256, D))],
        out_specs=[_tile(), _full((D, 512)), _full((8, 512)), _full((8, D))],
        out_shape=[SDS((T, D), f32), SDS((D, 512), bf16), SDS((8, 512), f32), SDS((8, D), f32)],
        scratch_shapes=[pltpu.VMEM((D, 512), f32)],
        compiler_params=_cp(dimension_semantics=("arbitrary",)),
    )(h, dh, dk4, dv4, kvn, g_kv, wkv, spread)


def final_loss(h, g, target):
    def body(h_ref, g_ref, t_ref, loss_ref, dh_ref, dg_ref):
        i = pl.program_id(0)

        @pl.when(i == 0)
        def _():
            loss_ref[...] = jnp.zeros_like(loss_ref)
            dg_ref[...] = jnp.zeros_like(dg_ref)

        h_ = h_ref[...]
        g_ = g_ref[...]
        y, _ = _rms(h_, g_)
        diff = y - t_ref[...]
        per_tok = jnp.mean(diff * diff, axis=-1, keepdims=True)
        tot = 0.5 * jnp.sum(per_tok, axis=0, keepdims=True)
        lane = lax.broadcasted_iota(jnp.int32, (8, 128), 1)
        row = lax.broadcasted_iota(jnp.int32, (8, 128), 0)
        loss_ref[...] += jnp.where((lane == 0) & (row == 0), jnp.broadcast_to(tot, (8, 128)), 0.0)
        dx, dgt = _rms_bwd(h_, g_, diff * (1.0 / D))
        dh_ref[...] = dx
        dg_ref[...] += _colsum8(dgt)

    return pl.pallas_call(
        body, name="final_loss", grid=(NT,), in_specs=[_tile(), _full((1, D)), _tile()],
        out_specs=[_full((8, 128)), _tile(), _full((8, D))],
        out_shape=[SDS((8, 128), f32), SDS((T, D), f32), SDS((8, D), f32)],
        compiler_params=_cp(dimension_semantics=("arbitrary",)),
    )(h, g, target)


def _to_chunked(a):
    return a.reshape(S5_CH, S5_STEPS, a.shape[-1]).transpose(1, 0, 2).reshape(T, a.shape[-1])


def _from_chunked(a):
    return a.reshape(S5_STEPS, S5_CH, a.shape[-1]).transpose(1, 0, 2).reshape(T, a.shape[-1])


def _rep4(w):
    return jnp.broadcast_to(w.reshape(w.shape[0], N_KV, 1, HEAD_DIM), (w.shape[0], N_KV, Q_PER_KV, HEAD_DIM)).reshape(
        w.shape[0], N_KV * Q_PER_KV * HEAD_DIM)


def _fold4(w):
    return w.reshape(w.shape[0], N_KV, Q_PER_KV, HEAD_DIM).sum(axis=2).reshape(w.shape[0], N_KV * HEAD_DIM)


def fwd_bwd(x, target, p, shards, core):
    row = lambda v: v.reshape(1, -1)
    (lam, bm, cm), prep_vjp = jax.vjp(s5_discretize, p["s5_a_re"][0], p["s5_a_im"][0], p["s5_log_dt"][0],
                                      p["s5_b_re"][0], p["s5_b_im"][0], p["s5_c_re"][0], p["s5_c_im"][0])
    bmb, cmb = bm.astype(bf16), cm.astype(bf16)
    g_mix0, g_mix1 = row(p["norm_mix"][0]), row(p["norm_mix"][1])
    g_mlp0, g_mlp1 = row(p["norm_mlp"][0]), row(p["norm_mlp"][1])
    g_kv, g_fin = row(p["norm_kv"]), row(p["norm_final"])
    bq, bo = p["b_q"], p["b_o"]
    bkv = row(p["b_kv"])
    spread = _spread4()
    sinks = p["sinks"].reshape(16)

    def reduce_pairs(names, bg):
        return [add_pairs(g, r, core, f"add_pairs_{n}") for n, g, r in zip(names, bg.arrs, bg.result)]

    xp = _to_chunked(x)
    hn0 = s5_pre(xp, g_mix0)
    ga = BgGather([shards["s5_w_glu"], shards["vecs"], shards["w_in0"]])
    ys = s5_core_fwd(hn0, bmb, lam, cmb, bg=[ga])
    wglu, gvec, win0 = ga.result
    d_skip = gvec[:, 0, :128].reshape(1, D)
    bglu = gvec[:, 0, 128:].reshape(1, 2 * D)
    gb = BgGather([shards["w_out0"]])
    y, z, h1 = s5_post(ys, xp, g_mix0, d_skip, wglu, bglu, bg=[gb])
    wout0, = gb.result
    gc = BgGather([shards["w_kv"], shards["w_q"], shards["w_o"], shards["w_in1"]])
    hm0, h2p = mlp_fwd(h1, g_mlp0, win0, wout0, 0, bg=[gc])
    wkv, wq, wo, win1 = gc.result
    wkv, wq, wo = wkv.reshape(D, 512), wq.reshape(D, D), wo.reshape(D, D)
    h2 = _from_chunked(h2p)
    kvn, hn1, k4, v4, q = attn_pre(h2, g_kv, g_mix1, wkv, bkv, spread, wq, bq)
    gd = BgGather([shards["w_out1"]])
    o = attn_core_fwd(q, k4, v4, sinks, bg=[gd])
    wout1, = gd.result
    h3 = attn_post(h2, o, wo, bo)
    hm1, h4 = mlp_fwd(h3, g_mlp1, win1, wout1, 1)
    loss, dh4, dg_fin = final_loss(h4, g_fin, target)

    big = {}
    dh3, dwin1, dwout1, dg_mlp1 = mlp_bwd(h3, hm1, g_mlp1, dh4, win1, wout1, 1)
    pa = BgPair([dwin1, dwout1])
    do, dwo, dbo = attn_bwd_pre(dh3, o, wo, bg=[pa])
    ca = BgChips(reduce_pairs(["w_in1", "w_out1"], pa))
    dq, dk4, dv4, dsink = attn_core_bwd(q, do, k4, v4, sinks, bg=[ca])
    big["w_in1"], big["w_out1"] = zip(ca.arrs, ca.result)
    dh2, dwq, dbq, dg_mix1 = attn_bwd_q(h2, dh3, dq, hn1, g_mix1, wq)
    dh2, dwkv, dbkv, dg_kv = attn_bwd_kv(h2, dh2, dk4, dv4, kvn, g_kv, wkv, spread)
    pb = BgPair([dwkv.reshape(NDEV, 128, 512), dwq.reshape(NDEV, 128, D), dwo.reshape(NDEV, 128, D)])
    dh2p = _to_chunked(dh2)
    dh1, dwin0, dwout0, dg_mlp0 = mlp_bwd(h1, hm0, g_mlp0, dh2p, win0, wout0, 0, bg=[pb])
    cb = BgChips(reduce_pairs(["w_kv", "w_q", "w_o"], pb))
    pc = BgPair([dwin0, dwout0])
    dy, dwglu, dbglu = s5_post_bwd(dh1, y, z, wglu, bg=[cb, pc])
    big["w_kv"], big["w_q"], big["w_o"] = zip(cb.arrs, cb.result)
    cc = BgChips(reduce_pairs(["w_in0", "w_out0"], pc))
    pd = BgPair([dwglu])
    du, dbm, dcmt, dlam = s5_core_bwd(hn0, dy, bmb, lam, cmb, bg=[cc, pd])
    big["w_in0"], big["w_out0"] = zip(cc.arrs, cc.result)
    cd = BgChips(reduce_pairs(["s5_w_glu"], pd))
    dxp, dg_mix0, dd = s5_pre_bwd(xp, g_mix0, du, dy, d_skip, dh1, bg=[cd])
    big["s5_w_glu"], = zip(cd.arrs, cd.result)
    grad_x = _from_chunked(dxp)
    da_re, da_im, dlog_dt, db_re, db_im, dc_re, dc_im = prep_vjp((dlam, dbm, dcmt.transpose(0, 2, 1)))

    def lanes(v_):
        v_ = v_.reshape(1, -1)
        return jnp.pad(v_, ((0, 0), (0, D - v_.shape[1])))

    small = jnp.concatenate([
        dg_mix0[0:1], dg_mix1[0:1], dg_mlp0[0:1], dg_mlp1[0:1], dg_kv[0:1], dg_fin[0:1], dd[0:1], dbq[0:1], dbo[0:1],
        dbglu[0:1].reshape(2, D), lanes(dbkv[0:1]),
        lanes(dsink[:, 0, :Q_PER_KV]), lanes(dlog_dt), lanes(loss[0:1, 0:1]), jnp.zeros((1, D), f32),
        da_re.reshape(4, D), da_im.reshape(4, D),
        db_re.transpose(0, 2, 1).reshape(64, D), db_im.transpose(0, 2, 1).reshape(64, D),
        dc_re.reshape(64, D), dc_im.reshape(64, D)], axis=0)
    return loss, grad_x, small, big


_ANY = pl.BlockSpec(memory_space=pl.ANY)


def _pos():
    return lax.axis_index("x"), lax.axis_index("y"), lax.axis_index("c")


def _other_chips(x, y):
    return [(1 - x, y), (x, 1 - y), (1 - x, 1 - y)]


def all_gather(arrs):
    n = len(arrs)

    def body(*refs):
        ins, outs = refs[:n], refs[n:2 * n]
        send_sems, recv_sems, local_sems = refs[2 * n:]
        x, y, c = _pos()
        me, sib = (x, y, c), (x, y, 1 - c)
        chips = _other_chips(x, y)

        def copy(a, k, block, to, src=None):
            dst = outs[a].at[4 * block[0] + 2 * block[1] + block[2]]
            return pltpu.make_async_remote_copy(
                src_ref=dst if src is None else src, dst_ref=dst, send_sem=send_sems.at[a, k],
                recv_sem=recv_sems.at[a, k], device_id=to, device_id_type=MESH)

        mine = [pltpu.make_async_copy(ins[a], outs[a].at[4 * x + 2 * y + c], local_sems.at[a]) for a in range(n)]
        for cp in mine:
            cp.start()
        first = []
        for a in range(n):
            first.append(copy(a, 0, me, sib, src=ins[a]))
            first += [copy(a, 1 + j, me, (*chip, c), src=ins[a]) for j, chip in enumerate(chips)]
        for cp in first:
            cp.start()
        passed = []
        for j, chip in enumerate(chips):
            for a in range(n):
                copy(a, 1 + j, (*chip, c), me).wait_recv()
                cp = copy(a, 4 + j, (*chip, c), sib)
                cp.start()
                passed.append(cp)
        for a in range(n):
            copy(a, 0, sib, me).wait_recv()
            for j, chip in enumerate(chips):
                copy(a, 4 + j, (*chip, 1 - c), me).wait_recv()
        for cp in first + passed:
            cp.wait_send()
        for cp in mine:
            cp.wait()

    return pl.pallas_call(
        body, name="all_gather", in_specs=[_ANY] * n, out_specs=[_ANY] * n,
        out_shape=[SDS((NDEV,) + a.shape, a.dtype) for a in arrs],
        scratch_shapes=[pltpu.SemaphoreType.DMA((n, 7)), pltpu.SemaphoreType.DMA((n, 7)),
                        pltpu.SemaphoreType.DMA((n,))],
    )(*arrs)


def rs_pair(grads):
    n = len(grads)

    def body(*refs):
        ins, outs = refs[:n], refs[n:2 * n]
        send_sems, recv_sems = refs[2 * n:]
        x, y, c = _pos()
        cps = []
        for a in range(n):
            for k in range(4):
                cps.append(pltpu.make_async_remote_copy(
                    src_ref=ins[a].at[2 * k + 1 - c], dst_ref=outs[a].at[k], send_sem=send_sems.at[a, k],
                    recv_sem=recv_sems.at[a, k], device_id=(x, y, 1 - c), device_id_type=MESH))
        for cp in cps:
            cp.start()
        for cp in cps:
            cp.wait_recv()
        for cp in cps:
            cp.wait_send()

    return pl.pallas_call(
        body, name="rs_pair", in_specs=[_ANY] * n, out_specs=[_ANY] * n,
        out_shape=[SDS((4,) + g.shape[1:], g.dtype) for g in grads],
        scratch_shapes=[pltpu.SemaphoreType.DMA((n, 4)), pltpu.SemaphoreType.DMA((n, 4))],
    )(*grads)


def rs_chips(parts):
    n = len(parts)

    def body(*refs):
        ins, outs = refs[:n], refs[n:2 * n]
        send_sems, recv_sems = refs[2 * n:]
        x, y, c = _pos()
        cps = []
        for a in range(n):
            for r, (px, py) in enumerate(_other_chips(x, y)):
                cps.append(pltpu.make_async_remote_copy(
                    src_ref=ins[a].at[2 * px + py], dst_ref=outs[a].at[r], send_sem=send_sems.at[a, r],
                    recv_sem=recv_sems.at[a, r], device_id=(px, py, c), device_id_type=MESH))
        for cp in cps:
            cp.start()
        for cp in cps:
            cp.wait_recv()
        for cp in cps:
            cp.wait_send()

    return pl.pallas_call(
        body, name="rs_chips", in_specs=[_ANY] * n, out_specs=[_ANY] * n,
        out_shape=[SDS((3,) + g.shape[1:], g.dtype) for g in parts],
        scratch_shapes=[pltpu.SemaphoreType.DMA((n, 3)), pltpu.SemaphoreType.DMA((n, 3))],
    )(*parts)


def _row_tile(r, c):
    return min(r, max(8, (256 * 1024) // c))


def add_pairs(g, r1, core, name):
    _, R, C = g.shape
    tr = _row_tile(R, C)

    def body(core_ref, g_ref, r_ref, o_ref):
        o_ref[...] = (g_ref[...].astype(f32) + r_ref[...].astype(f32)).astype(bf16)

    return pl.pallas_call(
        body, name=name, out_shape=SDS((4, R, C), bf16),
        grid_spec=pltpu.PrefetchScalarGridSpec(
            num_scalar_prefetch=1, grid=(4, R // tr),
            in_specs=[pl.BlockSpec((None, tr, C), lambda k, i, core: (2 * k + core[0], i, 0)),
                      pl.BlockSpec((None, tr, C), lambda k, i, core: (k, i, 0))],
            out_specs=pl.BlockSpec((None, tr, C), lambda k, i, core: (k, i, 0))),
        compiler_params=_cp(dimension_semantics=("arbitrary", "arbitrary")),
    )(core, g, r1)


def _adamw(w, g, m, v):
    m = ADAM_B1 * m + (1.0 - ADAM_B1) * g
    v = ADAM_B2 * v + (1.0 - ADAM_B2) * (g * g)
    m_hat = m / (1.0 - ADAM_B1 ** ADAM_STEP)
    v_hat = v / (1.0 - ADAM_B2 ** ADAM_STEP)
    delta = -ADAM_LR * (m_hat / (jnp.sqrt(v_hat) + ADAM_EPS) + ADAM_WD * w)
    return delta, m, v


def adam_big(w, m, v, part, r2, chip, name, layer=0, prev=None):
    L, R, C = w.shape
    tr = _row_tile(R, C)

    def body(chip_ref, w_ref, m_ref, v_ref, p_ref, r_ref, *rest):
        g_out, d_out, m_out, v_out = rest[-4:]
        g = p_ref[...].astype(f32) + r_ref[0].astype(f32) + r_ref[1].astype(f32) + r_ref[2].astype(f32)
        d, m_, v_ = _adamw(w_ref[...], g, m_ref[...], v_ref[...])
        g_out[...] = g
        d_out[...] = d
        m_out[...] = m_
        v_out[...] = v_

    blk = pl.BlockSpec((None, tr, C), lambda i, chip: (layer, i, 0))
    extra = [] if prev is None else list(prev)
    return pl.pallas_call(
        body, name=name, out_shape=[SDS((L, R, C), f32)] * 4,
        grid_spec=pltpu.PrefetchScalarGridSpec(
            num_scalar_prefetch=1, grid=(R // tr,),
            in_specs=[blk, blk, blk,
                      pl.BlockSpec((None, tr, C), lambda i, chip: (chip[0], i, 0)),
                      pl.BlockSpec((3, tr, C), lambda i, chip: (0, i, 0))] + [_ANY] * len(extra),
            out_specs=[blk] * 4),
        input_output_aliases={6 + k: k for k in range(len(extra))},
        compiler_params=_cp(dimension_semantics=("arbitrary",)),
    )(chip, w, m, v, part, r2, *extra)


def allreduce_small(buf):
    shp = buf.shape
    half = (shp[0] // 16) * 8
    parts = (pl.ds(0, half), pl.ds(half, shp[0] - half))

    def body(in_ref, out_ref, acc1, acc2, r0, r1, r2, send_sems, recv_sems):
        x, y, c = _pos()
        across = [(1 - x, y, c), (x, 1 - y, c)]

        def exchange(src, rcv, dst, copies):
            cps = [pltpu.make_async_remote_copy(
                src_ref=src.at[rows], dst_ref=rcv.at[rows], send_sem=send_sems.at[k], recv_sem=recv_sems.at[k],
                device_id=peer, device_id_type=MESH) for k, rows, peer in copies]
            for cp in cps:
                cp.start()
            for cp in cps:
                cp.wait()
            dst[...] = src[...] + rcv[...]

        exchange(in_ref, r0, acc1, [(0, pl.ds(0, shp[0]), (x, y, 1 - c))])
        exchange(acc1, r1, acc2, [(1, parts[0], across[0]), (2, parts[1], across[1])])
        exchange(acc2, r2, out_ref, [(3, parts[0], across[1]), (4, parts[1], across[0])])

    return pl.pallas_call(
        body, name="allreduce_small", out_shape=SDS(shp, f32),
        scratch_shapes=[pltpu.VMEM(shp, f32)] * 5 + [pltpu.SemaphoreType.DMA((5,)), pltpu.SemaphoreType.DMA((5,))],
    )(buf)


SMALL_ROWS = {'norm_mix': (0, 2, D), 'norm_mlp': (2, 2, D), 'norm_kv': (4, 1, D), 'norm_final': (5, 1, D),
              's5_d': (6, 1, D), 'b_q': (7, 1, D), 'b_o': (8, 1, D), 's5_b_glu': (9, 2, D), 'b_kv': (11, 1, 512),
              'sinks': (12, 1, 16), 's5_log_dt': (13, 1, 64), 's5_a_re': (16, 4, D), 's5_a_im': (20, 4, D),
              's5_b_re': (24, 64, D), 's5_b_im': (88, 64, D), 's5_c_re': (152, 64, D), 's5_c_im': (216, 64, D)}
LOSS_ROW = 14
ROW_PARAMS = ['norm_mix', 'norm_mlp', 'norm_kv', 'norm_final', 'b_q', 'b_o', 'b_kv', 'sinks', 's5_log_dt']
SHARD_PARAMS = ['s5_d', 's5_b_glu']
S5_PARAMS = ['s5_a_re', 's5_a_im', 's5_b_re', 's5_b_im', 's5_c_re', 's5_c_im']


def adam_small(dev, gsum, s5_grads, w, m, v):
    names = ROW_PARAMS + SHARD_PARAMS + S5_PARAMS
    n_g = len(ROW_PARAMS) + len(SHARD_PARAMS)

    def body(dev_ref, gs_ref, *refs):
        pos = [0]

        def take(k):
            r = refs[pos[0]:pos[0] + k]
            pos[0] += k
            return r

        g5 = take(len(S5_PARAMS))
        wr, mr, vr = take(len(names)), take(len(names)), take(len(names))
        g_out = take(n_g)
        d_out, m_out, v_out = take(len(names)), take(len(names)), take(len(names))
        dv = dev_ref[0]
        for i, n in enumerate(names):
            if n in S5_PARAMS:
                g = g5[S5_PARAMS.index(n)][...]
            elif n in SHARD_PARAMS:
                r0, _, _ = SMALL_ROWS[n]
                ln = wr[i].shape[1]
                g = jnp.zeros((1, ln), f32)
                for k in range(NDEV):
                    off = k * ln
                    piece = gs_ref[r0 + off // D:r0 + off // D + 1, off % D:off % D + ln]
                    g = g + jnp.where(dv == k, piece, 0.0)
                g_out[i][...] = g
            else:
                r0, nr, nl = SMALL_ROWS[n]
                g = gs_ref[r0:r0 + nr, 0:nl]
                g_out[i][...] = g
            d, m_, v_ = _adamw(wr[i][...], g, mr[i][...], vr[i][...])
            d_out[i][...] = d
            m_out[i][...] = m_
            v_out[i][...] = v_

    vm = pl.BlockSpec(memory_space=pltpu.VMEM)
    ins = [s5_grads[n] for n in S5_PARAMS] + [d[n] for d in (w, m, v) for n in names]
    shapes = [SDS(w[n].shape, f32) for n in names]
    res = pl.pallas_call(
        body, name="adam_small", in_specs=[pl.BlockSpec(memory_space=pltpu.SMEM)] + [vm] * (1 + len(ins)),
        out_specs=[vm] * (n_g + 3 * len(names)), out_shape=shapes[:n_g] + shapes * 3,
        compiler_params=_cp(),
    )(dev, gsum, *ins)
    g_o = dict(zip(names[:n_g], res[:n_g]))
    rest = res[n_g:]
    k = len(names)
    return g_o, dict(zip(names, rest[:k])), dict(zip(names, rest[k:2 * k])), dict(zip(names, rest[2 * k:]))


WEIGHTS = ['norm_mix', 'norm_mlp', 'norm_kv', 'norm_final', 's5_a_re', 's5_a_im', 's5_log_dt', 's5_b_re', 's5_b_im',
           's5_c_re', 's5_c_im', 's5_d', 's5_w_glu', 's5_b_glu', 'w_kv', 'b_kv', 'w_q', 'b_q', 'sinks', 'w_o', 'b_o',
           'w_mlp_in', 'w_mlp_out']
BIG = ['s5_w_glu', 'w_kv', 'w_q', 'w_o', 'w_mlp_in', 'w_mlp_out']
BIG_2D = {'s5_w_glu': (D, 256), 'w_kv': (128, 512), 'w_q': (128, D), 'w_o': (128, D), 'w_mlp_in': (2 * D, 512),
          'w_mlp_out': (2 * 512, D)}
SHARDED_SMALL = {'s5_d': D, 's5_b_glu': 2 * D}
SMALL = [n for n in WEIGHTS if n not in BIG]
SMALL_SIZE = {'norm_mix': 2 * D, 'norm_mlp': 2 * D, 'norm_kv': D, 'norm_final': D, 's5_a_re': 4096, 's5_a_im': 4096,
              's5_log_dt': 64, 's5_b_re': 65536, 's5_b_im': 65536, 's5_c_re': 65536, 's5_c_im': 65536, 's5_d': D,
              's5_b_glu': 2 * D, 'b_kv': 512, 'b_q': D, 'sinks': 16, 'b_o': D}


def _pack(vals):
    parts = []
    for n in SMALL:
        v = vals[n].reshape(-1).astype(f32)
        parts.append(jnp.pad(v, (0, (-v.shape[0]) % 128)))
    flat = jnp.concatenate(parts)
    flat = jnp.pad(flat, (0, (-flat.shape[0]) % 1024))
    return flat.reshape(-1, 128)


def _unpack(buf):
    flat = buf.reshape(-1)
    out, off = {}, 0
    for n in SMALL:
        sz = SMALL_SIZE[n]
        out[n] = flat[off:off + sz]
        off += sz + (-sz) % 128
    return out


def kernel(x, norm_mix, norm_mlp, norm_kv, norm_final, s5_a_re, s5_a_im, s5_log_dt, s5_b_re, s5_b_im, s5_c_re, s5_c_im, s5_d, s5_w_glu, s5_b_glu, w_kv, b_kv, w_q, b_q, sinks, w_o, b_o, w_mlp_in, w_mlp_out, loss_target, m_norm_mix, m_norm_mlp, m_norm_kv, m_norm_final, m_s5_a_re, m_s5_a_im, m_s5_log_dt, m_s5_b_re, m_s5_b_im, m_s5_c_re, m_s5_c_im, m_s5_d, m_s5_w_glu, m_s5_b_glu, m_w_kv, m_b_kv, m_w_q, m_b_q, m_sinks, m_w_o, m_b_o, m_w_mlp_in, m_w_mlp_out, v_norm_mix, v_norm_mlp, v_norm_kv, v_norm_final, v_s5_a_re, v_s5_a_im, v_s5_log_dt, v_s5_b_re, v_s5_b_im, v_s5_c_re, v_s5_c_im, v_s5_d, v_s5_w_glu, v_s5_b_glu, v_w_kv, v_b_kv, v_w_q, v_b_q, v_sinks, v_w_o, v_b_o, v_w_mlp_in, v_w_mlp_out):
    w = dict(norm_mix=norm_mix, norm_mlp=norm_mlp, norm_kv=norm_kv, norm_final=norm_final, s5_a_re=s5_a_re,
             s5_a_im=s5_a_im, s5_log_dt=s5_log_dt, s5_b_re=s5_b_re, s5_b_im=s5_b_im, s5_c_re=s5_c_re, s5_c_im=s5_c_im,
             s5_d=s5_d, s5_w_glu=s5_w_glu, s5_b_glu=s5_b_glu, w_kv=w_kv, b_kv=b_kv, w_q=w_q, b_q=b_q, sinks=sinks,
             w_o=w_o, b_o=b_o, w_mlp_in=w_mlp_in, w_mlp_out=w_mlp_out)
    m = dict(norm_mix=m_norm_mix, norm_mlp=m_norm_mlp, norm_kv=m_norm_kv, norm_final=m_norm_final, s5_a_re=m_s5_a_re,
             s5_a_im=m_s5_a_im, s5_log_dt=m_s5_log_dt, s5_b_re=m_s5_b_re, s5_b_im=m_s5_b_im, s5_c_re=m_s5_c_re,
             s5_c_im=m_s5_c_im, s5_d=m_s5_d, s5_w_glu=m_s5_w_glu, s5_b_glu=m_s5_b_glu, w_kv=m_w_kv, b_kv=m_b_kv,
             w_q=m_w_q, b_q=m_b_q, sinks=m_sinks, w_o=m_w_o, b_o=m_b_o, w_mlp_in=m_w_mlp_in, w_mlp_out=m_w_mlp_out)
    v = dict(norm_mix=v_norm_mix, norm_mlp=v_norm_mlp, norm_kv=v_norm_kv, norm_final=v_norm_final, s5_a_re=v_s5_a_re,
             s5_a_im=v_s5_a_im, s5_log_dt=v_s5_log_dt, s5_b_re=v_s5_b_re, s5_b_im=v_s5_b_im, s5_c_re=v_s5_c_re,
             s5_c_im=v_s5_c_im, s5_d=v_s5_d, s5_w_glu=v_s5_w_glu, s5_b_glu=v_s5_b_glu, w_kv=v_w_kv, b_kv=v_b_kv,
             w_q=v_w_q, b_q=v_b_q, sinks=v_sinks, w_o=v_w_o, b_o=v_b_o, w_mlp_in=v_w_mlp_in, w_mlp_out=v_w_mlp_out)
    xi, yi, ci = _pos()
    dev = 4 * xi + 2 * yi + ci
    core = ci.reshape(1).astype(jnp.int32)
    chip = (2 * xi + yi).reshape(1).astype(jnp.int32)

    shards = {
        "s5_w_glu": s5_w_glu[0].astype(bf16), "w_kv": w_kv.astype(bf16), "w_q": w_q[0].astype(bf16),
        "w_o": w_o[0].astype(bf16), "w_in0": w_mlp_in[0].astype(bf16), "w_in1": w_mlp_in[1].astype(bf16),
        "w_out0": w_mlp_out[0].astype(bf16), "w_out1": w_mlp_out[1].astype(bf16),
        "vecs": jnp.broadcast_to(jnp.concatenate([s5_d, s5_b_glu], axis=1), (8, 384)),
    }
    _, grad_x, grads, big = fwd_bwd(x[0], loss_target[0], {n: w[n] for n in SMALL}, shards, core)

    out_g, out_d, out_m, out_v = {}, {}, {}, {}
    for n in ("s5_w_glu", "w_kv", "w_q", "w_o"):
        shp = w[n].shape
        r3 = (1,) + BIG_2D[n]
        res = adam_big(w[n].reshape(r3), m[n].reshape(r3), v[n].reshape(r3), *big[n], chip, f"adam_{n}")
        out_g[n], out_d[n], out_m[n], out_v[n] = [r.reshape(shp) for r in res]
    for n, k in (("w_mlp_in", "w_in"), ("w_mlp_out", "w_out")):
        res = adam_big(w[n], m[n], v[n], *big[k + "1"], chip, f"adam_{k}1", layer=1)
        res = adam_big(w[n], m[n], v[n], *big[k + "0"], chip, f"adam_{k}0", layer=0, prev=res)
        out_g[n], out_d[n], out_m[n], out_v[n] = res

    gsum = allreduce_small(grads)
    loss = gsum[LOSS_ROW, 0]
    swapped = ("s5_b_re", "s5_b_im")
    swap = lambda a: a.transpose(0, 1, 3, 2)

    def kernel_side(d):
        d = {n: (d[n].reshape(1, -1) if d[n].ndim == 1 else d[n]) for n in SMALL}
        d.update({n: swap(d[n]) for n in swapped})
        return d

    s5_g = {}
    for n in S5_PARAMS:
        r0, nr, _ = SMALL_ROWS[n]
        s5_g[n] = gsum[r0:r0 + nr].reshape((1, 64, 16, 64) if n in swapped else w[n].shape)
        out_g[n] = s5_g[n]
    g_s, d_s, m_s, v_s = adam_small(dev.reshape(1).astype(jnp.int32), gsum, s5_g, kernel_side(w), kernel_side(m),
                                    kernel_side(v))
    for src, dst in ((g_s, out_g), (d_s, out_d), (m_s, out_m), (v_s, out_v)):
        dst.update(src)
    for dst in (out_g, out_d, out_m, out_v):
        for n in SMALL:
            dst[n] = (swap(dst[n]) if n in swapped else dst[n]).reshape(w[n].shape)

    return (loss, grad_x[None], *[out_g[n] for n in WEIGHTS], *[out_d[n] for n in WEIGHTS],
            *[out_m[n] for n in WEIGHTS], *[out_v[n] for n in WEIGHTS])
```

```python
import functools
import math

import jax
import jax.numpy as jnp
from jax import lax
from jax.experimental import pallas as pl
from jax.experimental.pallas import tpu as pltpu

f32 = jnp.float32
bf16 = jnp.bfloat16
SDS = jax.ShapeDtypeStruct

T = 2048
D = 1024
NDEV = 8
NORM_EPS = 1e-5
S5_G, S5_C, S5_P = 64, 16, 64
S5_SUB = 8
S5_CH = 8
S5_STEPS = T // S5_CH
DT_MIN_LAMBDA = -1e-4
HEAD_DIM = 64
N_KV = 4
Q_PER_KV = 4
BLK = 128
D_FF_SHARD = 512
ADAM_LR, ADAM_B1, ADAM_B2, ADAM_EPS, ADAM_WD, ADAM_STEP = 0.001, 0.9, 0.999, 1e-08, 0.01, 10
VMEM_LIMIT = 56 * 1024 * 1024
MESH = pl.DeviceIdType.MESH


def _cp(**kw):
    return pltpu.CompilerParams(vmem_limit_bytes=VMEM_LIMIT, **kw)


def _dot(a, b):
    return jnp.dot(a, b, preferred_element_type=f32)


def _dot_nt(a, b):
    return lax.dot_general(a, b, (((1,), (1,)), ((), ())), preferred_element_type=f32)


def _dot_tn(a, b):
    return lax.dot_general(a, b, (((0,), (0,)), ((), ())), preferred_element_type=f32)


def _rms(x, g):
    r = lax.rsqrt(jnp.mean(x * x, axis=-1, keepdims=True) + NORM_EPS)
    return x * r * g, r


def _rms_bwd(x, g, dy):
    r = lax.rsqrt(jnp.mean(x * x, axis=-1, keepdims=True) + NORM_EPS)
    u = dy * g
    dx = r * u - (r * r * r) * x * jnp.mean(u * x, axis=-1, keepdims=True)
    return dx, dy * x * r


def _colsum8(v):
    s = jnp.sum(v, axis=0, keepdims=True)
    row = lax.broadcasted_iota(jnp.int32, (8, v.shape[1]), 0)
    return jnp.where(row == 0, jnp.broadcast_to(s, (8, v.shape[1])), 0.0)


def _full(shape):
    nd = len(shape)
    return pl.BlockSpec(shape, lambda *_: (0,) * nd, pipeline_mode=pl.Buffered(1))


_ANY = pl.BlockSpec(memory_space=pl.ANY)


def _pos():
    return lax.axis_index("x"), lax.axis_index("y"), lax.axis_index("c")


def _other_chips(x, y):
    return [(1 - x, y), (x, 1 - y), (1 - x, 1 - y)]


class BgGather:
    SIB, XN, YN, FWD_Y, FWD_X, SIB_X, SIB_Y, SIB_D = range(8)

    def __init__(self, arrs, mids=(0.5, 0.75)):
        n = len(arrs)
        self.arrs = list(arrs)
        self.out_shape = [SDS((NDEV,) + a.shape, a.dtype) for a in arrs]
        self.scratch = [pltpu.SemaphoreType.DMA((n, 8)), pltpu.SemaphoreType.DMA((n, 8)),
                        pltpu.SemaphoreType.DMA((n,))]
        self.mids = mids
        self.result = None

    def mid_steps(self, nsteps):
        at = lambda f: min(nsteps - 1, max(0, int(f * nsteps) - 1))
        return [(at(self.mids[0]), self.mid), (max(at(self.mids[0]), at(self.mids[1])), self.mid2)]

    def _halves(self, a):
        rows = self.arrs[a].shape[0]
        cut = rows // 2 if rows >= 32 else rows
        return (0, cut), (cut, rows - cut)

    def _copy(self, ins, outs, sems, a, k, block, to, own=False, part=None):
        slot = 4 * block[0] + 2 * block[1] + block[2]
        rows = pl.ds(0, self.arrs[a].shape[0]) if part is None else pl.ds(*self._halves(a)[part])
        dst = outs[a].at[slot, rows]
        return pltpu.make_async_remote_copy(
            src_ref=ins[a].at[rows] if own else dst, dst_ref=dst, send_sem=sems[0].at[a, k],
            recv_sem=sems[1].at[a, k], device_id=to, device_id_type=MESH)

    def _mine(self, ins, outs, sems):
        x, y, c = _pos()
        return [pltpu.make_async_copy(ins[a], outs[a].at[4 * x + 2 * y + c], sems[2].at[a])
                for a in range(len(self.arrs))]

    def _split(self, a):
        return self._halves(a)[1][1] > 0

    def _sends(self, ins, outs, sems, phase):
        x, y, c = _pos()
        me, sib, xn, yn, dg = (x, y, c), (x, y, 1 - c), (1 - x, y, c), (x, 1 - y, c), (1 - x, 1 - y, c)
        cps = []
        for a in range(len(self.arrs)):
            cp = lambda k, block, to, **kw: self._copy(ins, outs, sems, a, k, block, to, **kw)
            if phase == 0:
                cps += [cp(self.SIB, me, sib, own=True), cp(self.XN, me, xn, own=True), cp(self.YN, me, yn, own=True)]
            elif phase == 1:
                cps.append(cp(self.FWD_Y, xn, yn, part=0))
                if self._split(a):
                    cps.append(cp(self.FWD_X, yn, xn, part=1))
                cps += [cp(self.SIB_X, xn, sib), cp(self.SIB_Y, yn, sib)]
            else:
                cps.append(cp(self.SIB_D, dg, sib))
        return cps

    def _arrivals(self, ins, outs, sems, phase):
        x, y, c = _pos()
        me, xn, yn, dg = (x, y, c), (1 - x, y, c), (x, 1 - y, c), (1 - x, 1 - y, c)
        cps = []
        for a in range(len(self.arrs)):
            cp = lambda k, block, **kw: self._copy(ins, outs, sems, a, k, block, me, **kw)
            if phase == 1:
                cps += [cp(self.XN, xn), cp(self.YN, yn)]
            elif phase == 2:
                cps.append(cp(self.FWD_Y, dg, part=0))
                if self._split(a):
                    cps.append(cp(self.FWD_X, dg, part=1))
            else:
                cps += [cp(self.SIB, (x, y, 1 - c)), cp(self.SIB_X, (1 - x, y, 1 - c)),
                        cp(self.SIB_Y, (x, 1 - y, 1 - c)), cp(self.SIB_D, (1 - x, 1 - y, 1 - c))]
        return cps

    def start(self, ins, outs, sems):
        for cp in self._mine(ins, outs, sems) + self._sends(ins, outs, sems, 0):
            cp.start()

    def mid(self, ins, outs, sems):
        for cp in self._arrivals(ins, outs, sems, 1):
            cp.wait_recv()
        for cp in self._sends(ins, outs, sems, 1):
            cp.start()

    def mid2(self, ins, outs, sems):
        for cp in self._arrivals(ins, outs, sems, 2):
            cp.wait_recv()
        for cp in self._sends(ins, outs, sems, 2):
            cp.start()

    def finish(self, ins, outs, sems):
        for cp in self._arrivals(ins, outs, sems, 3):
            cp.wait_recv()
        for ph in range(3):
            for cp in self._sends(ins, outs, sems, ph):
                cp.wait_send()
        for cp in self._mine(ins, outs, sems):
            cp.wait()


class BgPair:
    def __init__(self, arrs):
        n = len(arrs)
        self.arrs = list(arrs)
        self.out_shape = [SDS((4,) + a.shape[1:], a.dtype) for a in arrs]
        self.scratch = [pltpu.SemaphoreType.DMA((n, 4)), pltpu.SemaphoreType.DMA((n, 4))]
        self.result = None

    def mid_steps(self, nsteps):
        return []

    def _copies(self, ins, outs, sems):
        x, y, c = _pos()
        return [pltpu.make_async_remote_copy(
            src_ref=ins[a].at[2 * k + 1 - c], dst_ref=outs[a].at[k], send_sem=sems[0].at[a, k],
            recv_sem=sems[1].at[a, k], device_id=(x, y, 1 - c), device_id_type=MESH)
            for a in range(len(self.arrs)) for k in range(4)]

    def start(self, ins, outs, sems):
        for cp in self._copies(ins, outs, sems):
            cp.start()

    def finish(self, ins, outs, sems):
        cps = self._copies(ins, outs, sems)
        for cp in cps:
            cp.wait_recv()
        for cp in cps:
            cp.wait_send()


class BgChips(BgPair):
    def __init__(self, arrs):
        n = len(arrs)
        self.arrs = list(arrs)
        self.out_shape = [SDS((3,) + a.shape[1:], a.dtype) for a in arrs]
        self.scratch = [pltpu.SemaphoreType.DMA((n, 3)), pltpu.SemaphoreType.DMA((n, 3))]
        self.result = None

    def _copies(self, ins, outs, sems):
        x, y, c = _pos()
        return [pltpu.make_async_remote_copy(
            src_ref=ins[a].at[2 * px + py], dst_ref=outs[a].at[r], send_sem=sems[0].at[a, r],
            recv_sem=sems[1].at[a, r], device_id=(px, py, c), device_id_type=MESH)
            for a in range(len(self.arrs)) for r, (px, py) in enumerate(_other_chips(x, y))]


def _call(bgs, body, *, name, grid, in_specs, out_specs, out_shape, scratch_shapes=(), compiler_params=None):
    single = not isinstance(out_shape, (list, tuple))
    out_specs_l = [out_specs] if single else list(out_specs)
    out_shape_l = [out_shape] if single else list(out_shape)
    bgs = [b for b in (bgs or []) if b is not None]
    n_in, n_out, n_sc = len(in_specs), len(out_shape_l), len(scratch_shapes)
    nsteps = math.prod(grid)

    def full(*refs):
        pos = [0]

        def take(k):
            r = refs[pos[0]:pos[0] + k]
            pos[0] += k
            return r

        ins = take(n_in)
        b_ins = [take(len(b.arrs)) for b in bgs]
        outs = take(n_out)
        b_outs = [take(len(b.out_shape)) for b in bgs]
        sc = take(n_sc)
        b_sc = [take(len(b.scratch)) for b in bgs]
        if bgs:
            step = pl.program_id(0)
            for d in range(1, len(grid)):
                step = step * grid[d] + pl.program_id(d)

            @pl.when(step == 0)
            def _():
                for b, i_, o_, s_ in zip(bgs, b_ins, b_outs, b_sc):
                    b.start(i_, o_, s_)

        body(*ins, *outs, *sc)
        if bgs:
            for b, i_, o_, s_ in zip(bgs, b_ins, b_outs, b_sc):
                for at, fn in b.mid_steps(nsteps):
                    @pl.when(step == at)
                    def _():
                        fn(i_, o_, s_)

            @pl.when(step == nsteps - 1)
            def _():
                for b, i_, o_, s_ in zip(bgs, b_ins, b_outs, b_sc):
                    b.finish(i_, o_, s_)

    def run(*args):
        res = pl.pallas_call(
            full, name=name, grid=grid,
            in_specs=list(in_specs) + [_ANY] * sum(len(b.arrs) for b in bgs),
            out_specs=out_specs_l + [_ANY] * sum(len(b.out_shape) for b in bgs),
            out_shape=out_shape_l + [s for b in bgs for s in b.out_shape],
            scratch_shapes=list(scratch_shapes) + [s for b in bgs for s in b.scratch],
            compiler_params=compiler_params,
        )(*args, *[a for b in bgs for a in b.arrs])
        rest = list(res[n_out:])
        for b in bgs:
            b.result, rest = rest[:len(b.out_shape)], rest[len(b.out_shape):]
        return res[0] if single else list(res[:n_out])

    return run


def s5_discretize(a_re, a_im, log_dt, b_re, b_im, c_re, c_im):
    lam_r = jnp.minimum(a_re, DT_MIN_LAMBDA)
    lam_i = a_im
    dt = jnp.exp(log_dt)[:, None]
    e = jnp.exp(lam_r * dt)
    lbr = e * jnp.cos(lam_i * dt)
    lbi = e * jnp.sin(lam_i * dt)
    den = lam_r * lam_r + lam_i * lam_i
    cf_r = ((lbr - 1.0) * lam_r + lbi * lam_i) / den
    cf_i = (lbi * lam_r - (lbr - 1.0) * lam_i) / den
    bb_r = cf_r[:, :, None] * b_re - cf_i[:, :, None] * b_im
    bb_i = cf_r[:, :, None] * b_im + cf_i[:, :, None] * b_re
    eye = jnp.eye(8, dtype=f32)

    def blk_b(m):
        return jnp.einsum('bgpc,gh->bgchp', m.reshape(8, 8, S5_P, S5_C), eye).reshape(8, 128, 512)

    def blk_c(m):
        return jnp.einsum('bgcp,gh->bgphc', m.reshape(8, 8, S5_C, S5_P), eye).reshape(8, 512, 128)

    bm = jnp.concatenate([blk_b(bb_r), blk_b(bb_i)], axis=-1)
    cm = jnp.concatenate([blk_c(c_re), -blk_c(c_im)], axis=1)
    lam = jnp.stack([lbr.reshape(8, 512), lbi.reshape(8, 512)], axis=1)
    lam = jnp.broadcast_to(lam[:, :, None, :], (8, 2, 8, 512))
    return lam, bm, cm


def _cmul(ar, ai, br, bi):
    return ar * br - ai * bi, ar * bi + ai * br


def _shift_rows(v, k, up):
    row = lax.broadcasted_iota(jnp.int32, v.shape, 0)
    if up:
        return jnp.where(row < 8 - k, pltpu.roll(v, 8 - k, 0), 0.0)
    return jnp.where(row >= k, pltpu.roll(v, k, 0), 0.0)


def _chunk_scan(S, lr, li, reverse, aux=None):
    if reverse:
        li = -li
    z = jnp.zeros((8, 512), f32)

    def idx(i):
        return (S5_STEPS - 1 - i) if reverse else i

    def rec(xr, xi, row):
        br = S[row, 0:512]
        bi = S[row, 512:1024]
        return lr * xr - li * xi + br, lr * xi + li * xr + bi

    def step1(i, c):
        row = pl.ds(pl.multiple_of(idx(i) * 8, 8), 8)
        return rec(c[0], c[1], row)

    er, ei = lax.fori_loop(0, S5_STEPS, step1, (z, z), unroll=8)
    ar, ai = lr, li
    for _ in range(8):
        ar, ai = _cmul(ar, ai, ar, ai)
    cr, ci = _shift_rows(er, 1, reverse), _shift_rows(ei, 1, reverse)
    for k in (1, 2, 4):
        sr, si = _shift_rows(cr, k, reverse), _shift_rows(ci, k, reverse)
        pr, pi_ = _cmul(ar, ai, sr, si)
        cr, ci = cr + pr, ci + pi_
        ar, ai = _cmul(ar, ai, ar, ai)

    if aux is None:
        def step2(i, c):
            row = pl.ds(pl.multiple_of(idx(i) * 8, 8), 8)
            xr, xi = rec(c[0], c[1], row)
            S[row, 0:512] = xr
            S[row, 512:1024] = xi
            return xr, xi

        lax.fori_loop(0, S5_STEPS, step2, (cr, ci), unroll=8)
        return None

    def step2(i, c):
        gr0, gi0, dr, di = c
        s = idx(i)
        row = pl.ds(pl.multiple_of(s * 8, 8), 8)
        gr, gi = rec(gr0, gi0, row)
        S[row, 0:512] = gr
        S[row, 512:1024] = gi
        prow = pl.ds(pl.multiple_of(jnp.maximum(s - 1, 0) * 8, 8), 8)
        xr = aux[prow, 0:512]
        xi = aux[prow, 512:1024]
        dr = dr + gr * xr + gi * xi
        di = di + gi * xr - gr * xi
        return gr, gi, dr, di

    gr, gi, dr, di = lax.fori_loop(0, S5_STEPS - 1, step2, (cr, ci, z, z), unroll=8)
    row0 = pl.ds(0, 8)
    gr, gi = rec(gr, gi, row0)
    S[row0, 0:512] = gr
    S[row0, 512:1024] = gi
    last = pl.ds((S5_STEPS - 1) * 8, 8)
    xr = _shift_rows(aux[last, 0:512], 1, False)
    xi = _shift_rows(aux[last, 512:1024], 1, False)
    dr = dr + gr * xr + gi * xi
    di = di + gi * xr - gr * xi
    return dr, di


_ROWS = 256


def _row_loop(fn):
    def body(r, c):
        fn(pl.ds(pl.multiple_of(r * _ROWS, _ROWS), _ROWS))
        return c
    lax.fori_loop(0, T // _ROWS, body, 0)


def s5_core_fwd(hn, bm, lam, cm, bg=()):
    def body(u_ref, b_ref, lam_ref, c_ref, ys_ref, S):
        def bu(rows):
            S[rows, :] = _dot(u_ref[rows, :], b_ref[...])
        _row_loop(bu)
        _chunk_scan(S, lam_ref[0], lam_ref[1], False)

        def ys(rows):
            ys_ref[rows, :] = _dot(S[rows, :].astype(bf16), c_ref[...])
        _row_loop(ys)

    return _call(
        bg, body, name="s5_core_fwd", grid=(S5_SUB,),
        in_specs=[pl.BlockSpec((T, 128), lambda b: (0, b)),
                  pl.BlockSpec((None, 128, 1024), lambda b: (b, 0, 0)),
                  pl.BlockSpec((None, 2, 8, 512), lambda b: (b, 0, 0, 0)),
                  pl.BlockSpec((None, 1024, 128), lambda b: (b, 0, 0))],
        out_specs=pl.BlockSpec((T, 128), lambda b: (0, b)),
        out_shape=SDS((T, D), f32),
        scratch_shapes=[pltpu.VMEM((T, 1024), f32)],
        compiler_params=_cp(dimension_semantics=("arbitrary",)),
    )(hn, bm, lam, cm)


def s5_core_bwd(hn, dy, bm, lam, cm, bg=()):
    def body(u_ref, dy_ref, b_ref, lam_ref, c_ref, du_ref, db_ref, dct_ref, dlam_ref, S1, S2):
        def bu(rows):
            S1[rows, :] = _dot(u_ref[rows, :], b_ref[...])
        _row_loop(bu)
        _chunk_scan(S1, lam_ref[0], lam_ref[1], False)
        dct_ref[...] = jnp.zeros_like(dct_ref)

        def dx(rows):
            dyb = dy_ref[rows, :].astype(bf16)
            S2[rows, :] = _dot_nt(dyb, c_ref[...])
            dct_ref[...] += _dot_tn(dyb, S1[rows, :].astype(bf16))
        _row_loop(dx)
        dr, di = _chunk_scan(S2, lam_ref[0], lam_ref[1], True, aux=S1)
        dlam_ref[0] = dr
        dlam_ref[1] = di
        db_ref[...] = jnp.zeros_like(db_ref)

        def dbu(rows):
            gb = S2[rows, :].astype(bf16)
            db_ref[...] += _dot_tn(u_ref[rows, :], gb)
            du_ref[rows, :] = _dot_nt(gb, b_ref[...])
        _row_loop(dbu)

    return _call(
        bg, body, name="s5_core_bwd", grid=(S5_SUB,),
        in_specs=[pl.BlockSpec((T, 128), lambda b: (0, b)),
                  pl.BlockSpec((T, 128), lambda b: (0, b)),
                  pl.BlockSpec((None, 128, 1024), lambda b: (b, 0, 0)),
                  pl.BlockSpec((None, 2, 8, 512), lambda b: (b, 0, 0, 0)),
                  pl.BlockSpec((None, 1024, 128), lambda b: (b, 0, 0))],
        out_specs=[pl.BlockSpec((T, 128), lambda b: (0, b)),
                   pl.BlockSpec((None, 128, 1024), lambda b: (b, 0, 0)),
                   pl.BlockSpec((None, 128, 1024), lambda b: (b, 0, 0)),
                   pl.BlockSpec((None, 2, 8, 512), lambda b: (b, 0, 0, 0))],
        out_shape=[SDS((T, D), f32), SDS((8, 128, 1024), f32), SDS((8, 128, 1024), f32), SDS((8, 2, 8, 512), f32)],
        scratch_shapes=[pltpu.VMEM((T, 1024), f32), pltpu.VMEM((T, 1024), f32)],
        compiler_params=_cp(dimension_semantics=("arbitrary",)),
    )(hn, dy, bm, lam, cm)


TM = 512
NT = T // TM


def _tile(n=D):
    return pl.BlockSpec((TM, n), lambda i: (i, 0))


def s5_pre(xp, g):
    def body(x_ref, g_ref, hn_ref):
        hn, _ = _rms(x_ref[...], g_ref[...])
        hn_ref[...] = hn.astype(bf16)

    return pl.pallas_call(
        body, name="s5_pre", grid=(NT,), in_specs=[_tile(), _full((1, D))], out_specs=_tile(),
        out_shape=SDS((T, D), bf16), compiler_params=_cp(dimension_semantics=("arbitrary",)),
    )(xp, g)


def _gelu_grad(y):
    c = math.sqrt(2.0 / math.pi)
    t = jnp.tanh(c * (y + 0.044715 * y * y * y))
    return 0.5 * (1.0 + t) + 0.5 * y * (1.0 - t * t) * c * (1.0 + 3.0 * 0.044715 * y * y)


def s5_post(ys, xp, g, d, wglu, bglu, bg=()):
    def body(ys_ref, x_ref, g_ref, d_ref, w_ref, b_ref, y_ref, z_ref, h_ref):
        x = x_ref[...]
        hn, _ = _rms(x, g_ref[...])
        y = ys_ref[...] + d_ref[...] * hn
        y_ref[...] = y
        yg = jax.nn.gelu(y).astype(bf16)
        for j in range(4):
            cv = slice(j * 256, (j + 1) * 256)
            cg = slice(1024 + j * 256, 1024 + (j + 1) * 256)
            val = _dot(yg, w_ref[j]) + b_ref[:, cv]
            gate = _dot(yg, w_ref[j + 4]) + b_ref[:, cg]
            z_ref[:, cv] = val
            z_ref[:, cg] = gate
            h_ref[:, cv] = x[:, cv] + val * jax.nn.sigmoid(gate)

    return _call(
        bg, body, name="s5_post", grid=(NT,),
        in_specs=[_tile(), _tile(), _full((1, D)), _full((1, D)), _full((8, D, 256)), _full((1, 2 * D))],
        out_specs=[_tile(), _tile(2 * D), _tile()],
        out_shape=[SDS((T, D), f32), SDS((T, 2 * D), f32), SDS((T, D), f32)],
        compiler_params=_cp(dimension_semantics=("arbitrary",)),
    )(ys, xp, g, d, wglu, bglu)


def s5_post_bwd(dh, y, z, wglu, bg=()):
    def body(dh_ref, y_ref, z_ref, w_ref, dy_ref, dw_ref, db_ref, acc):
        i = pl.program_id(0)

        @pl.when(i == 0)
        def _():
            acc[...] = jnp.zeros_like(acc)
            db_ref[...] = jnp.zeros_like(db_ref)

        dh_ = dh_ref[...]
        y = y_ref[...]
        yg = jax.nn.gelu(y).astype(bf16)
        dyg = jnp.zeros((TM, D), f32)
        for j in range(4):
            cv = slice(j * 256, (j + 1) * 256)
            cg = slice(1024 + j * 256, 1024 + (j + 1) * 256)
            val = z_ref[:, cv]
            sg = jax.nn.sigmoid(z_ref[:, cg])
            dval = dh_[:, cv] * sg
            dgate = dh_[:, cv] * val * sg * (1.0 - sg)
            db_ref[:, cv] += _colsum8(dval)
            db_ref[:, cg] += _colsum8(dgate)
            dvb = dval.astype(bf16)
            dgb = dgate.astype(bf16)
            acc[j] += _dot_tn(yg, dvb)
            acc[j + 4] += _dot_tn(yg, dgb)
            dyg = dyg + _dot_nt(dvb, w_ref[j]) + _dot_nt(dgb, w_ref[j + 4])
        dy_ref[...] = dyg * _gelu_grad(y)

        @pl.when(i == NT - 1)
        def _():
            dw_ref[...] = acc[...].astype(bf16)

    return _call(
        bg, body, name="s5_post_bwd", grid=(NT,),
        in_specs=[_tile(), _tile(), _tile(2 * D), _full((8, D, 256))],
        out_specs=[_tile(), _full((8, D, 256)), _full((8, 2 * D))],
        out_shape=[SDS((T, D), f32), SDS((8, D, 256), bf16), SDS((8, 2 * D), f32)],
        scratch_shapes=[pltpu.VMEM((8, D, 256), f32)],
        compiler_params=_cp(dimension_semantics=("arbitrary",)),
    )(dh, y, z, wglu)


def s5_pre_bwd(xp, g, du, dy, d, dh, bg=()):
    def body(x_ref, g_ref, du_ref, dy_ref, d_ref, dh_ref, dx_ref, dg_ref, dd_ref):
        i = pl.program_id(0)

        @pl.when(i == 0)
        def _():
            dg_ref[...] = jnp.zeros_like(dg_ref)
            dd_ref[...] = jnp.zeros_like(dd_ref)

        x = x_ref[...]
        g = g_ref[...]
        dy = dy_ref[...]
        hn, _ = _rms(x, g)
        dhn = du_ref[...] + d_ref[...] * dy
        dx, dgt = _rms_bwd(x, g, dhn)
        dx_ref[...] = dh_ref[...] + dx
        dg_ref[...] += _colsum8(dgt)
        dd_ref[...] += _colsum8(dy * hn)

    return _call(
        bg, body, name="s5_pre_bwd", grid=(NT,),
        in_specs=[_tile(), _full((1, D)), _tile(), _tile(), _full((1, D)), _tile()],
        out_specs=[_tile(), _full((8, D)), _full((8, D))],
        out_shape=[SDS((T, D), f32), SDS((8, D), f32), SDS((8, D), f32)],
        compiler_params=_cp(dimension_semantics=("arbitrary",)),
    )(xp, g, du, dy, d, dh)


TMF = 1024


def mlp_fwd(h, g, w_in, w_out, layer, bg=()):
    def body(h_ref, g_ref, wi_ref, wo_ref, hm_ref, r_ref, out_ref, acc):
        j = pl.program_id(1)

        @pl.when(j == 0)
        def _():
            hm, _ = _rms(h_ref[...], g_ref[...])
            hm_ref[...] = hm.astype(bf16)
            acc[...] = jnp.zeros_like(acc)

        a = jnp.maximum(_dot(hm_ref[...], wi_ref[...]), 0.0)
        r_ref[...] = a.astype(bf16)
        acc[...] += _dot((a * a).astype(bf16), wo_ref[...])

        @pl.when(j == NDEV - 1)
        def _():
            out_ref[...] = h_ref[...] + acc[...]

    return _call(
        bg, body, name=f"mlp_fwd{layer}", grid=(T // TMF, NDEV),
        in_specs=[pl.BlockSpec((TMF, D), lambda i, j: (i, 0)),
                  pl.BlockSpec((1, D), lambda i, j: (0, 0)),
                  pl.BlockSpec((None, D, D_FF_SHARD), lambda i, j: (j, 0, 0)),
                  pl.BlockSpec((None, D_FF_SHARD, D), lambda i, j: (j, 0, 0))],
        out_specs=[pl.BlockSpec((TMF, D), lambda i, j: (i, 0)), pl.BlockSpec((TMF, D_FF_SHARD), lambda i, j: (i, j)),
                   pl.BlockSpec((TMF, D), lambda i, j: (i, 0))],
        out_shape=[SDS((T, D), bf16), SDS((T, NDEV * D_FF_SHARD), bf16), SDS((T, D), f32)],
        scratch_shapes=[pltpu.VMEM((TMF, D), f32)],
        compiler_params=_cp(dimension_semantics=("arbitrary", "arbitrary")),
    )(h, g, w_in, w_out)


def mlp_bwd(h, hm, r, g, dout, w_in, w_out, layer, bg=()):
    last = NDEV - 1

    def body(h_ref, hm_ref, r_ref, g_ref, do_ref, wi_ref, wo_ref, dh_ref, dwi_ref, dwo_ref, dg_ref, dhm, awi, awo):
        j = pl.program_id(0)
        i = pl.program_id(1)
        rows = pl.ds(pl.multiple_of(i * TM, TM), TM)

        @pl.when(i == 0)
        def _():
            awi[...] = jnp.zeros_like(awi)
            awo[...] = jnp.zeros_like(awo)

        hm_ = hm_ref[...]
        dob = do_ref[...].astype(bf16)
        r = r_ref[...].astype(f32)
        dz = (_dot_nt(dob, wo_ref[...]) * (2.0 * r)).astype(bf16)
        awo[...] += _dot_tn((r * r).astype(bf16), dob)
        awi[...] += _dot_tn(hm_, dz)
        part = _dot_nt(dz, wi_ref[...])

        @pl.when(j == 0)
        def _():
            dhm[rows, :] = part

        @pl.when(j > 0)
        def _():
            dhm[rows, :] += part

        @pl.when(i == NT - 1)
        def _():
            dwi_ref[...] = awi[...].astype(bf16)
            dwo_ref[...] = awo[...].astype(bf16)

        @pl.when(j == last)
        def _():
            @pl.when(i == 0)
            def _():
                dg_ref[...] = jnp.zeros_like(dg_ref)
            dx, dgt = _rms_bwd(h_ref[...], g_ref[...], dhm[rows, :])
            dh_ref[...] = do_ref[...] + dx
            dg_ref[...] += _colsum8(dgt)

    late = lambda j, i: (jnp.where(j == last, i, 0), 0)
    return _call(
        bg, body, name=f"mlp_bwd{layer}", grid=(NDEV, NT),
        in_specs=[pl.BlockSpec((TM, D), late),
                  pl.BlockSpec((TM, D), lambda j, i: (i, 0)),
                  pl.BlockSpec((TM, D_FF_SHARD), lambda j, i: (i, j)),
                  pl.BlockSpec((1, D), lambda j, i: (0, 0)),
                  pl.BlockSpec((TM, D), lambda j, i: (i, 0)),
                  pl.BlockSpec((None, D, D_FF_SHARD), lambda j, i: (j, 0, 0)),
                  pl.BlockSpec((None, D_FF_SHARD, D), lambda j, i: (j, 0, 0))],
        out_specs=[pl.BlockSpec((TM, D), late),
                   pl.BlockSpec((None, D, D_FF_SHARD), lambda j, i: (j, 0, 0)),
                   pl.BlockSpec((None, D_FF_SHARD, D), lambda j, i: (j, 0, 0)),
                   pl.BlockSpec((8, D), lambda j, i: (0, 0))],
        out_shape=[SDS((T, D), f32), SDS((NDEV, D, D_FF_SHARD), bf16), SDS((NDEV, D_FF_SHARD, D), bf16),
                   SDS((8, D), f32)],
        scratch_shapes=[pltpu.VMEM((T, D), f32), pltpu.VMEM((D, D_FF_SHARD), f32), pltpu.VMEM((D_FF_SHARD, D), f32)],
        compiler_params=_cp(dimension_semantics=("arbitrary", "arbitrary")),
    )(h, hm, r, g, dout, w_in, w_out)


def _spread4():
    r = lax.broadcasted_iota(jnp.int32, (256, D), 0)
    c = lax.broadcasted_iota(jnp.int32, (256, D), 1)
    return ((c // 256 == r // HEAD_DIM) & (c % HEAD_DIM == r % HEAD_DIM)).astype(bf16)


def attn_pre(h, g_kv, g_mix, wkv, bkv, spread, wq, bq):
    def body(h_ref, gkv_ref, gm_ref, wkv_ref, bkv_ref, sp_ref, wq_ref, bq_ref, kvn_ref, hn_ref, k_ref, v_ref, q_ref):
        h_ = h_ref[...]
        kvn = _rms(h_, gkv_ref[...])[0].astype(bf16)
        hn = _rms(h_, gm_ref[...])[0].astype(bf16)
        kvn_ref[...] = kvn
        hn_ref[...] = hn
        kv = (_dot(kvn, wkv_ref[...]) + bkv_ref[...]).astype(bf16)
        k_ref[...] = _dot(kv[:, :256], sp_ref[...]).astype(bf16)
        v_ref[...] = _dot(kv[:, 256:], sp_ref[...]).astype(bf16)
        q_ref[...] = (_dot(hn, wq_ref[...]) + bq_ref[...]).astype(bf16)

    return pl.pallas_call(
        body, name="attn_pre", grid=(NT,),
        in_specs=[_tile(), _full((1, D)), _full((1, D)), _full((D, 512)), _full((1, 512)), _full((256, D)),
                  _full((D, D)), _full((1, D))],
        out_specs=[_tile()] * 5,
        out_shape=[SDS((T, D), bf16)] * 5,
        compiler_params=_cp(dimension_semantics=("arbitrary",)),
    )(h, g_kv, g_mix, wkv, bkv, spread, wq, bq)


def _attn_specs():
    cur = pl.BlockSpec((TM, 256), lambda j, n: (n, j))
    prev = pl.BlockSpec((BLK, 256), lambda j, n: (jnp.maximum(n * (TM // BLK) - 1, 0), j))
    return cur, prev


def _head_mask(g):
    lane = lax.broadcasted_iota(jnp.int32, (1, 256), 1)
    return (lane >= g * HEAD_DIM) & (lane < (g + 1) * HEAD_DIM)


def _stack_heads(t):
    return jnp.concatenate([jnp.where(_head_mask(g), t, 0) for g in range(Q_PER_KV)], axis=0)


def _unstack_heads(t):
    out = jnp.where(_head_mask(0), t[0:BLK], 0.0)
    for g in range(1, Q_PER_KV):
        out = out + jnp.where(_head_mask(g), t[g * BLK:(g + 1) * BLK], 0.0)
    return out


def _attn_probs(qs, k2, sinks, first):
    rows = Q_PER_KV * BLK
    s = _dot_nt(qs, k2) * (1.0 / math.sqrt(HEAD_DIM))
    qi = jnp.bitwise_and(lax.broadcasted_iota(jnp.int32, (rows, 2 * BLK), 0), BLK - 1)
    kj = lax.broadcasted_iota(jnp.int32, (rows, 2 * BLK), 1)
    diff = qi + BLK - kj
    valid = (diff >= 0) & (diff < BLK) & (jnp.logical_not(first) | (kj >= BLK))
    s = jnp.where(valid, s, -jnp.inf)
    rb = lax.broadcasted_iota(jnp.int32, (rows, 1), 0)
    sink = jnp.where(rb < BLK, sinks[0], jnp.where(rb < 2 * BLK, sinks[1], jnp.where(rb < 3 * BLK, sinks[2], sinks[3])))
    m = jnp.maximum(jnp.max(s, axis=-1, keepdims=True), sink)
    p = jnp.exp(s - m)
    ps = jnp.exp(sink - m)
    denom = jnp.sum(p, axis=-1, keepdims=True) + ps
    return p / denom, ps / denom


def attn_core_fwd(q, k4, v4, sinks, bg=()):
    nb = TM // BLK

    def body(sink_ref, q_ref, kc_ref, kp_ref, vc_ref, vp_ref, o_ref):
        j = pl.program_id(0)
        n = pl.program_id(1)
        sk = [sink_ref[j * Q_PER_KV + g] for g in range(Q_PER_KV)]
        for b in range(nb):
            qb = q_ref[b * BLK:(b + 1) * BLK, :]
            if b == 0:
                k2 = jnp.concatenate([kp_ref[...], kc_ref[0:BLK, :]], axis=0)
                v2 = jnp.concatenate([vp_ref[...], vc_ref[0:BLK, :]], axis=0)
                first = n == 0
            else:
                k2 = kc_ref[(b - 1) * BLK:(b + 1) * BLK, :]
                v2 = vc_ref[(b - 1) * BLK:(b + 1) * BLK, :]
                first = False
            a, _ = _attn_probs(_stack_heads(qb), k2, sk, first)
            o_ref[b * BLK:(b + 1) * BLK, :] = _unstack_heads(_dot(a.astype(bf16), v2)).astype(bf16)

    cur, prev = _attn_specs()
    return _call(
        bg, body, name="attn_core_fwd", grid=(N_KV, NT),
        in_specs=[pl.BlockSpec(memory_space=pltpu.SMEM), cur, cur, prev, cur, prev],
        out_specs=cur, out_shape=SDS((T, D), bf16),
        compiler_params=_cp(dimension_semantics=("arbitrary", "arbitrary")),
    )(sinks, q, k4, k4, v4, v4)


def attn_post(h, o, wo, bo):
    def body(h_ref, o_ref, w_ref, b_ref, out_ref):
        out_ref[...] = h_ref[...] + _dot(o_ref[...], w_ref[...]) + b_ref[...]

    return pl.pallas_call(
        body, name="attn_post", grid=(NT,), in_specs=[_tile(), _tile(), _full((D, D)), _full((1, D))],
        out_specs=_tile(), out_shape=SDS((T, D), f32), compiler_params=_cp(dimension_semantics=("arbitrary",)),
    )(h, o, wo, bo)


def attn_bwd_pre(dh, o, wo, bg=()):
    def body(dh_ref, o_ref, w_ref, do_ref, dw_ref, db_ref, acc):
        i = pl.program_id(0)

        @pl.when(i == 0)
        def _():
            acc[...] = jnp.zeros_like(acc)
            db_ref[...] = jnp.zeros_like(db_ref)

        dh_ = dh_ref[...]
        dhb = dh_.astype(bf16)
        do_ref[...] = _dot_nt(dhb, w_ref[...]).astype(bf16)
        acc[...] += _dot_tn(o_ref[...], dhb)
        db_ref[...] += _colsum8(dh_)

        @pl.when(i == NT - 1)
        def _():
            dw_ref[...] = acc[...].astype(bf16)

    return _call(
        bg, body, name="attn_bwd_pre", grid=(NT,), in_specs=[_tile(), _tile(), _full((D, D))],
        out_specs=[_tile(), _full((D, D)), _full((8, D))],
        out_shape=[SDS((T, D), bf16), SDS((D, D), bf16), SDS((8, D), f32)],
        scratch_shapes=[pltpu.VMEM((D, D), f32)],
        compiler_params=_cp(dimension_semantics=("arbitrary",)),
    )(dh, o, wo)


def attn_core_bwd(q, do, k4, v4, sinks, bg=()):
    nb = TM // BLK

    def body(sink_ref, q_ref, do_ref, kc_ref, kp_ref, vc_ref, vp_ref, dq_ref, dk_ref, dv_ref, ds_ref):
        j = pl.program_id(0)
        n = pl.program_id(1)

        @pl.when(n == 0)
        def _():
            dk_ref[...] = jnp.zeros_like(dk_ref)
            dv_ref[...] = jnp.zeros_like(dv_ref)
            ds_ref[...] = jnp.zeros_like(ds_ref)

        lane8 = lax.broadcasted_iota(jnp.int32, (8, 128), 1)
        row8 = lax.broadcasted_iota(jnp.int32, (8, 128), 0)
        sk = [sink_ref[j * Q_PER_KV + g] for g in range(Q_PER_KV)]
        for b in range(nb):
            qs = _stack_heads(q_ref[b * BLK:(b + 1) * BLK, :])
            dos = _stack_heads(do_ref[b * BLK:(b + 1) * BLK, :])
            if b == 0:
                k2 = jnp.concatenate([kp_ref[...], kc_ref[0:BLK, :]], axis=0)
                v2 = jnp.concatenate([vp_ref[...], vc_ref[0:BLK, :]], axis=0)
                first = n == 0
            else:
                k2 = kc_ref[(b - 1) * BLK:(b + 1) * BLK, :]
                v2 = vc_ref[(b - 1) * BLK:(b + 1) * BLK, :]
                first = False
            a, asink = _attn_probs(qs, k2, sk, first)
            dp = _dot_nt(dos, v2)
            dd = jnp.sum(a * dp, axis=-1, keepdims=True)
            dsc = (a * (dp - dd) * (1.0 / math.sqrt(HEAD_DIM))).astype(bf16)
            t = asink * dd
            for g in range(Q_PER_KV):
                dsink = -jnp.sum(t[g * BLK:(g + 1) * BLK], axis=0, keepdims=True)
                ds_ref[...] += jnp.where((lane8 == g) & (row8 == 0), jnp.broadcast_to(dsink, (8, 128)), 0.0)
            dq_ref[b * BLK:(b + 1) * BLK, :] = _unstack_heads(_dot(dsc, k2))
            dk2 = _dot_tn(dsc, qs)
            dv2 = _dot_tn(a.astype(bf16), dos)
            cur = pl.ds(pl.multiple_of(n * TM + b * BLK, BLK), BLK)
            dk_ref[cur, :] += dk2[BLK:, :]
            dv_ref[cur, :] += dv2[BLK:, :]
            if b == 0:
                @pl.when(n > 0)
                def _():
                    prv = pl.ds(pl.multiple_of(n * TM - BLK, BLK), BLK)
                    dk_ref[prv, :] += dk2[:BLK, :]
                    dv_ref[prv, :] += dv2[:BLK, :]
            else:
                prv = pl.ds(pl.multiple_of(n * TM + (b - 1) * BLK, BLK), BLK)
                dk_ref[prv, :] += dk2[:BLK, :]
                dv_ref[prv, :] += dv2[:BLK, :]

    cur, prev = _attn_specs()
    col = pl.BlockSpec((T, 256), lambda j, n: (0, j))
    return _call(
        bg, body, name="attn_core_bwd", grid=(N_KV, NT),
        in_specs=[pl.BlockSpec(memory_space=pltpu.SMEM), cur, cur, cur, prev, cur, prev],
        out_specs=[cur, col, col, pl.BlockSpec((None, 8, 128), lambda j, n: (j, 0, 0))],
        out_shape=[SDS((T, D), f32), SDS((T, D), f32), SDS((T, D), f32), SDS((N_KV, 8, 128), f32)],
        compiler_params=_cp(dimension_semantics=("arbitrary", "arbitrary")),
    )(sinks, q, do, k4, k4, v4, v4)


def attn_bwd_q(h, dh, dq, hn, g_mix, wq):
    def body(h_ref, dh_ref, dq_ref, hn_ref, gm_ref, wq_ref, out_ref, dwq_ref, dbq_ref, dgm_ref, aq):
        i = pl.program_id(0)

        @pl.when(i == 0)
        def _():
            aq[...] = jnp.zeros_like(aq)
            dbq_ref[...] = jnp.zeros_like(dbq_ref)
            dgm_ref[...] = jnp.zeros_like(dgm_ref)

        dq_ = dq_ref[...]
        dqb = dq_.astype(bf16)
        aq[...] += _dot_tn(hn_ref[...], dqb)
        dbq_ref[...] += _colsum8(dq_)
        dx, dg = _rms_bwd(h_ref[...], gm_ref[...], _dot_nt(dqb, wq_ref[...]))
        out_ref[...] = dh_ref[...] + dx
        dgm_ref[...] += _colsum8(dg)

        @pl.when(i == NT - 1)
        def _():
            dwq_ref[...] = aq[...].astype(bf16)

    vec = _full((8, D))
    mat = _full((D, D))
    return pl.pallas_call(
        body, name="attn_bwd_q", grid=(NT,),
        in_specs=[_tile()] * 4 + [_full((1, D)), mat],
        out_specs=[_tile(), mat, vec, vec],
        out_shape=[SDS((T, D), f32), SDS((D, D), bf16), SDS((8, D), f32), SDS((8, D), f32)],
        scratch_shapes=[pltpu.VMEM((D, D), f32)],
        compiler_params=_cp(dimension_semantics=("arbitrary",)),
    )(h, dh, dq, hn, g_mix, wq)


def attn_bwd_kv(h, dh, dk4, dv4, kvn, g_kv, wkv, spread):
    def body(h_ref, dh_ref, dk_ref, dv_ref, kvn_ref, gkv_ref, wkv_ref, sp_ref, out_ref, dw_ref, db_ref, dgkv_ref, acc):
        i = pl.program_id(0)

        @pl.when(i == 0)
        def _():
            for r in (acc, db_ref, dgkv_ref):
                r[...] = jnp.zeros_like(r)

        dkv = jnp.concatenate([_dot_nt(dk_ref[...].astype(bf16), sp_ref[...]),
                               _dot_nt(dv_ref[...].astype(bf16), sp_ref[...])], axis=1)
        dkvb = dkv.astype(bf16)
        acc[...] += _dot_tn(kvn_ref[...], dkvb)
        db_ref[...] += _colsum8(dkv)
        dx, dg = _rms_bwd(h_ref[...], gkv_ref[...], _dot_nt(dkvb, wkv_ref[...]))
        out_ref[...] = dh_ref[...] + dx
        dgkv_ref[...] += _colsum8(dg)

        @pl.when(i == NT - 1)
        def _():
            dw_ref[...] = acc[...].astype(bf16)

    return pl.pallas_call(
        body, name="attn_bwd_kv", grid=(NT,),
        in_specs=[_tile()] * 5 + [_full((1, D)), _full((D, 512)), _full((256, D))],
        out_specs=[_tile(), _full((D, 512)), _full((8, 512)), _full((8, D))],
        out_shape=[SDS((T, D), f32), SDS((D, 512), bf16), SDS((8, 512), f32), SDS((8, D), f32)],
        scratch_shapes=[pltpu.VMEM((D, 512), f32)],
        compiler_params=_cp(dimension_semantics=("arbitrary",)),
    )(h, dh, dk4, dv4, kvn, g_kv, wkv, spread)


def final_loss(h, g, target):
    def body(h_ref, g_ref, t_ref, loss_ref, dh_ref, dg_ref):
        i = pl.program_id(0)

        @pl.when(i == 0)
        def _():
            loss_ref[...] = jnp.zeros_like(loss_ref)
            dg_ref[...] = jnp.zeros_like(dg_ref)

        h_ = h_ref[...]
        g_ = g_ref[...]
        y, _ = _rms(h_, g_)
        diff = y - t_ref[...]
        per_tok = jnp.mean(diff * diff, axis=-1, keepdims=True)
        tot = 0.5 * jnp.sum(per_tok, axis=0, keepdims=True)
        lane = lax.broadcasted_iota(jnp.int32, (8, 128), 1)
        row = lax.broadcasted_iota(jnp.int32, (8, 128), 0)
        loss_ref[...] += jnp.where((lane == 0) & (row == 0), jnp.broadcast_to(tot, (8, 128)), 0.0)
        dx, dgt = _rms_bwd(h_, g_, diff * (1.0 / D))
        dh_ref[...] = dx
        dg_ref[...] += _colsum8(dgt)

    return pl.pallas_call(
        body, name="final_loss", grid=(NT,), in_specs=[_tile(), _full((1, D)), _tile()],
        out_specs=[_full((8, 128)), _tile(), _full((8, D))],
        out_shape=[SDS((8, 128), f32), SDS((T, D), f32), SDS((8, D), f32)],
        compiler_params=_cp(dimension_semantics=("arbitrary",)),
    )(h, g, target)


def _to_chunked(a):
    return a.reshape(S5_CH, S5_STEPS, a.shape[-1]).transpose(1, 0, 2).reshape(T, a.shape[-1])


def _from_chunked(a):
    return a.reshape(S5_STEPS, S5_CH, a.shape[-1]).transpose(1, 0, 2).reshape(T, a.shape[-1])


def _rep4(w):
    return jnp.broadcast_to(w.reshape(w.shape[0], N_KV, 1, HEAD_DIM), (w.shape[0], N_KV, Q_PER_KV, HEAD_DIM)).reshape(
        w.shape[0], N_KV * Q_PER_KV * HEAD_DIM)


def _fold4(w):
    return w.reshape(w.shape[0], N_KV, Q_PER_KV, HEAD_DIM).sum(axis=2).reshape(w.shape[0], N_KV * HEAD_DIM)


def fwd_bwd(x, target, p, shards, core):
    row = lambda v: v.reshape(1, -1)
    (lam, bm, cm), prep_vjp = jax.vjp(s5_discretize, p["s5_a_re"][0], p["s5_a_im"][0], p["s5_log_dt"][0],
                                      p["s5_b_re"][0], p["s5_b_im"][0], p["s5_c_re"][0], p["s5_c_im"][0])
    bmb, cmb = bm.astype(bf16), cm.astype(bf16)
    g_mix0, g_mix1 = row(p["norm_mix"][0]), row(p["norm_mix"][1])
    g_mlp0, g_mlp1 = row(p["norm_mlp"][0]), row(p["norm_mlp"][1])
    g_kv, g_fin = row(p["norm_kv"]), row(p["norm_final"])
    bq, bo = p["b_q"], p["b_o"]
    bkv = row(p["b_kv"])
    spread = _spread4()
    sinks = p["sinks"].reshape(16)

    def reduce_pairs(names, bg):
        return [add_pairs(g, r, core, f"add_pairs_{n}") for n, g, r in zip(names, bg.arrs, bg.result)]

    xp = _to_chunked(x)
    hn0 = s5_pre(xp, g_mix0)
    ga = BgGather([shards["s5_w_glu"], shards["vecs"], shards["w_in0"]])
    ys = s5_core_fwd(hn0, bmb, lam, cmb, bg=[ga])
    wglu, gvec, win0 = ga.result
    d_skip = gvec[:, 0, :128].reshape(1, D)
    bglu = gvec[:, 0, 128:].reshape(1, 2 * D)
    gb = BgGather([shards["w_out0"]], mids=(1.0, 1.0))
    y, z, h1 = s5_post(ys, xp, g_mix0, d_skip, wglu, bglu, bg=[gb])
    wout0, = gb.result
    gc = BgGather([shards["w_kv"], shards["w_q"], shards["w_o"], shards["w_in1"]], mids=(0.8, 1.0))
    hm0, r0, h2p = mlp_fwd(h1, g_mlp0, win0, wout0, 0, bg=[gc])
    wkv, wq, wo, win1 = gc.result
    wkv, wq, wo = wkv.reshape(D, 512), wq.reshape(D, D), wo.reshape(D, D)
    h2 = _from_chunked(h2p)
    kvn, hn1, k4, v4, q = attn_pre(h2, g_kv, g_mix1, wkv, bkv, spread, wq, bq)
    gd = BgGather([shards["w_out1"]], mids=(0.94, 1.0))
    o = attn_core_fwd(q, k4, v4, sinks, bg=[gd])
    wout1, = gd.result
    h3 = attn_post(h2, o, wo, bo)
    hm1, r1, h4 = mlp_fwd(h3, g_mlp1, win1, wout1, 1)
    loss, dh4, dg_fin = final_loss(h4, g_fin, target)

    big = {}
    dh3, dwin1, dwout1, dg_mlp1 = mlp_bwd(h3, hm1, r1, g_mlp1, dh4, win1, wout1, 1)
    pa = BgPair([dwin1, dwout1])
    do, dwo, dbo = attn_bwd_pre(dh3, o, wo, bg=[pa])
    p_in1, p_out1 = reduce_pairs(["w_in1", "w_out1"], pa)
    ca = BgChips([p_in1])
    dq, dk4, dv4, dsink = attn_core_bwd(q, do, k4, v4, sinks, bg=[ca])
    big["w_in1"], = zip(ca.arrs, ca.result)
    dh2, dwq, dbq, dg_mix1 = attn_bwd_q(h2, dh3, dq, hn1, g_mix1, wq)
    dh2, dwkv, dbkv, dg_kv = attn_bwd_kv(h2, dh2, dk4, dv4, kvn, g_kv, wkv, spread)
    pb = BgPair([dwkv.reshape(NDEV, 128, 512), dwq.reshape(NDEV, 128, D), dwo.reshape(NDEV, 128, D)])
    ca2 = BgChips([p_out1])
    dh2p = _to_chunked(dh2)
    dh1, dwin0, dwout0, dg_mlp0 = mlp_bwd(h1, hm0, r0, g_mlp0, dh2p, win0, wout0, 0, bg=[pb, ca2])
    big["w_out1"], = zip(ca2.arrs, ca2.result)
    cb = BgChips(reduce_pairs(["w_kv", "w_q", "w_o"], pb))
    pc = BgPair([dwin0, dwout0])
    dy, dwglu, dbglu = s5_post_bwd(dh1, y, z, wglu, bg=[cb, pc])
    big["w_kv"], big["w_q"], big["w_o"] = zip(cb.arrs, cb.result)
    cc = BgChips(reduce_pairs(["w_in0", "w_out0"], pc))
    pd = BgPair([dwglu])
    du, dbm, dcmt, dlam = s5_core_bwd(hn0, dy, bmb, lam, cmb, bg=[cc, pd])
    big["w_in0"], big["w_out0"] = zip(cc.arrs, cc.result)
    cd = BgChips(reduce_pairs(["s5_w_glu"], pd))
    dxp, dg_mix0, dd = s5_pre_bwd(xp, g_mix0, du, dy, d_skip, dh1, bg=[cd])
    big["s5_w_glu"], = zip(cd.arrs, cd.result)
    grad_x = _from_chunked(dxp)
    da_re, da_im, dlog_dt, db_re, db_im, dc_re, dc_im = prep_vjp((dlam, dbm, dcmt.transpose(0, 2, 1)))

    def lanes(v_):
        v_ = v_.reshape(1, -1)
        return jnp.pad(v_, ((0, 0), (0, D - v_.shape[1])))

    small = jnp.concatenate([
        dg_mix0[0:1], dg_mix1[0:1], dg_mlp0[0:1], dg_mlp1[0:1], dg_kv[0:1], dg_fin[0:1], dd[0:1], dbq[0:1], dbo[0:1],
        dbglu[0:1].reshape(2, D), lanes(dbkv[0:1]),
        lanes(dsink[:, 0, :Q_PER_KV]), lanes(dlog_dt), lanes(loss[0:1, 0:1]), jnp.zeros((1, D), f32),
        da_re.reshape(4, D), da_im.reshape(4, D),
        db_re.transpose(0, 2, 1).reshape(64, D), db_im.transpose(0, 2, 1).reshape(64, D),
        dc_re.reshape(64, D), dc_im.reshape(64, D)], axis=0)
    return loss, grad_x, small, big


_ANY = pl.BlockSpec(memory_space=pl.ANY)


def _pos():
    return lax.axis_index("x"), lax.axis_index("y"), lax.axis_index("c")


def _other_chips(x, y):
    return [(1 - x, y), (x, 1 - y), (1 - x, 1 - y)]


def all_gather(arrs):
    n = len(arrs)

    def body(*refs):
        ins, outs = refs[:n], refs[n:2 * n]
        send_sems, recv_sems, local_sems = refs[2 * n:]
        x, y, c = _pos()
        me, sib = (x, y, c), (x, y, 1 - c)
        chips = _other_chips(x, y)

        def copy(a, k, block, to, src=None):
            dst = outs[a].at[4 * block[0] + 2 * block[1] + block[2]]
            return pltpu.make_async_remote_copy(
                src_ref=dst if src is None else src, dst_ref=dst, send_sem=send_sems.at[a, k],
                recv_sem=recv_sems.at[a, k], device_id=to, device_id_type=MESH)

        mine = [pltpu.make_async_copy(ins[a], outs[a].at[4 * x + 2 * y + c], local_sems.at[a]) for a in range(n)]
        for cp in mine:
            cp.start()
        first = []
        for a in range(n):
            first.append(copy(a, 0, me, sib, src=ins[a]))
            first += [copy(a, 1 + j, me, (*chip, c), src=ins[a]) for j, chip in enumerate(chips)]
        for cp in first:
            cp.start()
        passed = []
        for j, chip in enumerate(chips):
            for a in range(n):
                copy(a, 1 + j, (*chip, c), me).wait_recv()
                cp = copy(a, 4 + j, (*chip, c), sib)
                cp.start()
                passed.append(cp)
        for a in range(n):
            copy(a, 0, sib, me).wait_recv()
            for j, chip in enumerate(chips):
                copy(a, 4 + j, (*chip, 1 - c), me).wait_recv()
        for cp in first + passed:
            cp.wait_send()
        for cp in mine:
            cp.wait()

    return pl.pallas_call(
        body, name="all_gather", in_specs=[_ANY] * n, out_specs=[_ANY] * n,
        out_shape=[SDS((NDEV,) + a.shape, a.dtype) for a in arrs],
        scratch_shapes=[pltpu.SemaphoreType.DMA((n, 7)), pltpu.SemaphoreType.DMA((n, 7)),
                        pltpu.SemaphoreType.DMA((n,))],
    )(*arrs)


def rs_pair(grads):
    n = len(grads)

    def body(*refs):
        ins, outs = refs[:n], refs[n:2 * n]
        send_sems, recv_sems = refs[2 * n:]
        x, y, c = _pos()
        cps = []
        for a in range(n):
            for k in range(4):
                cps.append(pltpu.make_async_remote_copy(
                    src_ref=ins[a].at[2 * k + 1 - c], dst_ref=outs[a].at[k], send_sem=send_sems.at[a, k],
                    recv_sem=recv_sems.at[a, k], device_id=(x, y, 1 - c), device_id_type=MESH))
        for cp in cps:
            cp.start()
        for cp in cps:
            cp.wait_recv()
        for cp in cps:
            cp.wait_send()

    return pl.pallas_call(
        body, name="rs_pair", in_specs=[_ANY] * n, out_specs=[_ANY] * n,
        out_shape=[SDS((4,) + g.shape[1:], g.dtype) for g in grads],
        scratch_shapes=[pltpu.SemaphoreType.DMA((n, 4)), pltpu.SemaphoreType.DMA((n, 4))],
    )(*grads)


def rs_chips(parts):
    n = len(parts)

    def body(*refs):
        ins, outs = refs[:n], refs[n:2 * n]
        send_sems, recv_sems = refs[2 * n:]
        x, y, c = _pos()
        cps = []
        for a in range(n):
            for r, (px, py) in enumerate(_other_chips(x, y)):
                cps.append(pltpu.make_async_remote_copy(
                    src_ref=ins[a].at[2 * px + py], dst_ref=outs[a].at[r], send_sem=send_sems.at[a, r],
                    recv_sem=recv_sems.at[a, r], device_id=(px, py, c), device_id_type=MESH))
        for cp in cps:
            cp.start()
        for cp in cps:
            cp.wait_recv()
        for cp in cps:
            cp.wait_send()

    return pl.pallas_call(
        body, name="rs_chips", in_specs=[_ANY] * n, out_specs=[_ANY] * n,
        out_shape=[SDS((3,) + g.shape[1:], g.dtype) for g in parts],
        scratch_shapes=[pltpu.SemaphoreType.DMA((n, 3)), pltpu.SemaphoreType.DMA((n, 3))],
    )(*parts)


def _row_tile(r, c):
    return min(r, max(8, (256 * 1024) // c))


def add_pairs(g, r1, core, name):
    _, R, C = g.shape
    tr = _row_tile(R, C)

    def body(core_ref, g_ref, r_ref, o_ref):
        o_ref[...] = (g_ref[...].astype(f32) + r_ref[...].astype(f32)).astype(bf16)

    return pl.pallas_call(
        body, name=name, out_shape=SDS((4, R, C), bf16),
        grid_spec=pltpu.PrefetchScalarGridSpec(
            num_scalar_prefetch=1, grid=(4, R // tr),
            in_specs=[pl.BlockSpec((None, tr, C), lambda k, i, core: (2 * k + core[0], i, 0)),
                      pl.BlockSpec((None, tr, C), lambda k, i, core: (k, i, 0))],
            out_specs=pl.BlockSpec((None, tr, C), lambda k, i, core: (k, i, 0))),
        compiler_params=_cp(dimension_semantics=("arbitrary", "arbitrary")),
    )(core, g, r1)


def _adamw(w, g, m, v):
    m = ADAM_B1 * m + (1.0 - ADAM_B1) * g
    v = ADAM_B2 * v + (1.0 - ADAM_B2) * (g * g)
    m_hat = m / (1.0 - ADAM_B1 ** ADAM_STEP)
    v_hat = v / (1.0 - ADAM_B2 ** ADAM_STEP)
    delta = -ADAM_LR * (m_hat / (jnp.sqrt(v_hat) + ADAM_EPS) + ADAM_WD * w)
    return delta, m, v


def adam_big(w, m, v, part, r2, chip, name, layer=0, prev=None):
    L, R, C = w.shape
    tr = _row_tile(R, C)

    def body(chip_ref, w_ref, m_ref, v_ref, p_ref, r_ref, *rest):
        g_out, d_out, m_out, v_out = rest[-4:]
        g = p_ref[...].astype(f32) + r_ref[0].astype(f32) + r_ref[1].astype(f32) + r_ref[2].astype(f32)
        d, m_, v_ = _adamw(w_ref[...], g, m_ref[...], v_ref[...])
        g_out[...] = g
        d_out[...] = d
        m_out[...] = m_
        v_out[...] = v_

    blk = pl.BlockSpec((None, tr, C), lambda i, chip: (layer, i, 0))
    extra = [] if prev is None else list(prev)
    return pl.pallas_call(
        body, name=name, out_shape=[SDS((L, R, C), f32)] * 4,
        grid_spec=pltpu.PrefetchScalarGridSpec(
            num_scalar_prefetch=1, grid=(R // tr,),
            in_specs=[blk, blk, blk,
                      pl.BlockSpec((None, tr, C), lambda i, chip: (chip[0], i, 0)),
                      pl.BlockSpec((3, tr, C), lambda i, chip: (0, i, 0))] + [_ANY] * len(extra),
            out_specs=[blk] * 4),
        input_output_aliases={6 + k: k for k in range(len(extra))},
        compiler_params=_cp(dimension_semantics=("arbitrary",)),
    )(chip, w, m, v, part, r2, *extra)


def allreduce_small(buf):
    shp = buf.shape
    half = (shp[0] // 16) * 8
    parts = (pl.ds(0, half), pl.ds(half, shp[0] - half))

    def body(in_ref, out_ref, acc1, acc2, r0, r1, r2, send_sems, recv_sems):
        x, y, c = _pos()
        across = [(1 - x, y, c), (x, 1 - y, c)]

        def exchange(src, rcv, dst, copies):
            cps = [pltpu.make_async_remote_copy(
                src_ref=src.at[rows], dst_ref=rcv.at[rows], send_sem=send_sems.at[k], recv_sem=recv_sems.at[k],
                device_id=peer, device_id_type=MESH) for k, rows, peer in copies]
            for cp in cps:
                cp.start()
            for cp in cps:
                cp.wait()
            dst[...] = src[...] + rcv[...]

        exchange(in_ref, r0, acc1, [(0, pl.ds(0, shp[0]), (x, y, 1 - c))])
        exchange(acc1, r1, acc2, [(1, parts[0], across[0]), (2, parts[1], across[1])])
        exchange(acc2, r2, out_ref, [(3, parts[0], across[1]), (4, parts[1], across[0])])

    return pl.pallas_call(
        body, name="allreduce_small", out_shape=SDS(shp, f32),
        scratch_shapes=[pltpu.VMEM(shp, f32)] * 5 + [pltpu.SemaphoreType.DMA((5,)), pltpu.SemaphoreType.DMA((5,))],
    )(buf)


SMALL_ROWS = {'norm_mix': (0, 2, D), 'norm_mlp': (2, 2, D), 'norm_kv': (4, 1, D), 'norm_final': (5, 1, D),
              's5_d': (6, 1, D), 'b_q': (7, 1, D), 'b_o': (8, 1, D), 's5_b_glu': (9, 2, D), 'b_kv': (11, 1, 512),
              'sinks': (12, 1, 16), 's5_log_dt': (13, 1, 64), 's5_a_re': (16, 4, D), 's5_a_im': (20, 4, D),
              's5_b_re': (24, 64, D), 's5_b_im': (88, 64, D), 's5_c_re': (152, 64, D), 's5_c_im': (216, 64, D)}
LOSS_ROW = 14
ROW_PARAMS = ['norm_mix', 'norm_mlp', 'norm_kv', 'norm_final', 'b_q', 'b_o', 'b_kv', 'sinks', 's5_log_dt']
SHARD_PARAMS = ['s5_d', 's5_b_glu']
S5_PARAMS = ['s5_a_re', 's5_a_im', 's5_b_re', 's5_b_im', 's5_c_re', 's5_c_im']


def adam_small(dev, gsum, s5_grads, w, m, v):
    names = ROW_PARAMS + SHARD_PARAMS + S5_PARAMS
    n_g = len(ROW_PARAMS) + len(SHARD_PARAMS)

    def body(dev_ref, gs_ref, *refs):
        pos = [0]

        def take(k):
            r = refs[pos[0]:pos[0] + k]
            pos[0] += k
            return r

        g5 = take(len(S5_PARAMS))
        wr, mr, vr = take(len(names)), take(len(names)), take(len(names))
        g_out = take(n_g)
        d_out, m_out, v_out = take(len(names)), take(len(names)), take(len(names))
        dv = dev_ref[0]
        for i, n in enumerate(names):
            if n in S5_PARAMS:
                g = g5[S5_PARAMS.index(n)][...]
            elif n in SHARD_PARAMS:
                r0, _, _ = SMALL_ROWS[n]
                ln = wr[i].shape[1]
                g = jnp.zeros((1, ln), f32)
                for k in range(NDEV):
                    off = k * ln
                    piece = gs_ref[r0 + off // D:r0 + off // D + 1, off % D:off % D + ln]
                    g = g + jnp.where(dv == k, piece, 0.0)
                g_out[i][...] = g
            else:
                r0, nr, nl = SMALL_ROWS[n]
                g = gs_ref[r0:r0 + nr, 0:nl]
                g_out[i][...] = g
            d, m_, v_ = _adamw(wr[i][...], g, mr[i][...], vr[i][...])
            d_out[i][...] = d
            m_out[i][...] = m_
            v_out[i][...] = v_

    vm = pl.BlockSpec(memory_space=pltpu.VMEM)
    ins = [s5_grads[n] for n in S5_PARAMS] + [d[n] for d in (w, m, v) for n in names]
    shapes = [SDS(w[n].shape, f32) for n in names]
    res = pl.pallas_call(
        body, name="adam_small", in_specs=[pl.BlockSpec(memory_space=pltpu.SMEM)] + [vm] * (1 + len(ins)),
        out_specs=[vm] * (n_g + 3 * len(names)), out_shape=shapes[:n_g] + shapes * 3,
        compiler_params=_cp(),
    )(dev, gsum, *ins)
    g_o = dict(zip(names[:n_g], res[:n_g]))
    rest = res[n_g:]
    k = len(names)
    return g_o, dict(zip(names, rest[:k])), dict(zip(names, rest[k:2 * k])), dict(zip(names, rest[2 * k:]))


WEIGHTS = ['norm_mix', 'norm_mlp', 'norm_kv', 'norm_final', 's5_a_re', 's5_a_im', 's5_log_dt', 's5_b_re', 's5_b_im',
           's5_c_re', 's5_c_im', 's5_d', 's5_w_glu', 's5_b_glu', 'w_kv', 'b_kv', 'w_q', 'b_q', 'sinks', 'w_o', 'b_o',
           'w_mlp_in', 'w_mlp_out']
BIG = ['s5_w_glu', 'w_kv', 'w_q', 'w_o', 'w_mlp_in', 'w_mlp_out']
BIG_2D = {'s5_w_glu': (D, 256), 'w_kv': (128, 512), 'w_q': (128, D), 'w_o': (128, D), 'w_mlp_in': (2 * D, 512),
          'w_mlp_out': (2 * 512, D)}
SHARDED_SMALL = {'s5_d': D, 's5_b_glu': 2 * D}
SMALL = [n for n in WEIGHTS if n not in BIG]
SMALL_SIZE = {'norm_mix': 2 * D, 'norm_mlp': 2 * D, 'norm_kv': D, 'norm_final': D, 's5_a_re': 4096, 's5_a_im': 4096,
              's5_log_dt': 64, 's5_b_re': 65536, 's5_b_im': 65536, 's5_c_re': 65536, 's5_c_im': 65536, 's5_d': D,
              's5_b_glu': 2 * D, 'b_kv': 512, 'b_q': D, 'sinks': 16, 'b_o': D}


def _pack(vals):
    parts = []
    for n in SMALL:
        v = vals[n].reshape(-1).astype(f32)
        parts.append(jnp.pad(v, (0, (-v.shape[0]) % 128)))
    flat = jnp.concatenate(parts)
    flat = jnp.pad(flat, (0, (-flat.shape[0]) % 1024))
    return flat.reshape(-1, 128)


def _unpack(buf):
    flat = buf.reshape(-1)
    out, off = {}, 0
    for n in SMALL:
        sz = SMALL_SIZE[n]
        out[n] = flat[off:off + sz]
        off += sz + (-sz) % 128
    return out


def kernel(x, norm_mix, norm_mlp, norm_kv, norm_final, s5_a_re, s5_a_im, s5_log_dt, s5_b_re, s5_b_im, s5_c_re, s5_c_im, s5_d, s5_w_glu, s5_b_glu, w_kv, b_kv, w_q, b_q, sinks, w_o, b_o, w_mlp_in, w_mlp_out, loss_target, m_norm_mix, m_norm_mlp, m_norm_kv, m_norm_final, m_s5_a_re, m_s5_a_im, m_s5_log_dt, m_s5_b_re, m_s5_b_im, m_s5_c_re, m_s5_c_im, m_s5_d, m_s5_w_glu, m_s5_b_glu, m_w_kv, m_b_kv, m_w_q, m_b_q, m_sinks, m_w_o, m_b_o, m_w_mlp_in, m_w_mlp_out, v_norm_mix, v_norm_mlp, v_norm_kv, v_norm_final, v_s5_a_re, v_s5_a_im, v_s5_log_dt, v_s5_b_re, v_s5_b_im, v_s5_c_re, v_s5_c_im, v_s5_d, v_s5_w_glu, v_s5_b_glu, v_w_kv, v_b_kv, v_w_q, v_b_q, v_sinks, v_w_o, v_b_o, v_w_mlp_in, v_w_mlp_out):
    w = dict(norm_mix=norm_mix, norm_mlp=norm_mlp, norm_kv=norm_kv, norm_final=norm_final, s5_a_re=s5_a_re,
             s5_a_im=s5_a_im, s5_log_dt=s5_log_dt, s5_b_re=s5_b_re, s5_b_im=s5_b_im, s5_c_re=s5_c_re, s5_c_im=s5_c_im,
             s5_d=s5_d, s5_w_glu=s5_w_glu, s5_b_glu=s5_b_glu, w_kv=w_kv, b_kv=b_kv, w_q=w_q, b_q=b_q, sinks=sinks,
             w_o=w_o, b_o=b_o, w_mlp_in=w_mlp_in, w_mlp_out=w_mlp_out)
    m = dict(norm_mix=m_norm_mix, norm_mlp=m_norm_mlp, norm_kv=m_norm_kv, norm_final=m_norm_final, s5_a_re=m_s5_a_re,
             s5_a_im=m_s5_a_im, s5_log_dt=m_s5_log_dt, s5_b_re=m_s5_b_re, s5_b_im=m_s5_b_im, s5_c_re=m_s5_c_re,
             s5_c_im=m_s5_c_im, s5_d=m_s5_d, s5_w_glu=m_s5_w_glu, s5_b_glu=m_s5_b_glu, w_kv=m_w_kv, b_kv=m_b_kv,
             w_q=m_w_q, b_q=m_b_q, sinks=m_sinks, w_o=m_w_o, b_o=m_b_o, w_mlp_in=m_w_mlp_in, w_mlp_out=m_w_mlp_out)
    v = dict(norm_mix=v_norm_mix, norm_mlp=v_norm_mlp, norm_kv=v_norm_kv, norm_final=v_norm_final, s5_a_re=v_s5_a_re,
             s5_a_im=v_s5_a_im, s5_log_dt=v_s5_log_dt, s5_b_re=v_s5_b_re, s5_b_im=v_s5_b_im, s5_c_re=v_s5_c_re,
             s5_c_im=v_s5_c_im, s5_d=v_s5_d, s5_w_glu=v_s5_w_glu, s5_b_glu=v_s5_b_glu, w_kv=v_w_kv, b_kv=v_b_kv,
             w_q=v_w_q, b_q=v_b_q, sinks=v_sinks, w_o=v_w_o, b_o=v_b_o, w_mlp_in=v_w_mlp_in, w_mlp_out=v_w_mlp_out)
    xi, yi, ci = _pos()
    dev = 4 * xi + 2 * yi + ci
    core = ci.reshape(1).astype(jnp.int32)
    chip = (2 * xi + yi).reshape(1).astype(jnp.int32)

    shards = {
        "s5_w_glu": s5_w_glu[0].astype(bf16), "w_kv": w_kv.astype(bf16), "w_q": w_q[0].astype(bf16),
        "w_o": w_o[0].astype(bf16), "w_in0": w_mlp_in[0].astype(bf16), "w_in1": w_mlp_in[1].astype(bf16),
        "w_out0": w_mlp_out[0].astype(bf16), "w_out1": w_mlp_out[1].astype(bf16),
        "vecs": jnp.broadcast_to(jnp.concatenate([s5_d, s5_b_glu], axis=1), (8, 384)),
    }
    _, grad_x, grads, big = fwd_bwd(x[0], loss_target[0], {n: w[n] for n in SMALL}, shards, core)

    out_g, out_d, out_m, out_v = {}, {}, {}, {}
    for n in ("s5_w_glu", "w_kv", "w_q", "w_o"):
        shp = w[n].shape
        r3 = (1,) + BIG_2D[n]
        res = adam_big(w[n].reshape(r3), m[n].reshape(r3), v[n].reshape(r3), *big[n], chip, f"adam_{n}")
        out_g[n], out_d[n], out_m[n], out_v[n] = [r.reshape(shp) for r in res]
    for n, k in (("w_mlp_in", "w_in"), ("w_mlp_out", "w_out")):
        res = adam_big(w[n], m[n], v[n], *big[k + "1"], chip, f"adam_{k}1", layer=1)
        res = adam_big(w[n], m[n], v[n], *big[k + "0"], chip, f"adam_{k}0", layer=0, prev=res)
        out_g[n], out_d[n], out_m[n], out_v[n] = res

    gsum = allreduce_small(grads)
    loss = gsum[LOSS_ROW, 0]
    swapped = ("s5_b_re", "s5_b_im")
    swap = lambda a: a.transpose(0, 1, 3, 2)

    def kernel_side(d):
        d = {n: (d[n].reshape(1, -1) if d[n].ndim == 1 else d[n]) for n in SMALL}
        d.update({n: swap(d[n]) for n in swapped})
        return d

    s5_g = {}
    for n in S5_PARAMS:
        r0, nr, _ = SMALL_ROWS[n]
        s5_g[n] = gsum[r0:r0 + nr].reshape((1, 64, 16, 64) if n in swapped else w[n].shape)
        out_g[n] = s5_g[n]
    g_s, d_s, m_s, v_s = adam_small(dev.reshape(1).astype(jnp.int32), gsum, s5_g, kernel_side(w), kernel_side(m),
                                    kernel_side(v))
    for src, dst in ((g_s, out_g), (d_s, out_d), (m_s, out_m), (v_s, out_v)):
        dst.update(src)
    for dst in (out_g, out_d, out_m, out_v):
        for n in SMALL:
            dst[n] = (swap(dst[n]) if n in swapped else dst[n]).reshape(w[n].shape)

    return (loss, grad_x[None], *[out_g[n] for n in WEIGHTS], *[out_d[n] for n in WEIGHTS],
            *[out_m[n] for n in WEIGHTS], *[out_v[n] for n in WEIGHTS])
```

```python
import functools
import math

import jax
import jax.numpy as jnp
from jax import lax
from jax.experimental import pallas as pl
from jax.experimental.pallas import tpu as pltpu

f32 = jnp.float32
bf16 = jnp.bfloat16
SDS = jax.ShapeDtypeStruct

T = 2048
D = 1024
NDEV = 8
NORM_EPS = 1e-5
S5_G, S5_C, S5_P = 64, 16, 64
S5_SUB = 8
S5_CH = 8
S5_STEPS = T // S5_CH
DT_MIN_LAMBDA = -1e-4
HEAD_DIM = 64
N_KV = 4
Q_PER_KV = 4
BLK = 128
D_FF_SHARD = 512
ADAM_LR, ADAM_B1, ADAM_B2, ADAM_EPS, ADAM_WD, ADAM_STEP = 0.001, 0.9, 0.999, 1e-08, 0.01, 10
VMEM_LIMIT = 56 * 1024 * 1024
MESH = pl.DeviceIdType.MESH


def _cp(**kw):
    return pltpu.CompilerParams(vmem_limit_bytes=VMEM_LIMIT, **kw)


def _dot(a, b):
    return jnp.dot(a, b, preferred_element_type=f32)


def _dot_nt(a, b):
    return lax.dot_general(a, b, (((1,), (1,)), ((), ())), preferred_element_type=f32)


def _dot_tn(a, b):
    return lax.dot_general(a, b, (((0,), (0,)), ((), ())), preferred_element_type=f32)


def _rms(x, g):
    r = lax.rsqrt(jnp.mean(x * x, axis=-1, keepdims=True) + NORM_EPS)
    return x * r * g, r


def _rms_bwd(x, g, dy):
    r = lax.rsqrt(jnp.mean(x * x, axis=-1, keepdims=True) + NORM_EPS)
    u = dy * g
    dx = r * u - (r * r * r) * x * jnp.mean(u * x, axis=-1, keepdims=True)
    return dx, dy * x * r


def _colsum8(v):
    s = jnp.sum(v, axis=0, keepdims=True)
    row = lax.broadcasted_iota(jnp.int32, (8, v.shape[1]), 0)
    return jnp.where(row == 0, jnp.broadcast_to(s, (8, v.shape[1])), 0.0)


def _full(shape):
    nd = len(shape)
    return pl.BlockSpec(shape, lambda *_: (0,) * nd, pipeline_mode=pl.Buffered(1))


_ANY = pl.BlockSpec(memory_space=pl.ANY)


def _pos():
    return lax.axis_index("x"), lax.axis_index("y"), lax.axis_index("c")


def _other_chips(x, y):
    return [(1 - x, y), (x, 1 - y), (1 - x, 1 - y)]


class BgGather:
    SIB, XN, YN, FWD_Y, FWD_X, SIB_X, SIB_Y, SIB_D = range(8)

    def __init__(self, arrs, mids=(0.5, 0.75)):
        n = len(arrs)
        self.arrs = list(arrs)
        self.out_shape = [SDS((NDEV,) + a.shape, a.dtype) for a in arrs]
        self.scratch = [pltpu.SemaphoreType.DMA((n, 8)), pltpu.SemaphoreType.DMA((n, 8)),
                        pltpu.SemaphoreType.DMA((n,))]
        self.mids = mids
        self.result = None

    def mid_steps(self, nsteps):
        at = lambda f: min(nsteps - 1, max(0, int(f * nsteps) - 1))
        return [(at(self.mids[0]), self.mid), (max(at(self.mids[0]), at(self.mids[1])), self.mid2)]

    def _halves(self, a):
        rows = self.arrs[a].shape[0]
        cut = rows // 2 if rows >= 32 else rows
        return (0, cut), (cut, rows - cut)

    def _copy(self, ins, outs, sems, a, k, block, to, own=False, part=None):
        slot = 4 * block[0] + 2 * block[1] + block[2]
        rows = pl.ds(0, self.arrs[a].shape[0]) if part is None else pl.ds(*self._halves(a)[part])
        dst = outs[a].at[slot, rows]
        return pltpu.make_async_remote_copy(
            src_ref=ins[a].at[rows] if own else dst, dst_ref=dst, send_sem=sems[0].at[a, k],
            recv_sem=sems[1].at[a, k], device_id=to, device_id_type=MESH)

    def _mine(self, ins, outs, sems):
        x, y, c = _pos()
        return [pltpu.make_async_copy(ins[a], outs[a].at[4 * x + 2 * y + c], sems[2].at[a])
                for a in range(len(self.arrs))]

    def _split(self, a):
        return self._halves(a)[1][1] > 0

    def _sends(self, ins, outs, sems, phase):
        x, y, c = _pos()
        me, sib, xn, yn, dg = (x, y, c), (x, y, 1 - c), (1 - x, y, c), (x, 1 - y, c), (1 - x, 1 - y, c)
        cps = []
        for a in range(len(self.arrs)):
            cp = lambda k, block, to, **kw: self._copy(ins, outs, sems, a, k, block, to, **kw)
            if phase == 0:
                cps += [cp(self.SIB, me, sib, own=True), cp(self.XN, me, xn, own=True), cp(self.YN, me, yn, own=True)]
            elif phase == 1:
                cps.append(cp(self.FWD_Y, xn, yn, part=0))
                if self._split(a):
                    cps.append(cp(self.FWD_X, yn, xn, part=1))
                cps += [cp(self.SIB_X, xn, sib), cp(self.SIB_Y, yn, sib)]
            else:
                cps.append(cp(self.SIB_D, dg, sib))
        return cps

    def _arrivals(self, ins, outs, sems, phase):
        x, y, c = _pos()
        me, xn, yn, dg = (x, y, c), (1 - x, y, c), (x, 1 - y, c), (1 - x, 1 - y, c)
        cps = []
        for a in range(len(self.arrs)):
            cp = lambda k, block, **kw: self._copy(ins, outs, sems, a, k, block, me, **kw)
            if phase == 1:
                cps += [cp(self.XN, xn), cp(self.YN, yn)]
            elif phase == 2:
                cps.append(cp(self.FWD_Y, dg, part=0))
                if self._split(a):
                    cps.append(cp(self.FWD_X, dg, part=1))
            else:
                cps += [cp(self.SIB, (x, y, 1 - c)), cp(self.SIB_X, (1 - x, y, 1 - c)),
                        cp(self.SIB_Y, (x, 1 - y, 1 - c)), cp(self.SIB_D, (1 - x, 1 - y, 1 - c))]
        return cps

    def start(self, ins, outs, sems):
        for cp in self._mine(ins, outs, sems) + self._sends(ins, outs, sems, 0):
            cp.start()

    def mid(self, ins, outs, sems):
        for cp in self._arrivals(ins, outs, sems, 1):
            cp.wait_recv()
        for cp in self._sends(ins, outs, sems, 1):
            cp.start()

    def mid2(self, ins, outs, sems):
        for cp in self._arrivals(ins, outs, sems, 2):
            cp.wait_recv()
        for cp in self._sends(ins, outs, sems, 2):
            cp.start()

    def finish(self, ins, outs, sems):
        for cp in self._arrivals(ins, outs, sems, 3):
            cp.wait_recv()
        for ph in range(3):
            for cp in self._sends(ins, outs, sems, ph):
                cp.wait_send()
        for cp in self._mine(ins, outs, sems):
            cp.wait()


class BgPair:
    def __init__(self, arrs):
        n = len(arrs)
        self.arrs = list(arrs)
        self.out_shape = [SDS((4,) + a.shape[1:], a.dtype) for a in arrs]
        self.scratch = [pltpu.SemaphoreType.DMA((n, 4)), pltpu.SemaphoreType.DMA((n, 4))]
        self.result = None

    def mid_steps(self, nsteps):
        return []

    def _copies(self, ins, outs, sems):
        x, y, c = _pos()
        return [pltpu.make_async_remote_copy(
            src_ref=ins[a].at[2 * k + 1 - c], dst_ref=outs[a].at[k], send_sem=sems[0].at[a, k],
            recv_sem=sems[1].at[a, k], device_id=(x, y, 1 - c), device_id_type=MESH)
            for a in range(len(self.arrs)) for k in range(4)]

    def start(self, ins, outs, sems):
        for cp in self._copies(ins, outs, sems):
            cp.start()

    def finish(self, ins, outs, sems):
        cps = self._copies(ins, outs, sems)
        for cp in cps:
            cp.wait_recv()
        for cp in cps:
            cp.wait_send()


class BgChips(BgPair):
    def __init__(self, arrs):
        n = len(arrs)
        self.arrs = list(arrs)
        self.out_shape = [SDS((3,) + a.shape[1:], a.dtype) for a in arrs]
        self.scratch = [pltpu.SemaphoreType.DMA((n, 3)), pltpu.SemaphoreType.DMA((n, 3))]
        self.result = None

    def _copies(self, ins, outs, sems):
        x, y, c = _pos()
        return [pltpu.make_async_remote_copy(
            src_ref=ins[a].at[2 * px + py], dst_ref=outs[a].at[r], send_sem=sems[0].at[a, r],
            recv_sem=sems[1].at[a, r], device_id=(px, py, c), device_id_type=MESH)
            for a in range(len(self.arrs)) for r, (px, py) in enumerate(_other_chips(x, y))]


def _call(bgs, body, *, name, grid, in_specs, out_specs, out_shape, scratch_shapes=(), compiler_params=None):
    single = not isinstance(out_shape, (list, tuple))
    out_specs_l = [out_specs] if single else list(out_specs)
    out_shape_l = [out_shape] if single else list(out_shape)
    bgs = [b for b in (bgs or []) if b is not None]
    n_in, n_out, n_sc = len(in_specs), len(out_shape_l), len(scratch_shapes)
    nsteps = math.prod(grid)

    def full(*refs):
        pos = [0]

        def take(k):
            r = refs[pos[0]:pos[0] + k]
            pos[0] += k
            return r

        ins = take(n_in)
        b_ins = [take(len(b.arrs)) for b in bgs]
        outs = take(n_out)
        b_outs = [take(len(b.out_shape)) for b in bgs]
        sc = take(n_sc)
        b_sc = [take(len(b.scratch)) for b in bgs]
        if bgs:
            step = pl.program_id(0)
            for d in range(1, len(grid)):
                step = step * grid[d] + pl.program_id(d)

            @pl.when(step == 0)
            def _():
                for b, i_, o_, s_ in zip(bgs, b_ins, b_outs, b_sc):
                    b.start(i_, o_, s_)

        body(*ins, *outs, *sc)
        if bgs:
            for b, i_, o_, s_ in zip(bgs, b_ins, b_outs, b_sc):
                for at, fn in b.mid_steps(nsteps):
                    @pl.when(step == at)
                    def _():
                        fn(i_, o_, s_)

            @pl.when(step == nsteps - 1)
            def _():
                for b, i_, o_, s_ in zip(bgs, b_ins, b_outs, b_sc):
                    b.finish(i_, o_, s_)

    def run(*args):
        res = pl.pallas_call(
            full, name=name, grid=grid,
            in_specs=list(in_specs) + [_ANY] * sum(len(b.arrs) for b in bgs),
            out_specs=out_specs_l + [_ANY] * sum(len(b.out_shape) for b in bgs),
            out_shape=out_shape_l + [s for b in bgs for s in b.out_shape],
            scratch_shapes=list(scratch_shapes) + [s for b in bgs for s in b.scratch],
            compiler_params=compiler_params,
        )(*args, *[a for b in bgs for a in b.arrs])
        rest = list(res[n_out:])
        for b in bgs:
            b.result, rest = rest[:len(b.out_shape)], rest[len(b.out_shape):]
        return res[0] if single else list(res[:n_out])

    return run


def s5_discretize(a_re, a_im, log_dt, b_re, b_im, c_re, c_im):
    lam_r = jnp.minimum(a_re, DT_MIN_LAMBDA)
    lam_i = a_im
    dt = jnp.exp(log_dt)[:, None]
    e = jnp.exp(lam_r * dt)
    lbr = e * jnp.cos(lam_i * dt)
    lbi = e * jnp.sin(lam_i * dt)
    den = lam_r * lam_r + lam_i * lam_i
    cf_r = ((lbr - 1.0) * lam_r + lbi * lam_i) / den
    cf_i = (lbi * lam_r - (lbr - 1.0) * lam_i) / den
    bb_r = cf_r[:, :, None] * b_re - cf_i[:, :, None] * b_im
    bb_i = cf_r[:, :, None] * b_im + cf_i[:, :, None] * b_re
    eye = jnp.eye(8, dtype=f32)

    def blk_b(m):
        return jnp.einsum('bgpc,gh->bgchp', m.reshape(8, 8, S5_P, S5_C), eye).reshape(8, 128, 512)

    def blk_c(m):
        return jnp.einsum('bgcp,gh->bgphc', m.reshape(8, 8, S5_C, S5_P), eye).reshape(8, 512, 128)

    bm = jnp.concatenate([blk_b(bb_r), blk_b(bb_i)], axis=-1)
    cm = jnp.concatenate([blk_c(c_re), -blk_c(c_im)], axis=1)
    lam = jnp.stack([lbr.reshape(8, 512), lbi.reshape(8, 512)], axis=1)
    lam = jnp.broadcast_to(lam[:, :, None, :], (8, 2, 8, 512))
    return lam, bm, cm


def _cmul(ar, ai, br, bi):
    return ar * br - ai * bi, ar * bi + ai * br


def _shift_rows(v, k, up):
    row = lax.broadcasted_iota(jnp.int32, v.shape, 0)
    if up:
        return jnp.where(row < 8 - k, pltpu.roll(v, 8 - k, 0), 0.0)
    return jnp.where(row >= k, pltpu.roll(v, k, 0), 0.0)


def _chunk_scan(S, lr, li, reverse, aux=None):
    z = jnp.zeros((8, 512), f32)
    U = 4

    def idx(i):
        return (S5_STEPS - 1 - i) if reverse else i

    def rows_of(s):
        return pl.ds(s * 8, 8) if isinstance(s, int) else pl.ds(pl.multiple_of(s * 8, 8), 8)

    def rec(xr, xi, row):
        br = S[row, 0:512]
        bi = S[row, 512:1024]
        return lr * xr - li * xi + br, lr * xi + li * xr + bi

    def step1(i, c):
        for u in range(U):
            c = rec(c[0], c[1], rows_of(idx(i * U + u)))
        return c

    er, ei = lax.fori_loop(0, S5_STEPS // U, step1, (z, z))
    ar, ai = lr, li
    for _ in range(8):
        ar, ai = _cmul(ar, ai, ar, ai)
    cr, ci = _shift_rows(er, 1, reverse), _shift_rows(ei, 1, reverse)
    for k in (1, 2, 4):
        sr, si = _shift_rows(cr, k, reverse), _shift_rows(ci, k, reverse)
        pr, pi_ = _cmul(ar, ai, sr, si)
        cr, ci = cr + pr, ci + pi_
        ar, ai = _cmul(ar, ai, ar, ai)

    if aux is None:
        def step2(i, c):
            for u in range(U):
                row = rows_of(idx(i * U + u))
                c = rec(c[0], c[1], row)
                S[row, 0:512] = c[0]
                S[row, 512:1024] = c[1]
            return c

        lax.fori_loop(0, S5_STEPS // U, step2, (cr, ci))
        return None

    def one(s, c):
        gr0, gi0, dr, di = c
        row = rows_of(s)
        gr, gi = rec(gr0, gi0, row)
        S[row, 0:512] = gr
        S[row, 512:1024] = gi
        prow = rows_of(s - 1)
        xr = aux[prow, 0:512]
        xi = aux[prow, 512:1024]
        return gr, gi, dr + gr * xr + gi * xi, di + gi * xr - gr * xi

    def step2(i, c):
        for u in range(U):
            c = one(S5_STEPS - 1 - (i * U + u), c)
        return c

    c = lax.fori_loop(0, S5_STEPS // U - 1, step2, (cr, ci, z, z))
    for s in range(U - 1, 0, -1):
        c = one(s, c)
    gr, gi, dr, di = c
    row0 = pl.ds(0, 8)
    gr, gi = rec(gr, gi, row0)
    S[row0, 0:512] = gr
    S[row0, 512:1024] = gi
    last = pl.ds((S5_STEPS - 1) * 8, 8)
    xr = _shift_rows(aux[last, 0:512], 1, False)
    xi = _shift_rows(aux[last, 512:1024], 1, False)
    dr = dr + gr * xr + gi * xi
    di = di + gi * xr - gr * xi
    return dr, di


_ROWS = 256


def _row_loop(fn):
    def body(r, c):
        fn(pl.ds(pl.multiple_of(r * _ROWS, _ROWS), _ROWS))
        return c
    lax.fori_loop(0, T // _ROWS, body, 0)


def s5_core_fwd(hn, bm, lam, cm, bg=()):
    def body(u_ref, b_ref, lam_ref, c_ref, ys_ref, S):
        def bu(rows):
            S[rows, :] = _dot(u_ref[rows, :], b_ref[...])
        _row_loop(bu)
        _chunk_scan(S, lam_ref[0], lam_ref[1], False)

        def ys(rows):
            ys_ref[rows, :] = _dot(S[rows, :].astype(bf16), c_ref[...])
        _row_loop(ys)

    return _call(
        bg, body, name="s5_core_fwd", grid=(S5_SUB,),
        in_specs=[pl.BlockSpec((T, 128), lambda b: (0, b)),
                  pl.BlockSpec((None, 128, 1024), lambda b: (b, 0, 0)),
                  pl.BlockSpec((None, 4, 8, 512), lambda b: (b, 0, 0, 0)),
                  pl.BlockSpec((None, 1024, 128), lambda b: (b, 0, 0))],
        out_specs=pl.BlockSpec((T, 128), lambda b: (0, b)),
        out_shape=SDS((T, D), f32),
        scratch_shapes=[pltpu.VMEM((T, 1024), f32)],
        compiler_params=_cp(dimension_semantics=("arbitrary",)),
    )(hn, bm, lam, cm)


_SEG = _ROWS // S5_CH


def _scan_tile(S, lr, li, k, carry, reverse, store, aux=None):
    steps = range(k * _SEG, (k + 1) * _SEG)
    for s in (reversed(steps) if reverse else steps):
        row = pl.ds(s * 8, 8)
        xr, xi = carry[0], carry[1]
        nr = lr * xr - li * xi + S[row, 0:512]
        ni = lr * xi + li * xr + S[row, 512:1024]
        if store:
            S[row, 0:512] = nr
            S[row, 512:1024] = ni
        if aux is not None and s >= 1:
            prow = pl.ds((s - 1) * 8, 8)
            pr, pi_ = aux[prow, 0:512], aux[prow, 512:1024]
            carry = (nr, ni, carry[2] + nr * pr + ni * pi_, carry[3] + ni * pr - nr * pi_)
        elif aux is not None:
            carry = (nr, ni, carry[2], carry[3])
        else:
            carry = (nr, ni)
    return carry


def _chunk_starts(er, ei, lr, li, reverse):
    ar, ai = lr, li
    for _ in range(8):
        ar, ai = _cmul(ar, ai, ar, ai)
    cr, ci = _shift_rows(er, 1, reverse), _shift_rows(ei, 1, reverse)
    for k in (1, 2, 4):
        sr, si = _shift_rows(cr, k, reverse), _shift_rows(ci, k, reverse)
        pr, pi_ = _cmul(ar, ai, sr, si)
        cr, ci = cr + pr, ci + pi_
        ar, ai = _cmul(ar, ai, ar, ai)
    return cr, ci


def s5_core_bwd(hn, dy, bm, lam, cm, bg=()):
    nt = T // _ROWS

    def body(u_ref, dy_ref, b_ref, lam_ref, c_ref, du_ref, db_ref, dct_ref, dlam_ref, S1, S2):
        lr, li, lcr, lci = lam_ref[0], lam_ref[1], lam_ref[2], lam_ref[3]
        z = jnp.zeros((8, 512), f32)
        tile = lambda k: pl.ds(k * _ROWS, _ROWS)
        dyb = lambda k: dy_ref[tile(k), :].astype(bf16)

        c = (z, z)
        for k in range(nt):
            S1[tile(k), :] = _dot(u_ref[tile(k), :], b_ref[...])
            if k >= 1:
                c = _scan_tile(S1, lr, li, k - 1, c, False, False)
        c = _scan_tile(S1, lr, li, nt - 1, c, False, False)

        c = _chunk_starts(c[0], c[1], lr, li, False)
        dct_ref[...] = jnp.zeros_like(dct_ref)
        for k in range(nt):
            c = _scan_tile(S1, lr, li, k, c, False, True)
            if k >= 1:
                dct_ref[...] += _dot_tn(dyb(k - 1), S1[tile(k - 1), :].astype(bf16))
        dct_ref[...] += _dot_tn(dyb(nt - 1), S1[tile(nt - 1), :].astype(bf16))

        S2[tile(nt - 1), :] = _dot_nt(dyb(nt - 1), c_ref[...])
        c = (z, z)
        for k in range(nt - 1, -1, -1):
            if k >= 1:
                S2[tile(k - 1), :] = _dot_nt(dyb(k - 1), c_ref[...])
            c = _scan_tile(S2, lcr, lci, k, c, True, False)

        def dbu(k):
            gb = S2[tile(k), :].astype(bf16)
            db_ref[...] += _dot_tn(u_ref[tile(k), :], gb)
            du_ref[tile(k), :] = _dot_nt(gb, b_ref[...])

        c = _chunk_starts(c[0], c[1], lcr, lci, True) + (z, z)
        db_ref[...] = jnp.zeros_like(db_ref)
        for k in range(nt - 1, -1, -1):
            c = _scan_tile(S2, lcr, lci, k, c, True, True, aux=S1)
            if k + 1 < nt:
                dbu(k + 1)
        dbu(0)
        gr, gi, dr, di = c
        last = pl.ds((S5_STEPS - 1) * 8, 8)
        xr = _shift_rows(S1[last, 0:512], 1, False)
        xi = _shift_rows(S1[last, 512:1024], 1, False)
        dlam_ref[0] = dr + gr * xr + gi * xi
        dlam_ref[1] = di + gi * xr - gr * xi

    return _call(
        bg, body, name="s5_core_bwd", grid=(S5_SUB,),
        in_specs=[pl.BlockSpec((T, 128), lambda b: (0, b)),
                  pl.BlockSpec((T, 128), lambda b: (0, b)),
                  pl.BlockSpec((None, 128, 1024), lambda b: (b, 0, 0)),
                  pl.BlockSpec((None, 4, 8, 512), lambda b: (b, 0, 0, 0)),
                  pl.BlockSpec((None, 1024, 128), lambda b: (b, 0, 0))],
        out_specs=[pl.BlockSpec((T, 128), lambda b: (0, b)),
                   pl.BlockSpec((None, 128, 1024), lambda b: (b, 0, 0)),
                   pl.BlockSpec((None, 128, 1024), lambda b: (b, 0, 0)),
                   pl.BlockSpec((None, 2, 8, 512), lambda b: (b, 0, 0, 0))],
        out_shape=[SDS((T, D), f32), SDS((8, 128, 1024), f32), SDS((8, 128, 1024), f32), SDS((8, 2, 8, 512), f32)],
        scratch_shapes=[pltpu.VMEM((T, 1024), f32), pltpu.VMEM((T, 1024), f32)],
        compiler_params=_cp(dimension_semantics=("arbitrary",)),
    )(hn, dy, bm, lam, cm)


TM = 512
NT = T // TM


def _tile(n=D):
    return pl.BlockSpec((TM, n), lambda i: (i, 0))


def s5_pre(xp, g):
    def body(x_ref, g_ref, hn_ref):
        hn, _ = _rms(x_ref[...], g_ref[...])
        hn_ref[...] = hn.astype(bf16)

    return pl.pallas_call(
        body, name="s5_pre", grid=(NT,), in_specs=[_tile(), _full((1, D))], out_specs=_tile(),
        out_shape=SDS((T, D), bf16), compiler_params=_cp(dimension_semantics=("arbitrary",)),
    )(xp, g)


def _gelu_grad(y):
    c = math.sqrt(2.0 / math.pi)
    t = jnp.tanh(c * (y + 0.044715 * y * y * y))
    return 0.5 * (1.0 + t) + 0.5 * y * (1.0 - t * t) * c * (1.0 + 3.0 * 0.044715 * y * y)


def s5_post(ys, xp, g, d, wglu, bglu, bg=()):
    def body(ys_ref, x_ref, g_ref, d_ref, w_ref, b_ref, y_ref, z_ref, h_ref):
        x = x_ref[...]
        hn, _ = _rms(x, g_ref[...])
        y = ys_ref[...] + d_ref[...] * hn
        y_ref[...] = y
        yg = jax.nn.gelu(y).astype(bf16)
        for j in range(4):
            cv = slice(j * 256, (j + 1) * 256)
            cg = slice(1024 + j * 256, 1024 + (j + 1) * 256)
            val = _dot(yg, w_ref[j]) + b_ref[:, cv]
            gate = _dot(yg, w_ref[j + 4]) + b_ref[:, cg]
            z_ref[:, cv] = val
            z_ref[:, cg] = gate
            h_ref[:, cv] = x[:, cv] + val * jax.nn.sigmoid(gate)

    return _call(
        bg, body, name="s5_post", grid=(NT,),
        in_specs=[_tile(), _tile(), _full((1, D)), _full((1, D)), _full((8, D, 256)), _full((1, 2 * D))],
        out_specs=[_tile(), _tile(2 * D), _tile()],
        out_shape=[SDS((T, D), f32), SDS((T, 2 * D), f32), SDS((T, D), f32)],
        compiler_params=_cp(dimension_semantics=("arbitrary",)),
    )(ys, xp, g, d, wglu, bglu)


def s5_post_bwd(dh, y, z, wglu, bg=()):
    def body(dh_ref, y_ref, z_ref, w_ref, dy_ref, dw_ref, db_ref, acc):
        i = pl.program_id(0)

        @pl.when(i == 0)
        def _():
            acc[...] = jnp.zeros_like(acc)
            db_ref[...] = jnp.zeros_like(db_ref)

        dh_ = dh_ref[...]
        y = y_ref[...]
        yg = jax.nn.gelu(y).astype(bf16)
        dyg = jnp.zeros((TM, D), f32)
        for j in range(4):
            cv = slice(j * 256, (j + 1) * 256)
            cg = slice(1024 + j * 256, 1024 + (j + 1) * 256)
            val = z_ref[:, cv]
            sg = jax.nn.sigmoid(z_ref[:, cg])
            dval = dh_[:, cv] * sg
            dgate = dh_[:, cv] * val * sg * (1.0 - sg)
            db_ref[:, cv] += _colsum8(dval)
            db_ref[:, cg] += _colsum8(dgate)
            dvb = dval.astype(bf16)
            dgb = dgate.astype(bf16)
            acc[j] += _dot_tn(yg, dvb)
            acc[j + 4] += _dot_tn(yg, dgb)
            dyg = dyg + _dot_nt(dvb, w_ref[j]) + _dot_nt(dgb, w_ref[j + 4])
        dy_ref[...] = dyg * _gelu_grad(y)

        @pl.when(i == NT - 1)
        def _():
            dw_ref[...] = acc[...].astype(bf16)

    return _call(
        bg, body, name="s5_post_bwd", grid=(NT,),
        in_specs=[_tile(), _tile(), _tile(2 * D), _full((8, D, 256))],
        out_specs=[_tile(), _full((8, D, 256)), _full((8, 2 * D))],
        out_shape=[SDS((T, D), f32), SDS((8, D, 256), bf16), SDS((8, 2 * D), f32)],
        scratch_shapes=[pltpu.VMEM((8, D, 256), f32)],
        compiler_params=_cp(dimension_semantics=("arbitrary",)),
    )(dh, y, z, wglu)


def s5_pre_bwd(xp, g, du, dy, d, dh, bg=()):
    def body(x_ref, g_ref, du_ref, dy_ref, d_ref, dh_ref, dx_ref, dg_ref, dd_ref):
        i = pl.program_id(0)

        @pl.when(i == 0)
        def _():
            dg_ref[...] = jnp.zeros_like(dg_ref)
            dd_ref[...] = jnp.zeros_like(dd_ref)

        x = x_ref[...]
        g = g_ref[...]
        dy = dy_ref[...]
        hn, _ = _rms(x, g)
        dhn = du_ref[...] + d_ref[...] * dy
        dx, dgt = _rms_bwd(x, g, dhn)
        dx_ref[...] = dh_ref[...] + dx
        dg_ref[...] += _colsum8(dgt)
        dd_ref[...] += _colsum8(dy * hn)

    return _call(
        bg, body, name="s5_pre_bwd", grid=(NT,),
        in_specs=[_tile(), _full((1, D)), _tile(), _tile(), _full((1, D)), _tile()],
        out_specs=[_tile(), _full((8, D)), _full((8, D))],
        out_shape=[SDS((T, D), f32), SDS((8, D), f32), SDS((8, D), f32)],
        compiler_params=_cp(dimension_semantics=("arbitrary",)),
    )(xp, g, du, dy, d, dh)


TMF = 1024


def mlp_fwd(h, g, w_in, w_out, layer, bg=()):
    def body(h_ref, g_ref, wi_ref, wo_ref, hm_ref, r_ref, out_ref, acc):
        j = pl.program_id(1)

        @pl.when(j == 0)
        def _():
            hm, _ = _rms(h_ref[...], g_ref[...])
            hm_ref[...] = hm.astype(bf16)
            acc[...] = jnp.zeros_like(acc)

        a = jnp.maximum(_dot(hm_ref[...], wi_ref[...]), 0.0)
        r_ref[...] = a.astype(bf16)
        acc[...] += _dot((a * a).astype(bf16), wo_ref[...])

        @pl.when(j == NDEV - 1)
        def _():
            out_ref[...] = h_ref[...] + acc[...]

    return _call(
        bg, body, name=f"mlp_fwd{layer}", grid=(T // TMF, NDEV),
        in_specs=[pl.BlockSpec((TMF, D), lambda i, j: (i, 0)),
                  pl.BlockSpec((1, D), lambda i, j: (0, 0)),
                  pl.BlockSpec((None, D, D_FF_SHARD), lambda i, j: (j, 0, 0)),
                  pl.BlockSpec((None, D_FF_SHARD, D), lambda i, j: (j, 0, 0))],
        out_specs=[pl.BlockSpec((TMF, D), lambda i, j: (i, 0)), pl.BlockSpec((TMF, D_FF_SHARD), lambda i, j: (i, j)),
                   pl.BlockSpec((TMF, D), lambda i, j: (i, 0))],
        out_shape=[SDS((T, D), bf16), SDS((T, NDEV * D_FF_SHARD), bf16), SDS((T, D), f32)],
        scratch_shapes=[pltpu.VMEM((TMF, D), f32)],
        compiler_params=_cp(dimension_semantics=("arbitrary", "arbitrary")),
    )(h, g, w_in, w_out)


def mlp_bwd(h, hm, r, g, dout, w_in, w_out, layer, bg=()):
    last = NDEV - 1

    def body(h_ref, hm_ref, r_ref, g_ref, do_ref, wi_ref, wo_ref, dh_ref, dwi_ref, dwo_ref, dg_ref, dhm, awi, awo):
        j = pl.program_id(0)
        i = pl.program_id(1)
        rows = pl.ds(pl.multiple_of(i * TM, TM), TM)

        @pl.when(i == 0)
        def _():
            awi[...] = jnp.zeros_like(awi)
            awo[...] = jnp.zeros_like(awo)

        hm_ = hm_ref[...]
        dob = do_ref[...].astype(bf16)
        r = r_ref[...].astype(f32)
        dz = (_dot_nt(dob, wo_ref[...]) * (2.0 * r)).astype(bf16)
        awo[...] += _dot_tn((r * r).astype(bf16), dob)
        awi[...] += _dot_tn(hm_, dz)
        part = _dot_nt(dz, wi_ref[...])

        @pl.when(j == 0)
        def _():
            dhm[rows, :] = part

        @pl.when(j > 0)
        def _():
            dhm[rows, :] += part

        @pl.when(i == NT - 1)
        def _():
            dwi_ref[...] = awi[...].astype(bf16)
            dwo_ref[...] = awo[...].astype(bf16)

        @pl.when(j == last)
        def _():
            @pl.when(i == 0)
            def _():
                dg_ref[...] = jnp.zeros_like(dg_ref)
            dx, dgt = _rms_bwd(h_ref[...], g_ref[...], dhm[rows, :])
            dh_ref[...] = do_ref[...] + dx
            dg_ref[...] += _colsum8(dgt)

    late = lambda j, i: (jnp.where(j == last, i, 0), 0)
    return _call(
        bg, body, name=f"mlp_bwd{layer}", grid=(NDEV, NT),
        in_specs=[pl.BlockSpec((TM, D), late),
                  pl.BlockSpec((TM, D), lambda j, i: (i, 0)),
                  pl.BlockSpec((TM, D_FF_SHARD), lambda j, i: (i, j)),
                  pl.BlockSpec((1, D), lambda j, i: (0, 0)),
                  pl.BlockSpec((TM, D), lambda j, i: (i, 0)),
                  pl.BlockSpec((None, D, D_FF_SHARD), lambda j, i: (j, 0, 0)),
                  pl.BlockSpec((None, D_FF_SHARD, D), lambda j, i: (j, 0, 0))],
        out_specs=[pl.BlockSpec((TM, D), late),
                   pl.BlockSpec((None, D, D_FF_SHARD), lambda j, i: (j, 0, 0)),
                   pl.BlockSpec((None, D_FF_SHARD, D), lambda j, i: (j, 0, 0)),
                   pl.BlockSpec((8, D), lambda j, i: (0, 0))],
        out_shape=[SDS((T, D), f32), SDS((NDEV, D, D_FF_SHARD), bf16), SDS((NDEV, D_FF_SHARD, D), bf16),
                   SDS((8, D), f32)],
        scratch_shapes=[pltpu.VMEM((T, D), f32), pltpu.VMEM((D, D_FF_SHARD), f32), pltpu.VMEM((D_FF_SHARD, D), f32)],
        compiler_params=_cp(dimension_semantics=("arbitrary", "arbitrary")),
    )(h, hm, r, g, dout, w_in, w_out)


def _spread4():
    r = lax.broadcasted_iota(jnp.int32, (256, D), 0)
    c = lax.broadcasted_iota(jnp.int32, (256, D), 1)
    return ((c // 256 == r // HEAD_DIM) & (c % HEAD_DIM == r % HEAD_DIM)).astype(bf16)


def attn_pre(h, g_kv, g_mix, wkv, bkv, spread, wq, bq):
    def body(h_ref, gkv_ref, gm_ref, wkv_ref, bkv_ref, sp_ref, wq_ref, bq_ref, kvn_ref, hn_ref, k_ref, v_ref, q_ref):
        h_ = h_ref[...]
        kvn = _rms(h_, gkv_ref[...])[0].astype(bf16)
        hn = _rms(h_, gm_ref[...])[0].astype(bf16)
        kvn_ref[...] = kvn
        hn_ref[...] = hn
        kv = (_dot(kvn, wkv_ref[...]) + bkv_ref[...]).astype(bf16)
        k_ref[...] = _dot(kv[:, :256], sp_ref[...]).astype(bf16)
        v_ref[...] = _dot(kv[:, 256:], sp_ref[...]).astype(bf16)
        q_ref[...] = (_dot(hn, wq_ref[...]) + bq_ref[...]).astype(bf16)

    return pl.pallas_call(
        body, name="attn_pre", grid=(NT,),
        in_specs=[_tile(), _full((1, D)), _full((1, D)), _full((D, 512)), _full((1, 512)), _full((256, D)),
                  _full((D, D)), _full((1, D))],
        out_specs=[_tile()] * 5,
        out_shape=[SDS((T, D), bf16)] * 5,
        compiler_params=_cp(dimension_semantics=("arbitrary",)),
    )(h, g_kv, g_mix, wkv, bkv, spread, wq, bq)


def _attn_specs():
    cur = pl.BlockSpec((TM, 256), lambda j, n: (n, j))
    prev = pl.BlockSpec((BLK, 256), lambda j, n: (jnp.maximum(n * (TM // BLK) - 1, 0), j))
    return cur, prev


def _head_mask(g):
    lane = lax.broadcasted_iota(jnp.int32, (1, 256), 1)
    return (lane >= g * HEAD_DIM) & (lane < (g + 1) * HEAD_DIM)


def _stack_heads(t):
    return jnp.concatenate([jnp.where(_head_mask(g), t, 0) for g in range(Q_PER_KV)], axis=0)


def _unstack_heads(t):
    out = jnp.where(_head_mask(0), t[0:BLK], 0.0)
    for g in range(1, Q_PER_KV):
        out = out + jnp.where(_head_mask(g), t[g * BLK:(g + 1) * BLK], 0.0)
    return out


def _attn_probs(qs, k2, sinks, first):
    rows = Q_PER_KV * BLK
    s = _dot_nt(qs, k2) * (1.0 / math.sqrt(HEAD_DIM))
    qi = jnp.bitwise_and(lax.broadcasted_iota(jnp.int32, (rows, 2 * BLK), 0), BLK - 1)
    kj = lax.broadcasted_iota(jnp.int32, (rows, 2 * BLK), 1)
    diff = qi + BLK - kj
    valid = (diff >= 0) & (diff < BLK) & (jnp.logical_not(first) | (kj >= BLK))
    s = jnp.where(valid, s, -jnp.inf)
    rb = lax.broadcasted_iota(jnp.int32, (rows, 1), 0)
    sink = jnp.where(rb < BLK, sinks[0], jnp.where(rb < 2 * BLK, sinks[1], jnp.where(rb < 3 * BLK, sinks[2], sinks[3])))
    m = jnp.maximum(jnp.max(s, axis=-1, keepdims=True), sink)
    p = jnp.exp(s - m)
    ps = jnp.exp(sink - m)
    denom = jnp.sum(p, axis=-1, keepdims=True) + ps
    return p / denom, ps / denom


def attn_core_fwd(q, k4, v4, sinks, bg=()):
    nb = TM // BLK

    def body(sink_ref, q_ref, kc_ref, kp_ref, vc_ref, vp_ref, o_ref):
        j = pl.program_id(0)
        n = pl.program_id(1)
        sk = [sink_ref[j * Q_PER_KV + g] for g in range(Q_PER_KV)]
        for b in range(nb):
            qb = q_ref[b * BLK:(b + 1) * BLK, :]
            if b == 0:
                k2 = jnp.concatenate([kp_ref[...], kc_ref[0:BLK, :]], axis=0)
                v2 = jnp.concatenate([vp_ref[...], vc_ref[0:BLK, :]], axis=0)
                first = n == 0
            else:
                k2 = kc_ref[(b - 1) * BLK:(b + 1) * BLK, :]
                v2 = vc_ref[(b - 1) * BLK:(b + 1) * BLK, :]
                first = False
            a, _ = _attn_probs(_stack_heads(qb), k2, sk, first)
            o_ref[b * BLK:(b + 1) * BLK, :] = _unstack_heads(_dot(a.astype(bf16), v2)).astype(bf16)

    cur, prev = _attn_specs()
    return _call(
        bg, body, name="attn_core_fwd", grid=(N_KV, NT),
        in_specs=[pl.BlockSpec(memory_space=pltpu.SMEM), cur, cur, prev, cur, prev],
        out_specs=cur, out_shape=SDS((T, D), bf16),
        compiler_params=_cp(dimension_semantics=("arbitrary", "arbitrary")),
    )(sinks, q, k4, k4, v4, v4)


def attn_post(h, o, wo, bo):
    def body(h_ref, o_ref, w_ref, b_ref, out_ref):
        out_ref[...] = h_ref[...] + _dot(o_ref[...], w_ref[...]) + b_ref[...]

    return pl.pallas_call(
        body, name="attn_post", grid=(NT,), in_specs=[_tile(), _tile(), _full((D, D)), _full((1, D))],
        out_specs=_tile(), out_shape=SDS((T, D), f32), compiler_params=_cp(dimension_semantics=("arbitrary",)),
    )(h, o, wo, bo)


def attn_bwd_pre(dh, o, wo, bg=()):
    def body(dh_ref, o_ref, w_ref, do_ref, dw_ref, db_ref, acc):
        i = pl.program_id(0)

        @pl.when(i == 0)
        def _():
            acc[...] = jnp.zeros_like(acc)
            db_ref[...] = jnp.zeros_like(db_ref)

        dh_ = dh_ref[...]
        dhb = dh_.astype(bf16)
        do_ref[...] = _dot_nt(dhb, w_ref[...]).astype(bf16)
        acc[...] += _dot_tn(o_ref[...], dhb)
        db_ref[...] += _colsum8(dh_)

        @pl.when(i == NT - 1)
        def _():
            dw_ref[...] = acc[...].astype(bf16)

    return _call(
        bg, body, name="attn_bwd_pre", grid=(NT,), in_specs=[_tile(), _tile(), _full((D, D))],
        out_specs=[_tile(), _full((D, D)), _full((8, D))],
        out_shape=[SDS((T, D), bf16), SDS((D, D), bf16), SDS((8, D), f32)],
        scratch_shapes=[pltpu.VMEM((D, D), f32)],
        compiler_params=_cp(dimension_semantics=("arbitrary",)),
    )(dh, o, wo)


def attn_core_bwd(q, do, k4, v4, sinks, bg=()):
    nb = TM // BLK

    def body(sink_ref, q_ref, do_ref, kc_ref, kp_ref, vc_ref, vp_ref, dq_ref, dk_ref, dv_ref, ds_ref):
        j = pl.program_id(0)
        n = pl.program_id(1)

        @pl.when(n == 0)
        def _():
            dk_ref[...] = jnp.zeros_like(dk_ref)
            dv_ref[...] = jnp.zeros_like(dv_ref)
            ds_ref[...] = jnp.zeros_like(ds_ref)

        lane8 = lax.broadcasted_iota(jnp.int32, (8, 128), 1)
        row8 = lax.broadcasted_iota(jnp.int32, (8, 128), 0)
        sk = [sink_ref[j * Q_PER_KV + g] for g in range(Q_PER_KV)]
        for b in range(nb):
            qs = _stack_heads(q_ref[b * BLK:(b + 1) * BLK, :])
            dos = _stack_heads(do_ref[b * BLK:(b + 1) * BLK, :])
            if b == 0:
                k2 = jnp.concatenate([kp_ref[...], kc_ref[0:BLK, :]], axis=0)
                v2 = jnp.concatenate([vp_ref[...], vc_ref[0:BLK, :]], axis=0)
                first = n == 0
            else:
                k2 = kc_ref[(b - 1) * BLK:(b + 1) * BLK, :]
                v2 = vc_ref[(b - 1) * BLK:(b + 1) * BLK, :]
                first = False
            a, asink = _attn_probs(qs, k2, sk, first)
            dp = _dot_nt(dos, v2)
            dd = jnp.sum(a * dp, axis=-1, keepdims=True)
            dsc = (a * (dp - dd) * (1.0 / math.sqrt(HEAD_DIM))).astype(bf16)
            t = asink * dd
            for g in range(Q_PER_KV):
                dsink = -jnp.sum(t[g * BLK:(g + 1) * BLK], axis=0, keepdims=True)
                ds_ref[...] += jnp.where((lane8 == g) & (row8 == 0), jnp.broadcast_to(dsink, (8, 128)), 0.0)
            dq_ref[b * BLK:(b + 1) * BLK, :] = _unstack_heads(_dot(dsc, k2))
            dk2 = _dot_tn(dsc, qs)
            dv2 = _dot_tn(a.astype(bf16), dos)
            cur = pl.ds(pl.multiple_of(n * TM + b * BLK, BLK), BLK)
            dk_ref[cur, :] += dk2[BLK:, :]
            dv_ref[cur, :] += dv2[BLK:, :]
            if b == 0:
                @pl.when(n > 0)
                def _():
                    prv = pl.ds(pl.multiple_of(n * TM - BLK, BLK), BLK)
                    dk_ref[prv, :] += dk2[:BLK, :]
                    dv_ref[prv, :] += dv2[:BLK, :]
            else:
                prv = pl.ds(pl.multiple_of(n * TM + (b - 1) * BLK, BLK), BLK)
                dk_ref[prv, :] += dk2[:BLK, :]
                dv_ref[prv, :] += dv2[:BLK, :]

    cur, prev = _attn_specs()
    col = pl.BlockSpec((T, 256), lambda j, n: (0, j))
    return _call(
        bg, body, name="attn_core_bwd", grid=(N_KV, NT),
        in_specs=[pl.BlockSpec(memory_space=pltpu.SMEM), cur, cur, cur, prev, cur, prev],
        out_specs=[cur, col, col, pl.BlockSpec((None, 8, 128), lambda j, n: (j, 0, 0))],
        out_shape=[SDS((T, D), f32), SDS((T, D), f32), SDS((T, D), f32), SDS((N_KV, 8, 128), f32)],
        compiler_params=_cp(dimension_semantics=("arbitrary", "arbitrary")),
    )(sinks, q, do, k4, k4, v4, v4)


def attn_bwd_q(h, dh, dq, hn, g_mix, wq):
    def body(h_ref, dh_ref, dq_ref, hn_ref, gm_ref, wq_ref, out_ref, dwq_ref, dbq_ref, dgm_ref, aq):
        i = pl.program_id(0)

        @pl.when(i == 0)
        def _():
            aq[...] = jnp.zeros_like(aq)
            dbq_ref[...] = jnp.zeros_like(dbq_ref)
            dgm_ref[...] = jnp.zeros_like(dgm_ref)

        dq_ = dq_ref[...]
        dqb = dq_.astype(bf16)
        aq[...] += _dot_tn(hn_ref[...], dqb)
        dbq_ref[...] += _colsum8(dq_)
        dx, dg = _rms_bwd(h_ref[...], gm_ref[...], _dot_nt(dqb, wq_ref[...]))
        out_ref[...] = dh_ref[...] + dx
        dgm_ref[...] += _colsum8(dg)

        @pl.when(i == NT - 1)
        def _():
            dwq_ref[...] = aq[...].astype(bf16)

    vec = _full((8, D))
    mat = _full((D, D))
    return pl.pallas_call(
        body, name="attn_bwd_q", grid=(NT,),
        in_specs=[_tile()] * 4 + [_full((1, D)), mat],
        out_specs=[_tile(), mat, vec, vec],
        out_shape=[SDS((T, D), f32), SDS((D, D), bf16), SDS((8, D), f32), SDS((8, D), f32)],
        scratch_shapes=[pltpu.VMEM((D, D), f32)],
        compiler_params=_cp(dimension_semantics=("arbitrary",)),
    )(h, dh, dq, hn, g_mix, wq)


def attn_bwd_kv(h, dh, dk4, dv4, kvn, g_kv, wkv, spread):
    def body(h_ref, dh_ref, dk_ref, dv_ref, kvn_ref, gkv_ref, wkv_ref, sp_ref, out_ref, dw_ref, db_ref, dgkv_ref, acc):
        i = pl.program_id(0)

        @pl.when(i == 0)
        def _():
            for r in (acc, db_ref, dgkv_ref):
                r[...] = jnp.zeros_like(r)

        dkv = jnp.concatenate([_dot_nt(dk_ref[...].astype(bf16), sp_ref[...]),
                               _dot_nt(dv_ref[...].astype(bf16), sp_ref[...])], axis=1)
        dkvb = dkv.astype(bf16)
        acc[...] += _dot_tn(kvn_ref[...], dkvb)
        db_ref[...] += _colsum8(dkv)
        dx, dg = _rms_bwd(h_ref[...], gkv_ref[...], _dot_nt(dkvb, wkv_ref[...]))
        out_ref[...] = dh_ref[...] + dx
        dgkv_ref[...] += _colsum8(dg)

        @pl.when(i == NT - 1)
        def _():
            dw_ref[...] = acc[...].astype(bf16)

    return pl.pallas_call(
        body, name="attn_bwd_kv", grid=(NT,),
        in_specs=[_tile()] * 5 + [_full((1, D)), _full((D, 512)), _full((256, D))],
        out_specs=[_tile(), _full((D, 512)), _full((8, 512)), _full((8, D))],
        out_shape=[SDS((T, D), f32), SDS((D, 512), bf16), SDS((8, 512), f32), SDS((8, D), f32)],
        scratch_shapes=[pltpu.VMEM((D, 512), f32)],
        compiler_params=_cp(dimension_semantics=("arbitrary",)),
    )(h, dh, dk4, dv4, kvn, g_kv, wkv, spread)


def final_loss(h, g, target):
    def body(h_ref, g_ref, t_ref, loss_ref, dh_ref, dg_ref):
        i = pl.program_id(0)

        @pl.when(i == 0)
        def _():
            loss_ref[...] = jnp.zeros_like(loss_ref)
            dg_ref[...] = jnp.zeros_like(dg_ref)

        h_ = h_ref[...]
        g_ = g_ref[...]
        y, _ = _rms(h_, g_)
        diff = y - t_ref[...]
        per_tok = jnp.mean(diff * diff, axis=-1, keepdims=True)
        tot = 0.5 * jnp.sum(per_tok, axis=0, keepdims=True)
        lane = lax.broadcasted_iota(jnp.int32, (8, 128), 1)
        row = lax.broadcasted_iota(jnp.int32, (8, 128), 0)
        loss_ref[...] += jnp.where((lane == 0) & (row == 0), jnp.broadcast_to(tot, (8, 128)), 0.0)
        dx, dgt = _rms_bwd(h_, g_, diff * (1.0 / D))
        dh_ref[...] = dx
        dg_ref[...] += _colsum8(dgt)

    return pl.pallas_call(
        body, name="final_loss", grid=(NT,), in_specs=[_tile(), _full((1, D)), _tile()],
        out_specs=[_full((8, 128)), _tile(), _full((8, D))],
        out_shape=[SDS((8, 128), f32), SDS((T, D), f32), SDS((8, D), f32)],
        compiler_params=_cp(dimension_semantics=("arbitrary",)),
    )(h, g, target)


def _to_chunked(a):
    return a.reshape(S5_CH, S5_STEPS, a.shape[-1]).transpose(1, 0, 2).reshape(T, a.shape[-1])


def _from_chunked(a):
    return a.reshape(S5_STEPS, S5_CH, a.shape[-1]).transpose(1, 0, 2).reshape(T, a.shape[-1])


def _rep4(w):
    return jnp.broadcast_to(w.reshape(w.shape[0], N_KV, 1, HEAD_DIM), (w.shape[0], N_KV, Q_PER_KV, HEAD_DIM)).reshape(
        w.shape[0], N_KV * Q_PER_KV * HEAD_DIM)


def _fold4(w):
    return w.reshape(w.shape[0], N_KV, Q_PER_KV, HEAD_DIM).sum(axis=2).reshape(w.shape[0], N_KV * HEAD_DIM)


def fwd_bwd(x, target, p, shards, core):
    row = lambda v: v.reshape(1, -1)
    (lam, bm, cm), prep_vjp = jax.vjp(s5_discretize, p["s5_a_re"][0], p["s5_a_im"][0], p["s5_log_dt"][0],
                                      p["s5_b_re"][0], p["s5_b_im"][0], p["s5_c_re"][0], p["s5_c_im"][0])
    bmb, cmb = bm.astype(bf16), cm.astype(bf16)
    lam = jnp.concatenate([lam, lam * jnp.array([1.0, -1.0], f32).reshape(1, 2, 1, 1)], axis=1)
    g_mix0, g_mix1 = row(p["norm_mix"][0]), row(p["norm_mix"][1])
    g_mlp0, g_mlp1 = row(p["norm_mlp"][0]), row(p["norm_mlp"][1])
    g_kv, g_fin = row(p["norm_kv"]), row(p["norm_final"])
    bq, bo = p["b_q"], p["b_o"]
    bkv = row(p["b_kv"])
    spread = _spread4()
    sinks = p["sinks"].reshape(16)

    def reduce_pairs(names, bg):
        return [add_pairs(g, r, core, f"add_pairs_{n}") for n, g, r in zip(names, bg.arrs, bg.result)]

    xp = _to_chunked(x)
    hn0 = s5_pre(xp, g_mix0)
    ga = BgGather([shards["s5_w_glu"], shards["vecs"], shards["w_in0"]])
    ys = s5_core_fwd(hn0, bmb, lam, cmb, bg=[ga])
    wglu, gvec, win0 = ga.result
    d_skip = gvec[:, 0, :128].reshape(1, D)
    bglu = gvec[:, 0, 128:].reshape(1, 2 * D)
    gb = BgGather([shards["w_out0"]], mids=(1.0, 1.0))
    y, z, h1 = s5_post(ys, xp, g_mix0, d_skip, wglu, bglu, bg=[gb])
    wout0, = gb.result
    gc = BgGather([shards["w_kv"], shards["w_q"], shards["w_o"], shards["w_in1"]], mids=(0.8, 1.0))
    hm0, r0, h2p = mlp_fwd(h1, g_mlp0, win0, wout0, 0, bg=[gc])
    wkv, wq, wo, win1 = gc.result
    wkv, wq, wo = wkv.reshape(D, 512), wq.reshape(D, D), wo.reshape(D, D)
    h2 = _from_chunked(h2p)
    kvn, hn1, k4, v4, q = attn_pre(h2, g_kv, g_mix1, wkv, bkv, spread, wq, bq)
    gd = BgGather([shards["w_out1"]], mids=(0.94, 1.0))
    o = attn_core_fwd(q, k4, v4, sinks, bg=[gd])
    wout1, = gd.result
    h3 = attn_post(h2, o, wo, bo)
    hm1, r1, h4 = mlp_fwd(h3, g_mlp1, win1, wout1, 1)
    loss, dh4, dg_fin = final_loss(h4, g_fin, target)

    big = {}
    dh3, dwin1, dwout1, dg_mlp1 = mlp_bwd(h3, hm1, r1, g_mlp1, dh4, win1, wout1, 1)
    pa = BgPair([dwin1, dwout1])
    do, dwo, dbo = attn_bwd_pre(dh3, o, wo, bg=[pa])
    p_in1, p_out1 = reduce_pairs(["w_in1", "w_out1"], pa)
    ca = BgChips([p_in1])
    dq, dk4, dv4, dsink = attn_core_bwd(q, do, k4, v4, sinks, bg=[ca])
    big["w_in1"], = zip(ca.arrs, ca.result)
    dh2, dwq, dbq, dg_mix1 = attn_bwd_q(h2, dh3, dq, hn1, g_mix1, wq)
    dh2, dwkv, dbkv, dg_kv = attn_bwd_kv(h2, dh2, dk4, dv4, kvn, g_kv, wkv, spread)
    pb = BgPair([dwkv.reshape(NDEV, 128, 512), dwq.reshape(NDEV, 128, D), dwo.reshape(NDEV, 128, D)])
    ca2 = BgChips([p_out1])
    dh2p = _to_chunked(dh2)
    dh1, dwin0, dwout0, dg_mlp0 = mlp_bwd(h1, hm0, r0, g_mlp0, dh2p, win0, wout0, 0, bg=[pb, ca2])
    big["w_out1"], = zip(ca2.arrs, ca2.result)
    cb = BgChips(reduce_pairs(["w_kv", "w_q", "w_o"], pb))
    pc = BgPair([dwin0, dwout0])
    dy, dwglu, dbglu = s5_post_bwd(dh1, y, z, wglu, bg=[cb, pc])
    big["w_kv"], big["w_q"], big["w_o"] = zip(cb.arrs, cb.result)
    cc = BgChips(reduce_pairs(["w_in0", "w_out0"], pc))
    pd = BgPair([dwglu])
    du, dbm, dcmt, dlam = s5_core_bwd(hn0, dy, bmb, lam, cmb, bg=[cc, pd])
    big["w_in0"], big["w_out0"] = zip(cc.arrs, cc.result)
    cd = BgChips(reduce_pairs(["s5_w_glu"], pd))
    dxp, dg_mix0, dd = s5_pre_bwd(xp, g_mix0, du, dy, d_skip, dh1, bg=[cd])
    big["s5_w_glu"], = zip(cd.arrs, cd.result)
    grad_x = _from_chunked(dxp)
    da_re, da_im, dlog_dt, db_re, db_im, dc_re, dc_im = prep_vjp((dlam, dbm, dcmt.transpose(0, 2, 1)))

    def lanes(v_):
        v_ = v_.reshape(1, -1)
        return jnp.pad(v_, ((0, 0), (0, D - v_.shape[1])))

    small = jnp.concatenate([
        dg_mix0[0:1], dg_mix1[0:1], dg_mlp0[0:1], dg_mlp1[0:1], dg_kv[0:1], dg_fin[0:1], dd[0:1], dbq[0:1], dbo[0:1],
        dbglu[0:1].reshape(2, D), lanes(dbkv[0:1]),
        lanes(dsink[:, 0, :Q_PER_KV]), lanes(dlog_dt), lanes(loss[0:1, 0:1]), jnp.zeros((1, D), f32),
        da_re.reshape(4, D), da_im.reshape(4, D),
        db_re.transpose(0, 2, 1).reshape(64, D), db_im.transpose(0, 2, 1).reshape(64, D),
        dc_re.reshape(64, D), dc_im.reshape(64, D)], axis=0)
    return loss, grad_x, small, big


_ANY = pl.BlockSpec(memory_space=pl.ANY)


def _pos():
    return lax.axis_index("x"), lax.axis_index("y"), lax.axis_index("c")


def _other_chips(x, y):
    return [(1 - x, y), (x, 1 - y), (1 - x, 1 - y)]


def all_gather(arrs):
    n = len(arrs)

    def body(*refs):
        ins, outs = refs[:n], refs[n:2 * n]
        send_sems, recv_sems, local_sems = refs[2 * n:]
        x, y, c = _pos()
        me, sib = (x, y, c), (x, y, 1 - c)
        chips = _other_chips(x, y)

        def copy(a, k, block, to, src=None):
            dst = outs[a].at[4 * block[0] + 2 * block[1] + block[2]]
            return pltpu.make_async_remote_copy(
                src_ref=dst if src is None else src, dst_ref=dst, send_sem=send_sems.at[a, k],
                recv_sem=recv_sems.at[a, k], device_id=to, device_id_type=MESH)

        mine = [pltpu.make_async_copy(ins[a], outs[a].at[4 * x + 2 * y + c], local_sems.at[a]) for a in range(n)]
        for cp in mine:
            cp.start()
        first = []
        for a in range(n):
            first.append(copy(a, 0, me, sib, src=ins[a]))
            first += [copy(a, 1 + j, me, (*chip, c), src=ins[a]) for j, chip in enumerate(chips)]
        for cp in first:
            cp.start()
        passed = []
        for j, chip in enumerate(chips):
            for a in range(n):
                copy(a, 1 + j, (*chip, c), me).wait_recv()
                cp = copy(a, 4 + j, (*chip, c), sib)
                cp.start()
                passed.append(cp)
        for a in range(n):
            copy(a, 0, sib, me).wait_recv()
            for j, chip in enumerate(chips):
                copy(a, 4 + j, (*chip, 1 - c), me).wait_recv()
        for cp in first + passed:
            cp.wait_send()
        for cp in mine:
            cp.wait()

    return pl.pallas_call(
        body, name="all_gather", in_specs=[_ANY] * n, out_specs=[_ANY] * n,
        out_shape=[SDS((NDEV,) + a.shape, a.dtype) for a in arrs],
        scratch_shapes=[pltpu.SemaphoreType.DMA((n, 7)), pltpu.SemaphoreType.DMA((n, 7)),
                        pltpu.SemaphoreType.DMA((n,))],
    )(*arrs)


def rs_pair(grads):
    n = len(grads)

    def body(*refs):
        ins, outs = refs[:n], refs[n:2 * n]
        send_sems, recv_sems = refs[2 * n:]
        x, y, c = _pos()
        cps = []
        for a in range(n):
            for k in range(4):
                cps.append(pltpu.make_async_remote_copy(
                    src_ref=ins[a].at[2 * k + 1 - c], dst_ref=outs[a].at[k], send_sem=send_sems.at[a, k],
                    recv_sem=recv_sems.at[a, k], device_id=(x, y, 1 - c), device_id_type=MESH))
        for cp in cps:
            cp.start()
        for cp in cps:
            cp.wait_recv()
        for cp in cps:
            cp.wait_send()

    return pl.pallas_call(
        body, name="rs_pair", in_specs=[_ANY] * n, out_specs=[_ANY] * n,
        out_shape=[SDS((4,) + g.shape[1:], g.dtype) for g in grads],
        scratch_shapes=[pltpu.SemaphoreType.DMA((n, 4)), pltpu.SemaphoreType.DMA((n, 4))],
    )(*grads)


def rs_chips(parts):
    n = len(parts)

    def body(*refs):
        ins, outs = refs[:n], refs[n:2 * n]
        send_sems, recv_sems = refs[2 * n:]
        x, y, c = _pos()
        cps = []
        for a in range(n):
            for r, (px, py) in enumerate(_other_chips(x, y)):
                cps.append(pltpu.make_async_remote_copy(
                    src_ref=ins[a].at[2 * px + py], dst_ref=outs[a].at[r], send_sem=send_sems.at[a, r],
                    recv_sem=recv_sems.at[a, r], device_id=(px, py, c), device_id_type=MESH))
        for cp in cps:
            cp.start()
        for cp in cps:
            cp.wait_recv()
        for cp in cps:
            cp.wait_send()

    return pl.pallas_call(
        body, name="rs_chips", in_specs=[_ANY] * n, out_specs=[_ANY] * n,
        out_shape=[SDS((3,) + g.shape[1:], g.dtype) for g in parts],
        scratch_shapes=[pltpu.SemaphoreType.DMA((n, 3)), pltpu.SemaphoreType.DMA((n, 3))],
    )(*parts)


def _row_tile(r, c):
    return min(r, max(8, (256 * 1024) // c))


def add_pairs(g, r1, core, name):
    _, R, C = g.shape
    tr = _row_tile(R, C)

    def body(core_ref, g_ref, r_ref, o_ref):
        o_ref[...] = (g_ref[...].astype(f32) + r_ref[...].astype(f32)).astype(bf16)

    return pl.pallas_call(
        body, name=name, out_shape=SDS((4, R, C), bf16),
        grid_spec=pltpu.PrefetchScalarGridSpec(
            num_scalar_prefetch=1, grid=(4, R // tr),
            in_specs=[pl.BlockSpec((None, tr, C), lambda k, i, core: (2 * k + core[0], i, 0)),
                      pl.BlockSpec((None, tr, C), lambda k, i, core: (k, i, 0))],
            out_specs=pl.BlockSpec((None, tr, C), lambda k, i, core: (k, i, 0))),
        compiler_params=_cp(dimension_semantics=("arbitrary", "arbitrary")),
    )(core, g, r1)


def _adamw(w, g, m, v):
    m = ADAM_B1 * m + (1.0 - ADAM_B1) * g
    v = ADAM_B2 * v + (1.0 - ADAM_B2) * (g * g)
    m_hat = m / (1.0 - ADAM_B1 ** ADAM_STEP)
    v_hat = v / (1.0 - ADAM_B2 ** ADAM_STEP)
    delta = -ADAM_LR * (m_hat / (jnp.sqrt(v_hat) + ADAM_EPS) + ADAM_WD * w)
    return delta, m, v


def adam_big(w, m, v, part, r2, chip, name, layer=0, prev=None):
    L, R, C = w.shape
    tr = _row_tile(R, C)

    def body(chip_ref, w_ref, m_ref, v_ref, p_ref, r_ref, *rest):
        g_out, d_out, m_out, v_out = rest[-4:]
        g = p_ref[...].astype(f32) + r_ref[0].astype(f32) + r_ref[1].astype(f32) + r_ref[2].astype(f32)
        d, m_, v_ = _adamw(w_ref[...], g, m_ref[...], v_ref[...])
        g_out[...] = g
        d_out[...] = d
        m_out[...] = m_
        v_out[...] = v_

    blk = pl.BlockSpec((None, tr, C), lambda i, chip: (layer, i, 0))
    extra = [] if prev is None else list(prev)
    return pl.pallas_call(
        body, name=name, out_shape=[SDS((L, R, C), f32)] * 4,
        grid_spec=pltpu.PrefetchScalarGridSpec(
            num_scalar_prefetch=1, grid=(R // tr,),
            in_specs=[blk, blk, blk,
                      pl.BlockSpec((None, tr, C), lambda i, chip: (chip[0], i, 0)),
                      pl.BlockSpec((3, tr, C), lambda i, chip: (0, i, 0))] + [_ANY] * len(extra),
            out_specs=[blk] * 4),
        input_output_aliases={6 + k: k for k in range(len(extra))},
        compiler_params=_cp(dimension_semantics=("arbitrary",)),
    )(chip, w, m, v, part, r2, *extra)


def allreduce_small(buf):
    shp = buf.shape
    half = (shp[0] // 16) * 8
    parts = (pl.ds(0, half), pl.ds(half, shp[0] - half))

    def body(in_ref, out_ref, acc1, acc2, r0, r1, r2, send_sems, recv_sems):
        x, y, c = _pos()
        across = [(1 - x, y, c), (x, 1 - y, c)]

        def exchange(src, rcv, dst, copies):
            cps = [pltpu.make_async_remote_copy(
                src_ref=src.at[rows], dst_ref=rcv.at[rows], send_sem=send_sems.at[k], recv_sem=recv_sems.at[k],
                device_id=peer, device_id_type=MESH) for k, rows, peer in copies]
            for cp in cps:
                cp.start()
            for cp in cps:
                cp.wait()
            dst[...] = src[...] + rcv[...]

        exchange(in_ref, r0, acc1, [(0, pl.ds(0, shp[0]), (x, y, 1 - c))])
        exchange(acc1, r1, acc2, [(1, parts[0], across[0]), (2, parts[1], across[1])])
        exchange(acc2, r2, out_ref, [(3, parts[0], across[1]), (4, parts[1], across[0])])

    return pl.pallas_call(
        body, name="allreduce_small", out_shape=SDS(shp, f32),
        scratch_shapes=[pltpu.VMEM(shp, f32)] * 5 + [pltpu.SemaphoreType.DMA((5,)), pltpu.SemaphoreType.DMA((5,))],
    )(buf)


SMALL_ROWS = {'norm_mix': (0, 2, D), 'norm_mlp': (2, 2, D), 'norm_kv': (4, 1, D), 'norm_final': (5, 1, D),
              's5_d': (6, 1, D), 'b_q': (7, 1, D), 'b_o': (8, 1, D), 's5_b_glu': (9, 2, D), 'b_kv': (11, 1, 512),
              'sinks': (12, 1, 16), 's5_log_dt': (13, 1, 64), 's5_a_re': (16, 4, D), 's5_a_im': (20, 4, D),
              's5_b_re': (24, 64, D), 's5_b_im': (88, 64, D), 's5_c_re': (152, 64, D), 's5_c_im': (216, 64, D)}
LOSS_ROW = 14
ROW_PARAMS = ['norm_mix', 'norm_mlp', 'norm_kv', 'norm_final', 'b_q', 'b_o', 'b_kv', 'sinks', 's5_log_dt']
SHARD_PARAMS = ['s5_d', 's5_b_glu']
S5_PARAMS = ['s5_a_re', 's5_a_im', 's5_b_re', 's5_b_im', 's5_c_re', 's5_c_im']


def adam_small(dev, gsum, s5_grads, w, m, v):
    names = ROW_PARAMS + SHARD_PARAMS + S5_PARAMS
    n_g = len(ROW_PARAMS) + len(SHARD_PARAMS)

    def body(dev_ref, gs_ref, *refs):
        pos = [0]

        def take(k):
            r = refs[pos[0]:pos[0] + k]
            pos[0] += k
            return r

        g5 = take(len(S5_PARAMS))
        wr, mr, vr = take(len(names)), take(len(names)), take(len(names))
        g_out = take(n_g)
        d_out, m_out, v_out = take(len(names)), take(len(names)), take(len(names))
        dv = dev_ref[0]
        for i, n in enumerate(names):
            if n in S5_PARAMS:
                g = g5[S5_PARAMS.index(n)][...]
            elif n in SHARD_PARAMS:
                r0, _, _ = SMALL_ROWS[n]
                ln = wr[i].shape[1]
                g = jnp.zeros((1, ln), f32)
                for k in range(NDEV):
                    off = k * ln
                    piece = gs_ref[r0 + off // D:r0 + off // D + 1, off % D:off % D + ln]
                    g = g + jnp.where(dv == k, piece, 0.0)
                g_out[i][...] = g
            else:
                r0, nr, nl = SMALL_ROWS[n]
                g = gs_ref[r0:r0 + nr, 0:nl]
                g_out[i][...] = g
            d, m_, v_ = _adamw(wr[i][...], g, mr[i][...], vr[i][...])
            d_out[i][...] = d
            m_out[i][...] = m_
            v_out[i][...] = v_

    vm = pl.BlockSpec(memory_space=pltpu.VMEM)
    ins = [s5_grads[n] for n in S5_PARAMS] + [d[n] for d in (w, m, v) for n in names]
    shapes = [SDS(w[n].shape, f32) for n in names]
    res = pl.pallas_call(
        body, name="adam_small", in_specs=[pl.BlockSpec(memory_space=pltpu.SMEM)] + [vm] * (1 + len(ins)),
        out_specs=[vm] * (n_g + 3 * len(names)), out_shape=shapes[:n_g] + shapes * 3,
        compiler_params=_cp(),
    )(dev, gsum, *ins)
    g_o = dict(zip(names[:n_g], res[:n_g]))
    rest = res[n_g:]
    k = len(names)
    return g_o, dict(zip(names, rest[:k])), dict(zip(names, rest[k:2 * k])), dict(zip(names, rest[2 * k:]))


WEIGHTS = ['norm_mix', 'norm_mlp', 'norm_kv', 'norm_final', 's5_a_re', 's5_a_im', 's5_log_dt', 's5_b_re', 's5_b_im',
           's5_c_re', 's5_c_im', 's5_d', 's5_w_glu', 's5_b_glu', 'w_kv', 'b_kv', 'w_q', 'b_q', 'sinks', 'w_o', 'b_o',
           'w_mlp_in', 'w_mlp_out']
BIG = ['s5_w_glu', 'w_kv', 'w_q', 'w_o', 'w_mlp_in', 'w_mlp_out']
BIG_2D = {'s5_w_glu': (D, 256), 'w_kv': (128, 512), 'w_q': (128, D), 'w_o': (128, D), 'w_mlp_in': (2 * D, 512),
          'w_mlp_out': (2 * 512, D)}
SHARDED_SMALL = {'s5_d': D, 's5_b_glu': 2 * D}
SMALL = [n for n in WEIGHTS if n not in BIG]
SMALL_SIZE = {'norm_mix': 2 * D, 'norm_mlp': 2 * D, 'norm_kv': D, 'norm_final': D, 's5_a_re': 4096, 's5_a_im': 4096,
              's5_log_dt': 64, 's5_b_re': 65536, 's5_b_im': 65536, 's5_c_re': 65536, 's5_c_im': 65536, 's5_d': D,
              's5_b_glu': 2 * D, 'b_kv': 512, 'b_q': D, 'sinks': 16, 'b_o': D}


def _pack(vals):
    parts = []
    for n in SMALL:
        v = vals[n].reshape(-1).astype(f32)
        parts.append(jnp.pad(v, (0, (-v.shape[0]) % 128)))
    flat = jnp.concatenate(parts)
    flat = jnp.pad(flat, (0, (-flat.shape[0]) % 1024))
    return flat.reshape(-1, 128)


def _unpack(buf):
    flat = buf.reshape(-1)
    out, off = {}, 0
    for n in SMALL:
        sz = SMALL_SIZE[n]
        out[n] = flat[off:off + sz]
        off += sz + (-sz) % 128
    return out


def kernel(x, norm_mix, norm_mlp, norm_kv, norm_final, s5_a_re, s5_a_im, s5_log_dt, s5_b_re, s5_b_im, s5_c_re, s5_c_im, s5_d, s5_w_glu, s5_b_glu, w_kv, b_kv, w_q, b_q, sinks, w_o, b_o, w_mlp_in, w_mlp_out, loss_target, m_norm_mix, m_norm_mlp, m_norm_kv, m_norm_final, m_s5_a_re, m_s5_a_im, m_s5_log_dt, m_s5_b_re, m_s5_b_im, m_s5_c_re, m_s5_c_im, m_s5_d, m_s5_w_glu, m_s5_b_glu, m_w_kv, m_b_kv, m_w_q, m_b_q, m_sinks, m_w_o, m_b_o, m_w_mlp_in, m_w_mlp_out, v_norm_mix, v_norm_mlp, v_norm_kv, v_norm_final, v_s5_a_re, v_s5_a_im, v_s5_log_dt, v_s5_b_re, v_s5_b_im, v_s5_c_re, v_s5_c_im, v_s5_d, v_s5_w_glu, v_s5_b_glu, v_w_kv, v_b_kv, v_w_q, v_b_q, v_sinks, v_w_o, v_b_o, v_w_mlp_in, v_w_mlp_out):
    w = dict(norm_mix=norm_mix, norm_mlp=norm_mlp, norm_kv=norm_kv, norm_final=norm_final, s5_a_re=s5_a_re,
             s5_a_im=s5_a_im, s5_log_dt=s5_log_dt, s5_b_re=s5_b_re, s5_b_im=s5_b_im, s5_c_re=s5_c_re, s5_c_im=s5_c_im,
             s5_d=s5_d, s5_w_glu=s5_w_glu, s5_b_glu=s5_b_glu, w_kv=w_kv, b_kv=b_kv, w_q=w_q, b_q=b_q, sinks=sinks,
             w_o=w_o, b_o=b_o, w_mlp_in=w_mlp_in, w_mlp_out=w_mlp_out)
    m = dict(norm_mix=m_norm_mix, norm_mlp=m_norm_mlp, norm_kv=m_norm_kv, norm_final=m_norm_final, s5_a_re=m_s5_a_re,
             s5_a_im=m_s5_a_im, s5_log_dt=m_s5_log_dt, s5_b_re=m_s5_b_re, s5_b_im=m_s5_b_im, s5_c_re=m_s5_c_re,
             s5_c_im=m_s5_c_im, s5_d=m_s5_d, s5_w_glu=m_s5_w_glu, s5_b_glu=m_s5_b_glu, w_kv=m_w_kv, b_kv=m_b_kv,
             w_q=m_w_q, b_q=m_b_q, sinks=m_sinks, w_o=m_w_o, b_o=m_b_o, w_mlp_in=m_w_mlp_in, w_mlp_out=m_w_mlp_out)
    v = dict(norm_mix=v_norm_mix, norm_mlp=v_norm_mlp, norm_kv=v_norm_kv, norm_final=v_norm_final, s5_a_re=v_s5_a_re,
             s5_a_im=v_s5_a_im, s5_log_dt=v_s5_log_dt, s5_b_re=v_s5_b_re, s5_b_im=v_s5_b_im, s5_c_re=v_s5_c_re,
             s5_c_im=v_s5_c_im, s5_d=v_s5_d, s5_w_glu=v_s5_w_glu, s5_b_glu=v_s5_b_glu, w_kv=v_w_kv, b_kv=v_b_kv,
             w_q=v_w_q, b_q=v_b_q, sinks=v_sinks, w_o=v_w_o, b_o=v_b_o, w_mlp_in=v_w_mlp_in, w_mlp_out=v_w_mlp_out)
    xi, yi, ci = _pos()
    dev = 4 * xi + 2 * yi + ci
    core = ci.reshape(1).astype(jnp.int32)
    chip = (2 * xi + yi).reshape(1).astype(jnp.int32)

    shards = {
        "s5_w_glu": s5_w_glu[0].astype(bf16), "w_kv": w_kv.astype(bf16), "w_q": w_q[0].astype(bf16),
        "w_o": w_o[0].astype(bf16), "w_in0": w_mlp_in[0].astype(bf16), "w_in1": w_mlp_in[1].astype(bf16),
        "w_out0": w_mlp_out[0].astype(bf16), "w_out1": w_mlp_out[1].astype(bf16),
        "vecs": jnp.broadcast_to(jnp.concatenate([s5_d, s5_b_glu], axis=1), (8, 384)),
    }
    _, grad_x, grads, big = fwd_bwd(x[0], loss_target[0], {n: w[n] for n in SMALL}, shards, core)

    out_g, out_d, out_m, out_v = {}, {}, {}, {}
    for n in ("s5_w_glu", "w_kv", "w_q", "w_o"):
        shp = w[n].shape
        r3 = (1,) + BIG_2D[n]
        res = adam_big(w[n].reshape(r3), m[n].reshape(r3), v[n].reshape(r3), *big[n], chip, f"adam_{n}")
        out_g[n], out_d[n], out_m[n], out_v[n] = [r.reshape(shp) for r in res]
    for n, k in (("w_mlp_in", "w_in"), ("w_mlp_out", "w_out")):
        res = adam_big(w[n], m[n], v[n], *big[k + "1"], chip, f"adam_{k}1", layer=1)
        res = adam_big(w[n], m[n], v[n], *big[k + "0"], chip, f"adam_{k}0", layer=0, prev=res)
        out_g[n], out_d[n], out_m[n], out_v[n] = res

    gsum = allreduce_small(grads)
    loss = gsum[LOSS_ROW, 0]
    swapped = ("s5_b_re", "s5_b_im")
    swap = lambda a: a.transpose(0, 1, 3, 2)

    def kernel_side(d):
        d = {n: (d[n].reshape(1, -1) if d[n].ndim == 1 else d[n]) for n in SMALL}
        d.update({n: swap(d[n]) for n in swapped})
        return d

    s5_g = {}
    for n in S5_PARAMS:
        r0, nr, _ = SMALL_ROWS[n]
        s5_g[n] = gsum[r0:r0 + nr].reshape((1, 64, 16, 64) if n in swapped else w[n].shape)
        out_g[n] = s5_g[n]
    g_s, d_s, m_s, v_s = adam_small(dev.reshape(1).astype(jnp.int32), gsum, s5_g, kernel_side(w), kernel_side(m),
                                    kernel_side(v))
    for src, dst in ((g_s, out_g), (d_s, out_d), (m_s, out_m), (v_s, out_v)):
        dst.update(src)
    for dst in (out_g, out_d, out_m, out_v):
        for n in SMALL:
            dst[n] = (swap(dst[n]) if n in swapped else dst[n]).reshape(w[n].shape)

    return (loss, grad_x[None], *[out_g[n] for n in WEIGHTS], *[out_d[n] for n in WEIGHTS],
            *[out_m[n] for n in WEIGHTS], *[out_v[n] for n in WEIGHTS])
```

```python
import functools
import math

import jax
import jax.numpy as jnp
from jax import lax
from jax.experimental import pallas as pl
from jax.experimental.pallas import tpu as pltpu

f32 = jnp.float32
bf16 = jnp.bfloat16
SDS = jax.ShapeDtypeStruct

T = 2048
D = 1024
NDEV = 8
NORM_EPS = 1e-5
S5_G, S5_C, S5_P = 64, 16, 64
S5_SUB = 8
S5_CH = 8
S5_STEPS = T // S5_CH
DT_MIN_LAMBDA = -1e-4
HEAD_DIM = 64
N_KV = 4
Q_PER_KV = 4
BLK = 128
D_FF_SHARD = 512
ADAM_LR, ADAM_B1, ADAM_B2, ADAM_EPS, ADAM_WD, ADAM_STEP = 0.001, 0.9, 0.999, 1e-08, 0.01, 10
VMEM_LIMIT = 56 * 1024 * 1024
MESH = pl.DeviceIdType.MESH


def _cp(**kw):
    return pltpu.CompilerParams(vmem_limit_bytes=VMEM_LIMIT, **kw)


def _dot(a, b):
    return jnp.dot(a, b, preferred_element_type=f32)


def _dot_nt(a, b):
    return lax.dot_general(a, b, (((1,), (1,)), ((), ())), preferred_element_type=f32)


def _dot_tn(a, b):
    return lax.dot_general(a, b, (((0,), (0,)), ((), ())), preferred_element_type=f32)


def _rms(x, g):
    r = lax.rsqrt(jnp.mean(x * x, axis=-1, keepdims=True) + NORM_EPS)
    return x * r * g, r


def _rms_bwd(x, g, dy):
    r = lax.rsqrt(jnp.mean(x * x, axis=-1, keepdims=True) + NORM_EPS)
    u = dy * g
    dx = r * u - (r * r * r) * x * jnp.mean(u * x, axis=-1, keepdims=True)
    return dx, dy * x * r


def _colsum8(v):
    s = jnp.sum(v, axis=0, keepdims=True)
    row = lax.broadcasted_iota(jnp.int32, (8, v.shape[1]), 0)
    return jnp.where(row == 0, jnp.broadcast_to(s, (8, v.shape[1])), 0.0)


def _full(shape):
    nd = len(shape)
    return pl.BlockSpec(shape, lambda *_: (0,) * nd, pipeline_mode=pl.Buffered(1))


_ANY = pl.BlockSpec(memory_space=pl.ANY)


def _pos():
    return lax.axis_index("x"), lax.axis_index("y"), lax.axis_index("c")


def _other_chips(x, y):
    return [(1 - x, y), (x, 1 - y), (1 - x, 1 - y)]


class BgGather:
    SIB, XN, YN, FWD_Y, FWD_X, SIB_X, SIB_Y, SIB_D = range(8)

    def __init__(self, arrs, mids=(0.5, 0.75)):
        n = len(arrs)
        self.arrs = list(arrs)
        self.out_shape = [SDS((NDEV,) + a.shape, a.dtype) for a in arrs]
        self.scratch = [pltpu.SemaphoreType.DMA((n, 8)), pltpu.SemaphoreType.DMA((n, 8)),
                        pltpu.SemaphoreType.DMA((n,))]
        self.mids = mids
        self.result = None

    def mid_steps(self, nsteps):
        at = lambda f: min(nsteps - 1, max(0, int(f * nsteps) - 1))
        return [(at(self.mids[0]), self.mid), (max(at(self.mids[0]), at(self.mids[1])), self.mid2)]

    def _halves(self, a):
        rows = self.arrs[a].shape[0]
        cut = rows // 2 if rows >= 32 else rows
        return (0, cut), (cut, rows - cut)

    def _copy(self, ins, outs, sems, a, k, block, to, own=False, part=None):
        slot = 4 * block[0] + 2 * block[1] + block[2]
        rows = pl.ds(0, self.arrs[a].shape[0]) if part is None else pl.ds(*self._halves(a)[part])
        dst = outs[a].at[slot, rows]
        return pltpu.make_async_remote_copy(
            src_ref=ins[a].at[rows] if own else dst, dst_ref=dst, send_sem=sems[0].at[a, k],
            recv_sem=sems[1].at[a, k], device_id=to, device_id_type=MESH)

    def _mine(self, ins, outs, sems):
        x, y, c = _pos()
        return [pltpu.make_async_copy(ins[a], outs[a].at[4 * x + 2 * y + c], sems[2].at[a])
                for a in range(len(self.arrs))]

    def _split(self, a):
        return self._halves(a)[1][1] > 0

    def _sends(self, ins, outs, sems, phase):
        x, y, c = _pos()
        me, sib, xn, yn, dg = (x, y, c), (x, y, 1 - c), (1 - x, y, c), (x, 1 - y, c), (1 - x, 1 - y, c)
        cps = []
        for a in range(len(self.arrs)):
            cp = lambda k, block, to, **kw: self._copy(ins, outs, sems, a, k, block, to, **kw)
            if phase == 0:
                cps += [cp(self.SIB, me, sib, own=True), cp(self.XN, me, xn, own=True), cp(self.YN, me, yn, own=True)]
            elif phase == 1:
                cps.append(cp(self.FWD_Y, xn, yn, part=0))
                if self._split(a):
                    cps.append(cp(self.FWD_X, yn, xn, part=1))
                cps += [cp(self.SIB_X, xn, sib), cp(self.SIB_Y, yn, sib)]
            else:
                cps.append(cp(self.SIB_D, dg, sib))
        return cps

    def _arrivals(self, ins, outs, sems, phase):
        x, y, c = _pos()
        me, xn, yn, dg = (x, y, c), (1 - x, y, c), (x, 1 - y, c), (1 - x, 1 - y, c)
        cps = []
        for a in range(len(self.arrs)):
            cp = lambda k, block, **kw: self._copy(ins, outs, sems, a, k, block, me, **kw)
            if phase == 1:
                cps += [cp(self.XN, xn), cp(self.YN, yn)]
            elif phase == 2:
                cps.append(cp(self.FWD_Y, dg, part=0))
                if self._split(a):
                    cps.append(cp(self.FWD_X, dg, part=1))
            else:
                cps += [cp(self.SIB, (x, y, 1 - c)), cp(self.SIB_X, (1 - x, y, 1 - c)),
                        cp(self.SIB_Y, (x, 1 - y, 1 - c)), cp(self.SIB_D, (1 - x, 1 - y, 1 - c))]
        return cps

    def start(self, ins, outs, sems):
        for cp in self._mine(ins, outs, sems) + self._sends(ins, outs, sems, 0):
            cp.start()

    def mid(self, ins, outs, sems):
        for cp in self._arrivals(ins, outs, sems, 1):
            cp.wait_recv()
        for cp in self._sends(ins, outs, sems, 1):
            cp.start()

    def mid2(self, ins, outs, sems):
        for cp in self._arrivals(ins, outs, sems, 2):
            cp.wait_recv()
        for cp in self._sends(ins, outs, sems, 2):
            cp.start()

    def finish(self, ins, outs, sems):
        for cp in self._arrivals(ins, outs, sems, 3):
            cp.wait_recv()
        for ph in range(3):
            for cp in self._sends(ins, outs, sems, ph):
                cp.wait_send()
        for cp in self._mine(ins, outs, sems):
            cp.wait()


class BgPair:
    def __init__(self, arrs):
        n = len(arrs)
        self.arrs = list(arrs)
        self.out_shape = [SDS((4,) + a.shape[1:], a.dtype) for a in arrs]
        self.scratch = [pltpu.SemaphoreType.DMA((n, 4)), pltpu.SemaphoreType.DMA((n, 4))]
        self.result = None

    def mid_steps(self, nsteps):
        return []

    def _copies(self, ins, outs, sems):
        x, y, c = _pos()
        return [pltpu.make_async_remote_copy(
            src_ref=ins[a].at[2 * k + 1 - c], dst_ref=outs[a].at[k], send_sem=sems[0].at[a, k],
            recv_sem=sems[1].at[a, k], device_id=(x, y, 1 - c), device_id_type=MESH)
            for a in range(len(self.arrs)) for k in range(4)]

    def start(self, ins, outs, sems):
        for cp in self._copies(ins, outs, sems):
            cp.start()

    def finish(self, ins, outs, sems):
        cps = self._copies(ins, outs, sems)
        for cp in cps:
            cp.wait_recv()
        for cp in cps:
            cp.wait_send()


class BgChips(BgPair):
    def __init__(self, arrs):
        n = len(arrs)
        self.arrs = list(arrs)
        self.out_shape = [SDS((3,) + a.shape[1:], a.dtype) for a in arrs]
        self.scratch = [pltpu.SemaphoreType.DMA((n, 3)), pltpu.SemaphoreType.DMA((n, 3))]
        self.result = None

    def _copies(self, ins, outs, sems):
        x, y, c = _pos()
        return [pltpu.make_async_remote_copy(
            src_ref=ins[a].at[2 * px + py], dst_ref=outs[a].at[r], send_sem=sems[0].at[a, r],
            recv_sem=sems[1].at[a, r], device_id=(px, py, c), device_id_type=MESH)
            for a in range(len(self.arrs)) for r, (px, py) in enumerate(_other_chips(x, y))]


def _call(bgs, body, *, name, grid, in_specs, out_specs, out_shape, scratch_shapes=(), compiler_params=None):
    single = not isinstance(out_shape, (list, tuple))
    out_specs_l = [out_specs] if single else list(out_specs)
    out_shape_l = [out_shape] if single else list(out_shape)
    bgs = [b for b in (bgs or []) if b is not None]
    n_in, n_out, n_sc = len(in_specs), len(out_shape_l), len(scratch_shapes)
    nsteps = math.prod(grid)

    def full(*refs):
        pos = [0]

        def take(k):
            r = refs[pos[0]:pos[0] + k]
            pos[0] += k
            return r

        ins = take(n_in)
        b_ins = [take(len(b.arrs)) for b in bgs]
        outs = take(n_out)
        b_outs = [take(len(b.out_shape)) for b in bgs]
        sc = take(n_sc)
        b_sc = [take(len(b.scratch)) for b in bgs]
        if bgs:
            step = pl.program_id(0)
            for d in range(1, len(grid)):
                step = step * grid[d] + pl.program_id(d)

            @pl.when(step == 0)
            def _():
                for b, i_, o_, s_ in zip(bgs, b_ins, b_outs, b_sc):
                    b.start(i_, o_, s_)

        body(*ins, *outs, *sc)
        if bgs:
            for b, i_, o_, s_ in zip(bgs, b_ins, b_outs, b_sc):
                for at, fn in b.mid_steps(nsteps):
                    @pl.when(step == at)
                    def _():
                        fn(i_, o_, s_)

            @pl.when(step == nsteps - 1)
            def _():
                for b, i_, o_, s_ in zip(bgs, b_ins, b_outs, b_sc):
                    b.finish(i_, o_, s_)

    def run(*args):
        res = pl.pallas_call(
            full, name=name, grid=grid,
            in_specs=list(in_specs) + [_ANY] * sum(len(b.arrs) for b in bgs),
            out_specs=out_specs_l + [_ANY] * sum(len(b.out_shape) for b in bgs),
            out_shape=out_shape_l + [s for b in bgs for s in b.out_shape],
            scratch_shapes=list(scratch_shapes) + [s for b in bgs for s in b.scratch],
            compiler_params=compiler_params,
        )(*args, *[a for b in bgs for a in b.arrs])
        rest = list(res[n_out:])
        for b in bgs:
            b.result, rest = rest[:len(b.out_shape)], rest[len(b.out_shape):]
        return res[0] if single else list(res[:n_out])

    return run


def s5_discretize(a_re, a_im, log_dt, b_re, b_im, c_re, c_im):
    lam_r = jnp.minimum(a_re, DT_MIN_LAMBDA)
    lam_i = a_im
    dt = jnp.exp(log_dt)[:, None]
    e = jnp.exp(lam_r * dt)
    lbr = e * jnp.cos(lam_i * dt)
    lbi = e * jnp.sin(lam_i * dt)
    den = lam_r * lam_r + lam_i * lam_i
    cf_r = ((lbr - 1.0) * lam_r + lbi * lam_i) / den
    cf_i = (lbi * lam_r - (lbr - 1.0) * lam_i) / den
    bb_r = cf_r[:, :, None] * b_re - cf_i[:, :, None] * b_im
    bb_i = cf_r[:, :, None] * b_im + cf_i[:, :, None] * b_re
    eye = jnp.eye(8, dtype=f32)

    def blk_b(m):
        return jnp.einsum('bgpc,gh->bgchp', m.reshape(8, 8, S5_P, S5_C), eye).reshape(8, 128, 512)

    def blk_c(m):
        return jnp.einsum('bgcp,gh->bgphc', m.reshape(8, 8, S5_C, S5_P), eye).reshape(8, 512, 128)

    bm = jnp.concatenate([blk_b(bb_r), blk_b(bb_i)], axis=-1)
    cm = jnp.concatenate([blk_c(c_re), -blk_c(c_im)], axis=1)
    lam = jnp.stack([lbr.reshape(8, 512), lbi.reshape(8, 512)], axis=1)
    lam = jnp.broadcast_to(lam[:, :, None, :], (8, 2, 8, 512))
    return lam, bm, cm


def _cmul(ar, ai, br, bi):
    return ar * br - ai * bi, ar * bi + ai * br


def _shift_rows(v, k, up):
    row = lax.broadcasted_iota(jnp.int32, v.shape, 0)
    if up:
        return jnp.where(row < 8 - k, pltpu.roll(v, 8 - k, 0), 0.0)
    return jnp.where(row >= k, pltpu.roll(v, k, 0), 0.0)


def _chunk_scan(S, lr, li, reverse, aux=None):
    z = jnp.zeros((8, 512), f32)
    U = 4

    def idx(i):
        return (S5_STEPS - 1 - i) if reverse else i

    def rows_of(s):
        return pl.ds(s * 8, 8) if isinstance(s, int) else pl.ds(pl.multiple_of(s * 8, 8), 8)

    def rec(xr, xi, row):
        br = S[row, 0:512]
        bi = S[row, 512:1024]
        return lr * xr - li * xi + br, lr * xi + li * xr + bi

    def step1(i, c):
        for u in range(U):
            c = rec(c[0], c[1], rows_of(idx(i * U + u)))
        return c

    er, ei = lax.fori_loop(0, S5_STEPS // U, step1, (z, z))
    ar, ai = lr, li
    for _ in range(8):
        ar, ai = _cmul(ar, ai, ar, ai)
    cr, ci = _shift_rows(er, 1, reverse), _shift_rows(ei, 1, reverse)
    for k in (1, 2, 4):
        sr, si = _shift_rows(cr, k, reverse), _shift_rows(ci, k, reverse)
        pr, pi_ = _cmul(ar, ai, sr, si)
        cr, ci = cr + pr, ci + pi_
        ar, ai = _cmul(ar, ai, ar, ai)

    if aux is None:
        def step2(i, c):
            for u in range(U):
                row = rows_of(idx(i * U + u))
                c = rec(c[0], c[1], row)
                S[row, 0:512] = c[0]
                S[row, 512:1024] = c[1]
            return c

        lax.fori_loop(0, S5_STEPS // U, step2, (cr, ci))
        return None

    def one(s, c):
        gr0, gi0, dr, di = c
        row = rows_of(s)
        gr, gi = rec(gr0, gi0, row)
        S[row, 0:512] = gr
        S[row, 512:1024] = gi
        prow = rows_of(s - 1)
        xr = aux[prow, 0:512]
        xi = aux[prow, 512:1024]
        return gr, gi, dr + gr * xr + gi * xi, di + gi * xr - gr * xi

    def step2(i, c):
        for u in range(U):
            c = one(S5_STEPS - 1 - (i * U + u), c)
        return c

    c = lax.fori_loop(0, S5_STEPS // U - 1, step2, (cr, ci, z, z))
    for s in range(U - 1, 0, -1):
        c = one(s, c)
    gr, gi, dr, di = c
    row0 = pl.ds(0, 8)
    gr, gi = rec(gr, gi, row0)
    S[row0, 0:512] = gr
    S[row0, 512:1024] = gi
    last = pl.ds((S5_STEPS - 1) * 8, 8)
    xr = _shift_rows(aux[last, 0:512], 1, False)
    xi = _shift_rows(aux[last, 512:1024], 1, False)
    dr = dr + gr * xr + gi * xi
    di = di + gi * xr - gr * xi
    return dr, di


_ROWS = 256


def _row_loop(fn):
    def body(r, c):
        fn(pl.ds(pl.multiple_of(r * _ROWS, _ROWS), _ROWS))
        return c
    lax.fori_loop(0, T // _ROWS, body, 0)


def s5_core_fwd(hn, bm, lam, cm, bg=()):
    def body(u_ref, b_ref, lam_ref, c_ref, ys_ref, S):
        def bu(rows):
            S[rows, :] = _dot(u_ref[rows, :], b_ref[...])
        _row_loop(bu)
        _chunk_scan(S, lam_ref[0], lam_ref[1], False)

        def ys(rows):
            ys_ref[rows, :] = _dot(S[rows, :].astype(bf16), c_ref[...])
        _row_loop(ys)

    return _call(
        bg, body, name="s5_core_fwd", grid=(S5_SUB,),
        in_specs=[pl.BlockSpec((T, 128), lambda b: (0, b)),
                  pl.BlockSpec((None, 128, 1024), lambda b: (b, 0, 0)),
                  pl.BlockSpec((None, 4, 8, 512), lambda b: (b, 0, 0, 0)),
                  pl.BlockSpec((None, 1024, 128), lambda b: (b, 0, 0))],
        out_specs=pl.BlockSpec((T, 128), lambda b: (0, b)),
        out_shape=SDS((T, D), f32),
        scratch_shapes=[pltpu.VMEM((T, 1024), f32)],
        compiler_params=_cp(dimension_semantics=("arbitrary",)),
    )(hn, bm, lam, cm)


_SEG = _ROWS // S5_CH


def _scan_tile(S, lr, li, k, carry, reverse, store, aux=None):
    steps = range(k * _SEG, (k + 1) * _SEG)
    for s in (reversed(steps) if reverse else steps):
        row = pl.ds(s * 8, 8)
        xr, xi = carry[0], carry[1]
        nr = lr * xr - li * xi + S[row, 0:512]
        ni = lr * xi + li * xr + S[row, 512:1024]
        if store:
            S[row, 0:512] = nr
            S[row, 512:1024] = ni
        if aux is not None and s >= 1:
            prow = pl.ds((s - 1) * 8, 8)
            pr, pi_ = aux[prow, 0:512], aux[prow, 512:1024]
            carry = (nr, ni, carry[2] + nr * pr + ni * pi_, carry[3] + ni * pr - nr * pi_)
        elif aux is not None:
            carry = (nr, ni, carry[2], carry[3])
        else:
            carry = (nr, ni)
    return carry


def _chunk_starts(er, ei, lr, li, reverse):
    ar, ai = lr, li
    for _ in range(8):
        ar, ai = _cmul(ar, ai, ar, ai)
    cr, ci = _shift_rows(er, 1, reverse), _shift_rows(ei, 1, reverse)
    for k in (1, 2, 4):
        sr, si = _shift_rows(cr, k, reverse), _shift_rows(ci, k, reverse)
        pr, pi_ = _cmul(ar, ai, sr, si)
        cr, ci = cr + pr, ci + pi_
        ar, ai = _cmul(ar, ai, ar, ai)
    return cr, ci


def s5_core_bwd(hn, dy, bm, lam, cm, bg=()):
    nt = T // _ROWS

    def body(u_ref, dy_ref, b_ref, lam_ref, c_ref, du_ref, db_ref, dct_ref, dlam_ref, S1, S2):
        lr, li, lcr, lci = lam_ref[0], lam_ref[1], lam_ref[2], lam_ref[3]
        z = jnp.zeros((8, 512), f32)
        tile = lambda k: pl.ds(k * _ROWS, _ROWS)
        dyb = lambda k: dy_ref[tile(k), :].astype(bf16)

        c = (z, z)
        for k in range(nt):
            S1[tile(k), :] = _dot(u_ref[tile(k), :], b_ref[...])
            if k >= 1:
                c = _scan_tile(S1, lr, li, k - 1, c, False, False)
        c = _scan_tile(S1, lr, li, nt - 1, c, False, False)

        c = _chunk_starts(c[0], c[1], lr, li, False)
        dct_ref[...] = jnp.zeros_like(dct_ref)
        for k in range(nt):
            c = _scan_tile(S1, lr, li, k, c, False, True)
            if k >= 1:
                dct_ref[...] += _dot_tn(dyb(k - 1), S1[tile(k - 1), :].astype(bf16))
        dct_ref[...] += _dot_tn(dyb(nt - 1), S1[tile(nt - 1), :].astype(bf16))

        S2[tile(nt - 1), :] = _dot_nt(dyb(nt - 1), c_ref[...])
        c = (z, z)
        for k in range(nt - 1, -1, -1):
            if k >= 1:
                S2[tile(k - 1), :] = _dot_nt(dyb(k - 1), c_ref[...])
            c = _scan_tile(S2, lcr, lci, k, c, True, False)

        def dbu(k):
            gb = S2[tile(k), :].astype(bf16)
            db_ref[...] += _dot_tn(u_ref[tile(k), :], gb)
            du_ref[tile(k), :] = _dot_nt(gb, b_ref[...])

        c = _chunk_starts(c[0], c[1], lcr, lci, True) + (z, z)
        db_ref[...] = jnp.zeros_like(db_ref)
        for k in range(nt - 1, -1, -1):
            c = _scan_tile(S2, lcr, lci, k, c, True, True, aux=S1)
            if k + 1 < nt:
                dbu(k + 1)
        dbu(0)
        gr, gi, dr, di = c
        last = pl.ds((S5_STEPS - 1) * 8, 8)
        xr = _shift_rows(S1[last, 0:512], 1, False)
        xi = _shift_rows(S1[last, 512:1024], 1, False)
        dlam_ref[0] = dr + gr * xr + gi * xi
        dlam_ref[1] = di + gi * xr - gr * xi

    return _call(
        bg, body, name="s5_core_bwd", grid=(S5_SUB,),
        in_specs=[pl.BlockSpec((T, 128), lambda b: (0, b)),
                  pl.BlockSpec((T, 128), lambda b: (0, b)),
                  pl.BlockSpec((None, 128, 1024), lambda b: (b, 0, 0)),
                  pl.BlockSpec((None, 4, 8, 512), lambda b: (b, 0, 0, 0)),
                  pl.BlockSpec((None, 1024, 128), lambda b: (b, 0, 0))],
        out_specs=[pl.BlockSpec((T, 128), lambda b: (0, b)),
                   pl.BlockSpec((None, 128, 1024), lambda b: (b, 0, 0)),
                   pl.BlockSpec((None, 128, 1024), lambda b: (b, 0, 0)),
                   pl.BlockSpec((None, 2, 8, 512), lambda b: (b, 0, 0, 0))],
        out_shape=[SDS((T, D), f32), SDS((8, 128, 1024), f32), SDS((8, 128, 1024), f32), SDS((8, 2, 8, 512), f32)],
        scratch_shapes=[pltpu.VMEM((T, 1024), f32), pltpu.VMEM((T, 1024), f32)],
        compiler_params=_cp(dimension_semantics=("arbitrary",)),
    )(hn, dy, bm, lam, cm)


TM = 512
NT = T // TM


def _tile(n=D):
    return pl.BlockSpec((TM, n), lambda i: (i, 0))


def s5_pre(xp, g):
    def body(x_ref, g_ref, hn_ref):
        hn, _ = _rms(x_ref[...], g_ref[...])
        hn_ref[...] = hn.astype(bf16)

    return pl.pallas_call(
        body, name="s5_pre", grid=(NT,), in_specs=[_tile(), _full((1, D))], out_specs=_tile(),
        out_shape=SDS((T, D), bf16), compiler_params=_cp(dimension_semantics=("arbitrary",)),
    )(xp, g)


def _gelu_grad(y):
    c = math.sqrt(2.0 / math.pi)
    t = jnp.tanh(c * (y + 0.044715 * y * y * y))
    return 0.5 * (1.0 + t) + 0.5 * y * (1.0 - t * t) * c * (1.0 + 3.0 * 0.044715 * y * y)


def s5_post(ys, xp, g, d, wglu, bglu, bg=()):
    def body(ys_ref, x_ref, g_ref, d_ref, w_ref, b_ref, y_ref, z_ref, h_ref):
        x = x_ref[...]
        hn, _ = _rms(x, g_ref[...])
        y = ys_ref[...] + d_ref[...] * hn
        y_ref[...] = y
        yg = jax.nn.gelu(y).astype(bf16)
        for j in range(4):
            cv = slice(j * 256, (j + 1) * 256)
            cg = slice(1024 + j * 256, 1024 + (j + 1) * 256)
            val = _dot(yg, w_ref[j]) + b_ref[:, cv]
            gate = _dot(yg, w_ref[j + 4]) + b_ref[:, cg]
            z_ref[:, cv] = val
            z_ref[:, cg] = gate
            h_ref[:, cv] = x[:, cv] + val * jax.nn.sigmoid(gate)

    return _call(
        bg, body, name="s5_post", grid=(NT,),
        in_specs=[_tile(), _tile(), _full((1, D)), _full((1, D)), _full((8, D, 256)), _full((1, 2 * D))],
        out_specs=[_tile(), _tile(2 * D), _tile()],
        out_shape=[SDS((T, D), f32), SDS((T, 2 * D), f32), SDS((T, D), f32)],
        compiler_params=_cp(dimension_semantics=("arbitrary",)),
    )(ys, xp, g, d, wglu, bglu)


def s5_post_bwd(dh, y, z, wglu, bg=()):
    def body(dh_ref, y_ref, z_ref, w_ref, dy_ref, dw_ref, db_ref, acc):
        i = pl.program_id(0)

        @pl.when(i == 0)
        def _():
            acc[...] = jnp.zeros_like(acc)
            db_ref[...] = jnp.zeros_like(db_ref)

        dh_ = dh_ref[...]
        y = y_ref[...]
        yg = jax.nn.gelu(y).astype(bf16)
        dyg = jnp.zeros((TM, D), f32)
        for j in range(4):
            cv = slice(j * 256, (j + 1) * 256)
            cg = slice(1024 + j * 256, 1024 + (j + 1) * 256)
            val = z_ref[:, cv]
            sg = jax.nn.sigmoid(z_ref[:, cg])
            dval = dh_[:, cv] * sg
            dgate = dh_[:, cv] * val * sg * (1.0 - sg)
            db_ref[:, cv] += _colsum8(dval)
            db_ref[:, cg] += _colsum8(dgate)
            dvb = dval.astype(bf16)
            dgb = dgate.astype(bf16)
            acc[j] += _dot_tn(yg, dvb)
            acc[j + 4] += _dot_tn(yg, dgb)
            dyg = dyg + _dot_nt(dvb, w_ref[j]) + _dot_nt(dgb, w_ref[j + 4])
        dy_ref[...] = dyg * _gelu_grad(y)

        @pl.when(i == NT - 1)
        def _():
            dw_ref[...] = acc[...].astype(bf16)

    return _call(
        bg, body, name="s5_post_bwd", grid=(NT,),
        in_specs=[_tile(), _tile(), _tile(2 * D), _full((8, D, 256))],
        out_specs=[_tile(), _full((8, D, 256)), _full((8, 2 * D))],
        out_shape=[SDS((T, D), f32), SDS((8, D, 256), bf16), SDS((8, 2 * D), f32)],
        scratch_shapes=[pltpu.VMEM((8, D, 256), f32)],
        compiler_params=_cp(dimension_semantics=("arbitrary",)),
    )(dh, y, z, wglu)


def s5_pre_bwd(xp, g, du, dy, d, dh, bg=()):
    def body(x_ref, g_ref, du_ref, dy_ref, d_ref, dh_ref, dx_ref, dg_ref, dd_ref):
        i = pl.program_id(0)

        @pl.when(i == 0)
        def _():
            dg_ref[...] = jnp.zeros_like(dg_ref)
            dd_ref[...] = jnp.zeros_like(dd_ref)

        x = x_ref[...]
        g = g_ref[...]
        dy = dy_ref[...]
        hn, _ = _rms(x, g)
        dhn = du_ref[...] + d_ref[...] * dy
        dx, dgt = _rms_bwd(x, g, dhn)
        dx_ref[...] = dh_ref[...] + dx
        dg_ref[...] += _colsum8(dgt)
        dd_ref[...] += _colsum8(dy * hn)

    return _call(
        bg, body, name="s5_pre_bwd", grid=(NT,),
        in_specs=[_tile(), _full((1, D)), _tile(), _tile(), _full((1, D)), _tile()],
        out_specs=[_tile(), _full((8, D)), _full((8, D))],
        out_shape=[SDS((T, D), f32), SDS((8, D), f32), SDS((8, D), f32)],
        compiler_params=_cp(dimension_semantics=("arbitrary",)),
    )(xp, g, du, dy, d, dh)


TMF = 1024


def mlp_fwd(h, g, w_in, w_out, layer, bg=()):
    def body(h_ref, g_ref, wi_ref, wo_ref, hm_ref, r_ref, out_ref, acc):
        j = pl.program_id(1)

        @pl.when(j == 0)
        def _():
            hm, _ = _rms(h_ref[...], g_ref[...])
            hm_ref[...] = hm.astype(bf16)
            acc[...] = jnp.zeros_like(acc)

        a = jnp.maximum(_dot(hm_ref[...], wi_ref[...]), 0.0)
        r_ref[...] = a.astype(bf16)
        acc[...] += _dot((a * a).astype(bf16), wo_ref[...])

        @pl.when(j == NDEV - 1)
        def _():
            out_ref[...] = h_ref[...] + acc[...]

    return _call(
        bg, body, name=f"mlp_fwd{layer}", grid=(T // TMF, NDEV),
        in_specs=[pl.BlockSpec((TMF, D), lambda i, j: (i, 0)),
                  pl.BlockSpec((1, D), lambda i, j: (0, 0)),
                  pl.BlockSpec((None, D, D_FF_SHARD), lambda i, j: (j, 0, 0)),
                  pl.BlockSpec((None, D_FF_SHARD, D), lambda i, j: (j, 0, 0))],
        out_specs=[pl.BlockSpec((TMF, D), lambda i, j: (i, 0)), pl.BlockSpec((TMF, D_FF_SHARD), lambda i, j: (i, j)),
                   pl.BlockSpec((TMF, D), lambda i, j: (i, 0))],
        out_shape=[SDS((T, D), bf16), SDS((T, NDEV * D_FF_SHARD), bf16), SDS((T, D), f32)],
        scratch_shapes=[pltpu.VMEM((TMF, D), f32)],
        compiler_params=_cp(dimension_semantics=("arbitrary", "arbitrary")),
    )(h, g, w_in, w_out)


def mlp_bwd(h, hm, r, g, dout, w_in, w_out, layer, bg=()):
    last = NDEV - 1

    def body(h_ref, hm_ref, r_ref, g_ref, do_ref, wi_ref, wo_ref, dh_ref, dwi_ref, dwo_ref, dg_ref, dhm, awi, awo):
        j = pl.program_id(0)
        i = pl.program_id(1)
        rows = pl.ds(pl.multiple_of(i * TM, TM), TM)

        @pl.when(i == 0)
        def _():
            awi[...] = jnp.zeros_like(awi)
            awo[...] = jnp.zeros_like(awo)

        hm_ = hm_ref[...]
        dob = do_ref[...].astype(bf16)
        r = r_ref[...].astype(f32)
        dz = (_dot_nt(dob, wo_ref[...]) * (2.0 * r)).astype(bf16)
        awo[...] += _dot_tn((r * r).astype(bf16), dob)
        awi[...] += _dot_tn(hm_, dz)
        part = _dot_nt(dz, wi_ref[...])

        @pl.when(j == 0)
        def _():
            dhm[rows, :] = part

        @pl.when(j > 0)
        def _():
            dhm[rows, :] += part

        @pl.when(i == NT - 1)
        def _():
            dwi_ref[...] = awi[...].astype(bf16)
            dwo_ref[...] = awo[...].astype(bf16)

        @pl.when(j == last)
        def _():
            @pl.when(i == 0)
            def _():
                dg_ref[...] = jnp.zeros_like(dg_ref)
            dx, dgt = _rms_bwd(h_ref[...], g_ref[...], dhm[rows, :])
            dh_ref[...] = do_ref[...] + dx
            dg_ref[...] += _colsum8(dgt)

    late = lambda j, i: (jnp.where(j == last, i, 0), 0)
    return _call(
        bg, body, name=f"mlp_bwd{layer}", grid=(NDEV, NT),
        in_specs=[pl.BlockSpec((TM, D), late),
                  pl.BlockSpec((TM, D), lambda j, i: (i, 0)),
                  pl.BlockSpec((TM, D_FF_SHARD), lambda j, i: (i, j)),
                  pl.BlockSpec((1, D), lambda j, i: (0, 0)),
                  pl.BlockSpec((TM, D), lambda j, i: (i, 0)),
                  pl.BlockSpec((None, D, D_FF_SHARD), lambda j, i: (j, 0, 0)),
                  pl.BlockSpec((None, D_FF_SHARD, D), lambda j, i: (j, 0, 0))],
        out_specs=[pl.BlockSpec((TM, D), late),
                   pl.BlockSpec((None, D, D_FF_SHARD), lambda j, i: (j, 0, 0)),
                   pl.BlockSpec((None, D_FF_SHARD, D), lambda j, i: (j, 0, 0)),
                   pl.BlockSpec((8, D), lambda j, i: (0, 0))],
        out_shape=[SDS((T, D), f32), SDS((NDEV, D, D_FF_SHARD), bf16), SDS((NDEV, D_FF_SHARD, D), bf16),
                   SDS((8, D), f32)],
        scratch_shapes=[pltpu.VMEM((T, D), f32), pltpu.VMEM((D, D_FF_SHARD), f32), pltpu.VMEM((D_FF_SHARD, D), f32)],
        compiler_params=_cp(dimension_semantics=("arbitrary", "arbitrary")),
    )(h, hm, r, g, dout, w_in, w_out)


def _spread4():
    r = lax.broadcasted_iota(jnp.int32, (256, D), 0)
    c = lax.broadcasted_iota(jnp.int32, (256, D), 1)
    return ((c // 256 == r // HEAD_DIM) & (c % HEAD_DIM == r % HEAD_DIM)).astype(bf16)


def attn_pre(h, g_kv, g_mix, wkv, bkv, spread, wq, bq):
    def body(h_ref, gkv_ref, gm_ref, wkv_ref, bkv_ref, sp_ref, wq_ref, bq_ref, kvn_ref, hn_ref, k_ref, v_ref, q_ref):
        h_ = h_ref[...]
        kvn = _rms(h_, gkv_ref[...])[0].astype(bf16)
        hn = _rms(h_, gm_ref[...])[0].astype(bf16)
        kvn_ref[...] = kvn
        hn_ref[...] = hn
        kv = (_dot(kvn, wkv_ref[...]) + bkv_ref[...]).astype(bf16)
        k_ref[...] = _dot(kv[:, :256], sp_ref[...]).astype(bf16)
        v_ref[...] = _dot(kv[:, 256:], sp_ref[...]).astype(bf16)
        q_ref[...] = (_dot(hn, wq_ref[...]) + bq_ref[...]).astype(bf16)

    return pl.pallas_call(
        body, name="attn_pre", grid=(NT,),
        in_specs=[_tile(), _full((1, D)), _full((1, D)), _full((D, 512)), _full((1, 512)), _full((256, D)),
                  _full((D, D)), _full((1, D))],
        out_specs=[_tile()] * 5,
        out_shape=[SDS((T, D), bf16)] * 5,
        compiler_params=_cp(dimension_semantics=("arbitrary",)),
    )(h, g_kv, g_mix, wkv, bkv, spread, wq, bq)


def _attn_specs():
    cur = pl.BlockSpec((TM, 256), lambda j, n: (n, j))
    prev = pl.BlockSpec((BLK, 256), lambda j, n: (jnp.maximum(n * (TM // BLK) - 1, 0), j))
    return cur, prev


def _head_mask(g):
    lane = lax.broadcasted_iota(jnp.int32, (1, 256), 1)
    return (lane >= g * HEAD_DIM) & (lane < (g + 1) * HEAD_DIM)


def _stack_heads(t):
    return jnp.concatenate([jnp.where(_head_mask(g), t, 0) for g in range(Q_PER_KV)], axis=0)


def _unstack_heads(t):
    out = jnp.where(_head_mask(0), t[0:BLK], 0.0)
    for g in range(1, Q_PER_KV):
        out = out + jnp.where(_head_mask(g), t[g * BLK:(g + 1) * BLK], 0.0)
    return out


def _attn_probs(qs, k2, sinks, first):
    rows = Q_PER_KV * BLK
    s = _dot_nt(qs, k2) * (1.0 / math.sqrt(HEAD_DIM))
    qi = jnp.bitwise_and(lax.broadcasted_iota(jnp.int32, (rows, 2 * BLK), 0), BLK - 1)
    kj = lax.broadcasted_iota(jnp.int32, (rows, 2 * BLK), 1)
    diff = qi + BLK - kj
    valid = (diff >= 0) & (diff < BLK) & (jnp.logical_not(first) | (kj >= BLK))
    s = jnp.where(valid, s, -jnp.inf)
    rb = lax.broadcasted_iota(jnp.int32, (rows, 1), 0)
    sink = jnp.where(rb < BLK, sinks[0], jnp.where(rb < 2 * BLK, sinks[1], jnp.where(rb < 3 * BLK, sinks[2], sinks[3])))
    m = jnp.maximum(jnp.max(s, axis=-1, keepdims=True), sink)
    p = jnp.exp(s - m)
    ps = jnp.exp(sink - m)
    denom = jnp.sum(p, axis=-1, keepdims=True) + ps
    return p / denom, ps / denom


def attn_core_fwd(q, k4, v4, sinks, bg=()):
    nb = TM // BLK

    def body(sink_ref, q_ref, kc_ref, kp_ref, vc_ref, vp_ref, o_ref):
        j = pl.program_id(0)
        n = pl.program_id(1)
        sk = [sink_ref[j * Q_PER_KV + g] for g in range(Q_PER_KV)]
        for b in range(nb):
            qb = q_ref[b * BLK:(b + 1) * BLK, :]
            if b == 0:
                k2 = jnp.concatenate([kp_ref[...], kc_ref[0:BLK, :]], axis=0)
                v2 = jnp.concatenate([vp_ref[...], vc_ref[0:BLK, :]], axis=0)
                first = n == 0
            else:
                k2 = kc_ref[(b - 1) * BLK:(b + 1) * BLK, :]
                v2 = vc_ref[(b - 1) * BLK:(b + 1) * BLK, :]
                first = False
            a, _ = _attn_probs(_stack_heads(qb), k2, sk, first)
            o_ref[b * BLK:(b + 1) * BLK, :] = _unstack_heads(_dot(a.astype(bf16), v2)).astype(bf16)

    cur, prev = _attn_specs()
    return _call(
        bg, body, name="attn_core_fwd", grid=(N_KV, NT),
        in_specs=[pl.BlockSpec(memory_space=pltpu.SMEM), cur, cur, prev, cur, prev],
        out_specs=cur, out_shape=SDS((T, D), bf16),
        compiler_params=_cp(dimension_semantics=("arbitrary", "arbitrary")),
    )(sinks, q, k4, k4, v4, v4)


def attn_post(h, o, wo, bo):
    def body(h_ref, o_ref, w_ref, b_ref, out_ref):
        out_ref[...] = h_ref[...] + _dot(o_ref[...], w_ref[...]) + b_ref[...]

    return pl.pallas_call(
        body, name="attn_post", grid=(NT,), in_specs=[_tile(), _tile(), _full((D, D)), _full((1, D))],
        out_specs=_tile(), out_shape=SDS((T, D), f32), compiler_params=_cp(dimension_semantics=("arbitrary",)),
    )(h, o, wo, bo)


def attn_bwd_pre(dh, o, wo, bg=()):
    def body(dh_ref, o_ref, w_ref, do_ref, dw_ref, db_ref, acc):
        i = pl.program_id(0)

        @pl.when(i == 0)
        def _():
            acc[...] = jnp.zeros_like(acc)
            db_ref[...] = jnp.zeros_like(db_ref)

        dh_ = dh_ref[...]
        dhb = dh_.astype(bf16)
        do_ref[...] = _dot_nt(dhb, w_ref[...]).astype(bf16)
        acc[...] += _dot_tn(o_ref[...], dhb)
        db_ref[...] += _colsum8(dh_)

        @pl.when(i == NT - 1)
        def _():
            dw_ref[...] = acc[...].astype(bf16)

    return _call(
        bg, body, name="attn_bwd_pre", grid=(NT,), in_specs=[_tile(), _tile(), _full((D, D))],
        out_specs=[_tile(), _full((D, D)), _full((8, D))],
        out_shape=[SDS((T, D), bf16), SDS((D, D), bf16), SDS((8, D), f32)],
        scratch_shapes=[pltpu.VMEM((D, D), f32)],
        compiler_params=_cp(dimension_semantics=("arbitrary",)),
    )(dh, o, wo)


def attn_core_bwd(q, do, k4, v4, sinks, bg=()):
    nb = TM // BLK

    def body(sink_ref, q_ref, do_ref, kc_ref, kp_ref, vc_ref, vp_ref, dq_ref, dk_ref, dv_ref, ds_ref):
        j = pl.program_id(0)
        n = pl.program_id(1)

        @pl.when(n == 0)
        def _():
            dk_ref[...] = jnp.zeros_like(dk_ref)
            dv_ref[...] = jnp.zeros_like(dv_ref)
            ds_ref[...] = jnp.zeros_like(ds_ref)

        lane8 = lax.broadcasted_iota(jnp.int32, (8, 128), 1)
        row8 = lax.broadcasted_iota(jnp.int32, (8, 128), 0)
        sk = [sink_ref[j * Q_PER_KV + g] for g in range(Q_PER_KV)]
        for b in range(nb):
            qs = _stack_heads(q_ref[b * BLK:(b + 1) * BLK, :])
            dos = _stack_heads(do_ref[b * BLK:(b + 1) * BLK, :])
            if b == 0:
                k2 = jnp.concatenate([kp_ref[...], kc_ref[0:BLK, :]], axis=0)
                v2 = jnp.concatenate([vp_ref[...], vc_ref[0:BLK, :]], axis=0)
                first = n == 0
            else:
                k2 = kc_ref[(b - 1) * BLK:(b + 1) * BLK, :]
                v2 = vc_ref[(b - 1) * BLK:(b + 1) * BLK, :]
                first = False
            a, asink = _attn_probs(qs, k2, sk, first)
            dp = _dot_nt(dos, v2)
            dd = jnp.sum(a * dp, axis=-1, keepdims=True)
            dsc = (a * (dp - dd) * (1.0 / math.sqrt(HEAD_DIM))).astype(bf16)
            t = asink * dd
            for g in range(Q_PER_KV):
                dsink = -jnp.sum(t[g * BLK:(g + 1) * BLK], axis=0, keepdims=True)
                ds_ref[...] += jnp.where((lane8 == g) & (row8 == 0), jnp.broadcast_to(dsink, (8, 128)), 0.0)
            dq_ref[b * BLK:(b + 1) * BLK, :] = _unstack_heads(_dot(dsc, k2))
            dk2 = _dot_tn(dsc, qs)
            dv2 = _dot_tn(a.astype(bf16), dos)
            cur = pl.ds(pl.multiple_of(n * TM + b * BLK, BLK), BLK)
            dk_ref[cur, :] += dk2[BLK:, :]
            dv_ref[cur, :] += dv2[BLK:, :]
            if b == 0:
                @pl.when(n > 0)
                def _():
                    prv = pl.ds(pl.multiple_of(n * TM - BLK, BLK), BLK)
                    dk_ref[prv, :] += dk2[:BLK, :]
                    dv_ref[prv, :] += dv2[:BLK, :]
            else:
                prv = pl.ds(pl.multiple_of(n * TM + (b - 1) * BLK, BLK), BLK)
                dk_ref[prv, :] += dk2[:BLK, :]
                dv_ref[prv, :] += dv2[:BLK, :]

    cur, prev = _attn_specs()
    col = pl.BlockSpec((T, 256), lambda j, n: (0, j))
    return _call(
        bg, body, name="attn_core_bwd", grid=(N_KV, NT),
        in_specs=[pl.BlockSpec(memory_space=pltpu.SMEM), cur, cur, cur, prev, cur, prev],
        out_specs=[cur, col, col, pl.BlockSpec((None, 8, 128), lambda j, n: (j, 0, 0))],
        out_shape=[SDS((T, D), f32), SDS((T, D), f32), SDS((T, D), f32), SDS((N_KV, 8, 128), f32)],
        compiler_params=_cp(dimension_semantics=("arbitrary", "arbitrary")),
    )(sinks, q, do, k4, k4, v4, v4)


def attn_bwd_q(h, dh, dq, hn, g_mix, wq):
    def body(h_ref, dh_ref, dq_ref, hn_ref, gm_ref, wq_ref, out_ref, dwq_ref, dbq_ref, dgm_ref, aq):
        i = pl.program_id(0)

        @pl.when(i == 0)
        def _():
            aq[...] = jnp.zeros_like(aq)
            dbq_ref[...] = jnp.zeros_like(dbq_ref)
            dgm_ref[...] = jnp.zeros_like(dgm_ref)

        dq_ = dq_ref[...]
        dqb = dq_.astype(bf16)
        aq[...] += _dot_tn(hn_ref[...], dqb)
        dbq_ref[...] += _colsum8(dq_)
        dx, dg = _rms_bwd(h_ref[...], gm_ref[...], _dot_nt(dqb, wq_ref[...]))
        out_ref[...] = dh_ref[...] + dx
        dgm_ref[...] += _colsum8(dg)

        @pl.when(i == NT - 1)
        def _():
            dwq_ref[...] = aq[...].astype(bf16)

    vec = _full((8, D))
    mat = _full((D, D))
    return pl.pallas_call(
        body, name="attn_bwd_q", grid=(NT,),
        in_specs=[_tile()] * 4 + [_full((1, D)), mat],
        out_specs=[_tile(), mat, vec, vec],
        out_shape=[SDS((T, D), f32), SDS((D, D), bf16), SDS((8, D), f32), SDS((8, D), f32)],
        scratch_shapes=[pltpu.VMEM((D, D), f32)],
        compiler_params=_cp(dimension_semantics=("arbitrary",)),
    )(h, dh, dq, hn, g_mix, wq)


def attn_bwd_kv(h, dh, dk4, dv4, kvn, g_kv, wkv, spread):
    def body(h_ref, dh_ref, dk_ref, dv_ref, kvn_ref, gkv_ref, wkv_ref, sp_ref, out_ref, dw_ref, db_ref, dgkv_ref, acc):
        i = pl.program_id(0)

        @pl.when(i == 0)
        def _():
            for r in (acc, db_ref, dgkv_ref):
                r[...] = jnp.zeros_like(r)

        dkv = jnp.concatenate([_dot_nt(dk_ref[...].astype(bf16), sp_ref[...]),
                               _dot_nt(dv_ref[...].astype(bf16), sp_ref[...])], axis=1)
        dkvb = dkv.astype(bf16)
        acc[...] += _dot_tn(kvn_ref[...], dkvb)
        db_ref[...] += _colsum8(dkv)
        dx, dg = _rms_bwd(h_ref[...], gkv_ref[...], _dot_nt(dkvb, wkv_ref[...]))
        out_ref[...] = dh_ref[...] + dx
        dgkv_ref[...] += _colsum8(dg)

        @pl.when(i == NT - 1)
        def _():
            dw_ref[...] = acc[...].astype(bf16)

    return pl.pallas_call(
        body, name="attn_bwd_kv", grid=(NT,),
        in_specs=[_tile()] * 5 + [_full((1, D)), _full((D, 512)), _full((256, D))],
        out_specs=[_tile(), _full((D, 512)), _full((8, 512)), _full((8, D))],
        out_shape=[SDS((T, D), f32), SDS((D, 512), bf16), SDS((8, 512), f32), SDS((8, D), f32)],
        scratch_shapes=[pltpu.VMEM((D, 512), f32)],
        compiler_params=_cp(dimension_semantics=("arbitrary",)),
    )(h, dh, dk4, dv4, kvn, g_kv, wkv, spread)


def final_loss(h, g, target):
    def body(h_ref, g_ref, t_ref, loss_ref, dh_ref, dg_ref):
        i = pl.program_id(0)

        @pl.when(i == 0)
        def _():
            loss_ref[...] = jnp.zeros_like(loss_ref)
            dg_ref[...] = jnp.zeros_like(dg_ref)

        h_ = h_ref[...]
        g_ = g_ref[...]
        y, _ = _rms(h_, g_)
        diff = y - t_ref[...]
        per_tok = jnp.mean(diff * diff, axis=-1, keepdims=True)
        tot = 0.5 * jnp.sum(per_tok, axis=0, keepdims=True)
        lane = lax.broadcasted_iota(jnp.int32, (8, 128), 1)
        row = lax.broadcasted_iota(jnp.int32, (8, 128), 0)
        loss_ref[...] += jnp.where((lane == 0) & (row == 0), jnp.broadcast_to(tot, (8, 128)), 0.0)
        dx, dgt = _rms_bwd(h_, g_, diff * (1.0 / D))
        dh_ref[...] = dx
        dg_ref[...] += _colsum8(dgt)

    return pl.pallas_call(
        body, name="final_loss", grid=(NT,), in_specs=[_tile(), _full((1, D)), _tile()],
        out_specs=[_full((8, 128)), _tile(), _full((8, D))],
        out_shape=[SDS((8, 128), f32), SDS((T, D), f32), SDS((8, D), f32)],
        compiler_params=_cp(dimension_semantics=("arbitrary",)),
    )(h, g, target)


def _to_chunked(a):
    return a.reshape(S5_CH, S5_STEPS, a.shape[-1]).transpose(1, 0, 2).reshape(T, a.shape[-1])


def _from_chunked(a):
    return a.reshape(S5_STEPS, S5_CH, a.shape[-1]).transpose(1, 0, 2).reshape(T, a.shape[-1])


def _rep4(w):
    return jnp.broadcast_to(w.reshape(w.shape[0], N_KV, 1, HEAD_DIM), (w.shape[0], N_KV, Q_PER_KV, HEAD_DIM)).reshape(
        w.shape[0], N_KV * Q_PER_KV * HEAD_DIM)


def _fold4(w):
    return w.reshape(w.shape[0], N_KV, Q_PER_KV, HEAD_DIM).sum(axis=2).reshape(w.shape[0], N_KV * HEAD_DIM)


def fwd_bwd(x, target, p, shards, core):
    row = lambda v: v.reshape(1, -1)
    (lam, bm, cm), prep_vjp = jax.vjp(s5_discretize, p["s5_a_re"][0], p["s5_a_im"][0], p["s5_log_dt"][0],
                                      p["s5_b_re"][0], p["s5_b_im"][0], p["s5_c_re"][0], p["s5_c_im"][0])
    bmb, cmb = bm.astype(bf16), cm.astype(bf16)
    lam = jnp.concatenate([lam, lam * jnp.array([1.0, -1.0], f32).reshape(1, 2, 1, 1)], axis=1)
    g_mix0, g_mix1 = row(p["norm_mix"][0]), row(p["norm_mix"][1])
    g_mlp0, g_mlp1 = row(p["norm_mlp"][0]), row(p["norm_mlp"][1])
    g_kv, g_fin = row(p["norm_kv"]), row(p["norm_final"])
    bq, bo = p["b_q"], p["b_o"]
    bkv = row(p["b_kv"])
    spread = _spread4()
    sinks = p["sinks"].reshape(16)

    def reduce_pairs(names, bg):
        return [add_pairs(g, r, core, f"add_pairs_{n}") for n, g, r in zip(names, bg.arrs, bg.result)]

    xp = _to_chunked(x)
    hn0 = s5_pre(xp, g_mix0)
    ga = BgGather([shards["s5_w_glu"], shards["vecs"], shards["w_in0"], shards["w_out0"]], mids=(0.75, 1.0))
    ys = s5_core_fwd(hn0, bmb, lam, cmb, bg=[ga])
    wglu, gvec, win0, wout0 = ga.result
    d_skip = gvec[:, 0, :128].reshape(1, D)
    bglu = gvec[:, 0, 128:].reshape(1, 2 * D)
    y, z, h1 = s5_post(ys, xp, g_mix0, d_skip, wglu, bglu)
    gc = BgGather([shards["w_kv"], shards["w_q"], shards["w_o"], shards["w_in1"]], mids=(0.8, 1.0))
    hm0, r0, h2p = mlp_fwd(h1, g_mlp0, win0, wout0, 0, bg=[gc])
    wkv, wq, wo, win1 = gc.result
    wkv, wq, wo = wkv.reshape(D, 512), wq.reshape(D, D), wo.reshape(D, D)
    h2 = _from_chunked(h2p)
    kvn, hn1, k4, v4, q = attn_pre(h2, g_kv, g_mix1, wkv, bkv, spread, wq, bq)
    gd = BgGather([shards["w_out1"]], mids=(0.94, 1.0))
    o = attn_core_fwd(q, k4, v4, sinks, bg=[gd])
    wout1, = gd.result
    h3 = attn_post(h2, o, wo, bo)
    hm1, r1, h4 = mlp_fwd(h3, g_mlp1, win1, wout1, 1)
    loss, dh4, dg_fin = final_loss(h4, g_fin, target)

    big = {}
    dh3, dwin1, dwout1, dg_mlp1 = mlp_bwd(h3, hm1, r1, g_mlp1, dh4, win1, wout1, 1)
    pa = BgPair([dwin1, dwout1])
    do, dwo, dbo = attn_bwd_pre(dh3, o, wo, bg=[pa])
    p_in1, p_out1 = reduce_pairs(["w_in1", "w_out1"], pa)
    ca = BgChips([p_in1])
    dq, dk4, dv4, dsink = attn_core_bwd(q, do, k4, v4, sinks, bg=[ca])
    big["w_in1"], = zip(ca.arrs, ca.result)
    dh2, dwq, dbq, dg_mix1 = attn_bwd_q(h2, dh3, dq, hn1, g_mix1, wq)
    dh2, dwkv, dbkv, dg_kv = attn_bwd_kv(h2, dh2, dk4, dv4, kvn, g_kv, wkv, spread)
    pb = BgPair([dwkv.reshape(NDEV, 128, 512), dwq.reshape(NDEV, 128, D), dwo.reshape(NDEV, 128, D)])
    ca2 = BgChips([p_out1])
    dh2p = _to_chunked(dh2)
    dh1, dwin0, dwout0, dg_mlp0 = mlp_bwd(h1, hm0, r0, g_mlp0, dh2p, win0, wout0, 0, bg=[pb, ca2])
    big["w_out1"], = zip(ca2.arrs, ca2.result)
    cb = BgChips(reduce_pairs(["w_kv", "w_q", "w_o"], pb))
    pc = BgPair([dwin0, dwout0])
    dy, dwglu, dbglu = s5_post_bwd(dh1, y, z, wglu, bg=[cb, pc])
    big["w_kv"], big["w_q"], big["w_o"] = zip(cb.arrs, cb.result)
    cc = BgChips(reduce_pairs(["w_in0", "w_out0"], pc))
    pd = BgPair([dwglu])
    du, dbm, dcmt, dlam = s5_core_bwd(hn0, dy, bmb, lam, cmb, bg=[cc, pd])
    big["w_in0"], big["w_out0"] = zip(cc.arrs, cc.result)
    cd = BgChips(reduce_pairs(["s5_w_glu"], pd))
    dxp, dg_mix0, dd = s5_pre_bwd(xp, g_mix0, du, dy, d_skip, dh1, bg=[cd])
    big["s5_w_glu"], = zip(cd.arrs, cd.result)
    grad_x = _from_chunked(dxp)
    da_re, da_im, dlog_dt, db_re, db_im, dc_re, dc_im = prep_vjp((dlam, dbm, dcmt.transpose(0, 2, 1)))

    def lanes(v_):
        v_ = v_.reshape(1, -1)
        return jnp.pad(v_, ((0, 0), (0, D - v_.shape[1])))

    small = jnp.concatenate([
        dg_mix0[0:1], dg_mix1[0:1], dg_mlp0[0:1], dg_mlp1[0:1], dg_kv[0:1], dg_fin[0:1], dd[0:1], dbq[0:1], dbo[0:1],
        dbglu[0:1].reshape(2, D), lanes(dbkv[0:1]),
        lanes(dsink[:, 0, :Q_PER_KV]), lanes(dlog_dt), lanes(loss[0:1, 0:1]), jnp.zeros((1, D), f32),
        da_re.reshape(4, D), da_im.reshape(4, D),
        db_re.transpose(0, 2, 1).reshape(64, D), db_im.transpose(0, 2, 1).reshape(64, D),
        dc_re.reshape(64, D), dc_im.reshape(64, D)], axis=0)
    return loss, grad_x, small, big


_ANY = pl.BlockSpec(memory_space=pl.ANY)


def _pos():
    return lax.axis_index("x"), lax.axis_index("y"), lax.axis_index("c")


def _other_chips(x, y):
    return [(1 - x, y), (x, 1 - y), (1 - x, 1 - y)]


def all_gather(arrs):
    n = len(arrs)

    def body(*refs):
        ins, outs = refs[:n], refs[n:2 * n]
        send_sems, recv_sems, local_sems = refs[2 * n:]
        x, y, c = _pos()
        me, sib = (x, y, c), (x, y, 1 - c)
        chips = _other_chips(x, y)

        def copy(a, k, block, to, src=None):
            dst = outs[a].at[4 * block[0] + 2 * block[1] + block[2]]
            return pltpu.make_async_remote_copy(
                src_ref=dst if src is None else src, dst_ref=dst, send_sem=send_sems.at[a, k],
                recv_sem=recv_sems.at[a, k], device_id=to, device_id_type=MESH)

        mine = [pltpu.make_async_copy(ins[a], outs[a].at[4 * x + 2 * y + c], local_sems.at[a]) for a in range(n)]
        for cp in mine:
            cp.start()
        first = []
        for a in range(n):
            first.append(copy(a, 0, me, sib, src=ins[a]))
            first += [copy(a, 1 + j, me, (*chip, c), src=ins[a]) for j, chip in enumerate(chips)]
        for cp in first:
            cp.start()
        passed = []
        for j, chip in enumerate(chips):
            for a in range(n):
                copy(a, 1 + j, (*chip, c), me).wait_recv()
                cp = copy(a, 4 + j, (*chip, c), sib)
                cp.start()
                passed.append(cp)
        for a in range(n):
            copy(a, 0, sib, me).wait_recv()
            for j, chip in enumerate(chips):
                copy(a, 4 + j, (*chip, 1 - c), me).wait_recv()
        for cp in first + passed:
            cp.wait_send()
        for cp in mine:
            cp.wait()

    return pl.pallas_call(
        body, name="all_gather", in_specs=[_ANY] * n, out_specs=[_ANY] * n,
        out_shape=[SDS((NDEV,) + a.shape, a.dtype) for a in arrs],
        scratch_shapes=[pltpu.SemaphoreType.DMA((n, 7)), pltpu.SemaphoreType.DMA((n, 7)),
                        pltpu.SemaphoreType.DMA((n,))],
    )(*arrs)


def rs_pair(grads):
    n = len(grads)

    def body(*refs):
        ins, outs = refs[:n], refs[n:2 * n]
        send_sems, recv_sems = refs[2 * n:]
        x, y, c = _pos()
        cps = []
        for a in range(n):
            for k in range(4):
                cps.append(pltpu.make_async_remote_copy(
                    src_ref=ins[a].at[2 * k + 1 - c], dst_ref=outs[a].at[k], send_sem=send_sems.at[a, k],
                    recv_sem=recv_sems.at[a, k], device_id=(x, y, 1 - c), device_id_type=MESH))
        for cp in cps:
            cp.start()
        for cp in cps:
            cp.wait_recv()
        for cp in cps:
            cp.wait_send()

    return pl.pallas_call(
        body, name="rs_pair", in_specs=[_ANY] * n, out_specs=[_ANY] * n,
        out_shape=[SDS((4,) + g.shape[1:], g.dtype) for g in grads],
        scratch_shapes=[pltpu.SemaphoreType.DMA((n, 4)), pltpu.SemaphoreType.DMA((n, 4))],
    )(*grads)


def rs_chips(parts):
    n = len(parts)

    def body(*refs):
        ins, outs = refs[:n], refs[n:2 * n]
        send_sems, recv_sems = refs[2 * n:]
        x, y, c = _pos()
        cps = []
        for a in range(n):
            for r, (px, py) in enumerate(_other_chips(x, y)):
                cps.append(pltpu.make_async_remote_copy(
                    src_ref=ins[a].at[2 * px + py], dst_ref=outs[a].at[r], send_sem=send_sems.at[a, r],
                    recv_sem=recv_sems.at[a, r], device_id=(px, py, c), device_id_type=MESH))
        for cp in cps:
            cp.start()
        for cp in cps:
            cp.wait_recv()
        for cp in cps:
            cp.wait_send()

    return pl.pallas_call(
        body, name="rs_chips", in_specs=[_ANY] * n, out_specs=[_ANY] * n,
        out_shape=[SDS((3,) + g.shape[1:], g.dtype) for g in parts],
        scratch_shapes=[pltpu.SemaphoreType.DMA((n, 3)), pltpu.SemaphoreType.DMA((n, 3))],
    )(*parts)


def _row_tile(r, c):
    return min(r, max(8, (512 * 1024) // c))


def add_pairs(g, r1, core, name):
    _, R, C = g.shape
    tr = _row_tile(R, C)

    def body(core_ref, g_ref, r_ref, o_ref):
        o_ref[...] = (g_ref[...].astype(f32) + r_ref[...].astype(f32)).astype(bf16)

    return pl.pallas_call(
        body, name=name, out_shape=SDS((4, R, C), bf16),
        grid_spec=pltpu.PrefetchScalarGridSpec(
            num_scalar_prefetch=1, grid=(4, R // tr),
            in_specs=[pl.BlockSpec((None, tr, C), lambda k, i, core: (2 * k + core[0], i, 0)),
                      pl.BlockSpec((None, tr, C), lambda k, i, core: (k, i, 0))],
            out_specs=pl.BlockSpec((None, tr, C), lambda k, i, core: (k, i, 0))),
        compiler_params=_cp(dimension_semantics=("arbitrary", "arbitrary")),
    )(core, g, r1)


def _adamw(w, g, m, v):
    m = ADAM_B1 * m + (1.0 - ADAM_B1) * g
    v = ADAM_B2 * v + (1.0 - ADAM_B2) * (g * g)
    m_hat = m / (1.0 - ADAM_B1 ** ADAM_STEP)
    v_hat = v / (1.0 - ADAM_B2 ** ADAM_STEP)
    delta = -ADAM_LR * (m_hat / (jnp.sqrt(v_hat) + ADAM_EPS) + ADAM_WD * w)
    return delta, m, v


def adam_big(w, m, v, part, r2, chip, name, layer=0, prev=None):
    L, R, C = w.shape
    tr = _row_tile(R, C)

    def body(chip_ref, w_ref, m_ref, v_ref, p_ref, r_ref, *rest):
        g_out, d_out, m_out, v_out = rest[-4:]
        g = p_ref[...].astype(f32) + r_ref[0].astype(f32) + r_ref[1].astype(f32) + r_ref[2].astype(f32)
        d, m_, v_ = _adamw(w_ref[...], g, m_ref[...], v_ref[...])
        g_out[...] = g
        d_out[...] = d
        m_out[...] = m_
        v_out[...] = v_

    blk = pl.BlockSpec((None, tr, C), lambda i, chip: (layer, i, 0))
    extra = [] if prev is None else list(prev)
    return pl.pallas_call(
        body, name=name, out_shape=[SDS((L, R, C), f32)] * 4,
        grid_spec=pltpu.PrefetchScalarGridSpec(
            num_scalar_prefetch=1, grid=(R // tr,),
            in_specs=[blk, blk, blk,
                      pl.BlockSpec((None, tr, C), lambda i, chip: (chip[0], i, 0)),
                      pl.BlockSpec((3, tr, C), lambda i, chip: (0, i, 0))] + [_ANY] * len(extra),
            out_specs=[blk] * 4),
        input_output_aliases={6 + k: k for k in range(len(extra))},
        compiler_params=_cp(dimension_semantics=("arbitrary",)),
    )(chip, w, m, v, part, r2, *extra)


def allreduce_small(buf):
    shp = buf.shape
    half = (shp[0] // 16) * 8
    parts = (pl.ds(0, half), pl.ds(half, shp[0] - half))

    def body(in_ref, out_ref, acc1, acc2, r0, r1, r2, send_sems, recv_sems):
        x, y, c = _pos()
        across = [(1 - x, y, c), (x, 1 - y, c)]

        def exchange(src, rcv, dst, copies):
            cps = [pltpu.make_async_remote_copy(
                src_ref=src.at[rows], dst_ref=rcv.at[rows], send_sem=send_sems.at[k], recv_sem=recv_sems.at[k],
                device_id=peer, device_id_type=MESH) for k, rows, peer in copies]
            for cp in cps:
                cp.start()
            for cp in cps:
                cp.wait()
            dst[...] = src[...] + rcv[...]

        exchange(in_ref, r0, acc1, [(0, pl.ds(0, shp[0]), (x, y, 1 - c))])
        exchange(acc1, r1, acc2, [(1, parts[0], across[0]), (2, parts[1], across[1])])
        exchange(acc2, r2, out_ref, [(3, parts[0], across[1]), (4, parts[1], across[0])])

    return pl.pallas_call(
        body, name="allreduce_small", out_shape=SDS(shp, f32),
        scratch_shapes=[pltpu.VMEM(shp, f32)] * 5 + [pltpu.SemaphoreType.DMA((5,)), pltpu.SemaphoreType.DMA((5,))],
    )(buf)


SMALL_ROWS = {'norm_mix': (0, 2, D), 'norm_mlp': (2, 2, D), 'norm_kv': (4, 1, D), 'norm_final': (5, 1, D),
              's5_d': (6, 1, D), 'b_q': (7, 1, D), 'b_o': (8, 1, D), 's5_b_glu': (9, 2, D), 'b_kv': (11, 1, 512),
              'sinks': (12, 1, 16), 's5_log_dt': (13, 1, 64), 's5_a_re': (16, 4, D), 's5_a_im': (20, 4, D),
              's5_b_re': (24, 64, D), 's5_b_im': (88, 64, D), 's5_c_re': (152, 64, D), 's5_c_im': (216, 64, D)}
LOSS_ROW = 14
ROW_PARAMS = ['norm_mix', 'norm_mlp', 'norm_kv', 'norm_final', 'b_q', 'b_o', 'b_kv', 'sinks', 's5_log_dt']
SHARD_PARAMS = ['s5_d', 's5_b_glu']
S5_PARAMS = ['s5_a_re', 's5_a_im', 's5_b_re', 's5_b_im', 's5_c_re', 's5_c_im']


def adam_small(dev, gsum, s5_grads, w, m, v):
    names = ROW_PARAMS + SHARD_PARAMS + S5_PARAMS
    n_g = len(ROW_PARAMS) + len(SHARD_PARAMS)

    def body(dev_ref, gs_ref, *refs):
        pos = [0]

        def take(k):
            r = refs[pos[0]:pos[0] + k]
            pos[0] += k
            return r

        g5 = take(len(S5_PARAMS))
        wr, mr, vr = take(len(names)), take(len(names)), take(len(names))
        g_out = take(n_g)
        d_out, m_out, v_out = take(len(names)), take(len(names)), take(len(names))
        dv = dev_ref[0]
        for i, n in enumerate(names):
            if n in S5_PARAMS:
                g = g5[S5_PARAMS.index(n)][...]
            elif n in SHARD_PARAMS:
                r0, _, _ = SMALL_ROWS[n]
                ln = wr[i].shape[1]
                g = jnp.zeros((1, ln), f32)
                for k in range(NDEV):
                    off = k * ln
                    piece = gs_ref[r0 + off // D:r0 + off // D + 1, off % D:off % D + ln]
                    g = g + jnp.where(dv == k, piece, 0.0)
                g_out[i][...] = g
            else:
                r0, nr, nl = SMALL_ROWS[n]
                g = gs_ref[r0:r0 + nr, 0:nl]
                g_out[i][...] = g
            d, m_, v_ = _adamw(wr[i][...], g, mr[i][...], vr[i][...])
            d_out[i][...] = d
            m_out[i][...] = m_
            v_out[i][...] = v_

    vm = pl.BlockSpec(memory_space=pltpu.VMEM)
    ins = [s5_grads[n] for n in S5_PARAMS] + [d[n] for d in (w, m, v) for n in names]
    shapes = [SDS(w[n].shape, f32) for n in names]
    res = pl.pallas_call(
        body, name="adam_small", in_specs=[pl.BlockSpec(memory_space=pltpu.SMEM)] + [vm] * (1 + len(ins)),
        out_specs=[vm] * (n_g + 3 * len(names)), out_shape=shapes[:n_g] + shapes * 3,
        compiler_params=_cp(),
    )(dev, gsum, *ins)
    g_o = dict(zip(names[:n_g], res[:n_g]))
    rest = res[n_g:]
    k = len(names)
    return g_o, dict(zip(names, rest[:k])), dict(zip(names, rest[k:2 * k])), dict(zip(names, rest[2 * k:]))


WEIGHTS = ['norm_mix', 'norm_mlp', 'norm_kv', 'norm_final', 's5_a_re', 's5_a_im', 's5_log_dt', 's5_b_re', 's5_b_im',
           's5_c_re', 's5_c_im', 's5_d', 's5_w_glu', 's5_b_glu', 'w_kv', 'b_kv', 'w_q', 'b_q', 'sinks', 'w_o', 'b_o',
           'w_mlp_in', 'w_mlp_out']
BIG = ['s5_w_glu', 'w_kv', 'w_q', 'w_o', 'w_mlp_in', 'w_mlp_out']
BIG_2D = {'s5_w_glu': (D, 256), 'w_kv': (128, 512), 'w_q': (128, D), 'w_o': (128, D), 'w_mlp_in': (2 * D, 512),
          'w_mlp_out': (2 * 512, D)}
SHARDED_SMALL = {'s5_d': D, 's5_b_glu': 2 * D}
SMALL = [n for n in WEIGHTS if n not in BIG]
SMALL_SIZE = {'norm_mix': 2 * D, 'norm_mlp': 2 * D, 'norm_kv': D, 'norm_final': D, 's5_a_re': 4096, 's5_a_im': 4096,
              's5_log_dt': 64, 's5_b_re': 65536, 's5_b_im': 65536, 's5_c_re': 65536, 's5_c_im': 65536, 's5_d': D,
              's5_b_glu': 2 * D, 'b_kv': 512, 'b_q': D, 'sinks': 16, 'b_o': D}


def _pack(vals):
    parts = []
    for n in SMALL:
        v = vals[n].reshape(-1).astype(f32)
        parts.append(jnp.pad(v, (0, (-v.shape[0]) % 128)))
    flat = jnp.concatenate(parts)
    flat = jnp.pad(flat, (0, (-flat.shape[0]) % 1024))
    return flat.reshape(-1, 128)


def _unpack(buf):
    flat = buf.reshape(-1)
    out, off = {}, 0
    for n in SMALL:
        sz = SMALL_SIZE[n]
        out[n] = flat[off:off + sz]
        off += sz + (-sz) % 128
    return out


def kernel(x, norm_mix, norm_mlp, norm_kv, norm_final, s5_a_re, s5_a_im, s5_log_dt, s5_b_re, s5_b_im, s5_c_re, s5_c_im, s5_d, s5_w_glu, s5_b_glu, w_kv, b_kv, w_q, b_q, sinks, w_o, b_o, w_mlp_in, w_mlp_out, loss_target, m_norm_mix, m_norm_mlp, m_norm_kv, m_norm_final, m_s5_a_re, m_s5_a_im, m_s5_log_dt, m_s5_b_re, m_s5_b_im, m_s5_c_re, m_s5_c_im, m_s5_d, m_s5_w_glu, m_s5_b_glu, m_w_kv, m_b_kv, m_w_q, m_b_q, m_sinks, m_w_o, m_b_o, m_w_mlp_in, m_w_mlp_out, v_norm_mix, v_norm_mlp, v_norm_kv, v_norm_final, v_s5_a_re, v_s5_a_im, v_s5_log_dt, v_s5_b_re, v_s5_b_im, v_s5_c_re, v_s5_c_im, v_s5_d, v_s5_w_glu, v_s5_b_glu, v_w_kv, v_b_kv, v_w_q, v_b_q, v_sinks, v_w_o, v_b_o, v_w_mlp_in, v_w_mlp_out):
    w = dict(norm_mix=norm_mix, norm_mlp=norm_mlp, norm_kv=norm_kv, norm_final=norm_final, s5_a_re=s5_a_re,
             s5_a_im=s5_a_im, s5_log_dt=s5_log_dt, s5_b_re=s5_b_re, s5_b_im=s5_b_im, s5_c_re=s5_c_re, s5_c_im=s5_c_im,
             s5_d=s5_d, s5_w_glu=s5_w_glu, s5_b_glu=s5_b_glu, w_kv=w_kv, b_kv=b_kv, w_q=w_q, b_q=b_q, sinks=sinks,
             w_o=w_o, b_o=b_o, w_mlp_in=w_mlp_in, w_mlp_out=w_mlp_out)
    m = dict(norm_mix=m_norm_mix, norm_mlp=m_norm_mlp, norm_kv=m_norm_kv, norm_final=m_norm_final, s5_a_re=m_s5_a_re,
             s5_a_im=m_s5_a_im, s5_log_dt=m_s5_log_dt, s5_b_re=m_s5_b_re, s5_b_im=m_s5_b_im, s5_c_re=m_s5_c_re,
             s5_c_im=m_s5_c_im, s5_d=m_s5_d, s5_w_glu=m_s5_w_glu, s5_b_glu=m_s5_b_glu, w_kv=m_w_kv, b_kv=m_b_kv,
             w_q=m_w_q, b_q=m_b_q, sinks=m_sinks, w_o=m_w_o, b_o=m_b_o, w_mlp_in=m_w_mlp_in, w_mlp_out=m_w_mlp_out)
    v = dict(norm_mix=v_norm_mix, norm_mlp=v_norm_mlp, norm_kv=v_norm_kv, norm_final=v_norm_final, s5_a_re=v_s5_a_re,
             s5_a_im=v_s5_a_im, s5_log_dt=v_s5_log_dt, s5_b_re=v_s5_b_re, s5_b_im=v_s5_b_im, s5_c_re=v_s5_c_re,
             s5_c_im=v_s5_c_im, s5_d=v_s5_d, s5_w_glu=v_s5_w_glu, s5_b_glu=v_s5_b_glu, w_kv=v_w_kv, b_kv=v_b_kv,
             w_q=v_w_q, b_q=v_b_q, sinks=v_sinks, w_o=v_w_o, b_o=v_b_o, w_mlp_in=v_w_mlp_in, w_mlp_out=v_w_mlp_out)
    xi, yi, ci = _pos()
    dev = 4 * xi + 2 * yi + ci
    core = ci.reshape(1).astype(jnp.int32)
    chip = (2 * xi + yi).reshape(1).astype(jnp.int32)

    shards = {
        "s5_w_glu": s5_w_glu[0].astype(bf16), "w_kv": w_kv.astype(bf16), "w_q": w_q[0].astype(bf16),
        "w_o": w_o[0].astype(bf16), "w_in0": w_mlp_in[0].astype(bf16), "w_in1": w_mlp_in[1].astype(bf16),
        "w_out0": w_mlp_out[0].astype(bf16), "w_out1": w_mlp_out[1].astype(bf16),
        "vecs": jnp.broadcast_to(jnp.concatenate([s5_d, s5_b_glu], axis=1), (8, 384)),
    }
    _, grad_x, grads, big = fwd_bwd(x[0], loss_target[0], {n: w[n] for n in SMALL}, shards, core)

    out_g, out_d, out_m, out_v = {}, {}, {}, {}
    for n in ("s5_w_glu", "w_kv", "w_q", "w_o"):
        shp = w[n].shape
        r3 = (1,) + BIG_2D[n]
        res = adam_big(w[n].reshape(r3), m[n].reshape(r3), v[n].reshape(r3), *big[n], chip, f"adam_{n}")
        out_g[n], out_d[n], out_m[n], out_v[n] = [r.reshape(shp) for r in res]
    for n, k in (("w_mlp_in", "w_in"), ("w_mlp_out", "w_out")):
        res = adam_big(w[n], m[n], v[n], *big[k + "1"], chip, f"adam_{k}1", layer=1)
        res = adam_big(w[n], m[n], v[n], *big[k + "0"], chip, f"adam_{k}0", layer=0, prev=res)
        out_g[n], out_d[n], out_m[n], out_v[n] = res

    gsum = allreduce_small(grads)
    loss = gsum[LOSS_ROW, 0]
    swapped = ("s5_b_re", "s5_b_im")
    swap = lambda a: a.transpose(0, 1, 3, 2)

    def kernel_side(d):
        d = {n: (d[n].reshape(1, -1) if d[n].ndim == 1 else d[n]) for n in SMALL}
        d.update({n: swap(d[n]) for n in swapped})
        return d

    s5_g = {}
    for n in S5_PARAMS:
        r0, nr, _ = SMALL_ROWS[n]
        s5_g[n] = gsum[r0:r0 + nr].reshape((1, 64, 16, 64) if n in swapped else w[n].shape)
        out_g[n] = s5_g[n]
    g_s, d_s, m_s, v_s = adam_small(dev.reshape(1).astype(jnp.int32), gsum, s5_g, kernel_side(w), kernel_side(m),
                                    kernel_side(v))
    for src, dst in ((g_s, out_g), (d_s, out_d), (m_s, out_m), (v_s, out_v)):
        dst.update(src)
    for dst in (out_g, out_d, out_m, out_v):
        for n in SMALL:
            dst[n] = (swap(dst[n]) if n in swapped else dst[n]).reshape(w[n].shape)

    return (loss, grad_x[None], *[out_g[n] for n in WEIGHTS], *[out_d[n] for n in WEIGHTS],
            *[out_m[n] for n in WEIGHTS], *[out_v[n] for n in WEIGHTS])
```

```python
import functools
import math

import jax
import jax.numpy as jnp
from jax import lax
from jax.experimental import pallas as pl
from jax.experimental.pallas import tpu as pltpu

f32 = jnp.float32
bf16 = jnp.bfloat16
SDS = jax.ShapeDtypeStruct

T = 2048
D = 1024
NDEV = 8
NORM_EPS = 1e-5
S5_G, S5_C, S5_P = 64, 16, 64
S5_SUB = 8
S5_CH = 8
S5_STEPS = T // S5_CH
DT_MIN_LAMBDA = -1e-4
HEAD_DIM = 64
N_KV = 4
Q_PER_KV = 4
BLK = 128
D_FF_SHARD = 512
ADAM_LR, ADAM_B1, ADAM_B2, ADAM_EPS, ADAM_WD, ADAM_STEP = 0.001, 0.9, 0.999, 1e-08, 0.01, 10
VMEM_LIMIT = 56 * 1024 * 1024
MESH = pl.DeviceIdType.MESH


def _cp(**kw):
    return pltpu.CompilerParams(vmem_limit_bytes=VMEM_LIMIT, **kw)


def _dot(a, b):
    return jnp.dot(a, b, preferred_element_type=f32)


def _dot_nt(a, b):
    return lax.dot_general(a, b, (((1,), (1,)), ((), ())), preferred_element_type=f32)


def _dot_tn(a, b):
    return lax.dot_general(a, b, (((0,), (0,)), ((), ())), preferred_element_type=f32)


def _rms(x, g):
    r = lax.rsqrt(jnp.mean(x * x, axis=-1, keepdims=True) + NORM_EPS)
    return x * r * g, r


def _rms_bwd(x, g, dy):
    r = lax.rsqrt(jnp.mean(x * x, axis=-1, keepdims=True) + NORM_EPS)
    u = dy * g
    dx = r * u - (r * r * r) * x * jnp.mean(u * x, axis=-1, keepdims=True)
    return dx, dy * x * r


def _colsum8(v):
    s = jnp.sum(v, axis=0, keepdims=True)
    row = lax.broadcasted_iota(jnp.int32, (8, v.shape[1]), 0)
    return jnp.where(row == 0, jnp.broadcast_to(s, (8, v.shape[1])), 0.0)


def _full(shape):
    nd = len(shape)
    return pl.BlockSpec(shape, lambda *_: (0,) * nd, pipeline_mode=pl.Buffered(1))


_ANY = pl.BlockSpec(memory_space=pl.ANY)


def _pos():
    return lax.axis_index("x"), lax.axis_index("y"), lax.axis_index("c")


def _other_chips(x, y):
    return [(1 - x, y), (x, 1 - y), (1 - x, 1 - y)]


class BgGather:
    SIB, XN, YN, FWD_Y, FWD_X, SIB_X, SIB_Y, SIB_D = range(8)

    def __init__(self, arrs, mids=(0.5, 0.75)):
        n = len(arrs)
        self.arrs = list(arrs)
        self.out_shape = [SDS((NDEV,) + a.shape, a.dtype) for a in arrs]
        self.scratch = [pltpu.SemaphoreType.DMA((n, 8)), pltpu.SemaphoreType.DMA((n, 8)),
                        pltpu.SemaphoreType.DMA((n,))]
        self.mids = mids
        self.result = None

    def mid_steps(self, nsteps):
        at = lambda f: min(nsteps - 1, max(0, int(f * nsteps) - 1))
        return [(at(self.mids[0]), self.mid), (max(at(self.mids[0]), at(self.mids[1])), self.mid2)]

    def _halves(self, a):
        rows = self.arrs[a].shape[0]
        cut = rows // 2 if rows >= 32 else rows
        return (0, cut), (cut, rows - cut)

    def _copy(self, ins, outs, sems, a, k, block, to, own=False, part=None):
        slot = 4 * block[0] + 2 * block[1] + block[2]
        rows = pl.ds(0, self.arrs[a].shape[0]) if part is None else pl.ds(*self._halves(a)[part])
        dst = outs[a].at[slot, rows]
        return pltpu.make_async_remote_copy(
            src_ref=ins[a].at[rows] if own else dst, dst_ref=dst, send_sem=sems[0].at[a, k],
            recv_sem=sems[1].at[a, k], device_id=to, device_id_type=MESH)

    def _mine(self, ins, outs, sems):
        x, y, c = _pos()
        return [pltpu.make_async_copy(ins[a], outs[a].at[4 * x + 2 * y + c], sems[2].at[a])
                for a in range(len(self.arrs))]

    def _split(self, a):
        return self._halves(a)[1][1] > 0

    def _sends(self, ins, outs, sems, phase):
        x, y, c = _pos()
        me, sib, xn, yn, dg = (x, y, c), (x, y, 1 - c), (1 - x, y, c), (x, 1 - y, c), (1 - x, 1 - y, c)
        cps = []
        for a in range(len(self.arrs)):
            cp = lambda k, block, to, **kw: self._copy(ins, outs, sems, a, k, block, to, **kw)
            if phase == 0:
                cps += [cp(self.SIB, me, sib, own=True), cp(self.XN, me, xn, own=True), cp(self.YN, me, yn, own=True)]
            elif phase == 1:
                cps.append(cp(self.FWD_Y, xn, yn, part=0))
                if self._split(a):
                    cps.append(cp(self.FWD_X, yn, xn, part=1))
                cps += [cp(self.SIB_X, xn, sib), cp(self.SIB_Y, yn, sib)]
            else:
                cps.append(cp(self.SIB_D, dg, sib))
        return cps

    def _arrivals(self, ins, outs, sems, phase):
        x, y, c = _pos()
        me, xn, yn, dg = (x, y, c), (1 - x, y, c), (x, 1 - y, c), (1 - x, 1 - y, c)
        cps = []
        for a in range(len(self.arrs)):
            cp = lambda k, block, **kw: self._copy(ins, outs, sems, a, k, block, me, **kw)
            if phase == 1:
                cps += [cp(self.XN, xn), cp(self.YN, yn)]
            elif phase == 2:
                cps.append(cp(self.FWD_Y, dg, part=0))
                if self._split(a):
                    cps.append(cp(self.FWD_X, dg, part=1))
            else:
                cps += [cp(self.SIB, (x, y, 1 - c)), cp(self.SIB_X, (1 - x, y, 1 - c)),
                        cp(self.SIB_Y, (x, 1 - y, 1 - c)), cp(self.SIB_D, (1 - x, 1 - y, 1 - c))]
        return cps

    def start(self, ins, outs, sems):
        for cp in self._mine(ins, outs, sems) + self._sends(ins, outs, sems, 0):
            cp.start()

    def mid(self, ins, outs, sems):
        for cp in self._arrivals(ins, outs, sems, 1):
            cp.wait_recv()
        for cp in self._sends(ins, outs, sems, 1):
            cp.start()

    def mid2(self, ins, outs, sems):
        for cp in self._arrivals(ins, outs, sems, 2):
            cp.wait_recv()
        for cp in self._sends(ins, outs, sems, 2):
            cp.start()

    def finish(self, ins, outs, sems):
        for cp in self._arrivals(ins, outs, sems, 3):
            cp.wait_recv()
        for ph in range(3):
            for cp in self._sends(ins, outs, sems, ph):
                cp.wait_send()
        for cp in self._mine(ins, outs, sems):
            cp.wait()


class BgPair:
    def __init__(self, arrs):
        n = len(arrs)
        self.arrs = list(arrs)
        self.out_shape = [SDS((4,) + a.shape[1:], a.dtype) for a in arrs]
        self.scratch = [pltpu.SemaphoreType.DMA((n, 4)), pltpu.SemaphoreType.DMA((n, 4))]
        self.result = None

    def mid_steps(self, nsteps):
        return []

    def _copies(self, ins, outs, sems):
        x, y, c = _pos()
        return [pltpu.make_async_remote_copy(
            src_ref=ins[a].at[2 * k + 1 - c], dst_ref=outs[a].at[k], send_sem=sems[0].at[a, k],
            recv_sem=sems[1].at[a, k], device_id=(x, y, 1 - c), device_id_type=MESH)
            for a in range(len(self.arrs)) for k in range(4)]

    def start(self, ins, outs, sems):
        for cp in self._copies(ins, outs, sems):
            cp.start()

    def finish(self, ins, outs, sems):
        cps = self._copies(ins, outs, sems)
        for cp in cps:
            cp.wait_recv()
        for cp in cps:
            cp.wait_send()


class BgChips(BgPair):
    def __init__(self, arrs):
        n = len(arrs)
        self.arrs = list(arrs)
        self.out_shape = [SDS((3,) + a.shape[1:], a.dtype) for a in arrs]
        self.scratch = [pltpu.SemaphoreType.DMA((n, 3)), pltpu.SemaphoreType.DMA((n, 3))]
        self.result = None

    def _copies(self, ins, outs, sems):
        x, y, c = _pos()
        return [pltpu.make_async_remote_copy(
            src_ref=ins[a].at[2 * px + py], dst_ref=outs[a].at[r], send_sem=sems[0].at[a, r],
            recv_sem=sems[1].at[a, r], device_id=(px, py, c), device_id_type=MESH)
            for a in range(len(self.arrs)) for r, (px, py) in enumerate(_other_chips(x, y))]


class AdamRider:
    def __init__(self, w, m, v, part, r2, layer=0, prev=None):
        self.arrs = [w, m, v, part, r2] + list(prev or [])
        self.n_prev = len(prev or [])
        self.layer = layer
        self.out_shape = [SDS(w.shape, f32)] * 4
        self.scratch = []
        self.aliases = {5 + k: k for k in range(self.n_prev)}
        self.result = None

    def _tile(self, grid):
        assert len(grid) == 1
        _, R, C = self.arrs[0].shape
        return R // grid[0], C

    def in_specs(self, grid):
        tr, C = self._tile(grid)
        layer = self.layer
        blk = pl.BlockSpec((None, tr, C), lambda b: (layer, b, 0))
        mine = pl.BlockSpec((None, tr, C), lambda b: (2 * lax.axis_index("x") + lax.axis_index("y"), b, 0))
        return [blk, blk, blk, mine, pl.BlockSpec((3, tr, C), lambda b: (0, b, 0))] + [_ANY] * self.n_prev

    def out_specs(self, grid):
        tr, C = self._tile(grid)
        layer = self.layer
        return [pl.BlockSpec((None, tr, C), lambda b: (layer, b, 0))] * 4

    def mid_steps(self, nsteps):
        return []

    def start(self, ins, outs, sems):
        pass

    finish = start

    def step(self, ins, outs, sems):
        w_ref, m_ref, v_ref, p_ref, r_ref = ins[:5]
        g = p_ref[...].astype(f32) + r_ref[0].astype(f32) + r_ref[1].astype(f32) + r_ref[2].astype(f32)
        d, m_, v_ = _adamw(w_ref[...], g, m_ref[...], v_ref[...])
        for ref, val in zip(outs, (g, d, m_, v_)):
            ref[...] = val


def _call(bgs, body, *, name, grid, in_specs, out_specs, out_shape, scratch_shapes=(), compiler_params=None):
    single = not isinstance(out_shape, (list, tuple))
    out_specs_l = [out_specs] if single else list(out_specs)
    out_shape_l = [out_shape] if single else list(out_shape)
    bgs = [b for b in (bgs or []) if b is not None]
    n_in, n_out, n_sc = len(in_specs), len(out_shape_l), len(scratch_shapes)
    nsteps = math.prod(grid)
    b_in_specs = [b.in_specs(grid) if hasattr(b, "in_specs") else [_ANY] * len(b.arrs) for b in bgs]
    b_out_specs = [b.out_specs(grid) if hasattr(b, "out_specs") else [_ANY] * len(b.out_shape) for b in bgs]
    aliases, i_off, o_off = {}, n_in, n_out
    for b in bgs:
        aliases.update({i_off + i: o_off + o for i, o in getattr(b, "aliases", {}).items()})
        i_off, o_off = i_off + len(b.arrs), o_off + len(b.out_shape)

    def full(*refs):
        pos = [0]

        def take(k):
            r = refs[pos[0]:pos[0] + k]
            pos[0] += k
            return r

        ins = take(n_in)
        b_ins = [take(len(b.arrs)) for b in bgs]
        outs = take(n_out)
        b_outs = [take(len(b.out_shape)) for b in bgs]
        sc = take(n_sc)
        b_sc = [take(len(b.scratch)) for b in bgs]
        if bgs:
            step = pl.program_id(0)
            for d in range(1, len(grid)):
                step = step * grid[d] + pl.program_id(d)

            @pl.when(step == 0)
            def _():
                for b, i_, o_, s_ in zip(bgs, b_ins, b_outs, b_sc):
                    b.start(i_, o_, s_)

        body(*ins, *outs, *sc)
        if bgs:
            for b, i_, o_, s_ in zip(bgs, b_ins, b_outs, b_sc):
                if hasattr(b, "step"):
                    b.step(i_, o_, s_)
                for at, fn in b.mid_steps(nsteps):
                    @pl.when(step == at)
                    def _():
                        fn(i_, o_, s_)

            @pl.when(step == nsteps - 1)
            def _():
                for b, i_, o_, s_ in zip(bgs, b_ins, b_outs, b_sc):
                    b.finish(i_, o_, s_)

    def run(*args):
        res = pl.pallas_call(
            full, name=name, grid=grid,
            in_specs=list(in_specs) + [s for l in b_in_specs for s in l],
            out_specs=out_specs_l + [s for l in b_out_specs for s in l],
            out_shape=out_shape_l + [s for b in bgs for s in b.out_shape],
            scratch_shapes=list(scratch_shapes) + [s for b in bgs for s in b.scratch],
            input_output_aliases=aliases,
            compiler_params=compiler_params,
        )(*args, *[a for b in bgs for a in b.arrs])
        rest = list(res[n_out:])
        for b in bgs:
            b.result, rest = rest[:len(b.out_shape)], rest[len(b.out_shape):]
        return res[0] if single else list(res[:n_out])

    return run


def s5_discretize(a_re, a_im, log_dt, b_re, b_im, c_re, c_im):
    lam_r = jnp.minimum(a_re, DT_MIN_LAMBDA)
    lam_i = a_im
    dt = jnp.exp(log_dt)[:, None]
    e = jnp.exp(lam_r * dt)
    lbr = e * jnp.cos(lam_i * dt)
    lbi = e * jnp.sin(lam_i * dt)
    den = lam_r * lam_r + lam_i * lam_i
    cf_r = ((lbr - 1.0) * lam_r + lbi * lam_i) / den
    cf_i = (lbi * lam_r - (lbr - 1.0) * lam_i) / den
    bb_r = cf_r[:, :, None] * b_re - cf_i[:, :, None] * b_im
    bb_i = cf_r[:, :, None] * b_im + cf_i[:, :, None] * b_re
    eye = jnp.eye(8, dtype=f32)

    def blk_b(m):
        return jnp.einsum('bgpc,gh->bgchp', m.reshape(8, 8, S5_P, S5_C), eye).reshape(8, 128, 512)

    def blk_c(m):
        return jnp.einsum('bgcp,gh->bgphc', m.reshape(8, 8, S5_C, S5_P), eye).reshape(8, 512, 128)

    bm = jnp.concatenate([blk_b(bb_r), blk_b(bb_i)], axis=-1)
    cm = jnp.concatenate([blk_c(c_re), -blk_c(c_im)], axis=1)
    lam = jnp.stack([lbr.reshape(8, 512), lbi.reshape(8, 512)], axis=1)
    lam = jnp.broadcast_to(lam[:, :, None, :], (8, 2, 8, 512))
    return lam, bm, cm


def _cmul(ar, ai, br, bi):
    return ar * br - ai * bi, ar * bi + ai * br


def _shift_rows(v, k, up):
    row = lax.broadcasted_iota(jnp.int32, v.shape, 0)
    if up:
        return jnp.where(row < 8 - k, pltpu.roll(v, 8 - k, 0), 0.0)
    return jnp.where(row >= k, pltpu.roll(v, k, 0), 0.0)


def _chunk_scan(S, lr, li, reverse, aux=None):
    z = jnp.zeros((8, 512), f32)
    U = 4

    def idx(i):
        return (S5_STEPS - 1 - i) if reverse else i

    def rows_of(s):
        return pl.ds(s * 8, 8) if isinstance(s, int) else pl.ds(pl.multiple_of(s * 8, 8), 8)

    def rec(xr, xi, row):
        br = S[row, 0:512]
        bi = S[row, 512:1024]
        return lr * xr - li * xi + br, lr * xi + li * xr + bi

    def step1(i, c):
        for u in range(U):
            c = rec(c[0], c[1], rows_of(idx(i * U + u)))
        return c

    er, ei = lax.fori_loop(0, S5_STEPS // U, step1, (z, z))
    ar, ai = lr, li
    for _ in range(8):
        ar, ai = _cmul(ar, ai, ar, ai)
    cr, ci = _shift_rows(er, 1, reverse), _shift_rows(ei, 1, reverse)
    for k in (1, 2, 4):
        sr, si = _shift_rows(cr, k, reverse), _shift_rows(ci, k, reverse)
        pr, pi_ = _cmul(ar, ai, sr, si)
        cr, ci = cr + pr, ci + pi_
        ar, ai = _cmul(ar, ai, ar, ai)

    if aux is None:
        def step2(i, c):
            for u in range(U):
                row = rows_of(idx(i * U + u))
                c = rec(c[0], c[1], row)
                S[row, 0:512] = c[0]
                S[row, 512:1024] = c[1]
            return c

        lax.fori_loop(0, S5_STEPS // U, step2, (cr, ci))
        return None

    def one(s, c):
        gr0, gi0, dr, di = c
        row = rows_of(s)
        gr, gi = rec(gr0, gi0, row)
        S[row, 0:512] = gr
        S[row, 512:1024] = gi
        prow = rows_of(s - 1)
        xr = aux[prow, 0:512]
        xi = aux[prow, 512:1024]
        return gr, gi, dr + gr * xr + gi * xi, di + gi * xr - gr * xi

    def step2(i, c):
        for u in range(U):
            c = one(S5_STEPS - 1 - (i * U + u), c)
        return c

    c = lax.fori_loop(0, S5_STEPS // U - 1, step2, (cr, ci, z, z))
    for s in range(U - 1, 0, -1):
        c = one(s, c)
    gr, gi, dr, di = c
    row0 = pl.ds(0, 8)
    gr, gi = rec(gr, gi, row0)
    S[row0, 0:512] = gr
    S[row0, 512:1024] = gi
    last = pl.ds((S5_STEPS - 1) * 8, 8)
    xr = _shift_rows(aux[last, 0:512], 1, False)
    xi = _shift_rows(aux[last, 512:1024], 1, False)
    dr = dr + gr * xr + gi * xi
    di = di + gi * xr - gr * xi
    return dr, di


_ROWS = 256


def _row_loop(fn):
    def body(r, c):
        fn(pl.ds(pl.multiple_of(r * _ROWS, _ROWS), _ROWS))
        return c
    lax.fori_loop(0, T // _ROWS, body, 0)


def s5_core_fwd(hn, bm, lam, cm, bg=()):
    def body(u_ref, b_ref, lam_ref, c_ref, ys_ref, S):
        def bu(rows):
            S[rows, :] = _dot(u_ref[rows, :], b_ref[...])
        _row_loop(bu)
        _chunk_scan(S, lam_ref[0], lam_ref[1], False)

        def ys(rows):
            ys_ref[rows, :] = _dot(S[rows, :].astype(bf16), c_ref[...])
        _row_loop(ys)

    return _call(
        bg, body, name="s5_core_fwd", grid=(S5_SUB,),
        in_specs=[pl.BlockSpec((T, 128), lambda b: (0, b)),
                  pl.BlockSpec((None, 128, 1024), lambda b: (b, 0, 0)),
                  pl.BlockSpec((None, 4, 8, 512), lambda b: (b, 0, 0, 0)),
                  pl.BlockSpec((None, 1024, 128), lambda b: (b, 0, 0))],
        out_specs=pl.BlockSpec((T, 128), lambda b: (0, b)),
        out_shape=SDS((T, D), f32),
        scratch_shapes=[pltpu.VMEM((T, 1024), f32)],
        compiler_params=_cp(dimension_semantics=("arbitrary",)),
    )(hn, bm, lam, cm)


_SEG = _ROWS // S5_CH


def _scan_tile(S, lr, li, k, carry, reverse, store, aux=None):
    steps = range(k * _SEG, (k + 1) * _SEG)
    for s in (reversed(steps) if reverse else steps):
        row = pl.ds(s * 8, 8)
        xr, xi = carry[0], carry[1]
        nr = lr * xr - li * xi + S[row, 0:512]
        ni = lr * xi + li * xr + S[row, 512:1024]
        if store:
            S[row, 0:512] = nr
            S[row, 512:1024] = ni
        if aux is not None and s >= 1:
            prow = pl.ds((s - 1) * 8, 8)
            pr, pi_ = aux[prow, 0:512], aux[prow, 512:1024]
            carry = (nr, ni, carry[2] + nr * pr + ni * pi_, carry[3] + ni * pr - nr * pi_)
        elif aux is not None:
            carry = (nr, ni, carry[2], carry[3])
        else:
            carry = (nr, ni)
    return carry


def _chunk_starts(er, ei, lr, li, reverse):
    ar, ai = lr, li
    for _ in range(8):
        ar, ai = _cmul(ar, ai, ar, ai)
    cr, ci = _shift_rows(er, 1, reverse), _shift_rows(ei, 1, reverse)
    for k in (1, 2, 4):
        sr, si = _shift_rows(cr, k, reverse), _shift_rows(ci, k, reverse)
        pr, pi_ = _cmul(ar, ai, sr, si)
        cr, ci = cr + pr, ci + pi_
        ar, ai = _cmul(ar, ai, ar, ai)
    return cr, ci


def s5_core_bwd(hn, dy, bm, lam, cm, bg=()):
    nt = T // _ROWS

    def body(u_ref, dy_ref, b_ref, lam_ref, c_ref, du_ref, db_ref, dct_ref, dlam_ref, S1, S2):
        lr, li, lcr, lci = lam_ref[0], lam_ref[1], lam_ref[2], lam_ref[3]
        z = jnp.zeros((8, 512), f32)
        tile = lambda k: pl.ds(k * _ROWS, _ROWS)
        dyb = lambda k: dy_ref[tile(k), :].astype(bf16)

        c = (z, z)
        for k in range(nt):
            S1[tile(k), :] = _dot(u_ref[tile(k), :], b_ref[...])
            if k >= 1:
                c = _scan_tile(S1, lr, li, k - 1, c, False, False)
        c = _scan_tile(S1, lr, li, nt - 1, c, False, False)

        c = _chunk_starts(c[0], c[1], lr, li, False)
        dct_ref[...] = jnp.zeros_like(dct_ref)
        for k in range(nt):
            c = _scan_tile(S1, lr, li, k, c, False, True)
            if k >= 1:
                dct_ref[...] += _dot_tn(dyb(k - 1), S1[tile(k - 1), :].astype(bf16))
        dct_ref[...] += _dot_tn(dyb(nt - 1), S1[tile(nt - 1), :].astype(bf16))

        S2[tile(nt - 1), :] = _dot_nt(dyb(nt - 1), c_ref[...])
        c = (z, z)
        for k in range(nt - 1, -1, -1):
            if k >= 1:
                S2[tile(k - 1), :] = _dot_nt(dyb(k - 1), c_ref[...])
            c = _scan_tile(S2, lcr, lci, k, c, True, False)

        def dbu(k):
            gb = S2[tile(k), :].astype(bf16)
            db_ref[...] += _dot_tn(u_ref[tile(k), :], gb)
            du_ref[tile(k), :] = _dot_nt(gb, b_ref[...])

        c = _chunk_starts(c[0], c[1], lcr, lci, True) + (z, z)
        db_ref[...] = jnp.zeros_like(db_ref)
        for k in range(nt - 1, -1, -1):
            c = _scan_tile(S2, lcr, lci, k, c, True, True, aux=S1)
            if k + 1 < nt:
                dbu(k + 1)
        dbu(0)
        gr, gi, dr, di = c
        last = pl.ds((S5_STEPS - 1) * 8, 8)
        xr = _shift_rows(S1[last, 0:512], 1, False)
        xi = _shift_rows(S1[last, 512:1024], 1, False)
        dlam_ref[0] = dr + gr * xr + gi * xi
        dlam_ref[1] = di + gi * xr - gr * xi

    return _call(
        bg, body, name="s5_core_bwd", grid=(S5_SUB,),
        in_specs=[pl.BlockSpec((T, 128), lambda b: (0, b)),
                  pl.BlockSpec((T, 128), lambda b: (0, b)),
                  pl.BlockSpec((None, 128, 1024), lambda b: (b, 0, 0)),
                  pl.BlockSpec((None, 4, 8, 512), lambda b: (b, 0, 0, 0)),
                  pl.BlockSpec((None, 1024, 128), lambda b: (b, 0, 0))],
        out_specs=[pl.BlockSpec((T, 128), lambda b: (0, b)),
                   pl.BlockSpec((None, 128, 1024), lambda b: (b, 0, 0)),
                   pl.BlockSpec((None, 128, 1024), lambda b: (b, 0, 0)),
                   pl.BlockSpec((None, 2, 8, 512), lambda b: (b, 0, 0, 0))],
        out_shape=[SDS((T, D), f32), SDS((8, 128, 1024), f32), SDS((8, 128, 1024), f32), SDS((8, 2, 8, 512), f32)],
        scratch_shapes=[pltpu.VMEM((T, 1024), f32), pltpu.VMEM((T, 1024), f32)],
        compiler_params=_cp(dimension_semantics=("arbitrary",)),
    )(hn, dy, bm, lam, cm)


TM = 512
NT = T // TM


def _tile(n=D):
    return pl.BlockSpec((TM, n), lambda i: (i, 0))


def s5_pre(xp, g):
    def body(x_ref, g_ref, hn_ref):
        hn, _ = _rms(x_ref[...], g_ref[...])
        hn_ref[...] = hn.astype(bf16)

    return pl.pallas_call(
        body, name="s5_pre", grid=(NT,), in_specs=[_tile(), _full((1, D))], out_specs=_tile(),
        out_shape=SDS((T, D), bf16), compiler_params=_cp(dimension_semantics=("arbitrary",)),
    )(xp, g)


def _gelu_grad(y):
    c = math.sqrt(2.0 / math.pi)
    t = jnp.tanh(c * (y + 0.044715 * y * y * y))
    return 0.5 * (1.0 + t) + 0.5 * y * (1.0 - t * t) * c * (1.0 + 3.0 * 0.044715 * y * y)


def s5_post(ys, xp, g, d, wglu, bglu, bg=()):
    def body(ys_ref, x_ref, g_ref, d_ref, w_ref, b_ref, y_ref, z_ref, h_ref):
        x = x_ref[...]
        hn, _ = _rms(x, g_ref[...])
        y = ys_ref[...] + d_ref[...] * hn
        y_ref[...] = y
        yg = jax.nn.gelu(y).astype(bf16)
        for j in range(4):
            cv = slice(j * 256, (j + 1) * 256)
            cg = slice(1024 + j * 256, 1024 + (j + 1) * 256)
            val = _dot(yg, w_ref[j]) + b_ref[:, cv]
            gate = _dot(yg, w_ref[j + 4]) + b_ref[:, cg]
            z_ref[:, cv] = val
            z_ref[:, cg] = gate
            h_ref[:, cv] = x[:, cv] + val * jax.nn.sigmoid(gate)

    return _call(
        bg, body, name="s5_post", grid=(NT,),
        in_specs=[_tile(), _tile(), _full((1, D)), _full((1, D)), _full((8, D, 256)), _full((1, 2 * D))],
        out_specs=[_tile(), _tile(2 * D), _tile()],
        out_shape=[SDS((T, D), f32), SDS((T, 2 * D), f32), SDS((T, D), f32)],
        compiler_params=_cp(dimension_semantics=("arbitrary",)),
    )(ys, xp, g, d, wglu, bglu)


def s5_post_bwd(dh, y, z, wglu, bg=()):
    def body(dh_ref, y_ref, z_ref, w_ref, dy_ref, dw_ref, db_ref, acc):
        i = pl.program_id(0)

        @pl.when(i == 0)
        def _():
            acc[...] = jnp.zeros_like(acc)
            db_ref[...] = jnp.zeros_like(db_ref)

        dh_ = dh_ref[...]
        y = y_ref[...]
        yg = jax.nn.gelu(y).astype(bf16)
        dyg = jnp.zeros((TM, D), f32)
        for j in range(4):
            cv = slice(j * 256, (j + 1) * 256)
            cg = slice(1024 + j * 256, 1024 + (j + 1) * 256)
            val = z_ref[:, cv]
            sg = jax.nn.sigmoid(z_ref[:, cg])
            dval = dh_[:, cv] * sg
            dgate = dh_[:, cv] * val * sg * (1.0 - sg)
            db_ref[:, cv] += _colsum8(dval)
            db_ref[:, cg] += _colsum8(dgate)
            dvb = dval.astype(bf16)
            dgb = dgate.astype(bf16)
            acc[j] += _dot_tn(yg, dvb)
            acc[j + 4] += _dot_tn(yg, dgb)
            dyg = dyg + _dot_nt(dvb, w_ref[j]) + _dot_nt(dgb, w_ref[j + 4])
        dy_ref[...] = dyg * _gelu_grad(y)

        @pl.when(i == NT - 1)
        def _():
            dw_ref[...] = acc[...].astype(bf16)

    return _call(
        bg, body, name="s5_post_bwd", grid=(NT,),
        in_specs=[_tile(), _tile(), _tile(2 * D), _full((8, D, 256))],
        out_specs=[_tile(), _full((8, D, 256)), _full((8, 2 * D))],
        out_shape=[SDS((T, D), f32), SDS((8, D, 256), bf16), SDS((8, 2 * D), f32)],
        scratch_shapes=[pltpu.VMEM((8, D, 256), f32)],
        compiler_params=_cp(dimension_semantics=("arbitrary",)),
    )(dh, y, z, wglu)


def s5_pre_bwd(xp, g, du, dy, d, dh, bg=()):
    def body(x_ref, g_ref, du_ref, dy_ref, d_ref, dh_ref, dx_ref, dg_ref, dd_ref):
        i = pl.program_id(0)

        @pl.when(i == 0)
        def _():
            dg_ref[...] = jnp.zeros_like(dg_ref)
            dd_ref[...] = jnp.zeros_like(dd_ref)

        x = x_ref[...]
        g = g_ref[...]
        dy = dy_ref[...]
        hn, _ = _rms(x, g)
        dhn = du_ref[...] + d_ref[...] * dy
        dx, dgt = _rms_bwd(x, g, dhn)
        dx_ref[...] = dh_ref[...] + dx
        dg_ref[...] += _colsum8(dgt)
        dd_ref[...] += _colsum8(dy * hn)

    return _call(
        bg, body, name="s5_pre_bwd", grid=(NT,),
        in_specs=[_tile(), _full((1, D)), _tile(), _tile(), _full((1, D)), _tile()],
        out_specs=[_tile(), _full((8, D)), _full((8, D))],
        out_shape=[SDS((T, D), f32), SDS((8, D), f32), SDS((8, D), f32)],
        compiler_params=_cp(dimension_semantics=("arbitrary",)),
    )(xp, g, du, dy, d, dh)


TMF = 1024


def mlp_fwd(h, g, w_in, w_out, layer, bg=()):
    def body(h_ref, g_ref, wi_ref, wo_ref, hm_ref, r_ref, out_ref, acc):
        j = pl.program_id(1)

        @pl.when(j == 0)
        def _():
            hm, _ = _rms(h_ref[...], g_ref[...])
            hm_ref[...] = hm.astype(bf16)
            acc[...] = jnp.zeros_like(acc)

        a = jnp.maximum(_dot(hm_ref[...], wi_ref[...]), 0.0)
        r_ref[...] = a.astype(bf16)
        acc[...] += _dot((a * a).astype(bf16), wo_ref[...])

        @pl.when(j == NDEV - 1)
        def _():
            out_ref[...] = h_ref[...] + acc[...]

    return _call(
        bg, body, name=f"mlp_fwd{layer}", grid=(T // TMF, NDEV),
        in_specs=[pl.BlockSpec((TMF, D), lambda i, j: (i, 0)),
                  pl.BlockSpec((1, D), lambda i, j: (0, 0)),
                  pl.BlockSpec((None, D, D_FF_SHARD), lambda i, j: (j, 0, 0)),
                  pl.BlockSpec((None, D_FF_SHARD, D), lambda i, j: (j, 0, 0))],
        out_specs=[pl.BlockSpec((TMF, D), lambda i, j: (i, 0)), pl.BlockSpec((TMF, D_FF_SHARD), lambda i, j: (i, j)),
                   pl.BlockSpec((TMF, D), lambda i, j: (i, 0))],
        out_shape=[SDS((T, D), bf16), SDS((T, NDEV * D_FF_SHARD), bf16), SDS((T, D), f32)],
        scratch_shapes=[pltpu.VMEM((TMF, D), f32)],
        compiler_params=_cp(dimension_semantics=("arbitrary", "arbitrary")),
    )(h, g, w_in, w_out)


def mlp_bwd(h, hm, r, g, dout, dout_b, w_in, w_out, layer, bg=()):
    last = NDEV - 1

    def body(h_ref, hm_ref, r_ref, g_ref, do_ref, dob_ref, wi_ref, wo_ref, dh_ref, dwi_ref, dwo_ref, dg_ref,
             dhm, awi, awo):
        j = pl.program_id(0)
        i = pl.program_id(1)
        rows = pl.ds(pl.multiple_of(i * TM, TM), TM)

        @pl.when(i == 0)
        def _():
            awi[...] = jnp.zeros_like(awi)
            awo[...] = jnp.zeros_like(awo)

        hm_ = hm_ref[...]
        dob = dob_ref[...]
        r = r_ref[...].astype(f32)
        dz = (_dot_nt(dob, wo_ref[...]) * (2.0 * r)).astype(bf16)
        awo[...] += _dot_tn((r * r).astype(bf16), dob)
        awi[...] += _dot_tn(hm_, dz)
        part = _dot_nt(dz, wi_ref[...])

        @pl.when(j == 0)
        def _():
            dhm[rows, :] = part

        @pl.when(j > 0)
        def _():
            dhm[rows, :] += part

        @pl.when(i == NT - 1)
        def _():
            dwi_ref[...] = awi[...].astype(bf16)
            dwo_ref[...] = awo[...].astype(bf16)

        @pl.when(j == last)
        def _():
            @pl.when(i == 0)
            def _():
                dg_ref[...] = jnp.zeros_like(dg_ref)
            dx, dgt = _rms_bwd(h_ref[...], g_ref[...], dhm[rows, :])
            dh_ref[...] = do_ref[...] + dx
            dg_ref[...] += _colsum8(dgt)

    late = lambda j, i: (jnp.where(j == last, i, 0), 0)
    return _call(
        bg, body, name=f"mlp_bwd{layer}", grid=(NDEV, NT),
        in_specs=[pl.BlockSpec((TM, D), late),
                  pl.BlockSpec((TM, D), lambda j, i: (i, 0)),
                  pl.BlockSpec((TM, D_FF_SHARD), lambda j, i: (i, j)),
                  pl.BlockSpec((1, D), lambda j, i: (0, 0)),
                  pl.BlockSpec((TM, D), late),
                  pl.BlockSpec((TM, D), lambda j, i: (i, 0)),
                  pl.BlockSpec((None, D, D_FF_SHARD), lambda j, i: (j, 0, 0)),
                  pl.BlockSpec((None, D_FF_SHARD, D), lambda j, i: (j, 0, 0))],
        out_specs=[pl.BlockSpec((TM, D), late),
                   pl.BlockSpec((None, D, D_FF_SHARD), lambda j, i: (j, 0, 0)),
                   pl.BlockSpec((None, D_FF_SHARD, D), lambda j, i: (j, 0, 0)),
                   pl.BlockSpec((8, D), lambda j, i: (0, 0))],
        out_shape=[SDS((T, D), f32), SDS((NDEV, D, D_FF_SHARD), bf16), SDS((NDEV, D_FF_SHARD, D), bf16),
                   SDS((8, D), f32)],
        scratch_shapes=[pltpu.VMEM((T, D), f32), pltpu.VMEM((D, D_FF_SHARD), f32), pltpu.VMEM((D_FF_SHARD, D), f32)],
        compiler_params=_cp(dimension_semantics=("arbitrary", "arbitrary")),
    )(h, hm, r, g, dout, dout_b, w_in, w_out)


def _spread4():
    r = lax.broadcasted_iota(jnp.int32, (256, D), 0)
    c = lax.broadcasted_iota(jnp.int32, (256, D), 1)
    return ((c // 256 == r // HEAD_DIM) & (c % HEAD_DIM == r % HEAD_DIM)).astype(bf16)


def attn_pre(h, g_kv, g_mix, wkv, bkv, spread, wq, bq):
    def body(h_ref, gkv_ref, gm_ref, wkv_ref, bkv_ref, sp_ref, wq_ref, bq_ref, kvn_ref, hn_ref, k_ref, v_ref, q_ref):
        h_ = h_ref[...]
        kvn = _rms(h_, gkv_ref[...])[0].astype(bf16)
        hn = _rms(h_, gm_ref[...])[0].astype(bf16)
        kvn_ref[...] = kvn
        hn_ref[...] = hn
        kv = (_dot(kvn, wkv_ref[...]) + bkv_ref[...]).astype(bf16)
        k_ref[...] = _dot(kv[:, :256], sp_ref[...]).astype(bf16)
        v_ref[...] = _dot(kv[:, 256:], sp_ref[...]).astype(bf16)
        q_ref[...] = (_dot(hn, wq_ref[...]) + bq_ref[...]).astype(bf16)

    return pl.pallas_call(
        body, name="attn_pre", grid=(NT,),
        in_specs=[_tile(), _full((1, D)), _full((1, D)), _full((D, 512)), _full((1, 512)), _full((256, D)),
                  _full((D, D)), _full((1, D))],
        out_specs=[_tile()] * 5,
        out_shape=[SDS((T, D), bf16)] * 5,
        compiler_params=_cp(dimension_semantics=("arbitrary",)),
    )(h, g_kv, g_mix, wkv, bkv, spread, wq, bq)


def _attn_specs():
    cur = pl.BlockSpec((TM, 256), lambda j, n: (n, j))
    prev = pl.BlockSpec((BLK, 256), lambda j, n: (jnp.maximum(n * (TM // BLK) - 1, 0), j))
    return cur, prev


def _head_mask(g):
    lane = lax.broadcasted_iota(jnp.int32, (1, 256), 1)
    return (lane >= g * HEAD_DIM) & (lane < (g + 1) * HEAD_DIM)


def _stack_heads(t):
    return jnp.concatenate([jnp.where(_head_mask(g), t, 0) for g in range(Q_PER_KV)], axis=0)


def _unstack_heads(t):
    out = jnp.where(_head_mask(0), t[0:BLK], 0.0)
    for g in range(1, Q_PER_KV):
        out = out + jnp.where(_head_mask(g), t[g * BLK:(g + 1) * BLK], 0.0)
    return out


def _attn_probs(qs, k2, sinks, first):
    rows = Q_PER_KV * BLK
    s = _dot_nt(qs, k2) * (1.0 / math.sqrt(HEAD_DIM))
    qi = jnp.bitwise_and(lax.broadcasted_iota(jnp.int32, (rows, 2 * BLK), 0), BLK - 1)
    kj = lax.broadcasted_iota(jnp.int32, (rows, 2 * BLK), 1)
    diff = qi + BLK - kj
    valid = (diff >= 0) & (diff < BLK) & (jnp.logical_not(first) | (kj >= BLK))
    s = jnp.where(valid, s, -jnp.inf)
    rb = lax.broadcasted_iota(jnp.int32, (rows, 1), 0)
    sink = jnp.where(rb < BLK, sinks[0], jnp.where(rb < 2 * BLK, sinks[1], jnp.where(rb < 3 * BLK, sinks[2], sinks[3])))
    m = jnp.maximum(jnp.max(s, axis=-1, keepdims=True), sink)
    p = jnp.exp(s - m)
    ps = jnp.exp(sink - m)
    denom = jnp.sum(p, axis=-1, keepdims=True) + ps
    return p / denom, ps / denom


def attn_core_fwd(q, k4, v4, sinks, bg=()):
    nb = TM // BLK

    def body(sink_ref, q_ref, kc_ref, kp_ref, vc_ref, vp_ref, o_ref):
        j = pl.program_id(0)
        n = pl.program_id(1)
        sk = [sink_ref[j * Q_PER_KV + g] for g in range(Q_PER_KV)]
        for b in range(nb):
            qb = q_ref[b * BLK:(b + 1) * BLK, :]
            if b == 0:
                k2 = jnp.concatenate([kp_ref[...], kc_ref[0:BLK, :]], axis=0)
                v2 = jnp.concatenate([vp_ref[...], vc_ref[0:BLK, :]], axis=0)
                first = n == 0
            else:
                k2 = kc_ref[(b - 1) * BLK:(b + 1) * BLK, :]
                v2 = vc_ref[(b - 1) * BLK:(b + 1) * BLK, :]
                first = False
            a, _ = _attn_probs(_stack_heads(qb), k2, sk, first)
            o_ref[b * BLK:(b + 1) * BLK, :] = _unstack_heads(_dot(a.astype(bf16), v2)).astype(bf16)

    cur, prev = _attn_specs()
    return _call(
        bg, body, name="attn_core_fwd", grid=(N_KV, NT),
        in_specs=[pl.BlockSpec(memory_space=pltpu.SMEM), cur, cur, prev, cur, prev],
        out_specs=cur, out_shape=SDS((T, D), bf16),
        compiler_params=_cp(dimension_semantics=("arbitrary", "arbitrary")),
    )(sinks, q, k4, k4, v4, v4)


def attn_post(h, o, wo, bo):
    def body(h_ref, o_ref, w_ref, b_ref, out_ref):
        out_ref[...] = h_ref[...] + _dot(o_ref[...], w_ref[...]) + b_ref[...]

    return pl.pallas_call(
        body, name="attn_post", grid=(NT,), in_specs=[_tile(), _tile(), _full((D, D)), _full((1, D))],
        out_specs=_tile(), out_shape=SDS((T, D), f32), compiler_params=_cp(dimension_semantics=("arbitrary",)),
    )(h, o, wo, bo)


def attn_bwd_pre(dh, o, wo, bg=()):
    def body(dh_ref, o_ref, w_ref, do_ref, dw_ref, db_ref, acc):
        i = pl.program_id(0)

        @pl.when(i == 0)
        def _():
            acc[...] = jnp.zeros_like(acc)
            db_ref[...] = jnp.zeros_like(db_ref)

        dh_ = dh_ref[...]
        dhb = dh_.astype(bf16)
        do_ref[...] = _dot_nt(dhb, w_ref[...]).astype(bf16)
        acc[...] += _dot_tn(o_ref[...], dhb)
        db_ref[...] += _colsum8(dh_)

        @pl.when(i == NT - 1)
        def _():
            dw_ref[...] = acc[...].astype(bf16)

    return _call(
        bg, body, name="attn_bwd_pre", grid=(NT,), in_specs=[_tile(), _tile(), _full((D, D))],
        out_specs=[_tile(), _full((D, D)), _full((8, D))],
        out_shape=[SDS((T, D), bf16), SDS((D, D), bf16), SDS((8, D), f32)],
        scratch_shapes=[pltpu.VMEM((D, D), f32)],
        compiler_params=_cp(dimension_semantics=("arbitrary",)),
    )(dh, o, wo)


def attn_core_bwd(q, do, k4, v4, sinks, bg=()):
    nb = TM // BLK

    def body(sink_ref, q_ref, do_ref, kc_ref, kp_ref, vc_ref, vp_ref, dq_ref, dk_ref, dv_ref, ds_ref):
        j = pl.program_id(0)
        n = pl.program_id(1)

        @pl.when(n == 0)
        def _():
            dk_ref[...] = jnp.zeros_like(dk_ref)
            dv_ref[...] = jnp.zeros_like(dv_ref)
            ds_ref[...] = jnp.zeros_like(ds_ref)

        lane8 = lax.broadcasted_iota(jnp.int32, (8, 128), 1)
        row8 = lax.broadcasted_iota(jnp.int32, (8, 128), 0)
        sk = [sink_ref[j * Q_PER_KV + g] for g in range(Q_PER_KV)]
        for b in range(nb):
            qs = _stack_heads(q_ref[b * BLK:(b + 1) * BLK, :])
            dos = _stack_heads(do_ref[b * BLK:(b + 1) * BLK, :])
            if b == 0:
                k2 = jnp.concatenate([kp_ref[...], kc_ref[0:BLK, :]], axis=0)
                v2 = jnp.concatenate([vp_ref[...], vc_ref[0:BLK, :]], axis=0)
                first = n == 0
            else:
                k2 = kc_ref[(b - 1) * BLK:(b + 1) * BLK, :]
                v2 = vc_ref[(b - 1) * BLK:(b + 1) * BLK, :]
                first = False
            a, asink = _attn_probs(qs, k2, sk, first)
            dp = _dot_nt(dos, v2)
            dd = jnp.sum(a * dp, axis=-1, keepdims=True)
            dsc = (a * (dp - dd) * (1.0 / math.sqrt(HEAD_DIM))).astype(bf16)
            t = asink * dd
            for g in range(Q_PER_KV):
                dsink = -jnp.sum(t[g * BLK:(g + 1) * BLK], axis=0, keepdims=True)
                ds_ref[...] += jnp.where((lane8 == g) & (row8 == 0), jnp.broadcast_to(dsink, (8, 128)), 0.0)
            dq_ref[b * BLK:(b + 1) * BLK, :] = _unstack_heads(_dot(dsc, k2))
            dk2 = _dot_tn(dsc, qs)
            dv2 = _dot_tn(a.astype(bf16), dos)
            cur = pl.ds(pl.multiple_of(n * TM + b * BLK, BLK), BLK)
            dk_ref[cur, :] += dk2[BLK:, :]
            dv_ref[cur, :] += dv2[BLK:, :]
            if b == 0:
                @pl.when(n > 0)
                def _():
                    prv = pl.ds(pl.multiple_of(n * TM - BLK, BLK), BLK)
                    dk_ref[prv, :] += dk2[:BLK, :]
                    dv_ref[prv, :] += dv2[:BLK, :]
            else:
                prv = pl.ds(pl.multiple_of(n * TM + (b - 1) * BLK, BLK), BLK)
                dk_ref[prv, :] += dk2[:BLK, :]
                dv_ref[prv, :] += dv2[:BLK, :]

    cur, prev = _attn_specs()
    col = pl.BlockSpec((T, 256), lambda j, n: (0, j))
    return _call(
        bg, body, name="attn_core_bwd", grid=(N_KV, NT),
        in_specs=[pl.BlockSpec(memory_space=pltpu.SMEM), cur, cur, cur, prev, cur, prev],
        out_specs=[cur, col, col, pl.BlockSpec((None, 8, 128), lambda j, n: (j, 0, 0))],
        out_shape=[SDS((T, D), f32), SDS((T, D), f32), SDS((T, D), f32), SDS((N_KV, 8, 128), f32)],
        compiler_params=_cp(dimension_semantics=("arbitrary", "arbitrary")),
    )(sinks, q, do, k4, k4, v4, v4)


def attn_bwd_q(h, dh, dq, hn, g_mix, wq):
    def body(h_ref, dh_ref, dq_ref, hn_ref, gm_ref, wq_ref, out_ref, dwq_ref, dbq_ref, dgm_ref, aq):
        i = pl.program_id(0)

        @pl.when(i == 0)
        def _():
            aq[...] = jnp.zeros_like(aq)
            dbq_ref[...] = jnp.zeros_like(dbq_ref)
            dgm_ref[...] = jnp.zeros_like(dgm_ref)

        dq_ = dq_ref[...]
        dqb = dq_.astype(bf16)
        aq[...] += _dot_tn(hn_ref[...], dqb)
        dbq_ref[...] += _colsum8(dq_)
        dx, dg = _rms_bwd(h_ref[...], gm_ref[...], _dot_nt(dqb, wq_ref[...]))
        out_ref[...] = dh_ref[...] + dx
        dgm_ref[...] += _colsum8(dg)

        @pl.when(i == NT - 1)
        def _():
            dwq_ref[...] = aq[...].astype(bf16)

    vec = _full((8, D))
    mat = _full((D, D))
    return pl.pallas_call(
        body, name="attn_bwd_q", grid=(NT,),
        in_specs=[_tile()] * 4 + [_full((1, D)), mat],
        out_specs=[_tile(), mat, vec, vec],
        out_shape=[SDS((T, D), f32), SDS((D, D), bf16), SDS((8, D), f32), SDS((8, D), f32)],
        scratch_shapes=[pltpu.VMEM((D, D), f32)],
        compiler_params=_cp(dimension_semantics=("arbitrary",)),
    )(h, dh, dq, hn, g_mix, wq)


def attn_bwd_kv(h, dh, dk4, dv4, kvn, g_kv, wkv, spread):
    def body(h_ref, dh_ref, dk_ref, dv_ref, kvn_ref, gkv_ref, wkv_ref, sp_ref, out_ref, outb_ref, dw_ref, db_ref,
             dgkv_ref, acc):
        i = pl.program_id(0)

        @pl.when(i == 0)
        def _():
            for r in (acc, db_ref, dgkv_ref):
                r[...] = jnp.zeros_like(r)

        dkv = jnp.concatenate([_dot_nt(dk_ref[...].astype(bf16), sp_ref[...]),
                               _dot_nt(dv_ref[...].astype(bf16), sp_ref[...])], axis=1)
        dkvb = dkv.astype(bf16)
        acc[...] += _dot_tn(kvn_ref[...], dkvb)
        db_ref[...] += _colsum8(dkv)
        dx, dg = _rms_bwd(h_ref[...], gkv_ref[...], _dot_nt(dkvb, wkv_ref[...]))
        out = dh_ref[...] + dx
        out_ref[...] = out
        outb_ref[...] = out.astype(bf16)
        dgkv_ref[...] += _colsum8(dg)

        @pl.when(i == NT - 1)
        def _():
            dw_ref[...] = acc[...].astype(bf16)

    return pl.pallas_call(
        body, name="attn_bwd_kv", grid=(NT,),
        in_specs=[_tile()] * 5 + [_full((1, D)), _full((D, 512)), _full((256, D))],
        out_specs=[_tile(), _tile(), _full((D, 512)), _full((8, 512)), _full((8, D))],
        out_shape=[SDS((T, D), f32), SDS((T, D), bf16), SDS((D, 512), bf16), SDS((8, 512), f32), SDS((8, D), f32)],
        scratch_shapes=[pltpu.VMEM((D, 512), f32)],
        compiler_params=_cp(dimension_semantics=("arbitrary",)),
    )(h, dh, dk4, dv4, kvn, g_kv, wkv, spread)


def final_loss(h, g, target):
    def body(h_ref, g_ref, t_ref, loss_ref, dh_ref, dhb_ref, dg_ref):
        i = pl.program_id(0)

        @pl.when(i == 0)
        def _():
            loss_ref[...] = jnp.zeros_like(loss_ref)
            dg_ref[...] = jnp.zeros_like(dg_ref)

        h_ = h_ref[...]
        g_ = g_ref[...]
        y, _ = _rms(h_, g_)
        diff = y - t_ref[...]
        per_tok = jnp.mean(diff * diff, axis=-1, keepdims=True)
        tot = 0.5 * jnp.sum(per_tok, axis=0, keepdims=True)
        lane = lax.broadcasted_iota(jnp.int32, (8, 128), 1)
        row = lax.broadcasted_iota(jnp.int32, (8, 128), 0)
        loss_ref[...] += jnp.where((lane == 0) & (row == 0), jnp.broadcast_to(tot, (8, 128)), 0.0)
        dx, dgt = _rms_bwd(h_, g_, diff * (1.0 / D))
        dh_ref[...] = dx
        dhb_ref[...] = dx.astype(bf16)
        dg_ref[...] += _colsum8(dgt)

    return pl.pallas_call(
        body, name="final_loss", grid=(NT,), in_specs=[_tile(), _full((1, D)), _tile()],
        out_specs=[_full((8, 128)), _tile(), _tile(), _full((8, D))],
        out_shape=[SDS((8, 128), f32), SDS((T, D), f32), SDS((T, D), bf16), SDS((8, D), f32)],
        compiler_params=_cp(dimension_semantics=("arbitrary",)),
    )(h, g, target)


def _to_chunked(a):
    return a.reshape(S5_CH, S5_STEPS, a.shape[-1]).transpose(1, 0, 2).reshape(T, a.shape[-1])


def _from_chunked(a):
    return a.reshape(S5_STEPS, S5_CH, a.shape[-1]).transpose(1, 0, 2).reshape(T, a.shape[-1])


def _rep4(w):
    return jnp.broadcast_to(w.reshape(w.shape[0], N_KV, 1, HEAD_DIM), (w.shape[0], N_KV, Q_PER_KV, HEAD_DIM)).reshape(
        w.shape[0], N_KV * Q_PER_KV * HEAD_DIM)


def _fold4(w):
    return w.reshape(w.shape[0], N_KV, Q_PER_KV, HEAD_DIM).sum(axis=2).reshape(w.shape[0], N_KV * HEAD_DIM)


def fwd_bwd(x, target, p, shards, opt, core, chip):
    row = lambda v: v.reshape(1, -1)
    (lam, bm, cm), prep_vjp = jax.vjp(s5_discretize, p["s5_a_re"][0], p["s5_a_im"][0], p["s5_log_dt"][0],
                                      p["s5_b_re"][0], p["s5_b_im"][0], p["s5_c_re"][0], p["s5_c_im"][0])
    bmb, cmb = bm.astype(bf16), cm.astype(bf16)
    lam = jnp.concatenate([lam, lam * jnp.array([1.0, -1.0], f32).reshape(1, 2, 1, 1)], axis=1)
    g_mix0, g_mix1 = row(p["norm_mix"][0]), row(p["norm_mix"][1])
    g_mlp0, g_mlp1 = row(p["norm_mlp"][0]), row(p["norm_mlp"][1])
    g_kv, g_fin = row(p["norm_kv"]), row(p["norm_final"])
    bq, bo = p["b_q"], p["b_o"]
    bkv = row(p["b_kv"])
    spread = _spread4()
    sinks = p["sinks"].reshape(16)

    def reduce_pairs(names, bg):
        return [add_pairs(g, r, core, f"add_pairs_{n}") for n, g, r in zip(names, bg.arrs, bg.result)]

    xp = _to_chunked(x)
    hn0 = s5_pre(xp, g_mix0)
    ga = BgGather([shards["s5_w_glu"], shards["vecs"], shards["w_in0"], shards["w_out0"]], mids=(0.75, 1.0))
    ys = s5_core_fwd(hn0, bmb, lam, cmb, bg=[ga])
    wglu, gvec, win0, wout0 = ga.result
    d_skip = gvec[:, 0, :128].reshape(1, D)
    bglu = gvec[:, 0, 128:].reshape(1, 2 * D)
    y, z, h1 = s5_post(ys, xp, g_mix0, d_skip, wglu, bglu)
    gc = BgGather([shards["w_kv"], shards["w_q"], shards["w_o"], shards["w_in1"]], mids=(0.8, 1.0))
    hm0, r0, h2p = mlp_fwd(h1, g_mlp0, win0, wout0, 0, bg=[gc])
    wkv, wq, wo, win1 = gc.result
    wkv, wq, wo = wkv.reshape(D, 512), wq.reshape(D, D), wo.reshape(D, D)
    h2 = _from_chunked(h2p)
    kvn, hn1, k4, v4, q = attn_pre(h2, g_kv, g_mix1, wkv, bkv, spread, wq, bq)
    gd = BgGather([shards["w_out1"]], mids=(0.94, 1.0))
    o = attn_core_fwd(q, k4, v4, sinks, bg=[gd])
    wout1, = gd.result
    h3 = attn_post(h2, o, wo, bo)
    hm1, r1, h4 = mlp_fwd(h3, g_mlp1, win1, wout1, 1)
    loss, dh4, dh4b, dg_fin = final_loss(h4, g_fin, target)

    big = {}
    rider = lambda n, part, r2, **kw: AdamRider(*opt[n], part, r2, **kw)
    dh3, dwin1, dwout1, dg_mlp1 = mlp_bwd(h3, hm1, r1, g_mlp1, dh4, dh4b, win1, wout1, 1)
    pa = BgPair([dwin1, dwout1])
    do, dwo, dbo = attn_bwd_pre(dh3, o, wo, bg=[pa])
    p_in1, p_out1 = reduce_pairs(["w_in1", "w_out1"], pa)
    ca = BgChips([p_in1])
    dq, dk4, dv4, dsink = attn_core_bwd(q, do, k4, v4, sinks, bg=[ca])
    dh2, dwq, dbq, dg_mix1 = attn_bwd_q(h2, dh3, dq, hn1, g_mix1, wq)
    dh2, dh2b, dwkv, dbkv, dg_kv = attn_bwd_kv(h2, dh2, dk4, dv4, kvn, g_kv, wkv, spread)
    pb = BgPair([dwkv.reshape(NDEV, 128, 512), dwq.reshape(NDEV, 128, D), dwo.reshape(NDEV, 128, D)])
    ca2 = BgChips([p_out1])
    dh2p, dh2pb = _to_chunked(dh2), _to_chunked(dh2b)
    dh1, dwin0, dwout0, dg_mlp0 = mlp_bwd(h1, hm0, r0, g_mlp0, dh2p, dh2pb, win0, wout0, 0, bg=[pb, ca2])
    cb = BgChips(reduce_pairs(["w_kv", "w_q", "w_o"], pb))
    pc = BgPair([dwin0, dwout0])
    dy, dwglu, dbglu = s5_post_bwd(dh1, y, z, wglu, bg=[cb, pc])
    cc = BgChips(reduce_pairs(["w_in0", "w_out0"], pc))
    pd = BgPair([dwglu])
    a_in1 = rider("w_mlp_in", p_in1, ca.result[0], layer=1)
    a_out1 = rider("w_mlp_out", p_out1, ca2.result[0], layer=1)
    a_attn = [rider(n, part, r2) for n, part, r2 in zip(("w_kv", "w_q", "w_o"), cb.arrs, cb.result)]
    du, dbm, dcmt, dlam = s5_core_bwd(hn0, dy, bmb, lam, cmb, bg=[cc, pd, a_in1, a_out1] + a_attn)
    big["w_kv"], big["w_q"], big["w_o"] = [a.result for a in a_attn]
    cd = BgChips(reduce_pairs(["s5_w_glu"], pd))
    a_in0 = rider("w_mlp_in", cc.arrs[0], cc.result[0], layer=0, prev=a_in1.result)
    a_out0 = rider("w_mlp_out", cc.arrs[1], cc.result[1], layer=0, prev=a_out1.result)
    dxp, dg_mix0, dd = s5_pre_bwd(xp, g_mix0, du, dy, d_skip, dh1, bg=[cd, a_in0, a_out0])
    big["w_mlp_in"], big["w_mlp_out"] = a_in0.result, a_out0.result
    big["s5_w_glu"] = adam_big(*opt["s5_w_glu"], cd.arrs[0], cd.result[0], chip, "adam_s5_w_glu")
    grad_x = _from_chunked(dxp)
    da_re, da_im, dlog_dt, db_re, db_im, dc_re, dc_im = prep_vjp((dlam, dbm, dcmt.transpose(0, 2, 1)))

    def lanes(v_):
        v_ = v_.reshape(1, -1)
        return jnp.pad(v_, ((0, 0), (0, D - v_.shape[1])))

    small = jnp.concatenate([
        dg_mix0[0:1], dg_mix1[0:1], dg_mlp0[0:1], dg_mlp1[0:1], dg_kv[0:1], dg_fin[0:1], dd[0:1], dbq[0:1], dbo[0:1],
        dbglu[0:1].reshape(2, D), lanes(dbkv[0:1]),
        lanes(dsink[:, 0, :Q_PER_KV]), lanes(dlog_dt), lanes(loss[0:1, 0:1]), jnp.zeros((1, D), f32),
        da_re.reshape(4, D), da_im.reshape(4, D),
        db_re.transpose(0, 2, 1).reshape(64, D), db_im.transpose(0, 2, 1).reshape(64, D),
        dc_re.reshape(64, D), dc_im.reshape(64, D)], axis=0)
    return loss, grad_x, small, big


_ANY = pl.BlockSpec(memory_space=pl.ANY)


def _pos():
    return lax.axis_index("x"), lax.axis_index("y"), lax.axis_index("c")


def _other_chips(x, y):
    return [(1 - x, y), (x, 1 - y), (1 - x, 1 - y)]


def all_gather(arrs):
    n = len(arrs)

    def body(*refs):
        ins, outs = refs[:n], refs[n:2 * n]
        send_sems, recv_sems, local_sems = refs[2 * n:]
        x, y, c = _pos()
        me, sib = (x, y, c), (x, y, 1 - c)
        chips = _other_chips(x, y)

        def copy(a, k, block, to, src=None):
            dst = outs[a].at[4 * block[0] + 2 * block[1] + block[2]]
            return pltpu.make_async_remote_copy(
                src_ref=dst if src is None else src, dst_ref=dst, send_sem=send_sems.at[a, k],
                recv_sem=recv_sems.at[a, k], device_id=to, device_id_type=MESH)

        mine = [pltpu.make_async_copy(ins[a], outs[a].at[4 * x + 2 * y + c], local_sems.at[a]) for a in range(n)]
        for cp in mine:
            cp.start()
        first = []
        for a in range(n):
            first.append(copy(a, 0, me, sib, src=ins[a]))
            first += [copy(a, 1 + j, me, (*chip, c), src=ins[a]) for j, chip in enumerate(chips)]
        for cp in first:
            cp.start()
        passed = []
        for j, chip in enumerate(chips):
            for a in range(n):
                copy(a, 1 + j, (*chip, c), me).wait_recv()
                cp = copy(a, 4 + j, (*chip, c), sib)
                cp.start()
                passed.append(cp)
        for a in range(n):
            copy(a, 0, sib, me).wait_recv()
            for j, chip in enumerate(chips):
                copy(a, 4 + j, (*chip, 1 - c), me).wait_recv()
        for cp in first + passed:
            cp.wait_send()
        for cp in mine:
            cp.wait()

    return pl.pallas_call(
        body, name="all_gather", in_specs=[_ANY] * n, out_specs=[_ANY] * n,
        out_shape=[SDS((NDEV,) + a.shape, a.dtype) for a in arrs],
        scratch_shapes=[pltpu.SemaphoreType.DMA((n, 7)), pltpu.SemaphoreType.DMA((n, 7)),
                        pltpu.SemaphoreType.DMA((n,))],
    )(*arrs)


def rs_pair(grads):
    n = len(grads)

    def body(*refs):
        ins, outs = refs[:n], refs[n:2 * n]
        send_sems, recv_sems = refs[2 * n:]
        x, y, c = _pos()
        cps = []
        for a in range(n):
            for k in range(4):
                cps.append(pltpu.make_async_remote_copy(
                    src_ref=ins[a].at[2 * k + 1 - c], dst_ref=outs[a].at[k], send_sem=send_sems.at[a, k],
                    recv_sem=recv_sems.at[a, k], device_id=(x, y, 1 - c), device_id_type=MESH))
        for cp in cps:
            cp.start()
        for cp in cps:
            cp.wait_recv()
        for cp in cps:
            cp.wait_send()

    return pl.pallas_call(
        body, name="rs_pair", in_specs=[_ANY] * n, out_specs=[_ANY] * n,
        out_shape=[SDS((4,) + g.shape[1:], g.dtype) for g in grads],
        scratch_shapes=[pltpu.SemaphoreType.DMA((n, 4)), pltpu.SemaphoreType.DMA((n, 4))],
    )(*grads)


def rs_chips(parts):
    n = len(parts)

    def body(*refs):
        ins, outs = refs[:n], refs[n:2 * n]
        send_sems, recv_sems = refs[2 * n:]
        x, y, c = _pos()
        cps = []
        for a in range(n):
            for r, (px, py) in enumerate(_other_chips(x, y)):
                cps.append(pltpu.make_async_remote_copy(
                    src_ref=ins[a].at[2 * px + py], dst_ref=outs[a].at[r], send_sem=send_sems.at[a, r],
                    recv_sem=recv_sems.at[a, r], device_id=(px, py, c), device_id_type=MESH))
        for cp in cps:
            cp.start()
        for cp in cps:
            cp.wait_recv()
        for cp in cps:
            cp.wait_send()

    return pl.pallas_call(
        body, name="rs_chips", in_specs=[_ANY] * n, out_specs=[_ANY] * n,
        out_shape=[SDS((3,) + g.shape[1:], g.dtype) for g in parts],
        scratch_shapes=[pltpu.SemaphoreType.DMA((n, 3)), pltpu.SemaphoreType.DMA((n, 3))],
    )(*parts)


def _row_tile(r, c):
    return min(r, max(8, (512 * 1024) // c))


def add_pairs(g, r1, core, name):
    _, R, C = g.shape
    tr = _row_tile(R, C)

    def body(core_ref, g_ref, r_ref, o_ref):
        o_ref[...] = (g_ref[...].astype(f32) + r_ref[...].astype(f32)).astype(bf16)

    return pl.pallas_call(
        body, name=name, out_shape=SDS((4, R, C), bf16),
        grid_spec=pltpu.PrefetchScalarGridSpec(
            num_scalar_prefetch=1, grid=(4, R // tr),
            in_specs=[pl.BlockSpec((None, tr, C), lambda k, i, core: (2 * k + core[0], i, 0)),
                      pl.BlockSpec((None, tr, C), lambda k, i, core: (k, i, 0))],
            out_specs=pl.BlockSpec((None, tr, C), lambda k, i, core: (k, i, 0))),
        compiler_params=_cp(dimension_semantics=("arbitrary", "arbitrary")),
    )(core, g, r1)


def _adamw(w, g, m, v):
    m = ADAM_B1 * m + (1.0 - ADAM_B1) * g
    v = ADAM_B2 * v + (1.0 - ADAM_B2) * (g * g)
    m_hat = m / (1.0 - ADAM_B1 ** ADAM_STEP)
    v_hat = v / (1.0 - ADAM_B2 ** ADAM_STEP)
    delta = -ADAM_LR * (m_hat / (jnp.sqrt(v_hat) + ADAM_EPS) + ADAM_WD * w)
    return delta, m, v


def adam_big(w, m, v, part, r2, chip, name, layer=0, prev=None):
    L, R, C = w.shape
    tr = _row_tile(R, C)

    def body(chip_ref, w_ref, m_ref, v_ref, p_ref, r_ref, *rest):
        g_out, d_out, m_out, v_out = rest[-4:]
        g = p_ref[...].astype(f32) + r_ref[0].astype(f32) + r_ref[1].astype(f32) + r_ref[2].astype(f32)
        d, m_, v_ = _adamw(w_ref[...], g, m_ref[...], v_ref[...])
        g_out[...] = g
        d_out[...] = d
        m_out[...] = m_
        v_out[...] = v_

    blk = pl.BlockSpec((None, tr, C), lambda i, chip: (layer, i, 0))
    extra = [] if prev is None else list(prev)
    return pl.pallas_call(
        body, name=name, out_shape=[SDS((L, R, C), f32)] * 4,
        grid_spec=pltpu.PrefetchScalarGridSpec(
            num_scalar_prefetch=1, grid=(R // tr,),
            in_specs=[blk, blk, blk,
                      pl.BlockSpec((None, tr, C), lambda i, chip: (chip[0], i, 0)),
                      pl.BlockSpec((3, tr, C), lambda i, chip: (0, i, 0))] + [_ANY] * len(extra),
            out_specs=[blk] * 4),
        input_output_aliases={6 + k: k for k in range(len(extra))},
        compiler_params=_cp(dimension_semantics=("arbitrary",)),
    )(chip, w, m, v, part, r2, *extra)


def allreduce_small(buf):
    shp = buf.shape
    half = (shp[0] // 16) * 8
    parts = (pl.ds(0, half), pl.ds(half, shp[0] - half))

    def body(in_ref, out_ref, acc1, acc2, r0, r1, r2, send_sems, recv_sems):
        x, y, c = _pos()
        across = [(1 - x, y, c), (x, 1 - y, c)]

        def exchange(src, rcv, dst, copies):
            cps = [pltpu.make_async_remote_copy(
                src_ref=src.at[rows], dst_ref=rcv.at[rows], send_sem=send_sems.at[k], recv_sem=recv_sems.at[k],
                device_id=peer, device_id_type=MESH) for k, rows, peer in copies]
            for cp in cps:
                cp.start()
            for cp in cps:
                cp.wait()
            dst[...] = src[...] + rcv[...]

        exchange(in_ref, r0, acc1, [(0, pl.ds(0, shp[0]), (x, y, 1 - c))])
        exchange(acc1, r1, acc2, [(1, parts[0], across[0]), (2, parts[1], across[1])])
        exchange(acc2, r2, out_ref, [(3, parts[0], across[1]), (4, parts[1], across[0])])

    return pl.pallas_call(
        body, name="allreduce_small", out_shape=SDS(shp, f32),
        scratch_shapes=[pltpu.VMEM(shp, f32)] * 5 + [pltpu.SemaphoreType.DMA((5,)), pltpu.SemaphoreType.DMA((5,))],
    )(buf)


SMALL_ROWS = {'norm_mix': (0, 2, D), 'norm_mlp': (2, 2, D), 'norm_kv': (4, 1, D), 'norm_final': (5, 1, D),
              's5_d': (6, 1, D), 'b_q': (7, 1, D), 'b_o': (8, 1, D), 's5_b_glu': (9, 2, D), 'b_kv': (11, 1, 512),
              'sinks': (12, 1, 16), 's5_log_dt': (13, 1, 64), 's5_a_re': (16, 4, D), 's5_a_im': (20, 4, D),
              's5_b_re': (24, 64, D), 's5_b_im': (88, 64, D), 's5_c_re': (152, 64, D), 's5_c_im': (216, 64, D)}
LOSS_ROW = 14
ROW_PARAMS = ['norm_mix', 'norm_mlp', 'norm_kv', 'norm_final', 'b_q', 'b_o', 'b_kv', 'sinks', 's5_log_dt']
SHARD_PARAMS = ['s5_d', 's5_b_glu']
S5_PARAMS = ['s5_a_re', 's5_a_im', 's5_b_re', 's5_b_im', 's5_c_re', 's5_c_im']


def adam_small(dev, gsum, s5_grads, w, m, v):
    names = ROW_PARAMS + SHARD_PARAMS + S5_PARAMS
    n_g = len(ROW_PARAMS) + len(SHARD_PARAMS)

    def body(dev_ref, gs_ref, *refs):
        pos = [0]

        def take(k):
            r = refs[pos[0]:pos[0] + k]
            pos[0] += k
            return r

        g5 = take(len(S5_PARAMS))
        wr, mr, vr = take(len(names)), take(len(names)), take(len(names))
        g_out = take(n_g)
        d_out, m_out, v_out = take(len(names)), take(len(names)), take(len(names))
        dv = dev_ref[0]
        for i, n in enumerate(names):
            if n in S5_PARAMS:
                g = g5[S5_PARAMS.index(n)][...]
            elif n in SHARD_PARAMS:
                r0, _, _ = SMALL_ROWS[n]
                ln = wr[i].shape[1]
                g = jnp.zeros((1, ln), f32)
                for k in range(NDEV):
                    off = k * ln
                    piece = gs_ref[r0 + off // D:r0 + off // D + 1, off % D:off % D + ln]
                    g = g + jnp.where(dv == k, piece, 0.0)
                g_out[i][...] = g
            else:
                r0, nr, nl = SMALL_ROWS[n]
                g = gs_ref[r0:r0 + nr, 0:nl]
                g_out[i][...] = g
            d, m_, v_ = _adamw(wr[i][...], g, mr[i][...], vr[i][...])
            d_out[i][...] = d
            m_out[i][...] = m_
            v_out[i][...] = v_

    vm = pl.BlockSpec(memory_space=pltpu.VMEM)
    ins = [s5_grads[n] for n in S5_PARAMS] + [d[n] for d in (w, m, v) for n in names]
    shapes = [SDS(w[n].shape, f32) for n in names]
    res = pl.pallas_call(
        body, name="adam_small", in_specs=[pl.BlockSpec(memory_space=pltpu.SMEM)] + [vm] * (1 + len(ins)),
        out_specs=[vm] * (n_g + 3 * len(names)), out_shape=shapes[:n_g] + shapes * 3,
        compiler_params=_cp(),
    )(dev, gsum, *ins)
    g_o = dict(zip(names[:n_g], res[:n_g]))
    rest = res[n_g:]
    k = len(names)
    return g_o, dict(zip(names, rest[:k])), dict(zip(names, rest[k:2 * k])), dict(zip(names, rest[2 * k:]))


WEIGHTS = ['norm_mix', 'norm_mlp', 'norm_kv', 'norm_final', 's5_a_re', 's5_a_im', 's5_log_dt', 's5_b_re', 's5_b_im',
           's5_c_re', 's5_c_im', 's5_d', 's5_w_glu', 's5_b_glu', 'w_kv', 'b_kv', 'w_q', 'b_q', 'sinks', 'w_o', 'b_o',
           'w_mlp_in', 'w_mlp_out']
BIG = ['s5_w_glu', 'w_kv', 'w_q', 'w_o', 'w_mlp_in', 'w_mlp_out']
BIG_2D = {'s5_w_glu': (D, 256), 'w_kv': (128, 512), 'w_q': (128, D), 'w_o': (128, D), 'w_mlp_in': (2 * D, 512),
          'w_mlp_out': (2 * 512, D)}
SHARDED_SMALL = {'s5_d': D, 's5_b_glu': 2 * D}
SMALL = [n for n in WEIGHTS if n not in BIG]
SMALL_SIZE = {'norm_mix': 2 * D, 'norm_mlp': 2 * D, 'norm_kv': D, 'norm_final': D, 's5_a_re': 4096, 's5_a_im': 4096,
              's5_log_dt': 64, 's5_b_re': 65536, 's5_b_im': 65536, 's5_c_re': 65536, 's5_c_im': 65536, 's5_d': D,
              's5_b_glu': 2 * D, 'b_kv': 512, 'b_q': D, 'sinks': 16, 'b_o': D}


def _pack(vals):
    parts = []
    for n in SMALL:
        v = vals[n].reshape(-1).astype(f32)
        parts.append(jnp.pad(v, (0, (-v.shape[0]) % 128)))
    flat = jnp.concatenate(parts)
    flat = jnp.pad(flat, (0, (-flat.shape[0]) % 1024))
    return flat.reshape(-1, 128)


def _unpack(buf):
    flat = buf.reshape(-1)
    out, off = {}, 0
    for n in SMALL:
        sz = SMALL_SIZE[n]
        out[n] = flat[off:off + sz]
        off += sz + (-sz) % 128
    return out


def kernel(x, norm_mix, norm_mlp, norm_kv, norm_final, s5_a_re, s5_a_im, s5_log_dt, s5_b_re, s5_b_im, s5_c_re, s5_c_im, s5_d, s5_w_glu, s5_b_glu, w_kv, b_kv, w_q, b_q, sinks, w_o, b_o, w_mlp_in, w_mlp_out, loss_target, m_norm_mix, m_norm_mlp, m_norm_kv, m_norm_final, m_s5_a_re, m_s5_a_im, m_s5_log_dt, m_s5_b_re, m_s5_b_im, m_s5_c_re, m_s5_c_im, m_s5_d, m_s5_w_glu, m_s5_b_glu, m_w_kv, m_b_kv, m_w_q, m_b_q, m_sinks, m_w_o, m_b_o, m_w_mlp_in, m_w_mlp_out, v_norm_mix, v_norm_mlp, v_norm_kv, v_norm_final, v_s5_a_re, v_s5_a_im, v_s5_log_dt, v_s5_b_re, v_s5_b_im, v_s5_c_re, v_s5_c_im, v_s5_d, v_s5_w_glu, v_s5_b_glu, v_w_kv, v_b_kv, v_w_q, v_b_q, v_sinks, v_w_o, v_b_o, v_w_mlp_in, v_w_mlp_out):
    w = dict(norm_mix=norm_mix, norm_mlp=norm_mlp, norm_kv=norm_kv, norm_final=norm_final, s5_a_re=s5_a_re,
             s5_a_im=s5_a_im, s5_log_dt=s5_log_dt, s5_b_re=s5_b_re, s5_b_im=s5_b_im, s5_c_re=s5_c_re, s5_c_im=s5_c_im,
             s5_d=s5_d, s5_w_glu=s5_w_glu, s5_b_glu=s5_b_glu, w_kv=w_kv, b_kv=b_kv, w_q=w_q, b_q=b_q, sinks=sinks,
             w_o=w_o, b_o=b_o, w_mlp_in=w_mlp_in, w_mlp_out=w_mlp_out)
    m = dict(norm_mix=m_norm_mix, norm_mlp=m_norm_mlp, norm_kv=m_norm_kv, norm_final=m_norm_final, s5_a_re=m_s5_a_re,
             s5_a_im=m_s5_a_im, s5_log_dt=m_s5_log_dt, s5_b_re=m_s5_b_re, s5_b_im=m_s5_b_im, s5_c_re=m_s5_c_re,
             s5_c_im=m_s5_c_im, s5_d=m_s5_d, s5_w_glu=m_s5_w_glu, s5_b_glu=m_s5_b_glu, w_kv=m_w_kv, b_kv=m_b_kv,
             w_q=m_w_q, b_q=m_b_q, sinks=m_sinks, w_o=m_w_o, b_o=m_b_o, w_mlp_in=m_w_mlp_in, w_mlp_out=m_w_mlp_out)
    v = dict(norm_mix=v_norm_mix, norm_mlp=v_norm_mlp, norm_kv=v_norm_kv, norm_final=v_norm_final, s5_a_re=v_s5_a_re,
             s5_a_im=v_s5_a_im, s5_log_dt=v_s5_log_dt, s5_b_re=v_s5_b_re, s5_b_im=v_s5_b_im, s5_c_re=v_s5_c_re,
             s5_c_im=v_s5_c_im, s5_d=v_s5_d, s5_w_glu=v_s5_w_glu, s5_b_glu=v_s5_b_glu, w_kv=v_w_kv, b_kv=v_b_kv,
             w_q=v_w_q, b_q=v_b_q, sinks=v_sinks, w_o=v_w_o, b_o=v_b_o, w_mlp_in=v_w_mlp_in, w_mlp_out=v_w_mlp_out)
    xi, yi, ci = _pos()
    dev = 4 * xi + 2 * yi + ci
    core = ci.reshape(1).astype(jnp.int32)
    chip = (2 * xi + yi).reshape(1).astype(jnp.int32)

    shards = {
        "s5_w_glu": s5_w_glu[0].astype(bf16), "w_kv": w_kv.astype(bf16), "w_q": w_q[0].astype(bf16),
        "w_o": w_o[0].astype(bf16), "w_in0": w_mlp_in[0].astype(bf16), "w_in1": w_mlp_in[1].astype(bf16),
        "w_out0": w_mlp_out[0].astype(bf16), "w_out1": w_mlp_out[1].astype(bf16),
        "vecs": jnp.broadcast_to(jnp.concatenate([s5_d, s5_b_glu], axis=1), (8, 384)),
    }
    as3d = lambda a, n: a if a.ndim == 3 and a.shape[0] == 2 else a.reshape((1,) + BIG_2D[n])
    opt = {n: (as3d(w[n], n), as3d(m[n], n), as3d(v[n], n)) for n in BIG}
    _, grad_x, grads, big = fwd_bwd(x[0], loss_target[0], {n: w[n] for n in SMALL}, shards, opt, core, chip)

    out_g, out_d, out_m, out_v = {}, {}, {}, {}
    for n in BIG:
        out_g[n], out_d[n], out_m[n], out_v[n] = [r.reshape(w[n].shape) for r in big[n]]

    gsum = allreduce_small(grads)
    loss = gsum[LOSS_ROW, 0]
    swapped = ("s5_b_re", "s5_b_im")
    swap = lambda a: a.transpose(0, 1, 3, 2)

    def kernel_side(d):
        d = {n: (d[n].reshape(1, -1) if d[n].ndim == 1 else d[n]) for n in SMALL}
        d.update({n: swap(d[n]) for n in swapped})
        return d

    s5_g = {}
    for n in S5_PARAMS:
        r0, nr, _ = SMALL_ROWS[n]
        s5_g[n] = gsum[r0:r0 + nr].reshape((1, 64, 16, 64) if n in swapped else w[n].shape)
        out_g[n] = s5_g[n]
    g_s, d_s, m_s, v_s = adam_small(dev.reshape(1).astype(jnp.int32), gsum, s5_g, kernel_side(w), kernel_side(m),
                                    kernel_side(v))
    for src, dst in ((g_s, out_g), (d_s, out_d), (m_s, out_m), (v_s, out_v)):
        dst.update(src)
    for dst in (out_g, out_d, out_m, out_v):
        for n in SMALL:
            dst[n] = (swap(dst[n]) if n in swapped else dst[n]).reshape(w[n].shape)

    return (loss, grad_x[None], *[out_g[n] for n in WEIGHTS], *[out_d[n] for n in WEIGHTS],
            *[out_m[n] for n in WEIGHTS], *[out_v[n] for n in WEIGHTS])
```

```python
import functools
import math

import jax
import jax.numpy as jnp
from jax import lax
from jax.experimental import pallas as pl
from jax.experimental.pallas import tpu as pltpu

f32 = jnp.float32
bf16 = jnp.bfloat16
SDS = jax.ShapeDtypeStruct

T = 2048
D = 1024
NDEV = 8
NORM_EPS = 1e-5
S5_G, S5_C, S5_P = 64, 16, 64
S5_SUB = 8
S5_CH = 8
S5_STEPS = T // S5_CH
DT_MIN_LAMBDA = -1e-4
HEAD_DIM = 64
N_KV = 4
Q_PER_KV = 4
BLK = 128
D_FF_SHARD = 512
ADAM_LR, ADAM_B1, ADAM_B2, ADAM_EPS, ADAM_WD, ADAM_STEP = 0.001, 0.9, 0.999, 1e-08, 0.01, 10
VMEM_LIMIT = 56 * 1024 * 1024
MESH = pl.DeviceIdType.MESH


def _cp(**kw):
    return pltpu.CompilerParams(vmem_limit_bytes=VMEM_LIMIT, **kw)


def _dot(a, b):
    return jnp.dot(a, b, preferred_element_type=f32)


def _dot_nt(a, b):
    return lax.dot_general(a, b, (((1,), (1,)), ((), ())), preferred_element_type=f32)


def _dot_tn(a, b):
    return lax.dot_general(a, b, (((0,), (0,)), ((), ())), preferred_element_type=f32)


def _rms(x, g):
    r = lax.rsqrt(jnp.mean(x * x, axis=-1, keepdims=True) + NORM_EPS)
    return x * r * g, r


def _rms_bwd(x, g, dy):
    r = lax.rsqrt(jnp.mean(x * x, axis=-1, keepdims=True) + NORM_EPS)
    u = dy * g
    dx = r * u - (r * r * r) * x * jnp.mean(u * x, axis=-1, keepdims=True)
    return dx, dy * x * r


def _colsum8(v):
    s = jnp.sum(v, axis=0, keepdims=True)
    row = lax.broadcasted_iota(jnp.int32, (8, v.shape[1]), 0)
    return jnp.where(row == 0, jnp.broadcast_to(s, (8, v.shape[1])), 0.0)


def _full(shape):
    nd = len(shape)
    return pl.BlockSpec(shape, lambda *_: (0,) * nd, pipeline_mode=pl.Buffered(1))


_ANY = pl.BlockSpec(memory_space=pl.ANY)


def _pos():
    return lax.axis_index("x"), lax.axis_index("y"), lax.axis_index("c")


def _other_chips(x, y):
    return [(1 - x, y), (x, 1 - y), (1 - x, 1 - y)]


class BgGather:
    SIB, XN, YN, FWD_Y, FWD_X, SIB_X, SIB_Y, SIB_D = range(8)

    def __init__(self, arrs, mids=(0.5, 0.75)):
        n = len(arrs)
        self.arrs = list(arrs)
        self.out_shape = [SDS((NDEV,) + a.shape, a.dtype) for a in arrs]
        self.scratch = [pltpu.SemaphoreType.DMA((n, 8)), pltpu.SemaphoreType.DMA((n, 8)),
                        pltpu.SemaphoreType.DMA((n,))]
        self.mids = mids
        self.result = None

    def mid_steps(self, nsteps):
        at = lambda f: min(nsteps - 1, max(0, int(f * nsteps) - 1))
        return [(at(self.mids[0]), self.mid), (max(at(self.mids[0]), at(self.mids[1])), self.mid2)]

    def _halves(self, a):
        rows = self.arrs[a].shape[0]
        cut = rows // 2 if rows >= 32 else rows
        return (0, cut), (cut, rows - cut)

    def _copy(self, ins, outs, sems, a, k, block, to, own=False, part=None):
        slot = 4 * block[0] + 2 * block[1] + block[2]
        rows = pl.ds(0, self.arrs[a].shape[0]) if part is None else pl.ds(*self._halves(a)[part])
        dst = outs[a].at[slot, rows]
        return pltpu.make_async_remote_copy(
            src_ref=ins[a].at[rows] if own else dst, dst_ref=dst, send_sem=sems[0].at[a, k],
            recv_sem=sems[1].at[a, k], device_id=to, device_id_type=MESH)

    def _mine(self, ins, outs, sems):
        x, y, c = _pos()
        return [pltpu.make_async_copy(ins[a], outs[a].at[4 * x + 2 * y + c], sems[2].at[a])
                for a in range(len(self.arrs))]

    def _split(self, a):
        return self._halves(a)[1][1] > 0

    def _sends(self, ins, outs, sems, phase):
        x, y, c = _pos()
        me, sib, xn, yn, dg = (x, y, c), (x, y, 1 - c), (1 - x, y, c), (x, 1 - y, c), (1 - x, 1 - y, c)
        cps = []
        for a in range(len(self.arrs)):
            cp = lambda k, block, to, **kw: self._copy(ins, outs, sems, a, k, block, to, **kw)
            if phase == 0:
                cps += [cp(self.SIB, me, sib, own=True), cp(self.XN, me, xn, own=True), cp(self.YN, me, yn, own=True)]
            elif phase == 1:
                cps.append(cp(self.FWD_Y, xn, yn, part=0))
                if self._split(a):
                    cps.append(cp(self.FWD_X, yn, xn, part=1))
                cps += [cp(self.SIB_X, xn, sib), cp(self.SIB_Y, yn, sib)]
            else:
                cps.append(cp(self.SIB_D, dg, sib))
        return cps

    def _arrivals(self, ins, outs, sems, phase):
        x, y, c = _pos()
        me, xn, yn, dg = (x, y, c), (1 - x, y, c), (x, 1 - y, c), (1 - x, 1 - y, c)
        cps = []
        for a in range(len(self.arrs)):
            cp = lambda k, block, **kw: self._copy(ins, outs, sems, a, k, block, me, **kw)
            if phase == 1:
                cps += [cp(self.XN, xn), cp(self.YN, yn)]
            elif phase == 2:
                cps.append(cp(self.FWD_Y, dg, part=0))
                if self._split(a):
                    cps.append(cp(self.FWD_X, dg, part=1))
            else:
                cps += [cp(self.SIB, (x, y, 1 - c)), cp(self.SIB_X, (1 - x, y, 1 - c)),
                        cp(self.SIB_Y, (x, 1 - y, 1 - c)), cp(self.SIB_D, (1 - x, 1 - y, 1 - c))]
        return cps

    def start(self, ins, outs, sems):
        for cp in self._mine(ins, outs, sems) + self._sends(ins, outs, sems, 0):
            cp.start()

    def mid(self, ins, outs, sems):
        for cp in self._arrivals(ins, outs, sems, 1):
            cp.wait_recv()
        for cp in self._sends(ins, outs, sems, 1):
            cp.start()

    def mid2(self, ins, outs, sems):
        for cp in self._arrivals(ins, outs, sems, 2):
            cp.wait_recv()
        for cp in self._sends(ins, outs, sems, 2):
            cp.start()

    def finish(self, ins, outs, sems):
        for cp in self._arrivals(ins, outs, sems, 3):
            cp.wait_recv()
        for ph in range(3):
            for cp in self._sends(ins, outs, sems, ph):
                cp.wait_send()
        for cp in self._mine(ins, outs, sems):
            cp.wait()


class BgPair:
    def __init__(self, arrs):
        n = len(arrs)
        self.arrs = list(arrs)
        self.out_shape = [SDS((4,) + a.shape[1:], a.dtype) for a in arrs]
        self.scratch = [pltpu.SemaphoreType.DMA((n, 4)), pltpu.SemaphoreType.DMA((n, 4))]
        self.result = None

    def mid_steps(self, nsteps):
        return []

    def _copies(self, ins, outs, sems):
        x, y, c = _pos()
        return [pltpu.make_async_remote_copy(
            src_ref=ins[a].at[2 * k + 1 - c], dst_ref=outs[a].at[k], send_sem=sems[0].at[a, k],
            recv_sem=sems[1].at[a, k], device_id=(x, y, 1 - c), device_id_type=MESH)
            for a in range(len(self.arrs)) for k in range(4)]

    def start(self, ins, outs, sems):
        for cp in self._copies(ins, outs, sems):
            cp.start()

    def finish(self, ins, outs, sems):
        cps = self._copies(ins, outs, sems)
        for cp in cps:
            cp.wait_recv()
        for cp in cps:
            cp.wait_send()


class BgChips(BgPair):
    def __init__(self, arrs):
        n = len(arrs)
        self.arrs = list(arrs)
        self.out_shape = [SDS((3,) + a.shape[1:], a.dtype) for a in arrs]
        self.scratch = [pltpu.SemaphoreType.DMA((n, 3)), pltpu.SemaphoreType.DMA((n, 3))]
        self.result = None

    def _copies(self, ins, outs, sems):
        x, y, c = _pos()
        return [pltpu.make_async_remote_copy(
            src_ref=ins[a].at[2 * px + py], dst_ref=outs[a].at[r], send_sem=sems[0].at[a, r],
            recv_sem=sems[1].at[a, r], device_id=(px, py, c), device_id_type=MESH)
            for a in range(len(self.arrs)) for r, (px, py) in enumerate(_other_chips(x, y))]


class AdamRider:
    def __init__(self, w, m, v, part, r2, layer=0, prev=None):
        self.arrs = [w, m, v, part, r2] + list(prev or [])
        self.n_prev = len(prev or [])
        self.layer = layer
        self.out_shape = [SDS(w.shape, f32)] * 4
        self.scratch = []
        self.aliases = {5 + k: k for k in range(self.n_prev)}
        self.result = None

    def _tile(self, grid):
        assert len(grid) == 1
        _, R, C = self.arrs[0].shape
        return R // grid[0], C

    def in_specs(self, grid):
        tr, C = self._tile(grid)
        layer = self.layer
        blk = pl.BlockSpec((None, tr, C), lambda b: (layer, b, 0))
        mine = pl.BlockSpec((None, tr, C), lambda b: (2 * lax.axis_index("x") + lax.axis_index("y"), b, 0))
        return [blk, blk, blk, mine, pl.BlockSpec((3, tr, C), lambda b: (0, b, 0))] + [_ANY] * self.n_prev

    def out_specs(self, grid):
        tr, C = self._tile(grid)
        layer = self.layer
        return [pl.BlockSpec((None, tr, C), lambda b: (layer, b, 0))] * 4

    def mid_steps(self, nsteps):
        return []

    def start(self, ins, outs, sems):
        pass

    finish = start

    def step(self, ins, outs, sems):
        w_ref, m_ref, v_ref, p_ref, r_ref = ins[:5]
        g = p_ref[...].astype(f32) + r_ref[0].astype(f32) + r_ref[1].astype(f32) + r_ref[2].astype(f32)
        d, m_, v_ = _adamw(w_ref[...], g, m_ref[...], v_ref[...])
        for ref, val in zip(outs, (g, d, m_, v_)):
            ref[...] = val


def _call(bgs, body, *, name, grid, in_specs, out_specs, out_shape, scratch_shapes=(), compiler_params=None):
    single = not isinstance(out_shape, (list, tuple))
    out_specs_l = [out_specs] if single else list(out_specs)
    out_shape_l = [out_shape] if single else list(out_shape)
    bgs = [b for b in (bgs or []) if b is not None]
    n_in, n_out, n_sc = len(in_specs), len(out_shape_l), len(scratch_shapes)
    nsteps = math.prod(grid)
    b_in_specs = [b.in_specs(grid) if hasattr(b, "in_specs") else [_ANY] * len(b.arrs) for b in bgs]
    b_out_specs = [b.out_specs(grid) if hasattr(b, "out_specs") else [_ANY] * len(b.out_shape) for b in bgs]
    aliases, i_off, o_off = {}, n_in, n_out
    for b in bgs:
        aliases.update({i_off + i: o_off + o for i, o in getattr(b, "aliases", {}).items()})
        i_off, o_off = i_off + len(b.arrs), o_off + len(b.out_shape)

    def full(*refs):
        pos = [0]

        def take(k):
            r = refs[pos[0]:pos[0] + k]
            pos[0] += k
            return r

        ins = take(n_in)
        b_ins = [take(len(b.arrs)) for b in bgs]
        outs = take(n_out)
        b_outs = [take(len(b.out_shape)) for b in bgs]
        sc = take(n_sc)
        b_sc = [take(len(b.scratch)) for b in bgs]
        if bgs:
            step = pl.program_id(0)
            for d in range(1, len(grid)):
                step = step * grid[d] + pl.program_id(d)

            @pl.when(step == 0)
            def _():
                for b, i_, o_, s_ in zip(bgs, b_ins, b_outs, b_sc):
                    b.start(i_, o_, s_)

        body(*ins, *outs, *sc)
        if bgs:
            for b, i_, o_, s_ in zip(bgs, b_ins, b_outs, b_sc):
                if hasattr(b, "step"):
                    b.step(i_, o_, s_)
                for at, fn in b.mid_steps(nsteps):
                    @pl.when(step == at)
                    def _():
                        fn(i_, o_, s_)

            @pl.when(step == nsteps - 1)
            def _():
                for b, i_, o_, s_ in zip(bgs, b_ins, b_outs, b_sc):
                    b.finish(i_, o_, s_)

    def run(*args):
        res = pl.pallas_call(
            full, name=name, grid=grid,
            in_specs=list(in_specs) + [s for l in b_in_specs for s in l],
            out_specs=out_specs_l + [s for l in b_out_specs for s in l],
            out_shape=out_shape_l + [s for b in bgs for s in b.out_shape],
            scratch_shapes=list(scratch_shapes) + [s for b in bgs for s in b.scratch],
            input_output_aliases=aliases,
            compiler_params=compiler_params,
        )(*args, *[a for b in bgs for a in b.arrs])
        rest = list(res[n_out:])
        for b in bgs:
            b.result, rest = rest[:len(b.out_shape)], rest[len(b.out_shape):]
        return res[0] if single else list(res[:n_out])

    return run


def s5_discretize(a_re, a_im, log_dt, b_re, b_im, c_re, c_im):
    lam_r = jnp.minimum(a_re, DT_MIN_LAMBDA)
    lam_i = a_im
    dt = jnp.exp(log_dt)[:, None]
    e = jnp.exp(lam_r * dt)
    lbr = e * jnp.cos(lam_i * dt)
    lbi = e * jnp.sin(lam_i * dt)
    den = lam_r * lam_r + lam_i * lam_i
    cf_r = ((lbr - 1.0) * lam_r + lbi * lam_i) / den
    cf_i = (lbi * lam_r - (lbr - 1.0) * lam_i) / den
    bb_r = cf_r[:, :, None] * b_re - cf_i[:, :, None] * b_im
    bb_i = cf_r[:, :, None] * b_im + cf_i[:, :, None] * b_re
    eye = jnp.eye(8, dtype=f32)

    def blk_b(m):
        return jnp.einsum('bgpc,gh->bgchp', m.reshape(8, 8, S5_P, S5_C), eye).reshape(8, 128, 512)

    def blk_c(m):
        return jnp.einsum('bgcp,gh->bgphc', m.reshape(8, 8, S5_C, S5_P), eye).reshape(8, 512, 128)

    bm = jnp.concatenate([blk_b(bb_r), blk_b(bb_i)], axis=-1)
    cm = jnp.concatenate([blk_c(c_re), -blk_c(c_im)], axis=1)
    lam = jnp.stack([lbr.reshape(8, 512), lbi.reshape(8, 512)], axis=1)
    lam = jnp.broadcast_to(lam[:, :, None, :], (8, 2, 8, 512))
    return lam, bm, cm


def _cmul(ar, ai, br, bi):
    return ar * br - ai * bi, ar * bi + ai * br


def _shift_rows(v, k, up):
    row = lax.broadcasted_iota(jnp.int32, v.shape, 0)
    if up:
        return jnp.where(row < 8 - k, pltpu.roll(v, 8 - k, 0), 0.0)
    return jnp.where(row >= k, pltpu.roll(v, k, 0), 0.0)


def _chunk_scan(S, lr, li, reverse, aux=None):
    z = jnp.zeros((8, 512), f32)
    U = 4

    def idx(i):
        return (S5_STEPS - 1 - i) if reverse else i

    def rows_of(s):
        return pl.ds(s * 8, 8) if isinstance(s, int) else pl.ds(pl.multiple_of(s * 8, 8), 8)

    def rec(xr, xi, row):
        br = S[row, 0:512]
        bi = S[row, 512:1024]
        return lr * xr - li * xi + br, lr * xi + li * xr + bi

    def step1(i, c):
        for u in range(U):
            c = rec(c[0], c[1], rows_of(idx(i * U + u)))
        return c

    er, ei = lax.fori_loop(0, S5_STEPS // U, step1, (z, z))
    ar, ai = lr, li
    for _ in range(8):
        ar, ai = _cmul(ar, ai, ar, ai)
    cr, ci = _shift_rows(er, 1, reverse), _shift_rows(ei, 1, reverse)
    for k in (1, 2, 4):
        sr, si = _shift_rows(cr, k, reverse), _shift_rows(ci, k, reverse)
        pr, pi_ = _cmul(ar, ai, sr, si)
        cr, ci = cr + pr, ci + pi_
        ar, ai = _cmul(ar, ai, ar, ai)

    if aux is None:
        def step2(i, c):
            for u in range(U):
                row = rows_of(idx(i * U + u))
                c = rec(c[0], c[1], row)
                S[row, 0:512] = c[0]
                S[row, 512:1024] = c[1]
            return c

        lax.fori_loop(0, S5_STEPS // U, step2, (cr, ci))
        return None

    def one(s, c):
        gr0, gi0, dr, di = c
        row = rows_of(s)
        gr, gi = rec(gr0, gi0, row)
        S[row, 0:512] = gr
        S[row, 512:1024] = gi
        prow = rows_of(s - 1)
        xr = aux[prow, 0:512]
        xi = aux[prow, 512:1024]
        return gr, gi, dr + gr * xr + gi * xi, di + gi * xr - gr * xi

    def step2(i, c):
        for u in range(U):
            c = one(S5_STEPS - 1 - (i * U + u), c)
        return c

    c = lax.fori_loop(0, S5_STEPS // U - 1, step2, (cr, ci, z, z))
    for s in range(U - 1, 0, -1):
        c = one(s, c)
    gr, gi, dr, di = c
    row0 = pl.ds(0, 8)
    gr, gi = rec(gr, gi, row0)
    S[row0, 0:512] = gr
    S[row0, 512:1024] = gi
    last = pl.ds((S5_STEPS - 1) * 8, 8)
    xr = _shift_rows(aux[last, 0:512], 1, False)
    xi = _shift_rows(aux[last, 512:1024], 1, False)
    dr = dr + gr * xr + gi * xi
    di = di + gi * xr - gr * xi
    return dr, di


_ROWS = 256


def _row_loop(fn):
    def body(r, c):
        fn(pl.ds(pl.multiple_of(r * _ROWS, _ROWS), _ROWS))
        return c
    lax.fori_loop(0, T // _ROWS, body, 0)


def s5_core_fwd(hn, bm, lam, cm, bg=()):
    def body(u_ref, b_ref, lam_ref, c_ref, ys_ref, S):
        def bu(rows):
            S[rows, :] = _dot(u_ref[rows, :], b_ref[...])
        _row_loop(bu)
        _chunk_scan(S, lam_ref[0], lam_ref[1], False)

        def ys(rows):
            ys_ref[rows, :] = _dot(S[rows, :].astype(bf16), c_ref[...])
        _row_loop(ys)

    return _call(
        bg, body, name="s5_core_fwd", grid=(S5_SUB,),
        in_specs=[pl.BlockSpec((T, 128), lambda b: (0, b)),
                  pl.BlockSpec((None, 128, 1024), lambda b: (b, 0, 0)),
                  pl.BlockSpec((None, 4, 8, 512), lambda b: (b, 0, 0, 0)),
                  pl.BlockSpec((None, 1024, 128), lambda b: (b, 0, 0))],
        out_specs=pl.BlockSpec((T, 128), lambda b: (0, b)),
        out_shape=SDS((T, D), f32),
        scratch_shapes=[pltpu.VMEM((T, 1024), f32)],
        compiler_params=_cp(dimension_semantics=("arbitrary",)),
    )(hn, bm, lam, cm)


_SEG = _ROWS // S5_CH


def _scan_tile(S, lr, li, k, carry, reverse, store, aux=None):
    steps = range(k * _SEG, (k + 1) * _SEG)
    for s in (reversed(steps) if reverse else steps):
        row = pl.ds(s * 8, 8)
        xr, xi = carry[0], carry[1]
        nr = lr * xr - li * xi + S[row, 0:512]
        ni = lr * xi + li * xr + S[row, 512:1024]
        if store:
            S[row, 0:512] = nr
            S[row, 512:1024] = ni
        if aux is not None and s >= 1:
            prow = pl.ds((s - 1) * 8, 8)
            pr, pi_ = aux[prow, 0:512], aux[prow, 512:1024]
            carry = (nr, ni, carry[2] + nr * pr + ni * pi_, carry[3] + ni * pr - nr * pi_)
        elif aux is not None:
            carry = (nr, ni, carry[2], carry[3])
        else:
            carry = (nr, ni)
    return carry


def _chunk_starts(er, ei, lr, li, reverse):
    ar, ai = lr, li
    for _ in range(8):
        ar, ai = _cmul(ar, ai, ar, ai)
    cr, ci = _shift_rows(er, 1, reverse), _shift_rows(ei, 1, reverse)
    for k in (1, 2, 4):
        sr, si = _shift_rows(cr, k, reverse), _shift_rows(ci, k, reverse)
        pr, pi_ = _cmul(ar, ai, sr, si)
        cr, ci = cr + pr, ci + pi_
        ar, ai = _cmul(ar, ai, ar, ai)
    return cr, ci


def s5_core_bwd(hn, dy, bm, lam, cm, bg=()):
    nt = T // _ROWS

    def body(u_ref, dy_ref, b_ref, lam_ref, c_ref, du_ref, db_ref, dct_ref, dlam_ref, S1, S2):
        lr, li, lcr, lci = lam_ref[0], lam_ref[1], lam_ref[2], lam_ref[3]
        z = jnp.zeros((8, 512), f32)
        tile = lambda k: pl.ds(k * _ROWS, _ROWS)
        dyb = lambda k: dy_ref[tile(k), :].astype(bf16)

        c = (z, z)
        for k in range(nt):
            S1[tile(k), :] = _dot(u_ref[tile(k), :], b_ref[...])
            if k >= 1:
                c = _scan_tile(S1, lr, li, k - 1, c, False, False)
        c = _scan_tile(S1, lr, li, nt - 1, c, False, False)

        c = _chunk_starts(c[0], c[1], lr, li, False)
        dct_ref[...] = jnp.zeros_like(dct_ref)
        for k in range(nt):
            c = _scan_tile(S1, lr, li, k, c, False, True)
            if k >= 1:
                dct_ref[...] += _dot_tn(dyb(k - 1), S1[tile(k - 1), :].astype(bf16))
        dct_ref[...] += _dot_tn(dyb(nt - 1), S1[tile(nt - 1), :].astype(bf16))

        S2[tile(nt - 1), :] = _dot_nt(dyb(nt - 1), c_ref[...])
        c = (z, z)
        for k in range(nt - 1, -1, -1):
            if k >= 1:
                S2[tile(k - 1), :] = _dot_nt(dyb(k - 1), c_ref[...])
            c = _scan_tile(S2, lcr, lci, k, c, True, False)

        def dbu(k):
            gb = S2[tile(k), :].astype(bf16)
            db_ref[...] += _dot_tn(u_ref[tile(k), :], gb)
            du_ref[tile(k), :] = _dot_nt(gb, b_ref[...])

        c = _chunk_starts(c[0], c[1], lcr, lci, True) + (z, z)
        db_ref[...] = jnp.zeros_like(db_ref)
        for k in range(nt - 1, -1, -1):
            c = _scan_tile(S2, lcr, lci, k, c, True, True, aux=S1)
            if k + 1 < nt:
                dbu(k + 1)
        dbu(0)
        gr, gi, dr, di = c
        last = pl.ds((S5_STEPS - 1) * 8, 8)
        xr = _shift_rows(S1[last, 0:512], 1, False)
        xi = _shift_rows(S1[last, 512:1024], 1, False)
        dlam_ref[0] = dr + gr * xr + gi * xi
        dlam_ref[1] = di + gi * xr - gr * xi

    return _call(
        bg, body, name="s5_core_bwd", grid=(S5_SUB,),
        in_specs=[pl.BlockSpec((T, 128), lambda b: (0, b)),
                  pl.BlockSpec((T, 128), lambda b: (0, b)),
                  pl.BlockSpec((None, 128, 1024), lambda b: (b, 0, 0)),
                  pl.BlockSpec((None, 4, 8, 512), lambda b: (b, 0, 0, 0)),
                  pl.BlockSpec((None, 1024, 128), lambda b: (b, 0, 0))],
        out_specs=[pl.BlockSpec((T, 128), lambda b: (0, b)),
                   pl.BlockSpec((None, 128, 1024), lambda b: (b, 0, 0)),
                   pl.BlockSpec((None, 128, 1024), lambda b: (b, 0, 0)),
                   pl.BlockSpec((None, 2, 8, 512), lambda b: (b, 0, 0, 0))],
        out_shape=[SDS((T, D), f32), SDS((8, 128, 1024), f32), SDS((8, 128, 1024), f32), SDS((8, 2, 8, 512), f32)],
        scratch_shapes=[pltpu.VMEM((T, 1024), f32), pltpu.VMEM((T, 1024), f32)],
        compiler_params=_cp(dimension_semantics=("arbitrary",)),
    )(hn, dy, bm, lam, cm)


TM = 512
NT = T // TM


def _tile(n=D):
    return pl.BlockSpec((TM, n), lambda i: (i, 0))


def s5_pre(xp, g):
    def body(x_ref, g_ref, hn_ref):
        hn, _ = _rms(x_ref[...], g_ref[...])
        hn_ref[...] = hn.astype(bf16)

    return pl.pallas_call(
        body, name="s5_pre", grid=(NT,), in_specs=[_tile(), _full((1, D))], out_specs=_tile(),
        out_shape=SDS((T, D), bf16), compiler_params=_cp(dimension_semantics=("arbitrary",)),
    )(xp, g)


def _gelu_grad(y):
    c = math.sqrt(2.0 / math.pi)
    t = jnp.tanh(c * (y + 0.044715 * y * y * y))
    return 0.5 * (1.0 + t) + 0.5 * y * (1.0 - t * t) * c * (1.0 + 3.0 * 0.044715 * y * y)


def s5_post(ys, xp, g, d, wglu, bglu, bg=()):
    def body(ys_ref, x_ref, g_ref, d_ref, w_ref, b_ref, y_ref, z_ref, h_ref):
        x = x_ref[...]
        hn, _ = _rms(x, g_ref[...])
        y = ys_ref[...] + d_ref[...] * hn
        y_ref[...] = y
        yg = jax.nn.gelu(y).astype(bf16)
        for j in range(4):
            cv = slice(j * 256, (j + 1) * 256)
            cg = slice(1024 + j * 256, 1024 + (j + 1) * 256)
            val = _dot(yg, w_ref[j]) + b_ref[:, cv]
            gate = _dot(yg, w_ref[j + 4]) + b_ref[:, cg]
            z_ref[:, cv] = val
            z_ref[:, cg] = gate
            h_ref[:, cv] = x[:, cv] + val * jax.nn.sigmoid(gate)

    return _call(
        bg, body, name="s5_post", grid=(NT,),
        in_specs=[_tile(), _tile(), _full((1, D)), _full((1, D)), _full((8, D, 256)), _full((1, 2 * D))],
        out_specs=[_tile(), _tile(2 * D), _tile()],
        out_shape=[SDS((T, D), f32), SDS((T, 2 * D), f32), SDS((T, D), f32)],
        compiler_params=_cp(dimension_semantics=("arbitrary",)),
    )(ys, xp, g, d, wglu, bglu)


def s5_post_bwd(dh, y, z, wglu, bg=()):
    def body(dh_ref, y_ref, z_ref, w_ref, dy_ref, dw_ref, db_ref, acc):
        i = pl.program_id(0)

        @pl.when(i == 0)
        def _():
            acc[...] = jnp.zeros_like(acc)
            db_ref[...] = jnp.zeros_like(db_ref)

        dh_ = dh_ref[...]
        y = y_ref[...]
        yg = jax.nn.gelu(y).astype(bf16)
        dyg = jnp.zeros((TM, D), f32)
        for j in range(4):
            cv = slice(j * 256, (j + 1) * 256)
            cg = slice(1024 + j * 256, 1024 + (j + 1) * 256)
            val = z_ref[:, cv]
            sg = jax.nn.sigmoid(z_ref[:, cg])
            dval = dh_[:, cv] * sg
            dgate = dh_[:, cv] * val * sg * (1.0 - sg)
            db_ref[:, cv] += _colsum8(dval)
            db_ref[:, cg] += _colsum8(dgate)
            dvb = dval.astype(bf16)
            dgb = dgate.astype(bf16)
            acc[j] += _dot_tn(yg, dvb)
            acc[j + 4] += _dot_tn(yg, dgb)
            dyg = dyg + _dot_nt(dvb, w_ref[j]) + _dot_nt(dgb, w_ref[j + 4])
        dy_ref[...] = dyg * _gelu_grad(y)

        @pl.when(i == NT - 1)
        def _():
            dw_ref[...] = acc[...].astype(bf16)

    return _call(
        bg, body, name="s5_post_bwd", grid=(NT,),
        in_specs=[_tile(), _tile(), _tile(2 * D), _full((8, D, 256))],
        out_specs=[_tile(), _full((8, D, 256)), _full((8, 2 * D))],
        out_shape=[SDS((T, D), f32), SDS((8, D, 256), bf16), SDS((8, 2 * D), f32)],
        scratch_shapes=[pltpu.VMEM((8, D, 256), f32)],
        compiler_params=_cp(dimension_semantics=("arbitrary",)),
    )(dh, y, z, wglu)


def s5_pre_bwd(xp, g, du, dy, d, dh, bg=()):
    def body(x_ref, g_ref, du_ref, dy_ref, d_ref, dh_ref, dx_ref, dg_ref, dd_ref):
        i = pl.program_id(0)

        @pl.when(i == 0)
        def _():
            dg_ref[...] = jnp.zeros_like(dg_ref)
            dd_ref[...] = jnp.zeros_like(dd_ref)

        x = x_ref[...]
        g = g_ref[...]
        dy = dy_ref[...]
        hn, _ = _rms(x, g)
        dhn = du_ref[...] + d_ref[...] * dy
        dx, dgt = _rms_bwd(x, g, dhn)
        dx_ref[...] = dh_ref[...] + dx
        dg_ref[...] += _colsum8(dgt)
        dd_ref[...] += _colsum8(dy * hn)

    return _call(
        bg, body, name="s5_pre_bwd", grid=(NT,),
        in_specs=[_tile(), _full((1, D)), _tile(), _tile(), _full((1, D)), _tile()],
        out_specs=[_tile(), _full((8, D)), _full((8, D))],
        out_shape=[SDS((T, D), f32), SDS((8, D), f32), SDS((8, D), f32)],
        compiler_params=_cp(dimension_semantics=("arbitrary",)),
    )(xp, g, du, dy, d, dh)


TMF = 1024


def mlp_fwd(h, g, w_in, w_out, layer, bg=()):
    def body(h_ref, g_ref, wi_ref, wo_ref, hm_ref, r_ref, out_ref, acc):
        j = pl.program_id(1)

        @pl.when(j == 0)
        def _():
            hm, _ = _rms(h_ref[...], g_ref[...])
            hm_ref[...] = hm.astype(bf16)
            acc[...] = jnp.zeros_like(acc)

        a = jnp.maximum(_dot(hm_ref[...], wi_ref[...]), 0.0)
        r_ref[...] = a.astype(bf16)
        acc[...] += _dot((a * a).astype(bf16), wo_ref[...])

        @pl.when(j == NDEV - 1)
        def _():
            out_ref[...] = h_ref[...] + acc[...]

    return _call(
        bg, body, name=f"mlp_fwd{layer}", grid=(T // TMF, NDEV),
        in_specs=[pl.BlockSpec((TMF, D), lambda i, j: (i, 0)),
                  pl.BlockSpec((1, D), lambda i, j: (0, 0)),
                  pl.BlockSpec((None, D, D_FF_SHARD), lambda i, j: (j, 0, 0)),
                  pl.BlockSpec((None, D_FF_SHARD, D), lambda i, j: (j, 0, 0))],
        out_specs=[pl.BlockSpec((TMF, D), lambda i, j: (i, 0)), pl.BlockSpec((TMF, D_FF_SHARD), lambda i, j: (i, j)),
                   pl.BlockSpec((TMF, D), lambda i, j: (i, 0))],
        out_shape=[SDS((T, D), bf16), SDS((T, NDEV * D_FF_SHARD), bf16), SDS((T, D), f32)],
        scratch_shapes=[pltpu.VMEM((TMF, D), f32)],
        compiler_params=_cp(dimension_semantics=("arbitrary", "arbitrary")),
    )(h, g, w_in, w_out)


def mlp_bwd(h, hm, r, g, dout, dout_b, w_in, w_out, layer, bg=()):
    last = NDEV - 1

    def body(h_ref, hm_ref, r_ref, g_ref, do_ref, dob_ref, wi_ref, wo_ref, dh_ref, dwi_ref, dwo_ref, dg_ref,
             dhm, awi, awo):
        j = pl.program_id(0)
        i = pl.program_id(1)
        rows = pl.ds(pl.multiple_of(i * TM, TM), TM)

        @pl.when(i == 0)
        def _():
            awi[...] = jnp.zeros_like(awi)
            awo[...] = jnp.zeros_like(awo)

        hm_ = hm_ref[...]
        dob = dob_ref[...]
        r = r_ref[...].astype(f32)
        dz = (_dot_nt(dob, wo_ref[...]) * (2.0 * r)).astype(bf16)
        awo[...] += _dot_tn((r * r).astype(bf16), dob)
        awi[...] += _dot_tn(hm_, dz)
        part = _dot_nt(dz, wi_ref[...])

        @pl.when(j == 0)
        def _():
            dhm[rows, :] = part

        @pl.when(j > 0)
        def _():
            dhm[rows, :] += part

        @pl.when(i == NT - 1)
        def _():
            dwi_ref[...] = awi[...].astype(bf16)
            dwo_ref[...] = awo[...].astype(bf16)

        @pl.when(j == last)
        def _():
            @pl.when(i == 0)
            def _():
                dg_ref[...] = jnp.zeros_like(dg_ref)
            dx, dgt = _rms_bwd(h_ref[...], g_ref[...], dhm[rows, :])
            dh_ref[...] = do_ref[...] + dx
            dg_ref[...] += _colsum8(dgt)

    late = lambda j, i: (jnp.where(j == last, i, 0), 0)
    return _call(
        bg, body, name=f"mlp_bwd{layer}", grid=(NDEV, NT),
        in_specs=[pl.BlockSpec((TM, D), late),
                  pl.BlockSpec((TM, D), lambda j, i: (i, 0)),
                  pl.BlockSpec((TM, D_FF_SHARD), lambda j, i: (i, j)),
                  pl.BlockSpec((1, D), lambda j, i: (0, 0)),
                  pl.BlockSpec((TM, D), late),
                  pl.BlockSpec((TM, D), lambda j, i: (i, 0)),
                  pl.BlockSpec((None, D, D_FF_SHARD), lambda j, i: (j, 0, 0)),
                  pl.BlockSpec((None, D_FF_SHARD, D), lambda j, i: (j, 0, 0))],
        out_specs=[pl.BlockSpec((TM, D), late),
                   pl.BlockSpec((None, D, D_FF_SHARD), lambda j, i: (j, 0, 0)),
                   pl.BlockSpec((None, D_FF_SHARD, D), lambda j, i: (j, 0, 0)),
                   pl.BlockSpec((8, D), lambda j, i: (0, 0))],
        out_shape=[SDS((T, D), f32), SDS((NDEV, D, D_FF_SHARD), bf16), SDS((NDEV, D_FF_SHARD, D), bf16),
                   SDS((8, D), f32)],
        scratch_shapes=[pltpu.VMEM((T, D), f32), pltpu.VMEM((D, D_FF_SHARD), f32), pltpu.VMEM((D_FF_SHARD, D), f32)],
        compiler_params=_cp(dimension_semantics=("arbitrary", "arbitrary")),
    )(h, hm, r, g, dout, dout_b, w_in, w_out)


def _spread4():
    r = lax.broadcasted_iota(jnp.int32, (256, D), 0)
    c = lax.broadcasted_iota(jnp.int32, (256, D), 1)
    return ((c // 256 == r // HEAD_DIM) & (c % HEAD_DIM == r % HEAD_DIM)).astype(bf16)


def attn_pre(h, g_kv, g_mix, wkv, bkv, spread, wq, bq):
    def body(h_ref, gkv_ref, gm_ref, wkv_ref, bkv_ref, sp_ref, wq_ref, bq_ref, kvn_ref, hn_ref, k_ref, v_ref, q_ref):
        h_ = h_ref[...]
        kvn = _rms(h_, gkv_ref[...])[0].astype(bf16)
        hn = _rms(h_, gm_ref[...])[0].astype(bf16)
        kvn_ref[...] = kvn
        hn_ref[...] = hn
        kv = (_dot(kvn, wkv_ref[...]) + bkv_ref[...]).astype(bf16)
        k_ref[...] = _dot(kv[:, :256], sp_ref[...]).astype(bf16)
        v_ref[...] = _dot(kv[:, 256:], sp_ref[...]).astype(bf16)
        q_ref[...] = (_dot(hn, wq_ref[...]) + bq_ref[...]).astype(bf16)

    return pl.pallas_call(
        body, name="attn_pre", grid=(NT,),
        in_specs=[_tile(), _full((1, D)), _full((1, D)), _full((D, 512)), _full((1, 512)), _full((256, D)),
                  _full((D, D)), _full((1, D))],
        out_specs=[_tile()] * 5,
        out_shape=[SDS((T, D), bf16)] * 5,
        compiler_params=_cp(dimension_semantics=("arbitrary",)),
    )(h, g_kv, g_mix, wkv, bkv, spread, wq, bq)


def _attn_specs():
    cur = pl.BlockSpec((TM, 256), lambda j, n: (n, j))
    prev = pl.BlockSpec((BLK, 256), lambda j, n: (jnp.maximum(n * (TM // BLK) - 1, 0), j))
    return cur, prev


def _head_mask(g):
    lane = lax.broadcasted_iota(jnp.int32, (1, 256), 1)
    return (lane >= g * HEAD_DIM) & (lane < (g + 1) * HEAD_DIM)


def _stack_heads(t):
    return jnp.concatenate([jnp.where(_head_mask(g), t, 0) for g in range(Q_PER_KV)], axis=0)


def _unstack_heads(t):
    out = jnp.where(_head_mask(0), t[0:BLK], 0.0)
    for g in range(1, Q_PER_KV):
        out = out + jnp.where(_head_mask(g), t[g * BLK:(g + 1) * BLK], 0.0)
    return out


def _attn_probs(qs, k2, sinks, first):
    rows = Q_PER_KV * BLK
    s = _dot_nt(qs, k2) * (1.0 / math.sqrt(HEAD_DIM))
    qi = jnp.bitwise_and(lax.broadcasted_iota(jnp.int32, (rows, 2 * BLK), 0), BLK - 1)
    kj = lax.broadcasted_iota(jnp.int32, (rows, 2 * BLK), 1)
    diff = qi + BLK - kj
    valid = (diff >= 0) & (diff < BLK) & (jnp.logical_not(first) | (kj >= BLK))
    s = jnp.where(valid, s, -jnp.inf)
    rb = lax.broadcasted_iota(jnp.int32, (rows, 1), 0)
    sink = jnp.where(rb < BLK, sinks[0], jnp.where(rb < 2 * BLK, sinks[1], jnp.where(rb < 3 * BLK, sinks[2], sinks[3])))
    m = jnp.maximum(jnp.max(s, axis=-1, keepdims=True), sink)
    p = jnp.exp(s - m)
    ps = jnp.exp(sink - m)
    denom = jnp.sum(p, axis=-1, keepdims=True) + ps
    return p / denom, ps / denom


def attn_core_fwd(q, k4, v4, sinks, bg=()):
    nb = TM // BLK

    def body(sink_ref, q_ref, kc_ref, kp_ref, vc_ref, vp_ref, o_ref):
        j = pl.program_id(0)
        n = pl.program_id(1)
        sk = [sink_ref[j * Q_PER_KV + g] for g in range(Q_PER_KV)]
        for b in range(nb):
            qb = q_ref[b * BLK:(b + 1) * BLK, :]
            if b == 0:
                k2 = jnp.concatenate([kp_ref[...], kc_ref[0:BLK, :]], axis=0)
                v2 = jnp.concatenate([vp_ref[...], vc_ref[0:BLK, :]], axis=0)
                first = n == 0
            else:
                k2 = kc_ref[(b - 1) * BLK:(b + 1) * BLK, :]
                v2 = vc_ref[(b - 1) * BLK:(b + 1) * BLK, :]
                first = False
            a, _ = _attn_probs(_stack_heads(qb), k2, sk, first)
            o_ref[b * BLK:(b + 1) * BLK, :] = _unstack_heads(_dot(a.astype(bf16), v2)).astype(bf16)

    cur, prev = _attn_specs()
    return _call(
        bg, body, name="attn_core_fwd", grid=(N_KV, NT),
        in_specs=[pl.BlockSpec(memory_space=pltpu.SMEM), cur, cur, prev, cur, prev],
        out_specs=cur, out_shape=SDS((T, D), bf16),
        compiler_params=_cp(dimension_semantics=("arbitrary", "arbitrary")),
    )(sinks, q, k4, k4, v4, v4)


def attn_post(h, o, wo, bo):
    def body(h_ref, o_ref, w_ref, b_ref, out_ref):
        out_ref[...] = h_ref[...] + _dot(o_ref[...], w_ref[...]) + b_ref[...]

    return pl.pallas_call(
        body, name="attn_post", grid=(NT,), in_specs=[_tile(), _tile(), _full((D, D)), _full((1, D))],
        out_specs=_tile(), out_shape=SDS((T, D), f32), compiler_params=_cp(dimension_semantics=("arbitrary",)),
    )(h, o, wo, bo)


def attn_bwd_pre(dh, o, wo, bg=()):
    def body(dh_ref, o_ref, w_ref, do_ref, dw_ref, db_ref, acc):
        i = pl.program_id(0)

        @pl.when(i == 0)
        def _():
            acc[...] = jnp.zeros_like(acc)
            db_ref[...] = jnp.zeros_like(db_ref)

        dh_ = dh_ref[...]
        dhb = dh_.astype(bf16)
        do_ref[...] = _dot_nt(dhb, w_ref[...]).astype(bf16)
        acc[...] += _dot_tn(o_ref[...], dhb)
        db_ref[...] += _colsum8(dh_)

        @pl.when(i == NT - 1)
        def _():
            dw_ref[...] = acc[...].astype(bf16)

    return _call(
        bg, body, name="attn_bwd_pre", grid=(NT,), in_specs=[_tile(), _tile(), _full((D, D))],
        out_specs=[_tile(), _full((D, D)), _full((8, D))],
        out_shape=[SDS((T, D), bf16), SDS((D, D), bf16), SDS((8, D), f32)],
        scratch_shapes=[pltpu.VMEM((D, D), f32)],
        compiler_params=_cp(dimension_semantics=("arbitrary",)),
    )(dh, o, wo)


def attn_core_bwd(q, do, k4, v4, sinks, bg=()):
    nb = TM // BLK

    def body(sink_ref, q_ref, do_ref, kc_ref, kp_ref, vc_ref, vp_ref, dq_ref, dk_ref, dv_ref, ds_ref):
        j = pl.program_id(0)
        n = pl.program_id(1)

        @pl.when(n == 0)
        def _():
            dk_ref[...] = jnp.zeros_like(dk_ref)
            dv_ref[...] = jnp.zeros_like(dv_ref)
            ds_ref[...] = jnp.zeros_like(ds_ref)

        lane8 = lax.broadcasted_iota(jnp.int32, (8, 128), 1)
        row8 = lax.broadcasted_iota(jnp.int32, (8, 128), 0)
        sk = [sink_ref[j * Q_PER_KV + g] for g in range(Q_PER_KV)]
        for b in range(nb):
            qs = _stack_heads(q_ref[b * BLK:(b + 1) * BLK, :])
            dos = _stack_heads(do_ref[b * BLK:(b + 1) * BLK, :])
            if b == 0:
                k2 = jnp.concatenate([kp_ref[...], kc_ref[0:BLK, :]], axis=0)
                v2 = jnp.concatenate([vp_ref[...], vc_ref[0:BLK, :]], axis=0)
                first = n == 0
            else:
                k2 = kc_ref[(b - 1) * BLK:(b + 1) * BLK, :]
                v2 = vc_ref[(b - 1) * BLK:(b + 1) * BLK, :]
                first = False
            a, asink = _attn_probs(qs, k2, sk, first)
            dp = _dot_nt(dos, v2)
            dd = jnp.sum(a * dp, axis=-1, keepdims=True)
            dsc = (a * (dp - dd) * (1.0 / math.sqrt(HEAD_DIM))).astype(bf16)
            t = asink * dd
            for g in range(Q_PER_KV):
                dsink = -jnp.sum(t[g * BLK:(g + 1) * BLK], axis=0, keepdims=True)
                ds_ref[...] += jnp.where((lane8 == g) & (row8 == 0), jnp.broadcast_to(dsink, (8, 128)), 0.0)
            dq_ref[b * BLK:(b + 1) * BLK, :] = _unstack_heads(_dot(dsc, k2))
            dk2 = _dot_tn(dsc, qs)
            dv2 = _dot_tn(a.astype(bf16), dos)
            cur = pl.ds(pl.multiple_of(n * TM + b * BLK, BLK), BLK)
            dk_ref[cur, :] += dk2[BLK:, :]
            dv_ref[cur, :] += dv2[BLK:, :]
            if b == 0:
                @pl.when(n > 0)
                def _():
                    prv = pl.ds(pl.multiple_of(n * TM - BLK, BLK), BLK)
                    dk_ref[prv, :] += dk2[:BLK, :]
                    dv_ref[prv, :] += dv2[:BLK, :]
            else:
                prv = pl.ds(pl.multiple_of(n * TM + (b - 1) * BLK, BLK), BLK)
                dk_ref[prv, :] += dk2[:BLK, :]
                dv_ref[prv, :] += dv2[:BLK, :]

    cur, prev = _attn_specs()
    col = pl.BlockSpec((T, 256), lambda j, n: (0, j))
    return _call(
        bg, body, name="attn_core_bwd", grid=(N_KV, NT),
        in_specs=[pl.BlockSpec(memory_space=pltpu.SMEM), cur, cur, cur, prev, cur, prev],
        out_specs=[cur, col, col, pl.BlockSpec((None, 8, 128), lambda j, n: (j, 0, 0))],
        out_shape=[SDS((T, D), f32), SDS((T, D), f32), SDS((T, D), f32), SDS((N_KV, 8, 128), f32)],
        compiler_params=_cp(dimension_semantics=("arbitrary", "arbitrary")),
    )(sinks, q, do, k4, k4, v4, v4)


def attn_bwd_q(h, dh, dq, hn, g_mix, wq):
    def body(h_ref, dh_ref, dq_ref, hn_ref, gm_ref, wq_ref, out_ref, dwq_ref, dbq_ref, dgm_ref, aq):
        i = pl.program_id(0)

        @pl.when(i == 0)
        def _():
            aq[...] = jnp.zeros_like(aq)
            dbq_ref[...] = jnp.zeros_like(dbq_ref)
            dgm_ref[...] = jnp.zeros_like(dgm_ref)

        dq_ = dq_ref[...]
        dqb = dq_.astype(bf16)
        aq[...] += _dot_tn(hn_ref[...], dqb)
        dbq_ref[...] += _colsum8(dq_)
        dx, dg = _rms_bwd(h_ref[...], gm_ref[...], _dot_nt(dqb, wq_ref[...]))
        out_ref[...] = dh_ref[...] + dx
        dgm_ref[...] += _colsum8(dg)

        @pl.when(i == NT - 1)
        def _():
            dwq_ref[...] = aq[...].astype(bf16)

    vec = _full((8, D))
    mat = _full((D, D))
    return pl.pallas_call(
        body, name="attn_bwd_q", grid=(NT,),
        in_specs=[_tile()] * 4 + [_full((1, D)), mat],
        out_specs=[_tile(), mat, vec, vec],
        out_shape=[SDS((T, D), f32), SDS((D, D), bf16), SDS((8, D), f32), SDS((8, D), f32)],
        scratch_shapes=[pltpu.VMEM((D, D), f32)],
        compiler_params=_cp(dimension_semantics=("arbitrary",)),
    )(h, dh, dq, hn, g_mix, wq)


def attn_bwd_kv(h, dh, dk4, dv4, kvn, g_kv, wkv, spread):
    def body(h_ref, dh_ref, dk_ref, dv_ref, kvn_ref, gkv_ref, wkv_ref, sp_ref, out_ref, outb_ref, dw_ref, db_ref,
             dgkv_ref, acc):
        i = pl.program_id(0)

        @pl.when(i == 0)
        def _():
            for r in (acc, db_ref, dgkv_ref):
                r[...] = jnp.zeros_like(r)

        dkv = jnp.concatenate([_dot_nt(dk_ref[...].astype(bf16), sp_ref[...]),
                               _dot_nt(dv_ref[...].astype(bf16), sp_ref[...])], axis=1)
        dkvb = dkv.astype(bf16)
        acc[...] += _dot_tn(kvn_ref[...], dkvb)
        db_ref[...] += _colsum8(dkv)
        dx, dg = _rms_bwd(h_ref[...], gkv_ref[...], _dot_nt(dkvb, wkv_ref[...]))
        out = dh_ref[...] + dx
        out_ref[...] = out
        outb_ref[...] = out.astype(bf16)
        dgkv_ref[...] += _colsum8(dg)

        @pl.when(i == NT - 1)
        def _():
            dw_ref[...] = acc[...].astype(bf16)

    return pl.pallas_call(
        body, name="attn_bwd_kv", grid=(NT,),
        in_specs=[_tile()] * 5 + [_full((1, D)), _full((D, 512)), _full((256, D))],
        out_specs=[_tile(), _tile(), _full((D, 512)), _full((8, 512)), _full((8, D))],
        out_shape=[SDS((T, D), f32), SDS((T, D), bf16), SDS((D, 512), bf16), SDS((8, 512), f32), SDS((8, D), f32)],
        scratch_shapes=[pltpu.VMEM((D, 512), f32)],
        compiler_params=_cp(dimension_semantics=("arbitrary",)),
    )(h, dh, dk4, dv4, kvn, g_kv, wkv, spread)


def final_loss(h, g, target):
    def body(h_ref, g_ref, t_ref, loss_ref, dh_ref, dhb_ref, dg_ref):
        i = pl.program_id(0)

        @pl.when(i == 0)
        def _():
            loss_ref[...] = jnp.zeros_like(loss_ref)
            dg_ref[...] = jnp.zeros_like(dg_ref)

        h_ = h_ref[...]
        g_ = g_ref[...]
        y, _ = _rms(h_, g_)
        diff = y - t_ref[...]
        per_tok = jnp.mean(diff * diff, axis=-1, keepdims=True)
        tot = 0.5 * jnp.sum(per_tok, axis=0, keepdims=True)
        lane = lax.broadcasted_iota(jnp.int32, (8, 128), 1)
        row = lax.broadcasted_iota(jnp.int32, (8, 128), 0)
        loss_ref[...] += jnp.where((lane == 0) & (row == 0), jnp.broadcast_to(tot, (8, 128)), 0.0)
        dx, dgt = _rms_bwd(h_, g_, diff * (1.0 / D))
        dh_ref[...] = dx
        dhb_ref[...] = dx.astype(bf16)
        dg_ref[...] += _colsum8(dgt)

    return pl.pallas_call(
        body, name="final_loss", grid=(NT,), in_specs=[_tile(), _full((1, D)), _tile()],
        out_specs=[_full((8, 128)), _tile(), _tile(), _full((8, D))],
        out_shape=[SDS((8, 128), f32), SDS((T, D), f32), SDS((T, D), bf16), SDS((8, D), f32)],
        compiler_params=_cp(dimension_semantics=("arbitrary",)),
    )(h, g, target)


def _to_chunked(a):
    return a.reshape(S5_CH, S5_STEPS, a.shape[-1]).transpose(1, 0, 2).reshape(T, a.shape[-1])


def _from_chunked(a):
    return a.reshape(S5_STEPS, S5_CH, a.shape[-1]).transpose(1, 0, 2).reshape(T, a.shape[-1])


def _rep4(w):
    return jnp.broadcast_to(w.reshape(w.shape[0], N_KV, 1, HEAD_DIM), (w.shape[0], N_KV, Q_PER_KV, HEAD_DIM)).reshape(
        w.shape[0], N_KV * Q_PER_KV * HEAD_DIM)


def _fold4(w):
    return w.reshape(w.shape[0], N_KV, Q_PER_KV, HEAD_DIM).sum(axis=2).reshape(w.shape[0], N_KV * HEAD_DIM)


def fwd_bwd(x, target, p, shards, opt, core, chip):
    row = lambda v: v.reshape(1, -1)
    (lam, bm, cm), prep_vjp = jax.vjp(s5_discretize, p["s5_a_re"][0], p["s5_a_im"][0], p["s5_log_dt"][0],
                                      p["s5_b_re"][0], p["s5_b_im"][0], p["s5_c_re"][0], p["s5_c_im"][0])
    bmb, cmb = bm.astype(bf16), cm.astype(bf16)
    lam = jnp.concatenate([lam, lam * jnp.array([1.0, -1.0], f32).reshape(1, 2, 1, 1)], axis=1)
    g_mix0, g_mix1 = row(p["norm_mix"][0]), row(p["norm_mix"][1])
    g_mlp0, g_mlp1 = row(p["norm_mlp"][0]), row(p["norm_mlp"][1])
    g_kv, g_fin = row(p["norm_kv"]), row(p["norm_final"])
    bq, bo = p["b_q"], p["b_o"]
    bkv = row(p["b_kv"])
    spread = _spread4()
    sinks = p["sinks"].reshape(16)

    def reduce_pairs(names, bg):
        return [add_pairs(g, r, core, f"add_pairs_{n}") for n, g, r in zip(names, bg.arrs, bg.result)]

    xp = _to_chunked(x)
    hn0 = s5_pre(xp, g_mix0)
    ga = BgGather([shards["s5_w_glu"], shards["vecs"], shards["w_in0"], shards["w_out0"]], mids=(0.75, 1.0))
    ys = s5_core_fwd(hn0, bmb, lam, cmb, bg=[ga])
    wglu, gvec, win0, wout0 = ga.result
    d_skip = gvec[:, 0, :128].reshape(1, D)
    bglu = gvec[:, 0, 128:].reshape(1, 2 * D)
    y, z, h1 = s5_post(ys, xp, g_mix0, d_skip, wglu, bglu)
    gc = BgGather([shards["w_kv"], shards["w_q"], shards["w_o"], shards["w_in1"]], mids=(0.8, 1.0))
    hm0, r0, h2p = mlp_fwd(h1, g_mlp0, win0, wout0, 0, bg=[gc])
    wkv, wq, wo, win1 = gc.result
    wkv, wq, wo = wkv.reshape(D, 512), wq.reshape(D, D), wo.reshape(D, D)
    h2 = _from_chunked(h2p)
    kvn, hn1, k4, v4, q = attn_pre(h2, g_kv, g_mix1, wkv, bkv, spread, wq, bq)
    gd = BgGather([shards["w_out1"]], mids=(0.94, 1.0))
    o = attn_core_fwd(q, k4, v4, sinks, bg=[gd])
    wout1, = gd.result
    h3 = attn_post(h2, o, wo, bo)
    hm1, r1, h4 = mlp_fwd(h3, g_mlp1, win1, wout1, 1)
    loss, dh4, dh4b, dg_fin = final_loss(h4, g_fin, target)

    big = {}
    rider = lambda n, part, r2, **kw: AdamRider(*opt[n], part, r2, **kw)
    dh3, dwin1, dwout1, dg_mlp1 = mlp_bwd(h3, hm1, r1, g_mlp1, dh4, dh4b, win1, wout1, 1)
    pa = BgPair([dwin1, dwout1])
    do, dwo, dbo = attn_bwd_pre(dh3, o, wo, bg=[pa])
    p_in1, p_out1 = reduce_pairs(["w_in1", "w_out1"], pa)
    ca = BgChips([p_in1])
    dq, dk4, dv4, dsink = attn_core_bwd(q, do, k4, v4, sinks, bg=[ca])
    dh2, dwq, dbq, dg_mix1 = attn_bwd_q(h2, dh3, dq, hn1, g_mix1, wq)
    dh2, dh2b, dwkv, dbkv, dg_kv = attn_bwd_kv(h2, dh2, dk4, dv4, kvn, g_kv, wkv, spread)
    pb = BgPair([dwkv.reshape(NDEV, 128, 512), dwq.reshape(NDEV, 128, D), dwo.reshape(NDEV, 128, D)])
    ca2 = BgChips([p_out1])
    dh2p, dh2pb = _to_chunked(dh2), _to_chunked(dh2b)
    dh1, dwin0, dwout0, dg_mlp0 = mlp_bwd(h1, hm0, r0, g_mlp0, dh2p, dh2pb, win0, wout0, 0, bg=[pb, ca2])
    cb = BgChips(reduce_pairs(["w_kv", "w_q", "w_o"], pb))
    pc = BgPair([dwin0, dwout0])
    dy, dwglu, dbglu = s5_post_bwd(dh1, y, z, wglu, bg=[cb, pc])
    cc = BgChips(reduce_pairs(["w_in0", "w_out0"], pc))
    pd = BgPair([dwglu])
    a_in1 = rider("w_mlp_in", p_in1, ca.result[0], layer=1)
    a_out1 = rider("w_mlp_out", p_out1, ca2.result[0], layer=1)
    du, dbm, dcmt, dlam = s5_core_bwd(hn0, dy, bmb, lam, cmb, bg=[cc, pd, a_in1, a_out1])
    a_attn = [rider(n, part, r2) for n, part, r2 in zip(("w_kv", "w_q", "w_o"), cb.arrs, cb.result)]
    a_in0 = rider("w_mlp_in", cc.arrs[0], cc.result[0], layer=0, prev=a_in1.result)
    a_out0 = rider("w_mlp_out", cc.arrs[1], cc.result[1], layer=0, prev=a_out1.result)
    dxp, dg_mix0, dd = s5_pre_bwd(xp, g_mix0, du, dy, d_skip, dh1, bg=[a_in0, a_out0] + a_attn)
    big["w_kv"], big["w_q"], big["w_o"] = [a.result for a in a_attn]
    big["w_mlp_in"], big["w_mlp_out"] = a_in0.result, a_out0.result
    big["s5_w_glu"] = BgChips(reduce_pairs(["s5_w_glu"], pd))
    grad_x = _from_chunked(dxp)
    da_re, da_im, dlog_dt, db_re, db_im, dc_re, dc_im = prep_vjp((dlam, dbm, dcmt.transpose(0, 2, 1)))

    def lanes(v_):
        v_ = v_.reshape(1, -1)
        return jnp.pad(v_, ((0, 0), (0, D - v_.shape[1])))

    small = jnp.concatenate([
        dg_mix0[0:1], dg_mix1[0:1], dg_mlp0[0:1], dg_mlp1[0:1], dg_kv[0:1], dg_fin[0:1], dd[0:1], dbq[0:1], dbo[0:1],
        dbglu[0:1].reshape(2, D), lanes(dbkv[0:1]),
        lanes(dsink[:, 0, :Q_PER_KV]), lanes(dlog_dt), lanes(loss[0:1, 0:1]), jnp.zeros((1, D), f32),
        da_re.reshape(4, D), da_im.reshape(4, D),
        db_re.transpose(0, 2, 1).reshape(64, D), db_im.transpose(0, 2, 1).reshape(64, D),
        dc_re.reshape(64, D), dc_im.reshape(64, D)], axis=0)
    return loss, grad_x, small, big


_ANY = pl.BlockSpec(memory_space=pl.ANY)


def _pos():
    return lax.axis_index("x"), lax.axis_index("y"), lax.axis_index("c")


def _other_chips(x, y):
    return [(1 - x, y), (x, 1 - y), (1 - x, 1 - y)]


def all_gather(arrs):
    n = len(arrs)

    def body(*refs):
        ins, outs = refs[:n], refs[n:2 * n]
        send_sems, recv_sems, local_sems = refs[2 * n:]
        x, y, c = _pos()
        me, sib = (x, y, c), (x, y, 1 - c)
        chips = _other_chips(x, y)

        def copy(a, k, block, to, src=None):
            dst = outs[a].at[4 * block[0] + 2 * block[1] + block[2]]
            return pltpu.make_async_remote_copy(
                src_ref=dst if src is None else src, dst_ref=dst, send_sem=send_sems.at[a, k],
                recv_sem=recv_sems.at[a, k], device_id=to, device_id_type=MESH)

        mine = [pltpu.make_async_copy(ins[a], outs[a].at[4 * x + 2 * y + c], local_sems.at[a]) for a in range(n)]
        for cp in mine:
            cp.start()
        first = []
        for a in range(n):
            first.append(copy(a, 0, me, sib, src=ins[a]))
            first += [copy(a, 1 + j, me, (*chip, c), src=ins[a]) for j, chip in enumerate(chips)]
        for cp in first:
            cp.start()
        passed = []
        for j, chip in enumerate(chips):
            for a in range(n):
                copy(a, 1 + j, (*chip, c), me).wait_recv()
                cp = copy(a, 4 + j, (*chip, c), sib)
                cp.start()
                passed.append(cp)
        for a in range(n):
            copy(a, 0, sib, me).wait_recv()
            for j, chip in enumerate(chips):
                copy(a, 4 + j, (*chip, 1 - c), me).wait_recv()
        for cp in first + passed:
            cp.wait_send()
        for cp in mine:
            cp.wait()

    return pl.pallas_call(
        body, name="all_gather", in_specs=[_ANY] * n, out_specs=[_ANY] * n,
        out_shape=[SDS((NDEV,) + a.shape, a.dtype) for a in arrs],
        scratch_shapes=[pltpu.SemaphoreType.DMA((n, 7)), pltpu.SemaphoreType.DMA((n, 7)),
                        pltpu.SemaphoreType.DMA((n,))],
    )(*arrs)


def rs_pair(grads):
    n = len(grads)

    def body(*refs):
        ins, outs = refs[:n], refs[n:2 * n]
        send_sems, recv_sems = refs[2 * n:]
        x, y, c = _pos()
        cps = []
        for a in range(n):
            for k in range(4):
                cps.append(pltpu.make_async_remote_copy(
                    src_ref=ins[a].at[2 * k + 1 - c], dst_ref=outs[a].at[k], send_sem=send_sems.at[a, k],
                    recv_sem=recv_sems.at[a, k], device_id=(x, y, 1 - c), device_id_type=MESH))
        for cp in cps:
            cp.start()
        for cp in cps:
            cp.wait_recv()
        for cp in cps:
            cp.wait_send()

    return pl.pallas_call(
        body, name="rs_pair", in_specs=[_ANY] * n, out_specs=[_ANY] * n,
        out_shape=[SDS((4,) + g.shape[1:], g.dtype) for g in grads],
        scratch_shapes=[pltpu.SemaphoreType.DMA((n, 4)), pltpu.SemaphoreType.DMA((n, 4))],
    )(*grads)


def rs_chips(parts):
    n = len(parts)

    def body(*refs):
        ins, outs = refs[:n], refs[n:2 * n]
        send_sems, recv_sems = refs[2 * n:]
        x, y, c = _pos()
        cps = []
        for a in range(n):
            for r, (px, py) in enumerate(_other_chips(x, y)):
                cps.append(pltpu.make_async_remote_copy(
                    src_ref=ins[a].at[2 * px + py], dst_ref=outs[a].at[r], send_sem=send_sems.at[a, r],
                    recv_sem=recv_sems.at[a, r], device_id=(px, py, c), device_id_type=MESH))
        for cp in cps:
            cp.start()
        for cp in cps:
            cp.wait_recv()
        for cp in cps:
            cp.wait_send()

    return pl.pallas_call(
        body, name="rs_chips", in_specs=[_ANY] * n, out_specs=[_ANY] * n,
        out_shape=[SDS((3,) + g.shape[1:], g.dtype) for g in parts],
        scratch_shapes=[pltpu.SemaphoreType.DMA((n, 3)), pltpu.SemaphoreType.DMA((n, 3))],
    )(*parts)


def _row_tile(r, c):
    return min(r, max(8, (512 * 1024) // c))


def add_pairs(g, r1, core, name):
    _, R, C = g.shape
    tr = _row_tile(R, C)

    def body(core_ref, g_ref, r_ref, o_ref):
        o_ref[...] = (g_ref[...].astype(f32) + r_ref[...].astype(f32)).astype(bf16)

    return pl.pallas_call(
        body, name=name, out_shape=SDS((4, R, C), bf16),
        grid_spec=pltpu.PrefetchScalarGridSpec(
            num_scalar_prefetch=1, grid=(4, R // tr),
            in_specs=[pl.BlockSpec((None, tr, C), lambda k, i, core: (2 * k + core[0], i, 0)),
                      pl.BlockSpec((None, tr, C), lambda k, i, core: (k, i, 0))],
            out_specs=pl.BlockSpec((None, tr, C), lambda k, i, core: (k, i, 0))),
        compiler_params=_cp(dimension_semantics=("arbitrary", "arbitrary")),
    )(core, g, r1)


def _adamw(w, g, m, v):
    m = ADAM_B1 * m + (1.0 - ADAM_B1) * g
    v = ADAM_B2 * v + (1.0 - ADAM_B2) * (g * g)
    m_hat = m / (1.0 - ADAM_B1 ** ADAM_STEP)
    v_hat = v / (1.0 - ADAM_B2 ** ADAM_STEP)
    delta = -ADAM_LR * (m_hat / (jnp.sqrt(v_hat) + ADAM_EPS) + ADAM_WD * w)
    return delta, m, v


def adam_big(w, m, v, part, r2, chip, name, layer=0, prev=None):
    L, R, C = w.shape
    tr = _row_tile(R, C)

    def body(chip_ref, w_ref, m_ref, v_ref, p_ref, r_ref, *rest):
        g_out, d_out, m_out, v_out = rest[-4:]
        g = p_ref[...].astype(f32) + r_ref[0].astype(f32) + r_ref[1].astype(f32) + r_ref[2].astype(f32)
        d, m_, v_ = _adamw(w_ref[...], g, m_ref[...], v_ref[...])
        g_out[...] = g
        d_out[...] = d
        m_out[...] = m_
        v_out[...] = v_

    blk = pl.BlockSpec((None, tr, C), lambda i, chip: (layer, i, 0))
    extra = [] if prev is None else list(prev)
    return pl.pallas_call(
        body, name=name, out_shape=[SDS((L, R, C), f32)] * 4,
        grid_spec=pltpu.PrefetchScalarGridSpec(
            num_scalar_prefetch=1, grid=(R // tr,),
            in_specs=[blk, blk, blk,
                      pl.BlockSpec((None, tr, C), lambda i, chip: (chip[0], i, 0)),
                      pl.BlockSpec((3, tr, C), lambda i, chip: (0, i, 0))] + [_ANY] * len(extra),
            out_specs=[blk] * 4),
        input_output_aliases={6 + k: k for k in range(len(extra))},
        compiler_params=_cp(dimension_semantics=("arbitrary",)),
    )(chip, w, m, v, part, r2, *extra)


def allreduce_small(buf, chips=None):
    shp = buf.shape
    half = (shp[0] // 16) * 8
    parts = (pl.ds(0, half), pl.ds(half, shp[0] - half))
    n_c = 0 if chips is None else len(chips.arrs)

    def body(in_ref, *refs):
        c_in, out_ref, c_out = refs[:n_c], refs[n_c], refs[n_c + 1:2 * n_c + 1]
        acc1, acc2, r0, r1, r2, send_sems, recv_sems = refs[2 * n_c + 1:2 * n_c + 8]
        c_sems = refs[2 * n_c + 8:]
        if chips is not None:
            chips.start(c_in, c_out, c_sems)
        x, y, c = _pos()
        across = [(1 - x, y, c), (x, 1 - y, c)]

        def exchange(src, rcv, dst, copies):
            cps = [pltpu.make_async_remote_copy(
                src_ref=src.at[rows], dst_ref=rcv.at[rows], send_sem=send_sems.at[k], recv_sem=recv_sems.at[k],
                device_id=peer, device_id_type=MESH) for k, rows, peer in copies]
            for cp in cps:
                cp.start()
            for cp in cps:
                cp.wait()
            dst[...] = src[...] + rcv[...]

        exchange(in_ref, r0, acc1, [(0, pl.ds(0, shp[0]), (x, y, 1 - c))])
        exchange(acc1, r1, acc2, [(1, parts[0], across[0]), (2, parts[1], across[1])])
        exchange(acc2, r2, out_ref, [(3, parts[0], across[1]), (4, parts[1], across[0])])
        if chips is not None:
            chips.finish(c_in, c_out, c_sems)

    vm = pl.BlockSpec(memory_space=pltpu.VMEM)
    res = pl.pallas_call(
        body, name="allreduce_small", in_specs=[vm] + [_ANY] * n_c, out_specs=[vm] + [_ANY] * n_c,
        out_shape=[SDS(shp, f32)] + ([] if chips is None else chips.out_shape),
        scratch_shapes=[pltpu.VMEM(shp, f32)] * 5 + [pltpu.SemaphoreType.DMA((5,)), pltpu.SemaphoreType.DMA((5,))]
        + ([] if chips is None else chips.scratch),
    )(buf, *([] if chips is None else chips.arrs))
    if chips is not None:
        chips.result = list(res[1:])
    return res[0]


SMALL_ROWS = {'norm_mix': (0, 2, D), 'norm_mlp': (2, 2, D), 'norm_kv': (4, 1, D), 'norm_final': (5, 1, D),
              's5_d': (6, 1, D), 'b_q': (7, 1, D), 'b_o': (8, 1, D), 's5_b_glu': (9, 2, D), 'b_kv': (11, 1, 512),
              'sinks': (12, 1, 16), 's5_log_dt': (13, 1, 64), 's5_a_re': (16, 4, D), 's5_a_im': (20, 4, D),
              's5_b_re': (24, 64, D), 's5_b_im': (88, 64, D), 's5_c_re': (152, 64, D), 's5_c_im': (216, 64, D)}
LOSS_ROW = 14
ROW_PARAMS = ['norm_mix', 'norm_mlp', 'norm_kv', 'norm_final', 'b_q', 'b_o', 'b_kv', 'sinks', 's5_log_dt']
SHARD_PARAMS = ['s5_d', 's5_b_glu']
S5_PARAMS = ['s5_a_re', 's5_a_im', 's5_b_re', 's5_b_im', 's5_c_re', 's5_c_im']


def adam_small(dev, gsum, s5_grads, w, m, v):
    names = ROW_PARAMS + SHARD_PARAMS + S5_PARAMS
    n_g = len(ROW_PARAMS) + len(SHARD_PARAMS)

    def body(dev_ref, gs_ref, *refs):
        pos = [0]

        def take(k):
            r = refs[pos[0]:pos[0] + k]
            pos[0] += k
            return r

        g5 = take(len(S5_PARAMS))
        wr, mr, vr = take(len(names)), take(len(names)), take(len(names))
        g_out = take(n_g)
        d_out, m_out, v_out = take(len(names)), take(len(names)), take(len(names))
        dv = dev_ref[0]
        for i, n in enumerate(names):
            if n in S5_PARAMS:
                g = g5[S5_PARAMS.index(n)][...]
            elif n in SHARD_PARAMS:
                r0, _, _ = SMALL_ROWS[n]
                ln = wr[i].shape[1]
                g = jnp.zeros((1, ln), f32)
                for k in range(NDEV):
                    off = k * ln
                    piece = gs_ref[r0 + off // D:r0 + off // D + 1, off % D:off % D + ln]
                    g = g + jnp.where(dv == k, piece, 0.0)
                g_out[i][...] = g
            else:
                r0, nr, nl = SMALL_ROWS[n]
                g = gs_ref[r0:r0 + nr, 0:nl]
                g_out[i][...] = g
            d, m_, v_ = _adamw(wr[i][...], g, mr[i][...], vr[i][...])
            d_out[i][...] = d
            m_out[i][...] = m_
            v_out[i][...] = v_

    vm = pl.BlockSpec(memory_space=pltpu.VMEM)
    ins = [s5_grads[n] for n in S5_PARAMS] + [d[n] for d in (w, m, v) for n in names]
    shapes = [SDS(w[n].shape, f32) for n in names]
    res = pl.pallas_call(
        body, name="adam_small", in_specs=[pl.BlockSpec(memory_space=pltpu.SMEM)] + [vm] * (1 + len(ins)),
        out_specs=[vm] * (n_g + 3 * len(names)), out_shape=shapes[:n_g] + shapes * 3,
        compiler_params=_cp(),
    )(dev, gsum, *ins)
    g_o = dict(zip(names[:n_g], res[:n_g]))
    rest = res[n_g:]
    k = len(names)
    return g_o, dict(zip(names, rest[:k])), dict(zip(names, rest[k:2 * k])), dict(zip(names, rest[2 * k:]))


WEIGHTS = ['norm_mix', 'norm_mlp', 'norm_kv', 'norm_final', 's5_a_re', 's5_a_im', 's5_log_dt', 's5_b_re', 's5_b_im',
           's5_c_re', 's5_c_im', 's5_d', 's5_w_glu', 's5_b_glu', 'w_kv', 'b_kv', 'w_q', 'b_q', 'sinks', 'w_o', 'b_o',
           'w_mlp_in', 'w_mlp_out']
BIG = ['s5_w_glu', 'w_kv', 'w_q', 'w_o', 'w_mlp_in', 'w_mlp_out']
BIG_2D = {'s5_w_glu': (D, 256), 'w_kv': (128, 512), 'w_q': (128, D), 'w_o': (128, D), 'w_mlp_in': (2 * D, 512),
          'w_mlp_out': (2 * 512, D)}
SHARDED_SMALL = {'s5_d': D, 's5_b_glu': 2 * D}
SMALL = [n for n in WEIGHTS if n not in BIG]
SMALL_SIZE = {'norm_mix': 2 * D, 'norm_mlp': 2 * D, 'norm_kv': D, 'norm_final': D, 's5_a_re': 4096, 's5_a_im': 4096,
              's5_log_dt': 64, 's5_b_re': 65536, 's5_b_im': 65536, 's5_c_re': 65536, 's5_c_im': 65536, 's5_d': D,
              's5_b_glu': 2 * D, 'b_kv': 512, 'b_q': D, 'sinks': 16, 'b_o': D}


def _pack(vals):
    parts = []
    for n in SMALL:
        v = vals[n].reshape(-1).astype(f32)
        parts.append(jnp.pad(v, (0, (-v.shape[0]) % 128)))
    flat = jnp.concatenate(parts)
    flat = jnp.pad(flat, (0, (-flat.shape[0]) % 1024))
    return flat.reshape(-1, 128)


def _unpack(buf):
    flat = buf.reshape(-1)
    out, off = {}, 0
    for n in SMALL:
        sz = SMALL_SIZE[n]
        out[n] = flat[off:off + sz]
        off += sz + (-sz) % 128
    return out


def kernel(x, norm_mix, norm_mlp, norm_kv, norm_final, s5_a_re, s5_a_im, s5_log_dt, s5_b_re, s5_b_im, s5_c_re, s5_c_im, s5_d, s5_w_glu, s5_b_glu, w_kv, b_kv, w_q, b_q, sinks, w_o, b_o, w_mlp_in, w_mlp_out, loss_target, m_norm_mix, m_norm_mlp, m_norm_kv, m_norm_final, m_s5_a_re, m_s5_a_im, m_s5_log_dt, m_s5_b_re, m_s5_b_im, m_s5_c_re, m_s5_c_im, m_s5_d, m_s5_w_glu, m_s5_b_glu, m_w_kv, m_b_kv, m_w_q, m_b_q, m_sinks, m_w_o, m_b_o, m_w_mlp_in, m_w_mlp_out, v_norm_mix, v_norm_mlp, v_norm_kv, v_norm_final, v_s5_a_re, v_s5_a_im, v_s5_log_dt, v_s5_b_re, v_s5_b_im, v_s5_c_re, v_s5_c_im, v_s5_d, v_s5_w_glu, v_s5_b_glu, v_w_kv, v_b_kv, v_w_q, v_b_q, v_sinks, v_w_o, v_b_o, v_w_mlp_in, v_w_mlp_out):
    w = dict(norm_mix=norm_mix, norm_mlp=norm_mlp, norm_kv=norm_kv, norm_final=norm_final, s5_a_re=s5_a_re,
             s5_a_im=s5_a_im, s5_log_dt=s5_log_dt, s5_b_re=s5_b_re, s5_b_im=s5_b_im, s5_c_re=s5_c_re, s5_c_im=s5_c_im,
             s5_d=s5_d, s5_w_glu=s5_w_glu, s5_b_glu=s5_b_glu, w_kv=w_kv, b_kv=b_kv, w_q=w_q, b_q=b_q, sinks=sinks,
             w_o=w_o, b_o=b_o, w_mlp_in=w_mlp_in, w_mlp_out=w_mlp_out)
    m = dict(norm_mix=m_norm_mix, norm_mlp=m_norm_mlp, norm_kv=m_norm_kv, norm_final=m_norm_final, s5_a_re=m_s5_a_re,
             s5_a_im=m_s5_a_im, s5_log_dt=m_s5_log_dt, s5_b_re=m_s5_b_re, s5_b_im=m_s5_b_im, s5_c_re=m_s5_c_re,
             s5_c_im=m_s5_c_im, s5_d=m_s5_d, s5_w_glu=m_s5_w_glu, s5_b_glu=m_s5_b_glu, w_kv=m_w_kv, b_kv=m_b_kv,
             w_q=m_w_q, b_q=m_b_q, sinks=m_sinks, w_o=m_w_o, b_o=m_b_o, w_mlp_in=m_w_mlp_in, w_mlp_out=m_w_mlp_out)
    v = dict(norm_mix=v_norm_mix, norm_mlp=v_norm_mlp, norm_kv=v_norm_kv, norm_final=v_norm_final, s5_a_re=v_s5_a_re,
             s5_a_im=v_s5_a_im, s5_log_dt=v_s5_log_dt, s5_b_re=v_s5_b_re, s5_b_im=v_s5_b_im, s5_c_re=v_s5_c_re,
             s5_c_im=v_s5_c_im, s5_d=v_s5_d, s5_w_glu=v_s5_w_glu, s5_b_glu=v_s5_b_glu, w_kv=v_w_kv, b_kv=v_b_kv,
             w_q=v_w_q, b_q=v_b_q, sinks=v_sinks, w_o=v_w_o, b_o=v_b_o, w_mlp_in=v_w_mlp_in, w_mlp_out=v_w_mlp_out)
    xi, yi, ci = _pos()
    dev = 4 * xi + 2 * yi + ci
    core = ci.reshape(1).astype(jnp.int32)
    chip = (2 * xi + yi).reshape(1).astype(jnp.int32)

    shards = {
        "s5_w_glu": s5_w_glu[0].astype(bf16), "w_kv": w_kv.astype(bf16), "w_q": w_q[0].astype(bf16),
        "w_o": w_o[0].astype(bf16), "w_in0": w_mlp_in[0].astype(bf16), "w_in1": w_mlp_in[1].astype(bf16),
        "w_out0": w_mlp_out[0].astype(bf16), "w_out1": w_mlp_out[1].astype(bf16),
        "vecs": jnp.broadcast_to(jnp.concatenate([s5_d, s5_b_glu], axis=1), (8, 384)),
    }
    as3d = lambda a, n: a if a.ndim == 3 and a.shape[0] == 2 else a.reshape((1,) + BIG_2D[n])
    opt = {n: (as3d(w[n], n), as3d(m[n], n), as3d(v[n], n)) for n in BIG}
    _, grad_x, grads, big = fwd_bwd(x[0], loss_target[0], {n: w[n] for n in SMALL}, shards, opt, core, chip)

    glu_chips = big["s5_w_glu"]
    gsum = allreduce_small(grads, chips=glu_chips)
    big["s5_w_glu"] = adam_big(*opt["s5_w_glu"], glu_chips.arrs[0], glu_chips.result[0], chip, "adam_s5_w_glu")

    out_g, out_d, out_m, out_v = {}, {}, {}, {}
    for n in BIG:
        out_g[n], out_d[n], out_m[n], out_v[n] = [r.reshape(w[n].shape) for r in big[n]]

    loss = gsum[LOSS_ROW, 0]
    swapped = ("s5_b_re", "s5_b_im")
    swap = lambda a: a.transpose(0, 1, 3, 2)

    def kernel_side(d):
        d = {n: (d[n].reshape(1, -1) if d[n].ndim == 1 else d[n]) for n in SMALL}
        d.update({n: swap(d[n]) for n in swapped})
        return d

    s5_g = {}
    for n in S5_PARAMS:
        r0, nr, _ = SMALL_ROWS[n]
        s5_g[n] = gsum[r0:r0 + nr].reshape((1, 64, 16, 64) if n in swapped else w[n].shape)
        out_g[n] = s5_g[n]
    g_s, d_s, m_s, v_s = adam_small(dev.reshape(1).astype(jnp.int32), gsum, s5_g, kernel_side(w), kernel_side(m),
                                    kernel_side(v))
    for src, dst in ((g_s, out_g), (d_s, out_d), (m_s, out_m), (v_s, out_v)):
        dst.update(src)
    for dst in (out_g, out_d, out_m, out_v):
        for n in SMALL:
            dst[n] = (swap(dst[n]) if n in swapped else dst[n]).reshape(w[n].shape)

    return (loss, grad_x[None], *[out_g[n] for n in WEIGHTS], *[out_d[n] for n in WEIGHTS],
            *[out_m[n] for n in WEIGHTS], *[out_v[n] for n in WEIGHTS])
```

```python
import functools
import math

import jax
import jax.numpy as jnp
from jax import lax
from jax.experimental import pallas as pl
from jax.experimental.pallas import tpu as pltpu

f32 = jnp.float32
bf16 = jnp.bfloat16
SDS = jax.ShapeDtypeStruct

T = 2048
D = 1024
NDEV = 8
NORM_EPS = 1e-5
S5_G, S5_C, S5_P = 64, 16, 64
S5_SUB = 8
S5_CH = 8
S5_STEPS = T // S5_CH
DT_MIN_LAMBDA = -1e-4
HEAD_DIM = 64
N_KV = 4
Q_PER_KV = 4
BLK = 128
D_FF_SHARD = 512
ADAM_LR, ADAM_B1, ADAM_B2, ADAM_EPS, ADAM_WD, ADAM_STEP = 0.001, 0.9, 0.999, 1e-08, 0.01, 10
VMEM_LIMIT = 56 * 1024 * 1024
MESH = pl.DeviceIdType.MESH


def _cp(**kw):
    return pltpu.CompilerParams(vmem_limit_bytes=VMEM_LIMIT, **kw)


def _dot(a, b):
    return jnp.dot(a, b, preferred_element_type=f32)


def _dot_nt(a, b):
    return lax.dot_general(a, b, (((1,), (1,)), ((), ())), preferred_element_type=f32)


def _dot_tn(a, b):
    return lax.dot_general(a, b, (((0,), (0,)), ((), ())), preferred_element_type=f32)


def _rms(x, g):
    r = lax.rsqrt(jnp.mean(x * x, axis=-1, keepdims=True) + NORM_EPS)
    return x * r * g, r


def _rms_bwd(x, g, dy):
    r = lax.rsqrt(jnp.mean(x * x, axis=-1, keepdims=True) + NORM_EPS)
    u = dy * g
    dx = r * u - (r * r * r) * x * jnp.mean(u * x, axis=-1, keepdims=True)
    return dx, dy * x * r


def _colsum8(v):
    s = jnp.sum(v, axis=0, keepdims=True)
    row = lax.broadcasted_iota(jnp.int32, (8, v.shape[1]), 0)
    return jnp.where(row == 0, jnp.broadcast_to(s, (8, v.shape[1])), 0.0)


def _full(shape):
    nd = len(shape)
    return pl.BlockSpec(shape, lambda *_: (0,) * nd, pipeline_mode=pl.Buffered(1))


_ANY = pl.BlockSpec(memory_space=pl.ANY)


def _pos():
    return lax.axis_index("x"), lax.axis_index("y"), lax.axis_index("c")


def _other_chips(x, y):
    return [(1 - x, y), (x, 1 - y), (1 - x, 1 - y)]


class BgGather:
    SIB, XN, YN, FWD_Y, FWD_X, SIB_X, SIB_Y, SIB_D = range(8)

    def __init__(self, arrs, mids=(0.5, 0.75)):
        n = len(arrs)
        self.arrs = list(arrs)
        self.out_shape = [SDS((NDEV,) + a.shape, a.dtype) for a in arrs]
        self.scratch = [pltpu.SemaphoreType.DMA((n, 8)), pltpu.SemaphoreType.DMA((n, 8)),
                        pltpu.SemaphoreType.DMA((n,))]
        self.mids = mids
        self.result = None

    def mid_steps(self, nsteps):
        at = lambda f: min(nsteps - 1, max(0, int(f * nsteps) - 1))
        return [(at(self.mids[0]), self.mid), (max(at(self.mids[0]), at(self.mids[1])), self.mid2)]

    def _halves(self, a):
        rows = self.arrs[a].shape[0]
        cut = rows // 2 if rows >= 32 else rows
        return (0, cut), (cut, rows - cut)

    def _copy(self, ins, outs, sems, a, k, block, to, own=False, part=None):
        slot = 4 * block[0] + 2 * block[1] + block[2]
        rows = pl.ds(0, self.arrs[a].shape[0]) if part is None else pl.ds(*self._halves(a)[part])
        dst = outs[a].at[slot, rows]
        return pltpu.make_async_remote_copy(
            src_ref=ins[a].at[rows] if own else dst, dst_ref=dst, send_sem=sems[0].at[a, k],
            recv_sem=sems[1].at[a, k], device_id=to, device_id_type=MESH)

    def _mine(self, ins, outs, sems):
        x, y, c = _pos()
        return [pltpu.make_async_copy(ins[a], outs[a].at[4 * x + 2 * y + c], sems[2].at[a])
                for a in range(len(self.arrs))]

    def _split(self, a):
        return self._halves(a)[1][1] > 0

    def _sends(self, ins, outs, sems, phase):
        x, y, c = _pos()
        me, sib, xn, yn, dg = (x, y, c), (x, y, 1 - c), (1 - x, y, c), (x, 1 - y, c), (1 - x, 1 - y, c)
        cps = []
        for a in range(len(self.arrs)):
            cp = lambda k, block, to, **kw: self._copy(ins, outs, sems, a, k, block, to, **kw)
            if phase == 0:
                cps += [cp(self.SIB, me, sib, own=True), cp(self.XN, me, xn, own=True), cp(self.YN, me, yn, own=True)]
            elif phase == 1:
                cps.append(cp(self.FWD_Y, xn, yn, part=0))
                if self._split(a):
                    cps.append(cp(self.FWD_X, yn, xn, part=1))
                cps += [cp(self.SIB_X, xn, sib), cp(self.SIB_Y, yn, sib)]
            else:
                cps.append(cp(self.SIB_D, dg, sib))
        return cps

    def _arrivals(self, ins, outs, sems, phase):
        x, y, c = _pos()
        me, xn, yn, dg = (x, y, c), (1 - x, y, c), (x, 1 - y, c), (1 - x, 1 - y, c)
        cps = []
        for a in range(len(self.arrs)):
            cp = lambda k, block, **kw: self._copy(ins, outs, sems, a, k, block, me, **kw)
            if phase == 1:
                cps += [cp(self.XN, xn), cp(self.YN, yn)]
            elif phase == 2:
                cps.append(cp(self.FWD_Y, dg, part=0))
                if self._split(a):
                    cps.append(cp(self.FWD_X, dg, part=1))
            else:
                cps += [cp(self.SIB, (x, y, 1 - c)), cp(self.SIB_X, (1 - x, y, 1 - c)),
                        cp(self.SIB_Y, (x, 1 - y, 1 - c)), cp(self.SIB_D, (1 - x, 1 - y, 1 - c))]
        return cps

    def start(self, ins, outs, sems):
        for cp in self._mine(ins, outs, sems) + self._sends(ins, outs, sems, 0):
            cp.start()

    def mid(self, ins, outs, sems):
        for cp in self._arrivals(ins, outs, sems, 1):
            cp.wait_recv()
        for cp in self._sends(ins, outs, sems, 1):
            cp.start()

    def mid2(self, ins, outs, sems):
        for cp in self._arrivals(ins, outs, sems, 2):
            cp.wait_recv()
        for cp in self._sends(ins, outs, sems, 2):
            cp.start()

    def finish(self, ins, outs, sems):
        for cp in self._arrivals(ins, outs, sems, 3):
            cp.wait_recv()
        for ph in range(3):
            for cp in self._sends(ins, outs, sems, ph):
                cp.wait_send()
        for cp in self._mine(ins, outs, sems):
            cp.wait()


class BgPair:
    def __init__(self, arrs):
        n = len(arrs)
        self.arrs = list(arrs)
        self.out_shape = [SDS((4,) + a.shape[1:], a.dtype) for a in arrs]
        self.scratch = [pltpu.SemaphoreType.DMA((n, 4)), pltpu.SemaphoreType.DMA((n, 4))]
        self.result = None

    def mid_steps(self, nsteps):
        return []

    def _copies(self, ins, outs, sems):
        x, y, c = _pos()
        return [pltpu.make_async_remote_copy(
            src_ref=ins[a].at[2 * k + 1 - c], dst_ref=outs[a].at[k], send_sem=sems[0].at[a, k],
            recv_sem=sems[1].at[a, k], device_id=(x, y, 1 - c), device_id_type=MESH)
            for a in range(len(self.arrs)) for k in range(4)]

    def start(self, ins, outs, sems):
        for cp in self._copies(ins, outs, sems):
            cp.start()

    def finish(self, ins, outs, sems):
        cps = self._copies(ins, outs, sems)
        for cp in cps:
            cp.wait_recv()
        for cp in cps:
            cp.wait_send()


class BgChips(BgPair):
    def __init__(self, arrs):
        n = len(arrs)
        self.arrs = list(arrs)
        self.out_shape = [SDS((3,) + a.shape[1:], a.dtype) for a in arrs]
        self.scratch = [pltpu.SemaphoreType.DMA((n, 3)), pltpu.SemaphoreType.DMA((n, 3))]
        self.result = None

    def _copies(self, ins, outs, sems):
        x, y, c = _pos()
        return [pltpu.make_async_remote_copy(
            src_ref=ins[a].at[2 * px + py], dst_ref=outs[a].at[r], send_sem=sems[0].at[a, r],
            recv_sem=sems[1].at[a, r], device_id=(px, py, c), device_id_type=MESH)
            for a in range(len(self.arrs)) for r, (px, py) in enumerate(_other_chips(x, y))]


class AdamRider:
    def __init__(self, w, m, v, part, r2, layer=0, prev=None):
        self.arrs = [w, m, v, part, r2] + list(prev or [])
        self.n_prev = len(prev or [])
        self.layer = layer
        self.out_shape = [SDS(w.shape, f32)] * 4
        self.scratch = []
        self.aliases = {5 + k: k for k in range(self.n_prev)}
        self.result = None

    def _tile(self, grid):
        assert len(grid) == 1
        _, R, C = self.arrs[0].shape
        return R // grid[0], C

    def in_specs(self, grid):
        tr, C = self._tile(grid)
        layer = self.layer
        blk = pl.BlockSpec((None, tr, C), lambda b: (layer, b, 0))
        mine = pl.BlockSpec((None, tr, C), lambda b: (2 * lax.axis_index("x") + lax.axis_index("y"), b, 0))
        return [blk, blk, blk, mine, pl.BlockSpec((3, tr, C), lambda b: (0, b, 0))] + [_ANY] * self.n_prev

    def out_specs(self, grid):
        tr, C = self._tile(grid)
        layer = self.layer
        return [pl.BlockSpec((None, tr, C), lambda b: (layer, b, 0))] * 4

    def mid_steps(self, nsteps):
        return []

    def start(self, ins, outs, sems):
        pass

    finish = start

    def step(self, ins, outs, sems):
        w_ref, m_ref, v_ref, p_ref, r_ref = ins[:5]
        g = p_ref[...].astype(f32) + r_ref[0].astype(f32) + r_ref[1].astype(f32) + r_ref[2].astype(f32)
        d, m_, v_ = _adamw(w_ref[...], g, m_ref[...], v_ref[...])
        for ref, val in zip(outs, (g, d, m_, v_)):
            ref[...] = val


def _call(bgs, body, *, name, grid, in_specs, out_specs, out_shape, scratch_shapes=(), compiler_params=None):
    single = not isinstance(out_shape, (list, tuple))
    out_specs_l = [out_specs] if single else list(out_specs)
    out_shape_l = [out_shape] if single else list(out_shape)
    bgs = [b for b in (bgs or []) if b is not None]
    n_in, n_out, n_sc = len(in_specs), len(out_shape_l), len(scratch_shapes)
    nsteps = math.prod(grid)
    b_in_specs = [b.in_specs(grid) if hasattr(b, "in_specs") else [_ANY] * len(b.arrs) for b in bgs]
    b_out_specs = [b.out_specs(grid) if hasattr(b, "out_specs") else [_ANY] * len(b.out_shape) for b in bgs]
    aliases, i_off, o_off = {}, n_in, n_out
    for b in bgs:
        aliases.update({i_off + i: o_off + o for i, o in getattr(b, "aliases", {}).items()})
        i_off, o_off = i_off + len(b.arrs), o_off + len(b.out_shape)

    def full(*refs):
        pos = [0]

        def take(k):
            r = refs[pos[0]:pos[0] + k]
            pos[0] += k
            return r

        ins = take(n_in)
        b_ins = [take(len(b.arrs)) for b in bgs]
        outs = take(n_out)
        b_outs = [take(len(b.out_shape)) for b in bgs]
        sc = take(n_sc)
        b_sc = [take(len(b.scratch)) for b in bgs]
        if bgs:
            step = pl.program_id(0)
            for d in range(1, len(grid)):
                step = step * grid[d] + pl.program_id(d)

            @pl.when(step == 0)
            def _():
                for b, i_, o_, s_ in zip(bgs, b_ins, b_outs, b_sc):
                    b.start(i_, o_, s_)

        body(*ins, *outs, *sc)
        if bgs:
            for b, i_, o_, s_ in zip(bgs, b_ins, b_outs, b_sc):
                if hasattr(b, "step"):
                    b.step(i_, o_, s_)
                for at, fn in b.mid_steps(nsteps):
                    @pl.when(step == at)
                    def _():
                        fn(i_, o_, s_)

            @pl.when(step == nsteps - 1)
            def _():
                for b, i_, o_, s_ in zip(bgs, b_ins, b_outs, b_sc):
                    b.finish(i_, o_, s_)

    def run(*args):
        res = pl.pallas_call(
            full, name=name, grid=grid,
            in_specs=list(in_specs) + [s for l in b_in_specs for s in l],
            out_specs=out_specs_l + [s for l in b_out_specs for s in l],
            out_shape=out_shape_l + [s for b in bgs for s in b.out_shape],
            scratch_shapes=list(scratch_shapes) + [s for b in bgs for s in b.scratch],
            input_output_aliases=aliases,
            compiler_params=compiler_params,
        )(*args, *[a for b in bgs for a in b.arrs])
        rest = list(res[n_out:])
        for b in bgs:
            b.result, rest = rest[:len(b.out_shape)], rest[len(b.out_shape):]
        return res[0] if single else list(res[:n_out])

    return run


def s5_discretize(a_re, a_im, log_dt, b_re, b_im, c_re, c_im):
    lam_r = jnp.minimum(a_re, DT_MIN_LAMBDA)
    lam_i = a_im
    dt = jnp.exp(log_dt)[:, None]
    e = jnp.exp(lam_r * dt)
    lbr = e * jnp.cos(lam_i * dt)
    lbi = e * jnp.sin(lam_i * dt)
    den = lam_r * lam_r + lam_i * lam_i
    cf_r = ((lbr - 1.0) * lam_r + lbi * lam_i) / den
    cf_i = (lbi * lam_r - (lbr - 1.0) * lam_i) / den
    bb_r = cf_r[:, :, None] * b_re - cf_i[:, :, None] * b_im
    bb_i = cf_r[:, :, None] * b_im + cf_i[:, :, None] * b_re
    def block_diag(m, rows, cols):
        t = jnp.broadcast_to(m[:, :, None, :], (8, 8 * rows, 8, cols)).reshape(8, 8 * rows, 8 * cols)
        r = lax.broadcasted_iota(jnp.int32, t.shape, 1) // rows
        c = lax.broadcasted_iota(jnp.int32, t.shape, 2) // cols
        return jnp.where(r == c, t, 0.0)

    def blk_b(m):
        return block_diag(m.reshape(8, 8, S5_P, S5_C).transpose(0, 1, 3, 2).reshape(8, 128, S5_P), S5_C, S5_P)

    def blk_c(m):
        return block_diag(m.reshape(8, 8, S5_C, S5_P).transpose(0, 1, 3, 2).reshape(8, 512, S5_C), S5_P, S5_C)

    bm = jnp.concatenate([blk_b(bb_r), blk_b(bb_i)], axis=-1)
    cm = jnp.concatenate([blk_c(c_re), -blk_c(c_im)], axis=1)
    lam = jnp.stack([lbr.reshape(8, 512), lbi.reshape(8, 512)], axis=1)
    lam = jnp.broadcast_to(lam[:, :, None, :], (8, 2, 8, 512))
    return lam, bm, cm


def _cmul(ar, ai, br, bi):
    return ar * br - ai * bi, ar * bi + ai * br


def _shift_rows(v, k, up):
    row = lax.broadcasted_iota(jnp.int32, v.shape, 0)
    if up:
        return jnp.where(row < 8 - k, pltpu.roll(v, 8 - k, 0), 0.0)
    return jnp.where(row >= k, pltpu.roll(v, k, 0), 0.0)


def _chunk_scan(S, lr, li, reverse, aux=None):
    z = jnp.zeros((8, 512), f32)
    U = 4

    def idx(i):
        return (S5_STEPS - 1 - i) if reverse else i

    def rows_of(s):
        return pl.ds(s * 8, 8) if isinstance(s, int) else pl.ds(pl.multiple_of(s * 8, 8), 8)

    def rec(xr, xi, row):
        br = S[row, 0:512]
        bi = S[row, 512:1024]
        return lr * xr - li * xi + br, lr * xi + li * xr + bi

    def step1(i, c):
        for u in range(U):
            c = rec(c[0], c[1], rows_of(idx(i * U + u)))
        return c

    er, ei = lax.fori_loop(0, S5_STEPS // U, step1, (z, z))
    ar, ai = lr, li
    for _ in range(8):
        ar, ai = _cmul(ar, ai, ar, ai)
    cr, ci = _shift_rows(er, 1, reverse), _shift_rows(ei, 1, reverse)
    for k in (1, 2, 4):
        sr, si = _shift_rows(cr, k, reverse), _shift_rows(ci, k, reverse)
        pr, pi_ = _cmul(ar, ai, sr, si)
        cr, ci = cr + pr, ci + pi_
        ar, ai = _cmul(ar, ai, ar, ai)

    if aux is None:
        def step2(i, c):
            for u in range(U):
                row = rows_of(idx(i * U + u))
                c = rec(c[0], c[1], row)
                S[row, 0:512] = c[0]
                S[row, 512:1024] = c[1]
            return c

        lax.fori_loop(0, S5_STEPS // U, step2, (cr, ci))
        return None

    def one(s, c):
        gr0, gi0, dr, di = c
        row = rows_of(s)
        gr, gi = rec(gr0, gi0, row)
        S[row, 0:512] = gr
        S[row, 512:1024] = gi
        prow = rows_of(s - 1)
        xr = aux[prow, 0:512]
        xi = aux[prow, 512:1024]
        return gr, gi, dr + gr * xr + gi * xi, di + gi * xr - gr * xi

    def step2(i, c):
        for u in range(U):
            c = one(S5_STEPS - 1 - (i * U + u), c)
        return c

    c = lax.fori_loop(0, S5_STEPS // U - 1, step2, (cr, ci, z, z))
    for s in range(U - 1, 0, -1):
        c = one(s, c)
    gr, gi, dr, di = c
    row0 = pl.ds(0, 8)
    gr, gi = rec(gr, gi, row0)
    S[row0, 0:512] = gr
    S[row0, 512:1024] = gi
    last = pl.ds((S5_STEPS - 1) * 8, 8)
    xr = _shift_rows(aux[last, 0:512], 1, False)
    xi = _shift_rows(aux[last, 512:1024], 1, False)
    dr = dr + gr * xr + gi * xi
    di = di + gi * xr - gr * xi
    return dr, di


_ROWS = 256


def _row_loop(fn):
    def body(r, c):
        fn(pl.ds(pl.multiple_of(r * _ROWS, _ROWS), _ROWS))
        return c
    lax.fori_loop(0, T // _ROWS, body, 0)


def s5_core_fwd(hn, bm, lam, cm, bg=()):
    def body(u_ref, b_ref, lam_ref, c_ref, ys_ref, S):
        def bu(rows):
            S[rows, :] = _dot(u_ref[rows, :], b_ref[...])
        _row_loop(bu)
        _chunk_scan(S, lam_ref[0], lam_ref[1], False)

        def ys(rows):
            ys_ref[rows, :] = _dot(S[rows, :].astype(bf16), c_ref[...])
        _row_loop(ys)

    return _call(
        bg, body, name="s5_core_fwd", grid=(S5_SUB,),
        in_specs=[pl.BlockSpec((T, 128), lambda b: (0, b)),
                  pl.BlockSpec((None, 128, 1024), lambda b: (b, 0, 0)),
                  pl.BlockSpec((None, 4, 8, 512), lambda b: (b, 0, 0, 0)),
                  pl.BlockSpec((None, 1024, 128), lambda b: (b, 0, 0))],
        out_specs=pl.BlockSpec((T, 128), lambda b: (0, b)),
        out_shape=SDS((T, D), f32),
        scratch_shapes=[pltpu.VMEM((T, 1024), f32)],
        compiler_params=_cp(dimension_semantics=("arbitrary",)),
    )(hn, bm, lam, cm)


_SEG = _ROWS // S5_CH


def _scan_tile(S, lr, li, k, carry, reverse, store, aux=None):
    steps = range(k * _SEG, (k + 1) * _SEG)
    for s in (reversed(steps) if reverse else steps):
        row = pl.ds(s * 8, 8)
        xr, xi = carry[0], carry[1]
        nr = lr * xr - li * xi + S[row, 0:512]
        ni = lr * xi + li * xr + S[row, 512:1024]
        if store:
            S[row, 0:512] = nr
            S[row, 512:1024] = ni
        if aux is not None and s >= 1:
            prow = pl.ds((s - 1) * 8, 8)
            pr, pi_ = aux[prow, 0:512], aux[prow, 512:1024]
            carry = (nr, ni, carry[2] + nr * pr + ni * pi_, carry[3] + ni * pr - nr * pi_)
        elif aux is not None:
            carry = (nr, ni, carry[2], carry[3])
        else:
            carry = (nr, ni)
    return carry


def _chunk_starts(er, ei, lr, li, reverse):
    ar, ai = lr, li
    for _ in range(8):
        ar, ai = _cmul(ar, ai, ar, ai)
    cr, ci = _shift_rows(er, 1, reverse), _shift_rows(ei, 1, reverse)
    for k in (1, 2, 4):
        sr, si = _shift_rows(cr, k, reverse), _shift_rows(ci, k, reverse)
        pr, pi_ = _cmul(ar, ai, sr, si)
        cr, ci = cr + pr, ci + pi_
        ar, ai = _cmul(ar, ai, ar, ai)
    return cr, ci


def s5_core_bwd(hn, dy, bm, lam, cm, bg=()):
    nt = T // _ROWS

    def body(u_ref, dy_ref, b_ref, lam_ref, c_ref, du_ref, db_ref, dct_ref, dlam_ref, S1, S2):
        lr, li, lcr, lci = lam_ref[0], lam_ref[1], lam_ref[2], lam_ref[3]
        z = jnp.zeros((8, 512), f32)
        tile = lambda k: pl.ds(k * _ROWS, _ROWS)
        dyb = lambda k: dy_ref[tile(k), :].astype(bf16)

        c = (z, z)
        for k in range(nt):
            S1[tile(k), :] = _dot(u_ref[tile(k), :], b_ref[...])
            if k >= 1:
                c = _scan_tile(S1, lr, li, k - 1, c, False, False)
        c = _scan_tile(S1, lr, li, nt - 1, c, False, False)

        c = _chunk_starts(c[0], c[1], lr, li, False)
        dct_ref[...] = jnp.zeros_like(dct_ref)
        for k in range(nt):
            c = _scan_tile(S1, lr, li, k, c, False, True)
            if k >= 1:
                dct_ref[...] += _dot_tn(dyb(k - 1), S1[tile(k - 1), :].astype(bf16))
        dct_ref[...] += _dot_tn(dyb(nt - 1), S1[tile(nt - 1), :].astype(bf16))

        S2[tile(nt - 1), :] = _dot_nt(dyb(nt - 1), c_ref[...])
        c = (z, z)
        for k in range(nt - 1, -1, -1):
            if k >= 1:
                S2[tile(k - 1), :] = _dot_nt(dyb(k - 1), c_ref[...])
            c = _scan_tile(S2, lcr, lci, k, c, True, False)

        def dbu(k):
            gb = S2[tile(k), :].astype(bf16)
            db_ref[...] += _dot_tn(u_ref[tile(k), :], gb)
            du_ref[tile(k), :] = _dot_nt(gb, b_ref[...])

        c = _chunk_starts(c[0], c[1], lcr, lci, True) + (z, z)
        db_ref[...] = jnp.zeros_like(db_ref)
        for k in range(nt - 1, -1, -1):
            c = _scan_tile(S2, lcr, lci, k, c, True, True, aux=S1)
            if k + 1 < nt:
                dbu(k + 1)
        dbu(0)
        gr, gi, dr, di = c
        last = pl.ds((S5_STEPS - 1) * 8, 8)
        xr = _shift_rows(S1[last, 0:512], 1, False)
        xi = _shift_rows(S1[last, 512:1024], 1, False)
        dlam_ref[0] = dr + gr * xr + gi * xi
        dlam_ref[1] = di + gi * xr - gr * xi

    return _call(
        bg, body, name="s5_core_bwd", grid=(S5_SUB,),
        in_specs=[pl.BlockSpec((T, 128), lambda b: (0, b)),
                  pl.BlockSpec((T, 128), lambda b: (0, b)),
                  pl.BlockSpec((None, 128, 1024), lambda b: (b, 0, 0)),
                  pl.BlockSpec((None, 4, 8, 512), lambda b: (b, 0, 0, 0)),
                  pl.BlockSpec((None, 1024, 128), lambda b: (b, 0, 0))],
        out_specs=[pl.BlockSpec((T, 128), lambda b: (0, b)),
                   pl.BlockSpec((None, 128, 1024), lambda b: (b, 0, 0)),
                   pl.BlockSpec((None, 128, 1024), lambda b: (b, 0, 0)),
                   pl.BlockSpec((None, 2, 8, 512), lambda b: (b, 0, 0, 0))],
        out_shape=[SDS((T, D), f32), SDS((8, 128, 1024), f32), SDS((8, 128, 1024), f32), SDS((8, 2, 8, 512), f32)],
        scratch_shapes=[pltpu.VMEM((T, 1024), f32), pltpu.VMEM((T, 1024), f32)],
        compiler_params=_cp(dimension_semantics=("arbitrary",)),
    )(hn, dy, bm, lam, cm)


TM = 512
NT = T // TM


def _tile(n=D):
    return pl.BlockSpec((TM, n), lambda i: (i, 0))


def s5_pre(xp, g):
    def body(x_ref, g_ref, hn_ref):
        hn, _ = _rms(x_ref[...], g_ref[...])
        hn_ref[...] = hn.astype(bf16)

    return pl.pallas_call(
        body, name="s5_pre", grid=(NT,), in_specs=[_tile(), _full((1, D))], out_specs=_tile(),
        out_shape=SDS((T, D), bf16), compiler_params=_cp(dimension_semantics=("arbitrary",)),
    )(xp, g)


def _gelu_grad(y):
    c = math.sqrt(2.0 / math.pi)
    t = jnp.tanh(c * (y + 0.044715 * y * y * y))
    return 0.5 * (1.0 + t) + 0.5 * y * (1.0 - t * t) * c * (1.0 + 3.0 * 0.044715 * y * y)


def s5_post(ys, xp, g, d, wglu, bglu, bg=()):
    def body(ys_ref, x_ref, g_ref, d_ref, w_ref, b_ref, y_ref, z_ref, h_ref):
        x = x_ref[...]
        hn, _ = _rms(x, g_ref[...])
        y = ys_ref[...] + d_ref[...] * hn
        y_ref[...] = y
        yg = jax.nn.gelu(y).astype(bf16)
        for j in range(4):
            cv = slice(j * 256, (j + 1) * 256)
            cg = slice(1024 + j * 256, 1024 + (j + 1) * 256)
            val = _dot(yg, w_ref[j]) + b_ref[:, cv]
            gate = _dot(yg, w_ref[j + 4]) + b_ref[:, cg]
            z_ref[:, cv] = val
            z_ref[:, cg] = gate
            h_ref[:, cv] = x[:, cv] + val * jax.nn.sigmoid(gate)

    return _call(
        bg, body, name="s5_post", grid=(NT,),
        in_specs=[_tile(), _tile(), _full((1, D)), _full((1, D)), _full((8, D, 256)), _full((1, 2 * D))],
        out_specs=[_tile(), _tile(2 * D), _tile()],
        out_shape=[SDS((T, D), f32), SDS((T, 2 * D), f32), SDS((T, D), f32)],
        compiler_params=_cp(dimension_semantics=("arbitrary",)),
    )(ys, xp, g, d, wglu, bglu)


def s5_post_bwd(dh, y, z, wglu, bg=()):
    def body(dh_ref, y_ref, z_ref, w_ref, dy_ref, dw_ref, db_ref, acc):
        i = pl.program_id(0)

        @pl.when(i == 0)
        def _():
            acc[...] = jnp.zeros_like(acc)
            db_ref[...] = jnp.zeros_like(db_ref)

        dh_ = dh_ref[...]
        y = y_ref[...]
        yg = jax.nn.gelu(y).astype(bf16)
        dyg = jnp.zeros((TM, D), f32)
        for j in range(4):
            cv = slice(j * 256, (j + 1) * 256)
            cg = slice(1024 + j * 256, 1024 + (j + 1) * 256)
            val = z_ref[:, cv]
            sg = jax.nn.sigmoid(z_ref[:, cg])
            dval = dh_[:, cv] * sg
            dgate = dh_[:, cv] * val * sg * (1.0 - sg)
            db_ref[:, cv] += _colsum8(dval)
            db_ref[:, cg] += _colsum8(dgate)
            dvb = dval.astype(bf16)
            dgb = dgate.astype(bf16)
            acc[j] += _dot_tn(yg, dvb)
            acc[j + 4] += _dot_tn(yg, dgb)
            dyg = dyg + _dot_nt(dvb, w_ref[j]) + _dot_nt(dgb, w_ref[j + 4])
        dy_ref[...] = dyg * _gelu_grad(y)

        @pl.when(i == NT - 1)
        def _():
            dw_ref[...] = acc[...].astype(bf16)

    return _call(
        bg, body, name="s5_post_bwd", grid=(NT,),
        in_specs=[_tile(), _tile(), _tile(2 * D), _full((8, D, 256))],
        out_specs=[_tile(), _full((8, D, 256)), _full((8, 2 * D))],
        out_shape=[SDS((T, D), f32), SDS((8, D, 256), bf16), SDS((8, 2 * D), f32)],
        scratch_shapes=[pltpu.VMEM((8, D, 256), f32)],
        compiler_params=_cp(dimension_semantics=("arbitrary",)),
    )(dh, y, z, wglu)


def s5_pre_bwd(xp, g, du, dy, d, dh, bg=()):
    def body(x_ref, g_ref, du_ref, dy_ref, d_ref, dh_ref, dx_ref, dg_ref, dd_ref):
        i = pl.program_id(0)

        @pl.when(i == 0)
        def _():
            dg_ref[...] = jnp.zeros_like(dg_ref)
            dd_ref[...] = jnp.zeros_like(dd_ref)

        x = x_ref[...]
        g = g_ref[...]
        dy = dy_ref[...]
        hn, _ = _rms(x, g)
        dhn = du_ref[...] + d_ref[...] * dy
        dx, dgt = _rms_bwd(x, g, dhn)
        dx_ref[...] = dh_ref[...] + dx
        dg_ref[...] += _colsum8(dgt)
        dd_ref[...] += _colsum8(dy * hn)

    return _call(
        bg, body, name="s5_pre_bwd", grid=(NT,),
        in_specs=[_tile(), _full((1, D)), _tile(), _tile(), _full((1, D)), _tile()],
        out_specs=[_tile(), _full((8, D)), _full((8, D))],
        out_shape=[SDS((T, D), f32), SDS((8, D), f32), SDS((8, D), f32)],
        compiler_params=_cp(dimension_semantics=("arbitrary",)),
    )(xp, g, du, dy, d, dh)


TMF = 1024


def mlp_fwd(h, g, w_in, w_out, layer, bg=()):
    def body(h_ref, g_ref, wi_ref, wo_ref, hm_ref, r_ref, out_ref, acc):
        j = pl.program_id(1)

        @pl.when(j == 0)
        def _():
            hm, _ = _rms(h_ref[...], g_ref[...])
            hm_ref[...] = hm.astype(bf16)
            acc[...] = jnp.zeros_like(acc)

        a = jnp.maximum(_dot(hm_ref[...], wi_ref[...]), 0.0)
        r_ref[...] = a.astype(bf16)
        acc[...] += _dot((a * a).astype(bf16), wo_ref[...])

        @pl.when(j == NDEV - 1)
        def _():
            out_ref[...] = h_ref[...] + acc[...]

    return _call(
        bg, body, name=f"mlp_fwd{layer}", grid=(T // TMF, NDEV),
        in_specs=[pl.BlockSpec((TMF, D), lambda i, j: (i, 0)),
                  pl.BlockSpec((1, D), lambda i, j: (0, 0)),
                  pl.BlockSpec((None, D, D_FF_SHARD), lambda i, j: (j, 0, 0)),
                  pl.BlockSpec((None, D_FF_SHARD, D), lambda i, j: (j, 0, 0))],
        out_specs=[pl.BlockSpec((TMF, D), lambda i, j: (i, 0)), pl.BlockSpec((TMF, D_FF_SHARD), lambda i, j: (i, j)),
                   pl.BlockSpec((TMF, D), lambda i, j: (i, 0))],
        out_shape=[SDS((T, D), bf16), SDS((T, NDEV * D_FF_SHARD), bf16), SDS((T, D), f32)],
        scratch_shapes=[pltpu.VMEM((TMF, D), f32)],
        compiler_params=_cp(dimension_semantics=("arbitrary", "arbitrary")),
    )(h, g, w_in, w_out)


def mlp_bwd(h, hm, r, g, dout, dout_b, w_in, w_out, layer, bg=()):
    last = NDEV - 1

    def body(h_ref, hm_ref, r_ref, g_ref, do_ref, dob_ref, wi_ref, wo_ref, dh_ref, dwi_ref, dwo_ref, dg_ref,
             dhm, awi, awo):
        j = pl.program_id(0)
        i = pl.program_id(1)
        rows = pl.ds(pl.multiple_of(i * TM, TM), TM)

        @pl.when(i == 0)
        def _():
            awi[...] = jnp.zeros_like(awi)
            awo[...] = jnp.zeros_like(awo)

        hm_ = hm_ref[...]
        dob = dob_ref[...]
        r = r_ref[...].astype(f32)
        dz = (_dot_nt(dob, wo_ref[...]) * (2.0 * r)).astype(bf16)
        awo[...] += _dot_tn((r * r).astype(bf16), dob)
        awi[...] += _dot_tn(hm_, dz)
        part = _dot_nt(dz, wi_ref[...])

        @pl.when(j == 0)
        def _():
            dhm[rows, :] = part

        @pl.when(j > 0)
        def _():
            dhm[rows, :] += part

        @pl.when(i == NT - 1)
        def _():
            dwi_ref[...] = awi[...].astype(bf16)
            dwo_ref[...] = awo[...].astype(bf16)

        @pl.when(j == last)
        def _():
            @pl.when(i == 0)
            def _():
                dg_ref[...] = jnp.zeros_like(dg_ref)
            dx, dgt = _rms_bwd(h_ref[...], g_ref[...], dhm[rows, :])
            dh_ref[...] = do_ref[...] + dx
            dg_ref[...] += _colsum8(dgt)

    late = lambda j, i: (jnp.where(j == last, i, 0), 0)
    return _call(
        bg, body, name=f"mlp_bwd{layer}", grid=(NDEV, NT),
        in_specs=[pl.BlockSpec((TM, D), late),
                  pl.BlockSpec((TM, D), lambda j, i: (i, 0)),
                  pl.BlockSpec((TM, D_FF_SHARD), lambda j, i: (i, j)),
                  pl.BlockSpec((1, D), lambda j, i: (0, 0)),
                  pl.BlockSpec((TM, D), late),
                  pl.BlockSpec((TM, D), lambda j, i: (i, 0)),
                  pl.BlockSpec((None, D, D_FF_SHARD), lambda j, i: (j, 0, 0)),
                  pl.BlockSpec((None, D_FF_SHARD, D), lambda j, i: (j, 0, 0))],
        out_specs=[pl.BlockSpec((TM, D), late),
                   pl.BlockSpec((None, D, D_FF_SHARD), lambda j, i: (j, 0, 0)),
                   pl.BlockSpec((None, D_FF_SHARD, D), lambda j, i: (j, 0, 0)),
                   pl.BlockSpec((8, D), lambda j, i: (0, 0))],
        out_shape=[SDS((T, D), f32), SDS((NDEV, D, D_FF_SHARD), bf16), SDS((NDEV, D_FF_SHARD, D), bf16),
                   SDS((8, D), f32)],
        scratch_shapes=[pltpu.VMEM((T, D), f32), pltpu.VMEM((D, D_FF_SHARD), f32), pltpu.VMEM((D_FF_SHARD, D), f32)],
        compiler_params=_cp(dimension_semantics=("arbitrary", "arbitrary")),
    )(h, hm, r, g, dout, dout_b, w_in, w_out)


def _spread4():
    r = lax.broadcasted_iota(jnp.int32, (256, D), 0)
    c = lax.broadcasted_iota(jnp.int32, (256, D), 1)
    return ((c // 256 == r // HEAD_DIM) & (c % HEAD_DIM == r % HEAD_DIM)).astype(bf16)


def attn_pre(h, g_kv, g_mix, wkv, bkv, spread, wq, bq):
    def body(h_ref, gkv_ref, gm_ref, wkv_ref, bkv_ref, sp_ref, wq_ref, bq_ref, kvn_ref, hn_ref, k_ref, v_ref, q_ref):
        h_ = h_ref[...]
        kvn = _rms(h_, gkv_ref[...])[0].astype(bf16)
        hn = _rms(h_, gm_ref[...])[0].astype(bf16)
        kvn_ref[...] = kvn
        hn_ref[...] = hn
        kv = (_dot(kvn, wkv_ref[...]) + bkv_ref[...]).astype(bf16)
        k_ref[...] = _dot(kv[:, :256], sp_ref[...]).astype(bf16)
        v_ref[...] = _dot(kv[:, 256:], sp_ref[...]).astype(bf16)
        q_ref[...] = (_dot(hn, wq_ref[...]) + bq_ref[...]).astype(bf16)

    return pl.pallas_call(
        body, name="attn_pre", grid=(NT,),
        in_specs=[_tile(), _full((1, D)), _full((1, D)), _full((D, 512)), _full((1, 512)), _full((256, D)),
                  _full((D, D)), _full((1, D))],
        out_specs=[_tile()] * 5,
        out_shape=[SDS((T, D), bf16)] * 5,
        compiler_params=_cp(dimension_semantics=("arbitrary",)),
    )(h, g_kv, g_mix, wkv, bkv, spread, wq, bq)


def _attn_specs():
    cur = pl.BlockSpec((TM, 256), lambda j, n: (n, j))
    prev = pl.BlockSpec((BLK, 256), lambda j, n: (jnp.maximum(n * (TM // BLK) - 1, 0), j))
    return cur, prev


def _head_mask(g):
    lane = lax.broadcasted_iota(jnp.int32, (1, 256), 1)
    return (lane >= g * HEAD_DIM) & (lane < (g + 1) * HEAD_DIM)


def _stack_heads(t):
    return jnp.concatenate([jnp.where(_head_mask(g), t, 0) for g in range(Q_PER_KV)], axis=0)


def _unstack_heads(t):
    out = jnp.where(_head_mask(0), t[0:BLK], 0.0)
    for g in range(1, Q_PER_KV):
        out = out + jnp.where(_head_mask(g), t[g * BLK:(g + 1) * BLK], 0.0)
    return out


def _from_prev():
    rows = Q_PER_KV * BLK
    qi = jnp.bitwise_and(lax.broadcasted_iota(jnp.int32, (rows, BLK), 0), BLK - 1)
    return lax.broadcasted_iota(jnp.int32, (rows, BLK), 1) > qi


def _attn_probs(qs, kp, kc, sinks, first):
    rows = Q_PER_KV * BLK
    up = _from_prev()
    s_prev = jnp.where(first, -jnp.inf, _dot_nt(qs, kp))
    s = jnp.where(up, s_prev, _dot_nt(qs, kc)) * (1.0 / math.sqrt(HEAD_DIM))
    rb = lax.broadcasted_iota(jnp.int32, (rows, 1), 0)
    sink = jnp.where(rb < BLK, sinks[0], jnp.where(rb < 2 * BLK, sinks[1], jnp.where(rb < 3 * BLK, sinks[2], sinks[3])))
    m = jnp.maximum(jnp.max(s, axis=-1, keepdims=True), sink)
    p = jnp.exp(s - m)
    ps = jnp.exp(sink - m)
    denom = jnp.sum(p, axis=-1, keepdims=True) + ps
    return p / denom, ps / denom


def _window_blocks(b, n, kc_ref, kp_ref, vc_ref, vp_ref):
    if b == 0:
        return kp_ref[...], kc_ref[0:BLK, :], vp_ref[...], vc_ref[0:BLK, :], n == 0
    lo, hi = pl.ds((b - 1) * BLK, BLK), pl.ds(b * BLK, BLK)
    return kc_ref[lo, :], kc_ref[hi, :], vc_ref[lo, :], vc_ref[hi, :], False


def _split_window(t):
    up = _from_prev()
    return jnp.where(up, t, 0.0).astype(bf16), jnp.where(up, 0.0, t).astype(bf16)


def attn_core_fwd(q, k4, v4, sinks, bg=()):
    nb = TM // BLK

    def body(sink_ref, q_ref, kc_ref, kp_ref, vc_ref, vp_ref, o_ref):
        j = pl.program_id(0)
        n = pl.program_id(1)
        sk = [sink_ref[j * Q_PER_KV + g] for g in range(Q_PER_KV)]
        for b in range(nb):
            qb = q_ref[b * BLK:(b + 1) * BLK, :]
            kp, kc, vp, vc, first = _window_blocks(b, n, kc_ref, kp_ref, vc_ref, vp_ref)
            a, _ = _attn_probs(_stack_heads(qb), kp, kc, sk, first)
            a_up, a_lo = _split_window(a)
            o_ref[b * BLK:(b + 1) * BLK, :] = _unstack_heads(_dot(a_up, vp) + _dot(a_lo, vc)).astype(bf16)

    cur, prev = _attn_specs()
    return _call(
        bg, body, name="attn_core_fwd", grid=(N_KV, NT),
        in_specs=[pl.BlockSpec(memory_space=pltpu.SMEM), cur, cur, prev, cur, prev],
        out_specs=cur, out_shape=SDS((T, D), bf16),
        compiler_params=_cp(dimension_semantics=("arbitrary", "arbitrary")),
    )(sinks, q, k4, k4, v4, v4)


def attn_post(h, o, wo, bo):
    def body(h_ref, o_ref, w_ref, b_ref, out_ref):
        out_ref[...] = h_ref[...] + _dot(o_ref[...], w_ref[...]) + b_ref[...]

    return pl.pallas_call(
        body, name="attn_post", grid=(NT,), in_specs=[_tile(), _tile(), _full((D, D)), _full((1, D))],
        out_specs=_tile(), out_shape=SDS((T, D), f32), compiler_params=_cp(dimension_semantics=("arbitrary",)),
    )(h, o, wo, bo)


def attn_bwd_pre(dh, o, wo, bg=()):
    def body(dh_ref, o_ref, w_ref, do_ref, dw_ref, db_ref, acc):
        i = pl.program_id(0)

        @pl.when(i == 0)
        def _():
            acc[...] = jnp.zeros_like(acc)
            db_ref[...] = jnp.zeros_like(db_ref)

        dh_ = dh_ref[...]
        dhb = dh_.astype(bf16)
        do_ref[...] = _dot_nt(dhb, w_ref[...]).astype(bf16)
        acc[...] += _dot_tn(o_ref[...], dhb)
        db_ref[...] += _colsum8(dh_)

        @pl.when(i == NT - 1)
        def _():
            dw_ref[...] = acc[...].astype(bf16)

    return _call(
        bg, body, name="attn_bwd_pre", grid=(NT,), in_specs=[_tile(), _tile(), _full((D, D))],
        out_specs=[_tile(), _full((D, D)), _full((8, D))],
        out_shape=[SDS((T, D), bf16), SDS((D, D), bf16), SDS((8, D), f32)],
        scratch_shapes=[pltpu.VMEM((D, D), f32)],
        compiler_params=_cp(dimension_semantics=("arbitrary",)),
    )(dh, o, wo)


def attn_core_bwd(q, do, k4, v4, sinks, bg=()):
    nb = TM // BLK

    def body(sink_ref, q_ref, do_ref, kc_ref, kp_ref, vc_ref, vp_ref, dq_ref, dk_ref, dv_ref, ds_ref):
        j = pl.program_id(0)
        n = pl.program_id(1)

        @pl.when(n == 0)
        def _():
            dk_ref[...] = jnp.zeros_like(dk_ref)
            dv_ref[...] = jnp.zeros_like(dv_ref)
            ds_ref[...] = jnp.zeros_like(ds_ref)

        lane8 = lax.broadcasted_iota(jnp.int32, (8, 128), 1)
        row8 = lax.broadcasted_iota(jnp.int32, (8, 128), 0)
        sk = [sink_ref[j * Q_PER_KV + g] for g in range(Q_PER_KV)]
        for b in range(nb):
            qs = _stack_heads(q_ref[b * BLK:(b + 1) * BLK, :])
            dos = _stack_heads(do_ref[b * BLK:(b + 1) * BLK, :])
            kp, kc, vp, vc, first = _window_blocks(b, n, kc_ref, kp_ref, vc_ref, vp_ref)
            a, asink = _attn_probs(qs, kp, kc, sk, first)
            dp = jnp.where(_from_prev(), _dot_nt(dos, vp), _dot_nt(dos, vc))
            dd = jnp.sum(a * dp, axis=-1, keepdims=True)
            ds_up, ds_lo = _split_window(a * (dp - dd) * (1.0 / math.sqrt(HEAD_DIM)))
            a_up, a_lo = _split_window(a)
            t = asink * dd
            for g in range(Q_PER_KV):
                dsink = -jnp.sum(t[g * BLK:(g + 1) * BLK], axis=0, keepdims=True)
                ds_ref[...] += jnp.where((lane8 == g) & (row8 == 0), jnp.broadcast_to(dsink, (8, 128)), 0.0)
            dq_ref[b * BLK:(b + 1) * BLK, :] = _unstack_heads(_dot(ds_up, kp) + _dot(ds_lo, kc))
            cur = pl.ds(pl.multiple_of(n * TM + b * BLK, BLK), BLK)
            dk_ref[cur, :] += _dot_tn(ds_lo, qs)
            dv_ref[cur, :] += _dot_tn(a_lo, dos)
            if b == 0:
                @pl.when(n > 0)
                def _():
                    prv = pl.ds(pl.multiple_of(n * TM - BLK, BLK), BLK)
                    dk_ref[prv, :] += _dot_tn(ds_up, qs)
                    dv_ref[prv, :] += _dot_tn(a_up, dos)
            else:
                prv = pl.ds(pl.multiple_of(n * TM + (b - 1) * BLK, BLK), BLK)
                dk_ref[prv, :] += _dot_tn(ds_up, qs)
                dv_ref[prv, :] += _dot_tn(a_up, dos)

    cur, prev = _attn_specs()
    col = pl.BlockSpec((T, 256), lambda j, n: (0, j))
    return _call(
        bg, body, name="attn_core_bwd", grid=(N_KV, NT),
        in_specs=[pl.BlockSpec(memory_space=pltpu.SMEM), cur, cur, cur, prev, cur, prev],
        out_specs=[cur, col, col, pl.BlockSpec((None, 8, 128), lambda j, n: (j, 0, 0))],
        out_shape=[SDS((T, D), f32), SDS((T, D), f32), SDS((T, D), f32), SDS((N_KV, 8, 128), f32)],
        compiler_params=_cp(dimension_semantics=("arbitrary", "arbitrary")),
    )(sinks, q, do, k4, k4, v4, v4)


def attn_bwd_q(h, dh, dq, hn, g_mix, wq):
    def body(h_ref, dh_ref, dq_ref, hn_ref, gm_ref, wq_ref, out_ref, dwq_ref, dbq_ref, dgm_ref, aq):
        i = pl.program_id(0)

        @pl.when(i == 0)
        def _():
            aq[...] = jnp.zeros_like(aq)
            dbq_ref[...] = jnp.zeros_like(dbq_ref)
            dgm_ref[...] = jnp.zeros_like(dgm_ref)

        dq_ = dq_ref[...]
        dqb = dq_.astype(bf16)
        aq[...] += _dot_tn(hn_ref[...], dqb)
        dbq_ref[...] += _colsum8(dq_)
        dx, dg = _rms_bwd(h_ref[...], gm_ref[...], _dot_nt(dqb, wq_ref[...]))
        out_ref[...] = dh_ref[...] + dx
        dgm_ref[...] += _colsum8(dg)

        @pl.when(i == NT - 1)
        def _():
            dwq_ref[...] = aq[...].astype(bf16)

    vec = _full((8, D))
    mat = _full((D, D))
    return pl.pallas_call(
        body, name="attn_bwd_q", grid=(NT,),
        in_specs=[_tile()] * 4 + [_full((1, D)), mat],
        out_specs=[_tile(), mat, vec, vec],
        out_shape=[SDS((T, D), f32), SDS((D, D), bf16), SDS((8, D), f32), SDS((8, D), f32)],
        scratch_shapes=[pltpu.VMEM((D, D), f32)],
        compiler_params=_cp(dimension_semantics=("arbitrary",)),
    )(h, dh, dq, hn, g_mix, wq)


def attn_bwd_kv(h, dh, dk4, dv4, kvn, g_kv, wkv, spread):
    def body(h_ref, dh_ref, dk_ref, dv_ref, kvn_ref, gkv_ref, wkv_ref, sp_ref, out_ref, outb_ref, dw_ref, db_ref,
             dgkv_ref, acc):
        i = pl.program_id(0)

        @pl.when(i == 0)
        def _():
            for r in (acc, db_ref, dgkv_ref):
                r[...] = jnp.zeros_like(r)

        dkv = jnp.concatenate([_dot_nt(dk_ref[...].astype(bf16), sp_ref[...]),
                               _dot_nt(dv_ref[...].astype(bf16), sp_ref[...])], axis=1)
        dkvb = dkv.astype(bf16)
        acc[...] += _dot_tn(kvn_ref[...], dkvb)
        db_ref[...] += _colsum8(dkv)
        dx, dg = _rms_bwd(h_ref[...], gkv_ref[...], _dot_nt(dkvb, wkv_ref[...]))
        out = dh_ref[...] + dx
        out_ref[...] = out
        outb_ref[...] = out.astype(bf16)
        dgkv_ref[...] += _colsum8(dg)

        @pl.when(i == NT - 1)
        def _():
            dw_ref[...] = acc[...].astype(bf16)

    return pl.pallas_call(
        body, name="attn_bwd_kv", grid=(NT,),
        in_specs=[_tile()] * 5 + [_full((1, D)), _full((D, 512)), _full((256, D))],
        out_specs=[_tile(), _tile(), _full((D, 512)), _full((8, 512)), _full((8, D))],
        out_shape=[SDS((T, D), f32), SDS((T, D), bf16), SDS((D, 512), bf16), SDS((8, 512), f32), SDS((8, D), f32)],
        scratch_shapes=[pltpu.VMEM((D, 512), f32)],
        compiler_params=_cp(dimension_semantics=("arbitrary",)),
    )(h, dh, dk4, dv4, kvn, g_kv, wkv, spread)


def final_loss(h, g, target):
    def body(h_ref, g_ref, t_ref, loss_ref, dh_ref, dhb_ref, dg_ref):
        i = pl.program_id(0)

        @pl.when(i == 0)
        def _():
            loss_ref[...] = jnp.zeros_like(loss_ref)
            dg_ref[...] = jnp.zeros_like(dg_ref)

        h_ = h_ref[...]
        g_ = g_ref[...]
        y, _ = _rms(h_, g_)
        diff = y - t_ref[...]
        per_tok = jnp.mean(diff * diff, axis=-1, keepdims=True)
        tot = 0.5 * jnp.sum(per_tok, axis=0, keepdims=True)
        lane = lax.broadcasted_iota(jnp.int32, (8, 128), 1)
        row = lax.broadcasted_iota(jnp.int32, (8, 128), 0)
        loss_ref[...] += jnp.where((lane == 0) & (row == 0), jnp.broadcast_to(tot, (8, 128)), 0.0)
        dx, dgt = _rms_bwd(h_, g_, diff * (1.0 / D))
        dh_ref[...] = dx
        dhb_ref[...] = dx.astype(bf16)
        dg_ref[...] += _colsum8(dgt)

    return pl.pallas_call(
        body, name="final_loss", grid=(NT,), in_specs=[_tile(), _full((1, D)), _tile()],
        out_specs=[_full((8, 128)), _tile(), _tile(), _full((8, D))],
        out_shape=[SDS((8, 128), f32), SDS((T, D), f32), SDS((T, D), bf16), SDS((8, D), f32)],
        compiler_params=_cp(dimension_semantics=("arbitrary",)),
    )(h, g, target)


def _to_chunked(a):
    return a.reshape(S5_CH, S5_STEPS, a.shape[-1]).transpose(1, 0, 2).reshape(T, a.shape[-1])


def _from_chunked(a):
    return a.reshape(S5_STEPS, S5_CH, a.shape[-1]).transpose(1, 0, 2).reshape(T, a.shape[-1])


def _rep4(w):
    return jnp.broadcast_to(w.reshape(w.shape[0], N_KV, 1, HEAD_DIM), (w.shape[0], N_KV, Q_PER_KV, HEAD_DIM)).reshape(
        w.shape[0], N_KV * Q_PER_KV * HEAD_DIM)


def _fold4(w):
    return w.reshape(w.shape[0], N_KV, Q_PER_KV, HEAD_DIM).sum(axis=2).reshape(w.shape[0], N_KV * HEAD_DIM)


def fwd_bwd(x, target, p, shards, opt, core, chip):
    row = lambda v: v.reshape(1, -1)
    (lam, bm, cm), prep_vjp = jax.vjp(s5_discretize, p["s5_a_re"][0], p["s5_a_im"][0], p["s5_log_dt"][0],
                                      p["s5_b_re"][0], p["s5_b_im"][0], p["s5_c_re"][0], p["s5_c_im"][0])
    bmb, cmb = bm.astype(bf16), cm.astype(bf16)
    lam = jnp.concatenate([lam, lam * jnp.array([1.0, -1.0], f32).reshape(1, 2, 1, 1)], axis=1)
    g_mix0, g_mix1 = row(p["norm_mix"][0]), row(p["norm_mix"][1])
    g_mlp0, g_mlp1 = row(p["norm_mlp"][0]), row(p["norm_mlp"][1])
    g_kv, g_fin = row(p["norm_kv"]), row(p["norm_final"])
    bq, bo = p["b_q"], p["b_o"]
    bkv = row(p["b_kv"])
    spread = _spread4()
    sinks = p["sinks"].reshape(16)

    def reduce_pairs(names, bg):
        return [add_pairs(g, r, core, f"add_pairs_{n}") for n, g, r in zip(names, bg.arrs, bg.result)]

    xp = _to_chunked(x)
    hn0 = s5_pre(xp, g_mix0)
    ga = BgGather([shards["s5_w_glu"], shards["vecs"], shards["w_in0"], shards["w_out0"]], mids=(0.75, 1.0))
    ys = s5_core_fwd(hn0, bmb, lam, cmb, bg=[ga])
    wglu, gvec, win0, wout0 = ga.result
    d_skip = gvec[:, 0, :128].reshape(1, D)
    bglu = gvec[:, 0, 128:].reshape(1, 2 * D)
    y, z, h1 = s5_post(ys, xp, g_mix0, d_skip, wglu, bglu)
    gc = BgGather([shards["w_kv"], shards["w_q"], shards["w_o"], shards["w_in1"]], mids=(0.8, 1.0))
    hm0, r0, h2p = mlp_fwd(h1, g_mlp0, win0, wout0, 0, bg=[gc])
    wkv, wq, wo, win1 = gc.result
    wkv, wq, wo = wkv.reshape(D, 512), wq.reshape(D, D), wo.reshape(D, D)
    h2 = _from_chunked(h2p)
    kvn, hn1, k4, v4, q = attn_pre(h2, g_kv, g_mix1, wkv, bkv, spread, wq, bq)
    gd = BgGather([shards["w_out1"]], mids=(0.94, 1.0))
    o = attn_core_fwd(q, k4, v4, sinks, bg=[gd])
    wout1, = gd.result
    h3 = attn_post(h2, o, wo, bo)
    hm1, r1, h4 = mlp_fwd(h3, g_mlp1, win1, wout1, 1)
    loss, dh4, dh4b, dg_fin = final_loss(h4, g_fin, target)

    big = {}
    rider = lambda n, part, r2, **kw: AdamRider(*opt[n], part, r2, **kw)
    dh3, dwin1, dwout1, dg_mlp1 = mlp_bwd(h3, hm1, r1, g_mlp1, dh4, dh4b, win1, wout1, 1)
    pa = BgPair([dwin1, dwout1])
    do, dwo, dbo = attn_bwd_pre(dh3, o, wo, bg=[pa])
    p_in1, p_out1 = reduce_pairs(["w_in1", "w_out1"], pa)
    ca = BgChips([p_in1])
    dq, dk4, dv4, dsink = attn_core_bwd(q, do, k4, v4, sinks, bg=[ca])
    dh2, dwq, dbq, dg_mix1 = attn_bwd_q(h2, dh3, dq, hn1, g_mix1, wq)
    dh2, dh2b, dwkv, dbkv, dg_kv = attn_bwd_kv(h2, dh2, dk4, dv4, kvn, g_kv, wkv, spread)
    pb = BgPair([dwkv.reshape(NDEV, 128, 512), dwq.reshape(NDEV, 128, D), dwo.reshape(NDEV, 128, D)])
    ca2 = BgChips([p_out1])
    dh2p, dh2pb = _to_chunked(dh2), _to_chunked(dh2b)
    dh1, dwin0, dwout0, dg_mlp0 = mlp_bwd(h1, hm0, r0, g_mlp0, dh2p, dh2pb, win0, wout0, 0, bg=[pb, ca2])
    cb = BgChips(reduce_pairs(["w_kv", "w_q", "w_o"], pb))
    pc = BgPair([dwin0, dwout0])
    dy, dwglu, dbglu = s5_post_bwd(dh1, y, z, wglu, bg=[cb, pc])
    cc = BgChips(reduce_pairs(["w_in0", "w_out0"], pc))
    pd = BgPair([dwglu])
    a_in1 = rider("w_mlp_in", p_in1, ca.result[0], layer=1)
    a_out1 = rider("w_mlp_out", p_out1, ca2.result[0], layer=1)
    a_attn = [rider(n, part, r2) for n, part, r2 in zip(("w_kv", "w_q", "w_o"), cb.arrs, cb.result)]
    du, dbm, dcmt, dlam = s5_core_bwd(hn0, dy, bmb, lam, cmb, bg=[cc, pd, a_in1, a_out1] + a_attn)
    big["w_kv"], big["w_q"], big["w_o"] = [a.result for a in a_attn]
    cd = BgChips(reduce_pairs(["s5_w_glu"], pd))
    a_in0 = rider("w_mlp_in", cc.arrs[0], cc.result[0], layer=0, prev=a_in1.result)
    a_out0 = rider("w_mlp_out", cc.arrs[1], cc.result[1], layer=0, prev=a_out1.result)
    dxp, dg_mix0, dd = s5_pre_bwd(xp, g_mix0, du, dy, d_skip, dh1, bg=[cd, a_in0, a_out0])
    big["w_mlp_in"], big["w_mlp_out"] = a_in0.result, a_out0.result
    big["s5_w_glu"] = adam_big(*opt["s5_w_glu"], cd.arrs[0], cd.result[0], chip, "adam_s5_w_glu")
    grad_x = _from_chunked(dxp)
    da_re, da_im, dlog_dt, db_re, db_im, dc_re, dc_im = prep_vjp((dlam, dbm, dcmt.transpose(0, 2, 1)))

    def lanes(v_):
        v_ = v_.reshape(1, -1)
        return jnp.pad(v_, ((0, 0), (0, D - v_.shape[1])))

    small = jnp.concatenate([
        dg_mix0[0:1], dg_mix1[0:1], dg_mlp0[0:1], dg_mlp1[0:1], dg_kv[0:1], dg_fin[0:1], dd[0:1], dbq[0:1], dbo[0:1],
        dbglu[0:1].reshape(2, D), lanes(dbkv[0:1]),
        lanes(dsink[:, 0, :Q_PER_KV]), lanes(dlog_dt), lanes(loss[0:1, 0:1]), jnp.zeros((1, D), f32),
        da_re.reshape(4, D), da_im.reshape(4, D),
        db_re.transpose(0, 2, 1).reshape(64, D), db_im.transpose(0, 2, 1).reshape(64, D),
        dc_re.reshape(64, D), dc_im.reshape(64, D)], axis=0)
    return loss, grad_x, small, big


_ANY = pl.BlockSpec(memory_space=pl.ANY)


def _pos():
    return lax.axis_index("x"), lax.axis_index("y"), lax.axis_index("c")


def _other_chips(x, y):
    return [(1 - x, y), (x, 1 - y), (1 - x, 1 - y)]


def all_gather(arrs):
    n = len(arrs)

    def body(*refs):
        ins, outs = refs[:n], refs[n:2 * n]
        send_sems, recv_sems, local_sems = refs[2 * n:]
        x, y, c = _pos()
        me, sib = (x, y, c), (x, y, 1 - c)
        chips = _other_chips(x, y)

        def copy(a, k, block, to, src=None):
            dst = outs[a].at[4 * block[0] + 2 * block[1] + block[2]]
            return pltpu.make_async_remote_copy(
                src_ref=dst if src is None else src, dst_ref=dst, send_sem=send_sems.at[a, k],
                recv_sem=recv_sems.at[a, k], device_id=to, device_id_type=MESH)

        mine = [pltpu.make_async_copy(ins[a], outs[a].at[4 * x + 2 * y + c], local_sems.at[a]) for a in range(n)]
        for cp in mine:
            cp.start()
        first = []
        for a in range(n):
            first.append(copy(a, 0, me, sib, src=ins[a]))
            first += [copy(a, 1 + j, me, (*chip, c), src=ins[a]) for j, chip in enumerate(chips)]
        for cp in first:
            cp.start()
        passed = []
        for j, chip in enumerate(chips):
            for a in range(n):
                copy(a, 1 + j, (*chip, c), me).wait_recv()
                cp = copy(a, 4 + j, (*chip, c), sib)
                cp.start()
                passed.append(cp)
        for a in range(n):
            copy(a, 0, sib, me).wait_recv()
            for j, chip in enumerate(chips):
                copy(a, 4 + j, (*chip, 1 - c), me).wait_recv()
        for cp in first + passed:
            cp.wait_send()
        for cp in mine:
            cp.wait()

    return pl.pallas_call(
        body, name="all_gather", in_specs=[_ANY] * n, out_specs=[_ANY] * n,
        out_shape=[SDS((NDEV,) + a.shape, a.dtype) for a in arrs],
        scratch_shapes=[pltpu.SemaphoreType.DMA((n, 7)), pltpu.SemaphoreType.DMA((n, 7)),
                        pltpu.SemaphoreType.DMA((n,))],
    )(*arrs)


def rs_pair(grads):
    n = len(grads)

    def body(*refs):
        ins, outs = refs[:n], refs[n:2 * n]
        send_sems, recv_sems = refs[2 * n:]
        x, y, c = _pos()
        cps = []
        for a in range(n):
            for k in range(4):
                cps.append(pltpu.make_async_remote_copy(
                    src_ref=ins[a].at[2 * k + 1 - c], dst_ref=outs[a].at[k], send_sem=send_sems.at[a, k],
                    recv_sem=recv_sems.at[a, k], device_id=(x, y, 1 - c), device_id_type=MESH))
        for cp in cps:
            cp.start()
        for cp in cps:
            cp.wait_recv()
        for cp in cps:
            cp.wait_send()

    return pl.pallas_call(
        body, name="rs_pair", in_specs=[_ANY] * n, out_specs=[_ANY] * n,
        out_shape=[SDS((4,) + g.shape[1:], g.dtype) for g in grads],
        scratch_shapes=[pltpu.SemaphoreType.DMA((n, 4)), pltpu.SemaphoreType.DMA((n, 4))],
    )(*grads)


def rs_chips(parts):
    n = len(parts)

    def body(*refs):
        ins, outs = refs[:n], refs[n:2 * n]
        send_sems, recv_sems = refs[2 * n:]
        x, y, c = _pos()
        cps = []
        for a in range(n):
            for r, (px, py) in enumerate(_other_chips(x, y)):
                cps.append(pltpu.make_async_remote_copy(
                    src_ref=ins[a].at[2 * px + py], dst_ref=outs[a].at[r], send_sem=send_sems.at[a, r],
                    recv_sem=recv_sems.at[a, r], device_id=(px, py, c), device_id_type=MESH))
        for cp in cps:
            cp.start()
        for cp in cps:
            cp.wait_recv()
        for cp in cps:
            cp.wait_send()

    return pl.pallas_call(
        body, name="rs_chips", in_specs=[_ANY] * n, out_specs=[_ANY] * n,
        out_shape=[SDS((3,) + g.shape[1:], g.dtype) for g in parts],
        scratch_shapes=[pltpu.SemaphoreType.DMA((n, 3)), pltpu.SemaphoreType.DMA((n, 3))],
    )(*parts)


def _row_tile(r, c):
    return min(r, max(8, (512 * 1024) // c))


def add_pairs(g, r1, core, name):
    _, R, C = g.shape
    tr = _row_tile(R, C)

    def body(core_ref, g_ref, r_ref, o_ref):
        o_ref[...] = (g_ref[...].astype(f32) + r_ref[...].astype(f32)).astype(bf16)

    return pl.pallas_call(
        body, name=name, out_shape=SDS((4, R, C), bf16),
        grid_spec=pltpu.PrefetchScalarGridSpec(
            num_scalar_prefetch=1, grid=(4, R // tr),
            in_specs=[pl.BlockSpec((None, tr, C), lambda k, i, core: (2 * k + core[0], i, 0)),
                      pl.BlockSpec((None, tr, C), lambda k, i, core: (k, i, 0))],
            out_specs=pl.BlockSpec((None, tr, C), lambda k, i, core: (k, i, 0))),
        compiler_params=_cp(dimension_semantics=("arbitrary", "arbitrary")),
    )(core, g, r1)


def _adamw(w, g, m, v):
    m = ADAM_B1 * m + (1.0 - ADAM_B1) * g
    v = ADAM_B2 * v + (1.0 - ADAM_B2) * (g * g)
    m_hat = m / (1.0 - ADAM_B1 ** ADAM_STEP)
    v_hat = v / (1.0 - ADAM_B2 ** ADAM_STEP)
    delta = -ADAM_LR * (m_hat / (jnp.sqrt(v_hat) + ADAM_EPS) + ADAM_WD * w)
    return delta, m, v


def adam_big(w, m, v, part, r2, chip, name, layer=0, prev=None):
    L, R, C = w.shape
    tr = _row_tile(R, C)

    def body(chip_ref, w_ref, m_ref, v_ref, p_ref, r_ref, *rest):
        g_out, d_out, m_out, v_out = rest[-4:]
        g = p_ref[...].astype(f32) + r_ref[0].astype(f32) + r_ref[1].astype(f32) + r_ref[2].astype(f32)
        d, m_, v_ = _adamw(w_ref[...], g, m_ref[...], v_ref[...])
        g_out[...] = g
        d_out[...] = d
        m_out[...] = m_
        v_out[...] = v_

    blk = pl.BlockSpec((None, tr, C), lambda i, chip: (layer, i, 0))
    extra = [] if prev is None else list(prev)
    return pl.pallas_call(
        body, name=name, out_shape=[SDS((L, R, C), f32)] * 4,
        grid_spec=pltpu.PrefetchScalarGridSpec(
            num_scalar_prefetch=1, grid=(R // tr,),
            in_specs=[blk, blk, blk,
                      pl.BlockSpec((None, tr, C), lambda i, chip: (chip[0], i, 0)),
                      pl.BlockSpec((3, tr, C), lambda i, chip: (0, i, 0))] + [_ANY] * len(extra),
            out_specs=[blk] * 4),
        input_output_aliases={6 + k: k for k in range(len(extra))},
        compiler_params=_cp(dimension_semantics=("arbitrary",)),
    )(chip, w, m, v, part, r2, *extra)


def allreduce_small(buf, chips=None):
    shp = buf.shape
    half = (shp[0] // 16) * 8
    parts = (pl.ds(0, half), pl.ds(half, shp[0] - half))
    n_c = 0 if chips is None else len(chips.arrs)

    def body(in_ref, *refs):
        c_in, out_ref, c_out = refs[:n_c], refs[n_c], refs[n_c + 1:2 * n_c + 1]
        acc1, acc2, r0, r1, r2, send_sems, recv_sems = refs[2 * n_c + 1:2 * n_c + 8]
        c_sems = refs[2 * n_c + 8:]
        if chips is not None:
            chips.start(c_in, c_out, c_sems)
        x, y, c = _pos()
        across = [(1 - x, y, c), (x, 1 - y, c)]

        def exchange(src, rcv, dst, copies):
            cps = [pltpu.make_async_remote_copy(
                src_ref=src.at[rows], dst_ref=rcv.at[rows], send_sem=send_sems.at[k], recv_sem=recv_sems.at[k],
                device_id=peer, device_id_type=MESH) for k, rows, peer in copies]
            for cp in cps:
                cp.start()
            for cp in cps:
                cp.wait()
            dst[...] = src[...] + rcv[...]

        exchange(in_ref, r0, acc1, [(0, pl.ds(0, shp[0]), (x, y, 1 - c))])
        exchange(acc1, r1, acc2, [(1, parts[0], across[0]), (2, parts[1], across[1])])
        exchange(acc2, r2, out_ref, [(3, parts[0], across[1]), (4, parts[1], across[0])])
        if chips is not None:
            chips.finish(c_in, c_out, c_sems)

    vm = pl.BlockSpec(memory_space=pltpu.VMEM)
    res = pl.pallas_call(
        body, name="allreduce_small", in_specs=[vm] + [_ANY] * n_c, out_specs=[vm] + [_ANY] * n_c,
        out_shape=[SDS(shp, f32)] + ([] if chips is None else chips.out_shape),
        scratch_shapes=[pltpu.VMEM(shp, f32)] * 5 + [pltpu.SemaphoreType.DMA((5,)), pltpu.SemaphoreType.DMA((5,))]
        + ([] if chips is None else chips.scratch),
    )(buf, *([] if chips is None else chips.arrs))
    if chips is not None:
        chips.result = list(res[1:])
    return res[0]


SMALL_ROWS = {'norm_mix': (0, 2, D), 'norm_mlp': (2, 2, D), 'norm_kv': (4, 1, D), 'norm_final': (5, 1, D),
              's5_d': (6, 1, D), 'b_q': (7, 1, D), 'b_o': (8, 1, D), 's5_b_glu': (9, 2, D), 'b_kv': (11, 1, 512),
              'sinks': (12, 1, 16), 's5_log_dt': (13, 1, 64), 's5_a_re': (16, 4, D), 's5_a_im': (20, 4, D),
              's5_b_re': (24, 64, D), 's5_b_im': (88, 64, D), 's5_c_re': (152, 64, D), 's5_c_im': (216, 64, D)}
LOSS_ROW = 14
ROW_PARAMS = ['norm_mix', 'norm_mlp', 'norm_kv', 'norm_final', 'b_q', 'b_o', 'b_kv', 'sinks', 's5_log_dt']
SHARD_PARAMS = ['s5_d', 's5_b_glu']
S5_PARAMS = ['s5_a_re', 's5_a_im', 's5_b_re', 's5_b_im', 's5_c_re', 's5_c_im']


def adam_small(dev, gsum, s5_grads, w, m, v):
    names = ROW_PARAMS + SHARD_PARAMS + S5_PARAMS
    n_g = len(ROW_PARAMS) + len(SHARD_PARAMS)

    def body(dev_ref, gs_ref, *refs):
        pos = [0]

        def take(k):
            r = refs[pos[0]:pos[0] + k]
            pos[0] += k
            return r

        g5 = take(len(S5_PARAMS))
        wr, mr, vr = take(len(names)), take(len(names)), take(len(names))
        g_out = take(n_g)
        d_out, m_out, v_out = take(len(names)), take(len(names)), take(len(names))
        dv = dev_ref[0]
        for i, n in enumerate(names):
            if n in S5_PARAMS:
                g = g5[S5_PARAMS.index(n)][...]
            elif n in SHARD_PARAMS:
                r0, _, _ = SMALL_ROWS[n]
                ln = wr[i].shape[1]
                g = jnp.zeros((1, ln), f32)
                for k in range(NDEV):
                    off = k * ln
                    piece = gs_ref[r0 + off // D:r0 + off // D + 1, off % D:off % D + ln]
                    g = g + jnp.where(dv == k, piece, 0.0)
                g_out[i][...] = g
            else:
                r0, nr, nl = SMALL_ROWS[n]
                g = gs_ref[r0:r0 + nr, 0:nl]
                g_out[i][...] = g
            d, m_, v_ = _adamw(wr[i][...], g, mr[i][...], vr[i][...])
            d_out[i][...] = d
            m_out[i][...] = m_
            v_out[i][...] = v_

    vm = pl.BlockSpec(memory_space=pltpu.VMEM)
    ins = [s5_grads[n] for n in S5_PARAMS] + [d[n] for d in (w, m, v) for n in names]
    shapes = [SDS(w[n].shape, f32) for n in names]
    res = pl.pallas_call(
        body, name="adam_small", in_specs=[pl.BlockSpec(memory_space=pltpu.SMEM)] + [vm] * (1 + len(ins)),
        out_specs=[vm] * (n_g + 3 * len(names)), out_shape=shapes[:n_g] + shapes * 3,
        compiler_params=_cp(),
    )(dev, gsum, *ins)
    g_o = dict(zip(names[:n_g], res[:n_g]))
    rest = res[n_g:]
    k = len(names)
    return g_o, dict(zip(names, rest[:k])), dict(zip(names, rest[k:2 * k])), dict(zip(names, rest[2 * k:]))


WEIGHTS = ['norm_mix', 'norm_mlp', 'norm_kv', 'norm_final', 's5_a_re', 's5_a_im', 's5_log_dt', 's5_b_re', 's5_b_im',
           's5_c_re', 's5_c_im', 's5_d', 's5_w_glu', 's5_b_glu', 'w_kv', 'b_kv', 'w_q', 'b_q', 'sinks', 'w_o', 'b_o',
           'w_mlp_in', 'w_mlp_out']
BIG = ['s5_w_glu', 'w_kv', 'w_q', 'w_o', 'w_mlp_in', 'w_mlp_out']
BIG_2D = {'s5_w_glu': (D, 256), 'w_kv': (128, 512), 'w_q': (128, D), 'w_o': (128, D), 'w_mlp_in': (2 * D, 512),
          'w_mlp_out': (2 * 512, D)}
SHARDED_SMALL = {'s5_d': D, 's5_b_glu': 2 * D}
SMALL = [n for n in WEIGHTS if n not in BIG]
SMALL_SIZE = {'norm_mix': 2 * D, 'norm_mlp': 2 * D, 'norm_kv': D, 'norm_final': D, 's5_a_re': 4096, 's5_a_im': 4096,
              's5_log_dt': 64, 's5_b_re': 65536, 's5_b_im': 65536, 's5_c_re': 65536, 's5_c_im': 65536, 's5_d': D,
              's5_b_glu': 2 * D, 'b_kv': 512, 'b_q': D, 'sinks': 16, 'b_o': D}


def _pack(vals):
    parts = []
    for n in SMALL:
        v = vals[n].reshape(-1).astype(f32)
        parts.append(jnp.pad(v, (0, (-v.shape[0]) % 128)))
    flat = jnp.concatenate(parts)
    flat = jnp.pad(flat, (0, (-flat.shape[0]) % 1024))
    return flat.reshape(-1, 128)


def _unpack(buf):
    flat = buf.reshape(-1)
    out, off = {}, 0
    for n in SMALL:
        sz = SMALL_SIZE[n]
        out[n] = flat[off:off + sz]
        off += sz + (-sz) % 128
    return out


def kernel(x, norm_mix, norm_mlp, norm_kv, norm_final, s5_a_re, s5_a_im, s5_log_dt, s5_b_re, s5_b_im, s5_c_re, s5_c_im, s5_d, s5_w_glu, s5_b_glu, w_kv, b_kv, w_q, b_q, sinks, w_o, b_o, w_mlp_in, w_mlp_out, loss_target, m_norm_mix, m_norm_mlp, m_norm_kv, m_norm_final, m_s5_a_re, m_s5_a_im, m_s5_log_dt, m_s5_b_re, m_s5_b_im, m_s5_c_re, m_s5_c_im, m_s5_d, m_s5_w_glu, m_s5_b_glu, m_w_kv, m_b_kv, m_w_q, m_b_q, m_sinks, m_w_o, m_b_o, m_w_mlp_in, m_w_mlp_out, v_norm_mix, v_norm_mlp, v_norm_kv, v_norm_final, v_s5_a_re, v_s5_a_im, v_s5_log_dt, v_s5_b_re, v_s5_b_im, v_s5_c_re, v_s5_c_im, v_s5_d, v_s5_w_glu, v_s5_b_glu, v_w_kv, v_b_kv, v_w_q, v_b_q, v_sinks, v_w_o, v_b_o, v_w_mlp_in, v_w_mlp_out):
    w = dict(norm_mix=norm_mix, norm_mlp=norm_mlp, norm_kv=norm_kv, norm_final=norm_final, s5_a_re=s5_a_re,
             s5_a_im=s5_a_im, s5_log_dt=s5_log_dt, s5_b_re=s5_b_re, s5_b_im=s5_b_im, s5_c_re=s5_c_re, s5_c_im=s5_c_im,
             s5_d=s5_d, s5_w_glu=s5_w_glu, s5_b_glu=s5_b_glu, w_kv=w_kv, b_kv=b_kv, w_q=w_q, b_q=b_q, sinks=sinks,
             w_o=w_o, b_o=b_o, w_mlp_in=w_mlp_in, w_mlp_out=w_mlp_out)
    m = dict(norm_mix=m_norm_mix, norm_mlp=m_norm_mlp, norm_kv=m_norm_kv, norm_final=m_norm_final, s5_a_re=m_s5_a_re,
             s5_a_im=m_s5_a_im, s5_log_dt=m_s5_log_dt, s5_b_re=m_s5_b_re, s5_b_im=m_s5_b_im, s5_c_re=m_s5_c_re,
             s5_c_im=m_s5_c_im, s5_d=m_s5_d, s5_w_glu=m_s5_w_glu, s5_b_glu=m_s5_b_glu, w_kv=m_w_kv, b_kv=m_b_kv,
             w_q=m_w_q, b_q=m_b_q, sinks=m_sinks, w_o=m_w_o, b_o=m_b_o, w_mlp_in=m_w_mlp_in, w_mlp_out=m_w_mlp_out)
    v = dict(norm_mix=v_norm_mix, norm_mlp=v_norm_mlp, norm_kv=v_norm_kv, norm_final=v_norm_final, s5_a_re=v_s5_a_re,
             s5_a_im=v_s5_a_im, s5_log_dt=v_s5_log_dt, s5_b_re=v_s5_b_re, s5_b_im=v_s5_b_im, s5_c_re=v_s5_c_re,
             s5_c_im=v_s5_c_im, s5_d=v_s5_d, s5_w_glu=v_s5_w_glu, s5_b_glu=v_s5_b_glu, w_kv=v_w_kv, b_kv=v_b_kv,
             w_q=v_w_q, b_q=v_b_q, sinks=v_sinks, w_o=v_w_o, b_o=v_b_o, w_mlp_in=v_w_mlp_in, w_mlp_out=v_w_mlp_out)
    xi, yi, ci = _pos()
    dev = 4 * xi + 2 * yi + ci
    core = ci.reshape(1).astype(jnp.int32)
    chip = (2 * xi + yi).reshape(1).astype(jnp.int32)

    shards = {
        "s5_w_glu": s5_w_glu[0].astype(bf16), "w_kv": w_kv.astype(bf16), "w_q": w_q[0].astype(bf16),
        "w_o": w_o[0].astype(bf16), "w_in0": w_mlp_in[0].astype(bf16), "w_in1": w_mlp_in[1].astype(bf16),
        "w_out0": w_mlp_out[0].astype(bf16), "w_out1": w_mlp_out[1].astype(bf16),
        "vecs": jnp.broadcast_to(jnp.concatenate([s5_d, s5_b_glu], axis=1), (8, 384)),
    }
    as3d = lambda a, n: a if a.ndim == 3 and a.shape[0] == 2 else a.reshape((1,) + BIG_2D[n])
    opt = {n: (as3d(w[n], n), as3d(m[n], n), as3d(v[n], n)) for n in BIG}
    _, grad_x, grads, big = fwd_bwd(x[0], loss_target[0], {n: w[n] for n in SMALL}, shards, opt, core, chip)

    gsum = allreduce_small(grads)

    out_g, out_d, out_m, out_v = {}, {}, {}, {}
    for n in BIG:
        out_g[n], out_d[n], out_m[n], out_v[n] = [r.reshape(w[n].shape) for r in big[n]]

    loss = gsum[LOSS_ROW, 0]
    swapped = ("s5_b_re", "s5_b_im")
    swap = lambda a: a.transpose(0, 1, 3, 2)

    def kernel_side(d):
        d = {n: (d[n].reshape(1, -1) if d[n].ndim == 1 else d[n]) for n in SMALL}
        d.update({n: swap(d[n]) for n in swapped})
        return d

    s5_g = {}
    for n in S5_PARAMS:
        r0, nr, _ = SMALL_ROWS[n]
        s5_g[n] = gsum[r0:r0 + nr].reshape((1, 64, 16, 64) if n in swapped else w[n].shape)
        out_g[n] = s5_g[n]
    g_s, d_s, m_s, v_s = adam_small(dev.reshape(1).astype(jnp.int32), gsum, s5_g, kernel_side(w), kernel_side(m),
                                    kernel_side(v))
    for src, dst in ((g_s, out_g), (d_s, out_d), (m_s, out_m), (v_s, out_v)):
        dst.update(src)
    for dst in (out_g, out_d, out_m, out_v):
        for n in SMALL:
            dst[n] = (swap(dst[n]) if n in swapped else dst[n]).reshape(w[n].shape)

    return (loss, grad_x[None], *[out_g[n] for n in WEIGHTS], *[out_d[n] for n in WEIGHTS],
            *[out_m[n] for n in WEIGHTS], *[out_v[n] for n in WEIGHTS])
```

```python
import functools
import math

import jax
import jax.numpy as jnp
from jax import lax
from jax.experimental import pallas as pl
from jax.experimental.pallas import tpu as pltpu
from jax.experimental.pallas import tpu_sc as plsc

f32 = jnp.float32
bf16 = jnp.bfloat16
SDS = jax.ShapeDtypeStruct

T = 2048
D = 1024
NDEV = 8
NORM_EPS = 1e-5
S5_G, S5_C, S5_P = 64, 16, 64
S5_SUB = 8
S5_CH = 8
S5_STEPS = T // S5_CH
DT_MIN_LAMBDA = -1e-4
HEAD_DIM = 64
N_KV = 4
Q_PER_KV = 4
BLK = 128
D_FF_SHARD = 512
ADAM_LR, ADAM_B1, ADAM_B2, ADAM_EPS, ADAM_WD, ADAM_STEP = 0.001, 0.9, 0.999, 1e-08, 0.01, 10
VMEM_LIMIT = 56 * 1024 * 1024
MESH = pl.DeviceIdType.MESH


def _cp(**kw):
    return pltpu.CompilerParams(vmem_limit_bytes=VMEM_LIMIT, **kw)


def _dot(a, b):
    return jnp.dot(a, b, preferred_element_type=f32)


def _dot_nt(a, b):
    return lax.dot_general(a, b, (((1,), (1,)), ((), ())), preferred_element_type=f32)


def _dot_tn(a, b):
    return lax.dot_general(a, b, (((0,), (0,)), ((), ())), preferred_element_type=f32)


def _rms(x, g):
    r = lax.rsqrt(jnp.mean(x * x, axis=-1, keepdims=True) + NORM_EPS)
    return x * r * g, r


def _rms_bwd(x, g, dy):
    r = lax.rsqrt(jnp.mean(x * x, axis=-1, keepdims=True) + NORM_EPS)
    u = dy * g
    dx = r * u - (r * r * r) * x * jnp.mean(u * x, axis=-1, keepdims=True)
    return dx, dy * x * r


def _colsum8(v):
    s = jnp.sum(v, axis=0, keepdims=True)
    row = lax.broadcasted_iota(jnp.int32, (8, v.shape[1]), 0)
    return jnp.where(row == 0, jnp.broadcast_to(s, (8, v.shape[1])), 0.0)


def _full(shape):
    nd = len(shape)
    return pl.BlockSpec(shape, lambda *_: (0,) * nd, pipeline_mode=pl.Buffered(1))


_ANY = pl.BlockSpec(memory_space=pl.ANY)


def _pos():
    return lax.axis_index("x"), lax.axis_index("y"), lax.axis_index("c")


def _other_chips(x, y):
    return [(1 - x, y), (x, 1 - y), (1 - x, 1 - y)]


class BgGather:
    SIB, XN, YN, FWD_Y, FWD_X, SIB_X, SIB_Y, SIB_D = range(8)

    def __init__(self, arrs, mids=(0.5, 0.75)):
        n = len(arrs)
        self.arrs = list(arrs)
        self.out_shape = [SDS((NDEV,) + a.shape, a.dtype) for a in arrs]
        self.scratch = [pltpu.SemaphoreType.DMA((n, 8)), pltpu.SemaphoreType.DMA((n, 8)),
                        pltpu.SemaphoreType.DMA((n,))]
        self.mids = mids
        self.result = None

    def mid_steps(self, nsteps):
        at = lambda f: min(nsteps - 1, max(0, int(f * nsteps) - 1))
        return [(at(self.mids[0]), self.mid), (max(at(self.mids[0]), at(self.mids[1])), self.mid2)]

    def _halves(self, a):
        rows = self.arrs[a].shape[0]
        cut = rows // 2 if rows >= 32 else rows
        return (0, cut), (cut, rows - cut)

    def _copy(self, ins, outs, sems, a, k, block, to, own=False, part=None):
        slot = 4 * block[0] + 2 * block[1] + block[2]
        rows = pl.ds(0, self.arrs[a].shape[0]) if part is None else pl.ds(*self._halves(a)[part])
        dst = outs[a].at[slot, rows]
        return pltpu.make_async_remote_copy(
            src_ref=ins[a].at[rows] if own else dst, dst_ref=dst, send_sem=sems[0].at[a, k],
            recv_sem=sems[1].at[a, k], device_id=to, device_id_type=MESH)

    def _mine(self, ins, outs, sems):
        x, y, c = _pos()
        return [pltpu.make_async_copy(ins[a], outs[a].at[4 * x + 2 * y + c], sems[2].at[a])
                for a in range(len(self.arrs))]

    def _split(self, a):
        return self._halves(a)[1][1] > 0

    def _sends(self, ins, outs, sems, phase):
        x, y, c = _pos()
        me, sib, xn, yn, dg = (x, y, c), (x, y, 1 - c), (1 - x, y, c), (x, 1 - y, c), (1 - x, 1 - y, c)
        cps = []
        for a in range(len(self.arrs)):
            cp = lambda k, block, to, **kw: self._copy(ins, outs, sems, a, k, block, to, **kw)
            if phase == 0:
                cps += [cp(self.SIB, me, sib, own=True), cp(self.XN, me, xn, own=True), cp(self.YN, me, yn, own=True)]
            elif phase == 1:
                cps.append(cp(self.FWD_Y, xn, yn, part=0))
                if self._split(a):
                    cps.append(cp(self.FWD_X, yn, xn, part=1))
                cps += [cp(self.SIB_X, xn, sib), cp(self.SIB_Y, yn, sib)]
            else:
                cps.append(cp(self.SIB_D, dg, sib))
        return cps

    def _arrivals(self, ins, outs, sems, phase):
        x, y, c = _pos()
        me, xn, yn, dg = (x, y, c), (1 - x, y, c), (x, 1 - y, c), (1 - x, 1 - y, c)
        cps = []
        for a in range(len(self.arrs)):
            cp = lambda k, block, **kw: self._copy(ins, outs, sems, a, k, block, me, **kw)
            if phase == 1:
                cps += [cp(self.XN, xn), cp(self.YN, yn)]
            elif phase == 2:
                cps.append(cp(self.FWD_Y, dg, part=0))
                if self._split(a):
                    cps.append(cp(self.FWD_X, dg, part=1))
            else:
                cps += [cp(self.SIB, (x, y, 1 - c)), cp(self.SIB_X, (1 - x, y, 1 - c)),
                        cp(self.SIB_Y, (x, 1 - y, 1 - c)), cp(self.SIB_D, (1 - x, 1 - y, 1 - c))]
        return cps

    def start(self, ins, outs, sems):
        for cp in self._mine(ins, outs, sems) + self._sends(ins, outs, sems, 0):
            cp.start()

    def mid(self, ins, outs, sems):
        for cp in self._arrivals(ins, outs, sems, 1):
            cp.wait_recv()
        for cp in self._sends(ins, outs, sems, 1):
            cp.start()

    def mid2(self, ins, outs, sems):
        for cp in self._arrivals(ins, outs, sems, 2):
            cp.wait_recv()
        for cp in self._sends(ins, outs, sems, 2):
            cp.start()

    def finish(self, ins, outs, sems):
        for cp in self._arrivals(ins, outs, sems, 3):
            cp.wait_recv()
        for ph in range(3):
            for cp in self._sends(ins, outs, sems, ph):
                cp.wait_send()
        for cp in self._mine(ins, outs, sems):
            cp.wait()


def sc_gather(arrs, collective_id, name):
    n = len(arrs)
    g = BgGather(arrs)
    srcs = [jax.new_ref(a, memory_space=pltpu.MemorySpace.HBM) for a in arrs]
    dsts = [jax.empty_ref(s, memory_space=pltpu.MemorySpace.HBM) for s in g.out_shape]

    @pl.kernel(mesh=plsc.ScalarSubcoreMesh(axis_name="sequencer", num_cores=1), name=name,
               scratch_types=tuple(g.scratch), compiler_params=pltpu.CompilerParams(collective_id=collective_id))
    def launch(send_sems, recv_sems, local_sems):
        x, y, c = _pos()
        barrier = pltpu.get_barrier_semaphore()
        for peer in ((x, y, 1 - c), (1 - x, y, c), (x, 1 - y, c)):
            pl.semaphore_signal(barrier, inc=1, device_id=peer, device_id_type=MESH)
        pl.semaphore_wait(barrier, 3)
        sems = (send_sems, recv_sems, local_sems)
        for phase in (g.start, g.mid, g.mid2, g.finish):
            phase(srcs, dsts, sems)

    launch()
    return [d[...] for d in dsts]


class BgPair:
    def __init__(self, arrs):
        n = len(arrs)
        self.arrs = list(arrs)
        self.out_shape = [SDS((4,) + a.shape[1:], a.dtype) for a in arrs]
        self.scratch = [pltpu.SemaphoreType.DMA((n, 4)), pltpu.SemaphoreType.DMA((n, 4))]
        self.result = None

    def mid_steps(self, nsteps):
        return []

    def _copies(self, ins, outs, sems):
        x, y, c = _pos()
        return [pltpu.make_async_remote_copy(
            src_ref=ins[a].at[2 * k + 1 - c], dst_ref=outs[a].at[k], send_sem=sems[0].at[a, k],
            recv_sem=sems[1].at[a, k], device_id=(x, y, 1 - c), device_id_type=MESH)
            for a in range(len(self.arrs)) for k in range(4)]

    def start(self, ins, outs, sems):
        for cp in self._copies(ins, outs, sems):
            cp.start()

    def finish(self, ins, outs, sems):
        cps = self._copies(ins, outs, sems)
        for cp in cps:
            cp.wait_recv()
        for cp in cps:
            cp.wait_send()


class BgChips(BgPair):
    def __init__(self, arrs):
        n = len(arrs)
        self.arrs = list(arrs)
        self.out_shape = [SDS((3,) + a.shape[1:], a.dtype) for a in arrs]
        self.scratch = [pltpu.SemaphoreType.DMA((n, 3)), pltpu.SemaphoreType.DMA((n, 3))]
        self.result = None

    def _copies(self, ins, outs, sems):
        x, y, c = _pos()
        return [pltpu.make_async_remote_copy(
            src_ref=ins[a].at[2 * px + py], dst_ref=outs[a].at[r], send_sem=sems[0].at[a, r],
            recv_sem=sems[1].at[a, r], device_id=(px, py, c), device_id_type=MESH)
            for a in range(len(self.arrs)) for r, (px, py) in enumerate(_other_chips(x, y))]


class AdamRider:
    def __init__(self, w, m, v, part, r2, layer=0, prev=None):
        self.arrs = [w, m, v, part, r2] + list(prev or [])
        self.n_prev = len(prev or [])
        self.layer = layer
        self.out_shape = [SDS(w.shape, f32)] * 4
        self.scratch = []
        self.aliases = {5 + k: k for k in range(self.n_prev)}
        self.result = None

    def _tile(self, grid):
        assert len(grid) == 1
        _, R, C = self.arrs[0].shape
        return R // grid[0], C

    def in_specs(self, grid):
        tr, C = self._tile(grid)
        layer = self.layer
        blk = pl.BlockSpec((None, tr, C), lambda b: (layer, b, 0))
        mine = pl.BlockSpec((None, tr, C), lambda b: (2 * lax.axis_index("x") + lax.axis_index("y"), b, 0))
        return [blk, blk, blk, mine, pl.BlockSpec((3, tr, C), lambda b: (0, b, 0))] + [_ANY] * self.n_prev

    def out_specs(self, grid):
        tr, C = self._tile(grid)
        layer = self.layer
        return [pl.BlockSpec((None, tr, C), lambda b: (layer, b, 0))] * 4

    def mid_steps(self, nsteps):
        return []

    def start(self, ins, outs, sems):
        pass

    finish = start

    def step(self, ins, outs, sems):
        w_ref, m_ref, v_ref, p_ref, r_ref = ins[:5]
        g = p_ref[...].astype(f32) + r_ref[0].astype(f32) + r_ref[1].astype(f32) + r_ref[2].astype(f32)
        d, m_, v_ = _adamw(w_ref[...], g, m_ref[...], v_ref[...])
        for ref, val in zip(outs, (g, d, m_, v_)):
            ref[...] = val


def _call(bgs, body, *, name, grid, in_specs, out_specs, out_shape, scratch_shapes=(), compiler_params=None):
    single = not isinstance(out_shape, (list, tuple))
    out_specs_l = [out_specs] if single else list(out_specs)
    out_shape_l = [out_shape] if single else list(out_shape)
    bgs = [b for b in (bgs or []) if b is not None]
    n_in, n_out, n_sc = len(in_specs), len(out_shape_l), len(scratch_shapes)
    nsteps = math.prod(grid)
    b_in_specs = [b.in_specs(grid) if hasattr(b, "in_specs") else [_ANY] * len(b.arrs) for b in bgs]
    b_out_specs = [b.out_specs(grid) if hasattr(b, "out_specs") else [_ANY] * len(b.out_shape) for b in bgs]
    aliases, i_off, o_off = {}, n_in, n_out
    for b in bgs:
        aliases.update({i_off + i: o_off + o for i, o in getattr(b, "aliases", {}).items()})
        i_off, o_off = i_off + len(b.arrs), o_off + len(b.out_shape)

    def full(*refs):
        pos = [0]

        def take(k):
            r = refs[pos[0]:pos[0] + k]
            pos[0] += k
            return r

        ins = take(n_in)
        b_ins = [take(len(b.arrs)) for b in bgs]
        outs = take(n_out)
        b_outs = [take(len(b.out_shape)) for b in bgs]
        sc = take(n_sc)
        b_sc = [take(len(b.scratch)) for b in bgs]
        if bgs:
            step = pl.program_id(0)
            for d in range(1, len(grid)):
                step = step * grid[d] + pl.program_id(d)

            @pl.when(step == 0)
            def _():
                for b, i_, o_, s_ in zip(bgs, b_ins, b_outs, b_sc):
                    b.start(i_, o_, s_)

        body(*ins, *outs, *sc)
        if bgs:
            for b, i_, o_, s_ in zip(bgs, b_ins, b_outs, b_sc):
                if hasattr(b, "step"):
                    b.step(i_, o_, s_)
                for at, fn in b.mid_steps(nsteps):
                    @pl.when(step == at)
                    def _():
                        fn(i_, o_, s_)

            @pl.when(step == nsteps - 1)
            def _():
                for b, i_, o_, s_ in zip(bgs, b_ins, b_outs, b_sc):
                    b.finish(i_, o_, s_)

    def run(*args):
        res = pl.pallas_call(
            full, name=name, grid=grid,
            in_specs=list(in_specs) + [s for l in b_in_specs for s in l],
            out_specs=out_specs_l + [s for l in b_out_specs for s in l],
            out_shape=out_shape_l + [s for b in bgs for s in b.out_shape],
            scratch_shapes=list(scratch_shapes) + [s for b in bgs for s in b.scratch],
            input_output_aliases=aliases,
            compiler_params=compiler_params,
        )(*args, *[a for b in bgs for a in b.arrs])
        rest = list(res[n_out:])
        for b in bgs:
            b.result, rest = rest[:len(b.out_shape)], rest[len(b.out_shape):]
        return res[0] if single else list(res[:n_out])

    return run


def s5_discretize(a_re, a_im, log_dt, b_re, b_im, c_re, c_im):
    lam_r = jnp.minimum(a_re, DT_MIN_LAMBDA)
    lam_i = a_im
    dt = jnp.exp(log_dt)[:, None]
    e = jnp.exp(lam_r * dt)
    lbr = e * jnp.cos(lam_i * dt)
    lbi = e * jnp.sin(lam_i * dt)
    den = lam_r * lam_r + lam_i * lam_i
    cf_r = ((lbr - 1.0) * lam_r + lbi * lam_i) / den
    cf_i = (lbi * lam_r - (lbr - 1.0) * lam_i) / den
    bb_r = cf_r[:, :, None] * b_re - cf_i[:, :, None] * b_im
    bb_i = cf_r[:, :, None] * b_im + cf_i[:, :, None] * b_re
    eye = jnp.eye(8, dtype=f32)

    def blk_b(m):
        return jnp.einsum('bgpc,gh->bgchp', m.reshape(8, 8, S5_P, S5_C), eye).reshape(8, 128, 512)

    def blk_c(m):
        return jnp.einsum('bgcp,gh->bgphc', m.reshape(8, 8, S5_C, S5_P), eye).reshape(8, 512, 128)

    bm = jnp.concatenate([blk_b(bb_r), blk_b(bb_i)], axis=-1)
    cm = jnp.concatenate([blk_c(c_re), -blk_c(c_im)], axis=1)
    lam = jnp.stack([lbr.reshape(8, 512), lbi.reshape(8, 512)], axis=1)
    lam = jnp.broadcast_to(lam[:, :, None, :], (8, 2, 8, 512))
    return lam, bm, cm


def _cmul(ar, ai, br, bi):
    return ar * br - ai * bi, ar * bi + ai * br


def _shift_rows(v, k, up):
    row = lax.broadcasted_iota(jnp.int32, v.shape, 0)
    if up:
        return jnp.where(row < 8 - k, pltpu.roll(v, 8 - k, 0), 0.0)
    return jnp.where(row >= k, pltpu.roll(v, k, 0), 0.0)


def _chunk_scan(S, lr, li, reverse, aux=None):
    z = jnp.zeros((8, 512), f32)
    U = 4

    def idx(i):
        return (S5_STEPS - 1 - i) if reverse else i

    def rows_of(s):
        return pl.ds(s * 8, 8) if isinstance(s, int) else pl.ds(pl.multiple_of(s * 8, 8), 8)

    def rec(xr, xi, row):
        br = S[row, 0:512]
        bi = S[row, 512:1024]
        return lr * xr - li * xi + br, lr * xi + li * xr + bi

    def step1(i, c):
        for u in range(U):
            c = rec(c[0], c[1], rows_of(idx(i * U + u)))
        return c

    er, ei = lax.fori_loop(0, S5_STEPS // U, step1, (z, z))
    ar, ai = lr, li
    for _ in range(8):
        ar, ai = _cmul(ar, ai, ar, ai)
    cr, ci = _shift_rows(er, 1, reverse), _shift_rows(ei, 1, reverse)
    for k in (1, 2, 4):
        sr, si = _shift_rows(cr, k, reverse), _shift_rows(ci, k, reverse)
        pr, pi_ = _cmul(ar, ai, sr, si)
        cr, ci = cr + pr, ci + pi_
        ar, ai = _cmul(ar, ai, ar, ai)

    if aux is None:
        def step2(i, c):
            for u in range(U):
                row = rows_of(idx(i * U + u))
                c = rec(c[0], c[1], row)
                S[row, 0:512] = c[0]
                S[row, 512:1024] = c[1]
            return c

        lax.fori_loop(0, S5_STEPS // U, step2, (cr, ci))
        return None

    def one(s, c):
        gr0, gi0, dr, di = c
        row = rows_of(s)
        gr, gi = rec(gr0, gi0, row)
        S[row, 0:512] = gr
        S[row, 512:1024] = gi
        prow = rows_of(s - 1)
        xr = aux[prow, 0:512]
        xi = aux[prow, 512:1024]
        return gr, gi, dr + gr * xr + gi * xi, di + gi * xr - gr * xi

    def step2(i, c):
        for u in range(U):
            c = one(S5_STEPS - 1 - (i * U + u), c)
        return c

    c = lax.fori_loop(0, S5_STEPS // U - 1, step2, (cr, ci, z, z))
    for s in range(U - 1, 0, -1):
        c = one(s, c)
    gr, gi, dr, di = c
    row0 = pl.ds(0, 8)
    gr, gi = rec(gr, gi, row0)
    S[row0, 0:512] = gr
    S[row0, 512:1024] = gi
    last = pl.ds((S5_STEPS - 1) * 8, 8)
    xr = _shift_rows(aux[last, 0:512], 1, False)
    xi = _shift_rows(aux[last, 512:1024], 1, False)
    dr = dr + gr * xr + gi * xi
    di = di + gi * xr - gr * xi
    return dr, di


_ROWS = 256


def _row_loop(fn):
    def body(r, c):
        fn(pl.ds(pl.multiple_of(r * _ROWS, _ROWS), _ROWS))
        return c
    lax.fori_loop(0, T // _ROWS, body, 0)


def s5_core_fwd(hn, bm, lam, cm, bg=()):
    def body(u_ref, b_ref, lam_ref, c_ref, ys_ref, S):
        def bu(rows):
            S[rows, :] = _dot(u_ref[rows, :], b_ref[...])
        _row_loop(bu)
        _chunk_scan(S, lam_ref[0], lam_ref[1], False)

        def ys(rows):
            ys_ref[rows, :] = _dot(S[rows, :].astype(bf16), c_ref[...])
        _row_loop(ys)

    return _call(
        bg, body, name="s5_core_fwd", grid=(S5_SUB,),
        in_specs=[pl.BlockSpec((T, 128), lambda b: (0, b)),
                  pl.BlockSpec((None, 128, 1024), lambda b: (b, 0, 0)),
                  pl.BlockSpec((None, 4, 8, 512), lambda b: (b, 0, 0, 0)),
                  pl.BlockSpec((None, 1024, 128), lambda b: (b, 0, 0))],
        out_specs=pl.BlockSpec((T, 128), lambda b: (0, b)),
        out_shape=SDS((T, D), f32),
        scratch_shapes=[pltpu.VMEM((T, 1024), f32)],
        compiler_params=_cp(dimension_semantics=("arbitrary",)),
    )(hn, bm, lam, cm)


_SEG = _ROWS // S5_CH


def _scan_tile(S, lr, li, k, carry, reverse, store, aux=None):
    steps = range(k * _SEG, (k + 1) * _SEG)
    for s in (reversed(steps) if reverse else steps):
        row = pl.ds(s * 8, 8)
        xr, xi = carry[0], carry[1]
        nr = lr * xr - li * xi + S[row, 0:512]
        ni = lr * xi + li * xr + S[row, 512:1024]
        if store:
            S[row, 0:512] = nr
            S[row, 512:1024] = ni
        if aux is not None and s >= 1:
            prow = pl.ds((s - 1) * 8, 8)
            pr, pi_ = aux[prow, 0:512], aux[prow, 512:1024]
            carry = (nr, ni, carry[2] + nr * pr + ni * pi_, carry[3] + ni * pr - nr * pi_)
        elif aux is not None:
            carry = (nr, ni, carry[2], carry[3])
        else:
            carry = (nr, ni)
    return carry


def _chunk_starts(er, ei, lr, li, reverse):
    ar, ai = lr, li
    for _ in range(8):
        ar, ai = _cmul(ar, ai, ar, ai)
    cr, ci = _shift_rows(er, 1, reverse), _shift_rows(ei, 1, reverse)
    for k in (1, 2, 4):
        sr, si = _shift_rows(cr, k, reverse), _shift_rows(ci, k, reverse)
        pr, pi_ = _cmul(ar, ai, sr, si)
        cr, ci = cr + pr, ci + pi_
        ar, ai = _cmul(ar, ai, ar, ai)
    return cr, ci


def s5_core_bwd(hn, dy, bm, lam, cm, bg=()):
    nt = T // _ROWS

    def body(u_ref, dy_ref, b_ref, lam_ref, c_ref, du_ref, db_ref, dct_ref, dlam_ref, S1, S2):
        lr, li, lcr, lci = lam_ref[0], lam_ref[1], lam_ref[2], lam_ref[3]
        z = jnp.zeros((8, 512), f32)
        tile = lambda k: pl.ds(k * _ROWS, _ROWS)
        dyb = lambda k: dy_ref[tile(k), :].astype(bf16)

        c = (z, z)
        for k in range(nt):
            S1[tile(k), :] = _dot(u_ref[tile(k), :], b_ref[...])
            if k >= 1:
                c = _scan_tile(S1, lr, li, k - 1, c, False, False)
        c = _scan_tile(S1, lr, li, nt - 1, c, False, False)

        c = _chunk_starts(c[0], c[1], lr, li, False)
        dct_ref[...] = jnp.zeros_like(dct_ref)
        for k in range(nt):
            c = _scan_tile(S1, lr, li, k, c, False, True)
            if k >= 1:
                dct_ref[...] += _dot_tn(dyb(k - 1), S1[tile(k - 1), :].astype(bf16))
        dct_ref[...] += _dot_tn(dyb(nt - 1), S1[tile(nt - 1), :].astype(bf16))

        S2[tile(nt - 1), :] = _dot_nt(dyb(nt - 1), c_ref[...])
        c = (z, z)
        for k in range(nt - 1, -1, -1):
            if k >= 1:
                S2[tile(k - 1), :] = _dot_nt(dyb(k - 1), c_ref[...])
            c = _scan_tile(S2, lcr, lci, k, c, True, False)

        def dbu(k):
            gb = S2[tile(k), :].astype(bf16)
            db_ref[...] += _dot_tn(u_ref[tile(k), :], gb)
            du_ref[tile(k), :] = _dot_nt(gb, b_ref[...])

        c = _chunk_starts(c[0], c[1], lcr, lci, True) + (z, z)
        db_ref[...] = jnp.zeros_like(db_ref)
        for k in range(nt - 1, -1, -1):
            c = _scan_tile(S2, lcr, lci, k, c, True, True, aux=S1)
            if k + 1 < nt:
                dbu(k + 1)
        dbu(0)
        gr, gi, dr, di = c
        last = pl.ds((S5_STEPS - 1) * 8, 8)
        xr = _shift_rows(S1[last, 0:512], 1, False)
        xi = _shift_rows(S1[last, 512:1024], 1, False)
        dlam_ref[0] = dr + gr * xr + gi * xi
        dlam_ref[1] = di + gi * xr - gr * xi

    return _call(
        bg, body, name="s5_core_bwd", grid=(S5_SUB,),
        in_specs=[pl.BlockSpec((T, 128), lambda b: (0, b)),
                  pl.BlockSpec((T, 128), lambda b: (0, b)),
                  pl.BlockSpec((None, 128, 1024), lambda b: (b, 0, 0)),
                  pl.BlockSpec((None, 4, 8, 512), lambda b: (b, 0, 0, 0)),
                  pl.BlockSpec((None, 1024, 128), lambda b: (b, 0, 0))],
        out_specs=[pl.BlockSpec((T, 128), lambda b: (0, b)),
                   pl.BlockSpec((None, 128, 1024), lambda b: (b, 0, 0)),
                   pl.BlockSpec((None, 128, 1024), lambda b: (b, 0, 0)),
                   pl.BlockSpec((None, 2, 8, 512), lambda b: (b, 0, 0, 0))],
        out_shape=[SDS((T, D), f32), SDS((8, 128, 1024), f32), SDS((8, 128, 1024), f32), SDS((8, 2, 8, 512), f32)],
        scratch_shapes=[pltpu.VMEM((T, 1024), f32), pltpu.VMEM((T, 1024), f32)],
        compiler_params=_cp(dimension_semantics=("arbitrary",)),
    )(hn, dy, bm, lam, cm)


TM = 512
NT = T // TM


def _tile(n=D):
    return pl.BlockSpec((TM, n), lambda i: (i, 0))


def s5_pre(xp, g):
    def body(x_ref, g_ref, hn_ref):
        hn, _ = _rms(x_ref[...], g_ref[...])
        hn_ref[...] = hn.astype(bf16)

    return pl.pallas_call(
        body, name="s5_pre", grid=(NT,), in_specs=[_tile(), _full((1, D))], out_specs=_tile(),
        out_shape=SDS((T, D), bf16), compiler_params=_cp(dimension_semantics=("arbitrary",)),
    )(xp, g)


def _gelu_grad(y):
    c = math.sqrt(2.0 / math.pi)
    t = jnp.tanh(c * (y + 0.044715 * y * y * y))
    return 0.5 * (1.0 + t) + 0.5 * y * (1.0 - t * t) * c * (1.0 + 3.0 * 0.044715 * y * y)


def s5_post(ys, xp, g, d, wglu, bglu, bg=()):
    def body(ys_ref, x_ref, g_ref, d_ref, w_ref, b_ref, y_ref, z_ref, h_ref):
        x = x_ref[...]
        hn, _ = _rms(x, g_ref[...])
        y = ys_ref[...] + d_ref[...] * hn
        y_ref[...] = y
        yg = jax.nn.gelu(y).astype(bf16)
        for j in range(4):
            cv = slice(j * 256, (j + 1) * 256)
            cg = slice(1024 + j * 256, 1024 + (j + 1) * 256)
            val = _dot(yg, w_ref[j]) + b_ref[:, cv]
            gate = _dot(yg, w_ref[j + 4]) + b_ref[:, cg]
            z_ref[:, cv] = val
            z_ref[:, cg] = gate
            h_ref[:, cv] = x[:, cv] + val * jax.nn.sigmoid(gate)

    return _call(
        bg, body, name="s5_post", grid=(NT,),
        in_specs=[_tile(), _tile(), _full((1, D)), _full((1, D)), _full((8, D, 256)), _full((1, 2 * D))],
        out_specs=[_tile(), _tile(2 * D), _tile()],
        out_shape=[SDS((T, D), f32), SDS((T, 2 * D), f32), SDS((T, D), f32)],
        compiler_params=_cp(dimension_semantics=("arbitrary",)),
    )(ys, xp, g, d, wglu, bglu)


def s5_post_bwd(dh, y, z, wglu, bg=()):
    def body(dh_ref, y_ref, z_ref, w_ref, dy_ref, dw_ref, db_ref, acc):
        i = pl.program_id(0)

        @pl.when(i == 0)
        def _():
            acc[...] = jnp.zeros_like(acc)
            db_ref[...] = jnp.zeros_like(db_ref)

        dh_ = dh_ref[...]
        y = y_ref[...]
        yg = jax.nn.gelu(y).astype(bf16)
        dyg = jnp.zeros((TM, D), f32)
        for j in range(4):
            cv = slice(j * 256, (j + 1) * 256)
            cg = slice(1024 + j * 256, 1024 + (j + 1) * 256)
            val = z_ref[:, cv]
            sg = jax.nn.sigmoid(z_ref[:, cg])
            dval = dh_[:, cv] * sg
            dgate = dh_[:, cv] * val * sg * (1.0 - sg)
            db_ref[:, cv] += _colsum8(dval)
            db_ref[:, cg] += _colsum8(dgate)
            dvb = dval.astype(bf16)
            dgb = dgate.astype(bf16)
            acc[j] += _dot_tn(yg, dvb)
            acc[j + 4] += _dot_tn(yg, dgb)
            dyg = dyg + _dot_nt(dvb, w_ref[j]) + _dot_nt(dgb, w_ref[j + 4])
        dy_ref[...] = dyg * _gelu_grad(y)

        @pl.when(i == NT - 1)
        def _():
            dw_ref[...] = acc[...].astype(bf16)

    return _call(
        bg, body, name="s5_post_bwd", grid=(NT,),
        in_specs=[_tile(), _tile(), _tile(2 * D), _full((8, D, 256))],
        out_specs=[_tile(), _full((8, D, 256)), _full((8, 2 * D))],
        out_shape=[SDS((T, D), f32), SDS((8, D, 256), bf16), SDS((8, 2 * D), f32)],
        scratch_shapes=[pltpu.VMEM((8, D, 256), f32)],
        compiler_params=_cp(dimension_semantics=("arbitrary",)),
    )(dh, y, z, wglu)


def s5_pre_bwd(xp, g, du, dy, d, dh, bg=()):
    def body(x_ref, g_ref, du_ref, dy_ref, d_ref, dh_ref, dx_ref, dg_ref, dd_ref):
        i = pl.program_id(0)

        @pl.when(i == 0)
        def _():
            dg_ref[...] = jnp.zeros_like(dg_ref)
            dd_ref[...] = jnp.zeros_like(dd_ref)

        x = x_ref[...]
        g = g_ref[...]
        dy = dy_ref[...]
        hn, _ = _rms(x, g)
        dhn = du_ref[...] + d_ref[...] * dy
        dx, dgt = _rms_bwd(x, g, dhn)
        dx_ref[...] = dh_ref[...] + dx
        dg_ref[...] += _colsum8(dgt)
        dd_ref[...] += _colsum8(dy * hn)

    return _call(
        bg, body, name="s5_pre_bwd", grid=(NT,),
        in_specs=[_tile(), _full((1, D)), _tile(), _tile(), _full((1, D)), _tile()],
        out_specs=[_tile(), _full((8, D)), _full((8, D))],
        out_shape=[SDS((T, D), f32), SDS((8, D), f32), SDS((8, D), f32)],
        compiler_params=_cp(dimension_semantics=("arbitrary",)),
    )(xp, g, du, dy, d, dh)


TMF = 1024


def mlp_fwd(h, g, w_in, w_out, layer, bg=()):
    def body(h_ref, g_ref, wi_ref, wo_ref, hm_ref, r_ref, out_ref, acc):
        j = pl.program_id(1)

        @pl.when(j == 0)
        def _():
            hm, _ = _rms(h_ref[...], g_ref[...])
            hm_ref[...] = hm.astype(bf16)
            acc[...] = jnp.zeros_like(acc)

        a = jnp.maximum(_dot(hm_ref[...], wi_ref[...]), 0.0)
        r_ref[...] = a.astype(bf16)
        acc[...] += _dot((a * a).astype(bf16), wo_ref[...])

        @pl.when(j == NDEV - 1)
        def _():
            out_ref[...] = h_ref[...] + acc[...]

    return _call(
        bg, body, name=f"mlp_fwd{layer}", grid=(T // TMF, NDEV),
        in_specs=[pl.BlockSpec((TMF, D), lambda i, j: (i, 0)),
                  pl.BlockSpec((1, D), lambda i, j: (0, 0)),
                  pl.BlockSpec((None, D, D_FF_SHARD), lambda i, j: (j, 0, 0)),
                  pl.BlockSpec((None, D_FF_SHARD, D), lambda i, j: (j, 0, 0))],
        out_specs=[pl.BlockSpec((TMF, D), lambda i, j: (i, 0)), pl.BlockSpec((TMF, D_FF_SHARD), lambda i, j: (i, j)),
                   pl.BlockSpec((TMF, D), lambda i, j: (i, 0))],
        out_shape=[SDS((T, D), bf16), SDS((T, NDEV * D_FF_SHARD), bf16), SDS((T, D), f32)],
        scratch_shapes=[pltpu.VMEM((TMF, D), f32)],
        compiler_params=_cp(dimension_semantics=("arbitrary", "arbitrary")),
    )(h, g, w_in, w_out)


def mlp_bwd(h, hm, r, g, dout, dout_b, w_in, w_out, layer, bg=()):
    last = NDEV - 1

    def body(h_ref, hm_ref, r_ref, g_ref, do_ref, dob_ref, wi_ref, wo_ref, dh_ref, dwi_ref, dwo_ref, dg_ref,
             dhm, awi, awo):
        j = pl.program_id(0)
        i = pl.program_id(1)
        rows = pl.ds(pl.multiple_of(i * TM, TM), TM)

        @pl.when(i == 0)
        def _():
            awi[...] = jnp.zeros_like(awi)
            awo[...] = jnp.zeros_like(awo)

        dz = (_dot_nt(dob_ref[...], wo_ref[...]) * (2.0 * r_ref[...].astype(f32))).astype(bf16)
        rb = r_ref[...]
        awo[...] += _dot_tn(rb * rb, dob_ref[...])
        awi[...] += _dot_tn(hm_ref[...], dz)
        part = _dot_nt(dz, wi_ref[...])

        @pl.when(j == 0)
        def _():
            dhm[rows, :] = part

        @pl.when(j > 0)
        def _():
            dhm[rows, :] += part

        @pl.when(i == NT - 1)
        def _():
            dwi_ref[...] = awi[...].astype(bf16)
            dwo_ref[...] = awo[...].astype(bf16)

        @pl.when(j == last)
        def _():
            @pl.when(i == 0)
            def _():
                dg_ref[...] = jnp.zeros_like(dg_ref)
            dx, dgt = _rms_bwd(h_ref[...], g_ref[...], dhm[rows, :])
            dh_ref[...] = do_ref[...] + dx
            dg_ref[...] += _colsum8(dgt)

    late = lambda j, i: (jnp.where(j == last, i, 0), 0)
    return _call(
        bg, body, name=f"mlp_bwd{layer}", grid=(NDEV, NT),
        in_specs=[pl.BlockSpec((TM, D), late),
                  pl.BlockSpec((TM, D), lambda j, i: (i, 0)),
                  pl.BlockSpec((TM, D_FF_SHARD), lambda j, i: (i, j)),
                  pl.BlockSpec((1, D), lambda j, i: (0, 0)),
                  pl.BlockSpec((TM, D), late),
                  pl.BlockSpec((TM, D), lambda j, i: (i, 0)),
                  pl.BlockSpec((None, D, D_FF_SHARD), lambda j, i: (j, 0, 0)),
                  pl.BlockSpec((None, D_FF_SHARD, D), lambda j, i: (j, 0, 0))],
        out_specs=[pl.BlockSpec((TM, D), late),
                   pl.BlockSpec((None, D, D_FF_SHARD), lambda j, i: (j, 0, 0)),
                   pl.BlockSpec((None, D_FF_SHARD, D), lambda j, i: (j, 0, 0)),
                   pl.BlockSpec((8, D), lambda j, i: (0, 0))],
        out_shape=[SDS((T, D), f32), SDS((NDEV, D, D_FF_SHARD), bf16), SDS((NDEV, D_FF_SHARD, D), bf16),
                   SDS((8, D), f32)],
        scratch_shapes=[pltpu.VMEM((T, D), f32), pltpu.VMEM((D, D_FF_SHARD), f32), pltpu.VMEM((D_FF_SHARD, D), f32)],
        compiler_params=_cp(dimension_semantics=("arbitrary", "arbitrary")),
    )(h, hm, r, g, dout, dout_b, w_in, w_out)


def _spread4():
    r = lax.broadcasted_iota(jnp.int32, (256, D), 0)
    c = lax.broadcasted_iota(jnp.int32, (256, D), 1)
    return ((c // 256 == r // HEAD_DIM) & (c % HEAD_DIM == r % HEAD_DIM)).astype(bf16)


def attn_pre(h, g_kv, g_mix, wkv, bkv, spread, wq, bq):
    def body(h_ref, gkv_ref, gm_ref, wkv_ref, bkv_ref, sp_ref, wq_ref, bq_ref, kvn_ref, hn_ref, k_ref, v_ref, q_ref):
        h_ = h_ref[...]
        kvn = _rms(h_, gkv_ref[...])[0].astype(bf16)
        hn = _rms(h_, gm_ref[...])[0].astype(bf16)
        kvn_ref[...] = kvn
        hn_ref[...] = hn
        kv = (_dot(kvn, wkv_ref[...]) + bkv_ref[...]).astype(bf16)
        k_ref[...] = _dot(kv[:, :256], sp_ref[...]).astype(bf16)
        v_ref[...] = _dot(kv[:, 256:], sp_ref[...]).astype(bf16)
        q_ref[...] = (_dot(hn, wq_ref[...]) + bq_ref[...]).astype(bf16)

    return pl.pallas_call(
        body, name="attn_pre", grid=(NT,),
        in_specs=[_tile(), _full((1, D)), _full((1, D)), _full((D, 512)), _full((1, 512)), _full((256, D)),
                  _full((D, D)), _full((1, D))],
        out_specs=[_tile()] * 5,
        out_shape=[SDS((T, D), bf16)] * 5,
        compiler_params=_cp(dimension_semantics=("arbitrary",)),
    )(h, g_kv, g_mix, wkv, bkv, spread, wq, bq)


def _attn_specs():
    cur = pl.BlockSpec((TM, 256), lambda j, n: (n, j))
    prev = pl.BlockSpec((BLK, 256), lambda j, n: (jnp.maximum(n * (TM // BLK) - 1, 0), j))
    return cur, prev


def _head_mask(g):
    lane = lax.broadcasted_iota(jnp.int32, (1, 256), 1)
    return (lane >= g * HEAD_DIM) & (lane < (g + 1) * HEAD_DIM)


def _stack_heads(t):
    return jnp.concatenate([jnp.where(_head_mask(g), t, 0) for g in range(Q_PER_KV)], axis=0)


def _unstack_heads(t):
    out = jnp.where(_head_mask(0), t[0:BLK], 0.0)
    for g in range(1, Q_PER_KV):
        out = out + jnp.where(_head_mask(g), t[g * BLK:(g + 1) * BLK], 0.0)
    return out


def _attn_probs(qs, k2, sinks, first):
    rows = Q_PER_KV * BLK
    s = _dot_nt(qs, k2) * (1.0 / math.sqrt(HEAD_DIM))
    qi = jnp.bitwise_and(lax.broadcasted_iota(jnp.int32, (rows, 2 * BLK), 0), BLK - 1)
    kj = lax.broadcasted_iota(jnp.int32, (rows, 2 * BLK), 1)
    diff = qi + BLK - kj
    valid = (diff >= 0) & (diff < BLK) & (jnp.logical_not(first) | (kj >= BLK))
    s = jnp.where(valid, s, -jnp.inf)
    rb = lax.broadcasted_iota(jnp.int32, (rows, 1), 0)
    sink = jnp.where(rb < BLK, sinks[0], jnp.where(rb < 2 * BLK, sinks[1], jnp.where(rb < 3 * BLK, sinks[2], sinks[3])))
    m = jnp.maximum(jnp.max(s, axis=-1, keepdims=True), sink)
    p = jnp.exp(s - m)
    ps = jnp.exp(sink - m)
    denom = jnp.sum(p, axis=-1, keepdims=True) + ps
    return p / denom, ps / denom


def _window_blocks(b, n, kc_ref, kp_ref, vc_ref, vp_ref):
    if b == 0:
        return (jnp.concatenate([kp_ref[...], kc_ref[0:BLK, :]], axis=0),
                jnp.concatenate([vp_ref[...], vc_ref[0:BLK, :]], axis=0), n == 0)
    rows = pl.ds((b - 1) * BLK, 2 * BLK)
    return kc_ref[rows, :], vc_ref[rows, :], False


def attn_core_fwd(q, k4, v4, sinks, bg=()):
    nb = TM // BLK

    def body(sink_ref, q_ref, kc_ref, kp_ref, vc_ref, vp_ref, o_ref):
        j = pl.program_id(0)
        n = pl.program_id(1)
        sk = [sink_ref[j * Q_PER_KV + g] for g in range(Q_PER_KV)]
        for b in range(nb):
            qb = q_ref[b * BLK:(b + 1) * BLK, :]
            k2, v2, first = _window_blocks(b, n, kc_ref, kp_ref, vc_ref, vp_ref)
            a, _ = _attn_probs(_stack_heads(qb), k2, sk, first)
            o_ref[b * BLK:(b + 1) * BLK, :] = _unstack_heads(_dot(a.astype(bf16), v2)).astype(bf16)

    cur, prev = _attn_specs()
    return _call(
        bg, body, name="attn_core_fwd", grid=(N_KV, NT),
        in_specs=[pl.BlockSpec(memory_space=pltpu.SMEM), cur, cur, prev, cur, prev],
        out_specs=cur, out_shape=SDS((T, D), bf16),
        compiler_params=_cp(dimension_semantics=("arbitrary", "arbitrary")),
    )(sinks, q, k4, k4, v4, v4)


def attn_post(h, o, wo, bo):
    def body(h_ref, o_ref, w_ref, b_ref, out_ref):
        out_ref[...] = h_ref[...] + _dot(o_ref[...], w_ref[...]) + b_ref[...]

    return pl.pallas_call(
        body, name="attn_post", grid=(NT,), in_specs=[_tile(), _tile(), _full((D, D)), _full((1, D))],
        out_specs=_tile(), out_shape=SDS((T, D), f32), compiler_params=_cp(dimension_semantics=("arbitrary",)),
    )(h, o, wo, bo)


def attn_bwd_pre(dh, o, wo, bg=()):
    def body(dh_ref, o_ref, w_ref, do_ref, dw_ref, db_ref, acc):
        i = pl.program_id(0)

        @pl.when(i == 0)
        def _():
            acc[...] = jnp.zeros_like(acc)
            db_ref[...] = jnp.zeros_like(db_ref)

        dh_ = dh_ref[...]
        dhb = dh_.astype(bf16)
        do_ref[...] = _dot_nt(dhb, w_ref[...]).astype(bf16)
        acc[...] += _dot_tn(o_ref[...], dhb)
        db_ref[...] += _colsum8(dh_)

        @pl.when(i == NT - 1)
        def _():
            dw_ref[...] = acc[...].astype(bf16)

    return _call(
        bg, body, name="attn_bwd_pre", grid=(NT,), in_specs=[_tile(), _tile(), _full((D, D))],
        out_specs=[_tile(), _full((D, D)), _full((8, D))],
        out_shape=[SDS((T, D), bf16), SDS((D, D), bf16), SDS((8, D), f32)],
        scratch_shapes=[pltpu.VMEM((D, D), f32)],
        compiler_params=_cp(dimension_semantics=("arbitrary",)),
    )(dh, o, wo)


def attn_core_bwd(q, do, k4, v4, sinks, bg=()):
    nb = TM // BLK

    def body(sink_ref, q_ref, do_ref, kc_ref, kp_ref, vc_ref, vp_ref, dq_ref, dk_ref, dv_ref, ds_ref):
        j = pl.program_id(0)
        n = pl.program_id(1)

        @pl.when(n == 0)
        def _():
            dk_ref[...] = jnp.zeros_like(dk_ref)
            dv_ref[...] = jnp.zeros_like(dv_ref)
            ds_ref[...] = jnp.zeros_like(ds_ref)

        lane8 = lax.broadcasted_iota(jnp.int32, (8, 128), 1)
        row8 = lax.broadcasted_iota(jnp.int32, (8, 128), 0)
        sk = [sink_ref[j * Q_PER_KV + g] for g in range(Q_PER_KV)]
        for b in range(nb):
            qs = _stack_heads(q_ref[b * BLK:(b + 1) * BLK, :])
            dos = _stack_heads(do_ref[b * BLK:(b + 1) * BLK, :])
            k2, v2, first = _window_blocks(b, n, kc_ref, kp_ref, vc_ref, vp_ref)
            a, asink = _attn_probs(qs, k2, sk, first)
            dp = _dot_nt(dos, v2)
            dd = jnp.sum(a * dp, axis=-1, keepdims=True)
            dsc = (a * (dp - dd) * (1.0 / math.sqrt(HEAD_DIM))).astype(bf16)
            t = asink * dd
            for g in range(Q_PER_KV):
                dsink = -jnp.sum(t[g * BLK:(g + 1) * BLK], axis=0, keepdims=True)
                ds_ref[...] += jnp.where((lane8 == g) & (row8 == 0), jnp.broadcast_to(dsink, (8, 128)), 0.0)
            dq_ref[b * BLK:(b + 1) * BLK, :] = _unstack_heads(_dot(dsc, k2))
            dk2 = _dot_tn(dsc, qs)
            dv2 = _dot_tn(a.astype(bf16), dos)
            cur = pl.ds(pl.multiple_of(n * TM + b * BLK, BLK), BLK)
            dk_ref[cur, :] += dk2[BLK:, :]
            dv_ref[cur, :] += dv2[BLK:, :]
            if b == 0:
                @pl.when(n > 0)
                def _():
                    prv = pl.ds(pl.multiple_of(n * TM - BLK, BLK), BLK)
                    dk_ref[prv, :] += dk2[:BLK, :]
                    dv_ref[prv, :] += dv2[:BLK, :]
            else:
                prv = pl.ds(pl.multiple_of(n * TM + (b - 1) * BLK, BLK), BLK)
                dk_ref[prv, :] += dk2[:BLK, :]
                dv_ref[prv, :] += dv2[:BLK, :]

    cur, prev = _attn_specs()
    col = pl.BlockSpec((T, 256), lambda j, n: (0, j))
    return _call(
        bg, body, name="attn_core_bwd", grid=(N_KV, NT),
        in_specs=[pl.BlockSpec(memory_space=pltpu.SMEM), cur, cur, cur, prev, cur, prev],
        out_specs=[cur, col, col, pl.BlockSpec((None, 8, 128), lambda j, n: (j, 0, 0))],
        out_shape=[SDS((T, D), f32), SDS((T, D), f32), SDS((T, D), f32), SDS((N_KV, 8, 128), f32)],
        compiler_params=_cp(dimension_semantics=("arbitrary", "arbitrary")),
    )(sinks, q, do, k4, k4, v4, v4)


def attn_bwd_q(h, dh, dq, hn, g_mix, wq):
    def body(h_ref, dh_ref, dq_ref, hn_ref, gm_ref, wq_ref, out_ref, dwq_ref, dbq_ref, dgm_ref, aq):
        i = pl.program_id(0)

        @pl.when(i == 0)
        def _():
            aq[...] = jnp.zeros_like(aq)
            dbq_ref[...] = jnp.zeros_like(dbq_ref)
            dgm_ref[...] = jnp.zeros_like(dgm_ref)

        dq_ = dq_ref[...]
        dqb = dq_.astype(bf16)
        aq[...] += _dot_tn(hn_ref[...], dqb)
        dbq_ref[...] += _colsum8(dq_)
        dx, dg = _rms_bwd(h_ref[...], gm_ref[...], _dot_nt(dqb, wq_ref[...]))
        out_ref[...] = dh_ref[...] + dx
        dgm_ref[...] += _colsum8(dg)

        @pl.when(i == NT - 1)
        def _():
            dwq_ref[...] = aq[...].astype(bf16)

    vec = _full((8, D))
    mat = _full((D, D))
    return pl.pallas_call(
        body, name="attn_bwd_q", grid=(NT,),
        in_specs=[_tile()] * 4 + [_full((1, D)), mat],
        out_specs=[_tile(), mat, vec, vec],
        out_shape=[SDS((T, D), f32), SDS((D, D), bf16), SDS((8, D), f32), SDS((8, D), f32)],
        scratch_shapes=[pltpu.VMEM((D, D), f32)],
        compiler_params=_cp(dimension_semantics=("arbitrary",)),
    )(h, dh, dq, hn, g_mix, wq)


def attn_bwd_kv(h, dh, dk4, dv4, kvn, g_kv, wkv, spread):
    def body(h_ref, dh_ref, dk_ref, dv_ref, kvn_ref, gkv_ref, wkv_ref, sp_ref, out_ref, outb_ref, dw_ref, db_ref,
             dgkv_ref, acc):
        i = pl.program_id(0)

        @pl.when(i == 0)
        def _():
            for r in (acc, db_ref, dgkv_ref):
                r[...] = jnp.zeros_like(r)

        dkv = jnp.concatenate([_dot_nt(dk_ref[...].astype(bf16), sp_ref[...]),
                               _dot_nt(dv_ref[...].astype(bf16), sp_ref[...])], axis=1)
        dkvb = dkv.astype(bf16)
        acc[...] += _dot_tn(kvn_ref[...], dkvb)
        db_ref[...] += _colsum8(dkv)
        dx, dg = _rms_bwd(h_ref[...], gkv_ref[...], _dot_nt(dkvb, wkv_ref[...]))
        out = dh_ref[...] + dx
        out_ref[...] = out
        outb_ref[...] = out.astype(bf16)
        dgkv_ref[...] += _colsum8(dg)

        @pl.when(i == NT - 1)
        def _():
            dw_ref[...] = acc[...].astype(bf16)

    return pl.pallas_call(
        body, name="attn_bwd_kv", grid=(NT,),
        in_specs=[_tile()] * 5 + [_full((1, D)), _full((D, 512)), _full((256, D))],
        out_specs=[_tile(), _tile(), _full((D, 512)), _full((8, 512)), _full((8, D))],
        out_shape=[SDS((T, D), f32), SDS((T, D), bf16), SDS((D, 512), bf16), SDS((8, 512), f32), SDS((8, D), f32)],
        scratch_shapes=[pltpu.VMEM((D, 512), f32)],
        compiler_params=_cp(dimension_semantics=("arbitrary",)),
    )(h, dh, dk4, dv4, kvn, g_kv, wkv, spread)


def final_loss(h, g, target):
    def body(h_ref, g_ref, t_ref, loss_ref, dh_ref, dhb_ref, dg_ref):
        i = pl.program_id(0)

        @pl.when(i == 0)
        def _():
            loss_ref[...] = jnp.zeros_like(loss_ref)
            dg_ref[...] = jnp.zeros_like(dg_ref)

        h_ = h_ref[...]
        g_ = g_ref[...]
        y, _ = _rms(h_, g_)
        diff = y - t_ref[...]
        per_tok = jnp.mean(diff * diff, axis=-1, keepdims=True)
        tot = 0.5 * jnp.sum(per_tok, axis=0, keepdims=True)
        lane = lax.broadcasted_iota(jnp.int32, (8, 128), 1)
        row = lax.broadcasted_iota(jnp.int32, (8, 128), 0)
        loss_ref[...] += jnp.where((lane == 0) & (row == 0), jnp.broadcast_to(tot, (8, 128)), 0.0)
        dx, dgt = _rms_bwd(h_, g_, diff * (1.0 / D))
        dh_ref[...] = dx
        dhb_ref[...] = dx.astype(bf16)
        dg_ref[...] += _colsum8(dgt)

    return pl.pallas_call(
        body, name="final_loss", grid=(NT,), in_specs=[_tile(), _full((1, D)), _tile()],
        out_specs=[_full((8, 128)), _tile(), _tile(), _full((8, D))],
        out_shape=[SDS((8, 128), f32), SDS((T, D), f32), SDS((T, D), bf16), SDS((8, D), f32)],
        compiler_params=_cp(dimension_semantics=("arbitrary",)),
    )(h, g, target)


def _to_chunked(a):
    return a.reshape(S5_CH, S5_STEPS, a.shape[-1]).transpose(1, 0, 2).reshape(T, a.shape[-1])


def _from_chunked(a):
    return a.reshape(S5_STEPS, S5_CH, a.shape[-1]).transpose(1, 0, 2).reshape(T, a.shape[-1])


def _rep4(w):
    return jnp.broadcast_to(w.reshape(w.shape[0], N_KV, 1, HEAD_DIM), (w.shape[0], N_KV, Q_PER_KV, HEAD_DIM)).reshape(
        w.shape[0], N_KV * Q_PER_KV * HEAD_DIM)


def _fold4(w):
    return w.reshape(w.shape[0], N_KV, Q_PER_KV, HEAD_DIM).sum(axis=2).reshape(w.shape[0], N_KV * HEAD_DIM)


def fwd_bwd(x, target, p, shards, opt, core, chip):
    row = lambda v: v.reshape(1, -1)
    (lam, bm, cm), prep_vjp = jax.vjp(s5_discretize, p["s5_a_re"][0], p["s5_a_im"][0], p["s5_log_dt"][0],
                                      p["s5_b_re"][0], p["s5_b_im"][0], p["s5_c_re"][0], p["s5_c_im"][0])
    bmb, cmb = bm.astype(bf16), cm.astype(bf16)
    lam = jnp.concatenate([lam, lam * jnp.array([1.0, -1.0], f32).reshape(1, 2, 1, 1)], axis=1)
    g_mix0, g_mix1 = row(p["norm_mix"][0]), row(p["norm_mix"][1])
    g_mlp0, g_mlp1 = row(p["norm_mlp"][0]), row(p["norm_mlp"][1])
    g_kv, g_fin = row(p["norm_kv"]), row(p["norm_final"])
    bq, bo = p["b_q"], p["b_o"]
    bkv = row(p["b_kv"])
    spread = _spread4()
    sinks = p["sinks"].reshape(16)

    def reduce_pairs(names, bg):
        return [add_pairs(g, r, core, f"add_pairs_{n}") for n, g, r in zip(names, bg.arrs, bg.result)]

    xp = _to_chunked(x)
    hn0 = s5_pre(xp, g_mix0)
    ga = BgGather([shards["s5_w_glu"], shards["vecs"], shards["w_in0"], shards["w_out0"]], mids=(0.75, 1.0))
    ys = s5_core_fwd(hn0, bmb, lam, cmb, bg=[ga])
    wglu, gvec, win0, wout0 = ga.result
    d_skip = gvec[:, 0, :128].reshape(1, D)
    bglu = gvec[:, 0, 128:].reshape(1, 2 * D)
    y, z, h1 = s5_post(ys, xp, g_mix0, d_skip, wglu, bglu)
    gc = BgGather([shards["w_kv"], shards["w_q"], shards["w_o"], shards["w_in1"]], mids=(0.8, 1.0))
    hm0, r0, h2p = mlp_fwd(h1, g_mlp0, win0, wout0, 0, bg=[gc])
    wkv, wq, wo, win1 = gc.result
    wkv, wq, wo = wkv.reshape(D, 512), wq.reshape(D, D), wo.reshape(D, D)
    h2 = _from_chunked(h2p)
    kvn, hn1, k4, v4, q = attn_pre(h2, g_kv, g_mix1, wkv, bkv, spread, wq, bq)
    o = attn_core_fwd(q, k4, v4, sinks)
    wout1, = sc_gather([shards["w_out1"]], 3, "sc_gather_w_out1")
    h3 = attn_post(h2, o, wo, bo)
    hm1, r1, h4 = mlp_fwd(h3, g_mlp1, win1, wout1, 1)
    loss, dh4, dh4b, dg_fin = final_loss(h4, g_fin, target)

    big = {}
    rider = lambda n, part, r2, **kw: AdamRider(*opt[n], part, r2, **kw)
    dh3, dwin1, dwout1, dg_mlp1 = mlp_bwd(h3, hm1, r1, g_mlp1, dh4, dh4b, win1, wout1, 1)
    pa = BgPair([dwin1, dwout1])
    do, dwo, dbo = attn_bwd_pre(dh3, o, wo, bg=[pa])
    p_in1, p_out1 = reduce_pairs(["w_in1", "w_out1"], pa)
    ca = BgChips([p_in1])
    dq, dk4, dv4, dsink = attn_core_bwd(q, do, k4, v4, sinks, bg=[ca])
    dh2, dwq, dbq, dg_mix1 = attn_bwd_q(h2, dh3, dq, hn1, g_mix1, wq)
    dh2, dh2b, dwkv, dbkv, dg_kv = attn_bwd_kv(h2, dh2, dk4, dv4, kvn, g_kv, wkv, spread)
    pb = BgPair([dwkv.reshape(NDEV, 128, 512), dwq.reshape(NDEV, 128, D), dwo.reshape(NDEV, 128, D)])
    ca2 = BgChips([p_out1])
    dh2p, dh2pb = _to_chunked(dh2), _to_chunked(dh2b)
    dh1, dwin0, dwout0, dg_mlp0 = mlp_bwd(h1, hm0, r0, g_mlp0, dh2p, dh2pb, win0, wout0, 0, bg=[pb, ca2])
    cb = BgChips(reduce_pairs(["w_kv", "w_q", "w_o"], pb))
    pc = BgPair([dwin0, dwout0])
    dy, dwglu, dbglu = s5_post_bwd(dh1, y, z, wglu, bg=[cb, pc])
    cc = BgChips(reduce_pairs(["w_in0", "w_out0"], pc))
    pd = BgPair([dwglu])
    a_in1 = rider("w_mlp_in", p_in1, ca.result[0], layer=1)
    a_out1 = rider("w_mlp_out", p_out1, ca2.result[0], layer=1)
    a_attn = [rider(n, part, r2) for n, part, r2 in zip(("w_kv", "w_q", "w_o"), cb.arrs, cb.result)]
    du, dbm, dcmt, dlam = s5_core_bwd(hn0, dy, bmb, lam, cmb, bg=[cc, pd, a_in1, a_out1] + a_attn)
    big["w_kv"], big["w_q"], big["w_o"] = [a.result for a in a_attn]
    cd = BgChips(reduce_pairs(["s5_w_glu"], pd))
    a_in0 = rider("w_mlp_in", cc.arrs[0], cc.result[0], layer=0, prev=a_in1.result)
    a_out0 = rider("w_mlp_out", cc.arrs[1], cc.result[1], layer=0, prev=a_out1.result)
    dxp, dg_mix0, dd = s5_pre_bwd(xp, g_mix0, du, dy, d_skip, dh1, bg=[cd, a_in0, a_out0])
    big["w_mlp_in"], big["w_mlp_out"] = a_in0.result, a_out0.result
    big["s5_w_glu"] = adam_big(*opt["s5_w_glu"], cd.arrs[0], cd.result[0], chip, "adam_s5_w_glu")
    grad_x = _from_chunked(dxp)
    da_re, da_im, dlog_dt, db_re, db_im, dc_re, dc_im = prep_vjp((dlam, dbm, dcmt.transpose(0, 2, 1)))

    def lanes(v_):
        v_ = v_.reshape(1, -1)
        return jnp.pad(v_, ((0, 0), (0, D - v_.shape[1])))

    small = jnp.concatenate([
        dg_mix0[0:1], dg_mix1[0:1], dg_mlp0[0:1], dg_mlp1[0:1], dg_kv[0:1], dg_fin[0:1], dd[0:1], dbq[0:1], dbo[0:1],
        dbglu[0:1].reshape(2, D), lanes(dbkv[0:1]),
        lanes(dsink[:, 0, :Q_PER_KV]), lanes(dlog_dt), lanes(loss[0:1, 0:1]), jnp.zeros((1, D), f32),
        da_re.reshape(4, D), da_im.reshape(4, D),
        db_re.transpose(0, 2, 1).reshape(64, D), db_im.transpose(0, 2, 1).reshape(64, D),
        dc_re.reshape(64, D), dc_im.reshape(64, D)], axis=0)
    return loss, grad_x, small, big


_ANY = pl.BlockSpec(memory_space=pl.ANY)


def _pos():
    return lax.axis_index("x"), lax.axis_index("y"), lax.axis_index("c")


def _other_chips(x, y):
    return [(1 - x, y), (x, 1 - y), (1 - x, 1 - y)]


def all_gather(arrs):
    n = len(arrs)

    def body(*refs):
        ins, outs = refs[:n], refs[n:2 * n]
        send_sems, recv_sems, local_sems = refs[2 * n:]
        x, y, c = _pos()
        me, sib = (x, y, c), (x, y, 1 - c)
        chips = _other_chips(x, y)

        def copy(a, k, block, to, src=None):
            dst = outs[a].at[4 * block[0] + 2 * block[1] + block[2]]
            return pltpu.make_async_remote_copy(
                src_ref=dst if src is None else src, dst_ref=dst, send_sem=send_sems.at[a, k],
                recv_sem=recv_sems.at[a, k], device_id=to, device_id_type=MESH)

        mine = [pltpu.make_async_copy(ins[a], outs[a].at[4 * x + 2 * y + c], local_sems.at[a]) for a in range(n)]
        for cp in mine:
            cp.start()
        first = []
        for a in range(n):
            first.append(copy(a, 0, me, sib, src=ins[a]))
            first += [copy(a, 1 + j, me, (*chip, c), src=ins[a]) for j, chip in enumerate(chips)]
        for cp in first:
            cp.start()
        passed = []
        for j, chip in enumerate(chips):
            for a in range(n):
                copy(a, 1 + j, (*chip, c), me).wait_recv()
                cp = copy(a, 4 + j, (*chip, c), sib)
                cp.start()
                passed.append(cp)
        for a in range(n):
            copy(a, 0, sib, me).wait_recv()
            for j, chip in enumerate(chips):
                copy(a, 4 + j, (*chip, 1 - c), me).wait_recv()
        for cp in first + passed:
            cp.wait_send()
        for cp in mine:
            cp.wait()

    return pl.pallas_call(
        body, name="all_gather", in_specs=[_ANY] * n, out_specs=[_ANY] * n,
        out_shape=[SDS((NDEV,) + a.shape, a.dtype) for a in arrs],
        scratch_shapes=[pltpu.SemaphoreType.DMA((n, 7)), pltpu.SemaphoreType.DMA((n, 7)),
                        pltpu.SemaphoreType.DMA((n,))],
    )(*arrs)


def rs_pair(grads):
    n = len(grads)

    def body(*refs):
        ins, outs = refs[:n], refs[n:2 * n]
        send_sems, recv_sems = refs[2 * n:]
        x, y, c = _pos()
        cps = []
        for a in range(n):
            for k in range(4):
                cps.append(pltpu.make_async_remote_copy(
                    src_ref=ins[a].at[2 * k + 1 - c], dst_ref=outs[a].at[k], send_sem=send_sems.at[a, k],
                    recv_sem=recv_sems.at[a, k], device_id=(x, y, 1 - c), device_id_type=MESH))
        for cp in cps:
            cp.start()
        for cp in cps:
            cp.wait_recv()
        for cp in cps:
            cp.wait_send()

    return pl.pallas_call(
        body, name="rs_pair", in_specs=[_ANY] * n, out_specs=[_ANY] * n,
        out_shape=[SDS((4,) + g.shape[1:], g.dtype) for g in grads],
        scratch_shapes=[pltpu.SemaphoreType.DMA((n, 4)), pltpu.SemaphoreType.DMA((n, 4))],
    )(*grads)


def rs_chips(parts):
    n = len(parts)

    def body(*refs):
        ins, outs = refs[:n], refs[n:2 * n]
        send_sems, recv_sems = refs[2 * n:]
        x, y, c = _pos()
        cps = []
        for a in range(n):
            for r, (px, py) in enumerate(_other_chips(x, y)):
                cps.append(pltpu.make_async_remote_copy(
                    src_ref=ins[a].at[2 * px + py], dst_ref=outs[a].at[r], send_sem=send_sems.at[a, r],
                    recv_sem=recv_sems.at[a, r], device_id=(px, py, c), device_id_type=MESH))
        for cp in cps:
            cp.start()
        for cp in cps:
            cp.wait_recv()
        for cp in cps:
            cp.wait_send()

    return pl.pallas_call(
        body, name="rs_chips", in_specs=[_ANY] * n, out_specs=[_ANY] * n,
        out_shape=[SDS((3,) + g.shape[1:], g.dtype) for g in parts],
        scratch_shapes=[pltpu.SemaphoreType.DMA((n, 3)), pltpu.SemaphoreType.DMA((n, 3))],
    )(*parts)


def _row_tile(r, c):
    return min(r, max(8, (512 * 1024) // c))


def add_pairs(g, r1, core, name):
    _, R, C = g.shape
    tr = _row_tile(R, C)

    def body(core_ref, g_ref, r_ref, o_ref):
        o_ref[...] = (g_ref[...].astype(f32) + r_ref[...].astype(f32)).astype(bf16)

    return pl.pallas_call(
        body, name=name, out_shape=SDS((4, R, C), bf16),
        grid_spec=pltpu.PrefetchScalarGridSpec(
            num_scalar_prefetch=1, grid=(4, R // tr),
            in_specs=[pl.BlockSpec((None, tr, C), lambda k, i, core: (2 * k + core[0], i, 0)),
                      pl.BlockSpec((None, tr, C), lambda k, i, core: (k, i, 0))],
            out_specs=pl.BlockSpec((None, tr, C), lambda k, i, core: (k, i, 0))),
        compiler_params=_cp(dimension_semantics=("arbitrary", "arbitrary")),
    )(core, g, r1)


def _adamw(w, g, m, v):
    m = ADAM_B1 * m + (1.0 - ADAM_B1) * g
    v = ADAM_B2 * v + (1.0 - ADAM_B2) * (g * g)
    m_hat = m / (1.0 - ADAM_B1 ** ADAM_STEP)
    v_hat = v / (1.0 - ADAM_B2 ** ADAM_STEP)
    delta = -ADAM_LR * (m_hat / (jnp.sqrt(v_hat) + ADAM_EPS) + ADAM_WD * w)
    return delta, m, v


def adam_big(w, m, v, part, r2, chip, name, layer=0, prev=None):
    L, R, C = w.shape
    tr = _row_tile(R, C)

    def body(chip_ref, w_ref, m_ref, v_ref, p_ref, r_ref, *rest):
        g_out, d_out, m_out, v_out = rest[-4:]
        g = p_ref[...].astype(f32) + r_ref[0].astype(f32) + r_ref[1].astype(f32) + r_ref[2].astype(f32)
        d, m_, v_ = _adamw(w_ref[...], g, m_ref[...], v_ref[...])
        g_out[...] = g
        d_out[...] = d
        m_out[...] = m_
        v_out[...] = v_

    blk = pl.BlockSpec((None, tr, C), lambda i, chip: (layer, i, 0))
    extra = [] if prev is None else list(prev)
    return pl.pallas_call(
        body, name=name, out_shape=[SDS((L, R, C), f32)] * 4,
        grid_spec=pltpu.PrefetchScalarGridSpec(
            num_scalar_prefetch=1, grid=(R // tr,),
            in_specs=[blk, blk, blk,
                      pl.BlockSpec((None, tr, C), lambda i, chip: (chip[0], i, 0)),
                      pl.BlockSpec((3, tr, C), lambda i, chip: (0, i, 0))] + [_ANY] * len(extra),
            out_specs=[blk] * 4),
        input_output_aliases={6 + k: k for k in range(len(extra))},
        compiler_params=_cp(dimension_semantics=("arbitrary",)),
    )(chip, w, m, v, part, r2, *extra)


def allreduce_small(buf, chips=None):
    shp = buf.shape
    half = (shp[0] // 16) * 8
    parts = (pl.ds(0, half), pl.ds(half, shp[0] - half))
    n_c = 0 if chips is None else len(chips.arrs)

    def body(in_ref, *refs):
        c_in, out_ref, c_out = refs[:n_c], refs[n_c], refs[n_c + 1:2 * n_c + 1]
        acc1, acc2, r0, r1, r2, send_sems, recv_sems = refs[2 * n_c + 1:2 * n_c + 8]
        c_sems = refs[2 * n_c + 8:]
        if chips is not None:
            chips.start(c_in, c_out, c_sems)
        x, y, c = _pos()
        across = [(1 - x, y, c), (x, 1 - y, c)]

        def exchange(src, rcv, dst, copies):
            cps = [pltpu.make_async_remote_copy(
                src_ref=src.at[rows], dst_ref=rcv.at[rows], send_sem=send_sems.at[k], recv_sem=recv_sems.at[k],
                device_id=peer, device_id_type=MESH) for k, rows, peer in copies]
            for cp in cps:
                cp.start()
            for cp in cps:
                cp.wait()
            dst[...] = src[...] + rcv[...]

        exchange(in_ref, r0, acc1, [(0, pl.ds(0, shp[0]), (x, y, 1 - c))])
        exchange(acc1, r1, acc2, [(1, parts[0], across[0]), (2, parts[1], across[1])])
        exchange(acc2, r2, out_ref, [(3, parts[0], across[1]), (4, parts[1], across[0])])
        if chips is not None:
            chips.finish(c_in, c_out, c_sems)

    vm = pl.BlockSpec(memory_space=pltpu.VMEM)
    res = pl.pallas_call(
        body, name="allreduce_small", in_specs=[vm] + [_ANY] * n_c, out_specs=[vm] + [_ANY] * n_c,
        out_shape=[SDS(shp, f32)] + ([] if chips is None else chips.out_shape),
        scratch_shapes=[pltpu.VMEM(shp, f32)] * 5 + [pltpu.SemaphoreType.DMA((5,)), pltpu.SemaphoreType.DMA((5,))]
        + ([] if chips is None else chips.scratch),
    )(buf, *([] if chips is None else chips.arrs))
    if chips is not None:
        chips.result = list(res[1:])
    return res[0]


SMALL_ROWS = {'norm_mix': (0, 2, D), 'norm_mlp': (2, 2, D), 'norm_kv': (4, 1, D), 'norm_final': (5, 1, D),
              's5_d': (6, 1, D), 'b_q': (7, 1, D), 'b_o': (8, 1, D), 's5_b_glu': (9, 2, D), 'b_kv': (11, 1, 512),
              'sinks': (12, 1, 16), 's5_log_dt': (13, 1, 64), 's5_a_re': (16, 4, D), 's5_a_im': (20, 4, D),
              's5_b_re': (24, 64, D), 's5_b_im': (88, 64, D), 's5_c_re': (152, 64, D), 's5_c_im': (216, 64, D)}
LOSS_ROW = 14
ROW_PARAMS = ['norm_mix', 'norm_mlp', 'norm_kv', 'norm_final', 'b_q', 'b_o', 'b_kv', 'sinks', 's5_log_dt']
SHARD_PARAMS = ['s5_d', 's5_b_glu']
S5_PARAMS = ['s5_a_re', 's5_a_im', 's5_b_re', 's5_b_im', 's5_c_re', 's5_c_im']


def adam_small(dev, gsum, s5_grads, w, m, v):
    names = ROW_PARAMS + SHARD_PARAMS + S5_PARAMS
    n_g = len(ROW_PARAMS) + len(SHARD_PARAMS)

    def body(dev_ref, gs_ref, *refs):
        pos = [0]

        def take(k):
            r = refs[pos[0]:pos[0] + k]
            pos[0] += k
            return r

        g5 = take(len(S5_PARAMS))
        wr, mr, vr = take(len(names)), take(len(names)), take(len(names))
        g_out = take(n_g)
        d_out, m_out, v_out = take(len(names)), take(len(names)), take(len(names))
        dv = dev_ref[0]
        for i, n in enumerate(names):
            if n in S5_PARAMS:
                g = g5[S5_PARAMS.index(n)][...]
            elif n in SHARD_PARAMS:
                r0, _, _ = SMALL_ROWS[n]
                ln = wr[i].shape[1]
                g = jnp.zeros((1, ln), f32)
                for k in range(NDEV):
                    off = k * ln
                    piece = gs_ref[r0 + off // D:r0 + off // D + 1, off % D:off % D + ln]
                    g = g + jnp.where(dv == k, piece, 0.0)
                g_out[i][...] = g
            else:
                r0, nr, nl = SMALL_ROWS[n]
                g = gs_ref[r0:r0 + nr, 0:nl]
                g_out[i][...] = g
            d, m_, v_ = _adamw(wr[i][...], g, mr[i][...], vr[i][...])
            d_out[i][...] = d
            m_out[i][...] = m_
            v_out[i][...] = v_

    vm = pl.BlockSpec(memory_space=pltpu.VMEM)
    ins = [s5_grads[n] for n in S5_PARAMS] + [d[n] for d in (w, m, v) for n in names]
    shapes = [SDS(w[n].shape, f32) for n in names]
    res = pl.pallas_call(
        body, name="adam_small", in_specs=[pl.BlockSpec(memory_space=pltpu.SMEM)] + [vm] * (1 + len(ins)),
        out_specs=[vm] * (n_g + 3 * len(names)), out_shape=shapes[:n_g] + shapes * 3,
        compiler_params=_cp(),
    )(dev, gsum, *ins)
    g_o = dict(zip(names[:n_g], res[:n_g]))
    rest = res[n_g:]
    k = len(names)
    return g_o, dict(zip(names, rest[:k])), dict(zip(names, rest[k:2 * k])), dict(zip(names, rest[2 * k:]))


WEIGHTS = ['norm_mix', 'norm_mlp', 'norm_kv', 'norm_final', 's5_a_re', 's5_a_im', 's5_log_dt', 's5_b_re', 's5_b_im',
           's5_c_re', 's5_c_im', 's5_d', 's5_w_glu', 's5_b_glu', 'w_kv', 'b_kv', 'w_q', 'b_q', 'sinks', 'w_o', 'b_o',
           'w_mlp_in', 'w_mlp_out']
BIG = ['s5_w_glu', 'w_kv', 'w_q', 'w_o', 'w_mlp_in', 'w_mlp_out']
BIG_2D = {'s5_w_glu': (D, 256), 'w_kv': (128, 512), 'w_q': (128, D), 'w_o': (128, D), 'w_mlp_in': (2 * D, 512),
          'w_mlp_out': (2 * 512, D)}
SHARDED_SMALL = {'s5_d': D, 's5_b_glu': 2 * D}
SMALL = [n for n in WEIGHTS if n not in BIG]
SMALL_SIZE = {'norm_mix': 2 * D, 'norm_mlp': 2 * D, 'norm_kv': D, 'norm_final': D, 's5_a_re': 4096, 's5_a_im': 4096,
              's5_log_dt': 64, 's5_b_re': 65536, 's5_b_im': 65536, 's5_c_re': 65536, 's5_c_im': 65536, 's5_d': D,
              's5_b_glu': 2 * D, 'b_kv': 512, 'b_q': D, 'sinks': 16, 'b_o': D}


def _pack(vals):
    parts = []
    for n in SMALL:
        v = vals[n].reshape(-1).astype(f32)
        parts.append(jnp.pad(v, (0, (-v.shape[0]) % 128)))
    flat = jnp.concatenate(parts)
    flat = jnp.pad(flat, (0, (-flat.shape[0]) % 1024))
    return flat.reshape(-1, 128)


def _unpack(buf):
    flat = buf.reshape(-1)
    out, off = {}, 0
    for n in SMALL:
        sz = SMALL_SIZE[n]
        out[n] = flat[off:off + sz]
        off += sz + (-sz) % 128
    return out


def kernel(x, norm_mix, norm_mlp, norm_kv, norm_final, s5_a_re, s5_a_im, s5_log_dt, s5_b_re, s5_b_im, s5_c_re, s5_c_im, s5_d, s5_w_glu, s5_b_glu, w_kv, b_kv, w_q, b_q, sinks, w_o, b_o, w_mlp_in, w_mlp_out, loss_target, m_norm_mix, m_norm_mlp, m_norm_kv, m_norm_final, m_s5_a_re, m_s5_a_im, m_s5_log_dt, m_s5_b_re, m_s5_b_im, m_s5_c_re, m_s5_c_im, m_s5_d, m_s5_w_glu, m_s5_b_glu, m_w_kv, m_b_kv, m_w_q, m_b_q, m_sinks, m_w_o, m_b_o, m_w_mlp_in, m_w_mlp_out, v_norm_mix, v_norm_mlp, v_norm_kv, v_norm_final, v_s5_a_re, v_s5_a_im, v_s5_log_dt, v_s5_b_re, v_s5_b_im, v_s5_c_re, v_s5_c_im, v_s5_d, v_s5_w_glu, v_s5_b_glu, v_w_kv, v_b_kv, v_w_q, v_b_q, v_sinks, v_w_o, v_b_o, v_w_mlp_in, v_w_mlp_out):
    w = dict(norm_mix=norm_mix, norm_mlp=norm_mlp, norm_kv=norm_kv, norm_final=norm_final, s5_a_re=s5_a_re,
             s5_a_im=s5_a_im, s5_log_dt=s5_log_dt, s5_b_re=s5_b_re, s5_b_im=s5_b_im, s5_c_re=s5_c_re, s5_c_im=s5_c_im,
             s5_d=s5_d, s5_w_glu=s5_w_glu, s5_b_glu=s5_b_glu, w_kv=w_kv, b_kv=b_kv, w_q=w_q, b_q=b_q, sinks=sinks,
             w_o=w_o, b_o=b_o, w_mlp_in=w_mlp_in, w_mlp_out=w_mlp_out)
    m = dict(norm_mix=m_norm_mix, norm_mlp=m_norm_mlp, norm_kv=m_norm_kv, norm_final=m_norm_final, s5_a_re=m_s5_a_re,
             s5_a_im=m_s5_a_im, s5_log_dt=m_s5_log_dt, s5_b_re=m_s5_b_re, s5_b_im=m_s5_b_im, s5_c_re=m_s5_c_re,
             s5_c_im=m_s5_c_im, s5_d=m_s5_d, s5_w_glu=m_s5_w_glu, s5_b_glu=m_s5_b_glu, w_kv=m_w_kv, b_kv=m_b_kv,
             w_q=m_w_q, b_q=m_b_q, sinks=m_sinks, w_o=m_w_o, b_o=m_b_o, w_mlp_in=m_w_mlp_in, w_mlp_out=m_w_mlp_out)
    v = dict(norm_mix=v_norm_mix, norm_mlp=v_norm_mlp, norm_kv=v_norm_kv, norm_final=v_norm_final, s5_a_re=v_s5_a_re,
             s5_a_im=v_s5_a_im, s5_log_dt=v_s5_log_dt, s5_b_re=v_s5_b_re, s5_b_im=v_s5_b_im, s5_c_re=v_s5_c_re,
             s5_c_im=v_s5_c_im, s5_d=v_s5_d, s5_w_glu=v_s5_w_glu, s5_b_glu=v_s5_b_glu, w_kv=v_w_kv, b_kv=v_b_kv,
             w_q=v_w_q, b_q=v_b_q, sinks=v_sinks, w_o=v_w_o, b_o=v_b_o, w_mlp_in=v_w_mlp_in, w_mlp_out=v_w_mlp_out)
    xi, yi, ci = _pos()
    dev = 4 * xi + 2 * yi + ci
    core = ci.reshape(1).astype(jnp.int32)
    chip = (2 * xi + yi).reshape(1).astype(jnp.int32)

    shards = {
        "s5_w_glu": s5_w_glu[0].astype(bf16), "w_kv": w_kv.astype(bf16), "w_q": w_q[0].astype(bf16),
        "w_o": w_o[0].astype(bf16), "w_in0": w_mlp_in[0].astype(bf16), "w_in1": w_mlp_in[1].astype(bf16),
        "w_out0": w_mlp_out[0].astype(bf16), "w_out1": w_mlp_out[1].astype(bf16),
        "vecs": jnp.broadcast_to(jnp.concatenate([s5_d, s5_b_glu], axis=1), (8, 384)),
    }
    as3d = lambda a, n: a if a.ndim == 3 and a.shape[0] == 2 else a.reshape((1,) + BIG_2D[n])
    opt = {n: (as3d(w[n], n), as3d(m[n], n), as3d(v[n], n)) for n in BIG}
    _, grad_x, grads, big = fwd_bwd(x[0], loss_target[0], {n: w[n] for n in SMALL}, shards, opt, core, chip)

    gsum = allreduce_small(grads)

    out_g, out_d, out_m, out_v = {}, {}, {}, {}
    for n in BIG:
        out_g[n], out_d[n], out_m[n], out_v[n] = [r.reshape(w[n].shape) for r in big[n]]

    loss = gsum[LOSS_ROW, 0]
    swapped = ("s5_b_re", "s5_b_im")
    swap = lambda a: a.transpose(0, 1, 3, 2)

    def kernel_side(d):
        d = {n: (d[n].reshape(1, -1) if d[n].ndim == 1 else d[n]) for n in SMALL}
        d.update({n: swap(d[n]) for n in swapped})
        return d

    s5_g = {}
    for n in S5_PARAMS:
        r0, nr, _ = SMALL_ROWS[n]
        s5_g[n] = gsum[r0:r0 + nr].reshape((1, 64, 16, 64) if n in swapped else w[n].shape)
        out_g[n] = s5_g[n]
    g_s, d_s, m_s, v_s = adam_small(dev.reshape(1).astype(jnp.int32), gsum, s5_g, kernel_side(w), kernel_side(m),
                                    kernel_side(v))
    for src, dst in ((g_s, out_g), (d_s, out_d), (m_s, out_m), (v_s, out_v)):
        dst.update(src)
    for dst in (out_g, out_d, out_m, out_v):
        for n in SMALL:
            dst[n] = (swap(dst[n]) if n in swapped else dst[n]).reshape(w[n].shape)

    return (loss, grad_x[None], *[out_g[n] for n in WEIGHTS], *[out_d[n] for n in WEIGHTS],
            *[out_m[n] for n in WEIGHTS], *[out_v[n] for n in WEIGHTS])
```

```python
import functools
import math

import jax
import jax.numpy as jnp
from jax import lax
from jax.experimental import pallas as pl
from jax.experimental.pallas import tpu as pltpu
from jax.experimental.pallas import tpu_sc as plsc

f32 = jnp.float32
bf16 = jnp.bfloat16
SDS = jax.ShapeDtypeStruct

T = 2048
D = 1024
NDEV = 8
NORM_EPS = 1e-5
S5_G, S5_C, S5_P = 64, 16, 64
S5_SUB = 8
S5_CH = 8
S5_STEPS = T // S5_CH
DT_MIN_LAMBDA = -1e-4
HEAD_DIM = 64
N_KV = 4
Q_PER_KV = 4
BLK = 128
D_FF_SHARD = 512
ADAM_LR, ADAM_B1, ADAM_B2, ADAM_EPS, ADAM_WD, ADAM_STEP = 0.001, 0.9, 0.999, 1e-08, 0.01, 10
VMEM_LIMIT = 56 * 1024 * 1024
MESH = pl.DeviceIdType.MESH


def _cp(**kw):
    return pltpu.CompilerParams(vmem_limit_bytes=VMEM_LIMIT, **kw)


def _dot(a, b):
    return jnp.dot(a, b, preferred_element_type=f32)


def _dot_nt(a, b):
    return lax.dot_general(a, b, (((1,), (1,)), ((), ())), preferred_element_type=f32)


def _dot_tn(a, b):
    return lax.dot_general(a, b, (((0,), (0,)), ((), ())), preferred_element_type=f32)


def _rms(x, g):
    r = lax.rsqrt(jnp.mean(x * x, axis=-1, keepdims=True) + NORM_EPS)
    return x * r * g, r


def _rms_bwd(x, g, dy):
    r = lax.rsqrt(jnp.mean(x * x, axis=-1, keepdims=True) + NORM_EPS)
    u = dy * g
    dx = r * u - (r * r * r) * x * jnp.mean(u * x, axis=-1, keepdims=True)
    return dx, dy * x * r


def _colsum8(v):
    s = jnp.sum(v, axis=0, keepdims=True)
    row = lax.broadcasted_iota(jnp.int32, (8, v.shape[1]), 0)
    return jnp.where(row == 0, jnp.broadcast_to(s, (8, v.shape[1])), 0.0)


def _full(shape):
    nd = len(shape)
    return pl.BlockSpec(shape, lambda *_: (0,) * nd, pipeline_mode=pl.Buffered(1))


_ANY = pl.BlockSpec(memory_space=pl.ANY)


def _pos():
    return lax.axis_index("x"), lax.axis_index("y"), lax.axis_index("c")


def _other_chips(x, y):
    return [(1 - x, y), (x, 1 - y), (1 - x, 1 - y)]


class BgGather:
    SIB, XN, YN, FWD_Y, FWD_X, SIB_X, SIB_Y, SIB_D = range(8)

    def __init__(self, arrs, mids=(0.5, 0.75)):
        n = len(arrs)
        self.arrs = list(arrs)
        self.out_shape = [SDS((NDEV,) + a.shape, a.dtype) for a in arrs]
        self.scratch = [pltpu.SemaphoreType.DMA((n, 8)), pltpu.SemaphoreType.DMA((n, 8)),
                        pltpu.SemaphoreType.DMA((n,))]
        self.mids = mids
        self.result = None

    def mid_steps(self, nsteps):
        at = lambda f: min(nsteps - 1, max(0, int(f * nsteps) - 1))
        return [(at(self.mids[0]), self.mid), (max(at(self.mids[0]), at(self.mids[1])), self.mid2)]

    def _halves(self, a):
        rows = self.arrs[a].shape[0]
        cut = rows // 2 if rows >= 32 else rows
        return (0, cut), (cut, rows - cut)

    def _copy(self, ins, outs, sems, a, k, block, to, own=False, part=None):
        slot = 4 * block[0] + 2 * block[1] + block[2]
        rows = pl.ds(0, self.arrs[a].shape[0]) if part is None else pl.ds(*self._halves(a)[part])
        dst = outs[a].at[slot, rows]
        return pltpu.make_async_remote_copy(
            src_ref=ins[a].at[rows] if own else dst, dst_ref=dst, send_sem=sems[0].at[a, k],
            recv_sem=sems[1].at[a, k], device_id=to, device_id_type=MESH)

    def _mine(self, ins, outs, sems):
        x, y, c = _pos()
        return [pltpu.make_async_copy(ins[a], outs[a].at[4 * x + 2 * y + c], sems[2].at[a])
                for a in range(len(self.arrs))]

    def _split(self, a):
        return self._halves(a)[1][1] > 0

    def _sends(self, ins, outs, sems, phase):
        x, y, c = _pos()
        me, sib, xn, yn, dg = (x, y, c), (x, y, 1 - c), (1 - x, y, c), (x, 1 - y, c), (1 - x, 1 - y, c)
        cps = []
        for a in range(len(self.arrs)):
            cp = lambda k, block, to, **kw: self._copy(ins, outs, sems, a, k, block, to, **kw)
            if phase == 0:
                cps += [cp(self.SIB, me, sib, own=True), cp(self.XN, me, xn, own=True), cp(self.YN, me, yn, own=True)]
            elif phase == 1:
                cps.append(cp(self.FWD_Y, xn, yn, part=0))
                if self._split(a):
                    cps.append(cp(self.FWD_X, yn, xn, part=1))
                cps += [cp(self.SIB_X, xn, sib), cp(self.SIB_Y, yn, sib)]
            else:
                cps.append(cp(self.SIB_D, dg, sib))
        return cps

    def _arrivals(self, ins, outs, sems, phase):
        x, y, c = _pos()
        me, xn, yn, dg = (x, y, c), (1 - x, y, c), (x, 1 - y, c), (1 - x, 1 - y, c)
        cps = []
        for a in range(len(self.arrs)):
            cp = lambda k, block, **kw: self._copy(ins, outs, sems, a, k, block, me, **kw)
            if phase == 1:
                cps += [cp(self.XN, xn), cp(self.YN, yn)]
            elif phase == 2:
                cps.append(cp(self.FWD_Y, dg, part=0))
                if self._split(a):
                    cps.append(cp(self.FWD_X, dg, part=1))
            else:
                cps += [cp(self.SIB, (x, y, 1 - c)), cp(self.SIB_X, (1 - x, y, 1 - c)),
                        cp(self.SIB_Y, (x, 1 - y, 1 - c)), cp(self.SIB_D, (1 - x, 1 - y, 1 - c))]
        return cps

    def start(self, ins, outs, sems):
        for cp in self._mine(ins, outs, sems) + self._sends(ins, outs, sems, 0):
            cp.start()

    def mid(self, ins, outs, sems):
        for cp in self._arrivals(ins, outs, sems, 1):
            cp.wait_recv()
        for cp in self._sends(ins, outs, sems, 1):
            cp.start()

    def mid2(self, ins, outs, sems):
        for cp in self._arrivals(ins, outs, sems, 2):
            cp.wait_recv()
        for cp in self._sends(ins, outs, sems, 2):
            cp.start()

    def finish(self, ins, outs, sems):
        for cp in self._arrivals(ins, outs, sems, 3):
            cp.wait_recv()
        for ph in range(3):
            for cp in self._sends(ins, outs, sems, ph):
                cp.wait_send()
        for cp in self._mine(ins, outs, sems):
            cp.wait()


def sc_gather(arrs, collective_id, name):
    n = len(arrs)
    g = BgGather(arrs)
    srcs = [jax.new_ref(a, memory_space=pltpu.MemorySpace.HBM) for a in arrs]
    dsts = [jax.empty_ref(s, memory_space=pltpu.MemorySpace.HBM) for s in g.out_shape]

    @pl.kernel(mesh=plsc.ScalarSubcoreMesh(axis_name="sequencer", num_cores=1), name=name,
               scratch_types=tuple(g.scratch), compiler_params=pltpu.CompilerParams(collective_id=collective_id))
    def launch(send_sems, recv_sems, local_sems):
        x, y, c = _pos()
        barrier = pltpu.get_barrier_semaphore()
        for peer in ((x, y, 1 - c), (1 - x, y, c), (x, 1 - y, c)):
            pl.semaphore_signal(barrier, inc=1, device_id=peer, device_id_type=MESH)
        pl.semaphore_wait(barrier, 3)
        sems = (send_sems, recv_sems, local_sems)
        for phase in (g.start, g.mid, g.mid2, g.finish):
            phase(srcs, dsts, sems)

    launch()
    return [d[...] for d in dsts]


class BgPair:
    def __init__(self, arrs):
        n = len(arrs)
        self.arrs = list(arrs)
        self.out_shape = [SDS((4,) + a.shape[1:], a.dtype) for a in arrs]
        self.scratch = [pltpu.SemaphoreType.DMA((n, 4)), pltpu.SemaphoreType.DMA((n, 4))]
        self.result = None

    def mid_steps(self, nsteps):
        return []

    def _copies(self, ins, outs, sems):
        x, y, c = _pos()
        return [pltpu.make_async_remote_copy(
            src_ref=ins[a].at[2 * k + 1 - c], dst_ref=outs[a].at[k], send_sem=sems[0].at[a, k],
            recv_sem=sems[1].at[a, k], device_id=(x, y, 1 - c), device_id_type=MESH)
            for a in range(len(self.arrs)) for k in range(4)]

    def start(self, ins, outs, sems):
        for cp in self._copies(ins, outs, sems):
            cp.start()

    def finish(self, ins, outs, sems):
        cps = self._copies(ins, outs, sems)
        for cp in cps:
            cp.wait_recv()
        for cp in cps:
            cp.wait_send()


class BgChips(BgPair):
    def __init__(self, arrs):
        n = len(arrs)
        self.arrs = list(arrs)
        self.out_shape = [SDS((3,) + a.shape[1:], a.dtype) for a in arrs]
        self.scratch = [pltpu.SemaphoreType.DMA((n, 3)), pltpu.SemaphoreType.DMA((n, 3))]
        self.result = None

    def _copies(self, ins, outs, sems):
        x, y, c = _pos()
        return [pltpu.make_async_remote_copy(
            src_ref=ins[a].at[2 * px + py], dst_ref=outs[a].at[r], send_sem=sems[0].at[a, r],
            recv_sem=sems[1].at[a, r], device_id=(px, py, c), device_id_type=MESH)
            for a in range(len(self.arrs)) for r, (px, py) in enumerate(_other_chips(x, y))]


class AdamRider:
    def __init__(self, w, m, v, part, r2, layer=0, prev=None):
        self.arrs = [w, m, v, part, r2] + list(prev or [])
        self.n_prev = len(prev or [])
        self.layer = layer
        self.out_shape = [SDS(w.shape, f32)] * 4
        self.scratch = []
        self.aliases = {5 + k: k for k in range(self.n_prev)}
        self.result = None

    def _tile(self, grid):
        assert len(grid) == 1
        _, R, C = self.arrs[0].shape
        return R // grid[0], C

    def in_specs(self, grid):
        tr, C = self._tile(grid)
        layer = self.layer
        blk = pl.BlockSpec((None, tr, C), lambda b: (layer, b, 0))
        mine = pl.BlockSpec((None, tr, C), lambda b: (2 * lax.axis_index("x") + lax.axis_index("y"), b, 0))
        return [blk, blk, blk, mine, pl.BlockSpec((3, tr, C), lambda b: (0, b, 0))] + [_ANY] * self.n_prev

    def out_specs(self, grid):
        tr, C = self._tile(grid)
        layer = self.layer
        return [pl.BlockSpec((None, tr, C), lambda b: (layer, b, 0))] * 4

    def mid_steps(self, nsteps):
        return []

    def start(self, ins, outs, sems):
        pass

    finish = start

    def step(self, ins, outs, sems):
        w_ref, m_ref, v_ref, p_ref, r_ref = ins[:5]
        g = p_ref[...].astype(f32) + r_ref[0].astype(f32) + r_ref[1].astype(f32) + r_ref[2].astype(f32)
        d, m_, v_ = _adamw(w_ref[...], g, m_ref[...], v_ref[...])
        for ref, val in zip(outs, (g, d, m_, v_)):
            ref[...] = val


def _call(bgs, body, *, name, grid, in_specs, out_specs, out_shape, scratch_shapes=(), compiler_params=None):
    single = not isinstance(out_shape, (list, tuple))
    out_specs_l = [out_specs] if single else list(out_specs)
    out_shape_l = [out_shape] if single else list(out_shape)
    bgs = [b for b in (bgs or []) if b is not None]
    n_in, n_out, n_sc = len(in_specs), len(out_shape_l), len(scratch_shapes)
    nsteps = math.prod(grid)
    b_in_specs = [b.in_specs(grid) if hasattr(b, "in_specs") else [_ANY] * len(b.arrs) for b in bgs]
    b_out_specs = [b.out_specs(grid) if hasattr(b, "out_specs") else [_ANY] * len(b.out_shape) for b in bgs]
    aliases, i_off, o_off = {}, n_in, n_out
    for b in bgs:
        aliases.update({i_off + i: o_off + o for i, o in getattr(b, "aliases", {}).items()})
        i_off, o_off = i_off + len(b.arrs), o_off + len(b.out_shape)

    def full(*refs):
        pos = [0]

        def take(k):
            r = refs[pos[0]:pos[0] + k]
            pos[0] += k
            return r

        ins = take(n_in)
        b_ins = [take(len(b.arrs)) for b in bgs]
        outs = take(n_out)
        b_outs = [take(len(b.out_shape)) for b in bgs]
        sc = take(n_sc)
        b_sc = [take(len(b.scratch)) for b in bgs]
        if bgs:
            step = pl.program_id(0)
            for d in range(1, len(grid)):
                step = step * grid[d] + pl.program_id(d)

            @pl.when(step == 0)
            def _():
                for b, i_, o_, s_ in zip(bgs, b_ins, b_outs, b_sc):
                    b.start(i_, o_, s_)

        body(*ins, *outs, *sc)
        if bgs:
            for b, i_, o_, s_ in zip(bgs, b_ins, b_outs, b_sc):
                if hasattr(b, "step"):
                    b.step(i_, o_, s_)
                for at, fn in b.mid_steps(nsteps):
                    @pl.when(step == at)
                    def _():
                        fn(i_, o_, s_)

            @pl.when(step == nsteps - 1)
            def _():
                for b, i_, o_, s_ in zip(bgs, b_ins, b_outs, b_sc):
                    b.finish(i_, o_, s_)

    def run(*args):
        res = pl.pallas_call(
            full, name=name, grid=grid,
            in_specs=list(in_specs) + [s for l in b_in_specs for s in l],
            out_specs=out_specs_l + [s for l in b_out_specs for s in l],
            out_shape=out_shape_l + [s for b in bgs for s in b.out_shape],
            scratch_shapes=list(scratch_shapes) + [s for b in bgs for s in b.scratch],
            input_output_aliases=aliases,
            compiler_params=compiler_params,
        )(*args, *[a for b in bgs for a in b.arrs])
        rest = list(res[n_out:])
        for b in bgs:
            b.result, rest = rest[:len(b.out_shape)], rest[len(b.out_shape):]
        return res[0] if single else list(res[:n_out])

    return run


def s5_discretize(a_re, a_im, log_dt, b_re, b_im, c_re, c_im):
    lam_r = jnp.minimum(a_re, DT_MIN_LAMBDA)
    lam_i = a_im
    dt = jnp.exp(log_dt)[:, None]
    e = jnp.exp(lam_r * dt)
    lbr = e * jnp.cos(lam_i * dt)
    lbi = e * jnp.sin(lam_i * dt)
    den = lam_r * lam_r + lam_i * lam_i
    cf_r = ((lbr - 1.0) * lam_r + lbi * lam_i) / den
    cf_i = (lbi * lam_r - (lbr - 1.0) * lam_i) / den
    bb_r = cf_r[:, :, None] * b_re - cf_i[:, :, None] * b_im
    bb_i = cf_r[:, :, None] * b_im + cf_i[:, :, None] * b_re
    eye = jnp.eye(8, dtype=f32)

    def blk_b(m):
        return jnp.einsum('bgpc,gh->bgchp', m.reshape(8, 8, S5_P, S5_C), eye).reshape(8, 128, 512)

    def blk_c(m):
        return jnp.einsum('bgcp,gh->bgphc', m.reshape(8, 8, S5_C, S5_P), eye).reshape(8, 512, 128)

    bm = jnp.concatenate([blk_b(bb_r), blk_b(bb_i)], axis=-1)
    cm = jnp.concatenate([blk_c(c_re), -blk_c(c_im)], axis=1)
    lam = jnp.stack([lbr.reshape(8, 512), lbi.reshape(8, 512)], axis=1)
    lam = jnp.broadcast_to(lam[:, :, None, :], (8, 2, 8, 512))
    return lam, bm, cm


def _cmul(ar, ai, br, bi):
    return ar * br - ai * bi, ar * bi + ai * br


def _shift_rows(v, k, up):
    row = lax.broadcasted_iota(jnp.int32, v.shape, 0)
    if up:
        return jnp.where(row < 8 - k, pltpu.roll(v, 8 - k, 0), 0.0)
    return jnp.where(row >= k, pltpu.roll(v, k, 0), 0.0)


def _chunk_scan(S, lr, li, reverse, aux=None):
    z = jnp.zeros((8, 512), f32)
    U = 4

    def idx(i):
        return (S5_STEPS - 1 - i) if reverse else i

    def rows_of(s):
        return pl.ds(s * 8, 8) if isinstance(s, int) else pl.ds(pl.multiple_of(s * 8, 8), 8)

    def rec(xr, xi, row):
        br = S[row, 0:512]
        bi = S[row, 512:1024]
        return lr * xr - li * xi + br, lr * xi + li * xr + bi

    def step1(i, c):
        for u in range(U):
            c = rec(c[0], c[1], rows_of(idx(i * U + u)))
        return c

    er, ei = lax.fori_loop(0, S5_STEPS // U, step1, (z, z))
    ar, ai = lr, li
    for _ in range(8):
        ar, ai = _cmul(ar, ai, ar, ai)
    cr, ci = _shift_rows(er, 1, reverse), _shift_rows(ei, 1, reverse)
    for k in (1, 2, 4):
        sr, si = _shift_rows(cr, k, reverse), _shift_rows(ci, k, reverse)
        pr, pi_ = _cmul(ar, ai, sr, si)
        cr, ci = cr + pr, ci + pi_
        ar, ai = _cmul(ar, ai, ar, ai)

    if aux is None:
        def step2(i, c):
            for u in range(U):
                row = rows_of(idx(i * U + u))
                c = rec(c[0], c[1], row)
                S[row, 0:512] = c[0]
                S[row, 512:1024] = c[1]
            return c

        lax.fori_loop(0, S5_STEPS // U, step2, (cr, ci))
        return None

    def one(s, c):
        gr0, gi0, dr, di = c
        row = rows_of(s)
        gr, gi = rec(gr0, gi0, row)
        S[row, 0:512] = gr
        S[row, 512:1024] = gi
        prow = rows_of(s - 1)
        xr = aux[prow, 0:512]
        xi = aux[prow, 512:1024]
        return gr, gi, dr + gr * xr + gi * xi, di + gi * xr - gr * xi

    def step2(i, c):
        for u in range(U):
            c = one(S5_STEPS - 1 - (i * U + u), c)
        return c

    c = lax.fori_loop(0, S5_STEPS // U - 1, step2, (cr, ci, z, z))
    for s in range(U - 1, 0, -1):
        c = one(s, c)
    gr, gi, dr, di = c
    row0 = pl.ds(0, 8)
    gr, gi = rec(gr, gi, row0)
    S[row0, 0:512] = gr
    S[row0, 512:1024] = gi
    last = pl.ds((S5_STEPS - 1) * 8, 8)
    xr = _shift_rows(aux[last, 0:512], 1, False)
    xi = _shift_rows(aux[last, 512:1024], 1, False)
    dr = dr + gr * xr + gi * xi
    di = di + gi * xr - gr * xi
    return dr, di


_ROWS = 256


def _row_loop(fn):
    def body(r, c):
        fn(pl.ds(pl.multiple_of(r * _ROWS, _ROWS), _ROWS))
        return c
    lax.fori_loop(0, T // _ROWS, body, 0)


def s5_core_fwd(hn, bm, lam, cm, bg=()):
    def body(u_ref, b_ref, lam_ref, c_ref, ys_ref, S):
        def bu(rows):
            S[rows, :] = _dot(u_ref[rows, :], b_ref[...])
        _row_loop(bu)
        _chunk_scan(S, lam_ref[0], lam_ref[1], False)

        def ys(rows):
            ys_ref[rows, :] = _dot(S[rows, :].astype(bf16), c_ref[...])
        _row_loop(ys)

    return _call(
        bg, body, name="s5_core_fwd", grid=(S5_SUB,),
        in_specs=[pl.BlockSpec((T, 128), lambda b: (0, b)),
                  pl.BlockSpec((None, 128, 1024), lambda b: (b, 0, 0)),
                  pl.BlockSpec((None, 4, 8, 512), lambda b: (b, 0, 0, 0)),
                  pl.BlockSpec((None, 1024, 128), lambda b: (b, 0, 0))],
        out_specs=pl.BlockSpec((T, 128), lambda b: (0, b)),
        out_shape=SDS((T, D), f32),
        scratch_shapes=[pltpu.VMEM((T, 1024), f32)],
        compiler_params=_cp(dimension_semantics=("arbitrary",)),
    )(hn, bm, lam, cm)


_SEG = _ROWS // S5_CH


def _scan_tile(S, lr, li, k, carry, reverse, store, aux=None):
    steps = range(k * _SEG, (k + 1) * _SEG)
    for s in (reversed(steps) if reverse else steps):
        row = pl.ds(s * 8, 8)
        xr, xi = carry[0], carry[1]
        nr = lr * xr - li * xi + S[row, 0:512]
        ni = lr * xi + li * xr + S[row, 512:1024]
        if store:
            S[row, 0:512] = nr
            S[row, 512:1024] = ni
        if aux is not None and s >= 1:
            prow = pl.ds((s - 1) * 8, 8)
            pr, pi_ = aux[prow, 0:512], aux[prow, 512:1024]
            carry = (nr, ni, carry[2] + nr * pr + ni * pi_, carry[3] + ni * pr - nr * pi_)
        elif aux is not None:
            carry = (nr, ni, carry[2], carry[3])
        else:
            carry = (nr, ni)
    return carry


def _chunk_starts(er, ei, lr, li, reverse):
    ar, ai = lr, li
    for _ in range(8):
        ar, ai = _cmul(ar, ai, ar, ai)
    cr, ci = _shift_rows(er, 1, reverse), _shift_rows(ei, 1, reverse)
    for k in (1, 2, 4):
        sr, si = _shift_rows(cr, k, reverse), _shift_rows(ci, k, reverse)
        pr, pi_ = _cmul(ar, ai, sr, si)
        cr, ci = cr + pr, ci + pi_
        ar, ai = _cmul(ar, ai, ar, ai)
    return cr, ci


def s5_core_bwd(hn, dy, bm, lam, cm, bg=()):
    nt = T // _ROWS

    def body(u_ref, dy_ref, b_ref, lam_ref, c_ref, du_ref, db_ref, dct_ref, dlam_ref, S1, S2):
        lr, li, lcr, lci = lam_ref[0], lam_ref[1], lam_ref[2], lam_ref[3]
        z = jnp.zeros((8, 512), f32)
        tile = lambda k: pl.ds(k * _ROWS, _ROWS)
        dyb = lambda k: dy_ref[tile(k), :].astype(bf16)

        c = (z, z)
        for k in range(nt):
            S1[tile(k), :] = _dot(u_ref[tile(k), :], b_ref[...])
            if k >= 1:
                c = _scan_tile(S1, lr, li, k - 1, c, False, False)
        c = _scan_tile(S1, lr, li, nt - 1, c, False, False)

        c = _chunk_starts(c[0], c[1], lr, li, False)
        dct_ref[...] = jnp.zeros_like(dct_ref)
        for k in range(nt):
            c = _scan_tile(S1, lr, li, k, c, False, True)
            if k >= 1:
                dct_ref[...] += _dot_tn(dyb(k - 1), S1[tile(k - 1), :].astype(bf16))
        dct_ref[...] += _dot_tn(dyb(nt - 1), S1[tile(nt - 1), :].astype(bf16))

        S2[tile(nt - 1), :] = _dot_nt(dyb(nt - 1), c_ref[...])
        c = (z, z)
        for k in range(nt - 1, -1, -1):
            if k >= 1:
                S2[tile(k - 1), :] = _dot_nt(dyb(k - 1), c_ref[...])
            c = _scan_tile(S2, lcr, lci, k, c, True, False)

        def dbu(k):
            gb = S2[tile(k), :].astype(bf16)
            db_ref[...] += _dot_tn(u_ref[tile(k), :], gb)
            du_ref[tile(k), :] = _dot_nt(gb, b_ref[...])

        c = _chunk_starts(c[0], c[1], lcr, lci, True) + (z, z)
        db_ref[...] = jnp.zeros_like(db_ref)
        for k in range(nt - 1, -1, -1):
            c = _scan_tile(S2, lcr, lci, k, c, True, True, aux=S1)
            if k + 1 < nt:
                dbu(k + 1)
        dbu(0)
        gr, gi, dr, di = c
        last = pl.ds((S5_STEPS - 1) * 8, 8)
        xr = _shift_rows(S1[last, 0:512], 1, False)
        xi = _shift_rows(S1[last, 512:1024], 1, False)
        dlam_ref[0] = dr + gr * xr + gi * xi
        dlam_ref[1] = di + gi * xr - gr * xi

    return _call(
        bg, body, name="s5_core_bwd", grid=(S5_SUB,),
        in_specs=[pl.BlockSpec((T, 128), lambda b: (0, b)),
                  pl.BlockSpec((T, 128), lambda b: (0, b)),
                  pl.BlockSpec((None, 128, 1024), lambda b: (b, 0, 0)),
                  pl.BlockSpec((None, 4, 8, 512), lambda b: (b, 0, 0, 0)),
                  pl.BlockSpec((None, 1024, 128), lambda b: (b, 0, 0))],
        out_specs=[pl.BlockSpec((T, 128), lambda b: (0, b)),
                   pl.BlockSpec((None, 128, 1024), lambda b: (b, 0, 0)),
                   pl.BlockSpec((None, 128, 1024), lambda b: (b, 0, 0)),
                   pl.BlockSpec((None, 2, 8, 512), lambda b: (b, 0, 0, 0))],
        out_shape=[SDS((T, D), f32), SDS((8, 128, 1024), f32), SDS((8, 128, 1024), f32), SDS((8, 2, 8, 512), f32)],
        scratch_shapes=[pltpu.VMEM((T, 1024), f32), pltpu.VMEM((T, 1024), f32)],
        compiler_params=_cp(dimension_semantics=("arbitrary",)),
    )(hn, dy, bm, lam, cm)


TM = 512
NT = T // TM


def _tile(n=D):
    return pl.BlockSpec((TM, n), lambda i: (i, 0))


def s5_pre(xp, g):
    def body(x_ref, g_ref, hn_ref):
        hn, _ = _rms(x_ref[...], g_ref[...])
        hn_ref[...] = hn.astype(bf16)

    return pl.pallas_call(
        body, name="s5_pre", grid=(NT,), in_specs=[_tile(), _full((1, D))], out_specs=_tile(),
        out_shape=SDS((T, D), bf16), compiler_params=_cp(dimension_semantics=("arbitrary",)),
    )(xp, g)


def _gelu_grad(y):
    c = math.sqrt(2.0 / math.pi)
    t = jnp.tanh(c * (y + 0.044715 * y * y * y))
    return 0.5 * (1.0 + t) + 0.5 * y * (1.0 - t * t) * c * (1.0 + 3.0 * 0.044715 * y * y)


def s5_post(ys, xp, g, d, wglu, bglu, bg=()):
    def body(ys_ref, x_ref, g_ref, d_ref, w_ref, b_ref, y_ref, z_ref, h_ref):
        x = x_ref[...]
        hn, _ = _rms(x, g_ref[...])
        y = ys_ref[...] + d_ref[...] * hn
        y_ref[...] = y
        yg = jax.nn.gelu(y).astype(bf16)
        for j in range(4):
            cv = slice(j * 256, (j + 1) * 256)
            cg = slice(1024 + j * 256, 1024 + (j + 1) * 256)
            val = _dot(yg, w_ref[j]) + b_ref[:, cv]
            gate = _dot(yg, w_ref[j + 4]) + b_ref[:, cg]
            z_ref[:, cv] = val
            z_ref[:, cg] = gate
            h_ref[:, cv] = x[:, cv] + val * jax.nn.sigmoid(gate)

    return _call(
        bg, body, name="s5_post", grid=(NT,),
        in_specs=[_tile(), _tile(), _full((1, D)), _full((1, D)), _full((8, D, 256)), _full((1, 2 * D))],
        out_specs=[_tile(), _tile(2 * D), _tile()],
        out_shape=[SDS((T, D), f32), SDS((T, 2 * D), f32), SDS((T, D), f32)],
        compiler_params=_cp(dimension_semantics=("arbitrary",)),
    )(ys, xp, g, d, wglu, bglu)


def s5_post_bwd(dh, y, z, wglu, bg=()):
    def body(dh_ref, y_ref, z_ref, w_ref, dy_ref, dw_ref, db_ref, acc):
        i = pl.program_id(0)

        @pl.when(i == 0)
        def _():
            acc[...] = jnp.zeros_like(acc)
            db_ref[...] = jnp.zeros_like(db_ref)

        dh_ = dh_ref[...]
        y = y_ref[...]
        yg = jax.nn.gelu(y).astype(bf16)
        dyg = jnp.zeros((TM, D), f32)
        for j in range(4):
            cv = slice(j * 256, (j + 1) * 256)
            cg = slice(1024 + j * 256, 1024 + (j + 1) * 256)
            val = z_ref[:, cv]
            sg = jax.nn.sigmoid(z_ref[:, cg])
            dval = dh_[:, cv] * sg
            dgate = dh_[:, cv] * val * sg * (1.0 - sg)
            db_ref[:, cv] += _colsum8(dval)
            db_ref[:, cg] += _colsum8(dgate)
            dvb = dval.astype(bf16)
            dgb = dgate.astype(bf16)
            acc[j] += _dot_tn(yg, dvb)
            acc[j + 4] += _dot_tn(yg, dgb)
            dyg = dyg + _dot_nt(dvb, w_ref[j]) + _dot_nt(dgb, w_ref[j + 4])
        dy_ref[...] = dyg * _gelu_grad(y)

        @pl.when(i == NT - 1)
        def _():
            dw_ref[...] = acc[...].astype(bf16)

    return _call(
        bg, body, name="s5_post_bwd", grid=(NT,),
        in_specs=[_tile(), _tile(), _tile(2 * D), _full((8, D, 256))],
        out_specs=[_tile(), _full((8, D, 256)), _full((8, 2 * D))],
        out_shape=[SDS((T, D), f32), SDS((8, D, 256), bf16), SDS((8, 2 * D), f32)],
        scratch_shapes=[pltpu.VMEM((8, D, 256), f32)],
        compiler_params=_cp(dimension_semantics=("arbitrary",)),
    )(dh, y, z, wglu)


def s5_pre_bwd(xp, g, du, dy, d, dh, bg=()):
    def body(x_ref, g_ref, du_ref, dy_ref, d_ref, dh_ref, dx_ref, dg_ref, dd_ref):
        i = pl.program_id(0)

        @pl.when(i == 0)
        def _():
            dg_ref[...] = jnp.zeros_like(dg_ref)
            dd_ref[...] = jnp.zeros_like(dd_ref)

        x = x_ref[...]
        g = g_ref[...]
        dy = dy_ref[...]
        hn, _ = _rms(x, g)
        dhn = du_ref[...] + d_ref[...] * dy
        dx, dgt = _rms_bwd(x, g, dhn)
        dx_ref[...] = dh_ref[...] + dx
        dg_ref[...] += _colsum8(dgt)
        dd_ref[...] += _colsum8(dy * hn)

    return _call(
        bg, body, name="s5_pre_bwd", grid=(NT,),
        in_specs=[_tile(), _full((1, D)), _tile(), _tile(), _full((1, D)), _tile()],
        out_specs=[_tile(), _full((8, D)), _full((8, D))],
        out_shape=[SDS((T, D), f32), SDS((8, D), f32), SDS((8, D), f32)],
        compiler_params=_cp(dimension_semantics=("arbitrary",)),
    )(xp, g, du, dy, d, dh)


TMF = 1024


def mlp_fwd(h, g, w_in, w_out, layer, bg=()):
    def body(h_ref, g_ref, wi_ref, wo_ref, hm_ref, r_ref, out_ref, acc):
        j = pl.program_id(1)

        @pl.when(j == 0)
        def _():
            hm, _ = _rms(h_ref[...], g_ref[...])
            hm_ref[...] = hm.astype(bf16)
            acc[...] = jnp.zeros_like(acc)

        a = jnp.maximum(_dot(hm_ref[...], wi_ref[...]), 0.0)
        r_ref[...] = a.astype(bf16)
        acc[...] += _dot((a * a).astype(bf16), wo_ref[...])

        @pl.when(j == NDEV - 1)
        def _():
            out_ref[...] = h_ref[...] + acc[...]

    return _call(
        bg, body, name=f"mlp_fwd{layer}", grid=(T // TMF, NDEV),
        in_specs=[pl.BlockSpec((TMF, D), lambda i, j: (i, 0)),
                  pl.BlockSpec((1, D), lambda i, j: (0, 0)),
                  pl.BlockSpec((None, D, D_FF_SHARD), lambda i, j: (j, 0, 0)),
                  pl.BlockSpec((None, D_FF_SHARD, D), lambda i, j: (j, 0, 0))],
        out_specs=[pl.BlockSpec((TMF, D), lambda i, j: (i, 0)), pl.BlockSpec((TMF, D_FF_SHARD), lambda i, j: (i, j)),
                   pl.BlockSpec((TMF, D), lambda i, j: (i, 0))],
        out_shape=[SDS((T, D), bf16), SDS((T, NDEV * D_FF_SHARD), bf16), SDS((T, D), f32)],
        scratch_shapes=[pltpu.VMEM((TMF, D), f32)],
        compiler_params=_cp(dimension_semantics=("arbitrary", "arbitrary")),
    )(h, g, w_in, w_out)


def mlp_bwd(h, hm, r, g, dout, dout_b, w_in, w_out, layer, bg=()):
    last = NDEV - 1

    def body(h_ref, hm_ref, r_ref, g_ref, do_ref, dob_ref, wi_ref, wo_ref, dh_ref, dwi_ref, dwo_ref, dg_ref,
             dhm, awi, awo):
        j = pl.program_id(0)
        i = pl.program_id(1)
        rows = pl.ds(pl.multiple_of(i * TM, TM), TM)

        @pl.when(i == 0)
        def _():
            awi[...] = jnp.zeros_like(awi)
            awo[...] = jnp.zeros_like(awo)

        dz = (_dot_nt(dob_ref[...], wo_ref[...]) * (2.0 * r_ref[...].astype(f32))).astype(bf16)
        rb = r_ref[...]
        awo[...] += _dot_tn(rb * rb, dob_ref[...])
        awi[...] += _dot_tn(hm_ref[...], dz)
        part = _dot_nt(dz, wi_ref[...])

        @pl.when(j == 0)
        def _():
            dhm[rows, :] = part

        @pl.when(j > 0)
        def _():
            dhm[rows, :] += part

        @pl.when(i == NT - 1)
        def _():
            dwi_ref[...] = awi[...].astype(bf16)
            dwo_ref[...] = awo[...].astype(bf16)

        @pl.when(j == last)
        def _():
            @pl.when(i == 0)
            def _():
                dg_ref[...] = jnp.zeros_like(dg_ref)
            dx, dgt = _rms_bwd(h_ref[...], g_ref[...], dhm[rows, :])
            dh_ref[...] = do_ref[...] + dx
            dg_ref[...] += _colsum8(dgt)

    late = lambda j, i: (jnp.where(j == last, i, 0), 0)
    return _call(
        bg, body, name=f"mlp_bwd{layer}", grid=(NDEV, NT),
        in_specs=[pl.BlockSpec((TM, D), late),
                  pl.BlockSpec((TM, D), lambda j, i: (i, 0)),
                  pl.BlockSpec((TM, D_FF_SHARD), lambda j, i: (i, j)),
                  pl.BlockSpec((1, D), lambda j, i: (0, 0)),
                  pl.BlockSpec((TM, D), late),
                  pl.BlockSpec((TM, D), lambda j, i: (i, 0)),
                  pl.BlockSpec((None, D, D_FF_SHARD), lambda j, i: (j, 0, 0)),
                  pl.BlockSpec((None, D_FF_SHARD, D), lambda j, i: (j, 0, 0))],
        out_specs=[pl.BlockSpec((TM, D), late),
                   pl.BlockSpec((None, D, D_FF_SHARD), lambda j, i: (j, 0, 0)),
                   pl.BlockSpec((None, D_FF_SHARD, D), lambda j, i: (j, 0, 0)),
                   pl.BlockSpec((8, D), lambda j, i: (0, 0))],
        out_shape=[SDS((T, D), f32), SDS((NDEV, D, D_FF_SHARD), bf16), SDS((NDEV, D_FF_SHARD, D), bf16),
                   SDS((8, D), f32)],
        scratch_shapes=[pltpu.VMEM((T, D), f32), pltpu.VMEM((D, D_FF_SHARD), f32), pltpu.VMEM((D_FF_SHARD, D), f32)],
        compiler_params=_cp(dimension_semantics=("arbitrary", "arbitrary")),
    )(h, hm, r, g, dout, dout_b, w_in, w_out)


def _spread4():
    r = lax.broadcasted_iota(jnp.int32, (256, D), 0)
    c = lax.broadcasted_iota(jnp.int32, (256, D), 1)
    return ((c // 256 == r // HEAD_DIM) & (c % HEAD_DIM == r % HEAD_DIM)).astype(bf16)


def attn_pre(h, g_kv, g_mix, wkv, bkv, spread, wq, bq):
    def body(h_ref, gkv_ref, gm_ref, wkv_ref, bkv_ref, sp_ref, wq_ref, bq_ref, kvn_ref, hn_ref, k_ref, v_ref, q_ref):
        h_ = h_ref[...]
        kvn = _rms(h_, gkv_ref[...])[0].astype(bf16)
        hn = _rms(h_, gm_ref[...])[0].astype(bf16)
        kvn_ref[...] = kvn
        hn_ref[...] = hn
        kv = (_dot(kvn, wkv_ref[...]) + bkv_ref[...]).astype(bf16)
        k_ref[...] = _dot(kv[:, :256], sp_ref[...]).astype(bf16)
        v_ref[...] = _dot(kv[:, 256:], sp_ref[...]).astype(bf16)
        q_ref[...] = (_dot(hn, wq_ref[...]) + bq_ref[...]).astype(bf16)

    return pl.pallas_call(
        body, name="attn_pre", grid=(NT,),
        in_specs=[_tile(), _full((1, D)), _full((1, D)), _full((D, 512)), _full((1, 512)), _full((256, D)),
                  _full((D, D)), _full((1, D))],
        out_specs=[_tile()] * 5,
        out_shape=[SDS((T, D), bf16)] * 5,
        compiler_params=_cp(dimension_semantics=("arbitrary",)),
    )(h, g_kv, g_mix, wkv, bkv, spread, wq, bq)


def _attn_specs():
    cur = pl.BlockSpec((TM, 256), lambda j, n: (n, j))
    prev = pl.BlockSpec((BLK, 256), lambda j, n: (jnp.maximum(n * (TM // BLK) - 1, 0), j))
    return cur, prev


def _head_mask(g):
    lane = lax.broadcasted_iota(jnp.int32, (1, 256), 1)
    return (lane >= g * HEAD_DIM) & (lane < (g + 1) * HEAD_DIM)


def _stack_heads(t):
    return jnp.concatenate([jnp.where(_head_mask(g), t, 0) for g in range(Q_PER_KV)], axis=0)


def _unstack_heads(t):
    out = jnp.where(_head_mask(0), t[0:BLK], 0.0)
    for g in range(1, Q_PER_KV):
        out = out + jnp.where(_head_mask(g), t[g * BLK:(g + 1) * BLK], 0.0)
    return out


def _attn_probs(qs, k2, sinks, first):
    rows = Q_PER_KV * BLK
    s = _dot_nt(qs, k2) * (1.0 / math.sqrt(HEAD_DIM))
    qi = jnp.bitwise_and(lax.broadcasted_iota(jnp.int32, (rows, 2 * BLK), 0), BLK - 1)
    kj = lax.broadcasted_iota(jnp.int32, (rows, 2 * BLK), 1)
    diff = qi + BLK - kj
    valid = (diff >= 0) & (diff < BLK) & (jnp.logical_not(first) | (kj >= BLK))
    s = jnp.where(valid, s, -jnp.inf)
    rb = lax.broadcasted_iota(jnp.int32, (rows, 1), 0)
    sink = jnp.where(rb < BLK, sinks[0], jnp.where(rb < 2 * BLK, sinks[1], jnp.where(rb < 3 * BLK, sinks[2], sinks[3])))
    m = jnp.maximum(jnp.max(s, axis=-1, keepdims=True), sink)
    p = jnp.exp(s - m)
    ps = jnp.exp(sink - m)
    denom = jnp.sum(p, axis=-1, keepdims=True) + ps
    return p / denom, ps / denom


def _window_blocks(b, n, kc_ref, kp_ref, vc_ref, vp_ref):
    if b == 0:
        return (jnp.concatenate([kp_ref[...], kc_ref[0:BLK, :]], axis=0),
                jnp.concatenate([vp_ref[...], vc_ref[0:BLK, :]], axis=0), n == 0)
    rows = pl.ds((b - 1) * BLK, 2 * BLK)
    return kc_ref[rows, :], vc_ref[rows, :], False


def attn_core_fwd(q, k4, v4, sinks, bg=()):
    nb = TM // BLK

    def body(sink_ref, q_ref, kc_ref, kp_ref, vc_ref, vp_ref, o_ref):
        j = pl.program_id(0)
        n = pl.program_id(1)
        sk = [sink_ref[j * Q_PER_KV + g] for g in range(Q_PER_KV)]
        for b in range(nb):
            qb = q_ref[b * BLK:(b + 1) * BLK, :]
            k2, v2, first = _window_blocks(b, n, kc_ref, kp_ref, vc_ref, vp_ref)
            a, _ = _attn_probs(_stack_heads(qb), k2, sk, first)
            o_ref[b * BLK:(b + 1) * BLK, :] = _unstack_heads(_dot(a.astype(bf16), v2)).astype(bf16)

    cur, prev = _attn_specs()
    return _call(
        bg, body, name="attn_core_fwd", grid=(N_KV, NT),
        in_specs=[pl.BlockSpec(memory_space=pltpu.SMEM), cur, cur, prev, cur, prev],
        out_specs=cur, out_shape=SDS((T, D), bf16),
        compiler_params=_cp(dimension_semantics=("arbitrary", "arbitrary")),
    )(sinks, q, k4, k4, v4, v4)


def attn_post(h, o, wo, bo):
    def body(h_ref, o_ref, w_ref, b_ref, out_ref):
        out_ref[...] = h_ref[...] + _dot(o_ref[...], w_ref[...]) + b_ref[...]

    return pl.pallas_call(
        body, name="attn_post", grid=(NT,), in_specs=[_tile(), _tile(), _full((D, D)), _full((1, D))],
        out_specs=_tile(), out_shape=SDS((T, D), f32), compiler_params=_cp(dimension_semantics=("arbitrary",)),
    )(h, o, wo, bo)


def attn_bwd_pre(dh, o, wo, bg=()):
    def body(dh_ref, o_ref, w_ref, do_ref, dw_ref, db_ref, acc):
        i = pl.program_id(0)

        @pl.when(i == 0)
        def _():
            acc[...] = jnp.zeros_like(acc)
            db_ref[...] = jnp.zeros_like(db_ref)

        dh_ = dh_ref[...]
        dhb = dh_.astype(bf16)
        do_ref[...] = _dot_nt(dhb, w_ref[...]).astype(bf16)
        acc[...] += _dot_tn(o_ref[...], dhb)
        db_ref[...] += _colsum8(dh_)

        @pl.when(i == NT - 1)
        def _():
            dw_ref[...] = acc[...].astype(bf16)

    return _call(
        bg, body, name="attn_bwd_pre", grid=(NT,), in_specs=[_tile(), _tile(), _full((D, D))],
        out_specs=[_tile(), _full((D, D)), _full((8, D))],
        out_shape=[SDS((T, D), bf16), SDS((D, D), bf16), SDS((8, D), f32)],
        scratch_shapes=[pltpu.VMEM((D, D), f32)],
        compiler_params=_cp(dimension_semantics=("arbitrary",)),
    )(dh, o, wo)


def attn_core_bwd(q, do, k4, v4, sinks, bg=()):
    nb = TM // BLK

    def body(sink_ref, q_ref, do_ref, kc_ref, kp_ref, vc_ref, vp_ref, dq_ref, dk_ref, dv_ref, ds_ref):
        j = pl.program_id(0)
        n = pl.program_id(1)

        @pl.when(n == 0)
        def _():
            dk_ref[...] = jnp.zeros_like(dk_ref)
            dv_ref[...] = jnp.zeros_like(dv_ref)
            ds_ref[...] = jnp.zeros_like(ds_ref)

        lane8 = lax.broadcasted_iota(jnp.int32, (8, 128), 1)
        row8 = lax.broadcasted_iota(jnp.int32, (8, 128), 0)
        sk = [sink_ref[j * Q_PER_KV + g] for g in range(Q_PER_KV)]
        for b in range(nb):
            qs = _stack_heads(q_ref[b * BLK:(b + 1) * BLK, :])
            dos = _stack_heads(do_ref[b * BLK:(b + 1) * BLK, :])
            k2, v2, first = _window_blocks(b, n, kc_ref, kp_ref, vc_ref, vp_ref)
            a, asink = _attn_probs(qs, k2, sk, first)
            dp = _dot_nt(dos, v2)
            dd = jnp.sum(a * dp, axis=-1, keepdims=True)
            dsc = (a * (dp - dd) * (1.0 / math.sqrt(HEAD_DIM))).astype(bf16)
            t = asink * dd
            for g in range(Q_PER_KV):
                dsink = -jnp.sum(t[g * BLK:(g + 1) * BLK], axis=0, keepdims=True)
                ds_ref[...] += jnp.where((lane8 == g) & (row8 == 0), jnp.broadcast_to(dsink, (8, 128)), 0.0)
            dq_ref[b * BLK:(b + 1) * BLK, :] = _unstack_heads(_dot(dsc, k2))
            dk2 = _dot_tn(dsc, qs)
            dv2 = _dot_tn(a.astype(bf16), dos)
            cur = pl.ds(pl.multiple_of(n * TM + b * BLK, BLK), BLK)
            dk_ref[cur, :] += dk2[BLK:, :]
            dv_ref[cur, :] += dv2[BLK:, :]
            if b == 0:
                @pl.when(n > 0)
                def _():
                    prv = pl.ds(pl.multiple_of(n * TM - BLK, BLK), BLK)
                    dk_ref[prv, :] += dk2[:BLK, :]
                    dv_ref[prv, :] += dv2[:BLK, :]
            else:
                prv = pl.ds(pl.multiple_of(n * TM + (b - 1) * BLK, BLK), BLK)
                dk_ref[prv, :] += dk2[:BLK, :]
                dv_ref[prv, :] += dv2[:BLK, :]

    cur, prev = _attn_specs()
    col = pl.BlockSpec((T, 256), lambda j, n: (0, j))
    return _call(
        bg, body, name="attn_core_bwd", grid=(N_KV, NT),
        in_specs=[pl.BlockSpec(memory_space=pltpu.SMEM), cur, cur, cur, prev, cur, prev],
        out_specs=[cur, col, col, pl.BlockSpec((None, 8, 128), lambda j, n: (j, 0, 0))],
        out_shape=[SDS((T, D), f32), SDS((T, D), f32), SDS((T, D), f32), SDS((N_KV, 8, 128), f32)],
        compiler_params=_cp(dimension_semantics=("arbitrary", "arbitrary")),
    )(sinks, q, do, k4, k4, v4, v4)


def attn_bwd_q(h, dh, dq, hn, g_mix, wq):
    def body(h_ref, dh_ref, dq_ref, hn_ref, gm_ref, wq_ref, out_ref, dwq_ref, dbq_ref, dgm_ref, aq):
        i = pl.program_id(0)

        @pl.when(i == 0)
        def _():
            aq[...] = jnp.zeros_like(aq)
            dbq_ref[...] = jnp.zeros_like(dbq_ref)
            dgm_ref[...] = jnp.zeros_like(dgm_ref)

        dq_ = dq_ref[...]
        dqb = dq_.astype(bf16)
        aq[...] += _dot_tn(hn_ref[...], dqb)
        dbq_ref[...] += _colsum8(dq_)
        dx, dg = _rms_bwd(h_ref[...], gm_ref[...], _dot_nt(dqb, wq_ref[...]))
        out_ref[...] = dh_ref[...] + dx
        dgm_ref[...] += _colsum8(dg)

        @pl.when(i == NT - 1)
        def _():
            dwq_ref[...] = aq[...].astype(bf16)

    vec = _full((8, D))
    mat = _full((D, D))
    return pl.pallas_call(
        body, name="attn_bwd_q", grid=(NT,),
        in_specs=[_tile()] * 4 + [_full((1, D)), mat],
        out_specs=[_tile(), mat, vec, vec],
        out_shape=[SDS((T, D), f32), SDS((D, D), bf16), SDS((8, D), f32), SDS((8, D), f32)],
        scratch_shapes=[pltpu.VMEM((D, D), f32)],
        compiler_params=_cp(dimension_semantics=("arbitrary",)),
    )(h, dh, dq, hn, g_mix, wq)


def attn_bwd_kv(h, dh, dk4, dv4, kvn, g_kv, wkv, spread):
    def body(h_ref, dh_ref, dk_ref, dv_ref, kvn_ref, gkv_ref, wkv_ref, sp_ref, out_ref, outb_ref, dw_ref, db_ref,
             dgkv_ref, acc):
        i = pl.program_id(0)

        @pl.when(i == 0)
        def _():
            for r in (acc, db_ref, dgkv_ref):
                r[...] = jnp.zeros_like(r)

        dkv = jnp.concatenate([_dot_nt(dk_ref[...].astype(bf16), sp_ref[...]),
                               _dot_nt(dv_ref[...].astype(bf16), sp_ref[...])], axis=1)
        dkvb = dkv.astype(bf16)
        acc[...] += _dot_tn(kvn_ref[...], dkvb)
        db_ref[...] += _colsum8(dkv)
        dx, dg = _rms_bwd(h_ref[...], gkv_ref[...], _dot_nt(dkvb, wkv_ref[...]))
        out = dh_ref[...] + dx
        out_ref[...] = out
        outb_ref[...] = out.astype(bf16)
        dgkv_ref[...] += _colsum8(dg)

        @pl.when(i == NT - 1)
        def _():
            dw_ref[...] = acc[...].astype(bf16)

    return pl.pallas_call(
        body, name="attn_bwd_kv", grid=(NT,),
        in_specs=[_tile()] * 5 + [_full((1, D)), _full((D, 512)), _full((256, D))],
        out_specs=[_tile(), _tile(), _full((D, 512)), _full((8, 512)), _full((8, D))],
        out_shape=[SDS((T, D), f32), SDS((T, D), bf16), SDS((D, 512), bf16), SDS((8, 512), f32), SDS((8, D), f32)],
        scratch_shapes=[pltpu.VMEM((D, 512), f32)],
        compiler_params=_cp(dimension_semantics=("arbitrary",)),
    )(h, dh, dk4, dv4, kvn, g_kv, wkv, spread)


def final_loss(h, g, target):
    def body(h_ref, g_ref, t_ref, loss_ref, dh_ref, dhb_ref, dg_ref):
        i = pl.program_id(0)

        @pl.when(i == 0)
        def _():
            loss_ref[...] = jnp.zeros_like(loss_ref)
            dg_ref[...] = jnp.zeros_like(dg_ref)

        h_ = h_ref[...]
        g_ = g_ref[...]
        y, _ = _rms(h_, g_)
        diff = y - t_ref[...]
        per_tok = jnp.mean(diff * diff, axis=-1, keepdims=True)
        tot = 0.5 * jnp.sum(per_tok, axis=0, keepdims=True)
        lane = lax.broadcasted_iota(jnp.int32, (8, 128), 1)
        row = lax.broadcasted_iota(jnp.int32, (8, 128), 0)
        loss_ref[...] += jnp.where((lane == 0) & (row == 0), jnp.broadcast_to(tot, (8, 128)), 0.0)
        dx, dgt = _rms_bwd(h_, g_, diff * (1.0 / D))
        dh_ref[...] = dx
        dhb_ref[...] = dx.astype(bf16)
        dg_ref[...] += _colsum8(dgt)

    return pl.pallas_call(
        body, name="final_loss", grid=(NT,), in_specs=[_tile(), _full((1, D)), _tile()],
        out_specs=[_full((8, 128)), _tile(), _tile(), _full((8, D))],
        out_shape=[SDS((8, 128), f32), SDS((T, D), f32), SDS((T, D), bf16), SDS((8, D), f32)],
        compiler_params=_cp(dimension_semantics=("arbitrary",)),
    )(h, g, target)


def _to_chunked(a):
    return a.reshape(S5_CH, S5_STEPS, a.shape[-1]).transpose(1, 0, 2).reshape(T, a.shape[-1])


def _from_chunked(a):
    return a.reshape(S5_STEPS, S5_CH, a.shape[-1]).transpose(1, 0, 2).reshape(T, a.shape[-1])


def _rep4(w):
    return jnp.broadcast_to(w.reshape(w.shape[0], N_KV, 1, HEAD_DIM), (w.shape[0], N_KV, Q_PER_KV, HEAD_DIM)).reshape(
        w.shape[0], N_KV * Q_PER_KV * HEAD_DIM)


def _fold4(w):
    return w.reshape(w.shape[0], N_KV, Q_PER_KV, HEAD_DIM).sum(axis=2).reshape(w.shape[0], N_KV * HEAD_DIM)


def fwd_bwd(x, target, p, shards, opt, core, chip):
    row = lambda v: v.reshape(1, -1)
    (lam, bm, cm), prep_vjp = jax.vjp(s5_discretize, p["s5_a_re"][0], p["s5_a_im"][0], p["s5_log_dt"][0],
                                      p["s5_b_re"][0], p["s5_b_im"][0], p["s5_c_re"][0], p["s5_c_im"][0])
    bmb, cmb = bm.astype(bf16), cm.astype(bf16)
    lam = jnp.concatenate([lam, lam * jnp.array([1.0, -1.0], f32).reshape(1, 2, 1, 1)], axis=1)
    g_mix0, g_mix1 = row(p["norm_mix"][0]), row(p["norm_mix"][1])
    g_mlp0, g_mlp1 = row(p["norm_mlp"][0]), row(p["norm_mlp"][1])
    g_kv, g_fin = row(p["norm_kv"]), row(p["norm_final"])
    bq, bo = p["b_q"], p["b_o"]
    bkv = row(p["b_kv"])
    spread = _spread4()
    sinks = p["sinks"].reshape(16)

    def reduce_pairs(names, bg):
        return [add_pairs(g, r, core, f"add_pairs_{n}") for n, g, r in zip(names, bg.arrs, bg.result)]

    wglu, gvec, win0, wout0 = sc_gather(
        [shards["s5_w_glu"], shards["vecs"], shards["w_in0"], shards["w_out0"]], 3, "sc_gather_layer0")
    wkv, wq, wo, win1 = sc_gather([shards["w_kv"], shards["w_q"], shards["w_o"], shards["w_in1"]], 4, "sc_gather_attn")
    wout1, = sc_gather([shards["w_out1"]], 5, "sc_gather_w_out1")
    xp = _to_chunked(x)
    hn0 = s5_pre(xp, g_mix0)
    ys = s5_core_fwd(hn0, bmb, lam, cmb)
    d_skip = gvec[:, 0, :128].reshape(1, D)
    bglu = gvec[:, 0, 128:].reshape(1, 2 * D)
    y, z, h1 = s5_post(ys, xp, g_mix0, d_skip, wglu, bglu)
    hm0, r0, h2p = mlp_fwd(h1, g_mlp0, win0, wout0, 0)
    wkv, wq, wo = wkv.reshape(D, 512), wq.reshape(D, D), wo.reshape(D, D)
    h2 = _from_chunked(h2p)
    kvn, hn1, k4, v4, q = attn_pre(h2, g_kv, g_mix1, wkv, bkv, spread, wq, bq)
    o = attn_core_fwd(q, k4, v4, sinks)
    h3 = attn_post(h2, o, wo, bo)
    hm1, r1, h4 = mlp_fwd(h3, g_mlp1, win1, wout1, 1)
    loss, dh4, dh4b, dg_fin = final_loss(h4, g_fin, target)

    big = {}
    rider = lambda n, part, r2, **kw: AdamRider(*opt[n], part, r2, **kw)
    dh3, dwin1, dwout1, dg_mlp1 = mlp_bwd(h3, hm1, r1, g_mlp1, dh4, dh4b, win1, wout1, 1)
    pa = BgPair([dwin1, dwout1])
    do, dwo, dbo = attn_bwd_pre(dh3, o, wo, bg=[pa])
    p_in1, p_out1 = reduce_pairs(["w_in1", "w_out1"], pa)
    ca = BgChips([p_in1])
    dq, dk4, dv4, dsink = attn_core_bwd(q, do, k4, v4, sinks, bg=[ca])
    dh2, dwq, dbq, dg_mix1 = attn_bwd_q(h2, dh3, dq, hn1, g_mix1, wq)
    dh2, dh2b, dwkv, dbkv, dg_kv = attn_bwd_kv(h2, dh2, dk4, dv4, kvn, g_kv, wkv, spread)
    pb = BgPair([dwkv.reshape(NDEV, 128, 512), dwq.reshape(NDEV, 128, D), dwo.reshape(NDEV, 128, D)])
    ca2 = BgChips([p_out1])
    dh2p, dh2pb = _to_chunked(dh2), _to_chunked(dh2b)
    dh1, dwin0, dwout0, dg_mlp0 = mlp_bwd(h1, hm0, r0, g_mlp0, dh2p, dh2pb, win0, wout0, 0, bg=[pb, ca2])
    cb = BgChips(reduce_pairs(["w_kv", "w_q", "w_o"], pb))
    pc = BgPair([dwin0, dwout0])
    dy, dwglu, dbglu = s5_post_bwd(dh1, y, z, wglu, bg=[cb, pc])
    cc = BgChips(reduce_pairs(["w_in0", "w_out0"], pc))
    pd = BgPair([dwglu])
    a_in1 = rider("w_mlp_in", p_in1, ca.result[0], layer=1)
    a_out1 = rider("w_mlp_out", p_out1, ca2.result[0], layer=1)
    a_attn = [rider(n, part, r2) for n, part, r2 in zip(("w_kv", "w_q", "w_o"), cb.arrs, cb.result)]
    du, dbm, dcmt, dlam = s5_core_bwd(hn0, dy, bmb, lam, cmb, bg=[cc, pd, a_in1, a_out1] + a_attn)
    big["w_kv"], big["w_q"], big["w_o"] = [a.result for a in a_attn]
    cd = BgChips(reduce_pairs(["s5_w_glu"], pd))
    a_in0 = rider("w_mlp_in", cc.arrs[0], cc.result[0], layer=0, prev=a_in1.result)
    a_out0 = rider("w_mlp_out", cc.arrs[1], cc.result[1], layer=0, prev=a_out1.result)
    dxp, dg_mix0, dd = s5_pre_bwd(xp, g_mix0, du, dy, d_skip, dh1, bg=[cd, a_in0, a_out0])
    big["w_mlp_in"], big["w_mlp_out"] = a_in0.result, a_out0.result
    big["s5_w_glu"] = adam_big(*opt["s5_w_glu"], cd.arrs[0], cd.result[0], chip, "adam_s5_w_glu")
    grad_x = _from_chunked(dxp)
    da_re, da_im, dlog_dt, db_re, db_im, dc_re, dc_im = prep_vjp((dlam, dbm, dcmt.transpose(0, 2, 1)))

    def lanes(v_):
        v_ = v_.reshape(1, -1)
        return jnp.pad(v_, ((0, 0), (0, D - v_.shape[1])))

    small = jnp.concatenate([
        dg_mix0[0:1], dg_mix1[0:1], dg_mlp0[0:1], dg_mlp1[0:1], dg_kv[0:1], dg_fin[0:1], dd[0:1], dbq[0:1], dbo[0:1],
        dbglu[0:1].reshape(2, D), lanes(dbkv[0:1]),
        lanes(dsink[:, 0, :Q_PER_KV]), lanes(dlog_dt), lanes(loss[0:1, 0:1]), jnp.zeros((1, D), f32),
        da_re.reshape(4, D), da_im.reshape(4, D),
        db_re.transpose(0, 2, 1).reshape(64, D), db_im.transpose(0, 2, 1).reshape(64, D),
        dc_re.reshape(64, D), dc_im.reshape(64, D)], axis=0)
    return loss, grad_x, small, big


_ANY = pl.BlockSpec(memory_space=pl.ANY)


def _pos():
    return lax.axis_index("x"), lax.axis_index("y"), lax.axis_index("c")


def _other_chips(x, y):
    return [(1 - x, y), (x, 1 - y), (1 - x, 1 - y)]


def all_gather(arrs):
    n = len(arrs)

    def body(*refs):
        ins, outs = refs[:n], refs[n:2 * n]
        send_sems, recv_sems, local_sems = refs[2 * n:]
        x, y, c = _pos()
        me, sib = (x, y, c), (x, y, 1 - c)
        chips = _other_chips(x, y)

        def copy(a, k, block, to, src=None):
            dst = outs[a].at[4 * block[0] + 2 * block[1] + block[2]]
            return pltpu.make_async_remote_copy(
                src_ref=dst if src is None else src, dst_ref=dst, send_sem=send_sems.at[a, k],
                recv_sem=recv_sems.at[a, k], device_id=to, device_id_type=MESH)

        mine = [pltpu.make_async_copy(ins[a], outs[a].at[4 * x + 2 * y + c], local_sems.at[a]) for a in range(n)]
        for cp in mine:
            cp.start()
        first = []
        for a in range(n):
            first.append(copy(a, 0, me, sib, src=ins[a]))
            first += [copy(a, 1 + j, me, (*chip, c), src=ins[a]) for j, chip in enumerate(chips)]
        for cp in first:
            cp.start()
        passed = []
        for j, chip in enumerate(chips):
            for a in range(n):
                copy(a, 1 + j, (*chip, c), me).wait_recv()
                cp = copy(a, 4 + j, (*chip, c), sib)
                cp.start()
                passed.append(cp)
        for a in range(n):
            copy(a, 0, sib, me).wait_recv()
            for j, chip in enumerate(chips):
                copy(a, 4 + j, (*chip, 1 - c), me).wait_recv()
        for cp in first + passed:
            cp.wait_send()
        for cp in mine:
            cp.wait()

    return pl.pallas_call(
        body, name="all_gather", in_specs=[_ANY] * n, out_specs=[_ANY] * n,
        out_shape=[SDS((NDEV,) + a.shape, a.dtype) for a in arrs],
        scratch_shapes=[pltpu.SemaphoreType.DMA((n, 7)), pltpu.SemaphoreType.DMA((n, 7)),
                        pltpu.SemaphoreType.DMA((n,))],
    )(*arrs)


def rs_pair(grads):
    n = len(grads)

    def body(*refs):
        ins, outs = refs[:n], refs[n:2 * n]
        send_sems, recv_sems = refs[2 * n:]
        x, y, c = _pos()
        cps = []
        for a in range(n):
            for k in range(4):
                cps.append(pltpu.make_async_remote_copy(
                    src_ref=ins[a].at[2 * k + 1 - c], dst_ref=outs[a].at[k], send_sem=send_sems.at[a, k],
                    recv_sem=recv_sems.at[a, k], device_id=(x, y, 1 - c), device_id_type=MESH))
        for cp in cps:
            cp.start()
        for cp in cps:
            cp.wait_recv()
        for cp in cps:
            cp.wait_send()

    return pl.pallas_call(
        body, name="rs_pair", in_specs=[_ANY] * n, out_specs=[_ANY] * n,
        out_shape=[SDS((4,) + g.shape[1:], g.dtype) for g in grads],
        scratch_shapes=[pltpu.SemaphoreType.DMA((n, 4)), pltpu.SemaphoreType.DMA((n, 4))],
    )(*grads)


def rs_chips(parts):
    n = len(parts)

    def body(*refs):
        ins, outs = refs[:n], refs[n:2 * n]
        send_sems, recv_sems = refs[2 * n:]
        x, y, c = _pos()
        cps = []
        for a in range(n):
            for r, (px, py) in enumerate(_other_chips(x, y)):
                cps.append(pltpu.make_async_remote_copy(
                    src_ref=ins[a].at[2 * px + py], dst_ref=outs[a].at[r], send_sem=send_sems.at[a, r],
                    recv_sem=recv_sems.at[a, r], device_id=(px, py, c), device_id_type=MESH))
        for cp in cps:
            cp.start()
        for cp in cps:
            cp.wait_recv()
        for cp in cps:
            cp.wait_send()

    return pl.pallas_call(
        body, name="rs_chips", in_specs=[_ANY] * n, out_specs=[_ANY] * n,
        out_shape=[SDS((3,) + g.shape[1:], g.dtype) for g in parts],
        scratch_shapes=[pltpu.SemaphoreType.DMA((n, 3)), pltpu.SemaphoreType.DMA((n, 3))],
    )(*parts)


def _row_tile(r, c):
    return min(r, max(8, (512 * 1024) // c))


def add_pairs(g, r1, core, name):
    _, R, C = g.shape
    tr = _row_tile(R, C)

    def body(core_ref, g_ref, r_ref, o_ref):
        o_ref[...] = (g_ref[...].astype(f32) + r_ref[...].astype(f32)).astype(bf16)

    return pl.pallas_call(
        body, name=name, out_shape=SDS((4, R, C), bf16),
        grid_spec=pltpu.PrefetchScalarGridSpec(
            num_scalar_prefetch=1, grid=(4, R // tr),
            in_specs=[pl.BlockSpec((None, tr, C), lambda k, i, core: (2 * k + core[0], i, 0)),
                      pl.BlockSpec((None, tr, C), lambda k, i, core: (k, i, 0))],
            out_specs=pl.BlockSpec((None, tr, C), lambda k, i, core: (k, i, 0))),
        compiler_params=_cp(dimension_semantics=("arbitrary", "arbitrary")),
    )(core, g, r1)


def _adamw(w, g, m, v):
    m = ADAM_B1 * m + (1.0 - ADAM_B1) * g
    v = ADAM_B2 * v + (1.0 - ADAM_B2) * (g * g)
    m_hat = m / (1.0 - ADAM_B1 ** ADAM_STEP)
    v_hat = v / (1.0 - ADAM_B2 ** ADAM_STEP)
    delta = -ADAM_LR * (m_hat / (jnp.sqrt(v_hat) + ADAM_EPS) + ADAM_WD * w)
    return delta, m, v


def adam_big(w, m, v, part, r2, chip, name, layer=0, prev=None):
    L, R, C = w.shape
    tr = _row_tile(R, C)

    def body(chip_ref, w_ref, m_ref, v_ref, p_ref, r_ref, *rest):
        g_out, d_out, m_out, v_out = rest[-4:]
        g = p_ref[...].astype(f32) + r_ref[0].astype(f32) + r_ref[1].astype(f32) + r_ref[2].astype(f32)
        d, m_, v_ = _adamw(w_ref[...], g, m_ref[...], v_ref[...])
        g_out[...] = g
        d_out[...] = d
        m_out[...] = m_
        v_out[...] = v_

    blk = pl.BlockSpec((None, tr, C), lambda i, chip: (layer, i, 0))
    extra = [] if prev is None else list(prev)
    return pl.pallas_call(
        body, name=name, out_shape=[SDS((L, R, C), f32)] * 4,
        grid_spec=pltpu.PrefetchScalarGridSpec(
            num_scalar_prefetch=1, grid=(R // tr,),
            in_specs=[blk, blk, blk,
                      pl.BlockSpec((None, tr, C), lambda i, chip: (chip[0], i, 0)),
                      pl.BlockSpec((3, tr, C), lambda i, chip: (0, i, 0))] + [_ANY] * len(extra),
            out_specs=[blk] * 4),
        input_output_aliases={6 + k: k for k in range(len(extra))},
        compiler_params=_cp(dimension_semantics=("arbitrary",)),
    )(chip, w, m, v, part, r2, *extra)


def allreduce_small(buf, chips=None):
    shp = buf.shape
    half = (shp[0] // 16) * 8
    parts = (pl.ds(0, half), pl.ds(half, shp[0] - half))
    n_c = 0 if chips is None else len(chips.arrs)

    def body(in_ref, *refs):
        c_in, out_ref, c_out = refs[:n_c], refs[n_c], refs[n_c + 1:2 * n_c + 1]
        acc1, acc2, r0, r1, r2, send_sems, recv_sems = refs[2 * n_c + 1:2 * n_c + 8]
        c_sems = refs[2 * n_c + 8:]
        if chips is not None:
            chips.start(c_in, c_out, c_sems)
        x, y, c = _pos()
        across = [(1 - x, y, c), (x, 1 - y, c)]

        def exchange(src, rcv, dst, copies):
            cps = [pltpu.make_async_remote_copy(
                src_ref=src.at[rows], dst_ref=rcv.at[rows], send_sem=send_sems.at[k], recv_sem=recv_sems.at[k],
                device_id=peer, device_id_type=MESH) for k, rows, peer in copies]
            for cp in cps:
                cp.start()
            for cp in cps:
                cp.wait()
            dst[...] = src[...] + rcv[...]

        exchange(in_ref, r0, acc1, [(0, pl.ds(0, shp[0]), (x, y, 1 - c))])
        exchange(acc1, r1, acc2, [(1, parts[0], across[0]), (2, parts[1], across[1])])
        exchange(acc2, r2, out_ref, [(3, parts[0], across[1]), (4, parts[1], across[0])])
        if chips is not None:
            chips.finish(c_in, c_out, c_sems)

    vm = pl.BlockSpec(memory_space=pltpu.VMEM)
    res = pl.pallas_call(
        body, name="allreduce_small", in_specs=[vm] + [_ANY] * n_c, out_specs=[vm] + [_ANY] * n_c,
        out_shape=[SDS(shp, f32)] + ([] if chips is None else chips.out_shape),
        scratch_shapes=[pltpu.VMEM(shp, f32)] * 5 + [pltpu.SemaphoreType.DMA((5,)), pltpu.SemaphoreType.DMA((5,))]
        + ([] if chips is None else chips.scratch),
    )(buf, *([] if chips is None else chips.arrs))
    if chips is not None:
        chips.result = list(res[1:])
    return res[0]


SMALL_ROWS = {'norm_mix': (0, 2, D), 'norm_mlp': (2, 2, D), 'norm_kv': (4, 1, D), 'norm_final': (5, 1, D),
              's5_d': (6, 1, D), 'b_q': (7, 1, D), 'b_o': (8, 1, D), 's5_b_glu': (9, 2, D), 'b_kv': (11, 1, 512),
              'sinks': (12, 1, 16), 's5_log_dt': (13, 1, 64), 's5_a_re': (16, 4, D), 's5_a_im': (20, 4, D),
              's5_b_re': (24, 64, D), 's5_b_im': (88, 64, D), 's5_c_re': (152, 64, D), 's5_c_im': (216, 64, D)}
LOSS_ROW = 14
ROW_PARAMS = ['norm_mix', 'norm_mlp', 'norm_kv', 'norm_final', 'b_q', 'b_o', 'b_kv', 'sinks', 's5_log_dt']
SHARD_PARAMS = ['s5_d', 's5_b_glu']
S5_PARAMS = ['s5_a_re', 's5_a_im', 's5_b_re', 's5_b_im', 's5_c_re', 's5_c_im']


def adam_small(dev, gsum, s5_grads, w, m, v):
    names = ROW_PARAMS + SHARD_PARAMS + S5_PARAMS
    n_g = len(ROW_PARAMS) + len(SHARD_PARAMS)

    def body(dev_ref, gs_ref, *refs):
        pos = [0]

        def take(k):
            r = refs[pos[0]:pos[0] + k]
            pos[0] += k
            return r

        g5 = take(len(S5_PARAMS))
        wr, mr, vr = take(len(names)), take(len(names)), take(len(names))
        g_out = take(n_g)
        d_out, m_out, v_out = take(len(names)), take(len(names)), take(len(names))
        dv = dev_ref[0]
        for i, n in enumerate(names):
            if n in S5_PARAMS:
                g = g5[S5_PARAMS.index(n)][...]
            elif n in SHARD_PARAMS:
                r0, _, _ = SMALL_ROWS[n]
                ln = wr[i].shape[1]
                g = jnp.zeros((1, ln), f32)
                for k in range(NDEV):
                    off = k * ln
                    piece = gs_ref[r0 + off // D:r0 + off // D + 1, off % D:off % D + ln]
                    g = g + jnp.where(dv == k, piece, 0.0)
                g_out[i][...] = g
            else:
                r0, nr, nl = SMALL_ROWS[n]
                g = gs_ref[r0:r0 + nr, 0:nl]
                g_out[i][...] = g
            d, m_, v_ = _adamw(wr[i][...], g, mr[i][...], vr[i][...])
            d_out[i][...] = d
            m_out[i][...] = m_
            v_out[i][...] = v_

    vm = pl.BlockSpec(memory_space=pltpu.VMEM)
    ins = [s5_grads[n] for n in S5_PARAMS] + [d[n] for d in (w, m, v) for n in names]
    shapes = [SDS(w[n].shape, f32) for n in names]
    res = pl.pallas_call(
        body, name="adam_small", in_specs=[pl.BlockSpec(memory_space=pltpu.SMEM)] + [vm] * (1 + len(ins)),
        out_specs=[vm] * (n_g + 3 * len(names)), out_shape=shapes[:n_g] + shapes * 3,
        compiler_params=_cp(),
    )(dev, gsum, *ins)
    g_o = dict(zip(names[:n_g], res[:n_g]))
    rest = res[n_g:]
    k = len(names)
    return g_o, dict(zip(names, rest[:k])), dict(zip(names, rest[k:2 * k])), dict(zip(names, rest[2 * k:]))


WEIGHTS = ['norm_mix', 'norm_mlp', 'norm_kv', 'norm_final', 's5_a_re', 's5_a_im', 's5_log_dt', 's5_b_re', 's5_b_im',
           's5_c_re', 's5_c_im', 's5_d', 's5_w_glu', 's5_b_glu', 'w_kv', 'b_kv', 'w_q', 'b_q', 'sinks', 'w_o', 'b_o',
           'w_mlp_in', 'w_mlp_out']
BIG = ['s5_w_glu', 'w_kv', 'w_q', 'w_o', 'w_mlp_in', 'w_mlp_out']
BIG_2D = {'s5_w_glu': (D, 256), 'w_kv': (128, 512), 'w_q': (128, D), 'w_o': (128, D), 'w_mlp_in': (2 * D, 512),
          'w_mlp_out': (2 * 512, D)}
SHARDED_SMALL = {'s5_d': D, 's5_b_glu': 2 * D}
SMALL = [n for n in WEIGHTS if n not in BIG]
SMALL_SIZE = {'norm_mix': 2 * D, 'norm_mlp': 2 * D, 'norm_kv': D, 'norm_final': D, 's5_a_re': 4096, 's5_a_im': 4096,
              's5_log_dt': 64, 's5_b_re': 65536, 's5_b_im': 65536, 's5_c_re': 65536, 's5_c_im': 65536, 's5_d': D,
              's5_b_glu': 2 * D, 'b_kv': 512, 'b_q': D, 'sinks': 16, 'b_o': D}


def _pack(vals):
    parts = []
    for n in SMALL:
        v = vals[n].reshape(-1).astype(f32)
        parts.append(jnp.pad(v, (0, (-v.shape[0]) % 128)))
    flat = jnp.concatenate(parts)
    flat = jnp.pad(flat, (0, (-flat.shape[0]) % 1024))
    return flat.reshape(-1, 128)


def _unpack(buf):
    flat = buf.reshape(-1)
    out, off = {}, 0
    for n in SMALL:
        sz = SMALL_SIZE[n]
        out[n] = flat[off:off + sz]
        off += sz + (-sz) % 128
    return out


def kernel(x, norm_mix, norm_mlp, norm_kv, norm_final, s5_a_re, s5_a_im, s5_log_dt, s5_b_re, s5_b_im, s5_c_re, s5_c_im, s5_d, s5_w_glu, s5_b_glu, w_kv, b_kv, w_q, b_q, sinks, w_o, b_o, w_mlp_in, w_mlp_out, loss_target, m_norm_mix, m_norm_mlp, m_norm_kv, m_norm_final, m_s5_a_re, m_s5_a_im, m_s5_log_dt, m_s5_b_re, m_s5_b_im, m_s5_c_re, m_s5_c_im, m_s5_d, m_s5_w_glu, m_s5_b_glu, m_w_kv, m_b_kv, m_w_q, m_b_q, m_sinks, m_w_o, m_b_o, m_w_mlp_in, m_w_mlp_out, v_norm_mix, v_norm_mlp, v_norm_kv, v_norm_final, v_s5_a_re, v_s5_a_im, v_s5_log_dt, v_s5_b_re, v_s5_b_im, v_s5_c_re, v_s5_c_im, v_s5_d, v_s5_w_glu, v_s5_b_glu, v_w_kv, v_b_kv, v_w_q, v_b_q, v_sinks, v_w_o, v_b_o, v_w_mlp_in, v_w_mlp_out):
    w = dict(norm_mix=norm_mix, norm_mlp=norm_mlp, norm_kv=norm_kv, norm_final=norm_final, s5_a_re=s5_a_re,
             s5_a_im=s5_a_im, s5_log_dt=s5_log_dt, s5_b_re=s5_b_re, s5_b_im=s5_b_im, s5_c_re=s5_c_re, s5_c_im=s5_c_im,
             s5_d=s5_d, s5_w_glu=s5_w_glu, s5_b_glu=s5_b_glu, w_kv=w_kv, b_kv=b_kv, w_q=w_q, b_q=b_q, sinks=sinks,
             w_o=w_o, b_o=b_o, w_mlp_in=w_mlp_in, w_mlp_out=w_mlp_out)
    m = dict(norm_mix=m_norm_mix, norm_mlp=m_norm_mlp, norm_kv=m_norm_kv, norm_final=m_norm_final, s5_a_re=m_s5_a_re,
             s5_a_im=m_s5_a_im, s5_log_dt=m_s5_log_dt, s5_b_re=m_s5_b_re, s5_b_im=m_s5_b_im, s5_c_re=m_s5_c_re,
             s5_c_im=m_s5_c_im, s5_d=m_s5_d, s5_w_glu=m_s5_w_glu, s5_b_glu=m_s5_b_glu, w_kv=m_w_kv, b_kv=m_b_kv,
             w_q=m_w_q, b_q=m_b_q, sinks=m_sinks, w_o=m_w_o, b_o=m_b_o, w_mlp_in=m_w_mlp_in, w_mlp_out=m_w_mlp_out)
    v = dict(norm_mix=v_norm_mix, norm_mlp=v_norm_mlp, norm_kv=v_norm_kv, norm_final=v_norm_final, s5_a_re=v_s5_a_re,
             s5_a_im=v_s5_a_im, s5_log_dt=v_s5_log_dt, s5_b_re=v_s5_b_re, s5_b_im=v_s5_b_im, s5_c_re=v_s5_c_re,
             s5_c_im=v_s5_c_im, s5_d=v_s5_d, s5_w_glu=v_s5_w_glu, s5_b_glu=v_s5_b_glu, w_kv=v_w_kv, b_kv=v_b_kv,
             w_q=v_w_q, b_q=v_b_q, sinks=v_sinks, w_o=v_w_o, b_o=v_b_o, w_mlp_in=v_w_mlp_in, w_mlp_out=v_w_mlp_out)
    xi, yi, ci = _pos()
    dev = 4 * xi + 2 * yi + ci
    core = ci.reshape(1).astype(jnp.int32)
    chip = (2 * xi + yi).reshape(1).astype(jnp.int32)

    shards = {
        "s5_w_glu": s5_w_glu[0].astype(bf16), "w_kv": w_kv.astype(bf16), "w_q": w_q[0].astype(bf16),
        "w_o": w_o[0].astype(bf16), "w_in0": w_mlp_in[0].astype(bf16), "w_in1": w_mlp_in[1].astype(bf16),
        "w_out0": w_mlp_out[0].astype(bf16), "w_out1": w_mlp_out[1].astype(bf16),
        "vecs": jnp.broadcast_to(jnp.concatenate([s5_d, s5_b_glu], axis=1), (8, 384)),
    }
    as3d = lambda a, n: a if a.ndim == 3 and a.shape[0] == 2 else a.reshape((1,) + BIG_2D[n])
    opt = {n: (as3d(w[n], n), as3d(m[n], n), as3d(v[n], n)) for n in BIG}
    _, grad_x, grads, big = fwd_bwd(x[0], loss_target[0], {n: w[n] for n in SMALL}, shards, opt, core, chip)

    gsum = allreduce_small(grads)

    out_g, out_d, out_m, out_v = {}, {}, {}, {}
    for n in BIG:
        out_g[n], out_d[n], out_m[n], out_v[n] = [r.reshape(w[n].shape) for r in big[n]]

    loss = gsum[LOSS_ROW, 0]
    swapped = ("s5_b_re", "s5_b_im")
    swap = lambda a: a.transpose(0, 1, 3, 2)

    def kernel_side(d):
        d = {n: (d[n].reshape(1, -1) if d[n].ndim == 1 else d[n]) for n in SMALL}
        d.update({n: swap(d[n]) for n in swapped})
        return d

    s5_g = {}
    for n in S5_PARAMS:
        r0, nr, _ = SMALL_ROWS[n]
        s5_g[n] = gsum[r0:r0 + nr].reshape((1, 64, 16, 64) if n in swapped else w[n].shape)
        out_g[n] = s5_g[n]
    g_s, d_s, m_s, v_s = adam_small(dev.reshape(1).astype(jnp.int32), gsum, s5_g, kernel_side(w), kernel_side(m),
                                    kernel_side(v))
    for src, dst in ((g_s, out_g), (d_s, out_d), (m_s, out_m), (v_s, out_v)):
        dst.update(src)
    for dst in (out_g, out_d, out_m, out_v):
        for n in SMALL:
            dst[n] = (swap(dst[n]) if n in swapped else dst[n]).reshape(w[n].shape)

    return (loss, grad_x[None], *[out_g[n] for n in WEIGHTS], *[out_d[n] for n in WEIGHTS],
            *[out_m[n] for n in WEIGHTS], *[out_v[n] for n in WEIGHTS])
```

```python
import functools
import math

import jax
import jax.numpy as jnp
from jax import lax
from jax.experimental import pallas as pl
from jax.experimental.pallas import tpu as pltpu
from jax.experimental.pallas import tpu_sc as plsc

f32 = jnp.float32
bf16 = jnp.bfloat16
SDS = jax.ShapeDtypeStruct

T = 2048
D = 1024
NDEV = 8
NORM_EPS = 1e-5
S5_G, S5_C, S5_P = 64, 16, 64
S5_SUB = 8
S5_CH = 8
S5_STEPS = T // S5_CH
DT_MIN_LAMBDA = -1e-4
HEAD_DIM = 64
N_KV = 4
Q_PER_KV = 4
BLK = 128
D_FF_SHARD = 512
ADAM_LR, ADAM_B1, ADAM_B2, ADAM_EPS, ADAM_WD, ADAM_STEP = 0.001, 0.9, 0.999, 1e-08, 0.01, 10
VMEM_LIMIT = 56 * 1024 * 1024
MESH = pl.DeviceIdType.MESH


def _cp(**kw):
    return pltpu.CompilerParams(vmem_limit_bytes=VMEM_LIMIT, **kw)


def _dot(a, b):
    return jnp.dot(a, b, preferred_element_type=f32)


def _dot_nt(a, b):
    return lax.dot_general(a, b, (((1,), (1,)), ((), ())), preferred_element_type=f32)


def _dot_tn(a, b):
    return lax.dot_general(a, b, (((0,), (0,)), ((), ())), preferred_element_type=f32)


def _rms(x, g):
    r = lax.rsqrt(jnp.mean(x * x, axis=-1, keepdims=True) + NORM_EPS)
    return x * r * g, r


def _rms_bwd(x, g, dy):
    r = lax.rsqrt(jnp.mean(x * x, axis=-1, keepdims=True) + NORM_EPS)
    u = dy * g
    dx = r * u - (r * r * r) * x * jnp.mean(u * x, axis=-1, keepdims=True)
    return dx, dy * x * r


def _colsum8(v):
    s = jnp.sum(v, axis=0, keepdims=True)
    row = lax.broadcasted_iota(jnp.int32, (8, v.shape[1]), 0)
    return jnp.where(row == 0, jnp.broadcast_to(s, (8, v.shape[1])), 0.0)


def _full(shape):
    nd = len(shape)
    return pl.BlockSpec(shape, lambda *_: (0,) * nd, pipeline_mode=pl.Buffered(1))


_ANY = pl.BlockSpec(memory_space=pl.ANY)


def _pos():
    return lax.axis_index("x"), lax.axis_index("y"), lax.axis_index("c")


def _other_chips(x, y):
    return [(1 - x, y), (x, 1 - y), (1 - x, 1 - y)]


class BgGather:
    SIB, XN, YN, FWD_Y, FWD_X, SIB_X, SIB_Y, SIB_D = range(8)

    def __init__(self, arrs, mids=(0.5, 0.75)):
        n = len(arrs)
        self.arrs = list(arrs)
        self.out_shape = [SDS((NDEV,) + a.shape, a.dtype) for a in arrs]
        self.scratch = [pltpu.SemaphoreType.DMA((n, 8)), pltpu.SemaphoreType.DMA((n, 8)),
                        pltpu.SemaphoreType.DMA((n,))]
        self.mids = mids
        self.result = None

    @staticmethod
    def peers(x, y, c):
        return [(x, y, 1 - c), (1 - x, y, c), (x, 1 - y, c)]

    def mid_steps(self, nsteps):
        at = lambda f: min(nsteps - 1, max(0, int(f * nsteps) - 1))
        return [(at(self.mids[0]), self.mid), (max(at(self.mids[0]), at(self.mids[1])), self.mid2)]

    def _halves(self, a):
        rows = self.arrs[a].shape[0]
        cut = rows // 2 if rows >= 32 else rows
        return (0, cut), (cut, rows - cut)

    def _copy(self, ins, outs, sems, a, k, block, to, own=False, part=None):
        slot = 4 * block[0] + 2 * block[1] + block[2]
        rows = pl.ds(0, self.arrs[a].shape[0]) if part is None else pl.ds(*self._halves(a)[part])
        dst = outs[a].at[slot, rows]
        return pltpu.make_async_remote_copy(
            src_ref=ins[a].at[rows] if own else dst, dst_ref=dst, send_sem=sems[0].at[a, k],
            recv_sem=sems[1].at[a, k], device_id=to, device_id_type=MESH)

    def _mine(self, ins, outs, sems):
        x, y, c = _pos()
        return [pltpu.make_async_copy(ins[a], outs[a].at[4 * x + 2 * y + c], sems[2].at[a])
                for a in range(len(self.arrs))]

    def _split(self, a):
        return self._halves(a)[1][1] > 0

    def _sends(self, ins, outs, sems, phase):
        x, y, c = _pos()
        me, sib, xn, yn, dg = (x, y, c), (x, y, 1 - c), (1 - x, y, c), (x, 1 - y, c), (1 - x, 1 - y, c)
        cps = []
        for a in range(len(self.arrs)):
            cp = lambda k, block, to, **kw: self._copy(ins, outs, sems, a, k, block, to, **kw)
            if phase == 0:
                cps += [cp(self.SIB, me, sib, own=True), cp(self.XN, me, xn, own=True), cp(self.YN, me, yn, own=True)]
            elif phase == 1:
                cps.append(cp(self.FWD_Y, xn, yn, part=0))
                if self._split(a):
                    cps.append(cp(self.FWD_X, yn, xn, part=1))
                cps += [cp(self.SIB_X, xn, sib), cp(self.SIB_Y, yn, sib)]
            else:
                cps.append(cp(self.SIB_D, dg, sib))
        return cps

    def _arrivals(self, ins, outs, sems, phase):
        x, y, c = _pos()
        me, xn, yn, dg = (x, y, c), (1 - x, y, c), (x, 1 - y, c), (1 - x, 1 - y, c)
        cps = []
        for a in range(len(self.arrs)):
            cp = lambda k, block, **kw: self._copy(ins, outs, sems, a, k, block, me, **kw)
            if phase == 1:
                cps += [cp(self.XN, xn), cp(self.YN, yn)]
            elif phase == 2:
                cps.append(cp(self.FWD_Y, dg, part=0))
                if self._split(a):
                    cps.append(cp(self.FWD_X, dg, part=1))
            else:
                cps += [cp(self.SIB, (x, y, 1 - c)), cp(self.SIB_X, (1 - x, y, 1 - c)),
                        cp(self.SIB_Y, (x, 1 - y, 1 - c)), cp(self.SIB_D, (1 - x, 1 - y, 1 - c))]
        return cps

    def start(self, ins, outs, sems):
        for cp in self._mine(ins, outs, sems) + self._sends(ins, outs, sems, 0):
            cp.start()

    def mid(self, ins, outs, sems):
        for cp in self._arrivals(ins, outs, sems, 1):
            cp.wait_recv()
        for cp in self._sends(ins, outs, sems, 1):
            cp.start()

    def mid2(self, ins, outs, sems):
        for cp in self._arrivals(ins, outs, sems, 2):
            cp.wait_recv()
        for cp in self._sends(ins, outs, sems, 2):
            cp.start()

    def finish(self, ins, outs, sems):
        for cp in self._arrivals(ins, outs, sems, 3):
            cp.wait_recv()
        for ph in range(3):
            for cp in self._sends(ins, outs, sems, ph):
                cp.wait_send()
        for cp in self._mine(ins, outs, sems):
            cp.wait()


def sc_comm(g, collective_id, name):
    srcs = [jax.new_ref(a, memory_space=pltpu.MemorySpace.HBM) for a in g.arrs]
    dsts = [jax.empty_ref(s, memory_space=pltpu.MemorySpace.HBM) for s in g.out_shape]

    @pl.kernel(mesh=plsc.ScalarSubcoreMesh(axis_name="sequencer", num_cores=1), name=name,
               scratch_types=tuple(g.scratch), compiler_params=pltpu.CompilerParams(collective_id=collective_id))
    def launch(*sems):
        peers = g.peers(*_pos())
        barrier = pltpu.get_barrier_semaphore()
        for peer in peers:
            pl.semaphore_signal(barrier, inc=1, device_id=peer, device_id_type=MESH)
        pl.semaphore_wait(barrier, len(peers))
        g.start(srcs, dsts, sems)
        for _, phase in g.mid_steps(1):
            phase(srcs, dsts, sems)
        g.finish(srcs, dsts, sems)

    launch()
    return [d[...] for d in dsts]


def sc_gather(arrs, collective_id, name):
    return sc_comm(BgGather(arrs), collective_id, name)


class BgPair:
    def __init__(self, arrs):
        n = len(arrs)
        self.arrs = list(arrs)
        self.out_shape = [SDS((4,) + a.shape[1:], a.dtype) for a in arrs]
        self.scratch = [pltpu.SemaphoreType.DMA((n, 4)), pltpu.SemaphoreType.DMA((n, 4))]
        self.result = None

    @staticmethod
    def peers(x, y, c):
        return [(x, y, 1 - c)]

    def mid_steps(self, nsteps):
        return []

    def _copies(self, ins, outs, sems):
        x, y, c = _pos()
        return [pltpu.make_async_remote_copy(
            src_ref=ins[a].at[2 * k + 1 - c], dst_ref=outs[a].at[k], send_sem=sems[0].at[a, k],
            recv_sem=sems[1].at[a, k], device_id=(x, y, 1 - c), device_id_type=MESH)
            for a in range(len(self.arrs)) for k in range(4)]

    def start(self, ins, outs, sems):
        for cp in self._copies(ins, outs, sems):
            cp.start()

    def finish(self, ins, outs, sems):
        cps = self._copies(ins, outs, sems)
        for cp in cps:
            cp.wait_recv()
        for cp in cps:
            cp.wait_send()


class BgChips(BgPair):
    def __init__(self, arrs):
        n = len(arrs)
        self.arrs = list(arrs)
        self.out_shape = [SDS((3,) + a.shape[1:], a.dtype) for a in arrs]
        self.scratch = [pltpu.SemaphoreType.DMA((n, 3)), pltpu.SemaphoreType.DMA((n, 3))]
        self.result = None

    @staticmethod
    def peers(x, y, c):
        return [(px, py, c) for px, py in _other_chips(x, y)]

    def _copies(self, ins, outs, sems):
        x, y, c = _pos()
        return [pltpu.make_async_remote_copy(
            src_ref=ins[a].at[2 * px + py], dst_ref=outs[a].at[r], send_sem=sems[0].at[a, r],
            recv_sem=sems[1].at[a, r], device_id=(px, py, c), device_id_type=MESH)
            for a in range(len(self.arrs)) for r, (px, py) in enumerate(_other_chips(x, y))]


class AdamRider:
    def __init__(self, w, m, v, part, r2, layer=0, prev=None):
        self.arrs = [w, m, v, part, r2] + list(prev or [])
        self.n_prev = len(prev or [])
        self.layer = layer
        self.out_shape = [SDS(w.shape, f32)] * 4
        self.scratch = []
        self.aliases = {5 + k: k for k in range(self.n_prev)}
        self.result = None

    def _tile(self, grid):
        assert len(grid) == 1
        _, R, C = self.arrs[0].shape
        return R // grid[0], C

    def in_specs(self, grid):
        tr, C = self._tile(grid)
        layer = self.layer
        blk = pl.BlockSpec((None, tr, C), lambda b: (layer, b, 0))
        mine = pl.BlockSpec((None, tr, C), lambda b: (2 * lax.axis_index("x") + lax.axis_index("y"), b, 0))
        return [blk, blk, blk, mine, pl.BlockSpec((3, tr, C), lambda b: (0, b, 0))] + [_ANY] * self.n_prev

    def out_specs(self, grid):
        tr, C = self._tile(grid)
        layer = self.layer
        return [pl.BlockSpec((None, tr, C), lambda b: (layer, b, 0))] * 4

    def mid_steps(self, nsteps):
        return []

    def start(self, ins, outs, sems):
        pass

    finish = start

    def step(self, ins, outs, sems):
        w_ref, m_ref, v_ref, p_ref, r_ref = ins[:5]
        g = p_ref[...].astype(f32) + r_ref[0].astype(f32) + r_ref[1].astype(f32) + r_ref[2].astype(f32)
        d, m_, v_ = _adamw(w_ref[...], g, m_ref[...], v_ref[...])
        for ref, val in zip(outs, (g, d, m_, v_)):
            ref[...] = val


def _call(bgs, body, *, name, grid, in_specs, out_specs, out_shape, scratch_shapes=(), compiler_params=None):
    single = not isinstance(out_shape, (list, tuple))
    out_specs_l = [out_specs] if single else list(out_specs)
    out_shape_l = [out_shape] if single else list(out_shape)
    bgs = [b for b in (bgs or []) if b is not None]
    n_in, n_out, n_sc = len(in_specs), len(out_shape_l), len(scratch_shapes)
    nsteps = math.prod(grid)
    b_in_specs = [b.in_specs(grid) if hasattr(b, "in_specs") else [_ANY] * len(b.arrs) for b in bgs]
    b_out_specs = [b.out_specs(grid) if hasattr(b, "out_specs") else [_ANY] * len(b.out_shape) for b in bgs]
    aliases, i_off, o_off = {}, n_in, n_out
    for b in bgs:
        aliases.update({i_off + i: o_off + o for i, o in getattr(b, "aliases", {}).items()})
        i_off, o_off = i_off + len(b.arrs), o_off + len(b.out_shape)

    def full(*refs):
        pos = [0]

        def take(k):
            r = refs[pos[0]:pos[0] + k]
            pos[0] += k
            return r

        ins = take(n_in)
        b_ins = [take(len(b.arrs)) for b in bgs]
        outs = take(n_out)
        b_outs = [take(len(b.out_shape)) for b in bgs]
        sc = take(n_sc)
        b_sc = [take(len(b.scratch)) for b in bgs]
        if bgs:
            step = pl.program_id(0)
            for d in range(1, len(grid)):
                step = step * grid[d] + pl.program_id(d)

            @pl.when(step == 0)
            def _():
                for b, i_, o_, s_ in zip(bgs, b_ins, b_outs, b_sc):
                    b.start(i_, o_, s_)

        body(*ins, *outs, *sc)
        if bgs:
            for b, i_, o_, s_ in zip(bgs, b_ins, b_outs, b_sc):
                if hasattr(b, "step"):
                    b.step(i_, o_, s_)
                for at, fn in b.mid_steps(nsteps):
                    @pl.when(step == at)
                    def _():
                        fn(i_, o_, s_)

            @pl.when(step == nsteps - 1)
            def _():
                for b, i_, o_, s_ in zip(bgs, b_ins, b_outs, b_sc):
                    b.finish(i_, o_, s_)

    def run(*args):
        res = pl.pallas_call(
            full, name=name, grid=grid,
            in_specs=list(in_specs) + [s for l in b_in_specs for s in l],
            out_specs=out_specs_l + [s for l in b_out_specs for s in l],
            out_shape=out_shape_l + [s for b in bgs for s in b.out_shape],
            scratch_shapes=list(scratch_shapes) + [s for b in bgs for s in b.scratch],
            input_output_aliases=aliases,
            compiler_params=compiler_params,
        )(*args, *[a for b in bgs for a in b.arrs])
        rest = list(res[n_out:])
        for b in bgs:
            b.result, rest = rest[:len(b.out_shape)], rest[len(b.out_shape):]
        return res[0] if single else list(res[:n_out])

    return run


def s5_discretize(a_re, a_im, log_dt, b_re, b_im, c_re, c_im):
    lam_r = jnp.minimum(a_re, DT_MIN_LAMBDA)
    lam_i = a_im
    dt = jnp.exp(log_dt)[:, None]
    e = jnp.exp(lam_r * dt)
    lbr = e * jnp.cos(lam_i * dt)
    lbi = e * jnp.sin(lam_i * dt)
    den = lam_r * lam_r + lam_i * lam_i
    cf_r = ((lbr - 1.0) * lam_r + lbi * lam_i) / den
    cf_i = (lbi * lam_r - (lbr - 1.0) * lam_i) / den
    bb_r = cf_r[:, :, None] * b_re - cf_i[:, :, None] * b_im
    bb_i = cf_r[:, :, None] * b_im + cf_i[:, :, None] * b_re
    eye = jnp.eye(8, dtype=f32)

    def blk_b(m):
        return jnp.einsum('bgpc,gh->bgchp', m.reshape(8, 8, S5_P, S5_C), eye).reshape(8, 128, 512)

    def blk_c(m):
        return jnp.einsum('bgcp,gh->bgphc', m.reshape(8, 8, S5_C, S5_P), eye).reshape(8, 512, 128)

    bm = jnp.concatenate([blk_b(bb_r), blk_b(bb_i)], axis=-1)
    cm = jnp.concatenate([blk_c(c_re), -blk_c(c_im)], axis=1)
    lam = jnp.stack([lbr.reshape(8, 512), lbi.reshape(8, 512)], axis=1)
    lam = jnp.broadcast_to(lam[:, :, None, :], (8, 2, 8, 512))
    return lam, bm, cm


def _cmul(ar, ai, br, bi):
    return ar * br - ai * bi, ar * bi + ai * br


def _shift_rows(v, k, up):
    row = lax.broadcasted_iota(jnp.int32, v.shape, 0)
    if up:
        return jnp.where(row < 8 - k, pltpu.roll(v, 8 - k, 0), 0.0)
    return jnp.where(row >= k, pltpu.roll(v, k, 0), 0.0)


def _chunk_scan(S, lr, li, reverse, aux=None):
    z = jnp.zeros((8, 512), f32)
    U = 4

    def idx(i):
        return (S5_STEPS - 1 - i) if reverse else i

    def rows_of(s):
        return pl.ds(s * 8, 8) if isinstance(s, int) else pl.ds(pl.multiple_of(s * 8, 8), 8)

    def rec(xr, xi, row):
        br = S[row, 0:512]
        bi = S[row, 512:1024]
        return lr * xr - li * xi + br, lr * xi + li * xr + bi

    def step1(i, c):
        for u in range(U):
            c = rec(c[0], c[1], rows_of(idx(i * U + u)))
        return c

    er, ei = lax.fori_loop(0, S5_STEPS // U, step1, (z, z))
    ar, ai = lr, li
    for _ in range(8):
        ar, ai = _cmul(ar, ai, ar, ai)
    cr, ci = _shift_rows(er, 1, reverse), _shift_rows(ei, 1, reverse)
    for k in (1, 2, 4):
        sr, si = _shift_rows(cr, k, reverse), _shift_rows(ci, k, reverse)
        pr, pi_ = _cmul(ar, ai, sr, si)
        cr, ci = cr + pr, ci + pi_
        ar, ai = _cmul(ar, ai, ar, ai)

    if aux is None:
        def step2(i, c):
            for u in range(U):
                row = rows_of(idx(i * U + u))
                c = rec(c[0], c[1], row)
                S[row, 0:512] = c[0]
                S[row, 512:1024] = c[1]
            return c

        lax.fori_loop(0, S5_STEPS // U, step2, (cr, ci))
        return None

    def one(s, c):
        gr0, gi0, dr, di = c
        row = rows_of(s)
        gr, gi = rec(gr0, gi0, row)
        S[row, 0:512] = gr
        S[row, 512:1024] = gi
        prow = rows_of(s - 1)
        xr = aux[prow, 0:512]
        xi = aux[prow, 512:1024]
        return gr, gi, dr + gr * xr + gi * xi, di + gi * xr - gr * xi

    def step2(i, c):
        for u in range(U):
            c = one(S5_STEPS - 1 - (i * U + u), c)
        return c

    c = lax.fori_loop(0, S5_STEPS // U - 1, step2, (cr, ci, z, z))
    for s in range(U - 1, 0, -1):
        c = one(s, c)
    gr, gi, dr, di = c
    row0 = pl.ds(0, 8)
    gr, gi = rec(gr, gi, row0)
    S[row0, 0:512] = gr
    S[row0, 512:1024] = gi
    last = pl.ds((S5_STEPS - 1) * 8, 8)
    xr = _shift_rows(aux[last, 0:512], 1, False)
    xi = _shift_rows(aux[last, 512:1024], 1, False)
    dr = dr + gr * xr + gi * xi
    di = di + gi * xr - gr * xi
    return dr, di


_ROWS = 256


def _row_loop(fn):
    def body(r, c):
        fn(pl.ds(pl.multiple_of(r * _ROWS, _ROWS), _ROWS))
        return c
    lax.fori_loop(0, T // _ROWS, body, 0)


def s5_core_fwd(hn, bm, lam, cm, bg=()):
    def body(u_ref, b_ref, lam_ref, c_ref, ys_ref, S):
        def bu(rows):
            S[rows, :] = _dot(u_ref[rows, :], b_ref[...])
        _row_loop(bu)
        _chunk_scan(S, lam_ref[0], lam_ref[1], False)

        def ys(rows):
            ys_ref[rows, :] = _dot(S[rows, :].astype(bf16), c_ref[...])
        _row_loop(ys)

    return _call(
        bg, body, name="s5_core_fwd", grid=(S5_SUB,),
        in_specs=[pl.BlockSpec((T, 128), lambda b: (0, b)),
                  pl.BlockSpec((None, 128, 1024), lambda b: (b, 0, 0)),
                  pl.BlockSpec((None, 4, 8, 512), lambda b: (b, 0, 0, 0)),
                  pl.BlockSpec((None, 1024, 128), lambda b: (b, 0, 0))],
        out_specs=pl.BlockSpec((T, 128), lambda b: (0, b)),
        out_shape=SDS((T, D), f32),
        scratch_shapes=[pltpu.VMEM((T, 1024), f32)],
        compiler_params=_cp(dimension_semantics=("arbitrary",)),
    )(hn, bm, lam, cm)


_SEG = _ROWS // S5_CH


def _scan_tile(S, lr, li, k, carry, reverse, store, aux=None):
    steps = range(k * _SEG, (k + 1) * _SEG)
    for s in (reversed(steps) if reverse else steps):
        row = pl.ds(s * 8, 8)
        xr, xi = carry[0], carry[1]
        nr = lr * xr - li * xi + S[row, 0:512]
        ni = lr * xi + li * xr + S[row, 512:1024]
        if store:
            S[row, 0:512] = nr
            S[row, 512:1024] = ni
        if aux is not None and s >= 1:
            prow = pl.ds((s - 1) * 8, 8)
            pr, pi_ = aux[prow, 0:512], aux[prow, 512:1024]
            carry = (nr, ni, carry[2] + nr * pr + ni * pi_, carry[3] + ni * pr - nr * pi_)
        elif aux is not None:
            carry = (nr, ni, carry[2], carry[3])
        else:
            carry = (nr, ni)
    return carry


def _chunk_starts(er, ei, lr, li, reverse):
    ar, ai = lr, li
    for _ in range(8):
        ar, ai = _cmul(ar, ai, ar, ai)
    cr, ci = _shift_rows(er, 1, reverse), _shift_rows(ei, 1, reverse)
    for k in (1, 2, 4):
        sr, si = _shift_rows(cr, k, reverse), _shift_rows(ci, k, reverse)
        pr, pi_ = _cmul(ar, ai, sr, si)
        cr, ci = cr + pr, ci + pi_
        ar, ai = _cmul(ar, ai, ar, ai)
    return cr, ci


def s5_core_bwd(hn, dy, bm, lam, cm, bg=()):
    nt = T // _ROWS

    def body(u_ref, dy_ref, b_ref, lam_ref, c_ref, du_ref, db_ref, dct_ref, dlam_ref, S1, S2):
        lr, li, lcr, lci = lam_ref[0], lam_ref[1], lam_ref[2], lam_ref[3]
        z = jnp.zeros((8, 512), f32)
        tile = lambda k: pl.ds(k * _ROWS, _ROWS)
        dyb = lambda k: dy_ref[tile(k), :].astype(bf16)

        c = (z, z)
        for k in range(nt):
            S1[tile(k), :] = _dot(u_ref[tile(k), :], b_ref[...])
            if k >= 1:
                c = _scan_tile(S1, lr, li, k - 1, c, False, False)
        c = _scan_tile(S1, lr, li, nt - 1, c, False, False)

        c = _chunk_starts(c[0], c[1], lr, li, False)
        dct_ref[...] = jnp.zeros_like(dct_ref)
        for k in range(nt):
            c = _scan_tile(S1, lr, li, k, c, False, True)
            if k >= 1:
                dct_ref[...] += _dot_tn(dyb(k - 1), S1[tile(k - 1), :].astype(bf16))
        dct_ref[...] += _dot_tn(dyb(nt - 1), S1[tile(nt - 1), :].astype(bf16))

        S2[tile(nt - 1), :] = _dot_nt(dyb(nt - 1), c_ref[...])
        c = (z, z)
        for k in range(nt - 1, -1, -1):
            if k >= 1:
                S2[tile(k - 1), :] = _dot_nt(dyb(k - 1), c_ref[...])
            c = _scan_tile(S2, lcr, lci, k, c, True, False)

        def dbu(k):
            gb = S2[tile(k), :].astype(bf16)
            db_ref[...] += _dot_tn(u_ref[tile(k), :], gb)
            du_ref[tile(k), :] = _dot_nt(gb, b_ref[...])

        c = _chunk_starts(c[0], c[1], lcr, lci, True) + (z, z)
        db_ref[...] = jnp.zeros_like(db_ref)
        for k in range(nt - 1, -1, -1):
            c = _scan_tile(S2, lcr, lci, k, c, True, True, aux=S1)
            if k + 1 < nt:
                dbu(k + 1)
        dbu(0)
        gr, gi, dr, di = c
        last = pl.ds((S5_STEPS - 1) * 8, 8)
        xr = _shift_rows(S1[last, 0:512], 1, False)
        xi = _shift_rows(S1[last, 512:1024], 1, False)
        dlam_ref[0] = dr + gr * xr + gi * xi
        dlam_ref[1] = di + gi * xr - gr * xi

    return _call(
        bg, body, name="s5_core_bwd", grid=(S5_SUB,),
        in_specs=[pl.BlockSpec((T, 128), lambda b: (0, b)),
                  pl.BlockSpec((T, 128), lambda b: (0, b)),
                  pl.BlockSpec((None, 128, 1024), lambda b: (b, 0, 0)),
                  pl.BlockSpec((None, 4, 8, 512), lambda b: (b, 0, 0, 0)),
                  pl.BlockSpec((None, 1024, 128), lambda b: (b, 0, 0))],
        out_specs=[pl.BlockSpec((T, 128), lambda b: (0, b)),
                   pl.BlockSpec((None, 128, 1024), lambda b: (b, 0, 0)),
                   pl.BlockSpec((None, 128, 1024), lambda b: (b, 0, 0)),
                   pl.BlockSpec((None, 2, 8, 512), lambda b: (b, 0, 0, 0))],
        out_shape=[SDS((T, D), f32), SDS((8, 128, 1024), f32), SDS((8, 128, 1024), f32), SDS((8, 2, 8, 512), f32)],
        scratch_shapes=[pltpu.VMEM((T, 1024), f32), pltpu.VMEM((T, 1024), f32)],
        compiler_params=_cp(dimension_semantics=("arbitrary",)),
    )(hn, dy, bm, lam, cm)


TM = 512
NT = T // TM


def _tile(n=D):
    return pl.BlockSpec((TM, n), lambda i: (i, 0))


def s5_pre(xp, g):
    def body(x_ref, g_ref, hn_ref):
        hn, _ = _rms(x_ref[...], g_ref[...])
        hn_ref[...] = hn.astype(bf16)

    return pl.pallas_call(
        body, name="s5_pre", grid=(NT,), in_specs=[_tile(), _full((1, D))], out_specs=_tile(),
        out_shape=SDS((T, D), bf16), compiler_params=_cp(dimension_semantics=("arbitrary",)),
    )(xp, g)


def _gelu_grad(y):
    c = math.sqrt(2.0 / math.pi)
    t = jnp.tanh(c * (y + 0.044715 * y * y * y))
    return 0.5 * (1.0 + t) + 0.5 * y * (1.0 - t * t) * c * (1.0 + 3.0 * 0.044715 * y * y)


def s5_post(ys, xp, g, d, wglu, bglu, bg=()):
    def body(ys_ref, x_ref, g_ref, d_ref, w_ref, b_ref, y_ref, z_ref, h_ref):
        x = x_ref[...]
        hn, _ = _rms(x, g_ref[...])
        y = ys_ref[...] + d_ref[...] * hn
        y_ref[...] = y
        yg = jax.nn.gelu(y).astype(bf16)
        for j in range(4):
            cv = slice(j * 256, (j + 1) * 256)
            cg = slice(1024 + j * 256, 1024 + (j + 1) * 256)
            val = _dot(yg, w_ref[j]) + b_ref[:, cv]
            gate = _dot(yg, w_ref[j + 4]) + b_ref[:, cg]
            z_ref[:, cv] = val
            z_ref[:, cg] = gate
            h_ref[:, cv] = x[:, cv] + val * jax.nn.sigmoid(gate)

    return _call(
        bg, body, name="s5_post", grid=(NT,),
        in_specs=[_tile(), _tile(), _full((1, D)), _full((1, D)), _full((8, D, 256)), _full((1, 2 * D))],
        out_specs=[_tile(), _tile(2 * D), _tile()],
        out_shape=[SDS((T, D), f32), SDS((T, 2 * D), f32), SDS((T, D), f32)],
        compiler_params=_cp(dimension_semantics=("arbitrary",)),
    )(ys, xp, g, d, wglu, bglu)


def s5_post_bwd(dh, y, z, wglu, bg=()):
    def body(dh_ref, y_ref, z_ref, w_ref, dy_ref, dw_ref, db_ref, acc):
        i = pl.program_id(0)

        @pl.when(i == 0)
        def _():
            acc[...] = jnp.zeros_like(acc)
            db_ref[...] = jnp.zeros_like(db_ref)

        dh_ = dh_ref[...]
        y = y_ref[...]
        yg = jax.nn.gelu(y).astype(bf16)
        dyg = jnp.zeros((TM, D), f32)
        for j in range(4):
            cv = slice(j * 256, (j + 1) * 256)
            cg = slice(1024 + j * 256, 1024 + (j + 1) * 256)
            val = z_ref[:, cv]
            sg = jax.nn.sigmoid(z_ref[:, cg])
            dval = dh_[:, cv] * sg
            dgate = dh_[:, cv] * val * sg * (1.0 - sg)
            db_ref[:, cv] += _colsum8(dval)
            db_ref[:, cg] += _colsum8(dgate)
            dvb = dval.astype(bf16)
            dgb = dgate.astype(bf16)
            acc[j] += _dot_tn(yg, dvb)
            acc[j + 4] += _dot_tn(yg, dgb)
            dyg = dyg + _dot_nt(dvb, w_ref[j]) + _dot_nt(dgb, w_ref[j + 4])
        dy_ref[...] = dyg * _gelu_grad(y)

        @pl.when(i == NT - 1)
        def _():
            dw_ref[...] = acc[...].astype(bf16)

    return _call(
        bg, body, name="s5_post_bwd", grid=(NT,),
        in_specs=[_tile(), _tile(), _tile(2 * D), _full((8, D, 256))],
        out_specs=[_tile(), _full((8, D, 256)), _full((8, 2 * D))],
        out_shape=[SDS((T, D), f32), SDS((8, D, 256), bf16), SDS((8, 2 * D), f32)],
        scratch_shapes=[pltpu.VMEM((8, D, 256), f32)],
        compiler_params=_cp(dimension_semantics=("arbitrary",)),
    )(dh, y, z, wglu)


def s5_pre_bwd(xp, g, du, dy, d, dh, bg=()):
    def body(x_ref, g_ref, du_ref, dy_ref, d_ref, dh_ref, dx_ref, dg_ref, dd_ref):
        i = pl.program_id(0)

        @pl.when(i == 0)
        def _():
            dg_ref[...] = jnp.zeros_like(dg_ref)
            dd_ref[...] = jnp.zeros_like(dd_ref)

        x = x_ref[...]
        g = g_ref[...]
        dy = dy_ref[...]
        hn, _ = _rms(x, g)
        dhn = du_ref[...] + d_ref[...] * dy
        dx, dgt = _rms_bwd(x, g, dhn)
        dx_ref[...] = dh_ref[...] + dx
        dg_ref[...] += _colsum8(dgt)
        dd_ref[...] += _colsum8(dy * hn)

    return _call(
        bg, body, name="s5_pre_bwd", grid=(NT,),
        in_specs=[_tile(), _full((1, D)), _tile(), _tile(), _full((1, D)), _tile()],
        out_specs=[_tile(), _full((8, D)), _full((8, D))],
        out_shape=[SDS((T, D), f32), SDS((8, D), f32), SDS((8, D), f32)],
        compiler_params=_cp(dimension_semantics=("arbitrary",)),
    )(xp, g, du, dy, d, dh)


TMF = 1024


def mlp_fwd(h, g, w_in, w_out, layer, bg=()):
    def body(h_ref, g_ref, wi_ref, wo_ref, hm_ref, r_ref, out_ref, acc):
        j = pl.program_id(1)

        @pl.when(j == 0)
        def _():
            hm, _ = _rms(h_ref[...], g_ref[...])
            hm_ref[...] = hm.astype(bf16)
            acc[...] = jnp.zeros_like(acc)

        a = jnp.maximum(_dot(hm_ref[...], wi_ref[...]), 0.0)
        r_ref[...] = a.astype(bf16)
        acc[...] += _dot((a * a).astype(bf16), wo_ref[...])

        @pl.when(j == NDEV - 1)
        def _():
            out_ref[...] = h_ref[...] + acc[...]

    return _call(
        bg, body, name=f"mlp_fwd{layer}", grid=(T // TMF, NDEV),
        in_specs=[pl.BlockSpec((TMF, D), lambda i, j: (i, 0)),
                  pl.BlockSpec((1, D), lambda i, j: (0, 0)),
                  pl.BlockSpec((None, D, D_FF_SHARD), lambda i, j: (j, 0, 0)),
                  pl.BlockSpec((None, D_FF_SHARD, D), lambda i, j: (j, 0, 0))],
        out_specs=[pl.BlockSpec((TMF, D), lambda i, j: (i, 0)), pl.BlockSpec((TMF, D_FF_SHARD), lambda i, j: (i, j)),
                   pl.BlockSpec((TMF, D), lambda i, j: (i, 0))],
        out_shape=[SDS((T, D), bf16), SDS((T, NDEV * D_FF_SHARD), bf16), SDS((T, D), f32)],
        scratch_shapes=[pltpu.VMEM((TMF, D), f32)],
        compiler_params=_cp(dimension_semantics=("arbitrary", "arbitrary")),
    )(h, g, w_in, w_out)


def mlp_bwd(h, hm, r, g, dout, dout_b, w_in, w_out, layer, bg=()):
    last = NDEV - 1

    def body(h_ref, hm_ref, r_ref, g_ref, do_ref, dob_ref, wi_ref, wo_ref, dh_ref, dwi_ref, dwo_ref, dg_ref,
             dhm, awi, awo):
        j = pl.program_id(0)
        i = pl.program_id(1)
        rows = pl.ds(pl.multiple_of(i * TM, TM), TM)

        @pl.when(i == 0)
        def _():
            awi[...] = jnp.zeros_like(awi)
            awo[...] = jnp.zeros_like(awo)

        dz = (_dot_nt(dob_ref[...], wo_ref[...]) * (2.0 * r_ref[...].astype(f32))).astype(bf16)
        rb = r_ref[...]
        awo[...] += _dot_tn(rb * rb, dob_ref[...])
        awi[...] += _dot_tn(hm_ref[...], dz)
        part = _dot_nt(dz, wi_ref[...])

        @pl.when(j == 0)
        def _():
            dhm[rows, :] = part

        @pl.when(j > 0)
        def _():
            dhm[rows, :] += part

        @pl.when(i == NT - 1)
        def _():
            dwi_ref[...] = awi[...].astype(bf16)
            dwo_ref[...] = awo[...].astype(bf16)

        @pl.when(j == last)
        def _():
            @pl.when(i == 0)
            def _():
                dg_ref[...] = jnp.zeros_like(dg_ref)
            dx, dgt = _rms_bwd(h_ref[...], g_ref[...], dhm[rows, :])
            dh_ref[...] = do_ref[...] + dx
            dg_ref[...] += _colsum8(dgt)

    late = lambda j, i: (jnp.where(j == last, i, 0), 0)
    return _call(
        bg, body, name=f"mlp_bwd{layer}", grid=(NDEV, NT),
        in_specs=[pl.BlockSpec((TM, D), late),
                  pl.BlockSpec((TM, D), lambda j, i: (i, 0)),
                  pl.BlockSpec((TM, D_FF_SHARD), lambda j, i: (i, j)),
                  pl.BlockSpec((1, D), lambda j, i: (0, 0)),
                  pl.BlockSpec((TM, D), late),
                  pl.BlockSpec((TM, D), lambda j, i: (i, 0)),
                  pl.BlockSpec((None, D, D_FF_SHARD), lambda j, i: (j, 0, 0)),
                  pl.BlockSpec((None, D_FF_SHARD, D), lambda j, i: (j, 0, 0))],
        out_specs=[pl.BlockSpec((TM, D), late),
                   pl.BlockSpec((None, D, D_FF_SHARD), lambda j, i: (j, 0, 0)),
                   pl.BlockSpec((None, D_FF_SHARD, D), lambda j, i: (j, 0, 0)),
                   pl.BlockSpec((8, D), lambda j, i: (0, 0))],
        out_shape=[SDS((T, D), f32), SDS((NDEV, D, D_FF_SHARD), bf16), SDS((NDEV, D_FF_SHARD, D), bf16),
                   SDS((8, D), f32)],
        scratch_shapes=[pltpu.VMEM((T, D), f32), pltpu.VMEM((D, D_FF_SHARD), f32), pltpu.VMEM((D_FF_SHARD, D), f32)],
        compiler_params=_cp(dimension_semantics=("arbitrary", "arbitrary")),
    )(h, hm, r, g, dout, dout_b, w_in, w_out)


def _spread4():
    r = lax.broadcasted_iota(jnp.int32, (256, D), 0)
    c = lax.broadcasted_iota(jnp.int32, (256, D), 1)
    return ((c // 256 == r // HEAD_DIM) & (c % HEAD_DIM == r % HEAD_DIM)).astype(bf16)


def attn_pre(h, g_kv, g_mix, wkv, bkv, spread, wq, bq):
    def body(h_ref, gkv_ref, gm_ref, wkv_ref, bkv_ref, sp_ref, wq_ref, bq_ref, kvn_ref, hn_ref, k_ref, v_ref, q_ref):
        h_ = h_ref[...]
        kvn = _rms(h_, gkv_ref[...])[0].astype(bf16)
        hn = _rms(h_, gm_ref[...])[0].astype(bf16)
        kvn_ref[...] = kvn
        hn_ref[...] = hn
        kv = (_dot(kvn, wkv_ref[...]) + bkv_ref[...]).astype(bf16)
        k_ref[...] = _dot(kv[:, :256], sp_ref[...]).astype(bf16)
        v_ref[...] = _dot(kv[:, 256:], sp_ref[...]).astype(bf16)
        q_ref[...] = (_dot(hn, wq_ref[...]) + bq_ref[...]).astype(bf16)

    return pl.pallas_call(
        body, name="attn_pre", grid=(NT,),
        in_specs=[_tile(), _full((1, D)), _full((1, D)), _full((D, 512)), _full((1, 512)), _full((256, D)),
                  _full((D, D)), _full((1, D))],
        out_specs=[_tile()] * 5,
        out_shape=[SDS((T, D), bf16)] * 5,
        compiler_params=_cp(dimension_semantics=("arbitrary",)),
    )(h, g_kv, g_mix, wkv, bkv, spread, wq, bq)


def _attn_specs():
    cur = pl.BlockSpec((TM, 256), lambda j, n: (n, j))
    prev = pl.BlockSpec((BLK, 256), lambda j, n: (jnp.maximum(n * (TM // BLK) - 1, 0), j))
    return cur, prev


def _head_mask(g):
    lane = lax.broadcasted_iota(jnp.int32, (1, 256), 1)
    return (lane >= g * HEAD_DIM) & (lane < (g + 1) * HEAD_DIM)


def _stack_heads(t):
    return jnp.concatenate([jnp.where(_head_mask(g), t, 0) for g in range(Q_PER_KV)], axis=0)


def _unstack_heads(t):
    out = jnp.where(_head_mask(0), t[0:BLK], 0.0)
    for g in range(1, Q_PER_KV):
        out = out + jnp.where(_head_mask(g), t[g * BLK:(g + 1) * BLK], 0.0)
    return out


def _attn_probs(qs, k2, sinks, first):
    rows = Q_PER_KV * BLK
    s = _dot_nt(qs, k2) * (1.0 / math.sqrt(HEAD_DIM))
    qi = jnp.bitwise_and(lax.broadcasted_iota(jnp.int32, (rows, 2 * BLK), 0), BLK - 1)
    kj = lax.broadcasted_iota(jnp.int32, (rows, 2 * BLK), 1)
    diff = qi + BLK - kj
    valid = (diff >= 0) & (diff < BLK) & (jnp.logical_not(first) | (kj >= BLK))
    s = jnp.where(valid, s, -jnp.inf)
    rb = lax.broadcasted_iota(jnp.int32, (rows, 1), 0)
    sink = jnp.where(rb < BLK, sinks[0], jnp.where(rb < 2 * BLK, sinks[1], jnp.where(rb < 3 * BLK, sinks[2], sinks[3])))
    m = jnp.maximum(jnp.max(s, axis=-1, keepdims=True), sink)
    p = jnp.exp(s - m)
    ps = jnp.exp(sink - m)
    denom = jnp.sum(p, axis=-1, keepdims=True) + ps
    return p / denom, ps / denom


def _window_blocks(b, n, kc_ref, kp_ref, vc_ref, vp_ref):
    if b == 0:
        return (jnp.concatenate([kp_ref[...], kc_ref[0:BLK, :]], axis=0),
                jnp.concatenate([vp_ref[...], vc_ref[0:BLK, :]], axis=0), n == 0)
    rows = pl.ds((b - 1) * BLK, 2 * BLK)
    return kc_ref[rows, :], vc_ref[rows, :], False


def attn_core_fwd(q, k4, v4, sinks, bg=()):
    nb = TM // BLK

    def body(sink_ref, q_ref, kc_ref, kp_ref, vc_ref, vp_ref, o_ref):
        j = pl.program_id(0)
        n = pl.program_id(1)
        sk = [sink_ref[j * Q_PER_KV + g] for g in range(Q_PER_KV)]
        for b in range(nb):
            qb = q_ref[b * BLK:(b + 1) * BLK, :]
            k2, v2, first = _window_blocks(b, n, kc_ref, kp_ref, vc_ref, vp_ref)
            a, _ = _attn_probs(_stack_heads(qb), k2, sk, first)
            o_ref[b * BLK:(b + 1) * BLK, :] = _unstack_heads(_dot(a.astype(bf16), v2)).astype(bf16)

    cur, prev = _attn_specs()
    return _call(
        bg, body, name="attn_core_fwd", grid=(N_KV, NT),
        in_specs=[pl.BlockSpec(memory_space=pltpu.SMEM), cur, cur, prev, cur, prev],
        out_specs=cur, out_shape=SDS((T, D), bf16),
        compiler_params=_cp(dimension_semantics=("arbitrary", "arbitrary")),
    )(sinks, q, k4, k4, v4, v4)


def attn_post(h, o, wo, bo):
    def body(h_ref, o_ref, w_ref, b_ref, out_ref):
        out_ref[...] = h_ref[...] + _dot(o_ref[...], w_ref[...]) + b_ref[...]

    return pl.pallas_call(
        body, name="attn_post", grid=(NT,), in_specs=[_tile(), _tile(), _full((D, D)), _full((1, D))],
        out_specs=_tile(), out_shape=SDS((T, D), f32), compiler_params=_cp(dimension_semantics=("arbitrary",)),
    )(h, o, wo, bo)


def attn_bwd_pre(dh, o, wo, bg=()):
    def body(dh_ref, o_ref, w_ref, do_ref, dw_ref, db_ref, acc):
        i = pl.program_id(0)

        @pl.when(i == 0)
        def _():
            acc[...] = jnp.zeros_like(acc)
            db_ref[...] = jnp.zeros_like(db_ref)

        dh_ = dh_ref[...]
        dhb = dh_.astype(bf16)
        do_ref[...] = _dot_nt(dhb, w_ref[...]).astype(bf16)
        acc[...] += _dot_tn(o_ref[...], dhb)
        db_ref[...] += _colsum8(dh_)

        @pl.when(i == NT - 1)
        def _():
            dw_ref[...] = acc[...].astype(bf16)

    return _call(
        bg, body, name="attn_bwd_pre", grid=(NT,), in_specs=[_tile(), _tile(), _full((D, D))],
        out_specs=[_tile(), _full((D, D)), _full((8, D))],
        out_shape=[SDS((T, D), bf16), SDS((D, D), bf16), SDS((8, D), f32)],
        scratch_shapes=[pltpu.VMEM((D, D), f32)],
        compiler_params=_cp(dimension_semantics=("arbitrary",)),
    )(dh, o, wo)


def attn_core_bwd(q, do, k4, v4, sinks, bg=()):
    nb = TM // BLK

    def body(sink_ref, q_ref, do_ref, kc_ref, kp_ref, vc_ref, vp_ref, dq_ref, dk_ref, dv_ref, ds_ref):
        j = pl.program_id(0)
        n = pl.program_id(1)

        @pl.when(n == 0)
        def _():
            dk_ref[...] = jnp.zeros_like(dk_ref)
            dv_ref[...] = jnp.zeros_like(dv_ref)
            ds_ref[...] = jnp.zeros_like(ds_ref)

        lane8 = lax.broadcasted_iota(jnp.int32, (8, 128), 1)
        row8 = lax.broadcasted_iota(jnp.int32, (8, 128), 0)
        sk = [sink_ref[j * Q_PER_KV + g] for g in range(Q_PER_KV)]
        for b in range(nb):
            qs = _stack_heads(q_ref[b * BLK:(b + 1) * BLK, :])
            dos = _stack_heads(do_ref[b * BLK:(b + 1) * BLK, :])
            k2, v2, first = _window_blocks(b, n, kc_ref, kp_ref, vc_ref, vp_ref)
            a, asink = _attn_probs(qs, k2, sk, first)
            dp = _dot_nt(dos, v2)
            dd = jnp.sum(a * dp, axis=-1, keepdims=True)
            dsc = (a * (dp - dd) * (1.0 / math.sqrt(HEAD_DIM))).astype(bf16)
            t = asink * dd
            for g in range(Q_PER_KV):
                dsink = -jnp.sum(t[g * BLK:(g + 1) * BLK], axis=0, keepdims=True)
                ds_ref[...] += jnp.where((lane8 == g) & (row8 == 0), jnp.broadcast_to(dsink, (8, 128)), 0.0)
            dq_ref[b * BLK:(b + 1) * BLK, :] = _unstack_heads(_dot(dsc, k2))
            dk2 = _dot_tn(dsc, qs)
            dv2 = _dot_tn(a.astype(bf16), dos)
            cur = pl.ds(pl.multiple_of(n * TM + b * BLK, BLK), BLK)
            dk_ref[cur, :] += dk2[BLK:, :]
            dv_ref[cur, :] += dv2[BLK:, :]
            if b == 0:
                @pl.when(n > 0)
                def _():
                    prv = pl.ds(pl.multiple_of(n * TM - BLK, BLK), BLK)
                    dk_ref[prv, :] += dk2[:BLK, :]
                    dv_ref[prv, :] += dv2[:BLK, :]
            else:
                prv = pl.ds(pl.multiple_of(n * TM + (b - 1) * BLK, BLK), BLK)
                dk_ref[prv, :] += dk2[:BLK, :]
                dv_ref[prv, :] += dv2[:BLK, :]

    cur, prev = _attn_specs()
    col = pl.BlockSpec((T, 256), lambda j, n: (0, j))
    return _call(
        bg, body, name="attn_core_bwd", grid=(N_KV, NT),
        in_specs=[pl.BlockSpec(memory_space=pltpu.SMEM), cur, cur, cur, prev, cur, prev],
        out_specs=[cur, col, col, pl.BlockSpec((None, 8, 128), lambda j, n: (j, 0, 0))],
        out_shape=[SDS((T, D), f32), SDS((T, D), f32), SDS((T, D), f32), SDS((N_KV, 8, 128), f32)],
        compiler_params=_cp(dimension_semantics=("arbitrary", "arbitrary")),
    )(sinks, q, do, k4, k4, v4, v4)


def attn_bwd_q(h, dh, dq, hn, g_mix, wq):
    def body(h_ref, dh_ref, dq_ref, hn_ref, gm_ref, wq_ref, out_ref, dwq_ref, dbq_ref, dgm_ref, aq):
        i = pl.program_id(0)

        @pl.when(i == 0)
        def _():
            aq[...] = jnp.zeros_like(aq)
            dbq_ref[...] = jnp.zeros_like(dbq_ref)
            dgm_ref[...] = jnp.zeros_like(dgm_ref)

        dq_ = dq_ref[...]
        dqb = dq_.astype(bf16)
        aq[...] += _dot_tn(hn_ref[...], dqb)
        dbq_ref[...] += _colsum8(dq_)
        dx, dg = _rms_bwd(h_ref[...], gm_ref[...], _dot_nt(dqb, wq_ref[...]))
        out_ref[...] = dh_ref[...] + dx
        dgm_ref[...] += _colsum8(dg)

        @pl.when(i == NT - 1)
        def _():
            dwq_ref[...] = aq[...].astype(bf16)

    vec = _full((8, D))
    mat = _full((D, D))
    return pl.pallas_call(
        body, name="attn_bwd_q", grid=(NT,),
        in_specs=[_tile()] * 4 + [_full((1, D)), mat],
        out_specs=[_tile(), mat, vec, vec],
        out_shape=[SDS((T, D), f32), SDS((D, D), bf16), SDS((8, D), f32), SDS((8, D), f32)],
        scratch_shapes=[pltpu.VMEM((D, D), f32)],
        compiler_params=_cp(dimension_semantics=("arbitrary",)),
    )(h, dh, dq, hn, g_mix, wq)


def attn_bwd_kv(h, dh, dk4, dv4, kvn, g_kv, wkv, spread):
    def body(h_ref, dh_ref, dk_ref, dv_ref, kvn_ref, gkv_ref, wkv_ref, sp_ref, out_ref, outb_ref, dw_ref, db_ref,
             dgkv_ref, acc):
        i = pl.program_id(0)

        @pl.when(i == 0)
        def _():
            for r in (acc, db_ref, dgkv_ref):
                r[...] = jnp.zeros_like(r)

        dkv = jnp.concatenate([_dot_nt(dk_ref[...].astype(bf16), sp_ref[...]),
                               _dot_nt(dv_ref[...].astype(bf16), sp_ref[...])], axis=1)
        dkvb = dkv.astype(bf16)
        acc[...] += _dot_tn(kvn_ref[...], dkvb)
        db_ref[...] += _colsum8(dkv)
        dx, dg = _rms_bwd(h_ref[...], gkv_ref[...], _dot_nt(dkvb, wkv_ref[...]))
        out = dh_ref[...] + dx
        out_ref[...] = out
        outb_ref[...] = out.astype(bf16)
        dgkv_ref[...] += _colsum8(dg)

        @pl.when(i == NT - 1)
        def _():
            dw_ref[...] = acc[...].astype(bf16)

    return pl.pallas_call(
        body, name="attn_bwd_kv", grid=(NT,),
        in_specs=[_tile()] * 5 + [_full((1, D)), _full((D, 512)), _full((256, D))],
        out_specs=[_tile(), _tile(), _full((D, 512)), _full((8, 512)), _full((8, D))],
        out_shape=[SDS((T, D), f32), SDS((T, D), bf16), SDS((D, 512), bf16), SDS((8, 512), f32), SDS((8, D), f32)],
        scratch_shapes=[pltpu.VMEM((D, 512), f32)],
        compiler_params=_cp(dimension_semantics=("arbitrary",)),
    )(h, dh, dk4, dv4, kvn, g_kv, wkv, spread)


def final_loss(h, g, target):
    def body(h_ref, g_ref, t_ref, loss_ref, dh_ref, dhb_ref, dg_ref):
        i = pl.program_id(0)

        @pl.when(i == 0)
        def _():
            loss_ref[...] = jnp.zeros_like(loss_ref)
            dg_ref[...] = jnp.zeros_like(dg_ref)

        h_ = h_ref[...]
        g_ = g_ref[...]
        y, _ = _rms(h_, g_)
        diff = y - t_ref[...]
        per_tok = jnp.mean(diff * diff, axis=-1, keepdims=True)
        tot = 0.5 * jnp.sum(per_tok, axis=0, keepdims=True)
        lane = lax.broadcasted_iota(jnp.int32, (8, 128), 1)
        row = lax.broadcasted_iota(jnp.int32, (8, 128), 0)
        loss_ref[...] += jnp.where((lane == 0) & (row == 0), jnp.broadcast_to(tot, (8, 128)), 0.0)
        dx, dgt = _rms_bwd(h_, g_, diff * (1.0 / D))
        dh_ref[...] = dx
        dhb_ref[...] = dx.astype(bf16)
        dg_ref[...] += _colsum8(dgt)

    return pl.pallas_call(
        body, name="final_loss", grid=(NT,), in_specs=[_tile(), _full((1, D)), _tile()],
        out_specs=[_full((8, 128)), _tile(), _tile(), _full((8, D))],
        out_shape=[SDS((8, 128), f32), SDS((T, D), f32), SDS((T, D), bf16), SDS((8, D), f32)],
        compiler_params=_cp(dimension_semantics=("arbitrary",)),
    )(h, g, target)


def _to_chunked(a):
    return a.reshape(S5_CH, S5_STEPS, a.shape[-1]).transpose(1, 0, 2).reshape(T, a.shape[-1])


def _from_chunked(a):
    return a.reshape(S5_STEPS, S5_CH, a.shape[-1]).transpose(1, 0, 2).reshape(T, a.shape[-1])


def _rep4(w):
    return jnp.broadcast_to(w.reshape(w.shape[0], N_KV, 1, HEAD_DIM), (w.shape[0], N_KV, Q_PER_KV, HEAD_DIM)).reshape(
        w.shape[0], N_KV * Q_PER_KV * HEAD_DIM)


def _fold4(w):
    return w.reshape(w.shape[0], N_KV, Q_PER_KV, HEAD_DIM).sum(axis=2).reshape(w.shape[0], N_KV * HEAD_DIM)


def fwd_bwd(x, target, p, shards, opt, core, chip):
    row = lambda v: v.reshape(1, -1)
    (lam, bm, cm), prep_vjp = jax.vjp(s5_discretize, p["s5_a_re"][0], p["s5_a_im"][0], p["s5_log_dt"][0],
                                      p["s5_b_re"][0], p["s5_b_im"][0], p["s5_c_re"][0], p["s5_c_im"][0])
    bmb, cmb = bm.astype(bf16), cm.astype(bf16)
    lam = jnp.concatenate([lam, lam * jnp.array([1.0, -1.0], f32).reshape(1, 2, 1, 1)], axis=1)
    g_mix0, g_mix1 = row(p["norm_mix"][0]), row(p["norm_mix"][1])
    g_mlp0, g_mlp1 = row(p["norm_mlp"][0]), row(p["norm_mlp"][1])
    g_kv, g_fin = row(p["norm_kv"]), row(p["norm_final"])
    bq, bo = p["b_q"], p["b_o"]
    bkv = row(p["b_kv"])
    spread = _spread4()
    sinks = p["sinks"].reshape(16)

    def reduce_pairs(names, bg):
        return [add_pairs(g, r, core, f"add_pairs_{n}") for n, g, r in zip(names, bg.arrs, bg.result)]

    wglu, gvec, win0, wout0 = sc_gather(
        [shards["s5_w_glu"], shards["vecs"], shards["w_in0"], shards["w_out0"]], 3, "sc_gather_layer0")
    wkv, wq, wo, win1 = sc_gather([shards["w_kv"], shards["w_q"], shards["w_o"], shards["w_in1"]], 4, "sc_gather_attn")
    wout1, = sc_gather([shards["w_out1"]], 5, "sc_gather_w_out1")
    xp = _to_chunked(x)
    hn0 = s5_pre(xp, g_mix0)
    ys = s5_core_fwd(hn0, bmb, lam, cmb)
    d_skip = gvec[:, 0, :128].reshape(1, D)
    bglu = gvec[:, 0, 128:].reshape(1, 2 * D)
    y, z, h1 = s5_post(ys, xp, g_mix0, d_skip, wglu, bglu)
    hm0, r0, h2p = mlp_fwd(h1, g_mlp0, win0, wout0, 0)
    wkv, wq, wo = wkv.reshape(D, 512), wq.reshape(D, D), wo.reshape(D, D)
    h2 = _from_chunked(h2p)
    kvn, hn1, k4, v4, q = attn_pre(h2, g_kv, g_mix1, wkv, bkv, spread, wq, bq)
    o = attn_core_fwd(q, k4, v4, sinks)
    h3 = attn_post(h2, o, wo, bo)
    hm1, r1, h4 = mlp_fwd(h3, g_mlp1, win1, wout1, 1)
    loss, dh4, dh4b, dg_fin = final_loss(h4, g_fin, target)

    def reduce_scatter(names, grads, ids):
        r1 = sc_comm(BgPair(grads), ids[0], "sc_pair_" + names[0])
        parts = [add_pairs(g, r, core, f"add_pairs_{n}") for n, g, r in zip(names, grads, r1)]
        return list(zip(parts, sc_comm(BgChips(parts), ids[1], "sc_chips_" + names[0])))

    dh3, dwin1, dwout1, dg_mlp1 = mlp_bwd(h3, hm1, r1, g_mlp1, dh4, dh4b, win1, wout1, 1)
    rs_in1, rs_out1 = reduce_scatter(["w_in1", "w_out1"], [dwin1, dwout1], (6, 7))
    do, dwo, dbo = attn_bwd_pre(dh3, o, wo)
    dq, dk4, dv4, dsink = attn_core_bwd(q, do, k4, v4, sinks)
    dh2, dwq, dbq, dg_mix1 = attn_bwd_q(h2, dh3, dq, hn1, g_mix1, wq)
    dh2, dh2b, dwkv, dbkv, dg_kv = attn_bwd_kv(h2, dh2, dk4, dv4, kvn, g_kv, wkv, spread)
    rs_attn = reduce_scatter(["w_kv", "w_q", "w_o"], [dwkv.reshape(NDEV, 128, 512), dwq.reshape(NDEV, 128, D),
                                                      dwo.reshape(NDEV, 128, D)], (8, 9))
    dh2p, dh2pb = _to_chunked(dh2), _to_chunked(dh2b)
    dh1, dwin0, dwout0, dg_mlp0 = mlp_bwd(h1, hm0, r0, g_mlp0, dh2p, dh2pb, win0, wout0, 0)
    rs_in0, rs_out0 = reduce_scatter(["w_in0", "w_out0"], [dwin0, dwout0], (10, 11))
    dy, dwglu, dbglu = s5_post_bwd(dh1, y, z, wglu)
    rs_glu, = reduce_scatter(["s5_w_glu"], [dwglu], (12, 13))
    du, dbm, dcmt, dlam = s5_core_bwd(hn0, dy, bmb, lam, cmb)
    dxp, dg_mix0, dd = s5_pre_bwd(xp, g_mix0, du, dy, d_skip, dh1)

    big = {n: adam_big(*opt[n], *rs, chip, f"adam_{n}")
           for n, rs in zip(("w_kv", "w_q", "w_o", "s5_w_glu"), rs_attn + [rs_glu])}
    for n, rs1, rs0 in (("w_mlp_in", rs_in1, rs_in0), ("w_mlp_out", rs_out1, rs_out0)):
        res = adam_big(*opt[n], *rs1, chip, f"adam_{n}1", layer=1)
        big[n] = adam_big(*opt[n], *rs0, chip, f"adam_{n}0", layer=0, prev=res)
    grad_x = _from_chunked(dxp)
    da_re, da_im, dlog_dt, db_re, db_im, dc_re, dc_im = prep_vjp((dlam, dbm, dcmt.transpose(0, 2, 1)))

    def lanes(v_):
        v_ = v_.reshape(1, -1)
        return jnp.pad(v_, ((0, 0), (0, D - v_.shape[1])))

    small = jnp.concatenate([
        dg_mix0[0:1], dg_mix1[0:1], dg_mlp0[0:1], dg_mlp1[0:1], dg_kv[0:1], dg_fin[0:1], dd[0:1], dbq[0:1], dbo[0:1],
        dbglu[0:1].reshape(2, D), lanes(dbkv[0:1]),
        lanes(dsink[:, 0, :Q_PER_KV]), lanes(dlog_dt), lanes(loss[0:1, 0:1]), jnp.zeros((1, D), f32),
        da_re.reshape(4, D), da_im.reshape(4, D),
        db_re.transpose(0, 2, 1).reshape(64, D), db_im.transpose(0, 2, 1).reshape(64, D),
        dc_re.reshape(64, D), dc_im.reshape(64, D)], axis=0)
    return loss, grad_x, small, big


_ANY = pl.BlockSpec(memory_space=pl.ANY)


def _pos():
    return lax.axis_index("x"), lax.axis_index("y"), lax.axis_index("c")


def _other_chips(x, y):
    return [(1 - x, y), (x, 1 - y), (1 - x, 1 - y)]


def all_gather(arrs):
    n = len(arrs)

    def body(*refs):
        ins, outs = refs[:n], refs[n:2 * n]
        send_sems, recv_sems, local_sems = refs[2 * n:]
        x, y, c = _pos()
        me, sib = (x, y, c), (x, y, 1 - c)
        chips = _other_chips(x, y)

        def copy(a, k, block, to, src=None):
            dst = outs[a].at[4 * block[0] + 2 * block[1] + block[2]]
            return pltpu.make_async_remote_copy(
                src_ref=dst if src is None else src, dst_ref=dst, send_sem=send_sems.at[a, k],
                recv_sem=recv_sems.at[a, k], device_id=to, device_id_type=MESH)

        mine = [pltpu.make_async_copy(ins[a], outs[a].at[4 * x + 2 * y + c], local_sems.at[a]) for a in range(n)]
        for cp in mine:
            cp.start()
        first = []
        for a in range(n):
            first.append(copy(a, 0, me, sib, src=ins[a]))
            first += [copy(a, 1 + j, me, (*chip, c), src=ins[a]) for j, chip in enumerate(chips)]
        for cp in first:
            cp.start()
        passed = []
        for j, chip in enumerate(chips):
            for a in range(n):
                copy(a, 1 + j, (*chip, c), me).wait_recv()
                cp = copy(a, 4 + j, (*chip, c), sib)
                cp.start()
                passed.append(cp)
        for a in range(n):
            copy(a, 0, sib, me).wait_recv()
            for j, chip in enumerate(chips):
                copy(a, 4 + j, (*chip, 1 - c), me).wait_recv()
        for cp in first + passed:
            cp.wait_send()
        for cp in mine:
            cp.wait()

    return pl.pallas_call(
        body, name="all_gather", in_specs=[_ANY] * n, out_specs=[_ANY] * n,
        out_shape=[SDS((NDEV,) + a.shape, a.dtype) for a in arrs],
        scratch_shapes=[pltpu.SemaphoreType.DMA((n, 7)), pltpu.SemaphoreType.DMA((n, 7)),
                        pltpu.SemaphoreType.DMA((n,))],
    )(*arrs)


def rs_pair(grads):
    n = len(grads)

    def body(*refs):
        ins, outs = refs[:n], refs[n:2 * n]
        send_sems, recv_sems = refs[2 * n:]
        x, y, c = _pos()
        cps = []
        for a in range(n):
            for k in range(4):
                cps.append(pltpu.make_async_remote_copy(
                    src_ref=ins[a].at[2 * k + 1 - c], dst_ref=outs[a].at[k], send_sem=send_sems.at[a, k],
                    recv_sem=recv_sems.at[a, k], device_id=(x, y, 1 - c), device_id_type=MESH))
        for cp in cps:
            cp.start()
        for cp in cps:
            cp.wait_recv()
        for cp in cps:
            cp.wait_send()

    return pl.pallas_call(
        body, name="rs_pair", in_specs=[_ANY] * n, out_specs=[_ANY] * n,
        out_shape=[SDS((4,) + g.shape[1:], g.dtype) for g in grads],
        scratch_shapes=[pltpu.SemaphoreType.DMA((n, 4)), pltpu.SemaphoreType.DMA((n, 4))],
    )(*grads)


def rs_chips(parts):
    n = len(parts)

    def body(*refs):
        ins, outs = refs[:n], refs[n:2 * n]
        send_sems, recv_sems = refs[2 * n:]
        x, y, c = _pos()
        cps = []
        for a in range(n):
            for r, (px, py) in enumerate(_other_chips(x, y)):
                cps.append(pltpu.make_async_remote_copy(
                    src_ref=ins[a].at[2 * px + py], dst_ref=outs[a].at[r], send_sem=send_sems.at[a, r],
                    recv_sem=recv_sems.at[a, r], device_id=(px, py, c), device_id_type=MESH))
        for cp in cps:
            cp.start()
        for cp in cps:
            cp.wait_recv()
        for cp in cps:
            cp.wait_send()

    return pl.pallas_call(
        body, name="rs_chips", in_specs=[_ANY] * n, out_specs=[_ANY] * n,
        out_shape=[SDS((3,) + g.shape[1:], g.dtype) for g in parts],
        scratch_shapes=[pltpu.SemaphoreType.DMA((n, 3)), pltpu.SemaphoreType.DMA((n, 3))],
    )(*parts)


def _row_tile(r, c):
    return min(r, max(8, (512 * 1024) // c))


def add_pairs(g, r1, core, name):
    _, R, C = g.shape
    tr = _row_tile(R, C)

    def body(core_ref, g_ref, r_ref, o_ref):
        o_ref[...] = (g_ref[...].astype(f32) + r_ref[...].astype(f32)).astype(bf16)

    return pl.pallas_call(
        body, name=name, out_shape=SDS((4, R, C), bf16),
        grid_spec=pltpu.PrefetchScalarGridSpec(
            num_scalar_prefetch=1, grid=(4, R // tr),
            in_specs=[pl.BlockSpec((None, tr, C), lambda k, i, core: (2 * k + core[0], i, 0)),
                      pl.BlockSpec((None, tr, C), lambda k, i, core: (k, i, 0))],
            out_specs=pl.BlockSpec((None, tr, C), lambda k, i, core: (k, i, 0))),
        compiler_params=_cp(dimension_semantics=("arbitrary", "arbitrary")),
    )(core, g, r1)


def _adamw(w, g, m, v):
    m = ADAM_B1 * m + (1.0 - ADAM_B1) * g
    v = ADAM_B2 * v + (1.0 - ADAM_B2) * (g * g)
    m_hat = m / (1.0 - ADAM_B1 ** ADAM_STEP)
    v_hat = v / (1.0 - ADAM_B2 ** ADAM_STEP)
    delta = -ADAM_LR * (m_hat / (jnp.sqrt(v_hat) + ADAM_EPS) + ADAM_WD * w)
    return delta, m, v


def adam_big(w, m, v, part, r2, chip, name, layer=0, prev=None):
    L, R, C = w.shape
    tr = _row_tile(R, C)

    def body(chip_ref, w_ref, m_ref, v_ref, p_ref, r_ref, *rest):
        g_out, d_out, m_out, v_out = rest[-4:]
        g = p_ref[...].astype(f32) + r_ref[0].astype(f32) + r_ref[1].astype(f32) + r_ref[2].astype(f32)
        d, m_, v_ = _adamw(w_ref[...], g, m_ref[...], v_ref[...])
        g_out[...] = g
        d_out[...] = d
        m_out[...] = m_
        v_out[...] = v_

    blk = pl.BlockSpec((None, tr, C), lambda i, chip: (layer, i, 0))
    extra = [] if prev is None else list(prev)
    return pl.pallas_call(
        body, name=name, out_shape=[SDS((L, R, C), f32)] * 4,
        grid_spec=pltpu.PrefetchScalarGridSpec(
            num_scalar_prefetch=1, grid=(R // tr,),
            in_specs=[blk, blk, blk,
                      pl.BlockSpec((None, tr, C), lambda i, chip: (chip[0], i, 0)),
                      pl.BlockSpec((3, tr, C), lambda i, chip: (0, i, 0))] + [_ANY] * len(extra),
            out_specs=[blk] * 4),
        input_output_aliases={6 + k: k for k in range(len(extra))},
        compiler_params=_cp(dimension_semantics=("arbitrary",)),
    )(chip, w, m, v, part, r2, *extra)


def allreduce_small(buf, chips=None):
    shp = buf.shape
    half = (shp[0] // 16) * 8
    parts = (pl.ds(0, half), pl.ds(half, shp[0] - half))
    n_c = 0 if chips is None else len(chips.arrs)

    def body(in_ref, *refs):
        c_in, out_ref, c_out = refs[:n_c], refs[n_c], refs[n_c + 1:2 * n_c + 1]
        acc1, acc2, r0, r1, r2, send_sems, recv_sems = refs[2 * n_c + 1:2 * n_c + 8]
        c_sems = refs[2 * n_c + 8:]
        if chips is not None:
            chips.start(c_in, c_out, c_sems)
        x, y, c = _pos()
        across = [(1 - x, y, c), (x, 1 - y, c)]

        def exchange(src, rcv, dst, copies):
            cps = [pltpu.make_async_remote_copy(
                src_ref=src.at[rows], dst_ref=rcv.at[rows], send_sem=send_sems.at[k], recv_sem=recv_sems.at[k],
                device_id=peer, device_id_type=MESH) for k, rows, peer in copies]
            for cp in cps:
                cp.start()
            for cp in cps:
                cp.wait()
            dst[...] = src[...] + rcv[...]

        exchange(in_ref, r0, acc1, [(0, pl.ds(0, shp[0]), (x, y, 1 - c))])
        exchange(acc1, r1, acc2, [(1, parts[0], across[0]), (2, parts[1], across[1])])
        exchange(acc2, r2, out_ref, [(3, parts[0], across[1]), (4, parts[1], across[0])])
        if chips is not None:
            chips.finish(c_in, c_out, c_sems)

    vm = pl.BlockSpec(memory_space=pltpu.VMEM)
    res = pl.pallas_call(
        body, name="allreduce_small", in_specs=[vm] + [_ANY] * n_c, out_specs=[vm] + [_ANY] * n_c,
        out_shape=[SDS(shp, f32)] + ([] if chips is None else chips.out_shape),
        scratch_shapes=[pltpu.VMEM(shp, f32)] * 5 + [pltpu.SemaphoreType.DMA((5,)), pltpu.SemaphoreType.DMA((5,))]
        + ([] if chips is None else chips.scratch),
    )(buf, *([] if chips is None else chips.arrs))
    if chips is not None:
        chips.result = list(res[1:])
    return res[0]


SMALL_ROWS = {'norm_mix': (0, 2, D), 'norm_mlp': (2, 2, D), 'norm_kv': (4, 1, D), 'norm_final': (5, 1, D),
              's5_d': (6, 1, D), 'b_q': (7, 1, D), 'b_o': (8, 1, D), 's5_b_glu': (9, 2, D), 'b_kv': (11, 1, 512),
              'sinks': (12, 1, 16), 's5_log_dt': (13, 1, 64), 's5_a_re': (16, 4, D), 's5_a_im': (20, 4, D),
              's5_b_re': (24, 64, D), 's5_b_im': (88, 64, D), 's5_c_re': (152, 64, D), 's5_c_im': (216, 64, D)}
LOSS_ROW = 14
ROW_PARAMS = ['norm_mix', 'norm_mlp', 'norm_kv', 'norm_final', 'b_q', 'b_o', 'b_kv', 'sinks', 's5_log_dt']
SHARD_PARAMS = ['s5_d', 's5_b_glu']
S5_PARAMS = ['s5_a_re', 's5_a_im', 's5_b_re', 's5_b_im', 's5_c_re', 's5_c_im']


def adam_small(dev, gsum, s5_grads, w, m, v):
    names = ROW_PARAMS + SHARD_PARAMS + S5_PARAMS
    n_g = len(ROW_PARAMS) + len(SHARD_PARAMS)

    def body(dev_ref, gs_ref, *refs):
        pos = [0]

        def take(k):
            r = refs[pos[0]:pos[0] + k]
            pos[0] += k
            return r

        g5 = take(len(S5_PARAMS))
        wr, mr, vr = take(len(names)), take(len(names)), take(len(names))
        g_out = take(n_g)
        d_out, m_out, v_out = take(len(names)), take(len(names)), take(len(names))
        dv = dev_ref[0]
        for i, n in enumerate(names):
            if n in S5_PARAMS:
                g = g5[S5_PARAMS.index(n)][...]
            elif n in SHARD_PARAMS:
                r0, _, _ = SMALL_ROWS[n]
                ln = wr[i].shape[1]
                g = jnp.zeros((1, ln), f32)
                for k in range(NDEV):
                    off = k * ln
                    piece = gs_ref[r0 + off // D:r0 + off // D + 1, off % D:off % D + ln]
                    g = g + jnp.where(dv == k, piece, 0.0)
                g_out[i][...] = g
            else:
                r0, nr, nl = SMALL_ROWS[n]
                g = gs_ref[r0:r0 + nr, 0:nl]
                g_out[i][...] = g
            d, m_, v_ = _adamw(wr[i][...], g, mr[i][...], vr[i][...])
            d_out[i][...] = d
            m_out[i][...] = m_
            v_out[i][...] = v_

    vm = pl.BlockSpec(memory_space=pltpu.VMEM)
    ins = [s5_grads[n] for n in S5_PARAMS] + [d[n] for d in (w, m, v) for n in names]
    shapes = [SDS(w[n].shape, f32) for n in names]
    res = pl.pallas_call(
        body, name="adam_small", in_specs=[pl.BlockSpec(memory_space=pltpu.SMEM)] + [vm] * (1 + len(ins)),
        out_specs=[vm] * (n_g + 3 * len(names)), out_shape=shapes[:n_g] + shapes * 3,
        compiler_params=_cp(),
    )(dev, gsum, *ins)
    g_o = dict(zip(names[:n_g], res[:n_g]))
    rest = res[n_g:]
    k = len(names)
    return g_o, dict(zip(names, rest[:k])), dict(zip(names, rest[k:2 * k])), dict(zip(names, rest[2 * k:]))


WEIGHTS = ['norm_mix', 'norm_mlp', 'norm_kv', 'norm_final', 's5_a_re', 's5_a_im', 's5_log_dt', 's5_b_re', 's5_b_im',
           's5_c_re', 's5_c_im', 's5_d', 's5_w_glu', 's5_b_glu', 'w_kv', 'b_kv', 'w_q', 'b_q', 'sinks', 'w_o', 'b_o',
           'w_mlp_in', 'w_mlp_out']
BIG = ['s5_w_glu', 'w_kv', 'w_q', 'w_o', 'w_mlp_in', 'w_mlp_out']
BIG_2D = {'s5_w_glu': (D, 256), 'w_kv': (128, 512), 'w_q': (128, D), 'w_o': (128, D), 'w_mlp_in': (2 * D, 512),
          'w_mlp_out': (2 * 512, D)}
SHARDED_SMALL = {'s5_d': D, 's5_b_glu': 2 * D}
SMALL = [n for n in WEIGHTS if n not in BIG]
SMALL_SIZE = {'norm_mix': 2 * D, 'norm_mlp': 2 * D, 'norm_kv': D, 'norm_final': D, 's5_a_re': 4096, 's5_a_im': 4096,
              's5_log_dt': 64, 's5_b_re': 65536, 's5_b_im': 65536, 's5_c_re': 65536, 's5_c_im': 65536, 's5_d': D,
              's5_b_glu': 2 * D, 'b_kv': 512, 'b_q': D, 'sinks': 16, 'b_o': D}


def _pack(vals):
    parts = []
    for n in SMALL:
        v = vals[n].reshape(-1).astype(f32)
        parts.append(jnp.pad(v, (0, (-v.shape[0]) % 128)))
    flat = jnp.concatenate(parts)
    flat = jnp.pad(flat, (0, (-flat.shape[0]) % 1024))
    return flat.reshape(-1, 128)


def _unpack(buf):
    flat = buf.reshape(-1)
    out, off = {}, 0
    for n in SMALL:
        sz = SMALL_SIZE[n]
        out[n] = flat[off:off + sz]
        off += sz + (-sz) % 128
    return out


def kernel(x, norm_mix, norm_mlp, norm_kv, norm_final, s5_a_re, s5_a_im, s5_log_dt, s5_b_re, s5_b_im, s5_c_re, s5_c_im, s5_d, s5_w_glu, s5_b_glu, w_kv, b_kv, w_q, b_q, sinks, w_o, b_o, w_mlp_in, w_mlp_out, loss_target, m_norm_mix, m_norm_mlp, m_norm_kv, m_norm_final, m_s5_a_re, m_s5_a_im, m_s5_log_dt, m_s5_b_re, m_s5_b_im, m_s5_c_re, m_s5_c_im, m_s5_d, m_s5_w_glu, m_s5_b_glu, m_w_kv, m_b_kv, m_w_q, m_b_q, m_sinks, m_w_o, m_b_o, m_w_mlp_in, m_w_mlp_out, v_norm_mix, v_norm_mlp, v_norm_kv, v_norm_final, v_s5_a_re, v_s5_a_im, v_s5_log_dt, v_s5_b_re, v_s5_b_im, v_s5_c_re, v_s5_c_im, v_s5_d, v_s5_w_glu, v_s5_b_glu, v_w_kv, v_b_kv, v_w_q, v_b_q, v_sinks, v_w_o, v_b_o, v_w_mlp_in, v_w_mlp_out):
    w = dict(norm_mix=norm_mix, norm_mlp=norm_mlp, norm_kv=norm_kv, norm_final=norm_final, s5_a_re=s5_a_re,
             s5_a_im=s5_a_im, s5_log_dt=s5_log_dt, s5_b_re=s5_b_re, s5_b_im=s5_b_im, s5_c_re=s5_c_re, s5_c_im=s5_c_im,
             s5_d=s5_d, s5_w_glu=s5_w_glu, s5_b_glu=s5_b_glu, w_kv=w_kv, b_kv=b_kv, w_q=w_q, b_q=b_q, sinks=sinks,
             w_o=w_o, b_o=b_o, w_mlp_in=w_mlp_in, w_mlp_out=w_mlp_out)
    m = dict(norm_mix=m_norm_mix, norm_mlp=m_norm_mlp, norm_kv=m_norm_kv, norm_final=m_norm_final, s5_a_re=m_s5_a_re,
             s5_a_im=m_s5_a_im, s5_log_dt=m_s5_log_dt, s5_b_re=m_s5_b_re, s5_b_im=m_s5_b_im, s5_c_re=m_s5_c_re,
             s5_c_im=m_s5_c_im, s5_d=m_s5_d, s5_w_glu=m_s5_w_glu, s5_b_glu=m_s5_b_glu, w_kv=m_w_kv, b_kv=m_b_kv,
             w_q=m_w_q, b_q=m_b_q, sinks=m_sinks, w_o=m_w_o, b_o=m_b_o, w_mlp_in=m_w_mlp_in, w_mlp_out=m_w_mlp_out)
    v = dict(norm_mix=v_norm_mix, norm_mlp=v_norm_mlp, norm_kv=v_norm_kv, norm_final=v_norm_final, s5_a_re=v_s5_a_re,
             s5_a_im=v_s5_a_im, s5_log_dt=v_s5_log_dt, s5_b_re=v_s5_b_re, s5_b_im=v_s5_b_im, s5_c_re=v_s5_c_re,
             s5_c_im=v_s5_c_im, s5_d=v_s5_d, s5_w_glu=v_s5_w_glu, s5_b_glu=v_s5_b_glu, w_kv=v_w_kv, b_kv=v_b_kv,
             w_q=v_w_q, b_q=v_b_q, sinks=v_sinks, w_o=v_w_o, b_o=v_b_o, w_mlp_in=v_w_mlp_in, w_mlp_out=v_w_mlp_out)
    xi, yi, ci = _pos()
    dev = 4 * xi + 2 * yi + ci
    core = ci.reshape(1).astype(jnp.int32)
    chip = (2 * xi + yi).reshape(1).astype(jnp.int32)

    shards = {
        "s5_w_glu": s5_w_glu[0].astype(bf16), "w_kv": w_kv.astype(bf16), "w_q": w_q[0].astype(bf16),
        "w_o": w_o[0].astype(bf16), "w_in0": w_mlp_in[0].astype(bf16), "w_in1": w_mlp_in[1].astype(bf16),
        "w_out0": w_mlp_out[0].astype(bf16), "w_out1": w_mlp_out[1].astype(bf16),
        "vecs": jnp.broadcast_to(jnp.concatenate([s5_d, s5_b_glu], axis=1), (8, 384)),
    }
    as3d = lambda a, n: a if a.ndim == 3 and a.shape[0] == 2 else a.reshape((1,) + BIG_2D[n])
    opt = {n: (as3d(w[n], n), as3d(m[n], n), as3d(v[n], n)) for n in BIG}
    _, grad_x, grads, big = fwd_bwd(x[0], loss_target[0], {n: w[n] for n in SMALL}, shards, opt, core, chip)

    gsum = allreduce_small(grads)

    out_g, out_d, out_m, out_v = {}, {}, {}, {}
    for n in BIG:
        out_g[n], out_d[n], out_m[n], out_v[n] = [r.reshape(w[n].shape) for r in big[n]]

    loss = gsum[LOSS_ROW, 0]
    swapped = ("s5_b_re", "s5_b_im")
    swap = lambda a: a.transpose(0, 1, 3, 2)

    def kernel_side(d):
        d = {n: (d[n].reshape(1, -1) if d[n].ndim == 1 else d[n]) for n in SMALL}
        d.update({n: swap(d[n]) for n in swapped})
        return d

    s5_g = {}
    for n in S5_PARAMS:
        r0, nr, _ = SMALL_ROWS[n]
        s5_g[n] = gsum[r0:r0 + nr].reshape((1, 64, 16, 64) if n in swapped else w[n].shape)
        out_g[n] = s5_g[n]
    g_s, d_s, m_s, v_s = adam_small(dev.reshape(1).astype(jnp.int32), gsum, s5_g, kernel_side(w), kernel_side(m),
                                    kernel_side(v))
    for src, dst in ((g_s, out_g), (d_s, out_d), (m_s, out_m), (v_s, out_v)):
        dst.update(src)
    for dst in (out_g, out_d, out_m, out_v):
        for n in SMALL:
            dst[n] = (swap(dst[n]) if n in swapped else dst[n]).reshape(w[n].shape)

    return (loss, grad_x[None], *[out_g[n] for n in WEIGHTS], *[out_d[n] for n in WEIGHTS],
            *[out_m[n] for n in WEIGHTS], *[out_v[n] for n in WEIGHTS])
```

```python
import functools
import math

import jax
import jax.numpy as jnp
from jax import lax
from jax.experimental import pallas as pl
from jax.experimental.pallas import tpu as pltpu
from jax.experimental.pallas import tpu_sc as plsc

f32 = jnp.float32
bf16 = jnp.bfloat16
SDS = jax.ShapeDtypeStruct

T = 2048
D = 1024
NDEV = 8
NORM_EPS = 1e-5
S5_G, S5_C, S5_P = 64, 16, 64
S5_SUB = 8
S5_CH = 8
S5_STEPS = T // S5_CH
DT_MIN_LAMBDA = -1e-4
HEAD_DIM = 64
N_KV = 4
Q_PER_KV = 4
BLK = 128
D_FF_SHARD = 512
ADAM_LR, ADAM_B1, ADAM_B2, ADAM_EPS, ADAM_WD, ADAM_STEP = 0.001, 0.9, 0.999, 1e-08, 0.01, 10
VMEM_LIMIT = 56 * 1024 * 1024
MESH = pl.DeviceIdType.MESH


def _cp(**kw):
    return pltpu.CompilerParams(vmem_limit_bytes=VMEM_LIMIT, **kw)


def _dot(a, b):
    return jnp.dot(a, b, preferred_element_type=f32)


def _dot_nt(a, b):
    return lax.dot_general(a, b, (((1,), (1,)), ((), ())), preferred_element_type=f32)


def _dot_tn(a, b):
    return lax.dot_general(a, b, (((0,), (0,)), ((), ())), preferred_element_type=f32)


def _rms(x, g):
    r = lax.rsqrt(jnp.mean(x * x, axis=-1, keepdims=True) + NORM_EPS)
    return x * r * g, r


def _rms_bwd(x, g, dy):
    r = lax.rsqrt(jnp.mean(x * x, axis=-1, keepdims=True) + NORM_EPS)
    u = dy * g
    dx = r * u - (r * r * r) * x * jnp.mean(u * x, axis=-1, keepdims=True)
    return dx, dy * x * r


def _colsum8(v):
    s = jnp.sum(v, axis=0, keepdims=True)
    row = lax.broadcasted_iota(jnp.int32, (8, v.shape[1]), 0)
    return jnp.where(row == 0, jnp.broadcast_to(s, (8, v.shape[1])), 0.0)


def _full(shape):
    nd = len(shape)
    return pl.BlockSpec(shape, lambda *_: (0,) * nd, pipeline_mode=pl.Buffered(1))


_ANY = pl.BlockSpec(memory_space=pl.ANY)


def _pos():
    return lax.axis_index("x"), lax.axis_index("y"), lax.axis_index("c")


def _other_chips(x, y):
    return [(1 - x, y), (x, 1 - y), (1 - x, 1 - y)]


class BgGather:
    SIB, XN, YN, FWD_Y, FWD_X, SIB_X, SIB_Y, SIB_D = range(8)

    def __init__(self, arrs, mids=(0.5, 0.75)):
        n = len(arrs)
        self.arrs = list(arrs)
        self.out_shape = [SDS((NDEV,) + a.shape, a.dtype) for a in arrs]
        self.scratch = [pltpu.SemaphoreType.DMA((n, 8)), pltpu.SemaphoreType.DMA((n, 8)),
                        pltpu.SemaphoreType.DMA((n,))]
        self.mids = mids
        self.result = None

    @staticmethod
    def peers(x, y, c):
        return [(x, y, 1 - c), (1 - x, y, c), (x, 1 - y, c)]

    def mid_steps(self, nsteps):
        at = lambda f: min(nsteps - 1, max(0, int(f * nsteps) - 1))
        return [(at(self.mids[0]), self.mid), (max(at(self.mids[0]), at(self.mids[1])), self.mid2)]

    def _halves(self, a):
        rows = self.arrs[a].shape[0]
        cut = rows // 2 if rows >= 32 else rows
        return (0, cut), (cut, rows - cut)

    def _copy(self, ins, outs, sems, a, k, block, to, own=False, part=None):
        slot = 4 * block[0] + 2 * block[1] + block[2]
        rows = pl.ds(0, self.arrs[a].shape[0]) if part is None else pl.ds(*self._halves(a)[part])
        dst = outs[a].at[slot, rows]
        return pltpu.make_async_remote_copy(
            src_ref=ins[a].at[rows] if own else dst, dst_ref=dst, send_sem=sems[0].at[a, k],
            recv_sem=sems[1].at[a, k], device_id=to, device_id_type=MESH)

    def _mine(self, ins, outs, sems):
        x, y, c = _pos()
        return [pltpu.make_async_copy(ins[a], outs[a].at[4 * x + 2 * y + c], sems[2].at[a])
                for a in range(len(self.arrs))]

    def _split(self, a):
        return self._halves(a)[1][1] > 0

    def _sends(self, ins, outs, sems, phase):
        x, y, c = _pos()
        me, sib, xn, yn, dg = (x, y, c), (x, y, 1 - c), (1 - x, y, c), (x, 1 - y, c), (1 - x, 1 - y, c)
        cps = []
        for a in range(len(self.arrs)):
            cp = lambda k, block, to, **kw: self._copy(ins, outs, sems, a, k, block, to, **kw)
            if phase == 0:
                cps += [cp(self.SIB, me, sib, own=True), cp(self.XN, me, xn, own=True), cp(self.YN, me, yn, own=True)]
            elif phase == 1:
                cps.append(cp(self.FWD_Y, xn, yn, part=0))
                if self._split(a):
                    cps.append(cp(self.FWD_X, yn, xn, part=1))
                cps += [cp(self.SIB_X, xn, sib), cp(self.SIB_Y, yn, sib)]
            else:
                cps.append(cp(self.SIB_D, dg, sib))
        return cps

    def _arrivals(self, ins, outs, sems, phase):
        x, y, c = _pos()
        me, xn, yn, dg = (x, y, c), (1 - x, y, c), (x, 1 - y, c), (1 - x, 1 - y, c)
        cps = []
        for a in range(len(self.arrs)):
            cp = lambda k, block, **kw: self._copy(ins, outs, sems, a, k, block, me, **kw)
            if phase == 1:
                cps += [cp(self.XN, xn), cp(self.YN, yn)]
            elif phase == 2:
                cps.append(cp(self.FWD_Y, dg, part=0))
                if self._split(a):
                    cps.append(cp(self.FWD_X, dg, part=1))
            else:
                cps += [cp(self.SIB, (x, y, 1 - c)), cp(self.SIB_X, (1 - x, y, 1 - c)),
                        cp(self.SIB_Y, (x, 1 - y, 1 - c)), cp(self.SIB_D, (1 - x, 1 - y, 1 - c))]
        return cps

    def start(self, ins, outs, sems):
        for cp in self._mine(ins, outs, sems) + self._sends(ins, outs, sems, 0):
            cp.start()

    def mid(self, ins, outs, sems):
        for cp in self._arrivals(ins, outs, sems, 1):
            cp.wait_recv()
        for cp in self._sends(ins, outs, sems, 1):
            cp.start()

    def mid2(self, ins, outs, sems):
        for cp in self._arrivals(ins, outs, sems, 2):
            cp.wait_recv()
        for cp in self._sends(ins, outs, sems, 2):
            cp.start()

    def finish(self, ins, outs, sems):
        for cp in self._arrivals(ins, outs, sems, 3):
            cp.wait_recv()
        for ph in range(3):
            for cp in self._sends(ins, outs, sems, ph):
                cp.wait_send()
        for cp in self._mine(ins, outs, sems):
            cp.wait()


def sc_comm(g, collective_id, name):
    srcs = [jax.new_ref(a, memory_space=pltpu.MemorySpace.HBM) for a in g.arrs]
    dsts = [jax.empty_ref(s, memory_space=pltpu.MemorySpace.HBM) for s in g.out_shape]

    @pl.kernel(mesh=plsc.ScalarSubcoreMesh(axis_name="sequencer", num_cores=1), name=name,
               scratch_types=tuple(g.scratch), compiler_params=pltpu.CompilerParams(collective_id=collective_id))
    def launch(*sems):
        peers = g.peers(*_pos())
        barrier = pltpu.get_barrier_semaphore()
        for peer in peers:
            pl.semaphore_signal(barrier, inc=1, device_id=peer, device_id_type=MESH)
        pl.semaphore_wait(barrier, len(peers))
        g.start(srcs, dsts, sems)
        for _, phase in g.mid_steps(1):
            phase(srcs, dsts, sems)
        g.finish(srcs, dsts, sems)

    launch()
    return [d[...] for d in dsts]


def sc_gather(arrs, collective_id, name):
    return sc_comm(BgGather(arrs), collective_id, name)


class BgPair:
    def __init__(self, arrs):
        n = len(arrs)
        self.arrs = list(arrs)
        self.out_shape = [SDS((4,) + a.shape[1:], a.dtype) for a in arrs]
        self.scratch = [pltpu.SemaphoreType.DMA((n, 4)), pltpu.SemaphoreType.DMA((n, 4))]
        self.result = None

    @staticmethod
    def peers(x, y, c):
        return [(x, y, 1 - c)]

    def mid_steps(self, nsteps):
        return []

    def _copies(self, ins, outs, sems):
        x, y, c = _pos()
        return [pltpu.make_async_remote_copy(
            src_ref=ins[a].at[2 * k + 1 - c], dst_ref=outs[a].at[k], send_sem=sems[0].at[a, k],
            recv_sem=sems[1].at[a, k], device_id=(x, y, 1 - c), device_id_type=MESH)
            for a in range(len(self.arrs)) for k in range(4)]

    def start(self, ins, outs, sems):
        for cp in self._copies(ins, outs, sems):
            cp.start()

    def finish(self, ins, outs, sems):
        cps = self._copies(ins, outs, sems)
        for cp in cps:
            cp.wait_recv()
        for cp in cps:
            cp.wait_send()


class BgChips(BgPair):
    def __init__(self, arrs):
        n = len(arrs)
        self.arrs = list(arrs)
        self.out_shape = [SDS((3,) + a.shape[1:], a.dtype) for a in arrs]
        self.scratch = [pltpu.SemaphoreType.DMA((n, 3)), pltpu.SemaphoreType.DMA((n, 3))]
        self.result = None

    @staticmethod
    def peers(x, y, c):
        return [(px, py, c) for px, py in _other_chips(x, y)]

    def _copies(self, ins, outs, sems):
        x, y, c = _pos()
        return [pltpu.make_async_remote_copy(
            src_ref=ins[a].at[2 * px + py], dst_ref=outs[a].at[r], send_sem=sems[0].at[a, r],
            recv_sem=sems[1].at[a, r], device_id=(px, py, c), device_id_type=MESH)
            for a in range(len(self.arrs)) for r, (px, py) in enumerate(_other_chips(x, y))]


class AdamRider:
    def __init__(self, w, m, v, part, r2, layer=0, prev=None):
        self.arrs = [w, m, v, part, r2] + list(prev or [])
        self.n_prev = len(prev or [])
        self.layer = layer
        self.out_shape = [SDS(w.shape, f32)] * 4
        self.scratch = []
        self.aliases = {5 + k: k for k in range(self.n_prev)}
        self.result = None

    def _tile(self, grid):
        assert len(grid) == 1
        _, R, C = self.arrs[0].shape
        return R // grid[0], C

    def in_specs(self, grid):
        tr, C = self._tile(grid)
        layer = self.layer
        blk = pl.BlockSpec((None, tr, C), lambda b: (layer, b, 0))
        mine = pl.BlockSpec((None, tr, C), lambda b: (2 * lax.axis_index("x") + lax.axis_index("y"), b, 0))
        return [blk, blk, blk, mine, pl.BlockSpec((3, tr, C), lambda b: (0, b, 0))] + [_ANY] * self.n_prev

    def out_specs(self, grid):
        tr, C = self._tile(grid)
        layer = self.layer
        return [pl.BlockSpec((None, tr, C), lambda b: (layer, b, 0))] * 4

    def mid_steps(self, nsteps):
        return []

    def start(self, ins, outs, sems):
        pass

    finish = start

    def step(self, ins, outs, sems):
        w_ref, m_ref, v_ref, p_ref, r_ref = ins[:5]
        g = p_ref[...].astype(f32) + r_ref[0].astype(f32) + r_ref[1].astype(f32) + r_ref[2].astype(f32)
        d, m_, v_ = _adamw(w_ref[...], g, m_ref[...], v_ref[...])
        for ref, val in zip(outs, (g, d, m_, v_)):
            ref[...] = val


def _call(bgs, body, *, name, grid, in_specs, out_specs, out_shape, scratch_shapes=(), compiler_params=None):
    single = not isinstance(out_shape, (list, tuple))
    out_specs_l = [out_specs] if single else list(out_specs)
    out_shape_l = [out_shape] if single else list(out_shape)
    bgs = [b for b in (bgs or []) if b is not None]
    n_in, n_out, n_sc = len(in_specs), len(out_shape_l), len(scratch_shapes)
    nsteps = math.prod(grid)
    b_in_specs = [b.in_specs(grid) if hasattr(b, "in_specs") else [_ANY] * len(b.arrs) for b in bgs]
    b_out_specs = [b.out_specs(grid) if hasattr(b, "out_specs") else [_ANY] * len(b.out_shape) for b in bgs]
    aliases, i_off, o_off = {}, n_in, n_out
    for b in bgs:
        aliases.update({i_off + i: o_off + o for i, o in getattr(b, "aliases", {}).items()})
        i_off, o_off = i_off + len(b.arrs), o_off + len(b.out_shape)

    def full(*refs):
        pos = [0]

        def take(k):
            r = refs[pos[0]:pos[0] + k]
            pos[0] += k
            return r

        ins = take(n_in)
        b_ins = [take(len(b.arrs)) for b in bgs]
        outs = take(n_out)
        b_outs = [take(len(b.out_shape)) for b in bgs]
        sc = take(n_sc)
        b_sc = [take(len(b.scratch)) for b in bgs]
        if bgs:
            step = pl.program_id(0)
            for d in range(1, len(grid)):
                step = step * grid[d] + pl.program_id(d)

            @pl.when(step == 0)
            def _():
                for b, i_, o_, s_ in zip(bgs, b_ins, b_outs, b_sc):
                    b.start(i_, o_, s_)

        body(*ins, *outs, *sc)
        if bgs:
            for b, i_, o_, s_ in zip(bgs, b_ins, b_outs, b_sc):
                if hasattr(b, "step"):
                    b.step(i_, o_, s_)
                for at, fn in b.mid_steps(nsteps):
                    @pl.when(step == at)
                    def _():
                        fn(i_, o_, s_)

            @pl.when(step == nsteps - 1)
            def _():
                for b, i_, o_, s_ in zip(bgs, b_ins, b_outs, b_sc):
                    b.finish(i_, o_, s_)

    def run(*args):
        res = pl.pallas_call(
            full, name=name, grid=grid,
            in_specs=list(in_specs) + [s for l in b_in_specs for s in l],
            out_specs=out_specs_l + [s for l in b_out_specs for s in l],
            out_shape=out_shape_l + [s for b in bgs for s in b.out_shape],
            scratch_shapes=list(scratch_shapes) + [s for b in bgs for s in b.scratch],
            input_output_aliases=aliases,
            compiler_params=compiler_params,
        )(*args, *[a for b in bgs for a in b.arrs])
        rest = list(res[n_out:])
        for b in bgs:
            b.result, rest = rest[:len(b.out_shape)], rest[len(b.out_shape):]
        return res[0] if single else list(res[:n_out])

    return run


def s5_discretize(a_re, a_im, log_dt, b_re, b_im, c_re, c_im):
    lam_r = jnp.minimum(a_re, DT_MIN_LAMBDA)
    lam_i = a_im
    dt = jnp.exp(log_dt)[:, None]
    e = jnp.exp(lam_r * dt)
    lbr = e * jnp.cos(lam_i * dt)
    lbi = e * jnp.sin(lam_i * dt)
    den = lam_r * lam_r + lam_i * lam_i
    cf_r = ((lbr - 1.0) * lam_r + lbi * lam_i) / den
    cf_i = (lbi * lam_r - (lbr - 1.0) * lam_i) / den
    bb_r = cf_r[:, :, None] * b_re - cf_i[:, :, None] * b_im
    bb_i = cf_r[:, :, None] * b_im + cf_i[:, :, None] * b_re
    eye = jnp.eye(8, dtype=f32)

    def blk_b(m):
        return jnp.einsum('bgpc,gh->bgchp', m.reshape(8, 8, S5_P, S5_C), eye).reshape(8, 128, 512)

    def blk_c(m):
        return jnp.einsum('bgcp,gh->bgphc', m.reshape(8, 8, S5_C, S5_P), eye).reshape(8, 512, 128)

    bm = jnp.concatenate([blk_b(bb_r), blk_b(bb_i)], axis=-1)
    cm = jnp.concatenate([blk_c(c_re), -blk_c(c_im)], axis=1)
    lam = jnp.stack([lbr.reshape(8, 512), lbi.reshape(8, 512)], axis=1)
    lam = jnp.broadcast_to(lam[:, :, None, :], (8, 2, 8, 512))
    return lam, bm, cm


def _cmul(ar, ai, br, bi):
    return ar * br - ai * bi, ar * bi + ai * br


def _shift_rows(v, k, up):
    row = lax.broadcasted_iota(jnp.int32, v.shape, 0)
    if up:
        return jnp.where(row < 8 - k, pltpu.roll(v, 8 - k, 0), 0.0)
    return jnp.where(row >= k, pltpu.roll(v, k, 0), 0.0)


def _chunk_scan(S, lr, li, reverse, aux=None):
    z = jnp.zeros((8, 512), f32)
    U = 4

    def idx(i):
        return (S5_STEPS - 1 - i) if reverse else i

    def rows_of(s):
        return pl.ds(s * 8, 8) if isinstance(s, int) else pl.ds(pl.multiple_of(s * 8, 8), 8)

    def rec(xr, xi, row):
        br = S[row, 0:512]
        bi = S[row, 512:1024]
        return lr * xr - li * xi + br, lr * xi + li * xr + bi

    def step1(i, c):
        for u in range(U):
            c = rec(c[0], c[1], rows_of(idx(i * U + u)))
        return c

    er, ei = lax.fori_loop(0, S5_STEPS // U, step1, (z, z))
    ar, ai = lr, li
    for _ in range(8):
        ar, ai = _cmul(ar, ai, ar, ai)
    cr, ci = _shift_rows(er, 1, reverse), _shift_rows(ei, 1, reverse)
    for k in (1, 2, 4):
        sr, si = _shift_rows(cr, k, reverse), _shift_rows(ci, k, reverse)
        pr, pi_ = _cmul(ar, ai, sr, si)
        cr, ci = cr + pr, ci + pi_
        ar, ai = _cmul(ar, ai, ar, ai)

    if aux is None:
        def step2(i, c):
            for u in range(U):
                row = rows_of(idx(i * U + u))
                c = rec(c[0], c[1], row)
                S[row, 0:512] = c[0]
                S[row, 512:1024] = c[1]
            return c

        lax.fori_loop(0, S5_STEPS // U, step2, (cr, ci))
        return None

    def one(s, c):
        gr0, gi0, dr, di = c
        row = rows_of(s)
        gr, gi = rec(gr0, gi0, row)
        S[row, 0:512] = gr
        S[row, 512:1024] = gi
        prow = rows_of(s - 1)
        xr = aux[prow, 0:512]
        xi = aux[prow, 512:1024]
        return gr, gi, dr + gr * xr + gi * xi, di + gi * xr - gr * xi

    def step2(i, c):
        for u in range(U):
            c = one(S5_STEPS - 1 - (i * U + u), c)
        return c

    c = lax.fori_loop(0, S5_STEPS // U - 1, step2, (cr, ci, z, z))
    for s in range(U - 1, 0, -1):
        c = one(s, c)
    gr, gi, dr, di = c
    row0 = pl.ds(0, 8)
    gr, gi = rec(gr, gi, row0)
    S[row0, 0:512] = gr
    S[row0, 512:1024] = gi
    last = pl.ds((S5_STEPS - 1) * 8, 8)
    xr = _shift_rows(aux[last, 0:512], 1, False)
    xi = _shift_rows(aux[last, 512:1024], 1, False)
    dr = dr + gr * xr + gi * xi
    di = di + gi * xr - gr * xi
    return dr, di


_ROWS = 256


def _row_loop(fn):
    def body(r, c):
        fn(pl.ds(pl.multiple_of(r * _ROWS, _ROWS), _ROWS))
        return c
    lax.fori_loop(0, T // _ROWS, body, 0)


def s5_core_fwd(hn, bm, lam, cm, bg=()):
    def body(u_ref, b_ref, lam_ref, c_ref, ys_ref, S):
        def bu(rows):
            S[rows, :] = _dot(u_ref[rows, :], b_ref[...])
        _row_loop(bu)
        _chunk_scan(S, lam_ref[0], lam_ref[1], False)

        def ys(rows):
            ys_ref[rows, :] = _dot(S[rows, :].astype(bf16), c_ref[...])
        _row_loop(ys)

    return _call(
        bg, body, name="s5_core_fwd", grid=(S5_SUB,),
        in_specs=[pl.BlockSpec((T, 128), lambda b: (0, b)),
                  pl.BlockSpec((None, 128, 1024), lambda b: (b, 0, 0)),
                  pl.BlockSpec((None, 4, 8, 512), lambda b: (b, 0, 0, 0)),
                  pl.BlockSpec((None, 1024, 128), lambda b: (b, 0, 0))],
        out_specs=pl.BlockSpec((T, 128), lambda b: (0, b)),
        out_shape=SDS((T, D), f32),
        scratch_shapes=[pltpu.VMEM((T, 1024), f32)],
        compiler_params=_cp(dimension_semantics=("arbitrary",)),
    )(hn, bm, lam, cm)


_SEG = _ROWS // S5_CH


def _scan_tile(S, lr, li, k, carry, reverse, store, aux=None):
    steps = range(k * _SEG, (k + 1) * _SEG)
    for s in (reversed(steps) if reverse else steps):
        row = pl.ds(s * 8, 8)
        xr, xi = carry[0], carry[1]
        nr = lr * xr - li * xi + S[row, 0:512]
        ni = lr * xi + li * xr + S[row, 512:1024]
        if store:
            S[row, 0:512] = nr
            S[row, 512:1024] = ni
        if aux is not None and s >= 1:
            prow = pl.ds((s - 1) * 8, 8)
            pr, pi_ = aux[prow, 0:512], aux[prow, 512:1024]
            carry = (nr, ni, carry[2] + nr * pr + ni * pi_, carry[3] + ni * pr - nr * pi_)
        elif aux is not None:
            carry = (nr, ni, carry[2], carry[3])
        else:
            carry = (nr, ni)
    return carry


def _chunk_starts(er, ei, lr, li, reverse):
    ar, ai = lr, li
    for _ in range(8):
        ar, ai = _cmul(ar, ai, ar, ai)
    cr, ci = _shift_rows(er, 1, reverse), _shift_rows(ei, 1, reverse)
    for k in (1, 2, 4):
        sr, si = _shift_rows(cr, k, reverse), _shift_rows(ci, k, reverse)
        pr, pi_ = _cmul(ar, ai, sr, si)
        cr, ci = cr + pr, ci + pi_
        ar, ai = _cmul(ar, ai, ar, ai)
    return cr, ci


def s5_core_bwd(hn, dy, bm, lam, cm, bg=()):
    nt = T // _ROWS

    def body(u_ref, dy_ref, b_ref, lam_ref, c_ref, du_ref, db_ref, dct_ref, dlam_ref, S1, S2):
        lr, li, lcr, lci = lam_ref[0], lam_ref[1], lam_ref[2], lam_ref[3]
        z = jnp.zeros((8, 512), f32)
        tile = lambda k: pl.ds(k * _ROWS, _ROWS)
        dyb = lambda k: dy_ref[tile(k), :].astype(bf16)

        c = (z, z)
        for k in range(nt):
            S1[tile(k), :] = _dot(u_ref[tile(k), :], b_ref[...])
            if k >= 1:
                c = _scan_tile(S1, lr, li, k - 1, c, False, False)
        c = _scan_tile(S1, lr, li, nt - 1, c, False, False)

        c = _chunk_starts(c[0], c[1], lr, li, False)
        dct_ref[...] = jnp.zeros_like(dct_ref)
        for k in range(nt):
            c = _scan_tile(S1, lr, li, k, c, False, True)
            if k >= 1:
                dct_ref[...] += _dot_tn(dyb(k - 1), S1[tile(k - 1), :].astype(bf16))
        dct_ref[...] += _dot_tn(dyb(nt - 1), S1[tile(nt - 1), :].astype(bf16))

        S2[tile(nt - 1), :] = _dot_nt(dyb(nt - 1), c_ref[...])
        c = (z, z)
        for k in range(nt - 1, -1, -1):
            if k >= 1:
                S2[tile(k - 1), :] = _dot_nt(dyb(k - 1), c_ref[...])
            c = _scan_tile(S2, lcr, lci, k, c, True, False)

        def dbu(k):
            gb = S2[tile(k), :].astype(bf16)
            db_ref[...] += _dot_tn(u_ref[tile(k), :], gb)
            du_ref[tile(k), :] = _dot_nt(gb, b_ref[...])

        c = _chunk_starts(c[0], c[1], lcr, lci, True) + (z, z)
        db_ref[...] = jnp.zeros_like(db_ref)
        for k in range(nt - 1, -1, -1):
            c = _scan_tile(S2, lcr, lci, k, c, True, True, aux=S1)
            if k + 1 < nt:
                dbu(k + 1)
        dbu(0)
        gr, gi, dr, di = c
        last = pl.ds((S5_STEPS - 1) * 8, 8)
        xr = _shift_rows(S1[last, 0:512], 1, False)
        xi = _shift_rows(S1[last, 512:1024], 1, False)
        dlam_ref[0] = dr + gr * xr + gi * xi
        dlam_ref[1] = di + gi * xr - gr * xi

    return _call(
        bg, body, name="s5_core_bwd", grid=(S5_SUB,),
        in_specs=[pl.BlockSpec((T, 128), lambda b: (0, b)),
                  pl.BlockSpec((T, 128), lambda b: (0, b)),
                  pl.BlockSpec((None, 128, 1024), lambda b: (b, 0, 0)),
                  pl.BlockSpec((None, 4, 8, 512), lambda b: (b, 0, 0, 0)),
                  pl.BlockSpec((None, 1024, 128), lambda b: (b, 0, 0))],
        out_specs=[pl.BlockSpec((T, 128), lambda b: (0, b)),
                   pl.BlockSpec((None, 128, 1024), lambda b: (b, 0, 0)),
                   pl.BlockSpec((None, 128, 1024), lambda b: (b, 0, 0)),
                   pl.BlockSpec((None, 2, 8, 512), lambda b: (b, 0, 0, 0))],
        out_shape=[SDS((T, D), f32), SDS((8, 128, 1024), f32), SDS((8, 128, 1024), f32), SDS((8, 2, 8, 512), f32)],
        scratch_shapes=[pltpu.VMEM((T, 1024), f32), pltpu.VMEM((T, 1024), f32)],
        compiler_params=_cp(dimension_semantics=("arbitrary",)),
    )(hn, dy, bm, lam, cm)


TM = 512
NT = T // TM


def _tile(n=D):
    return pl.BlockSpec((TM, n), lambda i: (i, 0))


def s5_pre(xp, g):
    def body(x_ref, g_ref, hn_ref):
        hn, _ = _rms(x_ref[...], g_ref[...])
        hn_ref[...] = hn.astype(bf16)

    return pl.pallas_call(
        body, name="s5_pre", grid=(NT,), in_specs=[_tile(), _full((1, D))], out_specs=_tile(),
        out_shape=SDS((T, D), bf16), compiler_params=_cp(dimension_semantics=("arbitrary",)),
    )(xp, g)


def _gelu_grad(y):
    c = math.sqrt(2.0 / math.pi)
    t = jnp.tanh(c * (y + 0.044715 * y * y * y))
    return 0.5 * (1.0 + t) + 0.5 * y * (1.0 - t * t) * c * (1.0 + 3.0 * 0.044715 * y * y)


def s5_post(ys, xp, g, d, wglu, bglu, bg=()):
    def body(ys_ref, x_ref, g_ref, d_ref, w_ref, b_ref, y_ref, z_ref, h_ref):
        x = x_ref[...]
        hn, _ = _rms(x, g_ref[...])
        y = ys_ref[...] + d_ref[...] * hn
        y_ref[...] = y
        yg = jax.nn.gelu(y).astype(bf16)
        for j in range(4):
            cv = slice(j * 256, (j + 1) * 256)
            cg = slice(1024 + j * 256, 1024 + (j + 1) * 256)
            val = _dot(yg, w_ref[j]) + b_ref[:, cv]
            gate = _dot(yg, w_ref[j + 4]) + b_ref[:, cg]
            z_ref[:, cv] = val
            z_ref[:, cg] = gate
            h_ref[:, cv] = x[:, cv] + val * jax.nn.sigmoid(gate)

    return _call(
        bg, body, name="s5_post", grid=(NT,),
        in_specs=[_tile(), _tile(), _full((1, D)), _full((1, D)), _full((8, D, 256)), _full((1, 2 * D))],
        out_specs=[_tile(), _tile(2 * D), _tile()],
        out_shape=[SDS((T, D), f32), SDS((T, 2 * D), f32), SDS((T, D), f32)],
        compiler_params=_cp(dimension_semantics=("arbitrary",)),
    )(ys, xp, g, d, wglu, bglu)


def s5_post_bwd(dh, y, z, wglu, bg=()):
    def body(dh_ref, y_ref, z_ref, w_ref, dy_ref, dw_ref, db_ref, acc):
        i = pl.program_id(0)

        @pl.when(i == 0)
        def _():
            acc[...] = jnp.zeros_like(acc)
            db_ref[...] = jnp.zeros_like(db_ref)

        dh_ = dh_ref[...]
        y = y_ref[...]
        yg = jax.nn.gelu(y).astype(bf16)
        dyg = jnp.zeros((TM, D), f32)
        for j in range(4):
            cv = slice(j * 256, (j + 1) * 256)
            cg = slice(1024 + j * 256, 1024 + (j + 1) * 256)
            val = z_ref[:, cv]
            sg = jax.nn.sigmoid(z_ref[:, cg])
            dval = dh_[:, cv] * sg
            dgate = dh_[:, cv] * val * sg * (1.0 - sg)
            db_ref[:, cv] += _colsum8(dval)
            db_ref[:, cg] += _colsum8(dgate)
            dvb = dval.astype(bf16)
            dgb = dgate.astype(bf16)
            acc[j] += _dot_tn(yg, dvb)
            acc[j + 4] += _dot_tn(yg, dgb)
            dyg = dyg + _dot_nt(dvb, w_ref[j]) + _dot_nt(dgb, w_ref[j + 4])
        dy_ref[...] = dyg * _gelu_grad(y)

        @pl.when(i == NT - 1)
        def _():
            dw_ref[...] = acc[...].astype(bf16)

    return _call(
        bg, body, name="s5_post_bwd", grid=(NT,),
        in_specs=[_tile(), _tile(), _tile(2 * D), _full((8, D, 256))],
        out_specs=[_tile(), _full((8, D, 256)), _full((8, 2 * D))],
        out_shape=[SDS((T, D), f32), SDS((8, D, 256), bf16), SDS((8, 2 * D), f32)],
        scratch_shapes=[pltpu.VMEM((8, D, 256), f32)],
        compiler_params=_cp(dimension_semantics=("arbitrary",)),
    )(dh, y, z, wglu)


def s5_pre_bwd(xp, g, du, dy, d, dh, bg=()):
    def body(x_ref, g_ref, du_ref, dy_ref, d_ref, dh_ref, dx_ref, dg_ref, dd_ref):
        i = pl.program_id(0)

        @pl.when(i == 0)
        def _():
            dg_ref[...] = jnp.zeros_like(dg_ref)
            dd_ref[...] = jnp.zeros_like(dd_ref)

        x = x_ref[...]
        g = g_ref[...]
        dy = dy_ref[...]
        hn, _ = _rms(x, g)
        dhn = du_ref[...] + d_ref[...] * dy
        dx, dgt = _rms_bwd(x, g, dhn)
        dx_ref[...] = dh_ref[...] + dx
        dg_ref[...] += _colsum8(dgt)
        dd_ref[...] += _colsum8(dy * hn)

    return _call(
        bg, body, name="s5_pre_bwd", grid=(NT,),
        in_specs=[_tile(), _full((1, D)), _tile(), _tile(), _full((1, D)), _tile()],
        out_specs=[_tile(), _full((8, D)), _full((8, D))],
        out_shape=[SDS((T, D), f32), SDS((8, D), f32), SDS((8, D), f32)],
        compiler_params=_cp(dimension_semantics=("arbitrary",)),
    )(xp, g, du, dy, d, dh)


TMF = 1024


def mlp_fwd(h, g, w_in, w_out, layer, bg=()):
    def body(h_ref, g_ref, wi_ref, wo_ref, hm_ref, r_ref, out_ref, acc):
        j = pl.program_id(1)

        @pl.when(j == 0)
        def _():
            hm, _ = _rms(h_ref[...], g_ref[...])
            hm_ref[...] = hm.astype(bf16)
            acc[...] = jnp.zeros_like(acc)

        a = jnp.maximum(_dot(hm_ref[...], wi_ref[...]), 0.0)
        r_ref[...] = a.astype(bf16)
        acc[...] += _dot((a * a).astype(bf16), wo_ref[...])

        @pl.when(j == NDEV - 1)
        def _():
            out_ref[...] = h_ref[...] + acc[...]

    return _call(
        bg, body, name=f"mlp_fwd{layer}", grid=(T // TMF, NDEV),
        in_specs=[pl.BlockSpec((TMF, D), lambda i, j: (i, 0)),
                  pl.BlockSpec((1, D), lambda i, j: (0, 0)),
                  pl.BlockSpec((None, D, D_FF_SHARD), lambda i, j: (j, 0, 0)),
                  pl.BlockSpec((None, D_FF_SHARD, D), lambda i, j: (j, 0, 0))],
        out_specs=[pl.BlockSpec((TMF, D), lambda i, j: (i, 0)), pl.BlockSpec((TMF, D_FF_SHARD), lambda i, j: (i, j)),
                   pl.BlockSpec((TMF, D), lambda i, j: (i, 0))],
        out_shape=[SDS((T, D), bf16), SDS((T, NDEV * D_FF_SHARD), bf16), SDS((T, D), f32)],
        scratch_shapes=[pltpu.VMEM((TMF, D), f32)],
        compiler_params=_cp(dimension_semantics=("arbitrary", "arbitrary")),
    )(h, g, w_in, w_out)


def mlp_bwd(h, hm, r, g, dout, dout_b, w_in, w_out, layer, bg=()):
    last = NDEV - 1

    def body(h_ref, hm_ref, r_ref, g_ref, do_ref, dob_ref, wi_ref, wo_ref, dh_ref, dwi_ref, dwo_ref, dg_ref,
             dhm, awi, awo):
        j = pl.program_id(0)
        i = pl.program_id(1)
        rows = pl.ds(pl.multiple_of(i * TM, TM), TM)

        @pl.when(i == 0)
        def _():
            awi[...] = jnp.zeros_like(awi)
            awo[...] = jnp.zeros_like(awo)

        dz = (_dot_nt(dob_ref[...], wo_ref[...]) * (2.0 * r_ref[...].astype(f32))).astype(bf16)
        rb = r_ref[...]
        awo[...] += _dot_tn(rb * rb, dob_ref[...])
        awi[...] += _dot_tn(hm_ref[...], dz)
        part = _dot_nt(dz, wi_ref[...])

        @pl.when(j == 0)
        def _():
            dhm[rows, :] = part

        @pl.when(j > 0)
        def _():
            dhm[rows, :] += part

        @pl.when(i == NT - 1)
        def _():
            dwi_ref[...] = awi[...].astype(bf16)
            dwo_ref[...] = awo[...].astype(bf16)

        @pl.when(j == last)
        def _():
            @pl.when(i == 0)
            def _():
                dg_ref[...] = jnp.zeros_like(dg_ref)
            dx, dgt = _rms_bwd(h_ref[...], g_ref[...], dhm[rows, :])
            dh_ref[...] = do_ref[...] + dx
            dg_ref[...] += _colsum8(dgt)

    late = lambda j, i: (jnp.where(j == last, i, 0), 0)
    return _call(
        bg, body, name=f"mlp_bwd{layer}", grid=(NDEV, NT),
        in_specs=[pl.BlockSpec((TM, D), late),
                  pl.BlockSpec((TM, D), lambda j, i: (i, 0)),
                  pl.BlockSpec((TM, D_FF_SHARD), lambda j, i: (i, j)),
                  pl.BlockSpec((1, D), lambda j, i: (0, 0)),
                  pl.BlockSpec((TM, D), late),
                  pl.BlockSpec((TM, D), lambda j, i: (i, 0)),
                  pl.BlockSpec((None, D, D_FF_SHARD), lambda j, i: (j, 0, 0)),
                  pl.BlockSpec((None, D_FF_SHARD, D), lambda j, i: (j, 0, 0))],
        out_specs=[pl.BlockSpec((TM, D), late),
                   pl.BlockSpec((None, D, D_FF_SHARD), lambda j, i: (j, 0, 0)),
                   pl.BlockSpec((None, D_FF_SHARD, D), lambda j, i: (j, 0, 0)),
                   pl.BlockSpec((8, D), lambda j, i: (0, 0))],
        out_shape=[SDS((T, D), f32), SDS((NDEV, D, D_FF_SHARD), bf16), SDS((NDEV, D_FF_SHARD, D), bf16),
                   SDS((8, D), f32)],
        scratch_shapes=[pltpu.VMEM((T, D), f32), pltpu.VMEM((D, D_FF_SHARD), f32), pltpu.VMEM((D_FF_SHARD, D), f32)],
        compiler_params=_cp(dimension_semantics=("arbitrary", "arbitrary")),
    )(h, hm, r, g, dout, dout_b, w_in, w_out)


def _spread4():
    r = lax.broadcasted_iota(jnp.int32, (256, D), 0)
    c = lax.broadcasted_iota(jnp.int32, (256, D), 1)
    return ((c // 256 == r // HEAD_DIM) & (c % HEAD_DIM == r % HEAD_DIM)).astype(bf16)


def attn_pre(h, g_kv, g_mix, wkv, bkv, spread, wq, bq):
    def body(h_ref, gkv_ref, gm_ref, wkv_ref, bkv_ref, sp_ref, wq_ref, bq_ref, kvn_ref, hn_ref, k_ref, v_ref, q_ref):
        h_ = h_ref[...]
        kvn = _rms(h_, gkv_ref[...])[0].astype(bf16)
        hn = _rms(h_, gm_ref[...])[0].astype(bf16)
        kvn_ref[...] = kvn
        hn_ref[...] = hn
        kv = (_dot(kvn, wkv_ref[...]) + bkv_ref[...]).astype(bf16)
        k_ref[...] = _dot(kv[:, :256], sp_ref[...]).astype(bf16)
        v_ref[...] = _dot(kv[:, 256:], sp_ref[...]).astype(bf16)
        q_ref[...] = (_dot(hn, wq_ref[...]) + bq_ref[...]).astype(bf16)

    return pl.pallas_call(
        body, name="attn_pre", grid=(NT,),
        in_specs=[_tile(), _full((1, D)), _full((1, D)), _full((D, 512)), _full((1, 512)), _full((256, D)),
                  _full((D, D)), _full((1, D))],
        out_specs=[_tile()] * 5,
        out_shape=[SDS((T, D), bf16)] * 5,
        compiler_params=_cp(dimension_semantics=("arbitrary",)),
    )(h, g_kv, g_mix, wkv, bkv, spread, wq, bq)


def _attn_specs():
    cur = pl.BlockSpec((TM, 256), lambda j, n: (n, j))
    prev = pl.BlockSpec((BLK, 256), lambda j, n: (jnp.maximum(n * (TM // BLK) - 1, 0), j))
    return cur, prev


def _head_mask(g):
    lane = lax.broadcasted_iota(jnp.int32, (1, 256), 1)
    return (lane >= g * HEAD_DIM) & (lane < (g + 1) * HEAD_DIM)


def _stack_heads(t):
    return jnp.concatenate([jnp.where(_head_mask(g), t, 0) for g in range(Q_PER_KV)], axis=0)


def _unstack_heads(t):
    out = jnp.where(_head_mask(0), t[0:BLK], 0.0)
    for g in range(1, Q_PER_KV):
        out = out + jnp.where(_head_mask(g), t[g * BLK:(g + 1) * BLK], 0.0)
    return out


def _attn_probs(qs, k2, sinks, first):
    rows = Q_PER_KV * BLK
    s = _dot_nt(qs, k2) * (1.0 / math.sqrt(HEAD_DIM))
    qi = jnp.bitwise_and(lax.broadcasted_iota(jnp.int32, (rows, 2 * BLK), 0), BLK - 1)
    kj = lax.broadcasted_iota(jnp.int32, (rows, 2 * BLK), 1)
    diff = qi + BLK - kj
    valid = (diff >= 0) & (diff < BLK) & (jnp.logical_not(first) | (kj >= BLK))
    s = jnp.where(valid, s, -jnp.inf)
    rb = lax.broadcasted_iota(jnp.int32, (rows, 1), 0)
    sink = jnp.where(rb < BLK, sinks[0], jnp.where(rb < 2 * BLK, sinks[1], jnp.where(rb < 3 * BLK, sinks[2], sinks[3])))
    m = jnp.maximum(jnp.max(s, axis=-1, keepdims=True), sink)
    p = jnp.exp(s - m)
    ps = jnp.exp(sink - m)
    denom = jnp.sum(p, axis=-1, keepdims=True) + ps
    return p / denom, ps / denom


def _window_blocks(b, n, kc_ref, kp_ref, vc_ref, vp_ref):
    if b == 0:
        return (jnp.concatenate([kp_ref[...], kc_ref[0:BLK, :]], axis=0),
                jnp.concatenate([vp_ref[...], vc_ref[0:BLK, :]], axis=0), n == 0)
    rows = pl.ds((b - 1) * BLK, 2 * BLK)
    return kc_ref[rows, :], vc_ref[rows, :], False


def attn_core_fwd(q, k4, v4, sinks, bg=()):
    nb = TM // BLK

    def body(sink_ref, q_ref, kc_ref, kp_ref, vc_ref, vp_ref, o_ref):
        j = pl.program_id(0)
        n = pl.program_id(1)
        sk = [sink_ref[j * Q_PER_KV + g] for g in range(Q_PER_KV)]
        for b in range(nb):
            qb = q_ref[b * BLK:(b + 1) * BLK, :]
            k2, v2, first = _window_blocks(b, n, kc_ref, kp_ref, vc_ref, vp_ref)
            a, _ = _attn_probs(_stack_heads(qb), k2, sk, first)
            o_ref[b * BLK:(b + 1) * BLK, :] = _unstack_heads(_dot(a.astype(bf16), v2)).astype(bf16)

    cur, prev = _attn_specs()
    return _call(
        bg, body, name="attn_core_fwd", grid=(N_KV, NT),
        in_specs=[pl.BlockSpec(memory_space=pltpu.SMEM), cur, cur, prev, cur, prev],
        out_specs=cur, out_shape=SDS((T, D), bf16),
        compiler_params=_cp(dimension_semantics=("arbitrary", "arbitrary")),
    )(sinks, q, k4, k4, v4, v4)


def attn_post(h, o, wo, bo):
    def body(h_ref, o_ref, w_ref, b_ref, out_ref):
        out_ref[...] = h_ref[...] + _dot(o_ref[...], w_ref[...]) + b_ref[...]

    return pl.pallas_call(
        body, name="attn_post", grid=(NT,), in_specs=[_tile(), _tile(), _full((D, D)), _full((1, D))],
        out_specs=_tile(), out_shape=SDS((T, D), f32), compiler_params=_cp(dimension_semantics=("arbitrary",)),
    )(h, o, wo, bo)


def attn_bwd_pre(dh, o, wo, bg=()):
    def body(dh_ref, o_ref, w_ref, do_ref, dw_ref, db_ref, acc):
        i = pl.program_id(0)

        @pl.when(i == 0)
        def _():
            acc[...] = jnp.zeros_like(acc)
            db_ref[...] = jnp.zeros_like(db_ref)

        dh_ = dh_ref[...]
        dhb = dh_.astype(bf16)
        do_ref[...] = _dot_nt(dhb, w_ref[...]).astype(bf16)
        acc[...] += _dot_tn(o_ref[...], dhb)
        db_ref[...] += _colsum8(dh_)

        @pl.when(i == NT - 1)
        def _():
            dw_ref[...] = acc[...].astype(bf16)

    return _call(
        bg, body, name="attn_bwd_pre", grid=(NT,), in_specs=[_tile(), _tile(), _full((D, D))],
        out_specs=[_tile(), _full((D, D)), _full((8, D))],
        out_shape=[SDS((T, D), bf16), SDS((D, D), bf16), SDS((8, D), f32)],
        scratch_shapes=[pltpu.VMEM((D, D), f32)],
        compiler_params=_cp(dimension_semantics=("arbitrary",)),
    )(dh, o, wo)


def attn_core_bwd(q, do, k4, v4, sinks, bg=()):
    nb = TM // BLK

    def body(sink_ref, q_ref, do_ref, kc_ref, kp_ref, vc_ref, vp_ref, dq_ref, dk_ref, dv_ref, ds_ref):
        j = pl.program_id(0)
        n = pl.program_id(1)

        @pl.when(n == 0)
        def _():
            dk_ref[...] = jnp.zeros_like(dk_ref)
            dv_ref[...] = jnp.zeros_like(dv_ref)
            ds_ref[...] = jnp.zeros_like(ds_ref)

        lane8 = lax.broadcasted_iota(jnp.int32, (8, 128), 1)
        row8 = lax.broadcasted_iota(jnp.int32, (8, 128), 0)
        sk = [sink_ref[j * Q_PER_KV + g] for g in range(Q_PER_KV)]
        for b in range(nb):
            qs = _stack_heads(q_ref[b * BLK:(b + 1) * BLK, :])
            dos = _stack_heads(do_ref[b * BLK:(b + 1) * BLK, :])
            k2, v2, first = _window_blocks(b, n, kc_ref, kp_ref, vc_ref, vp_ref)
            a, asink = _attn_probs(qs, k2, sk, first)
            dp = _dot_nt(dos, v2)
            dd = jnp.sum(a * dp, axis=-1, keepdims=True)
            dsc = (a * (dp - dd) * (1.0 / math.sqrt(HEAD_DIM))).astype(bf16)
            t = asink * dd
            for g in range(Q_PER_KV):
                dsink = -jnp.sum(t[g * BLK:(g + 1) * BLK], axis=0, keepdims=True)
                ds_ref[...] += jnp.where((lane8 == g) & (row8 == 0), jnp.broadcast_to(dsink, (8, 128)), 0.0)
            dq_ref[b * BLK:(b + 1) * BLK, :] = _unstack_heads(_dot(dsc, k2))
            dk2 = _dot_tn(dsc, qs)
            dv2 = _dot_tn(a.astype(bf16), dos)
            cur = pl.ds(pl.multiple_of(n * TM + b * BLK, BLK), BLK)
            dk_ref[cur, :] += dk2[BLK:, :]
            dv_ref[cur, :] += dv2[BLK:, :]
            if b == 0:
                @pl.when(n > 0)
                def _():
                    prv = pl.ds(pl.multiple_of(n * TM - BLK, BLK), BLK)
                    dk_ref[prv, :] += dk2[:BLK, :]
                    dv_ref[prv, :] += dv2[:BLK, :]
            else:
                prv = pl.ds(pl.multiple_of(n * TM + (b - 1) * BLK, BLK), BLK)
                dk_ref[prv, :] += dk2[:BLK, :]
                dv_ref[prv, :] += dv2[:BLK, :]

    cur, prev = _attn_specs()
    col = pl.BlockSpec((T, 256), lambda j, n: (0, j))
    return _call(
        bg, body, name="attn_core_bwd", grid=(N_KV, NT),
        in_specs=[pl.BlockSpec(memory_space=pltpu.SMEM), cur, cur, cur, prev, cur, prev],
        out_specs=[cur, col, col, pl.BlockSpec((None, 8, 128), lambda j, n: (j, 0, 0))],
        out_shape=[SDS((T, D), f32), SDS((T, D), f32), SDS((T, D), f32), SDS((N_KV, 8, 128), f32)],
        compiler_params=_cp(dimension_semantics=("arbitrary", "arbitrary")),
    )(sinks, q, do, k4, k4, v4, v4)


def attn_bwd_q(h, dh, dq, hn, g_mix, wq):
    def body(h_ref, dh_ref, dq_ref, hn_ref, gm_ref, wq_ref, out_ref, dwq_ref, dbq_ref, dgm_ref, aq):
        i = pl.program_id(0)

        @pl.when(i == 0)
        def _():
            aq[...] = jnp.zeros_like(aq)
            dbq_ref[...] = jnp.zeros_like(dbq_ref)
            dgm_ref[...] = jnp.zeros_like(dgm_ref)

        dq_ = dq_ref[...]
        dqb = dq_.astype(bf16)
        aq[...] += _dot_tn(hn_ref[...], dqb)
        dbq_ref[...] += _colsum8(dq_)
        dx, dg = _rms_bwd(h_ref[...], gm_ref[...], _dot_nt(dqb, wq_ref[...]))
        out_ref[...] = dh_ref[...] + dx
        dgm_ref[...] += _colsum8(dg)

        @pl.when(i == NT - 1)
        def _():
            dwq_ref[...] = aq[...].astype(bf16)

    vec = _full((8, D))
    mat = _full((D, D))
    return pl.pallas_call(
        body, name="attn_bwd_q", grid=(NT,),
        in_specs=[_tile()] * 4 + [_full((1, D)), mat],
        out_specs=[_tile(), mat, vec, vec],
        out_shape=[SDS((T, D), f32), SDS((D, D), bf16), SDS((8, D), f32), SDS((8, D), f32)],
        scratch_shapes=[pltpu.VMEM((D, D), f32)],
        compiler_params=_cp(dimension_semantics=("arbitrary",)),
    )(h, dh, dq, hn, g_mix, wq)


def attn_bwd_kv(h, dh, dk4, dv4, kvn, g_kv, wkv, spread):
    def body(h_ref, dh_ref, dk_ref, dv_ref, kvn_ref, gkv_ref, wkv_ref, sp_ref, out_ref, outb_ref, dw_ref, db_ref,
             dgkv_ref, acc):
        i = pl.program_id(0)

        @pl.when(i == 0)
        def _():
            for r in (acc, db_ref, dgkv_ref):
                r[...] = jnp.zeros_like(r)

        dkv = jnp.concatenate([_dot_nt(dk_ref[...].astype(bf16), sp_ref[...]),
                               _dot_nt(dv_ref[...].astype(bf16), sp_ref[...])], axis=1)
        dkvb = dkv.astype(bf16)
        acc[...] += _dot_tn(kvn_ref[...], dkvb)
        db_ref[...] += _colsum8(dkv)
        dx, dg = _rms_bwd(h_ref[...], gkv_ref[...], _dot_nt(dkvb, wkv_ref[...]))
        out = dh_ref[...] + dx
        out_ref[...] = out
        outb_ref[...] = out.astype(bf16)
        dgkv_ref[...] += _colsum8(dg)

        @pl.when(i == NT - 1)
        def _():
            dw_ref[...] = acc[...].astype(bf16)

    return pl.pallas_call(
        body, name="attn_bwd_kv", grid=(NT,),
        in_specs=[_tile()] * 5 + [_full((1, D)), _full((D, 512)), _full((256, D))],
        out_specs=[_tile(), _tile(), _full((D, 512)), _full((8, 512)), _full((8, D))],
        out_shape=[SDS((T, D), f32), SDS((T, D), bf16), SDS((D, 512), bf16), SDS((8, 512), f32), SDS((8, D), f32)],
        scratch_shapes=[pltpu.VMEM((D, 512), f32)],
        compiler_params=_cp(dimension_semantics=("arbitrary",)),
    )(h, dh, dk4, dv4, kvn, g_kv, wkv, spread)


def final_loss(h, g, target):
    def body(h_ref, g_ref, t_ref, loss_ref, dh_ref, dhb_ref, dg_ref):
        i = pl.program_id(0)

        @pl.when(i == 0)
        def _():
            loss_ref[...] = jnp.zeros_like(loss_ref)
            dg_ref[...] = jnp.zeros_like(dg_ref)

        h_ = h_ref[...]
        g_ = g_ref[...]
        y, _ = _rms(h_, g_)
        diff = y - t_ref[...]
        per_tok = jnp.mean(diff * diff, axis=-1, keepdims=True)
        tot = 0.5 * jnp.sum(per_tok, axis=0, keepdims=True)
        lane = lax.broadcasted_iota(jnp.int32, (8, 128), 1)
        row = lax.broadcasted_iota(jnp.int32, (8, 128), 0)
        loss_ref[...] += jnp.where((lane == 0) & (row == 0), jnp.broadcast_to(tot, (8, 128)), 0.0)
        dx, dgt = _rms_bwd(h_, g_, diff * (1.0 / D))
        dh_ref[...] = dx
        dhb_ref[...] = dx.astype(bf16)
        dg_ref[...] += _colsum8(dgt)

    return pl.pallas_call(
        body, name="final_loss", grid=(NT,), in_specs=[_tile(), _full((1, D)), _tile()],
        out_specs=[_full((8, 128)), _tile(), _tile(), _full((8, D))],
        out_shape=[SDS((8, 128), f32), SDS((T, D), f32), SDS((T, D), bf16), SDS((8, D), f32)],
        compiler_params=_cp(dimension_semantics=("arbitrary",)),
    )(h, g, target)


def _to_chunked(a):
    return a.reshape(S5_CH, S5_STEPS, a.shape[-1]).transpose(1, 0, 2).reshape(T, a.shape[-1])


def _from_chunked(a):
    return a.reshape(S5_STEPS, S5_CH, a.shape[-1]).transpose(1, 0, 2).reshape(T, a.shape[-1])


def _rep4(w):
    return jnp.broadcast_to(w.reshape(w.shape[0], N_KV, 1, HEAD_DIM), (w.shape[0], N_KV, Q_PER_KV, HEAD_DIM)).reshape(
        w.shape[0], N_KV * Q_PER_KV * HEAD_DIM)


def _fold4(w):
    return w.reshape(w.shape[0], N_KV, Q_PER_KV, HEAD_DIM).sum(axis=2).reshape(w.shape[0], N_KV * HEAD_DIM)


def fwd_bwd(x, target, p, shards, opt, core, chip):
    row = lambda v: v.reshape(1, -1)
    (lam, bm, cm), prep_vjp = jax.vjp(s5_discretize, p["s5_a_re"][0], p["s5_a_im"][0], p["s5_log_dt"][0],
                                      p["s5_b_re"][0], p["s5_b_im"][0], p["s5_c_re"][0], p["s5_c_im"][0])
    bmb, cmb = bm.astype(bf16), cm.astype(bf16)
    lam = jnp.concatenate([lam, lam * jnp.array([1.0, -1.0], f32).reshape(1, 2, 1, 1)], axis=1)
    g_mix0, g_mix1 = row(p["norm_mix"][0]), row(p["norm_mix"][1])
    g_mlp0, g_mlp1 = row(p["norm_mlp"][0]), row(p["norm_mlp"][1])
    g_kv, g_fin = row(p["norm_kv"]), row(p["norm_final"])
    bq, bo = p["b_q"], p["b_o"]
    bkv = row(p["b_kv"])
    spread = _spread4()
    sinks = p["sinks"].reshape(16)

    def reduce_pairs(names, bg):
        return [add_pairs(g, r, core, f"add_pairs_{n}") for n, g, r in zip(names, bg.arrs, bg.result)]

    wglu, gvec, win0, wout0 = sc_gather(
        [shards["s5_w_glu"], shards["vecs"], shards["w_in0"], shards["w_out0"]], 3, "sc_gather_layer0")
    wkv, wq, wo, win1 = sc_gather([shards["w_kv"], shards["w_q"], shards["w_o"], shards["w_in1"]], 4, "sc_gather_attn")
    wout1, = sc_gather([shards["w_out1"]], 5, "sc_gather_w_out1")
    xp = _to_chunked(x)
    hn0 = s5_pre(xp, g_mix0)
    ys = s5_core_fwd(hn0, bmb, lam, cmb)
    d_skip = gvec[:, 0, :128].reshape(1, D)
    bglu = gvec[:, 0, 128:].reshape(1, 2 * D)
    y, z, h1 = s5_post(ys, xp, g_mix0, d_skip, wglu, bglu)
    hm0, r0, h2p = mlp_fwd(h1, g_mlp0, win0, wout0, 0)
    wkv, wq, wo = wkv.reshape(D, 512), wq.reshape(D, D), wo.reshape(D, D)
    h2 = _from_chunked(h2p)
    kvn, hn1, k4, v4, q = attn_pre(h2, g_kv, g_mix1, wkv, bkv, spread, wq, bq)
    o = attn_core_fwd(q, k4, v4, sinks)
    h3 = attn_post(h2, o, wo, bo)
    hm1, r1, h4 = mlp_fwd(h3, g_mlp1, win1, wout1, 1)
    loss, dh4, dh4b, dg_fin = final_loss(h4, g_fin, target)

    def pair_sums(names, grads, cid, before):
        r1 = sc_comm(BgPair(grads), cid, "sc_pair_" + names[0])
        parts = [add_pairs(g, r, core, f"add_pairs_{n}") for n, g, r in zip(names, grads, r1)]
        before, parts = lax.optimization_barrier((before, parts))
        return before, parts

    def across_chips(names, parts, cid):
        return list(zip(parts, sc_comm(BgChips(parts), cid, "sc_chips_" + names[0])))

    dh3, dwin1, dwout1, dg_mlp1 = mlp_bwd(h3, hm1, r1, g_mlp1, dh4, dh4b, win1, wout1, 1)
    do, dwo, dbo = attn_bwd_pre(dh3, o, wo)
    do, parts = pair_sums(["w_in1", "w_out1"], [dwin1, dwout1], 6, do)
    rs_in1, rs_out1 = across_chips(["w_in1", "w_out1"], parts, 7)
    dq, dk4, dv4, dsink = attn_core_bwd(q, do, k4, v4, sinks)
    dh2, dwq, dbq, dg_mix1 = attn_bwd_q(h2, dh3, dq, hn1, g_mix1, wq)
    dh2, dh2b, dwkv, dbkv, dg_kv = attn_bwd_kv(h2, dh2, dk4, dv4, kvn, g_kv, wkv, spread)
    dh2p, dh2pb = _to_chunked(dh2), _to_chunked(dh2b)
    names = ["w_kv", "w_q", "w_o"]
    dh2p, parts = pair_sums(names, [dwkv.reshape(NDEV, 128, 512), dwq.reshape(NDEV, 128, D),
                                    dwo.reshape(NDEV, 128, D)], 8, dh2p)
    rs_attn = across_chips(names, parts, 9)
    dh1, dwin0, dwout0, dg_mlp0 = mlp_bwd(h1, hm0, r0, g_mlp0, dh2p, dh2pb, win0, wout0, 0)
    dy, dwglu, dbglu = s5_post_bwd(dh1, y, z, wglu)
    dy, parts = pair_sums(["w_in0", "w_out0"], [dwin0, dwout0], 10, dy)
    rs_in0, rs_out0 = across_chips(["w_in0", "w_out0"], parts, 11)
    du, dbm, dcmt, dlam = s5_core_bwd(hn0, dy, bmb, lam, cmb)
    du, parts = pair_sums(["s5_w_glu"], [dwglu], 12, du)
    rs_glu, = across_chips(["s5_w_glu"], parts, 13)
    dxp, dg_mix0, dd = s5_pre_bwd(xp, g_mix0, du, dy, d_skip, dh1)

    big = {n: adam_big(*opt[n], *rs, chip, f"adam_{n}")
           for n, rs in zip(("w_kv", "w_q", "w_o", "s5_w_glu"), rs_attn + [rs_glu])}
    for n, rs1, rs0 in (("w_mlp_in", rs_in1, rs_in0), ("w_mlp_out", rs_out1, rs_out0)):
        res = adam_big(*opt[n], *rs1, chip, f"adam_{n}1", layer=1)
        big[n] = adam_big(*opt[n], *rs0, chip, f"adam_{n}0", layer=0, prev=res)
    grad_x = _from_chunked(dxp)
    da_re, da_im, dlog_dt, db_re, db_im, dc_re, dc_im = prep_vjp((dlam, dbm, dcmt.transpose(0, 2, 1)))

    def lanes(v_):
        v_ = v_.reshape(1, -1)
        return jnp.pad(v_, ((0, 0), (0, D - v_.shape[1])))

    small = jnp.concatenate([
        dg_mix0[0:1], dg_mix1[0:1], dg_mlp0[0:1], dg_mlp1[0:1], dg_kv[0:1], dg_fin[0:1], dd[0:1], dbq[0:1], dbo[0:1],
        dbglu[0:1].reshape(2, D), lanes(dbkv[0:1]),
        lanes(dsink[:, 0, :Q_PER_KV]), lanes(dlog_dt), lanes(loss[0:1, 0:1]), jnp.zeros((1, D), f32),
        da_re.reshape(4, D), da_im.reshape(4, D),
        db_re.transpose(0, 2, 1).reshape(64, D), db_im.transpose(0, 2, 1).reshape(64, D),
        dc_re.reshape(64, D), dc_im.reshape(64, D)], axis=0)
    return loss, grad_x, small, big


_ANY = pl.BlockSpec(memory_space=pl.ANY)


def _pos():
    return lax.axis_index("x"), lax.axis_index("y"), lax.axis_index("c")


def _other_chips(x, y):
    return [(1 - x, y), (x, 1 - y), (1 - x, 1 - y)]


def all_gather(arrs):
    n = len(arrs)

    def body(*refs):
        ins, outs = refs[:n], refs[n:2 * n]
        send_sems, recv_sems, local_sems = refs[2 * n:]
        x, y, c = _pos()
        me, sib = (x, y, c), (x, y, 1 - c)
        chips = _other_chips(x, y)

        def copy(a, k, block, to, src=None):
            dst = outs[a].at[4 * block[0] + 2 * block[1] + block[2]]
            return pltpu.make_async_remote_copy(
                src_ref=dst if src is None else src, dst_ref=dst, send_sem=send_sems.at[a, k],
                recv_sem=recv_sems.at[a, k], device_id=to, device_id_type=MESH)

        mine = [pltpu.make_async_copy(ins[a], outs[a].at[4 * x + 2 * y + c], local_sems.at[a]) for a in range(n)]
        for cp in mine:
            cp.start()
        first = []
        for a in range(n):
            first.append(copy(a, 0, me, sib, src=ins[a]))
            first += [copy(a, 1 + j, me, (*chip, c), src=ins[a]) for j, chip in enumerate(chips)]
        for cp in first:
            cp.start()
        passed = []
        for j, chip in enumerate(chips):
            for a in range(n):
                copy(a, 1 + j, (*chip, c), me).wait_recv()
                cp = copy(a, 4 + j, (*chip, c), sib)
                cp.start()
                passed.append(cp)
        for a in range(n):
            copy(a, 0, sib, me).wait_recv()
            for j, chip in enumerate(chips):
                copy(a, 4 + j, (*chip, 1 - c), me).wait_recv()
        for cp in first + passed:
            cp.wait_send()
        for cp in mine:
            cp.wait()

    return pl.pallas_call(
        body, name="all_gather", in_specs=[_ANY] * n, out_specs=[_ANY] * n,
        out_shape=[SDS((NDEV,) + a.shape, a.dtype) for a in arrs],
        scratch_shapes=[pltpu.SemaphoreType.DMA((n, 7)), pltpu.SemaphoreType.DMA((n, 7)),
                        pltpu.SemaphoreType.DMA((n,))],
    )(*arrs)


def rs_pair(grads):
    n = len(grads)

    def body(*refs):
        ins, outs = refs[:n], refs[n:2 * n]
        send_sems, recv_sems = refs[2 * n:]
        x, y, c = _pos()
        cps = []
        for a in range(n):
            for k in range(4):
                cps.append(pltpu.make_async_remote_copy(
                    src_ref=ins[a].at[2 * k + 1 - c], dst_ref=outs[a].at[k], send_sem=send_sems.at[a, k],
                    recv_sem=recv_sems.at[a, k], device_id=(x, y, 1 - c), device_id_type=MESH))
        for cp in cps:
            cp.start()
        for cp in cps:
            cp.wait_recv()
        for cp in cps:
            cp.wait_send()

    return pl.pallas_call(
        body, name="rs_pair", in_specs=[_ANY] * n, out_specs=[_ANY] * n,
        out_shape=[SDS((4,) + g.shape[1:], g.dtype) for g in grads],
        scratch_shapes=[pltpu.SemaphoreType.DMA((n, 4)), pltpu.SemaphoreType.DMA((n, 4))],
    )(*grads)


def rs_chips(parts):
    n = len(parts)

    def body(*refs):
        ins, outs = refs[:n], refs[n:2 * n]
        send_sems, recv_sems = refs[2 * n:]
        x, y, c = _pos()
        cps = []
        for a in range(n):
            for r, (px, py) in enumerate(_other_chips(x, y)):
                cps.append(pltpu.make_async_remote_copy(
                    src_ref=ins[a].at[2 * px + py], dst_ref=outs[a].at[r], send_sem=send_sems.at[a, r],
                    recv_sem=recv_sems.at[a, r], device_id=(px, py, c), device_id_type=MESH))
        for cp in cps:
            cp.start()
        for cp in cps:
            cp.wait_recv()
        for cp in cps:
            cp.wait_send()

    return pl.pallas_call(
        body, name="rs_chips", in_specs=[_ANY] * n, out_specs=[_ANY] * n,
        out_shape=[SDS((3,) + g.shape[1:], g.dtype) for g in parts],
        scratch_shapes=[pltpu.SemaphoreType.DMA((n, 3)), pltpu.SemaphoreType.DMA((n, 3))],
    )(*parts)


def _row_tile(r, c):
    return min(r, max(8, (512 * 1024) // c))


def add_pairs(g, r1, core, name):
    _, R, C = g.shape
    tr = _row_tile(R, C)

    def body(core_ref, g_ref, r_ref, o_ref):
        o_ref[...] = (g_ref[...].astype(f32) + r_ref[...].astype(f32)).astype(bf16)

    return pl.pallas_call(
        body, name=name, out_shape=SDS((4, R, C), bf16),
        grid_spec=pltpu.PrefetchScalarGridSpec(
            num_scalar_prefetch=1, grid=(4, R // tr),
            in_specs=[pl.BlockSpec((None, tr, C), lambda k, i, core: (2 * k + core[0], i, 0)),
                      pl.BlockSpec((None, tr, C), lambda k, i, core: (k, i, 0))],
            out_specs=pl.BlockSpec((None, tr, C), lambda k, i, core: (k, i, 0))),
        compiler_params=_cp(dimension_semantics=("arbitrary", "arbitrary")),
    )(core, g, r1)


def _adamw(w, g, m, v):
    m = ADAM_B1 * m + (1.0 - ADAM_B1) * g
    v = ADAM_B2 * v + (1.0 - ADAM_B2) * (g * g)
    m_hat = m / (1.0 - ADAM_B1 ** ADAM_STEP)
    v_hat = v / (1.0 - ADAM_B2 ** ADAM_STEP)
    delta = -ADAM_LR * (m_hat / (jnp.sqrt(v_hat) + ADAM_EPS) + ADAM_WD * w)
    return delta, m, v


def adam_big(w, m, v, part, r2, chip, name, layer=0, prev=None):
    L, R, C = w.shape
    tr = _row_tile(R, C)

    def body(chip_ref, w_ref, m_ref, v_ref, p_ref, r_ref, *rest):
        g_out, d_out, m_out, v_out = rest[-4:]
        g = p_ref[...].astype(f32) + r_ref[0].astype(f32) + r_ref[1].astype(f32) + r_ref[2].astype(f32)
        d, m_, v_ = _adamw(w_ref[...], g, m_ref[...], v_ref[...])
        g_out[...] = g
        d_out[...] = d
        m_out[...] = m_
        v_out[...] = v_

    blk = pl.BlockSpec((None, tr, C), lambda i, chip: (layer, i, 0))
    extra = [] if prev is None else list(prev)
    return pl.pallas_call(
        body, name=name, out_shape=[SDS((L, R, C), f32)] * 4,
        grid_spec=pltpu.PrefetchScalarGridSpec(
            num_scalar_prefetch=1, grid=(R // tr,),
            in_specs=[blk, blk, blk,
                      pl.BlockSpec((None, tr, C), lambda i, chip: (chip[0], i, 0)),
                      pl.BlockSpec((3, tr, C), lambda i, chip: (0, i, 0))] + [_ANY] * len(extra),
            out_specs=[blk] * 4),
        input_output_aliases={6 + k: k for k in range(len(extra))},
        compiler_params=_cp(dimension_semantics=("arbitrary",)),
    )(chip, w, m, v, part, r2, *extra)


def allreduce_small(buf, chips=None):
    shp = buf.shape
    half = (shp[0] // 16) * 8
    parts = (pl.ds(0, half), pl.ds(half, shp[0] - half))
    n_c = 0 if chips is None else len(chips.arrs)

    def body(in_ref, *refs):
        c_in, out_ref, c_out = refs[:n_c], refs[n_c], refs[n_c + 1:2 * n_c + 1]
        acc1, acc2, r0, r1, r2, send_sems, recv_sems = refs[2 * n_c + 1:2 * n_c + 8]
        c_sems = refs[2 * n_c + 8:]
        if chips is not None:
            chips.start(c_in, c_out, c_sems)
        x, y, c = _pos()
        across = [(1 - x, y, c), (x, 1 - y, c)]

        def exchange(src, rcv, dst, copies):
            cps = [pltpu.make_async_remote_copy(
                src_ref=src.at[rows], dst_ref=rcv.at[rows], send_sem=send_sems.at[k], recv_sem=recv_sems.at[k],
                device_id=peer, device_id_type=MESH) for k, rows, peer in copies]
            for cp in cps:
                cp.start()
            for cp in cps:
                cp.wait()
            dst[...] = src[...] + rcv[...]

        exchange(in_ref, r0, acc1, [(0, pl.ds(0, shp[0]), (x, y, 1 - c))])
        exchange(acc1, r1, acc2, [(1, parts[0], across[0]), (2, parts[1], across[1])])
        exchange(acc2, r2, out_ref, [(3, parts[0], across[1]), (4, parts[1], across[0])])
        if chips is not None:
            chips.finish(c_in, c_out, c_sems)

    vm = pl.BlockSpec(memory_space=pltpu.VMEM)
    res = pl.pallas_call(
        body, name="allreduce_small", in_specs=[vm] + [_ANY] * n_c, out_specs=[vm] + [_ANY] * n_c,
        out_shape=[SDS(shp, f32)] + ([] if chips is None else chips.out_shape),
        scratch_shapes=[pltpu.VMEM(shp, f32)] * 5 + [pltpu.SemaphoreType.DMA((5,)), pltpu.SemaphoreType.DMA((5,))]
        + ([] if chips is None else chips.scratch),
    )(buf, *([] if chips is None else chips.arrs))
    if chips is not None:
        chips.result = list(res[1:])
    return res[0]


SMALL_ROWS = {'norm_mix': (0, 2, D), 'norm_mlp': (2, 2, D), 'norm_kv': (4, 1, D), 'norm_final': (5, 1, D),
              's5_d': (6, 1, D), 'b_q': (7, 1, D), 'b_o': (8, 1, D), 's5_b_glu': (9, 2, D), 'b_kv': (11, 1, 512),
              'sinks': (12, 1, 16), 's5_log_dt': (13, 1, 64), 's5_a_re': (16, 4, D), 's5_a_im': (20, 4, D),
              's5_b_re': (24, 64, D), 's5_b_im': (88, 64, D), 's5_c_re': (152, 64, D), 's5_c_im': (216, 64, D)}
LOSS_ROW = 14
ROW_PARAMS = ['norm_mix', 'norm_mlp', 'norm_kv', 'norm_final', 'b_q', 'b_o', 'b_kv', 'sinks', 's5_log_dt']
SHARD_PARAMS = ['s5_d', 's5_b_glu']
S5_PARAMS = ['s5_a_re', 's5_a_im', 's5_b_re', 's5_b_im', 's5_c_re', 's5_c_im']


def adam_small(dev, gsum, s5_grads, w, m, v):
    names = ROW_PARAMS + SHARD_PARAMS + S5_PARAMS
    n_g = len(ROW_PARAMS) + len(SHARD_PARAMS)

    def body(dev_ref, gs_ref, *refs):
        pos = [0]

        def take(k):
            r = refs[pos[0]:pos[0] + k]
            pos[0] += k
            return r

        g5 = take(len(S5_PARAMS))
        wr, mr, vr = take(len(names)), take(len(names)), take(len(names))
        g_out = take(n_g)
        d_out, m_out, v_out = take(len(names)), take(len(names)), take(len(names))
        dv = dev_ref[0]
        for i, n in enumerate(names):
            if n in S5_PARAMS:
                g = g5[S5_PARAMS.index(n)][...]
            elif n in SHARD_PARAMS:
                r0, _, _ = SMALL_ROWS[n]
                ln = wr[i].shape[1]
                g = jnp.zeros((1, ln), f32)
                for k in range(NDEV):
                    off = k * ln
                    piece = gs_ref[r0 + off // D:r0 + off // D + 1, off % D:off % D + ln]
                    g = g + jnp.where(dv == k, piece, 0.0)
                g_out[i][...] = g
            else:
                r0, nr, nl = SMALL_ROWS[n]
                g = gs_ref[r0:r0 + nr, 0:nl]
                g_out[i][...] = g
            d, m_, v_ = _adamw(wr[i][...], g, mr[i][...], vr[i][...])
            d_out[i][...] = d
            m_out[i][...] = m_
            v_out[i][...] = v_

    vm = pl.BlockSpec(memory_space=pltpu.VMEM)
    ins = [s5_grads[n] for n in S5_PARAMS] + [d[n] for d in (w, m, v) for n in names]
    shapes = [SDS(w[n].shape, f32) for n in names]
    res = pl.pallas_call(
        body, name="adam_small", in_specs=[pl.BlockSpec(memory_space=pltpu.SMEM)] + [vm] * (1 + len(ins)),
        out_specs=[vm] * (n_g + 3 * len(names)), out_shape=shapes[:n_g] + shapes * 3,
        compiler_params=_cp(),
    )(dev, gsum, *ins)
    g_o = dict(zip(names[:n_g], res[:n_g]))
    rest = res[n_g:]
    k = len(names)
    return g_o, dict(zip(names, rest[:k])), dict(zip(names, rest[k:2 * k])), dict(zip(names, rest[2 * k:]))


WEIGHTS = ['norm_mix', 'norm_mlp', 'norm_kv', 'norm_final', 's5_a_re', 's5_a_im', 's5_log_dt', 's5_b_re', 's5_b_im',
           's5_c_re', 's5_c_im', 's5_d', 's5_w_glu', 's5_b_glu', 'w_kv', 'b_kv', 'w_q', 'b_q', 'sinks', 'w_o', 'b_o',
           'w_mlp_in', 'w_mlp_out']
BIG = ['s5_w_glu', 'w_kv', 'w_q', 'w_o', 'w_mlp_in', 'w_mlp_out']
BIG_2D = {'s5_w_glu': (D, 256), 'w_kv': (128, 512), 'w_q': (128, D), 'w_o': (128, D), 'w_mlp_in': (2 * D, 512),
          'w_mlp_out': (2 * 512, D)}
SHARDED_SMALL = {'s5_d': D, 's5_b_glu': 2 * D}
SMALL = [n for n in WEIGHTS if n not in BIG]
SMALL_SIZE = {'norm_mix': 2 * D, 'norm_mlp': 2 * D, 'norm_kv': D, 'norm_final': D, 's5_a_re': 4096, 's5_a_im': 4096,
              's5_log_dt': 64, 's5_b_re': 65536, 's5_b_im': 65536, 's5_c_re': 65536, 's5_c_im': 65536, 's5_d': D,
              's5_b_glu': 2 * D, 'b_kv': 512, 'b_q': D, 'sinks': 16, 'b_o': D}


def _pack(vals):
    parts = []
    for n in SMALL:
        v = vals[n].reshape(-1).astype(f32)
        parts.append(jnp.pad(v, (0, (-v.shape[0]) % 128)))
    flat = jnp.concatenate(parts)
    flat = jnp.pad(flat, (0, (-flat.shape[0]) % 1024))
    return flat.reshape(-1, 128)


def _unpack(buf):
    flat = buf.reshape(-1)
    out, off = {}, 0
    for n in SMALL:
        sz = SMALL_SIZE[n]
        out[n] = flat[off:off + sz]
        off += sz + (-sz) % 128
    return out


def kernel(x, norm_mix, norm_mlp, norm_kv, norm_final, s5_a_re, s5_a_im, s5_log_dt, s5_b_re, s5_b_im, s5_c_re, s5_c_im, s5_d, s5_w_glu, s5_b_glu, w_kv, b_kv, w_q, b_q, sinks, w_o, b_o, w_mlp_in, w_mlp_out, loss_target, m_norm_mix, m_norm_mlp, m_norm_kv, m_norm_final, m_s5_a_re, m_s5_a_im, m_s5_log_dt, m_s5_b_re, m_s5_b_im, m_s5_c_re, m_s5_c_im, m_s5_d, m_s5_w_glu, m_s5_b_glu, m_w_kv, m_b_kv, m_w_q, m_b_q, m_sinks, m_w_o, m_b_o, m_w_mlp_in, m_w_mlp_out, v_norm_mix, v_norm_mlp, v_norm_kv, v_norm_final, v_s5_a_re, v_s5_a_im, v_s5_log_dt, v_s5_b_re, v_s5_b_im, v_s5_c_re, v_s5_c_im, v_s5_d, v_s5_w_glu, v_s5_b_glu, v_w_kv, v_b_kv, v_w_q, v_b_q, v_sinks, v_w_o, v_b_o, v_w_mlp_in, v_w_mlp_out):
    w = dict(norm_mix=norm_mix, norm_mlp=norm_mlp, norm_kv=norm_kv, norm_final=norm_final, s5_a_re=s5_a_re,
             s5_a_im=s5_a_im, s5_log_dt=s5_log_dt, s5_b_re=s5_b_re, s5_b_im=s5_b_im, s5_c_re=s5_c_re, s5_c_im=s5_c_im,
             s5_d=s5_d, s5_w_glu=s5_w_glu, s5_b_glu=s5_b_glu, w_kv=w_kv, b_kv=b_kv, w_q=w_q, b_q=b_q, sinks=sinks,
             w_o=w_o, b_o=b_o, w_mlp_in=w_mlp_in, w_mlp_out=w_mlp_out)
    m = dict(norm_mix=m_norm_mix, norm_mlp=m_norm_mlp, norm_kv=m_norm_kv, norm_final=m_norm_final, s5_a_re=m_s5_a_re,
             s5_a_im=m_s5_a_im, s5_log_dt=m_s5_log_dt, s5_b_re=m_s5_b_re, s5_b_im=m_s5_b_im, s5_c_re=m_s5_c_re,
             s5_c_im=m_s5_c_im, s5_d=m_s5_d, s5_w_glu=m_s5_w_glu, s5_b_glu=m_s5_b_glu, w_kv=m_w_kv, b_kv=m_b_kv,
             w_q=m_w_q, b_q=m_b_q, sinks=m_sinks, w_o=m_w_o, b_o=m_b_o, w_mlp_in=m_w_mlp_in, w_mlp_out=m_w_mlp_out)
    v = dict(norm_mix=v_norm_mix, norm_mlp=v_norm_mlp, norm_kv=v_norm_kv, norm_final=v_norm_final, s5_a_re=v_s5_a_re,
             s5_a_im=v_s5_a_im, s5_log_dt=v_s5_log_dt, s5_b_re=v_s5_b_re, s5_b_im=v_s5_b_im, s5_c_re=v_s5_c_re,
             s5_c_im=v_s5_c_im, s5_d=v_s5_d, s5_w_glu=v_s5_w_glu, s5_b_glu=v_s5_b_glu, w_kv=v_w_kv, b_kv=v_b_kv,
             w_q=v_w_q, b_q=v_b_q, sinks=v_sinks, w_o=v_w_o, b_o=v_b_o, w_mlp_in=v_w_mlp_in, w_mlp_out=v_w_mlp_out)
    xi, yi, ci = _pos()
    dev = 4 * xi + 2 * yi + ci
    core = ci.reshape(1).astype(jnp.int32)
    chip = (2 * xi + yi).reshape(1).astype(jnp.int32)

    shards = {
        "s5_w_glu": s5_w_glu[0].astype(bf16), "w_kv": w_kv.astype(bf16), "w_q": w_q[0].astype(bf16),
        "w_o": w_o[0].astype(bf16), "w_in0": w_mlp_in[0].astype(bf16), "w_in1": w_mlp_in[1].astype(bf16),
        "w_out0": w_mlp_out[0].astype(bf16), "w_out1": w_mlp_out[1].astype(bf16),
        "vecs": jnp.broadcast_to(jnp.concatenate([s5_d, s5_b_glu], axis=1), (8, 384)),
    }
    as3d = lambda a, n: a if a.ndim == 3 and a.shape[0] == 2 else a.reshape((1,) + BIG_2D[n])
    opt = {n: (as3d(w[n], n), as3d(m[n], n), as3d(v[n], n)) for n in BIG}
    _, grad_x, grads, big = fwd_bwd(x[0], loss_target[0], {n: w[n] for n in SMALL}, shards, opt, core, chip)

    gsum = allreduce_small(grads)

    out_g, out_d, out_m, out_v = {}, {}, {}, {}
    for n in BIG:
        out_g[n], out_d[n], out_m[n], out_v[n] = [r.reshape(w[n].shape) for r in big[n]]

    loss = gsum[LOSS_ROW, 0]
    swapped = ("s5_b_re", "s5_b_im")
    swap = lambda a: a.transpose(0, 1, 3, 2)

    def kernel_side(d):
        d = {n: (d[n].reshape(1, -1) if d[n].ndim == 1 else d[n]) for n in SMALL}
        d.update({n: swap(d[n]) for n in swapped})
        return d

    s5_g = {}
    for n in S5_PARAMS:
        r0, nr, _ = SMALL_ROWS[n]
        s5_g[n] = gsum[r0:r0 + nr].reshape((1, 64, 16, 64) if n in swapped else w[n].shape)
        out_g[n] = s5_g[n]
    g_s, d_s, m_s, v_s = adam_small(dev.reshape(1).astype(jnp.int32), gsum, s5_g, kernel_side(w), kernel_side(m),
                                    kernel_side(v))
    for src, dst in ((g_s, out_g), (d_s, out_d), (m_s, out_m), (v_s, out_v)):
        dst.update(src)
    for dst in (out_g, out_d, out_m, out_v):
        for n in SMALL:
            dst[n] = (swap(dst[n]) if n in swapped else dst[n]).reshape(w[n].shape)

    return (loss, grad_x[None], *[out_g[n] for n in WEIGHTS], *[out_d[n] for n in WEIGHTS],
            *[out_m[n] for n in WEIGHTS], *[out_v[n] for n in WEIGHTS])
```

```python
import functools
import math

import jax
import jax.numpy as jnp
from jax import lax
from jax.experimental import pallas as pl
from jax.experimental.pallas import tpu as pltpu
from jax.experimental.pallas import tpu_sc as plsc

f32 = jnp.float32
bf16 = jnp.bfloat16
SDS = jax.ShapeDtypeStruct

T = 2048
D = 1024
NDEV = 8
NORM_EPS = 1e-5
S5_G, S5_C, S5_P = 64, 16, 64
S5_SUB = 8
S5_CH = 8
S5_STEPS = T // S5_CH
DT_MIN_LAMBDA = -1e-4
HEAD_DIM = 64
N_KV = 4
Q_PER_KV = 4
BLK = 128
D_FF_SHARD = 512
ADAM_LR, ADAM_B1, ADAM_B2, ADAM_EPS, ADAM_WD, ADAM_STEP = 0.001, 0.9, 0.999, 1e-08, 0.01, 10
VMEM_LIMIT = 56 * 1024 * 1024
MESH = pl.DeviceIdType.MESH


def _cp(**kw):
    return pltpu.CompilerParams(vmem_limit_bytes=VMEM_LIMIT, **kw)


def _dot(a, b):
    return jnp.dot(a, b, preferred_element_type=f32)


def _dot_nt(a, b):
    return lax.dot_general(a, b, (((1,), (1,)), ((), ())), preferred_element_type=f32)


def _dot_tn(a, b):
    return lax.dot_general(a, b, (((0,), (0,)), ((), ())), preferred_element_type=f32)


def _rms(x, g):
    r = lax.rsqrt(jnp.mean(x * x, axis=-1, keepdims=True) + NORM_EPS)
    return x * r * g, r


def _rms_bwd(x, g, dy):
    r = lax.rsqrt(jnp.mean(x * x, axis=-1, keepdims=True) + NORM_EPS)
    u = dy * g
    dx = r * u - (r * r * r) * x * jnp.mean(u * x, axis=-1, keepdims=True)
    return dx, dy * x * r


def _colsum8(v):
    s = jnp.sum(v, axis=0, keepdims=True)
    row = lax.broadcasted_iota(jnp.int32, (8, v.shape[1]), 0)
    return jnp.where(row == 0, jnp.broadcast_to(s, (8, v.shape[1])), 0.0)


def _full(shape):
    nd = len(shape)
    return pl.BlockSpec(shape, lambda *_: (0,) * nd, pipeline_mode=pl.Buffered(1))


_ANY = pl.BlockSpec(memory_space=pl.ANY)


def _pos():
    return lax.axis_index("x"), lax.axis_index("y"), lax.axis_index("c")


def _other_chips(x, y):
    return [(1 - x, y), (x, 1 - y), (1 - x, 1 - y)]


class BgGather:
    SIB, XN, YN, FWD_Y, FWD_X, SIB_X, SIB_Y, SIB_D = range(8)

    def __init__(self, arrs, mids=(0.5, 0.75)):
        n = len(arrs)
        self.arrs = list(arrs)
        self.out_shape = [SDS((NDEV,) + a.shape, a.dtype) for a in arrs]
        self.scratch = [pltpu.SemaphoreType.DMA((n, 8)), pltpu.SemaphoreType.DMA((n, 8)),
                        pltpu.SemaphoreType.DMA((n,))]
        self.mids = mids
        self.result = None

    @staticmethod
    def peers(x, y, c):
        return [(x, y, 1 - c), (1 - x, y, c), (x, 1 - y, c)]

    def mid_steps(self, nsteps):
        at = lambda f: min(nsteps - 1, max(0, int(f * nsteps) - 1))
        return [(at(self.mids[0]), self.mid), (max(at(self.mids[0]), at(self.mids[1])), self.mid2)]

    def _halves(self, a):
        rows = self.arrs[a].shape[0]
        cut = rows // 2 if rows >= 32 else rows
        return (0, cut), (cut, rows - cut)

    def _copy(self, ins, outs, sems, a, k, block, to, own=False, part=None):
        slot = 4 * block[0] + 2 * block[1] + block[2]
        rows = pl.ds(0, self.arrs[a].shape[0]) if part is None else pl.ds(*self._halves(a)[part])
        dst = outs[a].at[slot, rows]
        return pltpu.make_async_remote_copy(
            src_ref=ins[a].at[rows] if own else dst, dst_ref=dst, send_sem=sems[0].at[a, k],
            recv_sem=sems[1].at[a, k], device_id=to, device_id_type=MESH)

    def _mine(self, ins, outs, sems):
        x, y, c = _pos()
        return [pltpu.make_async_copy(ins[a], outs[a].at[4 * x + 2 * y + c], sems[2].at[a])
                for a in range(len(self.arrs))]

    def _split(self, a):
        return self._halves(a)[1][1] > 0

    def _sends(self, ins, outs, sems, phase):
        x, y, c = _pos()
        me, sib, xn, yn, dg = (x, y, c), (x, y, 1 - c), (1 - x, y, c), (x, 1 - y, c), (1 - x, 1 - y, c)
        cps = []
        for a in range(len(self.arrs)):
            cp = lambda k, block, to, **kw: self._copy(ins, outs, sems, a, k, block, to, **kw)
            if phase == 0:
                cps += [cp(self.SIB, me, sib, own=True), cp(self.XN, me, xn, own=True), cp(self.YN, me, yn, own=True)]
            elif phase == 1:
                cps.append(cp(self.FWD_Y, xn, yn, part=0))
                if self._split(a):
                    cps.append(cp(self.FWD_X, yn, xn, part=1))
                cps += [cp(self.SIB_X, xn, sib), cp(self.SIB_Y, yn, sib)]
            else:
                cps.append(cp(self.SIB_D, dg, sib))
        return cps

    def _arrivals(self, ins, outs, sems, phase):
        x, y, c = _pos()
        me, xn, yn, dg = (x, y, c), (1 - x, y, c), (x, 1 - y, c), (1 - x, 1 - y, c)
        cps = []
        for a in range(len(self.arrs)):
            cp = lambda k, block, **kw: self._copy(ins, outs, sems, a, k, block, me, **kw)
            if phase == 1:
                cps += [cp(self.XN, xn), cp(self.YN, yn)]
            elif phase == 2:
                cps.append(cp(self.FWD_Y, dg, part=0))
                if self._split(a):
                    cps.append(cp(self.FWD_X, dg, part=1))
            else:
                cps += [cp(self.SIB, (x, y, 1 - c)), cp(self.SIB_X, (1 - x, y, 1 - c)),
                        cp(self.SIB_Y, (x, 1 - y, 1 - c)), cp(self.SIB_D, (1 - x, 1 - y, 1 - c))]
        return cps

    def start(self, ins, outs, sems):
        for cp in self._mine(ins, outs, sems) + self._sends(ins, outs, sems, 0):
            cp.start()

    def mid(self, ins, outs, sems):
        for cp in self._arrivals(ins, outs, sems, 1):
            cp.wait_recv()
        for cp in self._sends(ins, outs, sems, 1):
            cp.start()

    def mid2(self, ins, outs, sems):
        for cp in self._arrivals(ins, outs, sems, 2):
            cp.wait_recv()
        for cp in self._sends(ins, outs, sems, 2):
            cp.start()

    def finish(self, ins, outs, sems):
        for cp in self._arrivals(ins, outs, sems, 3):
            cp.wait_recv()
        for ph in range(3):
            for cp in self._sends(ins, outs, sems, ph):
                cp.wait_send()
        for cp in self._mine(ins, outs, sems):
            cp.wait()


def sc_comm(g, collective_id, name):
    srcs = [jax.new_ref(a, memory_space=pltpu.MemorySpace.HBM) for a in g.arrs]
    dsts = [jax.empty_ref(s, memory_space=pltpu.MemorySpace.HBM) for s in g.out_shape]

    @pl.kernel(mesh=plsc.ScalarSubcoreMesh(axis_name="sequencer", num_cores=1), name=name,
               scratch_types=tuple(g.scratch), compiler_params=pltpu.CompilerParams(collective_id=collective_id))
    def launch(*sems):
        peers = g.peers(*_pos())
        barrier = pltpu.get_barrier_semaphore()
        for peer in peers:
            pl.semaphore_signal(barrier, inc=1, device_id=peer, device_id_type=MESH)
        pl.semaphore_wait(barrier, len(peers))
        g.start(srcs, dsts, sems)
        for _, phase in g.mid_steps(1):
            phase(srcs, dsts, sems)
        g.finish(srcs, dsts, sems)

    launch()
    return [d[...] for d in dsts]


def sc_gather(arrs, collective_id, name):
    return sc_comm(BgGather(arrs), collective_id, name)


class BgPair:
    def __init__(self, arrs):
        n = len(arrs)
        self.arrs = list(arrs)
        self.out_shape = [SDS((4,) + a.shape[1:], a.dtype) for a in arrs]
        self.scratch = [pltpu.SemaphoreType.DMA((n, 4)), pltpu.SemaphoreType.DMA((n, 4))]
        self.result = None

    @staticmethod
    def peers(x, y, c):
        return [(x, y, 1 - c)]

    def mid_steps(self, nsteps):
        return []

    def _copies(self, ins, outs, sems):
        x, y, c = _pos()
        return [pltpu.make_async_remote_copy(
            src_ref=ins[a].at[2 * k + 1 - c], dst_ref=outs[a].at[k], send_sem=sems[0].at[a, k],
            recv_sem=sems[1].at[a, k], device_id=(x, y, 1 - c), device_id_type=MESH)
            for a in range(len(self.arrs)) for k in range(4)]

    def start(self, ins, outs, sems):
        for cp in self._copies(ins, outs, sems):
            cp.start()

    def finish(self, ins, outs, sems):
        cps = self._copies(ins, outs, sems)
        for cp in cps:
            cp.wait_recv()
        for cp in cps:
            cp.wait_send()


class BgChips(BgPair):
    def __init__(self, arrs):
        n = len(arrs)
        self.arrs = list(arrs)
        self.out_shape = [SDS((3,) + a.shape[1:], a.dtype) for a in arrs]
        self.scratch = [pltpu.SemaphoreType.DMA((n, 3)), pltpu.SemaphoreType.DMA((n, 3))]
        self.result = None

    @staticmethod
    def peers(x, y, c):
        return [(px, py, c) for px, py in _other_chips(x, y)]

    def _copies(self, ins, outs, sems):
        x, y, c = _pos()
        return [pltpu.make_async_remote_copy(
            src_ref=ins[a].at[2 * px + py], dst_ref=outs[a].at[r], send_sem=sems[0].at[a, r],
            recv_sem=sems[1].at[a, r], device_id=(px, py, c), device_id_type=MESH)
            for a in range(len(self.arrs)) for r, (px, py) in enumerate(_other_chips(x, y))]


class AdamRider:
    def __init__(self, w, m, v, part, r2, layer=0, prev=None):
        self.arrs = [w, m, v, part, r2] + list(prev or [])
        self.n_prev = len(prev or [])
        self.layer = layer
        self.out_shape = [SDS(w.shape, f32)] * 4
        self.scratch = []
        self.aliases = {5 + k: k for k in range(self.n_prev)}
        self.result = None

    def _tile(self, grid):
        assert len(grid) == 1
        _, R, C = self.arrs[0].shape
        return R // grid[0], C

    def in_specs(self, grid):
        tr, C = self._tile(grid)
        layer = self.layer
        blk = pl.BlockSpec((None, tr, C), lambda b: (layer, b, 0))
        mine = pl.BlockSpec((None, tr, C), lambda b: (2 * lax.axis_index("x") + lax.axis_index("y"), b, 0))
        return [blk, blk, blk, mine, pl.BlockSpec((3, tr, C), lambda b: (0, b, 0))] + [_ANY] * self.n_prev

    def out_specs(self, grid):
        tr, C = self._tile(grid)
        layer = self.layer
        return [pl.BlockSpec((None, tr, C), lambda b: (layer, b, 0))] * 4

    def mid_steps(self, nsteps):
        return []

    def start(self, ins, outs, sems):
        pass

    finish = start

    def step(self, ins, outs, sems):
        w_ref, m_ref, v_ref, p_ref, r_ref = ins[:5]
        g = p_ref[...].astype(f32) + r_ref[0].astype(f32) + r_ref[1].astype(f32) + r_ref[2].astype(f32)
        d, m_, v_ = _adamw(w_ref[...], g, m_ref[...], v_ref[...])
        for ref, val in zip(outs, (g, d, m_, v_)):
            ref[...] = val


def _call(bgs, body, *, name, grid, in_specs, out_specs, out_shape, scratch_shapes=(), compiler_params=None):
    single = not isinstance(out_shape, (list, tuple))
    out_specs_l = [out_specs] if single else list(out_specs)
    out_shape_l = [out_shape] if single else list(out_shape)
    bgs = [b for b in (bgs or []) if b is not None]
    n_in, n_out, n_sc = len(in_specs), len(out_shape_l), len(scratch_shapes)
    nsteps = math.prod(grid)
    b_in_specs = [b.in_specs(grid) if hasattr(b, "in_specs") else [_ANY] * len(b.arrs) for b in bgs]
    b_out_specs = [b.out_specs(grid) if hasattr(b, "out_specs") else [_ANY] * len(b.out_shape) for b in bgs]
    aliases, i_off, o_off = {}, n_in, n_out
    for b in bgs:
        aliases.update({i_off + i: o_off + o for i, o in getattr(b, "aliases", {}).items()})
        i_off, o_off = i_off + len(b.arrs), o_off + len(b.out_shape)

    def full(*refs):
        pos = [0]

        def take(k):
            r = refs[pos[0]:pos[0] + k]
            pos[0] += k
            return r

        ins = take(n_in)
        b_ins = [take(len(b.arrs)) for b in bgs]
        outs = take(n_out)
        b_outs = [take(len(b.out_shape)) for b in bgs]
        sc = take(n_sc)
        b_sc = [take(len(b.scratch)) for b in bgs]
        if bgs:
            step = pl.program_id(0)
            for d in range(1, len(grid)):
                step = step * grid[d] + pl.program_id(d)

            @pl.when(step == 0)
            def _():
                for b, i_, o_, s_ in zip(bgs, b_ins, b_outs, b_sc):
                    b.start(i_, o_, s_)

        body(*ins, *outs, *sc)
        if bgs:
            for b, i_, o_, s_ in zip(bgs, b_ins, b_outs, b_sc):
                if hasattr(b, "step"):
                    b.step(i_, o_, s_)
                for at, fn in b.mid_steps(nsteps):
                    @pl.when(step == at)
                    def _():
                        fn(i_, o_, s_)

            @pl.when(step == nsteps - 1)
            def _():
                for b, i_, o_, s_ in zip(bgs, b_ins, b_outs, b_sc):
                    b.finish(i_, o_, s_)

    def run(*args):
        res = pl.pallas_call(
            full, name=name, grid=grid,
            in_specs=list(in_specs) + [s for l in b_in_specs for s in l],
            out_specs=out_specs_l + [s for l in b_out_specs for s in l],
            out_shape=out_shape_l + [s for b in bgs for s in b.out_shape],
            scratch_shapes=list(scratch_shapes) + [s for b in bgs for s in b.scratch],
            input_output_aliases=aliases,
            compiler_params=compiler_params,
        )(*args, *[a for b in bgs for a in b.arrs])
        rest = list(res[n_out:])
        for b in bgs:
            b.result, rest = rest[:len(b.out_shape)], rest[len(b.out_shape):]
        return res[0] if single else list(res[:n_out])

    return run


def s5_discretize(a_re, a_im, log_dt, b_re, b_im, c_re, c_im):
    lam_r = jnp.minimum(a_re, DT_MIN_LAMBDA)
    lam_i = a_im
    dt = jnp.exp(log_dt)[:, None]
    e = jnp.exp(lam_r * dt)
    lbr = e * jnp.cos(lam_i * dt)
    lbi = e * jnp.sin(lam_i * dt)
    den = lam_r * lam_r + lam_i * lam_i
    cf_r = ((lbr - 1.0) * lam_r + lbi * lam_i) / den
    cf_i = (lbi * lam_r - (lbr - 1.0) * lam_i) / den
    bb_r = cf_r[:, :, None] * b_re - cf_i[:, :, None] * b_im
    bb_i = cf_r[:, :, None] * b_im + cf_i[:, :, None] * b_re
    eye = jnp.eye(8, dtype=f32)

    def blk_b(m):
        return jnp.einsum('bgpc,gh->bgchp', m.reshape(8, 8, S5_P, S5_C), eye).reshape(8, 128, 512)

    def blk_c(m):
        return jnp.einsum('bgcp,gh->bgphc', m.reshape(8, 8, S5_C, S5_P), eye).reshape(8, 512, 128)

    bm = jnp.concatenate([blk_b(bb_r), blk_b(bb_i)], axis=-1)
    cm = jnp.concatenate([blk_c(c_re), -blk_c(c_im)], axis=1)
    lam = jnp.stack([lbr.reshape(8, 512), lbi.reshape(8, 512)], axis=1)
    lam = jnp.broadcast_to(lam[:, :, None, :], (8, 2, 8, 512))
    return lam, bm, cm


def _cmul(ar, ai, br, bi):
    return ar * br - ai * bi, ar * bi + ai * br


def _shift_rows(v, k, up):
    row = lax.broadcasted_iota(jnp.int32, v.shape, 0)
    if up:
        return jnp.where(row < 8 - k, pltpu.roll(v, 8 - k, 0), 0.0)
    return jnp.where(row >= k, pltpu.roll(v, k, 0), 0.0)


def _chunk_scan(S, lr, li, reverse, aux=None):
    z = jnp.zeros((8, 512), f32)
    U = 4

    def idx(i):
        return (S5_STEPS - 1 - i) if reverse else i

    def rows_of(s):
        return pl.ds(s * 8, 8) if isinstance(s, int) else pl.ds(pl.multiple_of(s * 8, 8), 8)

    def rec(xr, xi, row):
        br = S[row, 0:512]
        bi = S[row, 512:1024]
        return lr * xr - li * xi + br, lr * xi + li * xr + bi

    def step1(i, c):
        for u in range(U):
            c = rec(c[0], c[1], rows_of(idx(i * U + u)))
        return c

    er, ei = lax.fori_loop(0, S5_STEPS // U, step1, (z, z))
    ar, ai = lr, li
    for _ in range(8):
        ar, ai = _cmul(ar, ai, ar, ai)
    cr, ci = _shift_rows(er, 1, reverse), _shift_rows(ei, 1, reverse)
    for k in (1, 2, 4):
        sr, si = _shift_rows(cr, k, reverse), _shift_rows(ci, k, reverse)
        pr, pi_ = _cmul(ar, ai, sr, si)
        cr, ci = cr + pr, ci + pi_
        ar, ai = _cmul(ar, ai, ar, ai)

    if aux is None:
        def step2(i, c):
            for u in range(U):
                row = rows_of(idx(i * U + u))
                c = rec(c[0], c[1], row)
                S[row, 0:512] = c[0]
                S[row, 512:1024] = c[1]
            return c

        lax.fori_loop(0, S5_STEPS // U, step2, (cr, ci))
        return None

    def one(s, c):
        gr0, gi0, dr, di = c
        row = rows_of(s)
        gr, gi = rec(gr0, gi0, row)
        S[row, 0:512] = gr
        S[row, 512:1024] = gi
        prow = rows_of(s - 1)
        xr = aux[prow, 0:512]
        xi = aux[prow, 512:1024]
        return gr, gi, dr + gr * xr + gi * xi, di + gi * xr - gr * xi

    def step2(i, c):
        for u in range(U):
            c = one(S5_STEPS - 1 - (i * U + u), c)
        return c

    c = lax.fori_loop(0, S5_STEPS // U - 1, step2, (cr, ci, z, z))
    for s in range(U - 1, 0, -1):
        c = one(s, c)
    gr, gi, dr, di = c
    row0 = pl.ds(0, 8)
    gr, gi = rec(gr, gi, row0)
    S[row0, 0:512] = gr
    S[row0, 512:1024] = gi
    last = pl.ds((S5_STEPS - 1) * 8, 8)
    xr = _shift_rows(aux[last, 0:512], 1, False)
    xi = _shift_rows(aux[last, 512:1024], 1, False)
    dr = dr + gr * xr + gi * xi
    di = di + gi * xr - gr * xi
    return dr, di


_ROWS = 256


def _row_loop(fn):
    def body(r, c):
        fn(pl.ds(pl.multiple_of(r * _ROWS, _ROWS), _ROWS))
        return c
    lax.fori_loop(0, T // _ROWS, body, 0)


def s5_core_fwd(hn, bm, lam, cm, bg=()):
    def body(u_ref, b_ref, lam_ref, c_ref, ys_ref, S):
        def bu(rows):
            S[rows, :] = _dot(u_ref[rows, :], b_ref[...])
        _row_loop(bu)
        _chunk_scan(S, lam_ref[0], lam_ref[1], False)

        def ys(rows):
            ys_ref[rows, :] = _dot(S[rows, :].astype(bf16), c_ref[...])
        _row_loop(ys)

    return _call(
        bg, body, name="s5_core_fwd", grid=(S5_SUB,),
        in_specs=[pl.BlockSpec((T, 128), lambda b: (0, b)),
                  pl.BlockSpec((None, 128, 1024), lambda b: (b, 0, 0)),
                  pl.BlockSpec((None, 4, 8, 512), lambda b: (b, 0, 0, 0)),
                  pl.BlockSpec((None, 1024, 128), lambda b: (b, 0, 0))],
        out_specs=pl.BlockSpec((T, 128), lambda b: (0, b)),
        out_shape=SDS((T, D), f32),
        scratch_shapes=[pltpu.VMEM((T, 1024), f32)],
        compiler_params=_cp(dimension_semantics=("arbitrary",)),
    )(hn, bm, lam, cm)


_SEG = _ROWS // S5_CH


def _scan_tile(S, lr, li, k, carry, reverse, store, aux=None):
    steps = range(k * _SEG, (k + 1) * _SEG)
    for s in (reversed(steps) if reverse else steps):
        row = pl.ds(s * 8, 8)
        xr, xi = carry[0], carry[1]
        nr = lr * xr - li * xi + S[row, 0:512]
        ni = lr * xi + li * xr + S[row, 512:1024]
        if store:
            S[row, 0:512] = nr
            S[row, 512:1024] = ni
        if aux is not None and s >= 1:
            prow = pl.ds((s - 1) * 8, 8)
            pr, pi_ = aux[prow, 0:512], aux[prow, 512:1024]
            carry = (nr, ni, carry[2] + nr * pr + ni * pi_, carry[3] + ni * pr - nr * pi_)
        elif aux is not None:
            carry = (nr, ni, carry[2], carry[3])
        else:
            carry = (nr, ni)
    return carry


def _chunk_starts(er, ei, lr, li, reverse):
    ar, ai = lr, li
    for _ in range(8):
        ar, ai = _cmul(ar, ai, ar, ai)
    cr, ci = _shift_rows(er, 1, reverse), _shift_rows(ei, 1, reverse)
    for k in (1, 2, 4):
        sr, si = _shift_rows(cr, k, reverse), _shift_rows(ci, k, reverse)
        pr, pi_ = _cmul(ar, ai, sr, si)
        cr, ci = cr + pr, ci + pi_
        ar, ai = _cmul(ar, ai, ar, ai)
    return cr, ci


def s5_core_bwd(hn, dy, bm, lam, cm, bg=()):
    nt = T // _ROWS

    def body(u_ref, dy_ref, b_ref, lam_ref, c_ref, du_ref, db_ref, dct_ref, dlam_ref, S1, S2):
        lr, li, lcr, lci = lam_ref[0], lam_ref[1], lam_ref[2], lam_ref[3]
        z = jnp.zeros((8, 512), f32)
        tile = lambda k: pl.ds(k * _ROWS, _ROWS)
        dyb = lambda k: dy_ref[tile(k), :].astype(bf16)

        c = (z, z)
        for k in range(nt):
            S1[tile(k), :] = _dot(u_ref[tile(k), :], b_ref[...])
            if k >= 1:
                c = _scan_tile(S1, lr, li, k - 1, c, False, False)
        c = _scan_tile(S1, lr, li, nt - 1, c, False, False)

        c = _chunk_starts(c[0], c[1], lr, li, False)
        dct_ref[...] = jnp.zeros_like(dct_ref)
        for k in range(nt):
            c = _scan_tile(S1, lr, li, k, c, False, True)
            if k >= 1:
                dct_ref[...] += _dot_tn(dyb(k - 1), S1[tile(k - 1), :].astype(bf16))
        dct_ref[...] += _dot_tn(dyb(nt - 1), S1[tile(nt - 1), :].astype(bf16))

        S2[tile(nt - 1), :] = _dot_nt(dyb(nt - 1), c_ref[...])
        c = (z, z)
        for k in range(nt - 1, -1, -1):
            if k >= 1:
                S2[tile(k - 1), :] = _dot_nt(dyb(k - 1), c_ref[...])
            c = _scan_tile(S2, lcr, lci, k, c, True, False)

        def dbu(k):
            gb = S2[tile(k), :].astype(bf16)
            db_ref[...] += _dot_tn(u_ref[tile(k), :], gb)
            du_ref[tile(k), :] = _dot_nt(gb, b_ref[...])

        c = _chunk_starts(c[0], c[1], lcr, lci, True) + (z, z)
        db_ref[...] = jnp.zeros_like(db_ref)
        for k in range(nt - 1, -1, -1):
            c = _scan_tile(S2, lcr, lci, k, c, True, True, aux=S1)
            if k + 1 < nt:
                dbu(k + 1)
        dbu(0)
        gr, gi, dr, di = c
        last = pl.ds((S5_STEPS - 1) * 8, 8)
        xr = _shift_rows(S1[last, 0:512], 1, False)
        xi = _shift_rows(S1[last, 512:1024], 1, False)
        dlam_ref[0] = dr + gr * xr + gi * xi
        dlam_ref[1] = di + gi * xr - gr * xi

    return _call(
        bg, body, name="s5_core_bwd", grid=(S5_SUB,),
        in_specs=[pl.BlockSpec((T, 128), lambda b: (0, b)),
                  pl.BlockSpec((T, 128), lambda b: (0, b)),
                  pl.BlockSpec((None, 128, 1024), lambda b: (b, 0, 0)),
                  pl.BlockSpec((None, 4, 8, 512), lambda b: (b, 0, 0, 0)),
                  pl.BlockSpec((None, 1024, 128), lambda b: (b, 0, 0))],
        out_specs=[pl.BlockSpec((T, 128), lambda b: (0, b)),
                   pl.BlockSpec((None, 128, 1024), lambda b: (b, 0, 0)),
                   pl.BlockSpec((None, 128, 1024), lambda b: (b, 0, 0)),
                   pl.BlockSpec((None, 2, 8, 512), lambda b: (b, 0, 0, 0))],
        out_shape=[SDS((T, D), f32), SDS((8, 128, 1024), f32), SDS((8, 128, 1024), f32), SDS((8, 2, 8, 512), f32)],
        scratch_shapes=[pltpu.VMEM((T, 1024), f32), pltpu.VMEM((T, 1024), f32)],
        compiler_params=_cp(dimension_semantics=("arbitrary",)),
    )(hn, dy, bm, lam, cm)


TM = 512
NT = T // TM


def _tile(n=D):
    return pl.BlockSpec((TM, n), lambda i: (i, 0))


def s5_pre(xp, g):
    def body(x_ref, g_ref, hn_ref):
        hn, _ = _rms(x_ref[...], g_ref[...])
        hn_ref[...] = hn.astype(bf16)

    return pl.pallas_call(
        body, name="s5_pre", grid=(NT,), in_specs=[_tile(), _full((1, D))], out_specs=_tile(),
        out_shape=SDS((T, D), bf16), compiler_params=_cp(dimension_semantics=("arbitrary",)),
    )(xp, g)


def _gelu_grad(y):
    c = math.sqrt(2.0 / math.pi)
    t = jnp.tanh(c * (y + 0.044715 * y * y * y))
    return 0.5 * (1.0 + t) + 0.5 * y * (1.0 - t * t) * c * (1.0 + 3.0 * 0.044715 * y * y)


def s5_post(ys, xp, g, d, wglu, bglu, bg=()):
    def body(ys_ref, x_ref, g_ref, d_ref, w_ref, b_ref, y_ref, z_ref, h_ref):
        x = x_ref[...]
        hn, _ = _rms(x, g_ref[...])
        y = ys_ref[...] + d_ref[...] * hn
        y_ref[...] = y
        yg = jax.nn.gelu(y).astype(bf16)
        for j in range(4):
            cv = slice(j * 256, (j + 1) * 256)
            cg = slice(1024 + j * 256, 1024 + (j + 1) * 256)
            val = _dot(yg, w_ref[j]) + b_ref[:, cv]
            gate = _dot(yg, w_ref[j + 4]) + b_ref[:, cg]
            z_ref[:, cv] = val
            z_ref[:, cg] = gate
            h_ref[:, cv] = x[:, cv] + val * jax.nn.sigmoid(gate)

    return _call(
        bg, body, name="s5_post", grid=(NT,),
        in_specs=[_tile(), _tile(), _full((1, D)), _full((1, D)), _full((8, D, 256)), _full((1, 2 * D))],
        out_specs=[_tile(), _tile(2 * D), _tile()],
        out_shape=[SDS((T, D), f32), SDS((T, 2 * D), f32), SDS((T, D), f32)],
        compiler_params=_cp(dimension_semantics=("arbitrary",)),
    )(ys, xp, g, d, wglu, bglu)


def s5_post_bwd(dh, y, z, wglu, bg=()):
    def body(dh_ref, y_ref, z_ref, w_ref, dy_ref, dw_ref, db_ref, acc):
        i = pl.program_id(0)

        @pl.when(i == 0)
        def _():
            acc[...] = jnp.zeros_like(acc)
            db_ref[...] = jnp.zeros_like(db_ref)

        dh_ = dh_ref[...]
        y = y_ref[...]
        yg = jax.nn.gelu(y).astype(bf16)
        dyg = jnp.zeros((TM, D), f32)
        for j in range(4):
            cv = slice(j * 256, (j + 1) * 256)
            cg = slice(1024 + j * 256, 1024 + (j + 1) * 256)
            val = z_ref[:, cv]
            sg = jax.nn.sigmoid(z_ref[:, cg])
            dval = dh_[:, cv] * sg
            dgate = dh_[:, cv] * val * sg * (1.0 - sg)
            db_ref[:, cv] += _colsum8(dval)
            db_ref[:, cg] += _colsum8(dgate)
            dvb = dval.astype(bf16)
            dgb = dgate.astype(bf16)
            acc[j] += _dot_tn(yg, dvb)
            acc[j + 4] += _dot_tn(yg, dgb)
            dyg = dyg + _dot_nt(dvb, w_ref[j]) + _dot_nt(dgb, w_ref[j + 4])
        dy_ref[...] = dyg * _gelu_grad(y)

        @pl.when(i == NT - 1)
        def _():
            dw_ref[...] = acc[...].astype(bf16)

    return _call(
        bg, body, name="s5_post_bwd", grid=(NT,),
        in_specs=[_tile(), _tile(), _tile(2 * D), _full((8, D, 256))],
        out_specs=[_tile(), _full((8, D, 256)), _full((8, 2 * D))],
        out_shape=[SDS((T, D), f32), SDS((8, D, 256), bf16), SDS((8, 2 * D), f32)],
        scratch_shapes=[pltpu.VMEM((8, D, 256), f32)],
        compiler_params=_cp(dimension_semantics=("arbitrary",)),
    )(dh, y, z, wglu)


def s5_pre_bwd(xp, g, du, dy, d, dh, bg=()):
    def body(x_ref, g_ref, du_ref, dy_ref, d_ref, dh_ref, dx_ref, dg_ref, dd_ref):
        i = pl.program_id(0)

        @pl.when(i == 0)
        def _():
            dg_ref[...] = jnp.zeros_like(dg_ref)
            dd_ref[...] = jnp.zeros_like(dd_ref)

        x = x_ref[...]
        g = g_ref[...]
        dy = dy_ref[...]
        hn, _ = _rms(x, g)
        dhn = du_ref[...] + d_ref[...] * dy
        dx, dgt = _rms_bwd(x, g, dhn)
        dx_ref[...] = dh_ref[...] + dx
        dg_ref[...] += _colsum8(dgt)
        dd_ref[...] += _colsum8(dy * hn)

    return _call(
        bg, body, name="s5_pre_bwd", grid=(NT,),
        in_specs=[_tile(), _full((1, D)), _tile(), _tile(), _full((1, D)), _tile()],
        out_specs=[_tile(), _full((8, D)), _full((8, D))],
        out_shape=[SDS((T, D), f32), SDS((8, D), f32), SDS((8, D), f32)],
        compiler_params=_cp(dimension_semantics=("arbitrary",)),
    )(xp, g, du, dy, d, dh)


TMF = 1024


def mlp_fwd(h, g, w_in, w_out, layer, bg=()):
    def body(h_ref, g_ref, wi_ref, wo_ref, hm_ref, r_ref, out_ref, acc):
        j = pl.program_id(1)

        @pl.when(j == 0)
        def _():
            hm, _ = _rms(h_ref[...], g_ref[...])
            hm_ref[...] = hm.astype(bf16)
            acc[...] = jnp.zeros_like(acc)

        a = jnp.maximum(_dot(hm_ref[...], wi_ref[...]), 0.0)
        r_ref[...] = a.astype(bf16)
        acc[...] += _dot((a * a).astype(bf16), wo_ref[...])

        @pl.when(j == NDEV - 1)
        def _():
            out_ref[...] = h_ref[...] + acc[...]

    return _call(
        bg, body, name=f"mlp_fwd{layer}", grid=(T // TMF, NDEV),
        in_specs=[pl.BlockSpec((TMF, D), lambda i, j: (i, 0)),
                  pl.BlockSpec((1, D), lambda i, j: (0, 0)),
                  pl.BlockSpec((None, D, D_FF_SHARD), lambda i, j: (j, 0, 0)),
                  pl.BlockSpec((None, D_FF_SHARD, D), lambda i, j: (j, 0, 0))],
        out_specs=[pl.BlockSpec((TMF, D), lambda i, j: (i, 0)), pl.BlockSpec((TMF, D_FF_SHARD), lambda i, j: (i, j)),
                   pl.BlockSpec((TMF, D), lambda i, j: (i, 0))],
        out_shape=[SDS((T, D), bf16), SDS((T, NDEV * D_FF_SHARD), bf16), SDS((T, D), f32)],
        scratch_shapes=[pltpu.VMEM((TMF, D), f32)],
        compiler_params=_cp(dimension_semantics=("arbitrary", "arbitrary")),
    )(h, g, w_in, w_out)


def mlp_bwd(h, hm, r, g, dout, dout_b, w_in, w_out, layer, bg=()):
    last = NDEV - 1

    def body(h_ref, hm_ref, r_ref, g_ref, do_ref, dob_ref, wi_ref, wo_ref, dh_ref, dwi_ref, dwo_ref, dg_ref,
             dhm, awi, awo):
        j = pl.program_id(0)
        i = pl.program_id(1)
        rows = pl.ds(pl.multiple_of(i * TM, TM), TM)

        @pl.when(i == 0)
        def _():
            awi[...] = jnp.zeros_like(awi)
            awo[...] = jnp.zeros_like(awo)

        dz = (_dot_nt(dob_ref[...], wo_ref[...]) * (2.0 * r_ref[...].astype(f32))).astype(bf16)
        rb = r_ref[...]
        awo[...] += _dot_tn(rb * rb, dob_ref[...])
        awi[...] += _dot_tn(hm_ref[...], dz)
        part = _dot_nt(dz, wi_ref[...])

        @pl.when(j == 0)
        def _():
            dhm[rows, :] = part

        @pl.when(j > 0)
        def _():
            dhm[rows, :] += part

        @pl.when(i == NT - 1)
        def _():
            dwi_ref[...] = awi[...].astype(bf16)
            dwo_ref[...] = awo[...].astype(bf16)

        @pl.when(j == last)
        def _():
            @pl.when(i == 0)
            def _():
                dg_ref[...] = jnp.zeros_like(dg_ref)
            dx, dgt = _rms_bwd(h_ref[...], g_ref[...], dhm[rows, :])
            dh_ref[...] = do_ref[...] + dx
            dg_ref[...] += _colsum8(dgt)

    late = lambda j, i: (jnp.where(j == last, i, 0), 0)
    return _call(
        bg, body, name=f"mlp_bwd{layer}", grid=(NDEV, NT),
        in_specs=[pl.BlockSpec((TM, D), late),
                  pl.BlockSpec((TM, D), lambda j, i: (i, 0)),
                  pl.BlockSpec((TM, D_FF_SHARD), lambda j, i: (i, j)),
                  pl.BlockSpec((1, D), lambda j, i: (0, 0)),
                  pl.BlockSpec((TM, D), late),
                  pl.BlockSpec((TM, D), lambda j, i: (i, 0)),
                  pl.BlockSpec((None, D, D_FF_SHARD), lambda j, i: (j, 0, 0)),
                  pl.BlockSpec((None, D_FF_SHARD, D), lambda j, i: (j, 0, 0))],
        out_specs=[pl.BlockSpec((TM, D), late),
                   pl.BlockSpec((None, D, D_FF_SHARD), lambda j, i: (j, 0, 0)),
                   pl.BlockSpec((None, D_FF_SHARD, D), lambda j, i: (j, 0, 0)),
                   pl.BlockSpec((8, D), lambda j, i: (0, 0))],
        out_shape=[SDS((T, D), f32), SDS((NDEV, D, D_FF_SHARD), bf16), SDS((NDEV, D_FF_SHARD, D), bf16),
                   SDS((8, D), f32)],
        scratch_shapes=[pltpu.VMEM((T, D), f32), pltpu.VMEM((D, D_FF_SHARD), f32), pltpu.VMEM((D_FF_SHARD, D), f32)],
        compiler_params=_cp(dimension_semantics=("arbitrary", "arbitrary")),
    )(h, hm, r, g, dout, dout_b, w_in, w_out)


def _spread4():
    r = lax.broadcasted_iota(jnp.int32, (256, D), 0)
    c = lax.broadcasted_iota(jnp.int32, (256, D), 1)
    return ((c // 256 == r // HEAD_DIM) & (c % HEAD_DIM == r % HEAD_DIM)).astype(bf16)


def attn_pre(h, g_kv, g_mix, wkv, bkv, spread, wq, bq):
    def body(h_ref, gkv_ref, gm_ref, wkv_ref, bkv_ref, sp_ref, wq_ref, bq_ref, kvn_ref, hn_ref, k_ref, v_ref, q_ref):
        h_ = h_ref[...]
        kvn = _rms(h_, gkv_ref[...])[0].astype(bf16)
        hn = _rms(h_, gm_ref[...])[0].astype(bf16)
        kvn_ref[...] = kvn
        hn_ref[...] = hn
        kv = (_dot(kvn, wkv_ref[...]) + bkv_ref[...]).astype(bf16)
        k_ref[...] = _dot(kv[:, :256], sp_ref[...]).astype(bf16)
        v_ref[...] = _dot(kv[:, 256:], sp_ref[...]).astype(bf16)
        q_ref[...] = (_dot(hn, wq_ref[...]) + bq_ref[...]).astype(bf16)

    return pl.pallas_call(
        body, name="attn_pre", grid=(NT,),
        in_specs=[_tile(), _full((1, D)), _full((1, D)), _full((D, 512)), _full((1, 512)), _full((256, D)),
                  _full((D, D)), _full((1, D))],
        out_specs=[_tile()] * 5,
        out_shape=[SDS((T, D), bf16)] * 5,
        compiler_params=_cp(dimension_semantics=("arbitrary",)),
    )(h, g_kv, g_mix, wkv, bkv, spread, wq, bq)


def _attn_specs():
    cur = pl.BlockSpec((TM, 256), lambda j, n: (n, j))
    prev = pl.BlockSpec((BLK, 256), lambda j, n: (jnp.maximum(n * (TM // BLK) - 1, 0), j))
    return cur, prev


def _head_mask(g):
    lane = lax.broadcasted_iota(jnp.int32, (1, 256), 1)
    return (lane >= g * HEAD_DIM) & (lane < (g + 1) * HEAD_DIM)


def _stack_heads(t):
    return jnp.concatenate([jnp.where(_head_mask(g), t, 0) for g in range(Q_PER_KV)], axis=0)


def _unstack_heads(t):
    out = jnp.where(_head_mask(0), t[0:BLK], 0.0)
    for g in range(1, Q_PER_KV):
        out = out + jnp.where(_head_mask(g), t[g * BLK:(g + 1) * BLK], 0.0)
    return out


def _attn_probs(qs, k2, sinks, first):
    rows = Q_PER_KV * BLK
    s = _dot_nt(qs, k2) * (1.0 / math.sqrt(HEAD_DIM))
    qi = jnp.bitwise_and(lax.broadcasted_iota(jnp.int32, (rows, 2 * BLK), 0), BLK - 1)
    kj = lax.broadcasted_iota(jnp.int32, (rows, 2 * BLK), 1)
    diff = qi + BLK - kj
    valid = (diff >= 0) & (diff < BLK) & (jnp.logical_not(first) | (kj >= BLK))
    s = jnp.where(valid, s, -jnp.inf)
    rb = lax.broadcasted_iota(jnp.int32, (rows, 1), 0)
    sink = jnp.where(rb < BLK, sinks[0], jnp.where(rb < 2 * BLK, sinks[1], jnp.where(rb < 3 * BLK, sinks[2], sinks[3])))
    m = jnp.maximum(jnp.max(s, axis=-1, keepdims=True), sink)
    p = jnp.exp(s - m)
    ps = jnp.exp(sink - m)
    denom = jnp.sum(p, axis=-1, keepdims=True) + ps
    return p / denom, ps / denom


def _window_blocks(b, n, kc_ref, kp_ref, vc_ref, vp_ref):
    if b == 0:
        return (jnp.concatenate([kp_ref[...], kc_ref[0:BLK, :]], axis=0),
                jnp.concatenate([vp_ref[...], vc_ref[0:BLK, :]], axis=0), n == 0)
    rows = pl.ds((b - 1) * BLK, 2 * BLK)
    return kc_ref[rows, :], vc_ref[rows, :], False


def attn_core_fwd(q, k4, v4, sinks, bg=()):
    nb = TM // BLK

    def body(sink_ref, q_ref, kc_ref, kp_ref, vc_ref, vp_ref, o_ref):
        j = pl.program_id(0)
        n = pl.program_id(1)
        sk = [sink_ref[j * Q_PER_KV + g] for g in range(Q_PER_KV)]
        for b in range(nb):
            qb = q_ref[b * BLK:(b + 1) * BLK, :]
            k2, v2, first = _window_blocks(b, n, kc_ref, kp_ref, vc_ref, vp_ref)
            a, _ = _attn_probs(_stack_heads(qb), k2, sk, first)
            o_ref[b * BLK:(b + 1) * BLK, :] = _unstack_heads(_dot(a.astype(bf16), v2)).astype(bf16)

    cur, prev = _attn_specs()
    return _call(
        bg, body, name="attn_core_fwd", grid=(N_KV, NT),
        in_specs=[pl.BlockSpec(memory_space=pltpu.SMEM), cur, cur, prev, cur, prev],
        out_specs=cur, out_shape=SDS((T, D), bf16),
        compiler_params=_cp(dimension_semantics=("arbitrary", "arbitrary")),
    )(sinks, q, k4, k4, v4, v4)


def attn_post(h, o, wo, bo):
    def body(h_ref, o_ref, w_ref, b_ref, out_ref):
        out_ref[...] = h_ref[...] + _dot(o_ref[...], w_ref[...]) + b_ref[...]

    return pl.pallas_call(
        body, name="attn_post", grid=(NT,), in_specs=[_tile(), _tile(), _full((D, D)), _full((1, D))],
        out_specs=_tile(), out_shape=SDS((T, D), f32), compiler_params=_cp(dimension_semantics=("arbitrary",)),
    )(h, o, wo, bo)


def attn_bwd_pre(dh, o, wo, bg=()):
    def body(dh_ref, o_ref, w_ref, do_ref, dw_ref, db_ref, acc):
        i = pl.program_id(0)

        @pl.when(i == 0)
        def _():
            acc[...] = jnp.zeros_like(acc)
            db_ref[...] = jnp.zeros_like(db_ref)

        dh_ = dh_ref[...]
        dhb = dh_.astype(bf16)
        do_ref[...] = _dot_nt(dhb, w_ref[...]).astype(bf16)
        acc[...] += _dot_tn(o_ref[...], dhb)
        db_ref[...] += _colsum8(dh_)

        @pl.when(i == NT - 1)
        def _():
            dw_ref[...] = acc[...].astype(bf16)

    return _call(
        bg, body, name="attn_bwd_pre", grid=(NT,), in_specs=[_tile(), _tile(), _full((D, D))],
        out_specs=[_tile(), _full((D, D)), _full((8, D))],
        out_shape=[SDS((T, D), bf16), SDS((D, D), bf16), SDS((8, D), f32)],
        scratch_shapes=[pltpu.VMEM((D, D), f32)],
        compiler_params=_cp(dimension_semantics=("arbitrary",)),
    )(dh, o, wo)


def attn_core_bwd(q, do, k4, v4, sinks, bg=()):
    nb = TM // BLK

    def body(sink_ref, q_ref, do_ref, kc_ref, kp_ref, vc_ref, vp_ref, dq_ref, dk_ref, dv_ref, ds_ref):
        j = pl.program_id(0)
        n = pl.program_id(1)

        @pl.when(n == 0)
        def _():
            dk_ref[...] = jnp.zeros_like(dk_ref)
            dv_ref[...] = jnp.zeros_like(dv_ref)
            ds_ref[...] = jnp.zeros_like(ds_ref)

        lane8 = lax.broadcasted_iota(jnp.int32, (8, 128), 1)
        row8 = lax.broadcasted_iota(jnp.int32, (8, 128), 0)
        sk = [sink_ref[j * Q_PER_KV + g] for g in range(Q_PER_KV)]
        for b in range(nb):
            qs = _stack_heads(q_ref[b * BLK:(b + 1) * BLK, :])
            dos = _stack_heads(do_ref[b * BLK:(b + 1) * BLK, :])
            k2, v2, first = _window_blocks(b, n, kc_ref, kp_ref, vc_ref, vp_ref)
            a, asink = _attn_probs(qs, k2, sk, first)
            dp = _dot_nt(dos, v2)
            dd = jnp.sum(a * dp, axis=-1, keepdims=True)
            dsc = (a * (dp - dd) * (1.0 / math.sqrt(HEAD_DIM))).astype(bf16)
            t = asink * dd
            for g in range(Q_PER_KV):
                dsink = -jnp.sum(t[g * BLK:(g + 1) * BLK], axis=0, keepdims=True)
                ds_ref[...] += jnp.where((lane8 == g) & (row8 == 0), jnp.broadcast_to(dsink, (8, 128)), 0.0)
            dq_ref[b * BLK:(b + 1) * BLK, :] = _unstack_heads(_dot(dsc, k2))
            dk2 = _dot_tn(dsc, qs)
            dv2 = _dot_tn(a.astype(bf16), dos)
            cur = pl.ds(pl.multiple_of(n * TM + b * BLK, BLK), BLK)
            dk_ref[cur, :] += dk2[BLK:, :]
            dv_ref[cur, :] += dv2[BLK:, :]
            if b == 0:
                @pl.when(n > 0)
                def _():
                    prv = pl.ds(pl.multiple_of(n * TM - BLK, BLK), BLK)
                    dk_ref[prv, :] += dk2[:BLK, :]
                    dv_ref[prv, :] += dv2[:BLK, :]
            else:
                prv = pl.ds(pl.multiple_of(n * TM + (b - 1) * BLK, BLK), BLK)
                dk_ref[prv, :] += dk2[:BLK, :]
                dv_ref[prv, :] += dv2[:BLK, :]

    cur, prev = _attn_specs()
    col = pl.BlockSpec((T, 256), lambda j, n: (0, j))
    return _call(
        bg, body, name="attn_core_bwd", grid=(N_KV, NT),
        in_specs=[pl.BlockSpec(memory_space=pltpu.SMEM), cur, cur, cur, prev, cur, prev],
        out_specs=[cur, col, col, pl.BlockSpec((None, 8, 128), lambda j, n: (j, 0, 0))],
        out_shape=[SDS((T, D), f32), SDS((T, D), f32), SDS((T, D), f32), SDS((N_KV, 8, 128), f32)],
        compiler_params=_cp(dimension_semantics=("arbitrary", "arbitrary")),
    )(sinks, q, do, k4, k4, v4, v4)


def attn_bwd_q(h, dh, dq, hn, g_mix, wq):
    def body(h_ref, dh_ref, dq_ref, hn_ref, gm_ref, wq_ref, out_ref, dwq_ref, dbq_ref, dgm_ref, aq):
        i = pl.program_id(0)

        @pl.when(i == 0)
        def _():
            aq[...] = jnp.zeros_like(aq)
            dbq_ref[...] = jnp.zeros_like(dbq_ref)
            dgm_ref[...] = jnp.zeros_like(dgm_ref)

        dq_ = dq_ref[...]
        dqb = dq_.astype(bf16)
        aq[...] += _dot_tn(hn_ref[...], dqb)
        dbq_ref[...] += _colsum8(dq_)
        dx, dg = _rms_bwd(h_ref[...], gm_ref[...], _dot_nt(dqb, wq_ref[...]))
        out_ref[...] = dh_ref[...] + dx
        dgm_ref[...] += _colsum8(dg)

        @pl.when(i == NT - 1)
        def _():
            dwq_ref[...] = aq[...].astype(bf16)

    vec = _full((8, D))
    mat = _full((D, D))
    return pl.pallas_call(
        body, name="attn_bwd_q", grid=(NT,),
        in_specs=[_tile()] * 4 + [_full((1, D)), mat],
        out_specs=[_tile(), mat, vec, vec],
        out_shape=[SDS((T, D), f32), SDS((D, D), bf16), SDS((8, D), f32), SDS((8, D), f32)],
        scratch_shapes=[pltpu.VMEM((D, D), f32)],
        compiler_params=_cp(dimension_semantics=("arbitrary",)),
    )(h, dh, dq, hn, g_mix, wq)


def attn_bwd_kv(h, dh, dk4, dv4, kvn, g_kv, wkv, spread):
    def body(h_ref, dh_ref, dk_ref, dv_ref, kvn_ref, gkv_ref, wkv_ref, sp_ref, out_ref, outb_ref, dw_ref, db_ref,
             dgkv_ref, acc):
        i = pl.program_id(0)

        @pl.when(i == 0)
        def _():
            for r in (acc, db_ref, dgkv_ref):
                r[...] = jnp.zeros_like(r)

        dkv = jnp.concatenate([_dot_nt(dk_ref[...].astype(bf16), sp_ref[...]),
                               _dot_nt(dv_ref[...].astype(bf16), sp_ref[...])], axis=1)
        dkvb = dkv.astype(bf16)
        acc[...] += _dot_tn(kvn_ref[...], dkvb)
        db_ref[...] += _colsum8(dkv)
        dx, dg = _rms_bwd(h_ref[...], gkv_ref[...], _dot_nt(dkvb, wkv_ref[...]))
        out = dh_ref[...] + dx
        out_ref[...] = out
        outb_ref[...] = out.astype(bf16)
        dgkv_ref[...] += _colsum8(dg)

        @pl.when(i == NT - 1)
        def _():
            dw_ref[...] = acc[...].astype(bf16)

    return pl.pallas_call(
        body, name="attn_bwd_kv", grid=(NT,),
        in_specs=[_tile()] * 5 + [_full((1, D)), _full((D, 512)), _full((256, D))],
        out_specs=[_tile(), _tile(), _full((D, 512)), _full((8, 512)), _full((8, D))],
        out_shape=[SDS((T, D), f32), SDS((T, D), bf16), SDS((D, 512), bf16), SDS((8, 512), f32), SDS((8, D), f32)],
        scratch_shapes=[pltpu.VMEM((D, 512), f32)],
        compiler_params=_cp(dimension_semantics=("arbitrary",)),
    )(h, dh, dk4, dv4, kvn, g_kv, wkv, spread)


def final_loss(h, g, target):
    def body(h_ref, g_ref, t_ref, loss_ref, dh_ref, dhb_ref, dg_ref):
        i = pl.program_id(0)

        @pl.when(i == 0)
        def _():
            loss_ref[...] = jnp.zeros_like(loss_ref)
            dg_ref[...] = jnp.zeros_like(dg_ref)

        h_ = h_ref[...]
        g_ = g_ref[...]
        y, _ = _rms(h_, g_)
        diff = y - t_ref[...]
        per_tok = jnp.mean(diff * diff, axis=-1, keepdims=True)
        tot = 0.5 * jnp.sum(per_tok, axis=0, keepdims=True)
        lane = lax.broadcasted_iota(jnp.int32, (8, 128), 1)
        row = lax.broadcasted_iota(jnp.int32, (8, 128), 0)
        loss_ref[...] += jnp.where((lane == 0) & (row == 0), jnp.broadcast_to(tot, (8, 128)), 0.0)
        dx, dgt = _rms_bwd(h_, g_, diff * (1.0 / D))
        dh_ref[...] = dx
        dhb_ref[...] = dx.astype(bf16)
        dg_ref[...] += _colsum8(dgt)

    return pl.pallas_call(
        body, name="final_loss", grid=(NT,), in_specs=[_tile(), _full((1, D)), _tile()],
        out_specs=[_full((8, 128)), _tile(), _tile(), _full((8, D))],
        out_shape=[SDS((8, 128), f32), SDS((T, D), f32), SDS((T, D), bf16), SDS((8, D), f32)],
        compiler_params=_cp(dimension_semantics=("arbitrary",)),
    )(h, g, target)


def _to_chunked(a):
    return a.reshape(S5_CH, S5_STEPS, a.shape[-1]).transpose(1, 0, 2).reshape(T, a.shape[-1])


def _from_chunked(a):
    return a.reshape(S5_STEPS, S5_CH, a.shape[-1]).transpose(1, 0, 2).reshape(T, a.shape[-1])


def _rep4(w):
    return jnp.broadcast_to(w.reshape(w.shape[0], N_KV, 1, HEAD_DIM), (w.shape[0], N_KV, Q_PER_KV, HEAD_DIM)).reshape(
        w.shape[0], N_KV * Q_PER_KV * HEAD_DIM)


def _fold4(w):
    return w.reshape(w.shape[0], N_KV, Q_PER_KV, HEAD_DIM).sum(axis=2).reshape(w.shape[0], N_KV * HEAD_DIM)


def fwd_bwd(x, target, p, shards, opt, core, chip):
    row = lambda v: v.reshape(1, -1)
    (lam, bm, cm), prep_vjp = jax.vjp(s5_discretize, p["s5_a_re"][0], p["s5_a_im"][0], p["s5_log_dt"][0],
                                      p["s5_b_re"][0], p["s5_b_im"][0], p["s5_c_re"][0], p["s5_c_im"][0])
    bmb, cmb = bm.astype(bf16), cm.astype(bf16)
    lam = jnp.concatenate([lam, lam * jnp.array([1.0, -1.0], f32).reshape(1, 2, 1, 1)], axis=1)
    g_mix0, g_mix1 = row(p["norm_mix"][0]), row(p["norm_mix"][1])
    g_mlp0, g_mlp1 = row(p["norm_mlp"][0]), row(p["norm_mlp"][1])
    g_kv, g_fin = row(p["norm_kv"]), row(p["norm_final"])
    bq, bo = p["b_q"], p["b_o"]
    bkv = row(p["b_kv"])
    spread = _spread4()
    sinks = p["sinks"].reshape(16)

    def reduce_pairs(names, bg):
        return [add_pairs(g, r, core, f"add_pairs_{n}") for n, g, r in zip(names, bg.arrs, bg.result)]

    wglu, gvec, win0, wout0 = sc_gather(
        [shards["s5_w_glu"], shards["vecs"], shards["w_in0"], shards["w_out0"]], 3, "sc_gather_layer0")
    wkv, wq, wo, win1 = sc_gather([shards["w_kv"], shards["w_q"], shards["w_o"], shards["w_in1"]], 4, "sc_gather_attn")
    wout1, = sc_gather([shards["w_out1"]], 5, "sc_gather_w_out1")
    xp = _to_chunked(x)
    hn0 = s5_pre(xp, g_mix0)
    ys = s5_core_fwd(hn0, bmb, lam, cmb)
    d_skip = gvec[:, 0, :128].reshape(1, D)
    bglu = gvec[:, 0, 128:].reshape(1, 2 * D)
    y, z, h1 = s5_post(ys, xp, g_mix0, d_skip, wglu, bglu)
    hm0, r0, h2p = mlp_fwd(h1, g_mlp0, win0, wout0, 0)
    wkv, wq, wo = wkv.reshape(D, 512), wq.reshape(D, D), wo.reshape(D, D)
    h2 = _from_chunked(h2p)
    kvn, hn1, k4, v4, q = attn_pre(h2, g_kv, g_mix1, wkv, bkv, spread, wq, bq)
    o = attn_core_fwd(q, k4, v4, sinks)
    h3 = attn_post(h2, o, wo, bo)
    hm1, r1, h4 = mlp_fwd(h3, g_mlp1, win1, wout1, 1)
    loss, dh4, dh4b, dg_fin = final_loss(h4, g_fin, target)

    def pair_sums(names, grads, cid, before):
        r1 = sc_comm(BgPair(grads), cid, "sc_pair_" + names[0])
        parts = [add_pairs(g, r, core, f"add_pairs_{n}") for n, g, r in zip(names, grads, r1)]
        before, parts = lax.optimization_barrier((before, parts))
        return before, parts

    def across_chips(names, parts, cid):
        return list(zip(parts, sc_comm(BgChips(parts), cid, "sc_chips_" + names[0])))

    dh3, dwin1, dwout1, dg_mlp1 = mlp_bwd(h3, hm1, r1, g_mlp1, dh4, dh4b, win1, wout1, 1)
    do, dwo, dbo = attn_bwd_pre(dh3, o, wo)
    do, parts = pair_sums(["w_in1", "w_out1"], [dwin1, dwout1], 6, do)
    rs_in1, rs_out1 = across_chips(["w_in1", "w_out1"], parts, 7)
    dq, dk4, dv4, dsink = attn_core_bwd(q, do, k4, v4, sinks)
    dh2, dwq, dbq, dg_mix1 = attn_bwd_q(h2, dh3, dq, hn1, g_mix1, wq)
    dh2, dh2b, dwkv, dbkv, dg_kv = attn_bwd_kv(h2, dh2, dk4, dv4, kvn, g_kv, wkv, spread)
    dh2p, dh2pb = _to_chunked(dh2), _to_chunked(dh2b)
    big = {}
    a_in1 = adam_big(*opt["w_mlp_in"], *rs_in1, chip, "adam_w_mlp_in1", layer=1)
    a_out1 = adam_big(*opt["w_mlp_out"], *rs_out1, chip, "adam_w_mlp_out1", layer=1)
    dh2p, a_in1, a_out1 = lax.optimization_barrier((dh2p, a_in1, a_out1))
    names = ["w_kv", "w_q", "w_o"]
    dh2p, parts = pair_sums(names, [dwkv.reshape(NDEV, 128, 512), dwq.reshape(NDEV, 128, D),
                                    dwo.reshape(NDEV, 128, D)], 8, dh2p)
    rs_attn = across_chips(names, parts, 9)
    dh1, dwin0, dwout0, dg_mlp0 = mlp_bwd(h1, hm0, r0, g_mlp0, dh2p, dh2pb, win0, wout0, 0)
    a_attn = [adam_big(*opt[n], *rs, chip, f"adam_{n}") for n, rs in zip(names, rs_attn)]
    dh1, a_attn = lax.optimization_barrier((dh1, a_attn))
    big.update(zip(names, a_attn))
    dy, dwglu, dbglu = s5_post_bwd(dh1, y, z, wglu)
    dy, parts = pair_sums(["w_in0", "w_out0"], [dwin0, dwout0], 10, dy)
    rs_in0, rs_out0 = across_chips(["w_in0", "w_out0"], parts, 11)
    du, dbm, dcmt, dlam = s5_core_bwd(hn0, dy, bmb, lam, cmb)
    du, parts = pair_sums(["s5_w_glu"], [dwglu], 12, du)
    rs_glu, = across_chips(["s5_w_glu"], parts, 13)
    dxp, dg_mix0, dd = s5_pre_bwd(xp, g_mix0, du, dy, d_skip, dh1)
    big["w_mlp_in"] = adam_big(*opt["w_mlp_in"], *rs_in0, chip, "adam_w_mlp_in0", layer=0, prev=a_in1)
    big["w_mlp_out"] = adam_big(*opt["w_mlp_out"], *rs_out0, chip, "adam_w_mlp_out0", layer=0, prev=a_out1)
    big["s5_w_glu"] = adam_big(*opt["s5_w_glu"], *rs_glu, chip, "adam_s5_w_glu")
    grad_x = _from_chunked(dxp)
    da_re, da_im, dlog_dt, db_re, db_im, dc_re, dc_im = prep_vjp((dlam, dbm, dcmt.transpose(0, 2, 1)))

    def lanes(v_):
        v_ = v_.reshape(1, -1)
        return jnp.pad(v_, ((0, 0), (0, D - v_.shape[1])))

    small = jnp.concatenate([
        dg_mix0[0:1], dg_mix1[0:1], dg_mlp0[0:1], dg_mlp1[0:1], dg_kv[0:1], dg_fin[0:1], dd[0:1], dbq[0:1], dbo[0:1],
        dbglu[0:1].reshape(2, D), lanes(dbkv[0:1]),
        lanes(dsink[:, 0, :Q_PER_KV]), lanes(dlog_dt), lanes(loss[0:1, 0:1]), jnp.zeros((1, D), f32),
        da_re.reshape(4, D), da_im.reshape(4, D),
        db_re.transpose(0, 2, 1).reshape(64, D), db_im.transpose(0, 2, 1).reshape(64, D),
        dc_re.reshape(64, D), dc_im.reshape(64, D)], axis=0)
    small, big["w_mlp_in"], big["w_mlp_out"] = lax.optimization_barrier((small, big["w_mlp_in"], big["w_mlp_out"]))
    return loss, grad_x, small, big


_ANY = pl.BlockSpec(memory_space=pl.ANY)


def _pos():
    return lax.axis_index("x"), lax.axis_index("y"), lax.axis_index("c")


def _other_chips(x, y):
    return [(1 - x, y), (x, 1 - y), (1 - x, 1 - y)]


def all_gather(arrs):
    n = len(arrs)

    def body(*refs):
        ins, outs = refs[:n], refs[n:2 * n]
        send_sems, recv_sems, local_sems = refs[2 * n:]
        x, y, c = _pos()
        me, sib = (x, y, c), (x, y, 1 - c)
        chips = _other_chips(x, y)

        def copy(a, k, block, to, src=None):
            dst = outs[a].at[4 * block[0] + 2 * block[1] + block[2]]
            return pltpu.make_async_remote_copy(
                src_ref=dst if src is None else src, dst_ref=dst, send_sem=send_sems.at[a, k],
                recv_sem=recv_sems.at[a, k], device_id=to, device_id_type=MESH)

        mine = [pltpu.make_async_copy(ins[a], outs[a].at[4 * x + 2 * y + c], local_sems.at[a]) for a in range(n)]
        for cp in mine:
            cp.start()
        first = []
        for a in range(n):
            first.append(copy(a, 0, me, sib, src=ins[a]))
            first += [copy(a, 1 + j, me, (*chip, c), src=ins[a]) for j, chip in enumerate(chips)]
        for cp in first:
            cp.start()
        passed = []
        for j, chip in enumerate(chips):
            for a in range(n):
                copy(a, 1 + j, (*chip, c), me).wait_recv()
                cp = copy(a, 4 + j, (*chip, c), sib)
                cp.start()
                passed.append(cp)
        for a in range(n):
            copy(a, 0, sib, me).wait_recv()
            for j, chip in enumerate(chips):
                copy(a, 4 + j, (*chip, 1 - c), me).wait_recv()
        for cp in first + passed:
            cp.wait_send()
        for cp in mine:
            cp.wait()

    return pl.pallas_call(
        body, name="all_gather", in_specs=[_ANY] * n, out_specs=[_ANY] * n,
        out_shape=[SDS((NDEV,) + a.shape, a.dtype) for a in arrs],
        scratch_shapes=[pltpu.SemaphoreType.DMA((n, 7)), pltpu.SemaphoreType.DMA((n, 7)),
                        pltpu.SemaphoreType.DMA((n,))],
    )(*arrs)


def rs_pair(grads):
    n = len(grads)

    def body(*refs):
        ins, outs = refs[:n], refs[n:2 * n]
        send_sems, recv_sems = refs[2 * n:]
        x, y, c = _pos()
        cps = []
        for a in range(n):
            for k in range(4):
                cps.append(pltpu.make_async_remote_copy(
                    src_ref=ins[a].at[2 * k + 1 - c], dst_ref=outs[a].at[k], send_sem=send_sems.at[a, k],
                    recv_sem=recv_sems.at[a, k], device_id=(x, y, 1 - c), device_id_type=MESH))
        for cp in cps:
            cp.start()
        for cp in cps:
            cp.wait_recv()
        for cp in cps:
            cp.wait_send()

    return pl.pallas_call(
        body, name="rs_pair", in_specs=[_ANY] * n, out_specs=[_ANY] * n,
        out_shape=[SDS((4,) + g.shape[1:], g.dtype) for g in grads],
        scratch_shapes=[pltpu.SemaphoreType.DMA((n, 4)), pltpu.SemaphoreType.DMA((n, 4))],
    )(*grads)


def rs_chips(parts):
    n = len(parts)

    def body(*refs):
        ins, outs = refs[:n], refs[n:2 * n]
        send_sems, recv_sems = refs[2 * n:]
        x, y, c = _pos()
        cps = []
        for a in range(n):
            for r, (px, py) in enumerate(_other_chips(x, y)):
                cps.append(pltpu.make_async_remote_copy(
                    src_ref=ins[a].at[2 * px + py], dst_ref=outs[a].at[r], send_sem=send_sems.at[a, r],
                    recv_sem=recv_sems.at[a, r], device_id=(px, py, c), device_id_type=MESH))
        for cp in cps:
            cp.start()
        for cp in cps:
            cp.wait_recv()
        for cp in cps:
            cp.wait_send()

    return pl.pallas_call(
        body, name="rs_chips", in_specs=[_ANY] * n, out_specs=[_ANY] * n,
        out_shape=[SDS((3,) + g.shape[1:], g.dtype) for g in parts],
        scratch_shapes=[pltpu.SemaphoreType.DMA((n, 3)), pltpu.SemaphoreType.DMA((n, 3))],
    )(*parts)


def _row_tile(r, c):
    return min(r, max(8, (512 * 1024) // c))


def add_pairs(g, r1, core, name):
    _, R, C = g.shape
    tr = _row_tile(R, C)

    def body(core_ref, g_ref, r_ref, o_ref):
        o_ref[...] = (g_ref[...].astype(f32) + r_ref[...].astype(f32)).astype(bf16)

    return pl.pallas_call(
        body, name=name, out_shape=SDS((4, R, C), bf16),
        grid_spec=pltpu.PrefetchScalarGridSpec(
            num_scalar_prefetch=1, grid=(4, R // tr),
            in_specs=[pl.BlockSpec((None, tr, C), lambda k, i, core: (2 * k + core[0], i, 0)),
                      pl.BlockSpec((None, tr, C), lambda k, i, core: (k, i, 0))],
            out_specs=pl.BlockSpec((None, tr, C), lambda k, i, core: (k, i, 0))),
        compiler_params=_cp(dimension_semantics=("arbitrary", "arbitrary")),
    )(core, g, r1)


def _adamw(w, g, m, v):
    m = ADAM_B1 * m + (1.0 - ADAM_B1) * g
    v = ADAM_B2 * v + (1.0 - ADAM_B2) * (g * g)
    m_hat = m / (1.0 - ADAM_B1 ** ADAM_STEP)
    v_hat = v / (1.0 - ADAM_B2 ** ADAM_STEP)
    delta = -ADAM_LR * (m_hat / (jnp.sqrt(v_hat) + ADAM_EPS) + ADAM_WD * w)
    return delta, m, v


def adam_big(w, m, v, part, r2, chip, name, layer=0, prev=None):
    L, R, C = w.shape
    tr = _row_tile(R, C)

    def body(chip_ref, w_ref, m_ref, v_ref, p_ref, r_ref, *rest):
        g_out, d_out, m_out, v_out = rest[-4:]
        g = p_ref[...].astype(f32) + r_ref[0].astype(f32) + r_ref[1].astype(f32) + r_ref[2].astype(f32)
        d, m_, v_ = _adamw(w_ref[...], g, m_ref[...], v_ref[...])
        g_out[...] = g
        d_out[...] = d
        m_out[...] = m_
        v_out[...] = v_

    blk = pl.BlockSpec((None, tr, C), lambda i, chip: (layer, i, 0))
    extra = [] if prev is None else list(prev)
    return pl.pallas_call(
        body, name=name, out_shape=[SDS((L, R, C), f32)] * 4,
        grid_spec=pltpu.PrefetchScalarGridSpec(
            num_scalar_prefetch=1, grid=(R // tr,),
            in_specs=[blk, blk, blk,
                      pl.BlockSpec((None, tr, C), lambda i, chip: (chip[0], i, 0)),
                      pl.BlockSpec((3, tr, C), lambda i, chip: (0, i, 0))] + [_ANY] * len(extra),
            out_specs=[blk] * 4),
        input_output_aliases={6 + k: k for k in range(len(extra))},
        compiler_params=_cp(dimension_semantics=("arbitrary",)),
    )(chip, w, m, v, part, r2, *extra)


def allreduce_small(buf, chips=None):
    shp = buf.shape
    half = (shp[0] // 16) * 8
    parts = (pl.ds(0, half), pl.ds(half, shp[0] - half))
    n_c = 0 if chips is None else len(chips.arrs)

    def body(in_ref, *refs):
        c_in, out_ref, c_out = refs[:n_c], refs[n_c], refs[n_c + 1:2 * n_c + 1]
        acc1, acc2, r0, r1, r2, send_sems, recv_sems = refs[2 * n_c + 1:2 * n_c + 8]
        c_sems = refs[2 * n_c + 8:]
        if chips is not None:
            chips.start(c_in, c_out, c_sems)
        x, y, c = _pos()
        across = [(1 - x, y, c), (x, 1 - y, c)]

        def exchange(src, rcv, dst, copies):
            cps = [pltpu.make_async_remote_copy(
                src_ref=src.at[rows], dst_ref=rcv.at[rows], send_sem=send_sems.at[k], recv_sem=recv_sems.at[k],
                device_id=peer, device_id_type=MESH) for k, rows, peer in copies]
            for cp in cps:
                cp.start()
            for cp in cps:
                cp.wait()
            dst[...] = src[...] + rcv[...]

        exchange(in_ref, r0, acc1, [(0, pl.ds(0, shp[0]), (x, y, 1 - c))])
        exchange(acc1, r1, acc2, [(1, parts[0], across[0]), (2, parts[1], across[1])])
        exchange(acc2, r2, out_ref, [(3, parts[0], across[1]), (4, parts[1], across[0])])
        if chips is not None:
            chips.finish(c_in, c_out, c_sems)

    vm = pl.BlockSpec(memory_space=pltpu.VMEM)
    res = pl.pallas_call(
        body, name="allreduce_small", in_specs=[vm] + [_ANY] * n_c, out_specs=[vm] + [_ANY] * n_c,
        out_shape=[SDS(shp, f32)] + ([] if chips is None else chips.out_shape),
        scratch_shapes=[pltpu.VMEM(shp, f32)] * 5 + [pltpu.SemaphoreType.DMA((5,)), pltpu.SemaphoreType.DMA((5,))]
        + ([] if chips is None else chips.scratch),
    )(buf, *([] if chips is None else chips.arrs))
    if chips is not None:
        chips.result = list(res[1:])
    return res[0]


SMALL_ROWS = {'norm_mix': (0, 2, D), 'norm_mlp': (2, 2, D), 'norm_kv': (4, 1, D), 'norm_final': (5, 1, D),
              's5_d': (6, 1, D), 'b_q': (7, 1, D), 'b_o': (8, 1, D), 's5_b_glu': (9, 2, D), 'b_kv': (11, 1, 512),
              'sinks': (12, 1, 16), 's5_log_dt': (13, 1, 64), 's5_a_re': (16, 4, D), 's5_a_im': (20, 4, D),
              's5_b_re': (24, 64, D), 's5_b_im': (88, 64, D), 's5_c_re': (152, 64, D), 's5_c_im': (216, 64, D)}
LOSS_ROW = 14
ROW_PARAMS = ['norm_mix', 'norm_mlp', 'norm_kv', 'norm_final', 'b_q', 'b_o', 'b_kv', 'sinks', 's5_log_dt']
SHARD_PARAMS = ['s5_d', 's5_b_glu']
S5_PARAMS = ['s5_a_re', 's5_a_im', 's5_b_re', 's5_b_im', 's5_c_re', 's5_c_im']


def adam_small(dev, gsum, s5_grads, w, m, v):
    names = ROW_PARAMS + SHARD_PARAMS + S5_PARAMS
    n_g = len(ROW_PARAMS) + len(SHARD_PARAMS)

    def body(dev_ref, gs_ref, *refs):
        pos = [0]

        def take(k):
            r = refs[pos[0]:pos[0] + k]
            pos[0] += k
            return r

        g5 = take(len(S5_PARAMS))
        wr, mr, vr = take(len(names)), take(len(names)), take(len(names))
        g_out = take(n_g)
        d_out, m_out, v_out = take(len(names)), take(len(names)), take(len(names))
        dv = dev_ref[0]
        for i, n in enumerate(names):
            if n in S5_PARAMS:
                g = g5[S5_PARAMS.index(n)][...]
            elif n in SHARD_PARAMS:
                r0, _, _ = SMALL_ROWS[n]
                ln = wr[i].shape[1]
                g = jnp.zeros((1, ln), f32)
                for k in range(NDEV):
                    off = k * ln
                    piece = gs_ref[r0 + off // D:r0 + off // D + 1, off % D:off % D + ln]
                    g = g + jnp.where(dv == k, piece, 0.0)
                g_out[i][...] = g
            else:
                r0, nr, nl = SMALL_ROWS[n]
                g = gs_ref[r0:r0 + nr, 0:nl]
                g_out[i][...] = g
            d, m_, v_ = _adamw(wr[i][...], g, mr[i][...], vr[i][...])
            d_out[i][...] = d
            m_out[i][...] = m_
            v_out[i][...] = v_

    vm = pl.BlockSpec(memory_space=pltpu.VMEM)
    ins = [s5_grads[n] for n in S5_PARAMS] + [d[n] for d in (w, m, v) for n in names]
    shapes = [SDS(w[n].shape, f32) for n in names]
    res = pl.pallas_call(
        body, name="adam_small", in_specs=[pl.BlockSpec(memory_space=pltpu.SMEM)] + [vm] * (1 + len(ins)),
        out_specs=[vm] * (n_g + 3 * len(names)), out_shape=shapes[:n_g] + shapes * 3,
        compiler_params=_cp(),
    )(dev, gsum, *ins)
    g_o = dict(zip(names[:n_g], res[:n_g]))
    rest = res[n_g:]
    k = len(names)
    return g_o, dict(zip(names, rest[:k])), dict(zip(names, rest[k:2 * k])), dict(zip(names, rest[2 * k:]))


WEIGHTS = ['norm_mix', 'norm_mlp', 'norm_kv', 'norm_final', 's5_a_re', 's5_a_im', 's5_log_dt', 's5_b_re', 's5_b_im',
           's5_c_re', 's5_c_im', 's5_d', 's5_w_glu', 's5_b_glu', 'w_kv', 'b_kv', 'w_q', 'b_q', 'sinks', 'w_o', 'b_o',
           'w_mlp_in', 'w_mlp_out']
BIG = ['s5_w_glu', 'w_kv', 'w_q', 'w_o', 'w_mlp_in', 'w_mlp_out']
BIG_2D = {'s5_w_glu': (D, 256), 'w_kv': (128, 512), 'w_q': (128, D), 'w_o': (128, D), 'w_mlp_in': (2 * D, 512),
          'w_mlp_out': (2 * 512, D)}
SHARDED_SMALL = {'s5_d': D, 's5_b_glu': 2 * D}
SMALL = [n for n in WEIGHTS if n not in BIG]
SMALL_SIZE = {'norm_mix': 2 * D, 'norm_mlp': 2 * D, 'norm_kv': D, 'norm_final': D, 's5_a_re': 4096, 's5_a_im': 4096,
              's5_log_dt': 64, 's5_b_re': 65536, 's5_b_im': 65536, 's5_c_re': 65536, 's5_c_im': 65536, 's5_d': D,
              's5_b_glu': 2 * D, 'b_kv': 512, 'b_q': D, 'sinks': 16, 'b_o': D}


def _pack(vals):
    parts = []
    for n in SMALL:
        v = vals[n].reshape(-1).astype(f32)
        parts.append(jnp.pad(v, (0, (-v.shape[0]) % 128)))
    flat = jnp.concatenate(parts)
    flat = jnp.pad(flat, (0, (-flat.shape[0]) % 1024))
    return flat.reshape(-1, 128)


def _unpack(buf):
    flat = buf.reshape(-1)
    out, off = {}, 0
    for n in SMALL:
        sz = SMALL_SIZE[n]
        out[n] = flat[off:off + sz]
        off += sz + (-sz) % 128
    return out


def kernel(x, norm_mix, norm_mlp, norm_kv, norm_final, s5_a_re, s5_a_im, s5_log_dt, s5_b_re, s5_b_im, s5_c_re, s5_c_im, s5_d, s5_w_glu, s5_b_glu, w_kv, b_kv, w_q, b_q, sinks, w_o, b_o, w_mlp_in, w_mlp_out, loss_target, m_norm_mix, m_norm_mlp, m_norm_kv, m_norm_final, m_s5_a_re, m_s5_a_im, m_s5_log_dt, m_s5_b_re, m_s5_b_im, m_s5_c_re, m_s5_c_im, m_s5_d, m_s5_w_glu, m_s5_b_glu, m_w_kv, m_b_kv, m_w_q, m_b_q, m_sinks, m_w_o, m_b_o, m_w_mlp_in, m_w_mlp_out, v_norm_mix, v_norm_mlp, v_norm_kv, v_norm_final, v_s5_a_re, v_s5_a_im, v_s5_log_dt, v_s5_b_re, v_s5_b_im, v_s5_c_re, v_s5_c_im, v_s5_d, v_s5_w_glu, v_s5_b_glu, v_w_kv, v_b_kv, v_w_q, v_b_q, v_sinks, v_w_o, v_b_o, v_w_mlp_in, v_w_mlp_out):
    w = dict(norm_mix=norm_mix, norm_mlp=norm_mlp, norm_kv=norm_kv, norm_final=norm_final, s5_a_re=s5_a_re,
             s5_a_im=s5_a_im, s5_log_dt=s5_log_dt, s5_b_re=s5_b_re, s5_b_im=s5_b_im, s5_c_re=s5_c_re, s5_c_im=s5_c_im,
             s5_d=s5_d, s5_w_glu=s5_w_glu, s5_b_glu=s5_b_glu, w_kv=w_kv, b_kv=b_kv, w_q=w_q, b_q=b_q, sinks=sinks,
             w_o=w_o, b_o=b_o, w_mlp_in=w_mlp_in, w_mlp_out=w_mlp_out)
    m = dict(norm_mix=m_norm_mix, norm_mlp=m_norm_mlp, norm_kv=m_norm_kv, norm_final=m_norm_final, s5_a_re=m_s5_a_re,
             s5_a_im=m_s5_a_im, s5_log_dt=m_s5_log_dt, s5_b_re=m_s5_b_re, s5_b_im=m_s5_b_im, s5_c_re=m_s5_c_re,
             s5_c_im=m_s5_c_im, s5_d=m_s5_d, s5_w_glu=m_s5_w_glu, s5_b_glu=m_s5_b_glu, w_kv=m_w_kv, b_kv=m_b_kv,
             w_q=m_w_q, b_q=m_b_q, sinks=m_sinks, w_o=m_w_o, b_o=m_b_o, w_mlp_in=m_w_mlp_in, w_mlp_out=m_w_mlp_out)
    v = dict(norm_mix=v_norm_mix, norm_mlp=v_norm_mlp, norm_kv=v_norm_kv, norm_final=v_norm_final, s5_a_re=v_s5_a_re,
             s5_a_im=v_s5_a_im, s5_log_dt=v_s5_log_dt, s5_b_re=v_s5_b_re, s5_b_im=v_s5_b_im, s5_c_re=v_s5_c_re,
             s5_c_im=v_s5_c_im, s5_d=v_s5_d, s5_w_glu=v_s5_w_glu, s5_b_glu=v_s5_b_glu, w_kv=v_w_kv, b_kv=v_b_kv,
             w_q=v_w_q, b_q=v_b_q, sinks=v_sinks, w_o=v_w_o, b_o=v_b_o, w_mlp_in=v_w_mlp_in, w_mlp_out=v_w_mlp_out)
    xi, yi, ci = _pos()
    dev = 4 * xi + 2 * yi + ci
    core = ci.reshape(1).astype(jnp.int32)
    chip = (2 * xi + yi).reshape(1).astype(jnp.int32)

    shards = {
        "s5_w_glu": s5_w_glu[0].astype(bf16), "w_kv": w_kv.astype(bf16), "w_q": w_q[0].astype(bf16),
        "w_o": w_o[0].astype(bf16), "w_in0": w_mlp_in[0].astype(bf16), "w_in1": w_mlp_in[1].astype(bf16),
        "w_out0": w_mlp_out[0].astype(bf16), "w_out1": w_mlp_out[1].astype(bf16),
        "vecs": jnp.broadcast_to(jnp.concatenate([s5_d, s5_b_glu], axis=1), (8, 384)),
    }
    as3d = lambda a, n: a if a.ndim == 3 and a.shape[0] == 2 else a.reshape((1,) + BIG_2D[n])
    opt = {n: (as3d(w[n], n), as3d(m[n], n), as3d(v[n], n)) for n in BIG}
    _, grad_x, grads, big = fwd_bwd(x[0], loss_target[0], {n: w[n] for n in SMALL}, shards, opt, core, chip)

    gsum = allreduce_small(grads)

    out_g, out_d, out_m, out_v = {}, {}, {}, {}
    for n in BIG:
        out_g[n], out_d[n], out_m[n], out_v[n] = [r.reshape(w[n].shape) for r in big[n]]

    loss = gsum[LOSS_ROW, 0]
    swapped = ("s5_b_re", "s5_b_im")
    swap = lambda a: a.transpose(0, 1, 3, 2)

    def kernel_side(d):
        d = {n: (d[n].reshape(1, -1) if d[n].ndim == 1 else d[n]) for n in SMALL}
        d.update({n: swap(d[n]) for n in swapped})
        return d

    s5_g = {}
    for n in S5_PARAMS:
        r0, nr, _ = SMALL_ROWS[n]
        s5_g[n] = gsum[r0:r0 + nr].reshape((1, 64, 16, 64) if n in swapped else w[n].shape)
        out_g[n] = s5_g[n]
    g_s, d_s, m_s, v_s = adam_small(dev.reshape(1).astype(jnp.int32), gsum, s5_g, kernel_side(w), kernel_side(m),
                                    kernel_side(v))
    for src, dst in ((g_s, out_g), (d_s, out_d), (m_s, out_m), (v_s, out_v)):
        dst.update(src)
    for dst in (out_g, out_d, out_m, out_v):
        for n in SMALL:
            dst[n] = (swap(dst[n]) if n in swapped else dst[n]).reshape(w[n].shape)

    return (loss, grad_x[None], *[out_g[n] for n in WEIGHTS], *[out_d[n] for n in WEIGHTS],
            *[out_m[n] for n in WEIGHTS], *[out_v[n] for n in WEIGHTS])
```

```python
import functools
import math

import jax
import jax.numpy as jnp
from jax import lax
from jax.experimental import pallas as pl
from jax.experimental.pallas import tpu as pltpu
from jax.experimental.pallas import tpu_sc as plsc

f32 = jnp.float32
bf16 = jnp.bfloat16
SDS = jax.ShapeDtypeStruct

T = 2048
D = 1024
NDEV = 8
NORM_EPS = 1e-5
S5_G, S5_C, S5_P = 64, 16, 64
S5_SUB = 8
S5_CH = 8
S5_STEPS = T // S5_CH
DT_MIN_LAMBDA = -1e-4
HEAD_DIM = 64
N_KV = 4
Q_PER_KV = 4
BLK = 128
D_FF_SHARD = 512
ADAM_LR, ADAM_B1, ADAM_B2, ADAM_EPS, ADAM_WD, ADAM_STEP = 0.001, 0.9, 0.999, 1e-08, 0.01, 10
VMEM_LIMIT = 56 * 1024 * 1024
MESH = pl.DeviceIdType.MESH


def _cp(**kw):
    return pltpu.CompilerParams(vmem_limit_bytes=VMEM_LIMIT, **kw)


def _dot(a, b):
    return jnp.dot(a, b, preferred_element_type=f32)


def _dot_nt(a, b):
    return lax.dot_general(a, b, (((1,), (1,)), ((), ())), preferred_element_type=f32)


def _dot_tn(a, b):
    return lax.dot_general(a, b, (((0,), (0,)), ((), ())), preferred_element_type=f32)


def _rms(x, g):
    r = lax.rsqrt(jnp.mean(x * x, axis=-1, keepdims=True) + NORM_EPS)
    return x * r * g, r


def _rms_bwd(x, g, dy):
    r = lax.rsqrt(jnp.mean(x * x, axis=-1, keepdims=True) + NORM_EPS)
    u = dy * g
    dx = r * u - (r * r * r) * x * jnp.mean(u * x, axis=-1, keepdims=True)
    return dx, dy * x * r


def _colsum8(v):
    s = jnp.sum(v, axis=0, keepdims=True)
    row = lax.broadcasted_iota(jnp.int32, (8, v.shape[1]), 0)
    return jnp.where(row == 0, jnp.broadcast_to(s, (8, v.shape[1])), 0.0)


def _full(shape):
    nd = len(shape)
    return pl.BlockSpec(shape, lambda *_: (0,) * nd, pipeline_mode=pl.Buffered(1))


_ANY = pl.BlockSpec(memory_space=pl.ANY)


def _pos():
    return lax.axis_index("x"), lax.axis_index("y"), lax.axis_index("c")


def _other_chips(x, y):
    return [(1 - x, y), (x, 1 - y), (1 - x, 1 - y)]


class BgGather:
    SIB, XN, YN, FWD_Y, FWD_X, SIB_X, SIB_Y, SIB_D = range(8)

    def __init__(self, arrs, mids=(0.5, 0.75)):
        n = len(arrs)
        self.arrs = list(arrs)
        self.out_shape = [SDS((NDEV,) + a.shape, a.dtype) for a in arrs]
        self.scratch = [pltpu.SemaphoreType.DMA((n, 8)), pltpu.SemaphoreType.DMA((n, 8)),
                        pltpu.SemaphoreType.DMA((n,))]
        self.mids = mids
        self.result = None

    @staticmethod
    def peers(x, y, c):
        return [(x, y, 1 - c), (1 - x, y, c), (x, 1 - y, c)]

    def mid_steps(self, nsteps):
        at = lambda f: min(nsteps - 1, max(0, int(f * nsteps) - 1))
        return [(at(self.mids[0]), self.mid), (max(at(self.mids[0]), at(self.mids[1])), self.mid2)]

    def _halves(self, a):
        rows = self.arrs[a].shape[0]
        cut = (rows // 32) * 16 if rows >= 32 else rows
        return (0, cut), (cut, rows - cut)

    def _copy(self, ins, outs, sems, a, k, block, to, own=False, part=None):
        slot = 4 * block[0] + 2 * block[1] + block[2]
        rows = pl.ds(0, self.arrs[a].shape[0]) if part is None else pl.ds(*self._halves(a)[part])
        dst = outs[a].at[slot, rows]
        return pltpu.make_async_remote_copy(
            src_ref=ins[a].at[rows] if own else dst, dst_ref=dst, send_sem=sems[0].at[a, k],
            recv_sem=sems[1].at[a, k], device_id=to, device_id_type=MESH)

    def _mine(self, ins, outs, sems):
        x, y, c = _pos()
        return [pltpu.make_async_copy(ins[a], outs[a].at[4 * x + 2 * y + c], sems[2].at[a])
                for a in range(len(self.arrs))]

    def _split(self, a):
        return self._halves(a)[1][1] > 0

    def _sends(self, ins, outs, sems, phase):
        x, y, c = _pos()
        me, sib, xn, yn, dg = (x, y, c), (x, y, 1 - c), (1 - x, y, c), (x, 1 - y, c), (1 - x, 1 - y, c)
        cps = []
        for a in range(len(self.arrs)):
            cp = lambda k, block, to, **kw: self._copy(ins, outs, sems, a, k, block, to, **kw)
            if phase == 0:
                cps += [cp(self.SIB, me, sib, own=True), cp(self.XN, me, xn, own=True), cp(self.YN, me, yn, own=True)]
            elif phase == 1:
                cps.append(cp(self.FWD_Y, xn, yn, part=0))
                if self._split(a):
                    cps.append(cp(self.FWD_X, yn, xn, part=1))
                cps += [cp(self.SIB_X, xn, sib), cp(self.SIB_Y, yn, sib)]
            else:
                cps.append(cp(self.SIB_D, dg, sib))
        return cps

    def _arrivals(self, ins, outs, sems, phase):
        x, y, c = _pos()
        me, xn, yn, dg = (x, y, c), (1 - x, y, c), (x, 1 - y, c), (1 - x, 1 - y, c)
        cps = []
        for a in range(len(self.arrs)):
            cp = lambda k, block, **kw: self._copy(ins, outs, sems, a, k, block, me, **kw)
            if phase == 1:
                cps += [cp(self.XN, xn), cp(self.YN, yn)]
            elif phase == 2:
                cps.append(cp(self.FWD_Y, dg, part=0))
                if self._split(a):
                    cps.append(cp(self.FWD_X, dg, part=1))
            else:
                cps += [cp(self.SIB, (x, y, 1 - c)), cp(self.SIB_X, (1 - x, y, 1 - c)),
                        cp(self.SIB_Y, (x, 1 - y, 1 - c)), cp(self.SIB_D, (1 - x, 1 - y, 1 - c))]
        return cps

    def start(self, ins, outs, sems):
        for cp in self._mine(ins, outs, sems) + self._sends(ins, outs, sems, 0):
            cp.start()

    def mid(self, ins, outs, sems):
        for cp in self._arrivals(ins, outs, sems, 1):
            cp.wait_recv()
        for cp in self._sends(ins, outs, sems, 1):
            cp.start()

    def mid2(self, ins, outs, sems):
        for cp in self._arrivals(ins, outs, sems, 2):
            cp.wait_recv()
        for cp in self._sends(ins, outs, sems, 2):
            cp.start()

    def finish(self, ins, outs, sems):
        for cp in self._arrivals(ins, outs, sems, 3):
            cp.wait_recv()
        for ph in range(3):
            for cp in self._sends(ins, outs, sems, ph):
                cp.wait_send()
        for cp in self._mine(ins, outs, sems):
            cp.wait()


def sc_comm(g, collective_id, name):
    srcs = [jax.new_ref(a, memory_space=pltpu.MemorySpace.HBM) for a in g.arrs]
    dsts = [jax.empty_ref(s, memory_space=pltpu.MemorySpace.HBM) for s in g.out_shape]

    @pl.kernel(mesh=plsc.ScalarSubcoreMesh(axis_name="sequencer", num_cores=1), name=name,
               scratch_types=tuple(g.scratch), compiler_params=pltpu.CompilerParams(collective_id=collective_id))
    def launch(*sems):
        peers = g.peers(*_pos())
        barrier = pltpu.get_barrier_semaphore()
        for peer in peers:
            pl.semaphore_signal(barrier, inc=1, device_id=peer, device_id_type=MESH)
        pl.semaphore_wait(barrier, len(peers))
        g.start(srcs, dsts, sems)
        for _, phase in g.mid_steps(1):
            phase(srcs, dsts, sems)
        g.finish(srcs, dsts, sems)

    launch()
    return [d[...] for d in dsts]


def sc_gather(arrs, collective_id, name):
    return sc_comm(BgGather(arrs), collective_id, name)


class BgPair:
    def __init__(self, arrs):
        n = len(arrs)
        self.arrs = list(arrs)
        self.out_shape = [SDS((4,) + a.shape[1:], a.dtype) for a in arrs]
        self.scratch = [pltpu.SemaphoreType.DMA((n, 4)), pltpu.SemaphoreType.DMA((n, 4))]
        self.result = None

    @staticmethod
    def peers(x, y, c):
        return [(x, y, 1 - c)]

    def mid_steps(self, nsteps):
        return []

    def _copies(self, ins, outs, sems):
        x, y, c = _pos()
        return [pltpu.make_async_remote_copy(
            src_ref=ins[a].at[2 * k + 1 - c], dst_ref=outs[a].at[k], send_sem=sems[0].at[a, k],
            recv_sem=sems[1].at[a, k], device_id=(x, y, 1 - c), device_id_type=MESH)
            for a in range(len(self.arrs)) for k in range(4)]

    def start(self, ins, outs, sems):
        for cp in self._copies(ins, outs, sems):
            cp.start()

    def finish(self, ins, outs, sems):
        cps = self._copies(ins, outs, sems)
        for cp in cps:
            cp.wait_recv()
        for cp in cps:
            cp.wait_send()


class BgChips(BgPair):
    def __init__(self, arrs):
        n = len(arrs)
        self.arrs = list(arrs)
        self.out_shape = [SDS((3,) + a.shape[1:], a.dtype) for a in arrs]
        self.scratch = [pltpu.SemaphoreType.DMA((n, 3)), pltpu.SemaphoreType.DMA((n, 3))]
        self.result = None

    @staticmethod
    def peers(x, y, c):
        return [(px, py, c) for px, py in _other_chips(x, y)]

    def _copies(self, ins, outs, sems):
        x, y, c = _pos()
        return [pltpu.make_async_remote_copy(
            src_ref=ins[a].at[2 * px + py], dst_ref=outs[a].at[r], send_sem=sems[0].at[a, r],
            recv_sem=sems[1].at[a, r], device_id=(px, py, c), device_id_type=MESH)
            for a in range(len(self.arrs)) for r, (px, py) in enumerate(_other_chips(x, y))]


class AdamRider:
    def __init__(self, w, m, v, part, r2, layer=0, prev=None):
        self.arrs = [w, m, v, part, r2] + list(prev or [])
        self.n_prev = len(prev or [])
        self.layer = layer
        self.out_shape = [SDS(w.shape, f32)] * 4
        self.scratch = []
        self.aliases = {5 + k: k for k in range(self.n_prev)}
        self.result = None

    def _tile(self, grid):
        assert len(grid) == 1
        _, R, C = self.arrs[0].shape
        return R // grid[0], C

    def in_specs(self, grid):
        tr, C = self._tile(grid)
        layer = self.layer
        blk = pl.BlockSpec((None, tr, C), lambda b: (layer, b, 0))
        mine = pl.BlockSpec((None, tr, C), lambda b: (2 * lax.axis_index("x") + lax.axis_index("y"), b, 0))
        return [blk, blk, blk, mine, pl.BlockSpec((3, tr, C), lambda b: (0, b, 0))] + [_ANY] * self.n_prev

    def out_specs(self, grid):
        tr, C = self._tile(grid)
        layer = self.layer
        return [pl.BlockSpec((None, tr, C), lambda b: (layer, b, 0))] * 4

    def mid_steps(self, nsteps):
        return []

    def start(self, ins, outs, sems):
        pass

    finish = start

    def step(self, ins, outs, sems):
        w_ref, m_ref, v_ref, p_ref, r_ref = ins[:5]
        g = p_ref[...].astype(f32) + r_ref[0].astype(f32) + r_ref[1].astype(f32) + r_ref[2].astype(f32)
        d, m_, v_ = _adamw(w_ref[...], g, m_ref[...], v_ref[...])
        for ref, val in zip(outs, (g, d, m_, v_)):
            ref[...] = val


def _call(bgs, body, *, name, grid, in_specs, out_specs, out_shape, scratch_shapes=(), compiler_params=None):
    single = not isinstance(out_shape, (list, tuple))
    out_specs_l = [out_specs] if single else list(out_specs)
    out_shape_l = [out_shape] if single else list(out_shape)
    bgs = [b for b in (bgs or []) if b is not None]
    n_in, n_out, n_sc = len(in_specs), len(out_shape_l), len(scratch_shapes)
    nsteps = math.prod(grid)
    b_in_specs = [b.in_specs(grid) if hasattr(b, "in_specs") else [_ANY] * len(b.arrs) for b in bgs]
    b_out_specs = [b.out_specs(grid) if hasattr(b, "out_specs") else [_ANY] * len(b.out_shape) for b in bgs]
    aliases, i_off, o_off = {}, n_in, n_out
    for b in bgs:
        aliases.update({i_off + i: o_off + o for i, o in getattr(b, "aliases", {}).items()})
        i_off, o_off = i_off + len(b.arrs), o_off + len(b.out_shape)

    def full(*refs):
        pos = [0]

        def take(k):
            r = refs[pos[0]:pos[0] + k]
            pos[0] += k
            return r

        ins = take(n_in)
        b_ins = [take(len(b.arrs)) for b in bgs]
        outs = take(n_out)
        b_outs = [take(len(b.out_shape)) for b in bgs]
        sc = take(n_sc)
        b_sc = [take(len(b.scratch)) for b in bgs]
        if bgs:
            step = pl.program_id(0)
            for d in range(1, len(grid)):
                step = step * grid[d] + pl.program_id(d)

            @pl.when(step == 0)
            def _():
                for b, i_, o_, s_ in zip(bgs, b_ins, b_outs, b_sc):
                    b.start(i_, o_, s_)

        body(*ins, *outs, *sc)
        if bgs:
            for b, i_, o_, s_ in zip(bgs, b_ins, b_outs, b_sc):
                if hasattr(b, "step"):
                    b.step(i_, o_, s_)
                for at, fn in b.mid_steps(nsteps):
                    @pl.when(step == at)
                    def _():
                        fn(i_, o_, s_)

            @pl.when(step == nsteps - 1)
            def _():
                for b, i_, o_, s_ in zip(bgs, b_ins, b_outs, b_sc):
                    b.finish(i_, o_, s_)

    def run(*args):
        res = pl.pallas_call(
            full, name=name, grid=grid,
            in_specs=list(in_specs) + [s for l in b_in_specs for s in l],
            out_specs=out_specs_l + [s for l in b_out_specs for s in l],
            out_shape=out_shape_l + [s for b in bgs for s in b.out_shape],
            scratch_shapes=list(scratch_shapes) + [s for b in bgs for s in b.scratch],
            input_output_aliases=aliases,
            compiler_params=compiler_params,
        )(*args, *[a for b in bgs for a in b.arrs])
        rest = list(res[n_out:])
        for b in bgs:
            b.result, rest = rest[:len(b.out_shape)], rest[len(b.out_shape):]
        return res[0] if single else list(res[:n_out])

    return run


def s5_discretize(a_re, a_im, log_dt, b_re, b_im, c_re, c_im):
    lam_r = jnp.minimum(a_re, DT_MIN_LAMBDA)
    lam_i = a_im
    dt = jnp.exp(log_dt)[:, None]
    e = jnp.exp(lam_r * dt)
    lbr = e * jnp.cos(lam_i * dt)
    lbi = e * jnp.sin(lam_i * dt)
    den = lam_r * lam_r + lam_i * lam_i
    cf_r = ((lbr - 1.0) * lam_r + lbi * lam_i) / den
    cf_i = (lbi * lam_r - (lbr - 1.0) * lam_i) / den
    bb_r = cf_r[:, :, None] * b_re - cf_i[:, :, None] * b_im
    bb_i = cf_r[:, :, None] * b_im + cf_i[:, :, None] * b_re
    eye = jnp.eye(8, dtype=f32)

    def blk_b(m):
        return jnp.einsum('bgpc,gh->bgchp', m.reshape(8, 8, S5_P, S5_C), eye).reshape(8, 128, 512)

    def blk_c(m):
        return jnp.einsum('bgcp,gh->bgphc', m.reshape(8, 8, S5_C, S5_P), eye).reshape(8, 512, 128)

    bm = jnp.concatenate([blk_b(bb_r), blk_b(bb_i)], axis=-1)
    cm = jnp.concatenate([blk_c(c_re), -blk_c(c_im)], axis=1)
    lam = jnp.stack([lbr.reshape(8, 512), lbi.reshape(8, 512)], axis=1)
    lam = jnp.broadcast_to(lam[:, :, None, :], (8, 2, 8, 512))
    return lam, bm, cm


def _cmul(ar, ai, br, bi):
    return ar * br - ai * bi, ar * bi + ai * br


def _shift_rows(v, k, up):
    row = lax.broadcasted_iota(jnp.int32, v.shape, 0)
    if up:
        return jnp.where(row < 8 - k, pltpu.roll(v, 8 - k, 0), 0.0)
    return jnp.where(row >= k, pltpu.roll(v, k, 0), 0.0)


def _chunk_scan(S, lr, li, reverse, aux=None):
    z = jnp.zeros((8, 512), f32)
    U = 4

    def idx(i):
        return (S5_STEPS - 1 - i) if reverse else i

    def rows_of(s):
        return pl.ds(s * 8, 8) if isinstance(s, int) else pl.ds(pl.multiple_of(s * 8, 8), 8)

    def rec(xr, xi, row):
        br = S[row, 0:512]
        bi = S[row, 512:1024]
        return lr * xr - li * xi + br, lr * xi + li * xr + bi

    def step1(i, c):
        for u in range(U):
            c = rec(c[0], c[1], rows_of(idx(i * U + u)))
        return c

    er, ei = lax.fori_loop(0, S5_STEPS // U, step1, (z, z))
    ar, ai = lr, li
    for _ in range(8):
        ar, ai = _cmul(ar, ai, ar, ai)
    cr, ci = _shift_rows(er, 1, reverse), _shift_rows(ei, 1, reverse)
    for k in (1, 2, 4):
        sr, si = _shift_rows(cr, k, reverse), _shift_rows(ci, k, reverse)
        pr, pi_ = _cmul(ar, ai, sr, si)
        cr, ci = cr + pr, ci + pi_
        ar, ai = _cmul(ar, ai, ar, ai)

    if aux is None:
        def step2(i, c):
            for u in range(U):
                row = rows_of(idx(i * U + u))
                c = rec(c[0], c[1], row)
                S[row, 0:512] = c[0]
                S[row, 512:1024] = c[1]
            return c

        lax.fori_loop(0, S5_STEPS // U, step2, (cr, ci))
        return None

    def one(s, c):
        gr0, gi0, dr, di = c
        row = rows_of(s)
        gr, gi = rec(gr0, gi0, row)
        S[row, 0:512] = gr
        S[row, 512:1024] = gi
        prow = rows_of(s - 1)
        xr = aux[prow, 0:512]
        xi = aux[prow, 512:1024]
        return gr, gi, dr + gr * xr + gi * xi, di + gi * xr - gr * xi

    def step2(i, c):
        for u in range(U):
            c = one(S5_STEPS - 1 - (i * U + u), c)
        return c

    c = lax.fori_loop(0, S5_STEPS // U - 1, step2, (cr, ci, z, z))
    for s in range(U - 1, 0, -1):
        c = one(s, c)
    gr, gi, dr, di = c
    row0 = pl.ds(0, 8)
    gr, gi = rec(gr, gi, row0)
    S[row0, 0:512] = gr
    S[row0, 512:1024] = gi
    last = pl.ds((S5_STEPS - 1) * 8, 8)
    xr = _shift_rows(aux[last, 0:512], 1, False)
    xi = _shift_rows(aux[last, 512:1024], 1, False)
    dr = dr + gr * xr + gi * xi
    di = di + gi * xr - gr * xi
    return dr, di


_ROWS = 256


def _row_loop(fn):
    def body(r, c):
        fn(pl.ds(pl.multiple_of(r * _ROWS, _ROWS), _ROWS))
        return c
    lax.fori_loop(0, T // _ROWS, body, 0)


def s5_core_fwd(hn, bm, lam, cm, bg=()):
    nt = T // _ROWS

    def body(u_ref, b_ref, lam_ref, c_ref, ys_ref, S):
        lr, li = lam_ref[0], lam_ref[1]
        z = jnp.zeros((8, 512), f32)
        tile = lambda k: pl.ds(k * _ROWS, _ROWS)
        c = (z, z)
        for k in range(nt):
            S[tile(k), :] = _dot(u_ref[tile(k), :], b_ref[...])
            if k >= 1:
                c = _scan_tile(S, lr, li, k - 1, c, False, False)
        c = _scan_tile(S, lr, li, nt - 1, c, False, False)
        c = _chunk_starts(c[0], c[1], lr, li, False)
        for k in range(nt):
            c = _scan_tile(S, lr, li, k, c, False, True)
            if k >= 1:
                ys_ref[tile(k - 1), :] = _dot(S[tile(k - 1), :].astype(bf16), c_ref[...])
        ys_ref[tile(nt - 1), :] = _dot(S[tile(nt - 1), :].astype(bf16), c_ref[...])

    return _call(
        bg, body, name="s5_core_fwd", grid=(S5_SUB,),
        in_specs=[pl.BlockSpec((T, 128), lambda b: (0, b)),
                  pl.BlockSpec((None, 128, 1024), lambda b: (b, 0, 0)),
                  pl.BlockSpec((None, 4, 8, 512), lambda b: (b, 0, 0, 0)),
                  pl.BlockSpec((None, 1024, 128), lambda b: (b, 0, 0))],
        out_specs=pl.BlockSpec((T, 128), lambda b: (0, b)),
        out_shape=SDS((T, D), f32),
        scratch_shapes=[pltpu.VMEM((T, 1024), f32)],
        compiler_params=_cp(dimension_semantics=("arbitrary",)),
    )(hn, bm, lam, cm)


_SEG = _ROWS // S5_CH


def _scan_tile(S, lr, li, k, carry, reverse, store, aux=None):
    steps = range(k * _SEG, (k + 1) * _SEG)
    for s in (reversed(steps) if reverse else steps):
        row = pl.ds(s * 8, 8)
        xr, xi = carry[0], carry[1]
        nr = lr * xr - li * xi + S[row, 0:512]
        ni = lr * xi + li * xr + S[row, 512:1024]
        if store:
            S[row, 0:512] = nr
            S[row, 512:1024] = ni
        if aux is not None and s >= 1:
            prow = pl.ds((s - 1) * 8, 8)
            pr, pi_ = aux[prow, 0:512], aux[prow, 512:1024]
            carry = (nr, ni, carry[2] + nr * pr + ni * pi_, carry[3] + ni * pr - nr * pi_)
        elif aux is not None:
            carry = (nr, ni, carry[2], carry[3])
        else:
            carry = (nr, ni)
    return carry


def _chunk_starts(er, ei, lr, li, reverse):
    ar, ai = lr, li
    for _ in range(8):
        ar, ai = _cmul(ar, ai, ar, ai)
    cr, ci = _shift_rows(er, 1, reverse), _shift_rows(ei, 1, reverse)
    for k in (1, 2, 4):
        sr, si = _shift_rows(cr, k, reverse), _shift_rows(ci, k, reverse)
        pr, pi_ = _cmul(ar, ai, sr, si)
        cr, ci = cr + pr, ci + pi_
        ar, ai = _cmul(ar, ai, ar, ai)
    return cr, ci


def s5_core_bwd(hn, dy, bm, lam, cm, bg=()):
    nt = T // _ROWS

    def body(u_ref, dy_ref, b_ref, lam_ref, c_ref, du_ref, db_ref, dct_ref, dlam_ref, S1, S2):
        lr, li, lcr, lci = lam_ref[0], lam_ref[1], lam_ref[2], lam_ref[3]
        z = jnp.zeros((8, 512), f32)
        tile = lambda k: pl.ds(k * _ROWS, _ROWS)
        dyb = lambda k: dy_ref[tile(k), :].astype(bf16)

        c = (z, z)
        for k in range(nt):
            S1[tile(k), :] = _dot(u_ref[tile(k), :], b_ref[...])
            if k >= 1:
                c = _scan_tile(S1, lr, li, k - 1, c, False, False)
        c = _scan_tile(S1, lr, li, nt - 1, c, False, False)

        c = _chunk_starts(c[0], c[1], lr, li, False)
        dct_ref[...] = jnp.zeros_like(dct_ref)
        for k in range(nt):
            c = _scan_tile(S1, lr, li, k, c, False, True)
            if k >= 1:
                dct_ref[...] += _dot_tn(dyb(k - 1), S1[tile(k - 1), :].astype(bf16))
        dct_ref[...] += _dot_tn(dyb(nt - 1), S1[tile(nt - 1), :].astype(bf16))

        S2[tile(nt - 1), :] = _dot_nt(dyb(nt - 1), c_ref[...])
        c = (z, z)
        for k in range(nt - 1, -1, -1):
            if k >= 1:
                S2[tile(k - 1), :] = _dot_nt(dyb(k - 1), c_ref[...])
            c = _scan_tile(S2, lcr, lci, k, c, True, False)

        def dbu(k):
            gb = S2[tile(k), :].astype(bf16)
            db_ref[...] += _dot_tn(u_ref[tile(k), :], gb)
            du_ref[tile(k), :] = _dot_nt(gb, b_ref[...])

        c = _chunk_starts(c[0], c[1], lcr, lci, True) + (z, z)
        db_ref[...] = jnp.zeros_like(db_ref)
        for k in range(nt - 1, -1, -1):
            c = _scan_tile(S2, lcr, lci, k, c, True, True, aux=S1)
            if k + 1 < nt:
                dbu(k + 1)
        dbu(0)
        gr, gi, dr, di = c
        last = pl.ds((S5_STEPS - 1) * 8, 8)
        xr = _shift_rows(S1[last, 0:512], 1, False)
        xi = _shift_rows(S1[last, 512:1024], 1, False)
        dlam_ref[0] = dr + gr * xr + gi * xi
        dlam_ref[1] = di + gi * xr - gr * xi

    return _call(
        bg, body, name="s5_core_bwd", grid=(S5_SUB,),
        in_specs=[pl.BlockSpec((T, 128), lambda b: (0, b)),
                  pl.BlockSpec((T, 128), lambda b: (0, b)),
                  pl.BlockSpec((None, 128, 1024), lambda b: (b, 0, 0)),
                  pl.BlockSpec((None, 4, 8, 512), lambda b: (b, 0, 0, 0)),
                  pl.BlockSpec((None, 1024, 128), lambda b: (b, 0, 0))],
        out_specs=[pl.BlockSpec((T, 128), lambda b: (0, b)),
                   pl.BlockSpec((None, 128, 1024), lambda b: (b, 0, 0)),
                   pl.BlockSpec((None, 128, 1024), lambda b: (b, 0, 0)),
                   pl.BlockSpec((None, 2, 8, 512), lambda b: (b, 0, 0, 0))],
        out_shape=[SDS((T, D), f32), SDS((8, 128, 1024), f32), SDS((8, 128, 1024), f32), SDS((8, 2, 8, 512), f32)],
        scratch_shapes=[pltpu.VMEM((T, 1024), f32), pltpu.VMEM((T, 1024), f32)],
        compiler_params=_cp(dimension_semantics=("arbitrary",)),
    )(hn, dy, bm, lam, cm)


TM = 512
NT = T // TM


def _tile(n=D):
    return pl.BlockSpec((TM, n), lambda i: (i, 0))


def s5_pre(xp, g):
    def body(x_ref, g_ref, hn_ref):
        hn, _ = _rms(x_ref[...], g_ref[...])
        hn_ref[...] = hn.astype(bf16)

    return pl.pallas_call(
        body, name="s5_pre", grid=(NT,), in_specs=[_tile(), _full((1, D))], out_specs=_tile(),
        out_shape=SDS((T, D), bf16), compiler_params=_cp(dimension_semantics=("arbitrary",)),
    )(xp, g)


def _gelu_grad(y):
    c = math.sqrt(2.0 / math.pi)
    t = jnp.tanh(c * (y + 0.044715 * y * y * y))
    return 0.5 * (1.0 + t) + 0.5 * y * (1.0 - t * t) * c * (1.0 + 3.0 * 0.044715 * y * y)


def s5_post(ys, xp, g, d, wglu, bglu, bg=()):
    def body(ys_ref, x_ref, g_ref, d_ref, w_ref, b_ref, y_ref, z_ref, h_ref):
        x = x_ref[...]
        hn, _ = _rms(x, g_ref[...])
        y = ys_ref[...] + d_ref[...] * hn
        y_ref[...] = y
        yg = jax.nn.gelu(y).astype(bf16)
        for j in range(4):
            cv = slice(j * 256, (j + 1) * 256)
            cg = slice(1024 + j * 256, 1024 + (j + 1) * 256)
            val = _dot(yg, w_ref[j]) + b_ref[:, cv]
            gate = _dot(yg, w_ref[j + 4]) + b_ref[:, cg]
            z_ref[:, cv] = val
            z_ref[:, cg] = gate
            h_ref[:, cv] = x[:, cv] + val * jax.nn.sigmoid(gate)

    return _call(
        bg, body, name="s5_post", grid=(NT,),
        in_specs=[_tile(), _tile(), _full((1, D)), _full((1, D)), _full((8, D, 256)), _full((1, 2 * D))],
        out_specs=[_tile(), _tile(2 * D), _tile()],
        out_shape=[SDS((T, D), f32), SDS((T, 2 * D), f32), SDS((T, D), f32)],
        compiler_params=_cp(dimension_semantics=("arbitrary",)),
    )(ys, xp, g, d, wglu, bglu)


def s5_post_bwd(dh, y, z, wglu, bg=()):
    def body(dh_ref, y_ref, z_ref, w_ref, dy_ref, dw_ref, db_ref, acc):
        i = pl.program_id(0)

        @pl.when(i == 0)
        def _():
            acc[...] = jnp.zeros_like(acc)
            db_ref[...] = jnp.zeros_like(db_ref)

        dh_ = dh_ref[...]
        y = y_ref[...]
        yg = jax.nn.gelu(y).astype(bf16)
        dyg = jnp.zeros((TM, D), f32)
        for j in range(4):
            cv = slice(j * 256, (j + 1) * 256)
            cg = slice(1024 + j * 256, 1024 + (j + 1) * 256)
            val = z_ref[:, cv]
            sg = jax.nn.sigmoid(z_ref[:, cg])
            dval = dh_[:, cv] * sg
            dgate = dh_[:, cv] * val * sg * (1.0 - sg)
            db_ref[:, cv] += _colsum8(dval)
            db_ref[:, cg] += _colsum8(dgate)
            dvb = dval.astype(bf16)
            dgb = dgate.astype(bf16)
            acc[j] += _dot_tn(yg, dvb)
            acc[j + 4] += _dot_tn(yg, dgb)
            dyg = dyg + _dot_nt(dvb, w_ref[j]) + _dot_nt(dgb, w_ref[j + 4])
        dy_ref[...] = dyg * _gelu_grad(y)

        @pl.when(i == NT - 1)
        def _():
            dw_ref[...] = acc[...].astype(bf16)

    return _call(
        bg, body, name="s5_post_bwd", grid=(NT,),
        in_specs=[_tile(), _tile(), _tile(2 * D), _full((8, D, 256))],
        out_specs=[_tile(), _full((8, D, 256)), _full((8, 2 * D))],
        out_shape=[SDS((T, D), f32), SDS((8, D, 256), bf16), SDS((8, 2 * D), f32)],
        scratch_shapes=[pltpu.VMEM((8, D, 256), f32)],
        compiler_params=_cp(dimension_semantics=("arbitrary",)),
    )(dh, y, z, wglu)


def s5_pre_bwd(xp, g, du, dy, d, dh, bg=()):
    def body(x_ref, g_ref, du_ref, dy_ref, d_ref, dh_ref, dx_ref, dg_ref, dd_ref):
        i = pl.program_id(0)

        @pl.when(i == 0)
        def _():
            dg_ref[...] = jnp.zeros_like(dg_ref)
            dd_ref[...] = jnp.zeros_like(dd_ref)

        x = x_ref[...]
        g = g_ref[...]
        dy = dy_ref[...]
        hn, _ = _rms(x, g)
        dhn = du_ref[...] + d_ref[...] * dy
        dx, dgt = _rms_bwd(x, g, dhn)
        dx_ref[...] = dh_ref[...] + dx
        dg_ref[...] += _colsum8(dgt)
        dd_ref[...] += _colsum8(dy * hn)

    return _call(
        bg, body, name="s5_pre_bwd", grid=(NT,),
        in_specs=[_tile(), _full((1, D)), _tile(), _tile(), _full((1, D)), _tile()],
        out_specs=[_tile(), _full((8, D)), _full((8, D))],
        out_shape=[SDS((T, D), f32), SDS((8, D), f32), SDS((8, D), f32)],
        compiler_params=_cp(dimension_semantics=("arbitrary",)),
    )(xp, g, du, dy, d, dh)


TMF = 1024


def mlp_fwd(h, g, w_in, w_out, layer, bg=()):
    def body(h_ref, g_ref, wi_ref, wo_ref, hm_ref, r_ref, out_ref, acc):
        j = pl.program_id(1)

        @pl.when(j == 0)
        def _():
            hm, _ = _rms(h_ref[...], g_ref[...])
            hm_ref[...] = hm.astype(bf16)
            acc[...] = jnp.zeros_like(acc)

        a = jnp.maximum(_dot(hm_ref[...], wi_ref[...]), 0.0)
        r_ref[...] = a.astype(bf16)
        acc[...] += _dot((a * a).astype(bf16), wo_ref[...])

        @pl.when(j == NDEV - 1)
        def _():
            out_ref[...] = h_ref[...] + acc[...]

    return _call(
        bg, body, name=f"mlp_fwd{layer}", grid=(T // TMF, NDEV),
        in_specs=[pl.BlockSpec((TMF, D), lambda i, j: (i, 0)),
                  pl.BlockSpec((1, D), lambda i, j: (0, 0)),
                  pl.BlockSpec((None, D, D_FF_SHARD), lambda i, j: (j, 0, 0)),
                  pl.BlockSpec((None, D_FF_SHARD, D), lambda i, j: (j, 0, 0))],
        out_specs=[pl.BlockSpec((TMF, D), lambda i, j: (i, 0)), pl.BlockSpec((TMF, D_FF_SHARD), lambda i, j: (i, j)),
                   pl.BlockSpec((TMF, D), lambda i, j: (i, 0))],
        out_shape=[SDS((T, D), bf16), SDS((T, NDEV * D_FF_SHARD), bf16), SDS((T, D), f32)],
        scratch_shapes=[pltpu.VMEM((TMF, D), f32)],
        compiler_params=_cp(dimension_semantics=("arbitrary", "arbitrary")),
    )(h, g, w_in, w_out)


def mlp_bwd(h, hm, r, g, dout, dout_b, w_in, w_out, layer, bg=()):
    last = NDEV - 1

    def body(h_ref, hm_ref, r_ref, g_ref, do_ref, dob_ref, wi_ref, wo_ref, dh_ref, dwi_ref, dwo_ref, dg_ref,
             dhm, awi, awo):
        j = pl.program_id(0)
        i = pl.program_id(1)
        rows = pl.ds(pl.multiple_of(i * TM, TM), TM)

        @pl.when(i == 0)
        def _():
            awi[...] = jnp.zeros_like(awi)
            awo[...] = jnp.zeros_like(awo)

        dz = (_dot_nt(dob_ref[...], wo_ref[...]) * (2.0 * r_ref[...].astype(f32))).astype(bf16)
        rb = r_ref[...]
        awo[...] += _dot_tn(rb * rb, dob_ref[...])
        awi[...] += _dot_tn(hm_ref[...], dz)
        part = _dot_nt(dz, wi_ref[...])

        @pl.when(j == 0)
        def _():
            dhm[rows, :] = part

        @pl.when(j > 0)
        def _():
            dhm[rows, :] += part

        @pl.when(i == NT - 1)
        def _():
            dwi_ref[...] = awi[...].astype(bf16)
            dwo_ref[...] = awo[...].astype(bf16)

        @pl.when(j == last)
        def _():
            @pl.when(i == 0)
            def _():
                dg_ref[...] = jnp.zeros_like(dg_ref)
            dx, dgt = _rms_bwd(h_ref[...], g_ref[...], dhm[rows, :])
            dh_ref[...] = do_ref[...] + dx
            dg_ref[...] += _colsum8(dgt)

    late = lambda j, i: (jnp.where(j == last, i, 0), 0)
    return _call(
        bg, body, name=f"mlp_bwd{layer}", grid=(NDEV, NT),
        in_specs=[pl.BlockSpec((TM, D), late),
                  pl.BlockSpec((TM, D), lambda j, i: (i, 0)),
                  pl.BlockSpec((TM, D_FF_SHARD), lambda j, i: (i, j)),
                  pl.BlockSpec((1, D), lambda j, i: (0, 0)),
                  pl.BlockSpec((TM, D), late),
                  pl.BlockSpec((TM, D), lambda j, i: (i, 0)),
                  pl.BlockSpec((None, D, D_FF_SHARD), lambda j, i: (j, 0, 0)),
                  pl.BlockSpec((None, D_FF_SHARD, D), lambda j, i: (j, 0, 0))],
        out_specs=[pl.BlockSpec((TM, D), late),
                   pl.BlockSpec((None, D, D_FF_SHARD), lambda j, i: (j, 0, 0)),
                   pl.BlockSpec((None, D_FF_SHARD, D), lambda j, i: (j, 0, 0)),
                   pl.BlockSpec((8, D), lambda j, i: (0, 0))],
        out_shape=[SDS((T, D), f32), SDS((NDEV, D, D_FF_SHARD), bf16), SDS((NDEV, D_FF_SHARD, D), bf16),
                   SDS((8, D), f32)],
        scratch_shapes=[pltpu.VMEM((T, D), f32), pltpu.VMEM((D, D_FF_SHARD), f32), pltpu.VMEM((D_FF_SHARD, D), f32)],
        compiler_params=_cp(dimension_semantics=("arbitrary", "arbitrary")),
    )(h, hm, r, g, dout, dout_b, w_in, w_out)


def _spread4():
    r = lax.broadcasted_iota(jnp.int32, (256, D), 0)
    c = lax.broadcasted_iota(jnp.int32, (256, D), 1)
    return ((c // 256 == r // HEAD_DIM) & (c % HEAD_DIM == r % HEAD_DIM)).astype(bf16)


def attn_pre(h, g_kv, g_mix, wkv, bkv, spread, wq, bq):
    def body(h_ref, gkv_ref, gm_ref, wkv_ref, bkv_ref, sp_ref, wq_ref, bq_ref, kvn_ref, hn_ref, k_ref, v_ref, q_ref):
        h_ = h_ref[...]
        kvn = _rms(h_, gkv_ref[...])[0].astype(bf16)
        hn = _rms(h_, gm_ref[...])[0].astype(bf16)
        kvn_ref[...] = kvn
        hn_ref[...] = hn
        kv = (_dot(kvn, wkv_ref[...]) + bkv_ref[...]).astype(bf16)
        k_ref[...] = _dot(kv[:, :256], sp_ref[...]).astype(bf16)
        v_ref[...] = _dot(kv[:, 256:], sp_ref[...]).astype(bf16)
        q_ref[...] = (_dot(hn, wq_ref[...]) + bq_ref[...]).astype(bf16)

    return pl.pallas_call(
        body, name="attn_pre", grid=(NT,),
        in_specs=[_tile(), _full((1, D)), _full((1, D)), _full((D, 512)), _full((1, 512)), _full((256, D)),
                  _full((D, D)), _full((1, D))],
        out_specs=[_tile()] * 5,
        out_shape=[SDS((T, D), bf16)] * 5,
        compiler_params=_cp(dimension_semantics=("arbitrary",)),
    )(h, g_kv, g_mix, wkv, bkv, spread, wq, bq)


def _attn_specs():
    cur = pl.BlockSpec((TM, 256), lambda j, n: (n, j))
    prev = pl.BlockSpec((BLK, 256), lambda j, n: (jnp.maximum(n * (TM // BLK) - 1, 0), j))
    return cur, prev


def _head_mask(g):
    lane = lax.broadcasted_iota(jnp.int32, (1, 256), 1)
    return (lane >= g * HEAD_DIM) & (lane < (g + 1) * HEAD_DIM)


def _stack_heads(t):
    return jnp.concatenate([jnp.where(_head_mask(g), t, 0) for g in range(Q_PER_KV)], axis=0)


def _unstack_heads(t):
    out = jnp.where(_head_mask(0), t[0:BLK], 0.0)
    for g in range(1, Q_PER_KV):
        out = out + jnp.where(_head_mask(g), t[g * BLK:(g + 1) * BLK], 0.0)
    return out


def _attn_probs(qs, k2, sinks, first):
    rows = Q_PER_KV * BLK
    s = _dot_nt(qs, k2) * (1.0 / math.sqrt(HEAD_DIM))
    qi = jnp.bitwise_and(lax.broadcasted_iota(jnp.int32, (rows, 2 * BLK), 0), BLK - 1)
    kj = lax.broadcasted_iota(jnp.int32, (rows, 2 * BLK), 1)
    diff = qi + BLK - kj
    valid = (diff >= 0) & (diff < BLK) & (jnp.logical_not(first) | (kj >= BLK))
    s = jnp.where(valid, s, -jnp.inf)
    rb = lax.broadcasted_iota(jnp.int32, (rows, 1), 0)
    sink = jnp.where(rb < BLK, sinks[0], jnp.where(rb < 2 * BLK, sinks[1], jnp.where(rb < 3 * BLK, sinks[2], sinks[3])))
    m = jnp.maximum(jnp.max(s, axis=-1, keepdims=True), sink)
    p = jnp.exp(s - m)
    ps = jnp.exp(sink - m)
    denom = jnp.sum(p, axis=-1, keepdims=True) + ps
    return p / denom, ps / denom


def _window_blocks(b, n, kc_ref, kp_ref, vc_ref, vp_ref):
    if b == 0:
        return (jnp.concatenate([kp_ref[...], kc_ref[0:BLK, :]], axis=0),
                jnp.concatenate([vp_ref[...], vc_ref[0:BLK, :]], axis=0), n == 0)
    rows = pl.ds((b - 1) * BLK, 2 * BLK)
    return kc_ref[rows, :], vc_ref[rows, :], False


def attn_core_fwd(q, k4, v4, sinks, bg=()):
    nb = TM // BLK

    def body(sink_ref, q_ref, kc_ref, kp_ref, vc_ref, vp_ref, o_ref):
        j = pl.program_id(0)
        n = pl.program_id(1)
        sk = [sink_ref[j * Q_PER_KV + g] for g in range(Q_PER_KV)]
        for b in range(nb):
            qb = q_ref[b * BLK:(b + 1) * BLK, :]
            k2, v2, first = _window_blocks(b, n, kc_ref, kp_ref, vc_ref, vp_ref)
            a, _ = _attn_probs(_stack_heads(qb), k2, sk, first)
            o_ref[b * BLK:(b + 1) * BLK, :] = _unstack_heads(_dot(a.astype(bf16), v2)).astype(bf16)

    cur, prev = _attn_specs()
    return _call(
        bg, body, name="attn_core_fwd", grid=(N_KV, NT),
        in_specs=[pl.BlockSpec(memory_space=pltpu.SMEM), cur, cur, prev, cur, prev],
        out_specs=cur, out_shape=SDS((T, D), bf16),
        compiler_params=_cp(dimension_semantics=("arbitrary", "arbitrary")),
    )(sinks, q, k4, k4, v4, v4)


def attn_post(h, o, wo, bo):
    def body(h_ref, o_ref, w_ref, b_ref, out_ref):
        out_ref[...] = h_ref[...] + _dot(o_ref[...], w_ref[...]) + b_ref[...]

    return pl.pallas_call(
        body, name="attn_post", grid=(NT,), in_specs=[_tile(), _tile(), _full((D, D)), _full((1, D))],
        out_specs=_tile(), out_shape=SDS((T, D), f32), compiler_params=_cp(dimension_semantics=("arbitrary",)),
    )(h, o, wo, bo)


def attn_bwd_pre(dh, o, wo, bg=()):
    def body(dh_ref, o_ref, w_ref, do_ref, dw_ref, db_ref, acc):
        i = pl.program_id(0)

        @pl.when(i == 0)
        def _():
            acc[...] = jnp.zeros_like(acc)
            db_ref[...] = jnp.zeros_like(db_ref)

        dh_ = dh_ref[...]
        dhb = dh_.astype(bf16)
        do_ref[...] = _dot_nt(dhb, w_ref[...]).astype(bf16)
        acc[...] += _dot_tn(o_ref[...], dhb)
        db_ref[...] += _colsum8(dh_)

        @pl.when(i == NT - 1)
        def _():
            dw_ref[...] = acc[...].astype(bf16)

    return _call(
        bg, body, name="attn_bwd_pre", grid=(NT,), in_specs=[_tile(), _tile(), _full((D, D))],
        out_specs=[_tile(), _full((D, D)), _full((8, D))],
        out_shape=[SDS((T, D), bf16), SDS((D, D), bf16), SDS((8, D), f32)],
        scratch_shapes=[pltpu.VMEM((D, D), f32)],
        compiler_params=_cp(dimension_semantics=("arbitrary",)),
    )(dh, o, wo)


def attn_core_bwd(q, do, k4, v4, sinks, bg=()):
    nb = TM // BLK

    def body(sink_ref, q_ref, do_ref, kc_ref, kp_ref, vc_ref, vp_ref, dq_ref, dk_ref, dv_ref, ds_ref):
        j = pl.program_id(0)
        n = pl.program_id(1)

        @pl.when(n == 0)
        def _():
            dk_ref[...] = jnp.zeros_like(dk_ref)
            dv_ref[...] = jnp.zeros_like(dv_ref)
            ds_ref[...] = jnp.zeros_like(ds_ref)

        lane8 = lax.broadcasted_iota(jnp.int32, (8, 128), 1)
        row8 = lax.broadcasted_iota(jnp.int32, (8, 128), 0)
        sk = [sink_ref[j * Q_PER_KV + g] for g in range(Q_PER_KV)]
        for b in range(nb):
            qs = _stack_heads(q_ref[b * BLK:(b + 1) * BLK, :])
            dos = _stack_heads(do_ref[b * BLK:(b + 1) * BLK, :])
            k2, v2, first = _window_blocks(b, n, kc_ref, kp_ref, vc_ref, vp_ref)
            a, asink = _attn_probs(qs, k2, sk, first)
            dp = _dot_nt(dos, v2)
            dd = jnp.sum(a * dp, axis=-1, keepdims=True)
            dsc = (a * (dp - dd) * (1.0 / math.sqrt(HEAD_DIM))).astype(bf16)
            t = asink * dd
            for g in range(Q_PER_KV):
                dsink = -jnp.sum(t[g * BLK:(g + 1) * BLK], axis=0, keepdims=True)
                ds_ref[...] += jnp.where((lane8 == g) & (row8 == 0), jnp.broadcast_to(dsink, (8, 128)), 0.0)
            dq_ref[b * BLK:(b + 1) * BLK, :] = _unstack_heads(_dot(dsc, k2))
            dk2 = _dot_tn(dsc, qs)
            dv2 = _dot_tn(a.astype(bf16), dos)
            cur = pl.ds(pl.multiple_of(n * TM + b * BLK, BLK), BLK)
            dk_ref[cur, :] += dk2[BLK:, :]
            dv_ref[cur, :] += dv2[BLK:, :]
            if b == 0:
                @pl.when(n > 0)
                def _():
                    prv = pl.ds(pl.multiple_of(n * TM - BLK, BLK), BLK)
                    dk_ref[prv, :] += dk2[:BLK, :]
                    dv_ref[prv, :] += dv2[:BLK, :]
            else:
                prv = pl.ds(pl.multiple_of(n * TM + (b - 1) * BLK, BLK), BLK)
                dk_ref[prv, :] += dk2[:BLK, :]
                dv_ref[prv, :] += dv2[:BLK, :]

    cur, prev = _attn_specs()
    col = pl.BlockSpec((T, 256), lambda j, n: (0, j))
    return _call(
        bg, body, name="attn_core_bwd", grid=(N_KV, NT),
        in_specs=[pl.BlockSpec(memory_space=pltpu.SMEM), cur, cur, cur, prev, cur, prev],
        out_specs=[cur, col, col, pl.BlockSpec((None, 8, 128), lambda j, n: (j, 0, 0))],
        out_shape=[SDS((T, D), f32), SDS((T, D), f32), SDS((T, D), f32), SDS((N_KV, 8, 128), f32)],
        compiler_params=_cp(dimension_semantics=("arbitrary", "arbitrary")),
    )(sinks, q, do, k4, k4, v4, v4)


def attn_bwd_q(h, dh, dq, hn, g_mix, wq):
    def body(h_ref, dh_ref, dq_ref, hn_ref, gm_ref, wq_ref, out_ref, dwq_ref, dbq_ref, dgm_ref, aq):
        i = pl.program_id(0)

        @pl.when(i == 0)
        def _():
            aq[...] = jnp.zeros_like(aq)
            dbq_ref[...] = jnp.zeros_like(dbq_ref)
            dgm_ref[...] = jnp.zeros_like(dgm_ref)

        dq_ = dq_ref[...]
        dqb = dq_.astype(bf16)
        aq[...] += _dot_tn(hn_ref[...], dqb)
        dbq_ref[...] += _colsum8(dq_)
        dx, dg = _rms_bwd(h_ref[...], gm_ref[...], _dot_nt(dqb, wq_ref[...]))
        out_ref[...] = dh_ref[...] + dx
        dgm_ref[...] += _colsum8(dg)

        @pl.when(i == NT - 1)
        def _():
            dwq_ref[...] = aq[...].astype(bf16)

    vec = _full((8, D))
    mat = _full((D, D))
    return pl.pallas_call(
        body, name="attn_bwd_q", grid=(NT,),
        in_specs=[_tile()] * 4 + [_full((1, D)), mat],
        out_specs=[_tile(), mat, vec, vec],
        out_shape=[SDS((T, D), f32), SDS((D, D), bf16), SDS((8, D), f32), SDS((8, D), f32)],
        scratch_shapes=[pltpu.VMEM((D, D), f32)],
        compiler_params=_cp(dimension_semantics=("arbitrary",)),
    )(h, dh, dq, hn, g_mix, wq)


def attn_bwd_kv(h, dh, dk4, dv4, kvn, g_kv, wkv, spread):
    def body(h_ref, dh_ref, dk_ref, dv_ref, kvn_ref, gkv_ref, wkv_ref, sp_ref, out_ref, outb_ref, dw_ref, db_ref,
             dgkv_ref, acc):
        i = pl.program_id(0)

        @pl.when(i == 0)
        def _():
            for r in (acc, db_ref, dgkv_ref):
                r[...] = jnp.zeros_like(r)

        dkv = jnp.concatenate([_dot_nt(dk_ref[...].astype(bf16), sp_ref[...]),
                               _dot_nt(dv_ref[...].astype(bf16), sp_ref[...])], axis=1)
        dkvb = dkv.astype(bf16)
        acc[...] += _dot_tn(kvn_ref[...], dkvb)
        db_ref[...] += _colsum8(dkv)
        dx, dg = _rms_bwd(h_ref[...], gkv_ref[...], _dot_nt(dkvb, wkv_ref[...]))
        out = dh_ref[...] + dx
        out_ref[...] = out
        outb_ref[...] = out.astype(bf16)
        dgkv_ref[...] += _colsum8(dg)

        @pl.when(i == NT - 1)
        def _():
            dw_ref[...] = acc[...].astype(bf16)

    return pl.pallas_call(
        body, name="attn_bwd_kv", grid=(NT,),
        in_specs=[_tile()] * 5 + [_full((1, D)), _full((D, 512)), _full((256, D))],
        out_specs=[_tile(), _tile(), _full((D, 512)), _full((8, 512)), _full((8, D))],
        out_shape=[SDS((T, D), f32), SDS((T, D), bf16), SDS((D, 512), bf16), SDS((8, 512), f32), SDS((8, D), f32)],
        scratch_shapes=[pltpu.VMEM((D, 512), f32)],
        compiler_params=_cp(dimension_semantics=("arbitrary",)),
    )(h, dh, dk4, dv4, kvn, g_kv, wkv, spread)


def final_loss(h, g, target):
    def body(h_ref, g_ref, t_ref, loss_ref, dh_ref, dhb_ref, dg_ref):
        i = pl.program_id(0)

        @pl.when(i == 0)
        def _():
            loss_ref[...] = jnp.zeros_like(loss_ref)
            dg_ref[...] = jnp.zeros_like(dg_ref)

        h_ = h_ref[...]
        g_ = g_ref[...]
        y, _ = _rms(h_, g_)
        diff = y - t_ref[...]
        per_tok = jnp.mean(diff * diff, axis=-1, keepdims=True)
        tot = 0.5 * jnp.sum(per_tok, axis=0, keepdims=True)
        lane = lax.broadcasted_iota(jnp.int32, (8, 128), 1)
        row = lax.broadcasted_iota(jnp.int32, (8, 128), 0)
        loss_ref[...] += jnp.where((lane == 0) & (row == 0), jnp.broadcast_to(tot, (8, 128)), 0.0)
        dx, dgt = _rms_bwd(h_, g_, diff * (1.0 / D))
        dh_ref[...] = dx
        dhb_ref[...] = dx.astype(bf16)
        dg_ref[...] += _colsum8(dgt)

    return pl.pallas_call(
        body, name="final_loss", grid=(NT,), in_specs=[_tile(), _full((1, D)), _tile()],
        out_specs=[_full((8, 128)), _tile(), _tile(), _full((8, D))],
        out_shape=[SDS((8, 128), f32), SDS((T, D), f32), SDS((T, D), bf16), SDS((8, D), f32)],
        compiler_params=_cp(dimension_semantics=("arbitrary",)),
    )(h, g, target)


def _to_chunked(a):
    return a.reshape(S5_CH, S5_STEPS, a.shape[-1]).transpose(1, 0, 2).reshape(T, a.shape[-1])


def _from_chunked(a):
    return a.reshape(S5_STEPS, S5_CH, a.shape[-1]).transpose(1, 0, 2).reshape(T, a.shape[-1])


def _rep4(w):
    return jnp.broadcast_to(w.reshape(w.shape[0], N_KV, 1, HEAD_DIM), (w.shape[0], N_KV, Q_PER_KV, HEAD_DIM)).reshape(
        w.shape[0], N_KV * Q_PER_KV * HEAD_DIM)


def _fold4(w):
    return w.reshape(w.shape[0], N_KV, Q_PER_KV, HEAD_DIM).sum(axis=2).reshape(w.shape[0], N_KV * HEAD_DIM)


def fwd_bwd(x, target, p, shards, opt, core, chip):
    row = lambda v: v.reshape(1, -1)
    (lam, bm, cm), prep_vjp = jax.vjp(s5_discretize, p["s5_a_re"][0], p["s5_a_im"][0], p["s5_log_dt"][0],
                                      p["s5_b_re"][0], p["s5_b_im"][0], p["s5_c_re"][0], p["s5_c_im"][0])
    bmb, cmb = bm.astype(bf16), cm.astype(bf16)
    lam = jnp.concatenate([lam, lam * jnp.array([1.0, -1.0], f32).reshape(1, 2, 1, 1)], axis=1)
    g_mix0, g_mix1 = row(p["norm_mix"][0]), row(p["norm_mix"][1])
    g_mlp0, g_mlp1 = row(p["norm_mlp"][0]), row(p["norm_mlp"][1])
    g_kv, g_fin = row(p["norm_kv"]), row(p["norm_final"])
    bq, bo = p["b_q"], p["b_o"]
    bkv = row(p["b_kv"])
    spread = _spread4()
    sinks = p["sinks"].reshape(16)

    def reduce_pairs(names, bg):
        return [add_pairs(g, r, core, f"add_pairs_{n}") for n, g, r in zip(names, bg.arrs, bg.result)]

    wglu, gvec, win0, wout0 = sc_gather(
        [shards["s5_w_glu"], shards["vecs"], shards["w_in0"], shards["w_out0"]], 3, "sc_gather_layer0")
    wkv, wq, wo, win1 = sc_gather([shards["w_kv"], shards["w_q"], shards["w_o"], shards["w_in1"]], 4, "sc_gather_attn")
    wout1, = sc_gather([shards["w_out1"]], 5, "sc_gather_w_out1")
    xp = _to_chunked(x)
    hn0 = s5_pre(xp, g_mix0)
    ys = s5_core_fwd(hn0, bmb, lam, cmb)
    d_skip = gvec[:, 0, :128].reshape(1, D)
    bglu = gvec[:, 0, 128:].reshape(1, 2 * D)
    y, z, h1 = s5_post(ys, xp, g_mix0, d_skip, wglu, bglu)
    hm0, r0, h2p = mlp_fwd(h1, g_mlp0, win0, wout0, 0)
    wkv, wq, wo = wkv.reshape(D, 512), wq.reshape(D, D), wo.reshape(D, D)
    h2 = _from_chunked(h2p)
    kvn, hn1, k4, v4, q = attn_pre(h2, g_kv, g_mix1, wkv, bkv, spread, wq, bq)
    o = attn_core_fwd(q, k4, v4, sinks)
    h3 = attn_post(h2, o, wo, bo)
    hm1, r1, h4 = mlp_fwd(h3, g_mlp1, win1, wout1, 1)
    loss, dh4, dh4b, dg_fin = final_loss(h4, g_fin, target)

    def pair_sums(names, grads, cid, before):
        r1 = sc_comm(BgPair(grads), cid, "sc_pair_" + names[0])
        parts = [add_pairs(g, r, core, f"add_pairs_{n}") for n, g, r in zip(names, grads, r1)]
        before, parts = lax.optimization_barrier((before, parts))
        return before, parts

    def across_chips(names, parts, cid):
        return list(zip(parts, sc_comm(BgChips(parts), cid, "sc_chips_" + names[0])))

    dh3, dwin1, dwout1, dg_mlp1 = mlp_bwd(h3, hm1, r1, g_mlp1, dh4, dh4b, win1, wout1, 1)
    do, dwo, dbo = attn_bwd_pre(dh3, o, wo)
    do, parts = pair_sums(["w_in1", "w_out1"], [dwin1, dwout1], 6, do)
    rs_in1, rs_out1 = across_chips(["w_in1", "w_out1"], parts, 7)
    dq, dk4, dv4, dsink = attn_core_bwd(q, do, k4, v4, sinks)
    dh2, dwq, dbq, dg_mix1 = attn_bwd_q(h2, dh3, dq, hn1, g_mix1, wq)
    dh2, dh2b, dwkv, dbkv, dg_kv = attn_bwd_kv(h2, dh2, dk4, dv4, kvn, g_kv, wkv, spread)
    dh2p, dh2pb = _to_chunked(dh2), _to_chunked(dh2b)
    big = {}
    a_in1 = adam_big(*opt["w_mlp_in"], *rs_in1, chip, "adam_w_mlp_in1", layer=1)
    a_out1 = adam_big(*opt["w_mlp_out"], *rs_out1, chip, "adam_w_mlp_out1", layer=1)
    dh2p, a_in1, a_out1 = lax.optimization_barrier((dh2p, a_in1, a_out1))
    names = ["w_kv", "w_q", "w_o"]
    dh2p, parts = pair_sums(names, [dwkv.reshape(NDEV, 128, 512), dwq.reshape(NDEV, 128, D),
                                    dwo.reshape(NDEV, 128, D)], 8, dh2p)
    rs_attn = across_chips(names, parts, 9)
    dh1, dwin0, dwout0, dg_mlp0 = mlp_bwd(h1, hm0, r0, g_mlp0, dh2p, dh2pb, win0, wout0, 0)
    a_attn = [adam_big(*opt[n], *rs, chip, f"adam_{n}") for n, rs in zip(names, rs_attn)]
    dh1, a_attn = lax.optimization_barrier((dh1, a_attn))
    big.update(zip(names, a_attn))
    dy, dwglu, dbglu = s5_post_bwd(dh1, y, z, wglu)
    dy, parts = pair_sums(["w_in0", "w_out0"], [dwin0, dwout0], 10, dy)
    rs_in0, rs_out0 = across_chips(["w_in0", "w_out0"], parts, 11)
    du, dbm, dcmt, dlam = s5_core_bwd(hn0, dy, bmb, lam, cmb)
    du, parts = pair_sums(["s5_w_glu"], [dwglu], 12, du)
    rs_glu, = across_chips(["s5_w_glu"], parts, 13)
    dxp, dg_mix0, dd = s5_pre_bwd(xp, g_mix0, du, dy, d_skip, dh1)
    big["w_mlp_in"] = adam_big(*opt["w_mlp_in"], *rs_in0, chip, "adam_w_mlp_in0", layer=0, prev=a_in1)
    big["w_mlp_out"] = adam_big(*opt["w_mlp_out"], *rs_out0, chip, "adam_w_mlp_out0", layer=0, prev=a_out1)
    big["s5_w_glu"] = adam_big(*opt["s5_w_glu"], *rs_glu, chip, "adam_s5_w_glu")
    grad_x = _from_chunked(dxp)
    da_re, da_im, dlog_dt, db_re, db_im, dc_re, dc_im = prep_vjp((dlam, dbm, dcmt.transpose(0, 2, 1)))

    def lanes(v_):
        v_ = v_.reshape(1, -1)
        return jnp.pad(v_, ((0, 0), (0, D - v_.shape[1])))

    small_vec = jnp.concatenate([
        dg_mix0[0:1], dg_mix1[0:1], dg_mlp0[0:1], dg_mlp1[0:1], dg_kv[0:1], dg_fin[0:1], dd[0:1], dbq[0:1], dbo[0:1],
        dbglu[0:1].reshape(2, D), lanes(dbkv[0:1]),
        lanes(dsink[:, 0, :Q_PER_KV]), lanes(dlog_dt), lanes(loss[0:1, 0:1]), jnp.zeros((1, D), f32)], axis=0)
    small_mat = jnp.concatenate([
        da_re.reshape(4, D), da_im.reshape(4, D),
        db_re.transpose(0, 2, 1).reshape(64, D), db_im.transpose(0, 2, 1).reshape(64, D),
        dc_re.reshape(64, D), dc_im.reshape(64, D)], axis=0)
    small_vec, small_mat, big["w_mlp_in"], big["w_mlp_out"] = lax.optimization_barrier(
        (small_vec, small_mat, big["w_mlp_in"], big["w_mlp_out"]))
    return loss, grad_x, (small_vec, small_mat), big


_ANY = pl.BlockSpec(memory_space=pl.ANY)


def _pos():
    return lax.axis_index("x"), lax.axis_index("y"), lax.axis_index("c")


def _other_chips(x, y):
    return [(1 - x, y), (x, 1 - y), (1 - x, 1 - y)]


def all_gather(arrs):
    n = len(arrs)

    def body(*refs):
        ins, outs = refs[:n], refs[n:2 * n]
        send_sems, recv_sems, local_sems = refs[2 * n:]
        x, y, c = _pos()
        me, sib = (x, y, c), (x, y, 1 - c)
        chips = _other_chips(x, y)

        def copy(a, k, block, to, src=None):
            dst = outs[a].at[4 * block[0] + 2 * block[1] + block[2]]
            return pltpu.make_async_remote_copy(
                src_ref=dst if src is None else src, dst_ref=dst, send_sem=send_sems.at[a, k],
                recv_sem=recv_sems.at[a, k], device_id=to, device_id_type=MESH)

        mine = [pltpu.make_async_copy(ins[a], outs[a].at[4 * x + 2 * y + c], local_sems.at[a]) for a in range(n)]
        for cp in mine:
            cp.start()
        first = []
        for a in range(n):
            first.append(copy(a, 0, me, sib, src=ins[a]))
            first += [copy(a, 1 + j, me, (*chip, c), src=ins[a]) for j, chip in enumerate(chips)]
        for cp in first:
            cp.start()
        passed = []
        for j, chip in enumerate(chips):
            for a in range(n):
                copy(a, 1 + j, (*chip, c), me).wait_recv()
                cp = copy(a, 4 + j, (*chip, c), sib)
                cp.start()
                passed.append(cp)
        for a in range(n):
            copy(a, 0, sib, me).wait_recv()
            for j, chip in enumerate(chips):
                copy(a, 4 + j, (*chip, 1 - c), me).wait_recv()
        for cp in first + passed:
            cp.wait_send()
        for cp in mine:
            cp.wait()

    return pl.pallas_call(
        body, name="all_gather", in_specs=[_ANY] * n, out_specs=[_ANY] * n,
        out_shape=[SDS((NDEV,) + a.shape, a.dtype) for a in arrs],
        scratch_shapes=[pltpu.SemaphoreType.DMA((n, 7)), pltpu.SemaphoreType.DMA((n, 7)),
                        pltpu.SemaphoreType.DMA((n,))],
    )(*arrs)


def rs_pair(grads):
    n = len(grads)

    def body(*refs):
        ins, outs = refs[:n], refs[n:2 * n]
        send_sems, recv_sems = refs[2 * n:]
        x, y, c = _pos()
        cps = []
        for a in range(n):
            for k in range(4):
                cps.append(pltpu.make_async_remote_copy(
                    src_ref=ins[a].at[2 * k + 1 - c], dst_ref=outs[a].at[k], send_sem=send_sems.at[a, k],
                    recv_sem=recv_sems.at[a, k], device_id=(x, y, 1 - c), device_id_type=MESH))
        for cp in cps:
            cp.start()
        for cp in cps:
            cp.wait_recv()
        for cp in cps:
            cp.wait_send()

    return pl.pallas_call(
        body, name="rs_pair", in_specs=[_ANY] * n, out_specs=[_ANY] * n,
        out_shape=[SDS((4,) + g.shape[1:], g.dtype) for g in grads],
        scratch_shapes=[pltpu.SemaphoreType.DMA((n, 4)), pltpu.SemaphoreType.DMA((n, 4))],
    )(*grads)


def rs_chips(parts):
    n = len(parts)

    def body(*refs):
        ins, outs = refs[:n], refs[n:2 * n]
        send_sems, recv_sems = refs[2 * n:]
        x, y, c = _pos()
        cps = []
        for a in range(n):
            for r, (px, py) in enumerate(_other_chips(x, y)):
                cps.append(pltpu.make_async_remote_copy(
                    src_ref=ins[a].at[2 * px + py], dst_ref=outs[a].at[r], send_sem=send_sems.at[a, r],
                    recv_sem=recv_sems.at[a, r], device_id=(px, py, c), device_id_type=MESH))
        for cp in cps:
            cp.start()
        for cp in cps:
            cp.wait_recv()
        for cp in cps:
            cp.wait_send()

    return pl.pallas_call(
        body, name="rs_chips", in_specs=[_ANY] * n, out_specs=[_ANY] * n,
        out_shape=[SDS((3,) + g.shape[1:], g.dtype) for g in parts],
        scratch_shapes=[pltpu.SemaphoreType.DMA((n, 3)), pltpu.SemaphoreType.DMA((n, 3))],
    )(*parts)


def _row_tile(r, c):
    return min(r, max(8, (512 * 1024) // c))


def add_pairs(g, r1, core, name):
    _, R, C = g.shape
    tr = _row_tile(R, C)

    def body(core_ref, g_ref, r_ref, o_ref):
        o_ref[...] = (g_ref[...].astype(f32) + r_ref[...].astype(f32)).astype(bf16)

    return pl.pallas_call(
        body, name=name, out_shape=SDS((4, R, C), bf16),
        grid_spec=pltpu.PrefetchScalarGridSpec(
            num_scalar_prefetch=1, grid=(4, R // tr),
            in_specs=[pl.BlockSpec((None, tr, C), lambda k, i, core: (2 * k + core[0], i, 0)),
                      pl.BlockSpec((None, tr, C), lambda k, i, core: (k, i, 0))],
            out_specs=pl.BlockSpec((None, tr, C), lambda k, i, core: (k, i, 0))),
        compiler_params=_cp(dimension_semantics=("arbitrary", "arbitrary")),
    )(core, g, r1)


def _adamw(w, g, m, v):
    m = ADAM_B1 * m + (1.0 - ADAM_B1) * g
    v = ADAM_B2 * v + (1.0 - ADAM_B2) * (g * g)
    m_hat = m / (1.0 - ADAM_B1 ** ADAM_STEP)
    v_hat = v / (1.0 - ADAM_B2 ** ADAM_STEP)
    delta = -ADAM_LR * (m_hat / (jnp.sqrt(v_hat) + ADAM_EPS) + ADAM_WD * w)
    return delta, m, v


def adam_big(w, m, v, part, r2, chip, name, layer=0, prev=None):
    L, R, C = w.shape
    tr = _row_tile(R, C)

    def body(chip_ref, w_ref, m_ref, v_ref, p_ref, r_ref, *rest):
        g_out, d_out, m_out, v_out = rest[-4:]
        g = p_ref[...].astype(f32) + r_ref[0].astype(f32) + r_ref[1].astype(f32) + r_ref[2].astype(f32)
        d, m_, v_ = _adamw(w_ref[...], g, m_ref[...], v_ref[...])
        g_out[...] = g
        d_out[...] = d
        m_out[...] = m_
        v_out[...] = v_

    blk = pl.BlockSpec((None, tr, C), lambda i, chip: (layer, i, 0))
    extra = [] if prev is None else list(prev)
    return pl.pallas_call(
        body, name=name, out_shape=[SDS((L, R, C), f32)] * 4,
        grid_spec=pltpu.PrefetchScalarGridSpec(
            num_scalar_prefetch=1, grid=(R // tr,),
            in_specs=[blk, blk, blk,
                      pl.BlockSpec((None, tr, C), lambda i, chip: (chip[0], i, 0)),
                      pl.BlockSpec((3, tr, C), lambda i, chip: (0, i, 0))] + [_ANY] * len(extra),
            out_specs=[blk] * 4),
        input_output_aliases={6 + k: k for k in range(len(extra))},
        compiler_params=_cp(dimension_semantics=("arbitrary",)),
    )(chip, w, m, v, part, r2, *extra)


def allreduce_small(buf, chips=None):
    shp = buf.shape
    half = (shp[0] // 16) * 8
    parts = (pl.ds(0, half), pl.ds(half, shp[0] - half))
    n_c = 0 if chips is None else len(chips.arrs)

    def body(in_ref, *refs):
        c_in, out_ref, c_out = refs[:n_c], refs[n_c], refs[n_c + 1:2 * n_c + 1]
        acc1, acc2, r0, r1, r2, send_sems, recv_sems = refs[2 * n_c + 1:2 * n_c + 8]
        c_sems = refs[2 * n_c + 8:]
        if chips is not None:
            chips.start(c_in, c_out, c_sems)
        x, y, c = _pos()
        across = [(1 - x, y, c), (x, 1 - y, c)]

        def exchange(src, rcv, dst, copies):
            cps = [pltpu.make_async_remote_copy(
                src_ref=src.at[rows], dst_ref=rcv.at[rows], send_sem=send_sems.at[k], recv_sem=recv_sems.at[k],
                device_id=peer, device_id_type=MESH) for k, rows, peer in copies]
            for cp in cps:
                cp.start()
            for cp in cps:
                cp.wait()
            dst[...] = src[...] + rcv[...]

        exchange(in_ref, r0, acc1, [(0, pl.ds(0, shp[0]), (x, y, 1 - c))])
        exchange(acc1, r1, acc2, [(1, parts[0], across[0]), (2, parts[1], across[1])])
        exchange(acc2, r2, out_ref, [(3, parts[0], across[1]), (4, parts[1], across[0])])
        if chips is not None:
            chips.finish(c_in, c_out, c_sems)

    vm = pl.BlockSpec(memory_space=pltpu.VMEM)
    res = pl.pallas_call(
        body, name="allreduce_small", in_specs=[vm] + [_ANY] * n_c, out_specs=[vm] + [_ANY] * n_c,
        out_shape=[SDS(shp, f32)] + ([] if chips is None else chips.out_shape),
        scratch_shapes=[pltpu.VMEM(shp, f32)] * 5 + [pltpu.SemaphoreType.DMA((5,)), pltpu.SemaphoreType.DMA((5,))]
        + ([] if chips is None else chips.scratch),
    )(buf, *([] if chips is None else chips.arrs))
    if chips is not None:
        chips.result = list(res[1:])
    return res[0]


SMALL_ROWS = {'norm_mix': (0, 2, D), 'norm_mlp': (2, 2, D), 'norm_kv': (4, 1, D), 'norm_final': (5, 1, D),
              's5_d': (6, 1, D), 'b_q': (7, 1, D), 'b_o': (8, 1, D), 's5_b_glu': (9, 2, D), 'b_kv': (11, 1, 512),
              'sinks': (12, 1, 16), 's5_log_dt': (13, 1, 64), 's5_a_re': (16, 4, D), 's5_a_im': (20, 4, D),
              's5_b_re': (24, 64, D), 's5_b_im': (88, 64, D), 's5_c_re': (152, 64, D), 's5_c_im': (216, 64, D)}
LOSS_ROW = 14
VEC_ROWS = 16


def sum_slots(g8):
    def body(g_ref, o_ref):
        acc = g_ref[0]
        for k in range(1, NDEV):
            acc = acc + g_ref[k]
        o_ref[...] = acc

    return pl.pallas_call(body, name="sum_slots", out_shape=SDS(g8.shape[1:], f32), compiler_params=_cp())(g8)
ROW_PARAMS = ['norm_mix', 'norm_mlp', 'norm_kv', 'norm_final', 'b_q', 'b_o', 'b_kv', 'sinks', 's5_log_dt']
SHARD_PARAMS = ['s5_d', 's5_b_glu']
S5_PARAMS = ['s5_a_re', 's5_a_im', 's5_b_re', 's5_b_im', 's5_c_re', 's5_c_im']


def adam_small(dev, gsum, s5_grads, w, m, v):
    names = ROW_PARAMS + SHARD_PARAMS + S5_PARAMS
    n_g = len(ROW_PARAMS) + len(SHARD_PARAMS)

    def body(dev_ref, gs_ref, *refs):
        pos = [0]

        def take(k):
            r = refs[pos[0]:pos[0] + k]
            pos[0] += k
            return r

        g5 = take(len(S5_PARAMS))
        wr, mr, vr = take(len(names)), take(len(names)), take(len(names))
        g_out = take(n_g)
        d_out, m_out, v_out = take(len(names)), take(len(names)), take(len(names))
        dv = dev_ref[0]
        for i, n in enumerate(names):
            if n in S5_PARAMS:
                g = g5[S5_PARAMS.index(n)][...]
            elif n in SHARD_PARAMS:
                r0, _, _ = SMALL_ROWS[n]
                ln = wr[i].shape[1]
                g = jnp.zeros((1, ln), f32)
                for k in range(NDEV):
                    off = k * ln
                    piece = gs_ref[r0 + off // D:r0 + off // D + 1, off % D:off % D + ln]
                    g = g + jnp.where(dv == k, piece, 0.0)
                g_out[i][...] = g
            else:
                r0, nr, nl = SMALL_ROWS[n]
                g = gs_ref[r0:r0 + nr, 0:nl]
                g_out[i][...] = g
            d, m_, v_ = _adamw(wr[i][...], g, mr[i][...], vr[i][...])
            d_out[i][...] = d
            m_out[i][...] = m_
            v_out[i][...] = v_

    vm = pl.BlockSpec(memory_space=pltpu.VMEM)
    ins = [s5_grads[n] for n in S5_PARAMS] + [d[n] for d in (w, m, v) for n in names]
    shapes = [SDS(w[n].shape, f32) for n in names]
    res = pl.pallas_call(
        body, name="adam_small", in_specs=[pl.BlockSpec(memory_space=pltpu.SMEM)] + [vm] * (1 + len(ins)),
        out_specs=[vm] * (n_g + 3 * len(names)), out_shape=shapes[:n_g] + shapes * 3,
        compiler_params=_cp(),
    )(dev, gsum, *ins)
    g_o = dict(zip(names[:n_g], res[:n_g]))
    rest = res[n_g:]
    k = len(names)
    return g_o, dict(zip(names, rest[:k])), dict(zip(names, rest[k:2 * k])), dict(zip(names, rest[2 * k:]))


WEIGHTS = ['norm_mix', 'norm_mlp', 'norm_kv', 'norm_final', 's5_a_re', 's5_a_im', 's5_log_dt', 's5_b_re', 's5_b_im',
           's5_c_re', 's5_c_im', 's5_d', 's5_w_glu', 's5_b_glu', 'w_kv', 'b_kv', 'w_q', 'b_q', 'sinks', 'w_o', 'b_o',
           'w_mlp_in', 'w_mlp_out']
BIG = ['s5_w_glu', 'w_kv', 'w_q', 'w_o', 'w_mlp_in', 'w_mlp_out']
BIG_2D = {'s5_w_glu': (D, 256), 'w_kv': (128, 512), 'w_q': (128, D), 'w_o': (128, D), 'w_mlp_in': (2 * D, 512),
          'w_mlp_out': (2 * 512, D)}
SHARDED_SMALL = {'s5_d': D, 's5_b_glu': 2 * D}
SMALL = [n for n in WEIGHTS if n not in BIG]
SMALL_SIZE = {'norm_mix': 2 * D, 'norm_mlp': 2 * D, 'norm_kv': D, 'norm_final': D, 's5_a_re': 4096, 's5_a_im': 4096,
              's5_log_dt': 64, 's5_b_re': 65536, 's5_b_im': 65536, 's5_c_re': 65536, 's5_c_im': 65536, 's5_d': D,
              's5_b_glu': 2 * D, 'b_kv': 512, 'b_q': D, 'sinks': 16, 'b_o': D}


def _pack(vals):
    parts = []
    for n in SMALL:
        v = vals[n].reshape(-1).astype(f32)
        parts.append(jnp.pad(v, (0, (-v.shape[0]) % 128)))
    flat = jnp.concatenate(parts)
    flat = jnp.pad(flat, (0, (-flat.shape[0]) % 1024))
    return flat.reshape(-1, 128)


def _unpack(buf):
    flat = buf.reshape(-1)
    out, off = {}, 0
    for n in SMALL:
        sz = SMALL_SIZE[n]
        out[n] = flat[off:off + sz]
        off += sz + (-sz) % 128
    return out


def kernel(x, norm_mix, norm_mlp, norm_kv, norm_final, s5_a_re, s5_a_im, s5_log_dt, s5_b_re, s5_b_im, s5_c_re, s5_c_im, s5_d, s5_w_glu, s5_b_glu, w_kv, b_kv, w_q, b_q, sinks, w_o, b_o, w_mlp_in, w_mlp_out, loss_target, m_norm_mix, m_norm_mlp, m_norm_kv, m_norm_final, m_s5_a_re, m_s5_a_im, m_s5_log_dt, m_s5_b_re, m_s5_b_im, m_s5_c_re, m_s5_c_im, m_s5_d, m_s5_w_glu, m_s5_b_glu, m_w_kv, m_b_kv, m_w_q, m_b_q, m_sinks, m_w_o, m_b_o, m_w_mlp_in, m_w_mlp_out, v_norm_mix, v_norm_mlp, v_norm_kv, v_norm_final, v_s5_a_re, v_s5_a_im, v_s5_log_dt, v_s5_b_re, v_s5_b_im, v_s5_c_re, v_s5_c_im, v_s5_d, v_s5_w_glu, v_s5_b_glu, v_w_kv, v_b_kv, v_w_q, v_b_q, v_sinks, v_w_o, v_b_o, v_w_mlp_in, v_w_mlp_out):
    w = dict(norm_mix=norm_mix, norm_mlp=norm_mlp, norm_kv=norm_kv, norm_final=norm_final, s5_a_re=s5_a_re,
             s5_a_im=s5_a_im, s5_log_dt=s5_log_dt, s5_b_re=s5_b_re, s5_b_im=s5_b_im, s5_c_re=s5_c_re, s5_c_im=s5_c_im,
             s5_d=s5_d, s5_w_glu=s5_w_glu, s5_b_glu=s5_b_glu, w_kv=w_kv, b_kv=b_kv, w_q=w_q, b_q=b_q, sinks=sinks,
             w_o=w_o, b_o=b_o, w_mlp_in=w_mlp_in, w_mlp_out=w_mlp_out)
    m = dict(norm_mix=m_norm_mix, norm_mlp=m_norm_mlp, norm_kv=m_norm_kv, norm_final=m_norm_final, s5_a_re=m_s5_a_re,
             s5_a_im=m_s5_a_im, s5_log_dt=m_s5_log_dt, s5_b_re=m_s5_b_re, s5_b_im=m_s5_b_im, s5_c_re=m_s5_c_re,
             s5_c_im=m_s5_c_im, s5_d=m_s5_d, s5_w_glu=m_s5_w_glu, s5_b_glu=m_s5_b_glu, w_kv=m_w_kv, b_kv=m_b_kv,
             w_q=m_w_q, b_q=m_b_q, sinks=m_sinks, w_o=m_w_o, b_o=m_b_o, w_mlp_in=m_w_mlp_in, w_mlp_out=m_w_mlp_out)
    v = dict(norm_mix=v_norm_mix, norm_mlp=v_norm_mlp, norm_kv=v_norm_kv, norm_final=v_norm_final, s5_a_re=v_s5_a_re,
             s5_a_im=v_s5_a_im, s5_log_dt=v_s5_log_dt, s5_b_re=v_s5_b_re, s5_b_im=v_s5_b_im, s5_c_re=v_s5_c_re,
             s5_c_im=v_s5_c_im, s5_d=v_s5_d, s5_w_glu=v_s5_w_glu, s5_b_glu=v_s5_b_glu, w_kv=v_w_kv, b_kv=v_b_kv,
             w_q=v_w_q, b_q=v_b_q, sinks=v_sinks, w_o=v_w_o, b_o=v_b_o, w_mlp_in=v_w_mlp_in, w_mlp_out=v_w_mlp_out)
    xi, yi, ci = _pos()
    dev = 4 * xi + 2 * yi + ci
    core = ci.reshape(1).astype(jnp.int32)
    chip = (2 * xi + yi).reshape(1).astype(jnp.int32)

    shards = {
        "s5_w_glu": s5_w_glu[0].astype(bf16), "w_kv": w_kv.astype(bf16), "w_q": w_q[0].astype(bf16),
        "w_o": w_o[0].astype(bf16), "w_in0": w_mlp_in[0].astype(bf16), "w_in1": w_mlp_in[1].astype(bf16),
        "w_out0": w_mlp_out[0].astype(bf16), "w_out1": w_mlp_out[1].astype(bf16),
        "vecs": jnp.broadcast_to(jnp.concatenate([s5_d, s5_b_glu], axis=1), (8, 384)),
    }
    as3d = lambda a, n: a if a.ndim == 3 and a.shape[0] == 2 else a.reshape((1,) + BIG_2D[n])
    opt = {n: (as3d(w[n], n), as3d(m[n], n), as3d(v[n], n)) for n in BIG}
    _, grad_x, grads, big = fwd_bwd(x[0], loss_target[0], {n: w[n] for n in SMALL}, shards, opt, core, chip)

    gsum = allreduce_small(grads[0])
    gmat = sum_slots(sc_comm(BgGather([grads[1]]), 14, "sc_gather_small")[0])

    out_g, out_d, out_m, out_v = {}, {}, {}, {}
    for n in BIG:
        out_g[n], out_d[n], out_m[n], out_v[n] = [r.reshape(w[n].shape) for r in big[n]]

    loss = gsum[LOSS_ROW, 0]
    swapped = ("s5_b_re", "s5_b_im")
    swap = lambda a: a.transpose(0, 1, 3, 2)

    def kernel_side(d):
        d = {n: (d[n].reshape(1, -1) if d[n].ndim == 1 else d[n]) for n in SMALL}
        d.update({n: swap(d[n]) for n in swapped})
        return d

    s5_g = {}
    for n in S5_PARAMS:
        r0, nr, _ = SMALL_ROWS[n]
        s5_g[n] = gmat[r0 - VEC_ROWS:r0 - VEC_ROWS + nr].reshape((1, 64, 16, 64) if n in swapped else w[n].shape)
        out_g[n] = s5_g[n]
    g_s, d_s, m_s, v_s = adam_small(dev.reshape(1).astype(jnp.int32), gsum, s5_g, kernel_side(w), kernel_side(m),
                                    kernel_side(v))
    for src, dst in ((g_s, out_g), (d_s, out_d), (m_s, out_m), (v_s, out_v)):
        dst.update(src)
    for dst in (out_g, out_d, out_m, out_v):
        for n in SMALL:
            dst[n] = (swap(dst[n]) if n in swapped else dst[n]).reshape(w[n].shape)

    return (loss, grad_x[None], *[out_g[n] for n in WEIGHTS], *[out_d[n] for n in WEIGHTS],
            *[out_m[n] for n in WEIGHTS], *[out_v[n] for n in WEIGHTS])
```

```python
import functools
import math

import jax
import jax.numpy as jnp
from jax import lax
from jax.experimental import pallas as pl
from jax.experimental.pallas import tpu as pltpu
from jax.experimental.pallas import tpu_sc as plsc

f32 = jnp.float32
bf16 = jnp.bfloat16
SDS = jax.ShapeDtypeStruct

T = 2048
D = 1024
NDEV = 8
NORM_EPS = 1e-5
S5_G, S5_C, S5_P = 64, 16, 64
S5_SUB = 8
S5_CH = 8
S5_STEPS = T // S5_CH
DT_MIN_LAMBDA = -1e-4
HEAD_DIM = 64
N_KV = 4
Q_PER_KV = 4
BLK = 128
D_FF_SHARD = 512
ADAM_LR, ADAM_B1, ADAM_B2, ADAM_EPS, ADAM_WD, ADAM_STEP = 0.001, 0.9, 0.999, 1e-08, 0.01, 10
VMEM_LIMIT = 56 * 1024 * 1024
MESH = pl.DeviceIdType.MESH


def _cp(**kw):
    return pltpu.CompilerParams(vmem_limit_bytes=VMEM_LIMIT, **kw)


def _dot(a, b):
    return jnp.dot(a, b, preferred_element_type=f32)


def _dot_nt(a, b):
    return lax.dot_general(a, b, (((1,), (1,)), ((), ())), preferred_element_type=f32)


def _dot_tn(a, b):
    return lax.dot_general(a, b, (((0,), (0,)), ((), ())), preferred_element_type=f32)


def _rms(x, g):
    r = lax.rsqrt(jnp.mean(x * x, axis=-1, keepdims=True) + NORM_EPS)
    return x * r * g, r


def _rms_bwd(x, g, dy):
    r = lax.rsqrt(jnp.mean(x * x, axis=-1, keepdims=True) + NORM_EPS)
    u = dy * g
    dx = r * u - (r * r * r) * x * jnp.mean(u * x, axis=-1, keepdims=True)
    return dx, dy * x * r


def _colsum8(v):
    s = jnp.sum(v, axis=0, keepdims=True)
    row = lax.broadcasted_iota(jnp.int32, (8, v.shape[1]), 0)
    return jnp.where(row == 0, jnp.broadcast_to(s, (8, v.shape[1])), 0.0)


def _full(shape):
    nd = len(shape)
    return pl.BlockSpec(shape, lambda *_: (0,) * nd, pipeline_mode=pl.Buffered(1))


_ANY = pl.BlockSpec(memory_space=pl.ANY)


def _pos():
    return lax.axis_index("x"), lax.axis_index("y"), lax.axis_index("c")


def _other_chips(x, y):
    return [(1 - x, y), (x, 1 - y), (1 - x, 1 - y)]


class BgGather:
    SIB, XN, YN, FWD_Y, FWD_X, SIB_X, SIB_Y, SIB_D = range(8)

    def __init__(self, arrs, mids=(0.5, 0.75)):
        n = len(arrs)
        self.arrs = list(arrs)
        self.out_shape = [SDS((NDEV,) + a.shape, a.dtype) for a in arrs]
        self.scratch = [pltpu.SemaphoreType.DMA((n, 8)), pltpu.SemaphoreType.DMA((n, 8)),
                        pltpu.SemaphoreType.DMA((n,))]
        self.mids = mids
        self.result = None

    @staticmethod
    def peers(x, y, c):
        return [(x, y, 1 - c), (1 - x, y, c), (x, 1 - y, c)]

    def mid_steps(self, nsteps):
        at = lambda f: min(nsteps - 1, max(0, int(f * nsteps) - 1))
        return [(at(self.mids[0]), self.mid), (max(at(self.mids[0]), at(self.mids[1])), self.mid2)]

    def _halves(self, a):
        rows = self.arrs[a].shape[0]
        cut = (rows // 32) * 16 if rows >= 32 else rows
        return (0, cut), (cut, rows - cut)

    def _copy(self, ins, outs, sems, a, k, block, to, own=False, part=None):
        slot = 4 * block[0] + 2 * block[1] + block[2]
        rows = pl.ds(0, self.arrs[a].shape[0]) if part is None else pl.ds(*self._halves(a)[part])
        dst = outs[a].at[slot, rows]
        return pltpu.make_async_remote_copy(
            src_ref=ins[a].at[rows] if own else dst, dst_ref=dst, send_sem=sems[0].at[a, k],
            recv_sem=sems[1].at[a, k], device_id=to, device_id_type=MESH)

    def _mine(self, ins, outs, sems):
        x, y, c = _pos()
        return [pltpu.make_async_copy(ins[a], outs[a].at[4 * x + 2 * y + c], sems[2].at[a])
                for a in range(len(self.arrs))]

    def _split(self, a):
        return self._halves(a)[1][1] > 0

    def _sends(self, ins, outs, sems, phase):
        x, y, c = _pos()
        me, sib, xn, yn, dg = (x, y, c), (x, y, 1 - c), (1 - x, y, c), (x, 1 - y, c), (1 - x, 1 - y, c)
        cps = []
        for a in range(len(self.arrs)):
            cp = lambda k, block, to, **kw: self._copy(ins, outs, sems, a, k, block, to, **kw)
            if phase == 0:
                cps += [cp(self.SIB, me, sib, own=True), cp(self.XN, me, xn, own=True), cp(self.YN, me, yn, own=True)]
            elif phase == 1:
                cps.append(cp(self.FWD_Y, xn, yn, part=0))
                if self._split(a):
                    cps.append(cp(self.FWD_X, yn, xn, part=1))
                cps += [cp(self.SIB_X, xn, sib), cp(self.SIB_Y, yn, sib)]
            else:
                cps.append(cp(self.SIB_D, dg, sib))
        return cps

    def _arrivals(self, ins, outs, sems, phase):
        x, y, c = _pos()
        me, xn, yn, dg = (x, y, c), (1 - x, y, c), (x, 1 - y, c), (1 - x, 1 - y, c)
        cps = []
        for a in range(len(self.arrs)):
            cp = lambda k, block, **kw: self._copy(ins, outs, sems, a, k, block, me, **kw)
            if phase == 1:
                cps += [cp(self.XN, xn), cp(self.YN, yn)]
            elif phase == 2:
                cps.append(cp(self.FWD_Y, dg, part=0))
                if self._split(a):
                    cps.append(cp(self.FWD_X, dg, part=1))
            else:
                cps += [cp(self.SIB, (x, y, 1 - c)), cp(self.SIB_X, (1 - x, y, 1 - c)),
                        cp(self.SIB_Y, (x, 1 - y, 1 - c)), cp(self.SIB_D, (1 - x, 1 - y, 1 - c))]
        return cps

    def start(self, ins, outs, sems):
        for cp in self._mine(ins, outs, sems) + self._sends(ins, outs, sems, 0):
            cp.start()

    def mid(self, ins, outs, sems):
        for cp in self._arrivals(ins, outs, sems, 1):
            cp.wait_recv()
        for cp in self._sends(ins, outs, sems, 1):
            cp.start()

    def mid2(self, ins, outs, sems):
        for cp in self._arrivals(ins, outs, sems, 2):
            cp.wait_recv()
        for cp in self._sends(ins, outs, sems, 2):
            cp.start()

    def finish(self, ins, outs, sems):
        for cp in self._arrivals(ins, outs, sems, 3):
            cp.wait_recv()
        for ph in range(3):
            for cp in self._sends(ins, outs, sems, ph):
                cp.wait_send()
        for cp in self._mine(ins, outs, sems):
            cp.wait()


def sc_comm(g, collective_id, name):
    srcs = [jax.new_ref(a, memory_space=pltpu.MemorySpace.HBM) for a in g.arrs]
    dsts = [jax.empty_ref(s, memory_space=pltpu.MemorySpace.HBM) for s in g.out_shape]

    @pl.kernel(mesh=plsc.ScalarSubcoreMesh(axis_name="sequencer", num_cores=1), name=name,
               scratch_types=tuple(g.scratch), compiler_params=pltpu.CompilerParams(collective_id=collective_id))
    def launch(*sems):
        peers = g.peers(*_pos())
        barrier = pltpu.get_barrier_semaphore()
        for peer in peers:
            pl.semaphore_signal(barrier, inc=1, device_id=peer, device_id_type=MESH)
        pl.semaphore_wait(barrier, len(peers))
        g.start(srcs, dsts, sems)
        for _, phase in g.mid_steps(1):
            phase(srcs, dsts, sems)
        g.finish(srcs, dsts, sems)

    launch()
    return [d[...] for d in dsts]


def sc_gather(arrs, collective_id, name):
    return sc_comm(BgGather(arrs), collective_id, name)


class BgPair:
    def __init__(self, arrs):
        n = len(arrs)
        self.arrs = list(arrs)
        self.out_shape = [SDS((4,) + a.shape[1:], a.dtype) for a in arrs]
        self.scratch = [pltpu.SemaphoreType.DMA((n, 4)), pltpu.SemaphoreType.DMA((n, 4))]
        self.result = None

    @staticmethod
    def peers(x, y, c):
        return [(x, y, 1 - c)]

    def mid_steps(self, nsteps):
        return []

    def _copies(self, ins, outs, sems):
        x, y, c = _pos()
        return [pltpu.make_async_remote_copy(
            src_ref=ins[a].at[2 * k + 1 - c], dst_ref=outs[a].at[k], send_sem=sems[0].at[a, k],
            recv_sem=sems[1].at[a, k], device_id=(x, y, 1 - c), device_id_type=MESH)
            for a in range(len(self.arrs)) for k in range(4)]

    def start(self, ins, outs, sems):
        for cp in self._copies(ins, outs, sems):
            cp.start()

    def finish(self, ins, outs, sems):
        cps = self._copies(ins, outs, sems)
        for cp in cps:
            cp.wait_recv()
        for cp in cps:
            cp.wait_send()


class BgChips(BgPair):
    def __init__(self, arrs):
        n = len(arrs)
        self.arrs = list(arrs)
        self.out_shape = [SDS((3,) + a.shape[1:], a.dtype) for a in arrs]
        self.scratch = [pltpu.SemaphoreType.DMA((n, 3)), pltpu.SemaphoreType.DMA((n, 3))]
        self.result = None

    @staticmethod
    def peers(x, y, c):
        return [(px, py, c) for px, py in _other_chips(x, y)]

    def _copies(self, ins, outs, sems):
        x, y, c = _pos()
        return [pltpu.make_async_remote_copy(
            src_ref=ins[a].at[2 * px + py], dst_ref=outs[a].at[r], send_sem=sems[0].at[a, r],
            recv_sem=sems[1].at[a, r], device_id=(px, py, c), device_id_type=MESH)
            for a in range(len(self.arrs)) for r, (px, py) in enumerate(_other_chips(x, y))]


class AdamRider:
    def __init__(self, w, m, v, part, r2, layer=0, prev=None):
        self.arrs = [w, m, v, part, r2] + list(prev or [])
        self.n_prev = len(prev or [])
        self.layer = layer
        self.out_shape = [SDS(w.shape, f32)] * 4
        self.scratch = []
        self.aliases = {5 + k: k for k in range(self.n_prev)}
        self.result = None

    def _tile(self, grid):
        assert len(grid) == 1
        _, R, C = self.arrs[0].shape
        return R // grid[0], C

    def in_specs(self, grid):
        tr, C = self._tile(grid)
        layer = self.layer
        blk = pl.BlockSpec((None, tr, C), lambda b: (layer, b, 0))
        mine = pl.BlockSpec((None, tr, C), lambda b: (2 * lax.axis_index("x") + lax.axis_index("y"), b, 0))
        return [blk, blk, blk, mine, pl.BlockSpec((3, tr, C), lambda b: (0, b, 0))] + [_ANY] * self.n_prev

    def out_specs(self, grid):
        tr, C = self._tile(grid)
        layer = self.layer
        return [pl.BlockSpec((None, tr, C), lambda b: (layer, b, 0))] * 4

    def mid_steps(self, nsteps):
        return []

    def start(self, ins, outs, sems):
        pass

    finish = start

    def step(self, ins, outs, sems):
        w_ref, m_ref, v_ref, p_ref, r_ref = ins[:5]
        g = p_ref[...].astype(f32) + r_ref[0].astype(f32) + r_ref[1].astype(f32) + r_ref[2].astype(f32)
        d, m_, v_ = _adamw(w_ref[...], g, m_ref[...], v_ref[...])
        for ref, val in zip(outs, (g, d, m_, v_)):
            ref[...] = val


def _call(bgs, body, *, name, grid, in_specs, out_specs, out_shape, scratch_shapes=(), compiler_params=None):
    single = not isinstance(out_shape, (list, tuple))
    out_specs_l = [out_specs] if single else list(out_specs)
    out_shape_l = [out_shape] if single else list(out_shape)
    bgs = [b for b in (bgs or []) if b is not None]
    n_in, n_out, n_sc = len(in_specs), len(out_shape_l), len(scratch_shapes)
    nsteps = math.prod(grid)
    b_in_specs = [b.in_specs(grid) if hasattr(b, "in_specs") else [_ANY] * len(b.arrs) for b in bgs]
    b_out_specs = [b.out_specs(grid) if hasattr(b, "out_specs") else [_ANY] * len(b.out_shape) for b in bgs]
    aliases, i_off, o_off = {}, n_in, n_out
    for b in bgs:
        aliases.update({i_off + i: o_off + o for i, o in getattr(b, "aliases", {}).items()})
        i_off, o_off = i_off + len(b.arrs), o_off + len(b.out_shape)

    def full(*refs):
        pos = [0]

        def take(k):
            r = refs[pos[0]:pos[0] + k]
            pos[0] += k
            return r

        ins = take(n_in)
        b_ins = [take(len(b.arrs)) for b in bgs]
        outs = take(n_out)
        b_outs = [take(len(b.out_shape)) for b in bgs]
        sc = take(n_sc)
        b_sc = [take(len(b.scratch)) for b in bgs]
        if bgs:
            step = pl.program_id(0)
            for d in range(1, len(grid)):
                step = step * grid[d] + pl.program_id(d)

            @pl.when(step == 0)
            def _():
                for b, i_, o_, s_ in zip(bgs, b_ins, b_outs, b_sc):
                    b.start(i_, o_, s_)

        body(*ins, *outs, *sc)
        if bgs:
            for b, i_, o_, s_ in zip(bgs, b_ins, b_outs, b_sc):
                if hasattr(b, "step"):
                    b.step(i_, o_, s_)
                for at, fn in b.mid_steps(nsteps):
                    @pl.when(step == at)
                    def _():
                        fn(i_, o_, s_)

            @pl.when(step == nsteps - 1)
            def _():
                for b, i_, o_, s_ in zip(bgs, b_ins, b_outs, b_sc):
                    b.finish(i_, o_, s_)

    def run(*args):
        res = pl.pallas_call(
            full, name=name, grid=grid,
            in_specs=list(in_specs) + [s for l in b_in_specs for s in l],
            out_specs=out_specs_l + [s for l in b_out_specs for s in l],
            out_shape=out_shape_l + [s for b in bgs for s in b.out_shape],
            scratch_shapes=list(scratch_shapes) + [s for b in bgs for s in b.scratch],
            input_output_aliases=aliases,
            compiler_params=compiler_params,
        )(*args, *[a for b in bgs for a in b.arrs])
        rest = list(res[n_out:])
        for b in bgs:
            b.result, rest = rest[:len(b.out_shape)], rest[len(b.out_shape):]
        return res[0] if single else list(res[:n_out])

    return run


def s5_discretize(a_re, a_im, log_dt, b_re, b_im, c_re, c_im):
    lam_r = jnp.minimum(a_re, DT_MIN_LAMBDA)
    lam_i = a_im
    dt = jnp.exp(log_dt)[:, None]
    e = jnp.exp(lam_r * dt)
    lbr = e * jnp.cos(lam_i * dt)
    lbi = e * jnp.sin(lam_i * dt)
    den = lam_r * lam_r + lam_i * lam_i
    cf_r = ((lbr - 1.0) * lam_r + lbi * lam_i) / den
    cf_i = (lbi * lam_r - (lbr - 1.0) * lam_i) / den
    bb_r = cf_r[:, :, None] * b_re - cf_i[:, :, None] * b_im
    bb_i = cf_r[:, :, None] * b_im + cf_i[:, :, None] * b_re
    eye = jnp.eye(8, dtype=f32)

    def blk_b(m):
        return jnp.einsum('bgpc,gh->bgchp', m.reshape(8, 8, S5_P, S5_C), eye).reshape(8, 128, 512)

    def blk_c(m):
        return jnp.einsum('bgcp,gh->bgphc', m.reshape(8, 8, S5_C, S5_P), eye).reshape(8, 512, 128)

    bm = jnp.concatenate([blk_b(bb_r), blk_b(bb_i)], axis=-1)
    cm = jnp.concatenate([blk_c(c_re), -blk_c(c_im)], axis=1)
    lam = jnp.stack([lbr.reshape(8, 512), lbi.reshape(8, 512)], axis=1)
    lam = jnp.broadcast_to(lam[:, :, None, :], (8, 2, 8, 512))
    return lam, bm, cm


def _cmul(ar, ai, br, bi):
    return ar * br - ai * bi, ar * bi + ai * br


def _shift_rows(v, k, up):
    row = lax.broadcasted_iota(jnp.int32, v.shape, 0)
    if up:
        return jnp.where(row < 8 - k, pltpu.roll(v, 8 - k, 0), 0.0)
    return jnp.where(row >= k, pltpu.roll(v, k, 0), 0.0)


def _chunk_scan(S, lr, li, reverse, aux=None):
    z = jnp.zeros((8, 512), f32)
    U = 4

    def idx(i):
        return (S5_STEPS - 1 - i) if reverse else i

    def rows_of(s):
        return pl.ds(s * 8, 8) if isinstance(s, int) else pl.ds(pl.multiple_of(s * 8, 8), 8)

    def rec(xr, xi, row):
        br = S[row, 0:512]
        bi = S[row, 512:1024]
        return lr * xr - li * xi + br, lr * xi + li * xr + bi

    def step1(i, c):
        for u in range(U):
            c = rec(c[0], c[1], rows_of(idx(i * U + u)))
        return c

    er, ei = lax.fori_loop(0, S5_STEPS // U, step1, (z, z))
    ar, ai = lr, li
    for _ in range(8):
        ar, ai = _cmul(ar, ai, ar, ai)
    cr, ci = _shift_rows(er, 1, reverse), _shift_rows(ei, 1, reverse)
    for k in (1, 2, 4):
        sr, si = _shift_rows(cr, k, reverse), _shift_rows(ci, k, reverse)
        pr, pi_ = _cmul(ar, ai, sr, si)
        cr, ci = cr + pr, ci + pi_
        ar, ai = _cmul(ar, ai, ar, ai)

    if aux is None:
        def step2(i, c):
            for u in range(U):
                row = rows_of(idx(i * U + u))
                c = rec(c[0], c[1], row)
                S[row, 0:512] = c[0]
                S[row, 512:1024] = c[1]
            return c

        lax.fori_loop(0, S5_STEPS // U, step2, (cr, ci))
        return None

    def one(s, c):
        gr0, gi0, dr, di = c
        row = rows_of(s)
        gr, gi = rec(gr0, gi0, row)
        S[row, 0:512] = gr
        S[row, 512:1024] = gi
        prow = rows_of(s - 1)
        xr = aux[prow, 0:512]
        xi = aux[prow, 512:1024]
        return gr, gi, dr + gr * xr + gi * xi, di + gi * xr - gr * xi

    def step2(i, c):
        for u in range(U):
            c = one(S5_STEPS - 1 - (i * U + u), c)
        return c

    c = lax.fori_loop(0, S5_STEPS // U - 1, step2, (cr, ci, z, z))
    for s in range(U - 1, 0, -1):
        c = one(s, c)
    gr, gi, dr, di = c
    row0 = pl.ds(0, 8)
    gr, gi = rec(gr, gi, row0)
    S[row0, 0:512] = gr
    S[row0, 512:1024] = gi
    last = pl.ds((S5_STEPS - 1) * 8, 8)
    xr = _shift_rows(aux[last, 0:512], 1, False)
    xi = _shift_rows(aux[last, 512:1024], 1, False)
    dr = dr + gr * xr + gi * xi
    di = di + gi * xr - gr * xi
    return dr, di


_ROWS = 256


def _row_loop(fn):
    def body(r, c):
        fn(pl.ds(pl.multiple_of(r * _ROWS, _ROWS), _ROWS))
        return c
    lax.fori_loop(0, T // _ROWS, body, 0)


def s5_core_fwd(hn, bm, lam, cm, bg=()):
    nt = T // _ROWS

    def body(u_ref, b_ref, lam_ref, c_ref, ys_ref, S):
        lr, li = lam_ref[0], lam_ref[1]
        z = jnp.zeros((8, 512), f32)
        tile = lambda k: pl.ds(k * _ROWS, _ROWS)
        c = (z, z)
        for k in range(nt):
            S[tile(k), :] = _dot(u_ref[tile(k), :], b_ref[...])
            if k >= 1:
                c = _scan_tile(S, lr, li, k - 1, c, False, False)
        c = _scan_tile(S, lr, li, nt - 1, c, False, False)
        c = _chunk_starts(c[0], c[1], lr, li, False)
        for k in range(nt):
            c = _scan_tile(S, lr, li, k, c, False, True)
            if k >= 1:
                ys_ref[tile(k - 1), :] = _dot(S[tile(k - 1), :].astype(bf16), c_ref[...])
        ys_ref[tile(nt - 1), :] = _dot(S[tile(nt - 1), :].astype(bf16), c_ref[...])

    return _call(
        bg, body, name="s5_core_fwd", grid=(S5_SUB,),
        in_specs=[pl.BlockSpec((T, 128), lambda b: (0, b)),
                  pl.BlockSpec((None, 128, 1024), lambda b: (b, 0, 0)),
                  pl.BlockSpec((None, 4, 8, 512), lambda b: (b, 0, 0, 0)),
                  pl.BlockSpec((None, 1024, 128), lambda b: (b, 0, 0))],
        out_specs=pl.BlockSpec((T, 128), lambda b: (0, b)),
        out_shape=SDS((T, D), f32),
        scratch_shapes=[pltpu.VMEM((T, 1024), f32)],
        compiler_params=_cp(dimension_semantics=("arbitrary",)),
    )(hn, bm, lam, cm)


_SEG = _ROWS // S5_CH


def _scan_tile(S, lr, li, k, carry, reverse, store, aux=None):
    steps = range(k * _SEG, (k + 1) * _SEG)
    for s in (reversed(steps) if reverse else steps):
        row = pl.ds(s * 8, 8)
        xr, xi = carry[0], carry[1]
        nr = lr * xr - li * xi + S[row, 0:512]
        ni = lr * xi + li * xr + S[row, 512:1024]
        if store:
            S[row, 0:512] = nr
            S[row, 512:1024] = ni
        if aux is not None and s >= 1:
            prow = pl.ds((s - 1) * 8, 8)
            pr, pi_ = aux[prow, 0:512], aux[prow, 512:1024]
            carry = (nr, ni, carry[2] + nr * pr + ni * pi_, carry[3] + ni * pr - nr * pi_)
        elif aux is not None:
            carry = (nr, ni, carry[2], carry[3])
        else:
            carry = (nr, ni)
    return carry


def _chunk_starts(er, ei, lr, li, reverse):
    ar, ai = lr, li
    for _ in range(8):
        ar, ai = _cmul(ar, ai, ar, ai)
    cr, ci = _shift_rows(er, 1, reverse), _shift_rows(ei, 1, reverse)
    for k in (1, 2, 4):
        sr, si = _shift_rows(cr, k, reverse), _shift_rows(ci, k, reverse)
        pr, pi_ = _cmul(ar, ai, sr, si)
        cr, ci = cr + pr, ci + pi_
        ar, ai = _cmul(ar, ai, ar, ai)
    return cr, ci


def s5_core_bwd(hn, dy, bm, lam, cm, bg=()):
    nt = T // _ROWS

    def body(u_ref, dy_ref, b_ref, lam_ref, c_ref, du_ref, db_ref, dct_ref, dlam_ref, S1, S2):
        lr, li, lcr, lci = lam_ref[0], lam_ref[1], lam_ref[2], lam_ref[3]
        z = jnp.zeros((8, 512), f32)
        tile = lambda k: pl.ds(k * _ROWS, _ROWS)
        dyb = lambda k: dy_ref[tile(k), :].astype(bf16)

        c = (z, z)
        for k in range(nt):
            S1[tile(k), :] = _dot(u_ref[tile(k), :], b_ref[...])
            if k >= 1:
                c = _scan_tile(S1, lr, li, k - 1, c, False, False)
        c = _scan_tile(S1, lr, li, nt - 1, c, False, False)

        c = _chunk_starts(c[0], c[1], lr, li, False)
        dct_ref[...] = jnp.zeros_like(dct_ref)
        for k in range(nt):
            c = _scan_tile(S1, lr, li, k, c, False, True)
            if k >= 1:
                dct_ref[...] += _dot_tn(dyb(k - 1), S1[tile(k - 1), :].astype(bf16))
        dct_ref[...] += _dot_tn(dyb(nt - 1), S1[tile(nt - 1), :].astype(bf16))

        S2[tile(nt - 1), :] = _dot_nt(dyb(nt - 1), c_ref[...])
        c = (z, z)
        for k in range(nt - 1, -1, -1):
            if k >= 1:
                S2[tile(k - 1), :] = _dot_nt(dyb(k - 1), c_ref[...])
            c = _scan_tile(S2, lcr, lci, k, c, True, False)

        def dbu(k):
            gb = S2[tile(k), :].astype(bf16)
            db_ref[...] += _dot_tn(u_ref[tile(k), :], gb)
            du_ref[tile(k), :] = _dot_nt(gb, b_ref[...])

        c = _chunk_starts(c[0], c[1], lcr, lci, True) + (z, z)
        db_ref[...] = jnp.zeros_like(db_ref)
        for k in range(nt - 1, -1, -1):
            c = _scan_tile(S2, lcr, lci, k, c, True, True, aux=S1)
            if k + 1 < nt:
                dbu(k + 1)
        dbu(0)
        gr, gi, dr, di = c
        last = pl.ds((S5_STEPS - 1) * 8, 8)
        xr = _shift_rows(S1[last, 0:512], 1, False)
        xi = _shift_rows(S1[last, 512:1024], 1, False)
        dlam_ref[0] = dr + gr * xr + gi * xi
        dlam_ref[1] = di + gi * xr - gr * xi

    return _call(
        bg, body, name="s5_core_bwd", grid=(S5_SUB,),
        in_specs=[pl.BlockSpec((T, 128), lambda b: (0, b)),
                  pl.BlockSpec((T, 128), lambda b: (0, b)),
                  pl.BlockSpec((None, 128, 1024), lambda b: (b, 0, 0)),
                  pl.BlockSpec((None, 4, 8, 512), lambda b: (b, 0, 0, 0)),
                  pl.BlockSpec((None, 1024, 128), lambda b: (b, 0, 0))],
        out_specs=[pl.BlockSpec((T, 128), lambda b: (0, b)),
                   pl.BlockSpec((None, 128, 1024), lambda b: (b, 0, 0)),
                   pl.BlockSpec((None, 128, 1024), lambda b: (b, 0, 0)),
                   pl.BlockSpec((None, 2, 8, 512), lambda b: (b, 0, 0, 0))],
        out_shape=[SDS((T, D), f32), SDS((8, 128, 1024), f32), SDS((8, 128, 1024), f32), SDS((8, 2, 8, 512), f32)],
        scratch_shapes=[pltpu.VMEM((T, 1024), f32), pltpu.VMEM((T, 1024), f32)],
        compiler_params=_cp(dimension_semantics=("arbitrary",)),
    )(hn, dy, bm, lam, cm)


TM = 512
NT = T // TM


def _tile(n=D):
    return pl.BlockSpec((TM, n), lambda i: (i, 0))


def s5_pre(xp, g):
    def body(x_ref, g_ref, hn_ref):
        hn, _ = _rms(x_ref[...], g_ref[...])
        hn_ref[...] = hn.astype(bf16)

    return pl.pallas_call(
        body, name="s5_pre", grid=(NT,), in_specs=[_tile(), _full((1, D))], out_specs=_tile(),
        out_shape=SDS((T, D), bf16), compiler_params=_cp(dimension_semantics=("arbitrary",)),
    )(xp, g)


def _gelu_grad(y):
    c = math.sqrt(2.0 / math.pi)
    t = jnp.tanh(c * (y + 0.044715 * y * y * y))
    return 0.5 * (1.0 + t) + 0.5 * y * (1.0 - t * t) * c * (1.0 + 3.0 * 0.044715 * y * y)


def s5_post(ys, xp, g, d, wglu, bglu, bg=()):
    def body(ys_ref, x_ref, g_ref, d_ref, w_ref, b_ref, y_ref, z_ref, h_ref):
        x = x_ref[...]
        hn, _ = _rms(x, g_ref[...])
        y = ys_ref[...] + d_ref[...] * hn
        y_ref[...] = y
        yg = jax.nn.gelu(y).astype(bf16)
        for j in range(4):
            cv = slice(j * 256, (j + 1) * 256)
            cg = slice(1024 + j * 256, 1024 + (j + 1) * 256)
            val = _dot(yg, w_ref[j]) + b_ref[:, cv]
            gate = _dot(yg, w_ref[j + 4]) + b_ref[:, cg]
            z_ref[:, cv] = val
            z_ref[:, cg] = gate
            h_ref[:, cv] = x[:, cv] + val * jax.nn.sigmoid(gate)

    return _call(
        bg, body, name="s5_post", grid=(NT,),
        in_specs=[_tile(), _tile(), _full((1, D)), _full((1, D)), _full((8, D, 256)), _full((1, 2 * D))],
        out_specs=[_tile(), _tile(2 * D), _tile()],
        out_shape=[SDS((T, D), f32), SDS((T, 2 * D), f32), SDS((T, D), f32)],
        compiler_params=_cp(dimension_semantics=("arbitrary",)),
    )(ys, xp, g, d, wglu, bglu)


def s5_post_bwd(dh, y, z, wglu, bg=()):
    def body(dh_ref, y_ref, z_ref, w_ref, dy_ref, dw_ref, db_ref, acc):
        i = pl.program_id(0)

        @pl.when(i == 0)
        def _():
            acc[...] = jnp.zeros_like(acc)
            db_ref[...] = jnp.zeros_like(db_ref)

        dh_ = dh_ref[...]
        y = y_ref[...]
        yg = jax.nn.gelu(y).astype(bf16)
        dyg = jnp.zeros((TM, D), f32)
        for j in range(4):
            cv = slice(j * 256, (j + 1) * 256)
            cg = slice(1024 + j * 256, 1024 + (j + 1) * 256)
            val = z_ref[:, cv]
            sg = jax.nn.sigmoid(z_ref[:, cg])
            dval = dh_[:, cv] * sg
            dgate = dh_[:, cv] * val * sg * (1.0 - sg)
            db_ref[:, cv] += _colsum8(dval)
            db_ref[:, cg] += _colsum8(dgate)
            dvb = dval.astype(bf16)
            dgb = dgate.astype(bf16)
            acc[j] += _dot_tn(yg, dvb)
            acc[j + 4] += _dot_tn(yg, dgb)
            dyg = dyg + _dot_nt(dvb, w_ref[j]) + _dot_nt(dgb, w_ref[j + 4])
        dy_ref[...] = dyg * _gelu_grad(y)

        @pl.when(i == NT - 1)
        def _():
            dw_ref[...] = acc[...].astype(bf16)

    return _call(
        bg, body, name="s5_post_bwd", grid=(NT,),
        in_specs=[_tile(), _tile(), _tile(2 * D), _full((8, D, 256))],
        out_specs=[_tile(), _full((8, D, 256)), _full((8, 2 * D))],
        out_shape=[SDS((T, D), f32), SDS((8, D, 256), bf16), SDS((8, 2 * D), f32)],
        scratch_shapes=[pltpu.VMEM((8, D, 256), f32)],
        compiler_params=_cp(dimension_semantics=("arbitrary",)),
    )(dh, y, z, wglu)


def s5_pre_bwd(xp, g, du, dy, d, dh, bg=()):
    def body(x_ref, g_ref, du_ref, dy_ref, d_ref, dh_ref, dx_ref, dg_ref, dd_ref):
        i = pl.program_id(0)

        @pl.when(i == 0)
        def _():
            dg_ref[...] = jnp.zeros_like(dg_ref)
            dd_ref[...] = jnp.zeros_like(dd_ref)

        x = x_ref[...]
        g = g_ref[...]
        dy = dy_ref[...]
        hn, _ = _rms(x, g)
        dhn = du_ref[...] + d_ref[...] * dy
        dx, dgt = _rms_bwd(x, g, dhn)
        dx_ref[...] = dh_ref[...] + dx
        dg_ref[...] += _colsum8(dgt)
        dd_ref[...] += _colsum8(dy * hn)

    return _call(
        bg, body, name="s5_pre_bwd", grid=(NT,),
        in_specs=[_tile(), _full((1, D)), _tile(), _tile(), _full((1, D)), _tile()],
        out_specs=[_tile(), _full((8, D)), _full((8, D))],
        out_shape=[SDS((T, D), f32), SDS((8, D), f32), SDS((8, D), f32)],
        compiler_params=_cp(dimension_semantics=("arbitrary",)),
    )(xp, g, du, dy, d, dh)


TMF = 1024


def mlp_fwd(h, g, w_in, w_out, layer, bg=()):
    def body(h_ref, g_ref, wi_ref, wo_ref, hm_ref, r_ref, out_ref, acc):
        j = pl.program_id(1)

        @pl.when(j == 0)
        def _():
            hm, _ = _rms(h_ref[...], g_ref[...])
            hm_ref[...] = hm.astype(bf16)
            acc[...] = jnp.zeros_like(acc)

        a = jnp.maximum(_dot(hm_ref[...], wi_ref[...]), 0.0)
        r_ref[...] = a.astype(bf16)
        acc[...] += _dot((a * a).astype(bf16), wo_ref[...])

        @pl.when(j == NDEV - 1)
        def _():
            out_ref[...] = h_ref[...] + acc[...]

    return _call(
        bg, body, name=f"mlp_fwd{layer}", grid=(T // TMF, NDEV),
        in_specs=[pl.BlockSpec((TMF, D), lambda i, j: (i, 0)),
                  pl.BlockSpec((1, D), lambda i, j: (0, 0)),
                  pl.BlockSpec((None, D, D_FF_SHARD), lambda i, j: (j, 0, 0)),
                  pl.BlockSpec((None, D_FF_SHARD, D), lambda i, j: (j, 0, 0))],
        out_specs=[pl.BlockSpec((TMF, D), lambda i, j: (i, 0)), pl.BlockSpec((TMF, D_FF_SHARD), lambda i, j: (i, j)),
                   pl.BlockSpec((TMF, D), lambda i, j: (i, 0))],
        out_shape=[SDS((T, D), bf16), SDS((T, NDEV * D_FF_SHARD), bf16), SDS((T, D), f32)],
        scratch_shapes=[pltpu.VMEM((TMF, D), f32)],
        compiler_params=_cp(dimension_semantics=("arbitrary", "arbitrary")),
    )(h, g, w_in, w_out)


def mlp_bwd(h, hm, r, g, dout, dout_b, w_in, w_out, layer, bg=()):
    last = NDEV - 1

    def body(h_ref, hm_ref, r_ref, g_ref, do_ref, dob_ref, wi_ref, wo_ref, dh_ref, dwi_ref, dwo_ref, dg_ref,
             dhm, awi, awo):
        j = pl.program_id(0)
        i = pl.program_id(1)
        rows = pl.ds(pl.multiple_of(i * TM, TM), TM)

        @pl.when(i == 0)
        def _():
            awi[...] = jnp.zeros_like(awi)
            awo[...] = jnp.zeros_like(awo)

        dz = (_dot_nt(dob_ref[...], wo_ref[...]) * (2.0 * r_ref[...].astype(f32))).astype(bf16)
        rb = r_ref[...]
        awo[...] += _dot_tn(rb * rb, dob_ref[...])
        awi[...] += _dot_tn(hm_ref[...], dz)
        part = _dot_nt(dz, wi_ref[...])

        @pl.when(j == 0)
        def _():
            dhm[rows, :] = part

        @pl.when(j > 0)
        def _():
            dhm[rows, :] += part

        @pl.when(i == NT - 1)
        def _():
            dwi_ref[...] = awi[...].astype(bf16)
            dwo_ref[...] = awo[...].astype(bf16)

        @pl.when(j == last)
        def _():
            @pl.when(i == 0)
            def _():
                dg_ref[...] = jnp.zeros_like(dg_ref)
            dx, dgt = _rms_bwd(h_ref[...], g_ref[...], dhm[rows, :])
            dh_ref[...] = do_ref[...] + dx
            dg_ref[...] += _colsum8(dgt)

    late = lambda j, i: (jnp.where(j == last, i, 0), 0)
    return _call(
        bg, body, name=f"mlp_bwd{layer}", grid=(NDEV, NT),
        in_specs=[pl.BlockSpec((TM, D), late),
                  pl.BlockSpec((TM, D), lambda j, i: (i, 0)),
                  pl.BlockSpec((TM, D_FF_SHARD), lambda j, i: (i, j)),
                  pl.BlockSpec((1, D), lambda j, i: (0, 0)),
                  pl.BlockSpec((TM, D), late),
                  pl.BlockSpec((TM, D), lambda j, i: (i, 0)),
                  pl.BlockSpec((None, D, D_FF_SHARD), lambda j, i: (j, 0, 0)),
                  pl.BlockSpec((None, D_FF_SHARD, D), lambda j, i: (j, 0, 0))],
        out_specs=[pl.BlockSpec((TM, D), late),
                   pl.BlockSpec((None, D, D_FF_SHARD), lambda j, i: (j, 0, 0)),
                   pl.BlockSpec((None, D_FF_SHARD, D), lambda j, i: (j, 0, 0)),
                   pl.BlockSpec((8, D), lambda j, i: (0, 0))],
        out_shape=[SDS((T, D), f32), SDS((NDEV, D, D_FF_SHARD), bf16), SDS((NDEV, D_FF_SHARD, D), bf16),
                   SDS((8, D), f32)],
        scratch_shapes=[pltpu.VMEM((T, D), f32), pltpu.VMEM((D, D_FF_SHARD), f32), pltpu.VMEM((D_FF_SHARD, D), f32)],
        compiler_params=_cp(dimension_semantics=("arbitrary", "arbitrary")),
    )(h, hm, r, g, dout, dout_b, w_in, w_out)


def _spread4():
    r = lax.broadcasted_iota(jnp.int32, (256, D), 0)
    c = lax.broadcasted_iota(jnp.int32, (256, D), 1)
    return ((c // 256 == r // HEAD_DIM) & (c % HEAD_DIM == r % HEAD_DIM)).astype(bf16)


def attn_pre(h, g_kv, g_mix, wkv, bkv, spread, wq, bq):
    def body(h_ref, gkv_ref, gm_ref, wkv_ref, bkv_ref, sp_ref, wq_ref, bq_ref, kvn_ref, hn_ref, k_ref, v_ref, q_ref):
        h_ = h_ref[...]
        kvn = _rms(h_, gkv_ref[...])[0].astype(bf16)
        hn = _rms(h_, gm_ref[...])[0].astype(bf16)
        kvn_ref[...] = kvn
        hn_ref[...] = hn
        kv = (_dot(kvn, wkv_ref[...]) + bkv_ref[...]).astype(bf16)
        k_ref[...] = _dot(kv[:, :256], sp_ref[...]).astype(bf16)
        v_ref[...] = _dot(kv[:, 256:], sp_ref[...]).astype(bf16)
        q_ref[...] = (_dot(hn, wq_ref[...]) + bq_ref[...]).astype(bf16)

    return pl.pallas_call(
        body, name="attn_pre", grid=(NT,),
        in_specs=[_tile(), _full((1, D)), _full((1, D)), _full((D, 512)), _full((1, 512)), _full((256, D)),
                  _full((D, D)), _full((1, D))],
        out_specs=[_tile()] * 5,
        out_shape=[SDS((T, D), bf16)] * 5,
        compiler_params=_cp(dimension_semantics=("arbitrary",)),
    )(h, g_kv, g_mix, wkv, bkv, spread, wq, bq)


def _attn_specs():
    cur = pl.BlockSpec((TM, 256), lambda j, n: (n, j))
    prev = pl.BlockSpec((BLK, 256), lambda j, n: (jnp.maximum(n * (TM // BLK) - 1, 0), j))
    return cur, prev


def _head_mask(g):
    lane = lax.broadcasted_iota(jnp.int32, (1, 256), 1)
    return (lane >= g * HEAD_DIM) & (lane < (g + 1) * HEAD_DIM)


def _stack_heads(t):
    return jnp.concatenate([jnp.where(_head_mask(g), t, 0) for g in range(Q_PER_KV)], axis=0)


def _unstack_heads(t):
    out = jnp.where(_head_mask(0), t[0:BLK], 0.0)
    for g in range(1, Q_PER_KV):
        out = out + jnp.where(_head_mask(g), t[g * BLK:(g + 1) * BLK], 0.0)
    return out


def _attn_probs(qs, k2, sinks, first):
    rows = Q_PER_KV * BLK
    s = _dot_nt(qs, k2) * (1.0 / math.sqrt(HEAD_DIM))
    qi = jnp.bitwise_and(lax.broadcasted_iota(jnp.int32, (rows, 2 * BLK), 0), BLK - 1)
    kj = lax.broadcasted_iota(jnp.int32, (rows, 2 * BLK), 1)
    diff = qi + BLK - kj
    valid = (diff >= 0) & (diff < BLK) & (jnp.logical_not(first) | (kj >= BLK))
    s = jnp.where(valid, s, -jnp.inf)
    rb = lax.broadcasted_iota(jnp.int32, (rows, 1), 0)
    sink = jnp.where(rb < BLK, sinks[0], jnp.where(rb < 2 * BLK, sinks[1], jnp.where(rb < 3 * BLK, sinks[2], sinks[3])))
    m = jnp.maximum(jnp.max(s, axis=-1, keepdims=True), sink)
    p = jnp.exp(s - m)
    ps = jnp.exp(sink - m)
    denom = jnp.sum(p, axis=-1, keepdims=True) + ps
    return p / denom, ps / denom


def _window_blocks(b, n, kc_ref, kp_ref, vc_ref, vp_ref):
    if b == 0:
        return (jnp.concatenate([kp_ref[...], kc_ref[0:BLK, :]], axis=0),
                jnp.concatenate([vp_ref[...], vc_ref[0:BLK, :]], axis=0), n == 0)
    rows = pl.ds((b - 1) * BLK, 2 * BLK)
    return kc_ref[rows, :], vc_ref[rows, :], False


def attn_core_fwd(q, k4, v4, sinks, bg=()):
    nb = TM // BLK

    def body(sink_ref, q_ref, kc_ref, kp_ref, vc_ref, vp_ref, o_ref):
        j = pl.program_id(0)
        n = pl.program_id(1)
        sk = [sink_ref[j * Q_PER_KV + g] for g in range(Q_PER_KV)]
        for b in range(nb):
            qb = q_ref[b * BLK:(b + 1) * BLK, :]
            k2, v2, first = _window_blocks(b, n, kc_ref, kp_ref, vc_ref, vp_ref)
            a, _ = _attn_probs(_stack_heads(qb), k2, sk, first)
            o_ref[b * BLK:(b + 1) * BLK, :] = _unstack_heads(_dot(a.astype(bf16), v2)).astype(bf16)

    cur, prev = _attn_specs()
    return _call(
        bg, body, name="attn_core_fwd", grid=(N_KV, NT),
        in_specs=[pl.BlockSpec(memory_space=pltpu.SMEM), cur, cur, prev, cur, prev],
        out_specs=cur, out_shape=SDS((T, D), bf16),
        compiler_params=_cp(dimension_semantics=("arbitrary", "arbitrary")),
    )(sinks, q, k4, k4, v4, v4)


def attn_post(h, o, wo, bo):
    def body(h_ref, o_ref, w_ref, b_ref, out_ref):
        out_ref[...] = h_ref[...] + _dot(o_ref[...], w_ref[...]) + b_ref[...]

    return pl.pallas_call(
        body, name="attn_post", grid=(NT,), in_specs=[_tile(), _tile(), _full((D, D)), _full((1, D))],
        out_specs=_tile(), out_shape=SDS((T, D), f32), compiler_params=_cp(dimension_semantics=("arbitrary",)),
    )(h, o, wo, bo)


def attn_bwd_pre(dh, o, wo, bg=()):
    def body(dh_ref, o_ref, w_ref, do_ref, dw_ref, db_ref, acc):
        i = pl.program_id(0)

        @pl.when(i == 0)
        def _():
            acc[...] = jnp.zeros_like(acc)
            db_ref[...] = jnp.zeros_like(db_ref)

        dh_ = dh_ref[...]
        dhb = dh_.astype(bf16)
        do_ref[...] = _dot_nt(dhb, w_ref[...]).astype(bf16)
        acc[...] += _dot_tn(o_ref[...], dhb)
        db_ref[...] += _colsum8(dh_)

        @pl.when(i == NT - 1)
        def _():
            dw_ref[...] = acc[...].astype(bf16)

    return _call(
        bg, body, name="attn_bwd_pre", grid=(NT,), in_specs=[_tile(), _tile(), _full((D, D))],
        out_specs=[_tile(), _full((D, D)), _full((8, D))],
        out_shape=[SDS((T, D), bf16), SDS((D, D), bf16), SDS((8, D), f32)],
        scratch_shapes=[pltpu.VMEM((D, D), f32)],
        compiler_params=_cp(dimension_semantics=("arbitrary",)),
    )(dh, o, wo)


def attn_core_bwd(q, do, k4, v4, sinks, bg=()):
    nb = TM // BLK

    def body(sink_ref, q_ref, do_ref, kc_ref, kp_ref, vc_ref, vp_ref, dq_ref, dk_ref, dv_ref, ds_ref):
        j = pl.program_id(0)
        n = pl.program_id(1)

        @pl.when(n == 0)
        def _():
            dk_ref[...] = jnp.zeros_like(dk_ref)
            dv_ref[...] = jnp.zeros_like(dv_ref)
            ds_ref[...] = jnp.zeros_like(ds_ref)

        lane8 = lax.broadcasted_iota(jnp.int32, (8, 128), 1)
        row8 = lax.broadcasted_iota(jnp.int32, (8, 128), 0)
        sk = [sink_ref[j * Q_PER_KV + g] for g in range(Q_PER_KV)]
        for b in range(nb):
            qs = _stack_heads(q_ref[b * BLK:(b + 1) * BLK, :])
            dos = _stack_heads(do_ref[b * BLK:(b + 1) * BLK, :])
            k2, v2, first = _window_blocks(b, n, kc_ref, kp_ref, vc_ref, vp_ref)
            a, asink = _attn_probs(qs, k2, sk, first)
            dp = _dot_nt(dos, v2)
            dd = jnp.sum(a * dp, axis=-1, keepdims=True)
            dsc = (a * (dp - dd) * (1.0 / math.sqrt(HEAD_DIM))).astype(bf16)
            t = asink * dd
            for g in range(Q_PER_KV):
                dsink = -jnp.sum(t[g * BLK:(g + 1) * BLK], axis=0, keepdims=True)
                ds_ref[...] += jnp.where((lane8 == g) & (row8 == 0), jnp.broadcast_to(dsink, (8, 128)), 0.0)
            dq_ref[b * BLK:(b + 1) * BLK, :] = _unstack_heads(_dot(dsc, k2))
            dk2 = _dot_tn(dsc, qs)
            dv2 = _dot_tn(a.astype(bf16), dos)
            cur = pl.ds(pl.multiple_of(n * TM + b * BLK, BLK), BLK)
            dk_ref[cur, :] += dk2[BLK:, :]
            dv_ref[cur, :] += dv2[BLK:, :]
            if b == 0:
                @pl.when(n > 0)
                def _():
                    prv = pl.ds(pl.multiple_of(n * TM - BLK, BLK), BLK)
                    dk_ref[prv, :] += dk2[:BLK, :]
                    dv_ref[prv, :] += dv2[:BLK, :]
            else:
                prv = pl.ds(pl.multiple_of(n * TM + (b - 1) * BLK, BLK), BLK)
                dk_ref[prv, :] += dk2[:BLK, :]
                dv_ref[prv, :] += dv2[:BLK, :]

    cur, prev = _attn_specs()
    col = pl.BlockSpec((T, 256), lambda j, n: (0, j))
    return _call(
        bg, body, name="attn_core_bwd", grid=(N_KV, NT),
        in_specs=[pl.BlockSpec(memory_space=pltpu.SMEM), cur, cur, cur, prev, cur, prev],
        out_specs=[cur, col, col, pl.BlockSpec((None, 8, 128), lambda j, n: (j, 0, 0))],
        out_shape=[SDS((T, D), f32), SDS((T, D), f32), SDS((T, D), f32), SDS((N_KV, 8, 128), f32)],
        compiler_params=_cp(dimension_semantics=("arbitrary", "arbitrary")),
    )(sinks, q, do, k4, k4, v4, v4)


def attn_bwd_q(h, dh, dq, hn, g_mix, wq):
    def body(h_ref, dh_ref, dq_ref, hn_ref, gm_ref, wq_ref, out_ref, dwq_ref, dbq_ref, dgm_ref, aq):
        i = pl.program_id(0)

        @pl.when(i == 0)
        def _():
            aq[...] = jnp.zeros_like(aq)
            dbq_ref[...] = jnp.zeros_like(dbq_ref)
            dgm_ref[...] = jnp.zeros_like(dgm_ref)

        dq_ = dq_ref[...]
        dqb = dq_.astype(bf16)
        aq[...] += _dot_tn(hn_ref[...], dqb)
        dbq_ref[...] += _colsum8(dq_)
        dx, dg = _rms_bwd(h_ref[...], gm_ref[...], _dot_nt(dqb, wq_ref[...]))
        out_ref[...] = dh_ref[...] + dx
        dgm_ref[...] += _colsum8(dg)

        @pl.when(i == NT - 1)
        def _():
            dwq_ref[...] = aq[...].astype(bf16)

    vec = _full((8, D))
    mat = _full((D, D))
    return pl.pallas_call(
        body, name="attn_bwd_q", grid=(NT,),
        in_specs=[_tile()] * 4 + [_full((1, D)), mat],
        out_specs=[_tile(), mat, vec, vec],
        out_shape=[SDS((T, D), f32), SDS((D, D), bf16), SDS((8, D), f32), SDS((8, D), f32)],
        scratch_shapes=[pltpu.VMEM((D, D), f32)],
        compiler_params=_cp(dimension_semantics=("arbitrary",)),
    )(h, dh, dq, hn, g_mix, wq)


def attn_bwd_kv(h, dh, dk4, dv4, kvn, g_kv, wkv, spread):
    def body(h_ref, dh_ref, dk_ref, dv_ref, kvn_ref, gkv_ref, wkv_ref, sp_ref, out_ref, outb_ref, dw_ref, db_ref,
             dgkv_ref, acc):
        i = pl.program_id(0)

        @pl.when(i == 0)
        def _():
            for r in (acc, db_ref, dgkv_ref):
                r[...] = jnp.zeros_like(r)

        dkv = jnp.concatenate([_dot_nt(dk_ref[...].astype(bf16), sp_ref[...]),
                               _dot_nt(dv_ref[...].astype(bf16), sp_ref[...])], axis=1)
        dkvb = dkv.astype(bf16)
        acc[...] += _dot_tn(kvn_ref[...], dkvb)
        db_ref[...] += _colsum8(dkv)
        dx, dg = _rms_bwd(h_ref[...], gkv_ref[...], _dot_nt(dkvb, wkv_ref[...]))
        out = dh_ref[...] + dx
        out_ref[...] = out
        outb_ref[...] = out.astype(bf16)
        dgkv_ref[...] += _colsum8(dg)

        @pl.when(i == NT - 1)
        def _():
            dw_ref[...] = acc[...].astype(bf16)

    return pl.pallas_call(
        body, name="attn_bwd_kv", grid=(NT,),
        in_specs=[_tile()] * 5 + [_full((1, D)), _full((D, 512)), _full((256, D))],
        out_specs=[_tile(), _tile(), _full((D, 512)), _full((8, 512)), _full((8, D))],
        out_shape=[SDS((T, D), f32), SDS((T, D), bf16), SDS((D, 512), bf16), SDS((8, 512), f32), SDS((8, D), f32)],
        scratch_shapes=[pltpu.VMEM((D, 512), f32)],
        compiler_params=_cp(dimension_semantics=("arbitrary",)),
    )(h, dh, dk4, dv4, kvn, g_kv, wkv, spread)


def final_loss(h, g, target):
    def body(h_ref, g_ref, t_ref, loss_ref, dh_ref, dhb_ref, dg_ref):
        i = pl.program_id(0)

        @pl.when(i == 0)
        def _():
            loss_ref[...] = jnp.zeros_like(loss_ref)
            dg_ref[...] = jnp.zeros_like(dg_ref)

        h_ = h_ref[...]
        g_ = g_ref[...]
        y, _ = _rms(h_, g_)
        diff = y - t_ref[...]
        per_tok = jnp.mean(diff * diff, axis=-1, keepdims=True)
        tot = 0.5 * jnp.sum(per_tok, axis=0, keepdims=True)
        lane = lax.broadcasted_iota(jnp.int32, (8, 128), 1)
        row = lax.broadcasted_iota(jnp.int32, (8, 128), 0)
        loss_ref[...] += jnp.where((lane == 0) & (row == 0), jnp.broadcast_to(tot, (8, 128)), 0.0)
        dx, dgt = _rms_bwd(h_, g_, diff * (1.0 / D))
        dh_ref[...] = dx
        dhb_ref[...] = dx.astype(bf16)
        dg_ref[...] += _colsum8(dgt)

    return pl.pallas_call(
        body, name="final_loss", grid=(NT,), in_specs=[_tile(), _full((1, D)), _tile()],
        out_specs=[_full((8, 128)), _tile(), _tile(), _full((8, D))],
        out_shape=[SDS((8, 128), f32), SDS((T, D), f32), SDS((T, D), bf16), SDS((8, D), f32)],
        compiler_params=_cp(dimension_semantics=("arbitrary",)),
    )(h, g, target)


def _to_chunked(a):
    return a.reshape(S5_CH, S5_STEPS, a.shape[-1]).transpose(1, 0, 2).reshape(T, a.shape[-1])


def _from_chunked(a):
    return a.reshape(S5_STEPS, S5_CH, a.shape[-1]).transpose(1, 0, 2).reshape(T, a.shape[-1])


def _rep4(w):
    return jnp.broadcast_to(w.reshape(w.shape[0], N_KV, 1, HEAD_DIM), (w.shape[0], N_KV, Q_PER_KV, HEAD_DIM)).reshape(
        w.shape[0], N_KV * Q_PER_KV * HEAD_DIM)


def _fold4(w):
    return w.reshape(w.shape[0], N_KV, Q_PER_KV, HEAD_DIM).sum(axis=2).reshape(w.shape[0], N_KV * HEAD_DIM)


def fwd_bwd(x, target, p, shards, opt, core, chip):
    row = lambda v: v.reshape(1, -1)
    (lam, bm, cm), prep_vjp = jax.vjp(s5_discretize, p["s5_a_re"][0], p["s5_a_im"][0], p["s5_log_dt"][0],
                                      p["s5_b_re"][0], p["s5_b_im"][0], p["s5_c_re"][0], p["s5_c_im"][0])
    bmb, cmb = bm.astype(bf16), cm.astype(bf16)
    lam = jnp.concatenate([lam, lam * jnp.array([1.0, -1.0], f32).reshape(1, 2, 1, 1)], axis=1)
    g_mix0, g_mix1 = row(p["norm_mix"][0]), row(p["norm_mix"][1])
    g_mlp0, g_mlp1 = row(p["norm_mlp"][0]), row(p["norm_mlp"][1])
    g_kv, g_fin = row(p["norm_kv"]), row(p["norm_final"])
    bq, bo = p["b_q"], p["b_o"]
    bkv = row(p["b_kv"])
    spread = _spread4()
    sinks = p["sinks"].reshape(16)

    def reduce_pairs(names, bg):
        return [add_pairs(g, r, core, f"add_pairs_{n}") for n, g, r in zip(names, bg.arrs, bg.result)]

    wglu, gvec = sc_gather([shards["s5_w_glu"], shards["vecs"]], 3, "sc_gather_s5")
    win0, wout0 = sc_gather([shards["w_in0"], shards["w_out0"]], 14, "sc_gather_mlp0")
    wkv, wq, wo, win1 = sc_gather([shards["w_kv"], shards["w_q"], shards["w_o"], shards["w_in1"]], 4, "sc_gather_attn")
    wout1, = sc_gather([shards["w_out1"]], 5, "sc_gather_w_out1")
    xp = _to_chunked(x)
    hn0 = s5_pre(xp, g_mix0)
    ys = s5_core_fwd(hn0, bmb, lam, cmb)
    d_skip = gvec[:, 0, :128].reshape(1, D)
    bglu = gvec[:, 0, 128:].reshape(1, 2 * D)
    y, z, h1 = s5_post(ys, xp, g_mix0, d_skip, wglu, bglu)
    hm0, r0, h2p = mlp_fwd(h1, g_mlp0, win0, wout0, 0)
    wkv, wq, wo = wkv.reshape(D, 512), wq.reshape(D, D), wo.reshape(D, D)
    h2 = _from_chunked(h2p)
    kvn, hn1, k4, v4, q = attn_pre(h2, g_kv, g_mix1, wkv, bkv, spread, wq, bq)
    o = attn_core_fwd(q, k4, v4, sinks)
    h3 = attn_post(h2, o, wo, bo)
    hm1, r1, h4 = mlp_fwd(h3, g_mlp1, win1, wout1, 1)
    loss, dh4, dh4b, dg_fin = final_loss(h4, g_fin, target)

    def pair_sums(names, grads, cid, before):
        r1 = sc_comm(BgPair(grads), cid, "sc_pair_" + names[0])
        parts = [add_pairs(g, r, core, f"add_pairs_{n}") for n, g, r in zip(names, grads, r1)]
        before, parts = lax.optimization_barrier((before, parts))
        return before, parts

    def across_chips(names, parts, cid):
        return list(zip(parts, sc_comm(BgChips(parts), cid, "sc_chips_" + names[0])))

    dh3, dwin1, dwout1, dg_mlp1 = mlp_bwd(h3, hm1, r1, g_mlp1, dh4, dh4b, win1, wout1, 1)
    do, dwo, dbo = attn_bwd_pre(dh3, o, wo)
    do, parts = pair_sums(["w_in1", "w_out1"], [dwin1, dwout1], 6, do)
    rs_in1, rs_out1 = across_chips(["w_in1", "w_out1"], parts, 7)
    dq, dk4, dv4, dsink = attn_core_bwd(q, do, k4, v4, sinks)
    dh2, dwq, dbq, dg_mix1 = attn_bwd_q(h2, dh3, dq, hn1, g_mix1, wq)
    dh2, dh2b, dwkv, dbkv, dg_kv = attn_bwd_kv(h2, dh2, dk4, dv4, kvn, g_kv, wkv, spread)
    dh2p, dh2pb = _to_chunked(dh2), _to_chunked(dh2b)
    big = {}
    a_in1 = adam_big(*opt["w_mlp_in"], *rs_in1, chip, "adam_w_mlp_in1", layer=1)
    a_out1 = adam_big(*opt["w_mlp_out"], *rs_out1, chip, "adam_w_mlp_out1", layer=1)
    dh2p, a_in1, a_out1 = lax.optimization_barrier((dh2p, a_in1, a_out1))
    names = ["w_kv", "w_q", "w_o"]
    dh2p, parts = pair_sums(names, [dwkv.reshape(NDEV, 128, 512), dwq.reshape(NDEV, 128, D),
                                    dwo.reshape(NDEV, 128, D)], 8, dh2p)
    rs_attn = across_chips(names, parts, 9)
    dh1, dwin0, dwout0, dg_mlp0 = mlp_bwd(h1, hm0, r0, g_mlp0, dh2p, dh2pb, win0, wout0, 0)
    a_attn = [adam_big(*opt[n], *rs, chip, f"adam_{n}") for n, rs in zip(names, rs_attn)]
    dh1, a_attn = lax.optimization_barrier((dh1, a_attn))
    big.update(zip(names, a_attn))
    dy, dwglu, dbglu = s5_post_bwd(dh1, y, z, wglu)
    dy, parts = pair_sums(["w_in0", "w_out0"], [dwin0, dwout0], 10, dy)
    rs_in0, rs_out0 = across_chips(["w_in0", "w_out0"], parts, 11)
    du, dbm, dcmt, dlam = s5_core_bwd(hn0, dy, bmb, lam, cmb)
    du, parts = pair_sums(["s5_w_glu"], [dwglu], 12, du)
    rs_glu, = across_chips(["s5_w_glu"], parts, 13)
    dxp, dg_mix0, dd = s5_pre_bwd(xp, g_mix0, du, dy, d_skip, dh1)
    big["w_mlp_in"] = adam_big(*opt["w_mlp_in"], *rs_in0, chip, "adam_w_mlp_in0", layer=0, prev=a_in1)
    big["w_mlp_out"] = adam_big(*opt["w_mlp_out"], *rs_out0, chip, "adam_w_mlp_out0", layer=0, prev=a_out1)
    big["s5_w_glu"] = adam_big(*opt["s5_w_glu"], *rs_glu, chip, "adam_s5_w_glu")
    grad_x = _from_chunked(dxp)
    da_re, da_im, dlog_dt, db_re, db_im, dc_re, dc_im = prep_vjp((dlam, dbm, dcmt.transpose(0, 2, 1)))

    def lanes(v_):
        v_ = v_.reshape(1, -1)
        return jnp.pad(v_, ((0, 0), (0, D - v_.shape[1])))

    small = jnp.concatenate([
        dg_mix0[0:1], dg_mix1[0:1], dg_mlp0[0:1], dg_mlp1[0:1], dg_kv[0:1], dg_fin[0:1], dd[0:1], dbq[0:1], dbo[0:1],
        dbglu[0:1].reshape(2, D), lanes(dbkv[0:1]),
        lanes(dsink[:, 0, :Q_PER_KV]), lanes(dlog_dt), lanes(loss[0:1, 0:1]), jnp.zeros((1, D), f32),
        da_re.reshape(4, D), da_im.reshape(4, D),
        db_re.transpose(0, 2, 1).reshape(64, D), db_im.transpose(0, 2, 1).reshape(64, D),
        dc_re.reshape(64, D), dc_im.reshape(64, D)], axis=0)
    small, big["w_mlp_in"], big["w_mlp_out"] = lax.optimization_barrier((small, big["w_mlp_in"], big["w_mlp_out"]))
    return loss, grad_x, small, big


_ANY = pl.BlockSpec(memory_space=pl.ANY)


def _pos():
    return lax.axis_index("x"), lax.axis_index("y"), lax.axis_index("c")


def _other_chips(x, y):
    return [(1 - x, y), (x, 1 - y), (1 - x, 1 - y)]


def all_gather(arrs):
    n = len(arrs)

    def body(*refs):
        ins, outs = refs[:n], refs[n:2 * n]
        send_sems, recv_sems, local_sems = refs[2 * n:]
        x, y, c = _pos()
        me, sib = (x, y, c), (x, y, 1 - c)
        chips = _other_chips(x, y)

        def copy(a, k, block, to, src=None):
            dst = outs[a].at[4 * block[0] + 2 * block[1] + block[2]]
            return pltpu.make_async_remote_copy(
                src_ref=dst if src is None else src, dst_ref=dst, send_sem=send_sems.at[a, k],
                recv_sem=recv_sems.at[a, k], device_id=to, device_id_type=MESH)

        mine = [pltpu.make_async_copy(ins[a], outs[a].at[4 * x + 2 * y + c], local_sems.at[a]) for a in range(n)]
        for cp in mine:
            cp.start()
        first = []
        for a in range(n):
            first.append(copy(a, 0, me, sib, src=ins[a]))
            first += [copy(a, 1 + j, me, (*chip, c), src=ins[a]) for j, chip in enumerate(chips)]
        for cp in first:
            cp.start()
        passed = []
        for j, chip in enumerate(chips):
            for a in range(n):
                copy(a, 1 + j, (*chip, c), me).wait_recv()
                cp = copy(a, 4 + j, (*chip, c), sib)
                cp.start()
                passed.append(cp)
        for a in range(n):
            copy(a, 0, sib, me).wait_recv()
            for j, chip in enumerate(chips):
                copy(a, 4 + j, (*chip, 1 - c), me).wait_recv()
        for cp in first + passed:
            cp.wait_send()
        for cp in mine:
            cp.wait()

    return pl.pallas_call(
        body, name="all_gather", in_specs=[_ANY] * n, out_specs=[_ANY] * n,
        out_shape=[SDS((NDEV,) + a.shape, a.dtype) for a in arrs],
        scratch_shapes=[pltpu.SemaphoreType.DMA((n, 7)), pltpu.SemaphoreType.DMA((n, 7)),
                        pltpu.SemaphoreType.DMA((n,))],
    )(*arrs)


def rs_pair(grads):
    n = len(grads)

    def body(*refs):
        ins, outs = refs[:n], refs[n:2 * n]
        send_sems, recv_sems = refs[2 * n:]
        x, y, c = _pos()
        cps = []
        for a in range(n):
            for k in range(4):
                cps.append(pltpu.make_async_remote_copy(
                    src_ref=ins[a].at[2 * k + 1 - c], dst_ref=outs[a].at[k], send_sem=send_sems.at[a, k],
                    recv_sem=recv_sems.at[a, k], device_id=(x, y, 1 - c), device_id_type=MESH))
        for cp in cps:
            cp.start()
        for cp in cps:
            cp.wait_recv()
        for cp in cps:
            cp.wait_send()

    return pl.pallas_call(
        body, name="rs_pair", in_specs=[_ANY] * n, out_specs=[_ANY] * n,
        out_shape=[SDS((4,) + g.shape[1:], g.dtype) for g in grads],
        scratch_shapes=[pltpu.SemaphoreType.DMA((n, 4)), pltpu.SemaphoreType.DMA((n, 4))],
    )(*grads)


def rs_chips(parts):
    n = len(parts)

    def body(*refs):
        ins, outs = refs[:n], refs[n:2 * n]
        send_sems, recv_sems = refs[2 * n:]
        x, y, c = _pos()
        cps = []
        for a in range(n):
            for r, (px, py) in enumerate(_other_chips(x, y)):
                cps.append(pltpu.make_async_remote_copy(
                    src_ref=ins[a].at[2 * px + py], dst_ref=outs[a].at[r], send_sem=send_sems.at[a, r],
                    recv_sem=recv_sems.at[a, r], device_id=(px, py, c), device_id_type=MESH))
        for cp in cps:
            cp.start()
        for cp in cps:
            cp.wait_recv()
        for cp in cps:
            cp.wait_send()

    return pl.pallas_call(
        body, name="rs_chips", in_specs=[_ANY] * n, out_specs=[_ANY] * n,
        out_shape=[SDS((3,) + g.shape[1:], g.dtype) for g in parts],
        scratch_shapes=[pltpu.SemaphoreType.DMA((n, 3)), pltpu.SemaphoreType.DMA((n, 3))],
    )(*parts)


def _row_tile(r, c):
    return min(r, max(8, (512 * 1024) // c))


def add_pairs(g, r1, core, name):
    _, R, C = g.shape
    tr = _row_tile(R, C)

    def body(core_ref, g_ref, r_ref, o_ref):
        o_ref[...] = (g_ref[...].astype(f32) + r_ref[...].astype(f32)).astype(bf16)

    return pl.pallas_call(
        body, name=name, out_shape=SDS((4, R, C), bf16),
        grid_spec=pltpu.PrefetchScalarGridSpec(
            num_scalar_prefetch=1, grid=(4, R // tr),
            in_specs=[pl.BlockSpec((None, tr, C), lambda k, i, core: (2 * k + core[0], i, 0)),
                      pl.BlockSpec((None, tr, C), lambda k, i, core: (k, i, 0))],
            out_specs=pl.BlockSpec((None, tr, C), lambda k, i, core: (k, i, 0))),
        compiler_params=_cp(dimension_semantics=("arbitrary", "arbitrary")),
    )(core, g, r1)


def _adamw(w, g, m, v):
    m = ADAM_B1 * m + (1.0 - ADAM_B1) * g
    v = ADAM_B2 * v + (1.0 - ADAM_B2) * (g * g)
    m_hat = m / (1.0 - ADAM_B1 ** ADAM_STEP)
    v_hat = v / (1.0 - ADAM_B2 ** ADAM_STEP)
    delta = -ADAM_LR * (m_hat / (jnp.sqrt(v_hat) + ADAM_EPS) + ADAM_WD * w)
    return delta, m, v


def adam_big(w, m, v, part, r2, chip, name, layer=0, prev=None):
    L, R, C = w.shape
    tr = _row_tile(R, C)

    def body(chip_ref, w_ref, m_ref, v_ref, p_ref, r_ref, *rest):
        g_out, d_out, m_out, v_out = rest[-4:]
        g = p_ref[...].astype(f32) + r_ref[0].astype(f32) + r_ref[1].astype(f32) + r_ref[2].astype(f32)
        d, m_, v_ = _adamw(w_ref[...], g, m_ref[...], v_ref[...])
        g_out[...] = g
        d_out[...] = d
        m_out[...] = m_
        v_out[...] = v_

    blk = pl.BlockSpec((None, tr, C), lambda i, chip: (layer, i, 0))
    extra = [] if prev is None else list(prev)
    return pl.pallas_call(
        body, name=name, out_shape=[SDS((L, R, C), f32)] * 4,
        grid_spec=pltpu.PrefetchScalarGridSpec(
            num_scalar_prefetch=1, grid=(R // tr,),
            in_specs=[blk, blk, blk,
                      pl.BlockSpec((None, tr, C), lambda i, chip: (chip[0], i, 0)),
                      pl.BlockSpec((3, tr, C), lambda i, chip: (0, i, 0))] + [_ANY] * len(extra),
            out_specs=[blk] * 4),
        input_output_aliases={6 + k: k for k in range(len(extra))},
        compiler_params=_cp(dimension_semantics=("arbitrary",)),
    )(chip, w, m, v, part, r2, *extra)


def allreduce_small(buf, chips=None):
    shp = buf.shape
    half = (shp[0] // 16) * 8
    parts = (pl.ds(0, half), pl.ds(half, shp[0] - half))
    n_c = 0 if chips is None else len(chips.arrs)

    def body(in_ref, *refs):
        c_in, out_ref, c_out = refs[:n_c], refs[n_c], refs[n_c + 1:2 * n_c + 1]
        acc1, acc2, r0, r1, r2, send_sems, recv_sems = refs[2 * n_c + 1:2 * n_c + 8]
        c_sems = refs[2 * n_c + 8:]
        if chips is not None:
            chips.start(c_in, c_out, c_sems)
        x, y, c = _pos()
        across = [(1 - x, y, c), (x, 1 - y, c)]

        def exchange(src, rcv, dst, copies):
            cps = [pltpu.make_async_remote_copy(
                src_ref=src.at[rows], dst_ref=rcv.at[rows], send_sem=send_sems.at[k], recv_sem=recv_sems.at[k],
                device_id=peer, device_id_type=MESH) for k, rows, peer in copies]
            for cp in cps:
                cp.start()
            for cp in cps:
                cp.wait()
            dst[...] = src[...] + rcv[...]

        exchange(in_ref, r0, acc1, [(0, pl.ds(0, shp[0]), (x, y, 1 - c))])
        exchange(acc1, r1, acc2, [(1, parts[0], across[0]), (2, parts[1], across[1])])
        exchange(acc2, r2, out_ref, [(3, parts[0], across[1]), (4, parts[1], across[0])])
        if chips is not None:
            chips.finish(c_in, c_out, c_sems)

    vm = pl.BlockSpec(memory_space=pltpu.VMEM)
    res = pl.pallas_call(
        body, name="allreduce_small", in_specs=[vm] + [_ANY] * n_c, out_specs=[vm] + [_ANY] * n_c,
        out_shape=[SDS(shp, f32)] + ([] if chips is None else chips.out_shape),
        scratch_shapes=[pltpu.VMEM(shp, f32)] * 5 + [pltpu.SemaphoreType.DMA((5,)), pltpu.SemaphoreType.DMA((5,))]
        + ([] if chips is None else chips.scratch),
    )(buf, *([] if chips is None else chips.arrs))
    if chips is not None:
        chips.result = list(res[1:])
    return res[0]


SMALL_ROWS = {'norm_mix': (0, 2, D), 'norm_mlp': (2, 2, D), 'norm_kv': (4, 1, D), 'norm_final': (5, 1, D),
              's5_d': (6, 1, D), 'b_q': (7, 1, D), 'b_o': (8, 1, D), 's5_b_glu': (9, 2, D), 'b_kv': (11, 1, 512),
              'sinks': (12, 1, 16), 's5_log_dt': (13, 1, 64), 's5_a_re': (16, 4, D), 's5_a_im': (20, 4, D),
              's5_b_re': (24, 64, D), 's5_b_im': (88, 64, D), 's5_c_re': (152, 64, D), 's5_c_im': (216, 64, D)}
LOSS_ROW = 14
ROW_PARAMS = ['norm_mix', 'norm_mlp', 'norm_kv', 'norm_final', 'b_q', 'b_o', 'b_kv', 'sinks', 's5_log_dt']
SHARD_PARAMS = ['s5_d', 's5_b_glu']
S5_PARAMS = ['s5_a_re', 's5_a_im', 's5_b_re', 's5_b_im', 's5_c_re', 's5_c_im']


def adam_small(dev, gsum, s5_grads, w, m, v):
    names = ROW_PARAMS + SHARD_PARAMS + S5_PARAMS
    n_g = len(ROW_PARAMS) + len(SHARD_PARAMS)

    def body(dev_ref, gs_ref, *refs):
        pos = [0]

        def take(k):
            r = refs[pos[0]:pos[0] + k]
            pos[0] += k
            return r

        g5 = take(len(S5_PARAMS))
        wr, mr, vr = take(len(names)), take(len(names)), take(len(names))
        g_out = take(n_g)
        d_out, m_out, v_out = take(len(names)), take(len(names)), take(len(names))
        dv = dev_ref[0]
        for i, n in enumerate(names):
            if n in S5_PARAMS:
                g = g5[S5_PARAMS.index(n)][...]
            elif n in SHARD_PARAMS:
                r0, _, _ = SMALL_ROWS[n]
                ln = wr[i].shape[1]
                g = jnp.zeros((1, ln), f32)
                for k in range(NDEV):
                    off = k * ln
                    piece = gs_ref[r0 + off // D:r0 + off // D + 1, off % D:off % D + ln]
                    g = g + jnp.where(dv == k, piece, 0.0)
                g_out[i][...] = g
            else:
                r0, nr, nl = SMALL_ROWS[n]
                g = gs_ref[r0:r0 + nr, 0:nl]
                g_out[i][...] = g
            d, m_, v_ = _adamw(wr[i][...], g, mr[i][...], vr[i][...])
            d_out[i][...] = d
            m_out[i][...] = m_
            v_out[i][...] = v_

    vm = pl.BlockSpec(memory_space=pltpu.VMEM)
    ins = [s5_grads[n] for n in S5_PARAMS] + [d[n] for d in (w, m, v) for n in names]
    shapes = [SDS(w[n].shape, f32) for n in names]
    res = pl.pallas_call(
        body, name="adam_small", in_specs=[pl.BlockSpec(memory_space=pltpu.SMEM)] + [vm] * (1 + len(ins)),
        out_specs=[vm] * (n_g + 3 * len(names)), out_shape=shapes[:n_g] + shapes * 3,
        compiler_params=_cp(),
    )(dev, gsum, *ins)
    g_o = dict(zip(names[:n_g], res[:n_g]))
    rest = res[n_g:]
    k = len(names)
    return g_o, dict(zip(names, rest[:k])), dict(zip(names, rest[k:2 * k])), dict(zip(names, rest[2 * k:]))


WEIGHTS = ['norm_mix', 'norm_mlp', 'norm_kv', 'norm_final', 's5_a_re', 's5_a_im', 's5_log_dt', 's5_b_re', 's5_b_im',
           's5_c_re', 's5_c_im', 's5_d', 's5_w_glu', 's5_b_glu', 'w_kv', 'b_kv', 'w_q', 'b_q', 'sinks', 'w_o', 'b_o',
           'w_mlp_in', 'w_mlp_out']
BIG = ['s5_w_glu', 'w_kv', 'w_q', 'w_o', 'w_mlp_in', 'w_mlp_out']
BIG_2D = {'s5_w_glu': (D, 256), 'w_kv': (128, 512), 'w_q': (128, D), 'w_o': (128, D), 'w_mlp_in': (2 * D, 512),
          'w_mlp_out': (2 * 512, D)}
SHARDED_SMALL = {'s5_d': D, 's5_b_glu': 2 * D}
SMALL = [n for n in WEIGHTS if n not in BIG]
SMALL_SIZE = {'norm_mix': 2 * D, 'norm_mlp': 2 * D, 'norm_kv': D, 'norm_final': D, 's5_a_re': 4096, 's5_a_im': 4096,
              's5_log_dt': 64, 's5_b_re': 65536, 's5_b_im': 65536, 's5_c_re': 65536, 's5_c_im': 65536, 's5_d': D,
              's5_b_glu': 2 * D, 'b_kv': 512, 'b_q': D, 'sinks': 16, 'b_o': D}


def _pack(vals):
    parts = []
    for n in SMALL:
        v = vals[n].reshape(-1).astype(f32)
        parts.append(jnp.pad(v, (0, (-v.shape[0]) % 128)))
    flat = jnp.concatenate(parts)
    flat = jnp.pad(flat, (0, (-flat.shape[0]) % 1024))
    return flat.reshape(-1, 128)


def _unpack(buf):
    flat = buf.reshape(-1)
    out, off = {}, 0
    for n in SMALL:
        sz = SMALL_SIZE[n]
        out[n] = flat[off:off + sz]
        off += sz + (-sz) % 128
    return out


def kernel(x, norm_mix, norm_mlp, norm_kv, norm_final, s5_a_re, s5_a_im, s5_log_dt, s5_b_re, s5_b_im, s5_c_re, s5_c_im, s5_d, s5_w_glu, s5_b_glu, w_kv, b_kv, w_q, b_q, sinks, w_o, b_o, w_mlp_in, w_mlp_out, loss_target, m_norm_mix, m_norm_mlp, m_norm_kv, m_norm_final, m_s5_a_re, m_s5_a_im, m_s5_log_dt, m_s5_b_re, m_s5_b_im, m_s5_c_re, m_s5_c_im, m_s5_d, m_s5_w_glu, m_s5_b_glu, m_w_kv, m_b_kv, m_w_q, m_b_q, m_sinks, m_w_o, m_b_o, m_w_mlp_in, m_w_mlp_out, v_norm_mix, v_norm_mlp, v_norm_kv, v_norm_final, v_s5_a_re, v_s5_a_im, v_s5_log_dt, v_s5_b_re, v_s5_b_im, v_s5_c_re, v_s5_c_im, v_s5_d, v_s5_w_glu, v_s5_b_glu, v_w_kv, v_b_kv, v_w_q, v_b_q, v_sinks, v_w_o, v_b_o, v_w_mlp_in, v_w_mlp_out):
    w = dict(norm_mix=norm_mix, norm_mlp=norm_mlp, norm_kv=norm_kv, norm_final=norm_final, s5_a_re=s5_a_re,
             s5_a_im=s5_a_im, s5_log_dt=s5_log_dt, s5_b_re=s5_b_re, s5_b_im=s5_b_im, s5_c_re=s5_c_re, s5_c_im=s5_c_im,
             s5_d=s5_d, s5_w_glu=s5_w_glu, s5_b_glu=s5_b_glu, w_kv=w_kv, b_kv=b_kv, w_q=w_q, b_q=b_q, sinks=sinks,
             w_o=w_o, b_o=b_o, w_mlp_in=w_mlp_in, w_mlp_out=w_mlp_out)
    m = dict(norm_mix=m_norm_mix, norm_mlp=m_norm_mlp, norm_kv=m_norm_kv, norm_final=m_norm_final, s5_a_re=m_s5_a_re,
             s5_a_im=m_s5_a_im, s5_log_dt=m_s5_log_dt, s5_b_re=m_s5_b_re, s5_b_im=m_s5_b_im, s5_c_re=m_s5_c_re,
             s5_c_im=m_s5_c_im, s5_d=m_s5_d, s5_w_glu=m_s5_w_glu, s5_b_glu=m_s5_b_glu, w_kv=m_w_kv, b_kv=m_b_kv,
             w_q=m_w_q, b_q=m_b_q, sinks=m_sinks, w_o=m_w_o, b_o=m_b_o, w_mlp_in=m_w_mlp_in, w_mlp_out=m_w_mlp_out)
    v = dict(norm_mix=v_norm_mix, norm_mlp=v_norm_mlp, norm_kv=v_norm_kv, norm_final=v_norm_final, s5_a_re=v_s5_a_re,
             s5_a_im=v_s5_a_im, s5_log_dt=v_s5_log_dt, s5_b_re=v_s5_b_re, s5_b_im=v_s5_b_im, s5_c_re=v_s5_c_re,
             s5_c_im=v_s5_c_im, s5_d=v_s5_d, s5_w_glu=v_s5_w_glu, s5_b_glu=v_s5_b_glu, w_kv=v_w_kv, b_kv=v_b_kv,
             w_q=v_w_q, b_q=v_b_q, sinks=v_sinks, w_o=v_w_o, b_o=v_b_o, w_mlp_in=v_w_mlp_in, w_mlp_out=v_w_mlp_out)
    xi, yi, ci = _pos()
    dev = 4 * xi + 2 * yi + ci
    core = ci.reshape(1).astype(jnp.int32)
    chip = (2 * xi + yi).reshape(1).astype(jnp.int32)

    shards = {
        "s5_w_glu": s5_w_glu[0].astype(bf16), "w_kv": w_kv.astype(bf16), "w_q": w_q[0].astype(bf16),
        "w_o": w_o[0].astype(bf16), "w_in0": w_mlp_in[0].astype(bf16), "w_in1": w_mlp_in[1].astype(bf16),
        "w_out0": w_mlp_out[0].astype(bf16), "w_out1": w_mlp_out[1].astype(bf16),
        "vecs": jnp.broadcast_to(jnp.concatenate([s5_d, s5_b_glu], axis=1), (8, 384)),
    }
    as3d = lambda a, n: a if a.ndim == 3 and a.shape[0] == 2 else a.reshape((1,) + BIG_2D[n])
    opt = {n: (as3d(w[n], n), as3d(m[n], n), as3d(v[n], n)) for n in BIG}
    _, grad_x, grads, big = fwd_bwd(x[0], loss_target[0], {n: w[n] for n in SMALL}, shards, opt, core, chip)

    gsum = allreduce_small(grads)

    out_g, out_d, out_m, out_v = {}, {}, {}, {}
    for n in BIG:
        out_g[n], out_d[n], out_m[n], out_v[n] = [r.reshape(w[n].shape) for r in big[n]]

    loss = gsum[LOSS_ROW, 0]
    swapped = ("s5_b_re", "s5_b_im")
    swap = lambda a: a.transpose(0, 1, 3, 2)

    def kernel_side(d):
        d = {n: (d[n].reshape(1, -1) if d[n].ndim == 1 else d[n]) for n in SMALL}
        d.update({n: swap(d[n]) for n in swapped})
        return d

    s5_g = {}
    for n in S5_PARAMS:
        r0, nr, _ = SMALL_ROWS[n]
        s5_g[n] = gsum[r0:r0 + nr].reshape((1, 64, 16, 64) if n in swapped else w[n].shape)
        out_g[n] = s5_g[n]
    g_s, d_s, m_s, v_s = adam_small(dev.reshape(1).astype(jnp.int32), gsum, s5_g, kernel_side(w), kernel_side(m),
                                    kernel_side(v))
    for src, dst in ((g_s, out_g), (d_s, out_d), (m_s, out_m), (v_s, out_v)):
        dst.update(src)
    for dst in (out_g, out_d, out_m, out_v):
        for n in SMALL:
            dst[n] = (swap(dst[n]) if n in swapped else dst[n]).reshape(w[n].shape)

    return (loss, grad_x[None], *[out_g[n] for n in WEIGHTS], *[out_d[n] for n in WEIGHTS],
            *[out_m[n] for n in WEIGHTS], *[out_v[n] for n in WEIGHTS])
```

```python
import functools
import math

import jax
import jax.numpy as jnp
from jax import lax
from jax.experimental import pallas as pl
from jax.experimental.pallas import tpu as pltpu
from jax.experimental.pallas import tpu_sc as plsc

f32 = jnp.float32
bf16 = jnp.bfloat16
SDS = jax.ShapeDtypeStruct

T = 2048
D = 1024
NDEV = 8
NORM_EPS = 1e-5
S5_G, S5_C, S5_P = 64, 16, 64
S5_SUB = 8
S5_CH = 8
S5_STEPS = T // S5_CH
DT_MIN_LAMBDA = -1e-4
HEAD_DIM = 64
N_KV = 4
Q_PER_KV = 4
BLK = 128
D_FF_SHARD = 512
ADAM_LR, ADAM_B1, ADAM_B2, ADAM_EPS, ADAM_WD, ADAM_STEP = 0.001, 0.9, 0.999, 1e-08, 0.01, 10
VMEM_LIMIT = 56 * 1024 * 1024
MESH = pl.DeviceIdType.MESH


def _cp(**kw):
    return pltpu.CompilerParams(vmem_limit_bytes=VMEM_LIMIT, **kw)


def _dot(a, b):
    return jnp.dot(a, b, preferred_element_type=f32)


def _dot_nt(a, b):
    return lax.dot_general(a, b, (((1,), (1,)), ((), ())), preferred_element_type=f32)


def _dot_tn(a, b):
    return lax.dot_general(a, b, (((0,), (0,)), ((), ())), preferred_element_type=f32)


def _rms(x, g):
    r = lax.rsqrt(jnp.mean(x * x, axis=-1, keepdims=True) + NORM_EPS)
    return x * r * g, r


def _rms_bwd(x, g, dy):
    r = lax.rsqrt(jnp.mean(x * x, axis=-1, keepdims=True) + NORM_EPS)
    u = dy * g
    dx = r * u - (r * r * r) * x * jnp.mean(u * x, axis=-1, keepdims=True)
    return dx, dy * x * r


def _colsum8(v):
    s = jnp.sum(v, axis=0, keepdims=True)
    row = lax.broadcasted_iota(jnp.int32, (8, v.shape[1]), 0)
    return jnp.where(row == 0, jnp.broadcast_to(s, (8, v.shape[1])), 0.0)


def _full(shape):
    nd = len(shape)
    return pl.BlockSpec(shape, lambda *_: (0,) * nd, pipeline_mode=pl.Buffered(1))


_ANY = pl.BlockSpec(memory_space=pl.ANY)


def _pos():
    return lax.axis_index("x"), lax.axis_index("y"), lax.axis_index("c")


def _other_chips(x, y):
    return [(1 - x, y), (x, 1 - y), (1 - x, 1 - y)]


class BgGather:
    SIB, XN, YN, FWD_Y, FWD_X, SIB_X, SIB_Y, SIB_D = range(8)

    def __init__(self, arrs, mids=(0.5, 0.75)):
        n = len(arrs)
        self.arrs = list(arrs)
        self.out_shape = [SDS((NDEV,) + a.shape, a.dtype) for a in arrs]
        self.scratch = [pltpu.SemaphoreType.DMA((n, 8)), pltpu.SemaphoreType.DMA((n, 8)),
                        pltpu.SemaphoreType.DMA((n,))]
        self.mids = mids
        self.result = None

    @staticmethod
    def peers(x, y, c):
        return [(x, y, 1 - c), (1 - x, y, c), (x, 1 - y, c)]

    def mid_steps(self, nsteps):
        at = lambda f: min(nsteps - 1, max(0, int(f * nsteps) - 1))
        return [(at(self.mids[0]), self.mid), (max(at(self.mids[0]), at(self.mids[1])), self.mid2)]

    def _halves(self, a):
        rows = self.arrs[a].shape[0]
        cut = (rows // 32) * 16 if rows >= 32 else rows
        return (0, cut), (cut, rows - cut)

    def _copy(self, ins, outs, sems, a, k, block, to, own=False, part=None):
        slot = 4 * block[0] + 2 * block[1] + block[2]
        rows = pl.ds(0, self.arrs[a].shape[0]) if part is None else pl.ds(*self._halves(a)[part])
        dst = outs[a].at[slot, rows]
        return pltpu.make_async_remote_copy(
            src_ref=ins[a].at[rows] if own else dst, dst_ref=dst, send_sem=sems[0].at[a, k],
            recv_sem=sems[1].at[a, k], device_id=to, device_id_type=MESH)

    def _mine(self, ins, outs, sems):
        x, y, c = _pos()
        return [pltpu.make_async_copy(ins[a], outs[a].at[4 * x + 2 * y + c], sems[2].at[a])
                for a in range(len(self.arrs))]

    def _split(self, a):
        return self._halves(a)[1][1] > 0

    def _sends(self, ins, outs, sems, phase):
        x, y, c = _pos()
        me, sib, xn, yn, dg = (x, y, c), (x, y, 1 - c), (1 - x, y, c), (x, 1 - y, c), (1 - x, 1 - y, c)
        cps = []
        for a in range(len(self.arrs)):
            cp = lambda k, block, to, **kw: self._copy(ins, outs, sems, a, k, block, to, **kw)
            if phase == 0:
                cps += [cp(self.SIB, me, sib, own=True), cp(self.XN, me, xn, own=True), cp(self.YN, me, yn, own=True)]
            elif phase == 1:
                cps.append(cp(self.FWD_Y, xn, yn, part=0))
                if self._split(a):
                    cps.append(cp(self.FWD_X, yn, xn, part=1))
                cps += [cp(self.SIB_X, xn, sib), cp(self.SIB_Y, yn, sib)]
            else:
                cps.append(cp(self.SIB_D, dg, sib))
        return cps

    def _arrivals(self, ins, outs, sems, phase):
        x, y, c = _pos()
        me, xn, yn, dg = (x, y, c), (1 - x, y, c), (x, 1 - y, c), (1 - x, 1 - y, c)
        cps = []
        for a in range(len(self.arrs)):
            cp = lambda k, block, **kw: self._copy(ins, outs, sems, a, k, block, me, **kw)
            if phase == 1:
                cps += [cp(self.XN, xn), cp(self.YN, yn)]
            elif phase == 2:
                cps.append(cp(self.FWD_Y, dg, part=0))
                if self._split(a):
                    cps.append(cp(self.FWD_X, dg, part=1))
            else:
                cps += [cp(self.SIB, (x, y, 1 - c)), cp(self.SIB_X, (1 - x, y, 1 - c)),
                        cp(self.SIB_Y, (x, 1 - y, 1 - c)), cp(self.SIB_D, (1 - x, 1 - y, 1 - c))]
        return cps

    def start(self, ins, outs, sems):
        for cp in self._mine(ins, outs, sems) + self._sends(ins, outs, sems, 0):
            cp.start()

    def mid(self, ins, outs, sems):
        for cp in self._arrivals(ins, outs, sems, 1):
            cp.wait_recv()
        for cp in self._sends(ins, outs, sems, 1):
            cp.start()

    def mid2(self, ins, outs, sems):
        for cp in self._arrivals(ins, outs, sems, 2):
            cp.wait_recv()
        for cp in self._sends(ins, outs, sems, 2):
            cp.start()

    def finish(self, ins, outs, sems):
        for cp in self._arrivals(ins, outs, sems, 3):
            cp.wait_recv()
        for ph in range(3):
            for cp in self._sends(ins, outs, sems, ph):
                cp.wait_send()
        for cp in self._mine(ins, outs, sems):
            cp.wait()


def sc_comm(g, collective_id, name):
    srcs = [jax.new_ref(a, memory_space=pltpu.MemorySpace.HBM) for a in g.arrs]
    dsts = [jax.empty_ref(s, memory_space=pltpu.MemorySpace.HBM) for s in g.out_shape]

    @pl.kernel(mesh=plsc.ScalarSubcoreMesh(axis_name="sequencer", num_cores=1), name=name,
               scratch_types=tuple(g.scratch), compiler_params=pltpu.CompilerParams(collective_id=collective_id))
    def launch(*sems):
        peers = g.peers(*_pos())
        barrier = pltpu.get_barrier_semaphore()
        for peer in peers:
            pl.semaphore_signal(barrier, inc=1, device_id=peer, device_id_type=MESH)
        pl.semaphore_wait(barrier, len(peers))
        g.start(srcs, dsts, sems)
        for _, phase in g.mid_steps(1):
            phase(srcs, dsts, sems)
        g.finish(srcs, dsts, sems)

    launch()
    return [d[...] for d in dsts]


def sc_gather(arrs, collective_id, name):
    return sc_comm(BgGather(arrs), collective_id, name)


class BgPair:
    def __init__(self, arrs):
        n = len(arrs)
        self.arrs = list(arrs)
        self.out_shape = [SDS((4,) + a.shape[1:], a.dtype) for a in arrs]
        self.scratch = [pltpu.SemaphoreType.DMA((n, 4)), pltpu.SemaphoreType.DMA((n, 4))]
        self.result = None

    @staticmethod
    def peers(x, y, c):
        return [(x, y, 1 - c)]

    def mid_steps(self, nsteps):
        return []

    def _copies(self, ins, outs, sems):
        x, y, c = _pos()
        return [pltpu.make_async_remote_copy(
            src_ref=ins[a].at[2 * k + 1 - c], dst_ref=outs[a].at[k], send_sem=sems[0].at[a, k],
            recv_sem=sems[1].at[a, k], device_id=(x, y, 1 - c), device_id_type=MESH)
            for a in range(len(self.arrs)) for k in range(4)]

    def start(self, ins, outs, sems):
        for cp in self._copies(ins, outs, sems):
            cp.start()

    def finish(self, ins, outs, sems):
        cps = self._copies(ins, outs, sems)
        for cp in cps:
            cp.wait_recv()
        for cp in cps:
            cp.wait_send()


class BgChips(BgPair):
    def __init__(self, arrs):
        n = len(arrs)
        self.arrs = list(arrs)
        self.out_shape = [SDS((3,) + a.shape[1:], a.dtype) for a in arrs]
        self.scratch = [pltpu.SemaphoreType.DMA((n, 3)), pltpu.SemaphoreType.DMA((n, 3))]
        self.result = None

    @staticmethod
    def peers(x, y, c):
        return [(px, py, c) for px, py in _other_chips(x, y)]

    def _copies(self, ins, outs, sems):
        x, y, c = _pos()
        return [pltpu.make_async_remote_copy(
            src_ref=ins[a].at[2 * px + py], dst_ref=outs[a].at[r], send_sem=sems[0].at[a, r],
            recv_sem=sems[1].at[a, r], device_id=(px, py, c), device_id_type=MESH)
            for a in range(len(self.arrs)) for r, (px, py) in enumerate(_other_chips(x, y))]


class AdamRider:
    def __init__(self, w, m, v, part, r2, layer=0, prev=None):
        self.arrs = [w, m, v, part, r2] + list(prev or [])
        self.n_prev = len(prev or [])
        self.layer = layer
        self.out_shape = [SDS(w.shape, f32)] * 4
        self.scratch = []
        self.aliases = {5 + k: k for k in range(self.n_prev)}
        self.result = None

    def _tile(self, grid):
        assert len(grid) == 1
        _, R, C = self.arrs[0].shape
        return R // grid[0], C

    def in_specs(self, grid):
        tr, C = self._tile(grid)
        layer = self.layer
        blk = pl.BlockSpec((None, tr, C), lambda b: (layer, b, 0))
        mine = pl.BlockSpec((None, tr, C), lambda b: (2 * lax.axis_index("x") + lax.axis_index("y"), b, 0))
        return [blk, blk, blk, mine, pl.BlockSpec((3, tr, C), lambda b: (0, b, 0))] + [_ANY] * self.n_prev

    def out_specs(self, grid):
        tr, C = self._tile(grid)
        layer = self.layer
        return [pl.BlockSpec((None, tr, C), lambda b: (layer, b, 0))] * 4

    def mid_steps(self, nsteps):
        return []

    def start(self, ins, outs, sems):
        pass

    finish = start

    def step(self, ins, outs, sems):
        w_ref, m_ref, v_ref, p_ref, r_ref = ins[:5]
        g = p_ref[...].astype(f32) + r_ref[0].astype(f32) + r_ref[1].astype(f32) + r_ref[2].astype(f32)
        d, m_, v_ = _adamw(w_ref[...], g, m_ref[...], v_ref[...])
        for ref, val in zip(outs, (g, d, m_, v_)):
            ref[...] = val


def _call(bgs, body, *, name, grid, in_specs, out_specs, out_shape, scratch_shapes=(), compiler_params=None):
    single = not isinstance(out_shape, (list, tuple))
    out_specs_l = [out_specs] if single else list(out_specs)
    out_shape_l = [out_shape] if single else list(out_shape)
    bgs = [b for b in (bgs or []) if b is not None]
    n_in, n_out, n_sc = len(in_specs), len(out_shape_l), len(scratch_shapes)
    nsteps = math.prod(grid)
    b_in_specs = [b.in_specs(grid) if hasattr(b, "in_specs") else [_ANY] * len(b.arrs) for b in bgs]
    b_out_specs = [b.out_specs(grid) if hasattr(b, "out_specs") else [_ANY] * len(b.out_shape) for b in bgs]
    aliases, i_off, o_off = {}, n_in, n_out
    for b in bgs:
        aliases.update({i_off + i: o_off + o for i, o in getattr(b, "aliases", {}).items()})
        i_off, o_off = i_off + len(b.arrs), o_off + len(b.out_shape)

    def full(*refs):
        pos = [0]

        def take(k):
            r = refs[pos[0]:pos[0] + k]
            pos[0] += k
            return r

        ins = take(n_in)
        b_ins = [take(len(b.arrs)) for b in bgs]
        outs = take(n_out)
        b_outs = [take(len(b.out_shape)) for b in bgs]
        sc = take(n_sc)
        b_sc = [take(len(b.scratch)) for b in bgs]
        if bgs:
            step = pl.program_id(0)
            for d in range(1, len(grid)):
                step = step * grid[d] + pl.program_id(d)

            @pl.when(step == 0)
            def _():
                for b, i_, o_, s_ in zip(bgs, b_ins, b_outs, b_sc):
                    b.start(i_, o_, s_)

        body(*ins, *outs, *sc)
        if bgs:
            for b, i_, o_, s_ in zip(bgs, b_ins, b_outs, b_sc):
                if hasattr(b, "step"):
                    b.step(i_, o_, s_)
                for at, fn in b.mid_steps(nsteps):
                    @pl.when(step == at)
                    def _():
                        fn(i_, o_, s_)

            @pl.when(step == nsteps - 1)
            def _():
                for b, i_, o_, s_ in zip(bgs, b_ins, b_outs, b_sc):
                    b.finish(i_, o_, s_)

    def run(*args):
        res = pl.pallas_call(
            full, name=name, grid=grid,
            in_specs=list(in_specs) + [s for l in b_in_specs for s in l],
            out_specs=out_specs_l + [s for l in b_out_specs for s in l],
            out_shape=out_shape_l + [s for b in bgs for s in b.out_shape],
            scratch_shapes=list(scratch_shapes) + [s for b in bgs for s in b.scratch],
            input_output_aliases=aliases,
            compiler_params=compiler_params,
        )(*args, *[a for b in bgs for a in b.arrs])
        rest = list(res[n_out:])
        for b in bgs:
            b.result, rest = rest[:len(b.out_shape)], rest[len(b.out_shape):]
        return res[0] if single else list(res[:n_out])

    return run


def s5_discretize(a_re, a_im, log_dt, b_re, b_im, c_re, c_im):
    lam_r = jnp.minimum(a_re, DT_MIN_LAMBDA)
    lam_i = a_im
    dt = jnp.exp(log_dt)[:, None]
    e = jnp.exp(lam_r * dt)
    lbr = e * jnp.cos(lam_i * dt)
    lbi = e * jnp.sin(lam_i * dt)
    den = lam_r * lam_r + lam_i * lam_i
    cf_r = ((lbr - 1.0) * lam_r + lbi * lam_i) / den
    cf_i = (lbi * lam_r - (lbr - 1.0) * lam_i) / den
    bb_r = cf_r[:, :, None] * b_re - cf_i[:, :, None] * b_im
    bb_i = cf_r[:, :, None] * b_im + cf_i[:, :, None] * b_re
    eye = jnp.eye(8, dtype=f32)

    def blk_b(m):
        return jnp.einsum('bgpc,gh->bgchp', m.reshape(8, 8, S5_P, S5_C), eye).reshape(8, 128, 512)

    def blk_c(m):
        return jnp.einsum('bgcp,gh->bgphc', m.reshape(8, 8, S5_C, S5_P), eye).reshape(8, 512, 128)

    bm = jnp.concatenate([blk_b(bb_r), blk_b(bb_i)], axis=-1)
    cm = jnp.concatenate([blk_c(c_re), -blk_c(c_im)], axis=1)
    lam = jnp.stack([lbr.reshape(8, 512), lbi.reshape(8, 512)], axis=1)
    lam = jnp.broadcast_to(lam[:, :, None, :], (8, 2, 8, 512))
    return lam, bm, cm


def _cmul(ar, ai, br, bi):
    return ar * br - ai * bi, ar * bi + ai * br


def _shift_rows(v, k, up):
    row = lax.broadcasted_iota(jnp.int32, v.shape, 0)
    if up:
        return jnp.where(row < 8 - k, pltpu.roll(v, 8 - k, 0), 0.0)
    return jnp.where(row >= k, pltpu.roll(v, k, 0), 0.0)


def _chunk_scan(S, lr, li, reverse, aux=None):
    z = jnp.zeros((8, 512), f32)
    U = 4

    def idx(i):
        return (S5_STEPS - 1 - i) if reverse else i

    def rows_of(s):
        return pl.ds(s * 8, 8) if isinstance(s, int) else pl.ds(pl.multiple_of(s * 8, 8), 8)

    def rec(xr, xi, row):
        br = S[row, 0:512]
        bi = S[row, 512:1024]
        return lr * xr - li * xi + br, lr * xi + li * xr + bi

    def step1(i, c):
        for u in range(U):
            c = rec(c[0], c[1], rows_of(idx(i * U + u)))
        return c

    er, ei = lax.fori_loop(0, S5_STEPS // U, step1, (z, z))
    ar, ai = lr, li
    for _ in range(8):
        ar, ai = _cmul(ar, ai, ar, ai)
    cr, ci = _shift_rows(er, 1, reverse), _shift_rows(ei, 1, reverse)
    for k in (1, 2, 4):
        sr, si = _shift_rows(cr, k, reverse), _shift_rows(ci, k, reverse)
        pr, pi_ = _cmul(ar, ai, sr, si)
        cr, ci = cr + pr, ci + pi_
        ar, ai = _cmul(ar, ai, ar, ai)

    if aux is None:
        def step2(i, c):
            for u in range(U):
                row = rows_of(idx(i * U + u))
                c = rec(c[0], c[1], row)
                S[row, 0:512] = c[0]
                S[row, 512:1024] = c[1]
            return c

        lax.fori_loop(0, S5_STEPS // U, step2, (cr, ci))
        return None

    def one(s, c):
        gr0, gi0, dr, di = c
        row = rows_of(s)
        gr, gi = rec(gr0, gi0, row)
        S[row, 0:512] = gr
        S[row, 512:1024] = gi
        prow = rows_of(s - 1)
        xr = aux[prow, 0:512]
        xi = aux[prow, 512:1024]
        return gr, gi, dr + gr * xr + gi * xi, di + gi * xr - gr * xi

    def step2(i, c):
        for u in range(U):
            c = one(S5_STEPS - 1 - (i * U + u), c)
        return c

    c = lax.fori_loop(0, S5_STEPS // U - 1, step2, (cr, ci, z, z))
    for s in range(U - 1, 0, -1):
        c = one(s, c)
    gr, gi, dr, di = c
    row0 = pl.ds(0, 8)
    gr, gi = rec(gr, gi, row0)
    S[row0, 0:512] = gr
    S[row0, 512:1024] = gi
    last = pl.ds((S5_STEPS - 1) * 8, 8)
    xr = _shift_rows(aux[last, 0:512], 1, False)
    xi = _shift_rows(aux[last, 512:1024], 1, False)
    dr = dr + gr * xr + gi * xi
    di = di + gi * xr - gr * xi
    return dr, di


_ROWS = 256


def _row_loop(fn):
    def body(r, c):
        fn(pl.ds(pl.multiple_of(r * _ROWS, _ROWS), _ROWS))
        return c
    lax.fori_loop(0, T // _ROWS, body, 0)


def s5_core_fwd(hn, bm, lam, cm, bg=()):
    nt = T // _ROWS

    def body(u_ref, b_ref, lam_ref, c_ref, ys_ref, S):
        lr, li = lam_ref[0], lam_ref[1]
        z = jnp.zeros((8, 512), f32)
        tile = lambda k: pl.ds(k * _ROWS, _ROWS)
        c = (z, z)
        for k in range(nt):
            S[tile(k), :] = _dot(u_ref[tile(k), :], b_ref[...])
            if k >= 1:
                c = _scan_tile(S, lr, li, k - 1, c, False, False)
        c = _scan_tile(S, lr, li, nt - 1, c, False, False)
        c = _chunk_starts(c[0], c[1], lr, li, False)
        for k in range(nt):
            c = _scan_tile(S, lr, li, k, c, False, True)
            if k >= 1:
                ys_ref[tile(k - 1), :] = _dot(S[tile(k - 1), :].astype(bf16), c_ref[...])
        ys_ref[tile(nt - 1), :] = _dot(S[tile(nt - 1), :].astype(bf16), c_ref[...])

    return _call(
        bg, body, name="s5_core_fwd", grid=(S5_SUB,),
        in_specs=[pl.BlockSpec((T, 128), lambda b: (0, b)),
                  pl.BlockSpec((None, 128, 1024), lambda b: (b, 0, 0)),
                  pl.BlockSpec((None, 4, 8, 512), lambda b: (b, 0, 0, 0)),
                  pl.BlockSpec((None, 1024, 128), lambda b: (b, 0, 0))],
        out_specs=pl.BlockSpec((T, 128), lambda b: (0, b)),
        out_shape=SDS((T, D), f32),
        scratch_shapes=[pltpu.VMEM((T, 1024), f32)],
        compiler_params=_cp(dimension_semantics=("arbitrary",)),
    )(hn, bm, lam, cm)


_SEG = _ROWS // S5_CH


def _scan_tile(S, lr, li, k, carry, reverse, store, aux=None):
    steps = range(k * _SEG, (k + 1) * _SEG)
    for s in (reversed(steps) if reverse else steps):
        row = pl.ds(s * 8, 8)
        xr, xi = carry[0], carry[1]
        nr = lr * xr - li * xi + S[row, 0:512]
        ni = lr * xi + li * xr + S[row, 512:1024]
        if store:
            S[row, 0:512] = nr
            S[row, 512:1024] = ni
        if aux is not None and s >= 1:
            prow = pl.ds((s - 1) * 8, 8)
            pr, pi_ = aux[prow, 0:512], aux[prow, 512:1024]
            carry = (nr, ni, carry[2] + nr * pr + ni * pi_, carry[3] + ni * pr - nr * pi_)
        elif aux is not None:
            carry = (nr, ni, carry[2], carry[3])
        else:
            carry = (nr, ni)
    return carry


def _chunk_starts(er, ei, lr, li, reverse):
    ar, ai = lr, li
    for _ in range(8):
        ar, ai = _cmul(ar, ai, ar, ai)
    cr, ci = _shift_rows(er, 1, reverse), _shift_rows(ei, 1, reverse)
    for k in (1, 2, 4):
        sr, si = _shift_rows(cr, k, reverse), _shift_rows(ci, k, reverse)
        pr, pi_ = _cmul(ar, ai, sr, si)
        cr, ci = cr + pr, ci + pi_
        ar, ai = _cmul(ar, ai, ar, ai)
    return cr, ci


def s5_core_bwd(hn, dy, bm, lam, cm, bg=()):
    nt = T // _ROWS

    def body(u_ref, dy_ref, b_ref, lam_ref, c_ref, du_ref, db_ref, dct_ref, dlam_ref, S1, S2):
        lr, li, lcr, lci = lam_ref[0], lam_ref[1], lam_ref[2], lam_ref[3]
        z = jnp.zeros((8, 512), f32)
        tile = lambda k: pl.ds(k * _ROWS, _ROWS)
        dyb = lambda k: dy_ref[tile(k), :].astype(bf16)

        c = (z, z)
        for k in range(nt):
            S1[tile(k), :] = _dot(u_ref[tile(k), :], b_ref[...])
            if k >= 1:
                c = _scan_tile(S1, lr, li, k - 1, c, False, False)
        c = _scan_tile(S1, lr, li, nt - 1, c, False, False)

        c = _chunk_starts(c[0], c[1], lr, li, False)
        dct_ref[...] = jnp.zeros_like(dct_ref)
        for k in range(nt):
            c = _scan_tile(S1, lr, li, k, c, False, True)
            if k >= 1:
                dct_ref[...] += _dot_tn(dyb(k - 1), S1[tile(k - 1), :].astype(bf16))
        dct_ref[...] += _dot_tn(dyb(nt - 1), S1[tile(nt - 1), :].astype(bf16))

        S2[tile(nt - 1), :] = _dot_nt(dyb(nt - 1), c_ref[...])
        c = (z, z)
        for k in range(nt - 1, -1, -1):
            if k >= 1:
                S2[tile(k - 1), :] = _dot_nt(dyb(k - 1), c_ref[...])
            c = _scan_tile(S2, lcr, lci, k, c, True, False)

        def dbu(k):
            gb = S2[tile(k), :].astype(bf16)
            db_ref[...] += _dot_tn(u_ref[tile(k), :], gb)
            du_ref[tile(k), :] = _dot_nt(gb, b_ref[...])

        c = _chunk_starts(c[0], c[1], lcr, lci, True) + (z, z)
        db_ref[...] = jnp.zeros_like(db_ref)
        for k in range(nt - 1, -1, -1):
            c = _scan_tile(S2, lcr, lci, k, c, True, True, aux=S1)
            if k + 1 < nt:
                dbu(k + 1)
        dbu(0)
        gr, gi, dr, di = c
        last = pl.ds((S5_STEPS - 1) * 8, 8)
        xr = _shift_rows(S1[last, 0:512], 1, False)
        xi = _shift_rows(S1[last, 512:1024], 1, False)
        dlam_ref[0] = dr + gr * xr + gi * xi
        dlam_ref[1] = di + gi * xr - gr * xi

    return _call(
        bg, body, name="s5_core_bwd", grid=(S5_SUB,),
        in_specs=[pl.BlockSpec((T, 128), lambda b: (0, b)),
                  pl.BlockSpec((T, 128), lambda b: (0, b)),
                  pl.BlockSpec((None, 128, 1024), lambda b: (b, 0, 0)),
                  pl.BlockSpec((None, 4, 8, 512), lambda b: (b, 0, 0, 0)),
                  pl.BlockSpec((None, 1024, 128), lambda b: (b, 0, 0))],
        out_specs=[pl.BlockSpec((T, 128), lambda b: (0, b)),
                   pl.BlockSpec((None, 128, 1024), lambda b: (b, 0, 0)),
                   pl.BlockSpec((None, 128, 1024), lambda b: (b, 0, 0)),
                   pl.BlockSpec((None, 2, 8, 512), lambda b: (b, 0, 0, 0))],
        out_shape=[SDS((T, D), f32), SDS((8, 128, 1024), f32), SDS((8, 128, 1024), f32), SDS((8, 2, 8, 512), f32)],
        scratch_shapes=[pltpu.VMEM((T, 1024), f32), pltpu.VMEM((T, 1024), f32)],
        compiler_params=_cp(dimension_semantics=("arbitrary",)),
    )(hn, dy, bm, lam, cm)


TM = 512
NT = T // TM


def _tile(n=D):
    return pl.BlockSpec((TM, n), lambda i: (i, 0))


def s5_pre(xp, g):
    def body(x_ref, g_ref, hn_ref):
        hn, _ = _rms(x_ref[...], g_ref[...])
        hn_ref[...] = hn.astype(bf16)

    return pl.pallas_call(
        body, name="s5_pre", grid=(NT,), in_specs=[_tile(), _full((1, D))], out_specs=_tile(),
        out_shape=SDS((T, D), bf16), compiler_params=_cp(dimension_semantics=("arbitrary",)),
    )(xp, g)


def _gelu_grad(y):
    c = math.sqrt(2.0 / math.pi)
    t = jnp.tanh(c * (y + 0.044715 * y * y * y))
    return 0.5 * (1.0 + t) + 0.5 * y * (1.0 - t * t) * c * (1.0 + 3.0 * 0.044715 * y * y)


def s5_post(ys, xp, g, d, wglu, bglu, bg=()):
    def body(ys_ref, x_ref, g_ref, d_ref, w_ref, b_ref, y_ref, z_ref, h_ref):
        x = x_ref[...]
        hn, _ = _rms(x, g_ref[...])
        y = ys_ref[...] + d_ref[...] * hn
        y_ref[...] = y
        yg = jax.nn.gelu(y).astype(bf16)
        for j in range(4):
            cv = slice(j * 256, (j + 1) * 256)
            cg = slice(1024 + j * 256, 1024 + (j + 1) * 256)
            val = _dot(yg, w_ref[j]) + b_ref[:, cv]
            gate = _dot(yg, w_ref[j + 4]) + b_ref[:, cg]
            z_ref[:, cv] = val
            z_ref[:, cg] = gate
            h_ref[:, cv] = x[:, cv] + val * jax.nn.sigmoid(gate)

    return _call(
        bg, body, name="s5_post", grid=(NT,),
        in_specs=[_tile(), _tile(), _full((1, D)), _full((1, D)), _full((8, D, 256)), _full((1, 2 * D))],
        out_specs=[_tile(), _tile(2 * D), _tile()],
        out_shape=[SDS((T, D), f32), SDS((T, 2 * D), f32), SDS((T, D), f32)],
        compiler_params=_cp(dimension_semantics=("arbitrary",)),
    )(ys, xp, g, d, wglu, bglu)


def s5_post_bwd(dh, y, z, wglu, bg=()):
    def body(dh_ref, y_ref, z_ref, w_ref, dy_ref, dw_ref, db_ref, acc):
        i = pl.program_id(0)

        @pl.when(i == 0)
        def _():
            acc[...] = jnp.zeros_like(acc)
            db_ref[...] = jnp.zeros_like(db_ref)

        dh_ = dh_ref[...]
        y = y_ref[...]
        yg = jax.nn.gelu(y).astype(bf16)
        dyg = jnp.zeros((TM, D), f32)
        for j in range(4):
            cv = slice(j * 256, (j + 1) * 256)
            cg = slice(1024 + j * 256, 1024 + (j + 1) * 256)
            val = z_ref[:, cv]
            sg = jax.nn.sigmoid(z_ref[:, cg])
            dval = dh_[:, cv] * sg
            dgate = dh_[:, cv] * val * sg * (1.0 - sg)
            db_ref[:, cv] += _colsum8(dval)
            db_ref[:, cg] += _colsum8(dgate)
            dvb = dval.astype(bf16)
            dgb = dgate.astype(bf16)
            acc[j] += _dot_tn(yg, dvb)
            acc[j + 4] += _dot_tn(yg, dgb)
            dyg = dyg + _dot_nt(dvb, w_ref[j]) + _dot_nt(dgb, w_ref[j + 4])
        dy_ref[...] = dyg * _gelu_grad(y)

        @pl.when(i == NT - 1)
        def _():
            dw_ref[...] = acc[...].astype(bf16)

    return _call(
        bg, body, name="s5_post_bwd", grid=(NT,),
        in_specs=[_tile(), _tile(), _tile(2 * D), _full((8, D, 256))],
        out_specs=[_tile(), _full((8, D, 256)), _full((8, 2 * D))],
        out_shape=[SDS((T, D), f32), SDS((8, D, 256), bf16), SDS((8, 2 * D), f32)],
        scratch_shapes=[pltpu.VMEM((8, D, 256), f32)],
        compiler_params=_cp(dimension_semantics=("arbitrary",)),
    )(dh, y, z, wglu)


def s5_pre_bwd(xp, g, du, dy, d, dh, bg=()):
    def body(x_ref, g_ref, du_ref, dy_ref, d_ref, dh_ref, dx_ref, dg_ref, dd_ref):
        i = pl.program_id(0)

        @pl.when(i == 0)
        def _():
            dg_ref[...] = jnp.zeros_like(dg_ref)
            dd_ref[...] = jnp.zeros_like(dd_ref)

        x = x_ref[...]
        g = g_ref[...]
        dy = dy_ref[...]
        hn, _ = _rms(x, g)
        dhn = du_ref[...] + d_ref[...] * dy
        dx, dgt = _rms_bwd(x, g, dhn)
        dx_ref[...] = dh_ref[...] + dx
        dg_ref[...] += _colsum8(dgt)
        dd_ref[...] += _colsum8(dy * hn)

    return _call(
        bg, body, name="s5_pre_bwd", grid=(NT,),
        in_specs=[_tile(), _full((1, D)), _tile(), _tile(), _full((1, D)), _tile()],
        out_specs=[_tile(), _full((8, D)), _full((8, D))],
        out_shape=[SDS((T, D), f32), SDS((8, D), f32), SDS((8, D), f32)],
        compiler_params=_cp(dimension_semantics=("arbitrary",)),
    )(xp, g, du, dy, d, dh)


TMF = 1024


def mlp_fwd(h, g, w_in, w_out, layer, bg=()):
    def body(h_ref, g_ref, wi_ref, wo_ref, hm_ref, r_ref, out_ref, acc):
        j = pl.program_id(1)

        @pl.when(j == 0)
        def _():
            hm, _ = _rms(h_ref[...], g_ref[...])
            hm_ref[...] = hm.astype(bf16)
            acc[...] = jnp.zeros_like(acc)

        a = jnp.maximum(_dot(hm_ref[...], wi_ref[...]), 0.0)
        r_ref[...] = a.astype(bf16)
        acc[...] += _dot((a * a).astype(bf16), wo_ref[...])

        @pl.when(j == NDEV - 1)
        def _():
            out_ref[...] = h_ref[...] + acc[...]

    return _call(
        bg, body, name=f"mlp_fwd{layer}", grid=(T // TMF, NDEV),
        in_specs=[pl.BlockSpec((TMF, D), lambda i, j: (i, 0)),
                  pl.BlockSpec((1, D), lambda i, j: (0, 0)),
                  pl.BlockSpec((None, D, D_FF_SHARD), lambda i, j: (j, 0, 0)),
                  pl.BlockSpec((None, D_FF_SHARD, D), lambda i, j: (j, 0, 0))],
        out_specs=[pl.BlockSpec((TMF, D), lambda i, j: (i, 0)), pl.BlockSpec((TMF, D_FF_SHARD), lambda i, j: (i, j)),
                   pl.BlockSpec((TMF, D), lambda i, j: (i, 0))],
        out_shape=[SDS((T, D), bf16), SDS((T, NDEV * D_FF_SHARD), bf16), SDS((T, D), f32)],
        scratch_shapes=[pltpu.VMEM((TMF, D), f32)],
        compiler_params=_cp(dimension_semantics=("arbitrary", "arbitrary")),
    )(h, g, w_in, w_out)


def mlp_bwd(h, hm, r, g, dout, dout_b, w_in, w_out, layer, bg=()):
    last = NDEV - 1

    def body(h_ref, hm_ref, r_ref, g_ref, do_ref, dob_ref, wi_ref, wo_ref, dh_ref, dwi_ref, dwo_ref, dg_ref,
             dhm, awi, awo):
        j = pl.program_id(0)
        i = pl.program_id(1)
        rows = pl.ds(pl.multiple_of(i * TM, TM), TM)

        @pl.when(i == 0)
        def _():
            awi[...] = jnp.zeros_like(awi)
            awo[...] = jnp.zeros_like(awo)

        dz = (_dot_nt(dob_ref[...], wo_ref[...]) * (2.0 * r_ref[...].astype(f32))).astype(bf16)
        rb = r_ref[...]
        awo[...] += _dot_tn(rb * rb, dob_ref[...])
        awi[...] += _dot_tn(hm_ref[...], dz)
        part = _dot_nt(dz, wi_ref[...])

        @pl.when(j == 0)
        def _():
            dhm[rows, :] = part

        @pl.when(j > 0)
        def _():
            dhm[rows, :] += part

        @pl.when(i == NT - 1)
        def _():
            dwi_ref[...] = awi[...].astype(bf16)
            dwo_ref[...] = awo[...].astype(bf16)

        @pl.when(j == last)
        def _():
            @pl.when(i == 0)
            def _():
                dg_ref[...] = jnp.zeros_like(dg_ref)
            dx, dgt = _rms_bwd(h_ref[...], g_ref[...], dhm[rows, :])
            dh_ref[...] = do_ref[...] + dx
            dg_ref[...] += _colsum8(dgt)

    late = lambda j, i: (jnp.where(j == last, i, 0), 0)
    return _call(
        bg, body, name=f"mlp_bwd{layer}", grid=(NDEV, NT),
        in_specs=[pl.BlockSpec((TM, D), late),
                  pl.BlockSpec((TM, D), lambda j, i: (i, 0)),
                  pl.BlockSpec((TM, D_FF_SHARD), lambda j, i: (i, j)),
                  pl.BlockSpec((1, D), lambda j, i: (0, 0)),
                  pl.BlockSpec((TM, D), late),
                  pl.BlockSpec((TM, D), lambda j, i: (i, 0)),
                  pl.BlockSpec((None, D, D_FF_SHARD), lambda j, i: (j, 0, 0)),
                  pl.BlockSpec((None, D_FF_SHARD, D), lambda j, i: (j, 0, 0))],
        out_specs=[pl.BlockSpec((TM, D), late),
                   pl.BlockSpec((None, D, D_FF_SHARD), lambda j, i: (j, 0, 0)),
                   pl.BlockSpec((None, D_FF_SHARD, D), lambda j, i: (j, 0, 0)),
                   pl.BlockSpec((8, D), lambda j, i: (0, 0))],
        out_shape=[SDS((T, D), f32), SDS((NDEV, D, D_FF_SHARD), bf16), SDS((NDEV, D_FF_SHARD, D), bf16),
                   SDS((8, D), f32)],
        scratch_shapes=[pltpu.VMEM((T, D), f32), pltpu.VMEM((D, D_FF_SHARD), f32), pltpu.VMEM((D_FF_SHARD, D), f32)],
        compiler_params=_cp(dimension_semantics=("arbitrary", "arbitrary")),
    )(h, hm, r, g, dout, dout_b, w_in, w_out)


def _spread4():
    r = lax.broadcasted_iota(jnp.int32, (256, D), 0)
    c = lax.broadcasted_iota(jnp.int32, (256, D), 1)
    return ((c // 256 == r // HEAD_DIM) & (c % HEAD_DIM == r % HEAD_DIM)).astype(bf16)


def attn_pre(h, g_kv, g_mix, wkv, bkv, spread, wq, bq):
    def body(h_ref, gkv_ref, gm_ref, wkv_ref, bkv_ref, sp_ref, wq_ref, bq_ref, kvn_ref, hn_ref, k_ref, v_ref, q_ref):
        h_ = h_ref[...]
        kvn = _rms(h_, gkv_ref[...])[0].astype(bf16)
        hn = _rms(h_, gm_ref[...])[0].astype(bf16)
        kvn_ref[...] = kvn
        hn_ref[...] = hn
        kv = (_dot(kvn, wkv_ref[...]) + bkv_ref[...]).astype(bf16)
        k_ref[...] = _dot(kv[:, :256], sp_ref[...]).astype(bf16)
        v_ref[...] = _dot(kv[:, 256:], sp_ref[...]).astype(bf16)
        q_ref[...] = (_dot(hn, wq_ref[...]) + bq_ref[...]).astype(bf16)

    return pl.pallas_call(
        body, name="attn_pre", grid=(NT,),
        in_specs=[_tile(), _full((1, D)), _full((1, D)), _full((D, 512)), _full((1, 512)), _full((256, D)),
                  _full((D, D)), _full((1, D))],
        out_specs=[_tile()] * 5,
        out_shape=[SDS((T, D), bf16)] * 5,
        compiler_params=_cp(dimension_semantics=("arbitrary",)),
    )(h, g_kv, g_mix, wkv, bkv, spread, wq, bq)


def _attn_specs():
    cur = pl.BlockSpec((TM, 256), lambda j, n: (n, j))
    prev = pl.BlockSpec((BLK, 256), lambda j, n: (jnp.maximum(n * (TM // BLK) - 1, 0), j))
    return cur, prev


def _head_mask(g):
    lane = lax.broadcasted_iota(jnp.int32, (1, 256), 1)
    return (lane >= g * HEAD_DIM) & (lane < (g + 1) * HEAD_DIM)


def _stack_heads(t):
    return jnp.concatenate([jnp.where(_head_mask(g), t, 0) for g in range(Q_PER_KV)], axis=0)


def _unstack_heads(t):
    out = jnp.where(_head_mask(0), t[0:BLK], 0.0)
    for g in range(1, Q_PER_KV):
        out = out + jnp.where(_head_mask(g), t[g * BLK:(g + 1) * BLK], 0.0)
    return out


def _attn_probs(qs, k2, sinks, first):
    rows = Q_PER_KV * BLK
    s = _dot_nt(qs, k2) * (1.0 / math.sqrt(HEAD_DIM))
    qi = jnp.bitwise_and(lax.broadcasted_iota(jnp.int32, (rows, 2 * BLK), 0), BLK - 1)
    kj = lax.broadcasted_iota(jnp.int32, (rows, 2 * BLK), 1)
    diff = qi + BLK - kj
    valid = (diff >= 0) & (diff < BLK) & (jnp.logical_not(first) | (kj >= BLK))
    s = jnp.where(valid, s, -jnp.inf)
    rb = lax.broadcasted_iota(jnp.int32, (rows, 1), 0)
    sink = jnp.where(rb < BLK, sinks[0], jnp.where(rb < 2 * BLK, sinks[1], jnp.where(rb < 3 * BLK, sinks[2], sinks[3])))
    m = jnp.maximum(jnp.max(s, axis=-1, keepdims=True), sink)
    p = jnp.exp(s - m)
    ps = jnp.exp(sink - m)
    denom = jnp.sum(p, axis=-1, keepdims=True) + ps
    return p / denom, ps / denom


def _window_blocks(b, n, kc_ref, kp_ref, vc_ref, vp_ref):
    if b == 0:
        return (jnp.concatenate([kp_ref[...], kc_ref[0:BLK, :]], axis=0),
                jnp.concatenate([vp_ref[...], vc_ref[0:BLK, :]], axis=0), n == 0)
    rows = pl.ds((b - 1) * BLK, 2 * BLK)
    return kc_ref[rows, :], vc_ref[rows, :], False


def attn_core_fwd(q, k4, v4, sinks, bg=()):
    nb = TM // BLK

    def body(sink_ref, q_ref, kc_ref, kp_ref, vc_ref, vp_ref, o_ref):
        j = pl.program_id(0)
        n = pl.program_id(1)
        sk = [sink_ref[j * Q_PER_KV + g] for g in range(Q_PER_KV)]
        for b in range(nb):
            qb = q_ref[b * BLK:(b + 1) * BLK, :]
            k2, v2, first = _window_blocks(b, n, kc_ref, kp_ref, vc_ref, vp_ref)
            a, _ = _attn_probs(_stack_heads(qb), k2, sk, first)
            o_ref[b * BLK:(b + 1) * BLK, :] = _unstack_heads(_dot(a.astype(bf16), v2)).astype(bf16)

    cur, prev = _attn_specs()
    return _call(
        bg, body, name="attn_core_fwd", grid=(N_KV, NT),
        in_specs=[pl.BlockSpec(memory_space=pltpu.SMEM), cur, cur, prev, cur, prev],
        out_specs=cur, out_shape=SDS((T, D), bf16),
        compiler_params=_cp(dimension_semantics=("arbitrary", "arbitrary")),
    )(sinks, q, k4, k4, v4, v4)


def attn_post(h, o, wo, bo):
    def body(h_ref, o_ref, w_ref, b_ref, out_ref):
        out_ref[...] = h_ref[...] + _dot(o_ref[...], w_ref[...]) + b_ref[...]

    return pl.pallas_call(
        body, name="attn_post", grid=(NT,), in_specs=[_tile(), _tile(), _full((D, D)), _full((1, D))],
        out_specs=_tile(), out_shape=SDS((T, D), f32), compiler_params=_cp(dimension_semantics=("arbitrary",)),
    )(h, o, wo, bo)


def attn_bwd_pre(dh, o, wo, bg=()):
    def body(dh_ref, o_ref, w_ref, do_ref, dw_ref, db_ref, acc):
        i = pl.program_id(0)

        @pl.when(i == 0)
        def _():
            acc[...] = jnp.zeros_like(acc)
            db_ref[...] = jnp.zeros_like(db_ref)

        dh_ = dh_ref[...]
        dhb = dh_.astype(bf16)
        do_ref[...] = _dot_nt(dhb, w_ref[...]).astype(bf16)
        acc[...] += _dot_tn(o_ref[...], dhb)
        db_ref[...] += _colsum8(dh_)

        @pl.when(i == NT - 1)
        def _():
            dw_ref[...] = acc[...].astype(bf16)

    return _call(
        bg, body, name="attn_bwd_pre", grid=(NT,), in_specs=[_tile(), _tile(), _full((D, D))],
        out_specs=[_tile(), _full((D, D)), _full((8, D))],
        out_shape=[SDS((T, D), bf16), SDS((D, D), bf16), SDS((8, D), f32)],
        scratch_shapes=[pltpu.VMEM((D, D), f32)],
        compiler_params=_cp(dimension_semantics=("arbitrary",)),
    )(dh, o, wo)


def attn_core_bwd(q, do, k4, v4, sinks, bg=()):
    nb = TM // BLK

    def body(sink_ref, q_ref, do_ref, kc_ref, kp_ref, vc_ref, vp_ref, dq_ref, dk_ref, dv_ref, ds_ref):
        j = pl.program_id(0)
        n = pl.program_id(1)

        @pl.when(n == 0)
        def _():
            dk_ref[...] = jnp.zeros_like(dk_ref)
            dv_ref[...] = jnp.zeros_like(dv_ref)
            ds_ref[...] = jnp.zeros_like(ds_ref)

        lane8 = lax.broadcasted_iota(jnp.int32, (8, 128), 1)
        row8 = lax.broadcasted_iota(jnp.int32, (8, 128), 0)
        sk = [sink_ref[j * Q_PER_KV + g] for g in range(Q_PER_KV)]
        for b in range(nb):
            qs = _stack_heads(q_ref[b * BLK:(b + 1) * BLK, :])
            dos = _stack_heads(do_ref[b * BLK:(b + 1) * BLK, :])
            k2, v2, first = _window_blocks(b, n, kc_ref, kp_ref, vc_ref, vp_ref)
            a, asink = _attn_probs(qs, k2, sk, first)
            dp = _dot_nt(dos, v2)
            dd = jnp.sum(a * dp, axis=-1, keepdims=True)
            dsc = (a * (dp - dd) * (1.0 / math.sqrt(HEAD_DIM))).astype(bf16)
            t = asink * dd
            for g in range(Q_PER_KV):
                dsink = -jnp.sum(t[g * BLK:(g + 1) * BLK], axis=0, keepdims=True)
                ds_ref[...] += jnp.where((lane8 == g) & (row8 == 0), jnp.broadcast_to(dsink, (8, 128)), 0.0)
            dq_ref[b * BLK:(b + 1) * BLK, :] = _unstack_heads(_dot(dsc, k2))
            dk2 = _dot_tn(dsc, qs)
            dv2 = _dot_tn(a.astype(bf16), dos)
            cur = pl.ds(pl.multiple_of(n * TM + b * BLK, BLK), BLK)
            dk_ref[cur, :] += dk2[BLK:, :]
            dv_ref[cur, :] += dv2[BLK:, :]
            if b == 0:
                @pl.when(n > 0)
                def _():
                    prv = pl.ds(pl.multiple_of(n * TM - BLK, BLK), BLK)
                    dk_ref[prv, :] += dk2[:BLK, :]
                    dv_ref[prv, :] += dv2[:BLK, :]
            else:
                prv = pl.ds(pl.multiple_of(n * TM + (b - 1) * BLK, BLK), BLK)
                dk_ref[prv, :] += dk2[:BLK, :]
                dv_ref[prv, :] += dv2[:BLK, :]

    cur, prev = _attn_specs()
    col = pl.BlockSpec((T, 256), lambda j, n: (0, j))
    return _call(
        bg, body, name="attn_core_bwd", grid=(N_KV, NT),
        in_specs=[pl.BlockSpec(memory_space=pltpu.SMEM), cur, cur, cur, prev, cur, prev],
        out_specs=[cur, col, col, pl.BlockSpec((None, 8, 128), lambda j, n: (j, 0, 0))],
        out_shape=[SDS((T, D), f32), SDS((T, D), f32), SDS((T, D), f32), SDS((N_KV, 8, 128), f32)],
        compiler_params=_cp(dimension_semantics=("arbitrary", "arbitrary")),
    )(sinks, q, do, k4, k4, v4, v4)


def attn_bwd_q(h, dh, dq, hn, g_mix, wq):
    def body(h_ref, dh_ref, dq_ref, hn_ref, gm_ref, wq_ref, out_ref, dwq_ref, dbq_ref, dgm_ref, aq):
        i = pl.program_id(0)

        @pl.when(i == 0)
        def _():
            aq[...] = jnp.zeros_like(aq)
            dbq_ref[...] = jnp.zeros_like(dbq_ref)
            dgm_ref[...] = jnp.zeros_like(dgm_ref)

        dq_ = dq_ref[...]
        dqb = dq_.astype(bf16)
        aq[...] += _dot_tn(hn_ref[...], dqb)
        dbq_ref[...] += _colsum8(dq_)
        dx, dg = _rms_bwd(h_ref[...], gm_ref[...], _dot_nt(dqb, wq_ref[...]))
        out_ref[...] = dh_ref[...] + dx
        dgm_ref[...] += _colsum8(dg)

        @pl.when(i == NT - 1)
        def _():
            dwq_ref[...] = aq[...].astype(bf16)

    vec = _full((8, D))
    mat = _full((D, D))
    return pl.pallas_call(
        body, name="attn_bwd_q", grid=(NT,),
        in_specs=[_tile()] * 4 + [_full((1, D)), mat],
        out_specs=[_tile(), mat, vec, vec],
        out_shape=[SDS((T, D), f32), SDS((D, D), bf16), SDS((8, D), f32), SDS((8, D), f32)],
        scratch_shapes=[pltpu.VMEM((D, D), f32)],
        compiler_params=_cp(dimension_semantics=("arbitrary",)),
    )(h, dh, dq, hn, g_mix, wq)


def attn_bwd_kv(h, dh, dk4, dv4, kvn, g_kv, wkv, spread):
    def body(h_ref, dh_ref, dk_ref, dv_ref, kvn_ref, gkv_ref, wkv_ref, sp_ref, out_ref, outb_ref, dw_ref, db_ref,
             dgkv_ref, acc):
        i = pl.program_id(0)

        @pl.when(i == 0)
        def _():
            for r in (acc, db_ref, dgkv_ref):
                r[...] = jnp.zeros_like(r)

        dkv = jnp.concatenate([_dot_nt(dk_ref[...].astype(bf16), sp_ref[...]),
                               _dot_nt(dv_ref[...].astype(bf16), sp_ref[...])], axis=1)
        dkvb = dkv.astype(bf16)
        acc[...] += _dot_tn(kvn_ref[...], dkvb)
        db_ref[...] += _colsum8(dkv)
        dx, dg = _rms_bwd(h_ref[...], gkv_ref[...], _dot_nt(dkvb, wkv_ref[...]))
        out = dh_ref[...] + dx
        out_ref[...] = out
        outb_ref[...] = out.astype(bf16)
        dgkv_ref[...] += _colsum8(dg)

        @pl.when(i == NT - 1)
        def _():
            dw_ref[...] = acc[...].astype(bf16)

    return pl.pallas_call(
        body, name="attn_bwd_kv", grid=(NT,),
        in_specs=[_tile()] * 5 + [_full((1, D)), _full((D, 512)), _full((256, D))],
        out_specs=[_tile(), _tile(), _full((D, 512)), _full((8, 512)), _full((8, D))],
        out_shape=[SDS((T, D), f32), SDS((T, D), bf16), SDS((D, 512), bf16), SDS((8, 512), f32), SDS((8, D), f32)],
        scratch_shapes=[pltpu.VMEM((D, 512), f32)],
        compiler_params=_cp(dimension_semantics=("arbitrary",)),
    )(h, dh, dk4, dv4, kvn, g_kv, wkv, spread)


def final_loss(h, g, target):
    def body(h_ref, g_ref, t_ref, loss_ref, dh_ref, dhb_ref, dg_ref):
        i = pl.program_id(0)

        @pl.when(i == 0)
        def _():
            loss_ref[...] = jnp.zeros_like(loss_ref)
            dg_ref[...] = jnp.zeros_like(dg_ref)

        h_ = h_ref[...]
        g_ = g_ref[...]
        y, _ = _rms(h_, g_)
        diff = y - t_ref[...]
        per_tok = jnp.mean(diff * diff, axis=-1, keepdims=True)
        tot = 0.5 * jnp.sum(per_tok, axis=0, keepdims=True)
        lane = lax.broadcasted_iota(jnp.int32, (8, 128), 1)
        row = lax.broadcasted_iota(jnp.int32, (8, 128), 0)
        loss_ref[...] += jnp.where((lane == 0) & (row == 0), jnp.broadcast_to(tot, (8, 128)), 0.0)
        dx, dgt = _rms_bwd(h_, g_, diff * (1.0 / D))
        dh_ref[...] = dx
        dhb_ref[...] = dx.astype(bf16)
        dg_ref[...] += _colsum8(dgt)

    return pl.pallas_call(
        body, name="final_loss", grid=(NT,), in_specs=[_tile(), _full((1, D)), _tile()],
        out_specs=[_full((8, 128)), _tile(), _tile(), _full((8, D))],
        out_shape=[SDS((8, 128), f32), SDS((T, D), f32), SDS((T, D), bf16), SDS((8, D), f32)],
        compiler_params=_cp(dimension_semantics=("arbitrary",)),
    )(h, g, target)


def _permute_rows(a, to_chunked, name, also_bf16=False):
    def body(in_ref, out_ref, *rest):
        def two_steps(i, c):
            s = i * 2
            if to_chunked:
                v = jnp.concatenate([in_ref[pl.ds(s, S5_CH, stride=S5_STEPS), :],
                                     in_ref[pl.ds(s + 1, S5_CH, stride=S5_STEPS), :]], axis=0)
                rows = pl.ds(pl.multiple_of(s * S5_CH, 2 * S5_CH), 2 * S5_CH)
                out_ref[rows, :] = v
                if also_bf16:
                    rest[0][rows, :] = v.astype(bf16)
            else:
                for u in range(2):
                    out_ref[pl.ds(s + u, S5_CH, stride=S5_STEPS), :] = in_ref[
                        pl.ds(pl.multiple_of((s + u) * S5_CH, S5_CH), S5_CH), :]
            return c
        lax.fori_loop(0, S5_STEPS // 2, two_steps, 0)

    strip = pl.BlockSpec((T, 128), lambda k: (0, k))
    outs = [SDS((T, D), f32)] + ([SDS((T, D), bf16)] if also_bf16 else [])
    res = pl.pallas_call(
        body, name=name, grid=(D // 128,), in_specs=[strip], out_specs=[strip] * len(outs), out_shape=outs,
        compiler_params=_cp(dimension_semantics=("arbitrary",)),
    )(a)
    return res if also_bf16 else res[0]


def _to_chunked(a, name, also_bf16=False):
    return _permute_rows(a, True, name, also_bf16)


def _from_chunked(a, name):
    return _permute_rows(a, False, name)


def _rep4(w):
    return jnp.broadcast_to(w.reshape(w.shape[0], N_KV, 1, HEAD_DIM), (w.shape[0], N_KV, Q_PER_KV, HEAD_DIM)).reshape(
        w.shape[0], N_KV * Q_PER_KV * HEAD_DIM)


def _fold4(w):
    return w.reshape(w.shape[0], N_KV, Q_PER_KV, HEAD_DIM).sum(axis=2).reshape(w.shape[0], N_KV * HEAD_DIM)


def fwd_bwd(x, target, p, shards, opt, core, chip):
    row = lambda v: v.reshape(1, -1)
    (lam, bm, cm), prep_vjp = jax.vjp(s5_discretize, p["s5_a_re"][0], p["s5_a_im"][0], p["s5_log_dt"][0],
                                      p["s5_b_re"][0], p["s5_b_im"][0], p["s5_c_re"][0], p["s5_c_im"][0])
    bmb, cmb = bm.astype(bf16), cm.astype(bf16)
    lam = jnp.concatenate([lam, lam * jnp.array([1.0, -1.0], f32).reshape(1, 2, 1, 1)], axis=1)
    g_mix0, g_mix1 = row(p["norm_mix"][0]), row(p["norm_mix"][1])
    g_mlp0, g_mlp1 = row(p["norm_mlp"][0]), row(p["norm_mlp"][1])
    g_kv, g_fin = row(p["norm_kv"]), row(p["norm_final"])
    bq, bo = p["b_q"], p["b_o"]
    bkv = row(p["b_kv"])
    spread = _spread4()
    sinks = p["sinks"].reshape(16)

    def reduce_pairs(names, bg):
        return [add_pairs(g, r, core, f"add_pairs_{n}") for n, g, r in zip(names, bg.arrs, bg.result)]

    wglu, gvec = sc_gather([shards["s5_w_glu"], shards["vecs"]], 3, "sc_gather_s5")
    win0, wout0 = sc_gather([shards["w_in0"], shards["w_out0"]], 14, "sc_gather_mlp0")
    wkv, wq, wo = sc_gather([shards["w_kv"], shards["w_q"], shards["w_o"]], 4, "sc_gather_attn")
    win1, wout1 = sc_gather([shards["w_in1"], shards["w_out1"]], 5, "sc_gather_mlp1")
    xp = _to_chunked(x, "rows_x")
    hn0 = s5_pre(xp, g_mix0)
    ys = s5_core_fwd(hn0, bmb, lam, cmb)
    d_skip = gvec[:, 0, :128].reshape(1, D)
    bglu = gvec[:, 0, 128:].reshape(1, 2 * D)
    y, z, h1 = s5_post(ys, xp, g_mix0, d_skip, wglu, bglu)
    hm0, r0, h2p = mlp_fwd(h1, g_mlp0, win0, wout0, 0)
    wkv, wq, wo = wkv.reshape(D, 512), wq.reshape(D, D), wo.reshape(D, D)
    h2 = _from_chunked(h2p, "rows_h2")
    kvn, hn1, k4, v4, q = attn_pre(h2, g_kv, g_mix1, wkv, bkv, spread, wq, bq)
    o = attn_core_fwd(q, k4, v4, sinks)
    h3 = attn_post(h2, o, wo, bo)
    hm1, r1, h4 = mlp_fwd(h3, g_mlp1, win1, wout1, 1)
    loss, dh4, dh4b, dg_fin = final_loss(h4, g_fin, target)

    def pair_sums(names, grads, cid, before):
        r1 = sc_comm(BgPair(grads), cid, "sc_pair_" + names[0])
        parts = [add_pairs(g, r, core, f"add_pairs_{n}") for n, g, r in zip(names, grads, r1)]
        before, parts = lax.optimization_barrier((before, parts))
        return before, parts

    def across_chips(names, parts, cid):
        return list(zip(parts, sc_comm(BgChips(parts), cid, "sc_chips_" + names[0])))

    dh3, dwin1, dwout1, dg_mlp1 = mlp_bwd(h3, hm1, r1, g_mlp1, dh4, dh4b, win1, wout1, 1)
    do, dwo, dbo = attn_bwd_pre(dh3, o, wo)
    do, parts = pair_sums(["w_in1", "w_out1"], [dwin1, dwout1], 6, do)
    rs_in1, rs_out1 = across_chips(["w_in1", "w_out1"], parts, 7)
    dq, dk4, dv4, dsink = attn_core_bwd(q, do, k4, v4, sinks)
    dh2, dwq, dbq, dg_mix1 = attn_bwd_q(h2, dh3, dq, hn1, g_mix1, wq)
    dh2, dh2b, dwkv, dbkv, dg_kv = attn_bwd_kv(h2, dh2, dk4, dv4, kvn, g_kv, wkv, spread)
    dh2p, dh2pb = _to_chunked(dh2, "rows_dh2", also_bf16=True)
    big = {}
    a_in1 = adam_big(*opt["w_mlp_in"], *rs_in1, chip, "adam_w_mlp_in1", layer=1)
    a_out1 = adam_big(*opt["w_mlp_out"], *rs_out1, chip, "adam_w_mlp_out1", layer=1)
    dh2p, a_in1, a_out1 = lax.optimization_barrier((dh2p, a_in1, a_out1))
    names = ["w_kv", "w_q", "w_o"]
    dh2p, parts = pair_sums(names, [dwkv.reshape(NDEV, 128, 512), dwq.reshape(NDEV, 128, D),
                                    dwo.reshape(NDEV, 128, D)], 8, dh2p)
    rs_attn = across_chips(names, parts, 9)
    dh1, dwin0, dwout0, dg_mlp0 = mlp_bwd(h1, hm0, r0, g_mlp0, dh2p, dh2pb, win0, wout0, 0)
    a_attn = [adam_big(*opt[n], *rs, chip, f"adam_{n}") for n, rs in zip(names, rs_attn)]
    dh1, a_attn = lax.optimization_barrier((dh1, a_attn))
    big.update(zip(names, a_attn))
    dy, dwglu, dbglu = s5_post_bwd(dh1, y, z, wglu)
    dy, parts = pair_sums(["w_in0", "w_out0"], [dwin0, dwout0], 10, dy)
    rs_in0, rs_out0 = across_chips(["w_in0", "w_out0"], parts, 11)
    du, dbm, dcmt, dlam = s5_core_bwd(hn0, dy, bmb, lam, cmb)
    du, parts = pair_sums(["s5_w_glu"], [dwglu], 12, du)
    rs_glu, = across_chips(["s5_w_glu"], parts, 13)
    dxp, dg_mix0, dd = s5_pre_bwd(xp, g_mix0, du, dy, d_skip, dh1)
    big["w_mlp_in"] = adam_big(*opt["w_mlp_in"], *rs_in0, chip, "adam_w_mlp_in0", layer=0, prev=a_in1)
    big["w_mlp_out"] = adam_big(*opt["w_mlp_out"], *rs_out0, chip, "adam_w_mlp_out0", layer=0, prev=a_out1)
    big["s5_w_glu"] = adam_big(*opt["s5_w_glu"], *rs_glu, chip, "adam_s5_w_glu")
    grad_x = _from_chunked(dxp, "rows_grad_x")
    da_re, da_im, dlog_dt, db_re, db_im, dc_re, dc_im = prep_vjp((dlam, dbm, dcmt.transpose(0, 2, 1)))

    def lanes(v_):
        v_ = v_.reshape(1, -1)
        return jnp.pad(v_, ((0, 0), (0, D - v_.shape[1])))

    small = jnp.concatenate([
        dg_mix0[0:1], dg_mix1[0:1], dg_mlp0[0:1], dg_mlp1[0:1], dg_kv[0:1], dg_fin[0:1], dd[0:1], dbq[0:1], dbo[0:1],
        dbglu[0:1].reshape(2, D), lanes(dbkv[0:1]),
        lanes(dsink[:, 0, :Q_PER_KV]), lanes(dlog_dt), lanes(loss[0:1, 0:1]), jnp.zeros((1, D), f32),
        da_re.reshape(4, D), da_im.reshape(4, D),
        db_re.transpose(0, 2, 1).reshape(64, D), db_im.transpose(0, 2, 1).reshape(64, D),
        dc_re.reshape(64, D), dc_im.reshape(64, D)], axis=0)
    small, big["w_mlp_in"], big["w_mlp_out"] = lax.optimization_barrier((small, big["w_mlp_in"], big["w_mlp_out"]))
    return loss, grad_x, small, big


_ANY = pl.BlockSpec(memory_space=pl.ANY)


def _pos():
    return lax.axis_index("x"), lax.axis_index("y"), lax.axis_index("c")


def _other_chips(x, y):
    return [(1 - x, y), (x, 1 - y), (1 - x, 1 - y)]


def all_gather(arrs):
    n = len(arrs)

    def body(*refs):
        ins, outs = refs[:n], refs[n:2 * n]
        send_sems, recv_sems, local_sems = refs[2 * n:]
        x, y, c = _pos()
        me, sib = (x, y, c), (x, y, 1 - c)
        chips = _other_chips(x, y)

        def copy(a, k, block, to, src=None):
            dst = outs[a].at[4 * block[0] + 2 * block[1] + block[2]]
            return pltpu.make_async_remote_copy(
                src_ref=dst if src is None else src, dst_ref=dst, send_sem=send_sems.at[a, k],
                recv_sem=recv_sems.at[a, k], device_id=to, device_id_type=MESH)

        mine = [pltpu.make_async_copy(ins[a], outs[a].at[4 * x + 2 * y + c], local_sems.at[a]) for a in range(n)]
        for cp in mine:
            cp.start()
        first = []
        for a in range(n):
            first.append(copy(a, 0, me, sib, src=ins[a]))
            first += [copy(a, 1 + j, me, (*chip, c), src=ins[a]) for j, chip in enumerate(chips)]
        for cp in first:
            cp.start()
        passed = []
        for j, chip in enumerate(chips):
            for a in range(n):
                copy(a, 1 + j, (*chip, c), me).wait_recv()
                cp = copy(a, 4 + j, (*chip, c), sib)
                cp.start()
                passed.append(cp)
        for a in range(n):
            copy(a, 0, sib, me).wait_recv()
            for j, chip in enumerate(chips):
                copy(a, 4 + j, (*chip, 1 - c), me).wait_recv()
        for cp in first + passed:
            cp.wait_send()
        for cp in mine:
            cp.wait()

    return pl.pallas_call(
        body, name="all_gather", in_specs=[_ANY] * n, out_specs=[_ANY] * n,
        out_shape=[SDS((NDEV,) + a.shape, a.dtype) for a in arrs],
        scratch_shapes=[pltpu.SemaphoreType.DMA((n, 7)), pltpu.SemaphoreType.DMA((n, 7)),
                        pltpu.SemaphoreType.DMA((n,))],
    )(*arrs)


def rs_pair(grads):
    n = len(grads)

    def body(*refs):
        ins, outs = refs[:n], refs[n:2 * n]
        send_sems, recv_sems = refs[2 * n:]
        x, y, c = _pos()
        cps = []
        for a in range(n):
            for k in range(4):
                cps.append(pltpu.make_async_remote_copy(
                    src_ref=ins[a].at[2 * k + 1 - c], dst_ref=outs[a].at[k], send_sem=send_sems.at[a, k],
                    recv_sem=recv_sems.at[a, k], device_id=(x, y, 1 - c), device_id_type=MESH))
        for cp in cps:
            cp.start()
        for cp in cps:
            cp.wait_recv()
        for cp in cps:
            cp.wait_send()

    return pl.pallas_call(
        body, name="rs_pair", in_specs=[_ANY] * n, out_specs=[_ANY] * n,
        out_shape=[SDS((4,) + g.shape[1:], g.dtype) for g in grads],
        scratch_shapes=[pltpu.SemaphoreType.DMA((n, 4)), pltpu.SemaphoreType.DMA((n, 4))],
    )(*grads)


def rs_chips(parts):
    n = len(parts)

    def body(*refs):
        ins, outs = refs[:n], refs[n:2 * n]
        send_sems, recv_sems = refs[2 * n:]
        x, y, c = _pos()
        cps = []
        for a in range(n):
            for r, (px, py) in enumerate(_other_chips(x, y)):
                cps.append(pltpu.make_async_remote_copy(
                    src_ref=ins[a].at[2 * px + py], dst_ref=outs[a].at[r], send_sem=send_sems.at[a, r],
                    recv_sem=recv_sems.at[a, r], device_id=(px, py, c), device_id_type=MESH))
        for cp in cps:
            cp.start()
        for cp in cps:
            cp.wait_recv()
        for cp in cps:
            cp.wait_send()

    return pl.pallas_call(
        body, name="rs_chips", in_specs=[_ANY] * n, out_specs=[_ANY] * n,
        out_shape=[SDS((3,) + g.shape[1:], g.dtype) for g in parts],
        scratch_shapes=[pltpu.SemaphoreType.DMA((n, 3)), pltpu.SemaphoreType.DMA((n, 3))],
    )(*parts)


def _row_tile(r, c):
    return min(r, max(8, (512 * 1024) // c))


def add_pairs(g, r1, core, name):
    _, R, C = g.shape
    tr = _row_tile(R, C)

    def body(core_ref, g_ref, r_ref, o_ref):
        o_ref[...] = (g_ref[...].astype(f32) + r_ref[...].astype(f32)).astype(bf16)

    return pl.pallas_call(
        body, name=name, out_shape=SDS((4, R, C), bf16),
        grid_spec=pltpu.PrefetchScalarGridSpec(
            num_scalar_prefetch=1, grid=(4, R // tr),
            in_specs=[pl.BlockSpec((None, tr, C), lambda k, i, core: (2 * k + core[0], i, 0)),
                      pl.BlockSpec((None, tr, C), lambda k, i, core: (k, i, 0))],
            out_specs=pl.BlockSpec((None, tr, C), lambda k, i, core: (k, i, 0))),
        compiler_params=_cp(dimension_semantics=("arbitrary", "arbitrary")),
    )(core, g, r1)


def _adamw(w, g, m, v):
    m = ADAM_B1 * m + (1.0 - ADAM_B1) * g
    v = ADAM_B2 * v + (1.0 - ADAM_B2) * (g * g)
    m_hat = m / (1.0 - ADAM_B1 ** ADAM_STEP)
    v_hat = v / (1.0 - ADAM_B2 ** ADAM_STEP)
    delta = -ADAM_LR * (m_hat / (jnp.sqrt(v_hat) + ADAM_EPS) + ADAM_WD * w)
    return delta, m, v


def adam_big(w, m, v, part, r2, chip, name, layer=0, prev=None):
    L, R, C = w.shape
    tr = _row_tile(R, C)

    def body(chip_ref, w_ref, m_ref, v_ref, p_ref, r_ref, *rest):
        g_out, d_out, m_out, v_out = rest[-4:]
        g = p_ref[...].astype(f32) + r_ref[0].astype(f32) + r_ref[1].astype(f32) + r_ref[2].astype(f32)
        d, m_, v_ = _adamw(w_ref[...], g, m_ref[...], v_ref[...])
        g_out[...] = g
        d_out[...] = d
        m_out[...] = m_
        v_out[...] = v_

    blk = pl.BlockSpec((None, tr, C), lambda i, chip: (layer, i, 0))
    extra = [] if prev is None else list(prev)
    return pl.pallas_call(
        body, name=name, out_shape=[SDS((L, R, C), f32)] * 4,
        grid_spec=pltpu.PrefetchScalarGridSpec(
            num_scalar_prefetch=1, grid=(R // tr,),
            in_specs=[blk, blk, blk,
                      pl.BlockSpec((None, tr, C), lambda i, chip: (chip[0], i, 0)),
                      pl.BlockSpec((3, tr, C), lambda i, chip: (0, i, 0))] + [_ANY] * len(extra),
            out_specs=[blk] * 4),
        input_output_aliases={6 + k: k for k in range(len(extra))},
        compiler_params=_cp(dimension_semantics=("arbitrary",)),
    )(chip, w, m, v, part, r2, *extra)


def allreduce_small(buf, chips=None):
    shp = buf.shape
    half = (shp[0] // 16) * 8
    parts = (pl.ds(0, half), pl.ds(half, shp[0] - half))
    n_c = 0 if chips is None else len(chips.arrs)

    def body(in_ref, *refs):
        c_in, out_ref, c_out = refs[:n_c], refs[n_c], refs[n_c + 1:2 * n_c + 1]
        acc1, acc2, r0, r1, r2, send_sems, recv_sems = refs[2 * n_c + 1:2 * n_c + 8]
        c_sems = refs[2 * n_c + 8:]
        if chips is not None:
            chips.start(c_in, c_out, c_sems)
        x, y, c = _pos()
        across = [(1 - x, y, c), (x, 1 - y, c)]

        def exchange(src, rcv, dst, copies):
            cps = [pltpu.make_async_remote_copy(
                src_ref=src.at[rows], dst_ref=rcv.at[rows], send_sem=send_sems.at[k], recv_sem=recv_sems.at[k],
                device_id=peer, device_id_type=MESH) for k, rows, peer in copies]
            for cp in cps:
                cp.start()
            for cp in cps:
                cp.wait()
            dst[...] = src[...] + rcv[...]

        exchange(in_ref, r0, acc1, [(0, pl.ds(0, shp[0]), (x, y, 1 - c))])
        exchange(acc1, r1, acc2, [(1, parts[0], across[0]), (2, parts[1], across[1])])
        exchange(acc2, r2, out_ref, [(3, parts[0], across[1]), (4, parts[1], across[0])])
        if chips is not None:
            chips.finish(c_in, c_out, c_sems)

    vm = pl.BlockSpec(memory_space=pltpu.VMEM)
    res = pl.pallas_call(
        body, name="allreduce_small", in_specs=[vm] + [_ANY] * n_c, out_specs=[vm] + [_ANY] * n_c,
        out_shape=[SDS(shp, f32)] + ([] if chips is None else chips.out_shape),
        scratch_shapes=[pltpu.VMEM(shp, f32)] * 5 + [pltpu.SemaphoreType.DMA((5,)), pltpu.SemaphoreType.DMA((5,))]
        + ([] if chips is None else chips.scratch),
    )(buf, *([] if chips is None else chips.arrs))
    if chips is not None:
        chips.result = list(res[1:])
    return res[0]


SMALL_ROWS = {'norm_mix': (0, 2, D), 'norm_mlp': (2, 2, D), 'norm_kv': (4, 1, D), 'norm_final': (5, 1, D),
              's5_d': (6, 1, D), 'b_q': (7, 1, D), 'b_o': (8, 1, D), 's5_b_glu': (9, 2, D), 'b_kv': (11, 1, 512),
              'sinks': (12, 1, 16), 's5_log_dt': (13, 1, 64), 's5_a_re': (16, 4, D), 's5_a_im': (20, 4, D),
              's5_b_re': (24, 64, D), 's5_b_im': (88, 64, D), 's5_c_re': (152, 64, D), 's5_c_im': (216, 64, D)}
LOSS_ROW = 14
ROW_PARAMS = ['norm_mix', 'norm_mlp', 'norm_kv', 'norm_final', 'b_q', 'b_o', 'b_kv', 'sinks', 's5_log_dt']
SHARD_PARAMS = ['s5_d', 's5_b_glu']
S5_PARAMS = ['s5_a_re', 's5_a_im', 's5_b_re', 's5_b_im', 's5_c_re', 's5_c_im']


def adam_small(dev, gsum, s5_grads, w, m, v):
    names = ROW_PARAMS + SHARD_PARAMS + S5_PARAMS
    n_g = len(ROW_PARAMS) + len(SHARD_PARAMS)

    def body(dev_ref, gs_ref, *refs):
        pos = [0]

        def take(k):
            r = refs[pos[0]:pos[0] + k]
            pos[0] += k
            return r

        g5 = take(len(S5_PARAMS))
        wr, mr, vr = take(len(names)), take(len(names)), take(len(names))
        g_out = take(n_g)
        d_out, m_out, v_out = take(len(names)), take(len(names)), take(len(names))
        dv = dev_ref[0]
        for i, n in enumerate(names):
            if n in S5_PARAMS:
                g = g5[S5_PARAMS.index(n)][...]
            elif n in SHARD_PARAMS:
                r0, _, _ = SMALL_ROWS[n]
                ln = wr[i].shape[1]
                g = jnp.zeros((1, ln), f32)
                for k in range(NDEV):
                    off = k * ln
                    piece = gs_ref[r0 + off // D:r0 + off // D + 1, off % D:off % D + ln]
                    g = g + jnp.where(dv == k, piece, 0.0)
                g_out[i][...] = g
            else:
                r0, nr, nl = SMALL_ROWS[n]
                g = gs_ref[r0:r0 + nr, 0:nl]
                g_out[i][...] = g
            d, m_, v_ = _adamw(wr[i][...], g, mr[i][...], vr[i][...])
            d_out[i][...] = d
            m_out[i][...] = m_
            v_out[i][...] = v_

    vm = pl.BlockSpec(memory_space=pltpu.VMEM)
    ins = [s5_grads[n] for n in S5_PARAMS] + [d[n] for d in (w, m, v) for n in names]
    shapes = [SDS(w[n].shape, f32) for n in names]
    res = pl.pallas_call(
        body, name="adam_small", in_specs=[pl.BlockSpec(memory_space=pltpu.SMEM)] + [vm] * (1 + len(ins)),
        out_specs=[vm] * (n_g + 3 * len(names)), out_shape=shapes[:n_g] + shapes * 3,
        compiler_params=_cp(),
    )(dev, gsum, *ins)
    g_o = dict(zip(names[:n_g], res[:n_g]))
    rest = res[n_g:]
    k = len(names)
    return g_o, dict(zip(names, rest[:k])), dict(zip(names, rest[k:2 * k])), dict(zip(names, rest[2 * k:]))


WEIGHTS = ['norm_mix', 'norm_mlp', 'norm_kv', 'norm_final', 's5_a_re', 's5_a_im', 's5_log_dt', 's5_b_re', 's5_b_im',
           's5_c_re', 's5_c_im', 's5_d', 's5_w_glu', 's5_b_glu', 'w_kv', 'b_kv', 'w_q', 'b_q', 'sinks', 'w_o', 'b_o',
           'w_mlp_in', 'w_mlp_out']
BIG = ['s5_w_glu', 'w_kv', 'w_q', 'w_o', 'w_mlp_in', 'w_mlp_out']
BIG_2D = {'s5_w_glu': (D, 256), 'w_kv': (128, 512), 'w_q': (128, D), 'w_o': (128, D), 'w_mlp_in': (2 * D, 512),
          'w_mlp_out': (2 * 512, D)}
SHARDED_SMALL = {'s5_d': D, 's5_b_glu': 2 * D}
SMALL = [n for n in WEIGHTS if n not in BIG]
SMALL_SIZE = {'norm_mix': 2 * D, 'norm_mlp': 2 * D, 'norm_kv': D, 'norm_final': D, 's5_a_re': 4096, 's5_a_im': 4096,
              's5_log_dt': 64, 's5_b_re': 65536, 's5_b_im': 65536, 's5_c_re': 65536, 's5_c_im': 65536, 's5_d': D,
              's5_b_glu': 2 * D, 'b_kv': 512, 'b_q': D, 'sinks': 16, 'b_o': D}


def _pack(vals):
    parts = []
    for n in SMALL:
        v = vals[n].reshape(-1).astype(f32)
        parts.append(jnp.pad(v, (0, (-v.shape[0]) % 128)))
    flat = jnp.concatenate(parts)
    flat = jnp.pad(flat, (0, (-flat.shape[0]) % 1024))
    return flat.reshape(-1, 128)


def _unpack(buf):
    flat = buf.reshape(-1)
    out, off = {}, 0
    for n in SMALL:
        sz = SMALL_SIZE[n]
        out[n] = flat[off:off + sz]
        off += sz + (-sz) % 128
    return out


def kernel(x, norm_mix, norm_mlp, norm_kv, norm_final, s5_a_re, s5_a_im, s5_log_dt, s5_b_re, s5_b_im, s5_c_re, s5_c_im, s5_d, s5_w_glu, s5_b_glu, w_kv, b_kv, w_q, b_q, sinks, w_o, b_o, w_mlp_in, w_mlp_out, loss_target, m_norm_mix, m_norm_mlp, m_norm_kv, m_norm_final, m_s5_a_re, m_s5_a_im, m_s5_log_dt, m_s5_b_re, m_s5_b_im, m_s5_c_re, m_s5_c_im, m_s5_d, m_s5_w_glu, m_s5_b_glu, m_w_kv, m_b_kv, m_w_q, m_b_q, m_sinks, m_w_o, m_b_o, m_w_mlp_in, m_w_mlp_out, v_norm_mix, v_norm_mlp, v_norm_kv, v_norm_final, v_s5_a_re, v_s5_a_im, v_s5_log_dt, v_s5_b_re, v_s5_b_im, v_s5_c_re, v_s5_c_im, v_s5_d, v_s5_w_glu, v_s5_b_glu, v_w_kv, v_b_kv, v_w_q, v_b_q, v_sinks, v_w_o, v_b_o, v_w_mlp_in, v_w_mlp_out):
    w = dict(norm_mix=norm_mix, norm_mlp=norm_mlp, norm_kv=norm_kv, norm_final=norm_final, s5_a_re=s5_a_re,
             s5_a_im=s5_a_im, s5_log_dt=s5_log_dt, s5_b_re=s5_b_re, s5_b_im=s5_b_im, s5_c_re=s5_c_re, s5_c_im=s5_c_im,
             s5_d=s5_d, s5_w_glu=s5_w_glu, s5_b_glu=s5_b_glu, w_kv=w_kv, b_kv=b_kv, w_q=w_q, b_q=b_q, sinks=sinks,
             w_o=w_o, b_o=b_o, w_mlp_in=w_mlp_in, w_mlp_out=w_mlp_out)
    m = dict(norm_mix=m_norm_mix, norm_mlp=m_norm_mlp, norm_kv=m_norm_kv, norm_final=m_norm_final, s5_a_re=m_s5_a_re,
             s5_a_im=m_s5_a_im, s5_log_dt=m_s5_log_dt, s5_b_re=m_s5_b_re, s5_b_im=m_s5_b_im, s5_c_re=m_s5_c_re,
             s5_c_im=m_s5_c_im, s5_d=m_s5_d, s5_w_glu=m_s5_w_glu, s5_b_glu=m_s5_b_glu, w_kv=m_w_kv, b_kv=m_b_kv,
             w_q=m_w_q, b_q=m_b_q, sinks=m_sinks, w_o=m_w_o, b_o=m_b_o, w_mlp_in=m_w_mlp_in, w_mlp_out=m_w_mlp_out)
    v = dict(norm_mix=v_norm_mix, norm_mlp=v_norm_mlp, norm_kv=v_norm_kv, norm_final=v_norm_final, s5_a_re=v_s5_a_re,
             s5_a_im=v_s5_a_im, s5_log_dt=v_s5_log_dt, s5_b_re=v_s5_b_re, s5_b_im=v_s5_b_im, s5_c_re=v_s5_c_re,
             s5_c_im=v_s5_c_im, s5_d=v_s5_d, s5_w_glu=v_s5_w_glu, s5_b_glu=v_s5_b_glu, w_kv=v_w_kv, b_kv=v_b_kv,
             w_q=v_w_q, b_q=v_b_q, sinks=v_sinks, w_o=v_w_o, b_o=v_b_o, w_mlp_in=v_w_mlp_in, w_mlp_out=v_w_mlp_out)
    xi, yi, ci = _pos()
    dev = 4 * xi + 2 * yi + ci
    core = ci.reshape(1).astype(jnp.int32)
    chip = (2 * xi + yi).reshape(1).astype(jnp.int32)

    shards = {
        "s5_w_glu": s5_w_glu[0].astype(bf16), "w_kv": w_kv.astype(bf16), "w_q": w_q[0].astype(bf16),
        "w_o": w_o[0].astype(bf16), "w_in0": w_mlp_in[0].astype(bf16), "w_in1": w_mlp_in[1].astype(bf16),
        "w_out0": w_mlp_out[0].astype(bf16), "w_out1": w_mlp_out[1].astype(bf16),
        "vecs": jnp.broadcast_to(jnp.concatenate([s5_d, s5_b_glu], axis=1), (8, 384)),
    }
    as3d = lambda a, n: a if a.ndim == 3 and a.shape[0] == 2 else a.reshape((1,) + BIG_2D[n])
    opt = {n: (as3d(w[n], n), as3d(m[n], n), as3d(v[n], n)) for n in BIG}
    _, grad_x, grads, big = fwd_bwd(x[0], loss_target[0], {n: w[n] for n in SMALL}, shards, opt, core, chip)

    gsum = allreduce_small(grads)

    out_g, out_d, out_m, out_v = {}, {}, {}, {}
    for n in BIG:
        out_g[n], out_d[n], out_m[n], out_v[n] = [r.reshape(w[n].shape) for r in big[n]]

    loss = gsum[LOSS_ROW, 0]
    swapped = ("s5_b_re", "s5_b_im")
    swap = lambda a: a.transpose(0, 1, 3, 2)

    def kernel_side(d):
        d = {n: (d[n].reshape(1, -1) if d[n].ndim == 1 else d[n]) for n in SMALL}
        d.update({n: swap(d[n]) for n in swapped})
        return d

    s5_g = {}
    for n in S5_PARAMS:
        r0, nr, _ = SMALL_ROWS[n]
        s5_g[n] = gsum[r0:r0 + nr].reshape((1, 64, 16, 64) if n in swapped else w[n].shape)
        out_g[n] = s5_g[n]
    g_s, d_s, m_s, v_s = adam_small(dev.reshape(1).astype(jnp.int32), gsum, s5_g, kernel_side(w), kernel_side(m),
                                    kernel_side(v))
    for src, dst in ((g_s, out_g), (d_s, out_d), (m_s, out_m), (v_s, out_v)):
        dst.update(src)
    for dst in (out_g, out_d, out_m, out_v):
        for n in SMALL:
            dst[n] = (swap(dst[n]) if n in swapped else dst[n]).reshape(w[n].shape)

    return (loss, grad_x[None], *[out_g[n] for n in WEIGHTS], *[out_d[n] for n in WEIGHTS],
            *[out_m[n] for n in WEIGHTS], *[out_v[n] for n in WEIGHTS])
```

```python
import functools
import math

import jax
import jax.numpy as jnp
from jax import lax
from jax.experimental import pallas as pl
from jax.experimental.pallas import tpu as pltpu
from jax.experimental.pallas import tpu_sc as plsc

f32 = jnp.float32
bf16 = jnp.bfloat16
SDS = jax.ShapeDtypeStruct

T = 2048
D = 1024
NDEV = 8
NORM_EPS = 1e-5
S5_G, S5_C, S5_P = 64, 16, 64
S5_SUB = 8
S5_CH = 8
S5_STEPS = T // S5_CH
DT_MIN_LAMBDA = -1e-4
HEAD_DIM = 64
N_KV = 4
Q_PER_KV = 4
BLK = 128
D_FF_SHARD = 512
ADAM_LR, ADAM_B1, ADAM_B2, ADAM_EPS, ADAM_WD, ADAM_STEP = 0.001, 0.9, 0.999, 1e-08, 0.01, 10
VMEM_LIMIT = 56 * 1024 * 1024
MESH = pl.DeviceIdType.MESH


def _cp(**kw):
    return pltpu.CompilerParams(vmem_limit_bytes=VMEM_LIMIT, **kw)


def _dot(a, b):
    return jnp.dot(a, b, preferred_element_type=f32)


def _dot_nt(a, b):
    return lax.dot_general(a, b, (((1,), (1,)), ((), ())), preferred_element_type=f32)


def _dot_tn(a, b):
    return lax.dot_general(a, b, (((0,), (0,)), ((), ())), preferred_element_type=f32)


def _rms(x, g):
    r = lax.rsqrt(jnp.mean(x * x, axis=-1, keepdims=True) + NORM_EPS)
    return x * r * g, r


def _rms_bwd(x, g, dy):
    r = lax.rsqrt(jnp.mean(x * x, axis=-1, keepdims=True) + NORM_EPS)
    u = dy * g
    dx = r * u - (r * r * r) * x * jnp.mean(u * x, axis=-1, keepdims=True)
    return dx, dy * x * r


def _colsum8(v):
    s = jnp.sum(v, axis=0, keepdims=True)
    row = lax.broadcasted_iota(jnp.int32, (8, v.shape[1]), 0)
    return jnp.where(row == 0, jnp.broadcast_to(s, (8, v.shape[1])), 0.0)


def _full(shape):
    nd = len(shape)
    return pl.BlockSpec(shape, lambda *_: (0,) * nd, pipeline_mode=pl.Buffered(1))


_ANY = pl.BlockSpec(memory_space=pl.ANY)


def _pos():
    return lax.axis_index("x"), lax.axis_index("y"), lax.axis_index("c")


def _other_chips(x, y):
    return [(1 - x, y), (x, 1 - y), (1 - x, 1 - y)]


class BgGather:
    SIB, XN, YN, FWD_Y, FWD_X, SIB_X, SIB_Y, SIB_D = range(8)

    def __init__(self, arrs, mids=(0.5, 0.75)):
        n = len(arrs)
        self.arrs = list(arrs)
        self.out_shape = [SDS((NDEV,) + a.shape, a.dtype) for a in arrs]
        self.scratch = [pltpu.SemaphoreType.DMA((n, 8)), pltpu.SemaphoreType.DMA((n, 8)),
                        pltpu.SemaphoreType.DMA((n,))]
        self.mids = mids
        self.result = None

    @staticmethod
    def peers(x, y, c):
        return [(x, y, 1 - c), (1 - x, y, c), (x, 1 - y, c)]

    def mid_steps(self, nsteps):
        at = lambda f: min(nsteps - 1, max(0, int(f * nsteps) - 1))
        return [(at(self.mids[0]), self.mid), (max(at(self.mids[0]), at(self.mids[1])), self.mid2)]

    def _halves(self, a):
        rows = self.arrs[a].shape[0]
        cut = (rows // 32) * 16 if rows >= 32 else rows
        return (0, cut), (cut, rows - cut)

    def _copy(self, ins, outs, sems, a, k, block, to, own=False, part=None):
        slot = 4 * block[0] + 2 * block[1] + block[2]
        rows = pl.ds(0, self.arrs[a].shape[0]) if part is None else pl.ds(*self._halves(a)[part])
        dst = outs[a].at[slot, rows]
        return pltpu.make_async_remote_copy(
            src_ref=ins[a].at[rows] if own else dst, dst_ref=dst, send_sem=sems[0].at[a, k],
            recv_sem=sems[1].at[a, k], device_id=to, device_id_type=MESH)

    def _mine(self, ins, outs, sems):
        x, y, c = _pos()
        return [pltpu.make_async_copy(ins[a], outs[a].at[4 * x + 2 * y + c], sems[2].at[a])
                for a in range(len(self.arrs))]

    def _split(self, a):
        return self._halves(a)[1][1] > 0

    def _sends(self, ins, outs, sems, phase):
        x, y, c = _pos()
        me, sib, xn, yn, dg = (x, y, c), (x, y, 1 - c), (1 - x, y, c), (x, 1 - y, c), (1 - x, 1 - y, c)
        cps = []
        for a in range(len(self.arrs)):
            cp = lambda k, block, to, **kw: self._copy(ins, outs, sems, a, k, block, to, **kw)
            if phase == 0:
                cps += [cp(self.SIB, me, sib, own=True), cp(self.XN, me, xn, own=True), cp(self.YN, me, yn, own=True)]
            elif phase == 1:
                cps.append(cp(self.FWD_Y, xn, yn, part=0))
                if self._split(a):
                    cps.append(cp(self.FWD_X, yn, xn, part=1))
                cps += [cp(self.SIB_X, xn, sib), cp(self.SIB_Y, yn, sib)]
            else:
                cps.append(cp(self.SIB_D, dg, sib))
        return cps

    def _arrivals(self, ins, outs, sems, phase):
        x, y, c = _pos()
        me, xn, yn, dg = (x, y, c), (1 - x, y, c), (x, 1 - y, c), (1 - x, 1 - y, c)
        cps = []
        for a in range(len(self.arrs)):
            cp = lambda k, block, **kw: self._copy(ins, outs, sems, a, k, block, me, **kw)
            if phase == 1:
                cps += [cp(self.XN, xn), cp(self.YN, yn)]
            elif phase == 2:
                cps.append(cp(self.FWD_Y, dg, part=0))
                if self._split(a):
                    cps.append(cp(self.FWD_X, dg, part=1))
            else:
                cps += [cp(self.SIB, (x, y, 1 - c)), cp(self.SIB_X, (1 - x, y, 1 - c)),
                        cp(self.SIB_Y, (x, 1 - y, 1 - c)), cp(self.SIB_D, (1 - x, 1 - y, 1 - c))]
        return cps

    def start(self, ins, outs, sems):
        for cp in self._mine(ins, outs, sems) + self._sends(ins, outs, sems, 0):
            cp.start()

    def mid(self, ins, outs, sems):
        for cp in self._arrivals(ins, outs, sems, 1):
            cp.wait_recv()
        for cp in self._sends(ins, outs, sems, 1):
            cp.start()

    def mid2(self, ins, outs, sems):
        for cp in self._arrivals(ins, outs, sems, 2):
            cp.wait_recv()
        for cp in self._sends(ins, outs, sems, 2):
            cp.start()

    def finish(self, ins, outs, sems):
        for cp in self._arrivals(ins, outs, sems, 3):
            cp.wait_recv()
        for ph in range(3):
            for cp in self._sends(ins, outs, sems, ph):
                cp.wait_send()
        for cp in self._mine(ins, outs, sems):
            cp.wait()


def sc_comm(g, collective_id, name):
    srcs = [jax.new_ref(a, memory_space=pltpu.MemorySpace.HBM) for a in g.arrs]
    dsts = [jax.empty_ref(s, memory_space=pltpu.MemorySpace.HBM) for s in g.out_shape]

    @pl.kernel(mesh=plsc.ScalarSubcoreMesh(axis_name="sequencer", num_cores=1), name=name,
               scratch_types=tuple(g.scratch), compiler_params=pltpu.CompilerParams(collective_id=collective_id))
    def launch(*sems):
        peers = g.peers(*_pos())
        barrier = pltpu.get_barrier_semaphore()
        for peer in peers:
            pl.semaphore_signal(barrier, inc=1, device_id=peer, device_id_type=MESH)
        pl.semaphore_wait(barrier, len(peers))
        g.start(srcs, dsts, sems)
        for _, phase in g.mid_steps(1):
            phase(srcs, dsts, sems)
        g.finish(srcs, dsts, sems)

    launch()
    return [d[...] for d in dsts]


def sc_gather(arrs, collective_id, name):
    return sc_comm(BgGather(arrs), collective_id, name)


class BgPair:
    def __init__(self, arrs):
        n = len(arrs)
        self.arrs = list(arrs)
        self.out_shape = [SDS((4,) + a.shape[1:], a.dtype) for a in arrs]
        self.scratch = [pltpu.SemaphoreType.DMA((n, 4)), pltpu.SemaphoreType.DMA((n, 4))]
        self.result = None

    @staticmethod
    def peers(x, y, c):
        return [(x, y, 1 - c)]

    def mid_steps(self, nsteps):
        return []

    def _copies(self, ins, outs, sems):
        x, y, c = _pos()
        return [pltpu.make_async_remote_copy(
            src_ref=ins[a].at[2 * k + 1 - c], dst_ref=outs[a].at[k], send_sem=sems[0].at[a, k],
            recv_sem=sems[1].at[a, k], device_id=(x, y, 1 - c), device_id_type=MESH)
            for a in range(len(self.arrs)) for k in range(4)]

    def start(self, ins, outs, sems):
        for cp in self._copies(ins, outs, sems):
            cp.start()

    def finish(self, ins, outs, sems):
        cps = self._copies(ins, outs, sems)
        for cp in cps:
            cp.wait_recv()
        for cp in cps:
            cp.wait_send()


class BgChips(BgPair):
    def __init__(self, arrs):
        n = len(arrs)
        self.arrs = list(arrs)
        self.out_shape = [SDS((3,) + a.shape[1:], a.dtype) for a in arrs]
        self.scratch = [pltpu.SemaphoreType.DMA((n, 3)), pltpu.SemaphoreType.DMA((n, 3))]
        self.result = None

    @staticmethod
    def peers(x, y, c):
        return [(px, py, c) for px, py in _other_chips(x, y)]

    def _copies(self, ins, outs, sems):
        x, y, c = _pos()
        return [pltpu.make_async_remote_copy(
            src_ref=ins[a].at[2 * px + py], dst_ref=outs[a].at[r], send_sem=sems[0].at[a, r],
            recv_sem=sems[1].at[a, r], device_id=(px, py, c), device_id_type=MESH)
            for a in range(len(self.arrs)) for r, (px, py) in enumerate(_other_chips(x, y))]


class AdamRider:
    def __init__(self, w, m, v, part, r2, layer=0, prev=None):
        self.arrs = [w, m, v, part, r2] + list(prev or [])
        self.n_prev = len(prev or [])
        self.layer = layer
        self.out_shape = [SDS(w.shape, f32)] * 4
        self.scratch = []
        self.aliases = {5 + k: k for k in range(self.n_prev)}
        self.result = None

    def _tile(self, grid):
        assert len(grid) == 1
        _, R, C = self.arrs[0].shape
        return R // grid[0], C

    def in_specs(self, grid):
        tr, C = self._tile(grid)
        layer = self.layer
        blk = pl.BlockSpec((None, tr, C), lambda b: (layer, b, 0))
        mine = pl.BlockSpec((None, tr, C), lambda b: (2 * lax.axis_index("x") + lax.axis_index("y"), b, 0))
        return [blk, blk, blk, mine, pl.BlockSpec((3, tr, C), lambda b: (0, b, 0))] + [_ANY] * self.n_prev

    def out_specs(self, grid):
        tr, C = self._tile(grid)
        layer = self.layer
        return [pl.BlockSpec((None, tr, C), lambda b: (layer, b, 0))] * 4

    def mid_steps(self, nsteps):
        return []

    def start(self, ins, outs, sems):
        pass

    finish = start

    def step(self, ins, outs, sems):
        w_ref, m_ref, v_ref, p_ref, r_ref = ins[:5]
        g = p_ref[...].astype(f32) + r_ref[0].astype(f32) + r_ref[1].astype(f32) + r_ref[2].astype(f32)
        d, m_, v_ = _adamw(w_ref[...], g, m_ref[...], v_ref[...])
        for ref, val in zip(outs, (g, d, m_, v_)):
            ref[...] = val


def _call(bgs, body, *, name, grid, in_specs, out_specs, out_shape, scratch_shapes=(), compiler_params=None):
    single = not isinstance(out_shape, (list, tuple))
    out_specs_l = [out_specs] if single else list(out_specs)
    out_shape_l = [out_shape] if single else list(out_shape)
    bgs = [b for b in (bgs or []) if b is not None]
    n_in, n_out, n_sc = len(in_specs), len(out_shape_l), len(scratch_shapes)
    nsteps = math.prod(grid)
    b_in_specs = [b.in_specs(grid) if hasattr(b, "in_specs") else [_ANY] * len(b.arrs) for b in bgs]
    b_out_specs = [b.out_specs(grid) if hasattr(b, "out_specs") else [_ANY] * len(b.out_shape) for b in bgs]
    aliases, i_off, o_off = {}, n_in, n_out
    for b in bgs:
        aliases.update({i_off + i: o_off + o for i, o in getattr(b, "aliases", {}).items()})
        i_off, o_off = i_off + len(b.arrs), o_off + len(b.out_shape)

    def full(*refs):
        pos = [0]

        def take(k):
            r = refs[pos[0]:pos[0] + k]
            pos[0] += k
            return r

        ins = take(n_in)
        b_ins = [take(len(b.arrs)) for b in bgs]
        outs = take(n_out)
        b_outs = [take(len(b.out_shape)) for b in bgs]
        sc = take(n_sc)
        b_sc = [take(len(b.scratch)) for b in bgs]
        if bgs:
            step = pl.program_id(0)
            for d in range(1, len(grid)):
                step = step * grid[d] + pl.program_id(d)

            @pl.when(step == 0)
            def _():
                for b, i_, o_, s_ in zip(bgs, b_ins, b_outs, b_sc):
                    b.start(i_, o_, s_)

        body(*ins, *outs, *sc)
        if bgs:
            for b, i_, o_, s_ in zip(bgs, b_ins, b_outs, b_sc):
                if hasattr(b, "step"):
                    b.step(i_, o_, s_)
                for at, fn in b.mid_steps(nsteps):
                    @pl.when(step == at)
                    def _():
                        fn(i_, o_, s_)

            @pl.when(step == nsteps - 1)
            def _():
                for b, i_, o_, s_ in zip(bgs, b_ins, b_outs, b_sc):
                    b.finish(i_, o_, s_)

    def run(*args):
        res = pl.pallas_call(
            full, name=name, grid=grid,
            in_specs=list(in_specs) + [s for l in b_in_specs for s in l],
            out_specs=out_specs_l + [s for l in b_out_specs for s in l],
            out_shape=out_shape_l + [s for b in bgs for s in b.out_shape],
            scratch_shapes=list(scratch_shapes) + [s for b in bgs for s in b.scratch],
            input_output_aliases=aliases,
            compiler_params=compiler_params,
        )(*args, *[a for b in bgs for a in b.arrs])
        rest = list(res[n_out:])
        for b in bgs:
            b.result, rest = rest[:len(b.out_shape)], rest[len(b.out_shape):]
        return res[0] if single else list(res[:n_out])

    return run


def s5_discretize(a_re, a_im, log_dt, b_re, b_im, c_re, c_im):
    lam_r = jnp.minimum(a_re, DT_MIN_LAMBDA)
    lam_i = a_im
    dt = jnp.exp(log_dt)[:, None]
    e = jnp.exp(lam_r * dt)
    lbr = e * jnp.cos(lam_i * dt)
    lbi = e * jnp.sin(lam_i * dt)
    den = lam_r * lam_r + lam_i * lam_i
    cf_r = ((lbr - 1.0) * lam_r + lbi * lam_i) / den
    cf_i = (lbi * lam_r - (lbr - 1.0) * lam_i) / den
    bb_r = cf_r[:, :, None] * b_re - cf_i[:, :, None] * b_im
    bb_i = cf_r[:, :, None] * b_im + cf_i[:, :, None] * b_re
    eye = jnp.eye(8, dtype=f32)

    def blk_b(m):
        return jnp.einsum('bgpc,gh->bgchp', m.reshape(8, 8, S5_P, S5_C), eye).reshape(8, 128, 512)

    def blk_c(m):
        return jnp.einsum('bgcp,gh->bgphc', m.reshape(8, 8, S5_C, S5_P), eye).reshape(8, 512, 128)

    bm = jnp.concatenate([blk_b(bb_r), blk_b(bb_i)], axis=-1)
    cm = jnp.concatenate([blk_c(c_re), -blk_c(c_im)], axis=1)
    lam = jnp.stack([lbr.reshape(8, 512), lbi.reshape(8, 512)], axis=1)
    lam = jnp.broadcast_to(lam[:, :, None, :], (8, 2, 8, 512))
    return lam, bm, cm


def _cmul(ar, ai, br, bi):
    return ar * br - ai * bi, ar * bi + ai * br


def _shift_rows(v, k, up):
    row = lax.broadcasted_iota(jnp.int32, v.shape, 0)
    if up:
        return jnp.where(row < 8 - k, pltpu.roll(v, 8 - k, 0), 0.0)
    return jnp.where(row >= k, pltpu.roll(v, k, 0), 0.0)


def _chunk_scan(S, lr, li, reverse, aux=None):
    z = jnp.zeros((8, 512), f32)
    U = 4

    def idx(i):
        return (S5_STEPS - 1 - i) if reverse else i

    def rows_of(s):
        return pl.ds(s * 8, 8) if isinstance(s, int) else pl.ds(pl.multiple_of(s * 8, 8), 8)

    def rec(xr, xi, row):
        br = S[row, 0:512]
        bi = S[row, 512:1024]
        return lr * xr - li * xi + br, lr * xi + li * xr + bi

    def step1(i, c):
        for u in range(U):
            c = rec(c[0], c[1], rows_of(idx(i * U + u)))
        return c

    er, ei = lax.fori_loop(0, S5_STEPS // U, step1, (z, z))
    ar, ai = lr, li
    for _ in range(8):
        ar, ai = _cmul(ar, ai, ar, ai)
    cr, ci = _shift_rows(er, 1, reverse), _shift_rows(ei, 1, reverse)
    for k in (1, 2, 4):
        sr, si = _shift_rows(cr, k, reverse), _shift_rows(ci, k, reverse)
        pr, pi_ = _cmul(ar, ai, sr, si)
        cr, ci = cr + pr, ci + pi_
        ar, ai = _cmul(ar, ai, ar, ai)

    if aux is None:
        def step2(i, c):
            for u in range(U):
                row = rows_of(idx(i * U + u))
                c = rec(c[0], c[1], row)
                S[row, 0:512] = c[0]
                S[row, 512:1024] = c[1]
            return c

        lax.fori_loop(0, S5_STEPS // U, step2, (cr, ci))
        return None

    def one(s, c):
        gr0, gi0, dr, di = c
        row = rows_of(s)
        gr, gi = rec(gr0, gi0, row)
        S[row, 0:512] = gr
        S[row, 512:1024] = gi
        prow = rows_of(s - 1)
        xr = aux[prow, 0:512]
        xi = aux[prow, 512:1024]
        return gr, gi, dr + gr * xr + gi * xi, di + gi * xr - gr * xi

    def step2(i, c):
        for u in range(U):
            c = one(S5_STEPS - 1 - (i * U + u), c)
        return c

    c = lax.fori_loop(0, S5_STEPS // U - 1, step2, (cr, ci, z, z))
    for s in range(U - 1, 0, -1):
        c = one(s, c)
    gr, gi, dr, di = c
    row0 = pl.ds(0, 8)
    gr, gi = rec(gr, gi, row0)
    S[row0, 0:512] = gr
    S[row0, 512:1024] = gi
    last = pl.ds((S5_STEPS - 1) * 8, 8)
    xr = _shift_rows(aux[last, 0:512], 1, False)
    xi = _shift_rows(aux[last, 512:1024], 1, False)
    dr = dr + gr * xr + gi * xi
    di = di + gi * xr - gr * xi
    return dr, di


_ROWS = 256


def _row_loop(fn):
    def body(r, c):
        fn(pl.ds(pl.multiple_of(r * _ROWS, _ROWS), _ROWS))
        return c
    lax.fori_loop(0, T // _ROWS, body, 0)


def s5_core_fwd(hn, bm, lam, cm, bg=()):
    nt = T // _ROWS

    def body(u_ref, b_ref, lam_ref, c_ref, ys_ref, S):
        lr, li = lam_ref[0], lam_ref[1]
        z = jnp.zeros((8, 512), f32)
        tile = lambda k: pl.ds(k * _ROWS, _ROWS)
        c = (z, z)
        for k in range(nt):
            S[tile(k), :] = _dot(_rows_in(u_ref, k).astype(bf16), b_ref[...])
            if k >= 1:
                c = _scan_tile(S, lr, li, k - 1, c, False, False)
        c = _scan_tile(S, lr, li, nt - 1, c, False, False)
        c = _chunk_starts(c[0], c[1], lr, li, False)
        for k in range(nt):
            c = _scan_tile(S, lr, li, k, c, False, True)
            if k >= 1:
                _rows_out(ys_ref, k - 1, _dot(S[tile(k - 1), :].astype(bf16), c_ref[...]))
        _rows_out(ys_ref, nt - 1, _dot(S[tile(nt - 1), :].astype(bf16), c_ref[...]))

    return _call(
        bg, body, name="s5_core_fwd", grid=(S5_SUB,),
        in_specs=[pl.BlockSpec((T, 128), lambda b: (0, b)),
                  pl.BlockSpec((None, 128, 1024), lambda b: (b, 0, 0)),
                  pl.BlockSpec((None, 4, 8, 512), lambda b: (b, 0, 0, 0)),
                  pl.BlockSpec((None, 1024, 128), lambda b: (b, 0, 0))],
        out_specs=pl.BlockSpec((T, 128), lambda b: (0, b)),
        out_shape=SDS((T, D), f32),
        scratch_shapes=[pltpu.VMEM((T, 1024), f32)],
        compiler_params=_cp(dimension_semantics=("arbitrary",)),
    )(hn, bm, lam, cm)


_SEG = _ROWS // S5_CH


def _rows_in(ref, k):
    return jnp.concatenate([ref[pl.ds(s, S5_CH, stride=S5_STEPS), :] for s in range(k * _SEG, (k + 1) * _SEG)], axis=0)


def _rows_out(ref, k, val):
    for j, s in enumerate(range(k * _SEG, (k + 1) * _SEG)):
        ref[pl.ds(s, S5_CH, stride=S5_STEPS), :] = val[j * S5_CH:(j + 1) * S5_CH, :]


def _scan_tile(S, lr, li, k, carry, reverse, store, aux=None):
    steps = range(k * _SEG, (k + 1) * _SEG)
    for s in (reversed(steps) if reverse else steps):
        row = pl.ds(s * 8, 8)
        xr, xi = carry[0], carry[1]
        nr = lr * xr - li * xi + S[row, 0:512]
        ni = lr * xi + li * xr + S[row, 512:1024]
        if store:
            S[row, 0:512] = nr
            S[row, 512:1024] = ni
        if aux is not None and s >= 1:
            prow = pl.ds((s - 1) * 8, 8)
            pr, pi_ = aux[prow, 0:512], aux[prow, 512:1024]
            carry = (nr, ni, carry[2] + nr * pr + ni * pi_, carry[3] + ni * pr - nr * pi_)
        elif aux is not None:
            carry = (nr, ni, carry[2], carry[3])
        else:
            carry = (nr, ni)
    return carry


def _chunk_starts(er, ei, lr, li, reverse):
    ar, ai = lr, li
    for _ in range(8):
        ar, ai = _cmul(ar, ai, ar, ai)
    cr, ci = _shift_rows(er, 1, reverse), _shift_rows(ei, 1, reverse)
    for k in (1, 2, 4):
        sr, si = _shift_rows(cr, k, reverse), _shift_rows(ci, k, reverse)
        pr, pi_ = _cmul(ar, ai, sr, si)
        cr, ci = cr + pr, ci + pi_
        ar, ai = _cmul(ar, ai, ar, ai)
    return cr, ci


def s5_core_bwd(hn, dy, bm, lam, cm, bg=()):
    nt = T // _ROWS

    def body(u_ref, dy_ref, b_ref, lam_ref, c_ref, du_ref, db_ref, dct_ref, dlam_ref, S1, S2):
        lr, li, lcr, lci = lam_ref[0], lam_ref[1], lam_ref[2], lam_ref[3]
        z = jnp.zeros((8, 512), f32)
        tile = lambda k: pl.ds(k * _ROWS, _ROWS)
        dyb = lambda k: _rows_in(dy_ref, k).astype(bf16)

        c = (z, z)
        for k in range(nt):
            S1[tile(k), :] = _dot(_rows_in(u_ref, k).astype(bf16), b_ref[...])
            if k >= 1:
                c = _scan_tile(S1, lr, li, k - 1, c, False, False)
        c = _scan_tile(S1, lr, li, nt - 1, c, False, False)

        c = _chunk_starts(c[0], c[1], lr, li, False)
        dct_ref[...] = jnp.zeros_like(dct_ref)
        for k in range(nt):
            c = _scan_tile(S1, lr, li, k, c, False, True)
            if k >= 1:
                dct_ref[...] += _dot_tn(dyb(k - 1), S1[tile(k - 1), :].astype(bf16))
        dct_ref[...] += _dot_tn(dyb(nt - 1), S1[tile(nt - 1), :].astype(bf16))

        S2[tile(nt - 1), :] = _dot_nt(dyb(nt - 1), c_ref[...])
        c = (z, z)
        for k in range(nt - 1, -1, -1):
            if k >= 1:
                S2[tile(k - 1), :] = _dot_nt(dyb(k - 1), c_ref[...])
            c = _scan_tile(S2, lcr, lci, k, c, True, False)

        def dbu(k):
            gb = S2[tile(k), :].astype(bf16)
            db_ref[...] += _dot_tn(_rows_in(u_ref, k).astype(bf16), gb)
            _rows_out(du_ref, k, _dot_nt(gb, b_ref[...]))

        c = _chunk_starts(c[0], c[1], lcr, lci, True) + (z, z)
        db_ref[...] = jnp.zeros_like(db_ref)
        for k in range(nt - 1, -1, -1):
            c = _scan_tile(S2, lcr, lci, k, c, True, True, aux=S1)
            if k + 1 < nt:
                dbu(k + 1)
        dbu(0)
        gr, gi, dr, di = c
        last = pl.ds((S5_STEPS - 1) * 8, 8)
        xr = _shift_rows(S1[last, 0:512], 1, False)
        xi = _shift_rows(S1[last, 512:1024], 1, False)
        dlam_ref[0] = dr + gr * xr + gi * xi
        dlam_ref[1] = di + gi * xr - gr * xi

    return _call(
        bg, body, name="s5_core_bwd", grid=(S5_SUB,),
        in_specs=[pl.BlockSpec((T, 128), lambda b: (0, b)),
                  pl.BlockSpec((T, 128), lambda b: (0, b)),
                  pl.BlockSpec((None, 128, 1024), lambda b: (b, 0, 0)),
                  pl.BlockSpec((None, 4, 8, 512), lambda b: (b, 0, 0, 0)),
                  pl.BlockSpec((None, 1024, 128), lambda b: (b, 0, 0))],
        out_specs=[pl.BlockSpec((T, 128), lambda b: (0, b)),
                   pl.BlockSpec((None, 128, 1024), lambda b: (b, 0, 0)),
                   pl.BlockSpec((None, 128, 1024), lambda b: (b, 0, 0)),
                   pl.BlockSpec((None, 2, 8, 512), lambda b: (b, 0, 0, 0))],
        out_shape=[SDS((T, D), f32), SDS((8, 128, 1024), f32), SDS((8, 128, 1024), f32), SDS((8, 2, 8, 512), f32)],
        scratch_shapes=[pltpu.VMEM((T, 1024), f32), pltpu.VMEM((T, 1024), f32)],
        compiler_params=_cp(dimension_semantics=("arbitrary",)),
    )(hn, dy, bm, lam, cm)


TM = 512
NT = T // TM


def _tile(n=D):
    return pl.BlockSpec((TM, n), lambda i: (i, 0))


def s5_pre(xp, g):
    def body(x_ref, g_ref, hn_ref):
        hn_ref[...] = _rms(x_ref[...], g_ref[...])[0]

    return pl.pallas_call(
        body, name="s5_pre", grid=(NT,), in_specs=[_tile(), _full((1, D))], out_specs=_tile(),
        out_shape=SDS((T, D), f32), compiler_params=_cp(dimension_semantics=("arbitrary",)),
    )(xp, g)


def _gelu_grad(y):
    c = math.sqrt(2.0 / math.pi)
    t = jnp.tanh(c * (y + 0.044715 * y * y * y))
    return 0.5 * (1.0 + t) + 0.5 * y * (1.0 - t * t) * c * (1.0 + 3.0 * 0.044715 * y * y)


def s5_post(ys, xp, g, d, wglu, bglu, bg=()):
    def body(ys_ref, x_ref, g_ref, d_ref, w_ref, b_ref, y_ref, z_ref, h_ref):
        x = x_ref[...]
        hn, _ = _rms(x, g_ref[...])
        y = ys_ref[...] + d_ref[...] * hn
        y_ref[...] = y
        yg = jax.nn.gelu(y).astype(bf16)
        for j in range(4):
            cv = slice(j * 256, (j + 1) * 256)
            cg = slice(1024 + j * 256, 1024 + (j + 1) * 256)
            val = _dot(yg, w_ref[j]) + b_ref[:, cv]
            gate = _dot(yg, w_ref[j + 4]) + b_ref[:, cg]
            z_ref[:, cv] = val
            z_ref[:, cg] = gate
            h_ref[:, cv] = x[:, cv] + val * jax.nn.sigmoid(gate)

    return _call(
        bg, body, name="s5_post", grid=(NT,),
        in_specs=[_tile(), _tile(), _full((1, D)), _full((1, D)), _full((8, D, 256)), _full((1, 2 * D))],
        out_specs=[_tile(), _tile(2 * D), _tile()],
        out_shape=[SDS((T, D), f32), SDS((T, 2 * D), f32), SDS((T, D), f32)],
        compiler_params=_cp(dimension_semantics=("arbitrary",)),
    )(ys, xp, g, d, wglu, bglu)


def s5_post_bwd(dh, y, z, wglu, bg=()):
    def body(dh_ref, y_ref, z_ref, w_ref, dy_ref, dw_ref, db_ref, acc):
        i = pl.program_id(0)

        @pl.when(i == 0)
        def _():
            acc[...] = jnp.zeros_like(acc)
            db_ref[...] = jnp.zeros_like(db_ref)

        dh_ = dh_ref[...]
        y = y_ref[...]
        yg = jax.nn.gelu(y).astype(bf16)
        dyg = jnp.zeros((TM, D), f32)
        for j in range(4):
            cv = slice(j * 256, (j + 1) * 256)
            cg = slice(1024 + j * 256, 1024 + (j + 1) * 256)
            val = z_ref[:, cv]
            sg = jax.nn.sigmoid(z_ref[:, cg])
            dval = dh_[:, cv] * sg
            dgate = dh_[:, cv] * val * sg * (1.0 - sg)
            db_ref[:, cv] += _colsum8(dval)
            db_ref[:, cg] += _colsum8(dgate)
            dvb = dval.astype(bf16)
            dgb = dgate.astype(bf16)
            acc[j] += _dot_tn(yg, dvb)
            acc[j + 4] += _dot_tn(yg, dgb)
            dyg = dyg + _dot_nt(dvb, w_ref[j]) + _dot_nt(dgb, w_ref[j + 4])
        dy_ref[...] = dyg * _gelu_grad(y)

        @pl.when(i == NT - 1)
        def _():
            dw_ref[...] = acc[...].astype(bf16)

    return _call(
        bg, body, name="s5_post_bwd", grid=(NT,),
        in_specs=[_tile(), _tile(), _tile(2 * D), _full((8, D, 256))],
        out_specs=[_tile(), _full((8, D, 256)), _full((8, 2 * D))],
        out_shape=[SDS((T, D), f32), SDS((8, D, 256), bf16), SDS((8, 2 * D), f32)],
        scratch_shapes=[pltpu.VMEM((8, D, 256), f32)],
        compiler_params=_cp(dimension_semantics=("arbitrary",)),
    )(dh, y, z, wglu)


def s5_pre_bwd(xp, g, du, dy, d, dh, bg=()):
    def body(x_ref, g_ref, du_ref, dy_ref, d_ref, dh_ref, dx_ref, dg_ref, dd_ref):
        i = pl.program_id(0)

        @pl.when(i == 0)
        def _():
            dg_ref[...] = jnp.zeros_like(dg_ref)
            dd_ref[...] = jnp.zeros_like(dd_ref)

        x = x_ref[...]
        g = g_ref[...]
        dy = dy_ref[...]
        hn, _ = _rms(x, g)
        dhn = du_ref[...] + d_ref[...] * dy
        dx, dgt = _rms_bwd(x, g, dhn)
        dx_ref[...] = dh_ref[...] + dx
        dg_ref[...] += _colsum8(dgt)
        dd_ref[...] += _colsum8(dy * hn)

    return _call(
        bg, body, name="s5_pre_bwd", grid=(NT,),
        in_specs=[_tile(), _full((1, D)), _tile(), _tile(), _full((1, D)), _tile()],
        out_specs=[_tile(), _full((8, D)), _full((8, D))],
        out_shape=[SDS((T, D), f32), SDS((8, D), f32), SDS((8, D), f32)],
        compiler_params=_cp(dimension_semantics=("arbitrary",)),
    )(xp, g, du, dy, d, dh)


TMF = 1024


def mlp_fwd(h, g, w_in, w_out, layer, bg=()):
    def body(h_ref, g_ref, wi_ref, wo_ref, hm_ref, r_ref, out_ref, acc):
        j = pl.program_id(1)

        @pl.when(j == 0)
        def _():
            hm, _ = _rms(h_ref[...], g_ref[...])
            hm_ref[...] = hm.astype(bf16)
            acc[...] = jnp.zeros_like(acc)

        a = jnp.maximum(_dot(hm_ref[...], wi_ref[...]), 0.0)
        r_ref[...] = a.astype(bf16)
        acc[...] += _dot((a * a).astype(bf16), wo_ref[...])

        @pl.when(j == NDEV - 1)
        def _():
            out_ref[...] = h_ref[...] + acc[...]

    return _call(
        bg, body, name=f"mlp_fwd{layer}", grid=(T // TMF, NDEV),
        in_specs=[pl.BlockSpec((TMF, D), lambda i, j: (i, 0)),
                  pl.BlockSpec((1, D), lambda i, j: (0, 0)),
                  pl.BlockSpec((None, D, D_FF_SHARD), lambda i, j: (j, 0, 0)),
                  pl.BlockSpec((None, D_FF_SHARD, D), lambda i, j: (j, 0, 0))],
        out_specs=[pl.BlockSpec((TMF, D), lambda i, j: (i, 0)), pl.BlockSpec((TMF, D_FF_SHARD), lambda i, j: (i, j)),
                   pl.BlockSpec((TMF, D), lambda i, j: (i, 0))],
        out_shape=[SDS((T, D), bf16), SDS((T, NDEV * D_FF_SHARD), bf16), SDS((T, D), f32)],
        scratch_shapes=[pltpu.VMEM((TMF, D), f32)],
        compiler_params=_cp(dimension_semantics=("arbitrary", "arbitrary")),
    )(h, g, w_in, w_out)


def mlp_bwd(h, hm, r, g, dout, dout_b, w_in, w_out, layer, bg=()):
    last = NDEV - 1

    def body(h_ref, hm_ref, r_ref, g_ref, do_ref, dob_ref, wi_ref, wo_ref, dh_ref, dwi_ref, dwo_ref, dg_ref,
             dhm, awi, awo):
        j = pl.program_id(0)
        i = pl.program_id(1)
        rows = pl.ds(pl.multiple_of(i * TM, TM), TM)

        @pl.when(i == 0)
        def _():
            awi[...] = jnp.zeros_like(awi)
            awo[...] = jnp.zeros_like(awo)

        dz = (_dot_nt(dob_ref[...], wo_ref[...]) * (2.0 * r_ref[...].astype(f32))).astype(bf16)
        rb = r_ref[...]
        awo[...] += _dot_tn(rb * rb, dob_ref[...])
        awi[...] += _dot_tn(hm_ref[...], dz)
        part = _dot_nt(dz, wi_ref[...])

        @pl.when(j == 0)
        def _():
            dhm[rows, :] = part

        @pl.when(j > 0)
        def _():
            dhm[rows, :] += part

        @pl.when(i == NT - 1)
        def _():
            dwi_ref[...] = awi[...].astype(bf16)
            dwo_ref[...] = awo[...].astype(bf16)

        @pl.when(j == last)
        def _():
            @pl.when(i == 0)
            def _():
                dg_ref[...] = jnp.zeros_like(dg_ref)
            dx, dgt = _rms_bwd(h_ref[...], g_ref[...], dhm[rows, :])
            dh_ref[...] = do_ref[...] + dx
            dg_ref[...] += _colsum8(dgt)

    late = lambda j, i: (jnp.where(j == last, i, 0), 0)
    return _call(
        bg, body, name=f"mlp_bwd{layer}", grid=(NDEV, NT),
        in_specs=[pl.BlockSpec((TM, D), late),
                  pl.BlockSpec((TM, D), lambda j, i: (i, 0)),
                  pl.BlockSpec((TM, D_FF_SHARD), lambda j, i: (i, j)),
                  pl.BlockSpec((1, D), lambda j, i: (0, 0)),
                  pl.BlockSpec((TM, D), late),
                  pl.BlockSpec((TM, D), lambda j, i: (i, 0)),
                  pl.BlockSpec((None, D, D_FF_SHARD), lambda j, i: (j, 0, 0)),
                  pl.BlockSpec((None, D_FF_SHARD, D), lambda j, i: (j, 0, 0))],
        out_specs=[pl.BlockSpec((TM, D), late),
                   pl.BlockSpec((None, D, D_FF_SHARD), lambda j, i: (j, 0, 0)),
                   pl.BlockSpec((None, D_FF_SHARD, D), lambda j, i: (j, 0, 0)),
                   pl.BlockSpec((8, D), lambda j, i: (0, 0))],
        out_shape=[SDS((T, D), f32), SDS((NDEV, D, D_FF_SHARD), bf16), SDS((NDEV, D_FF_SHARD, D), bf16),
                   SDS((8, D), f32)],
        scratch_shapes=[pltpu.VMEM((T, D), f32), pltpu.VMEM((D, D_FF_SHARD), f32), pltpu.VMEM((D_FF_SHARD, D), f32)],
        compiler_params=_cp(dimension_semantics=("arbitrary", "arbitrary")),
    )(h, hm, r, g, dout, dout_b, w_in, w_out)


def _spread4():
    r = lax.broadcasted_iota(jnp.int32, (256, D), 0)
    c = lax.broadcasted_iota(jnp.int32, (256, D), 1)
    return ((c // 256 == r // HEAD_DIM) & (c % HEAD_DIM == r % HEAD_DIM)).astype(bf16)


def attn_pre(h, g_kv, g_mix, wkv, bkv, spread, wq, bq):
    def body(h_ref, gkv_ref, gm_ref, wkv_ref, bkv_ref, sp_ref, wq_ref, bq_ref, kvn_ref, hn_ref, k_ref, v_ref, q_ref):
        h_ = h_ref[...]
        kvn = _rms(h_, gkv_ref[...])[0].astype(bf16)
        hn = _rms(h_, gm_ref[...])[0].astype(bf16)
        kvn_ref[...] = kvn
        hn_ref[...] = hn
        kv = (_dot(kvn, wkv_ref[...]) + bkv_ref[...]).astype(bf16)
        k_ref[...] = _dot(kv[:, :256], sp_ref[...]).astype(bf16)
        v_ref[...] = _dot(kv[:, 256:], sp_ref[...]).astype(bf16)
        q_ref[...] = (_dot(hn, wq_ref[...]) + bq_ref[...]).astype(bf16)

    return pl.pallas_call(
        body, name="attn_pre", grid=(NT,),
        in_specs=[_tile(), _full((1, D)), _full((1, D)), _full((D, 512)), _full((1, 512)), _full((256, D)),
                  _full((D, D)), _full((1, D))],
        out_specs=[_tile()] * 5,
        out_shape=[SDS((T, D), bf16)] * 5,
        compiler_params=_cp(dimension_semantics=("arbitrary",)),
    )(h, g_kv, g_mix, wkv, bkv, spread, wq, bq)


def _attn_specs():
    cur = pl.BlockSpec((TM, 256), lambda j, n: (n, j))
    prev = pl.BlockSpec((BLK, 256), lambda j, n: (jnp.maximum(n * (TM // BLK) - 1, 0), j))
    return cur, prev


def _head_mask(g):
    lane = lax.broadcasted_iota(jnp.int32, (1, 256), 1)
    return (lane >= g * HEAD_DIM) & (lane < (g + 1) * HEAD_DIM)


def _stack_heads(t):
    return jnp.concatenate([jnp.where(_head_mask(g), t, 0) for g in range(Q_PER_KV)], axis=0)


def _unstack_heads(t):
    out = jnp.where(_head_mask(0), t[0:BLK], 0.0)
    for g in range(1, Q_PER_KV):
        out = out + jnp.where(_head_mask(g), t[g * BLK:(g + 1) * BLK], 0.0)
    return out


def _attn_probs(qs, k2, sinks, first):
    rows = Q_PER_KV * BLK
    s = _dot_nt(qs, k2) * (1.0 / math.sqrt(HEAD_DIM))
    qi = jnp.bitwise_and(lax.broadcasted_iota(jnp.int32, (rows, 2 * BLK), 0), BLK - 1)
    kj = lax.broadcasted_iota(jnp.int32, (rows, 2 * BLK), 1)
    diff = qi + BLK - kj
    valid = (diff >= 0) & (diff < BLK) & (jnp.logical_not(first) | (kj >= BLK))
    s = jnp.where(valid, s, -jnp.inf)
    rb = lax.broadcasted_iota(jnp.int32, (rows, 1), 0)
    sink = jnp.where(rb < BLK, sinks[0], jnp.where(rb < 2 * BLK, sinks[1], jnp.where(rb < 3 * BLK, sinks[2], sinks[3])))
    m = jnp.maximum(jnp.max(s, axis=-1, keepdims=True), sink)
    p = jnp.exp(s - m)
    ps = jnp.exp(sink - m)
    denom = jnp.sum(p, axis=-1, keepdims=True) + ps
    return p / denom, ps / denom


def _window_blocks(b, n, kc_ref, kp_ref, vc_ref, vp_ref):
    if b == 0:
        return (jnp.concatenate([kp_ref[...], kc_ref[0:BLK, :]], axis=0),
                jnp.concatenate([vp_ref[...], vc_ref[0:BLK, :]], axis=0), n == 0)
    rows = pl.ds((b - 1) * BLK, 2 * BLK)
    return kc_ref[rows, :], vc_ref[rows, :], False


def attn_core_fwd(q, k4, v4, sinks, bg=()):
    nb = TM // BLK

    def body(sink_ref, q_ref, kc_ref, kp_ref, vc_ref, vp_ref, o_ref):
        j = pl.program_id(0)
        n = pl.program_id(1)
        sk = [sink_ref[j * Q_PER_KV + g] for g in range(Q_PER_KV)]
        for b in range(nb):
            qb = q_ref[b * BLK:(b + 1) * BLK, :]
            k2, v2, first = _window_blocks(b, n, kc_ref, kp_ref, vc_ref, vp_ref)
            a, _ = _attn_probs(_stack_heads(qb), k2, sk, first)
            o_ref[b * BLK:(b + 1) * BLK, :] = _unstack_heads(_dot(a.astype(bf16), v2)).astype(bf16)

    cur, prev = _attn_specs()
    return _call(
        bg, body, name="attn_core_fwd", grid=(N_KV, NT),
        in_specs=[pl.BlockSpec(memory_space=pltpu.SMEM), cur, cur, prev, cur, prev],
        out_specs=cur, out_shape=SDS((T, D), bf16),
        compiler_params=_cp(dimension_semantics=("arbitrary", "arbitrary")),
    )(sinks, q, k4, k4, v4, v4)


def attn_post(h, o, wo, bo):
    def body(h_ref, o_ref, w_ref, b_ref, out_ref):
        out_ref[...] = h_ref[...] + _dot(o_ref[...], w_ref[...]) + b_ref[...]

    return pl.pallas_call(
        body, name="attn_post", grid=(NT,), in_specs=[_tile(), _tile(), _full((D, D)), _full((1, D))],
        out_specs=_tile(), out_shape=SDS((T, D), f32), compiler_params=_cp(dimension_semantics=("arbitrary",)),
    )(h, o, wo, bo)


def attn_bwd_pre(dh, o, wo, bg=()):
    def body(dh_ref, o_ref, w_ref, do_ref, dw_ref, db_ref, acc):
        i = pl.program_id(0)

        @pl.when(i == 0)
        def _():
            acc[...] = jnp.zeros_like(acc)
            db_ref[...] = jnp.zeros_like(db_ref)

        dh_ = dh_ref[...]
        dhb = dh_.astype(bf16)
        do_ref[...] = _dot_nt(dhb, w_ref[...]).astype(bf16)
        acc[...] += _dot_tn(o_ref[...], dhb)
        db_ref[...] += _colsum8(dh_)

        @pl.when(i == NT - 1)
        def _():
            dw_ref[...] = acc[...].astype(bf16)

    return _call(
        bg, body, name="attn_bwd_pre", grid=(NT,), in_specs=[_tile(), _tile(), _full((D, D))],
        out_specs=[_tile(), _full((D, D)), _full((8, D))],
        out_shape=[SDS((T, D), bf16), SDS((D, D), bf16), SDS((8, D), f32)],
        scratch_shapes=[pltpu.VMEM((D, D), f32)],
        compiler_params=_cp(dimension_semantics=("arbitrary",)),
    )(dh, o, wo)


def attn_core_bwd(q, do, k4, v4, sinks, bg=()):
    nb = TM // BLK

    def body(sink_ref, q_ref, do_ref, kc_ref, kp_ref, vc_ref, vp_ref, dq_ref, dk_ref, dv_ref, ds_ref):
        j = pl.program_id(0)
        n = pl.program_id(1)

        @pl.when(n == 0)
        def _():
            dk_ref[...] = jnp.zeros_like(dk_ref)
            dv_ref[...] = jnp.zeros_like(dv_ref)
            ds_ref[...] = jnp.zeros_like(ds_ref)

        lane8 = lax.broadcasted_iota(jnp.int32, (8, 128), 1)
        row8 = lax.broadcasted_iota(jnp.int32, (8, 128), 0)
        sk = [sink_ref[j * Q_PER_KV + g] for g in range(Q_PER_KV)]
        for b in range(nb):
            qs = _stack_heads(q_ref[b * BLK:(b + 1) * BLK, :])
            dos = _stack_heads(do_ref[b * BLK:(b + 1) * BLK, :])
            k2, v2, first = _window_blocks(b, n, kc_ref, kp_ref, vc_ref, vp_ref)
            a, asink = _attn_probs(qs, k2, sk, first)
            dp = _dot_nt(dos, v2)
            dd = jnp.sum(a * dp, axis=-1, keepdims=True)
            dsc = (a * (dp - dd) * (1.0 / math.sqrt(HEAD_DIM))).astype(bf16)
            t = asink * dd
            for g in range(Q_PER_KV):
                dsink = -jnp.sum(t[g * BLK:(g + 1) * BLK], axis=0, keepdims=True)
                ds_ref[...] += jnp.where((lane8 == g) & (row8 == 0), jnp.broadcast_to(dsink, (8, 128)), 0.0)
            dq_ref[b * BLK:(b + 1) * BLK, :] = _unstack_heads(_dot(dsc, k2))
            dk2 = _dot_tn(dsc, qs)
            dv2 = _dot_tn(a.astype(bf16), dos)
            cur = pl.ds(pl.multiple_of(n * TM + b * BLK, BLK), BLK)
            dk_ref[cur, :] += dk2[BLK:, :]
            dv_ref[cur, :] += dv2[BLK:, :]
            if b == 0:
                @pl.when(n > 0)
                def _():
                    prv = pl.ds(pl.multiple_of(n * TM - BLK, BLK), BLK)
                    dk_ref[prv, :] += dk2[:BLK, :]
                    dv_ref[prv, :] += dv2[:BLK, :]
            else:
                prv = pl.ds(pl.multiple_of(n * TM + (b - 1) * BLK, BLK), BLK)
                dk_ref[prv, :] += dk2[:BLK, :]
                dv_ref[prv, :] += dv2[:BLK, :]

    cur, prev = _attn_specs()
    col = pl.BlockSpec((T, 256), lambda j, n: (0, j))
    return _call(
        bg, body, name="attn_core_bwd", grid=(N_KV, NT),
        in_specs=[pl.BlockSpec(memory_space=pltpu.SMEM), cur, cur, cur, prev, cur, prev],
        out_specs=[cur, col, col, pl.BlockSpec((None, 8, 128), lambda j, n: (j, 0, 0))],
        out_shape=[SDS((T, D), f32), SDS((T, D), f32), SDS((T, D), f32), SDS((N_KV, 8, 128), f32)],
        compiler_params=_cp(dimension_semantics=("arbitrary", "arbitrary")),
    )(sinks, q, do, k4, k4, v4, v4)


def attn_bwd_q(h, dh, dq, hn, g_mix, wq):
    def body(h_ref, dh_ref, dq_ref, hn_ref, gm_ref, wq_ref, out_ref, dwq_ref, dbq_ref, dgm_ref, aq):
        i = pl.program_id(0)

        @pl.when(i == 0)
        def _():
            aq[...] = jnp.zeros_like(aq)
            dbq_ref[...] = jnp.zeros_like(dbq_ref)
            dgm_ref[...] = jnp.zeros_like(dgm_ref)

        dq_ = dq_ref[...]
        dqb = dq_.astype(bf16)
        aq[...] += _dot_tn(hn_ref[...], dqb)
        dbq_ref[...] += _colsum8(dq_)
        dx, dg = _rms_bwd(h_ref[...], gm_ref[...], _dot_nt(dqb, wq_ref[...]))
        out_ref[...] = dh_ref[...] + dx
        dgm_ref[...] += _colsum8(dg)

        @pl.when(i == NT - 1)
        def _():
            dwq_ref[...] = aq[...].astype(bf16)

    vec = _full((8, D))
    mat = _full((D, D))
    return pl.pallas_call(
        body, name="attn_bwd_q", grid=(NT,),
        in_specs=[_tile()] * 4 + [_full((1, D)), mat],
        out_specs=[_tile(), mat, vec, vec],
        out_shape=[SDS((T, D), f32), SDS((D, D), bf16), SDS((8, D), f32), SDS((8, D), f32)],
        scratch_shapes=[pltpu.VMEM((D, D), f32)],
        compiler_params=_cp(dimension_semantics=("arbitrary",)),
    )(h, dh, dq, hn, g_mix, wq)


def attn_bwd_kv(h, dh, dk4, dv4, kvn, g_kv, wkv, spread):
    def body(h_ref, dh_ref, dk_ref, dv_ref, kvn_ref, gkv_ref, wkv_ref, sp_ref, out_ref, outb_ref, dw_ref, db_ref,
             dgkv_ref, acc):
        i = pl.program_id(0)

        @pl.when(i == 0)
        def _():
            for r in (acc, db_ref, dgkv_ref):
                r[...] = jnp.zeros_like(r)

        dkv = jnp.concatenate([_dot_nt(dk_ref[...].astype(bf16), sp_ref[...]),
                               _dot_nt(dv_ref[...].astype(bf16), sp_ref[...])], axis=1)
        dkvb = dkv.astype(bf16)
        acc[...] += _dot_tn(kvn_ref[...], dkvb)
        db_ref[...] += _colsum8(dkv)
        dx, dg = _rms_bwd(h_ref[...], gkv_ref[...], _dot_nt(dkvb, wkv_ref[...]))
        out = dh_ref[...] + dx
        out_ref[...] = out
        outb_ref[...] = out.astype(bf16)
        dgkv_ref[...] += _colsum8(dg)

        @pl.when(i == NT - 1)
        def _():
            dw_ref[...] = acc[...].astype(bf16)

    return pl.pallas_call(
        body, name="attn_bwd_kv", grid=(NT,),
        in_specs=[_tile()] * 5 + [_full((1, D)), _full((D, 512)), _full((256, D))],
        out_specs=[_tile(), _tile(), _full((D, 512)), _full((8, 512)), _full((8, D))],
        out_shape=[SDS((T, D), f32), SDS((T, D), bf16), SDS((D, 512), bf16), SDS((8, 512), f32), SDS((8, D), f32)],
        scratch_shapes=[pltpu.VMEM((D, 512), f32)],
        compiler_params=_cp(dimension_semantics=("arbitrary",)),
    )(h, dh, dk4, dv4, kvn, g_kv, wkv, spread)


def final_loss(h, g, target):
    def body(h_ref, g_ref, t_ref, loss_ref, dh_ref, dhb_ref, dg_ref):
        i = pl.program_id(0)

        @pl.when(i == 0)
        def _():
            loss_ref[...] = jnp.zeros_like(loss_ref)
            dg_ref[...] = jnp.zeros_like(dg_ref)

        h_ = h_ref[...]
        g_ = g_ref[...]
        y, _ = _rms(h_, g_)
        diff = y - t_ref[...]
        per_tok = jnp.mean(diff * diff, axis=-1, keepdims=True)
        tot = 0.5 * jnp.sum(per_tok, axis=0, keepdims=True)
        lane = lax.broadcasted_iota(jnp.int32, (8, 128), 1)
        row = lax.broadcasted_iota(jnp.int32, (8, 128), 0)
        loss_ref[...] += jnp.where((lane == 0) & (row == 0), jnp.broadcast_to(tot, (8, 128)), 0.0)
        dx, dgt = _rms_bwd(h_, g_, diff * (1.0 / D))
        dh_ref[...] = dx
        dhb_ref[...] = dx.astype(bf16)
        dg_ref[...] += _colsum8(dgt)

    return pl.pallas_call(
        body, name="final_loss", grid=(NT,), in_specs=[_tile(), _full((1, D)), _tile()],
        out_specs=[_full((8, 128)), _tile(), _tile(), _full((8, D))],
        out_shape=[SDS((8, 128), f32), SDS((T, D), f32), SDS((T, D), bf16), SDS((8, D), f32)],
        compiler_params=_cp(dimension_semantics=("arbitrary",)),
    )(h, g, target)


def _permute_rows(a, to_chunked, name, also_bf16=False):
    def body(in_ref, out_ref, *rest):
        def two_steps(i, c):
            s = i * 2
            if to_chunked:
                v = jnp.concatenate([in_ref[pl.ds(s, S5_CH, stride=S5_STEPS), :],
                                     in_ref[pl.ds(s + 1, S5_CH, stride=S5_STEPS), :]], axis=0)
                rows = pl.ds(pl.multiple_of(s * S5_CH, 2 * S5_CH), 2 * S5_CH)
                out_ref[rows, :] = v
                if also_bf16:
                    rest[0][rows, :] = v.astype(bf16)
            else:
                for u in range(2):
                    out_ref[pl.ds(s + u, S5_CH, stride=S5_STEPS), :] = in_ref[
                        pl.ds(pl.multiple_of((s + u) * S5_CH, S5_CH), S5_CH), :]
            return c
        lax.fori_loop(0, S5_STEPS // 2, two_steps, 0)

    strip = pl.BlockSpec((T, 128), lambda k: (0, k))
    outs = [SDS((T, D), f32)] + ([SDS((T, D), bf16)] if also_bf16 else [])
    res = pl.pallas_call(
        body, name=name, grid=(D // 128,), in_specs=[strip], out_specs=[strip] * len(outs), out_shape=outs,
        compiler_params=_cp(dimension_semantics=("arbitrary",)),
    )(a)
    return res if also_bf16 else res[0]


def _to_chunked(a, name, also_bf16=False):
    return _permute_rows(a, True, name, also_bf16)


def _from_chunked(a, name):
    return _permute_rows(a, False, name)


def _rep4(w):
    return jnp.broadcast_to(w.reshape(w.shape[0], N_KV, 1, HEAD_DIM), (w.shape[0], N_KV, Q_PER_KV, HEAD_DIM)).reshape(
        w.shape[0], N_KV * Q_PER_KV * HEAD_DIM)


def _fold4(w):
    return w.reshape(w.shape[0], N_KV, Q_PER_KV, HEAD_DIM).sum(axis=2).reshape(w.shape[0], N_KV * HEAD_DIM)


def fwd_bwd(x, target, p, shards, opt, core, chip):
    row = lambda v: v.reshape(1, -1)
    (lam, bm, cm), prep_vjp = jax.vjp(s5_discretize, p["s5_a_re"][0], p["s5_a_im"][0], p["s5_log_dt"][0],
                                      p["s5_b_re"][0], p["s5_b_im"][0], p["s5_c_re"][0], p["s5_c_im"][0])
    bmb, cmb = bm.astype(bf16), cm.astype(bf16)
    lam = jnp.concatenate([lam, lam * jnp.array([1.0, -1.0], f32).reshape(1, 2, 1, 1)], axis=1)
    g_mix0, g_mix1 = row(p["norm_mix"][0]), row(p["norm_mix"][1])
    g_mlp0, g_mlp1 = row(p["norm_mlp"][0]), row(p["norm_mlp"][1])
    g_kv, g_fin = row(p["norm_kv"]), row(p["norm_final"])
    bq, bo = p["b_q"], p["b_o"]
    bkv = row(p["b_kv"])
    spread = _spread4()
    sinks = p["sinks"].reshape(16)

    def reduce_pairs(names, bg):
        return [add_pairs(g, r, core, f"add_pairs_{n}") for n, g, r in zip(names, bg.arrs, bg.result)]

    wglu, gvec = sc_gather([shards["s5_w_glu"], shards["vecs"]], 3, "sc_gather_s5")
    win0, wout0 = sc_gather([shards["w_in0"], shards["w_out0"]], 14, "sc_gather_mlp0")
    wkv, wq, wo = sc_gather([shards["w_kv"], shards["w_q"], shards["w_o"]], 4, "sc_gather_attn")
    win1, wout1 = sc_gather([shards["w_in1"], shards["w_out1"]], 5, "sc_gather_mlp1")
    xp = x
    hn0 = s5_pre(xp, g_mix0)
    ys = s5_core_fwd(hn0, bmb, lam, cmb)
    d_skip = gvec[:, 0, :128].reshape(1, D)
    bglu = gvec[:, 0, 128:].reshape(1, 2 * D)
    y, z, h1 = s5_post(ys, xp, g_mix0, d_skip, wglu, bglu)
    hm0, r0, h2p = mlp_fwd(h1, g_mlp0, win0, wout0, 0)
    wkv, wq, wo = wkv.reshape(D, 512), wq.reshape(D, D), wo.reshape(D, D)
    h2 = h2p
    kvn, hn1, k4, v4, q = attn_pre(h2, g_kv, g_mix1, wkv, bkv, spread, wq, bq)
    o = attn_core_fwd(q, k4, v4, sinks)
    h3 = attn_post(h2, o, wo, bo)
    hm1, r1, h4 = mlp_fwd(h3, g_mlp1, win1, wout1, 1)
    loss, dh4, dh4b, dg_fin = final_loss(h4, g_fin, target)

    def pair_sums(names, grads, cid, before):
        r1 = sc_comm(BgPair(grads), cid, "sc_pair_" + names[0])
        parts = [add_pairs(g, r, core, f"add_pairs_{n}") for n, g, r in zip(names, grads, r1)]
        before, parts = lax.optimization_barrier((before, parts))
        return before, parts

    def across_chips(names, parts, cid):
        return list(zip(parts, sc_comm(BgChips(parts), cid, "sc_chips_" + names[0])))

    dh3, dwin1, dwout1, dg_mlp1 = mlp_bwd(h3, hm1, r1, g_mlp1, dh4, dh4b, win1, wout1, 1)
    do, dwo, dbo = attn_bwd_pre(dh3, o, wo)
    do, parts = pair_sums(["w_in1", "w_out1"], [dwin1, dwout1], 6, do)
    rs_in1, rs_out1 = across_chips(["w_in1", "w_out1"], parts, 7)
    dq, dk4, dv4, dsink = attn_core_bwd(q, do, k4, v4, sinks)
    dh2, dwq, dbq, dg_mix1 = attn_bwd_q(h2, dh3, dq, hn1, g_mix1, wq)
    dh2, dh2b, dwkv, dbkv, dg_kv = attn_bwd_kv(h2, dh2, dk4, dv4, kvn, g_kv, wkv, spread)
    dh2p, dh2pb = dh2, dh2b
    big = {}
    a_in1 = adam_big(*opt["w_mlp_in"], *rs_in1, chip, "adam_w_mlp_in1", layer=1)
    a_out1 = adam_big(*opt["w_mlp_out"], *rs_out1, chip, "adam_w_mlp_out1", layer=1)
    dh2p, a_in1, a_out1 = lax.optimization_barrier((dh2p, a_in1, a_out1))
    names = ["w_kv", "w_q", "w_o"]
    dh2p, parts = pair_sums(names, [dwkv.reshape(NDEV, 128, 512), dwq.reshape(NDEV, 128, D),
                                    dwo.reshape(NDEV, 128, D)], 8, dh2p)
    rs_attn = across_chips(names, parts, 9)
    dh1, dwin0, dwout0, dg_mlp0 = mlp_bwd(h1, hm0, r0, g_mlp0, dh2p, dh2pb, win0, wout0, 0)
    a_attn = [adam_big(*opt[n], *rs, chip, f"adam_{n}") for n, rs in zip(names, rs_attn)]
    dh1, a_attn = lax.optimization_barrier((dh1, a_attn))
    big.update(zip(names, a_attn))
    dy, dwglu, dbglu = s5_post_bwd(dh1, y, z, wglu)
    dy, parts = pair_sums(["w_in0", "w_out0"], [dwin0, dwout0], 10, dy)
    rs_in0, rs_out0 = across_chips(["w_in0", "w_out0"], parts, 11)
    du, dbm, dcmt, dlam = s5_core_bwd(hn0, dy, bmb, lam, cmb)
    du, parts = pair_sums(["s5_w_glu"], [dwglu], 12, du)
    rs_glu, = across_chips(["s5_w_glu"], parts, 13)
    dxp, dg_mix0, dd = s5_pre_bwd(xp, g_mix0, du, dy, d_skip, dh1)
    big["w_mlp_in"] = adam_big(*opt["w_mlp_in"], *rs_in0, chip, "adam_w_mlp_in0", layer=0, prev=a_in1)
    big["w_mlp_out"] = adam_big(*opt["w_mlp_out"], *rs_out0, chip, "adam_w_mlp_out0", layer=0, prev=a_out1)
    big["s5_w_glu"] = adam_big(*opt["s5_w_glu"], *rs_glu, chip, "adam_s5_w_glu")
    grad_x = dxp
    da_re, da_im, dlog_dt, db_re, db_im, dc_re, dc_im = prep_vjp((dlam, dbm, dcmt.transpose(0, 2, 1)))

    def lanes(v_):
        v_ = v_.reshape(1, -1)
        return jnp.pad(v_, ((0, 0), (0, D - v_.shape[1])))

    small = jnp.concatenate([
        dg_mix0[0:1], dg_mix1[0:1], dg_mlp0[0:1], dg_mlp1[0:1], dg_kv[0:1], dg_fin[0:1], dd[0:1], dbq[0:1], dbo[0:1],
        dbglu[0:1].reshape(2, D), lanes(dbkv[0:1]),
        lanes(dsink[:, 0, :Q_PER_KV]), lanes(dlog_dt), lanes(loss[0:1, 0:1]), jnp.zeros((1, D), f32),
        da_re.reshape(4, D), da_im.reshape(4, D),
        db_re.transpose(0, 2, 1).reshape(64, D), db_im.transpose(0, 2, 1).reshape(64, D),
        dc_re.reshape(64, D), dc_im.reshape(64, D)], axis=0)
    small, big["w_mlp_in"], big["w_mlp_out"] = lax.optimization_barrier((small, big["w_mlp_in"], big["w_mlp_out"]))
    return loss, grad_x, small, big


_ANY = pl.BlockSpec(memory_space=pl.ANY)


def _pos():
    return lax.axis_index("x"), lax.axis_index("y"), lax.axis_index("c")


def _other_chips(x, y):
    return [(1 - x, y), (x, 1 - y), (1 - x, 1 - y)]


def all_gather(arrs):
    n = len(arrs)

    def body(*refs):
        ins, outs = refs[:n], refs[n:2 * n]
        send_sems, recv_sems, local_sems = refs[2 * n:]
        x, y, c = _pos()
        me, sib = (x, y, c), (x, y, 1 - c)
        chips = _other_chips(x, y)

        def copy(a, k, block, to, src=None):
            dst = outs[a].at[4 * block[0] + 2 * block[1] + block[2]]
            return pltpu.make_async_remote_copy(
                src_ref=dst if src is None else src, dst_ref=dst, send_sem=send_sems.at[a, k],
                recv_sem=recv_sems.at[a, k], device_id=to, device_id_type=MESH)

        mine = [pltpu.make_async_copy(ins[a], outs[a].at[4 * x + 2 * y + c], local_sems.at[a]) for a in range(n)]
        for cp in mine:
            cp.start()
        first = []
        for a in range(n):
            first.append(copy(a, 0, me, sib, src=ins[a]))
            first += [copy(a, 1 + j, me, (*chip, c), src=ins[a]) for j, chip in enumerate(chips)]
        for cp in first:
            cp.start()
        passed = []
        for j, chip in enumerate(chips):
            for a in range(n):
                copy(a, 1 + j, (*chip, c), me).wait_recv()
                cp = copy(a, 4 + j, (*chip, c), sib)
                cp.start()
                passed.append(cp)
        for a in range(n):
            copy(a, 0, sib, me).wait_recv()
            for j, chip in enumerate(chips):
                copy(a, 4 + j, (*chip, 1 - c), me).wait_recv()
        for cp in first + passed:
            cp.wait_send()
        for cp in mine:
            cp.wait()

    return pl.pallas_call(
        body, name="all_gather", in_specs=[_ANY] * n, out_specs=[_ANY] * n,
        out_shape=[SDS((NDEV,) + a.shape, a.dtype) for a in arrs],
        scratch_shapes=[pltpu.SemaphoreType.DMA((n, 7)), pltpu.SemaphoreType.DMA((n, 7)),
                        pltpu.SemaphoreType.DMA((n,))],
    )(*arrs)


def rs_pair(grads):
    n = len(grads)

    def body(*refs):
        ins, outs = refs[:n], refs[n:2 * n]
        send_sems, recv_sems = refs[2 * n:]
        x, y, c = _pos()
        cps = []
        for a in range(n):
            for k in range(4):
                cps.append(pltpu.make_async_remote_copy(
                    src_ref=ins[a].at[2 * k + 1 - c], dst_ref=outs[a].at[k], send_sem=send_sems.at[a, k],
                    recv_sem=recv_sems.at[a, k], device_id=(x, y, 1 - c), device_id_type=MESH))
        for cp in cps:
            cp.start()
        for cp in cps:
            cp.wait_recv()
        for cp in cps:
            cp.wait_send()

    return pl.pallas_call(
        body, name="rs_pair", in_specs=[_ANY] * n, out_specs=[_ANY] * n,
        out_shape=[SDS((4,) + g.shape[1:], g.dtype) for g in grads],
        scratch_shapes=[pltpu.SemaphoreType.DMA((n, 4)), pltpu.SemaphoreType.DMA((n, 4))],
    )(*grads)


def rs_chips(parts):
    n = len(parts)

    def body(*refs):
        ins, outs = refs[:n], refs[n:2 * n]
        send_sems, recv_sems = refs[2 * n:]
        x, y, c = _pos()
        cps = []
        for a in range(n):
            for r, (px, py) in enumerate(_other_chips(x, y)):
                cps.append(pltpu.make_async_remote_copy(
                    src_ref=ins[a].at[2 * px + py], dst_ref=outs[a].at[r], send_sem=send_sems.at[a, r],
                    recv_sem=recv_sems.at[a, r], device_id=(px, py, c), device_id_type=MESH))
        for cp in cps:
            cp.start()
        for cp in cps:
            cp.wait_recv()
        for cp in cps:
            cp.wait_send()

    return pl.pallas_call(
        body, name="rs_chips", in_specs=[_ANY] * n, out_specs=[_ANY] * n,
        out_shape=[SDS((3,) + g.shape[1:], g.dtype) for g in parts],
        scratch_shapes=[pltpu.SemaphoreType.DMA((n, 3)), pltpu.SemaphoreType.DMA((n, 3))],
    )(*parts)


def _row_tile(r, c):
    return min(r, max(8, (512 * 1024) // c))


def add_pairs(g, r1, core, name):
    _, R, C = g.shape
    tr = _row_tile(R, C)

    def body(core_ref, g_ref, r_ref, o_ref):
        o_ref[...] = (g_ref[...].astype(f32) + r_ref[...].astype(f32)).astype(bf16)

    return pl.pallas_call(
        body, name=name, out_shape=SDS((4, R, C), bf16),
        grid_spec=pltpu.PrefetchScalarGridSpec(
            num_scalar_prefetch=1, grid=(4, R // tr),
            in_specs=[pl.BlockSpec((None, tr, C), lambda k, i, core: (2 * k + core[0], i, 0)),
                      pl.BlockSpec((None, tr, C), lambda k, i, core: (k, i, 0))],
            out_specs=pl.BlockSpec((None, tr, C), lambda k, i, core: (k, i, 0))),
        compiler_params=_cp(dimension_semantics=("arbitrary", "arbitrary")),
    )(core, g, r1)


def _adamw(w, g, m, v):
    m = ADAM_B1 * m + (1.0 - ADAM_B1) * g
    v = ADAM_B2 * v + (1.0 - ADAM_B2) * (g * g)
    m_hat = m / (1.0 - ADAM_B1 ** ADAM_STEP)
    v_hat = v / (1.0 - ADAM_B2 ** ADAM_STEP)
    delta = -ADAM_LR * (m_hat / (jnp.sqrt(v_hat) + ADAM_EPS) + ADAM_WD * w)
    return delta, m, v


def adam_big(w, m, v, part, r2, chip, name, layer=0, prev=None):
    L, R, C = w.shape
    tr = _row_tile(R, C)

    def body(chip_ref, w_ref, m_ref, v_ref, p_ref, r_ref, *rest):
        g_out, d_out, m_out, v_out = rest[-4:]
        g = p_ref[...].astype(f32) + r_ref[0].astype(f32) + r_ref[1].astype(f32) + r_ref[2].astype(f32)
        d, m_, v_ = _adamw(w_ref[...], g, m_ref[...], v_ref[...])
        g_out[...] = g
        d_out[...] = d
        m_out[...] = m_
        v_out[...] = v_

    blk = pl.BlockSpec((None, tr, C), lambda i, chip: (layer, i, 0))
    extra = [] if prev is None else list(prev)
    return pl.pallas_call(
        body, name=name, out_shape=[SDS((L, R, C), f32)] * 4,
        grid_spec=pltpu.PrefetchScalarGridSpec(
            num_scalar_prefetch=1, grid=(R // tr,),
            in_specs=[blk, blk, blk,
                      pl.BlockSpec((None, tr, C), lambda i, chip: (chip[0], i, 0)),
                      pl.BlockSpec((3, tr, C), lambda i, chip: (0, i, 0))] + [_ANY] * len(extra),
            out_specs=[blk] * 4),
        input_output_aliases={6 + k: k for k in range(len(extra))},
        compiler_params=_cp(dimension_semantics=("arbitrary",)),
    )(chip, w, m, v, part, r2, *extra)


def allreduce_small(buf, chips=None):
    shp = buf.shape
    half = (shp[0] // 16) * 8
    parts = (pl.ds(0, half), pl.ds(half, shp[0] - half))
    n_c = 0 if chips is None else len(chips.arrs)

    def body(in_ref, *refs):
        c_in, out_ref, c_out = refs[:n_c], refs[n_c], refs[n_c + 1:2 * n_c + 1]
        acc1, acc2, r0, r1, r2, send_sems, recv_sems = refs[2 * n_c + 1:2 * n_c + 8]
        c_sems = refs[2 * n_c + 8:]
        if chips is not None:
            chips.start(c_in, c_out, c_sems)
        x, y, c = _pos()
        across = [(1 - x, y, c), (x, 1 - y, c)]

        def exchange(src, rcv, dst, copies):
            cps = [pltpu.make_async_remote_copy(
                src_ref=src.at[rows], dst_ref=rcv.at[rows], send_sem=send_sems.at[k], recv_sem=recv_sems.at[k],
                device_id=peer, device_id_type=MESH) for k, rows, peer in copies]
            for cp in cps:
                cp.start()
            for cp in cps:
                cp.wait()
            dst[...] = src[...] + rcv[...]

        exchange(in_ref, r0, acc1, [(0, pl.ds(0, shp[0]), (x, y, 1 - c))])
        exchange(acc1, r1, acc2, [(1, parts[0], across[0]), (2, parts[1], across[1])])
        exchange(acc2, r2, out_ref, [(3, parts[0], across[1]), (4, parts[1], across[0])])
        if chips is not None:
            chips.finish(c_in, c_out, c_sems)

    vm = pl.BlockSpec(memory_space=pltpu.VMEM)
    res = pl.pallas_call(
        body, name="allreduce_small", in_specs=[vm] + [_ANY] * n_c, out_specs=[vm] + [_ANY] * n_c,
        out_shape=[SDS(shp, f32)] + ([] if chips is None else chips.out_shape),
        scratch_shapes=[pltpu.VMEM(shp, f32)] * 5 + [pltpu.SemaphoreType.DMA((5,)), pltpu.SemaphoreType.DMA((5,))]
        + ([] if chips is None else chips.scratch),
    )(buf, *([] if chips is None else chips.arrs))
    if chips is not None:
        chips.result = list(res[1:])
    return res[0]


SMALL_ROWS = {'norm_mix': (0, 2, D), 'norm_mlp': (2, 2, D), 'norm_kv': (4, 1, D), 'norm_final': (5, 1, D),
              's5_d': (6, 1, D), 'b_q': (7, 1, D), 'b_o': (8, 1, D), 's5_b_glu': (9, 2, D), 'b_kv': (11, 1, 512),
              'sinks': (12, 1, 16), 's5_log_dt': (13, 1, 64), 's5_a_re': (16, 4, D), 's5_a_im': (20, 4, D),
              's5_b_re': (24, 64, D), 's5_b_im': (88, 64, D), 's5_c_re': (152, 64, D), 's5_c_im': (216, 64, D)}
LOSS_ROW = 14
ROW_PARAMS = ['norm_mix', 'norm_mlp', 'norm_kv', 'norm_final', 'b_q', 'b_o', 'b_kv', 'sinks', 's5_log_dt']
SHARD_PARAMS = ['s5_d', 's5_b_glu']
S5_PARAMS = ['s5_a_re', 's5_a_im', 's5_b_re', 's5_b_im', 's5_c_re', 's5_c_im']


def adam_small(dev, gsum, s5_grads, w, m, v):
    names = ROW_PARAMS + SHARD_PARAMS + S5_PARAMS
    n_g = len(ROW_PARAMS) + len(SHARD_PARAMS)

    def body(dev_ref, gs_ref, *refs):
        pos = [0]

        def take(k):
            r = refs[pos[0]:pos[0] + k]
            pos[0] += k
            return r

        g5 = take(len(S5_PARAMS))
        wr, mr, vr = take(len(names)), take(len(names)), take(len(names))
        g_out = take(n_g)
        d_out, m_out, v_out = take(len(names)), take(len(names)), take(len(names))
        dv = dev_ref[0]
        for i, n in enumerate(names):
            if n in S5_PARAMS:
                g = g5[S5_PARAMS.index(n)][...]
            elif n in SHARD_PARAMS:
                r0, _, _ = SMALL_ROWS[n]
                ln = wr[i].shape[1]
                g = jnp.zeros((1, ln), f32)
                for k in range(NDEV):
                    off = k * ln
                    piece = gs_ref[r0 + off // D:r0 + off // D + 1, off % D:off % D + ln]
                    g = g + jnp.where(dv == k, piece, 0.0)
                g_out[i][...] = g
            else:
                r0, nr, nl = SMALL_ROWS[n]
                g = gs_ref[r0:r0 + nr, 0:nl]
                g_out[i][...] = g
            d, m_, v_ = _adamw(wr[i][...], g, mr[i][...], vr[i][...])
            d_out[i][...] = d
            m_out[i][...] = m_
            v_out[i][...] = v_

    vm = pl.BlockSpec(memory_space=pltpu.VMEM)
    ins = [s5_grads[n] for n in S5_PARAMS] + [d[n] for d in (w, m, v) for n in names]
    shapes = [SDS(w[n].shape, f32) for n in names]
    res = pl.pallas_call(
        body, name="adam_small", in_specs=[pl.BlockSpec(memory_space=pltpu.SMEM)] + [vm] * (1 + len(ins)),
        out_specs=[vm] * (n_g + 3 * len(names)), out_shape=shapes[:n_g] + shapes * 3,
        compiler_params=_cp(),
    )(dev, gsum, *ins)
    g_o = dict(zip(names[:n_g], res[:n_g]))
    rest = res[n_g:]
    k = len(names)
    return g_o, dict(zip(names, rest[:k])), dict(zip(names, rest[k:2 * k])), dict(zip(names, rest[2 * k:]))


WEIGHTS = ['norm_mix', 'norm_mlp', 'norm_kv', 'norm_final', 's5_a_re', 's5_a_im', 's5_log_dt', 's5_b_re', 's5_b_im',
           's5_c_re', 's5_c_im', 's5_d', 's5_w_glu', 's5_b_glu', 'w_kv', 'b_kv', 'w_q', 'b_q', 'sinks', 'w_o', 'b_o',
           'w_mlp_in', 'w_mlp_out']
BIG = ['s5_w_glu', 'w_kv', 'w_q', 'w_o', 'w_mlp_in', 'w_mlp_out']
BIG_2D = {'s5_w_glu': (D, 256), 'w_kv': (128, 512), 'w_q': (128, D), 'w_o': (128, D), 'w_mlp_in': (2 * D, 512),
          'w_mlp_out': (2 * 512, D)}
SHARDED_SMALL = {'s5_d': D, 's5_b_glu': 2 * D}
SMALL = [n for n in WEIGHTS if n not in BIG]
SMALL_SIZE = {'norm_mix': 2 * D, 'norm_mlp': 2 * D, 'norm_kv': D, 'norm_final': D, 's5_a_re': 4096, 's5_a_im': 4096,
              's5_log_dt': 64, 's5_b_re': 65536, 's5_b_im': 65536, 's5_c_re': 65536, 's5_c_im': 65536, 's5_d': D,
              's5_b_glu': 2 * D, 'b_kv': 512, 'b_q': D, 'sinks': 16, 'b_o': D}


def _pack(vals):
    parts = []
    for n in SMALL:
        v = vals[n].reshape(-1).astype(f32)
        parts.append(jnp.pad(v, (0, (-v.shape[0]) % 128)))
    flat = jnp.concatenate(parts)
    flat = jnp.pad(flat, (0, (-flat.shape[0]) % 1024))
    return flat.reshape(-1, 128)


def _unpack(buf):
    flat = buf.reshape(-1)
    out, off = {}, 0
    for n in SMALL:
        sz = SMALL_SIZE[n]
        out[n] = flat[off:off + sz]
        off += sz + (-sz) % 128
    return out


def kernel(x, norm_mix, norm_mlp, norm_kv, norm_final, s5_a_re, s5_a_im, s5_log_dt, s5_b_re, s5_b_im, s5_c_re, s5_c_im, s5_d, s5_w_glu, s5_b_glu, w_kv, b_kv, w_q, b_q, sinks, w_o, b_o, w_mlp_in, w_mlp_out, loss_target, m_norm_mix, m_norm_mlp, m_norm_kv, m_norm_final, m_s5_a_re, m_s5_a_im, m_s5_log_dt, m_s5_b_re, m_s5_b_im, m_s5_c_re, m_s5_c_im, m_s5_d, m_s5_w_glu, m_s5_b_glu, m_w_kv, m_b_kv, m_w_q, m_b_q, m_sinks, m_w_o, m_b_o, m_w_mlp_in, m_w_mlp_out, v_norm_mix, v_norm_mlp, v_norm_kv, v_norm_final, v_s5_a_re, v_s5_a_im, v_s5_log_dt, v_s5_b_re, v_s5_b_im, v_s5_c_re, v_s5_c_im, v_s5_d, v_s5_w_glu, v_s5_b_glu, v_w_kv, v_b_kv, v_w_q, v_b_q, v_sinks, v_w_o, v_b_o, v_w_mlp_in, v_w_mlp_out):
    w = dict(norm_mix=norm_mix, norm_mlp=norm_mlp, norm_kv=norm_kv, norm_final=norm_final, s5_a_re=s5_a_re,
             s5_a_im=s5_a_im, s5_log_dt=s5_log_dt, s5_b_re=s5_b_re, s5_b_im=s5_b_im, s5_c_re=s5_c_re, s5_c_im=s5_c_im,
             s5_d=s5_d, s5_w_glu=s5_w_glu, s5_b_glu=s5_b_glu, w_kv=w_kv, b_kv=b_kv, w_q=w_q, b_q=b_q, sinks=sinks,
             w_o=w_o, b_o=b_o, w_mlp_in=w_mlp_in, w_mlp_out=w_mlp_out)
    m = dict(norm_mix=m_norm_mix, norm_mlp=m_norm_mlp, norm_kv=m_norm_kv, norm_final=m_norm_final, s5_a_re=m_s5_a_re,
             s5_a_im=m_s5_a_im, s5_log_dt=m_s5_log_dt, s5_b_re=m_s5_b_re, s5_b_im=m_s5_b_im, s5_c_re=m_s5_c_re,
             s5_c_im=m_s5_c_im, s5_d=m_s5_d, s5_w_glu=m_s5_w_glu, s5_b_glu=m_s5_b_glu, w_kv=m_w_kv, b_kv=m_b_kv,
             w_q=m_w_q, b_q=m_b_q, sinks=m_sinks, w_o=m_w_o, b_o=m_b_o, w_mlp_in=m_w_mlp_in, w_mlp_out=m_w_mlp_out)
    v = dict(norm_mix=v_norm_mix, norm_mlp=v_norm_mlp, norm_kv=v_norm_kv, norm_final=v_norm_final, s5_a_re=v_s5_a_re,
             s5_a_im=v_s5_a_im, s5_log_dt=v_s5_log_dt, s5_b_re=v_s5_b_re, s5_b_im=v_s5_b_im, s5_c_re=v_s5_c_re,
             s5_c_im=v_s5_c_im, s5_d=v_s5_d, s5_w_glu=v_s5_w_glu, s5_b_glu=v_s5_b_glu, w_kv=v_w_kv, b_kv=v_b_kv,
             w_q=v_w_q, b_q=v_b_q, sinks=v_sinks, w_o=v_w_o, b_o=v_b_o, w_mlp_in=v_w_mlp_in, w_mlp_out=v_w_mlp_out)
    xi, yi, ci = _pos()
    dev = 4 * xi + 2 * yi + ci
    core = ci.reshape(1).astype(jnp.int32)
    chip = (2 * xi + yi).reshape(1).astype(jnp.int32)

    shards = {
        "s5_w_glu": s5_w_glu[0].astype(bf16), "w_kv": w_kv.astype(bf16), "w_q": w_q[0].astype(bf16),
        "w_o": w_o[0].astype(bf16), "w_in0": w_mlp_in[0].astype(bf16), "w_in1": w_mlp_in[1].astype(bf16),
        "w_out0": w_mlp_out[0].astype(bf16), "w_out1": w_mlp_out[1].astype(bf16),
        "vecs": jnp.broadcast_to(jnp.concatenate([s5_d, s5_b_glu], axis=1), (8, 384)),
    }
    as3d = lambda a, n: a if a.ndim == 3 and a.shape[0] == 2 else a.reshape((1,) + BIG_2D[n])
    opt = {n: (as3d(w[n], n), as3d(m[n], n), as3d(v[n], n)) for n in BIG}
    _, grad_x, grads, big = fwd_bwd(x[0], loss_target[0], {n: w[n] for n in SMALL}, shards, opt, core, chip)

    gsum = allreduce_small(grads)

    out_g, out_d, out_m, out_v = {}, {}, {}, {}
    for n in BIG:
        out_g[n], out_d[n], out_m[n], out_v[n] = [r.reshape(w[n].shape) for r in big[n]]

    loss = gsum[LOSS_ROW, 0]
    swapped = ("s5_b_re", "s5_b_im")
    swap = lambda a: a.transpose(0, 1, 3, 2)

    def kernel_side(d):
        d = {n: (d[n].reshape(1, -1) if d[n].ndim == 1 else d[n]) for n in SMALL}
        d.update({n: swap(d[n]) for n in swapped})
        return d

    s5_g = {}
    for n in S5_PARAMS:
        r0, nr, _ = SMALL_ROWS[n]
        s5_g[n] = gsum[r0:r0 + nr].reshape((1, 64, 16, 64) if n in swapped else w[n].shape)
        out_g[n] = s5_g[n]
    g_s, d_s, m_s, v_s = adam_small(dev.reshape(1).astype(jnp.int32), gsum, s5_g, kernel_side(w), kernel_side(m),
                                    kernel_side(v))
    for src, dst in ((g_s, out_g), (d_s, out_d), (m_s, out_m), (v_s, out_v)):
        dst.update(src)
    for dst in (out_g, out_d, out_m, out_v):
        for n in SMALL:
            dst[n] = (swap(dst[n]) if n in swapped else dst[n]).reshape(w[n].shape)

    return (loss, grad_x[None], *[out_g[n] for n in WEIGHTS], *[out_d[n] for n in WEIGHTS],
            *[out_m[n] for n in WEIGHTS], *[out_v[n] for n in WEIGHTS])
```

```python
import functools
import math

import jax
import jax.numpy as jnp
from jax import lax
from jax.experimental import pallas as pl
from jax.experimental.pallas import tpu as pltpu
from jax.experimental.pallas import tpu_sc as plsc

f32 = jnp.float32
bf16 = jnp.bfloat16
SDS = jax.ShapeDtypeStruct

T = 2048
D = 1024
NDEV = 8
NORM_EPS = 1e-5
S5_G, S5_C, S5_P = 64, 16, 64
S5_SUB = 8
S5_CH = 8
S5_STEPS = T // S5_CH
DT_MIN_LAMBDA = -1e-4
HEAD_DIM = 64
N_KV = 4
Q_PER_KV = 4
BLK = 128
D_FF_SHARD = 512
ADAM_LR, ADAM_B1, ADAM_B2, ADAM_EPS, ADAM_WD, ADAM_STEP = 0.001, 0.9, 0.999, 1e-08, 0.01, 10
VMEM_LIMIT = 56 * 1024 * 1024
MESH = pl.DeviceIdType.MESH


def _cp(**kw):
    return pltpu.CompilerParams(vmem_limit_bytes=VMEM_LIMIT, **kw)


def _dot(a, b):
    return jnp.dot(a, b, preferred_element_type=f32)


def _dot_nt(a, b):
    return lax.dot_general(a, b, (((1,), (1,)), ((), ())), preferred_element_type=f32)


def _dot_tn(a, b):
    return lax.dot_general(a, b, (((0,), (0,)), ((), ())), preferred_element_type=f32)


def _rms(x, g):
    r = lax.rsqrt(jnp.mean(x * x, axis=-1, keepdims=True) + NORM_EPS)
    return x * r * g, r


def _rms_bwd(x, g, dy):
    r = lax.rsqrt(jnp.mean(x * x, axis=-1, keepdims=True) + NORM_EPS)
    u = dy * g
    dx = r * u - (r * r * r) * x * jnp.mean(u * x, axis=-1, keepdims=True)
    return dx, dy * x * r


def _colsum8(v):
    s = jnp.sum(v, axis=0, keepdims=True)
    row = lax.broadcasted_iota(jnp.int32, (8, v.shape[1]), 0)
    return jnp.where(row == 0, jnp.broadcast_to(s, (8, v.shape[1])), 0.0)


def _full(shape):
    nd = len(shape)
    return pl.BlockSpec(shape, lambda *_: (0,) * nd, pipeline_mode=pl.Buffered(1))


_ANY = pl.BlockSpec(memory_space=pl.ANY)


def _pos():
    return lax.axis_index("x"), lax.axis_index("y"), lax.axis_index("c")


def _other_chips(x, y):
    return [(1 - x, y), (x, 1 - y), (1 - x, 1 - y)]


class BgGather:
    SIB, XN, YN, FWD_Y, FWD_X, SIB_X, SIB_Y, SIB_D = range(8)

    def __init__(self, arrs, mids=(0.5, 0.75)):
        n = len(arrs)
        self.arrs = list(arrs)
        self.out_shape = [SDS((NDEV,) + a.shape, a.dtype) for a in arrs]
        self.scratch = [pltpu.SemaphoreType.DMA((n, 8)), pltpu.SemaphoreType.DMA((n, 8)),
                        pltpu.SemaphoreType.DMA((n,))]
        self.mids = mids
        self.result = None

    @staticmethod
    def peers(x, y, c):
        return [(x, y, 1 - c), (1 - x, y, c), (x, 1 - y, c)]

    def mid_steps(self, nsteps):
        at = lambda f: min(nsteps - 1, max(0, int(f * nsteps) - 1))
        return [(at(self.mids[0]), self.mid), (max(at(self.mids[0]), at(self.mids[1])), self.mid2)]

    def _halves(self, a):
        rows = self.arrs[a].shape[0]
        cut = (rows // 32) * 16 if rows >= 32 else rows
        return (0, cut), (cut, rows - cut)

    def _copy(self, ins, outs, sems, a, k, block, to, own=False, part=None):
        slot = 4 * block[0] + 2 * block[1] + block[2]
        rows = pl.ds(0, self.arrs[a].shape[0]) if part is None else pl.ds(*self._halves(a)[part])
        dst = outs[a].at[slot, rows]
        return pltpu.make_async_remote_copy(
            src_ref=ins[a].at[rows] if own else dst, dst_ref=dst, send_sem=sems[0].at[a, k],
            recv_sem=sems[1].at[a, k], device_id=to, device_id_type=MESH)

    def _mine(self, ins, outs, sems):
        x, y, c = _pos()
        return [pltpu.make_async_copy(ins[a], outs[a].at[4 * x + 2 * y + c], sems[2].at[a])
                for a in range(len(self.arrs))]

    def _split(self, a):
        return self._halves(a)[1][1] > 0

    def _sends(self, ins, outs, sems, phase):
        x, y, c = _pos()
        me, sib, xn, yn, dg = (x, y, c), (x, y, 1 - c), (1 - x, y, c), (x, 1 - y, c), (1 - x, 1 - y, c)
        cps = []
        for a in range(len(self.arrs)):
            cp = lambda k, block, to, **kw: self._copy(ins, outs, sems, a, k, block, to, **kw)
            if phase == 0:
                cps += [cp(self.SIB, me, sib, own=True), cp(self.XN, me, xn, own=True), cp(self.YN, me, yn, own=True)]
            elif phase == 1:
                cps.append(cp(self.FWD_Y, xn, yn, part=0))
                if self._split(a):
                    cps.append(cp(self.FWD_X, yn, xn, part=1))
                cps += [cp(self.SIB_X, xn, sib), cp(self.SIB_Y, yn, sib)]
            else:
                cps.append(cp(self.SIB_D, dg, sib))
        return cps

    def _arrivals(self, ins, outs, sems, phase):
        x, y, c = _pos()
        me, xn, yn, dg = (x, y, c), (1 - x, y, c), (x, 1 - y, c), (1 - x, 1 - y, c)
        cps = []
        for a in range(len(self.arrs)):
            cp = lambda k, block, **kw: self._copy(ins, outs, sems, a, k, block, me, **kw)
            if phase == 1:
                cps += [cp(self.XN, xn), cp(self.YN, yn)]
            elif phase == 2:
                cps.append(cp(self.FWD_Y, dg, part=0))
                if self._split(a):
                    cps.append(cp(self.FWD_X, dg, part=1))
            else:
                cps += [cp(self.SIB, (x, y, 1 - c)), cp(self.SIB_X, (1 - x, y, 1 - c)),
                        cp(self.SIB_Y, (x, 1 - y, 1 - c)), cp(self.SIB_D, (1 - x, 1 - y, 1 - c))]
        return cps

    def start(self, ins, outs, sems):
        for cp in self._mine(ins, outs, sems) + self._sends(ins, outs, sems, 0):
            cp.start()

    def mid(self, ins, outs, sems):
        for cp in self._arrivals(ins, outs, sems, 1):
            cp.wait_recv()
        for cp in self._sends(ins, outs, sems, 1):
            cp.start()

    def mid2(self, ins, outs, sems):
        for cp in self._arrivals(ins, outs, sems, 2):
            cp.wait_recv()
        for cp in self._sends(ins, outs, sems, 2):
            cp.start()

    def finish(self, ins, outs, sems):
        for cp in self._arrivals(ins, outs, sems, 3):
            cp.wait_recv()
        for ph in range(3):
            for cp in self._sends(ins, outs, sems, ph):
                cp.wait_send()
        for cp in self._mine(ins, outs, sems):
            cp.wait()


def sc_comm(g, collective_id, name):
    srcs = [jax.new_ref(a, memory_space=pltpu.MemorySpace.HBM) for a in g.arrs]
    dsts = [jax.empty_ref(s, memory_space=pltpu.MemorySpace.HBM) for s in g.out_shape]

    @pl.kernel(mesh=plsc.ScalarSubcoreMesh(axis_name="sequencer", num_cores=1), name=name,
               scratch_types=tuple(g.scratch), compiler_params=pltpu.CompilerParams(collective_id=collective_id))
    def launch(*sems):
        peers = g.peers(*_pos())
        barrier = pltpu.get_barrier_semaphore()
        for peer in peers:
            pl.semaphore_signal(barrier, inc=1, device_id=peer, device_id_type=MESH)
        pl.semaphore_wait(barrier, len(peers))
        g.start(srcs, dsts, sems)
        for _, phase in g.mid_steps(1):
            phase(srcs, dsts, sems)
        g.finish(srcs, dsts, sems)

    launch()
    return [d[...] for d in dsts]


def sc_gather(arrs, collective_id, name):
    return sc_comm(BgGather(arrs), collective_id, name)


class BgPair:
    def __init__(self, arrs):
        n = len(arrs)
        self.arrs = list(arrs)
        self.out_shape = [SDS((4,) + a.shape[1:], a.dtype) for a in arrs]
        self.scratch = [pltpu.SemaphoreType.DMA((n, 4)), pltpu.SemaphoreType.DMA((n, 4))]
        self.result = None

    @staticmethod
    def peers(x, y, c):
        return [(x, y, 1 - c)]

    def mid_steps(self, nsteps):
        return []

    def _copies(self, ins, outs, sems):
        x, y, c = _pos()
        return [pltpu.make_async_remote_copy(
            src_ref=ins[a].at[2 * k + 1 - c], dst_ref=outs[a].at[k], send_sem=sems[0].at[a, k],
            recv_sem=sems[1].at[a, k], device_id=(x, y, 1 - c), device_id_type=MESH)
            for a in range(len(self.arrs)) for k in range(4)]

    def start(self, ins, outs, sems):
        for cp in self._copies(ins, outs, sems):
            cp.start()

    def finish(self, ins, outs, sems):
        cps = self._copies(ins, outs, sems)
        for cp in cps:
            cp.wait_recv()
        for cp in cps:
            cp.wait_send()


class BgChips(BgPair):
    def __init__(self, arrs):
        n = len(arrs)
        self.arrs = list(arrs)
        self.out_shape = [SDS((3,) + a.shape[1:], a.dtype) for a in arrs]
        self.scratch = [pltpu.SemaphoreType.DMA((n, 3)), pltpu.SemaphoreType.DMA((n, 3))]
        self.result = None

    @staticmethod
    def peers(x, y, c):
        return [(px, py, c) for px, py in _other_chips(x, y)]

    def _copies(self, ins, outs, sems):
        x, y, c = _pos()
        return [pltpu.make_async_remote_copy(
            src_ref=ins[a].at[2 * px + py], dst_ref=outs[a].at[r], send_sem=sems[0].at[a, r],
            recv_sem=sems[1].at[a, r], device_id=(px, py, c), device_id_type=MESH)
            for a in range(len(self.arrs)) for r, (px, py) in enumerate(_other_chips(x, y))]


class AdamRider:
    def __init__(self, w, m, v, part, r2, layer=0, prev=None):
        self.arrs = [w, m, v, part, r2] + list(prev or [])
        self.n_prev = len(prev or [])
        self.layer = layer
        self.out_shape = [SDS(w.shape, f32)] * 4
        self.scratch = []
        self.aliases = {5 + k: k for k in range(self.n_prev)}
        self.result = None

    def _tile(self, grid):
        assert len(grid) == 1
        _, R, C = self.arrs[0].shape
        return R // grid[0], C

    def in_specs(self, grid):
        tr, C = self._tile(grid)
        layer = self.layer
        blk = pl.BlockSpec((None, tr, C), lambda b: (layer, b, 0))
        mine = pl.BlockSpec((None, tr, C), lambda b: (2 * lax.axis_index("x") + lax.axis_index("y"), b, 0))
        return [blk, blk, blk, mine, pl.BlockSpec((3, tr, C), lambda b: (0, b, 0))] + [_ANY] * self.n_prev

    def out_specs(self, grid):
        tr, C = self._tile(grid)
        layer = self.layer
        return [pl.BlockSpec((None, tr, C), lambda b: (layer, b, 0))] * 4

    def mid_steps(self, nsteps):
        return []

    def start(self, ins, outs, sems):
        pass

    finish = start

    def step(self, ins, outs, sems):
        w_ref, m_ref, v_ref, p_ref, r_ref = ins[:5]
        g = p_ref[...].astype(f32) + r_ref[0].astype(f32) + r_ref[1].astype(f32) + r_ref[2].astype(f32)
        d, m_, v_ = _adamw(w_ref[...], g, m_ref[...], v_ref[...])
        for ref, val in zip(outs, (g, d, m_, v_)):
            ref[...] = val


def _call(bgs, body, *, name, grid, in_specs, out_specs, out_shape, scratch_shapes=(), compiler_params=None):
    single = not isinstance(out_shape, (list, tuple))
    out_specs_l = [out_specs] if single else list(out_specs)
    out_shape_l = [out_shape] if single else list(out_shape)
    bgs = [b for b in (bgs or []) if b is not None]
    n_in, n_out, n_sc = len(in_specs), len(out_shape_l), len(scratch_shapes)
    nsteps = math.prod(grid)
    b_in_specs = [b.in_specs(grid) if hasattr(b, "in_specs") else [_ANY] * len(b.arrs) for b in bgs]
    b_out_specs = [b.out_specs(grid) if hasattr(b, "out_specs") else [_ANY] * len(b.out_shape) for b in bgs]
    aliases, i_off, o_off = {}, n_in, n_out
    for b in bgs:
        aliases.update({i_off + i: o_off + o for i, o in getattr(b, "aliases", {}).items()})
        i_off, o_off = i_off + len(b.arrs), o_off + len(b.out_shape)

    def full(*refs):
        pos = [0]

        def take(k):
            r = refs[pos[0]:pos[0] + k]
            pos[0] += k
            return r

        ins = take(n_in)
        b_ins = [take(len(b.arrs)) for b in bgs]
        outs = take(n_out)
        b_outs = [take(len(b.out_shape)) for b in bgs]
        sc = take(n_sc)
        b_sc = [take(len(b.scratch)) for b in bgs]
        if bgs:
            step = pl.program_id(0)
            for d in range(1, len(grid)):
                step = step * grid[d] + pl.program_id(d)

            @pl.when(step == 0)
            def _():
                for b, i_, o_, s_ in zip(bgs, b_ins, b_outs, b_sc):
                    b.start(i_, o_, s_)

        body(*ins, *outs, *sc)
        if bgs:
            for b, i_, o_, s_ in zip(bgs, b_ins, b_outs, b_sc):
                if hasattr(b, "step"):
                    b.step(i_, o_, s_)
                for at, fn in b.mid_steps(nsteps):
                    @pl.when(step == at)
                    def _():
                        fn(i_, o_, s_)

            @pl.when(step == nsteps - 1)
            def _():
                for b, i_, o_, s_ in zip(bgs, b_ins, b_outs, b_sc):
                    b.finish(i_, o_, s_)

    def run(*args):
        res = pl.pallas_call(
            full, name=name, grid=grid,
            in_specs=list(in_specs) + [s for l in b_in_specs for s in l],
            out_specs=out_specs_l + [s for l in b_out_specs for s in l],
            out_shape=out_shape_l + [s for b in bgs for s in b.out_shape],
            scratch_shapes=list(scratch_shapes) + [s for b in bgs for s in b.scratch],
            input_output_aliases=aliases,
            compiler_params=compiler_params,
        )(*args, *[a for b in bgs for a in b.arrs])
        rest = list(res[n_out:])
        for b in bgs:
            b.result, rest = rest[:len(b.out_shape)], rest[len(b.out_shape):]
        return res[0] if single else list(res[:n_out])

    return run


def s5_discretize(a_re, a_im, log_dt, b_re, b_im, c_re, c_im):
    lam_r = jnp.minimum(a_re, DT_MIN_LAMBDA)
    lam_i = a_im
    dt = jnp.exp(log_dt)[:, None]
    e = jnp.exp(lam_r * dt)
    lbr = e * jnp.cos(lam_i * dt)
    lbi = e * jnp.sin(lam_i * dt)
    den = lam_r * lam_r + lam_i * lam_i
    cf_r = ((lbr - 1.0) * lam_r + lbi * lam_i) / den
    cf_i = (lbi * lam_r - (lbr - 1.0) * lam_i) / den
    bb_r = cf_r[:, :, None] * b_re - cf_i[:, :, None] * b_im
    bb_i = cf_r[:, :, None] * b_im + cf_i[:, :, None] * b_re
    eye = jnp.eye(8, dtype=f32)

    def blk_b(m):
        return jnp.einsum('bgpc,gh->bgchp', m.reshape(8, 8, S5_P, S5_C), eye).reshape(8, 128, 512)

    def blk_c(m):
        return jnp.einsum('bgcp,gh->bgphc', m.reshape(8, 8, S5_C, S5_P), eye).reshape(8, 512, 128)

    bm = jnp.concatenate([blk_b(bb_r), blk_b(bb_i)], axis=-1)
    cm = jnp.concatenate([blk_c(c_re), -blk_c(c_im)], axis=1)
    lam = jnp.stack([lbr.reshape(8, 512), lbi.reshape(8, 512)], axis=1)
    lam = jnp.broadcast_to(lam[:, :, None, :], (8, 2, 8, 512))
    return lam, bm, cm


def _cmul(ar, ai, br, bi):
    return ar * br - ai * bi, ar * bi + ai * br


def _shift_rows(v, k, up):
    row = lax.broadcasted_iota(jnp.int32, v.shape, 0)
    if up:
        return jnp.where(row < 8 - k, pltpu.roll(v, 8 - k, 0), 0.0)
    return jnp.where(row >= k, pltpu.roll(v, k, 0), 0.0)


def _chunk_scan(S, lr, li, reverse, aux=None):
    z = jnp.zeros((8, 512), f32)
    U = 4

    def idx(i):
        return (S5_STEPS - 1 - i) if reverse else i

    def rows_of(s):
        return pl.ds(s * 8, 8) if isinstance(s, int) else pl.ds(pl.multiple_of(s * 8, 8), 8)

    def rec(xr, xi, row):
        br = S[row, 0:512]
        bi = S[row, 512:1024]
        return lr * xr - li * xi + br, lr * xi + li * xr + bi

    def step1(i, c):
        for u in range(U):
            c = rec(c[0], c[1], rows_of(idx(i * U + u)))
        return c

    er, ei = lax.fori_loop(0, S5_STEPS // U, step1, (z, z))
    ar, ai = lr, li
    for _ in range(8):
        ar, ai = _cmul(ar, ai, ar, ai)
    cr, ci = _shift_rows(er, 1, reverse), _shift_rows(ei, 1, reverse)
    for k in (1, 2, 4):
        sr, si = _shift_rows(cr, k, reverse), _shift_rows(ci, k, reverse)
        pr, pi_ = _cmul(ar, ai, sr, si)
        cr, ci = cr + pr, ci + pi_
        ar, ai = _cmul(ar, ai, ar, ai)

    if aux is None:
        def step2(i, c):
            for u in range(U):
                row = rows_of(idx(i * U + u))
                c = rec(c[0], c[1], row)
                S[row, 0:512] = c[0]
                S[row, 512:1024] = c[1]
            return c

        lax.fori_loop(0, S5_STEPS // U, step2, (cr, ci))
        return None

    def one(s, c):
        gr0, gi0, dr, di = c
        row = rows_of(s)
        gr, gi = rec(gr0, gi0, row)
        S[row, 0:512] = gr
        S[row, 512:1024] = gi
        prow = rows_of(s - 1)
        xr = aux[prow, 0:512]
        xi = aux[prow, 512:1024]
        return gr, gi, dr + gr * xr + gi * xi, di + gi * xr - gr * xi

    def step2(i, c):
        for u in range(U):
            c = one(S5_STEPS - 1 - (i * U + u), c)
        return c

    c = lax.fori_loop(0, S5_STEPS // U - 1, step2, (cr, ci, z, z))
    for s in range(U - 1, 0, -1):
        c = one(s, c)
    gr, gi, dr, di = c
    row0 = pl.ds(0, 8)
    gr, gi = rec(gr, gi, row0)
    S[row0, 0:512] = gr
    S[row0, 512:1024] = gi
    last = pl.ds((S5_STEPS - 1) * 8, 8)
    xr = _shift_rows(aux[last, 0:512], 1, False)
    xi = _shift_rows(aux[last, 512:1024], 1, False)
    dr = dr + gr * xr + gi * xi
    di = di + gi * xr - gr * xi
    return dr, di


_ROWS = 256


def _row_loop(fn):
    def body(r, c):
        fn(pl.ds(pl.multiple_of(r * _ROWS, _ROWS), _ROWS))
        return c
    lax.fori_loop(0, T // _ROWS, body, 0)


def s5_core_fwd(hn, bm, lam, cm, bg=()):
    nt = T // _ROWS

    def body(u_ref, b_ref, lam_ref, c_ref, ys_ref, S):
        lr, li = lam_ref[0], lam_ref[1]
        z = jnp.zeros((8, 512), f32)
        tile = lambda k: pl.ds(k * _ROWS, _ROWS)
        c = (z, z)
        for k in range(nt):
            S[tile(k), :] = _dot(_rows_in(u_ref, k).astype(bf16), b_ref[...])
            if k >= 1:
                c = _scan_tile(S, lr, li, k - 1, c, False, False)
        c = _scan_tile(S, lr, li, nt - 1, c, False, False)
        c = _chunk_starts(c[0], c[1], lr, li, False)
        for k in range(nt):
            c = _scan_tile(S, lr, li, k, c, False, True)
            if k >= 1:
                _rows_out(ys_ref, k - 1, _dot(S[tile(k - 1), :].astype(bf16), c_ref[...]))
        _rows_out(ys_ref, nt - 1, _dot(S[tile(nt - 1), :].astype(bf16), c_ref[...]))

    return _call(
        bg, body, name="s5_core_fwd", grid=(S5_SUB,),
        in_specs=[pl.BlockSpec((T, 128), lambda b: (0, b)),
                  pl.BlockSpec((None, 128, 1024), lambda b: (b, 0, 0)),
                  pl.BlockSpec((None, 4, 8, 512), lambda b: (b, 0, 0, 0)),
                  pl.BlockSpec((None, 1024, 128), lambda b: (b, 0, 0))],
        out_specs=[pl.BlockSpec((T, 128), lambda b: (0, b)), pl.BlockSpec((T, 1024), lambda b: (0, b))],
        out_shape=[SDS((T, D), f32), SDS((T, S5_SUB * 1024), f32)],
        compiler_params=_cp(dimension_semantics=("arbitrary",)),
    )(hn, bm, lam, cm)


_SEG = _ROWS // S5_CH


def _rows_in(ref, k):
    return jnp.concatenate([ref[pl.ds(s, S5_CH, stride=S5_STEPS), :] for s in range(k * _SEG, (k + 1) * _SEG)], axis=0)


def _rows_out(ref, k, val):
    for j, s in enumerate(range(k * _SEG, (k + 1) * _SEG)):
        ref[pl.ds(s, S5_CH, stride=S5_STEPS), :] = val[j * S5_CH:(j + 1) * S5_CH, :]


def _scan_tile(S, lr, li, k, carry, reverse, store, aux=None):
    steps = range(k * _SEG, (k + 1) * _SEG)
    for s in (reversed(steps) if reverse else steps):
        row = pl.ds(s * 8, 8)
        xr, xi = carry[0], carry[1]
        nr = lr * xr - li * xi + S[row, 0:512]
        ni = lr * xi + li * xr + S[row, 512:1024]
        if store:
            S[row, 0:512] = nr
            S[row, 512:1024] = ni
        if aux is not None and s >= 1:
            prow = pl.ds((s - 1) * 8, 8)
            pr, pi_ = aux[prow, 0:512], aux[prow, 512:1024]
            carry = (nr, ni, carry[2] + nr * pr + ni * pi_, carry[3] + ni * pr - nr * pi_)
        elif aux is not None:
            carry = (nr, ni, carry[2], carry[3])
        else:
            carry = (nr, ni)
    return carry


def _chunk_starts(er, ei, lr, li, reverse):
    ar, ai = lr, li
    for _ in range(8):
        ar, ai = _cmul(ar, ai, ar, ai)
    cr, ci = _shift_rows(er, 1, reverse), _shift_rows(ei, 1, reverse)
    for k in (1, 2, 4):
        sr, si = _shift_rows(cr, k, reverse), _shift_rows(ci, k, reverse)
        pr, pi_ = _cmul(ar, ai, sr, si)
        cr, ci = cr + pr, ci + pi_
        ar, ai = _cmul(ar, ai, ar, ai)
    return cr, ci


def s5_core_bwd(hn, dy, xs, bm, lam, cm, bg=()):
    nt = T // _ROWS

    def body(u_ref, dy_ref, S1, b_ref, lam_ref, c_ref, du_ref, db_ref, dct_ref, dlam_ref, S2):
        lcr, lci = lam_ref[2], lam_ref[3]
        z = jnp.zeros((8, 512), f32)
        tile = lambda k: pl.ds(k * _ROWS, _ROWS)

        def dx(k):
            dyb = _rows_in(dy_ref, k).astype(bf16)
            S2[tile(k), :] = _dot_nt(dyb, c_ref[...])
            dct_ref[...] += _dot_tn(dyb, S1[tile(k), :].astype(bf16))

        dct_ref[...] = jnp.zeros_like(dct_ref)
        dx(nt - 1)
        c = (z, z)
        for k in range(nt - 1, -1, -1):
            if k >= 1:
                dx(k - 1)
            c = _scan_tile(S2, lcr, lci, k, c, True, False)

        def dbu(k):
            gb = S2[tile(k), :].astype(bf16)
            db_ref[...] += _dot_tn(_rows_in(u_ref, k).astype(bf16), gb)
            _rows_out(du_ref, k, _dot_nt(gb, b_ref[...]))

        c = _chunk_starts(c[0], c[1], lcr, lci, True) + (z, z)
        db_ref[...] = jnp.zeros_like(db_ref)
        for k in range(nt - 1, -1, -1):
            c = _scan_tile(S2, lcr, lci, k, c, True, True, aux=S1)
            if k + 1 < nt:
                dbu(k + 1)
        dbu(0)
        gr, gi, dr, di = c
        last = pl.ds((S5_STEPS - 1) * 8, 8)
        xr = _shift_rows(S1[last, 0:512], 1, False)
        xi = _shift_rows(S1[last, 512:1024], 1, False)
        dlam_ref[0] = dr + gr * xr + gi * xi
        dlam_ref[1] = di + gi * xr - gr * xi

    return _call(
        bg, body, name="s5_core_bwd", grid=(S5_SUB,),
        in_specs=[pl.BlockSpec((T, 128), lambda b: (0, b)),
                  pl.BlockSpec((T, 128), lambda b: (0, b)),
                  pl.BlockSpec((T, 1024), lambda b: (0, b)),
                  pl.BlockSpec((None, 128, 1024), lambda b: (b, 0, 0)),
                  pl.BlockSpec((None, 4, 8, 512), lambda b: (b, 0, 0, 0)),
                  pl.BlockSpec((None, 1024, 128), lambda b: (b, 0, 0))],
        out_specs=[pl.BlockSpec((T, 128), lambda b: (0, b)),
                   pl.BlockSpec((None, 128, 1024), lambda b: (b, 0, 0)),
                   pl.BlockSpec((None, 128, 1024), lambda b: (b, 0, 0)),
                   pl.BlockSpec((None, 2, 8, 512), lambda b: (b, 0, 0, 0))],
        out_shape=[SDS((T, D), f32), SDS((8, 128, 1024), f32), SDS((8, 128, 1024), f32), SDS((8, 2, 8, 512), f32)],
        scratch_shapes=[pltpu.VMEM((T, 1024), f32)],
        compiler_params=_cp(dimension_semantics=("arbitrary",)),
    )(hn, dy, xs, bm, lam, cm)


TM = 512
NT = T // TM


def _tile(n=D):
    return pl.BlockSpec((TM, n), lambda i: (i, 0))


def s5_pre(xp, g):
    def body(x_ref, g_ref, hn_ref):
        hn_ref[...] = _rms(x_ref[...], g_ref[...])[0]

    return pl.pallas_call(
        body, name="s5_pre", grid=(NT,), in_specs=[_tile(), _full((1, D))], out_specs=_tile(),
        out_shape=SDS((T, D), f32), compiler_params=_cp(dimension_semantics=("arbitrary",)),
    )(xp, g)


def _gelu_grad(y):
    c = math.sqrt(2.0 / math.pi)
    t = jnp.tanh(c * (y + 0.044715 * y * y * y))
    return 0.5 * (1.0 + t) + 0.5 * y * (1.0 - t * t) * c * (1.0 + 3.0 * 0.044715 * y * y)


def s5_post(ys, xp, g, d, wglu, bglu, bg=()):
    def body(ys_ref, x_ref, g_ref, d_ref, w_ref, b_ref, y_ref, z_ref, h_ref):
        x = x_ref[...]
        hn, _ = _rms(x, g_ref[...])
        y = ys_ref[...] + d_ref[...] * hn
        y_ref[...] = y
        yg = jax.nn.gelu(y).astype(bf16)
        for j in range(4):
            cv = slice(j * 256, (j + 1) * 256)
            cg = slice(1024 + j * 256, 1024 + (j + 1) * 256)
            val = _dot(yg, w_ref[j]) + b_ref[:, cv]
            gate = _dot(yg, w_ref[j + 4]) + b_ref[:, cg]
            z_ref[:, cv] = val
            z_ref[:, cg] = gate
            h_ref[:, cv] = x[:, cv] + val * jax.nn.sigmoid(gate)

    return _call(
        bg, body, name="s5_post", grid=(NT,),
        in_specs=[_tile(), _tile(), _full((1, D)), _full((1, D)), _full((8, D, 256)), _full((1, 2 * D))],
        out_specs=[_tile(), _tile(2 * D), _tile()],
        out_shape=[SDS((T, D), f32), SDS((T, 2 * D), f32), SDS((T, D), f32)],
        compiler_params=_cp(dimension_semantics=("arbitrary",)),
    )(ys, xp, g, d, wglu, bglu)


def s5_post_bwd(dh, y, z, wglu, bg=()):
    def body(dh_ref, y_ref, z_ref, w_ref, dy_ref, dw_ref, db_ref, acc):
        i = pl.program_id(0)

        @pl.when(i == 0)
        def _():
            acc[...] = jnp.zeros_like(acc)
            db_ref[...] = jnp.zeros_like(db_ref)

        dh_ = dh_ref[...]
        y = y_ref[...]
        yg = jax.nn.gelu(y).astype(bf16)
        dyg = jnp.zeros((TM, D), f32)
        for j in range(4):
            cv = slice(j * 256, (j + 1) * 256)
            cg = slice(1024 + j * 256, 1024 + (j + 1) * 256)
            val = z_ref[:, cv]
            sg = jax.nn.sigmoid(z_ref[:, cg])
            dval = dh_[:, cv] * sg
            dgate = dh_[:, cv] * val * sg * (1.0 - sg)
            db_ref[:, cv] += _colsum8(dval)
            db_ref[:, cg] += _colsum8(dgate)
            dvb = dval.astype(bf16)
            dgb = dgate.astype(bf16)
            acc[j] += _dot_tn(yg, dvb)
            acc[j + 4] += _dot_tn(yg, dgb)
            dyg = dyg + _dot_nt(dvb, w_ref[j]) + _dot_nt(dgb, w_ref[j + 4])
        dy_ref[...] = dyg * _gelu_grad(y)

        @pl.when(i == NT - 1)
        def _():
            dw_ref[...] = acc[...].astype(bf16)

    return _call(
        bg, body, name="s5_post_bwd", grid=(NT,),
        in_specs=[_tile(), _tile(), _tile(2 * D), _full((8, D, 256))],
        out_specs=[_tile(), _full((8, D, 256)), _full((8, 2 * D))],
        out_shape=[SDS((T, D), f32), SDS((8, D, 256), bf16), SDS((8, 2 * D), f32)],
        scratch_shapes=[pltpu.VMEM((8, D, 256), f32)],
        compiler_params=_cp(dimension_semantics=("arbitrary",)),
    )(dh, y, z, wglu)


def s5_pre_bwd(xp, g, du, dy, d, dh, bg=()):
    def body(x_ref, g_ref, du_ref, dy_ref, d_ref, dh_ref, dx_ref, dg_ref, dd_ref):
        i = pl.program_id(0)

        @pl.when(i == 0)
        def _():
            dg_ref[...] = jnp.zeros_like(dg_ref)
            dd_ref[...] = jnp.zeros_like(dd_ref)

        x = x_ref[...]
        g = g_ref[...]
        dy = dy_ref[...]
        hn, _ = _rms(x, g)
        dhn = du_ref[...] + d_ref[...] * dy
        dx, dgt = _rms_bwd(x, g, dhn)
        dx_ref[...] = dh_ref[...] + dx
        dg_ref[...] += _colsum8(dgt)
        dd_ref[...] += _colsum8(dy * hn)

    return _call(
        bg, body, name="s5_pre_bwd", grid=(NT,),
        in_specs=[_tile(), _full((1, D)), _tile(), _tile(), _full((1, D)), _tile()],
        out_specs=[_tile(), _full((8, D)), _full((8, D))],
        out_shape=[SDS((T, D), f32), SDS((8, D), f32), SDS((8, D), f32)],
        compiler_params=_cp(dimension_semantics=("arbitrary",)),
    )(xp, g, du, dy, d, dh)


TMF = 1024


def mlp_fwd(h, g, w_in, w_out, layer, bg=()):
    def body(h_ref, g_ref, wi_ref, wo_ref, hm_ref, r_ref, out_ref, acc):
        j = pl.program_id(1)

        @pl.when(j == 0)
        def _():
            hm, _ = _rms(h_ref[...], g_ref[...])
            hm_ref[...] = hm.astype(bf16)
            acc[...] = jnp.zeros_like(acc)

        a = jnp.maximum(_dot(hm_ref[...], wi_ref[...]), 0.0)
        r_ref[...] = a.astype(bf16)
        acc[...] += _dot((a * a).astype(bf16), wo_ref[...])

        @pl.when(j == NDEV - 1)
        def _():
            out_ref[...] = h_ref[...] + acc[...]

    return _call(
        bg, body, name=f"mlp_fwd{layer}", grid=(T // TMF, NDEV),
        in_specs=[pl.BlockSpec((TMF, D), lambda i, j: (i, 0)),
                  pl.BlockSpec((1, D), lambda i, j: (0, 0)),
                  pl.BlockSpec((None, D, D_FF_SHARD), lambda i, j: (j, 0, 0)),
                  pl.BlockSpec((None, D_FF_SHARD, D), lambda i, j: (j, 0, 0))],
        out_specs=[pl.BlockSpec((TMF, D), lambda i, j: (i, 0)), pl.BlockSpec((TMF, D_FF_SHARD), lambda i, j: (i, j)),
                   pl.BlockSpec((TMF, D), lambda i, j: (i, 0))],
        out_shape=[SDS((T, D), bf16), SDS((T, NDEV * D_FF_SHARD), bf16), SDS((T, D), f32)],
        scratch_shapes=[pltpu.VMEM((TMF, D), f32)],
        compiler_params=_cp(dimension_semantics=("arbitrary", "arbitrary")),
    )(h, g, w_in, w_out)


def mlp_bwd(h, hm, r, g, dout, dout_b, w_in, w_out, layer, bg=()):
    last = NDEV - 1

    def body(h_ref, hm_ref, r_ref, g_ref, do_ref, dob_ref, wi_ref, wo_ref, dh_ref, dwi_ref, dwo_ref, dg_ref,
             dhm, awi, awo):
        j = pl.program_id(0)
        i = pl.program_id(1)
        rows = pl.ds(pl.multiple_of(i * TM, TM), TM)

        @pl.when(i == 0)
        def _():
            awi[...] = jnp.zeros_like(awi)
            awo[...] = jnp.zeros_like(awo)

        dz = (_dot_nt(dob_ref[...], wo_ref[...]) * (2.0 * r_ref[...].astype(f32))).astype(bf16)
        rb = r_ref[...]
        awo[...] += _dot_tn(rb * rb, dob_ref[...])
        awi[...] += _dot_tn(hm_ref[...], dz)
        part = _dot_nt(dz, wi_ref[...])

        @pl.when(j == 0)
        def _():
            dhm[rows, :] = part

        @pl.when(j > 0)
        def _():
            dhm[rows, :] += part

        @pl.when(i == NT - 1)
        def _():
            dwi_ref[...] = awi[...].astype(bf16)
            dwo_ref[...] = awo[...].astype(bf16)

        @pl.when(j == last)
        def _():
            @pl.when(i == 0)
            def _():
                dg_ref[...] = jnp.zeros_like(dg_ref)
            dx, dgt = _rms_bwd(h_ref[...], g_ref[...], dhm[rows, :])
            dh_ref[...] = do_ref[...] + dx
            dg_ref[...] += _colsum8(dgt)

    late = lambda j, i: (jnp.where(j == last, i, 0), 0)
    return _call(
        bg, body, name=f"mlp_bwd{layer}", grid=(NDEV, NT),
        in_specs=[pl.BlockSpec((TM, D), late),
                  pl.BlockSpec((TM, D), lambda j, i: (i, 0)),
                  pl.BlockSpec((TM, D_FF_SHARD), lambda j, i: (i, j)),
                  pl.BlockSpec((1, D), lambda j, i: (0, 0)),
                  pl.BlockSpec((TM, D), late),
                  pl.BlockSpec((TM, D), lambda j, i: (i, 0)),
                  pl.BlockSpec((None, D, D_FF_SHARD), lambda j, i: (j, 0, 0)),
                  pl.BlockSpec((None, D_FF_SHARD, D), lambda j, i: (j, 0, 0))],
        out_specs=[pl.BlockSpec((TM, D), late),
                   pl.BlockSpec((None, D, D_FF_SHARD), lambda j, i: (j, 0, 0)),
                   pl.BlockSpec((None, D_FF_SHARD, D), lambda j, i: (j, 0, 0)),
                   pl.BlockSpec((8, D), lambda j, i: (0, 0))],
        out_shape=[SDS((T, D), f32), SDS((NDEV, D, D_FF_SHARD), bf16), SDS((NDEV, D_FF_SHARD, D), bf16),
                   SDS((8, D), f32)],
        scratch_shapes=[pltpu.VMEM((T, D), f32), pltpu.VMEM((D, D_FF_SHARD), f32), pltpu.VMEM((D_FF_SHARD, D), f32)],
        compiler_params=_cp(dimension_semantics=("arbitrary", "arbitrary")),
    )(h, hm, r, g, dout, dout_b, w_in, w_out)


def _spread4():
    r = lax.broadcasted_iota(jnp.int32, (256, D), 0)
    c = lax.broadcasted_iota(jnp.int32, (256, D), 1)
    return ((c // 256 == r // HEAD_DIM) & (c % HEAD_DIM == r % HEAD_DIM)).astype(bf16)


def attn_pre(h, g_kv, g_mix, wkv, bkv, spread, wq, bq):
    def body(h_ref, gkv_ref, gm_ref, wkv_ref, bkv_ref, sp_ref, wq_ref, bq_ref, kvn_ref, hn_ref, k_ref, v_ref, q_ref):
        h_ = h_ref[...]
        kvn = _rms(h_, gkv_ref[...])[0].astype(bf16)
        hn = _rms(h_, gm_ref[...])[0].astype(bf16)
        kvn_ref[...] = kvn
        hn_ref[...] = hn
        kv = (_dot(kvn, wkv_ref[...]) + bkv_ref[...]).astype(bf16)
        k_ref[...] = _dot(kv[:, :256], sp_ref[...]).astype(bf16)
        v_ref[...] = _dot(kv[:, 256:], sp_ref[...]).astype(bf16)
        q_ref[...] = (_dot(hn, wq_ref[...]) + bq_ref[...]).astype(bf16)

    return pl.pallas_call(
        body, name="attn_pre", grid=(NT,),
        in_specs=[_tile(), _full((1, D)), _full((1, D)), _full((D, 512)), _full((1, 512)), _full((256, D)),
                  _full((D, D)), _full((1, D))],
        out_specs=[_tile()] * 5,
        out_shape=[SDS((T, D), bf16)] * 5,
        compiler_params=_cp(dimension_semantics=("arbitrary",)),
    )(h, g_kv, g_mix, wkv, bkv, spread, wq, bq)


def _attn_specs():
    cur = pl.BlockSpec((TM, 256), lambda j, n: (n, j))
    prev = pl.BlockSpec((BLK, 256), lambda j, n: (jnp.maximum(n * (TM // BLK) - 1, 0), j))
    return cur, prev


def _head_mask(g):
    lane = lax.broadcasted_iota(jnp.int32, (1, 256), 1)
    return (lane >= g * HEAD_DIM) & (lane < (g + 1) * HEAD_DIM)


def _stack_heads(t):
    return jnp.concatenate([jnp.where(_head_mask(g), t, 0) for g in range(Q_PER_KV)], axis=0)


def _unstack_heads(t):
    out = jnp.where(_head_mask(0), t[0:BLK], 0.0)
    for g in range(1, Q_PER_KV):
        out = out + jnp.where(_head_mask(g), t[g * BLK:(g + 1) * BLK], 0.0)
    return out


def _attn_probs(qs, k2, sinks, first):
    rows = Q_PER_KV * BLK
    s = _dot_nt(qs, k2) * (1.0 / math.sqrt(HEAD_DIM))
    qi = jnp.bitwise_and(lax.broadcasted_iota(jnp.int32, (rows, 2 * BLK), 0), BLK - 1)
    kj = lax.broadcasted_iota(jnp.int32, (rows, 2 * BLK), 1)
    diff = qi + BLK - kj
    valid = (diff >= 0) & (diff < BLK) & (jnp.logical_not(first) | (kj >= BLK))
    s = jnp.where(valid, s, -jnp.inf)
    rb = lax.broadcasted_iota(jnp.int32, (rows, 1), 0)
    sink = jnp.where(rb < BLK, sinks[0], jnp.where(rb < 2 * BLK, sinks[1], jnp.where(rb < 3 * BLK, sinks[2], sinks[3])))
    m = jnp.maximum(jnp.max(s, axis=-1, keepdims=True), sink)
    p = jnp.exp(s - m)
    ps = jnp.exp(sink - m)
    denom = jnp.sum(p, axis=-1, keepdims=True) + ps
    return p / denom, ps / denom


def _window_blocks(b, n, kc_ref, kp_ref, vc_ref, vp_ref):
    if b == 0:
        return (jnp.concatenate([kp_ref[...], kc_ref[0:BLK, :]], axis=0),
                jnp.concatenate([vp_ref[...], vc_ref[0:BLK, :]], axis=0), n == 0)
    rows = pl.ds((b - 1) * BLK, 2 * BLK)
    return kc_ref[rows, :], vc_ref[rows, :], False


def attn_core_fwd(q, k4, v4, sinks, bg=()):
    nb = TM // BLK

    def body(sink_ref, q_ref, kc_ref, kp_ref, vc_ref, vp_ref, o_ref):
        j = pl.program_id(0)
        n = pl.program_id(1)
        sk = [sink_ref[j * Q_PER_KV + g] for g in range(Q_PER_KV)]
        for b in range(nb):
            qb = q_ref[b * BLK:(b + 1) * BLK, :]
            k2, v2, first = _window_blocks(b, n, kc_ref, kp_ref, vc_ref, vp_ref)
            a, _ = _attn_probs(_stack_heads(qb), k2, sk, first)
            o_ref[b * BLK:(b + 1) * BLK, :] = _unstack_heads(_dot(a.astype(bf16), v2)).astype(bf16)

    cur, prev = _attn_specs()
    return _call(
        bg, body, name="attn_core_fwd", grid=(N_KV, NT),
        in_specs=[pl.BlockSpec(memory_space=pltpu.SMEM), cur, cur, prev, cur, prev],
        out_specs=cur, out_shape=SDS((T, D), bf16),
        compiler_params=_cp(dimension_semantics=("arbitrary", "arbitrary")),
    )(sinks, q, k4, k4, v4, v4)


def attn_post(h, o, wo, bo):
    def body(h_ref, o_ref, w_ref, b_ref, out_ref):
        out_ref[...] = h_ref[...] + _dot(o_ref[...], w_ref[...]) + b_ref[...]

    return pl.pallas_call(
        body, name="attn_post", grid=(NT,), in_specs=[_tile(), _tile(), _full((D, D)), _full((1, D))],
        out_specs=_tile(), out_shape=SDS((T, D), f32), compiler_params=_cp(dimension_semantics=("arbitrary",)),
    )(h, o, wo, bo)


def attn_bwd_pre(dh, o, wo, bg=()):
    def body(dh_ref, o_ref, w_ref, do_ref, dw_ref, db_ref, acc):
        i = pl.program_id(0)

        @pl.when(i == 0)
        def _():
            acc[...] = jnp.zeros_like(acc)
            db_ref[...] = jnp.zeros_like(db_ref)

        dh_ = dh_ref[...]
        dhb = dh_.astype(bf16)
        do_ref[...] = _dot_nt(dhb, w_ref[...]).astype(bf16)
        acc[...] += _dot_tn(o_ref[...], dhb)
        db_ref[...] += _colsum8(dh_)

        @pl.when(i == NT - 1)
        def _():
            dw_ref[...] = acc[...].astype(bf16)

    return _call(
        bg, body, name="attn_bwd_pre", grid=(NT,), in_specs=[_tile(), _tile(), _full((D, D))],
        out_specs=[_tile(), _full((D, D)), _full((8, D))],
        out_shape=[SDS((T, D), bf16), SDS((D, D), bf16), SDS((8, D), f32)],
        scratch_shapes=[pltpu.VMEM((D, D), f32)],
        compiler_params=_cp(dimension_semantics=("arbitrary",)),
    )(dh, o, wo)


def attn_core_bwd(q, do, k4, v4, sinks, bg=()):
    nb = TM // BLK

    def body(sink_ref, q_ref, do_ref, kc_ref, kp_ref, vc_ref, vp_ref, dq_ref, dk_ref, dv_ref, ds_ref):
        j = pl.program_id(0)
        n = pl.program_id(1)

        @pl.when(n == 0)
        def _():
            dk_ref[...] = jnp.zeros_like(dk_ref)
            dv_ref[...] = jnp.zeros_like(dv_ref)
            ds_ref[...] = jnp.zeros_like(ds_ref)

        lane8 = lax.broadcasted_iota(jnp.int32, (8, 128), 1)
        row8 = lax.broadcasted_iota(jnp.int32, (8, 128), 0)
        sk = [sink_ref[j * Q_PER_KV + g] for g in range(Q_PER_KV)]
        for b in range(nb):
            qs = _stack_heads(q_ref[b * BLK:(b + 1) * BLK, :])
            dos = _stack_heads(do_ref[b * BLK:(b + 1) * BLK, :])
            k2, v2, first = _window_blocks(b, n, kc_ref, kp_ref, vc_ref, vp_ref)
            a, asink = _attn_probs(qs, k2, sk, first)
            dp = _dot_nt(dos, v2)
            dd = jnp.sum(a * dp, axis=-1, keepdims=True)
            dsc = (a * (dp - dd) * (1.0 / math.sqrt(HEAD_DIM))).astype(bf16)
            t = asink * dd
            for g in range(Q_PER_KV):
                dsink = -jnp.sum(t[g * BLK:(g + 1) * BLK], axis=0, keepdims=True)
                ds_ref[...] += jnp.where((lane8 == g) & (row8 == 0), jnp.broadcast_to(dsink, (8, 128)), 0.0)
            dq_ref[b * BLK:(b + 1) * BLK, :] = _unstack_heads(_dot(dsc, k2))
            dk2 = _dot_tn(dsc, qs)
            dv2 = _dot_tn(a.astype(bf16), dos)
            cur = pl.ds(pl.multiple_of(n * TM + b * BLK, BLK), BLK)
            dk_ref[cur, :] += dk2[BLK:, :]
            dv_ref[cur, :] += dv2[BLK:, :]
            if b == 0:
                @pl.when(n > 0)
                def _():
                    prv = pl.ds(pl.multiple_of(n * TM - BLK, BLK), BLK)
                    dk_ref[prv, :] += dk2[:BLK, :]
                    dv_ref[prv, :] += dv2[:BLK, :]
            else:
                prv = pl.ds(pl.multiple_of(n * TM + (b - 1) * BLK, BLK), BLK)
                dk_ref[prv, :] += dk2[:BLK, :]
                dv_ref[prv, :] += dv2[:BLK, :]

    cur, prev = _attn_specs()
    col = pl.BlockSpec((T, 256), lambda j, n: (0, j))
    return _call(
        bg, body, name="attn_core_bwd", grid=(N_KV, NT),
        in_specs=[pl.BlockSpec(memory_space=pltpu.SMEM), cur, cur, cur, prev, cur, prev],
        out_specs=[cur, col, col, pl.BlockSpec((None, 8, 128), lambda j, n: (j, 0, 0))],
        out_shape=[SDS((T, D), f32), SDS((T, D), f32), SDS((T, D), f32), SDS((N_KV, 8, 128), f32)],
        compiler_params=_cp(dimension_semantics=("arbitrary", "arbitrary")),
    )(sinks, q, do, k4, k4, v4, v4)


def attn_bwd_q(h, dh, dq, hn, g_mix, wq):
    def body(h_ref, dh_ref, dq_ref, hn_ref, gm_ref, wq_ref, out_ref, dwq_ref, dbq_ref, dgm_ref, aq):
        i = pl.program_id(0)

        @pl.when(i == 0)
        def _():
            aq[...] = jnp.zeros_like(aq)
            dbq_ref[...] = jnp.zeros_like(dbq_ref)
            dgm_ref[...] = jnp.zeros_like(dgm_ref)

        dq_ = dq_ref[...]
        dqb = dq_.astype(bf16)
        aq[...] += _dot_tn(hn_ref[...], dqb)
        dbq_ref[...] += _colsum8(dq_)
        dx, dg = _rms_bwd(h_ref[...], gm_ref[...], _dot_nt(dqb, wq_ref[...]))
        out_ref[...] = dh_ref[...] + dx
        dgm_ref[...] += _colsum8(dg)

        @pl.when(i == NT - 1)
        def _():
            dwq_ref[...] = aq[...].astype(bf16)

    vec = _full((8, D))
    mat = _full((D, D))
    return pl.pallas_call(
        body, name="attn_bwd_q", grid=(NT,),
        in_specs=[_tile()] * 4 + [_full((1, D)), mat],
        out_specs=[_tile(), mat, vec, vec],
        out_shape=[SDS((T, D), f32), SDS((D, D), bf16), SDS((8, D), f32), SDS((8, D), f32)],
        scratch_shapes=[pltpu.VMEM((D, D), f32)],
        compiler_params=_cp(dimension_semantics=("arbitrary",)),
    )(h, dh, dq, hn, g_mix, wq)


def attn_bwd_kv(h, dh, dk4, dv4, kvn, g_kv, wkv, spread):
    def body(h_ref, dh_ref, dk_ref, dv_ref, kvn_ref, gkv_ref, wkv_ref, sp_ref, out_ref, outb_ref, dw_ref, db_ref,
             dgkv_ref, acc):
        i = pl.program_id(0)

        @pl.when(i == 0)
        def _():
            for r in (acc, db_ref, dgkv_ref):
                r[...] = jnp.zeros_like(r)

        dkv = jnp.concatenate([_dot_nt(dk_ref[...].astype(bf16), sp_ref[...]),
                               _dot_nt(dv_ref[...].astype(bf16), sp_ref[...])], axis=1)
        dkvb = dkv.astype(bf16)
        acc[...] += _dot_tn(kvn_ref[...], dkvb)
        db_ref[...] += _colsum8(dkv)
        dx, dg = _rms_bwd(h_ref[...], gkv_ref[...], _dot_nt(dkvb, wkv_ref[...]))
        out = dh_ref[...] + dx
        out_ref[...] = out
        outb_ref[...] = out.astype(bf16)
        dgkv_ref[...] += _colsum8(dg)

        @pl.when(i == NT - 1)
        def _():
            dw_ref[...] = acc[...].astype(bf16)

    return pl.pallas_call(
        body, name="attn_bwd_kv", grid=(NT,),
        in_specs=[_tile()] * 5 + [_full((1, D)), _full((D, 512)), _full((256, D))],
        out_specs=[_tile(), _tile(), _full((D, 512)), _full((8, 512)), _full((8, D))],
        out_shape=[SDS((T, D), f32), SDS((T, D), bf16), SDS((D, 512), bf16), SDS((8, 512), f32), SDS((8, D), f32)],
        scratch_shapes=[pltpu.VMEM((D, 512), f32)],
        compiler_params=_cp(dimension_semantics=("arbitrary",)),
    )(h, dh, dk4, dv4, kvn, g_kv, wkv, spread)


def final_loss(h, g, target):
    def body(h_ref, g_ref, t_ref, loss_ref, dh_ref, dhb_ref, dg_ref):
        i = pl.program_id(0)

        @pl.when(i == 0)
        def _():
            loss_ref[...] = jnp.zeros_like(loss_ref)
            dg_ref[...] = jnp.zeros_like(dg_ref)

        h_ = h_ref[...]
        g_ = g_ref[...]
        y, _ = _rms(h_, g_)
        diff = y - t_ref[...]
        per_tok = jnp.mean(diff * diff, axis=-1, keepdims=True)
        tot = 0.5 * jnp.sum(per_tok, axis=0, keepdims=True)
        lane = lax.broadcasted_iota(jnp.int32, (8, 128), 1)
        row = lax.broadcasted_iota(jnp.int32, (8, 128), 0)
        loss_ref[...] += jnp.where((lane == 0) & (row == 0), jnp.broadcast_to(tot, (8, 128)), 0.0)
        dx, dgt = _rms_bwd(h_, g_, diff * (1.0 / D))
        dh_ref[...] = dx
        dhb_ref[...] = dx.astype(bf16)
        dg_ref[...] += _colsum8(dgt)

    return pl.pallas_call(
        body, name="final_loss", grid=(NT,), in_specs=[_tile(), _full((1, D)), _tile()],
        out_specs=[_full((8, 128)), _tile(), _tile(), _full((8, D))],
        out_shape=[SDS((8, 128), f32), SDS((T, D), f32), SDS((T, D), bf16), SDS((8, D), f32)],
        compiler_params=_cp(dimension_semantics=("arbitrary",)),
    )(h, g, target)


def _permute_rows(a, to_chunked, name, also_bf16=False):
    def body(in_ref, out_ref, *rest):
        def two_steps(i, c):
            s = i * 2
            if to_chunked:
                v = jnp.concatenate([in_ref[pl.ds(s, S5_CH, stride=S5_STEPS), :],
                                     in_ref[pl.ds(s + 1, S5_CH, stride=S5_STEPS), :]], axis=0)
                rows = pl.ds(pl.multiple_of(s * S5_CH, 2 * S5_CH), 2 * S5_CH)
                out_ref[rows, :] = v
                if also_bf16:
                    rest[0][rows, :] = v.astype(bf16)
            else:
                for u in range(2):
                    out_ref[pl.ds(s + u, S5_CH, stride=S5_STEPS), :] = in_ref[
                        pl.ds(pl.multiple_of((s + u) * S5_CH, S5_CH), S5_CH), :]
            return c
        lax.fori_loop(0, S5_STEPS // 2, two_steps, 0)

    strip = pl.BlockSpec((T, 128), lambda k: (0, k))
    outs = [SDS((T, D), f32)] + ([SDS((T, D), bf16)] if also_bf16 else [])
    res = pl.pallas_call(
        body, name=name, grid=(D // 128,), in_specs=[strip], out_specs=[strip] * len(outs), out_shape=outs,
        compiler_params=_cp(dimension_semantics=("arbitrary",)),
    )(a)
    return res if also_bf16 else res[0]


def _to_chunked(a, name, also_bf16=False):
    return _permute_rows(a, True, name, also_bf16)


def _from_chunked(a, name):
    return _permute_rows(a, False, name)


def _rep4(w):
    return jnp.broadcast_to(w.reshape(w.shape[0], N_KV, 1, HEAD_DIM), (w.shape[0], N_KV, Q_PER_KV, HEAD_DIM)).reshape(
        w.shape[0], N_KV * Q_PER_KV * HEAD_DIM)


def _fold4(w):
    return w.reshape(w.shape[0], N_KV, Q_PER_KV, HEAD_DIM).sum(axis=2).reshape(w.shape[0], N_KV * HEAD_DIM)


def fwd_bwd(x, target, p, shards, opt, core, chip):
    row = lambda v: v.reshape(1, -1)
    (lam, bm, cm), prep_vjp = jax.vjp(s5_discretize, p["s5_a_re"][0], p["s5_a_im"][0], p["s5_log_dt"][0],
                                      p["s5_b_re"][0], p["s5_b_im"][0], p["s5_c_re"][0], p["s5_c_im"][0])
    bmb, cmb = bm.astype(bf16), cm.astype(bf16)
    lam = jnp.concatenate([lam, lam * jnp.array([1.0, -1.0], f32).reshape(1, 2, 1, 1)], axis=1)
    g_mix0, g_mix1 = row(p["norm_mix"][0]), row(p["norm_mix"][1])
    g_mlp0, g_mlp1 = row(p["norm_mlp"][0]), row(p["norm_mlp"][1])
    g_kv, g_fin = row(p["norm_kv"]), row(p["norm_final"])
    bq, bo = p["b_q"], p["b_o"]
    bkv = row(p["b_kv"])
    spread = _spread4()
    sinks = p["sinks"].reshape(16)

    def reduce_pairs(names, bg):
        return [add_pairs(g, r, core, f"add_pairs_{n}") for n, g, r in zip(names, bg.arrs, bg.result)]

    wglu, gvec = sc_gather([shards["s5_w_glu"], shards["vecs"]], 3, "sc_gather_s5")
    win0, wout0 = sc_gather([shards["w_in0"], shards["w_out0"]], 14, "sc_gather_mlp0")
    wkv, wq, wo = sc_gather([shards["w_kv"], shards["w_q"], shards["w_o"]], 4, "sc_gather_attn")
    win1, wout1 = sc_gather([shards["w_in1"], shards["w_out1"]], 5, "sc_gather_mlp1")
    xp = x
    hn0 = s5_pre(xp, g_mix0)
    ys, xs = s5_core_fwd(hn0, bmb, lam, cmb)
    d_skip = gvec[:, 0, :128].reshape(1, D)
    bglu = gvec[:, 0, 128:].reshape(1, 2 * D)
    y, z, h1 = s5_post(ys, xp, g_mix0, d_skip, wglu, bglu)
    hm0, r0, h2p = mlp_fwd(h1, g_mlp0, win0, wout0, 0)
    wkv, wq, wo = wkv.reshape(D, 512), wq.reshape(D, D), wo.reshape(D, D)
    h2 = h2p
    kvn, hn1, k4, v4, q = attn_pre(h2, g_kv, g_mix1, wkv, bkv, spread, wq, bq)
    o = attn_core_fwd(q, k4, v4, sinks)
    h3 = attn_post(h2, o, wo, bo)
    hm1, r1, h4 = mlp_fwd(h3, g_mlp1, win1, wout1, 1)
    loss, dh4, dh4b, dg_fin = final_loss(h4, g_fin, target)

    def pair_sums(names, grads, cid, before):
        r1 = sc_comm(BgPair(grads), cid, "sc_pair_" + names[0])
        parts = [add_pairs(g, r, core, f"add_pairs_{n}") for n, g, r in zip(names, grads, r1)]
        before, parts = lax.optimization_barrier((before, parts))
        return before, parts

    def across_chips(names, parts, cid):
        return list(zip(parts, sc_comm(BgChips(parts), cid, "sc_chips_" + names[0])))

    dh3, dwin1, dwout1, dg_mlp1 = mlp_bwd(h3, hm1, r1, g_mlp1, dh4, dh4b, win1, wout1, 1)
    do, dwo, dbo = attn_bwd_pre(dh3, o, wo)
    do, parts = pair_sums(["w_in1", "w_out1"], [dwin1, dwout1], 6, do)
    rs_in1, rs_out1 = across_chips(["w_in1", "w_out1"], parts, 7)
    dq, dk4, dv4, dsink = attn_core_bwd(q, do, k4, v4, sinks)
    dh2, dwq, dbq, dg_mix1 = attn_bwd_q(h2, dh3, dq, hn1, g_mix1, wq)
    dh2, dh2b, dwkv, dbkv, dg_kv = attn_bwd_kv(h2, dh2, dk4, dv4, kvn, g_kv, wkv, spread)
    dh2p, dh2pb = dh2, dh2b
    big = {}
    a_in1 = adam_big(*opt["w_mlp_in"], *rs_in1, chip, "adam_w_mlp_in1", layer=1)
    a_out1 = adam_big(*opt["w_mlp_out"], *rs_out1, chip, "adam_w_mlp_out1", layer=1)
    dh2p, a_in1, a_out1 = lax.optimization_barrier((dh2p, a_in1, a_out1))
    names = ["w_kv", "w_q", "w_o"]
    dh2p, parts = pair_sums(names, [dwkv.reshape(NDEV, 128, 512), dwq.reshape(NDEV, 128, D),
                                    dwo.reshape(NDEV, 128, D)], 8, dh2p)
    rs_attn = across_chips(names, parts, 9)
    dh1, dwin0, dwout0, dg_mlp0 = mlp_bwd(h1, hm0, r0, g_mlp0, dh2p, dh2pb, win0, wout0, 0)
    a_attn = [adam_big(*opt[n], *rs, chip, f"adam_{n}") for n, rs in zip(names, rs_attn)]
    dh1, a_attn = lax.optimization_barrier((dh1, a_attn))
    big.update(zip(names, a_attn))
    dy, dwglu, dbglu = s5_post_bwd(dh1, y, z, wglu)
    dy, parts = pair_sums(["w_in0", "w_out0"], [dwin0, dwout0], 10, dy)
    rs_in0, rs_out0 = across_chips(["w_in0", "w_out0"], parts, 11)
    du, dbm, dcmt, dlam = s5_core_bwd(hn0, dy, xs, bmb, lam, cmb)
    du, parts = pair_sums(["s5_w_glu"], [dwglu], 12, du)
    rs_glu, = across_chips(["s5_w_glu"], parts, 13)
    dxp, dg_mix0, dd = s5_pre_bwd(xp, g_mix0, du, dy, d_skip, dh1)
    big["w_mlp_in"] = adam_big(*opt["w_mlp_in"], *rs_in0, chip, "adam_w_mlp_in0", layer=0, prev=a_in1)
    big["w_mlp_out"] = adam_big(*opt["w_mlp_out"], *rs_out0, chip, "adam_w_mlp_out0", layer=0, prev=a_out1)
    big["s5_w_glu"] = adam_big(*opt["s5_w_glu"], *rs_glu, chip, "adam_s5_w_glu")
    grad_x = dxp
    da_re, da_im, dlog_dt, db_re, db_im, dc_re, dc_im = prep_vjp((dlam, dbm, dcmt.transpose(0, 2, 1)))

    def lanes(v_):
        v_ = v_.reshape(1, -1)
        return jnp.pad(v_, ((0, 0), (0, D - v_.shape[1])))

    small = jnp.concatenate([
        dg_mix0[0:1], dg_mix1[0:1], dg_mlp0[0:1], dg_mlp1[0:1], dg_kv[0:1], dg_fin[0:1], dd[0:1], dbq[0:1], dbo[0:1],
        dbglu[0:1].reshape(2, D), lanes(dbkv[0:1]),
        lanes(dsink[:, 0, :Q_PER_KV]), lanes(dlog_dt), lanes(loss[0:1, 0:1]), jnp.zeros((1, D), f32),
        da_re.reshape(4, D), da_im.reshape(4, D),
        db_re.transpose(0, 2, 1).reshape(64, D), db_im.transpose(0, 2, 1).reshape(64, D),
        dc_re.reshape(64, D), dc_im.reshape(64, D)], axis=0)
    small, big["w_mlp_in"], big["w_mlp_out"] = lax.optimization_barrier((small, big["w_mlp_in"], big["w_mlp_out"]))
    return loss, grad_x, small, big


_ANY = pl.BlockSpec(memory_space=pl.ANY)


def _pos():
    return lax.axis_index("x"), lax.axis_index("y"), lax.axis_index("c")


def _other_chips(x, y):
    return [(1 - x, y), (x, 1 - y), (1 - x, 1 - y)]


def all_gather(arrs):
    n = len(arrs)

    def body(*refs):
        ins, outs = refs[:n], refs[n:2 * n]
        send_sems, recv_sems, local_sems = refs[2 * n:]
        x, y, c = _pos()
        me, sib = (x, y, c), (x, y, 1 - c)
        chips = _other_chips(x, y)

        def copy(a, k, block, to, src=None):
            dst = outs[a].at[4 * block[0] + 2 * block[1] + block[2]]
            return pltpu.make_async_remote_copy(
                src_ref=dst if src is None else src, dst_ref=dst, send_sem=send_sems.at[a, k],
                recv_sem=recv_sems.at[a, k], device_id=to, device_id_type=MESH)

        mine = [pltpu.make_async_copy(ins[a], outs[a].at[4 * x + 2 * y + c], local_sems.at[a]) for a in range(n)]
        for cp in mine:
            cp.start()
        first = []
        for a in range(n):
            first.append(copy(a, 0, me, sib, src=ins[a]))
            first += [copy(a, 1 + j, me, (*chip, c), src=ins[a]) for j, chip in enumerate(chips)]
        for cp in first:
            cp.start()
        passed = []
        for j, chip in enumerate(chips):
            for a in range(n):
                copy(a, 1 + j, (*chip, c), me).wait_recv()
                cp = copy(a, 4 + j, (*chip, c), sib)
                cp.start()
                passed.append(cp)
        for a in range(n):
            copy(a, 0, sib, me).wait_recv()
            for j, chip in enumerate(chips):
                copy(a, 4 + j, (*chip, 1 - c), me).wait_recv()
        for cp in first + passed:
            cp.wait_send()
        for cp in mine:
            cp.wait()

    return pl.pallas_call(
        body, name="all_gather", in_specs=[_ANY] * n, out_specs=[_ANY] * n,
        out_shape=[SDS((NDEV,) + a.shape, a.dtype) for a in arrs],
        scratch_shapes=[pltpu.SemaphoreType.DMA((n, 7)), pltpu.SemaphoreType.DMA((n, 7)),
                        pltpu.SemaphoreType.DMA((n,))],
    )(*arrs)


def rs_pair(grads):
    n = len(grads)

    def body(*refs):
        ins, outs = refs[:n], refs[n:2 * n]
        send_sems, recv_sems = refs[2 * n:]
        x, y, c = _pos()
        cps = []
        for a in range(n):
            for k in range(4):
                cps.append(pltpu.make_async_remote_copy(
                    src_ref=ins[a].at[2 * k + 1 - c], dst_ref=outs[a].at[k], send_sem=send_sems.at[a, k],
                    recv_sem=recv_sems.at[a, k], device_id=(x, y, 1 - c), device_id_type=MESH))
        for cp in cps:
            cp.start()
        for cp in cps:
            cp.wait_recv()
        for cp in cps:
            cp.wait_send()

    return pl.pallas_call(
        body, name="rs_pair", in_specs=[_ANY] * n, out_specs=[_ANY] * n,
        out_shape=[SDS((4,) + g.shape[1:], g.dtype) for g in grads],
        scratch_shapes=[pltpu.SemaphoreType.DMA((n, 4)), pltpu.SemaphoreType.DMA((n, 4))],
    )(*grads)


def rs_chips(parts):
    n = len(parts)

    def body(*refs):
        ins, outs = refs[:n], refs[n:2 * n]
        send_sems, recv_sems = refs[2 * n:]
        x, y, c = _pos()
        cps = []
        for a in range(n):
            for r, (px, py) in enumerate(_other_chips(x, y)):
                cps.append(pltpu.make_async_remote_copy(
                    src_ref=ins[a].at[2 * px + py], dst_ref=outs[a].at[r], send_sem=send_sems.at[a, r],
                    recv_sem=recv_sems.at[a, r], device_id=(px, py, c), device_id_type=MESH))
        for cp in cps:
            cp.start()
        for cp in cps:
            cp.wait_recv()
        for cp in cps:
            cp.wait_send()

    return pl.pallas_call(
        body, name="rs_chips", in_specs=[_ANY] * n, out_specs=[_ANY] * n,
        out_shape=[SDS((3,) + g.shape[1:], g.dtype) for g in parts],
        scratch_shapes=[pltpu.SemaphoreType.DMA((n, 3)), pltpu.SemaphoreType.DMA((n, 3))],
    )(*parts)


def _row_tile(r, c):
    return min(r, max(8, (512 * 1024) // c))


def add_pairs(g, r1, core, name):
    _, R, C = g.shape
    tr = _row_tile(R, C)

    def body(core_ref, g_ref, r_ref, o_ref):
        o_ref[...] = (g_ref[...].astype(f32) + r_ref[...].astype(f32)).astype(bf16)

    return pl.pallas_call(
        body, name=name, out_shape=SDS((4, R, C), bf16),
        grid_spec=pltpu.PrefetchScalarGridSpec(
            num_scalar_prefetch=1, grid=(4, R // tr),
            in_specs=[pl.BlockSpec((None, tr, C), lambda k, i, core: (2 * k + core[0], i, 0)),
                      pl.BlockSpec((None, tr, C), lambda k, i, core: (k, i, 0))],
            out_specs=pl.BlockSpec((None, tr, C), lambda k, i, core: (k, i, 0))),
        compiler_params=_cp(dimension_semantics=("arbitrary", "arbitrary")),
    )(core, g, r1)


def _adamw(w, g, m, v):
    m = ADAM_B1 * m + (1.0 - ADAM_B1) * g
    v = ADAM_B2 * v + (1.0 - ADAM_B2) * (g * g)
    m_hat = m / (1.0 - ADAM_B1 ** ADAM_STEP)
    v_hat = v / (1.0 - ADAM_B2 ** ADAM_STEP)
    delta = -ADAM_LR * (m_hat / (jnp.sqrt(v_hat) + ADAM_EPS) + ADAM_WD * w)
    return delta, m, v


def adam_big(w, m, v, part, r2, chip, name, layer=0, prev=None):
    L, R, C = w.shape
    tr = _row_tile(R, C)

    def body(chip_ref, w_ref, m_ref, v_ref, p_ref, r_ref, *rest):
        g_out, d_out, m_out, v_out = rest[-4:]
        g = p_ref[...].astype(f32) + r_ref[0].astype(f32) + r_ref[1].astype(f32) + r_ref[2].astype(f32)
        d, m_, v_ = _adamw(w_ref[...], g, m_ref[...], v_ref[...])
        g_out[...] = g
        d_out[...] = d
        m_out[...] = m_
        v_out[...] = v_

    blk = pl.BlockSpec((None, tr, C), lambda i, chip: (layer, i, 0))
    extra = [] if prev is None else list(prev)
    return pl.pallas_call(
        body, name=name, out_shape=[SDS((L, R, C), f32)] * 4,
        grid_spec=pltpu.PrefetchScalarGridSpec(
            num_scalar_prefetch=1, grid=(R // tr,),
            in_specs=[blk, blk, blk,
                      pl.BlockSpec((None, tr, C), lambda i, chip: (chip[0], i, 0)),
                      pl.BlockSpec((3, tr, C), lambda i, chip: (0, i, 0))] + [_ANY] * len(extra),
            out_specs=[blk] * 4),
        input_output_aliases={6 + k: k for k in range(len(extra))},
        compiler_params=_cp(dimension_semantics=("arbitrary",)),
    )(chip, w, m, v, part, r2, *extra)


def allreduce_small(buf, chips=None):
    shp = buf.shape
    half = (shp[0] // 16) * 8
    parts = (pl.ds(0, half), pl.ds(half, shp[0] - half))
    n_c = 0 if chips is None else len(chips.arrs)

    def body(in_ref, *refs):
        c_in, out_ref, c_out = refs[:n_c], refs[n_c], refs[n_c + 1:2 * n_c + 1]
        acc1, acc2, r0, r1, r2, send_sems, recv_sems = refs[2 * n_c + 1:2 * n_c + 8]
        c_sems = refs[2 * n_c + 8:]
        if chips is not None:
            chips.start(c_in, c_out, c_sems)
        x, y, c = _pos()
        across = [(1 - x, y, c), (x, 1 - y, c)]

        def exchange(src, rcv, dst, copies):
            cps = [pltpu.make_async_remote_copy(
                src_ref=src.at[rows], dst_ref=rcv.at[rows], send_sem=send_sems.at[k], recv_sem=recv_sems.at[k],
                device_id=peer, device_id_type=MESH) for k, rows, peer in copies]
            for cp in cps:
                cp.start()
            for cp in cps:
                cp.wait()
            dst[...] = src[...] + rcv[...]

        exchange(in_ref, r0, acc1, [(0, pl.ds(0, shp[0]), (x, y, 1 - c))])
        exchange(acc1, r1, acc2, [(1, parts[0], across[0]), (2, parts[1], across[1])])
        exchange(acc2, r2, out_ref, [(3, parts[0], across[1]), (4, parts[1], across[0])])
        if chips is not None:
            chips.finish(c_in, c_out, c_sems)

    vm = pl.BlockSpec(memory_space=pltpu.VMEM)
    res = pl.pallas_call(
        body, name="allreduce_small", in_specs=[vm] + [_ANY] * n_c, out_specs=[vm] + [_ANY] * n_c,
        out_shape=[SDS(shp, f32)] + ([] if chips is None else chips.out_shape),
        scratch_shapes=[pltpu.VMEM(shp, f32)] * 5 + [pltpu.SemaphoreType.DMA((5,)), pltpu.SemaphoreType.DMA((5,))]
        + ([] if chips is None else chips.scratch),
    )(buf, *([] if chips is None else chips.arrs))
    if chips is not None:
        chips.result = list(res[1:])
    return res[0]


SMALL_ROWS = {'norm_mix': (0, 2, D), 'norm_mlp': (2, 2, D), 'norm_kv': (4, 1, D), 'norm_final': (5, 1, D),
              's5_d': (6, 1, D), 'b_q': (7, 1, D), 'b_o': (8, 1, D), 's5_b_glu': (9, 2, D), 'b_kv': (11, 1, 512),
              'sinks': (12, 1, 16), 's5_log_dt': (13, 1, 64), 's5_a_re': (16, 4, D), 's5_a_im': (20, 4, D),
              's5_b_re': (24, 64, D), 's5_b_im': (88, 64, D), 's5_c_re': (152, 64, D), 's5_c_im': (216, 64, D)}
LOSS_ROW = 14
ROW_PARAMS = ['norm_mix', 'norm_mlp', 'norm_kv', 'norm_final', 'b_q', 'b_o', 'b_kv', 'sinks', 's5_log_dt']
SHARD_PARAMS = ['s5_d', 's5_b_glu']
S5_PARAMS = ['s5_a_re', 's5_a_im', 's5_b_re', 's5_b_im', 's5_c_re', 's5_c_im']


def adam_small(dev, gsum, s5_grads, w, m, v):
    names = ROW_PARAMS + SHARD_PARAMS + S5_PARAMS
    n_g = len(ROW_PARAMS) + len(SHARD_PARAMS)

    def body(dev_ref, gs_ref, *refs):
        pos = [0]

        def take(k):
            r = refs[pos[0]:pos[0] + k]
            pos[0] += k
            return r

        g5 = take(len(S5_PARAMS))
        wr, mr, vr = take(len(names)), take(len(names)), take(len(names))
        g_out = take(n_g)
        d_out, m_out, v_out = take(len(names)), take(len(names)), take(len(names))
        dv = dev_ref[0]
        for i, n in enumerate(names):
            if n in S5_PARAMS:
                g = g5[S5_PARAMS.index(n)][...]
            elif n in SHARD_PARAMS:
                r0, _, _ = SMALL_ROWS[n]
                ln = wr[i].shape[1]
                g = jnp.zeros((1, ln), f32)
                for k in range(NDEV):
                    off = k * ln
                    piece = gs_ref[r0 + off // D:r0 + off // D + 1, off % D:off % D + ln]
                    g = g + jnp.where(dv == k, piece, 0.0)
                g_out[i][...] = g
            else:
                r0, nr, nl = SMALL_ROWS[n]
                g = gs_ref[r0:r0 + nr, 0:nl]
                g_out[i][...] = g
            d, m_, v_ = _adamw(wr[i][...], g, mr[i][...], vr[i][...])
            d_out[i][...] = d
            m_out[i][...] = m_
            v_out[i][...] = v_

    vm = pl.BlockSpec(memory_space=pltpu.VMEM)
    ins = [s5_grads[n] for n in S5_PARAMS] + [d[n] for d in (w, m, v) for n in names]
    shapes = [SDS(w[n].shape, f32) for n in names]
    res = pl.pallas_call(
        body, name="adam_small", in_specs=[pl.BlockSpec(memory_space=pltpu.SMEM)] + [vm] * (1 + len(ins)),
        out_specs=[vm] * (n_g + 3 * len(names)), out_shape=shapes[:n_g] + shapes * 3,
        compiler_params=_cp(),
    )(dev, gsum, *ins)
    g_o = dict(zip(names[:n_g], res[:n_g]))
    rest = res[n_g:]
    k = len(names)
    return g_o, dict(zip(names, rest[:k])), dict(zip(names, rest[k:2 * k])), dict(zip(names, rest[2 * k:]))


WEIGHTS = ['norm_mix', 'norm_mlp', 'norm_kv', 'norm_final', 's5_a_re', 's5_a_im', 's5_log_dt', 's5_b_re', 's5_b_im',
           's5_c_re', 's5_c_im', 's5_d', 's5_w_glu', 's5_b_glu', 'w_kv', 'b_kv', 'w_q', 'b_q', 'sinks', 'w_o', 'b_o',
           'w_mlp_in', 'w_mlp_out']
BIG = ['s5_w_glu', 'w_kv', 'w_q', 'w_o', 'w_mlp_in', 'w_mlp_out']
BIG_2D = {'s5_w_glu': (D, 256), 'w_kv': (128, 512), 'w_q': (128, D), 'w_o': (128, D), 'w_mlp_in': (2 * D, 512),
          'w_mlp_out': (2 * 512, D)}
SHARDED_SMALL = {'s5_d': D, 's5_b_glu': 2 * D}
SMALL = [n for n in WEIGHTS if n not in BIG]
SMALL_SIZE = {'norm_mix': 2 * D, 'norm_mlp': 2 * D, 'norm_kv': D, 'norm_final': D, 's5_a_re': 4096, 's5_a_im': 4096,
              's5_log_dt': 64, 's5_b_re': 65536, 's5_b_im': 65536, 's5_c_re': 65536, 's5_c_im': 65536, 's5_d': D,
              's5_b_glu': 2 * D, 'b_kv': 512, 'b_q': D, 'sinks': 16, 'b_o': D}


def _pack(vals):
    parts = []
    for n in SMALL:
        v = vals[n].reshape(-1).astype(f32)
        parts.append(jnp.pad(v, (0, (-v.shape[0]) % 128)))
    flat = jnp.concatenate(parts)
    flat = jnp.pad(flat, (0, (-flat.shape[0]) % 1024))
    return flat.reshape(-1, 128)


def _unpack(buf):
    flat = buf.reshape(-1)
    out, off = {}, 0
    for n in SMALL:
        sz = SMALL_SIZE[n]
        out[n] = flat[off:off + sz]
        off += sz + (-sz) % 128
    return out


def kernel(x, norm_mix, norm_mlp, norm_kv, norm_final, s5_a_re, s5_a_im, s5_log_dt, s5_b_re, s5_b_im, s5_c_re, s5_c_im, s5_d, s5_w_glu, s5_b_glu, w_kv, b_kv, w_q, b_q, sinks, w_o, b_o, w_mlp_in, w_mlp_out, loss_target, m_norm_mix, m_norm_mlp, m_norm_kv, m_norm_final, m_s5_a_re, m_s5_a_im, m_s5_log_dt, m_s5_b_re, m_s5_b_im, m_s5_c_re, m_s5_c_im, m_s5_d, m_s5_w_glu, m_s5_b_glu, m_w_kv, m_b_kv, m_w_q, m_b_q, m_sinks, m_w_o, m_b_o, m_w_mlp_in, m_w_mlp_out, v_norm_mix, v_norm_mlp, v_norm_kv, v_norm_final, v_s5_a_re, v_s5_a_im, v_s5_log_dt, v_s5_b_re, v_s5_b_im, v_s5_c_re, v_s5_c_im, v_s5_d, v_s5_w_glu, v_s5_b_glu, v_w_kv, v_b_kv, v_w_q, v_b_q, v_sinks, v_w_o, v_b_o, v_w_mlp_in, v_w_mlp_out):
    w = dict(norm_mix=norm_mix, norm_mlp=norm_mlp, norm_kv=norm_kv, norm_final=norm_final, s5_a_re=s5_a_re,
             s5_a_im=s5_a_im, s5_log_dt=s5_log_dt, s5_b_re=s5_b_re, s5_b_im=s5_b_im, s5_c_re=s5_c_re, s5_c_im=s5_c_im,
             s5_d=s5_d, s5_w_glu=s5_w_glu, s5_b_glu=s5_b_glu, w_kv=w_kv, b_kv=b_kv, w_q=w_q, b_q=b_q, sinks=sinks,
             w_o=w_o, b_o=b_o, w_mlp_in=w_mlp_in, w_mlp_out=w_mlp_out)
    m = dict(norm_mix=m_norm_mix, norm_mlp=m_norm_mlp, norm_kv=m_norm_kv, norm_final=m_norm_final, s5_a_re=m_s5_a_re,
             s5_a_im=m_s5_a_im, s5_log_dt=m_s5_log_dt, s5_b_re=m_s5_b_re, s5_b_im=m_s5_b_im, s5_c_re=m_s5_c_re,
             s5_c_im=m_s5_c_im, s5_d=m_s5_d, s5_w_glu=m_s5_w_glu, s5_b_glu=m_s5_b_glu, w_kv=m_w_kv, b_kv=m_b_kv,
             w_q=m_w_q, b_q=m_b_q, sinks=m_sinks, w_o=m_w_o, b_o=m_b_o, w_mlp_in=m_w_mlp_in, w_mlp_out=m_w_mlp_out)
    v = dict(norm_mix=v_norm_mix, norm_mlp=v_norm_mlp, norm_kv=v_norm_kv, norm_final=v_norm_final, s5_a_re=v_s5_a_re,
             s5_a_im=v_s5_a_im, s5_log_dt=v_s5_log_dt, s5_b_re=v_s5_b_re, s5_b_im=v_s5_b_im, s5_c_re=v_s5_c_re,
             s5_c_im=v_s5_c_im, s5_d=v_s5_d, s5_w_glu=v_s5_w_glu, s5_b_glu=v_s5_b_glu, w_kv=v_w_kv, b_kv=v_b_kv,
             w_q=v_w_q, b_q=v_b_q, sinks=v_sinks, w_o=v_w_o, b_o=v_b_o, w_mlp_in=v_w_mlp_in, w_mlp_out=v_w_mlp_out)
    xi, yi, ci = _pos()
    dev = 4 * xi + 2 * yi + ci
    core = ci.reshape(1).astype(jnp.int32)
    chip = (2 * xi + yi).reshape(1).astype(jnp.int32)

    shards = {
        "s5_w_glu": s5_w_glu[0].astype(bf16), "w_kv": w_kv.astype(bf16), "w_q": w_q[0].astype(bf16),
        "w_o": w_o[0].astype(bf16), "w_in0": w_mlp_in[0].astype(bf16), "w_in1": w_mlp_in[1].astype(bf16),
        "w_out0": w_mlp_out[0].astype(bf16), "w_out1": w_mlp_out[1].astype(bf16),
        "vecs": jnp.broadcast_to(jnp.concatenate([s5_d, s5_b_glu], axis=1), (8, 384)),
    }
    as3d = lambda a, n: a if a.ndim == 3 and a.shape[0] == 2 else a.reshape((1,) + BIG_2D[n])
    opt = {n: (as3d(w[n], n), as3d(m[n], n), as3d(v[n], n)) for n in BIG}
    _, grad_x, grads, big = fwd_bwd(x[0], loss_target[0], {n: w[n] for n in SMALL}, shards, opt, core, chip)

    gsum = allreduce_small(grads)

    out_g, out_d, out_m, out_v = {}, {}, {}, {}
    for n in BIG:
        out_g[n], out_d[n], out_m[n], out_v[n] = [r.reshape(w[n].shape) for r in big[n]]

    loss = gsum[LOSS_ROW, 0]
    swapped = ("s5_b_re", "s5_b_im")
    swap = lambda a: a.transpose(0, 1, 3, 2)

    def kernel_side(d):
        d = {n: (d[n].reshape(1, -1) if d[n].ndim == 1 else d[n]) for n in SMALL}
        d.update({n: swap(d[n]) for n in swapped})
        return d

    s5_g = {}
    for n in S5_PARAMS:
        r0, nr, _ = SMALL_ROWS[n]
        s5_g[n] = gsum[r0:r0 + nr].reshape((1, 64, 16, 64) if n in swapped else w[n].shape)
        out_g[n] = s5_g[n]
    g_s, d_s, m_s, v_s = adam_small(dev.reshape(1).astype(jnp.int32), gsum, s5_g, kernel_side(w), kernel_side(m),
                                    kernel_side(v))
    for src, dst in ((g_s, out_g), (d_s, out_d), (m_s, out_m), (v_s, out_v)):
        dst.update(src)
    for dst in (out_g, out_d, out_m, out_v):
        for n in SMALL:
            dst[n] = (swap(dst[n]) if n in swapped else dst[n]).reshape(w[n].shape)

    return (loss, grad_x[None], *[out_g[n] for n in WEIGHTS], *[out_d[n] for n in WEIGHTS],
            *[out_m[n] for n in WEIGHTS], *[out_v[n] for n in WEIGHTS])
```

```python
import functools
import math

import jax
import jax.numpy as jnp
from jax import lax
from jax.experimental import pallas as pl
from jax.experimental.pallas import tpu as pltpu
from jax.experimental.pallas import tpu_sc as plsc

f32 = jnp.float32
bf16 = jnp.bfloat16
SDS = jax.ShapeDtypeStruct

T = 2048
D = 1024
NDEV = 8
NORM_EPS = 1e-5
S5_G, S5_C, S5_P = 64, 16, 64
S5_SUB = 8
S5_CH = 8
S5_STEPS = T // S5_CH
DT_MIN_LAMBDA = -1e-4
HEAD_DIM = 64
N_KV = 4
Q_PER_KV = 4
BLK = 128
D_FF_SHARD = 512
ADAM_LR, ADAM_B1, ADAM_B2, ADAM_EPS, ADAM_WD, ADAM_STEP = 0.001, 0.9, 0.999, 1e-08, 0.01, 10
VMEM_LIMIT = 56 * 1024 * 1024
MESH = pl.DeviceIdType.MESH


def _cp(**kw):
    return pltpu.CompilerParams(vmem_limit_bytes=VMEM_LIMIT, **kw)


def _dot(a, b):
    return jnp.dot(a, b, preferred_element_type=f32)


def _dot_nt(a, b):
    return lax.dot_general(a, b, (((1,), (1,)), ((), ())), preferred_element_type=f32)


def _dot_tn(a, b):
    return lax.dot_general(a, b, (((0,), (0,)), ((), ())), preferred_element_type=f32)


def _rms(x, g):
    r = lax.rsqrt(jnp.mean(x * x, axis=-1, keepdims=True) + NORM_EPS)
    return x * r * g, r


def _rms_bwd(x, g, dy):
    r = lax.rsqrt(jnp.mean(x * x, axis=-1, keepdims=True) + NORM_EPS)
    u = dy * g
    dx = r * u - (r * r * r) * x * jnp.mean(u * x, axis=-1, keepdims=True)
    return dx, dy * x * r


def _colsum8(v):
    s = jnp.sum(v, axis=0, keepdims=True)
    row = lax.broadcasted_iota(jnp.int32, (8, v.shape[1]), 0)
    return jnp.where(row == 0, jnp.broadcast_to(s, (8, v.shape[1])), 0.0)


def _full(shape):
    nd = len(shape)
    return pl.BlockSpec(shape, lambda *_: (0,) * nd, pipeline_mode=pl.Buffered(1))


_ANY = pl.BlockSpec(memory_space=pl.ANY)


def _pos():
    return lax.axis_index("x"), lax.axis_index("y"), lax.axis_index("c")


def _other_chips(x, y):
    return [(1 - x, y), (x, 1 - y), (1 - x, 1 - y)]


class BgGather:
    SIB, XN, YN, FWD_Y, FWD_X, SIB_X, SIB_Y, SIB_D = range(8)

    def __init__(self, arrs, mids=(0.5, 0.75)):
        n = len(arrs)
        self.arrs = list(arrs)
        self.out_shape = [SDS((NDEV,) + a.shape, a.dtype) for a in arrs]
        self.scratch = [pltpu.SemaphoreType.DMA((n, 8)), pltpu.SemaphoreType.DMA((n, 8)),
                        pltpu.SemaphoreType.DMA((n,))]
        self.mids = mids
        self.result = None

    @staticmethod
    def peers(x, y, c):
        return [(x, y, 1 - c), (1 - x, y, c), (x, 1 - y, c)]

    def mid_steps(self, nsteps):
        at = lambda f: min(nsteps - 1, max(0, int(f * nsteps) - 1))
        return [(at(self.mids[0]), self.mid), (max(at(self.mids[0]), at(self.mids[1])), self.mid2)]

    def _halves(self, a):
        rows = self.arrs[a].shape[0]
        cut = (rows // 32) * 16 if rows >= 32 else rows
        return (0, cut), (cut, rows - cut)

    def _copy(self, ins, outs, sems, a, k, block, to, own=False, part=None):
        slot = 4 * block[0] + 2 * block[1] + block[2]
        rows = pl.ds(0, self.arrs[a].shape[0]) if part is None else pl.ds(*self._halves(a)[part])
        dst = outs[a].at[slot, rows]
        return pltpu.make_async_remote_copy(
            src_ref=ins[a].at[rows] if own else dst, dst_ref=dst, send_sem=sems[0].at[a, k],
            recv_sem=sems[1].at[a, k], device_id=to, device_id_type=MESH)

    def _mine(self, ins, outs, sems):
        x, y, c = _pos()
        return [pltpu.make_async_copy(ins[a], outs[a].at[4 * x + 2 * y + c], sems[2].at[a])
                for a in range(len(self.arrs))]

    def _split(self, a):
        return self._halves(a)[1][1] > 0

    def _sends(self, ins, outs, sems, phase):
        x, y, c = _pos()
        me, sib, xn, yn, dg = (x, y, c), (x, y, 1 - c), (1 - x, y, c), (x, 1 - y, c), (1 - x, 1 - y, c)
        cps = []
        for a in range(len(self.arrs)):
            cp = lambda k, block, to, **kw: self._copy(ins, outs, sems, a, k, block, to, **kw)
            if phase == 0:
                cps += [cp(self.SIB, me, sib, own=True), cp(self.XN, me, xn, own=True), cp(self.YN, me, yn, own=True)]
            elif phase == 1:
                cps.append(cp(self.FWD_Y, xn, yn, part=0))
                if self._split(a):
                    cps.append(cp(self.FWD_X, yn, xn, part=1))
                cps += [cp(self.SIB_X, xn, sib), cp(self.SIB_Y, yn, sib)]
            else:
                cps.append(cp(self.SIB_D, dg, sib))
        return cps

    def _arrivals(self, ins, outs, sems, phase):
        x, y, c = _pos()
        me, xn, yn, dg = (x, y, c), (1 - x, y, c), (x, 1 - y, c), (1 - x, 1 - y, c)
        cps = []
        for a in range(len(self.arrs)):
            cp = lambda k, block, **kw: self._copy(ins, outs, sems, a, k, block, me, **kw)
            if phase == 1:
                cps += [cp(self.XN, xn), cp(self.YN, yn)]
            elif phase == 2:
                cps.append(cp(self.FWD_Y, dg, part=0))
                if self._split(a):
                    cps.append(cp(self.FWD_X, dg, part=1))
            else:
                cps += [cp(self.SIB, (x, y, 1 - c)), cp(self.SIB_X, (1 - x, y, 1 - c)),
                        cp(self.SIB_Y, (x, 1 - y, 1 - c)), cp(self.SIB_D, (1 - x, 1 - y, 1 - c))]
        return cps

    def start(self, ins, outs, sems):
        for cp in self._mine(ins, outs, sems) + self._sends(ins, outs, sems, 0):
            cp.start()

    def mid(self, ins, outs, sems):
        for cp in self._arrivals(ins, outs, sems, 1):
            cp.wait_recv()
        for cp in self._sends(ins, outs, sems, 1):
            cp.start()

    def mid2(self, ins, outs, sems):
        for cp in self._arrivals(ins, outs, sems, 2):
            cp.wait_recv()
        for cp in self._sends(ins, outs, sems, 2):
            cp.start()

    def finish(self, ins, outs, sems):
        for cp in self._arrivals(ins, outs, sems, 3):
            cp.wait_recv()
        for ph in range(3):
            for cp in self._sends(ins, outs, sems, ph):
                cp.wait_send()
        for cp in self._mine(ins, outs, sems):
            cp.wait()


def sc_comm(g, collective_id, name):
    srcs = [jax.new_ref(a, memory_space=pltpu.MemorySpace.HBM) for a in g.arrs]
    dsts = [jax.empty_ref(s, memory_space=pltpu.MemorySpace.HBM) for s in g.out_shape]

    @pl.kernel(mesh=plsc.ScalarSubcoreMesh(axis_name="sequencer", num_cores=1), name=name,
               scratch_types=tuple(g.scratch), compiler_params=pltpu.CompilerParams(collective_id=collective_id))
    def launch(*sems):
        peers = g.peers(*_pos())
        barrier = pltpu.get_barrier_semaphore()
        for peer in peers:
            pl.semaphore_signal(barrier, inc=1, device_id=peer, device_id_type=MESH)
        pl.semaphore_wait(barrier, len(peers))
        g.start(srcs, dsts, sems)
        for _, phase in g.mid_steps(1):
            phase(srcs, dsts, sems)
        g.finish(srcs, dsts, sems)

    launch()
    return [d[...] for d in dsts]


def sc_gather(arrs, collective_id, name):
    return sc_comm(BgGather(arrs), collective_id, name)


class BgPair:
    def __init__(self, arrs):
        n = len(arrs)
        self.arrs = list(arrs)
        self.out_shape = [SDS((4,) + a.shape[1:], a.dtype) for a in arrs]
        self.scratch = [pltpu.SemaphoreType.DMA((n, 4)), pltpu.SemaphoreType.DMA((n, 4))]
        self.result = None

    @staticmethod
    def peers(x, y, c):
        return [(x, y, 1 - c)]

    def mid_steps(self, nsteps):
        return []

    def _copies(self, ins, outs, sems):
        x, y, c = _pos()
        return [pltpu.make_async_remote_copy(
            src_ref=ins[a].at[2 * k + 1 - c], dst_ref=outs[a].at[k], send_sem=sems[0].at[a, k],
            recv_sem=sems[1].at[a, k], device_id=(x, y, 1 - c), device_id_type=MESH)
            for a in range(len(self.arrs)) for k in range(4)]

    def start(self, ins, outs, sems):
        for cp in self._copies(ins, outs, sems):
            cp.start()

    def finish(self, ins, outs, sems):
        cps = self._copies(ins, outs, sems)
        for cp in cps:
            cp.wait_recv()
        for cp in cps:
            cp.wait_send()


class BgChips(BgPair):
    def __init__(self, arrs):
        n = len(arrs)
        self.arrs = list(arrs)
        self.out_shape = [SDS((3,) + a.shape[1:], a.dtype) for a in arrs]
        self.scratch = [pltpu.SemaphoreType.DMA((n, 3)), pltpu.SemaphoreType.DMA((n, 3))]
        self.result = None

    @staticmethod
    def peers(x, y, c):
        return [(px, py, c) for px, py in _other_chips(x, y)]

    def _copies(self, ins, outs, sems):
        x, y, c = _pos()
        return [pltpu.make_async_remote_copy(
            src_ref=ins[a].at[2 * px + py], dst_ref=outs[a].at[r], send_sem=sems[0].at[a, r],
            recv_sem=sems[1].at[a, r], device_id=(px, py, c), device_id_type=MESH)
            for a in range(len(self.arrs)) for r, (px, py) in enumerate(_other_chips(x, y))]


def _call(bgs, body, *, name, grid, in_specs, out_specs, out_shape, scratch_shapes=(), compiler_params=None):
    single = not isinstance(out_shape, (list, tuple))
    out_specs_l = [out_specs] if single else list(out_specs)
    out_shape_l = [out_shape] if single else list(out_shape)
    bgs = [b for b in (bgs or []) if b is not None]
    n_in, n_out, n_sc = len(in_specs), len(out_shape_l), len(scratch_shapes)
    nsteps = math.prod(grid)
    b_in_specs = [b.in_specs(grid) if hasattr(b, "in_specs") else [_ANY] * len(b.arrs) for b in bgs]
    b_out_specs = [b.out_specs(grid) if hasattr(b, "out_specs") else [_ANY] * len(b.out_shape) for b in bgs]
    aliases, i_off, o_off = {}, n_in, n_out
    for b in bgs:
        aliases.update({i_off + i: o_off + o for i, o in getattr(b, "aliases", {}).items()})
        i_off, o_off = i_off + len(b.arrs), o_off + len(b.out_shape)

    def full(*refs):
        pos = [0]

        def take(k):
            r = refs[pos[0]:pos[0] + k]
            pos[0] += k
            return r

        ins = take(n_in)
        b_ins = [take(len(b.arrs)) for b in bgs]
        outs = take(n_out)
        b_outs = [take(len(b.out_shape)) for b in bgs]
        sc = take(n_sc)
        b_sc = [take(len(b.scratch)) for b in bgs]
        if bgs:
            step = pl.program_id(0)
            for d in range(1, len(grid)):
                step = step * grid[d] + pl.program_id(d)

            @pl.when(step == 0)
            def _():
                for b, i_, o_, s_ in zip(bgs, b_ins, b_outs, b_sc):
                    b.start(i_, o_, s_)

        body(*ins, *outs, *sc)
        if bgs:
            for b, i_, o_, s_ in zip(bgs, b_ins, b_outs, b_sc):
                if hasattr(b, "step"):
                    b.step(i_, o_, s_)
                for at, fn in b.mid_steps(nsteps):
                    @pl.when(step == at)
                    def _():
                        fn(i_, o_, s_)

            @pl.when(step == nsteps - 1)
            def _():
                for b, i_, o_, s_ in zip(bgs, b_ins, b_outs, b_sc):
                    b.finish(i_, o_, s_)

    def run(*args):
        res = pl.pallas_call(
            full, name=name, grid=grid,
            in_specs=list(in_specs) + [s for l in b_in_specs for s in l],
            out_specs=out_specs_l + [s for l in b_out_specs for s in l],
            out_shape=out_shape_l + [s for b in bgs for s in b.out_shape],
            scratch_shapes=list(scratch_shapes) + [s for b in bgs for s in b.scratch],
            input_output_aliases=aliases,
            compiler_params=compiler_params,
        )(*args, *[a for b in bgs for a in b.arrs])
        rest = list(res[n_out:])
        for b in bgs:
            b.result, rest = rest[:len(b.out_shape)], rest[len(b.out_shape):]
        return res[0] if single else list(res[:n_out])

    return run


def s5_discretize(a_re, a_im, log_dt, b_re, b_im, c_re, c_im):
    lam_r = jnp.minimum(a_re, DT_MIN_LAMBDA)
    lam_i = a_im
    dt = jnp.exp(log_dt)[:, None]
    e = jnp.exp(lam_r * dt)
    lbr = e * jnp.cos(lam_i * dt)
    lbi = e * jnp.sin(lam_i * dt)
    den = lam_r * lam_r + lam_i * lam_i
    cf_r = ((lbr - 1.0) * lam_r + lbi * lam_i) / den
    cf_i = (lbi * lam_r - (lbr - 1.0) * lam_i) / den
    bb_r = cf_r[:, :, None] * b_re - cf_i[:, :, None] * b_im
    bb_i = cf_r[:, :, None] * b_im + cf_i[:, :, None] * b_re
    eye = jnp.eye(8, dtype=f32)

    def blk_b(m):
        return jnp.einsum('bgpc,gh->bgchp', m.reshape(8, 8, S5_P, S5_C), eye).reshape(8, 128, 512)

    def blk_c(m):
        return jnp.einsum('bgcp,gh->bgphc', m.reshape(8, 8, S5_C, S5_P), eye).reshape(8, 512, 128)

    bm = jnp.concatenate([blk_b(bb_r), blk_b(bb_i)], axis=-1)
    cm = jnp.concatenate([blk_c(c_re), -blk_c(c_im)], axis=1)
    lam = jnp.stack([lbr.reshape(8, 512), lbi.reshape(8, 512)], axis=1)
    lam = jnp.broadcast_to(lam[:, :, None, :], (8, 2, 8, 512))
    return lam, bm, cm


def _cmul(ar, ai, br, bi):
    return ar * br - ai * bi, ar * bi + ai * br


def _shift_rows(v, k, up):
    row = lax.broadcasted_iota(jnp.int32, v.shape, 0)
    if up:
        return jnp.where(row < 8 - k, pltpu.roll(v, 8 - k, 0), 0.0)
    return jnp.where(row >= k, pltpu.roll(v, k, 0), 0.0)


_ROWS = 256


def s5_core_fwd(hn, bm, lam, cm, bg=()):
    nt = T // _ROWS

    def body(u_ref, b_ref, lam_ref, c_ref, ys_ref, S):
        lr, li = lam_ref[0], lam_ref[1]
        z = jnp.zeros((8, 512), f32)
        tile = lambda k: pl.ds(k * _ROWS, _ROWS)
        c = (z, z)
        for k in range(nt):
            S[tile(k), :] = _dot(_rows_in(u_ref, k).astype(bf16), b_ref[...])
            if k >= 1:
                c = _scan_tile(S, lr, li, k - 1, c, False, False)
        c = _scan_tile(S, lr, li, nt - 1, c, False, False)
        c = _chunk_starts(c[0], c[1], lr, li, False)
        for k in range(nt):
            c = _scan_tile(S, lr, li, k, c, False, True)
            if k >= 1:
                _rows_out(ys_ref, k - 1, _dot(S[tile(k - 1), :].astype(bf16), c_ref[...]))
        _rows_out(ys_ref, nt - 1, _dot(S[tile(nt - 1), :].astype(bf16), c_ref[...]))

    return _call(
        bg, body, name="s5_core_fwd", grid=(S5_SUB,),
        in_specs=[pl.BlockSpec((T, 128), lambda b: (0, b)),
                  pl.BlockSpec((None, 128, 1024), lambda b: (b, 0, 0)),
                  pl.BlockSpec((None, 4, 8, 512), lambda b: (b, 0, 0, 0)),
                  pl.BlockSpec((None, 1024, 128), lambda b: (b, 0, 0))],
        out_specs=[pl.BlockSpec((T, 128), lambda b: (0, b)), pl.BlockSpec((T, 1024), lambda b: (0, b))],
        out_shape=[SDS((T, D), f32), SDS((T, S5_SUB * 1024), f32)],
        compiler_params=_cp(dimension_semantics=("arbitrary",)),
    )(hn, bm, lam, cm)


_SEG = _ROWS // S5_CH


def _rows_in(ref, k):
    return jnp.concatenate([ref[pl.ds(s, S5_CH, stride=S5_STEPS), :] for s in range(k * _SEG, (k + 1) * _SEG)], axis=0)


def _rows_out(ref, k, val):
    for j, s in enumerate(range(k * _SEG, (k + 1) * _SEG)):
        ref[pl.ds(s, S5_CH, stride=S5_STEPS), :] = val[j * S5_CH:(j + 1) * S5_CH, :]


def _scan_tile(S, lr, li, k, carry, reverse, store, aux=None):
    steps = range(k * _SEG, (k + 1) * _SEG)
    for s in (reversed(steps) if reverse else steps):
        row = pl.ds(s * 8, 8)
        xr, xi = carry[0], carry[1]
        nr = lr * xr - li * xi + S[row, 0:512]
        ni = lr * xi + li * xr + S[row, 512:1024]
        if store:
            S[row, 0:512] = nr
            S[row, 512:1024] = ni
        if aux is not None and s >= 1:
            prow = pl.ds((s - 1) * 8, 8)
            pr, pi_ = aux[prow, 0:512], aux[prow, 512:1024]
            carry = (nr, ni, carry[2] + nr * pr + ni * pi_, carry[3] + ni * pr - nr * pi_)
        elif aux is not None:
            carry = (nr, ni, carry[2], carry[3])
        else:
            carry = (nr, ni)
    return carry


def _chunk_starts(er, ei, lr, li, reverse):
    ar, ai = lr, li
    for _ in range(8):
        ar, ai = _cmul(ar, ai, ar, ai)
    cr, ci = _shift_rows(er, 1, reverse), _shift_rows(ei, 1, reverse)
    for k in (1, 2, 4):
        sr, si = _shift_rows(cr, k, reverse), _shift_rows(ci, k, reverse)
        pr, pi_ = _cmul(ar, ai, sr, si)
        cr, ci = cr + pr, ci + pi_
        ar, ai = _cmul(ar, ai, ar, ai)
    return cr, ci


def s5_core_bwd(hn, dy, xs, bm, lam, cm, bg=()):
    nt = T // _ROWS

    def body(u_ref, dy_ref, S1, b_ref, lam_ref, c_ref, du_ref, db_ref, dct_ref, dlam_ref, S2):
        lcr, lci = lam_ref[2], lam_ref[3]
        z = jnp.zeros((8, 512), f32)
        tile = lambda k: pl.ds(k * _ROWS, _ROWS)

        def dx(k):
            dyb = _rows_in(dy_ref, k).astype(bf16)
            S2[tile(k), :] = _dot_nt(dyb, c_ref[...])
            dct_ref[...] += _dot_tn(dyb, S1[tile(k), :].astype(bf16))

        dct_ref[...] = jnp.zeros_like(dct_ref)
        dx(nt - 1)
        c = (z, z)
        for k in range(nt - 1, -1, -1):
            if k >= 1:
                dx(k - 1)
            c = _scan_tile(S2, lcr, lci, k, c, True, False)

        def dbu(k):
            gb = S2[tile(k), :].astype(bf16)
            db_ref[...] += _dot_tn(_rows_in(u_ref, k).astype(bf16), gb)
            _rows_out(du_ref, k, _dot_nt(gb, b_ref[...]))

        c = _chunk_starts(c[0], c[1], lcr, lci, True) + (z, z)
        db_ref[...] = jnp.zeros_like(db_ref)
        for k in range(nt - 1, -1, -1):
            c = _scan_tile(S2, lcr, lci, k, c, True, True, aux=S1)
            if k + 1 < nt:
                dbu(k + 1)
        dbu(0)
        gr, gi, dr, di = c
        last = pl.ds((S5_STEPS - 1) * 8, 8)
        xr = _shift_rows(S1[last, 0:512], 1, False)
        xi = _shift_rows(S1[last, 512:1024], 1, False)
        dlam_ref[0] = dr + gr * xr + gi * xi
        dlam_ref[1] = di + gi * xr - gr * xi

    return _call(
        bg, body, name="s5_core_bwd", grid=(S5_SUB,),
        in_specs=[pl.BlockSpec((T, 128), lambda b: (0, b)),
                  pl.BlockSpec((T, 128), lambda b: (0, b)),
                  pl.BlockSpec((T, 1024), lambda b: (0, b)),
                  pl.BlockSpec((None, 128, 1024), lambda b: (b, 0, 0)),
                  pl.BlockSpec((None, 4, 8, 512), lambda b: (b, 0, 0, 0)),
                  pl.BlockSpec((None, 1024, 128), lambda b: (b, 0, 0))],
        out_specs=[pl.BlockSpec((T, 128), lambda b: (0, b)),
                   pl.BlockSpec((None, 128, 1024), lambda b: (b, 0, 0)),
                   pl.BlockSpec((None, 128, 1024), lambda b: (b, 0, 0)),
                   pl.BlockSpec((None, 2, 8, 512), lambda b: (b, 0, 0, 0))],
        out_shape=[SDS((T, D), f32), SDS((8, 128, 1024), f32), SDS((8, 128, 1024), f32), SDS((8, 2, 8, 512), f32)],
        scratch_shapes=[pltpu.VMEM((T, 1024), f32)],
        compiler_params=_cp(dimension_semantics=("arbitrary",)),
    )(hn, dy, xs, bm, lam, cm)


TM = 512
NT = T // TM


def _tile(n=D):
    return pl.BlockSpec((TM, n), lambda i: (i, 0))


def s5_pre(xp, g):
    def body(x_ref, g_ref, hn_ref):
        hn_ref[...] = _rms(x_ref[...], g_ref[...])[0]

    return pl.pallas_call(
        body, name="s5_pre", grid=(NT,), in_specs=[_tile(), _full((1, D))], out_specs=_tile(),
        out_shape=SDS((T, D), f32), compiler_params=_cp(dimension_semantics=("arbitrary",)),
    )(xp, g)


def _gelu_grad(y):
    c = math.sqrt(2.0 / math.pi)
    t = jnp.tanh(c * (y + 0.044715 * y * y * y))
    return 0.5 * (1.0 + t) + 0.5 * y * (1.0 - t * t) * c * (1.0 + 3.0 * 0.044715 * y * y)


def s5_post(ys, xp, g, d, wglu, bglu, bg=()):
    def body(ys_ref, x_ref, g_ref, d_ref, w_ref, b_ref, y_ref, z_ref, h_ref):
        x = x_ref[...]
        hn, _ = _rms(x, g_ref[...])
        y = ys_ref[...] + d_ref[...] * hn
        y_ref[...] = y
        yg = jax.nn.gelu(y).astype(bf16)
        for j in range(4):
            cv = slice(j * 256, (j + 1) * 256)
            cg = slice(1024 + j * 256, 1024 + (j + 1) * 256)
            val = _dot(yg, w_ref[j]) + b_ref[:, cv]
            gate = _dot(yg, w_ref[j + 4]) + b_ref[:, cg]
            z_ref[:, cv] = val
            z_ref[:, cg] = gate
            h_ref[:, cv] = x[:, cv] + val * jax.nn.sigmoid(gate)

    return _call(
        bg, body, name="s5_post", grid=(NT,),
        in_specs=[_tile(), _tile(), _full((1, D)), _full((1, D)), _full((8, D, 256)), _full((1, 2 * D))],
        out_specs=[_tile(), _tile(2 * D), _tile()],
        out_shape=[SDS((T, D), f32), SDS((T, 2 * D), f32), SDS((T, D), f32)],
        compiler_params=_cp(dimension_semantics=("arbitrary",)),
    )(ys, xp, g, d, wglu, bglu)


def s5_post_bwd(dh, y, z, wglu, bg=()):
    def body(dh_ref, y_ref, z_ref, w_ref, dy_ref, dw_ref, db_ref, acc):
        i = pl.program_id(0)

        @pl.when(i == 0)
        def _():
            acc[...] = jnp.zeros_like(acc)
            db_ref[...] = jnp.zeros_like(db_ref)

        dh_ = dh_ref[...]
        y = y_ref[...]
        yg = jax.nn.gelu(y).astype(bf16)
        dyg = jnp.zeros((TM, D), f32)
        for j in range(4):
            cv = slice(j * 256, (j + 1) * 256)
            cg = slice(1024 + j * 256, 1024 + (j + 1) * 256)
            val = z_ref[:, cv]
            sg = jax.nn.sigmoid(z_ref[:, cg])
            dval = dh_[:, cv] * sg
            dgate = dh_[:, cv] * val * sg * (1.0 - sg)
            db_ref[:, cv] += _colsum8(dval)
            db_ref[:, cg] += _colsum8(dgate)
            dvb = dval.astype(bf16)
            dgb = dgate.astype(bf16)
            acc[j] += _dot_tn(yg, dvb)
            acc[j + 4] += _dot_tn(yg, dgb)
            dyg = dyg + _dot_nt(dvb, w_ref[j]) + _dot_nt(dgb, w_ref[j + 4])
        dy_ref[...] = dyg * _gelu_grad(y)

        @pl.when(i == NT - 1)
        def _():
            dw_ref[...] = acc[...].astype(bf16)

    return _call(
        bg, body, name="s5_post_bwd", grid=(NT,),
        in_specs=[_tile(), _tile(), _tile(2 * D), _full((8, D, 256))],
        out_specs=[_tile(), _full((8, D, 256)), _full((8, 2 * D))],
        out_shape=[SDS((T, D), f32), SDS((8, D, 256), bf16), SDS((8, 2 * D), f32)],
        scratch_shapes=[pltpu.VMEM((8, D, 256), f32)],
        compiler_params=_cp(dimension_semantics=("arbitrary",)),
    )(dh, y, z, wglu)


def s5_pre_bwd(xp, g, du, dy, d, dh, bg=()):
    def body(x_ref, g_ref, du_ref, dy_ref, d_ref, dh_ref, dx_ref, dg_ref, dd_ref):
        i = pl.program_id(0)

        @pl.when(i == 0)
        def _():
            dg_ref[...] = jnp.zeros_like(dg_ref)
            dd_ref[...] = jnp.zeros_like(dd_ref)

        x = x_ref[...]
        g = g_ref[...]
        dy = dy_ref[...]
        hn, _ = _rms(x, g)
        dhn = du_ref[...] + d_ref[...] * dy
        dx, dgt = _rms_bwd(x, g, dhn)
        dx_ref[...] = dh_ref[...] + dx
        dg_ref[...] += _colsum8(dgt)
        dd_ref[...] += _colsum8(dy * hn)

    return _call(
        bg, body, name="s5_pre_bwd", grid=(NT,),
        in_specs=[_tile(), _full((1, D)), _tile(), _tile(), _full((1, D)), _tile()],
        out_specs=[_tile(), _full((8, D)), _full((8, D))],
        out_shape=[SDS((T, D), f32), SDS((8, D), f32), SDS((8, D), f32)],
        compiler_params=_cp(dimension_semantics=("arbitrary",)),
    )(xp, g, du, dy, d, dh)


TMF = 1024


def mlp_fwd(h, g, w_in, w_out, layer, bg=()):
    def body(h_ref, g_ref, wi_ref, wo_ref, hm_ref, r_ref, out_ref, acc):
        j = pl.program_id(1)

        @pl.when(j == 0)
        def _():
            hm, _ = _rms(h_ref[...], g_ref[...])
            hm_ref[...] = hm.astype(bf16)
            acc[...] = jnp.zeros_like(acc)

        a = jnp.maximum(_dot(hm_ref[...], wi_ref[...]), 0.0)
        r_ref[...] = a.astype(bf16)
        acc[...] += _dot((a * a).astype(bf16), wo_ref[...])

        @pl.when(j == NDEV - 1)
        def _():
            out_ref[...] = h_ref[...] + acc[...]

    return _call(
        bg, body, name=f"mlp_fwd{layer}", grid=(T // TMF, NDEV),
        in_specs=[pl.BlockSpec((TMF, D), lambda i, j: (i, 0)),
                  pl.BlockSpec((1, D), lambda i, j: (0, 0)),
                  pl.BlockSpec((None, D, D_FF_SHARD), lambda i, j: (j, 0, 0)),
                  pl.BlockSpec((None, D_FF_SHARD, D), lambda i, j: (j, 0, 0))],
        out_specs=[pl.BlockSpec((TMF, D), lambda i, j: (i, 0)), pl.BlockSpec((TMF, D_FF_SHARD), lambda i, j: (i, j)),
                   pl.BlockSpec((TMF, D), lambda i, j: (i, 0))],
        out_shape=[SDS((T, D), bf16), SDS((T, NDEV * D_FF_SHARD), bf16), SDS((T, D), f32)],
        scratch_shapes=[pltpu.VMEM((TMF, D), f32)],
        compiler_params=_cp(dimension_semantics=("arbitrary", "arbitrary")),
    )(h, g, w_in, w_out)


def mlp_bwd(h, hm, r, g, dout, dout_b, w_in, w_out, layer, bg=()):
    last = NDEV - 1

    def body(h_ref, hm_ref, r_ref, g_ref, do_ref, dob_ref, wi_ref, wo_ref, dh_ref, dwi_ref, dwo_ref, dg_ref,
             dhm, awi, awo):
        j = pl.program_id(0)
        i = pl.program_id(1)
        rows = pl.ds(pl.multiple_of(i * TM, TM), TM)

        @pl.when(i == 0)
        def _():
            awi[...] = jnp.zeros_like(awi)
            awo[...] = jnp.zeros_like(awo)

        dz = (_dot_nt(dob_ref[...], wo_ref[...]) * (2.0 * r_ref[...].astype(f32))).astype(bf16)
        rb = r_ref[...]
        awo[...] += _dot_tn(rb * rb, dob_ref[...])
        awi[...] += _dot_tn(hm_ref[...], dz)
        part = _dot_nt(dz, wi_ref[...])

        @pl.when(j == 0)
        def _():
            dhm[rows, :] = part

        @pl.when(j > 0)
        def _():
            dhm[rows, :] += part

        @pl.when(i == NT - 1)
        def _():
            dwi_ref[...] = awi[...].astype(bf16)
            dwo_ref[...] = awo[...].astype(bf16)

        @pl.when(j == last)
        def _():
            @pl.when(i == 0)
            def _():
                dg_ref[...] = jnp.zeros_like(dg_ref)
            dx, dgt = _rms_bwd(h_ref[...], g_ref[...], dhm[rows, :])
            dh_ref[...] = do_ref[...] + dx
            dg_ref[...] += _colsum8(dgt)

    late = lambda j, i: (jnp.where(j == last, i, 0), 0)
    return _call(
        bg, body, name=f"mlp_bwd{layer}", grid=(NDEV, NT),
        in_specs=[pl.BlockSpec((TM, D), late),
                  pl.BlockSpec((TM, D), lambda j, i: (i, 0)),
                  pl.BlockSpec((TM, D_FF_SHARD), lambda j, i: (i, j)),
                  pl.BlockSpec((1, D), lambda j, i: (0, 0)),
                  pl.BlockSpec((TM, D), late),
                  pl.BlockSpec((TM, D), lambda j, i: (i, 0)),
                  pl.BlockSpec((None, D, D_FF_SHARD), lambda j, i: (j, 0, 0)),
                  pl.BlockSpec((None, D_FF_SHARD, D), lambda j, i: (j, 0, 0))],
        out_specs=[pl.BlockSpec((TM, D), late),
                   pl.BlockSpec((None, D, D_FF_SHARD), lambda j, i: (j, 0, 0)),
                   pl.BlockSpec((None, D_FF_SHARD, D), lambda j, i: (j, 0, 0)),
                   pl.BlockSpec((8, D), lambda j, i: (0, 0))],
        out_shape=[SDS((T, D), f32), SDS((NDEV, D, D_FF_SHARD), bf16), SDS((NDEV, D_FF_SHARD, D), bf16),
                   SDS((8, D), f32)],
        scratch_shapes=[pltpu.VMEM((T, D), f32), pltpu.VMEM((D, D_FF_SHARD), f32), pltpu.VMEM((D_FF_SHARD, D), f32)],
        compiler_params=_cp(dimension_semantics=("arbitrary", "arbitrary")),
    )(h, hm, r, g, dout, dout_b, w_in, w_out)


def _spread4():
    r = lax.broadcasted_iota(jnp.int32, (256, D), 0)
    c = lax.broadcasted_iota(jnp.int32, (256, D), 1)
    return ((c // 256 == r // HEAD_DIM) & (c % HEAD_DIM == r % HEAD_DIM)).astype(bf16)


def attn_pre(h, g_kv, g_mix, wkv, bkv, spread, wq, bq):
    def body(h_ref, gkv_ref, gm_ref, wkv_ref, bkv_ref, sp_ref, wq_ref, bq_ref, kvn_ref, hn_ref, k_ref, v_ref, q_ref):
        h_ = h_ref[...]
        kvn = _rms(h_, gkv_ref[...])[0].astype(bf16)
        hn = _rms(h_, gm_ref[...])[0].astype(bf16)
        kvn_ref[...] = kvn
        hn_ref[...] = hn
        kv = (_dot(kvn, wkv_ref[...]) + bkv_ref[...]).astype(bf16)
        k_ref[...] = _dot(kv[:, :256], sp_ref[...]).astype(bf16)
        v_ref[...] = _dot(kv[:, 256:], sp_ref[...]).astype(bf16)
        q_ref[...] = (_dot(hn, wq_ref[...]) + bq_ref[...]).astype(bf16)

    return pl.pallas_call(
        body, name="attn_pre", grid=(NT,),
        in_specs=[_tile(), _full((1, D)), _full((1, D)), _full((D, 512)), _full((1, 512)), _full((256, D)),
                  _full((D, D)), _full((1, D))],
        out_specs=[_tile()] * 5,
        out_shape=[SDS((T, D), bf16)] * 5,
        compiler_params=_cp(dimension_semantics=("arbitrary",)),
    )(h, g_kv, g_mix, wkv, bkv, spread, wq, bq)


def _attn_specs():
    cur = pl.BlockSpec((TM, 256), lambda j, n: (n, j))
    prev = pl.BlockSpec((BLK, 256), lambda j, n: (jnp.maximum(n * (TM // BLK) - 1, 0), j))
    return cur, prev


def _head_mask(g):
    lane = lax.broadcasted_iota(jnp.int32, (1, 256), 1)
    return (lane >= g * HEAD_DIM) & (lane < (g + 1) * HEAD_DIM)


def _stack_heads(t):
    return jnp.concatenate([jnp.where(_head_mask(g), t, 0) for g in range(Q_PER_KV)], axis=0)


def _unstack_heads(t):
    out = jnp.where(_head_mask(0), t[0:BLK], 0.0)
    for g in range(1, Q_PER_KV):
        out = out + jnp.where(_head_mask(g), t[g * BLK:(g + 1) * BLK], 0.0)
    return out


def _attn_probs(qs, k2, sinks, first):
    rows = Q_PER_KV * BLK
    s = _dot_nt(qs, k2) * (1.0 / math.sqrt(HEAD_DIM))
    qi = jnp.bitwise_and(lax.broadcasted_iota(jnp.int32, (rows, 2 * BLK), 0), BLK - 1)
    kj = lax.broadcasted_iota(jnp.int32, (rows, 2 * BLK), 1)
    diff = qi + BLK - kj
    valid = (diff >= 0) & (diff < BLK) & (jnp.logical_not(first) | (kj >= BLK))
    s = jnp.where(valid, s, -jnp.inf)
    rb = lax.broadcasted_iota(jnp.int32, (rows, 1), 0)
    sink = jnp.where(rb < BLK, sinks[0], jnp.where(rb < 2 * BLK, sinks[1], jnp.where(rb < 3 * BLK, sinks[2], sinks[3])))
    m = jnp.maximum(jnp.max(s, axis=-1, keepdims=True), sink)
    p = jnp.exp(s - m)
    ps = jnp.exp(sink - m)
    denom = jnp.sum(p, axis=-1, keepdims=True) + ps
    return p / denom, ps / denom


def _window_blocks(b, n, kc_ref, kp_ref, vc_ref, vp_ref):
    if b == 0:
        return (jnp.concatenate([kp_ref[...], kc_ref[0:BLK, :]], axis=0),
                jnp.concatenate([vp_ref[...], vc_ref[0:BLK, :]], axis=0), n == 0)
    rows = pl.ds((b - 1) * BLK, 2 * BLK)
    return kc_ref[rows, :], vc_ref[rows, :], False


def attn_core_fwd(q, k4, v4, sinks, bg=()):
    nb = TM // BLK

    def body(sink_ref, q_ref, kc_ref, kp_ref, vc_ref, vp_ref, o_ref, a_ref, as_ref):
        j = pl.program_id(0)
        n = pl.program_id(1)
        sk = [sink_ref[j * Q_PER_KV + g] for g in range(Q_PER_KV)]
        for b in range(nb):
            qb = q_ref[b * BLK:(b + 1) * BLK, :]
            k2, v2, first = _window_blocks(b, n, kc_ref, kp_ref, vc_ref, vp_ref)
            a, asink = _attn_probs(_stack_heads(qb), k2, sk, first)
            ab = a.astype(bf16)
            a_ref[b] = ab
            as_ref[b] = jnp.broadcast_to(asink, (Q_PER_KV * BLK, 128)).astype(bf16)
            o_ref[b * BLK:(b + 1) * BLK, :] = _unstack_heads(_dot(ab, v2)).astype(bf16)

    cur, prev = _attn_specs()
    rows = Q_PER_KV * BLK
    return _call(
        bg, body, name="attn_core_fwd", grid=(N_KV, NT),
        in_specs=[pl.BlockSpec(memory_space=pltpu.SMEM), cur, cur, prev, cur, prev],
        out_specs=[cur, pl.BlockSpec((None, nb, rows, 2 * BLK), lambda j, n: (j, n, 0, 0)),
                   pl.BlockSpec((None, nb, rows, 128), lambda j, n: (j, n, 0, 0))],
        out_shape=[SDS((T, D), bf16), SDS((N_KV, T // BLK, rows, 2 * BLK), bf16), SDS((N_KV, T // BLK, rows, 128), bf16)],
        compiler_params=_cp(dimension_semantics=("arbitrary", "arbitrary")),
    )(sinks, q, k4, k4, v4, v4)


def attn_post(h, o, wo, bo):
    def body(h_ref, o_ref, w_ref, b_ref, out_ref):
        out_ref[...] = h_ref[...] + _dot(o_ref[...], w_ref[...]) + b_ref[...]

    return pl.pallas_call(
        body, name="attn_post", grid=(NT,), in_specs=[_tile(), _tile(), _full((D, D)), _full((1, D))],
        out_specs=_tile(), out_shape=SDS((T, D), f32), compiler_params=_cp(dimension_semantics=("arbitrary",)),
    )(h, o, wo, bo)


def attn_bwd_pre(dh, o, wo, bg=()):
    def body(dh_ref, o_ref, w_ref, do_ref, dw_ref, db_ref, acc):
        i = pl.program_id(0)

        @pl.when(i == 0)
        def _():
            acc[...] = jnp.zeros_like(acc)
            db_ref[...] = jnp.zeros_like(db_ref)

        dh_ = dh_ref[...]
        dhb = dh_.astype(bf16)
        do_ref[...] = _dot_nt(dhb, w_ref[...]).astype(bf16)
        acc[...] += _dot_tn(o_ref[...], dhb)
        db_ref[...] += _colsum8(dh_)

        @pl.when(i == NT - 1)
        def _():
            dw_ref[...] = acc[...].astype(bf16)

    return _call(
        bg, body, name="attn_bwd_pre", grid=(NT,), in_specs=[_tile(), _tile(), _full((D, D))],
        out_specs=[_tile(), _full((D, D)), _full((8, D))],
        out_shape=[SDS((T, D), bf16), SDS((D, D), bf16), SDS((8, D), f32)],
        scratch_shapes=[pltpu.VMEM((D, D), f32)],
        compiler_params=_cp(dimension_semantics=("arbitrary",)),
    )(dh, o, wo)


def attn_core_bwd(q, do, k4, v4, probs, sink_w, bg=()):
    nb = TM // BLK

    def body(q_ref, do_ref, kc_ref, kp_ref, vc_ref, vp_ref, a_ref, as_ref, dq_ref, dk_ref, dv_ref, ds_ref):
        j = pl.program_id(0)
        n = pl.program_id(1)

        @pl.when(n == 0)
        def _():
            dk_ref[...] = jnp.zeros_like(dk_ref)
            dv_ref[...] = jnp.zeros_like(dv_ref)
            ds_ref[...] = jnp.zeros_like(ds_ref)

        lane8 = lax.broadcasted_iota(jnp.int32, (8, 128), 1)
        row8 = lax.broadcasted_iota(jnp.int32, (8, 128), 0)
        for b in range(nb):
            qs = _stack_heads(q_ref[b * BLK:(b + 1) * BLK, :])
            dos = _stack_heads(do_ref[b * BLK:(b + 1) * BLK, :])
            k2, v2, _ = _window_blocks(b, n, kc_ref, kp_ref, vc_ref, vp_ref)
            ab = a_ref[b]
            a = ab.astype(f32)
            asink = as_ref[b][:, 0:1].astype(f32)
            dp = _dot_nt(dos, v2)
            dd = jnp.sum(a * dp, axis=-1, keepdims=True)
            dsc = (a * (dp - dd) * (1.0 / math.sqrt(HEAD_DIM))).astype(bf16)
            t = asink * dd
            for g in range(Q_PER_KV):
                dsink = -jnp.sum(t[g * BLK:(g + 1) * BLK], axis=0, keepdims=True)
                ds_ref[...] += jnp.where((lane8 == g) & (row8 == 0), jnp.broadcast_to(dsink, (8, 128)), 0.0)
            dq_ref[b * BLK:(b + 1) * BLK, :] = _unstack_heads(_dot(dsc, k2))
            dk2 = _dot_tn(dsc, qs)
            dv2 = _dot_tn(ab, dos)
            cur = pl.ds(pl.multiple_of(n * TM + b * BLK, BLK), BLK)
            dk_ref[cur, :] += dk2[BLK:, :]
            dv_ref[cur, :] += dv2[BLK:, :]
            if b == 0:
                @pl.when(n > 0)
                def _():
                    prv = pl.ds(pl.multiple_of(n * TM - BLK, BLK), BLK)
                    dk_ref[prv, :] += dk2[:BLK, :]
                    dv_ref[prv, :] += dv2[:BLK, :]
            else:
                prv = pl.ds(pl.multiple_of(n * TM + (b - 1) * BLK, BLK), BLK)
                dk_ref[prv, :] += dk2[:BLK, :]
                dv_ref[prv, :] += dv2[:BLK, :]

    cur, prev = _attn_specs()
    col = pl.BlockSpec((T, 256), lambda j, n: (0, j))
    rows = Q_PER_KV * BLK
    return _call(
        bg, body, name="attn_core_bwd", grid=(N_KV, NT),
        in_specs=[cur, cur, cur, prev, cur, prev,
                  pl.BlockSpec((None, nb, rows, 2 * BLK), lambda j, n: (j, n, 0, 0)),
                  pl.BlockSpec((None, nb, rows, 128), lambda j, n: (j, n, 0, 0))],
        out_specs=[cur, col, col, pl.BlockSpec((None, 8, 128), lambda j, n: (j, 0, 0))],
        out_shape=[SDS((T, D), f32), SDS((T, D), f32), SDS((T, D), f32), SDS((N_KV, 8, 128), f32)],
        compiler_params=_cp(dimension_semantics=("arbitrary", "arbitrary")),
    )(q, do, k4, k4, v4, v4, probs, sink_w)


def attn_bwd_q(h, dh, dq, hn, g_mix, wq):
    def body(h_ref, dh_ref, dq_ref, hn_ref, gm_ref, wq_ref, out_ref, dwq_ref, dbq_ref, dgm_ref, aq):
        i = pl.program_id(0)

        @pl.when(i == 0)
        def _():
            aq[...] = jnp.zeros_like(aq)
            dbq_ref[...] = jnp.zeros_like(dbq_ref)
            dgm_ref[...] = jnp.zeros_like(dgm_ref)

        dq_ = dq_ref[...]
        dqb = dq_.astype(bf16)
        aq[...] += _dot_tn(hn_ref[...], dqb)
        dbq_ref[...] += _colsum8(dq_)
        dx, dg = _rms_bwd(h_ref[...], gm_ref[...], _dot_nt(dqb, wq_ref[...]))
        out_ref[...] = dh_ref[...] + dx
        dgm_ref[...] += _colsum8(dg)

        @pl.when(i == NT - 1)
        def _():
            dwq_ref[...] = aq[...].astype(bf16)

    vec = _full((8, D))
    mat = _full((D, D))
    return pl.pallas_call(
        body, name="attn_bwd_q", grid=(NT,),
        in_specs=[_tile()] * 4 + [_full((1, D)), mat],
        out_specs=[_tile(), mat, vec, vec],
        out_shape=[SDS((T, D), f32), SDS((D, D), bf16), SDS((8, D), f32), SDS((8, D), f32)],
        scratch_shapes=[pltpu.VMEM((D, D), f32)],
        compiler_params=_cp(dimension_semantics=("arbitrary",)),
    )(h, dh, dq, hn, g_mix, wq)


def attn_bwd_kv(h, dh, dk4, dv4, kvn, g_kv, wkv, spread):
    def body(h_ref, dh_ref, dk_ref, dv_ref, kvn_ref, gkv_ref, wkv_ref, sp_ref, out_ref, outb_ref, dw_ref, db_ref,
             dgkv_ref, acc):
        i = pl.program_id(0)

        @pl.when(i == 0)
        def _():
            for r in (acc, db_ref, dgkv_ref):
                r[...] = jnp.zeros_like(r)

        dkv = jnp.concatenate([_dot_nt(dk_ref[...].astype(bf16), sp_ref[...]),
                               _dot_nt(dv_ref[...].astype(bf16), sp_ref[...])], axis=1)
        dkvb = dkv.astype(bf16)
        acc[...] += _dot_tn(kvn_ref[...], dkvb)
        db_ref[...] += _colsum8(dkv)
        dx, dg = _rms_bwd(h_ref[...], gkv_ref[...], _dot_nt(dkvb, wkv_ref[...]))
        out = dh_ref[...] + dx
        out_ref[...] = out
        outb_ref[...] = out.astype(bf16)
        dgkv_ref[...] += _colsum8(dg)

        @pl.when(i == NT - 1)
        def _():
            dw_ref[...] = acc[...].astype(bf16)

    return pl.pallas_call(
        body, name="attn_bwd_kv", grid=(NT,),
        in_specs=[_tile()] * 5 + [_full((1, D)), _full((D, 512)), _full((256, D))],
        out_specs=[_tile(), _tile(), _full((D, 512)), _full((8, 512)), _full((8, D))],
        out_shape=[SDS((T, D), f32), SDS((T, D), bf16), SDS((D, 512), bf16), SDS((8, 512), f32), SDS((8, D), f32)],
        scratch_shapes=[pltpu.VMEM((D, 512), f32)],
        compiler_params=_cp(dimension_semantics=("arbitrary",)),
    )(h, dh, dk4, dv4, kvn, g_kv, wkv, spread)


def final_loss(h, g, target):
    def body(h_ref, g_ref, t_ref, loss_ref, dh_ref, dhb_ref, dg_ref):
        i = pl.program_id(0)

        @pl.when(i == 0)
        def _():
            loss_ref[...] = jnp.zeros_like(loss_ref)
            dg_ref[...] = jnp.zeros_like(dg_ref)

        h_ = h_ref[...]
        g_ = g_ref[...]
        y, _ = _rms(h_, g_)
        diff = y - t_ref[...]
        per_tok = jnp.mean(diff * diff, axis=-1, keepdims=True)
        tot = 0.5 * jnp.sum(per_tok, axis=0, keepdims=True)
        lane = lax.broadcasted_iota(jnp.int32, (8, 128), 1)
        row = lax.broadcasted_iota(jnp.int32, (8, 128), 0)
        loss_ref[...] += jnp.where((lane == 0) & (row == 0), jnp.broadcast_to(tot, (8, 128)), 0.0)
        dx, dgt = _rms_bwd(h_, g_, diff * (1.0 / D))
        dh_ref[...] = dx
        dhb_ref[...] = dx.astype(bf16)
        dg_ref[...] += _colsum8(dgt)

    return pl.pallas_call(
        body, name="final_loss", grid=(NT,), in_specs=[_tile(), _full((1, D)), _tile()],
        out_specs=[_full((8, 128)), _tile(), _tile(), _full((8, D))],
        out_shape=[SDS((8, 128), f32), SDS((T, D), f32), SDS((T, D), bf16), SDS((8, D), f32)],
        compiler_params=_cp(dimension_semantics=("arbitrary",)),
    )(h, g, target)


def fwd_bwd(x, target, p, shards, opt, core, chip):
    row = lambda v: v.reshape(1, -1)
    (lam, bm, cm), prep_vjp = jax.vjp(s5_discretize, p["s5_a_re"][0], p["s5_a_im"][0], p["s5_log_dt"][0],
                                      p["s5_b_re"][0], p["s5_b_im"][0], p["s5_c_re"][0], p["s5_c_im"][0])
    bmb, cmb = bm.astype(bf16), cm.astype(bf16)
    lam = jnp.concatenate([lam, lam * jnp.array([1.0, -1.0], f32).reshape(1, 2, 1, 1)], axis=1)
    g_mix0, g_mix1 = row(p["norm_mix"][0]), row(p["norm_mix"][1])
    g_mlp0, g_mlp1 = row(p["norm_mlp"][0]), row(p["norm_mlp"][1])
    g_kv, g_fin = row(p["norm_kv"]), row(p["norm_final"])
    bq, bo = p["b_q"], p["b_o"]
    bkv = row(p["b_kv"])
    spread = _spread4()
    sinks = p["sinks"].reshape(16)

    wglu, gvec = sc_gather([shards["s5_w_glu"], shards["vecs"]], 3, "sc_gather_s5")
    win0, wout0 = sc_gather([shards["w_in0"], shards["w_out0"]], 14, "sc_gather_mlp0")
    wkv, wq, wo = sc_gather([shards["w_kv"], shards["w_q"], shards["w_o"]], 4, "sc_gather_attn")
    win1, wout1 = sc_gather([shards["w_in1"], shards["w_out1"]], 5, "sc_gather_mlp1")
    xp = x
    hn0 = s5_pre(xp, g_mix0)
    ys, xs = s5_core_fwd(hn0, bmb, lam, cmb)
    d_skip = gvec[:, 0, :128].reshape(1, D)
    bglu = gvec[:, 0, 128:].reshape(1, 2 * D)
    y, z, h1 = s5_post(ys, xp, g_mix0, d_skip, wglu, bglu)
    hm0, r0, h2p = mlp_fwd(h1, g_mlp0, win0, wout0, 0)
    wkv, wq, wo = wkv.reshape(D, 512), wq.reshape(D, D), wo.reshape(D, D)
    h2 = h2p
    kvn, hn1, k4, v4, q = attn_pre(h2, g_kv, g_mix1, wkv, bkv, spread, wq, bq)
    o, probs, sink_w = attn_core_fwd(q, k4, v4, sinks)
    h3 = attn_post(h2, o, wo, bo)
    hm1, r1, h4 = mlp_fwd(h3, g_mlp1, win1, wout1, 1)
    loss, dh4, dh4b, dg_fin = final_loss(h4, g_fin, target)

    def pair_sums(names, grads, cid, before):
        r1 = sc_comm(BgPair(grads), cid, "sc_pair_" + names[0])
        parts = [add_pairs(g, r, core, f"add_pairs_{n}") for n, g, r in zip(names, grads, r1)]
        before, parts = lax.optimization_barrier((before, parts))
        return before, parts

    def across_chips(names, parts, cid):
        return list(zip(parts, sc_comm(BgChips(parts), cid, "sc_chips_" + names[0])))

    dh3, dwin1, dwout1, dg_mlp1 = mlp_bwd(h3, hm1, r1, g_mlp1, dh4, dh4b, win1, wout1, 1)
    do, dwo, dbo = attn_bwd_pre(dh3, o, wo)
    do, parts = pair_sums(["w_in1", "w_out1"], [dwin1, dwout1], 6, do)
    rs_in1, rs_out1 = across_chips(["w_in1", "w_out1"], parts, 7)
    dq, dk4, dv4, dsink = attn_core_bwd(q, do, k4, v4, probs, sink_w)
    dh2, dwq, dbq, dg_mix1 = attn_bwd_q(h2, dh3, dq, hn1, g_mix1, wq)
    dh2, dh2b, dwkv, dbkv, dg_kv = attn_bwd_kv(h2, dh2, dk4, dv4, kvn, g_kv, wkv, spread)
    dh2p, dh2pb = dh2, dh2b
    big = {}
    a_in1 = adam_big(*opt["w_mlp_in"], *rs_in1, chip, "adam_w_mlp_in1", layer=1)
    a_out1 = adam_big(*opt["w_mlp_out"], *rs_out1, chip, "adam_w_mlp_out1", layer=1)
    dh2p, a_in1, a_out1 = lax.optimization_barrier((dh2p, a_in1, a_out1))
    names = ["w_kv", "w_q", "w_o"]
    dh2p, parts = pair_sums(names, [dwkv.reshape(NDEV, 128, 512), dwq.reshape(NDEV, 128, D),
                                    dwo.reshape(NDEV, 128, D)], 8, dh2p)
    rs_attn = across_chips(names, parts, 9)
    dh1, dwin0, dwout0, dg_mlp0 = mlp_bwd(h1, hm0, r0, g_mlp0, dh2p, dh2pb, win0, wout0, 0)
    a_attn = [adam_big(*opt[n], *rs, chip, f"adam_{n}") for n, rs in zip(names, rs_attn)]
    dh1, a_attn = lax.optimization_barrier((dh1, a_attn))
    big.update(zip(names, a_attn))
    dy, dwglu, dbglu = s5_post_bwd(dh1, y, z, wglu)
    dy, parts = pair_sums(["w_in0", "w_out0"], [dwin0, dwout0], 10, dy)
    rs_in0, rs_out0 = across_chips(["w_in0", "w_out0"], parts, 11)
    du, dbm, dcmt, dlam = s5_core_bwd(hn0, dy, xs, bmb, lam, cmb)
    du, parts = pair_sums(["s5_w_glu"], [dwglu], 12, du)
    rs_glu, = across_chips(["s5_w_glu"], parts, 13)
    dxp, dg_mix0, dd = s5_pre_bwd(xp, g_mix0, du, dy, d_skip, dh1)
    big["w_mlp_in"] = adam_big(*opt["w_mlp_in"], *rs_in0, chip, "adam_w_mlp_in0", layer=0, prev=a_in1)
    big["w_mlp_out"] = adam_big(*opt["w_mlp_out"], *rs_out0, chip, "adam_w_mlp_out0", layer=0, prev=a_out1)
    big["s5_w_glu"] = adam_big(*opt["s5_w_glu"], *rs_glu, chip, "adam_s5_w_glu")
    grad_x = dxp
    da_re, da_im, dlog_dt, db_re, db_im, dc_re, dc_im = prep_vjp((dlam, dbm, dcmt.transpose(0, 2, 1)))

    def lanes(v_):
        v_ = v_.reshape(1, -1)
        return jnp.pad(v_, ((0, 0), (0, D - v_.shape[1])))

    small = jnp.concatenate([
        dg_mix0[0:1], dg_mix1[0:1], dg_mlp0[0:1], dg_mlp1[0:1], dg_kv[0:1], dg_fin[0:1], dd[0:1], dbq[0:1], dbo[0:1],
        dbglu[0:1].reshape(2, D), lanes(dbkv[0:1]),
        lanes(dsink[:, 0, :Q_PER_KV]), lanes(dlog_dt), lanes(loss[0:1, 0:1]), jnp.zeros((1, D), f32),
        da_re.reshape(4, D), da_im.reshape(4, D),
        db_re.transpose(0, 2, 1).reshape(64, D), db_im.transpose(0, 2, 1).reshape(64, D),
        dc_re.reshape(64, D), dc_im.reshape(64, D)], axis=0)
    small, big["w_mlp_in"], big["w_mlp_out"] = lax.optimization_barrier((small, big["w_mlp_in"], big["w_mlp_out"]))
    return loss, grad_x, small, big


def _row_tile(r, c):
    return min(r, max(8, (512 * 1024) // c))


def add_pairs(g, r1, core, name):
    _, R, C = g.shape
    tr = _row_tile(R, C)

    def body(core_ref, g_ref, r_ref, o_ref):
        o_ref[...] = (g_ref[...].astype(f32) + r_ref[...].astype(f32)).astype(bf16)

    return pl.pallas_call(
        body, name=name, out_shape=SDS((4, R, C), bf16),
        grid_spec=pltpu.PrefetchScalarGridSpec(
            num_scalar_prefetch=1, grid=(4, R // tr),
            in_specs=[pl.BlockSpec((None, tr, C), lambda k, i, core: (2 * k + core[0], i, 0)),
                      pl.BlockSpec((None, tr, C), lambda k, i, core: (k, i, 0))],
            out_specs=pl.BlockSpec((None, tr, C), lambda k, i, core: (k, i, 0))),
        compiler_params=_cp(dimension_semantics=("arbitrary", "arbitrary")),
    )(core, g, r1)


def _adamw(w, g, m, v):
    m = ADAM_B1 * m + (1.0 - ADAM_B1) * g
    v = ADAM_B2 * v + (1.0 - ADAM_B2) * (g * g)
    m_hat = m / (1.0 - ADAM_B1 ** ADAM_STEP)
    v_hat = v / (1.0 - ADAM_B2 ** ADAM_STEP)
    delta = -ADAM_LR * (m_hat / (jnp.sqrt(v_hat) + ADAM_EPS) + ADAM_WD * w)
    return delta, m, v


def adam_big(w, m, v, part, r2, chip, name, layer=0, prev=None):
    L, R, C = w.shape
    tr = _row_tile(R, C)

    def body(chip_ref, w_ref, m_ref, v_ref, p_ref, r_ref, *rest):
        g_out, d_out, m_out, v_out = rest[-4:]
        g = p_ref[...].astype(f32) + r_ref[0].astype(f32) + r_ref[1].astype(f32) + r_ref[2].astype(f32)
        d, m_, v_ = _adamw(w_ref[...], g, m_ref[...], v_ref[...])
        g_out[...] = g
        d_out[...] = d
        m_out[...] = m_
        v_out[...] = v_

    blk = pl.BlockSpec((None, tr, C), lambda i, chip: (layer, i, 0))
    extra = [] if prev is None else list(prev)
    return pl.pallas_call(
        body, name=name, out_shape=[SDS((L, R, C), f32)] * 4,
        grid_spec=pltpu.PrefetchScalarGridSpec(
            num_scalar_prefetch=1, grid=(R // tr,),
            in_specs=[blk, blk, blk,
                      pl.BlockSpec((None, tr, C), lambda i, chip: (chip[0], i, 0)),
                      pl.BlockSpec((3, tr, C), lambda i, chip: (0, i, 0))] + [_ANY] * len(extra),
            out_specs=[blk] * 4),
        input_output_aliases={6 + k: k for k in range(len(extra))},
        compiler_params=_cp(dimension_semantics=("arbitrary",)),
    )(chip, w, m, v, part, r2, *extra)


def allreduce_small(buf, chips=None):
    shp = buf.shape
    half = (shp[0] // 16) * 8
    parts = (pl.ds(0, half), pl.ds(half, shp[0] - half))
    n_c = 0 if chips is None else len(chips.arrs)

    def body(in_ref, *refs):
        c_in, out_ref, c_out = refs[:n_c], refs[n_c], refs[n_c + 1:2 * n_c + 1]
        acc1, acc2, r0, r1, r2, send_sems, recv_sems = refs[2 * n_c + 1:2 * n_c + 8]
        c_sems = refs[2 * n_c + 8:]
        if chips is not None:
            chips.start(c_in, c_out, c_sems)
        x, y, c = _pos()
        across = [(1 - x, y, c), (x, 1 - y, c)]

        def exchange(src, rcv, dst, copies):
            cps = [pltpu.make_async_remote_copy(
                src_ref=src.at[rows], dst_ref=rcv.at[rows], send_sem=send_sems.at[k], recv_sem=recv_sems.at[k],
                device_id=peer, device_id_type=MESH) for k, rows, peer in copies]
            for cp in cps:
                cp.start()
            for cp in cps:
                cp.wait()
            dst[...] = src[...] + rcv[...]

        exchange(in_ref, r0, acc1, [(0, pl.ds(0, shp[0]), (x, y, 1 - c))])
        exchange(acc1, r1, acc2, [(1, parts[0], across[0]), (2, parts[1], across[1])])
        exchange(acc2, r2, out_ref, [(3, parts[0], across[1]), (4, parts[1], across[0])])
        if chips is not None:
            chips.finish(c_in, c_out, c_sems)

    vm = pl.BlockSpec(memory_space=pltpu.VMEM)
    res = pl.pallas_call(
        body, name="allreduce_small", in_specs=[vm] + [_ANY] * n_c, out_specs=[vm] + [_ANY] * n_c,
        out_shape=[SDS(shp, f32)] + ([] if chips is None else chips.out_shape),
        scratch_shapes=[pltpu.VMEM(shp, f32)] * 5 + [pltpu.SemaphoreType.DMA((5,)), pltpu.SemaphoreType.DMA((5,))]
        + ([] if chips is None else chips.scratch),
    )(buf, *([] if chips is None else chips.arrs))
    if chips is not None:
        chips.result = list(res[1:])
    return res[0]


SMALL_ROWS = {'norm_mix': (0, 2, D), 'norm_mlp': (2, 2, D), 'norm_kv': (4, 1, D), 'norm_final': (5, 1, D),
              's5_d': (6, 1, D), 'b_q': (7, 1, D), 'b_o': (8, 1, D), 's5_b_glu': (9, 2, D), 'b_kv': (11, 1, 512),
              'sinks': (12, 1, 16), 's5_log_dt': (13, 1, 64), 's5_a_re': (16, 4, D), 's5_a_im': (20, 4, D),
              's5_b_re': (24, 64, D), 's5_b_im': (88, 64, D), 's5_c_re': (152, 64, D), 's5_c_im': (216, 64, D)}
LOSS_ROW = 14
ROW_PARAMS = ['norm_mix', 'norm_mlp', 'norm_kv', 'norm_final', 'b_q', 'b_o', 'b_kv', 'sinks', 's5_log_dt']
SHARD_PARAMS = ['s5_d', 's5_b_glu']
S5_PARAMS = ['s5_a_re', 's5_a_im', 's5_b_re', 's5_b_im', 's5_c_re', 's5_c_im']


def adam_small(dev, gsum, s5_grads, w, m, v):
    names = ROW_PARAMS + SHARD_PARAMS + S5_PARAMS
    n_g = len(ROW_PARAMS) + len(SHARD_PARAMS)

    def body(dev_ref, gs_ref, *refs):
        pos = [0]

        def take(k):
            r = refs[pos[0]:pos[0] + k]
            pos[0] += k
            return r

        g5 = take(len(S5_PARAMS))
        wr, mr, vr = take(len(names)), take(len(names)), take(len(names))
        g_out = take(n_g)
        d_out, m_out, v_out = take(len(names)), take(len(names)), take(len(names))
        dv = dev_ref[0]
        for i, n in enumerate(names):
            if n in S5_PARAMS:
                g = g5[S5_PARAMS.index(n)][...]
            elif n in SHARD_PARAMS:
                r0, _, _ = SMALL_ROWS[n]
                ln = wr[i].shape[1]
                g = jnp.zeros((1, ln), f32)
                for k in range(NDEV):
                    off = k * ln
                    piece = gs_ref[r0 + off // D:r0 + off // D + 1, off % D:off % D + ln]
                    g = g + jnp.where(dv == k, piece, 0.0)
                g_out[i][...] = g
            else:
                r0, nr, nl = SMALL_ROWS[n]
                g = gs_ref[r0:r0 + nr, 0:nl]
                g_out[i][...] = g
            d, m_, v_ = _adamw(wr[i][...], g, mr[i][...], vr[i][...])
            d_out[i][...] = d
            m_out[i][...] = m_
            v_out[i][...] = v_

    vm = pl.BlockSpec(memory_space=pltpu.VMEM)
    ins = [s5_grads[n] for n in S5_PARAMS] + [d[n] for d in (w, m, v) for n in names]
    shapes = [SDS(w[n].shape, f32) for n in names]
    res = pl.pallas_call(
        body, name="adam_small", in_specs=[pl.BlockSpec(memory_space=pltpu.SMEM)] + [vm] * (1 + len(ins)),
        out_specs=[vm] * (n_g + 3 * len(names)), out_shape=shapes[:n_g] + shapes * 3,
        compiler_params=_cp(),
    )(dev, gsum, *ins)
    g_o = dict(zip(names[:n_g], res[:n_g]))
    rest = res[n_g:]
    k = len(names)
    return g_o, dict(zip(names, rest[:k])), dict(zip(names, rest[k:2 * k])), dict(zip(names, rest[2 * k:]))


WEIGHTS = ['norm_mix', 'norm_mlp', 'norm_kv', 'norm_final', 's5_a_re', 's5_a_im', 's5_log_dt', 's5_b_re', 's5_b_im',
           's5_c_re', 's5_c_im', 's5_d', 's5_w_glu', 's5_b_glu', 'w_kv', 'b_kv', 'w_q', 'b_q', 'sinks', 'w_o', 'b_o',
           'w_mlp_in', 'w_mlp_out']
BIG = ['s5_w_glu', 'w_kv', 'w_q', 'w_o', 'w_mlp_in', 'w_mlp_out']
BIG_2D = {'s5_w_glu': (D, 256), 'w_kv': (128, 512), 'w_q': (128, D), 'w_o': (128, D), 'w_mlp_in': (2 * D, 512),
          'w_mlp_out': (2 * 512, D)}
SMALL = [n for n in WEIGHTS if n not in BIG]


def kernel(x, norm_mix, norm_mlp, norm_kv, norm_final, s5_a_re, s5_a_im, s5_log_dt, s5_b_re, s5_b_im, s5_c_re, s5_c_im, s5_d, s5_w_glu, s5_b_glu, w_kv, b_kv, w_q, b_q, sinks, w_o, b_o, w_mlp_in, w_mlp_out, loss_target, m_norm_mix, m_norm_mlp, m_norm_kv, m_norm_final, m_s5_a_re, m_s5_a_im, m_s5_log_dt, m_s5_b_re, m_s5_b_im, m_s5_c_re, m_s5_c_im, m_s5_d, m_s5_w_glu, m_s5_b_glu, m_w_kv, m_b_kv, m_w_q, m_b_q, m_sinks, m_w_o, m_b_o, m_w_mlp_in, m_w_mlp_out, v_norm_mix, v_norm_mlp, v_norm_kv, v_norm_final, v_s5_a_re, v_s5_a_im, v_s5_log_dt, v_s5_b_re, v_s5_b_im, v_s5_c_re, v_s5_c_im, v_s5_d, v_s5_w_glu, v_s5_b_glu, v_w_kv, v_b_kv, v_w_q, v_b_q, v_sinks, v_w_o, v_b_o, v_w_mlp_in, v_w_mlp_out):
    w = dict(norm_mix=norm_mix, norm_mlp=norm_mlp, norm_kv=norm_kv, norm_final=norm_final, s5_a_re=s5_a_re,
             s5_a_im=s5_a_im, s5_log_dt=s5_log_dt, s5_b_re=s5_b_re, s5_b_im=s5_b_im, s5_c_re=s5_c_re, s5_c_im=s5_c_im,
             s5_d=s5_d, s5_w_glu=s5_w_glu, s5_b_glu=s5_b_glu, w_kv=w_kv, b_kv=b_kv, w_q=w_q, b_q=b_q, sinks=sinks,
             w_o=w_o, b_o=b_o, w_mlp_in=w_mlp_in, w_mlp_out=w_mlp_out)
    m = dict(norm_mix=m_norm_mix, norm_mlp=m_norm_mlp, norm_kv=m_norm_kv, norm_final=m_norm_final, s5_a_re=m_s5_a_re,
             s5_a_im=m_s5_a_im, s5_log_dt=m_s5_log_dt, s5_b_re=m_s5_b_re, s5_b_im=m_s5_b_im, s5_c_re=m_s5_c_re,
             s5_c_im=m_s5_c_im, s5_d=m_s5_d, s5_w_glu=m_s5_w_glu, s5_b_glu=m_s5_b_glu, w_kv=m_w_kv, b_kv=m_b_kv,
             w_q=m_w_q, b_q=m_b_q, sinks=m_sinks, w_o=m_w_o, b_o=m_b_o, w_mlp_in=m_w_mlp_in, w_mlp_out=m_w_mlp_out)
    v = dict(norm_mix=v_norm_mix, norm_mlp=v_norm_mlp, norm_kv=v_norm_kv, norm_final=v_norm_final, s5_a_re=v_s5_a_re,
             s5_a_im=v_s5_a_im, s5_log_dt=v_s5_log_dt, s5_b_re=v_s5_b_re, s5_b_im=v_s5_b_im, s5_c_re=v_s5_c_re,
             s5_c_im=v_s5_c_im, s5_d=v_s5_d, s5_w_glu=v_s5_w_glu, s5_b_glu=v_s5_b_glu, w_kv=v_w_kv, b_kv=v_b_kv,
             w_q=v_w_q, b_q=v_b_q, sinks=v_sinks, w_o=v_w_o, b_o=v_b_o, w_mlp_in=v_w_mlp_in, w_mlp_out=v_w_mlp_out)
    xi, yi, ci = _pos()
    dev = 4 * xi + 2 * yi + ci
    core = ci.reshape(1).astype(jnp.int32)
    chip = (2 * xi + yi).reshape(1).astype(jnp.int32)

    shards = {
        "s5_w_glu": s5_w_glu[0].astype(bf16), "w_kv": w_kv.astype(bf16), "w_q": w_q[0].astype(bf16),
        "w_o": w_o[0].astype(bf16), "w_in0": w_mlp_in[0].astype(bf16), "w_in1": w_mlp_in[1].astype(bf16),
        "w_out0": w_mlp_out[0].astype(bf16), "w_out1": w_mlp_out[1].astype(bf16),
        "vecs": jnp.broadcast_to(jnp.concatenate([s5_d, s5_b_glu], axis=1), (8, 384)),
    }
    as3d = lambda a, n: a if a.ndim == 3 and a.shape[0] == 2 else a.reshape((1,) + BIG_2D[n])
    opt = {n: (as3d(w[n], n), as3d(m[n], n), as3d(v[n], n)) for n in BIG}
    _, grad_x, grads, big = fwd_bwd(x[0], loss_target[0], {n: w[n] for n in SMALL}, shards, opt, core, chip)

    gsum = allreduce_small(grads)

    out_g, out_d, out_m, out_v = {}, {}, {}, {}
    for n in BIG:
        out_g[n], out_d[n], out_m[n], out_v[n] = [r.reshape(w[n].shape) for r in big[n]]

    loss = gsum[LOSS_ROW, 0]
    swapped = ("s5_b_re", "s5_b_im")
    swap = lambda a: a.transpose(0, 1, 3, 2)

    def kernel_side(d):
        d = {n: (d[n].reshape(1, -1) if d[n].ndim == 1 else d[n]) for n in SMALL}
        d.update({n: swap(d[n]) for n in swapped})
        return d

    s5_g = {}
    for n in S5_PARAMS:
        r0, nr, _ = SMALL_ROWS[n]
        s5_g[n] = gsum[r0:r0 + nr].reshape((1, 64, 16, 64) if n in swapped else w[n].shape)
        out_g[n] = s5_g[n]
    g_s, d_s, m_s, v_s = adam_small(dev.reshape(1).astype(jnp.int32), gsum, s5_g, kernel_side(w), kernel_side(m),
                                    kernel_side(v))
    for src, dst in ((g_s, out_g), (d_s, out_d), (m_s, out_m), (v_s, out_v)):
        dst.update(src)
    for dst in (out_g, out_d, out_m, out_v):
        for n in SMALL:
            dst[n] = (swap(dst[n]) if n in swapped else dst[n]).reshape(w[n].shape)

    return (loss, grad_x[None], *[out_g[n] for n in WEIGHTS], *[out_d[n] for n in WEIGHTS],
            *[out_m[n] for n in WEIGHTS], *[out_v[n] for n in WEIGHTS])
```

```python
import functools
import math

import jax
import jax.numpy as jnp
from jax import lax
from jax.experimental import pallas as pl
from jax.experimental.pallas import tpu as pltpu
from jax.experimental.pallas import tpu_sc as plsc

f32 = jnp.float32
bf16 = jnp.bfloat16
SDS = jax.ShapeDtypeStruct

T = 2048
D = 1024
NDEV = 8
NORM_EPS = 1e-5
S5_G, S5_C, S5_P = 64, 16, 64
S5_SUB = 8
S5_CH = 8
S5_STEPS = T // S5_CH
DT_MIN_LAMBDA = -1e-4
HEAD_DIM = 64
N_KV = 4
Q_PER_KV = 4
BLK = 128
D_FF_SHARD = 512
ADAM_LR, ADAM_B1, ADAM_B2, ADAM_EPS, ADAM_WD, ADAM_STEP = 0.001, 0.9, 0.999, 1e-08, 0.01, 10
VMEM_LIMIT = 56 * 1024 * 1024
MESH = pl.DeviceIdType.MESH


def _cp(**kw):
    return pltpu.CompilerParams(vmem_limit_bytes=VMEM_LIMIT, **kw)


def _dot(a, b):
    return jnp.dot(a, b, preferred_element_type=f32)


def _dot_nt(a, b):
    return lax.dot_general(a, b, (((1,), (1,)), ((), ())), preferred_element_type=f32)


def _dot_tn(a, b):
    return lax.dot_general(a, b, (((0,), (0,)), ((), ())), preferred_element_type=f32)


def _rms(x, g):
    r = lax.rsqrt(jnp.mean(x * x, axis=-1, keepdims=True) + NORM_EPS)
    return x * r * g, r


def _rms_bwd(x, g, dy):
    r = lax.rsqrt(jnp.mean(x * x, axis=-1, keepdims=True) + NORM_EPS)
    u = dy * g
    dx = r * u - (r * r * r) * x * jnp.mean(u * x, axis=-1, keepdims=True)
    return dx, dy * x * r


def _colsum8(v):
    s = jnp.sum(v, axis=0, keepdims=True)
    row = lax.broadcasted_iota(jnp.int32, (8, v.shape[1]), 0)
    return jnp.where(row == 0, jnp.broadcast_to(s, (8, v.shape[1])), 0.0)


def _full(shape):
    nd = len(shape)
    return pl.BlockSpec(shape, lambda *_: (0,) * nd, pipeline_mode=pl.Buffered(1))


_ANY = pl.BlockSpec(memory_space=pl.ANY)


def _pos():
    return lax.axis_index("x"), lax.axis_index("y"), lax.axis_index("c")


def _other_chips(x, y):
    return [(1 - x, y), (x, 1 - y), (1 - x, 1 - y)]


class BgGather:
    SIB, XN, YN, FWD_Y, FWD_X, SIB_X, SIB_Y, SIB_D = range(8)

    def __init__(self, arrs, mids=(0.5, 0.75)):
        n = len(arrs)
        self.arrs = list(arrs)
        self.out_shape = [SDS((NDEV,) + a.shape, a.dtype) for a in arrs]
        self.scratch = [pltpu.SemaphoreType.DMA((n, 8)), pltpu.SemaphoreType.DMA((n, 8)),
                        pltpu.SemaphoreType.DMA((n,))]
        self.mids = mids
        self.result = None

    @staticmethod
    def peers(x, y, c):
        return [(x, y, 1 - c), (1 - x, y, c), (x, 1 - y, c)]

    def mid_steps(self, nsteps):
        at = lambda f: min(nsteps - 1, max(0, int(f * nsteps) - 1))
        return [(at(self.mids[0]), self.mid), (max(at(self.mids[0]), at(self.mids[1])), self.mid2)]

    def _halves(self, a):
        rows = self.arrs[a].shape[0]
        cut = (rows // 32) * 16 if rows >= 32 else rows
        return (0, cut), (cut, rows - cut)

    def _copy(self, ins, outs, sems, a, k, block, to, own=False, part=None):
        slot = 4 * block[0] + 2 * block[1] + block[2]
        rows = pl.ds(0, self.arrs[a].shape[0]) if part is None else pl.ds(*self._halves(a)[part])
        dst = outs[a].at[slot, rows]
        return pltpu.make_async_remote_copy(
            src_ref=ins[a].at[rows] if own else dst, dst_ref=dst, send_sem=sems[0].at[a, k],
            recv_sem=sems[1].at[a, k], device_id=to, device_id_type=MESH)

    def _mine(self, ins, outs, sems):
        x, y, c = _pos()
        return [pltpu.make_async_copy(ins[a], outs[a].at[4 * x + 2 * y + c], sems[2].at[a])
                for a in range(len(self.arrs))]

    def _split(self, a):
        return self._halves(a)[1][1] > 0

    def _sends(self, ins, outs, sems, phase):
        x, y, c = _pos()
        me, sib, xn, yn, dg = (x, y, c), (x, y, 1 - c), (1 - x, y, c), (x, 1 - y, c), (1 - x, 1 - y, c)
        cps = []
        for a in range(len(self.arrs)):
            cp = lambda k, block, to, **kw: self._copy(ins, outs, sems, a, k, block, to, **kw)
            if phase == 0:
                cps += [cp(self.SIB, me, sib, own=True), cp(self.XN, me, xn, own=True), cp(self.YN, me, yn, own=True)]
            elif phase == 1:
                cps.append(cp(self.FWD_Y, xn, yn, part=0))
                if self._split(a):
                    cps.append(cp(self.FWD_X, yn, xn, part=1))
                cps += [cp(self.SIB_X, xn, sib), cp(self.SIB_Y, yn, sib)]
            else:
                cps.append(cp(self.SIB_D, dg, sib))
        return cps

    def _arrivals(self, ins, outs, sems, phase):
        x, y, c = _pos()
        me, xn, yn, dg = (x, y, c), (1 - x, y, c), (x, 1 - y, c), (1 - x, 1 - y, c)
        cps = []
        for a in range(len(self.arrs)):
            cp = lambda k, block, **kw: self._copy(ins, outs, sems, a, k, block, me, **kw)
            if phase == 1:
                cps += [cp(self.XN, xn), cp(self.YN, yn)]
            elif phase == 2:
                cps.append(cp(self.FWD_Y, dg, part=0))
                if self._split(a):
                    cps.append(cp(self.FWD_X, dg, part=1))
            else:
                cps += [cp(self.SIB, (x, y, 1 - c)), cp(self.SIB_X, (1 - x, y, 1 - c)),
                        cp(self.SIB_Y, (x, 1 - y, 1 - c)), cp(self.SIB_D, (1 - x, 1 - y, 1 - c))]
        return cps

    def start(self, ins, outs, sems):
        for cp in self._mine(ins, outs, sems) + self._sends(ins, outs, sems, 0):
            cp.start()

    def mid(self, ins, outs, sems):
        for cp in self._arrivals(ins, outs, sems, 1):
            cp.wait_recv()
        for cp in self._sends(ins, outs, sems, 1):
            cp.start()

    def mid2(self, ins, outs, sems):
        for cp in self._arrivals(ins, outs, sems, 2):
            cp.wait_recv()
        for cp in self._sends(ins, outs, sems, 2):
            cp.start()

    def finish(self, ins, outs, sems):
        for cp in self._arrivals(ins, outs, sems, 3):
            cp.wait_recv()
        for ph in range(3):
            for cp in self._sends(ins, outs, sems, ph):
                cp.wait_send()
        for cp in self._mine(ins, outs, sems):
            cp.wait()


def sc_comm(g, collective_id, name):
    srcs = [jax.new_ref(a, memory_space=pltpu.MemorySpace.HBM) for a in g.arrs]
    dsts = [jax.empty_ref(s, memory_space=pltpu.MemorySpace.HBM) for s in g.out_shape]

    @pl.kernel(mesh=plsc.ScalarSubcoreMesh(axis_name="sequencer", num_cores=1), name=name,
               scratch_types=tuple(g.scratch), compiler_params=pltpu.CompilerParams(collective_id=collective_id))
    def launch(*sems):
        peers = g.peers(*_pos())
        barrier = pltpu.get_barrier_semaphore()
        for peer in peers:
            pl.semaphore_signal(barrier, inc=1, device_id=peer, device_id_type=MESH)
        pl.semaphore_wait(barrier, len(peers))
        g.start(srcs, dsts, sems)
        for _, phase in g.mid_steps(1):
            phase(srcs, dsts, sems)
        g.finish(srcs, dsts, sems)

    launch()
    return [d[...] for d in dsts]


def sc_gather(arrs, collective_id, name):
    return sc_comm(BgGather(arrs), collective_id, name)


class BgPair:
    def __init__(self, arrs):
        n = len(arrs)
        self.arrs = list(arrs)
        self.out_shape = [SDS((4,) + a.shape[1:], a.dtype) for a in arrs]
        self.scratch = [pltpu.SemaphoreType.DMA((n, 4)), pltpu.SemaphoreType.DMA((n, 4))]
        self.result = None

    @staticmethod
    def peers(x, y, c):
        return [(x, y, 1 - c)]

    def mid_steps(self, nsteps):
        return []

    def _copies(self, ins, outs, sems):
        x, y, c = _pos()
        return [pltpu.make_async_remote_copy(
            src_ref=ins[a].at[2 * k + 1 - c], dst_ref=outs[a].at[k], send_sem=sems[0].at[a, k],
            recv_sem=sems[1].at[a, k], device_id=(x, y, 1 - c), device_id_type=MESH)
            for a in range(len(self.arrs)) for k in range(4)]

    def start(self, ins, outs, sems):
        for cp in self._copies(ins, outs, sems):
            cp.start()

    def finish(self, ins, outs, sems):
        cps = self._copies(ins, outs, sems)
        for cp in cps:
            cp.wait_recv()
        for cp in cps:
            cp.wait_send()


class BgChips(BgPair):
    def __init__(self, arrs):
        n = len(arrs)
        self.arrs = list(arrs)
        self.out_shape = [SDS((3,) + a.shape[1:], a.dtype) for a in arrs]
        self.scratch = [pltpu.SemaphoreType.DMA((n, 3)), pltpu.SemaphoreType.DMA((n, 3))]
        self.result = None

    @staticmethod
    def peers(x, y, c):
        return [(px, py, c) for px, py in _other_chips(x, y)]

    def _copies(self, ins, outs, sems):
        x, y, c = _pos()
        return [pltpu.make_async_remote_copy(
            src_ref=ins[a].at[2 * px + py], dst_ref=outs[a].at[r], send_sem=sems[0].at[a, r],
            recv_sem=sems[1].at[a, r], device_id=(px, py, c), device_id_type=MESH)
            for a in range(len(self.arrs)) for r, (px, py) in enumerate(_other_chips(x, y))]


def _call(bgs, body, *, name, grid, in_specs, out_specs, out_shape, scratch_shapes=(), compiler_params=None):
    single = not isinstance(out_shape, (list, tuple))
    out_specs_l = [out_specs] if single else list(out_specs)
    out_shape_l = [out_shape] if single else list(out_shape)
    bgs = [b for b in (bgs or []) if b is not None]
    n_in, n_out, n_sc = len(in_specs), len(out_shape_l), len(scratch_shapes)
    nsteps = math.prod(grid)
    b_in_specs = [b.in_specs(grid) if hasattr(b, "in_specs") else [_ANY] * len(b.arrs) for b in bgs]
    b_out_specs = [b.out_specs(grid) if hasattr(b, "out_specs") else [_ANY] * len(b.out_shape) for b in bgs]
    aliases, i_off, o_off = {}, n_in, n_out
    for b in bgs:
        aliases.update({i_off + i: o_off + o for i, o in getattr(b, "aliases", {}).items()})
        i_off, o_off = i_off + len(b.arrs), o_off + len(b.out_shape)

    def full(*refs):
        pos = [0]

        def take(k):
            r = refs[pos[0]:pos[0] + k]
            pos[0] += k
            return r

        ins = take(n_in)
        b_ins = [take(len(b.arrs)) for b in bgs]
        outs = take(n_out)
        b_outs = [take(len(b.out_shape)) for b in bgs]
        sc = take(n_sc)
        b_sc = [take(len(b.scratch)) for b in bgs]
        if bgs:
            step = pl.program_id(0)
            for d in range(1, len(grid)):
                step = step * grid[d] + pl.program_id(d)

            @pl.when(step == 0)
            def _():
                for b, i_, o_, s_ in zip(bgs, b_ins, b_outs, b_sc):
                    b.start(i_, o_, s_)

        body(*ins, *outs, *sc)
        if bgs:
            for b, i_, o_, s_ in zip(bgs, b_ins, b_outs, b_sc):
                if hasattr(b, "step"):
                    b.step(i_, o_, s_)
                for at, fn in b.mid_steps(nsteps):
                    @pl.when(step == at)
                    def _():
                        fn(i_, o_, s_)

            @pl.when(step == nsteps - 1)
            def _():
                for b, i_, o_, s_ in zip(bgs, b_ins, b_outs, b_sc):
                    b.finish(i_, o_, s_)

    def run(*args):
        res = pl.pallas_call(
            full, name=name, grid=grid,
            in_specs=list(in_specs) + [s for l in b_in_specs for s in l],
            out_specs=out_specs_l + [s for l in b_out_specs for s in l],
            out_shape=out_shape_l + [s for b in bgs for s in b.out_shape],
            scratch_shapes=list(scratch_shapes) + [s for b in bgs for s in b.scratch],
            input_output_aliases=aliases,
            compiler_params=compiler_params,
        )(*args, *[a for b in bgs for a in b.arrs])
        rest = list(res[n_out:])
        for b in bgs:
            b.result, rest = rest[:len(b.out_shape)], rest[len(b.out_shape):]
        return res[0] if single else list(res[:n_out])

    return run


def s5_discretize(a_re, a_im, log_dt, b_re, b_im, c_re, c_im):
    lam_r = jnp.minimum(a_re, DT_MIN_LAMBDA)
    lam_i = a_im
    dt = jnp.exp(log_dt)[:, None]
    e = jnp.exp(lam_r * dt)
    lbr = e * jnp.cos(lam_i * dt)
    lbi = e * jnp.sin(lam_i * dt)
    den = lam_r * lam_r + lam_i * lam_i
    cf_r = ((lbr - 1.0) * lam_r + lbi * lam_i) / den
    cf_i = (lbi * lam_r - (lbr - 1.0) * lam_i) / den
    bb_r = cf_r[:, :, None] * b_re - cf_i[:, :, None] * b_im
    bb_i = cf_r[:, :, None] * b_im + cf_i[:, :, None] * b_re
    eye = jnp.eye(8, dtype=f32)

    def blk_b(m):
        return jnp.einsum('bgpc,gh->bgchp', m.reshape(8, 8, S5_P, S5_C), eye).reshape(8, 128, 512)

    def blk_c(m):
        return jnp.einsum('bgcp,gh->bgphc', m.reshape(8, 8, S5_C, S5_P), eye).reshape(8, 512, 128)

    bm = jnp.concatenate([blk_b(bb_r), blk_b(bb_i)], axis=-1)
    cm = jnp.concatenate([blk_c(c_re), -blk_c(c_im)], axis=1)
    lam = jnp.stack([lbr.reshape(8, 512), lbi.reshape(8, 512)], axis=1)
    lam = jnp.broadcast_to(lam[:, :, None, :], (8, 2, 8, 512))
    return lam, bm, cm


def _cmul(ar, ai, br, bi):
    return ar * br - ai * bi, ar * bi + ai * br


def _shift_rows(v, k, up):
    row = lax.broadcasted_iota(jnp.int32, v.shape, 0)
    if up:
        return jnp.where(row < 8 - k, pltpu.roll(v, 8 - k, 0), 0.0)
    return jnp.where(row >= k, pltpu.roll(v, k, 0), 0.0)


_ROWS = 256


def s5_core_fwd(hn, bm, lam, cm, bg=()):
    nt = T // _ROWS

    def body(u_ref, b_ref, lam_ref, c_ref, ys_ref, S):
        lr, li = lam_ref[0], lam_ref[1]
        z = jnp.zeros((8, 512), f32)
        tile = lambda k: pl.ds(k * _ROWS, _ROWS)
        c = (z, z)
        for k in range(nt):
            S[tile(k), :] = _dot(_rows_in(u_ref, k).astype(bf16), b_ref[...])
            if k >= 1:
                c = _scan_tile(S, lr, li, k - 1, c, False, False)
        c = _scan_tile(S, lr, li, nt - 1, c, False, False)
        c = _chunk_starts(c[0], c[1], lr, li, False)
        for k in range(nt):
            c = _scan_tile(S, lr, li, k, c, False, True)
            if k >= 1:
                _rows_out(ys_ref, k - 1, _dot(S[tile(k - 1), :].astype(bf16), c_ref[...]))
        _rows_out(ys_ref, nt - 1, _dot(S[tile(nt - 1), :].astype(bf16), c_ref[...]))

    return _call(
        bg, body, name="s5_core_fwd", grid=(S5_SUB,),
        in_specs=[pl.BlockSpec((T, 128), lambda b: (0, b)),
                  pl.BlockSpec((None, 128, 1024), lambda b: (b, 0, 0)),
                  pl.BlockSpec((None, 4, 8, 512), lambda b: (b, 0, 0, 0)),
                  pl.BlockSpec((None, 1024, 128), lambda b: (b, 0, 0))],
        out_specs=[pl.BlockSpec((T, 128), lambda b: (0, b)), pl.BlockSpec((T, 1024), lambda b: (0, b))],
        out_shape=[SDS((T, D), f32), SDS((T, S5_SUB * 1024), f32)],
        compiler_params=_cp(dimension_semantics=("arbitrary",)),
    )(hn, bm, lam, cm)


_SEG = _ROWS // S5_CH


def _rows_in(ref, k):
    return jnp.concatenate([ref[pl.ds(s, S5_CH, stride=S5_STEPS), :] for s in range(k * _SEG, (k + 1) * _SEG)], axis=0)


def _rows_out(ref, k, val):
    for j, s in enumerate(range(k * _SEG, (k + 1) * _SEG)):
        ref[pl.ds(s, S5_CH, stride=S5_STEPS), :] = val[j * S5_CH:(j + 1) * S5_CH, :]


def _scan_tile(S, lr, li, k, carry, reverse, store, aux=None):
    steps = range(k * _SEG, (k + 1) * _SEG)
    for s in (reversed(steps) if reverse else steps):
        row = pl.ds(s * 8, 8)
        xr, xi = carry[0], carry[1]
        nr = lr * xr - li * xi + S[row, 0:512]
        ni = lr * xi + li * xr + S[row, 512:1024]
        if store:
            S[row, 0:512] = nr
            S[row, 512:1024] = ni
        if aux is not None and s >= 1:
            prow = pl.ds((s - 1) * 8, 8)
            pr, pi_ = aux[prow, 0:512], aux[prow, 512:1024]
            carry = (nr, ni, carry[2] + nr * pr + ni * pi_, carry[3] + ni * pr - nr * pi_)
        elif aux is not None:
            carry = (nr, ni, carry[2], carry[3])
        else:
            carry = (nr, ni)
    return carry


def _chunk_starts(er, ei, lr, li, reverse):
    ar, ai = lr, li
    for _ in range(8):
        ar, ai = _cmul(ar, ai, ar, ai)
    cr, ci = _shift_rows(er, 1, reverse), _shift_rows(ei, 1, reverse)
    for k in (1, 2, 4):
        sr, si = _shift_rows(cr, k, reverse), _shift_rows(ci, k, reverse)
        pr, pi_ = _cmul(ar, ai, sr, si)
        cr, ci = cr + pr, ci + pi_
        ar, ai = _cmul(ar, ai, ar, ai)
    return cr, ci


def s5_core_bwd(hn, dy, xs, bm, lam, cm, bg=()):
    nt = T // _ROWS

    def body(u_ref, dy_ref, S1, b_ref, lam_ref, c_ref, du_ref, db_ref, dct_ref, dlam_ref, S2):
        lcr, lci = lam_ref[2], lam_ref[3]
        z = jnp.zeros((8, 512), f32)
        tile = lambda k: pl.ds(k * _ROWS, _ROWS)

        def dx(k):
            dyb = _rows_in(dy_ref, k).astype(bf16)
            S2[tile(k), :] = _dot_nt(dyb, c_ref[...])
            dct_ref[...] += _dot_tn(dyb, S1[tile(k), :].astype(bf16))

        dct_ref[...] = jnp.zeros_like(dct_ref)
        dx(nt - 1)
        c = (z, z)
        for k in range(nt - 1, -1, -1):
            if k >= 1:
                dx(k - 1)
            c = _scan_tile(S2, lcr, lci, k, c, True, False)

        def dbu(k):
            gb = S2[tile(k), :].astype(bf16)
            db_ref[...] += _dot_tn(_rows_in(u_ref, k).astype(bf16), gb)
            _rows_out(du_ref, k, _dot_nt(gb, b_ref[...]))

        c = _chunk_starts(c[0], c[1], lcr, lci, True) + (z, z)
        db_ref[...] = jnp.zeros_like(db_ref)
        for k in range(nt - 1, -1, -1):
            c = _scan_tile(S2, lcr, lci, k, c, True, True, aux=S1)
            if k + 1 < nt:
                dbu(k + 1)
        dbu(0)
        gr, gi, dr, di = c
        last = pl.ds((S5_STEPS - 1) * 8, 8)
        xr = _shift_rows(S1[last, 0:512], 1, False)
        xi = _shift_rows(S1[last, 512:1024], 1, False)
        dlam_ref[0] = dr + gr * xr + gi * xi
        dlam_ref[1] = di + gi * xr - gr * xi

    return _call(
        bg, body, name="s5_core_bwd", grid=(S5_SUB,),
        in_specs=[pl.BlockSpec((T, 128), lambda b: (0, b)),
                  pl.BlockSpec((T, 128), lambda b: (0, b)),
                  pl.BlockSpec((T, 1024), lambda b: (0, b)),
                  pl.BlockSpec((None, 128, 1024), lambda b: (b, 0, 0)),
                  pl.BlockSpec((None, 4, 8, 512), lambda b: (b, 0, 0, 0)),
                  pl.BlockSpec((None, 1024, 128), lambda b: (b, 0, 0))],
        out_specs=[pl.BlockSpec((T, 128), lambda b: (0, b)),
                   pl.BlockSpec((None, 128, 1024), lambda b: (b, 0, 0)),
                   pl.BlockSpec((None, 128, 1024), lambda b: (b, 0, 0)),
                   pl.BlockSpec((None, 2, 8, 512), lambda b: (b, 0, 0, 0))],
        out_shape=[SDS((T, D), f32), SDS((8, 128, 1024), f32), SDS((8, 128, 1024), f32), SDS((8, 2, 8, 512), f32)],
        scratch_shapes=[pltpu.VMEM((T, 1024), f32)],
        compiler_params=_cp(dimension_semantics=("arbitrary",)),
    )(hn, dy, xs, bm, lam, cm)


TM = 512
NT = T // TM


def _tile(n=D):
    return pl.BlockSpec((TM, n), lambda i: (i, 0))


def s5_pre(xp, g):
    def body(x_ref, g_ref, hn_ref):
        hn_ref[...] = _rms(x_ref[...], g_ref[...])[0]

    return pl.pallas_call(
        body, name="s5_pre", grid=(NT,), in_specs=[_tile(), _full((1, D))], out_specs=_tile(),
        out_shape=SDS((T, D), f32), compiler_params=_cp(dimension_semantics=("arbitrary",)),
    )(xp, g)


def _gelu_grad(y):
    c = math.sqrt(2.0 / math.pi)
    t = jnp.tanh(c * (y + 0.044715 * y * y * y))
    return 0.5 * (1.0 + t) + 0.5 * y * (1.0 - t * t) * c * (1.0 + 3.0 * 0.044715 * y * y)


def s5_post(ys, xp, g, d, wglu, bglu, bg=()):
    def body(ys_ref, x_ref, g_ref, d_ref, w_ref, b_ref, y_ref, z_ref, h_ref):
        x = x_ref[...]
        hn, _ = _rms(x, g_ref[...])
        y = ys_ref[...] + d_ref[...] * hn
        y_ref[...] = y
        yg = jax.nn.gelu(y).astype(bf16)
        for j in range(4):
            cv = slice(j * 256, (j + 1) * 256)
            cg = slice(1024 + j * 256, 1024 + (j + 1) * 256)
            val = _dot(yg, w_ref[j]) + b_ref[:, cv]
            gate = _dot(yg, w_ref[j + 4]) + b_ref[:, cg]
            z_ref[:, cv] = val
            z_ref[:, cg] = gate
            h_ref[:, cv] = x[:, cv] + val * jax.nn.sigmoid(gate)

    return _call(
        bg, body, name="s5_post", grid=(NT,),
        in_specs=[_tile(), _tile(), _full((1, D)), _full((1, D)), _full((8, D, 256)), _full((1, 2 * D))],
        out_specs=[_tile(), _tile(2 * D), _tile()],
        out_shape=[SDS((T, D), f32), SDS((T, 2 * D), f32), SDS((T, D), f32)],
        compiler_params=_cp(dimension_semantics=("arbitrary",)),
    )(ys, xp, g, d, wglu, bglu)


def s5_post_bwd(dh, y, z, wglu, bg=()):
    def body(dh_ref, y_ref, z_ref, w_ref, dy_ref, dw_ref, db_ref, acc):
        i = pl.program_id(0)

        @pl.when(i == 0)
        def _():
            acc[...] = jnp.zeros_like(acc)
            db_ref[...] = jnp.zeros_like(db_ref)

        dh_ = dh_ref[...]
        y = y_ref[...]
        yg = jax.nn.gelu(y).astype(bf16)
        dyg = jnp.zeros((TM, D), f32)
        for j in range(4):
            cv = slice(j * 256, (j + 1) * 256)
            cg = slice(1024 + j * 256, 1024 + (j + 1) * 256)
            val = z_ref[:, cv]
            sg = jax.nn.sigmoid(z_ref[:, cg])
            dval = dh_[:, cv] * sg
            dgate = dh_[:, cv] * val * sg * (1.0 - sg)
            db_ref[:, cv] += _colsum8(dval)
            db_ref[:, cg] += _colsum8(dgate)
            dvb = dval.astype(bf16)
            dgb = dgate.astype(bf16)
            acc[j] += _dot_tn(yg, dvb)
            acc[j + 4] += _dot_tn(yg, dgb)
            dyg = dyg + _dot_nt(dvb, w_ref[j]) + _dot_nt(dgb, w_ref[j + 4])
        dy_ref[...] = dyg * _gelu_grad(y)

        @pl.when(i == NT - 1)
        def _():
            dw_ref[...] = acc[...].astype(bf16)

    return _call(
        bg, body, name="s5_post_bwd", grid=(NT,),
        in_specs=[_tile(), _tile(), _tile(2 * D), _full((8, D, 256))],
        out_specs=[_tile(), _full((8, D, 256)), _full((8, 2 * D))],
        out_shape=[SDS((T, D), f32), SDS((8, D, 256), bf16), SDS((8, 2 * D), f32)],
        scratch_shapes=[pltpu.VMEM((8, D, 256), f32)],
        compiler_params=_cp(dimension_semantics=("arbitrary",)),
    )(dh, y, z, wglu)


def s5_pre_bwd(xp, g, du, dy, d, dh, bg=()):
    def body(x_ref, g_ref, du_ref, dy_ref, d_ref, dh_ref, dx_ref, dg_ref, dd_ref):
        i = pl.program_id(0)

        @pl.when(i == 0)
        def _():
            dg_ref[...] = jnp.zeros_like(dg_ref)
            dd_ref[...] = jnp.zeros_like(dd_ref)

        x = x_ref[...]
        g = g_ref[...]
        dy = dy_ref[...]
        hn, _ = _rms(x, g)
        dhn = du_ref[...] + d_ref[...] * dy
        dx, dgt = _rms_bwd(x, g, dhn)
        dx_ref[...] = dh_ref[...] + dx
        dg_ref[...] += _colsum8(dgt)
        dd_ref[...] += _colsum8(dy * hn)

    return _call(
        bg, body, name="s5_pre_bwd", grid=(NT,),
        in_specs=[_tile(), _full((1, D)), _tile(), _tile(), _full((1, D)), _tile()],
        out_specs=[_tile(), _full((8, D)), _full((8, D))],
        out_shape=[SDS((T, D), f32), SDS((8, D), f32), SDS((8, D), f32)],
        compiler_params=_cp(dimension_semantics=("arbitrary",)),
    )(xp, g, du, dy, d, dh)


TMF = 1024


def mlp_fwd(h, g, w_in, w_out, layer, bg=()):
    def body(h_ref, g_ref, wi_ref, wo_ref, hm_ref, r_ref, out_ref, acc):
        j = pl.program_id(1)

        @pl.when(j == 0)
        def _():
            hm, _ = _rms(h_ref[...], g_ref[...])
            hm_ref[...] = hm.astype(bf16)
            acc[...] = jnp.zeros_like(acc)

        a = jnp.maximum(_dot(hm_ref[...], wi_ref[...]), 0.0)
        r_ref[...] = a.astype(bf16)
        acc[...] += _dot((a * a).astype(bf16), wo_ref[...])

        @pl.when(j == NDEV - 1)
        def _():
            out_ref[...] = h_ref[...] + acc[...]

    return _call(
        bg, body, name=f"mlp_fwd{layer}", grid=(T // TMF, NDEV),
        in_specs=[pl.BlockSpec((TMF, D), lambda i, j: (i, 0)),
                  pl.BlockSpec((1, D), lambda i, j: (0, 0)),
                  pl.BlockSpec((None, D, D_FF_SHARD), lambda i, j: (j, 0, 0)),
                  pl.BlockSpec((None, D_FF_SHARD, D), lambda i, j: (j, 0, 0))],
        out_specs=[pl.BlockSpec((TMF, D), lambda i, j: (i, 0)), pl.BlockSpec((TMF, D_FF_SHARD), lambda i, j: (i, j)),
                   pl.BlockSpec((TMF, D), lambda i, j: (i, 0))],
        out_shape=[SDS((T, D), bf16), SDS((T, NDEV * D_FF_SHARD), bf16), SDS((T, D), f32)],
        scratch_shapes=[pltpu.VMEM((TMF, D), f32)],
        compiler_params=_cp(dimension_semantics=("arbitrary", "arbitrary")),
    )(h, g, w_in, w_out)


def mlp_bwd(h, hm, r, g, dout, dout_b, w_in, w_out, layer, bg=()):
    def body(h_ref, hm_ref, r_ref, g_ref, do_ref, dob_ref, wi_ref, wo_ref, dh_ref, dwi_ref, dwo_ref, dg_ref,
             dhm, dzs):
        s = pl.program_id(0)

        @pl.when(s == 0)
        def _():
            dhm[...] = jnp.zeros_like(dhm)

        @pl.when(s < NDEV)
        def _():
            for c in range(NT):
                rows = pl.ds(c * TM, TM)
                dz = (_dot_nt(dob_ref[rows, :], wo_ref[...]) * (2.0 * r_ref[rows, :].astype(f32))).astype(bf16)
                dzs[rows, :] = dz
                dhm[rows, :] += _dot_nt(dz, wi_ref[...])
            rb = r_ref[...]
            dwo_ref[...] = _dot_tn(rb * rb, dob_ref[...]).astype(bf16)
            dwi_ref[...] = _dot_tn(hm_ref[...], dzs[...]).astype(bf16)

        @pl.when(s >= NDEV)
        def _():
            @pl.when(s == NDEV)
            def _():
                dg_ref[...] = jnp.zeros_like(dg_ref)
            rows = pl.ds(pl.multiple_of((s - NDEV) * TM, TM), TM)
            dx, dgt = _rms_bwd(h_ref[...], g_ref[...], dhm[rows, :])
            dh_ref[...] = do_ref[...] + dx
            dg_ref[...] += _colsum8(dgt)

    shard = lambda s: (jnp.minimum(s, NDEV - 1), 0, 0)
    tile = lambda s: (jnp.maximum(s - NDEV, 0), 0)
    return _call(
        bg, body, name=f"mlp_bwd{layer}", grid=(NDEV + NT,),
        in_specs=[pl.BlockSpec((TM, D), tile),
                  _full((T, D)),
                  pl.BlockSpec((T, D_FF_SHARD), lambda s: (0, jnp.minimum(s, NDEV - 1))),
                  _full((1, D)),
                  pl.BlockSpec((TM, D), tile),
                  _full((T, D)),
                  pl.BlockSpec((None, D, D_FF_SHARD), shard),
                  pl.BlockSpec((None, D_FF_SHARD, D), shard)],
        out_specs=[pl.BlockSpec((TM, D), tile),
                   pl.BlockSpec((None, D, D_FF_SHARD), shard),
                   pl.BlockSpec((None, D_FF_SHARD, D), shard),
                   pl.BlockSpec((8, D), lambda s: (0, 0))],
        out_shape=[SDS((T, D), f32), SDS((NDEV, D, D_FF_SHARD), bf16), SDS((NDEV, D_FF_SHARD, D), bf16),
                   SDS((8, D), f32)],
        scratch_shapes=[pltpu.VMEM((T, D), f32), pltpu.VMEM((T, D_FF_SHARD), bf16)],
        compiler_params=_cp(dimension_semantics=("arbitrary",)),
    )(h, hm, r, g, dout, dout_b, w_in, w_out)


def _spread4():
    r = lax.broadcasted_iota(jnp.int32, (256, D), 0)
    c = lax.broadcasted_iota(jnp.int32, (256, D), 1)
    return ((c // 256 == r // HEAD_DIM) & (c % HEAD_DIM == r % HEAD_DIM)).astype(bf16)


def attn_pre(h, g_kv, g_mix, wkv, bkv, spread, wq, bq):
    def body(h_ref, gkv_ref, gm_ref, wkv_ref, bkv_ref, sp_ref, wq_ref, bq_ref, kvn_ref, hn_ref, k_ref, v_ref, q_ref):
        h_ = h_ref[...]
        kvn = _rms(h_, gkv_ref[...])[0].astype(bf16)
        hn = _rms(h_, gm_ref[...])[0].astype(bf16)
        kvn_ref[...] = kvn
        hn_ref[...] = hn
        kv = (_dot(kvn, wkv_ref[...]) + bkv_ref[...]).astype(bf16)
        k_ref[...] = _dot(kv[:, :256], sp_ref[...]).astype(bf16)
        v_ref[...] = _dot(kv[:, 256:], sp_ref[...]).astype(bf16)
        q_ref[...] = (_dot(hn, wq_ref[...]) + bq_ref[...]).astype(bf16)

    return pl.pallas_call(
        body, name="attn_pre", grid=(NT,),
        in_specs=[_tile(), _full((1, D)), _full((1, D)), _full((D, 512)), _full((1, 512)), _full((256, D)),
                  _full((D, D)), _full((1, D))],
        out_specs=[_tile()] * 5,
        out_shape=[SDS((T, D), bf16)] * 5,
        compiler_params=_cp(dimension_semantics=("arbitrary",)),
    )(h, g_kv, g_mix, wkv, bkv, spread, wq, bq)


def _attn_specs():
    cur = pl.BlockSpec((TM, 256), lambda j, n: (n, j))
    prev = pl.BlockSpec((BLK, 256), lambda j, n: (jnp.maximum(n * (TM // BLK) - 1, 0), j))
    return cur, prev


def _head_mask(g):
    lane = lax.broadcasted_iota(jnp.int32, (1, 256), 1)
    return (lane >= g * HEAD_DIM) & (lane < (g + 1) * HEAD_DIM)


def _stack_heads(t):
    return jnp.concatenate([jnp.where(_head_mask(g), t, 0) for g in range(Q_PER_KV)], axis=0)


def _unstack_heads(t):
    out = jnp.where(_head_mask(0), t[0:BLK], 0.0)
    for g in range(1, Q_PER_KV):
        out = out + jnp.where(_head_mask(g), t[g * BLK:(g + 1) * BLK], 0.0)
    return out


def _attn_probs(qs, k2, sinks, first):
    rows = Q_PER_KV * BLK
    s = _dot_nt(qs, k2) * (1.0 / math.sqrt(HEAD_DIM))
    qi = jnp.bitwise_and(lax.broadcasted_iota(jnp.int32, (rows, 2 * BLK), 0), BLK - 1)
    kj = lax.broadcasted_iota(jnp.int32, (rows, 2 * BLK), 1)
    diff = qi + BLK - kj
    valid = (diff >= 0) & (diff < BLK) & (jnp.logical_not(first) | (kj >= BLK))
    s = jnp.where(valid, s, -jnp.inf)
    rb = lax.broadcasted_iota(jnp.int32, (rows, 1), 0)
    sink = jnp.where(rb < BLK, sinks[0], jnp.where(rb < 2 * BLK, sinks[1], jnp.where(rb < 3 * BLK, sinks[2], sinks[3])))
    m = jnp.maximum(jnp.max(s, axis=-1, keepdims=True), sink)
    p = jnp.exp(s - m)
    ps = jnp.exp(sink - m)
    denom = jnp.sum(p, axis=-1, keepdims=True) + ps
    return p / denom, ps / denom


def _window_blocks(b, n, kc_ref, kp_ref, vc_ref, vp_ref):
    if b == 0:
        return (jnp.concatenate([kp_ref[...], kc_ref[0:BLK, :]], axis=0),
                jnp.concatenate([vp_ref[...], vc_ref[0:BLK, :]], axis=0), n == 0)
    rows = pl.ds((b - 1) * BLK, 2 * BLK)
    return kc_ref[rows, :], vc_ref[rows, :], False


def attn_core_fwd(q, k4, v4, sinks, bg=()):
    nb = TM // BLK

    def body(sink_ref, q_ref, kc_ref, kp_ref, vc_ref, vp_ref, o_ref, a_ref, as_ref):
        j = pl.program_id(0)
        n = pl.program_id(1)
        sk = [sink_ref[j * Q_PER_KV + g] for g in range(Q_PER_KV)]
        for b in range(nb):
            qb = q_ref[b * BLK:(b + 1) * BLK, :]
            k2, v2, first = _window_blocks(b, n, kc_ref, kp_ref, vc_ref, vp_ref)
            a, asink = _attn_probs(_stack_heads(qb), k2, sk, first)
            ab = a.astype(bf16)
            a_ref[b] = ab
            as_ref[b] = jnp.broadcast_to(asink, (Q_PER_KV * BLK, 128)).astype(bf16)
            o_ref[b * BLK:(b + 1) * BLK, :] = _unstack_heads(_dot(ab, v2)).astype(bf16)

    cur, prev = _attn_specs()
    rows = Q_PER_KV * BLK
    return _call(
        bg, body, name="attn_core_fwd", grid=(N_KV, NT),
        in_specs=[pl.BlockSpec(memory_space=pltpu.SMEM), cur, cur, prev, cur, prev],
        out_specs=[cur, pl.BlockSpec((None, nb, rows, 2 * BLK), lambda j, n: (j, n, 0, 0)),
                   pl.BlockSpec((None, nb, rows, 128), lambda j, n: (j, n, 0, 0))],
        out_shape=[SDS((T, D), bf16), SDS((N_KV, T // BLK, rows, 2 * BLK), bf16), SDS((N_KV, T // BLK, rows, 128), bf16)],
        compiler_params=_cp(dimension_semantics=("arbitrary", "arbitrary")),
    )(sinks, q, k4, k4, v4, v4)


def attn_post(h, o, wo, bo):
    def body(h_ref, o_ref, w_ref, b_ref, out_ref):
        out_ref[...] = h_ref[...] + _dot(o_ref[...], w_ref[...]) + b_ref[...]

    return pl.pallas_call(
        body, name="attn_post", grid=(NT,), in_specs=[_tile(), _tile(), _full((D, D)), _full((1, D))],
        out_specs=_tile(), out_shape=SDS((T, D), f32), compiler_params=_cp(dimension_semantics=("arbitrary",)),
    )(h, o, wo, bo)


def attn_bwd_pre(dh, o, wo, bg=()):
    def body(dh_ref, o_ref, w_ref, do_ref, dw_ref, db_ref, acc):
        i = pl.program_id(0)

        @pl.when(i == 0)
        def _():
            acc[...] = jnp.zeros_like(acc)
            db_ref[...] = jnp.zeros_like(db_ref)

        dh_ = dh_ref[...]
        dhb = dh_.astype(bf16)
        do_ref[...] = _dot_nt(dhb, w_ref[...]).astype(bf16)
        acc[...] += _dot_tn(o_ref[...], dhb)
        db_ref[...] += _colsum8(dh_)

        @pl.when(i == NT - 1)
        def _():
            dw_ref[...] = acc[...].astype(bf16)

    return _call(
        bg, body, name="attn_bwd_pre", grid=(NT,), in_specs=[_tile(), _tile(), _full((D, D))],
        out_specs=[_tile(), _full((D, D)), _full((8, D))],
        out_shape=[SDS((T, D), bf16), SDS((D, D), bf16), SDS((8, D), f32)],
        scratch_shapes=[pltpu.VMEM((D, D), f32)],
        compiler_params=_cp(dimension_semantics=("arbitrary",)),
    )(dh, o, wo)


def attn_core_bwd(q, do, k4, v4, probs, sink_w, bg=()):
    nb = TM // BLK

    def body(q_ref, do_ref, kc_ref, kp_ref, vc_ref, vp_ref, a_ref, as_ref, dq_ref, dk_ref, dv_ref, ds_ref):
        j = pl.program_id(0)
        n = pl.program_id(1)

        @pl.when(n == 0)
        def _():
            dk_ref[...] = jnp.zeros_like(dk_ref)
            dv_ref[...] = jnp.zeros_like(dv_ref)
            ds_ref[...] = jnp.zeros_like(ds_ref)

        lane8 = lax.broadcasted_iota(jnp.int32, (8, 128), 1)
        row8 = lax.broadcasted_iota(jnp.int32, (8, 128), 0)
        for b in range(nb):
            qs = _stack_heads(q_ref[b * BLK:(b + 1) * BLK, :])
            dos = _stack_heads(do_ref[b * BLK:(b + 1) * BLK, :])
            k2, v2, _ = _window_blocks(b, n, kc_ref, kp_ref, vc_ref, vp_ref)
            ab = a_ref[b]
            a = ab.astype(f32)
            asink = as_ref[b][:, 0:1].astype(f32)
            dp = _dot_nt(dos, v2)
            dd = jnp.sum(a * dp, axis=-1, keepdims=True)
            dsc = (a * (dp - dd) * (1.0 / math.sqrt(HEAD_DIM))).astype(bf16)
            t = asink * dd
            for g in range(Q_PER_KV):
                dsink = -jnp.sum(t[g * BLK:(g + 1) * BLK], axis=0, keepdims=True)
                ds_ref[...] += jnp.where((lane8 == g) & (row8 == 0), jnp.broadcast_to(dsink, (8, 128)), 0.0)
            dq_ref[b * BLK:(b + 1) * BLK, :] = _unstack_heads(_dot(dsc, k2))
            dk2 = _dot_tn(dsc, qs)
            dv2 = _dot_tn(ab, dos)
            cur = pl.ds(pl.multiple_of(n * TM + b * BLK, BLK), BLK)
            dk_ref[cur, :] += dk2[BLK:, :]
            dv_ref[cur, :] += dv2[BLK:, :]
            if b == 0:
                @pl.when(n > 0)
                def _():
                    prv = pl.ds(pl.multiple_of(n * TM - BLK, BLK), BLK)
                    dk_ref[prv, :] += dk2[:BLK, :]
                    dv_ref[prv, :] += dv2[:BLK, :]
            else:
                prv = pl.ds(pl.multiple_of(n * TM + (b - 1) * BLK, BLK), BLK)
                dk_ref[prv, :] += dk2[:BLK, :]
                dv_ref[prv, :] += dv2[:BLK, :]

    cur, prev = _attn_specs()
    col = pl.BlockSpec((T, 256), lambda j, n: (0, j))
    rows = Q_PER_KV * BLK
    return _call(
        bg, body, name="attn_core_bwd", grid=(N_KV, NT),
        in_specs=[cur, cur, cur, prev, cur, prev,
                  pl.BlockSpec((None, nb, rows, 2 * BLK), lambda j, n: (j, n, 0, 0)),
                  pl.BlockSpec((None, nb, rows, 128), lambda j, n: (j, n, 0, 0))],
        out_specs=[cur, col, col, pl.BlockSpec((None, 8, 128), lambda j, n: (j, 0, 0))],
        out_shape=[SDS((T, D), f32), SDS((T, D), f32), SDS((T, D), f32), SDS((N_KV, 8, 128), f32)],
        compiler_params=_cp(dimension_semantics=("arbitrary", "arbitrary")),
    )(q, do, k4, k4, v4, v4, probs, sink_w)


def attn_bwd_q(h, dh, dq, hn, g_mix, wq):
    def body(h_ref, dh_ref, dq_ref, hn_ref, gm_ref, wq_ref, out_ref, dwq_ref, dbq_ref, dgm_ref, aq):
        i = pl.program_id(0)

        @pl.when(i == 0)
        def _():
            aq[...] = jnp.zeros_like(aq)
            dbq_ref[...] = jnp.zeros_like(dbq_ref)
            dgm_ref[...] = jnp.zeros_like(dgm_ref)

        dq_ = dq_ref[...]
        dqb = dq_.astype(bf16)
        aq[...] += _dot_tn(hn_ref[...], dqb)
        dbq_ref[...] += _colsum8(dq_)
        dx, dg = _rms_bwd(h_ref[...], gm_ref[...], _dot_nt(dqb, wq_ref[...]))
        out_ref[...] = dh_ref[...] + dx
        dgm_ref[...] += _colsum8(dg)

        @pl.when(i == NT - 1)
        def _():
            dwq_ref[...] = aq[...].astype(bf16)

    vec = _full((8, D))
    mat = _full((D, D))
    return pl.pallas_call(
        body, name="attn_bwd_q", grid=(NT,),
        in_specs=[_tile()] * 4 + [_full((1, D)), mat],
        out_specs=[_tile(), mat, vec, vec],
        out_shape=[SDS((T, D), f32), SDS((D, D), bf16), SDS((8, D), f32), SDS((8, D), f32)],
        scratch_shapes=[pltpu.VMEM((D, D), f32)],
        compiler_params=_cp(dimension_semantics=("arbitrary",)),
    )(h, dh, dq, hn, g_mix, wq)


def attn_bwd_kv(h, dh, dk4, dv4, kvn, g_kv, wkv, spread):
    def body(h_ref, dh_ref, dk_ref, dv_ref, kvn_ref, gkv_ref, wkv_ref, sp_ref, out_ref, outb_ref, dw_ref, db_ref,
             dgkv_ref, acc):
        i = pl.program_id(0)

        @pl.when(i == 0)
        def _():
            for r in (acc, db_ref, dgkv_ref):
                r[...] = jnp.zeros_like(r)

        dkv = jnp.concatenate([_dot_nt(dk_ref[...].astype(bf16), sp_ref[...]),
                               _dot_nt(dv_ref[...].astype(bf16), sp_ref[...])], axis=1)
        dkvb = dkv.astype(bf16)
        acc[...] += _dot_tn(kvn_ref[...], dkvb)
        db_ref[...] += _colsum8(dkv)
        dx, dg = _rms_bwd(h_ref[...], gkv_ref[...], _dot_nt(dkvb, wkv_ref[...]))
        out = dh_ref[...] + dx
        out_ref[...] = out
        outb_ref[...] = out.astype(bf16)
        dgkv_ref[...] += _colsum8(dg)

        @pl.when(i == NT - 1)
        def _():
            dw_ref[...] = acc[...].astype(bf16)

    return pl.pallas_call(
        body, name="attn_bwd_kv", grid=(NT,),
        in_specs=[_tile()] * 5 + [_full((1, D)), _full((D, 512)), _full((256, D))],
        out_specs=[_tile(), _tile(), _full((D, 512)), _full((8, 512)), _full((8, D))],
        out_shape=[SDS((T, D), f32), SDS((T, D), bf16), SDS((D, 512), bf16), SDS((8, 512), f32), SDS((8, D), f32)],
        scratch_shapes=[pltpu.VMEM((D, 512), f32)],
        compiler_params=_cp(dimension_semantics=("arbitrary",)),
    )(h, dh, dk4, dv4, kvn, g_kv, wkv, spread)


def final_loss(h, g, target):
    def body(h_ref, g_ref, t_ref, loss_ref, dh_ref, dhb_ref, dg_ref):
        i = pl.program_id(0)

        @pl.when(i == 0)
        def _():
            loss_ref[...] = jnp.zeros_like(loss_ref)
            dg_ref[...] = jnp.zeros_like(dg_ref)

        h_ = h_ref[...]
        g_ = g_ref[...]
        y, _ = _rms(h_, g_)
        diff = y - t_ref[...]
        per_tok = jnp.mean(diff * diff, axis=-1, keepdims=True)
        tot = 0.5 * jnp.sum(per_tok, axis=0, keepdims=True)
        lane = lax.broadcasted_iota(jnp.int32, (8, 128), 1)
        row = lax.broadcasted_iota(jnp.int32, (8, 128), 0)
        loss_ref[...] += jnp.where((lane == 0) & (row == 0), jnp.broadcast_to(tot, (8, 128)), 0.0)
        dx, dgt = _rms_bwd(h_, g_, diff * (1.0 / D))
        dh_ref[...] = dx
        dhb_ref[...] = dx.astype(bf16)
        dg_ref[...] += _colsum8(dgt)

    return pl.pallas_call(
        body, name="final_loss", grid=(NT,), in_specs=[_tile(), _full((1, D)), _tile()],
        out_specs=[_full((8, 128)), _tile(), _tile(), _full((8, D))],
        out_shape=[SDS((8, 128), f32), SDS((T, D), f32), SDS((T, D), bf16), SDS((8, D), f32)],
        compiler_params=_cp(dimension_semantics=("arbitrary",)),
    )(h, g, target)


def fwd_bwd(x, target, p, shards, opt, core, chip):
    row = lambda v: v.reshape(1, -1)
    (lam, bm, cm), prep_vjp = jax.vjp(s5_discretize, p["s5_a_re"][0], p["s5_a_im"][0], p["s5_log_dt"][0],
                                      p["s5_b_re"][0], p["s5_b_im"][0], p["s5_c_re"][0], p["s5_c_im"][0])
    bmb, cmb = bm.astype(bf16), cm.astype(bf16)
    lam = jnp.concatenate([lam, lam * jnp.array([1.0, -1.0], f32).reshape(1, 2, 1, 1)], axis=1)
    g_mix0, g_mix1 = row(p["norm_mix"][0]), row(p["norm_mix"][1])
    g_mlp0, g_mlp1 = row(p["norm_mlp"][0]), row(p["norm_mlp"][1])
    g_kv, g_fin = row(p["norm_kv"]), row(p["norm_final"])
    bq, bo = p["b_q"], p["b_o"]
    bkv = row(p["b_kv"])
    spread = _spread4()
    sinks = p["sinks"].reshape(16)

    wglu, gvec = sc_gather([shards["s5_w_glu"], shards["vecs"]], 3, "sc_gather_s5")
    win0, wout0 = sc_gather([shards["w_in0"], shards["w_out0"]], 14, "sc_gather_mlp0")
    wkv, wq, wo = sc_gather([shards["w_kv"], shards["w_q"], shards["w_o"]], 4, "sc_gather_attn")
    win1, wout1 = sc_gather([shards["w_in1"], shards["w_out1"]], 5, "sc_gather_mlp1")
    xp = x
    hn0 = s5_pre(xp, g_mix0)
    ys, xs = s5_core_fwd(hn0, bmb, lam, cmb)
    d_skip = gvec[:, 0, :128].reshape(1, D)
    bglu = gvec[:, 0, 128:].reshape(1, 2 * D)
    y, z, h1 = s5_post(ys, xp, g_mix0, d_skip, wglu, bglu)
    hm0, r0, h2p = mlp_fwd(h1, g_mlp0, win0, wout0, 0)
    wkv, wq, wo = wkv.reshape(D, 512), wq.reshape(D, D), wo.reshape(D, D)
    h2 = h2p
    kvn, hn1, k4, v4, q = attn_pre(h2, g_kv, g_mix1, wkv, bkv, spread, wq, bq)
    o, probs, sink_w = attn_core_fwd(q, k4, v4, sinks)
    h3 = attn_post(h2, o, wo, bo)
    hm1, r1, h4 = mlp_fwd(h3, g_mlp1, win1, wout1, 1)
    loss, dh4, dh4b, dg_fin = final_loss(h4, g_fin, target)

    def pair_sums(names, grads, cid, before):
        r1 = sc_comm(BgPair(grads), cid, "sc_pair_" + names[0])
        parts = [add_pairs(g, r, core, f"add_pairs_{n}") for n, g, r in zip(names, grads, r1)]
        before, parts = lax.optimization_barrier((before, parts))
        return before, parts

    def across_chips(names, parts, cid):
        return list(zip(parts, sc_comm(BgChips(parts), cid, "sc_chips_" + names[0])))

    dh3, dwin1, dwout1, dg_mlp1 = mlp_bwd(h3, hm1, r1, g_mlp1, dh4, dh4b, win1, wout1, 1)
    do, dwo, dbo = attn_bwd_pre(dh3, o, wo)
    do, parts = pair_sums(["w_in1", "w_out1"], [dwin1, dwout1], 6, do)
    rs_in1, rs_out1 = across_chips(["w_in1", "w_out1"], parts, 7)
    dq, dk4, dv4, dsink = attn_core_bwd(q, do, k4, v4, probs, sink_w)
    dh2, dwq, dbq, dg_mix1 = attn_bwd_q(h2, dh3, dq, hn1, g_mix1, wq)
    dh2, dh2b, dwkv, dbkv, dg_kv = attn_bwd_kv(h2, dh2, dk4, dv4, kvn, g_kv, wkv, spread)
    dh2p, dh2pb = dh2, dh2b
    big = {}
    a_in1 = adam_big(*opt["w_mlp_in"], *rs_in1, chip, "adam_w_mlp_in1", layer=1)
    a_out1 = adam_big(*opt["w_mlp_out"], *rs_out1, chip, "adam_w_mlp_out1", layer=1)
    dh2p, a_in1, a_out1 = lax.optimization_barrier((dh2p, a_in1, a_out1))
    names = ["w_kv", "w_q", "w_o"]
    dh2p, parts = pair_sums(names, [dwkv.reshape(NDEV, 128, 512), dwq.reshape(NDEV, 128, D),
                                    dwo.reshape(NDEV, 128, D)], 8, dh2p)
    rs_attn = across_chips(names, parts, 9)
    dh1, dwin0, dwout0, dg_mlp0 = mlp_bwd(h1, hm0, r0, g_mlp0, dh2p, dh2pb, win0, wout0, 0)
    a_attn = [adam_big(*opt[n], *rs, chip, f"adam_{n}") for n, rs in zip(names, rs_attn)]
    dh1, a_attn = lax.optimization_barrier((dh1, a_attn))
    big.update(zip(names, a_attn))
    dy, dwglu, dbglu = s5_post_bwd(dh1, y, z, wglu)
    dy, parts = pair_sums(["w_in0", "w_out0"], [dwin0, dwout0], 10, dy)
    rs_in0, rs_out0 = across_chips(["w_in0", "w_out0"], parts, 11)
    du, dbm, dcmt, dlam = s5_core_bwd(hn0, dy, xs, bmb, lam, cmb)
    du, parts = pair_sums(["s5_w_glu"], [dwglu], 12, du)
    rs_glu, = across_chips(["s5_w_glu"], parts, 13)
    dxp, dg_mix0, dd = s5_pre_bwd(xp, g_mix0, du, dy, d_skip, dh1)
    big["w_mlp_in"] = adam_big(*opt["w_mlp_in"], *rs_in0, chip, "adam_w_mlp_in0", layer=0, prev=a_in1)
    big["w_mlp_out"] = adam_big(*opt["w_mlp_out"], *rs_out0, chip, "adam_w_mlp_out0", layer=0, prev=a_out1)
    big["s5_w_glu"] = adam_big(*opt["s5_w_glu"], *rs_glu, chip, "adam_s5_w_glu")
    grad_x = dxp
    da_re, da_im, dlog_dt, db_re, db_im, dc_re, dc_im = prep_vjp((dlam, dbm, dcmt.transpose(0, 2, 1)))

    def lanes(v_):
        v_ = v_.reshape(1, -1)
        return jnp.pad(v_, ((0, 0), (0, D - v_.shape[1])))

    small = jnp.concatenate([
        dg_mix0[0:1], dg_mix1[0:1], dg_mlp0[0:1], dg_mlp1[0:1], dg_kv[0:1], dg_fin[0:1], dd[0:1], dbq[0:1], dbo[0:1],
        dbglu[0:1].reshape(2, D), lanes(dbkv[0:1]),
        lanes(dsink[:, 0, :Q_PER_KV]), lanes(dlog_dt), lanes(loss[0:1, 0:1]), jnp.zeros((1, D), f32),
        da_re.reshape(4, D), da_im.reshape(4, D),
        db_re.transpose(0, 2, 1).reshape(64, D), db_im.transpose(0, 2, 1).reshape(64, D),
        dc_re.reshape(64, D), dc_im.reshape(64, D)], axis=0)
    small, big["w_mlp_in"], big["w_mlp_out"] = lax.optimization_barrier((small, big["w_mlp_in"], big["w_mlp_out"]))
    return loss, grad_x, small, big


def _row_tile(r, c):
    return min(r, max(8, (512 * 1024) // c))


def add_pairs(g, r1, core, name):
    _, R, C = g.shape
    tr = _row_tile(R, C)

    def body(core_ref, g_ref, r_ref, o_ref):
        o_ref[...] = (g_ref[...].astype(f32) + r_ref[...].astype(f32)).astype(bf16)

    return pl.pallas_call(
        body, name=name, out_shape=SDS((4, R, C), bf16),
        grid_spec=pltpu.PrefetchScalarGridSpec(
            num_scalar_prefetch=1, grid=(4, R // tr),
            in_specs=[pl.BlockSpec((None, tr, C), lambda k, i, core: (2 * k + core[0], i, 0)),
                      pl.BlockSpec((None, tr, C), lambda k, i, core: (k, i, 0))],
            out_specs=pl.BlockSpec((None, tr, C), lambda k, i, core: (k, i, 0))),
        compiler_params=_cp(dimension_semantics=("arbitrary", "arbitrary")),
    )(core, g, r1)


def _adamw(w, g, m, v):
    m = ADAM_B1 * m + (1.0 - ADAM_B1) * g
    v = ADAM_B2 * v + (1.0 - ADAM_B2) * (g * g)
    m_hat = m / (1.0 - ADAM_B1 ** ADAM_STEP)
    v_hat = v / (1.0 - ADAM_B2 ** ADAM_STEP)
    delta = -ADAM_LR * (m_hat / (jnp.sqrt(v_hat) + ADAM_EPS) + ADAM_WD * w)
    return delta, m, v


def adam_big(w, m, v, part, r2, chip, name, layer=0, prev=None):
    L, R, C = w.shape
    tr = _row_tile(R, C)

    def body(chip_ref, w_ref, m_ref, v_ref, p_ref, r_ref, *rest):
        g_out, d_out, m_out, v_out = rest[-4:]
        g = p_ref[...].astype(f32) + r_ref[0].astype(f32) + r_ref[1].astype(f32) + r_ref[2].astype(f32)
        d, m_, v_ = _adamw(w_ref[...], g, m_ref[...], v_ref[...])
        g_out[...] = g
        d_out[...] = d
        m_out[...] = m_
        v_out[...] = v_

    blk = pl.BlockSpec((None, tr, C), lambda i, chip: (layer, i, 0))
    extra = [] if prev is None else list(prev)
    return pl.pallas_call(
        body, name=name, out_shape=[SDS((L, R, C), f32)] * 4,
        grid_spec=pltpu.PrefetchScalarGridSpec(
            num_scalar_prefetch=1, grid=(R // tr,),
            in_specs=[blk, blk, blk,
                      pl.BlockSpec((None, tr, C), lambda i, chip: (chip[0], i, 0)),
                      pl.BlockSpec((3, tr, C), lambda i, chip: (0, i, 0))] + [_ANY] * len(extra),
            out_specs=[blk] * 4),
        input_output_aliases={6 + k: k for k in range(len(extra))},
        compiler_params=_cp(dimension_semantics=("arbitrary",)),
    )(chip, w, m, v, part, r2, *extra)


def allreduce_small(buf, chips=None):
    shp = buf.shape
    half = (shp[0] // 16) * 8
    parts = (pl.ds(0, half), pl.ds(half, shp[0] - half))
    n_c = 0 if chips is None else len(chips.arrs)

    def body(in_ref, *refs):
        c_in, out_ref, c_out = refs[:n_c], refs[n_c], refs[n_c + 1:2 * n_c + 1]
        acc1, acc2, r0, r1, r2, send_sems, recv_sems = refs[2 * n_c + 1:2 * n_c + 8]
        c_sems = refs[2 * n_c + 8:]
        if chips is not None:
            chips.start(c_in, c_out, c_sems)
        x, y, c = _pos()
        across = [(1 - x, y, c), (x, 1 - y, c)]

        def exchange(src, rcv, dst, copies):
            cps = [pltpu.make_async_remote_copy(
                src_ref=src.at[rows], dst_ref=rcv.at[rows], send_sem=send_sems.at[k], recv_sem=recv_sems.at[k],
                device_id=peer, device_id_type=MESH) for k, rows, peer in copies]
            for cp in cps:
                cp.start()
            for cp in cps:
                cp.wait()
            dst[...] = src[...] + rcv[...]

        exchange(in_ref, r0, acc1, [(0, pl.ds(0, shp[0]), (x, y, 1 - c))])
        exchange(acc1, r1, acc2, [(1, parts[0], across[0]), (2, parts[1], across[1])])
        exchange(acc2, r2, out_ref, [(3, parts[0], across[1]), (4, parts[1], across[0])])
        if chips is not None:
            chips.finish(c_in, c_out, c_sems)

    vm = pl.BlockSpec(memory_space=pltpu.VMEM)
    res = pl.pallas_call(
        body, name="allreduce_small", in_specs=[vm] + [_ANY] * n_c, out_specs=[vm] + [_ANY] * n_c,
        out_shape=[SDS(shp, f32)] + ([] if chips is None else chips.out_shape),
        scratch_shapes=[pltpu.VMEM(shp, f32)] * 5 + [pltpu.SemaphoreType.DMA((5,)), pltpu.SemaphoreType.DMA((5,))]
        + ([] if chips is None else chips.scratch),
    )(buf, *([] if chips is None else chips.arrs))
    if chips is not None:
        chips.result = list(res[1:])
    return res[0]


SMALL_ROWS = {'norm_mix': (0, 2, D), 'norm_mlp': (2, 2, D), 'norm_kv': (4, 1, D), 'norm_final': (5, 1, D),
              's5_d': (6, 1, D), 'b_q': (7, 1, D), 'b_o': (8, 1, D), 's5_b_glu': (9, 2, D), 'b_kv': (11, 1, 512),
              'sinks': (12, 1, 16), 's5_log_dt': (13, 1, 64), 's5_a_re': (16, 4, D), 's5_a_im': (20, 4, D),
              's5_b_re': (24, 64, D), 's5_b_im': (88, 64, D), 's5_c_re': (152, 64, D), 's5_c_im': (216, 64, D)}
LOSS_ROW = 14
ROW_PARAMS = ['norm_mix', 'norm_mlp', 'norm_kv', 'norm_final', 'b_q', 'b_o', 'b_kv', 'sinks', 's5_log_dt']
SHARD_PARAMS = ['s5_d', 's5_b_glu']
S5_PARAMS = ['s5_a_re', 's5_a_im', 's5_b_re', 's5_b_im', 's5_c_re', 's5_c_im']


def adam_small(dev, gsum, s5_grads, w, m, v):
    names = ROW_PARAMS + SHARD_PARAMS + S5_PARAMS
    n_g = len(ROW_PARAMS) + len(SHARD_PARAMS)

    def body(dev_ref, gs_ref, *refs):
        pos = [0]

        def take(k):
            r = refs[pos[0]:pos[0] + k]
            pos[0] += k
            return r

        g5 = take(len(S5_PARAMS))
        wr, mr, vr = take(len(names)), take(len(names)), take(len(names))
        g_out = take(n_g)
        d_out, m_out, v_out = take(len(names)), take(len(names)), take(len(names))
        dv = dev_ref[0]
        for i, n in enumerate(names):
            if n in S5_PARAMS:
                g = g5[S5_PARAMS.index(n)][...]
            elif n in SHARD_PARAMS:
                r0, _, _ = SMALL_ROWS[n]
                ln = wr[i].shape[1]
                g = jnp.zeros((1, ln), f32)
                for k in range(NDEV):
                    off = k * ln
                    piece = gs_ref[r0 + off // D:r0 + off // D + 1, off % D:off % D + ln]
                    g = g + jnp.where(dv == k, piece, 0.0)
                g_out[i][...] = g
            else:
                r0, nr, nl = SMALL_ROWS[n]
                g = gs_ref[r0:r0 + nr, 0:nl]
                g_out[i][...] = g
            d, m_, v_ = _adamw(wr[i][...], g, mr[i][...], vr[i][...])
            d_out[i][...] = d
            m_out[i][...] = m_
            v_out[i][...] = v_

    vm = pl.BlockSpec(memory_space=pltpu.VMEM)
    ins = [s5_grads[n] for n in S5_PARAMS] + [d[n] for d in (w, m, v) for n in names]
    shapes = [SDS(w[n].shape, f32) for n in names]
    res = pl.pallas_call(
        body, name="adam_small", in_specs=[pl.BlockSpec(memory_space=pltpu.SMEM)] + [vm] * (1 + len(ins)),
        out_specs=[vm] * (n_g + 3 * len(names)), out_shape=shapes[:n_g] + shapes * 3,
        compiler_params=_cp(),
    )(dev, gsum, *ins)
    g_o = dict(zip(names[:n_g], res[:n_g]))
    rest = res[n_g:]
    k = len(names)
    return g_o, dict(zip(names, rest[:k])), dict(zip(names, rest[k:2 * k])), dict(zip(names, rest[2 * k:]))


WEIGHTS = ['norm_mix', 'norm_mlp', 'norm_kv', 'norm_final', 's5_a_re', 's5_a_im', 's5_log_dt', 's5_b_re', 's5_b_im',
           's5_c_re', 's5_c_im', 's5_d', 's5_w_glu', 's5_b_glu', 'w_kv', 'b_kv', 'w_q', 'b_q', 'sinks', 'w_o', 'b_o',
           'w_mlp_in', 'w_mlp_out']
BIG = ['s5_w_glu', 'w_kv', 'w_q', 'w_o', 'w_mlp_in', 'w_mlp_out']
BIG_2D = {'s5_w_glu': (D, 256), 'w_kv': (128, 512), 'w_q': (128, D), 'w_o': (128, D), 'w_mlp_in': (2 * D, 512),
          'w_mlp_out': (2 * 512, D)}
SMALL = [n for n in WEIGHTS if n not in BIG]


def kernel(x, norm_mix, norm_mlp, norm_kv, norm_final, s5_a_re, s5_a_im, s5_log_dt, s5_b_re, s5_b_im, s5_c_re, s5_c_im, s5_d, s5_w_glu, s5_b_glu, w_kv, b_kv, w_q, b_q, sinks, w_o, b_o, w_mlp_in, w_mlp_out, loss_target, m_norm_mix, m_norm_mlp, m_norm_kv, m_norm_final, m_s5_a_re, m_s5_a_im, m_s5_log_dt, m_s5_b_re, m_s5_b_im, m_s5_c_re, m_s5_c_im, m_s5_d, m_s5_w_glu, m_s5_b_glu, m_w_kv, m_b_kv, m_w_q, m_b_q, m_sinks, m_w_o, m_b_o, m_w_mlp_in, m_w_mlp_out, v_norm_mix, v_norm_mlp, v_norm_kv, v_norm_final, v_s5_a_re, v_s5_a_im, v_s5_log_dt, v_s5_b_re, v_s5_b_im, v_s5_c_re, v_s5_c_im, v_s5_d, v_s5_w_glu, v_s5_b_glu, v_w_kv, v_b_kv, v_w_q, v_b_q, v_sinks, v_w_o, v_b_o, v_w_mlp_in, v_w_mlp_out):
    w = dict(norm_mix=norm_mix, norm_mlp=norm_mlp, norm_kv=norm_kv, norm_final=norm_final, s5_a_re=s5_a_re,
             s5_a_im=s5_a_im, s5_log_dt=s5_log_dt, s5_b_re=s5_b_re, s5_b_im=s5_b_im, s5_c_re=s5_c_re, s5_c_im=s5_c_im,
             s5_d=s5_d, s5_w_glu=s5_w_glu, s5_b_glu=s5_b_glu, w_kv=w_kv, b_kv=b_kv, w_q=w_q, b_q=b_q, sinks=sinks,
             w_o=w_o, b_o=b_o, w_mlp_in=w_mlp_in, w_mlp_out=w_mlp_out)
    m = dict(norm_mix=m_norm_mix, norm_mlp=m_norm_mlp, norm_kv=m_norm_kv, norm_final=m_norm_final, s5_a_re=m_s5_a_re,
             s5_a_im=m_s5_a_im, s5_log_dt=m_s5_log_dt, s5_b_re=m_s5_b_re, s5_b_im=m_s5_b_im, s5_c_re=m_s5_c_re,
             s5_c_im=m_s5_c_im, s5_d=m_s5_d, s5_w_glu=m_s5_w_glu, s5_b_glu=m_s5_b_glu, w_kv=m_w_kv, b_kv=m_b_kv,
             w_q=m_w_q, b_q=m_b_q, sinks=m_sinks, w_o=m_w_o, b_o=m_b_o, w_mlp_in=m_w_mlp_in, w_mlp_out=m_w_mlp_out)
    v = dict(norm_mix=v_norm_mix, norm_mlp=v_norm_mlp, norm_kv=v_norm_kv, norm_final=v_norm_final, s5_a_re=v_s5_a_re,
             s5_a_im=v_s5_a_im, s5_log_dt=v_s5_log_dt, s5_b_re=v_s5_b_re, s5_b_im=v_s5_b_im, s5_c_re=v_s5_c_re,
             s5_c_im=v_s5_c_im, s5_d=v_s5_d, s5_w_glu=v_s5_w_glu, s5_b_glu=v_s5_b_glu, w_kv=v_w_kv, b_kv=v_b_kv,
             w_q=v_w_q, b_q=v_b_q, sinks=v_sinks, w_o=v_w_o, b_o=v_b_o, w_mlp_in=v_w_mlp_in, w_mlp_out=v_w_mlp_out)
    xi, yi, ci = _pos()
    dev = 4 * xi + 2 * yi + ci
    core = ci.reshape(1).astype(jnp.int32)
    chip = (2 * xi + yi).reshape(1).astype(jnp.int32)

    shards = {
        "s5_w_glu": s5_w_glu[0].astype(bf16), "w_kv": w_kv.astype(bf16), "w_q": w_q[0].astype(bf16),
        "w_o": w_o[0].astype(bf16), "w_in0": w_mlp_in[0].astype(bf16), "w_in1": w_mlp_in[1].astype(bf16),
        "w_out0": w_mlp_out[0].astype(bf16), "w_out1": w_mlp_out[1].astype(bf16),
        "vecs": jnp.broadcast_to(jnp.concatenate([s5_d, s5_b_glu], axis=1), (8, 384)),
    }
    as3d = lambda a, n: a if a.ndim == 3 and a.shape[0] == 2 else a.reshape((1,) + BIG_2D[n])
    opt = {n: (as3d(w[n], n), as3d(m[n], n), as3d(v[n], n)) for n in BIG}
    _, grad_x, grads, big = fwd_bwd(x[0], loss_target[0], {n: w[n] for n in SMALL}, shards, opt, core, chip)

    gsum = allreduce_small(grads)

    out_g, out_d, out_m, out_v = {}, {}, {}, {}
    for n in BIG:
        out_g[n], out_d[n], out_m[n], out_v[n] = [r.reshape(w[n].shape) for r in big[n]]

    loss = gsum[LOSS_ROW, 0]
    swapped = ("s5_b_re", "s5_b_im")
    swap = lambda a: a.transpose(0, 1, 3, 2)

    def kernel_side(d):
        d = {n: (d[n].reshape(1, -1) if d[n].ndim == 1 else d[n]) for n in SMALL}
        d.update({n: swap(d[n]) for n in swapped})
        return d

    s5_g = {}
    for n in S5_PARAMS:
        r0, nr, _ = SMALL_ROWS[n]
        s5_g[n] = gsum[r0:r0 + nr].reshape((1, 64, 16, 64) if n in swapped else w[n].shape)
        out_g[n] = s5_g[n]
    g_s, d_s, m_s, v_s = adam_small(dev.reshape(1).astype(jnp.int32), gsum, s5_g, kernel_side(w), kernel_side(m),
                                    kernel_side(v))
    for src, dst in ((g_s, out_g), (d_s, out_d), (m_s, out_m), (v_s, out_v)):
        dst.update(src)
    for dst in (out_g, out_d, out_m, out_v):
        for n in SMALL:
            dst[n] = (swap(dst[n]) if n in swapped else dst[n]).reshape(w[n].shape)

    return (loss, grad_x[None], *[out_g[n] for n in WEIGHTS], *[out_d[n] for n in WEIGHTS],
            *[out_m[n] for n in WEIGHTS], *[out_v[n] for n in WEIGHTS])
```

```python
import functools
import math

import jax
import jax.numpy as jnp
from jax import lax
from jax.experimental import pallas as pl
from jax.experimental.pallas import tpu as pltpu
from jax.experimental.pallas import tpu_sc as plsc

f32 = jnp.float32
bf16 = jnp.bfloat16
SDS = jax.ShapeDtypeStruct

T = 2048
D = 1024
NDEV = 8
NORM_EPS = 1e-5
S5_G, S5_C, S5_P = 64, 16, 64
S5_SUB = 8
S5_CH = 8
S5_STEPS = T // S5_CH
DT_MIN_LAMBDA = -1e-4
HEAD_DIM = 64
N_KV = 4
Q_PER_KV = 4
BLK = 128
D_FF_SHARD = 512
ADAM_LR, ADAM_B1, ADAM_B2, ADAM_EPS, ADAM_WD, ADAM_STEP = 0.001, 0.9, 0.999, 1e-08, 0.01, 10
VMEM_LIMIT = 56 * 1024 * 1024
MESH = pl.DeviceIdType.MESH


def _cp(**kw):
    return pltpu.CompilerParams(vmem_limit_bytes=VMEM_LIMIT, **kw)


def _dot(a, b):
    return jnp.dot(a, b, preferred_element_type=f32)


def _dot_nt(a, b):
    return lax.dot_general(a, b, (((1,), (1,)), ((), ())), preferred_element_type=f32)


def _dot_tn(a, b):
    return lax.dot_general(a, b, (((0,), (0,)), ((), ())), preferred_element_type=f32)


def _rms(x, g):
    r = lax.rsqrt(jnp.mean(x * x, axis=-1, keepdims=True) + NORM_EPS)
    return x * r * g, r


def _rms_bwd(x, g, dy):
    r = lax.rsqrt(jnp.mean(x * x, axis=-1, keepdims=True) + NORM_EPS)
    u = dy * g
    dx = r * u - (r * r * r) * x * jnp.mean(u * x, axis=-1, keepdims=True)
    return dx, dy * x * r


def _colsum8(v):
    s = jnp.sum(v, axis=0, keepdims=True)
    row = lax.broadcasted_iota(jnp.int32, (8, v.shape[1]), 0)
    return jnp.where(row == 0, jnp.broadcast_to(s, (8, v.shape[1])), 0.0)


def _full(shape):
    nd = len(shape)
    return pl.BlockSpec(shape, lambda *_: (0,) * nd, pipeline_mode=pl.Buffered(1))


_ANY = pl.BlockSpec(memory_space=pl.ANY)


def _pos():
    return lax.axis_index("x"), lax.axis_index("y"), lax.axis_index("c")


def _other_chips(x, y):
    return [(1 - x, y), (x, 1 - y), (1 - x, 1 - y)]


class BgGather:
    SIB, XN, YN, FWD_Y, FWD_X, SIB_X, SIB_Y, SIB_D = range(8)

    def __init__(self, arrs, mids=(0.5, 0.75)):
        n = len(arrs)
        self.arrs = list(arrs)
        self.out_shape = [SDS((NDEV,) + a.shape, a.dtype) for a in arrs]
        self.scratch = [pltpu.SemaphoreType.DMA((n, 8)), pltpu.SemaphoreType.DMA((n, 8)),
                        pltpu.SemaphoreType.DMA((n,))]
        self.mids = mids
        self.result = None

    @staticmethod
    def peers(x, y, c):
        return [(x, y, 1 - c), (1 - x, y, c), (x, 1 - y, c)]

    def mid_steps(self, nsteps):
        at = lambda f: min(nsteps - 1, max(0, int(f * nsteps) - 1))
        return [(at(self.mids[0]), self.mid), (max(at(self.mids[0]), at(self.mids[1])), self.mid2)]

    def _halves(self, a):
        rows = self.arrs[a].shape[0]
        cut = (rows // 32) * 16 if rows >= 32 else rows
        return (0, cut), (cut, rows - cut)

    def _copy(self, ins, outs, sems, a, k, block, to, own=False, part=None):
        slot = 4 * block[0] + 2 * block[1] + block[2]
        rows = pl.ds(0, self.arrs[a].shape[0]) if part is None else pl.ds(*self._halves(a)[part])
        dst = outs[a].at[slot, rows]
        return pltpu.make_async_remote_copy(
            src_ref=ins[a].at[rows] if own else dst, dst_ref=dst, send_sem=sems[0].at[a, k],
            recv_sem=sems[1].at[a, k], device_id=to, device_id_type=MESH)

    def _mine(self, ins, outs, sems):
        x, y, c = _pos()
        return [pltpu.make_async_copy(ins[a], outs[a].at[4 * x + 2 * y + c], sems[2].at[a])
                for a in range(len(self.arrs))]

    def _split(self, a):
        return self._halves(a)[1][1] > 0

    def _sends(self, ins, outs, sems, phase):
        x, y, c = _pos()
        me, sib, xn, yn, dg = (x, y, c), (x, y, 1 - c), (1 - x, y, c), (x, 1 - y, c), (1 - x, 1 - y, c)
        cps = []
        for a in range(len(self.arrs)):
            cp = lambda k, block, to, **kw: self._copy(ins, outs, sems, a, k, block, to, **kw)
            if phase == 0:
                cps += [cp(self.SIB, me, sib, own=True), cp(self.XN, me, xn, own=True), cp(self.YN, me, yn, own=True)]
            elif phase == 1:
                cps.append(cp(self.FWD_Y, xn, yn, part=0))
                if self._split(a):
                    cps.append(cp(self.FWD_X, yn, xn, part=1))
                cps += [cp(self.SIB_X, xn, sib), cp(self.SIB_Y, yn, sib)]
            else:
                cps.append(cp(self.SIB_D, dg, sib))
        return cps

    def _arrivals(self, ins, outs, sems, phase):
        x, y, c = _pos()
        me, xn, yn, dg = (x, y, c), (1 - x, y, c), (x, 1 - y, c), (1 - x, 1 - y, c)
        cps = []
        for a in range(len(self.arrs)):
            cp = lambda k, block, **kw: self._copy(ins, outs, sems, a, k, block, me, **kw)
            if phase == 1:
                cps += [cp(self.XN, xn), cp(self.YN, yn)]
            elif phase == 2:
                cps.append(cp(self.FWD_Y, dg, part=0))
                if self._split(a):
                    cps.append(cp(self.FWD_X, dg, part=1))
            else:
                cps += [cp(self.SIB, (x, y, 1 - c)), cp(self.SIB_X, (1 - x, y, 1 - c)),
                        cp(self.SIB_Y, (x, 1 - y, 1 - c)), cp(self.SIB_D, (1 - x, 1 - y, 1 - c))]
        return cps

    def start(self, ins, outs, sems):
        for cp in self._mine(ins, outs, sems) + self._sends(ins, outs, sems, 0):
            cp.start()

    def mid(self, ins, outs, sems):
        for cp in self._arrivals(ins, outs, sems, 1):
            cp.wait_recv()
        for cp in self._sends(ins, outs, sems, 1):
            cp.start()

    def mid2(self, ins, outs, sems):
        for cp in self._arrivals(ins, outs, sems, 2):
            cp.wait_recv()
        for cp in self._sends(ins, outs, sems, 2):
            cp.start()

    def finish(self, ins, outs, sems):
        for cp in self._arrivals(ins, outs, sems, 3):
            cp.wait_recv()
        for ph in range(3):
            for cp in self._sends(ins, outs, sems, ph):
                cp.wait_send()
        for cp in self._mine(ins, outs, sems):
            cp.wait()


def sc_comm(g, collective_id, name):
    srcs = [jax.new_ref(a, memory_space=pltpu.MemorySpace.HBM) for a in g.arrs]
    dsts = [jax.empty_ref(s, memory_space=pltpu.MemorySpace.HBM) for s in g.out_shape]

    @pl.kernel(mesh=plsc.ScalarSubcoreMesh(axis_name="sequencer", num_cores=1), name=name,
               scratch_types=tuple(g.scratch), compiler_params=pltpu.CompilerParams(collective_id=collective_id))
    def launch(*sems):
        peers = g.peers(*_pos())
        barrier = pltpu.get_barrier_semaphore()
        for peer in peers:
            pl.semaphore_signal(barrier, inc=1, device_id=peer, device_id_type=MESH)
        pl.semaphore_wait(barrier, len(peers))
        g.start(srcs, dsts, sems)
        for _, phase in g.mid_steps(1):
            phase(srcs, dsts, sems)
        g.finish(srcs, dsts, sems)

    launch()
    return [d[...] for d in dsts]


def sc_gather(arrs, collective_id, name):
    return sc_comm(BgGather(arrs), collective_id, name)


class BgPair:
    def __init__(self, arrs):
        n = len(arrs)
        self.arrs = list(arrs)
        self.out_shape = [SDS((4,) + a.shape[1:], a.dtype) for a in arrs]
        self.scratch = [pltpu.SemaphoreType.DMA((n, 4)), pltpu.SemaphoreType.DMA((n, 4))]
        self.result = None

    @staticmethod
    def peers(x, y, c):
        return [(x, y, 1 - c)]

    def mid_steps(self, nsteps):
        return []

    def _copies(self, ins, outs, sems):
        x, y, c = _pos()
        return [pltpu.make_async_remote_copy(
            src_ref=ins[a].at[2 * k + 1 - c], dst_ref=outs[a].at[k], send_sem=sems[0].at[a, k],
            recv_sem=sems[1].at[a, k], device_id=(x, y, 1 - c), device_id_type=MESH)
            for a in range(len(self.arrs)) for k in range(4)]

    def start(self, ins, outs, sems):
        for cp in self._copies(ins, outs, sems):
            cp.start()

    def finish(self, ins, outs, sems):
        cps = self._copies(ins, outs, sems)
        for cp in cps:
            cp.wait_recv()
        for cp in cps:
            cp.wait_send()


class BgChips(BgPair):
    def __init__(self, arrs):
        n = len(arrs)
        self.arrs = list(arrs)
        self.out_shape = [SDS((3,) + a.shape[1:], a.dtype) for a in arrs]
        self.scratch = [pltpu.SemaphoreType.DMA((n, 3)), pltpu.SemaphoreType.DMA((n, 3))]
        self.result = None

    @staticmethod
    def peers(x, y, c):
        return [(px, py, c) for px, py in _other_chips(x, y)]

    def _copies(self, ins, outs, sems):
        x, y, c = _pos()
        return [pltpu.make_async_remote_copy(
            src_ref=ins[a].at[2 * px + py], dst_ref=outs[a].at[r], send_sem=sems[0].at[a, r],
            recv_sem=sems[1].at[a, r], device_id=(px, py, c), device_id_type=MESH)
            for a in range(len(self.arrs)) for r, (px, py) in enumerate(_other_chips(x, y))]


def _call(bgs, body, *, name, grid, in_specs, out_specs, out_shape, scratch_shapes=(), compiler_params=None):
    single = not isinstance(out_shape, (list, tuple))
    out_specs_l = [out_specs] if single else list(out_specs)
    out_shape_l = [out_shape] if single else list(out_shape)
    bgs = [b for b in (bgs or []) if b is not None]
    n_in, n_out, n_sc = len(in_specs), len(out_shape_l), len(scratch_shapes)
    nsteps = math.prod(grid)
    b_in_specs = [b.in_specs(grid) if hasattr(b, "in_specs") else [_ANY] * len(b.arrs) for b in bgs]
    b_out_specs = [b.out_specs(grid) if hasattr(b, "out_specs") else [_ANY] * len(b.out_shape) for b in bgs]
    aliases, i_off, o_off = {}, n_in, n_out
    for b in bgs:
        aliases.update({i_off + i: o_off + o for i, o in getattr(b, "aliases", {}).items()})
        i_off, o_off = i_off + len(b.arrs), o_off + len(b.out_shape)

    def full(*refs):
        pos = [0]

        def take(k):
            r = refs[pos[0]:pos[0] + k]
            pos[0] += k
            return r

        ins = take(n_in)
        b_ins = [take(len(b.arrs)) for b in bgs]
        outs = take(n_out)
        b_outs = [take(len(b.out_shape)) for b in bgs]
        sc = take(n_sc)
        b_sc = [take(len(b.scratch)) for b in bgs]
        if bgs:
            step = pl.program_id(0)
            for d in range(1, len(grid)):
                step = step * grid[d] + pl.program_id(d)

            @pl.when(step == 0)
            def _():
                for b, i_, o_, s_ in zip(bgs, b_ins, b_outs, b_sc):
                    b.start(i_, o_, s_)

        body(*ins, *outs, *sc)
        if bgs:
            for b, i_, o_, s_ in zip(bgs, b_ins, b_outs, b_sc):
                if hasattr(b, "step"):
                    b.step(i_, o_, s_)
                for at, fn in b.mid_steps(nsteps):
                    @pl.when(step == at)
                    def _():
                        fn(i_, o_, s_)

            @pl.when(step == nsteps - 1)
            def _():
                for b, i_, o_, s_ in zip(bgs, b_ins, b_outs, b_sc):
                    b.finish(i_, o_, s_)

    def run(*args):
        res = pl.pallas_call(
            full, name=name, grid=grid,
            in_specs=list(in_specs) + [s for l in b_in_specs for s in l],
            out_specs=out_specs_l + [s for l in b_out_specs for s in l],
            out_shape=out_shape_l + [s for b in bgs for s in b.out_shape],
            scratch_shapes=list(scratch_shapes) + [s for b in bgs for s in b.scratch],
            input_output_aliases=aliases,
            compiler_params=compiler_params,
        )(*args, *[a for b in bgs for a in b.arrs])
        rest = list(res[n_out:])
        for b in bgs:
            b.result, rest = rest[:len(b.out_shape)], rest[len(b.out_shape):]
        return res[0] if single else list(res[:n_out])

    return run


def s5_discretize(a_re, a_im, log_dt, b_re, b_im, c_re, c_im):
    lam_r = jnp.minimum(a_re, DT_MIN_LAMBDA)
    lam_i = a_im
    dt = jnp.exp(log_dt)[:, None]
    e = jnp.exp(lam_r * dt)
    lbr = e * jnp.cos(lam_i * dt)
    lbi = e * jnp.sin(lam_i * dt)
    den = lam_r * lam_r + lam_i * lam_i
    cf_r = ((lbr - 1.0) * lam_r + lbi * lam_i) / den
    cf_i = (lbi * lam_r - (lbr - 1.0) * lam_i) / den
    bb_r = cf_r[:, :, None] * b_re - cf_i[:, :, None] * b_im
    bb_i = cf_r[:, :, None] * b_im + cf_i[:, :, None] * b_re
    eye = jnp.eye(8, dtype=f32)

    def blk_b(m):
        return jnp.einsum('bgpc,gh->bgchp', m.reshape(8, 8, S5_P, S5_C), eye).reshape(8, 128, 512)

    def blk_c(m):
        return jnp.einsum('bgcp,gh->bgphc', m.reshape(8, 8, S5_C, S5_P), eye).reshape(8, 512, 128)

    bm = jnp.concatenate([blk_b(bb_r), blk_b(bb_i)], axis=-1)
    cm = jnp.concatenate([blk_c(c_re), -blk_c(c_im)], axis=1)
    lam = jnp.stack([lbr.reshape(8, 512), lbi.reshape(8, 512)], axis=1)
    lam = jnp.broadcast_to(lam[:, :, None, :], (8, 2, 8, 512))
    return lam, bm, cm


def _cmul(ar, ai, br, bi):
    return ar * br - ai * bi, ar * bi + ai * br


def _shift_rows(v, k, up):
    row = lax.broadcasted_iota(jnp.int32, v.shape, 0)
    if up:
        return jnp.where(row < 8 - k, pltpu.roll(v, 8 - k, 0), 0.0)
    return jnp.where(row >= k, pltpu.roll(v, k, 0), 0.0)


_ROWS = 256


def s5_core_fwd(hn, bm, lam, cm, bg=()):
    nt = T // _ROWS

    def body(u_ref, b_ref, lam_ref, c_ref, ys_ref, S):
        lr, li = lam_ref[0], lam_ref[1]
        z = jnp.zeros((8, 512), f32)
        tile = lambda k: pl.ds(k * _ROWS, _ROWS)
        c = (z, z)
        for k in range(nt):
            S[tile(k), :] = _dot(_rows_in(u_ref, k).astype(bf16), b_ref[...])
            if k >= 1:
                c = _scan_tile(S, lr, li, k - 1, c, False, False)
        c = _scan_tile(S, lr, li, nt - 1, c, False, False)
        c = _chunk_starts(c[0], c[1], lr, li, False)
        for k in range(nt):
            c = _scan_tile(S, lr, li, k, c, False, True)
            if k >= 1:
                _rows_out(ys_ref, k - 1, _dot(S[tile(k - 1), :].astype(bf16), c_ref[...]))
        _rows_out(ys_ref, nt - 1, _dot(S[tile(nt - 1), :].astype(bf16), c_ref[...]))

    return _call(
        bg, body, name="s5_core_fwd", grid=(S5_SUB,),
        in_specs=[pl.BlockSpec((T, 128), lambda b: (0, b)),
                  pl.BlockSpec((None, 128, 1024), lambda b: (b, 0, 0)),
                  pl.BlockSpec((None, 4, 8, 512), lambda b: (b, 0, 0, 0)),
                  pl.BlockSpec((None, 1024, 128), lambda b: (b, 0, 0))],
        out_specs=[pl.BlockSpec((T, 128), lambda b: (0, b)), pl.BlockSpec((T, 1024), lambda b: (0, b))],
        out_shape=[SDS((T, D), f32), SDS((T, S5_SUB * 1024), f32)],
        compiler_params=_cp(dimension_semantics=("arbitrary",)),
    )(hn, bm, lam, cm)


_SEG = _ROWS // S5_CH


def _rows_in(ref, k):
    return jnp.concatenate([ref[pl.ds(s, S5_CH, stride=S5_STEPS), :] for s in range(k * _SEG, (k + 1) * _SEG)], axis=0)


def _rows_out(ref, k, val):
    for j, s in enumerate(range(k * _SEG, (k + 1) * _SEG)):
        ref[pl.ds(s, S5_CH, stride=S5_STEPS), :] = val[j * S5_CH:(j + 1) * S5_CH, :]


def _scan_tile(S, lr, li, k, carry, reverse, store, aux=None):
    steps = range(k * _SEG, (k + 1) * _SEG)
    for s in (reversed(steps) if reverse else steps):
        row = pl.ds(s * 8, 8)
        xr, xi = carry[0], carry[1]
        nr = lr * xr - li * xi + S[row, 0:512]
        ni = lr * xi + li * xr + S[row, 512:1024]
        if store:
            S[row, 0:512] = nr
            S[row, 512:1024] = ni
        if aux is not None and s >= 1:
            prow = pl.ds((s - 1) * 8, 8)
            pr, pi_ = aux[prow, 0:512], aux[prow, 512:1024]
            carry = (nr, ni, carry[2] + nr * pr + ni * pi_, carry[3] + ni * pr - nr * pi_)
        elif aux is not None:
            carry = (nr, ni, carry[2], carry[3])
        else:
            carry = (nr, ni)
    return carry


def _chunk_starts(er, ei, lr, li, reverse):
    ar, ai = lr, li
    for _ in range(8):
        ar, ai = _cmul(ar, ai, ar, ai)
    cr, ci = _shift_rows(er, 1, reverse), _shift_rows(ei, 1, reverse)
    for k in (1, 2, 4):
        sr, si = _shift_rows(cr, k, reverse), _shift_rows(ci, k, reverse)
        pr, pi_ = _cmul(ar, ai, sr, si)
        cr, ci = cr + pr, ci + pi_
        ar, ai = _cmul(ar, ai, ar, ai)
    return cr, ci


def s5_core_bwd(hn, dy, xs, bm, lam, cm, bg=()):
    nt = T // _ROWS

    def body(u_ref, dy_ref, S1, b_ref, lam_ref, c_ref, du_ref, db_ref, dct_ref, dlam_ref, S2):
        lcr, lci = lam_ref[2], lam_ref[3]
        z = jnp.zeros((8, 512), f32)
        tile = lambda k: pl.ds(k * _ROWS, _ROWS)

        def dx(k):
            dyb = _rows_in(dy_ref, k).astype(bf16)
            S2[tile(k), :] = _dot_nt(dyb, c_ref[...])
            dct_ref[...] += _dot_tn(dyb, S1[tile(k), :].astype(bf16))

        dct_ref[...] = jnp.zeros_like(dct_ref)
        dx(nt - 1)
        c = (z, z)
        for k in range(nt - 1, -1, -1):
            if k >= 1:
                dx(k - 1)
            c = _scan_tile(S2, lcr, lci, k, c, True, False)

        def dbu(k):
            gb = S2[tile(k), :].astype(bf16)
            db_ref[...] += _dot_tn(_rows_in(u_ref, k).astype(bf16), gb)
            _rows_out(du_ref, k, _dot_nt(gb, b_ref[...]))

        c = _chunk_starts(c[0], c[1], lcr, lci, True) + (z, z)
        db_ref[...] = jnp.zeros_like(db_ref)
        for k in range(nt - 1, -1, -1):
            c = _scan_tile(S2, lcr, lci, k, c, True, True, aux=S1)
            if k + 1 < nt:
                dbu(k + 1)
        dbu(0)
        gr, gi, dr, di = c
        last = pl.ds((S5_STEPS - 1) * 8, 8)
        xr = _shift_rows(S1[last, 0:512], 1, False)
        xi = _shift_rows(S1[last, 512:1024], 1, False)
        dlam_ref[0] = dr + gr * xr + gi * xi
        dlam_ref[1] = di + gi * xr - gr * xi

    return _call(
        bg, body, name="s5_core_bwd", grid=(S5_SUB,),
        in_specs=[pl.BlockSpec((T, 128), lambda b: (0, b)),
                  pl.BlockSpec((T, 128), lambda b: (0, b)),
                  pl.BlockSpec((T, 1024), lambda b: (0, b)),
                  pl.BlockSpec((None, 128, 1024), lambda b: (b, 0, 0)),
                  pl.BlockSpec((None, 4, 8, 512), lambda b: (b, 0, 0, 0)),
                  pl.BlockSpec((None, 1024, 128), lambda b: (b, 0, 0))],
        out_specs=[pl.BlockSpec((T, 128), lambda b: (0, b)),
                   pl.BlockSpec((None, 128, 1024), lambda b: (b, 0, 0)),
                   pl.BlockSpec((None, 128, 1024), lambda b: (b, 0, 0)),
                   pl.BlockSpec((None, 2, 8, 512), lambda b: (b, 0, 0, 0))],
        out_shape=[SDS((T, D), f32), SDS((8, 128, 1024), f32), SDS((8, 128, 1024), f32), SDS((8, 2, 8, 512), f32)],
        scratch_shapes=[pltpu.VMEM((T, 1024), f32)],
        compiler_params=_cp(dimension_semantics=("arbitrary",)),
    )(hn, dy, xs, bm, lam, cm)


TM = 512
NT = T // TM


def _tile(n=D):
    return pl.BlockSpec((TM, n), lambda i: (i, 0))


def s5_pre(xp, g):
    def body(x_ref, g_ref, hn_ref):
        hn_ref[...] = _rms(x_ref[...], g_ref[...])[0]

    return pl.pallas_call(
        body, name="s5_pre", grid=(NT,), in_specs=[_tile(), _full((1, D))], out_specs=_tile(),
        out_shape=SDS((T, D), f32), compiler_params=_cp(dimension_semantics=("arbitrary",)),
    )(xp, g)


def _gelu_grad(y):
    c = math.sqrt(2.0 / math.pi)
    t = jnp.tanh(c * (y + 0.044715 * y * y * y))
    return 0.5 * (1.0 + t) + 0.5 * y * (1.0 - t * t) * c * (1.0 + 3.0 * 0.044715 * y * y)


def s5_post(ys, xp, g, d, wglu, bglu, bg=()):
    def body(ys_ref, x_ref, g_ref, d_ref, w_ref, b_ref, y_ref, z_ref, h_ref):
        x = x_ref[...]
        hn, _ = _rms(x, g_ref[...])
        y = ys_ref[...] + d_ref[...] * hn
        y_ref[...] = y
        yg = jax.nn.gelu(y).astype(bf16)
        for j in range(4):
            cv = slice(j * 256, (j + 1) * 256)
            cg = slice(1024 + j * 256, 1024 + (j + 1) * 256)
            val = _dot(yg, w_ref[j]) + b_ref[:, cv]
            gate = _dot(yg, w_ref[j + 4]) + b_ref[:, cg]
            z_ref[:, cv] = val
            z_ref[:, cg] = gate
            h_ref[:, cv] = x[:, cv] + val * jax.nn.sigmoid(gate)

    return _call(
        bg, body, name="s5_post", grid=(NT,),
        in_specs=[_tile(), _tile(), _full((1, D)), _full((1, D)), _full((8, D, 256)), _full((1, 2 * D))],
        out_specs=[_tile(), _tile(2 * D), _tile()],
        out_shape=[SDS((T, D), f32), SDS((T, 2 * D), f32), SDS((T, D), f32)],
        compiler_params=_cp(dimension_semantics=("arbitrary",)),
    )(ys, xp, g, d, wglu, bglu)


def s5_post_bwd(dh, y, z, wglu, bg=()):
    def body(dh_ref, y_ref, z_ref, w_ref, dy_ref, dw_ref, db_ref, acc):
        i = pl.program_id(0)

        @pl.when(i == 0)
        def _():
            acc[...] = jnp.zeros_like(acc)
            db_ref[...] = jnp.zeros_like(db_ref)

        dh_ = dh_ref[...]
        y = y_ref[...]
        yg = jax.nn.gelu(y).astype(bf16)
        dyg = jnp.zeros((TM, D), f32)
        for j in range(4):
            cv = slice(j * 256, (j + 1) * 256)
            cg = slice(1024 + j * 256, 1024 + (j + 1) * 256)
            val = z_ref[:, cv]
            sg = jax.nn.sigmoid(z_ref[:, cg])
            dval = dh_[:, cv] * sg
            dgate = dh_[:, cv] * val * sg * (1.0 - sg)
            db_ref[:, cv] += _colsum8(dval)
            db_ref[:, cg] += _colsum8(dgate)
            dvb = dval.astype(bf16)
            dgb = dgate.astype(bf16)
            acc[j] += _dot_tn(yg, dvb)
            acc[j + 4] += _dot_tn(yg, dgb)
            dyg = dyg + _dot_nt(dvb, w_ref[j]) + _dot_nt(dgb, w_ref[j + 4])
        dy_ref[...] = dyg * _gelu_grad(y)

        @pl.when(i == NT - 1)
        def _():
            dw_ref[...] = acc[...].astype(bf16)

    return _call(
        bg, body, name="s5_post_bwd", grid=(NT,),
        in_specs=[_tile(), _tile(), _tile(2 * D), _full((8, D, 256))],
        out_specs=[_tile(), _full((8, D, 256)), _full((8, 2 * D))],
        out_shape=[SDS((T, D), f32), SDS((8, D, 256), bf16), SDS((8, 2 * D), f32)],
        scratch_shapes=[pltpu.VMEM((8, D, 256), f32)],
        compiler_params=_cp(dimension_semantics=("arbitrary",)),
    )(dh, y, z, wglu)


def s5_pre_bwd(xp, g, du, dy, d, dh, bg=()):
    def body(x_ref, g_ref, du_ref, dy_ref, d_ref, dh_ref, dx_ref, dg_ref, dd_ref):
        i = pl.program_id(0)

        @pl.when(i == 0)
        def _():
            dg_ref[...] = jnp.zeros_like(dg_ref)
            dd_ref[...] = jnp.zeros_like(dd_ref)

        x = x_ref[...]
        g = g_ref[...]
        dy = dy_ref[...]
        hn, _ = _rms(x, g)
        dhn = du_ref[...] + d_ref[...] * dy
        dx, dgt = _rms_bwd(x, g, dhn)
        dx_ref[...] = dh_ref[...] + dx
        dg_ref[...] += _colsum8(dgt)
        dd_ref[...] += _colsum8(dy * hn)

    return _call(
        bg, body, name="s5_pre_bwd", grid=(NT,),
        in_specs=[_tile(), _full((1, D)), _tile(), _tile(), _full((1, D)), _tile()],
        out_specs=[_tile(), _full((8, D)), _full((8, D))],
        out_shape=[SDS((T, D), f32), SDS((8, D), f32), SDS((8, D), f32)],
        compiler_params=_cp(dimension_semantics=("arbitrary",)),
    )(xp, g, du, dy, d, dh)


TMF = 1024


def mlp_fwd(h, g, w_in, w_out, layer, bg=()):
    def body(h_ref, g_ref, wi_ref, wo_ref, hm_ref, r_ref, out_ref, acc):
        j = pl.program_id(1)

        @pl.when(j == 0)
        def _():
            hm, _ = _rms(h_ref[...], g_ref[...])
            hm_ref[...] = hm.astype(bf16)
            acc[...] = jnp.zeros_like(acc)

        a = jnp.maximum(_dot(hm_ref[...], wi_ref[...]), 0.0)
        r_ref[...] = a.astype(bf16)
        acc[...] += _dot((a * a).astype(bf16), wo_ref[...])

        @pl.when(j == NDEV - 1)
        def _():
            out_ref[...] = h_ref[...] + acc[...]

    return _call(
        bg, body, name=f"mlp_fwd{layer}", grid=(T // TMF, NDEV),
        in_specs=[pl.BlockSpec((TMF, D), lambda i, j: (i, 0)),
                  pl.BlockSpec((1, D), lambda i, j: (0, 0)),
                  pl.BlockSpec((None, D, D_FF_SHARD), lambda i, j: (j, 0, 0)),
                  pl.BlockSpec((None, D_FF_SHARD, D), lambda i, j: (j, 0, 0))],
        out_specs=[pl.BlockSpec((TMF, D), lambda i, j: (i, 0)), pl.BlockSpec((TMF, D_FF_SHARD), lambda i, j: (i, j)),
                   pl.BlockSpec((TMF, D), lambda i, j: (i, 0))],
        out_shape=[SDS((T, D), bf16), SDS((T, NDEV * D_FF_SHARD), bf16), SDS((T, D), f32)],
        scratch_shapes=[pltpu.VMEM((TMF, D), f32)],
        compiler_params=_cp(dimension_semantics=("arbitrary", "arbitrary")),
    )(h, g, w_in, w_out)


def mlp_bwd(h, hm, r, g, dout, dout_b, w_in, w_out, layer, bg=()):
    def body(h_ref, hm_ref, r_ref, g_ref, do_ref, dob_ref, wi_ref, wo_ref, dh_ref, dwi_ref, dwo_ref, dg_ref,
             dhm, dzs):
        s = pl.program_id(0)

        @pl.when(s == 0)
        def _():
            dhm[...] = jnp.zeros_like(dhm)

        @pl.when(s < NDEV)
        def _():
            for c in range(NT):
                rows = pl.ds(c * TM, TM)
                dz = (_dot_nt(dob_ref[rows, :], wo_ref[...]) * (2.0 * r_ref[rows, :].astype(f32))).astype(bf16)
                dzs[rows, :] = dz
                dhm[rows, :] += _dot_nt(dz, wi_ref[...])
            rb = r_ref[...]
            dwo_ref[...] = _dot_tn(rb * rb, dob_ref[...]).astype(bf16)
            dwi_ref[...] = _dot_tn(hm_ref[...], dzs[...]).astype(bf16)

        @pl.when(s >= NDEV)
        def _():
            @pl.when(s == NDEV)
            def _():
                dg_ref[...] = jnp.zeros_like(dg_ref)
            rows = pl.ds(pl.multiple_of((s - NDEV) * TM, TM), TM)
            dx, dgt = _rms_bwd(h_ref[...], g_ref[...], dhm[rows, :])
            dh_ref[...] = do_ref[...] + dx
            dg_ref[...] += _colsum8(dgt)

    shard = lambda s: (jnp.minimum(s, NDEV - 1), 0, 0)
    tile = lambda s: (jnp.maximum(s - NDEV, 0), 0)
    return _call(
        bg, body, name=f"mlp_bwd{layer}", grid=(NDEV + NT,),
        in_specs=[pl.BlockSpec((TM, D), tile),
                  _full((T, D)),
                  pl.BlockSpec((T, D_FF_SHARD), lambda s: (0, jnp.minimum(s, NDEV - 1))),
                  _full((1, D)),
                  pl.BlockSpec((TM, D), tile),
                  _full((T, D)),
                  pl.BlockSpec((None, D, D_FF_SHARD), shard),
                  pl.BlockSpec((None, D_FF_SHARD, D), shard)],
        out_specs=[pl.BlockSpec((TM, D), tile),
                   pl.BlockSpec((None, D, D_FF_SHARD), shard),
                   pl.BlockSpec((None, D_FF_SHARD, D), shard),
                   pl.BlockSpec((8, D), lambda s: (0, 0))],
        out_shape=[SDS((T, D), f32), SDS((NDEV, D, D_FF_SHARD), bf16), SDS((NDEV, D_FF_SHARD, D), bf16),
                   SDS((8, D), f32)],
        scratch_shapes=[pltpu.VMEM((T, D), f32), pltpu.VMEM((T, D_FF_SHARD), bf16)],
        compiler_params=_cp(dimension_semantics=("arbitrary",)),
    )(h, hm, r, g, dout, dout_b, w_in, w_out)


def _spread4():
    r = lax.broadcasted_iota(jnp.int32, (256, D), 0)
    c = lax.broadcasted_iota(jnp.int32, (256, D), 1)
    return ((c // 256 == r // HEAD_DIM) & (c % HEAD_DIM == r % HEAD_DIM)).astype(bf16)


def attn_pre(h, g_kv, g_mix, wkv, bkv, spread, wq, bq):
    def body(h_ref, gkv_ref, gm_ref, wkv_ref, bkv_ref, sp_ref, wq_ref, bq_ref, kvn_ref, hn_ref, k_ref, v_ref, q_ref):
        h_ = h_ref[...]
        kvn = _rms(h_, gkv_ref[...])[0].astype(bf16)
        hn = _rms(h_, gm_ref[...])[0].astype(bf16)
        kvn_ref[...] = kvn
        hn_ref[...] = hn
        kv = (_dot(kvn, wkv_ref[...]) + bkv_ref[...]).astype(bf16)
        k_ref[...] = _dot(kv[:, :256], sp_ref[...]).astype(bf16)
        v_ref[...] = _dot(kv[:, 256:], sp_ref[...]).astype(bf16)
        q_ref[...] = (_dot(hn, wq_ref[...]) + bq_ref[...]).astype(bf16)

    return pl.pallas_call(
        body, name="attn_pre", grid=(NT,),
        in_specs=[_tile(), _full((1, D)), _full((1, D)), _full((D, 512)), _full((1, 512)), _full((256, D)),
                  _full((D, D)), _full((1, D))],
        out_specs=[_tile()] * 5,
        out_shape=[SDS((T, D), bf16)] * 5,
        compiler_params=_cp(dimension_semantics=("arbitrary",)),
    )(h, g_kv, g_mix, wkv, bkv, spread, wq, bq)


def _attn_specs():
    cur = pl.BlockSpec((TM, 256), lambda j, n: (n, j))
    prev = pl.BlockSpec((BLK, 256), lambda j, n: (jnp.maximum(n * (TM // BLK) - 1, 0), j))
    return cur, prev


def _head_mask(g):
    lane = lax.broadcasted_iota(jnp.int32, (1, 256), 1)
    return (lane >= g * HEAD_DIM) & (lane < (g + 1) * HEAD_DIM)


def _stack_heads(t):
    return jnp.concatenate([jnp.where(_head_mask(g), t, 0) for g in range(Q_PER_KV)], axis=0)


def _unstack_heads(t):
    out = jnp.where(_head_mask(0), t[0:BLK], 0.0)
    for g in range(1, Q_PER_KV):
        out = out + jnp.where(_head_mask(g), t[g * BLK:(g + 1) * BLK], 0.0)
    return out


def _attn_probs(qs, k2, sinks, first):
    rows = Q_PER_KV * BLK
    s = _dot_nt(qs, k2) * (1.0 / math.sqrt(HEAD_DIM))
    qi = jnp.bitwise_and(lax.broadcasted_iota(jnp.int32, (rows, 2 * BLK), 0), BLK - 1)
    kj = lax.broadcasted_iota(jnp.int32, (rows, 2 * BLK), 1)
    diff = qi + BLK - kj
    valid = (diff >= 0) & (diff < BLK) & (jnp.logical_not(first) | (kj >= BLK))
    s = jnp.where(valid, s, -jnp.inf)
    rb = lax.broadcasted_iota(jnp.int32, (rows, 1), 0)
    sink = jnp.where(rb < BLK, sinks[0], jnp.where(rb < 2 * BLK, sinks[1], jnp.where(rb < 3 * BLK, sinks[2], sinks[3])))
    m = jnp.maximum(jnp.max(s, axis=-1, keepdims=True), sink)
    p = jnp.exp(s - m)
    ps = jnp.exp(sink - m)
    denom = jnp.sum(p, axis=-1, keepdims=True) + ps
    return p / denom, ps / denom


def _window_blocks(b, n, kc_ref, kp_ref, vc_ref, vp_ref):
    if b == 0:
        return (jnp.concatenate([kp_ref[...], kc_ref[0:BLK, :]], axis=0),
                jnp.concatenate([vp_ref[...], vc_ref[0:BLK, :]], axis=0), n == 0)
    rows = pl.ds((b - 1) * BLK, 2 * BLK)
    return kc_ref[rows, :], vc_ref[rows, :], False


def attn_core_fwd(q, k4, v4, sinks, bg=()):
    nb = TM // BLK

    def body(sink_ref, q_ref, kc_ref, kp_ref, vc_ref, vp_ref, o_ref, a_ref, as_ref):
        j = pl.program_id(0)
        n = pl.program_id(1)
        sk = [sink_ref[j * Q_PER_KV + g] for g in range(Q_PER_KV)]
        for b in range(nb):
            qb = q_ref[b * BLK:(b + 1) * BLK, :]
            k2, v2, first = _window_blocks(b, n, kc_ref, kp_ref, vc_ref, vp_ref)
            a, asink = _attn_probs(_stack_heads(qb), k2, sk, first)
            ab = a.astype(bf16)
            a_ref[b] = ab
            as_ref[b] = jnp.broadcast_to(asink, (Q_PER_KV * BLK, 128)).astype(bf16)
            o_ref[b * BLK:(b + 1) * BLK, :] = _unstack_heads(_dot(ab, v2)).astype(bf16)

    cur, prev = _attn_specs()
    rows = Q_PER_KV * BLK
    return _call(
        bg, body, name="attn_core_fwd", grid=(N_KV, NT),
        in_specs=[pl.BlockSpec(memory_space=pltpu.SMEM), cur, cur, prev, cur, prev],
        out_specs=[cur, pl.BlockSpec((None, nb, rows, 2 * BLK), lambda j, n: (j, n, 0, 0)),
                   pl.BlockSpec((None, nb, rows, 128), lambda j, n: (j, n, 0, 0))],
        out_shape=[SDS((T, D), bf16), SDS((N_KV, T // BLK, rows, 2 * BLK), bf16), SDS((N_KV, T // BLK, rows, 128), bf16)],
        compiler_params=_cp(dimension_semantics=("arbitrary", "arbitrary")),
    )(sinks, q, k4, k4, v4, v4)


def attn_post(h, o, wo, bo):
    def body(h_ref, o_ref, w_ref, b_ref, out_ref):
        out_ref[...] = h_ref[...] + _dot(o_ref[...], w_ref[...]) + b_ref[...]

    return pl.pallas_call(
        body, name="attn_post", grid=(NT,), in_specs=[_tile(), _tile(), _full((D, D)), _full((1, D))],
        out_specs=_tile(), out_shape=SDS((T, D), f32), compiler_params=_cp(dimension_semantics=("arbitrary",)),
    )(h, o, wo, bo)


def attn_bwd_pre(dh, o, wo, bg=()):
    def body(dh_ref, o_ref, w_ref, do_ref, dw_ref, db_ref, acc):
        i = pl.program_id(0)

        @pl.when(i == 0)
        def _():
            acc[...] = jnp.zeros_like(acc)
            db_ref[...] = jnp.zeros_like(db_ref)

        dh_ = dh_ref[...]
        dhb = dh_.astype(bf16)
        do_ref[...] = _dot_nt(dhb, w_ref[...]).astype(bf16)
        acc[...] += _dot_tn(o_ref[...], dhb)
        db_ref[...] += _colsum8(dh_)

        @pl.when(i == NT - 1)
        def _():
            dw_ref[...] = acc[...].astype(bf16)

    return _call(
        bg, body, name="attn_bwd_pre", grid=(NT,), in_specs=[_tile(), _tile(), _full((D, D))],
        out_specs=[_tile(), _full((D, D)), _full((8, D))],
        out_shape=[SDS((T, D), bf16), SDS((D, D), bf16), SDS((8, D), f32)],
        scratch_shapes=[pltpu.VMEM((D, D), f32)],
        compiler_params=_cp(dimension_semantics=("arbitrary",)),
    )(dh, o, wo)


def attn_core_bwd(q, do, k4, v4, probs, sink_w, bg=()):
    nb = TM // BLK

    def body(q_ref, do_ref, kc_ref, kp_ref, vc_ref, vp_ref, a_ref, as_ref, dq_ref, dk_ref, dv_ref, ds_ref):
        j = pl.program_id(0)
        n = pl.program_id(1)

        @pl.when(n == 0)
        def _():
            dk_ref[...] = jnp.zeros_like(dk_ref)
            dv_ref[...] = jnp.zeros_like(dv_ref)
            ds_ref[...] = jnp.zeros_like(ds_ref)

        lane8 = lax.broadcasted_iota(jnp.int32, (8, 128), 1)
        row8 = lax.broadcasted_iota(jnp.int32, (8, 128), 0)
        for b in range(nb):
            qs = _stack_heads(q_ref[b * BLK:(b + 1) * BLK, :])
            dos = _stack_heads(do_ref[b * BLK:(b + 1) * BLK, :])
            k2, v2, _ = _window_blocks(b, n, kc_ref, kp_ref, vc_ref, vp_ref)
            ab = a_ref[b]
            a = ab.astype(f32)
            asink = as_ref[b][:, 0:1].astype(f32)
            dp = _dot_nt(dos, v2)
            dd = jnp.sum(a * dp, axis=-1, keepdims=True)
            dsc = (a * (dp - dd) * (1.0 / math.sqrt(HEAD_DIM))).astype(bf16)
            t = asink * dd
            for g in range(Q_PER_KV):
                dsink = -jnp.sum(t[g * BLK:(g + 1) * BLK], axis=0, keepdims=True)
                ds_ref[...] += jnp.where((lane8 == g) & (row8 == 0), jnp.broadcast_to(dsink, (8, 128)), 0.0)
            dq_ref[b * BLK:(b + 1) * BLK, :] = _unstack_heads(_dot(dsc, k2))
            dk2 = _dot_tn(dsc, qs)
            dv2 = _dot_tn(ab, dos)
            cur = pl.ds(pl.multiple_of(n * TM + b * BLK, BLK), BLK)
            dk_ref[cur, :] += dk2[BLK:, :]
            dv_ref[cur, :] += dv2[BLK:, :]
            if b == 0:
                @pl.when(n > 0)
                def _():
                    prv = pl.ds(pl.multiple_of(n * TM - BLK, BLK), BLK)
                    dk_ref[prv, :] += dk2[:BLK, :]
                    dv_ref[prv, :] += dv2[:BLK, :]
            else:
                prv = pl.ds(pl.multiple_of(n * TM + (b - 1) * BLK, BLK), BLK)
                dk_ref[prv, :] += dk2[:BLK, :]
                dv_ref[prv, :] += dv2[:BLK, :]

    cur, prev = _attn_specs()
    col = pl.BlockSpec((T, 256), lambda j, n: (0, j))
    rows = Q_PER_KV * BLK
    return _call(
        bg, body, name="attn_core_bwd", grid=(N_KV, NT),
        in_specs=[cur, cur, cur, prev, cur, prev,
                  pl.BlockSpec((None, nb, rows, 2 * BLK), lambda j, n: (j, n, 0, 0)),
                  pl.BlockSpec((None, nb, rows, 128), lambda j, n: (j, n, 0, 0))],
        out_specs=[cur, col, col, pl.BlockSpec((None, 8, 128), lambda j, n: (j, 0, 0))],
        out_shape=[SDS((T, D), f32), SDS((T, D), f32), SDS((T, D), f32), SDS((N_KV, 8, 128), f32)],
        compiler_params=_cp(dimension_semantics=("arbitrary", "arbitrary")),
    )(q, do, k4, k4, v4, v4, probs, sink_w)


def attn_bwd_q(h, dh, dq, hn, g_mix, wq):
    def body(h_ref, dh_ref, dq_ref, hn_ref, gm_ref, wq_ref, out_ref, dwq_ref, dbq_ref, dgm_ref, aq):
        i = pl.program_id(0)

        @pl.when(i == 0)
        def _():
            aq[...] = jnp.zeros_like(aq)
            dbq_ref[...] = jnp.zeros_like(dbq_ref)
            dgm_ref[...] = jnp.zeros_like(dgm_ref)

        dq_ = dq_ref[...]
        dqb = dq_.astype(bf16)
        aq[...] += _dot_tn(hn_ref[...], dqb)
        dbq_ref[...] += _colsum8(dq_)
        dx, dg = _rms_bwd(h_ref[...], gm_ref[...], _dot_nt(dqb, wq_ref[...]))
        out_ref[...] = dh_ref[...] + dx
        dgm_ref[...] += _colsum8(dg)

        @pl.when(i == NT - 1)
        def _():
            dwq_ref[...] = aq[...].astype(bf16)

    vec = _full((8, D))
    mat = _full((D, D))
    return pl.pallas_call(
        body, name="attn_bwd_q", grid=(NT,),
        in_specs=[_tile()] * 4 + [_full((1, D)), mat],
        out_specs=[_tile(), mat, vec, vec],
        out_shape=[SDS((T, D), f32), SDS((D, D), bf16), SDS((8, D), f32), SDS((8, D), f32)],
        scratch_shapes=[pltpu.VMEM((D, D), f32)],
        compiler_params=_cp(dimension_semantics=("arbitrary",)),
    )(h, dh, dq, hn, g_mix, wq)


def attn_bwd_kv(h, dh, dk4, dv4, kvn, g_kv, wkv, spread):
    def body(h_ref, dh_ref, dk_ref, dv_ref, kvn_ref, gkv_ref, wkv_ref, sp_ref, out_ref, outb_ref, dw_ref, db_ref,
             dgkv_ref, acc):
        i = pl.program_id(0)

        @pl.when(i == 0)
        def _():
            for r in (acc, db_ref, dgkv_ref):
                r[...] = jnp.zeros_like(r)

        dkv = jnp.concatenate([_dot_nt(dk_ref[...].astype(bf16), sp_ref[...]),
                               _dot_nt(dv_ref[...].astype(bf16), sp_ref[...])], axis=1)
        dkvb = dkv.astype(bf16)
        acc[...] += _dot_tn(kvn_ref[...], dkvb)
        db_ref[...] += _colsum8(dkv)
        dx, dg = _rms_bwd(h_ref[...], gkv_ref[...], _dot_nt(dkvb, wkv_ref[...]))
        out = dh_ref[...] + dx
        out_ref[...] = out
        outb_ref[...] = out.astype(bf16)
        dgkv_ref[...] += _colsum8(dg)

        @pl.when(i == NT - 1)
        def _():
            dw_ref[...] = acc[...].astype(bf16)

    return pl.pallas_call(
        body, name="attn_bwd_kv", grid=(NT,),
        in_specs=[_tile()] * 5 + [_full((1, D)), _full((D, 512)), _full((256, D))],
        out_specs=[_tile(), _tile(), _full((D, 512)), _full((8, 512)), _full((8, D))],
        out_shape=[SDS((T, D), f32), SDS((T, D), bf16), SDS((D, 512), bf16), SDS((8, 512), f32), SDS((8, D), f32)],
        scratch_shapes=[pltpu.VMEM((D, 512), f32)],
        compiler_params=_cp(dimension_semantics=("arbitrary",)),
    )(h, dh, dk4, dv4, kvn, g_kv, wkv, spread)


def final_loss(h, g, target):
    def body(h_ref, g_ref, t_ref, loss_ref, dh_ref, dhb_ref, dg_ref):
        i = pl.program_id(0)

        @pl.when(i == 0)
        def _():
            loss_ref[...] = jnp.zeros_like(loss_ref)
            dg_ref[...] = jnp.zeros_like(dg_ref)

        h_ = h_ref[...]
        g_ = g_ref[...]
        y, _ = _rms(h_, g_)
        diff = y - t_ref[...]
        per_tok = jnp.mean(diff * diff, axis=-1, keepdims=True)
        tot = 0.5 * jnp.sum(per_tok, axis=0, keepdims=True)
        lane = lax.broadcasted_iota(jnp.int32, (8, 128), 1)
        row = lax.broadcasted_iota(jnp.int32, (8, 128), 0)
        loss_ref[...] += jnp.where((lane == 0) & (row == 0), jnp.broadcast_to(tot, (8, 128)), 0.0)
        dx, dgt = _rms_bwd(h_, g_, diff * (1.0 / D))
        dh_ref[...] = dx
        dhb_ref[...] = dx.astype(bf16)
        dg_ref[...] += _colsum8(dgt)

    return pl.pallas_call(
        body, name="final_loss", grid=(NT,), in_specs=[_tile(), _full((1, D)), _tile()],
        out_specs=[_full((8, 128)), _tile(), _tile(), _full((8, D))],
        out_shape=[SDS((8, 128), f32), SDS((T, D), f32), SDS((T, D), bf16), SDS((8, D), f32)],
        compiler_params=_cp(dimension_semantics=("arbitrary",)),
    )(h, g, target)


def fwd_bwd(x, target, p, shards, opt, core, chip):
    row = lambda v: v.reshape(1, -1)
    (lam, bm, cm), prep_vjp = jax.vjp(s5_discretize, p["s5_a_re"][0], p["s5_a_im"][0], p["s5_log_dt"][0],
                                      p["s5_b_re"][0], p["s5_b_im"][0], p["s5_c_re"][0], p["s5_c_im"][0])
    bmb, cmb = bm.astype(bf16), cm.astype(bf16)
    lam = jnp.concatenate([lam, lam * jnp.array([1.0, -1.0], f32).reshape(1, 2, 1, 1)], axis=1)
    g_mix0, g_mix1 = row(p["norm_mix"][0]), row(p["norm_mix"][1])
    g_mlp0, g_mlp1 = row(p["norm_mlp"][0]), row(p["norm_mlp"][1])
    g_kv, g_fin = row(p["norm_kv"]), row(p["norm_final"])
    bq, bo = p["b_q"], p["b_o"]
    bkv = row(p["b_kv"])
    spread = _spread4()
    sinks = p["sinks"].reshape(16)

    wglu, gvec = sc_gather([shards["s5_w_glu"], shards["vecs"]], 3, "sc_gather_s5")
    win0, wout0 = sc_gather([shards["w_in0"], shards["w_out0"]], 14, "sc_gather_mlp0")
    wkv, wq, wo = sc_gather([shards["w_kv"], shards["w_q"], shards["w_o"]], 4, "sc_gather_attn")
    win1, wout1 = sc_gather([shards["w_in1"], shards["w_out1"]], 5, "sc_gather_mlp1")
    xp = x
    hn0 = s5_pre(xp, g_mix0)
    ys, xs = s5_core_fwd(hn0, bmb, lam, cmb)
    d_skip = gvec[:, 0, :128].reshape(1, D)
    bglu = gvec[:, 0, 128:].reshape(1, 2 * D)
    y, z, h1 = s5_post(ys, xp, g_mix0, d_skip, wglu, bglu)
    hm0, r0, h2p = mlp_fwd(h1, g_mlp0, win0, wout0, 0)
    wkv, wq, wo = wkv.reshape(D, 512), wq.reshape(D, D), wo.reshape(D, D)
    h2 = h2p
    kvn, hn1, k4, v4, q = attn_pre(h2, g_kv, g_mix1, wkv, bkv, spread, wq, bq)
    o, probs, sink_w = attn_core_fwd(q, k4, v4, sinks)
    h3 = attn_post(h2, o, wo, bo)
    hm1, r1, h4 = mlp_fwd(h3, g_mlp1, win1, wout1, 1)
    loss, dh4, dh4b, dg_fin = final_loss(h4, g_fin, target)

    def pair_sums(names, grads, cid, before):
        r1 = sc_comm(BgPair(grads), cid, "sc_pair_" + names[0])
        parts = [add_pairs(g, r, core, f"add_pairs_{n}") for n, g, r in zip(names, grads, r1)]
        before, parts = lax.optimization_barrier((before, parts))
        return before, parts

    def across_chips(names, parts, cid):
        return list(zip(parts, sc_comm(BgChips(parts), cid, "sc_chips_" + names[0])))

    dh3, dwin1, dwout1, dg_mlp1 = mlp_bwd(h3, hm1, r1, g_mlp1, dh4, dh4b, win1, wout1, 1)
    do, dwo, dbo = attn_bwd_pre(dh3, o, wo)
    do, parts = pair_sums(["w_in1", "w_out1"], [dwin1, dwout1], 6, do)
    rs_in1, rs_out1 = across_chips(["w_in1", "w_out1"], parts, 7)
    dq, dk4, dv4, dsink = attn_core_bwd(q, do, k4, v4, probs, sink_w)
    dh2, dwq, dbq, dg_mix1 = attn_bwd_q(h2, dh3, dq, hn1, g_mix1, wq)
    dh2, dh2b, dwkv, dbkv, dg_kv = attn_bwd_kv(h2, dh2, dk4, dv4, kvn, g_kv, wkv, spread)
    dh2p, dh2pb = dh2, dh2b
    big = {}
    a_in1 = adam_big(*opt["w_mlp_in"], *rs_in1, chip, "adam_w_mlp_in1", layer=1)
    a_out1 = adam_big(*opt["w_mlp_out"], *rs_out1, chip, "adam_w_mlp_out1", layer=1)
    dh2p, a_in1, a_out1 = lax.optimization_barrier((dh2p, a_in1, a_out1))
    names = ["w_kv", "w_q", "w_o"]
    dh2p, parts = pair_sums(names, [dwkv.reshape(NDEV, 128, 512), dwq.reshape(NDEV, 128, D),
                                    dwo.reshape(NDEV, 128, D)], 8, dh2p)
    rs_attn = across_chips(names, parts, 9)
    dh1, dwin0, dwout0, dg_mlp0 = mlp_bwd(h1, hm0, r0, g_mlp0, dh2p, dh2pb, win0, wout0, 0)
    a_attn = [adam_big(*opt[n], *rs, chip, f"adam_{n}") for n, rs in zip(names, rs_attn)]
    dh1, a_attn = lax.optimization_barrier((dh1, a_attn))
    big.update(zip(names, a_attn))
    dy, dwglu, dbglu = s5_post_bwd(dh1, y, z, wglu)
    dy, parts = pair_sums(["w_in0", "w_out0"], [dwin0, dwout0], 10, dy)
    rs_in0, rs_out0 = across_chips(["w_in0", "w_out0"], parts, 11)
    du, dbm, dcmt, dlam = s5_core_bwd(hn0, dy, xs, bmb, lam, cmb)
    du, parts = pair_sums(["s5_w_glu"], [dwglu], 12, du)
    rs_glu, = across_chips(["s5_w_glu"], parts, 13)
    dxp, dg_mix0, dd = s5_pre_bwd(xp, g_mix0, du, dy, d_skip, dh1)
    big["w_mlp_in"] = adam_big(*opt["w_mlp_in"], *rs_in0, chip, "adam_w_mlp_in0", layer=0, prev=a_in1)
    big["w_mlp_out"] = adam_big(*opt["w_mlp_out"], *rs_out0, chip, "adam_w_mlp_out0", layer=0, prev=a_out1)
    big["s5_w_glu"] = adam_big(*opt["s5_w_glu"], *rs_glu, chip, "adam_s5_w_glu")
    grad_x = dxp
    da_re, da_im, dlog_dt, db_re, db_im, dc_re, dc_im = prep_vjp((dlam, dbm, dcmt.transpose(0, 2, 1)))

    def lanes(v_):
        v_ = v_.reshape(1, -1)
        return jnp.pad(v_, ((0, 0), (0, D - v_.shape[1])))

    small = jnp.concatenate([
        dg_mix0[0:1], dg_mix1[0:1], dg_mlp0[0:1], dg_mlp1[0:1], dg_kv[0:1], dg_fin[0:1], dd[0:1], dbq[0:1], dbo[0:1],
        dbglu[0:1].reshape(2, D), lanes(dbkv[0:1]),
        lanes(dsink[:, 0, :Q_PER_KV]), lanes(dlog_dt), lanes(loss[0:1, 0:1]), jnp.zeros((1, D), f32),
        da_re.reshape(4, D), da_im.reshape(4, D),
        db_re.transpose(0, 2, 1).reshape(64, D), db_im.transpose(0, 2, 1).reshape(64, D),
        dc_re.reshape(64, D), dc_im.reshape(64, D), jnp.zeros((8, D), f32)], axis=0)
    small, big["w_mlp_in"], big["w_mlp_out"] = lax.optimization_barrier((small, big["w_mlp_in"], big["w_mlp_out"]))
    return loss, grad_x, small, big


def _row_tile(r, c):
    return min(r, max(8, (512 * 1024) // c))


def add_pairs(g, r1, core, name):
    _, R, C = g.shape
    tr = _row_tile(R, C)

    def body(core_ref, g_ref, r_ref, o_ref):
        o_ref[...] = (g_ref[...].astype(f32) + r_ref[...].astype(f32)).astype(bf16)

    return pl.pallas_call(
        body, name=name, out_shape=SDS((4, R, C), bf16),
        grid_spec=pltpu.PrefetchScalarGridSpec(
            num_scalar_prefetch=1, grid=(4, R // tr),
            in_specs=[pl.BlockSpec((None, tr, C), lambda k, i, core: (2 * k + core[0], i, 0)),
                      pl.BlockSpec((None, tr, C), lambda k, i, core: (k, i, 0))],
            out_specs=pl.BlockSpec((None, tr, C), lambda k, i, core: (k, i, 0))),
        compiler_params=_cp(dimension_semantics=("arbitrary", "arbitrary")),
    )(core, g, r1)


def _adamw(w, g, m, v):
    m = ADAM_B1 * m + (1.0 - ADAM_B1) * g
    v = ADAM_B2 * v + (1.0 - ADAM_B2) * (g * g)
    m_hat = m / (1.0 - ADAM_B1 ** ADAM_STEP)
    v_hat = v / (1.0 - ADAM_B2 ** ADAM_STEP)
    delta = -ADAM_LR * (m_hat / (jnp.sqrt(v_hat) + ADAM_EPS) + ADAM_WD * w)
    return delta, m, v


def adam_big(w, m, v, part, r2, chip, name, layer=0, prev=None):
    L, R, C = w.shape
    tr = _row_tile(R, C)

    def body(chip_ref, w_ref, m_ref, v_ref, p_ref, r_ref, *rest):
        g_out, d_out, m_out, v_out = rest[-4:]
        g = p_ref[...].astype(f32) + r_ref[0].astype(f32) + r_ref[1].astype(f32) + r_ref[2].astype(f32)
        d, m_, v_ = _adamw(w_ref[...], g, m_ref[...], v_ref[...])
        g_out[...] = g
        d_out[...] = d
        m_out[...] = m_
        v_out[...] = v_

    blk = pl.BlockSpec((None, tr, C), lambda i, chip: (layer, i, 0))
    extra = [] if prev is None else list(prev)
    return pl.pallas_call(
        body, name=name, out_shape=[SDS((L, R, C), f32)] * 4,
        grid_spec=pltpu.PrefetchScalarGridSpec(
            num_scalar_prefetch=1, grid=(R // tr,),
            in_specs=[blk, blk, blk,
                      pl.BlockSpec((None, tr, C), lambda i, chip: (chip[0], i, 0)),
                      pl.BlockSpec((3, tr, C), lambda i, chip: (0, i, 0))] + [_ANY] * len(extra),
            out_specs=[blk] * 4),
        input_output_aliases={6 + k: k for k in range(len(extra))},
        compiler_params=_cp(dimension_semantics=("arbitrary",)),
    )(chip, w, m, v, part, r2, *extra)


def allreduce_small(buf):
    R = buf.shape[0]
    half, quarter = R // 2, R // 4
    assert R % 32 == 0

    def body(in_ref, out_ref, acc1, acc2, r0, r1, r2, send_sems, recv_sems):
        x, y, c = _pos()
        sibling, over_x, over_y = (x, y, 1 - c), (1 - x, y, c), (x, 1 - y, c)
        first = pl.multiple_of(c * half, 8)
        mine = pl.ds(first, half)
        theirs = pl.ds(pl.multiple_of((1 - c) * half, 8), half)
        qa = pl.ds(first, quarter)
        qb = pl.ds(pl.multiple_of(first + quarter, 8), quarter)

        def exchange(copies):
            cps = [pltpu.make_async_remote_copy(
                src_ref=src.at[rows], dst_ref=dst.at[rows], send_sem=send_sems.at[k], recv_sem=recv_sems.at[k],
                device_id=peer, device_id_type=MESH) for k, src, dst, rows, peer in copies]
            for cp in cps:
                cp.start()
            for cp in cps:
                cp.wait()

        exchange([(0, in_ref, r0, theirs, sibling)])
        acc1[mine, :] = in_ref[mine, :] + r0[mine, :]
        exchange([(1, acc1, r1, qa, over_x), (2, acc1, r1, qb, over_y)])
        acc2[mine, :] = acc1[mine, :] + r1[mine, :]
        exchange([(3, acc2, r2, qa, over_y), (4, acc2, r2, qb, over_x)])
        out_ref[mine, :] = acc2[mine, :] + r2[mine, :]
        exchange([(5, out_ref, out_ref, mine, sibling)])

    vm = pl.BlockSpec(memory_space=pltpu.VMEM)
    return pl.pallas_call(
        body, name="allreduce_small", in_specs=[vm], out_specs=vm, out_shape=SDS(buf.shape, f32),
        scratch_shapes=[pltpu.VMEM(buf.shape, f32)] * 5 + [pltpu.SemaphoreType.DMA((6,)), pltpu.SemaphoreType.DMA((6,))],
    )(buf)


SMALL_ROWS = {'norm_mix': (0, 2, D), 'norm_mlp': (2, 2, D), 'norm_kv': (4, 1, D), 'norm_final': (5, 1, D),
              's5_d': (6, 1, D), 'b_q': (7, 1, D), 'b_o': (8, 1, D), 's5_b_glu': (9, 2, D), 'b_kv': (11, 1, 512),
              'sinks': (12, 1, 16), 's5_log_dt': (13, 1, 64), 's5_a_re': (16, 4, D), 's5_a_im': (20, 4, D),
              's5_b_re': (24, 64, D), 's5_b_im': (88, 64, D), 's5_c_re': (152, 64, D), 's5_c_im': (216, 64, D)}
LOSS_ROW = 14
ROW_PARAMS = ['norm_mix', 'norm_mlp', 'norm_kv', 'norm_final', 'b_q', 'b_o', 'b_kv', 'sinks', 's5_log_dt']
SHARD_PARAMS = ['s5_d', 's5_b_glu']
S5_PARAMS = ['s5_a_re', 's5_a_im', 's5_b_re', 's5_b_im', 's5_c_re', 's5_c_im']


def adam_small(dev, gsum, s5_grads, w, m, v):
    names = ROW_PARAMS + SHARD_PARAMS + S5_PARAMS
    n_g = len(ROW_PARAMS) + len(SHARD_PARAMS)

    def body(dev_ref, gs_ref, *refs):
        pos = [0]

        def take(k):
            r = refs[pos[0]:pos[0] + k]
            pos[0] += k
            return r

        g5 = take(len(S5_PARAMS))
        wr, mr, vr = take(len(names)), take(len(names)), take(len(names))
        g_out = take(n_g)
        d_out, m_out, v_out = take(len(names)), take(len(names)), take(len(names))
        dv = dev_ref[0]
        for i, n in enumerate(names):
            if n in S5_PARAMS:
                g = g5[S5_PARAMS.index(n)][...]
            elif n in SHARD_PARAMS:
                r0, _, _ = SMALL_ROWS[n]
                ln = wr[i].shape[1]
                g = jnp.zeros((1, ln), f32)
                for k in range(NDEV):
                    off = k * ln
                    piece = gs_ref[r0 + off // D:r0 + off // D + 1, off % D:off % D + ln]
                    g = g + jnp.where(dv == k, piece, 0.0)
                g_out[i][...] = g
            else:
                r0, nr, nl = SMALL_ROWS[n]
                g = gs_ref[r0:r0 + nr, 0:nl]
                g_out[i][...] = g
            d, m_, v_ = _adamw(wr[i][...], g, mr[i][...], vr[i][...])
            d_out[i][...] = d
            m_out[i][...] = m_
            v_out[i][...] = v_

    vm = pl.BlockSpec(memory_space=pltpu.VMEM)
    ins = [s5_grads[n] for n in S5_PARAMS] + [d[n] for d in (w, m, v) for n in names]
    shapes = [SDS(w[n].shape, f32) for n in names]
    res = pl.pallas_call(
        body, name="adam_small", in_specs=[pl.BlockSpec(memory_space=pltpu.SMEM)] + [vm] * (1 + len(ins)),
        out_specs=[vm] * (n_g + 3 * len(names)), out_shape=shapes[:n_g] + shapes * 3,
        compiler_params=_cp(),
    )(dev, gsum, *ins)
    g_o = dict(zip(names[:n_g], res[:n_g]))
    rest = res[n_g:]
    k = len(names)
    return g_o, dict(zip(names, rest[:k])), dict(zip(names, rest[k:2 * k])), dict(zip(names, rest[2 * k:]))


WEIGHTS = ['norm_mix', 'norm_mlp', 'norm_kv', 'norm_final', 's5_a_re', 's5_a_im', 's5_log_dt', 's5_b_re', 's5_b_im',
           's5_c_re', 's5_c_im', 's5_d', 's5_w_glu', 's5_b_glu', 'w_kv', 'b_kv', 'w_q', 'b_q', 'sinks', 'w_o', 'b_o',
           'w_mlp_in', 'w_mlp_out']
BIG = ['s5_w_glu', 'w_kv', 'w_q', 'w_o', 'w_mlp_in', 'w_mlp_out']
BIG_2D = {'s5_w_glu': (D, 256), 'w_kv': (128, 512), 'w_q': (128, D), 'w_o': (128, D), 'w_mlp_in': (2 * D, 512),
          'w_mlp_out': (2 * 512, D)}
SMALL = [n for n in WEIGHTS if n not in BIG]


def kernel(x, norm_mix, norm_mlp, norm_kv, norm_final, s5_a_re, s5_a_im, s5_log_dt, s5_b_re, s5_b_im, s5_c_re, s5_c_im, s5_d, s5_w_glu, s5_b_glu, w_kv, b_kv, w_q, b_q, sinks, w_o, b_o, w_mlp_in, w_mlp_out, loss_target, m_norm_mix, m_norm_mlp, m_norm_kv, m_norm_final, m_s5_a_re, m_s5_a_im, m_s5_log_dt, m_s5_b_re, m_s5_b_im, m_s5_c_re, m_s5_c_im, m_s5_d, m_s5_w_glu, m_s5_b_glu, m_w_kv, m_b_kv, m_w_q, m_b_q, m_sinks, m_w_o, m_b_o, m_w_mlp_in, m_w_mlp_out, v_norm_mix, v_norm_mlp, v_norm_kv, v_norm_final, v_s5_a_re, v_s5_a_im, v_s5_log_dt, v_s5_b_re, v_s5_b_im, v_s5_c_re, v_s5_c_im, v_s5_d, v_s5_w_glu, v_s5_b_glu, v_w_kv, v_b_kv, v_w_q, v_b_q, v_sinks, v_w_o, v_b_o, v_w_mlp_in, v_w_mlp_out):
    w = dict(norm_mix=norm_mix, norm_mlp=norm_mlp, norm_kv=norm_kv, norm_final=norm_final, s5_a_re=s5_a_re,
             s5_a_im=s5_a_im, s5_log_dt=s5_log_dt, s5_b_re=s5_b_re, s5_b_im=s5_b_im, s5_c_re=s5_c_re, s5_c_im=s5_c_im,
             s5_d=s5_d, s5_w_glu=s5_w_glu, s5_b_glu=s5_b_glu, w_kv=w_kv, b_kv=b_kv, w_q=w_q, b_q=b_q, sinks=sinks,
             w_o=w_o, b_o=b_o, w_mlp_in=w_mlp_in, w_mlp_out=w_mlp_out)
    m = dict(norm_mix=m_norm_mix, norm_mlp=m_norm_mlp, norm_kv=m_norm_kv, norm_final=m_norm_final, s5_a_re=m_s5_a_re,
             s5_a_im=m_s5_a_im, s5_log_dt=m_s5_log_dt, s5_b_re=m_s5_b_re, s5_b_im=m_s5_b_im, s5_c_re=m_s5_c_re,
             s5_c_im=m_s5_c_im, s5_d=m_s5_d, s5_w_glu=m_s5_w_glu, s5_b_glu=m_s5_b_glu, w_kv=m_w_kv, b_kv=m_b_kv,
             w_q=m_w_q, b_q=m_b_q, sinks=m_sinks, w_o=m_w_o, b_o=m_b_o, w_mlp_in=m_w_mlp_in, w_mlp_out=m_w_mlp_out)
    v = dict(norm_mix=v_norm_mix, norm_mlp=v_norm_mlp, norm_kv=v_norm_kv, norm_final=v_norm_final, s5_a_re=v_s5_a_re,
             s5_a_im=v_s5_a_im, s5_log_dt=v_s5_log_dt, s5_b_re=v_s5_b_re, s5_b_im=v_s5_b_im, s5_c_re=v_s5_c_re,
             s5_c_im=v_s5_c_im, s5_d=v_s5_d, s5_w_glu=v_s5_w_glu, s5_b_glu=v_s5_b_glu, w_kv=v_w_kv, b_kv=v_b_kv,
             w_q=v_w_q, b_q=v_b_q, sinks=v_sinks, w_o=v_w_o, b_o=v_b_o, w_mlp_in=v_w_mlp_in, w_mlp_out=v_w_mlp_out)
    xi, yi, ci = _pos()
    dev = 4 * xi + 2 * yi + ci
    core = ci.reshape(1).astype(jnp.int32)
    chip = (2 * xi + yi).reshape(1).astype(jnp.int32)

    shards = {
        "s5_w_glu": s5_w_glu[0].astype(bf16), "w_kv": w_kv.astype(bf16), "w_q": w_q[0].astype(bf16),
        "w_o": w_o[0].astype(bf16), "w_in0": w_mlp_in[0].astype(bf16), "w_in1": w_mlp_in[1].astype(bf16),
        "w_out0": w_mlp_out[0].astype(bf16), "w_out1": w_mlp_out[1].astype(bf16),
        "vecs": jnp.broadcast_to(jnp.concatenate([s5_d, s5_b_glu], axis=1), (8, 384)),
    }
    as3d = lambda a, n: a if a.ndim == 3 and a.shape[0] == 2 else a.reshape((1,) + BIG_2D[n])
    opt = {n: (as3d(w[n], n), as3d(m[n], n), as3d(v[n], n)) for n in BIG}
    _, grad_x, grads, big = fwd_bwd(x[0], loss_target[0], {n: w[n] for n in SMALL}, shards, opt, core, chip)

    gsum = allreduce_small(grads)

    out_g, out_d, out_m, out_v = {}, {}, {}, {}
    for n in BIG:
        out_g[n], out_d[n], out_m[n], out_v[n] = [r.reshape(w[n].shape) for r in big[n]]

    loss = gsum[LOSS_ROW, 0]
    swapped = ("s5_b_re", "s5_b_im")
    swap = lambda a: a.transpose(0, 1, 3, 2)

    def kernel_side(d):
        d = {n: (d[n].reshape(1, -1) if d[n].ndim == 1 else d[n]) for n in SMALL}
        d.update({n: swap(d[n]) for n in swapped})
        return d

    s5_g = {}
    for n in S5_PARAMS:
        r0, nr, _ = SMALL_ROWS[n]
        s5_g[n] = gsum[r0:r0 + nr].reshape((1, 64, 16, 64) if n in swapped else w[n].shape)
        out_g[n] = s5_g[n]
    g_s, d_s, m_s, v_s = adam_small(dev.reshape(1).astype(jnp.int32), gsum, s5_g, kernel_side(w), kernel_side(m),
                                    kernel_side(v))
    for src, dst in ((g_s, out_g), (d_s, out_d), (m_s, out_m), (v_s, out_v)):
        dst.update(src)
    for dst in (out_g, out_d, out_m, out_v):
        for n in SMALL:
            dst[n] = (swap(dst[n]) if n in swapped else dst[n]).reshape(w[n].shape)

    return (loss, grad_x[None], *[out_g[n] for n in WEIGHTS], *[out_d[n] for n in WEIGHTS],
            *[out_m[n] for n in WEIGHTS], *[out_v[n] for n in WEIGHTS])
```

```python
import functools
import math

import jax
import jax.numpy as jnp
from jax import lax
from jax.experimental import pallas as pl
from jax.experimental.pallas import tpu as pltpu
from jax.experimental.pallas import tpu_sc as plsc

f32 = jnp.float32
bf16 = jnp.bfloat16
SDS = jax.ShapeDtypeStruct

T = 2048
D = 1024
NDEV = 8
NORM_EPS = 1e-5
S5_G, S5_C, S5_P = 64, 16, 64
S5_SUB = 8
S5_CH = 8
S5_STEPS = T // S5_CH
DT_MIN_LAMBDA = -1e-4
HEAD_DIM = 64
N_KV = 4
Q_PER_KV = 4
BLK = 128
D_FF_SHARD = 512
ADAM_LR, ADAM_B1, ADAM_B2, ADAM_EPS, ADAM_WD, ADAM_STEP = 0.001, 0.9, 0.999, 1e-08, 0.01, 10
VMEM_LIMIT = 56 * 1024 * 1024
MESH = pl.DeviceIdType.MESH


def _cp(**kw):
    return pltpu.CompilerParams(vmem_limit_bytes=VMEM_LIMIT, **kw)


def _dot(a, b):
    return jnp.dot(a, b, preferred_element_type=f32)


def _dot_nt(a, b):
    return lax.dot_general(a, b, (((1,), (1,)), ((), ())), preferred_element_type=f32)


def _dot_tn(a, b):
    return lax.dot_general(a, b, (((0,), (0,)), ((), ())), preferred_element_type=f32)


def _rms(x, g):
    r = lax.rsqrt(jnp.mean(x * x, axis=-1, keepdims=True) + NORM_EPS)
    return x * r * g, r


def _rms_bwd(x, g, dy):
    r = lax.rsqrt(jnp.mean(x * x, axis=-1, keepdims=True) + NORM_EPS)
    u = dy * g
    dx = r * u - (r * r * r) * x * jnp.mean(u * x, axis=-1, keepdims=True)
    return dx, dy * x * r


def _colsum8(v):
    s = jnp.sum(v, axis=0, keepdims=True)
    row = lax.broadcasted_iota(jnp.int32, (8, v.shape[1]), 0)
    return jnp.where(row == 0, jnp.broadcast_to(s, (8, v.shape[1])), 0.0)


def _full(shape):
    nd = len(shape)
    return pl.BlockSpec(shape, lambda *_: (0,) * nd, pipeline_mode=pl.Buffered(1))


_ANY = pl.BlockSpec(memory_space=pl.ANY)


def _pos():
    return lax.axis_index("x"), lax.axis_index("y"), lax.axis_index("c")


def _other_chips(x, y):
    return [(1 - x, y), (x, 1 - y), (1 - x, 1 - y)]


class BgGather:
    SIB, XN, YN, FWD_Y, FWD_X, SIB_X, SIB_Y, SIB_D = range(8)

    def __init__(self, arrs, mids=(0.5, 0.75)):
        n = len(arrs)
        self.arrs = list(arrs)
        self.out_shape = [SDS((NDEV,) + a.shape, a.dtype) for a in arrs]
        self.scratch = [pltpu.SemaphoreType.DMA((n, 8)), pltpu.SemaphoreType.DMA((n, 8)),
                        pltpu.SemaphoreType.DMA((n,))]
        self.mids = mids
        self.result = None

    @staticmethod
    def peers(x, y, c):
        return [(x, y, 1 - c), (1 - x, y, c), (x, 1 - y, c)]

    def mid_steps(self, nsteps):
        at = lambda f: min(nsteps - 1, max(0, int(f * nsteps) - 1))
        return [(at(self.mids[0]), self.mid), (max(at(self.mids[0]), at(self.mids[1])), self.mid2)]

    def _halves(self, a):
        rows = self.arrs[a].shape[0]
        cut = (rows // 32) * 16 if rows >= 32 else rows
        return (0, cut), (cut, rows - cut)

    def _copy(self, ins, outs, sems, a, k, block, to, own=False, part=None):
        slot = 4 * block[0] + 2 * block[1] + block[2]
        rows = pl.ds(0, self.arrs[a].shape[0]) if part is None else pl.ds(*self._halves(a)[part])
        dst = outs[a].at[slot, rows]
        return pltpu.make_async_remote_copy(
            src_ref=ins[a].at[rows] if own else dst, dst_ref=dst, send_sem=sems[0].at[a, k],
            recv_sem=sems[1].at[a, k], device_id=to, device_id_type=MESH)

    def _mine(self, ins, outs, sems):
        x, y, c = _pos()
        return [pltpu.make_async_copy(ins[a], outs[a].at[4 * x + 2 * y + c], sems[2].at[a])
                for a in range(len(self.arrs))]

    def _split(self, a):
        return self._halves(a)[1][1] > 0

    def _sends(self, ins, outs, sems, phase):
        x, y, c = _pos()
        me, sib, xn, yn, dg = (x, y, c), (x, y, 1 - c), (1 - x, y, c), (x, 1 - y, c), (1 - x, 1 - y, c)
        cps = []
        for a in range(len(self.arrs)):
            cp = lambda k, block, to, **kw: self._copy(ins, outs, sems, a, k, block, to, **kw)
            if phase == 0:
                cps += [cp(self.SIB, me, sib, own=True), cp(self.XN, me, xn, own=True), cp(self.YN, me, yn, own=True)]
            elif phase == 1:
                cps.append(cp(self.FWD_Y, xn, yn, part=0))
                if self._split(a):
                    cps.append(cp(self.FWD_X, yn, xn, part=1))
                cps += [cp(self.SIB_X, xn, sib), cp(self.SIB_Y, yn, sib)]
            else:
                cps.append(cp(self.SIB_D, dg, sib))
        return cps

    def _arrivals(self, ins, outs, sems, phase):
        x, y, c = _pos()
        me, xn, yn, dg = (x, y, c), (1 - x, y, c), (x, 1 - y, c), (1 - x, 1 - y, c)
        cps = []
        for a in range(len(self.arrs)):
            cp = lambda k, block, **kw: self._copy(ins, outs, sems, a, k, block, me, **kw)
            if phase == 1:
                cps += [cp(self.XN, xn), cp(self.YN, yn)]
            elif phase == 2:
                cps.append(cp(self.FWD_Y, dg, part=0))
                if self._split(a):
                    cps.append(cp(self.FWD_X, dg, part=1))
            else:
                cps += [cp(self.SIB, (x, y, 1 - c)), cp(self.SIB_X, (1 - x, y, 1 - c)),
                        cp(self.SIB_Y, (x, 1 - y, 1 - c)), cp(self.SIB_D, (1 - x, 1 - y, 1 - c))]
        return cps

    def start(self, ins, outs, sems):
        for cp in self._mine(ins, outs, sems) + self._sends(ins, outs, sems, 0):
            cp.start()

    def mid(self, ins, outs, sems):
        for cp in self._arrivals(ins, outs, sems, 1):
            cp.wait_recv()
        for cp in self._sends(ins, outs, sems, 1):
            cp.start()

    def mid2(self, ins, outs, sems):
        for cp in self._arrivals(ins, outs, sems, 2):
            cp.wait_recv()
        for cp in self._sends(ins, outs, sems, 2):
            cp.start()

    def finish(self, ins, outs, sems):
        for cp in self._arrivals(ins, outs, sems, 3):
            cp.wait_recv()
        for ph in range(3):
            for cp in self._sends(ins, outs, sems, ph):
                cp.wait_send()
        for cp in self._mine(ins, outs, sems):
            cp.wait()


def sc_comm(g, collective_id, name):
    srcs = [jax.new_ref(a, memory_space=pltpu.MemorySpace.HBM) for a in g.arrs]
    dsts = [jax.empty_ref(s, memory_space=pltpu.MemorySpace.HBM) for s in g.out_shape]

    @pl.kernel(mesh=plsc.ScalarSubcoreMesh(axis_name="sequencer", num_cores=1), name=name,
               scratch_types=tuple(g.scratch), compiler_params=pltpu.CompilerParams(collective_id=collective_id))
    def launch(*sems):
        peers = g.peers(*_pos())
        barrier = pltpu.get_barrier_semaphore()
        for peer in peers:
            pl.semaphore_signal(barrier, inc=1, device_id=peer, device_id_type=MESH)
        pl.semaphore_wait(barrier, len(peers))
        g.start(srcs, dsts, sems)
        for _, phase in g.mid_steps(1):
            phase(srcs, dsts, sems)
        g.finish(srcs, dsts, sems)

    launch()
    return [d[...] for d in dsts]


def sc_gather(arrs, collective_id, name):
    return sc_comm(BgGather(arrs), collective_id, name)


class BgPair:
    def __init__(self, arrs):
        n = len(arrs)
        self.arrs = list(arrs)
        self.out_shape = [SDS((4,) + a.shape[1:], a.dtype) for a in arrs]
        self.scratch = [pltpu.SemaphoreType.DMA((n, 4)), pltpu.SemaphoreType.DMA((n, 4))]
        self.result = None

    @staticmethod
    def peers(x, y, c):
        return [(x, y, 1 - c)]

    def mid_steps(self, nsteps):
        return []

    def _copies(self, ins, outs, sems):
        x, y, c = _pos()
        return [pltpu.make_async_remote_copy(
            src_ref=ins[a].at[2 * k + 1 - c], dst_ref=outs[a].at[k], send_sem=sems[0].at[a, k],
            recv_sem=sems[1].at[a, k], device_id=(x, y, 1 - c), device_id_type=MESH)
            for a in range(len(self.arrs)) for k in range(4)]

    def start(self, ins, outs, sems):
        for cp in self._copies(ins, outs, sems):
            cp.start()

    def finish(self, ins, outs, sems):
        cps = self._copies(ins, outs, sems)
        for cp in cps:
            cp.wait_recv()
        for cp in cps:
            cp.wait_send()


class BgChips(BgPair):
    def __init__(self, arrs):
        n = len(arrs)
        self.arrs = list(arrs)
        self.out_shape = [SDS((3,) + a.shape[1:], a.dtype) for a in arrs]
        self.scratch = [pltpu.SemaphoreType.DMA((n, 3)), pltpu.SemaphoreType.DMA((n, 3))]
        self.result = None

    @staticmethod
    def peers(x, y, c):
        return [(px, py, c) for px, py in _other_chips(x, y)]

    def _copies(self, ins, outs, sems):
        x, y, c = _pos()
        return [pltpu.make_async_remote_copy(
            src_ref=ins[a].at[2 * px + py], dst_ref=outs[a].at[r], send_sem=sems[0].at[a, r],
            recv_sem=sems[1].at[a, r], device_id=(px, py, c), device_id_type=MESH)
            for a in range(len(self.arrs)) for r, (px, py) in enumerate(_other_chips(x, y))]


def _call(bgs, body, *, name, grid, in_specs, out_specs, out_shape, scratch_shapes=(), compiler_params=None):
    single = not isinstance(out_shape, (list, tuple))
    out_specs_l = [out_specs] if single else list(out_specs)
    out_shape_l = [out_shape] if single else list(out_shape)
    bgs = [b for b in (bgs or []) if b is not None]
    n_in, n_out, n_sc = len(in_specs), len(out_shape_l), len(scratch_shapes)
    nsteps = math.prod(grid)
    b_in_specs = [b.in_specs(grid) if hasattr(b, "in_specs") else [_ANY] * len(b.arrs) for b in bgs]
    b_out_specs = [b.out_specs(grid) if hasattr(b, "out_specs") else [_ANY] * len(b.out_shape) for b in bgs]
    aliases, i_off, o_off = {}, n_in, n_out
    for b in bgs:
        aliases.update({i_off + i: o_off + o for i, o in getattr(b, "aliases", {}).items()})
        i_off, o_off = i_off + len(b.arrs), o_off + len(b.out_shape)

    def full(*refs):
        pos = [0]

        def take(k):
            r = refs[pos[0]:pos[0] + k]
            pos[0] += k
            return r

        ins = take(n_in)
        b_ins = [take(len(b.arrs)) for b in bgs]
        outs = take(n_out)
        b_outs = [take(len(b.out_shape)) for b in bgs]
        sc = take(n_sc)
        b_sc = [take(len(b.scratch)) for b in bgs]
        if bgs:
            step = pl.program_id(0)
            for d in range(1, len(grid)):
                step = step * grid[d] + pl.program_id(d)

            @pl.when(step == 0)
            def _():
                for b, i_, o_, s_ in zip(bgs, b_ins, b_outs, b_sc):
                    b.start(i_, o_, s_)

        body(*ins, *outs, *sc)
        if bgs:
            for b, i_, o_, s_ in zip(bgs, b_ins, b_outs, b_sc):
                if hasattr(b, "step"):
                    b.step(i_, o_, s_)
                for at, fn in b.mid_steps(nsteps):
                    @pl.when(step == at)
                    def _():
                        fn(i_, o_, s_)

            @pl.when(step == nsteps - 1)
            def _():
                for b, i_, o_, s_ in zip(bgs, b_ins, b_outs, b_sc):
                    b.finish(i_, o_, s_)

    def run(*args):
        res = pl.pallas_call(
            full, name=name, grid=grid,
            in_specs=list(in_specs) + [s for l in b_in_specs for s in l],
            out_specs=out_specs_l + [s for l in b_out_specs for s in l],
            out_shape=out_shape_l + [s for b in bgs for s in b.out_shape],
            scratch_shapes=list(scratch_shapes) + [s for b in bgs for s in b.scratch],
            input_output_aliases=aliases,
            compiler_params=compiler_params,
        )(*args, *[a for b in bgs for a in b.arrs])
        rest = list(res[n_out:])
        for b in bgs:
            b.result, rest = rest[:len(b.out_shape)], rest[len(b.out_shape):]
        return res[0] if single else list(res[:n_out])

    return run


def s5_discretize(a_re, a_im, log_dt, b_re, b_im, c_re, c_im):
    lam_r = jnp.minimum(a_re, DT_MIN_LAMBDA)
    lam_i = a_im
    dt = jnp.exp(log_dt)[:, None]
    e = jnp.exp(lam_r * dt)
    lbr = e * jnp.cos(lam_i * dt)
    lbi = e * jnp.sin(lam_i * dt)
    den = lam_r * lam_r + lam_i * lam_i
    cf_r = ((lbr - 1.0) * lam_r + lbi * lam_i) / den
    cf_i = (lbi * lam_r - (lbr - 1.0) * lam_i) / den
    bb_r = cf_r[:, :, None] * b_re - cf_i[:, :, None] * b_im
    bb_i = cf_r[:, :, None] * b_im + cf_i[:, :, None] * b_re
    eye = jnp.eye(8, dtype=f32)

    def blk_b(m):
        return jnp.einsum('bgpc,gh->bgchp', m.reshape(8, 8, S5_P, S5_C), eye).reshape(8, 128, 512)

    def blk_c(m):
        return jnp.einsum('bgcp,gh->bgphc', m.reshape(8, 8, S5_C, S5_P), eye).reshape(8, 512, 128)

    bm = jnp.concatenate([blk_b(bb_r), blk_b(bb_i)], axis=-1)
    cm = jnp.concatenate([blk_c(c_re), -blk_c(c_im)], axis=1)
    lam = jnp.stack([lbr.reshape(8, 512), lbi.reshape(8, 512)], axis=1)
    lam = jnp.broadcast_to(lam[:, :, None, :], (8, 2, 8, 512))
    return lam, bm, cm


def _cmul(ar, ai, br, bi):
    return ar * br - ai * bi, ar * bi + ai * br


def _shift_rows(v, k, up):
    row = lax.broadcasted_iota(jnp.int32, v.shape, 0)
    if up:
        return jnp.where(row < 8 - k, pltpu.roll(v, 8 - k, 0), 0.0)
    return jnp.where(row >= k, pltpu.roll(v, k, 0), 0.0)


_ROWS = 256


def s5_core_fwd(hn, bm, lam, cm, bg=()):
    nt = T // _ROWS

    def body(u_ref, b_ref, lam_ref, c_ref, ys_ref, S):
        lr, li = lam_ref[0], lam_ref[1]
        z = jnp.zeros((8, 512), f32)
        tile = lambda k: pl.ds(k * _ROWS, _ROWS)
        c = (z, z)
        for k in range(nt):
            S[tile(k), :] = _dot(_rows_in(u_ref, k).astype(bf16), b_ref[...])
            if k >= 1:
                c = _scan_tile(S, lr, li, k - 1, c, False, False)
        c = _scan_tile(S, lr, li, nt - 1, c, False, False)
        c = _chunk_starts(c[0], c[1], lr, li, False)
        for k in range(nt):
            c = _scan_tile(S, lr, li, k, c, False, True)
            if k >= 1:
                _rows_out(ys_ref, k - 1, _dot(S[tile(k - 1), :].astype(bf16), c_ref[...]))
        _rows_out(ys_ref, nt - 1, _dot(S[tile(nt - 1), :].astype(bf16), c_ref[...]))

    return _call(
        bg, body, name="s5_core_fwd", grid=(S5_SUB,),
        in_specs=[pl.BlockSpec((T, 128), lambda b: (0, b)),
                  pl.BlockSpec((None, 128, 1024), lambda b: (b, 0, 0)),
                  pl.BlockSpec((None, 4, 8, 512), lambda b: (b, 0, 0, 0)),
                  pl.BlockSpec((None, 1024, 128), lambda b: (b, 0, 0))],
        out_specs=[pl.BlockSpec((T, 128), lambda b: (0, b)), pl.BlockSpec((T, 1024), lambda b: (0, b))],
        out_shape=[SDS((T, D), f32), SDS((T, S5_SUB * 1024), f32)],
        compiler_params=_cp(dimension_semantics=("arbitrary",)),
    )(hn, bm, lam, cm)


_SEG = _ROWS // S5_CH


def _rows_in(ref, k):
    return jnp.concatenate([ref[pl.ds(s, S5_CH, stride=S5_STEPS), :] for s in range(k * _SEG, (k + 1) * _SEG)], axis=0)


def _rows_out(ref, k, val):
    for j, s in enumerate(range(k * _SEG, (k + 1) * _SEG)):
        ref[pl.ds(s, S5_CH, stride=S5_STEPS), :] = val[j * S5_CH:(j + 1) * S5_CH, :]


def _scan_tile(S, lr, li, k, carry, reverse, store, aux=None):
    steps = range(k * _SEG, (k + 1) * _SEG)
    for s in (reversed(steps) if reverse else steps):
        row = pl.ds(s * 8, 8)
        xr, xi = carry[0], carry[1]
        nr = lr * xr - li * xi + S[row, 0:512]
        ni = lr * xi + li * xr + S[row, 512:1024]
        if store:
            S[row, 0:512] = nr
            S[row, 512:1024] = ni
        if aux is not None and s >= 1:
            prow = pl.ds((s - 1) * 8, 8)
            pr, pi_ = aux[prow, 0:512], aux[prow, 512:1024]
            carry = (nr, ni, carry[2] + nr * pr + ni * pi_, carry[3] + ni * pr - nr * pi_)
        elif aux is not None:
            carry = (nr, ni, carry[2], carry[3])
        else:
            carry = (nr, ni)
    return carry


def _chunk_starts(er, ei, lr, li, reverse):
    ar, ai = lr, li
    for _ in range(8):
        ar, ai = _cmul(ar, ai, ar, ai)
    cr, ci = _shift_rows(er, 1, reverse), _shift_rows(ei, 1, reverse)
    for k in (1, 2, 4):
        sr, si = _shift_rows(cr, k, reverse), _shift_rows(ci, k, reverse)
        pr, pi_ = _cmul(ar, ai, sr, si)
        cr, ci = cr + pr, ci + pi_
        ar, ai = _cmul(ar, ai, ar, ai)
    return cr, ci


def s5_core_bwd(hn, dy, xs, bm, lam, cm, bg=()):
    nt = T // _ROWS

    def body(u_ref, dy_ref, S1, b_ref, lam_ref, c_ref, du_ref, db_ref, dct_ref, dlam_ref, S2):
        lcr, lci = lam_ref[2], lam_ref[3]
        z = jnp.zeros((8, 512), f32)
        tile = lambda k: pl.ds(k * _ROWS, _ROWS)

        def dx(k):
            dyb = _rows_in(dy_ref, k).astype(bf16)
            S2[tile(k), :] = _dot_nt(dyb, c_ref[...])
            dct_ref[...] += _dot_tn(dyb, S1[tile(k), :].astype(bf16))

        dct_ref[...] = jnp.zeros_like(dct_ref)
        dx(nt - 1)
        c = (z, z)
        for k in range(nt - 1, -1, -1):
            if k >= 1:
                dx(k - 1)
            c = _scan_tile(S2, lcr, lci, k, c, True, False)

        def dbu(k):
            gb = S2[tile(k), :].astype(bf16)
            db_ref[...] += _dot_tn(_rows_in(u_ref, k).astype(bf16), gb)
            _rows_out(du_ref, k, _dot_nt(gb, b_ref[...]))

        c = _chunk_starts(c[0], c[1], lcr, lci, True) + (z, z)
        db_ref[...] = jnp.zeros_like(db_ref)
        for k in range(nt - 1, -1, -1):
            c = _scan_tile(S2, lcr, lci, k, c, True, True, aux=S1)
            if k + 1 < nt:
                dbu(k + 1)
        dbu(0)
        gr, gi, dr, di = c
        last = pl.ds((S5_STEPS - 1) * 8, 8)
        xr = _shift_rows(S1[last, 0:512], 1, False)
        xi = _shift_rows(S1[last, 512:1024], 1, False)
        dlam_ref[0] = dr + gr * xr + gi * xi
        dlam_ref[1] = di + gi * xr - gr * xi

    return _call(
        bg, body, name="s5_core_bwd", grid=(S5_SUB,),
        in_specs=[pl.BlockSpec((T, 128), lambda b: (0, b)),
                  pl.BlockSpec((T, 128), lambda b: (0, b)),
                  pl.BlockSpec((T, 1024), lambda b: (0, b)),
                  pl.BlockSpec((None, 128, 1024), lambda b: (b, 0, 0)),
                  pl.BlockSpec((None, 4, 8, 512), lambda b: (b, 0, 0, 0)),
                  pl.BlockSpec((None, 1024, 128), lambda b: (b, 0, 0))],
        out_specs=[pl.BlockSpec((T, 128), lambda b: (0, b)),
                   pl.BlockSpec((None, 128, 1024), lambda b: (b, 0, 0)),
                   pl.BlockSpec((None, 128, 1024), lambda b: (b, 0, 0)),
                   pl.BlockSpec((None, 2, 8, 512), lambda b: (b, 0, 0, 0))],
        out_shape=[SDS((T, D), f32), SDS((8, 128, 1024), f32), SDS((8, 128, 1024), f32), SDS((8, 2, 8, 512), f32)],
        scratch_shapes=[pltpu.VMEM((T, 1024), f32)],
        compiler_params=_cp(dimension_semantics=("arbitrary",)),
    )(hn, dy, xs, bm, lam, cm)


TM = 512
NT = T // TM


def _tile(n=D):
    return pl.BlockSpec((TM, n), lambda i: (i, 0))


def s5_pre(xp, g):
    def body(x_ref, g_ref, hn_ref):
        hn_ref[...] = _rms(x_ref[...], g_ref[...])[0]

    return pl.pallas_call(
        body, name="s5_pre", grid=(NT,), in_specs=[_tile(), _full((1, D))], out_specs=_tile(),
        out_shape=SDS((T, D), f32), compiler_params=_cp(dimension_semantics=("arbitrary",)),
    )(xp, g)


def _gelu_grad(y):
    c = math.sqrt(2.0 / math.pi)
    t = jnp.tanh(c * (y + 0.044715 * y * y * y))
    return 0.5 * (1.0 + t) + 0.5 * y * (1.0 - t * t) * c * (1.0 + 3.0 * 0.044715 * y * y)


def s5_post(ys, xp, g, d, wglu, bglu, bg=()):
    def body(ys_ref, x_ref, g_ref, d_ref, w_ref, b_ref, y_ref, z_ref, h_ref):
        x = x_ref[...]
        hn, _ = _rms(x, g_ref[...])
        y = ys_ref[...] + d_ref[...] * hn
        y_ref[...] = y
        yg = jax.nn.gelu(y).astype(bf16)
        for j in range(4):
            cv = slice(j * 256, (j + 1) * 256)
            cg = slice(1024 + j * 256, 1024 + (j + 1) * 256)
            val = _dot(yg, w_ref[j]) + b_ref[:, cv]
            gate = _dot(yg, w_ref[j + 4]) + b_ref[:, cg]
            z_ref[:, cv] = val
            z_ref[:, cg] = gate
            h_ref[:, cv] = x[:, cv] + val * jax.nn.sigmoid(gate)

    return _call(
        bg, body, name="s5_post", grid=(NT,),
        in_specs=[_tile(), _tile(), _full((1, D)), _full((1, D)), _full((8, D, 256)), _full((1, 2 * D))],
        out_specs=[_tile(), _tile(2 * D), _tile()],
        out_shape=[SDS((T, D), f32), SDS((T, 2 * D), f32), SDS((T, D), f32)],
        compiler_params=_cp(dimension_semantics=("arbitrary",)),
    )(ys, xp, g, d, wglu, bglu)


def s5_post_bwd(dh, y, z, wglu, bg=()):
    def body(dh_ref, y_ref, z_ref, w_ref, dy_ref, dw_ref, db_ref, ygs, dzs):
        i = pl.program_id(0)
        rows = pl.ds(pl.multiple_of(i * TM, TM), TM)

        @pl.when(i == 0)
        def _():
            db_ref[...] = jnp.zeros_like(db_ref)

        dh_ = dh_ref[...]
        y = y_ref[...]
        ygs[rows, :] = jax.nn.gelu(y).astype(bf16)
        dyg = jnp.zeros((TM, D), f32)
        for j in range(4):
            cv = slice(j * 256, (j + 1) * 256)
            cg = slice(1024 + j * 256, 1024 + (j + 1) * 256)
            val = z_ref[:, cv]
            sg = jax.nn.sigmoid(z_ref[:, cg])
            dval = dh_[:, cv] * sg
            dgate = dh_[:, cv] * val * sg * (1.0 - sg)
            db_ref[:, cv] += _colsum8(dval)
            db_ref[:, cg] += _colsum8(dgate)
            dvb = dval.astype(bf16)
            dgb = dgate.astype(bf16)
            dzs[rows, cv] = dvb
            dzs[rows, cg] = dgb
            dyg = dyg + _dot_nt(dvb, w_ref[j]) + _dot_nt(dgb, w_ref[j + 4])
        dy_ref[...] = dyg * _gelu_grad(y)

        @pl.when(i == NT - 1)
        def _():
            for half in range(2):
                dw = _dot_tn(ygs[...], dzs[:, half * D:(half + 1) * D])
                for j in range(4):
                    dw_ref[4 * half + j] = dw[:, j * 256:(j + 1) * 256].astype(bf16)

    return _call(
        bg, body, name="s5_post_bwd", grid=(NT,),
        in_specs=[_tile(), _tile(), _tile(2 * D), _full((8, D, 256))],
        out_specs=[_tile(), _full((8, D, 256)), _full((8, 2 * D))],
        out_shape=[SDS((T, D), f32), SDS((8, D, 256), bf16), SDS((8, 2 * D), f32)],
        scratch_shapes=[pltpu.VMEM((T, D), bf16), pltpu.VMEM((T, 2 * D), bf16)],
        compiler_params=_cp(dimension_semantics=("arbitrary",)),
    )(dh, y, z, wglu)


def s5_pre_bwd(xp, g, du, dy, d, dh, bg=()):
    def body(x_ref, g_ref, du_ref, dy_ref, d_ref, dh_ref, dx_ref, dg_ref, dd_ref):
        i = pl.program_id(0)

        @pl.when(i == 0)
        def _():
            dg_ref[...] = jnp.zeros_like(dg_ref)
            dd_ref[...] = jnp.zeros_like(dd_ref)

        x = x_ref[...]
        g = g_ref[...]
        dy = dy_ref[...]
        hn, _ = _rms(x, g)
        dhn = du_ref[...] + d_ref[...] * dy
        dx, dgt = _rms_bwd(x, g, dhn)
        dx_ref[...] = dh_ref[...] + dx
        dg_ref[...] += _colsum8(dgt)
        dd_ref[...] += _colsum8(dy * hn)

    return _call(
        bg, body, name="s5_pre_bwd", grid=(NT,),
        in_specs=[_tile(), _full((1, D)), _tile(), _tile(), _full((1, D)), _tile()],
        out_specs=[_tile(), _full((8, D)), _full((8, D))],
        out_shape=[SDS((T, D), f32), SDS((8, D), f32), SDS((8, D), f32)],
        compiler_params=_cp(dimension_semantics=("arbitrary",)),
    )(xp, g, du, dy, d, dh)


TMF = 1024


def mlp_fwd(h, g, w_in, w_out, layer, bg=()):
    def body(h_ref, g_ref, wi_ref, wo_ref, hm_ref, r_ref, out_ref, acc):
        j = pl.program_id(1)

        @pl.when(j == 0)
        def _():
            hm, _ = _rms(h_ref[...], g_ref[...])
            hm_ref[...] = hm.astype(bf16)
            acc[...] = jnp.zeros_like(acc)

        a = jnp.maximum(_dot(hm_ref[...], wi_ref[...]), 0.0)
        r_ref[...] = a.astype(bf16)
        acc[...] += _dot((a * a).astype(bf16), wo_ref[...])

        @pl.when(j == NDEV - 1)
        def _():
            out_ref[...] = h_ref[...] + acc[...]

    return _call(
        bg, body, name=f"mlp_fwd{layer}", grid=(T // TMF, NDEV),
        in_specs=[pl.BlockSpec((TMF, D), lambda i, j: (i, 0)),
                  pl.BlockSpec((1, D), lambda i, j: (0, 0)),
                  pl.BlockSpec((None, D, D_FF_SHARD), lambda i, j: (j, 0, 0)),
                  pl.BlockSpec((None, D_FF_SHARD, D), lambda i, j: (j, 0, 0))],
        out_specs=[pl.BlockSpec((TMF, D), lambda i, j: (i, 0)), pl.BlockSpec((TMF, D_FF_SHARD), lambda i, j: (i, j)),
                   pl.BlockSpec((TMF, D), lambda i, j: (i, 0))],
        out_shape=[SDS((T, D), bf16), SDS((T, NDEV * D_FF_SHARD), bf16), SDS((T, D), f32)],
        scratch_shapes=[pltpu.VMEM((TMF, D), f32)],
        compiler_params=_cp(dimension_semantics=("arbitrary", "arbitrary")),
    )(h, g, w_in, w_out)


def mlp_bwd(h, hm, r, g, dout, dout_b, w_in, w_out, layer, bg=()):
    def body(h_ref, hm_ref, r_ref, g_ref, do_ref, dob_ref, wi_ref, wo_ref, dh_ref, dwi_ref, dwo_ref, dg_ref,
             dhm, dzs):
        s = pl.program_id(0)

        @pl.when(s == 0)
        def _():
            dhm[...] = jnp.zeros_like(dhm)

        @pl.when(s < NDEV)
        def _():
            for c in range(NT):
                rows = pl.ds(c * TM, TM)
                dz = (_dot_nt(dob_ref[rows, :], wo_ref[...]) * (2.0 * r_ref[rows, :].astype(f32))).astype(bf16)
                dzs[rows, :] = dz
                dhm[rows, :] += _dot_nt(dz, wi_ref[...])
            rb = r_ref[...]
            dwo_ref[...] = _dot_tn(rb * rb, dob_ref[...]).astype(bf16)
            dwi_ref[...] = _dot_tn(hm_ref[...], dzs[...]).astype(bf16)

        @pl.when(s >= NDEV)
        def _():
            @pl.when(s == NDEV)
            def _():
                dg_ref[...] = jnp.zeros_like(dg_ref)
            rows = pl.ds(pl.multiple_of((s - NDEV) * TM, TM), TM)
            dx, dgt = _rms_bwd(h_ref[...], g_ref[...], dhm[rows, :])
            dh_ref[...] = do_ref[...] + dx
            dg_ref[...] += _colsum8(dgt)

    shard = lambda s: (jnp.minimum(s, NDEV - 1), 0, 0)
    tile = lambda s: (jnp.maximum(s - NDEV, 0), 0)
    return _call(
        bg, body, name=f"mlp_bwd{layer}", grid=(NDEV + NT,),
        in_specs=[pl.BlockSpec((TM, D), tile),
                  _full((T, D)),
                  pl.BlockSpec((T, D_FF_SHARD), lambda s: (0, jnp.minimum(s, NDEV - 1))),
                  _full((1, D)),
                  pl.BlockSpec((TM, D), tile),
                  _full((T, D)),
                  pl.BlockSpec((None, D, D_FF_SHARD), shard),
                  pl.BlockSpec((None, D_FF_SHARD, D), shard)],
        out_specs=[pl.BlockSpec((TM, D), tile),
                   pl.BlockSpec((None, D, D_FF_SHARD), shard),
                   pl.BlockSpec((None, D_FF_SHARD, D), shard),
                   pl.BlockSpec((8, D), lambda s: (0, 0))],
        out_shape=[SDS((T, D), f32), SDS((NDEV, D, D_FF_SHARD), bf16), SDS((NDEV, D_FF_SHARD, D), bf16),
                   SDS((8, D), f32)],
        scratch_shapes=[pltpu.VMEM((T, D), f32), pltpu.VMEM((T, D_FF_SHARD), bf16)],
        compiler_params=_cp(dimension_semantics=("arbitrary",)),
    )(h, hm, r, g, dout, dout_b, w_in, w_out)


def _spread4():
    r = lax.broadcasted_iota(jnp.int32, (256, D), 0)
    c = lax.broadcasted_iota(jnp.int32, (256, D), 1)
    return ((c // 256 == r // HEAD_DIM) & (c % HEAD_DIM == r % HEAD_DIM)).astype(bf16)


def attn_pre(h, g_kv, g_mix, wkv, bkv, spread, wq, bq):
    def body(h_ref, gkv_ref, gm_ref, wkv_ref, bkv_ref, sp_ref, wq_ref, bq_ref, kvn_ref, hn_ref, k_ref, v_ref, q_ref):
        h_ = h_ref[...]
        kvn = _rms(h_, gkv_ref[...])[0].astype(bf16)
        hn = _rms(h_, gm_ref[...])[0].astype(bf16)
        kvn_ref[...] = kvn
        hn_ref[...] = hn
        kv = (_dot(kvn, wkv_ref[...]) + bkv_ref[...]).astype(bf16)
        k_ref[...] = _dot(kv[:, :256], sp_ref[...]).astype(bf16)
        v_ref[...] = _dot(kv[:, 256:], sp_ref[...]).astype(bf16)
        q_ref[...] = (_dot(hn, wq_ref[...]) + bq_ref[...]).astype(bf16)

    return pl.pallas_call(
        body, name="attn_pre", grid=(NT,),
        in_specs=[_tile(), _full((1, D)), _full((1, D)), _full((D, 512)), _full((1, 512)), _full((256, D)),
                  _full((D, D)), _full((1, D))],
        out_specs=[_tile()] * 5,
        out_shape=[SDS((T, D), bf16)] * 5,
        compiler_params=_cp(dimension_semantics=("arbitrary",)),
    )(h, g_kv, g_mix, wkv, bkv, spread, wq, bq)


def _attn_specs():
    cur = pl.BlockSpec((TM, 256), lambda j, n: (n, j))
    prev = pl.BlockSpec((BLK, 256), lambda j, n: (jnp.maximum(n * (TM // BLK) - 1, 0), j))
    return cur, prev


def _head_mask(g):
    lane = lax.broadcasted_iota(jnp.int32, (1, 256), 1)
    return (lane >= g * HEAD_DIM) & (lane < (g + 1) * HEAD_DIM)


def _stack_heads(t):
    return jnp.concatenate([jnp.where(_head_mask(g), t, 0) for g in range(Q_PER_KV)], axis=0)


def _unstack_heads(t):
    out = jnp.where(_head_mask(0), t[0:BLK], 0.0)
    for g in range(1, Q_PER_KV):
        out = out + jnp.where(_head_mask(g), t[g * BLK:(g + 1) * BLK], 0.0)
    return out


def _attn_probs(qs, k2, sinks, first):
    rows = Q_PER_KV * BLK
    s = _dot_nt(qs, k2) * (1.0 / math.sqrt(HEAD_DIM))
    qi = jnp.bitwise_and(lax.broadcasted_iota(jnp.int32, (rows, 2 * BLK), 0), BLK - 1)
    kj = lax.broadcasted_iota(jnp.int32, (rows, 2 * BLK), 1)
    diff = qi + BLK - kj
    valid = (diff >= 0) & (diff < BLK) & (jnp.logical_not(first) | (kj >= BLK))
    s = jnp.where(valid, s, -jnp.inf)
    rb = lax.broadcasted_iota(jnp.int32, (rows, 1), 0)
    sink = jnp.where(rb < BLK, sinks[0], jnp.where(rb < 2 * BLK, sinks[1], jnp.where(rb < 3 * BLK, sinks[2], sinks[3])))
    m = jnp.maximum(jnp.max(s, axis=-1, keepdims=True), sink)
    p = jnp.exp(s - m)
    ps = jnp.exp(sink - m)
    denom = jnp.sum(p, axis=-1, keepdims=True) + ps
    return p / denom, ps / denom


def _window_blocks(b, n, kc_ref, kp_ref, vc_ref, vp_ref):
    if b == 0:
        return (jnp.concatenate([kp_ref[...], kc_ref[0:BLK, :]], axis=0),
                jnp.concatenate([vp_ref[...], vc_ref[0:BLK, :]], axis=0), n == 0)
    rows = pl.ds((b - 1) * BLK, 2 * BLK)
    return kc_ref[rows, :], vc_ref[rows, :], False


def attn_core_fwd(q, k4, v4, sinks, bg=()):
    nb = TM // BLK

    def body(sink_ref, q_ref, kc_ref, kp_ref, vc_ref, vp_ref, o_ref, a_ref, as_ref):
        j = pl.program_id(0)
        n = pl.program_id(1)
        sk = [sink_ref[j * Q_PER_KV + g] for g in range(Q_PER_KV)]
        for b in range(nb):
            qb = q_ref[b * BLK:(b + 1) * BLK, :]
            k2, v2, first = _window_blocks(b, n, kc_ref, kp_ref, vc_ref, vp_ref)
            a, asink = _attn_probs(_stack_heads(qb), k2, sk, first)
            ab = a.astype(bf16)
            a_ref[b] = ab
            as_ref[b] = jnp.broadcast_to(asink, (Q_PER_KV * BLK, 128)).astype(bf16)
            o_ref[b * BLK:(b + 1) * BLK, :] = _unstack_heads(_dot(ab, v2)).astype(bf16)

    cur, prev = _attn_specs()
    rows = Q_PER_KV * BLK
    return _call(
        bg, body, name="attn_core_fwd", grid=(N_KV, NT),
        in_specs=[pl.BlockSpec(memory_space=pltpu.SMEM), cur, cur, prev, cur, prev],
        out_specs=[cur, pl.BlockSpec((None, nb, rows, 2 * BLK), lambda j, n: (j, n, 0, 0)),
                   pl.BlockSpec((None, nb, rows, 128), lambda j, n: (j, n, 0, 0))],
        out_shape=[SDS((T, D), bf16), SDS((N_KV, T // BLK, rows, 2 * BLK), bf16), SDS((N_KV, T // BLK, rows, 128), bf16)],
        compiler_params=_cp(dimension_semantics=("arbitrary", "arbitrary")),
    )(sinks, q, k4, k4, v4, v4)


def attn_post(h, o, wo, bo):
    def body(h_ref, o_ref, w_ref, b_ref, out_ref):
        out_ref[...] = h_ref[...] + _dot(o_ref[...], w_ref[...]) + b_ref[...]

    return pl.pallas_call(
        body, name="attn_post", grid=(NT,), in_specs=[_tile(), _tile(), _full((D, D)), _full((1, D))],
        out_specs=_tile(), out_shape=SDS((T, D), f32), compiler_params=_cp(dimension_semantics=("arbitrary",)),
    )(h, o, wo, bo)


def attn_bwd_pre(dh, o, wo, bg=()):
    def body(dh_ref, o_ref, w_ref, do_ref, dw_ref, db_ref, acc):
        i = pl.program_id(0)

        @pl.when(i == 0)
        def _():
            acc[...] = jnp.zeros_like(acc)
            db_ref[...] = jnp.zeros_like(db_ref)

        dh_ = dh_ref[...]
        dhb = dh_.astype(bf16)
        do_ref[...] = _dot_nt(dhb, w_ref[...]).astype(bf16)
        acc[...] += _dot_tn(o_ref[...], dhb)
        db_ref[...] += _colsum8(dh_)

        @pl.when(i == NT - 1)
        def _():
            dw_ref[...] = acc[...].astype(bf16)

    return _call(
        bg, body, name="attn_bwd_pre", grid=(NT,), in_specs=[_tile(), _tile(), _full((D, D))],
        out_specs=[_tile(), _full((D, D)), _full((8, D))],
        out_shape=[SDS((T, D), bf16), SDS((D, D), bf16), SDS((8, D), f32)],
        scratch_shapes=[pltpu.VMEM((D, D), f32)],
        compiler_params=_cp(dimension_semantics=("arbitrary",)),
    )(dh, o, wo)


def attn_core_bwd(q, do, k4, v4, probs, sink_w, bg=()):
    nb = TM // BLK

    def body(q_ref, do_ref, kc_ref, kp_ref, vc_ref, vp_ref, a_ref, as_ref, dq_ref, dk_ref, dv_ref, ds_ref):
        j = pl.program_id(0)
        n = pl.program_id(1)

        @pl.when(n == 0)
        def _():
            dk_ref[...] = jnp.zeros_like(dk_ref)
            dv_ref[...] = jnp.zeros_like(dv_ref)
            ds_ref[...] = jnp.zeros_like(ds_ref)

        lane8 = lax.broadcasted_iota(jnp.int32, (8, 128), 1)
        row8 = lax.broadcasted_iota(jnp.int32, (8, 128), 0)
        for b in range(nb):
            qs = _stack_heads(q_ref[b * BLK:(b + 1) * BLK, :])
            dos = _stack_heads(do_ref[b * BLK:(b + 1) * BLK, :])
            k2, v2, _ = _window_blocks(b, n, kc_ref, kp_ref, vc_ref, vp_ref)
            ab = a_ref[b]
            a = ab.astype(f32)
            asink = as_ref[b][:, 0:1].astype(f32)
            dp = _dot_nt(dos, v2)
            dd = jnp.sum(a * dp, axis=-1, keepdims=True)
            dsc = (a * (dp - dd) * (1.0 / math.sqrt(HEAD_DIM))).astype(bf16)
            t = asink * dd
            for g in range(Q_PER_KV):
                dsink = -jnp.sum(t[g * BLK:(g + 1) * BLK], axis=0, keepdims=True)
                ds_ref[...] += jnp.where((lane8 == g) & (row8 == 0), jnp.broadcast_to(dsink, (8, 128)), 0.0)
            dq_ref[b * BLK:(b + 1) * BLK, :] = _unstack_heads(_dot(dsc, k2))
            dk2 = _dot_tn(dsc, qs)
            dv2 = _dot_tn(ab, dos)
            cur = pl.ds(pl.multiple_of(n * TM + b * BLK, BLK), BLK)
            dk_ref[cur, :] += dk2[BLK:, :]
            dv_ref[cur, :] += dv2[BLK:, :]
            if b == 0:
                @pl.when(n > 0)
                def _():
                    prv = pl.ds(pl.multiple_of(n * TM - BLK, BLK), BLK)
                    dk_ref[prv, :] += dk2[:BLK, :]
                    dv_ref[prv, :] += dv2[:BLK, :]
            else:
                prv = pl.ds(pl.multiple_of(n * TM + (b - 1) * BLK, BLK), BLK)
                dk_ref[prv, :] += dk2[:BLK, :]
                dv_ref[prv, :] += dv2[:BLK, :]

    cur, prev = _attn_specs()
    col = pl.BlockSpec((T, 256), lambda j, n: (0, j))
    rows = Q_PER_KV * BLK
    return _call(
        bg, body, name="attn_core_bwd", grid=(N_KV, NT),
        in_specs=[cur, cur, cur, prev, cur, prev,
                  pl.BlockSpec((None, nb, rows, 2 * BLK), lambda j, n: (j, n, 0, 0)),
                  pl.BlockSpec((None, nb, rows, 128), lambda j, n: (j, n, 0, 0))],
        out_specs=[cur, col, col, pl.BlockSpec((None, 8, 128), lambda j, n: (j, 0, 0))],
        out_shape=[SDS((T, D), f32), SDS((T, D), f32), SDS((T, D), f32), SDS((N_KV, 8, 128), f32)],
        compiler_params=_cp(dimension_semantics=("arbitrary", "arbitrary")),
    )(q, do, k4, k4, v4, v4, probs, sink_w)


def attn_bwd_q(h, dh, dq, hn, g_mix, wq):
    def body(h_ref, dh_ref, dq_ref, hn_ref, gm_ref, wq_ref, out_ref, dwq_ref, dbq_ref, dgm_ref, aq):
        i = pl.program_id(0)

        @pl.when(i == 0)
        def _():
            aq[...] = jnp.zeros_like(aq)
            dbq_ref[...] = jnp.zeros_like(dbq_ref)
            dgm_ref[...] = jnp.zeros_like(dgm_ref)

        dq_ = dq_ref[...]
        dqb = dq_.astype(bf16)
        aq[...] += _dot_tn(hn_ref[...], dqb)
        dbq_ref[...] += _colsum8(dq_)
        dx, dg = _rms_bwd(h_ref[...], gm_ref[...], _dot_nt(dqb, wq_ref[...]))
        out_ref[...] = dh_ref[...] + dx
        dgm_ref[...] += _colsum8(dg)

        @pl.when(i == NT - 1)
        def _():
            dwq_ref[...] = aq[...].astype(bf16)

    vec = _full((8, D))
    mat = _full((D, D))
    return pl.pallas_call(
        body, name="attn_bwd_q", grid=(NT,),
        in_specs=[_tile()] * 4 + [_full((1, D)), mat],
        out_specs=[_tile(), mat, vec, vec],
        out_shape=[SDS((T, D), f32), SDS((D, D), bf16), SDS((8, D), f32), SDS((8, D), f32)],
        scratch_shapes=[pltpu.VMEM((D, D), f32)],
        compiler_params=_cp(dimension_semantics=("arbitrary",)),
    )(h, dh, dq, hn, g_mix, wq)


def attn_bwd_kv(h, dh, dk4, dv4, kvn, g_kv, wkv, spread):
    def body(h_ref, dh_ref, dk_ref, dv_ref, kvn_ref, gkv_ref, wkv_ref, sp_ref, out_ref, outb_ref, dw_ref, db_ref,
             dgkv_ref, acc):
        i = pl.program_id(0)

        @pl.when(i == 0)
        def _():
            for r in (acc, db_ref, dgkv_ref):
                r[...] = jnp.zeros_like(r)

        dkv = jnp.concatenate([_dot_nt(dk_ref[...].astype(bf16), sp_ref[...]),
                               _dot_nt(dv_ref[...].astype(bf16), sp_ref[...])], axis=1)
        dkvb = dkv.astype(bf16)
        acc[...] += _dot_tn(kvn_ref[...], dkvb)
        db_ref[...] += _colsum8(dkv)
        dx, dg = _rms_bwd(h_ref[...], gkv_ref[...], _dot_nt(dkvb, wkv_ref[...]))
        out = dh_ref[...] + dx
        out_ref[...] = out
        outb_ref[...] = out.astype(bf16)
        dgkv_ref[...] += _colsum8(dg)

        @pl.when(i == NT - 1)
        def _():
            dw_ref[...] = acc[...].astype(bf16)

    return pl.pallas_call(
        body, name="attn_bwd_kv", grid=(NT,),
        in_specs=[_tile()] * 5 + [_full((1, D)), _full((D, 512)), _full((256, D))],
        out_specs=[_tile(), _tile(), _full((D, 512)), _full((8, 512)), _full((8, D))],
        out_shape=[SDS((T, D), f32), SDS((T, D), bf16), SDS((D, 512), bf16), SDS((8, 512), f32), SDS((8, D), f32)],
        scratch_shapes=[pltpu.VMEM((D, 512), f32)],
        compiler_params=_cp(dimension_semantics=("arbitrary",)),
    )(h, dh, dk4, dv4, kvn, g_kv, wkv, spread)


def final_loss(h, g, target):
    def body(h_ref, g_ref, t_ref, loss_ref, dh_ref, dhb_ref, dg_ref):
        i = pl.program_id(0)

        @pl.when(i == 0)
        def _():
            loss_ref[...] = jnp.zeros_like(loss_ref)
            dg_ref[...] = jnp.zeros_like(dg_ref)

        h_ = h_ref[...]
        g_ = g_ref[...]
        y, _ = _rms(h_, g_)
        diff = y - t_ref[...]
        per_tok = jnp.mean(diff * diff, axis=-1, keepdims=True)
        tot = 0.5 * jnp.sum(per_tok, axis=0, keepdims=True)
        lane = lax.broadcasted_iota(jnp.int32, (8, 128), 1)
        row = lax.broadcasted_iota(jnp.int32, (8, 128), 0)
        loss_ref[...] += jnp.where((lane == 0) & (row == 0), jnp.broadcast_to(tot, (8, 128)), 0.0)
        dx, dgt = _rms_bwd(h_, g_, diff * (1.0 / D))
        dh_ref[...] = dx
        dhb_ref[...] = dx.astype(bf16)
        dg_ref[...] += _colsum8(dgt)

    return pl.pallas_call(
        body, name="final_loss", grid=(NT,), in_specs=[_tile(), _full((1, D)), _tile()],
        out_specs=[_full((8, 128)), _tile(), _tile(), _full((8, D))],
        out_shape=[SDS((8, 128), f32), SDS((T, D), f32), SDS((T, D), bf16), SDS((8, D), f32)],
        compiler_params=_cp(dimension_semantics=("arbitrary",)),
    )(h, g, target)


def fwd_bwd(x, target, p, shards, opt, core, chip):
    row = lambda v: v.reshape(1, -1)
    (lam, bm, cm), prep_vjp = jax.vjp(s5_discretize, p["s5_a_re"][0], p["s5_a_im"][0], p["s5_log_dt"][0],
                                      p["s5_b_re"][0], p["s5_b_im"][0], p["s5_c_re"][0], p["s5_c_im"][0])
    bmb, cmb = bm.astype(bf16), cm.astype(bf16)
    lam = jnp.concatenate([lam, lam * jnp.array([1.0, -1.0], f32).reshape(1, 2, 1, 1)], axis=1)
    g_mix0, g_mix1 = row(p["norm_mix"][0]), row(p["norm_mix"][1])
    g_mlp0, g_mlp1 = row(p["norm_mlp"][0]), row(p["norm_mlp"][1])
    g_kv, g_fin = row(p["norm_kv"]), row(p["norm_final"])
    bq, bo = p["b_q"], p["b_o"]
    bkv = row(p["b_kv"])
    spread = _spread4()
    sinks = p["sinks"].reshape(16)

    wglu, gvec = sc_gather([shards["s5_w_glu"], shards["vecs"]], 3, "sc_gather_s5")
    win0, wout0 = sc_gather([shards["w_in0"], shards["w_out0"]], 14, "sc_gather_mlp0")
    wkv, wq, wo = sc_gather([shards["w_kv"], shards["w_q"], shards["w_o"]], 4, "sc_gather_attn")
    win1, wout1 = sc_gather([shards["w_in1"], shards["w_out1"]], 5, "sc_gather_mlp1")
    xp = x
    hn0 = s5_pre(xp, g_mix0)
    ys, xs = s5_core_fwd(hn0, bmb, lam, cmb)
    d_skip = gvec[:, 0, :128].reshape(1, D)
    bglu = gvec[:, 0, 128:].reshape(1, 2 * D)
    y, z, h1 = s5_post(ys, xp, g_mix0, d_skip, wglu, bglu)
    hm0, r0, h2p = mlp_fwd(h1, g_mlp0, win0, wout0, 0)
    wkv, wq, wo = wkv.reshape(D, 512), wq.reshape(D, D), wo.reshape(D, D)
    h2 = h2p
    kvn, hn1, k4, v4, q = attn_pre(h2, g_kv, g_mix1, wkv, bkv, spread, wq, bq)
    o, probs, sink_w = attn_core_fwd(q, k4, v4, sinks)
    h3 = attn_post(h2, o, wo, bo)
    hm1, r1, h4 = mlp_fwd(h3, g_mlp1, win1, wout1, 1)
    loss, dh4, dh4b, dg_fin = final_loss(h4, g_fin, target)

    def pair_sums(names, grads, cid, before):
        r1 = sc_comm(BgPair(grads), cid, "sc_pair_" + names[0])
        parts = [add_pairs(g, r, core, f"add_pairs_{n}") for n, g, r in zip(names, grads, r1)]
        before, parts = lax.optimization_barrier((before, parts))
        return before, parts

    def across_chips(names, parts, cid):
        return list(zip(parts, sc_comm(BgChips(parts), cid, "sc_chips_" + names[0])))

    dh3, dwin1, dwout1, dg_mlp1 = mlp_bwd(h3, hm1, r1, g_mlp1, dh4, dh4b, win1, wout1, 1)
    do, dwo, dbo = attn_bwd_pre(dh3, o, wo)
    do, parts = pair_sums(["w_in1", "w_out1"], [dwin1, dwout1], 6, do)
    rs_in1, rs_out1 = across_chips(["w_in1", "w_out1"], parts, 7)
    dq, dk4, dv4, dsink = attn_core_bwd(q, do, k4, v4, probs, sink_w)
    dh2, dwq, dbq, dg_mix1 = attn_bwd_q(h2, dh3, dq, hn1, g_mix1, wq)
    dh2, dh2b, dwkv, dbkv, dg_kv = attn_bwd_kv(h2, dh2, dk4, dv4, kvn, g_kv, wkv, spread)
    dh2p, dh2pb = dh2, dh2b
    big = {}
    a_in1 = adam_big(*opt["w_mlp_in"], *rs_in1, chip, "adam_w_mlp_in1", layer=1)
    a_out1 = adam_big(*opt["w_mlp_out"], *rs_out1, chip, "adam_w_mlp_out1", layer=1)
    dh2p, a_in1, a_out1 = lax.optimization_barrier((dh2p, a_in1, a_out1))
    names = ["w_kv", "w_q", "w_o"]
    dh2p, parts = pair_sums(names, [dwkv.reshape(NDEV, 128, 512), dwq.reshape(NDEV, 128, D),
                                    dwo.reshape(NDEV, 128, D)], 8, dh2p)
    rs_attn = across_chips(names, parts, 9)
    dh1, dwin0, dwout0, dg_mlp0 = mlp_bwd(h1, hm0, r0, g_mlp0, dh2p, dh2pb, win0, wout0, 0)
    a_attn = [adam_big(*opt[n], *rs, chip, f"adam_{n}") for n, rs in zip(names, rs_attn)]
    dh1, a_attn = lax.optimization_barrier((dh1, a_attn))
    big.update(zip(names, a_attn))
    dy, dwglu, dbglu = s5_post_bwd(dh1, y, z, wglu)
    dy, parts = pair_sums(["w_in0", "w_out0"], [dwin0, dwout0], 10, dy)
    rs_in0, rs_out0 = across_chips(["w_in0", "w_out0"], parts, 11)
    du, dbm, dcmt, dlam = s5_core_bwd(hn0, dy, xs, bmb, lam, cmb)
    du, parts = pair_sums(["s5_w_glu"], [dwglu], 12, du)
    rs_glu, = across_chips(["s5_w_glu"], parts, 13)
    dxp, dg_mix0, dd = s5_pre_bwd(xp, g_mix0, du, dy, d_skip, dh1)
    big["w_mlp_in"] = adam_big(*opt["w_mlp_in"], *rs_in0, chip, "adam_w_mlp_in0", layer=0, prev=a_in1)
    big["w_mlp_out"] = adam_big(*opt["w_mlp_out"], *rs_out0, chip, "adam_w_mlp_out0", layer=0, prev=a_out1)
    big["s5_w_glu"] = adam_big(*opt["s5_w_glu"], *rs_glu, chip, "adam_s5_w_glu")
    grad_x = dxp
    da_re, da_im, dlog_dt, db_re, db_im, dc_re, dc_im = prep_vjp((dlam, dbm, dcmt.transpose(0, 2, 1)))

    def lanes(v_):
        v_ = v_.reshape(1, -1)
        return jnp.pad(v_, ((0, 0), (0, D - v_.shape[1])))

    small = dict(
        rows8=[dg_mix0, dg_mix1, dg_mlp0, dg_mlp1, dg_kv, dg_fin, dd, dbq, dbo], b_glu=dbglu, b_kv=dbkv,
        misc=jnp.concatenate([lanes(dsink[:, 0, :Q_PER_KV]), lanes(dlog_dt), lanes(loss[0:1, 0:1])], axis=0),
        s5=[da_re.reshape(4, D), da_im.reshape(4, D),
            db_re.transpose(0, 2, 1).reshape(64, D), db_im.transpose(0, 2, 1).reshape(64, D),
            dc_re.reshape(64, D), dc_im.reshape(64, D)])
    small, big["w_mlp_in"], big["w_mlp_out"] = lax.optimization_barrier((small, big["w_mlp_in"], big["w_mlp_out"]))
    return loss, grad_x, small, big


def _row_tile(r, c):
    return min(r, max(8, (512 * 1024) // c))


def add_pairs(g, r1, core, name):
    _, R, C = g.shape
    tr = _row_tile(R, C)

    def body(core_ref, g_ref, r_ref, o_ref):
        o_ref[...] = (g_ref[...].astype(f32) + r_ref[...].astype(f32)).astype(bf16)

    return pl.pallas_call(
        body, name=name, out_shape=SDS((4, R, C), bf16),
        grid_spec=pltpu.PrefetchScalarGridSpec(
            num_scalar_prefetch=1, grid=(4, R // tr),
            in_specs=[pl.BlockSpec((None, tr, C), lambda k, i, core: (2 * k + core[0], i, 0)),
                      pl.BlockSpec((None, tr, C), lambda k, i, core: (k, i, 0))],
            out_specs=pl.BlockSpec((None, tr, C), lambda k, i, core: (k, i, 0))),
        compiler_params=_cp(dimension_semantics=("arbitrary", "arbitrary")),
    )(core, g, r1)


def _adamw(w, g, m, v):
    m = ADAM_B1 * m + (1.0 - ADAM_B1) * g
    v = ADAM_B2 * v + (1.0 - ADAM_B2) * (g * g)
    m_hat = m / (1.0 - ADAM_B1 ** ADAM_STEP)
    v_hat = v / (1.0 - ADAM_B2 ** ADAM_STEP)
    delta = -ADAM_LR * (m_hat / (jnp.sqrt(v_hat) + ADAM_EPS) + ADAM_WD * w)
    return delta, m, v


def adam_big(w, m, v, part, r2, chip, name, layer=0, prev=None):
    L, R, C = w.shape
    tr = R // 4

    def body(chip_ref, w_ref, m_ref, v_ref, p_ref, r_ref, *rest):
        g_out, d_out, m_out, v_out = rest[-4:]
        g = p_ref[...].astype(f32) + r_ref[0].astype(f32) + r_ref[1].astype(f32) + r_ref[2].astype(f32)
        d, m_, v_ = _adamw(w_ref[...], g, m_ref[...], v_ref[...])
        g_out[...] = g
        d_out[...] = d
        m_out[...] = m_
        v_out[...] = v_

    blk = pl.BlockSpec((None, tr, C), lambda i, chip: (layer, i, 0))
    extra = [] if prev is None else list(prev)
    return pl.pallas_call(
        body, name=name, out_shape=[SDS((L, R, C), f32)] * 4,
        grid_spec=pltpu.PrefetchScalarGridSpec(
            num_scalar_prefetch=1, grid=(R // tr,),
            in_specs=[blk, blk, blk,
                      pl.BlockSpec((None, tr, C), lambda i, chip: (chip[0], i, 0)),
                      pl.BlockSpec((3, tr, C), lambda i, chip: (0, i, 0))] + [_ANY] * len(extra),
            out_specs=[blk] * 4),
        input_output_aliases={6 + k: k for k in range(len(extra))},
        compiler_params=_cp(dimension_semantics=("arbitrary",)),
    )(chip, w, m, v, part, r2, *extra)


SMALL_BUF_ROWS = 288


def allreduce_small(rows8, b_glu, b_kv, misc, s5):
    R = SMALL_BUF_ROWS
    half, quarter = R // 2, R // 4
    pieces = [*rows8, b_glu, b_kv, misc, *s5]

    def body(*refs):
        ins, (out_ref, in_ref, acc1, acc2, r0, r1, r2, send_sems, recv_sems) = refs[:len(pieces)], refs[len(pieces):]
        in_ref[8:16, :] = jnp.zeros((8, D), f32)
        in_ref[R - 8:R, :] = jnp.zeros((8, D), f32)
        for k in range(len(rows8)):
            in_ref[k:k + 1, :] = ins[k][0:1, :]
        glu_ref, kv_ref, misc_ref = ins[len(rows8):len(rows8) + 3]
        in_ref[9:10, :] = glu_ref[0:1, 0:D]
        in_ref[10:11, :] = glu_ref[0:1, D:2 * D]
        in_ref[11:12, 0:kv_ref.shape[1]] = kv_ref[0:1, :]
        in_ref[12:15, :] = misc_ref[...]
        row = 16
        for a in ins[len(rows8) + 3:]:
            in_ref[row:row + a.shape[0], :] = a[...]
            row += a.shape[0]
        x, y, c = _pos()
        sibling, over_x, over_y = (x, y, 1 - c), (1 - x, y, c), (x, 1 - y, c)
        first = pl.multiple_of(c * half, 8)
        mine = pl.ds(first, half)
        theirs = pl.ds(pl.multiple_of((1 - c) * half, 8), half)
        qa = pl.ds(first, quarter)
        qb = pl.ds(pl.multiple_of(first + quarter, 8), quarter)

        def exchange(copies):
            cps = [pltpu.make_async_remote_copy(
                src_ref=src.at[rows], dst_ref=dst.at[rows], send_sem=send_sems.at[k], recv_sem=recv_sems.at[k],
                device_id=peer, device_id_type=MESH) for k, src, dst, rows, peer in copies]
            for cp in cps:
                cp.start()
            for cp in cps:
                cp.wait()

        exchange([(0, in_ref, r0, theirs, sibling)])
        acc1[mine, :] = in_ref[mine, :] + r0[mine, :]
        exchange([(1, acc1, r1, qa, over_x), (2, acc1, r1, qb, over_y)])
        acc2[mine, :] = acc1[mine, :] + r1[mine, :]
        exchange([(3, acc2, r2, qa, over_y), (4, acc2, r2, qb, over_x)])
        out_ref[mine, :] = acc2[mine, :] + r2[mine, :]
        exchange([(5, out_ref, out_ref, mine, sibling)])

    vm = pl.BlockSpec(memory_space=pltpu.VMEM)
    return pl.pallas_call(
        body, name="allreduce_small", in_specs=[vm] * len(pieces), out_specs=vm, out_shape=SDS((R, D), f32),
        scratch_shapes=[pltpu.VMEM((R, D), f32)] * 6 + [pltpu.SemaphoreType.DMA((6,)), pltpu.SemaphoreType.DMA((6,))],
    )(*pieces)


SMALL_ROWS = {'norm_mix': (0, 2, D), 'norm_mlp': (2, 2, D), 'norm_kv': (4, 1, D), 'norm_final': (5, 1, D),
              's5_d': (6, 1, D), 'b_q': (7, 1, D), 'b_o': (8, 1, D), 's5_b_glu': (9, 2, D), 'b_kv': (11, 1, 512),
              'sinks': (12, 1, 16), 's5_log_dt': (13, 1, 64), 's5_a_re': (16, 4, D), 's5_a_im': (20, 4, D),
              's5_b_re': (24, 64, D), 's5_b_im': (88, 64, D), 's5_c_re': (152, 64, D), 's5_c_im': (216, 64, D)}
LOSS_ROW = 14
ROW_PARAMS = ['norm_mix', 'norm_mlp', 'norm_kv', 'norm_final', 'b_q', 'b_o', 'b_kv', 'sinks', 's5_log_dt']
SHARD_PARAMS = ['s5_d', 's5_b_glu']
S5_PARAMS = ['s5_a_re', 's5_a_im', 's5_b_re', 's5_b_im', 's5_c_re', 's5_c_im']


def adam_small(dev, gsum, s5_grads, w, m, v):
    names = ROW_PARAMS + SHARD_PARAMS + S5_PARAMS
    n_g = len(ROW_PARAMS) + len(SHARD_PARAMS)

    def body(dev_ref, gs_ref, *refs):
        pos = [0]

        def take(k):
            r = refs[pos[0]:pos[0] + k]
            pos[0] += k
            return r

        g5 = take(len(S5_PARAMS))
        wr, mr, vr = take(len(names)), take(len(names)), take(len(names))
        g_out = take(n_g)
        d_out, m_out, v_out = take(len(names)), take(len(names)), take(len(names))
        dv = dev_ref[0]
        for i, n in enumerate(names):
            if n in S5_PARAMS:
                g = g5[S5_PARAMS.index(n)][...]
            elif n in SHARD_PARAMS:
                r0, _, _ = SMALL_ROWS[n]
                ln = wr[i].shape[1]
                g = jnp.zeros((1, ln), f32)
                for k in range(NDEV):
                    off = k * ln
                    piece = gs_ref[r0 + off // D:r0 + off // D + 1, off % D:off % D + ln]
                    g = g + jnp.where(dv == k, piece, 0.0)
                g_out[i][...] = g
            else:
                r0, nr, nl = SMALL_ROWS[n]
                g = gs_ref[r0:r0 + nr, 0:nl]
                g_out[i][...] = g
            d, m_, v_ = _adamw(wr[i][...], g, mr[i][...], vr[i][...])
            d_out[i][...] = d
            m_out[i][...] = m_
            v_out[i][...] = v_

    vm = pl.BlockSpec(memory_space=pltpu.VMEM)
    ins = [s5_grads[n] for n in S5_PARAMS] + [d[n] for d in (w, m, v) for n in names]
    shapes = [SDS(w[n].shape, f32) for n in names]
    res = pl.pallas_call(
        body, name="adam_small", in_specs=[pl.BlockSpec(memory_space=pltpu.SMEM)] + [vm] * (1 + len(ins)),
        out_specs=[vm] * (n_g + 3 * len(names)), out_shape=shapes[:n_g] + shapes * 3,
        compiler_params=_cp(),
    )(dev, gsum, *ins)
    g_o = dict(zip(names[:n_g], res[:n_g]))
    rest = res[n_g:]
    k = len(names)
    return g_o, dict(zip(names, rest[:k])), dict(zip(names, rest[k:2 * k])), dict(zip(names, rest[2 * k:]))


WEIGHTS = ['norm_mix', 'norm_mlp', 'norm_kv', 'norm_final', 's5_a_re', 's5_a_im', 's5_log_dt', 's5_b_re', 's5_b_im',
           's5_c_re', 's5_c_im', 's5_d', 's5_w_glu', 's5_b_glu', 'w_kv', 'b_kv', 'w_q', 'b_q', 'sinks', 'w_o', 'b_o',
           'w_mlp_in', 'w_mlp_out']
BIG = ['s5_w_glu', 'w_kv', 'w_q', 'w_o', 'w_mlp_in', 'w_mlp_out']
BIG_2D = {'s5_w_glu': (D, 256), 'w_kv': (128, 512), 'w_q': (128, D), 'w_o': (128, D), 'w_mlp_in': (2 * D, 512),
          'w_mlp_out': (2 * 512, D)}
SMALL = [n for n in WEIGHTS if n not in BIG]


def kernel(x, norm_mix, norm_mlp, norm_kv, norm_final, s5_a_re, s5_a_im, s5_log_dt, s5_b_re, s5_b_im, s5_c_re, s5_c_im, s5_d, s5_w_glu, s5_b_glu, w_kv, b_kv, w_q, b_q, sinks, w_o, b_o, w_mlp_in, w_mlp_out, loss_target, m_norm_mix, m_norm_mlp, m_norm_kv, m_norm_final, m_s5_a_re, m_s5_a_im, m_s5_log_dt, m_s5_b_re, m_s5_b_im, m_s5_c_re, m_s5_c_im, m_s5_d, m_s5_w_glu, m_s5_b_glu, m_w_kv, m_b_kv, m_w_q, m_b_q, m_sinks, m_w_o, m_b_o, m_w_mlp_in, m_w_mlp_out, v_norm_mix, v_norm_mlp, v_norm_kv, v_norm_final, v_s5_a_re, v_s5_a_im, v_s5_log_dt, v_s5_b_re, v_s5_b_im, v_s5_c_re, v_s5_c_im, v_s5_d, v_s5_w_glu, v_s5_b_glu, v_w_kv, v_b_kv, v_w_q, v_b_q, v_sinks, v_w_o, v_b_o, v_w_mlp_in, v_w_mlp_out):
    w = dict(norm_mix=norm_mix, norm_mlp=norm_mlp, norm_kv=norm_kv, norm_final=norm_final, s5_a_re=s5_a_re,
             s5_a_im=s5_a_im, s5_log_dt=s5_log_dt, s5_b_re=s5_b_re, s5_b_im=s5_b_im, s5_c_re=s5_c_re, s5_c_im=s5_c_im,
             s5_d=s5_d, s5_w_glu=s5_w_glu, s5_b_glu=s5_b_glu, w_kv=w_kv, b_kv=b_kv, w_q=w_q, b_q=b_q, sinks=sinks,
             w_o=w_o, b_o=b_o, w_mlp_in=w_mlp_in, w_mlp_out=w_mlp_out)
    m = dict(norm_mix=m_norm_mix, norm_mlp=m_norm_mlp, norm_kv=m_norm_kv, norm_final=m_norm_final, s5_a_re=m_s5_a_re,
             s5_a_im=m_s5_a_im, s5_log_dt=m_s5_log_dt, s5_b_re=m_s5_b_re, s5_b_im=m_s5_b_im, s5_c_re=m_s5_c_re,
             s5_c_im=m_s5_c_im, s5_d=m_s5_d, s5_w_glu=m_s5_w_glu, s5_b_glu=m_s5_b_glu, w_kv=m_w_kv, b_kv=m_b_kv,
             w_q=m_w_q, b_q=m_b_q, sinks=m_sinks, w_o=m_w_o, b_o=m_b_o, w_mlp_in=m_w_mlp_in, w_mlp_out=m_w_mlp_out)
    v = dict(norm_mix=v_norm_mix, norm_mlp=v_norm_mlp, norm_kv=v_norm_kv, norm_final=v_norm_final, s5_a_re=v_s5_a_re,
             s5_a_im=v_s5_a_im, s5_log_dt=v_s5_log_dt, s5_b_re=v_s5_b_re, s5_b_im=v_s5_b_im, s5_c_re=v_s5_c_re,
             s5_c_im=v_s5_c_im, s5_d=v_s5_d, s5_w_glu=v_s5_w_glu, s5_b_glu=v_s5_b_glu, w_kv=v_w_kv, b_kv=v_b_kv,
             w_q=v_w_q, b_q=v_b_q, sinks=v_sinks, w_o=v_w_o, b_o=v_b_o, w_mlp_in=v_w_mlp_in, w_mlp_out=v_w_mlp_out)
    xi, yi, ci = _pos()
    dev = 4 * xi + 2 * yi + ci
    core = ci.reshape(1).astype(jnp.int32)
    chip = (2 * xi + yi).reshape(1).astype(jnp.int32)

    shards = {
        "s5_w_glu": s5_w_glu[0].astype(bf16), "w_kv": w_kv.astype(bf16), "w_q": w_q[0].astype(bf16),
        "w_o": w_o[0].astype(bf16), "w_in0": w_mlp_in[0].astype(bf16), "w_in1": w_mlp_in[1].astype(bf16),
        "w_out0": w_mlp_out[0].astype(bf16), "w_out1": w_mlp_out[1].astype(bf16),
        "vecs": jnp.broadcast_to(jnp.concatenate([s5_d, s5_b_glu], axis=1), (8, 384)),
    }
    as3d = lambda a, n: a if a.ndim == 3 and a.shape[0] == 2 else a.reshape((1,) + BIG_2D[n])
    opt = {n: (as3d(w[n], n), as3d(m[n], n), as3d(v[n], n)) for n in BIG}
    _, grad_x, grads, big = fwd_bwd(x[0], loss_target[0], {n: w[n] for n in SMALL}, shards, opt, core, chip)

    gsum = allreduce_small(**grads)

    out_g, out_d, out_m, out_v = {}, {}, {}, {}
    for n in BIG:
        out_g[n], out_d[n], out_m[n], out_v[n] = [r.reshape(w[n].shape) for r in big[n]]

    loss = gsum[LOSS_ROW, 0]
    swapped = ("s5_b_re", "s5_b_im")
    swap = lambda a: a.transpose(0, 1, 3, 2)

    def kernel_side(d):
        d = {n: (d[n].reshape(1, -1) if d[n].ndim == 1 else d[n]) for n in SMALL}
        d.update({n: swap(d[n]) for n in swapped})
        return d

    s5_g = {}
    for n in S5_PARAMS:
        r0, nr, _ = SMALL_ROWS[n]
        s5_g[n] = gsum[r0:r0 + nr].reshape((1, 64, 16, 64) if n in swapped else w[n].shape)
        out_g[n] = s5_g[n]
    g_s, d_s, m_s, v_s = adam_small(dev.reshape(1).astype(jnp.int32), gsum, s5_g, kernel_side(w), kernel_side(m),
                                    kernel_side(v))
    for src, dst in ((g_s, out_g), (d_s, out_d), (m_s, out_m), (v_s, out_v)):
        dst.update(src)
    for dst in (out_g, out_d, out_m, out_v):
        for n in SMALL:
            dst[n] = (swap(dst[n]) if n in swapped else dst[n]).reshape(w[n].shape)

    return (loss, grad_x[None], *[out_g[n] for n in WEIGHTS], *[out_d[n] for n in WEIGHTS],
            *[out_m[n] for n in WEIGHTS], *[out_v[n] for n in WEIGHTS])
```

```python
import functools
import math

import jax
import jax.numpy as jnp
from jax import lax
from jax.experimental import pallas as pl
from jax.experimental.pallas import tpu as pltpu
from jax.experimental.pallas import tpu_sc as plsc

f32 = jnp.float32
bf16 = jnp.bfloat16
SDS = jax.ShapeDtypeStruct

T = 2048
D = 1024
NDEV = 8
NORM_EPS = 1e-5
S5_G, S5_C, S5_P = 64, 16, 64
S5_SUB = 8
S5_CH = 8
S5_STEPS = T // S5_CH
DT_MIN_LAMBDA = -1e-4
HEAD_DIM = 64
N_KV = 4
Q_PER_KV = 4
BLK = 128
D_FF_SHARD = 512
ADAM_LR, ADAM_B1, ADAM_B2, ADAM_EPS, ADAM_WD, ADAM_STEP = 0.001, 0.9, 0.999, 1e-08, 0.01, 10
VMEM_LIMIT = 56 * 1024 * 1024
MESH = pl.DeviceIdType.MESH


def _cp(**kw):
    return pltpu.CompilerParams(vmem_limit_bytes=VMEM_LIMIT, **kw)


def _dot(a, b):
    return jnp.dot(a, b, preferred_element_type=f32)


def _dot_nt(a, b):
    return lax.dot_general(a, b, (((1,), (1,)), ((), ())), preferred_element_type=f32)


def _dot_tn(a, b):
    return lax.dot_general(a, b, (((0,), (0,)), ((), ())), preferred_element_type=f32)


def _rms(x, g):
    r = lax.rsqrt(jnp.mean(x * x, axis=-1, keepdims=True) + NORM_EPS)
    return x * r * g, r


def _rms_bwd(x, g, dy):
    r = lax.rsqrt(jnp.mean(x * x, axis=-1, keepdims=True) + NORM_EPS)
    u = dy * g
    dx = r * u - (r * r * r) * x * jnp.mean(u * x, axis=-1, keepdims=True)
    return dx, dy * x * r


def _colsum8(v):
    s = jnp.sum(v, axis=0, keepdims=True)
    row = lax.broadcasted_iota(jnp.int32, (8, v.shape[1]), 0)
    return jnp.where(row == 0, jnp.broadcast_to(s, (8, v.shape[1])), 0.0)


def _full(shape):
    nd = len(shape)
    return pl.BlockSpec(shape, lambda *_: (0,) * nd, pipeline_mode=pl.Buffered(1))


_ANY = pl.BlockSpec(memory_space=pl.ANY)


def _pos():
    return lax.axis_index("x"), lax.axis_index("y"), lax.axis_index("c")


def _other_chips(x, y):
    return [(1 - x, y), (x, 1 - y), (1 - x, 1 - y)]


class BgGather:
    SIB, XN, YN, FWD_Y, FWD_X, SIB_X, SIB_Y, SIB_D = range(8)

    def __init__(self, arrs, mids=(0.5, 0.75)):
        n = len(arrs)
        self.arrs = list(arrs)
        self.out_shape = [SDS((NDEV,) + a.shape, a.dtype) for a in arrs]
        self.scratch = [pltpu.SemaphoreType.DMA((n, 8)), pltpu.SemaphoreType.DMA((n, 8)),
                        pltpu.SemaphoreType.DMA((n,))]
        self.mids = mids
        self.result = None

    @staticmethod
    def peers(x, y, c):
        return [(x, y, 1 - c), (1 - x, y, c), (x, 1 - y, c)]

    def mid_steps(self, nsteps):
        at = lambda f: min(nsteps - 1, max(0, int(f * nsteps) - 1))
        return [(at(self.mids[0]), self.mid), (max(at(self.mids[0]), at(self.mids[1])), self.mid2)]

    def _halves(self, a):
        rows = self.arrs[a].shape[0]
        cut = (rows // 32) * 16 if rows >= 32 else rows
        return (0, cut), (cut, rows - cut)

    def _copy(self, ins, outs, sems, a, k, block, to, own=False, part=None):
        slot = 4 * block[0] + 2 * block[1] + block[2]
        rows = pl.ds(0, self.arrs[a].shape[0]) if part is None else pl.ds(*self._halves(a)[part])
        dst = outs[a].at[slot, rows]
        return pltpu.make_async_remote_copy(
            src_ref=ins[a].at[rows] if own else dst, dst_ref=dst, send_sem=sems[0].at[a, k],
            recv_sem=sems[1].at[a, k], device_id=to, device_id_type=MESH)

    def _mine(self, ins, outs, sems):
        x, y, c = _pos()
        return [pltpu.make_async_copy(ins[a], outs[a].at[4 * x + 2 * y + c], sems[2].at[a])
                for a in range(len(self.arrs))]

    def _split(self, a):
        return self._halves(a)[1][1] > 0

    def _sends(self, ins, outs, sems, phase):
        x, y, c = _pos()
        me, sib, xn, yn, dg = (x, y, c), (x, y, 1 - c), (1 - x, y, c), (x, 1 - y, c), (1 - x, 1 - y, c)
        cps = []
        for a in range(len(self.arrs)):
            cp = lambda k, block, to, **kw: self._copy(ins, outs, sems, a, k, block, to, **kw)
            if phase == 0:
                cps += [cp(self.SIB, me, sib, own=True), cp(self.XN, me, xn, own=True), cp(self.YN, me, yn, own=True)]
            elif phase == 1:
                cps.append(cp(self.FWD_Y, xn, yn, part=0))
                if self._split(a):
                    cps.append(cp(self.FWD_X, yn, xn, part=1))
                cps += [cp(self.SIB_X, xn, sib), cp(self.SIB_Y, yn, sib)]
            else:
                cps.append(cp(self.SIB_D, dg, sib))
        return cps

    def _arrivals(self, ins, outs, sems, phase):
        x, y, c = _pos()
        me, xn, yn, dg = (x, y, c), (1 - x, y, c), (x, 1 - y, c), (1 - x, 1 - y, c)
        cps = []
        for a in range(len(self.arrs)):
            cp = lambda k, block, **kw: self._copy(ins, outs, sems, a, k, block, me, **kw)
            if phase == 1:
                cps += [cp(self.XN, xn), cp(self.YN, yn)]
            elif phase == 2:
                cps.append(cp(self.FWD_Y, dg, part=0))
                if self._split(a):
                    cps.append(cp(self.FWD_X, dg, part=1))
            else:
                cps += [cp(self.SIB, (x, y, 1 - c)), cp(self.SIB_X, (1 - x, y, 1 - c)),
                        cp(self.SIB_Y, (x, 1 - y, 1 - c)), cp(self.SIB_D, (1 - x, 1 - y, 1 - c))]
        return cps

    def start(self, ins, outs, sems):
        for cp in self._mine(ins, outs, sems) + self._sends(ins, outs, sems, 0):
            cp.start()

    def mid(self, ins, outs, sems):
        for cp in self._arrivals(ins, outs, sems, 1):
            cp.wait_recv()
        for cp in self._sends(ins, outs, sems, 1):
            cp.start()

    def mid2(self, ins, outs, sems):
        for cp in self._arrivals(ins, outs, sems, 2):
            cp.wait_recv()
        for cp in self._sends(ins, outs, sems, 2):
            cp.start()

    def finish(self, ins, outs, sems):
        for cp in self._arrivals(ins, outs, sems, 3):
            cp.wait_recv()
        for ph in range(3):
            for cp in self._sends(ins, outs, sems, ph):
                cp.wait_send()
        for cp in self._mine(ins, outs, sems):
            cp.wait()


def sc_comm(g, collective_id, name):
    srcs = [jax.new_ref(a, memory_space=pltpu.MemorySpace.HBM) for a in g.arrs]
    dsts = [jax.empty_ref(s, memory_space=pltpu.MemorySpace.HBM) for s in g.out_shape]

    @pl.kernel(mesh=plsc.ScalarSubcoreMesh(axis_name="sequencer", num_cores=1), name=name,
               scratch_types=tuple(g.scratch), compiler_params=pltpu.CompilerParams(collective_id=collective_id))
    def launch(*sems):
        peers = g.peers(*_pos())
        barrier = pltpu.get_barrier_semaphore()
        for peer in peers:
            pl.semaphore_signal(barrier, inc=1, device_id=peer, device_id_type=MESH)
        pl.semaphore_wait(barrier, len(peers))
        g.start(srcs, dsts, sems)
        for _, phase in g.mid_steps(1):
            phase(srcs, dsts, sems)
        g.finish(srcs, dsts, sems)

    launch()
    return [d[...] for d in dsts]


def sc_gather(arrs, collective_id, name):
    return sc_comm(BgGather(arrs), collective_id, name)


class BgPair:
    def __init__(self, arrs):
        n = len(arrs)
        self.arrs = list(arrs)
        self.out_shape = [SDS((4,) + a.shape[1:], a.dtype) for a in arrs]
        self.scratch = [pltpu.SemaphoreType.DMA((n, 4)), pltpu.SemaphoreType.DMA((n, 4))]
        self.result = None

    @staticmethod
    def peers(x, y, c):
        return [(x, y, 1 - c)]

    def mid_steps(self, nsteps):
        return []

    def _copies(self, ins, outs, sems):
        x, y, c = _pos()
        return [pltpu.make_async_remote_copy(
            src_ref=ins[a].at[2 * k + 1 - c], dst_ref=outs[a].at[k], send_sem=sems[0].at[a, k],
            recv_sem=sems[1].at[a, k], device_id=(x, y, 1 - c), device_id_type=MESH)
            for a in range(len(self.arrs)) for k in range(4)]

    def start(self, ins, outs, sems):
        for cp in self._copies(ins, outs, sems):
            cp.start()

    def finish(self, ins, outs, sems):
        cps = self._copies(ins, outs, sems)
        for cp in cps:
            cp.wait_recv()
        for cp in cps:
            cp.wait_send()


class BgChips(BgPair):
    def __init__(self, arrs):
        n = len(arrs)
        self.arrs = list(arrs)
        self.out_shape = [SDS((3,) + a.shape[1:], a.dtype) for a in arrs]
        self.scratch = [pltpu.SemaphoreType.DMA((n, 3)), pltpu.SemaphoreType.DMA((n, 3))]
        self.result = None

    @staticmethod
    def peers(x, y, c):
        return [(px, py, c) for px, py in _other_chips(x, y)]

    def _copies(self, ins, outs, sems):
        x, y, c = _pos()
        return [pltpu.make_async_remote_copy(
            src_ref=ins[a].at[2 * px + py], dst_ref=outs[a].at[r], send_sem=sems[0].at[a, r],
            recv_sem=sems[1].at[a, r], device_id=(px, py, c), device_id_type=MESH)
            for a in range(len(self.arrs)) for r, (px, py) in enumerate(_other_chips(x, y))]


def _call(bgs, body, *, name, grid, in_specs, out_specs, out_shape, scratch_shapes=(), compiler_params=None):
    single = not isinstance(out_shape, (list, tuple))
    out_specs_l = [out_specs] if single else list(out_specs)
    out_shape_l = [out_shape] if single else list(out_shape)
    bgs = [b for b in (bgs or []) if b is not None]
    n_in, n_out, n_sc = len(in_specs), len(out_shape_l), len(scratch_shapes)
    nsteps = math.prod(grid)
    b_in_specs = [b.in_specs(grid) if hasattr(b, "in_specs") else [_ANY] * len(b.arrs) for b in bgs]
    b_out_specs = [b.out_specs(grid) if hasattr(b, "out_specs") else [_ANY] * len(b.out_shape) for b in bgs]
    aliases, i_off, o_off = {}, n_in, n_out
    for b in bgs:
        aliases.update({i_off + i: o_off + o for i, o in getattr(b, "aliases", {}).items()})
        i_off, o_off = i_off + len(b.arrs), o_off + len(b.out_shape)

    def full(*refs):
        pos = [0]

        def take(k):
            r = refs[pos[0]:pos[0] + k]
            pos[0] += k
            return r

        ins = take(n_in)
        b_ins = [take(len(b.arrs)) for b in bgs]
        outs = take(n_out)
        b_outs = [take(len(b.out_shape)) for b in bgs]
        sc = take(n_sc)
        b_sc = [take(len(b.scratch)) for b in bgs]
        if bgs:
            step = pl.program_id(0)
            for d in range(1, len(grid)):
                step = step * grid[d] + pl.program_id(d)

            @pl.when(step == 0)
            def _():
                for b, i_, o_, s_ in zip(bgs, b_ins, b_outs, b_sc):
                    b.start(i_, o_, s_)

        body(*ins, *outs, *sc)
        if bgs:
            for b, i_, o_, s_ in zip(bgs, b_ins, b_outs, b_sc):
                if hasattr(b, "step"):
                    b.step(i_, o_, s_)
                for at, fn in b.mid_steps(nsteps):
                    @pl.when(step == at)
                    def _():
                        fn(i_, o_, s_)

            @pl.when(step == nsteps - 1)
            def _():
                for b, i_, o_, s_ in zip(bgs, b_ins, b_outs, b_sc):
                    b.finish(i_, o_, s_)

    def run(*args):
        res = pl.pallas_call(
            full, name=name, grid=grid,
            in_specs=list(in_specs) + [s for l in b_in_specs for s in l],
            out_specs=out_specs_l + [s for l in b_out_specs for s in l],
            out_shape=out_shape_l + [s for b in bgs for s in b.out_shape],
            scratch_shapes=list(scratch_shapes) + [s for b in bgs for s in b.scratch],
            input_output_aliases=aliases,
            compiler_params=compiler_params,
        )(*args, *[a for b in bgs for a in b.arrs])
        rest = list(res[n_out:])
        for b in bgs:
            b.result, rest = rest[:len(b.out_shape)], rest[len(b.out_shape):]
        return res[0] if single else list(res[:n_out])

    return run


def s5_discretize(a_re, a_im, log_dt, b_re, b_im, c_re, c_im):
    lam_r = jnp.minimum(a_re, DT_MIN_LAMBDA)
    lam_i = a_im
    dt = jnp.exp(log_dt)[:, None]
    e = jnp.exp(lam_r * dt)
    lbr = e * jnp.cos(lam_i * dt)
    lbi = e * jnp.sin(lam_i * dt)
    den = lam_r * lam_r + lam_i * lam_i
    cf_r = ((lbr - 1.0) * lam_r + lbi * lam_i) / den
    cf_i = (lbi * lam_r - (lbr - 1.0) * lam_i) / den
    bb_r = cf_r[:, :, None] * b_re - cf_i[:, :, None] * b_im
    bb_i = cf_r[:, :, None] * b_im + cf_i[:, :, None] * b_re
    eye = jnp.eye(8, dtype=f32)

    def blk_b(m):
        return jnp.einsum('bgpc,gh->bgchp', m.reshape(8, 8, S5_P, S5_C), eye).reshape(8, 128, 512)

    def blk_c(m):
        return jnp.einsum('bgcp,gh->bgphc', m.reshape(8, 8, S5_C, S5_P), eye).reshape(8, 512, 128)

    bm = jnp.concatenate([blk_b(bb_r), blk_b(bb_i)], axis=-1)
    cm = jnp.concatenate([blk_c(c_re), -blk_c(c_im)], axis=1)
    lam = jnp.stack([lbr.reshape(8, 512), lbi.reshape(8, 512)], axis=1)
    lam = jnp.broadcast_to(lam[:, :, None, :], (8, 2, 8, 512))
    return lam, bm, cm


def _cmul(ar, ai, br, bi):
    return ar * br - ai * bi, ar * bi + ai * br


def _shift_rows(v, k, up):
    row = lax.broadcasted_iota(jnp.int32, v.shape, 0)
    if up:
        return jnp.where(row < 8 - k, pltpu.roll(v, 8 - k, 0), 0.0)
    return jnp.where(row >= k, pltpu.roll(v, k, 0), 0.0)


_ROWS = 256


def s5_core_fwd(hn, bm, lam, cm, bg=()):
    nt = T // _ROWS

    def body(u_ref, b_ref, lam_ref, c_ref, ys_ref, S):
        lr, li = lam_ref[0], lam_ref[1]
        z = jnp.zeros((8, 512), f32)
        tile = lambda k: pl.ds(k * _ROWS, _ROWS)
        c = (z, z)
        for k in range(nt):
            S[tile(k), :] = _dot(_rows_in(u_ref, k).astype(bf16), b_ref[...])
            if k >= 1:
                c = _scan_tile(S, lr, li, k - 1, c, False, False)
        c = _scan_tile(S, lr, li, nt - 1, c, False, False)
        c = _chunk_starts(c[0], c[1], lr, li, False)
        for k in range(nt):
            c = _scan_tile(S, lr, li, k, c, False, True)
            if k >= 1:
                _rows_out(ys_ref, k - 1, _dot(S[tile(k - 1), :].astype(bf16), c_ref[...]))
        _rows_out(ys_ref, nt - 1, _dot(S[tile(nt - 1), :].astype(bf16), c_ref[...]))

    return _call(
        bg, body, name="s5_core_fwd", grid=(S5_SUB,),
        in_specs=[pl.BlockSpec((T, 128), lambda b: (0, b)),
                  pl.BlockSpec((None, 128, 1024), lambda b: (b, 0, 0)),
                  pl.BlockSpec((None, 4, 8, 512), lambda b: (b, 0, 0, 0)),
                  pl.BlockSpec((None, 1024, 128), lambda b: (b, 0, 0))],
        out_specs=[pl.BlockSpec((T, 128), lambda b: (0, b)), pl.BlockSpec((T, 1024), lambda b: (0, b))],
        out_shape=[SDS((T, D), f32), SDS((T, S5_SUB * 1024), f32)],
        compiler_params=_cp(dimension_semantics=("arbitrary",)),
    )(hn, bm, lam, cm)


_SEG = _ROWS // S5_CH


def _rows_in(ref, k):
    return jnp.concatenate([ref[pl.ds(s, S5_CH, stride=S5_STEPS), :] for s in range(k * _SEG, (k + 1) * _SEG)], axis=0)


def _rows_out(ref, k, val):
    for j, s in enumerate(range(k * _SEG, (k + 1) * _SEG)):
        ref[pl.ds(s, S5_CH, stride=S5_STEPS), :] = val[j * S5_CH:(j + 1) * S5_CH, :]


def _scan_tile(S, lr, li, k, carry, reverse, store, aux=None):
    steps = range(k * _SEG, (k + 1) * _SEG)
    for s in (reversed(steps) if reverse else steps):
        row = pl.ds(s * 8, 8)
        xr, xi = carry[0], carry[1]
        nr = lr * xr - li * xi + S[row, 0:512]
        ni = lr * xi + li * xr + S[row, 512:1024]
        if store:
            S[row, 0:512] = nr
            S[row, 512:1024] = ni
        if aux is not None and s >= 1:
            prow = pl.ds((s - 1) * 8, 8)
            pr, pi_ = aux[prow, 0:512], aux[prow, 512:1024]
            carry = (nr, ni, carry[2] + nr * pr + ni * pi_, carry[3] + ni * pr - nr * pi_)
        elif aux is not None:
            carry = (nr, ni, carry[2], carry[3])
        else:
            carry = (nr, ni)
    return carry


def _chunk_starts(er, ei, lr, li, reverse):
    ar, ai = lr, li
    for _ in range(8):
        ar, ai = _cmul(ar, ai, ar, ai)
    cr, ci = _shift_rows(er, 1, reverse), _shift_rows(ei, 1, reverse)
    for k in (1, 2, 4):
        sr, si = _shift_rows(cr, k, reverse), _shift_rows(ci, k, reverse)
        pr, pi_ = _cmul(ar, ai, sr, si)
        cr, ci = cr + pr, ci + pi_
        ar, ai = _cmul(ar, ai, ar, ai)
    return cr, ci


def s5_core_bwd(hn, dy, xs, bm, lam, cm, bg=()):
    nt = T // _ROWS

    def body(u_ref, dy_ref, S1, b_ref, lam_ref, c_ref, du_ref, db_ref, dct_ref, dlam_ref, S2):
        lcr, lci = lam_ref[2], lam_ref[3]
        z = jnp.zeros((8, 512), f32)
        tile = lambda k: pl.ds(k * _ROWS, _ROWS)

        def dx(k):
            dyb = _rows_in(dy_ref, k).astype(bf16)
            S2[tile(k), :] = _dot_nt(dyb, c_ref[...])
            dct_ref[...] += _dot_tn(dyb, S1[tile(k), :].astype(bf16))

        dct_ref[...] = jnp.zeros_like(dct_ref)
        dx(nt - 1)
        c = (z, z)
        for k in range(nt - 1, -1, -1):
            if k >= 1:
                dx(k - 1)
            c = _scan_tile(S2, lcr, lci, k, c, True, False)

        def dbu(k):
            gb = S2[tile(k), :].astype(bf16)
            db_ref[...] += _dot_tn(_rows_in(u_ref, k).astype(bf16), gb)
            _rows_out(du_ref, k, _dot_nt(gb, b_ref[...]))

        c = _chunk_starts(c[0], c[1], lcr, lci, True) + (z, z)
        db_ref[...] = jnp.zeros_like(db_ref)
        for k in range(nt - 1, -1, -1):
            c = _scan_tile(S2, lcr, lci, k, c, True, True, aux=S1)
            if k + 1 < nt:
                dbu(k + 1)
        dbu(0)
        gr, gi, dr, di = c
        last = pl.ds((S5_STEPS - 1) * 8, 8)
        xr = _shift_rows(S1[last, 0:512], 1, False)
        xi = _shift_rows(S1[last, 512:1024], 1, False)
        dlam_ref[0] = dr + gr * xr + gi * xi
        dlam_ref[1] = di + gi * xr - gr * xi

    return _call(
        bg, body, name="s5_core_bwd", grid=(S5_SUB,),
        in_specs=[pl.BlockSpec((T, 128), lambda b: (0, b)),
                  pl.BlockSpec((T, 128), lambda b: (0, b)),
                  pl.BlockSpec((T, 1024), lambda b: (0, b)),
                  pl.BlockSpec((None, 128, 1024), lambda b: (b, 0, 0)),
                  pl.BlockSpec((None, 4, 8, 512), lambda b: (b, 0, 0, 0)),
                  pl.BlockSpec((None, 1024, 128), lambda b: (b, 0, 0))],
        out_specs=[pl.BlockSpec((T, 128), lambda b: (0, b)),
                   pl.BlockSpec((None, 128, 1024), lambda b: (b, 0, 0)),
                   pl.BlockSpec((None, 128, 1024), lambda b: (b, 0, 0)),
                   pl.BlockSpec((None, 2, 8, 512), lambda b: (b, 0, 0, 0))],
        out_shape=[SDS((T, D), f32), SDS((8, 128, 1024), f32), SDS((8, 128, 1024), f32), SDS((8, 2, 8, 512), f32)],
        scratch_shapes=[pltpu.VMEM((T, 1024), f32)],
        compiler_params=_cp(dimension_semantics=("arbitrary",)),
    )(hn, dy, xs, bm, lam, cm)


TM = 512
NT = T // TM


def _tile(n=D):
    return pl.BlockSpec((TM, n), lambda i: (i, 0))


def s5_pre(xp, g):
    def body(x_ref, g_ref, hn_ref):
        hn_ref[...] = _rms(x_ref[...], g_ref[...])[0]

    return pl.pallas_call(
        body, name="s5_pre", grid=(NT,), in_specs=[_tile(), _full((1, D))], out_specs=_tile(),
        out_shape=SDS((T, D), f32), compiler_params=_cp(dimension_semantics=("arbitrary",)),
    )(xp, g)


def _gelu_grad(y):
    c = math.sqrt(2.0 / math.pi)
    t = jnp.tanh(c * (y + 0.044715 * y * y * y))
    return 0.5 * (1.0 + t) + 0.5 * y * (1.0 - t * t) * c * (1.0 + 3.0 * 0.044715 * y * y)


def s5_post(ys, xp, g, d, wglu, bglu, bg=()):
    def body(ys_ref, x_ref, g_ref, d_ref, w_ref, b_ref, y_ref, z_ref, h_ref):
        x = x_ref[...]
        hn, _ = _rms(x, g_ref[...])
        y = ys_ref[...] + d_ref[...] * hn
        y_ref[...] = y
        yg = jax.nn.gelu(y).astype(bf16)
        for j in range(4):
            cv = slice(j * 256, (j + 1) * 256)
            cg = slice(1024 + j * 256, 1024 + (j + 1) * 256)
            val = _dot(yg, w_ref[j]) + b_ref[:, cv]
            gate = _dot(yg, w_ref[j + 4]) + b_ref[:, cg]
            z_ref[:, cv] = val
            z_ref[:, cg] = gate
            h_ref[:, cv] = x[:, cv] + val * jax.nn.sigmoid(gate)

    return _call(
        bg, body, name="s5_post", grid=(NT,),
        in_specs=[_tile(), _tile(), _full((1, D)), _full((1, D)), _full((8, D, 256)), _full((1, 2 * D))],
        out_specs=[_tile(), _tile(2 * D), _tile()],
        out_shape=[SDS((T, D), f32), SDS((T, 2 * D), f32), SDS((T, D), f32)],
        compiler_params=_cp(dimension_semantics=("arbitrary",)),
    )(ys, xp, g, d, wglu, bglu)


def s5_post_bwd(dh, y, z, wglu, bg=()):
    def body(dh_ref, y_ref, z_ref, w_ref, dy_ref, dw_ref, db_ref, ygs, dzs):
        i = pl.program_id(0)
        rows = pl.ds(pl.multiple_of(i * TM, TM), TM)

        @pl.when(i == 0)
        def _():
            db_ref[...] = jnp.zeros_like(db_ref)

        dh_ = dh_ref[...]
        y = y_ref[...]
        ygs[rows, :] = jax.nn.gelu(y).astype(bf16)
        dyg = jnp.zeros((TM, D), f32)
        for j in range(4):
            cv = slice(j * 256, (j + 1) * 256)
            cg = slice(1024 + j * 256, 1024 + (j + 1) * 256)
            val = z_ref[:, cv]
            sg = jax.nn.sigmoid(z_ref[:, cg])
            dval = dh_[:, cv] * sg
            dgate = dh_[:, cv] * val * sg * (1.0 - sg)
            db_ref[:, cv] += _colsum8(dval)
            db_ref[:, cg] += _colsum8(dgate)
            dvb = dval.astype(bf16)
            dgb = dgate.astype(bf16)
            dzs[rows, cv] = dvb
            dzs[rows, cg] = dgb
            dyg = dyg + _dot_nt(dvb, w_ref[j]) + _dot_nt(dgb, w_ref[j + 4])
        dy_ref[...] = dyg * _gelu_grad(y)

        @pl.when(i == NT - 1)
        def _():
            for half in range(2):
                dw = _dot_tn(ygs[...], dzs[:, half * D:(half + 1) * D])
                for j in range(4):
                    dw_ref[4 * half + j] = dw[:, j * 256:(j + 1) * 256].astype(bf16)

    return _call(
        bg, body, name="s5_post_bwd", grid=(NT,),
        in_specs=[_tile(), _tile(), _tile(2 * D), _full((8, D, 256))],
        out_specs=[_tile(), _full((8, D, 256)), _full((8, 2 * D))],
        out_shape=[SDS((T, D), f32), SDS((8, D, 256), bf16), SDS((8, 2 * D), f32)],
        scratch_shapes=[pltpu.VMEM((T, D), bf16), pltpu.VMEM((T, 2 * D), bf16)],
        compiler_params=_cp(dimension_semantics=("arbitrary",)),
    )(dh, y, z, wglu)


def s5_pre_bwd(xp, g, du, dy, d, dh, bg=()):
    def body(x_ref, g_ref, du_ref, dy_ref, d_ref, dh_ref, dx_ref, dg_ref, dd_ref):
        i = pl.program_id(0)

        @pl.when(i == 0)
        def _():
            dg_ref[...] = jnp.zeros_like(dg_ref)
            dd_ref[...] = jnp.zeros_like(dd_ref)

        x = x_ref[...]
        g = g_ref[...]
        dy = dy_ref[...]
        hn, _ = _rms(x, g)
        dhn = du_ref[...] + d_ref[...] * dy
        dx, dgt = _rms_bwd(x, g, dhn)
        dx_ref[...] = dh_ref[...] + dx
        dg_ref[...] += _colsum8(dgt)
        dd_ref[...] += _colsum8(dy * hn)

    return _call(
        bg, body, name="s5_pre_bwd", grid=(NT,),
        in_specs=[_tile(), _full((1, D)), _tile(), _tile(), _full((1, D)), _tile()],
        out_specs=[_tile(), _full((8, D)), _full((8, D))],
        out_shape=[SDS((T, D), f32), SDS((8, D), f32), SDS((8, D), f32)],
        compiler_params=_cp(dimension_semantics=("arbitrary",)),
    )(xp, g, du, dy, d, dh)


TMF = 1024


def mlp_fwd(h, g, w_in, w_out, layer, bg=()):
    def body(h_ref, g_ref, wi_ref, wo_ref, hm_ref, r_ref, out_ref, acc):
        j = pl.program_id(1)

        @pl.when(j == 0)
        def _():
            hm, _ = _rms(h_ref[...], g_ref[...])
            hm_ref[...] = hm.astype(bf16)
            acc[...] = jnp.zeros_like(acc)

        a = jnp.maximum(_dot(hm_ref[...], wi_ref[...]), 0.0)
        r_ref[...] = a.astype(bf16)
        acc[...] += _dot((a * a).astype(bf16), wo_ref[...])

        @pl.when(j == NDEV - 1)
        def _():
            out_ref[...] = h_ref[...] + acc[...]

    return _call(
        bg, body, name=f"mlp_fwd{layer}", grid=(T // TMF, NDEV),
        in_specs=[pl.BlockSpec((TMF, D), lambda i, j: (i, 0)),
                  pl.BlockSpec((1, D), lambda i, j: (0, 0)),
                  pl.BlockSpec((None, D, D_FF_SHARD), lambda i, j: (j, 0, 0)),
                  pl.BlockSpec((None, D_FF_SHARD, D), lambda i, j: (j, 0, 0))],
        out_specs=[pl.BlockSpec((TMF, D), lambda i, j: (i, 0)), pl.BlockSpec((TMF, D_FF_SHARD), lambda i, j: (i, j)),
                   pl.BlockSpec((TMF, D), lambda i, j: (i, 0))],
        out_shape=[SDS((T, D), bf16), SDS((T, NDEV * D_FF_SHARD), bf16), SDS((T, D), f32)],
        scratch_shapes=[pltpu.VMEM((TMF, D), f32)],
        compiler_params=_cp(dimension_semantics=("arbitrary", "arbitrary")),
    )(h, g, w_in, w_out)


def mlp_bwd(h, hm, r, g, dout, dout_b, w_in, w_out, layer, bg=()):
    def body(h_ref, hm_ref, r_ref, g_ref, do_ref, dob_ref, wi_ref, wo_ref, dh_ref, dwi_ref, dwo_ref, dg_ref,
             dhm, dzs):
        s = pl.program_id(0)

        @pl.when(s == 0)
        def _():
            dhm[...] = jnp.zeros_like(dhm)

        @pl.when(s < NDEV)
        def _():
            for c in range(NT):
                rows = pl.ds(c * TM, TM)
                dz = (_dot_nt(dob_ref[rows, :], wo_ref[...]) * (2.0 * r_ref[rows, :].astype(f32))).astype(bf16)
                dzs[rows, :] = dz
                dhm[rows, :] += _dot_nt(dz, wi_ref[...])
            rb = r_ref[...]
            dwo_ref[...] = _dot_tn(rb * rb, dob_ref[...]).astype(bf16)
            dwi_ref[...] = _dot_tn(hm_ref[...], dzs[...]).astype(bf16)

        @pl.when(s >= NDEV)
        def _():
            @pl.when(s == NDEV)
            def _():
                dg_ref[...] = jnp.zeros_like(dg_ref)
            rows = pl.ds(pl.multiple_of((s - NDEV) * TM, TM), TM)
            dx, dgt = _rms_bwd(h_ref[...], g_ref[...], dhm[rows, :])
            dh_ref[...] = do_ref[...] + dx
            dg_ref[...] += _colsum8(dgt)

    shard = lambda s: (jnp.minimum(s, NDEV - 1), 0, 0)
    tile = lambda s: (jnp.maximum(s - NDEV, 0), 0)
    return _call(
        bg, body, name=f"mlp_bwd{layer}", grid=(NDEV + NT,),
        in_specs=[pl.BlockSpec((TM, D), tile),
                  _full((T, D)),
                  pl.BlockSpec((T, D_FF_SHARD), lambda s: (0, jnp.minimum(s, NDEV - 1))),
                  _full((1, D)),
                  pl.BlockSpec((TM, D), tile),
                  _full((T, D)),
                  pl.BlockSpec((None, D, D_FF_SHARD), shard),
                  pl.BlockSpec((None, D_FF_SHARD, D), shard)],
        out_specs=[pl.BlockSpec((TM, D), tile),
                   pl.BlockSpec((None, D, D_FF_SHARD), shard),
                   pl.BlockSpec((None, D_FF_SHARD, D), shard),
                   pl.BlockSpec((8, D), lambda s: (0, 0))],
        out_shape=[SDS((T, D), f32), SDS((NDEV, D, D_FF_SHARD), bf16), SDS((NDEV, D_FF_SHARD, D), bf16),
                   SDS((8, D), f32)],
        scratch_shapes=[pltpu.VMEM((T, D), f32), pltpu.VMEM((T, D_FF_SHARD), bf16)],
        compiler_params=_cp(dimension_semantics=("arbitrary",)),
    )(h, hm, r, g, dout, dout_b, w_in, w_out)


def _spread4():
    r = lax.broadcasted_iota(jnp.int32, (256, D), 0)
    c = lax.broadcasted_iota(jnp.int32, (256, D), 1)
    return ((c // 256 == r // HEAD_DIM) & (c % HEAD_DIM == r % HEAD_DIM)).astype(bf16)


def attn_pre(h, g_kv, g_mix, wkv, bkv, spread, wq, bq):
    def body(h_ref, gkv_ref, gm_ref, wkv_ref, bkv_ref, sp_ref, wq_ref, bq_ref, kvn_ref, hn_ref, k_ref, v_ref, q_ref):
        h_ = h_ref[...]
        kvn = _rms(h_, gkv_ref[...])[0].astype(bf16)
        hn = _rms(h_, gm_ref[...])[0].astype(bf16)
        kvn_ref[...] = kvn
        hn_ref[...] = hn
        kv = (_dot(kvn, wkv_ref[...]) + bkv_ref[...]).astype(bf16)
        k_ref[...] = _dot(kv[:, :256], sp_ref[...]).astype(bf16)
        v_ref[...] = _dot(kv[:, 256:], sp_ref[...]).astype(bf16)
        q_ref[...] = (_dot(hn, wq_ref[...]) + bq_ref[...]).astype(bf16)

    return pl.pallas_call(
        body, name="attn_pre", grid=(NT,),
        in_specs=[_tile(), _full((1, D)), _full((1, D)), _full((D, 512)), _full((1, 512)), _full((256, D)),
                  _full((D, D)), _full((1, D))],
        out_specs=[_tile()] * 5,
        out_shape=[SDS((T, D), bf16)] * 5,
        compiler_params=_cp(dimension_semantics=("arbitrary",)),
    )(h, g_kv, g_mix, wkv, bkv, spread, wq, bq)


def _attn_specs():
    cur = pl.BlockSpec((TM, 256), lambda j, n: (n, j))
    prev = pl.BlockSpec((BLK, 256), lambda j, n: (jnp.maximum(n * (TM // BLK) - 1, 0), j))
    return cur, prev


def _head_mask(g):
    lane = lax.broadcasted_iota(jnp.int32, (1, 256), 1)
    return (lane >= g * HEAD_DIM) & (lane < (g + 1) * HEAD_DIM)


def _stack_heads(t):
    return jnp.concatenate([jnp.where(_head_mask(g), t, 0) for g in range(Q_PER_KV)], axis=0)


def _unstack_heads(t):
    out = jnp.where(_head_mask(0), t[0:BLK], 0.0)
    for g in range(1, Q_PER_KV):
        out = out + jnp.where(_head_mask(g), t[g * BLK:(g + 1) * BLK], 0.0)
    return out


def _attn_probs(qs, k2, sinks, first):
    rows = Q_PER_KV * BLK
    s = _dot_nt(qs, k2) * (1.0 / math.sqrt(HEAD_DIM))
    qi = jnp.bitwise_and(lax.broadcasted_iota(jnp.int32, (rows, 2 * BLK), 0), BLK - 1)
    kj = lax.broadcasted_iota(jnp.int32, (rows, 2 * BLK), 1)
    diff = qi + BLK - kj
    valid = (diff >= 0) & (diff < BLK) & (jnp.logical_not(first) | (kj >= BLK))
    s = jnp.where(valid, s, -jnp.inf)
    rb = lax.broadcasted_iota(jnp.int32, (rows, 1), 0)
    sink = jnp.where(rb < BLK, sinks[0], jnp.where(rb < 2 * BLK, sinks[1], jnp.where(rb < 3 * BLK, sinks[2], sinks[3])))
    m = jnp.maximum(jnp.max(s, axis=-1, keepdims=True), sink)
    p = jnp.exp(s - m)
    ps = jnp.exp(sink - m)
    denom = jnp.sum(p, axis=-1, keepdims=True) + ps
    return p / denom, ps / denom


def _window_blocks(b, n, kc_ref, kp_ref, vc_ref, vp_ref):
    if b == 0:
        return (jnp.concatenate([kp_ref[...], kc_ref[0:BLK, :]], axis=0),
                jnp.concatenate([vp_ref[...], vc_ref[0:BLK, :]], axis=0), n == 0)
    rows = pl.ds((b - 1) * BLK, 2 * BLK)
    return kc_ref[rows, :], vc_ref[rows, :], False


def attn_core_fwd(q, k4, v4, sinks, bg=()):
    nb = TM // BLK

    def body(sink_ref, q_ref, kc_ref, kp_ref, vc_ref, vp_ref, o_ref, a_ref, as_ref):
        j = pl.program_id(0)
        n = pl.program_id(1)
        sk = [sink_ref[j * Q_PER_KV + g] for g in range(Q_PER_KV)]
        for b in range(nb):
            qb = q_ref[b * BLK:(b + 1) * BLK, :]
            k2, v2, first = _window_blocks(b, n, kc_ref, kp_ref, vc_ref, vp_ref)
            a, asink = _attn_probs(_stack_heads(qb), k2, sk, first)
            ab = a.astype(bf16)
            a_ref[b] = ab
            as_ref[b] = jnp.broadcast_to(asink, (Q_PER_KV * BLK, 128)).astype(bf16)
            o_ref[b * BLK:(b + 1) * BLK, :] = _unstack_heads(_dot(ab, v2)).astype(bf16)

    cur, prev = _attn_specs()
    rows = Q_PER_KV * BLK
    return _call(
        bg, body, name="attn_core_fwd", grid=(N_KV, NT),
        in_specs=[pl.BlockSpec(memory_space=pltpu.SMEM), cur, cur, prev, cur, prev],
        out_specs=[cur, pl.BlockSpec((None, nb, rows, 2 * BLK), lambda j, n: (j, n, 0, 0)),
                   pl.BlockSpec((None, nb, rows, 128), lambda j, n: (j, n, 0, 0))],
        out_shape=[SDS((T, D), bf16), SDS((N_KV, T // BLK, rows, 2 * BLK), bf16), SDS((N_KV, T // BLK, rows, 128), bf16)],
        compiler_params=_cp(dimension_semantics=("arbitrary", "arbitrary")),
    )(sinks, q, k4, k4, v4, v4)


def attn_post(h, o, wo, bo):
    def body(h_ref, o_ref, w_ref, b_ref, out_ref):
        out_ref[...] = h_ref[...] + _dot(o_ref[...], w_ref[...]) + b_ref[...]

    return pl.pallas_call(
        body, name="attn_post", grid=(NT,), in_specs=[_tile(), _tile(), _full((D, D)), _full((1, D))],
        out_specs=_tile(), out_shape=SDS((T, D), f32), compiler_params=_cp(dimension_semantics=("arbitrary",)),
    )(h, o, wo, bo)


def attn_bwd_pre(dh, o, wo, bg=()):
    def body(dh_ref, o_ref, w_ref, do_ref, dw_ref, db_ref, acc):
        i = pl.program_id(0)

        @pl.when(i == 0)
        def _():
            acc[...] = jnp.zeros_like(acc)
            db_ref[...] = jnp.zeros_like(db_ref)

        dh_ = dh_ref[...]
        dhb = dh_.astype(bf16)
        do_ref[...] = _dot_nt(dhb, w_ref[...]).astype(bf16)
        acc[...] += _dot_tn(o_ref[...], dhb)
        db_ref[...] += _colsum8(dh_)

        @pl.when(i == NT - 1)
        def _():
            dw_ref[...] = acc[...].astype(bf16)

    return _call(
        bg, body, name="attn_bwd_pre", grid=(NT,), in_specs=[_tile(), _tile(), _full((D, D))],
        out_specs=[_tile(), _full((D, D)), _full((8, D))],
        out_shape=[SDS((T, D), bf16), SDS((D, D), bf16), SDS((8, D), f32)],
        scratch_shapes=[pltpu.VMEM((D, D), f32)],
        compiler_params=_cp(dimension_semantics=("arbitrary",)),
    )(dh, o, wo)


def attn_core_bwd(q, do, k4, v4, probs, sink_w, bg=()):
    nb = TM // BLK

    def body(q_ref, do_ref, kc_ref, kp_ref, vc_ref, vp_ref, a_ref, as_ref, dq_ref, dk_ref, dv_ref, ds_ref):
        j = pl.program_id(0)
        n = pl.program_id(1)

        @pl.when(n == 0)
        def _():
            dk_ref[...] = jnp.zeros_like(dk_ref)
            dv_ref[...] = jnp.zeros_like(dv_ref)
            ds_ref[...] = jnp.zeros_like(ds_ref)

        lane8 = lax.broadcasted_iota(jnp.int32, (8, 128), 1)
        row8 = lax.broadcasted_iota(jnp.int32, (8, 128), 0)
        for b in range(nb):
            qs = _stack_heads(q_ref[b * BLK:(b + 1) * BLK, :])
            dos = _stack_heads(do_ref[b * BLK:(b + 1) * BLK, :])
            k2, v2, _ = _window_blocks(b, n, kc_ref, kp_ref, vc_ref, vp_ref)
            ab = a_ref[b]
            a = ab.astype(f32)
            asink = as_ref[b][:, 0:1].astype(f32)
            dp = _dot_nt(dos, v2)
            dd = jnp.sum(a * dp, axis=-1, keepdims=True)
            dsc = (a * (dp - dd) * (1.0 / math.sqrt(HEAD_DIM))).astype(bf16)
            t = asink * dd
            for g in range(Q_PER_KV):
                dsink = -jnp.sum(t[g * BLK:(g + 1) * BLK], axis=0, keepdims=True)
                ds_ref[...] += jnp.where((lane8 == g) & (row8 == 0), jnp.broadcast_to(dsink, (8, 128)), 0.0)
            dq_ref[b * BLK:(b + 1) * BLK, :] = _unstack_heads(_dot(dsc, k2))
            dk2 = _dot_tn(dsc, qs)
            dv2 = _dot_tn(ab, dos)
            cur = pl.ds(pl.multiple_of(n * TM + b * BLK, BLK), BLK)
            dk_ref[cur, :] += dk2[BLK:, :]
            dv_ref[cur, :] += dv2[BLK:, :]
            if b == 0:
                @pl.when(n > 0)
                def _():
                    prv = pl.ds(pl.multiple_of(n * TM - BLK, BLK), BLK)
                    dk_ref[prv, :] += dk2[:BLK, :]
                    dv_ref[prv, :] += dv2[:BLK, :]
            else:
                prv = pl.ds(pl.multiple_of(n * TM + (b - 1) * BLK, BLK), BLK)
                dk_ref[prv, :] += dk2[:BLK, :]
                dv_ref[prv, :] += dv2[:BLK, :]

    cur, prev = _attn_specs()
    col = pl.BlockSpec((T, 256), lambda j, n: (0, j))
    rows = Q_PER_KV * BLK
    return _call(
        bg, body, name="attn_core_bwd", grid=(N_KV, NT),
        in_specs=[cur, cur, cur, prev, cur, prev,
                  pl.BlockSpec((None, nb, rows, 2 * BLK), lambda j, n: (j, n, 0, 0)),
                  pl.BlockSpec((None, nb, rows, 128), lambda j, n: (j, n, 0, 0))],
        out_specs=[cur, col, col, pl.BlockSpec((None, 8, 128), lambda j, n: (j, 0, 0))],
        out_shape=[SDS((T, D), f32), SDS((T, D), f32), SDS((T, D), f32), SDS((N_KV, 8, 128), f32)],
        compiler_params=_cp(dimension_semantics=("arbitrary", "arbitrary")),
    )(q, do, k4, k4, v4, v4, probs, sink_w)


def attn_bwd_q(h, dh, dq, hn, g_mix, wq):
    def body(h_ref, dh_ref, dq_ref, hn_ref, gm_ref, wq_ref, out_ref, dwq_ref, dbq_ref, dgm_ref, aq):
        i = pl.program_id(0)

        @pl.when(i == 0)
        def _():
            aq[...] = jnp.zeros_like(aq)
            dbq_ref[...] = jnp.zeros_like(dbq_ref)
            dgm_ref[...] = jnp.zeros_like(dgm_ref)

        dq_ = dq_ref[...]
        dqb = dq_.astype(bf16)
        aq[...] += _dot_tn(hn_ref[...], dqb)
        dbq_ref[...] += _colsum8(dq_)
        dx, dg = _rms_bwd(h_ref[...], gm_ref[...], _dot_nt(dqb, wq_ref[...]))
        out_ref[...] = dh_ref[...] + dx
        dgm_ref[...] += _colsum8(dg)

        @pl.when(i == NT - 1)
        def _():
            dwq_ref[...] = aq[...].astype(bf16)

    vec = _full((8, D))
    mat = _full((D, D))
    return pl.pallas_call(
        body, name="attn_bwd_q", grid=(NT,),
        in_specs=[_tile()] * 4 + [_full((1, D)), mat],
        out_specs=[_tile(), mat, vec, vec],
        out_shape=[SDS((T, D), f32), SDS((D, D), bf16), SDS((8, D), f32), SDS((8, D), f32)],
        scratch_shapes=[pltpu.VMEM((D, D), f32)],
        compiler_params=_cp(dimension_semantics=("arbitrary",)),
    )(h, dh, dq, hn, g_mix, wq)


def attn_bwd_kv(h, dh, dk4, dv4, kvn, g_kv, wkv, spread):
    def body(h_ref, dh_ref, dk_ref, dv_ref, kvn_ref, gkv_ref, wkv_ref, sp_ref, out_ref, outb_ref, dw_ref, db_ref,
             dgkv_ref, acc):
        i = pl.program_id(0)

        @pl.when(i == 0)
        def _():
            for r in (acc, db_ref, dgkv_ref):
                r[...] = jnp.zeros_like(r)

        dkv = jnp.concatenate([_dot_nt(dk_ref[...].astype(bf16), sp_ref[...]),
                               _dot_nt(dv_ref[...].astype(bf16), sp_ref[...])], axis=1)
        dkvb = dkv.astype(bf16)
        acc[...] += _dot_tn(kvn_ref[...], dkvb)
        db_ref[...] += _colsum8(dkv)
        dx, dg = _rms_bwd(h_ref[...], gkv_ref[...], _dot_nt(dkvb, wkv_ref[...]))
        out = dh_ref[...] + dx
        out_ref[...] = out
        outb_ref[...] = out.astype(bf16)
        dgkv_ref[...] += _colsum8(dg)

        @pl.when(i == NT - 1)
        def _():
            dw_ref[...] = acc[...].astype(bf16)

    return pl.pallas_call(
        body, name="attn_bwd_kv", grid=(NT,),
        in_specs=[_tile()] * 5 + [_full((1, D)), _full((D, 512)), _full((256, D))],
        out_specs=[_tile(), _tile(), _full((D, 512)), _full((8, 512)), _full((8, D))],
        out_shape=[SDS((T, D), f32), SDS((T, D), bf16), SDS((D, 512), bf16), SDS((8, 512), f32), SDS((8, D), f32)],
        scratch_shapes=[pltpu.VMEM((D, 512), f32)],
        compiler_params=_cp(dimension_semantics=("arbitrary",)),
    )(h, dh, dk4, dv4, kvn, g_kv, wkv, spread)


def final_loss(h, g, target):
    def body(h_ref, g_ref, t_ref, loss_ref, dh_ref, dhb_ref, dg_ref):
        i = pl.program_id(0)

        @pl.when(i == 0)
        def _():
            loss_ref[...] = jnp.zeros_like(loss_ref)
            dg_ref[...] = jnp.zeros_like(dg_ref)

        h_ = h_ref[...]
        g_ = g_ref[...]
        y, _ = _rms(h_, g_)
        diff = y - t_ref[...]
        per_tok = jnp.mean(diff * diff, axis=-1, keepdims=True)
        tot = 0.5 * jnp.sum(per_tok, axis=0, keepdims=True)
        lane = lax.broadcasted_iota(jnp.int32, (8, 128), 1)
        row = lax.broadcasted_iota(jnp.int32, (8, 128), 0)
        loss_ref[...] += jnp.where((lane == 0) & (row == 0), jnp.broadcast_to(tot, (8, 128)), 0.0)
        dx, dgt = _rms_bwd(h_, g_, diff * (1.0 / D))
        dh_ref[...] = dx
        dhb_ref[...] = dx.astype(bf16)
        dg_ref[...] += _colsum8(dgt)

    return pl.pallas_call(
        body, name="final_loss", grid=(NT,), in_specs=[_tile(), _full((1, D)), _tile()],
        out_specs=[_full((8, 128)), _tile(), _tile(), _full((8, D))],
        out_shape=[SDS((8, 128), f32), SDS((T, D), f32), SDS((T, D), bf16), SDS((8, D), f32)],
        compiler_params=_cp(dimension_semantics=("arbitrary",)),
    )(h, g, target)


def fwd_bwd(x, target, p, shards, opt, pos):
    row = lambda v: v.reshape(1, -1)
    (lam, bm, cm), prep_vjp = jax.vjp(s5_discretize, p["s5_a_re"][0], p["s5_a_im"][0], p["s5_log_dt"][0],
                                      p["s5_b_re"][0], p["s5_b_im"][0], p["s5_c_re"][0], p["s5_c_im"][0])
    bmb, cmb = bm.astype(bf16), cm.astype(bf16)
    lam = jnp.concatenate([lam, lam * jnp.array([1.0, -1.0], f32).reshape(1, 2, 1, 1)], axis=1)
    g_mix0, g_mix1 = row(p["norm_mix"][0]), row(p["norm_mix"][1])
    g_mlp0, g_mlp1 = row(p["norm_mlp"][0]), row(p["norm_mlp"][1])
    g_kv, g_fin = row(p["norm_kv"]), row(p["norm_final"])
    bq, bo = p["b_q"], p["b_o"]
    bkv = row(p["b_kv"])
    spread = _spread4()
    sinks = p["sinks"].reshape(16)

    wglu, gvec = sc_gather([shards["s5_w_glu"], shards["vecs"]], 3, "sc_gather_s5")
    win0, wout0 = sc_gather([shards["w_in0"], shards["w_out0"]], 14, "sc_gather_mlp0")
    wkv, wq, wo = sc_gather([shards["w_kv"], shards["w_q"], shards["w_o"]], 4, "sc_gather_attn")
    win1, wout1 = sc_gather([shards["w_in1"], shards["w_out1"]], 5, "sc_gather_mlp1")
    xp = x
    hn0 = s5_pre(xp, g_mix0)
    ys, xs = s5_core_fwd(hn0, bmb, lam, cmb)
    d_skip = gvec[:, 0, :128].reshape(1, D)
    bglu = gvec[:, 0, 128:].reshape(1, 2 * D)
    y, z, h1 = s5_post(ys, xp, g_mix0, d_skip, wglu, bglu)
    hm0, r0, h2p = mlp_fwd(h1, g_mlp0, win0, wout0, 0)
    wkv, wq, wo = wkv.reshape(D, 512), wq.reshape(D, D), wo.reshape(D, D)
    h2 = h2p
    kvn, hn1, k4, v4, q = attn_pre(h2, g_kv, g_mix1, wkv, bkv, spread, wq, bq)
    o, probs, sink_w = attn_core_fwd(q, k4, v4, sinks)
    h3 = attn_post(h2, o, wo, bo)
    hm1, r1, h4 = mlp_fwd(h3, g_mlp1, win1, wout1, 1)
    loss, dh4, dh4b, dg_fin = final_loss(h4, g_fin, target)

    def pair_sums(names, grads, cid, before):
        r1 = sc_comm(BgPair(grads), cid, "sc_pair_" + names[0])
        parts = [add_pairs(g, r, pos, f"add_pairs_{n}") for n, g, r in zip(names, grads, r1)]
        before, parts = lax.optimization_barrier((before, parts))
        return before, (parts, list(zip(grads, r1)))

    def across_chips(names, sums, cid):
        parts, own = sums
        return [(g, r1, r2) for (g, r1), r2 in zip(own, sc_comm(BgChips(parts), cid, "sc_chips_" + names[0]))]

    dh3, dwin1, dwout1, dg_mlp1 = mlp_bwd(h3, hm1, r1, g_mlp1, dh4, dh4b, win1, wout1, 1)
    do, dwo, dbo = attn_bwd_pre(dh3, o, wo)
    do, parts = pair_sums(["w_in1", "w_out1"], [dwin1, dwout1], 6, do)
    rs_in1, rs_out1 = across_chips(["w_in1", "w_out1"], parts, 7)
    dq, dk4, dv4, dsink = attn_core_bwd(q, do, k4, v4, probs, sink_w)
    dh2, dwq, dbq, dg_mix1 = attn_bwd_q(h2, dh3, dq, hn1, g_mix1, wq)
    dh2, dh2b, dwkv, dbkv, dg_kv = attn_bwd_kv(h2, dh2, dk4, dv4, kvn, g_kv, wkv, spread)
    dh2p, dh2pb = dh2, dh2b
    big = {}
    a_in1 = adam_big(*opt["w_mlp_in"], *rs_in1, pos, "adam_w_mlp_in1", layer=1)
    a_out1 = adam_big(*opt["w_mlp_out"], *rs_out1, pos, "adam_w_mlp_out1", layer=1)
    dh2p, a_in1, a_out1 = lax.optimization_barrier((dh2p, a_in1, a_out1))
    names = ["w_kv", "w_q", "w_o"]
    dh2p, parts = pair_sums(names, [dwkv.reshape(NDEV, 128, 512), dwq.reshape(NDEV, 128, D),
                                    dwo.reshape(NDEV, 128, D)], 8, dh2p)
    rs_attn = across_chips(names, parts, 9)
    dh1, dwin0, dwout0, dg_mlp0 = mlp_bwd(h1, hm0, r0, g_mlp0, dh2p, dh2pb, win0, wout0, 0)
    a_attn = [adam_big(*opt[n], *rs, pos, f"adam_{n}") for n, rs in zip(names, rs_attn)]
    dh1, a_attn = lax.optimization_barrier((dh1, a_attn))
    big.update(zip(names, a_attn))
    dy, dwglu, dbglu = s5_post_bwd(dh1, y, z, wglu)
    dy, parts = pair_sums(["w_in0", "w_out0"], [dwin0, dwout0], 10, dy)
    rs_in0, rs_out0 = across_chips(["w_in0", "w_out0"], parts, 11)
    du, dbm, dcmt, dlam = s5_core_bwd(hn0, dy, xs, bmb, lam, cmb)
    du, parts = pair_sums(["s5_w_glu"], [dwglu], 12, du)
    rs_glu, = across_chips(["s5_w_glu"], parts, 13)
    dxp, dg_mix0, dd = s5_pre_bwd(xp, g_mix0, du, dy, d_skip, dh1)
    big["w_mlp_in"] = adam_big(*opt["w_mlp_in"], *rs_in0, pos, "adam_w_mlp_in0", layer=0, prev=a_in1)
    big["w_mlp_out"] = adam_big(*opt["w_mlp_out"], *rs_out0, pos, "adam_w_mlp_out0", layer=0, prev=a_out1)
    big["s5_w_glu"] = adam_big(*opt["s5_w_glu"], *rs_glu, pos, "adam_s5_w_glu")
    grad_x = dxp
    da_re, da_im, dlog_dt, db_re, db_im, dc_re, dc_im = prep_vjp((dlam, dbm, dcmt.transpose(0, 2, 1)))

    def lanes(v_):
        v_ = v_.reshape(1, -1)
        return jnp.pad(v_, ((0, 0), (0, D - v_.shape[1])))

    small = dict(
        rows8=[dg_mix0, dg_mix1, dg_mlp0, dg_mlp1, dg_kv, dg_fin, dd, dbq, dbo], b_glu=dbglu, b_kv=dbkv,
        misc=jnp.concatenate([lanes(dsink[:, 0, :Q_PER_KV]), lanes(dlog_dt), lanes(loss[0:1, 0:1])], axis=0),
        s5=[da_re.reshape(4, D), da_im.reshape(4, D),
            db_re.transpose(0, 2, 1).reshape(64, D), db_im.transpose(0, 2, 1).reshape(64, D),
            dc_re.reshape(64, D), dc_im.reshape(64, D)])
    small, big["w_mlp_in"], big["w_mlp_out"] = lax.optimization_barrier((small, big["w_mlp_in"], big["w_mlp_out"]))
    return loss, grad_x, small, big


def _row_tile(r, c):
    return min(r, max(8, (512 * 1024) // c))


def add_pairs(g, r1, pos, name):
    _, R, C = g.shape
    tr = _row_tile(R, C)

    def body(pos_ref, g_ref, r_ref, o_ref):
        o_ref[...] = (g_ref[...].astype(f32) + r_ref[...].astype(f32)).astype(bf16)

    other = lambda k, pos: (pos[1] + 1 + k) % 4
    return pl.pallas_call(
        body, name=name, out_shape=SDS((4, R, C), bf16),
        grid_spec=pltpu.PrefetchScalarGridSpec(
            num_scalar_prefetch=1, grid=(3, R // tr),
            in_specs=[pl.BlockSpec((None, tr, C), lambda k, i, pos: (2 * other(k, pos) + pos[0], i, 0)),
                      pl.BlockSpec((None, tr, C), lambda k, i, pos: (other(k, pos), i, 0))],
            out_specs=pl.BlockSpec((None, tr, C), lambda k, i, pos: (other(k, pos), i, 0))),
        compiler_params=_cp(dimension_semantics=("arbitrary", "arbitrary")),
    )(pos, g, r1)


def _adamw(w, g, m, v):
    m = ADAM_B1 * m + (1.0 - ADAM_B1) * g
    v = ADAM_B2 * v + (1.0 - ADAM_B2) * (g * g)
    m_hat = m / (1.0 - ADAM_B1 ** ADAM_STEP)
    v_hat = v / (1.0 - ADAM_B2 ** ADAM_STEP)
    delta = -ADAM_LR * (m_hat / (jnp.sqrt(v_hat) + ADAM_EPS) + ADAM_WD * w)
    return delta, m, v


def adam_big(w, m, v, g, r1, r2, pos, name, layer=0, prev=None):
    L, R, C = w.shape
    tr = max(R // 4, 128)

    def body(pos_ref, w_ref, m_ref, v_ref, p_ref, q_ref, r_ref, *rest):
        g_out, d_out, m_out, v_out = rest[-4:]
        g = (p_ref[...].astype(f32) + q_ref[...].astype(f32) + r_ref[0].astype(f32) + r_ref[1].astype(f32)
             + r_ref[2].astype(f32))
        d, m_, v_ = _adamw(w_ref[...], g, m_ref[...], v_ref[...])
        g_out[...] = g
        d_out[...] = d
        m_out[...] = m_
        v_out[...] = v_

    blk = pl.BlockSpec((None, tr, C), lambda i, pos: (layer, i, 0))
    extra = [] if prev is None else list(prev)
    return pl.pallas_call(
        body, name=name, out_shape=[SDS((L, R, C), f32)] * 4,
        grid_spec=pltpu.PrefetchScalarGridSpec(
            num_scalar_prefetch=1, grid=(R // tr,),
            in_specs=[blk, blk, blk,
                      pl.BlockSpec((None, tr, C), lambda i, pos: (2 * pos[1] + pos[0], i, 0)),
                      pl.BlockSpec((None, tr, C), lambda i, pos: (pos[1], i, 0)),
                      pl.BlockSpec((3, tr, C), lambda i, pos: (0, i, 0))] + [_ANY] * len(extra),
            out_specs=[blk] * 4),
        input_output_aliases={7 + k: k for k in range(len(extra))},
        compiler_params=_cp(dimension_semantics=("arbitrary",)),
    )(pos, w, m, v, g, r1, r2, *extra)


SMALL_BUF_ROWS = 288


def allreduce_small(rows8, b_glu, b_kv, misc, s5):
    R = SMALL_BUF_ROWS
    half, quarter = R // 2, R // 4
    pieces = [*rows8, b_glu, b_kv, misc, *s5]

    def body(*refs):
        ins, (out_ref, in_ref, acc1, acc2, r0, r1, r2, send_sems, recv_sems) = refs[:len(pieces)], refs[len(pieces):]
        in_ref[8:16, :] = jnp.zeros((8, D), f32)
        in_ref[R - 8:R, :] = jnp.zeros((8, D), f32)
        for k in range(len(rows8)):
            in_ref[k:k + 1, :] = ins[k][0:1, :]
        glu_ref, kv_ref, misc_ref = ins[len(rows8):len(rows8) + 3]
        in_ref[9:10, :] = glu_ref[0:1, 0:D]
        in_ref[10:11, :] = glu_ref[0:1, D:2 * D]
        in_ref[11:12, 0:kv_ref.shape[1]] = kv_ref[0:1, :]
        in_ref[12:15, :] = misc_ref[...]
        row = 16
        for a in ins[len(rows8) + 3:]:
            in_ref[row:row + a.shape[0], :] = a[...]
            row += a.shape[0]
        x, y, c = _pos()
        sibling, over_x, over_y = (x, y, 1 - c), (1 - x, y, c), (x, 1 - y, c)
        first = pl.multiple_of(c * half, 8)
        mine = pl.ds(first, half)
        theirs = pl.ds(pl.multiple_of((1 - c) * half, 8), half)
        qa = pl.ds(first, quarter)
        qb = pl.ds(pl.multiple_of(first + quarter, 8), quarter)

        def exchange(copies):
            cps = [pltpu.make_async_remote_copy(
                src_ref=src.at[rows], dst_ref=dst.at[rows], send_sem=send_sems.at[k], recv_sem=recv_sems.at[k],
                device_id=peer, device_id_type=MESH) for k, src, dst, rows, peer in copies]
            for cp in cps:
                cp.start()
            for cp in cps:
                cp.wait()

        exchange([(0, in_ref, r0, theirs, sibling)])
        acc1[mine, :] = in_ref[mine, :] + r0[mine, :]
        exchange([(1, acc1, r1, qa, over_x), (2, acc1, r1, qb, over_y)])
        acc2[mine, :] = acc1[mine, :] + r1[mine, :]
        exchange([(3, acc2, r2, qa, over_y), (4, acc2, r2, qb, over_x)])
        out_ref[mine, :] = acc2[mine, :] + r2[mine, :]
        exchange([(5, out_ref, out_ref, mine, sibling)])

    vm = pl.BlockSpec(memory_space=pltpu.VMEM)
    return pl.pallas_call(
        body, name="allreduce_small", in_specs=[vm] * len(pieces), out_specs=vm, out_shape=SDS((R, D), f32),
        scratch_shapes=[pltpu.VMEM((R, D), f32)] * 6 + [pltpu.SemaphoreType.DMA((6,)), pltpu.SemaphoreType.DMA((6,))],
    )(*pieces)


SMALL_ROWS = {'norm_mix': (0, 2, D), 'norm_mlp': (2, 2, D), 'norm_kv': (4, 1, D), 'norm_final': (5, 1, D),
              's5_d': (6, 1, D), 'b_q': (7, 1, D), 'b_o': (8, 1, D), 's5_b_glu': (9, 2, D), 'b_kv': (11, 1, 512),
              'sinks': (12, 1, 16), 's5_log_dt': (13, 1, 64), 's5_a_re': (16, 4, D), 's5_a_im': (20, 4, D),
              's5_b_re': (24, 64, D), 's5_b_im': (88, 64, D), 's5_c_re': (152, 64, D), 's5_c_im': (216, 64, D)}
LOSS_ROW = 14
ROW_PARAMS = ['norm_mix', 'norm_mlp', 'norm_kv', 'norm_final', 'b_q', 'b_o', 'b_kv', 'sinks', 's5_log_dt']
SHARD_PARAMS = ['s5_d', 's5_b_glu']
S5_PARAMS = ['s5_a_re', 's5_a_im', 's5_b_re', 's5_b_im', 's5_c_re', 's5_c_im']


def adam_small(dev, gsum, s5_grads, w, m, v):
    names = ROW_PARAMS + SHARD_PARAMS + S5_PARAMS
    n_g = len(ROW_PARAMS) + len(SHARD_PARAMS)

    def body(dev_ref, gs_ref, *refs):
        pos = [0]

        def take(k):
            r = refs[pos[0]:pos[0] + k]
            pos[0] += k
            return r

        g5 = take(len(S5_PARAMS))
        wr, mr, vr = take(len(names)), take(len(names)), take(len(names))
        g_out = take(n_g)
        d_out, m_out, v_out = take(len(names)), take(len(names)), take(len(names))
        dv = dev_ref[0]
        for i, n in enumerate(names):
            if n in S5_PARAMS:
                g = g5[S5_PARAMS.index(n)][...]
            elif n in SHARD_PARAMS:
                r0, _, _ = SMALL_ROWS[n]
                ln = wr[i].shape[1]
                g = jnp.zeros((1, ln), f32)
                for k in range(NDEV):
                    off = k * ln
                    piece = gs_ref[r0 + off // D:r0 + off // D + 1, off % D:off % D + ln]
                    g = g + jnp.where(dv == k, piece, 0.0)
                g_out[i][...] = g
            else:
                r0, nr, nl = SMALL_ROWS[n]
                g = gs_ref[r0:r0 + nr, 0:nl]
                g_out[i][...] = g
            d, m_, v_ = _adamw(wr[i][...], g, mr[i][...], vr[i][...])
            d_out[i][...] = d
            m_out[i][...] = m_
            v_out[i][...] = v_

    vm = pl.BlockSpec(memory_space=pltpu.VMEM)
    ins = [s5_grads[n] for n in S5_PARAMS] + [d[n] for d in (w, m, v) for n in names]
    shapes = [SDS(w[n].shape, f32) for n in names]
    res = pl.pallas_call(
        body, name="adam_small", in_specs=[pl.BlockSpec(memory_space=pltpu.SMEM)] + [vm] * (1 + len(ins)),
        out_specs=[vm] * (n_g + 3 * len(names)), out_shape=shapes[:n_g] + shapes * 3,
        compiler_params=_cp(),
    )(dev, gsum, *ins)
    g_o = dict(zip(names[:n_g], res[:n_g]))
    rest = res[n_g:]
    k = len(names)
    return g_o, dict(zip(names, rest[:k])), dict(zip(names, rest[k:2 * k])), dict(zip(names, rest[2 * k:]))


WEIGHTS = ['norm_mix', 'norm_mlp', 'norm_kv', 'norm_final', 's5_a_re', 's5_a_im', 's5_log_dt', 's5_b_re', 's5_b_im',
           's5_c_re', 's5_c_im', 's5_d', 's5_w_glu', 's5_b_glu', 'w_kv', 'b_kv', 'w_q', 'b_q', 'sinks', 'w_o', 'b_o',
           'w_mlp_in', 'w_mlp_out']
BIG = ['s5_w_glu', 'w_kv', 'w_q', 'w_o', 'w_mlp_in', 'w_mlp_out']
BIG_2D = {'s5_w_glu': (D, 256), 'w_kv': (128, 512), 'w_q': (128, D), 'w_o': (128, D), 'w_mlp_in': (2 * D, 512),
          'w_mlp_out': (2 * 512, D)}
SMALL = [n for n in WEIGHTS if n not in BIG]


def kernel(x, norm_mix, norm_mlp, norm_kv, norm_final, s5_a_re, s5_a_im, s5_log_dt, s5_b_re, s5_b_im, s5_c_re, s5_c_im, s5_d, s5_w_glu, s5_b_glu, w_kv, b_kv, w_q, b_q, sinks, w_o, b_o, w_mlp_in, w_mlp_out, loss_target, m_norm_mix, m_norm_mlp, m_norm_kv, m_norm_final, m_s5_a_re, m_s5_a_im, m_s5_log_dt, m_s5_b_re, m_s5_b_im, m_s5_c_re, m_s5_c_im, m_s5_d, m_s5_w_glu, m_s5_b_glu, m_w_kv, m_b_kv, m_w_q, m_b_q, m_sinks, m_w_o, m_b_o, m_w_mlp_in, m_w_mlp_out, v_norm_mix, v_norm_mlp, v_norm_kv, v_norm_final, v_s5_a_re, v_s5_a_im, v_s5_log_dt, v_s5_b_re, v_s5_b_im, v_s5_c_re, v_s5_c_im, v_s5_d, v_s5_w_glu, v_s5_b_glu, v_w_kv, v_b_kv, v_w_q, v_b_q, v_sinks, v_w_o, v_b_o, v_w_mlp_in, v_w_mlp_out):
    w = dict(norm_mix=norm_mix, norm_mlp=norm_mlp, norm_kv=norm_kv, norm_final=norm_final, s5_a_re=s5_a_re,
             s5_a_im=s5_a_im, s5_log_dt=s5_log_dt, s5_b_re=s5_b_re, s5_b_im=s5_b_im, s5_c_re=s5_c_re, s5_c_im=s5_c_im,
             s5_d=s5_d, s5_w_glu=s5_w_glu, s5_b_glu=s5_b_glu, w_kv=w_kv, b_kv=b_kv, w_q=w_q, b_q=b_q, sinks=sinks,
             w_o=w_o, b_o=b_o, w_mlp_in=w_mlp_in, w_mlp_out=w_mlp_out)
    m = dict(norm_mix=m_norm_mix, norm_mlp=m_norm_mlp, norm_kv=m_norm_kv, norm_final=m_norm_final, s5_a_re=m_s5_a_re,
             s5_a_im=m_s5_a_im, s5_log_dt=m_s5_log_dt, s5_b_re=m_s5_b_re, s5_b_im=m_s5_b_im, s5_c_re=m_s5_c_re,
             s5_c_im=m_s5_c_im, s5_d=m_s5_d, s5_w_glu=m_s5_w_glu, s5_b_glu=m_s5_b_glu, w_kv=m_w_kv, b_kv=m_b_kv,
             w_q=m_w_q, b_q=m_b_q, sinks=m_sinks, w_o=m_w_o, b_o=m_b_o, w_mlp_in=m_w_mlp_in, w_mlp_out=m_w_mlp_out)
    v = dict(norm_mix=v_norm_mix, norm_mlp=v_norm_mlp, norm_kv=v_norm_kv, norm_final=v_norm_final, s5_a_re=v_s5_a_re,
             s5_a_im=v_s5_a_im, s5_log_dt=v_s5_log_dt, s5_b_re=v_s5_b_re, s5_b_im=v_s5_b_im, s5_c_re=v_s5_c_re,
             s5_c_im=v_s5_c_im, s5_d=v_s5_d, s5_w_glu=v_s5_w_glu, s5_b_glu=v_s5_b_glu, w_kv=v_w_kv, b_kv=v_b_kv,
             w_q=v_w_q, b_q=v_b_q, sinks=v_sinks, w_o=v_w_o, b_o=v_b_o, w_mlp_in=v_w_mlp_in, w_mlp_out=v_w_mlp_out)
    xi, yi, ci = _pos()
    dev = 4 * xi + 2 * yi + ci
    pos = jnp.stack([ci, 2 * xi + yi]).astype(jnp.int32)

    shards = {
        "s5_w_glu": s5_w_glu[0].astype(bf16), "w_kv": w_kv.astype(bf16), "w_q": w_q[0].astype(bf16),
        "w_o": w_o[0].astype(bf16), "w_in0": w_mlp_in[0].astype(bf16), "w_in1": w_mlp_in[1].astype(bf16),
        "w_out0": w_mlp_out[0].astype(bf16), "w_out1": w_mlp_out[1].astype(bf16),
        "vecs": jnp.broadcast_to(jnp.concatenate([s5_d, s5_b_glu], axis=1), (8, 384)),
    }
    as3d = lambda a, n: a if a.ndim == 3 and a.shape[0] == 2 else a.reshape((1,) + BIG_2D[n])
    opt = {n: (as3d(w[n], n), as3d(m[n], n), as3d(v[n], n)) for n in BIG}
    _, grad_x, grads, big = fwd_bwd(x[0], loss_target[0], {n: w[n] for n in SMALL}, shards, opt, pos)

    gsum = allreduce_small(**grads)

    out_g, out_d, out_m, out_v = {}, {}, {}, {}
    for n in BIG:
        out_g[n], out_d[n], out_m[n], out_v[n] = [r.reshape(w[n].shape) for r in big[n]]

    loss = gsum[LOSS_ROW, 0]
    swapped = ("s5_b_re", "s5_b_im")
    swap = lambda a: a.transpose(0, 1, 3, 2)

    def kernel_side(d):
        d = {n: (d[n].reshape(1, -1) if d[n].ndim == 1 else d[n]) for n in SMALL}
        d.update({n: swap(d[n]) for n in swapped})
        return d

    s5_g = {}
    for n in S5_PARAMS:
        r0, nr, _ = SMALL_ROWS[n]
        s5_g[n] = gsum[r0:r0 + nr].reshape((1, 64, 16, 64) if n in swapped else w[n].shape)
        out_g[n] = s5_g[n]
    g_s, d_s, m_s, v_s = adam_small(dev.reshape(1).astype(jnp.int32), gsum, s5_g, kernel_side(w), kernel_side(m),
                                    kernel_side(v))
    for src, dst in ((g_s, out_g), (d_s, out_d), (m_s, out_m), (v_s, out_v)):
        dst.update(src)
    for dst in (out_g, out_d, out_m, out_v):
        for n in SMALL:
            dst[n] = (swap(dst[n]) if n in swapped else dst[n]).reshape(w[n].shape)

    return (loss, grad_x[None], *[out_g[n] for n in WEIGHTS], *[out_d[n] for n in WEIGHTS],
            *[out_m[n] for n in WEIGHTS], *[out_v[n] for n in WEIGHTS])
```

```python
import functools
import math

import jax
import jax.numpy as jnp
from jax import lax
from jax.experimental import pallas as pl
from jax.experimental.pallas import tpu as pltpu
from jax.experimental.pallas import tpu_sc as plsc

f32 = jnp.float32
bf16 = jnp.bfloat16
SDS = jax.ShapeDtypeStruct

T = 2048
D = 1024
NDEV = 8
NORM_EPS = 1e-5
S5_G, S5_C, S5_P = 64, 16, 64
S5_SUB = 8
S5_CH = 8
S5_STEPS = T // S5_CH
DT_MIN_LAMBDA = -1e-4
HEAD_DIM = 64
N_KV = 4
Q_PER_KV = 4
BLK = 128
D_FF_SHARD = 512
ADAM_LR, ADAM_B1, ADAM_B2, ADAM_EPS, ADAM_WD, ADAM_STEP = 0.001, 0.9, 0.999, 1e-08, 0.01, 10
VMEM_LIMIT = 56 * 1024 * 1024
MESH = pl.DeviceIdType.MESH


def _cp(**kw):
    return pltpu.CompilerParams(vmem_limit_bytes=VMEM_LIMIT, **kw)


def _dot(a, b):
    return jnp.dot(a, b, preferred_element_type=f32)


def _dot_nt(a, b):
    return lax.dot_general(a, b, (((1,), (1,)), ((), ())), preferred_element_type=f32)


def _dot_tn(a, b):
    return lax.dot_general(a, b, (((0,), (0,)), ((), ())), preferred_element_type=f32)


def _rms(x, g):
    r = lax.rsqrt(jnp.mean(x * x, axis=-1, keepdims=True) + NORM_EPS)
    return x * r * g, r


def _rms_bwd(x, g, dy):
    r = lax.rsqrt(jnp.mean(x * x, axis=-1, keepdims=True) + NORM_EPS)
    u = dy * g
    dx = r * u - (r * r * r) * x * jnp.mean(u * x, axis=-1, keepdims=True)
    return dx, dy * x * r


def _colsum8(v):
    s = jnp.sum(v, axis=0, keepdims=True)
    row = lax.broadcasted_iota(jnp.int32, (8, v.shape[1]), 0)
    return jnp.where(row == 0, jnp.broadcast_to(s, (8, v.shape[1])), 0.0)


def _full(shape):
    nd = len(shape)
    return pl.BlockSpec(shape, lambda *_: (0,) * nd, pipeline_mode=pl.Buffered(1))


_ANY = pl.BlockSpec(memory_space=pl.ANY)


def _pos():
    return lax.axis_index("x"), lax.axis_index("y"), lax.axis_index("c")


def _other_chips(x, y):
    return [(1 - x, y), (x, 1 - y), (1 - x, 1 - y)]


class BgGather:
    SIB, XN, YN, FWD_Y, FWD_X, SIB_X, SIB_Y, SIB_D = range(8)

    def __init__(self, arrs, mids=(0.5, 0.75)):
        n = len(arrs)
        self.arrs = list(arrs)
        self.out_shape = [SDS((NDEV,) + a.shape, a.dtype) for a in arrs]
        self.scratch = [pltpu.SemaphoreType.DMA((n, 8)), pltpu.SemaphoreType.DMA((n, 8)),
                        pltpu.SemaphoreType.DMA((n,))]
        self.mids = mids
        self.result = None

    @staticmethod
    def peers(x, y, c):
        return [(x, y, 1 - c), (1 - x, y, c), (x, 1 - y, c)]

    def mid_steps(self, nsteps):
        at = lambda f: min(nsteps - 1, max(0, int(f * nsteps) - 1))
        return [(at(self.mids[0]), self.mid), (max(at(self.mids[0]), at(self.mids[1])), self.mid2)]

    def _halves(self, a):
        rows = self.arrs[a].shape[0]
        cut = (rows // 32) * 16 if rows >= 32 else rows
        return (0, cut), (cut, rows - cut)

    def _copy(self, ins, outs, sems, a, k, block, to, own=False, part=None):
        slot = 4 * block[0] + 2 * block[1] + block[2]
        rows = pl.ds(0, self.arrs[a].shape[0]) if part is None else pl.ds(*self._halves(a)[part])
        dst = outs[a].at[slot, rows]
        return pltpu.make_async_remote_copy(
            src_ref=ins[a].at[rows] if own else dst, dst_ref=dst, send_sem=sems[0].at[a, k],
            recv_sem=sems[1].at[a, k], device_id=to, device_id_type=MESH)

    def _mine(self, ins, outs, sems):
        x, y, c = _pos()
        return [pltpu.make_async_copy(ins[a], outs[a].at[4 * x + 2 * y + c], sems[2].at[a])
                for a in range(len(self.arrs))]

    def _split(self, a):
        return self._halves(a)[1][1] > 0

    def _sends(self, ins, outs, sems, phase):
        x, y, c = _pos()
        me, sib, xn, yn, dg = (x, y, c), (x, y, 1 - c), (1 - x, y, c), (x, 1 - y, c), (1 - x, 1 - y, c)
        cps = []
        for a in range(len(self.arrs)):
            cp = lambda k, block, to, **kw: self._copy(ins, outs, sems, a, k, block, to, **kw)
            if phase == 0:
                cps += [cp(self.SIB, me, sib, own=True), cp(self.XN, me, xn, own=True), cp(self.YN, me, yn, own=True)]
            elif phase == 1:
                cps.append(cp(self.FWD_Y, xn, yn, part=0))
                if self._split(a):
                    cps.append(cp(self.FWD_X, yn, xn, part=1))
                cps += [cp(self.SIB_X, xn, sib), cp(self.SIB_Y, yn, sib)]
            else:
                cps.append(cp(self.SIB_D, dg, sib))
        return cps

    def _arrivals(self, ins, outs, sems, phase):
        x, y, c = _pos()
        me, xn, yn, dg = (x, y, c), (1 - x, y, c), (x, 1 - y, c), (1 - x, 1 - y, c)
        cps = []
        for a in range(len(self.arrs)):
            cp = lambda k, block, **kw: self._copy(ins, outs, sems, a, k, block, me, **kw)
            if phase == 1:
                cps += [cp(self.XN, xn), cp(self.YN, yn)]
            elif phase == 2:
                cps.append(cp(self.FWD_Y, dg, part=0))
                if self._split(a):
                    cps.append(cp(self.FWD_X, dg, part=1))
            else:
                cps += [cp(self.SIB, (x, y, 1 - c)), cp(self.SIB_X, (1 - x, y, 1 - c)),
                        cp(self.SIB_Y, (x, 1 - y, 1 - c)), cp(self.SIB_D, (1 - x, 1 - y, 1 - c))]
        return cps

    def start(self, ins, outs, sems):
        for cp in self._mine(ins, outs, sems) + self._sends(ins, outs, sems, 0):
            cp.start()

    def mid(self, ins, outs, sems):
        for cp in self._arrivals(ins, outs, sems, 1):
            cp.wait_recv()
        for cp in self._sends(ins, outs, sems, 1):
            cp.start()

    def mid2(self, ins, outs, sems):
        for cp in self._arrivals(ins, outs, sems, 2):
            cp.wait_recv()
        for cp in self._sends(ins, outs, sems, 2):
            cp.start()

    def finish(self, ins, outs, sems):
        for cp in self._arrivals(ins, outs, sems, 3):
            cp.wait_recv()
        for ph in range(3):
            for cp in self._sends(ins, outs, sems, ph):
                cp.wait_send()
        for cp in self._mine(ins, outs, sems):
            cp.wait()


def sc_comm(g, collective_id, name):
    srcs = [jax.new_ref(a, memory_space=pltpu.MemorySpace.HBM) for a in g.arrs]
    dsts = [jax.empty_ref(s, memory_space=pltpu.MemorySpace.HBM) for s in g.out_shape]

    @pl.kernel(mesh=plsc.ScalarSubcoreMesh(axis_name="sequencer", num_cores=1), name=name,
               scratch_types=tuple(g.scratch), compiler_params=pltpu.CompilerParams(collective_id=collective_id))
    def launch(*sems):
        peers = g.peers(*_pos())
        barrier = pltpu.get_barrier_semaphore()
        for peer in peers:
            pl.semaphore_signal(barrier, inc=1, device_id=peer, device_id_type=MESH)
        pl.semaphore_wait(barrier, len(peers))
        g.start(srcs, dsts, sems)
        for _, phase in g.mid_steps(1):
            phase(srcs, dsts, sems)
        g.finish(srcs, dsts, sems)

    launch()
    return [d[...] for d in dsts]


def sc_gather(arrs, collective_id, name):
    return sc_comm(BgGather(arrs), collective_id, name)


class BgPair:
    def __init__(self, arrs):
        n = len(arrs)
        self.arrs = list(arrs)
        self.out_shape = [SDS((4,) + a.shape[1:], a.dtype) for a in arrs]
        self.scratch = [pltpu.SemaphoreType.DMA((n, 4)), pltpu.SemaphoreType.DMA((n, 4))]
        self.result = None

    @staticmethod
    def peers(x, y, c):
        return [(x, y, 1 - c)]

    def mid_steps(self, nsteps):
        return []

    def _copies(self, ins, outs, sems):
        x, y, c = _pos()
        return [pltpu.make_async_remote_copy(
            src_ref=ins[a].at[2 * k + 1 - c], dst_ref=outs[a].at[k], send_sem=sems[0].at[a, k],
            recv_sem=sems[1].at[a, k], device_id=(x, y, 1 - c), device_id_type=MESH)
            for a in range(len(self.arrs)) for k in range(4)]

    def start(self, ins, outs, sems):
        for cp in self._copies(ins, outs, sems):
            cp.start()

    def finish(self, ins, outs, sems):
        cps = self._copies(ins, outs, sems)
        for cp in cps:
            cp.wait_recv()
        for cp in cps:
            cp.wait_send()


class BgChips(BgPair):
    def __init__(self, arrs):
        n = len(arrs)
        self.arrs = list(arrs)
        self.out_shape = [SDS((3,) + a.shape[1:], a.dtype) for a in arrs]
        self.scratch = [pltpu.SemaphoreType.DMA((n, 3)), pltpu.SemaphoreType.DMA((n, 3))]
        self.result = None

    @staticmethod
    def peers(x, y, c):
        return [(px, py, c) for px, py in _other_chips(x, y)]

    def _copies(self, ins, outs, sems):
        x, y, c = _pos()
        return [pltpu.make_async_remote_copy(
            src_ref=ins[a].at[2 * px + py], dst_ref=outs[a].at[r], send_sem=sems[0].at[a, r],
            recv_sem=sems[1].at[a, r], device_id=(px, py, c), device_id_type=MESH)
            for a in range(len(self.arrs)) for r, (px, py) in enumerate(_other_chips(x, y))]


def _call(bgs, body, *, name, grid, in_specs, out_specs, out_shape, scratch_shapes=(), compiler_params=None):
    single = not isinstance(out_shape, (list, tuple))
    out_specs_l = [out_specs] if single else list(out_specs)
    out_shape_l = [out_shape] if single else list(out_shape)
    bgs = [b for b in (bgs or []) if b is not None]
    n_in, n_out, n_sc = len(in_specs), len(out_shape_l), len(scratch_shapes)
    nsteps = math.prod(grid)
    b_in_specs = [b.in_specs(grid) if hasattr(b, "in_specs") else [_ANY] * len(b.arrs) for b in bgs]
    b_out_specs = [b.out_specs(grid) if hasattr(b, "out_specs") else [_ANY] * len(b.out_shape) for b in bgs]
    aliases, i_off, o_off = {}, n_in, n_out
    for b in bgs:
        aliases.update({i_off + i: o_off + o for i, o in getattr(b, "aliases", {}).items()})
        i_off, o_off = i_off + len(b.arrs), o_off + len(b.out_shape)

    def full(*refs):
        pos = [0]

        def take(k):
            r = refs[pos[0]:pos[0] + k]
            pos[0] += k
            return r

        ins = take(n_in)
        b_ins = [take(len(b.arrs)) for b in bgs]
        outs = take(n_out)
        b_outs = [take(len(b.out_shape)) for b in bgs]
        sc = take(n_sc)
        b_sc = [take(len(b.scratch)) for b in bgs]
        if bgs:
            step = pl.program_id(0)
            for d in range(1, len(grid)):
                step = step * grid[d] + pl.program_id(d)

            @pl.when(step == 0)
            def _():
                for b, i_, o_, s_ in zip(bgs, b_ins, b_outs, b_sc):
                    b.start(i_, o_, s_)

        body(*ins, *outs, *sc)
        if bgs:
            for b, i_, o_, s_ in zip(bgs, b_ins, b_outs, b_sc):
                if hasattr(b, "step"):
                    b.step(i_, o_, s_)
                for at, fn in b.mid_steps(nsteps):
                    @pl.when(step == at)
                    def _():
                        fn(i_, o_, s_)

            @pl.when(step == nsteps - 1)
            def _():
                for b, i_, o_, s_ in zip(bgs, b_ins, b_outs, b_sc):
                    b.finish(i_, o_, s_)

    def run(*args):
        res = pl.pallas_call(
            full, name=name, grid=grid,
            in_specs=list(in_specs) + [s for l in b_in_specs for s in l],
            out_specs=out_specs_l + [s for l in b_out_specs for s in l],
            out_shape=out_shape_l + [s for b in bgs for s in b.out_shape],
            scratch_shapes=list(scratch_shapes) + [s for b in bgs for s in b.scratch],
            input_output_aliases=aliases,
            compiler_params=compiler_params,
        )(*args, *[a for b in bgs for a in b.arrs])
        rest = list(res[n_out:])
        for b in bgs:
            b.result, rest = rest[:len(b.out_shape)], rest[len(b.out_shape):]
        return res[0] if single else list(res[:n_out])

    return run


def s5_discretize(a_re, a_im, log_dt, b_re, b_im, c_re, c_im):
    lam_r = jnp.minimum(a_re, DT_MIN_LAMBDA)
    lam_i = a_im
    dt = jnp.exp(log_dt)[:, None]
    e = jnp.exp(lam_r * dt)
    lbr = e * jnp.cos(lam_i * dt)
    lbi = e * jnp.sin(lam_i * dt)
    den = lam_r * lam_r + lam_i * lam_i
    cf_r = ((lbr - 1.0) * lam_r + lbi * lam_i) / den
    cf_i = (lbi * lam_r - (lbr - 1.0) * lam_i) / den
    bb_r = cf_r[:, :, None] * b_re - cf_i[:, :, None] * b_im
    bb_i = cf_r[:, :, None] * b_im + cf_i[:, :, None] * b_re
    eye = jnp.eye(8, dtype=f32)

    def blk_b(m):
        return jnp.einsum('bgpc,gh->bgchp', m.reshape(8, 8, S5_P, S5_C), eye).reshape(8, 128, 512)

    def blk_c(m):
        return jnp.einsum('bgcp,gh->bgphc', m.reshape(8, 8, S5_C, S5_P), eye).reshape(8, 512, 128)

    bm = jnp.concatenate([blk_b(bb_r), blk_b(bb_i)], axis=-1)
    cm = jnp.concatenate([blk_c(c_re), -blk_c(c_im)], axis=1)
    lam = jnp.stack([lbr.reshape(8, 512), lbi.reshape(8, 512)], axis=1)
    lam = jnp.broadcast_to(lam[:, :, None, :], (8, 2, 8, 512))
    return lam, bm, cm


def _cmul(ar, ai, br, bi):
    return ar * br - ai * bi, ar * bi + ai * br


def _shift_rows(v, k, up):
    row = lax.broadcasted_iota(jnp.int32, v.shape, 0)
    if up:
        return jnp.where(row < 8 - k, pltpu.roll(v, 8 - k, 0), 0.0)
    return jnp.where(row >= k, pltpu.roll(v, k, 0), 0.0)


_ROWS = 256


def s5_core_fwd(hn, bm, lam, cm, bg=()):
    nt = T // _ROWS

    def body(u_ref, b_ref, lam_ref, c_ref, ys_ref, S):
        lr, li = lam_ref[0], lam_ref[1]
        z = jnp.zeros((8, 512), f32)
        tile = lambda k: pl.ds(k * _ROWS, _ROWS)
        c = (z, z)
        for k in range(nt):
            S[tile(k), :] = _dot(_rows_in(u_ref, k).astype(bf16), b_ref[...])
            if k >= 1:
                c = _scan_tile(S, lr, li, k - 1, c, False, False)
        c = _scan_tile(S, lr, li, nt - 1, c, False, False)
        c = _chunk_starts(c[0], c[1], lr, li, False)
        for k in range(nt):
            c = _scan_tile(S, lr, li, k, c, False, True)
            if k >= 1:
                _rows_out(ys_ref, k - 1, _dot(S[tile(k - 1), :].astype(bf16), c_ref[...]))
        _rows_out(ys_ref, nt - 1, _dot(S[tile(nt - 1), :].astype(bf16), c_ref[...]))

    return _call(
        bg, body, name="s5_core_fwd", grid=(S5_SUB,),
        in_specs=[pl.BlockSpec((T, 128), lambda b: (0, b)),
                  pl.BlockSpec((None, 128, 1024), lambda b: (b, 0, 0)),
                  pl.BlockSpec((None, 4, 8, 512), lambda b: (b, 0, 0, 0)),
                  pl.BlockSpec((None, 1024, 128), lambda b: (b, 0, 0))],
        out_specs=[pl.BlockSpec((T, 128), lambda b: (0, b)), pl.BlockSpec((T, 1024), lambda b: (0, b))],
        out_shape=[SDS((T, D), f32), SDS((T, S5_SUB * 1024), f32)],
        compiler_params=_cp(dimension_semantics=("arbitrary",)),
    )(hn, bm, lam, cm)


_SEG = _ROWS // S5_CH


def _rows_in(ref, k):
    return jnp.concatenate([ref[pl.ds(s, S5_CH, stride=S5_STEPS), :] for s in range(k * _SEG, (k + 1) * _SEG)], axis=0)


def _rows_out(ref, k, val):
    for j, s in enumerate(range(k * _SEG, (k + 1) * _SEG)):
        ref[pl.ds(s, S5_CH, stride=S5_STEPS), :] = val[j * S5_CH:(j + 1) * S5_CH, :]


def _scan_tile(S, lr, li, k, carry, reverse, store, aux=None):
    steps = range(k * _SEG, (k + 1) * _SEG)
    for s in (reversed(steps) if reverse else steps):
        row = pl.ds(s * 8, 8)
        xr, xi = carry[0], carry[1]
        nr = lr * xr - li * xi + S[row, 0:512]
        ni = lr * xi + li * xr + S[row, 512:1024]
        if store:
            S[row, 0:512] = nr
            S[row, 512:1024] = ni
        if aux is not None and s >= 1:
            prow = pl.ds((s - 1) * 8, 8)
            pr, pi_ = aux[prow, 0:512], aux[prow, 512:1024]
            carry = (nr, ni, carry[2] + nr * pr + ni * pi_, carry[3] + ni * pr - nr * pi_)
        elif aux is not None:
            carry = (nr, ni, carry[2], carry[3])
        else:
            carry = (nr, ni)
    return carry


def _chunk_starts(er, ei, lr, li, reverse):
    ar, ai = lr, li
    for _ in range(8):
        ar, ai = _cmul(ar, ai, ar, ai)
    cr, ci = _shift_rows(er, 1, reverse), _shift_rows(ei, 1, reverse)
    for k in (1, 2, 4):
        sr, si = _shift_rows(cr, k, reverse), _shift_rows(ci, k, reverse)
        pr, pi_ = _cmul(ar, ai, sr, si)
        cr, ci = cr + pr, ci + pi_
        ar, ai = _cmul(ar, ai, ar, ai)
    return cr, ci


def s5_core_bwd(hn, dy, xs, bm, lam, cm, bg=()):
    nt = T // _ROWS

    def body(u_ref, dy_ref, S1, b_ref, lam_ref, c_ref, du_ref, db_ref, dct_ref, dlam_ref, S2):
        lcr, lci = lam_ref[2], lam_ref[3]
        z = jnp.zeros((8, 512), f32)
        tile = lambda k: pl.ds(k * _ROWS, _ROWS)

        def dx(k):
            dyb = _rows_in(dy_ref, k).astype(bf16)
            S2[tile(k), :] = _dot_nt(dyb, c_ref[...])
            dct_ref[...] += _dot_tn(dyb, S1[tile(k), :].astype(bf16))

        dct_ref[...] = jnp.zeros_like(dct_ref)
        dx(nt - 1)
        c = (z, z)
        for k in range(nt - 1, -1, -1):
            if k >= 1:
                dx(k - 1)
            c = _scan_tile(S2, lcr, lci, k, c, True, False)

        def dbu(k):
            gb = S2[tile(k), :].astype(bf16)
            db_ref[...] += _dot_tn(_rows_in(u_ref, k).astype(bf16), gb)
            _rows_out(du_ref, k, _dot_nt(gb, b_ref[...]))

        c = _chunk_starts(c[0], c[1], lcr, lci, True) + (z, z)
        db_ref[...] = jnp.zeros_like(db_ref)
        for k in range(nt - 1, -1, -1):
            c = _scan_tile(S2, lcr, lci, k, c, True, True, aux=S1)
            if k + 1 < nt:
                dbu(k + 1)
        dbu(0)
        gr, gi, dr, di = c
        last = pl.ds((S5_STEPS - 1) * 8, 8)
        xr = _shift_rows(S1[last, 0:512], 1, False)
        xi = _shift_rows(S1[last, 512:1024], 1, False)
        dlam_ref[0] = dr + gr * xr + gi * xi
        dlam_ref[1] = di + gi * xr - gr * xi

    return _call(
        bg, body, name="s5_core_bwd", grid=(S5_SUB,),
        in_specs=[pl.BlockSpec((T, 128), lambda b: (0, b)),
                  pl.BlockSpec((T, 128), lambda b: (0, b)),
                  pl.BlockSpec((T, 1024), lambda b: (0, b)),
                  pl.BlockSpec((None, 128, 1024), lambda b: (b, 0, 0)),
                  pl.BlockSpec((None, 4, 8, 512), lambda b: (b, 0, 0, 0)),
                  pl.BlockSpec((None, 1024, 128), lambda b: (b, 0, 0))],
        out_specs=[pl.BlockSpec((T, 128), lambda b: (0, b)),
                   pl.BlockSpec((None, 128, 1024), lambda b: (b, 0, 0)),
                   pl.BlockSpec((None, 128, 1024), lambda b: (b, 0, 0)),
                   pl.BlockSpec((None, 2, 8, 512), lambda b: (b, 0, 0, 0))],
        out_shape=[SDS((T, D), f32), SDS((8, 128, 1024), f32), SDS((8, 128, 1024), f32), SDS((8, 2, 8, 512), f32)],
        scratch_shapes=[pltpu.VMEM((T, 1024), f32)],
        compiler_params=_cp(dimension_semantics=("arbitrary",)),
    )(hn, dy, xs, bm, lam, cm)


TM = 512
NT = T // TM


def _tile(n=D):
    return pl.BlockSpec((TM, n), lambda i: (i, 0))


def s5_pre(xp, g):
    def body(x_ref, g_ref, hn_ref):
        hn_ref[...] = _rms(x_ref[...], g_ref[...])[0]

    return pl.pallas_call(
        body, name="s5_pre", grid=(NT,), in_specs=[_tile(), _full((1, D))], out_specs=_tile(),
        out_shape=SDS((T, D), f32), compiler_params=_cp(dimension_semantics=("arbitrary",)),
    )(xp, g)


def _gelu_grad(y):
    c = math.sqrt(2.0 / math.pi)
    t = jnp.tanh(c * (y + 0.044715 * y * y * y))
    return 0.5 * (1.0 + t) + 0.5 * y * (1.0 - t * t) * c * (1.0 + 3.0 * 0.044715 * y * y)


def s5_post(ys, xp, g, d, wglu, bglu, bg=()):
    def body(ys_ref, x_ref, g_ref, d_ref, w_ref, b_ref, y_ref, z_ref, h_ref):
        x = x_ref[...]
        hn, _ = _rms(x, g_ref[...])
        y = ys_ref[...] + d_ref[...] * hn
        y_ref[...] = y
        yg = jax.nn.gelu(y).astype(bf16)
        for j in range(4):
            cv = slice(j * 256, (j + 1) * 256)
            cg = slice(1024 + j * 256, 1024 + (j + 1) * 256)
            val = _dot(yg, w_ref[j]) + b_ref[:, cv]
            gate = _dot(yg, w_ref[j + 4]) + b_ref[:, cg]
            z_ref[:, cv] = val
            z_ref[:, cg] = gate
            h_ref[:, cv] = x[:, cv] + val * jax.nn.sigmoid(gate)

    return _call(
        bg, body, name="s5_post", grid=(NT,),
        in_specs=[_tile(), _tile(), _full((1, D)), _full((1, D)), _full((8, D, 256)), _full((1, 2 * D))],
        out_specs=[_tile(), _tile(2 * D), _tile()],
        out_shape=[SDS((T, D), f32), SDS((T, 2 * D), f32), SDS((T, D), f32)],
        compiler_params=_cp(dimension_semantics=("arbitrary",)),
    )(ys, xp, g, d, wglu, bglu)


def s5_post_bwd(dh, y, z, wglu, bg=()):
    def body(dh_ref, y_ref, z_ref, w_ref, dy_ref, dw_ref, db_ref, ygs, dzs):
        i = pl.program_id(0)
        rows = pl.ds(pl.multiple_of(i * TM, TM), TM)

        @pl.when(i == 0)
        def _():
            db_ref[...] = jnp.zeros_like(db_ref)

        dh_ = dh_ref[...]
        y = y_ref[...]
        ygs[rows, :] = jax.nn.gelu(y).astype(bf16)
        dyg = jnp.zeros((TM, D), f32)
        for j in range(4):
            cv = slice(j * 256, (j + 1) * 256)
            cg = slice(1024 + j * 256, 1024 + (j + 1) * 256)
            val = z_ref[:, cv]
            sg = jax.nn.sigmoid(z_ref[:, cg])
            dval = dh_[:, cv] * sg
            dgate = dh_[:, cv] * val * sg * (1.0 - sg)
            db_ref[:, cv] += _colsum8(dval)
            db_ref[:, cg] += _colsum8(dgate)
            dvb = dval.astype(bf16)
            dgb = dgate.astype(bf16)
            dzs[rows, cv] = dvb
            dzs[rows, cg] = dgb
            dyg = dyg + _dot_nt(dvb, w_ref[j]) + _dot_nt(dgb, w_ref[j + 4])
        dy_ref[...] = dyg * _gelu_grad(y)

        @pl.when(i == NT - 1)
        def _():
            for half in range(2):
                dw = _dot_tn(ygs[...], dzs[:, half * D:(half + 1) * D])
                for j in range(4):
                    dw_ref[4 * half + j] = dw[:, j * 256:(j + 1) * 256].astype(bf16)

    return _call(
        bg, body, name="s5_post_bwd", grid=(NT,),
        in_specs=[_tile(), _tile(), _tile(2 * D), _full((8, D, 256))],
        out_specs=[_tile(), _full((8, D, 256)), _full((8, 2 * D))],
        out_shape=[SDS((T, D), f32), SDS((8, D, 256), bf16), SDS((8, 2 * D), f32)],
        scratch_shapes=[pltpu.VMEM((T, D), bf16), pltpu.VMEM((T, 2 * D), bf16)],
        compiler_params=_cp(dimension_semantics=("arbitrary",)),
    )(dh, y, z, wglu)


def s5_pre_bwd(xp, g, du, dy, d, dh, bg=()):
    def body(x_ref, g_ref, du_ref, dy_ref, d_ref, dh_ref, dx_ref, dg_ref, dd_ref):
        i = pl.program_id(0)

        @pl.when(i == 0)
        def _():
            dg_ref[...] = jnp.zeros_like(dg_ref)
            dd_ref[...] = jnp.zeros_like(dd_ref)

        x = x_ref[...]
        g = g_ref[...]
        dy = dy_ref[...]
        hn, _ = _rms(x, g)
        dhn = du_ref[...] + d_ref[...] * dy
        dx, dgt = _rms_bwd(x, g, dhn)
        dx_ref[...] = dh_ref[...] + dx
        dg_ref[...] += _colsum8(dgt)
        dd_ref[...] += _colsum8(dy * hn)

    return _call(
        bg, body, name="s5_pre_bwd", grid=(NT,),
        in_specs=[_tile(), _full((1, D)), _tile(), _tile(), _full((1, D)), _tile()],
        out_specs=[_tile(), _full((8, D)), _full((8, D))],
        out_shape=[SDS((T, D), f32), SDS((8, D), f32), SDS((8, D), f32)],
        compiler_params=_cp(dimension_semantics=("arbitrary",)),
    )(xp, g, du, dy, d, dh)


TMF = 1024


def mlp_fwd(h, g, w_in, w_out, layer, bg=()):
    def body(h_ref, g_ref, wi_ref, wo_ref, hm_ref, r_ref, out_ref, acc):
        j = pl.program_id(1)

        @pl.when(j == 0)
        def _():
            hm, _ = _rms(h_ref[...], g_ref[...])
            hm_ref[...] = hm.astype(bf16)
            acc[...] = jnp.zeros_like(acc)

        a = jnp.maximum(_dot(hm_ref[...], wi_ref[...]), 0.0)
        r_ref[...] = a.astype(bf16)
        acc[...] += _dot((a * a).astype(bf16), wo_ref[...])

        @pl.when(j == NDEV - 1)
        def _():
            out_ref[...] = h_ref[...] + acc[...]

    return _call(
        bg, body, name=f"mlp_fwd{layer}", grid=(T // TMF, NDEV),
        in_specs=[pl.BlockSpec((TMF, D), lambda i, j: (i, 0)),
                  pl.BlockSpec((1, D), lambda i, j: (0, 0)),
                  pl.BlockSpec((None, D, D_FF_SHARD), lambda i, j: (j, 0, 0)),
                  pl.BlockSpec((None, D_FF_SHARD, D), lambda i, j: (j, 0, 0))],
        out_specs=[pl.BlockSpec((TMF, D), lambda i, j: (i, 0)), pl.BlockSpec((TMF, D_FF_SHARD), lambda i, j: (i, j)),
                   pl.BlockSpec((TMF, D), lambda i, j: (i, 0))],
        out_shape=[SDS((T, D), bf16), SDS((T, NDEV * D_FF_SHARD), bf16), SDS((T, D), f32)],
        scratch_shapes=[pltpu.VMEM((TMF, D), f32)],
        compiler_params=_cp(dimension_semantics=("arbitrary", "arbitrary")),
    )(h, g, w_in, w_out)


def mlp_bwd(h, hm, r, g, dout, dout_b, w_in, w_out, layer, bg=()):
    def body(h_ref, hm_ref, r_ref, g_ref, do_ref, dob_ref, wi_ref, wo_ref, dh_ref, dwi_ref, dwo_ref, dg_ref,
             dhm, dzs):
        s = pl.program_id(0)

        @pl.when(s == 0)
        def _():
            dhm[...] = jnp.zeros_like(dhm)

        @pl.when(s < NDEV)
        def _():
            for c in range(NT):
                rows = pl.ds(c * TM, TM)
                dz = (_dot_nt(dob_ref[rows, :], wo_ref[...]) * (2.0 * r_ref[rows, :].astype(f32))).astype(bf16)
                dzs[rows, :] = dz
                dhm[rows, :] += _dot_nt(dz, wi_ref[...])
            rb = r_ref[...]
            dwo_ref[...] = _dot_tn(rb * rb, dob_ref[...]).astype(bf16)
            dwi_ref[...] = _dot_tn(hm_ref[...], dzs[...]).astype(bf16)

        @pl.when(s >= NDEV)
        def _():
            @pl.when(s == NDEV)
            def _():
                dg_ref[...] = jnp.zeros_like(dg_ref)
            rows = pl.ds(pl.multiple_of((s - NDEV) * TM, TM), TM)
            dx, dgt = _rms_bwd(h_ref[...], g_ref[...], dhm[rows, :])
            dh_ref[...] = do_ref[...] + dx
            dg_ref[...] += _colsum8(dgt)

    shard = lambda s: (jnp.minimum(s, NDEV - 1), 0, 0)
    tile = lambda s: (jnp.maximum(s - NDEV, 0), 0)
    return _call(
        bg, body, name=f"mlp_bwd{layer}", grid=(NDEV + NT,),
        in_specs=[pl.BlockSpec((TM, D), tile),
                  _full((T, D)),
                  pl.BlockSpec((T, D_FF_SHARD), lambda s: (0, jnp.minimum(s, NDEV - 1))),
                  _full((1, D)),
                  pl.BlockSpec((TM, D), tile),
                  _full((T, D)),
                  pl.BlockSpec((None, D, D_FF_SHARD), shard),
                  pl.BlockSpec((None, D_FF_SHARD, D), shard)],
        out_specs=[pl.BlockSpec((TM, D), tile),
                   pl.BlockSpec((None, D, D_FF_SHARD), shard),
                   pl.BlockSpec((None, D_FF_SHARD, D), shard),
                   pl.BlockSpec((8, D), lambda s: (0, 0))],
        out_shape=[SDS((T, D), f32), SDS((NDEV, D, D_FF_SHARD), bf16), SDS((NDEV, D_FF_SHARD, D), bf16),
                   SDS((8, D), f32)],
        scratch_shapes=[pltpu.VMEM((T, D), f32), pltpu.VMEM((T, D_FF_SHARD), bf16)],
        compiler_params=_cp(dimension_semantics=("arbitrary",)),
    )(h, hm, r, g, dout, dout_b, w_in, w_out)


def _spread4():
    r = lax.broadcasted_iota(jnp.int32, (256, D), 0)
    c = lax.broadcasted_iota(jnp.int32, (256, D), 1)
    return ((c // 256 == r // HEAD_DIM) & (c % HEAD_DIM == r % HEAD_DIM)).astype(bf16)


def attn_pre(h, g_kv, g_mix, wkv, bkv, spread, wq, bq):
    def body(h_ref, gkv_ref, gm_ref, wkv_ref, bkv_ref, sp_ref, wq_ref, bq_ref, kvn_ref, hn_ref, k_ref, v_ref, q_ref):
        h_ = h_ref[...]
        kvn = _rms(h_, gkv_ref[...])[0].astype(bf16)
        hn = _rms(h_, gm_ref[...])[0].astype(bf16)
        kvn_ref[...] = kvn
        hn_ref[...] = hn
        kv = (_dot(kvn, wkv_ref[...]) + bkv_ref[...]).astype(bf16)
        k_ref[...] = _dot(kv[:, :256], sp_ref[...]).astype(bf16)
        v_ref[...] = _dot(kv[:, 256:], sp_ref[...]).astype(bf16)
        q_ref[...] = (_dot(hn, wq_ref[...]) + bq_ref[...]).astype(bf16)

    return pl.pallas_call(
        body, name="attn_pre", grid=(NT,),
        in_specs=[_tile(), _full((1, D)), _full((1, D)), _full((D, 512)), _full((1, 512)), _full((256, D)),
                  _full((D, D)), _full((1, D))],
        out_specs=[_tile()] * 5,
        out_shape=[SDS((T, D), bf16)] * 5,
        compiler_params=_cp(dimension_semantics=("arbitrary",)),
    )(h, g_kv, g_mix, wkv, bkv, spread, wq, bq)


def _attn_specs():
    cur = pl.BlockSpec((TM, 256), lambda j, n: (n, j))
    prev = pl.BlockSpec((BLK, 256), lambda j, n: (jnp.maximum(n * (TM // BLK) - 1, 0), j))
    return cur, prev


def _head_mask(g):
    lane = lax.broadcasted_iota(jnp.int32, (1, 256), 1)
    return (lane >= g * HEAD_DIM) & (lane < (g + 1) * HEAD_DIM)


def _stack_heads(t):
    return jnp.concatenate([jnp.where(_head_mask(g), t, 0) for g in range(Q_PER_KV)], axis=0)


def _unstack_heads(t):
    out = jnp.where(_head_mask(0), t[0:BLK], 0.0)
    for g in range(1, Q_PER_KV):
        out = out + jnp.where(_head_mask(g), t[g * BLK:(g + 1) * BLK], 0.0)
    return out


def _attn_probs(qs, k2, sinks, first):
    rows = Q_PER_KV * BLK
    s = _dot_nt(qs, k2) * (1.0 / math.sqrt(HEAD_DIM))
    qi = jnp.bitwise_and(lax.broadcasted_iota(jnp.int32, (rows, 2 * BLK), 0), BLK - 1)
    kj = lax.broadcasted_iota(jnp.int32, (rows, 2 * BLK), 1)
    diff = qi + BLK - kj
    valid = (diff >= 0) & (diff < BLK) & (jnp.logical_not(first) | (kj >= BLK))
    s = jnp.where(valid, s, -jnp.inf)
    rb = lax.broadcasted_iota(jnp.int32, (rows, 1), 0)
    sink = jnp.where(rb < BLK, sinks[0], jnp.where(rb < 2 * BLK, sinks[1], jnp.where(rb < 3 * BLK, sinks[2], sinks[3])))
    m = jnp.maximum(jnp.max(s, axis=-1, keepdims=True), sink)
    p = jnp.exp(s - m)
    ps = jnp.exp(sink - m)
    denom = jnp.sum(p, axis=-1, keepdims=True) + ps
    return p / denom, ps / denom


def _window_blocks(b, n, kc_ref, kp_ref, vc_ref, vp_ref):
    if b == 0:
        return (jnp.concatenate([kp_ref[...], kc_ref[0:BLK, :]], axis=0),
                jnp.concatenate([vp_ref[...], vc_ref[0:BLK, :]], axis=0), n == 0)
    rows = pl.ds((b - 1) * BLK, 2 * BLK)
    return kc_ref[rows, :], vc_ref[rows, :], False


def attn_core_fwd(q, k4, v4, sinks, bg=()):
    nb = TM // BLK

    def body(sink_ref, q_ref, kc_ref, kp_ref, vc_ref, vp_ref, o_ref, a_ref, as_ref):
        j = pl.program_id(0)
        n = pl.program_id(1)
        sk = [sink_ref[j * Q_PER_KV + g] for g in range(Q_PER_KV)]
        for b in range(nb):
            qb = q_ref[b * BLK:(b + 1) * BLK, :]
            k2, v2, first = _window_blocks(b, n, kc_ref, kp_ref, vc_ref, vp_ref)
            a, asink = _attn_probs(_stack_heads(qb), k2, sk, first)
            ab = a.astype(bf16)
            a_ref[b] = ab
            as_ref[b] = jnp.broadcast_to(asink, (Q_PER_KV * BLK, 128)).astype(bf16)
            o_ref[b * BLK:(b + 1) * BLK, :] = _unstack_heads(_dot(ab, v2)).astype(bf16)

    cur, prev = _attn_specs()
    rows = Q_PER_KV * BLK
    return _call(
        bg, body, name="attn_core_fwd", grid=(N_KV, NT),
        in_specs=[pl.BlockSpec(memory_space=pltpu.SMEM), cur, cur, prev, cur, prev],
        out_specs=[cur, pl.BlockSpec((None, nb, rows, 2 * BLK), lambda j, n: (j, n, 0, 0)),
                   pl.BlockSpec((None, nb, rows, 128), lambda j, n: (j, n, 0, 0))],
        out_shape=[SDS((T, D), bf16), SDS((N_KV, T // BLK, rows, 2 * BLK), bf16), SDS((N_KV, T // BLK, rows, 128), bf16)],
        compiler_params=_cp(dimension_semantics=("arbitrary", "arbitrary")),
    )(sinks, q, k4, k4, v4, v4)


def attn_post(h, o, wo, bo):
    def body(h_ref, o_ref, w_ref, b_ref, out_ref):
        out_ref[...] = h_ref[...] + _dot(o_ref[...], w_ref[...]) + b_ref[...]

    return pl.pallas_call(
        body, name="attn_post", grid=(NT,), in_specs=[_tile(), _tile(), _full((D, D)), _full((1, D))],
        out_specs=_tile(), out_shape=SDS((T, D), f32), compiler_params=_cp(dimension_semantics=("arbitrary",)),
    )(h, o, wo, bo)


def attn_bwd_pre(dh, o, wo, bg=()):
    def body(dh_ref, o_ref, w_ref, do_ref, dw_ref, db_ref, acc):
        i = pl.program_id(0)

        @pl.when(i == 0)
        def _():
            acc[...] = jnp.zeros_like(acc)
            db_ref[...] = jnp.zeros_like(db_ref)

        dh_ = dh_ref[...]
        dhb = dh_.astype(bf16)
        do_ref[...] = _dot_nt(dhb, w_ref[...]).astype(bf16)
        acc[...] += _dot_tn(o_ref[...], dhb)
        db_ref[...] += _colsum8(dh_)

        @pl.when(i == NT - 1)
        def _():
            dw_ref[...] = acc[...].astype(bf16)

    return _call(
        bg, body, name="attn_bwd_pre", grid=(NT,), in_specs=[_tile(), _tile(), _full((D, D))],
        out_specs=[_tile(), _full((D, D)), _full((8, D))],
        out_shape=[SDS((T, D), bf16), SDS((D, D), bf16), SDS((8, D), f32)],
        scratch_shapes=[pltpu.VMEM((D, D), f32)],
        compiler_params=_cp(dimension_semantics=("arbitrary",)),
    )(dh, o, wo)


def attn_core_bwd(q, do, k4, v4, probs, sink_w, bg=()):
    nb = TM // BLK

    def body(q_ref, do_ref, kc_ref, kp_ref, vc_ref, vp_ref, a_ref, as_ref, dq_ref, dk_ref, dv_ref, ds_ref):
        j = pl.program_id(0)
        n = pl.program_id(1)

        @pl.when(n == 0)
        def _():
            dk_ref[...] = jnp.zeros_like(dk_ref)
            dv_ref[...] = jnp.zeros_like(dv_ref)
            ds_ref[...] = jnp.zeros_like(ds_ref)

        lane8 = lax.broadcasted_iota(jnp.int32, (8, 128), 1)
        row8 = lax.broadcasted_iota(jnp.int32, (8, 128), 0)
        for b in range(nb):
            qs = _stack_heads(q_ref[b * BLK:(b + 1) * BLK, :])
            dos = _stack_heads(do_ref[b * BLK:(b + 1) * BLK, :])
            k2, v2, _ = _window_blocks(b, n, kc_ref, kp_ref, vc_ref, vp_ref)
            ab = a_ref[b]
            a = ab.astype(f32)
            asink = as_ref[b][:, 0:1].astype(f32)
            dp = _dot_nt(dos, v2)
            dd = jnp.sum(a * dp, axis=-1, keepdims=True)
            dsc = (a * (dp - dd) * (1.0 / math.sqrt(HEAD_DIM))).astype(bf16)
            t = asink * dd
            for g in range(Q_PER_KV):
                dsink = -jnp.sum(t[g * BLK:(g + 1) * BLK], axis=0, keepdims=True)
                ds_ref[...] += jnp.where((lane8 == g) & (row8 == 0), jnp.broadcast_to(dsink, (8, 128)), 0.0)
            dq_ref[b * BLK:(b + 1) * BLK, :] = _unstack_heads(_dot(dsc, k2))
            dk2 = _dot_tn(dsc, qs)
            dv2 = _dot_tn(ab, dos)
            cur = pl.ds(pl.multiple_of(n * TM + b * BLK, BLK), BLK)
            dk_ref[cur, :] += dk2[BLK:, :]
            dv_ref[cur, :] += dv2[BLK:, :]
            if b == 0:
                @pl.when(n > 0)
                def _():
                    prv = pl.ds(pl.multiple_of(n * TM - BLK, BLK), BLK)
                    dk_ref[prv, :] += dk2[:BLK, :]
                    dv_ref[prv, :] += dv2[:BLK, :]
            else:
                prv = pl.ds(pl.multiple_of(n * TM + (b - 1) * BLK, BLK), BLK)
                dk_ref[prv, :] += dk2[:BLK, :]
                dv_ref[prv, :] += dv2[:BLK, :]

    cur, prev = _attn_specs()
    col = pl.BlockSpec((T, 256), lambda j, n: (0, j))
    rows = Q_PER_KV * BLK
    return _call(
        bg, body, name="attn_core_bwd", grid=(N_KV, NT),
        in_specs=[cur, cur, cur, prev, cur, prev,
                  pl.BlockSpec((None, nb, rows, 2 * BLK), lambda j, n: (j, n, 0, 0)),
                  pl.BlockSpec((None, nb, rows, 128), lambda j, n: (j, n, 0, 0))],
        out_specs=[cur, col, col, pl.BlockSpec((None, 8, 128), lambda j, n: (j, 0, 0))],
        out_shape=[SDS((T, D), f32), SDS((T, D), f32), SDS((T, D), f32), SDS((N_KV, 8, 128), f32)],
        compiler_params=_cp(dimension_semantics=("arbitrary", "arbitrary")),
    )(q, do, k4, k4, v4, v4, probs, sink_w)


def attn_bwd_q(h, dh, dq, hn, g_mix, wq):
    def body(h_ref, dh_ref, dq_ref, hn_ref, gm_ref, wq_ref, out_ref, dwq_ref, dbq_ref, dgm_ref, aq):
        i = pl.program_id(0)

        @pl.when(i == 0)
        def _():
            aq[...] = jnp.zeros_like(aq)
            dbq_ref[...] = jnp.zeros_like(dbq_ref)
            dgm_ref[...] = jnp.zeros_like(dgm_ref)

        dq_ = dq_ref[...]
        dqb = dq_.astype(bf16)
        aq[...] += _dot_tn(hn_ref[...], dqb)
        dbq_ref[...] += _colsum8(dq_)
        dx, dg = _rms_bwd(h_ref[...], gm_ref[...], _dot_nt(dqb, wq_ref[...]))
        out_ref[...] = dh_ref[...] + dx
        dgm_ref[...] += _colsum8(dg)

        @pl.when(i == NT - 1)
        def _():
            dwq_ref[...] = aq[...].astype(bf16)

    vec = _full((8, D))
    mat = _full((D, D))
    return pl.pallas_call(
        body, name="attn_bwd_q", grid=(NT,),
        in_specs=[_tile()] * 4 + [_full((1, D)), mat],
        out_specs=[_tile(), mat, vec, vec],
        out_shape=[SDS((T, D), f32), SDS((D, D), bf16), SDS((8, D), f32), SDS((8, D), f32)],
        scratch_shapes=[pltpu.VMEM((D, D), f32)],
        compiler_params=_cp(dimension_semantics=("arbitrary",)),
    )(h, dh, dq, hn, g_mix, wq)


def attn_bwd_kv(h, dh, dk4, dv4, kvn, g_kv, wkv, spread):
    def body(h_ref, dh_ref, dk_ref, dv_ref, kvn_ref, gkv_ref, wkv_ref, sp_ref, out_ref, outb_ref, dw_ref, db_ref,
             dgkv_ref, acc):
        i = pl.program_id(0)

        @pl.when(i == 0)
        def _():
            for r in (acc, db_ref, dgkv_ref):
                r[...] = jnp.zeros_like(r)

        dkv = jnp.concatenate([_dot_nt(dk_ref[...].astype(bf16), sp_ref[...]),
                               _dot_nt(dv_ref[...].astype(bf16), sp_ref[...])], axis=1)
        dkvb = dkv.astype(bf16)
        acc[...] += _dot_tn(kvn_ref[...], dkvb)
        db_ref[...] += _colsum8(dkv)
        dx, dg = _rms_bwd(h_ref[...], gkv_ref[...], _dot_nt(dkvb, wkv_ref[...]))
        out = dh_ref[...] + dx
        out_ref[...] = out
        outb_ref[...] = out.astype(bf16)
        dgkv_ref[...] += _colsum8(dg)

        @pl.when(i == NT - 1)
        def _():
            dw_ref[...] = acc[...].astype(bf16)

    return pl.pallas_call(
        body, name="attn_bwd_kv", grid=(NT,),
        in_specs=[_tile()] * 5 + [_full((1, D)), _full((D, 512)), _full((256, D))],
        out_specs=[_tile(), _tile(), _full((D, 512)), _full((8, 512)), _full((8, D))],
        out_shape=[SDS((T, D), f32), SDS((T, D), bf16), SDS((D, 512), bf16), SDS((8, 512), f32), SDS((8, D), f32)],
        scratch_shapes=[pltpu.VMEM((D, 512), f32)],
        compiler_params=_cp(dimension_semantics=("arbitrary",)),
    )(h, dh, dk4, dv4, kvn, g_kv, wkv, spread)


def final_loss(h, g, target):
    def body(h_ref, g_ref, t_ref, loss_ref, dh_ref, dhb_ref, dg_ref):
        i = pl.program_id(0)

        @pl.when(i == 0)
        def _():
            loss_ref[...] = jnp.zeros_like(loss_ref)
            dg_ref[...] = jnp.zeros_like(dg_ref)

        h_ = h_ref[...]
        g_ = g_ref[...]
        y, _ = _rms(h_, g_)
        diff = y - t_ref[...]
        per_tok = jnp.mean(diff * diff, axis=-1, keepdims=True)
        tot = 0.5 * jnp.sum(per_tok, axis=0, keepdims=True)
        lane = lax.broadcasted_iota(jnp.int32, (8, 128), 1)
        row = lax.broadcasted_iota(jnp.int32, (8, 128), 0)
        loss_ref[...] += jnp.where((lane == 0) & (row == 0), jnp.broadcast_to(tot, (8, 128)), 0.0)
        dx, dgt = _rms_bwd(h_, g_, diff * (1.0 / D))
        dh_ref[...] = dx
        dhb_ref[...] = dx.astype(bf16)
        dg_ref[...] += _colsum8(dgt)

    return pl.pallas_call(
        body, name="final_loss", grid=(NT,), in_specs=[_tile(), _full((1, D)), _tile()],
        out_specs=[_full((8, 128)), _tile(), _tile(), _full((8, D))],
        out_shape=[SDS((8, 128), f32), SDS((T, D), f32), SDS((T, D), bf16), SDS((8, D), f32)],
        compiler_params=_cp(dimension_semantics=("arbitrary",)),
    )(h, g, target)


def fwd_bwd(x, target, p, shards, opt, core, chip):
    row = lambda v: v.reshape(1, -1)
    (lam, bm, cm), prep_vjp = jax.vjp(s5_discretize, p["s5_a_re"][0], p["s5_a_im"][0], p["s5_log_dt"][0],
                                      p["s5_b_re"][0], p["s5_b_im"][0], p["s5_c_re"][0], p["s5_c_im"][0])
    bmb, cmb = bm.astype(bf16), cm.astype(bf16)
    lam = jnp.concatenate([lam, lam * jnp.array([1.0, -1.0], f32).reshape(1, 2, 1, 1)], axis=1)
    g_mix0, g_mix1 = row(p["norm_mix"][0]), row(p["norm_mix"][1])
    g_mlp0, g_mlp1 = row(p["norm_mlp"][0]), row(p["norm_mlp"][1])
    g_kv, g_fin = row(p["norm_kv"]), row(p["norm_final"])
    bq, bo = p["b_q"], p["b_o"]
    bkv = row(p["b_kv"])
    spread = _spread4()
    sinks = p["sinks"].reshape(16)

    wglu, gvec = sc_gather([shards["s5_w_glu"], shards["vecs"]], 3, "sc_gather_s5")
    win0, wout0 = sc_gather([shards["w_in0"], shards["w_out0"]], 14, "sc_gather_mlp0")
    wkv, wq, wo = sc_gather([shards["w_kv"], shards["w_q"], shards["w_o"]], 4, "sc_gather_attn")
    win1, wout1 = sc_gather([shards["w_in1"], shards["w_out1"]], 5, "sc_gather_mlp1")
    xp = x
    hn0 = s5_pre(xp, g_mix0)
    ys, xs = s5_core_fwd(hn0, bmb, lam, cmb)
    d_skip = gvec[:, 0, :128].reshape(1, D)
    bglu = gvec[:, 0, 128:].reshape(1, 2 * D)
    y, z, h1 = s5_post(ys, xp, g_mix0, d_skip, wglu, bglu)
    hm0, r0, h2p = mlp_fwd(h1, g_mlp0, win0, wout0, 0)
    wkv, wq, wo = wkv.reshape(D, 512), wq.reshape(D, D), wo.reshape(D, D)
    h2 = h2p
    kvn, hn1, k4, v4, q = attn_pre(h2, g_kv, g_mix1, wkv, bkv, spread, wq, bq)
    o, probs, sink_w = attn_core_fwd(q, k4, v4, sinks)
    h3 = attn_post(h2, o, wo, bo)
    hm1, r1, h4 = mlp_fwd(h3, g_mlp1, win1, wout1, 1)
    loss, dh4, dh4b, dg_fin = final_loss(h4, g_fin, target)

    def pair_sums(names, grads, cid, before):
        r1 = sc_comm(BgPair(grads), cid, "sc_pair_" + names[0])
        parts = add_pairs(grads, r1, core, "add_pairs_" + names[0])
        before, parts = lax.optimization_barrier((before, parts))
        return before, parts

    def across_chips(names, parts, cid):
        return list(zip(parts, sc_comm(BgChips(parts), cid, "sc_chips_" + names[0])))

    dh3, dwin1, dwout1, dg_mlp1 = mlp_bwd(h3, hm1, r1, g_mlp1, dh4, dh4b, win1, wout1, 1)
    do, dwo, dbo = attn_bwd_pre(dh3, o, wo)
    do, parts = pair_sums(["w_in1", "w_out1"], [dwin1, dwout1], 6, do)
    rs_in1, rs_out1 = across_chips(["w_in1", "w_out1"], parts, 7)
    dq, dk4, dv4, dsink = attn_core_bwd(q, do, k4, v4, probs, sink_w)
    dh2, dwq, dbq, dg_mix1 = attn_bwd_q(h2, dh3, dq, hn1, g_mix1, wq)
    dh2, dh2b, dwkv, dbkv, dg_kv = attn_bwd_kv(h2, dh2, dk4, dv4, kvn, g_kv, wkv, spread)
    dh2p, dh2pb = dh2, dh2b
    big = {}
    a_in1 = adam_big(*opt["w_mlp_in"], *rs_in1, chip, "adam_w_mlp_in1", layer=1)
    a_out1 = adam_big(*opt["w_mlp_out"], *rs_out1, chip, "adam_w_mlp_out1", layer=1)
    dh2p, a_in1, a_out1 = lax.optimization_barrier((dh2p, a_in1, a_out1))
    names = ["w_kv", "w_q", "w_o"]
    dh2p, parts = pair_sums(names, [dwkv.reshape(NDEV, 128, 512), dwq.reshape(NDEV, 128, D),
                                    dwo.reshape(NDEV, 128, D)], 8, dh2p)
    rs_attn = across_chips(names, parts, 9)
    dh1, dwin0, dwout0, dg_mlp0 = mlp_bwd(h1, hm0, r0, g_mlp0, dh2p, dh2pb, win0, wout0, 0)
    a_attn = [adam_big(*opt[n], *rs, chip, f"adam_{n}") for n, rs in zip(names, rs_attn)]
    dh1, a_attn = lax.optimization_barrier((dh1, a_attn))
    big.update(zip(names, a_attn))
    dy, dwglu, dbglu = s5_post_bwd(dh1, y, z, wglu)
    dy, parts = pair_sums(["w_in0", "w_out0"], [dwin0, dwout0], 10, dy)
    rs_in0, rs_out0 = across_chips(["w_in0", "w_out0"], parts, 11)
    du, dbm, dcmt, dlam = s5_core_bwd(hn0, dy, xs, bmb, lam, cmb)
    du, parts = pair_sums(["s5_w_glu"], [dwglu], 12, du)
    rs_glu, = across_chips(["s5_w_glu"], parts, 13)
    dxp, dg_mix0, dd = s5_pre_bwd(xp, g_mix0, du, dy, d_skip, dh1)
    big["w_mlp_in"] = adam_big(*opt["w_mlp_in"], *rs_in0, chip, "adam_w_mlp_in0", layer=0, prev=a_in1)
    big["w_mlp_out"] = adam_big(*opt["w_mlp_out"], *rs_out0, chip, "adam_w_mlp_out0", layer=0, prev=a_out1)
    big["s5_w_glu"] = adam_big(*opt["s5_w_glu"], *rs_glu, chip, "adam_s5_w_glu")
    grad_x = dxp
    da_re, da_im, dlog_dt, db_re, db_im, dc_re, dc_im = prep_vjp((dlam, dbm, dcmt.transpose(0, 2, 1)))

    def lanes(v_):
        v_ = v_.reshape(1, -1)
        return jnp.pad(v_, ((0, 0), (0, D - v_.shape[1])))

    small = dict(
        rows8=[dg_mix0, dg_mix1, dg_mlp0, dg_mlp1, dg_kv, dg_fin, dd, dbq, dbo], b_glu=dbglu, b_kv=dbkv,
        misc=jnp.concatenate([lanes(dsink[:, 0, :Q_PER_KV]), lanes(dlog_dt), lanes(loss[0:1, 0:1])], axis=0),
        s5=[da_re.reshape(4, D), da_im.reshape(4, D),
            db_re.transpose(0, 2, 1).reshape(64, D), db_im.transpose(0, 2, 1).reshape(64, D),
            dc_re.reshape(64, D), dc_im.reshape(64, D)])
    small, big["w_mlp_in"], big["w_mlp_out"] = lax.optimization_barrier((small, big["w_mlp_in"], big["w_mlp_out"]))
    return loss, grad_x, small, big


def add_pairs(gs, r1s, core, name):
    n = len(gs)

    def body(core_ref, *refs):
        for g_ref, r_ref, o_ref in zip(refs[:n], refs[n:2 * n], refs[2 * n:]):
            o_ref[...] = (g_ref[...].astype(f32) + r_ref[...].astype(f32)).astype(bf16)

    mine = [pl.BlockSpec((None,) + g.shape[1:], lambda k, core: (2 * k + core[0], 0, 0)) for g in gs]
    slot = [pl.BlockSpec((None,) + g.shape[1:], lambda k, core: (k, 0, 0)) for g in gs]
    return pl.pallas_call(
        body, name=name, out_shape=[SDS((4,) + g.shape[1:], bf16) for g in gs],
        grid_spec=pltpu.PrefetchScalarGridSpec(num_scalar_prefetch=1, grid=(4,), in_specs=mine + slot, out_specs=slot),
        compiler_params=_cp(dimension_semantics=("arbitrary",)),
    )(core, *gs, *r1s)


def _adamw(w, g, m, v):
    m = ADAM_B1 * m + (1.0 - ADAM_B1) * g
    v = ADAM_B2 * v + (1.0 - ADAM_B2) * (g * g)
    m_hat = m / (1.0 - ADAM_B1 ** ADAM_STEP)
    v_hat = v / (1.0 - ADAM_B2 ** ADAM_STEP)
    delta = -ADAM_LR * (m_hat / (jnp.sqrt(v_hat) + ADAM_EPS) + ADAM_WD * w)
    return delta, m, v


def adam_big(w, m, v, part, r2, chip, name, layer=0, prev=None):
    L, R, C = w.shape
    tr = max(R // 4, 128)

    def body(chip_ref, w_ref, m_ref, v_ref, p_ref, r_ref, *rest):
        g_out, d_out, m_out, v_out = rest[-4:]
        g = p_ref[...].astype(f32) + r_ref[0].astype(f32) + r_ref[1].astype(f32) + r_ref[2].astype(f32)
        d, m_, v_ = _adamw(w_ref[...], g, m_ref[...], v_ref[...])
        g_out[...] = g
        d_out[...] = d
        m_out[...] = m_
        v_out[...] = v_

    blk = pl.BlockSpec((None, tr, C), lambda i, chip: (layer, i, 0))
    extra = [] if prev is None else list(prev)
    return pl.pallas_call(
        body, name=name, out_shape=[SDS((L, R, C), f32)] * 4,
        grid_spec=pltpu.PrefetchScalarGridSpec(
            num_scalar_prefetch=1, grid=(R // tr,),
            in_specs=[blk, blk, blk,
                      pl.BlockSpec((None, tr, C), lambda i, chip: (chip[0], i, 0)),
                      pl.BlockSpec((3, tr, C), lambda i, chip: (0, i, 0))] + [_ANY] * len(extra),
            out_specs=[blk] * 4),
        input_output_aliases={6 + k: k for k in range(len(extra))},
        compiler_params=_cp(dimension_semantics=("arbitrary",)),
    )(chip, w, m, v, part, r2, *extra)


SMALL_BUF_ROWS = 288


def allreduce_small(rows8, b_glu, b_kv, misc, s5):
    R = SMALL_BUF_ROWS
    half, quarter = R // 2, R // 4
    pieces = [*rows8, b_glu, b_kv, misc, *s5]

    def body(*refs):
        ins, (out_ref, in_ref, acc1, acc2, r0, r1, r2, send_sems, recv_sems) = refs[:len(pieces)], refs[len(pieces):]
        in_ref[8:16, :] = jnp.zeros((8, D), f32)
        in_ref[R - 8:R, :] = jnp.zeros((8, D), f32)
        for k in range(len(rows8)):
            in_ref[k:k + 1, :] = ins[k][0:1, :]
        glu_ref, kv_ref, misc_ref = ins[len(rows8):len(rows8) + 3]
        in_ref[9:10, :] = glu_ref[0:1, 0:D]
        in_ref[10:11, :] = glu_ref[0:1, D:2 * D]
        in_ref[11:12, 0:kv_ref.shape[1]] = kv_ref[0:1, :]
        in_ref[12:15, :] = misc_ref[...]
        row = 16
        for a in ins[len(rows8) + 3:]:
            in_ref[row:row + a.shape[0], :] = a[...]
            row += a.shape[0]
        x, y, c = _pos()
        sibling, over_x, over_y = (x, y, 1 - c), (1 - x, y, c), (x, 1 - y, c)
        first = pl.multiple_of(c * half, 8)
        mine = pl.ds(first, half)
        theirs = pl.ds(pl.multiple_of((1 - c) * half, 8), half)
        qa = pl.ds(first, quarter)
        qb = pl.ds(pl.multiple_of(first + quarter, 8), quarter)

        def exchange(copies):
            cps = [pltpu.make_async_remote_copy(
                src_ref=src.at[rows], dst_ref=dst.at[rows], send_sem=send_sems.at[k], recv_sem=recv_sems.at[k],
                device_id=peer, device_id_type=MESH) for k, src, dst, rows, peer in copies]
            for cp in cps:
                cp.start()
            for cp in cps:
                cp.wait()

        exchange([(0, in_ref, r0, theirs, sibling)])
        acc1[mine, :] = in_ref[mine, :] + r0[mine, :]
        exchange([(1, acc1, r1, qa, over_x), (2, acc1, r1, qb, over_y)])
        acc2[mine, :] = acc1[mine, :] + r1[mine, :]
        exchange([(3, acc2, r2, qa, over_y), (4, acc2, r2, qb, over_x)])
        out_ref[mine, :] = acc2[mine, :] + r2[mine, :]
        exchange([(5, out_ref, out_ref, mine, sibling)])

    vm = pl.BlockSpec(memory_space=pltpu.VMEM)
    return pl.pallas_call(
        body, name="allreduce_small", in_specs=[vm] * len(pieces), out_specs=vm, out_shape=SDS((R, D), f32),
        scratch_shapes=[pltpu.VMEM((R, D), f32)] * 6 + [pltpu.SemaphoreType.DMA((6,)), pltpu.SemaphoreType.DMA((6,))],
    )(*pieces)


SMALL_ROWS = {'norm_mix': (0, 2, D), 'norm_mlp': (2, 2, D), 'norm_kv': (4, 1, D), 'norm_final': (5, 1, D),
              's5_d': (6, 1, D), 'b_q': (7, 1, D), 'b_o': (8, 1, D), 's5_b_glu': (9, 2, D), 'b_kv': (11, 1, 512),
              'sinks': (12, 1, 16), 's5_log_dt': (13, 1, 64), 's5_a_re': (16, 4, D), 's5_a_im': (20, 4, D),
              's5_b_re': (24, 64, D), 's5_b_im': (88, 64, D), 's5_c_re': (152, 64, D), 's5_c_im': (216, 64, D)}
LOSS_ROW = 14
ROW_PARAMS = ['norm_mix', 'norm_mlp', 'norm_kv', 'norm_final', 'b_q', 'b_o', 'b_kv', 'sinks', 's5_log_dt']
SHARD_PARAMS = ['s5_d', 's5_b_glu']
S5_PARAMS = ['s5_a_re', 's5_a_im', 's5_b_re', 's5_b_im', 's5_c_re', 's5_c_im']


def adam_small(dev, gsum, s5_grads, w, m, v):
    names = ROW_PARAMS + SHARD_PARAMS + S5_PARAMS
    n_g = len(ROW_PARAMS) + len(SHARD_PARAMS)

    def body(dev_ref, gs_ref, *refs):
        pos = [0]

        def take(k):
            r = refs[pos[0]:pos[0] + k]
            pos[0] += k
            return r

        g5 = take(len(S5_PARAMS))
        wr, mr, vr = take(len(names)), take(len(names)), take(len(names))
        g_out = take(n_g)
        d_out, m_out, v_out = take(len(names)), take(len(names)), take(len(names))
        dv = dev_ref[0]
        for i, n in enumerate(names):
            if n in S5_PARAMS:
                g = g5[S5_PARAMS.index(n)][...]
            elif n in SHARD_PARAMS:
                r0, _, _ = SMALL_ROWS[n]
                ln = wr[i].shape[1]
                g = jnp.zeros((1, ln), f32)
                for k in range(NDEV):
                    off = k * ln
                    piece = gs_ref[r0 + off // D:r0 + off // D + 1, off % D:off % D + ln]
                    g = g + jnp.where(dv == k, piece, 0.0)
                g_out[i][...] = g
            else:
                r0, nr, nl = SMALL_ROWS[n]
                g = gs_ref[r0:r0 + nr, 0:nl]
                g_out[i][...] = g
            d, m_, v_ = _adamw(wr[i][...], g, mr[i][...], vr[i][...])
            d_out[i][...] = d
            m_out[i][...] = m_
            v_out[i][...] = v_

    vm = pl.BlockSpec(memory_space=pltpu.VMEM)
    ins = [s5_grads[n] for n in S5_PARAMS] + [d[n] for d in (w, m, v) for n in names]
    shapes = [SDS(w[n].shape, f32) for n in names]
    res = pl.pallas_call(
        body, name="adam_small", in_specs=[pl.BlockSpec(memory_space=pltpu.SMEM)] + [vm] * (1 + len(ins)),
        out_specs=[vm] * (n_g + 3 * len(names)), out_shape=shapes[:n_g] + shapes * 3,
        compiler_params=_cp(),
    )(dev, gsum, *ins)
    g_o = dict(zip(names[:n_g], res[:n_g]))
    rest = res[n_g:]
    k = len(names)
    return g_o, dict(zip(names, rest[:k])), dict(zip(names, rest[k:2 * k])), dict(zip(names, rest[2 * k:]))


WEIGHTS = ['norm_mix', 'norm_mlp', 'norm_kv', 'norm_final', 's5_a_re', 's5_a_im', 's5_log_dt', 's5_b_re', 's5_b_im',
           's5_c_re', 's5_c_im', 's5_d', 's5_w_glu', 's5_b_glu', 'w_kv', 'b_kv', 'w_q', 'b_q', 'sinks', 'w_o', 'b_o',
           'w_mlp_in', 'w_mlp_out']
BIG = ['s5_w_glu', 'w_kv', 'w_q', 'w_o', 'w_mlp_in', 'w_mlp_out']
BIG_2D = {'s5_w_glu': (D, 256), 'w_kv': (128, 512), 'w_q': (128, D), 'w_o': (128, D), 'w_mlp_in': (2 * D, 512),
          'w_mlp_out': (2 * 512, D)}
SMALL = [n for n in WEIGHTS if n not in BIG]


def kernel(x, norm_mix, norm_mlp, norm_kv, norm_final, s5_a_re, s5_a_im, s5_log_dt, s5_b_re, s5_b_im, s5_c_re, s5_c_im, s5_d, s5_w_glu, s5_b_glu, w_kv, b_kv, w_q, b_q, sinks, w_o, b_o, w_mlp_in, w_mlp_out, loss_target, m_norm_mix, m_norm_mlp, m_norm_kv, m_norm_final, m_s5_a_re, m_s5_a_im, m_s5_log_dt, m_s5_b_re, m_s5_b_im, m_s5_c_re, m_s5_c_im, m_s5_d, m_s5_w_glu, m_s5_b_glu, m_w_kv, m_b_kv, m_w_q, m_b_q, m_sinks, m_w_o, m_b_o, m_w_mlp_in, m_w_mlp_out, v_norm_mix, v_norm_mlp, v_norm_kv, v_norm_final, v_s5_a_re, v_s5_a_im, v_s5_log_dt, v_s5_b_re, v_s5_b_im, v_s5_c_re, v_s5_c_im, v_s5_d, v_s5_w_glu, v_s5_b_glu, v_w_kv, v_b_kv, v_w_q, v_b_q, v_sinks, v_w_o, v_b_o, v_w_mlp_in, v_w_mlp_out):
    w = dict(norm_mix=norm_mix, norm_mlp=norm_mlp, norm_kv=norm_kv, norm_final=norm_final, s5_a_re=s5_a_re,
             s5_a_im=s5_a_im, s5_log_dt=s5_log_dt, s5_b_re=s5_b_re, s5_b_im=s5_b_im, s5_c_re=s5_c_re, s5_c_im=s5_c_im,
             s5_d=s5_d, s5_w_glu=s5_w_glu, s5_b_glu=s5_b_glu, w_kv=w_kv, b_kv=b_kv, w_q=w_q, b_q=b_q, sinks=sinks,
             w_o=w_o, b_o=b_o, w_mlp_in=w_mlp_in, w_mlp_out=w_mlp_out)
    m = dict(norm_mix=m_norm_mix, norm_mlp=m_norm_mlp, norm_kv=m_norm_kv, norm_final=m_norm_final, s5_a_re=m_s5_a_re,
             s5_a_im=m_s5_a_im, s5_log_dt=m_s5_log_dt, s5_b_re=m_s5_b_re, s5_b_im=m_s5_b_im, s5_c_re=m_s5_c_re,
             s5_c_im=m_s5_c_im, s5_d=m_s5_d, s5_w_glu=m_s5_w_glu, s5_b_glu=m_s5_b_glu, w_kv=m_w_kv, b_kv=m_b_kv,
             w_q=m_w_q, b_q=m_b_q, sinks=m_sinks, w_o=m_w_o, b_o=m_b_o, w_mlp_in=m_w_mlp_in, w_mlp_out=m_w_mlp_out)
    v = dict(norm_mix=v_norm_mix, norm_mlp=v_norm_mlp, norm_kv=v_norm_kv, norm_final=v_norm_final, s5_a_re=v_s5_a_re,
             s5_a_im=v_s5_a_im, s5_log_dt=v_s5_log_dt, s5_b_re=v_s5_b_re, s5_b_im=v_s5_b_im, s5_c_re=v_s5_c_re,
             s5_c_im=v_s5_c_im, s5_d=v_s5_d, s5_w_glu=v_s5_w_glu, s5_b_glu=v_s5_b_glu, w_kv=v_w_kv, b_kv=v_b_kv,
             w_q=v_w_q, b_q=v_b_q, sinks=v_sinks, w_o=v_w_o, b_o=v_b_o, w_mlp_in=v_w_mlp_in, w_mlp_out=v_w_mlp_out)
    xi, yi, ci = _pos()
    dev = 4 * xi + 2 * yi + ci
    core = ci.reshape(1).astype(jnp.int32)
    chip = (2 * xi + yi).reshape(1).astype(jnp.int32)

    shards = {
        "s5_w_glu": s5_w_glu[0].astype(bf16), "w_kv": w_kv.astype(bf16), "w_q": w_q[0].astype(bf16),
        "w_o": w_o[0].astype(bf16), "w_in0": w_mlp_in[0].astype(bf16), "w_in1": w_mlp_in[1].astype(bf16),
        "w_out0": w_mlp_out[0].astype(bf16), "w_out1": w_mlp_out[1].astype(bf16),
        "vecs": jnp.broadcast_to(jnp.concatenate([s5_d, s5_b_glu], axis=1), (8, 384)),
    }
    as3d = lambda a, n: a if a.ndim == 3 and a.shape[0] == 2 else a.reshape((1,) + BIG_2D[n])
    opt = {n: (as3d(w[n], n), as3d(m[n], n), as3d(v[n], n)) for n in BIG}
    _, grad_x, grads, big = fwd_bwd(x[0], loss_target[0], {n: w[n] for n in SMALL}, shards, opt, core, chip)

    gsum = allreduce_small(**grads)

    out_g, out_d, out_m, out_v = {}, {}, {}, {}
    for n in BIG:
        out_g[n], out_d[n], out_m[n], out_v[n] = [r.reshape(w[n].shape) for r in big[n]]

    loss = gsum[LOSS_ROW, 0]
    swapped = ("s5_b_re", "s5_b_im")
    swap = lambda a: a.transpose(0, 1, 3, 2)

    def kernel_side(d):
        d = {n: (d[n].reshape(1, -1) if d[n].ndim == 1 else d[n]) for n in SMALL}
        d.update({n: swap(d[n]) for n in swapped})
        return d

    s5_g = {}
    for n in S5_PARAMS:
        r0, nr, _ = SMALL_ROWS[n]
        s5_g[n] = gsum[r0:r0 + nr].reshape((1, 64, 16, 64) if n in swapped else w[n].shape)
        out_g[n] = s5_g[n]
    g_s, d_s, m_s, v_s = adam_small(dev.reshape(1).astype(jnp.int32), gsum, s5_g, kernel_side(w), kernel_side(m),
                                    kernel_side(v))
    for src, dst in ((g_s, out_g), (d_s, out_d), (m_s, out_m), (v_s, out_v)):
        dst.update(src)
    for dst in (out_g, out_d, out_m, out_v):
        for n in SMALL:
            dst[n] = (swap(dst[n]) if n in swapped else dst[n]).reshape(w[n].shape)

    return (loss, grad_x[None], *[out_g[n] for n in WEIGHTS], *[out_d[n] for n in WEIGHTS],
            *[out_m[n] for n in WEIGHTS], *[out_v[n] for n in WEIGHTS])
```

```python
import functools
import math

import jax
import jax.numpy as jnp
from jax import lax
from jax.experimental import pallas as pl
from jax.experimental.pallas import tpu as pltpu
from jax.experimental.pallas import tpu_sc as plsc

f32 = jnp.float32
bf16 = jnp.bfloat16
SDS = jax.ShapeDtypeStruct

T = 2048
D = 1024
NDEV = 8
NORM_EPS = 1e-5
S5_G, S5_C, S5_P = 64, 16, 64
S5_SUB = 8
S5_CH = 8
S5_STEPS = T // S5_CH
DT_MIN_LAMBDA = -1e-4
HEAD_DIM = 64
N_KV = 4
Q_PER_KV = 4
BLK = 128
D_FF_SHARD = 512
ADAM_LR, ADAM_B1, ADAM_B2, ADAM_EPS, ADAM_WD, ADAM_STEP = 0.001, 0.9, 0.999, 1e-08, 0.01, 10
VMEM_LIMIT = 56 * 1024 * 1024
MESH = pl.DeviceIdType.MESH


def _cp(**kw):
    return pltpu.CompilerParams(vmem_limit_bytes=VMEM_LIMIT, **kw)


def _dot(a, b):
    return jnp.dot(a, b, preferred_element_type=f32)


def _dot_nt(a, b):
    return lax.dot_general(a, b, (((1,), (1,)), ((), ())), preferred_element_type=f32)


def _dot_tn(a, b):
    return lax.dot_general(a, b, (((0,), (0,)), ((), ())), preferred_element_type=f32)


def _rms(x, g):
    r = lax.rsqrt(jnp.mean(x * x, axis=-1, keepdims=True) + NORM_EPS)
    return x * r * g, r


def _rms_bwd(x, g, dy):
    r = lax.rsqrt(jnp.mean(x * x, axis=-1, keepdims=True) + NORM_EPS)
    u = dy * g
    dx = r * u - (r * r * r) * x * jnp.mean(u * x, axis=-1, keepdims=True)
    return dx, dy * x * r


def _colsum8(v):
    s = jnp.sum(v, axis=0, keepdims=True)
    row = lax.broadcasted_iota(jnp.int32, (8, v.shape[1]), 0)
    return jnp.where(row == 0, jnp.broadcast_to(s, (8, v.shape[1])), 0.0)


def _full(shape):
    nd = len(shape)
    return pl.BlockSpec(shape, lambda *_: (0,) * nd, pipeline_mode=pl.Buffered(1))


_ANY = pl.BlockSpec(memory_space=pl.ANY)


def _pos():
    return lax.axis_index("x"), lax.axis_index("y"), lax.axis_index("c")


def _other_chips(x, y):
    return [(1 - x, y), (x, 1 - y), (1 - x, 1 - y)]


class BgGather:
    SIB, XN, YN, FWD_Y, FWD_X, SIB_X, SIB_Y, SIB_D = range(8)

    def __init__(self, arrs, mids=(0.5, 0.75)):
        n = len(arrs)
        self.arrs = list(arrs)
        self.out_shape = [SDS((NDEV,) + a.shape, a.dtype) for a in arrs]
        self.scratch = [pltpu.SemaphoreType.DMA((n, 8)), pltpu.SemaphoreType.DMA((n, 8)),
                        pltpu.SemaphoreType.DMA((n,))]
        self.mids = mids
        self.result = None

    @staticmethod
    def peers(x, y, c):
        return [(x, y, 1 - c), (1 - x, y, c), (x, 1 - y, c)]

    def mid_steps(self, nsteps):
        at = lambda f: min(nsteps - 1, max(0, int(f * nsteps) - 1))
        return [(at(self.mids[0]), self.mid), (max(at(self.mids[0]), at(self.mids[1])), self.mid2)]

    def _halves(self, a):
        rows = self.arrs[a].shape[0]
        cut = (rows // 32) * 16 if rows >= 32 else rows
        return (0, cut), (cut, rows - cut)

    def _copy(self, ins, outs, sems, a, k, block, to, own=False, part=None):
        slot = 4 * block[0] + 2 * block[1] + block[2]
        rows = pl.ds(0, self.arrs[a].shape[0]) if part is None else pl.ds(*self._halves(a)[part])
        dst = outs[a].at[slot, rows]
        return pltpu.make_async_remote_copy(
            src_ref=ins[a].at[rows] if own else dst, dst_ref=dst, send_sem=sems[0].at[a, k],
            recv_sem=sems[1].at[a, k], device_id=to, device_id_type=MESH)

    def _mine(self, ins, outs, sems):
        x, y, c = _pos()
        return [pltpu.make_async_copy(ins[a], outs[a].at[4 * x + 2 * y + c], sems[2].at[a])
                for a in range(len(self.arrs))]

    def _split(self, a):
        return self._halves(a)[1][1] > 0

    def _sends(self, ins, outs, sems, phase):
        x, y, c = _pos()
        me, sib, xn, yn, dg = (x, y, c), (x, y, 1 - c), (1 - x, y, c), (x, 1 - y, c), (1 - x, 1 - y, c)
        cps = []
        for a in range(len(self.arrs)):
            cp = lambda k, block, to, **kw: self._copy(ins, outs, sems, a, k, block, to, **kw)
            if phase == 0:
                cps += [cp(self.SIB, me, sib, own=True), cp(self.XN, me, xn, own=True), cp(self.YN, me, yn, own=True)]
            elif phase == 1:
                cps.append(cp(self.FWD_Y, xn, yn, part=0))
                if self._split(a):
                    cps.append(cp(self.FWD_X, yn, xn, part=1))
                cps += [cp(self.SIB_X, xn, sib), cp(self.SIB_Y, yn, sib)]
            else:
                cps.append(cp(self.SIB_D, dg, sib))
        return cps

    def _arrivals(self, ins, outs, sems, phase):
        x, y, c = _pos()
        me, xn, yn, dg = (x, y, c), (1 - x, y, c), (x, 1 - y, c), (1 - x, 1 - y, c)
        cps = []
        for a in range(len(self.arrs)):
            cp = lambda k, block, **kw: self._copy(ins, outs, sems, a, k, block, me, **kw)
            if phase == 1:
                cps += [cp(self.XN, xn), cp(self.YN, yn)]
            elif phase == 2:
                cps.append(cp(self.FWD_Y, dg, part=0))
                if self._split(a):
                    cps.append(cp(self.FWD_X, dg, part=1))
            else:
                cps += [cp(self.SIB, (x, y, 1 - c)), cp(self.SIB_X, (1 - x, y, 1 - c)),
                        cp(self.SIB_Y, (x, 1 - y, 1 - c)), cp(self.SIB_D, (1 - x, 1 - y, 1 - c))]
        return cps

    def start(self, ins, outs, sems):
        for cp in self._mine(ins, outs, sems) + self._sends(ins, outs, sems, 0):
            cp.start()

    def mid(self, ins, outs, sems):
        for cp in self._arrivals(ins, outs, sems, 1):
            cp.wait_recv()
        for cp in self._sends(ins, outs, sems, 1):
            cp.start()

    def mid2(self, ins, outs, sems):
        for cp in self._arrivals(ins, outs, sems, 2):
            cp.wait_recv()
        for cp in self._sends(ins, outs, sems, 2):
            cp.start()

    def finish(self, ins, outs, sems):
        for cp in self._arrivals(ins, outs, sems, 3):
            cp.wait_recv()
        for ph in range(3):
            for cp in self._sends(ins, outs, sems, ph):
                cp.wait_send()
        for cp in self._mine(ins, outs, sems):
            cp.wait()


def sc_comm(g, collective_id, name):
    srcs = [jax.new_ref(a, memory_space=pltpu.MemorySpace.HBM) for a in g.arrs]
    dsts = [jax.empty_ref(s, memory_space=pltpu.MemorySpace.HBM) for s in g.out_shape]

    @pl.kernel(mesh=plsc.ScalarSubcoreMesh(axis_name="sequencer", num_cores=1), name=name,
               scratch_types=tuple(g.scratch), compiler_params=pltpu.CompilerParams(collective_id=collective_id))
    def launch(*sems):
        peers = g.peers(*_pos())
        barrier = pltpu.get_barrier_semaphore()
        for peer in peers:
            pl.semaphore_signal(barrier, inc=1, device_id=peer, device_id_type=MESH)
        pl.semaphore_wait(barrier, len(peers))
        g.start(srcs, dsts, sems)
        for _, phase in g.mid_steps(1):
            phase(srcs, dsts, sems)
        g.finish(srcs, dsts, sems)

    launch()
    return [d[...] for d in dsts]


def sc_gather(arrs, collective_id, name):
    return sc_comm(BgGather(arrs), collective_id, name)


class BgPair:
    def __init__(self, arrs):
        n = len(arrs)
        self.arrs = list(arrs)
        self.out_shape = [SDS((4,) + a.shape[1:], a.dtype) for a in arrs]
        self.scratch = [pltpu.SemaphoreType.DMA((n, 4)), pltpu.SemaphoreType.DMA((n, 4))]
        self.result = None

    @staticmethod
    def peers(x, y, c):
        return [(x, y, 1 - c)]

    def mid_steps(self, nsteps):
        return []

    def _copies(self, ins, outs, sems):
        x, y, c = _pos()
        return [pltpu.make_async_remote_copy(
            src_ref=ins[a].at[2 * k + 1 - c], dst_ref=outs[a].at[k], send_sem=sems[0].at[a, k],
            recv_sem=sems[1].at[a, k], device_id=(x, y, 1 - c), device_id_type=MESH)
            for a in range(len(self.arrs)) for k in range(4)]

    def start(self, ins, outs, sems):
        for cp in self._copies(ins, outs, sems):
            cp.start()

    def finish(self, ins, outs, sems):
        cps = self._copies(ins, outs, sems)
        for cp in cps:
            cp.wait_recv()
        for cp in cps:
            cp.wait_send()


class BgChips(BgPair):
    def __init__(self, arrs):
        n = len(arrs)
        self.arrs = list(arrs)
        self.out_shape = [SDS((3,) + a.shape[1:], a.dtype) for a in arrs]
        self.scratch = [pltpu.SemaphoreType.DMA((n, 3)), pltpu.SemaphoreType.DMA((n, 3))]
        self.result = None

    @staticmethod
    def peers(x, y, c):
        return [(px, py, c) for px, py in _other_chips(x, y)]

    def _copies(self, ins, outs, sems):
        x, y, c = _pos()
        return [pltpu.make_async_remote_copy(
            src_ref=ins[a].at[2 * px + py], dst_ref=outs[a].at[r], send_sem=sems[0].at[a, r],
            recv_sem=sems[1].at[a, r], device_id=(px, py, c), device_id_type=MESH)
            for a in range(len(self.arrs)) for r, (px, py) in enumerate(_other_chips(x, y))]


def _call(bgs, body, *, name, grid, in_specs, out_specs, out_shape, scratch_shapes=(), compiler_params=None):
    single = not isinstance(out_shape, (list, tuple))
    out_specs_l = [out_specs] if single else list(out_specs)
    out_shape_l = [out_shape] if single else list(out_shape)
    bgs = [b for b in (bgs or []) if b is not None]
    n_in, n_out, n_sc = len(in_specs), len(out_shape_l), len(scratch_shapes)
    nsteps = math.prod(grid)
    b_in_specs = [b.in_specs(grid) if hasattr(b, "in_specs") else [_ANY] * len(b.arrs) for b in bgs]
    b_out_specs = [b.out_specs(grid) if hasattr(b, "out_specs") else [_ANY] * len(b.out_shape) for b in bgs]
    aliases, i_off, o_off = {}, n_in, n_out
    for b in bgs:
        aliases.update({i_off + i: o_off + o for i, o in getattr(b, "aliases", {}).items()})
        i_off, o_off = i_off + len(b.arrs), o_off + len(b.out_shape)

    def full(*refs):
        pos = [0]

        def take(k):
            r = refs[pos[0]:pos[0] + k]
            pos[0] += k
            return r

        ins = take(n_in)
        b_ins = [take(len(b.arrs)) for b in bgs]
        outs = take(n_out)
        b_outs = [take(len(b.out_shape)) for b in bgs]
        sc = take(n_sc)
        b_sc = [take(len(b.scratch)) for b in bgs]
        if bgs:
            step = pl.program_id(0)
            for d in range(1, len(grid)):
                step = step * grid[d] + pl.program_id(d)

            @pl.when(step == 0)
            def _():
                for b, i_, o_, s_ in zip(bgs, b_ins, b_outs, b_sc):
                    b.start(i_, o_, s_)

        body(*ins, *outs, *sc)
        if bgs:
            for b, i_, o_, s_ in zip(bgs, b_ins, b_outs, b_sc):
                if hasattr(b, "step"):
                    b.step(i_, o_, s_)
                for at, fn in b.mid_steps(nsteps):
                    @pl.when(step == at)
                    def _():
                        fn(i_, o_, s_)

            @pl.when(step == nsteps - 1)
            def _():
                for b, i_, o_, s_ in zip(bgs, b_ins, b_outs, b_sc):
                    b.finish(i_, o_, s_)

    def run(*args):
        res = pl.pallas_call(
            full, name=name, grid=grid,
            in_specs=list(in_specs) + [s for l in b_in_specs for s in l],
            out_specs=out_specs_l + [s for l in b_out_specs for s in l],
            out_shape=out_shape_l + [s for b in bgs for s in b.out_shape],
            scratch_shapes=list(scratch_shapes) + [s for b in bgs for s in b.scratch],
            input_output_aliases=aliases,
            compiler_params=compiler_params,
        )(*args, *[a for b in bgs for a in b.arrs])
        rest = list(res[n_out:])
        for b in bgs:
            b.result, rest = rest[:len(b.out_shape)], rest[len(b.out_shape):]
        return res[0] if single else list(res[:n_out])

    return run


def s5_discretize(a_re, a_im, log_dt, b_re, b_im, c_re, c_im):
    lam_r = jnp.minimum(a_re, DT_MIN_LAMBDA)
    lam_i = a_im
    dt = jnp.exp(log_dt)[:, None]
    e = jnp.exp(lam_r * dt)
    lbr = e * jnp.cos(lam_i * dt)
    lbi = e * jnp.sin(lam_i * dt)
    den = lam_r * lam_r + lam_i * lam_i
    cf_r = ((lbr - 1.0) * lam_r + lbi * lam_i) / den
    cf_i = (lbi * lam_r - (lbr - 1.0) * lam_i) / den
    bb_r = cf_r[:, :, None] * b_re - cf_i[:, :, None] * b_im
    bb_i = cf_r[:, :, None] * b_im + cf_i[:, :, None] * b_re
    eye = jnp.eye(8, dtype=f32)

    def blk_b(m):
        return jnp.einsum('bgpc,gh->bgchp', m.reshape(8, 8, S5_P, S5_C), eye).reshape(8, 128, 512)

    def blk_c(m):
        return jnp.einsum('bgcp,gh->bgphc', m.reshape(8, 8, S5_C, S5_P), eye).reshape(8, 512, 128)

    bm = jnp.concatenate([blk_b(bb_r), blk_b(bb_i)], axis=-1)
    cm = jnp.concatenate([blk_c(c_re), -blk_c(c_im)], axis=1)
    lam = jnp.stack([lbr.reshape(8, 512), lbi.reshape(8, 512)], axis=1)
    lam = jnp.broadcast_to(lam[:, :, None, :], (8, 2, 8, 512))
    return lam, bm, cm


def _cmul(ar, ai, br, bi):
    return ar * br - ai * bi, ar * bi + ai * br


def _shift_rows(v, k, up):
    row = lax.broadcasted_iota(jnp.int32, v.shape, 0)
    if up:
        return jnp.where(row < 8 - k, pltpu.roll(v, 8 - k, 0), 0.0)
    return jnp.where(row >= k, pltpu.roll(v, k, 0), 0.0)


_ROWS = 256


def s5_core_fwd(hn, bm, lam, cm, bg=()):
    nt = T // _ROWS

    def body(u_ref, b_ref, lam_ref, c_ref, ys_ref, S):
        lr, li = lam_ref[0], lam_ref[1]
        z = jnp.zeros((8, 512), f32)
        tile = lambda k: pl.ds(k * _ROWS, _ROWS)
        c = (z, z)
        for k in range(nt):
            S[tile(k), :] = _dot(_rows_in(u_ref, k).astype(bf16), b_ref[...])
            if k >= 1:
                c = _scan_tile(S, lr, li, k - 1, c, False, False)
        c = _scan_tile(S, lr, li, nt - 1, c, False, False)
        c = _chunk_starts(c[0], c[1], lr, li, False)
        for k in range(nt):
            c = _scan_tile(S, lr, li, k, c, False, True)
            if k >= 1:
                _rows_out(ys_ref, k - 1, _dot(S[tile(k - 1), :].astype(bf16), c_ref[...]))
        _rows_out(ys_ref, nt - 1, _dot(S[tile(nt - 1), :].astype(bf16), c_ref[...]))

    return _call(
        bg, body, name="s5_core_fwd", grid=(S5_SUB,),
        in_specs=[pl.BlockSpec((T, 128), lambda b: (0, b)),
                  pl.BlockSpec((None, 128, 1024), lambda b: (b, 0, 0)),
                  pl.BlockSpec((None, 4, 8, 512), lambda b: (b, 0, 0, 0)),
                  pl.BlockSpec((None, 1024, 128), lambda b: (b, 0, 0))],
        out_specs=[pl.BlockSpec((T, 128), lambda b: (0, b)), pl.BlockSpec((T, 1024), lambda b: (0, b))],
        out_shape=[SDS((T, D), f32), SDS((T, S5_SUB * 1024), f32)],
        compiler_params=_cp(dimension_semantics=("arbitrary",)),
    )(hn, bm, lam, cm)


_SEG = _ROWS // S5_CH


def _rows_in(ref, k):
    return jnp.concatenate([ref[pl.ds(s, S5_CH, stride=S5_STEPS), :] for s in range(k * _SEG, (k + 1) * _SEG)], axis=0)


def _rows_out(ref, k, val):
    for j, s in enumerate(range(k * _SEG, (k + 1) * _SEG)):
        ref[pl.ds(s, S5_CH, stride=S5_STEPS), :] = val[j * S5_CH:(j + 1) * S5_CH, :]


def _scan_tile(S, lr, li, k, carry, reverse, store, aux=None):
    steps = range(k * _SEG, (k + 1) * _SEG)
    for s in (reversed(steps) if reverse else steps):
        row = pl.ds(s * 8, 8)
        xr, xi = carry[0], carry[1]
        nr = lr * xr - li * xi + S[row, 0:512]
        ni = lr * xi + li * xr + S[row, 512:1024]
        if store:
            S[row, 0:512] = nr
            S[row, 512:1024] = ni
        if aux is not None and s >= 1:
            prow = pl.ds((s - 1) * 8, 8)
            pr, pi_ = aux[prow, 0:512], aux[prow, 512:1024]
            carry = (nr, ni, carry[2] + nr * pr + ni * pi_, carry[3] + ni * pr - nr * pi_)
        elif aux is not None:
            carry = (nr, ni, carry[2], carry[3])
        else:
            carry = (nr, ni)
    return carry


def _chunk_starts(er, ei, lr, li, reverse):
    ar, ai = lr, li
    for _ in range(8):
        ar, ai = _cmul(ar, ai, ar, ai)
    cr, ci = _shift_rows(er, 1, reverse), _shift_rows(ei, 1, reverse)
    for k in (1, 2, 4):
        sr, si = _shift_rows(cr, k, reverse), _shift_rows(ci, k, reverse)
        pr, pi_ = _cmul(ar, ai, sr, si)
        cr, ci = cr + pr, ci + pi_
        ar, ai = _cmul(ar, ai, ar, ai)
    return cr, ci


def s5_core_bwd(hn, dy, xs, bm, lam, cm, bg=()):
    nt = T // _ROWS

    def body(u_ref, dy_ref, S1, b_ref, lam_ref, c_ref, du_ref, db_ref, dct_ref, dlam_ref, S2):
        lcr, lci = lam_ref[2], lam_ref[3]
        z = jnp.zeros((8, 512), f32)
        tile = lambda k: pl.ds(k * _ROWS, _ROWS)

        def dx(k):
            dyb = _rows_in(dy_ref, k).astype(bf16)
            S2[tile(k), :] = _dot_nt(dyb, c_ref[...])
            dct_ref[...] += _dot_tn(dyb, S1[tile(k), :].astype(bf16))

        dct_ref[...] = jnp.zeros_like(dct_ref)
        dx(nt - 1)
        c = (z, z)
        for k in range(nt - 1, -1, -1):
            if k >= 1:
                dx(k - 1)
            c = _scan_tile(S2, lcr, lci, k, c, True, False)

        def dbu(k):
            gb = S2[tile(k), :].astype(bf16)
            db_ref[...] += _dot_tn(_rows_in(u_ref, k).astype(bf16), gb)
            _rows_out(du_ref, k, _dot_nt(gb, b_ref[...]))

        c = _chunk_starts(c[0], c[1], lcr, lci, True) + (z, z)
        db_ref[...] = jnp.zeros_like(db_ref)
        for k in range(nt - 1, -1, -1):
            c = _scan_tile(S2, lcr, lci, k, c, True, True, aux=S1)
            if k + 1 < nt:
                dbu(k + 1)
        dbu(0)
        gr, gi, dr, di = c
        last = pl.ds((S5_STEPS - 1) * 8, 8)
        xr = _shift_rows(S1[last, 0:512], 1, False)
        xi = _shift_rows(S1[last, 512:1024], 1, False)
        dlam_ref[0] = dr + gr * xr + gi * xi
        dlam_ref[1] = di + gi * xr - gr * xi

    return _call(
        bg, body, name="s5_core_bwd", grid=(S5_SUB,),
        in_specs=[pl.BlockSpec((T, 128), lambda b: (0, b)),
                  pl.BlockSpec((T, 128), lambda b: (0, b)),
                  pl.BlockSpec((T, 1024), lambda b: (0, b)),
                  pl.BlockSpec((None, 128, 1024), lambda b: (b, 0, 0)),
                  pl.BlockSpec((None, 4, 8, 512), lambda b: (b, 0, 0, 0)),
                  pl.BlockSpec((None, 1024, 128), lambda b: (b, 0, 0))],
        out_specs=[pl.BlockSpec((T, 128), lambda b: (0, b)),
                   pl.BlockSpec((None, 128, 1024), lambda b: (b, 0, 0)),
                   pl.BlockSpec((None, 128, 1024), lambda b: (b, 0, 0)),
                   pl.BlockSpec((None, 2, 8, 512), lambda b: (b, 0, 0, 0))],
        out_shape=[SDS((T, D), f32), SDS((8, 128, 1024), f32), SDS((8, 128, 1024), f32), SDS((8, 2, 8, 512), f32)],
        scratch_shapes=[pltpu.VMEM((T, 1024), f32)],
        compiler_params=_cp(dimension_semantics=("arbitrary",)),
    )(hn, dy, xs, bm, lam, cm)


TM = 512
NT = T // TM


def _tile(n=D):
    return pl.BlockSpec((TM, n), lambda i: (i, 0))


def s5_pre(xp, g):
    def body(x_ref, g_ref, hn_ref):
        hn_ref[...] = _rms(x_ref[...], g_ref[...])[0]

    return pl.pallas_call(
        body, name="s5_pre", grid=(NT,), in_specs=[_tile(), _full((1, D))], out_specs=_tile(),
        out_shape=SDS((T, D), f32), compiler_params=_cp(dimension_semantics=("arbitrary",)),
    )(xp, g)


def _gelu_grad(y):
    c = math.sqrt(2.0 / math.pi)
    t = jnp.tanh(c * (y + 0.044715 * y * y * y))
    return 0.5 * (1.0 + t) + 0.5 * y * (1.0 - t * t) * c * (1.0 + 3.0 * 0.044715 * y * y)


def s5_post(ys, xp, g, d, wglu, bglu, bg=()):
    def body(ys_ref, x_ref, g_ref, d_ref, w_ref, b_ref, y_ref, z_ref, h_ref):
        x = x_ref[...]
        hn, _ = _rms(x, g_ref[...])
        y = ys_ref[...] + d_ref[...] * hn
        y_ref[...] = y
        yg = jax.nn.gelu(y).astype(bf16)
        for j in range(4):
            cv = slice(j * 256, (j + 1) * 256)
            cg = slice(1024 + j * 256, 1024 + (j + 1) * 256)
            val = _dot(yg, w_ref[j]) + b_ref[:, cv]
            gate = _dot(yg, w_ref[j + 4]) + b_ref[:, cg]
            z_ref[:, cv] = val
            z_ref[:, cg] = gate
            h_ref[:, cv] = x[:, cv] + val * jax.nn.sigmoid(gate)

    return _call(
        bg, body, name="s5_post", grid=(NT,),
        in_specs=[_tile(), _tile(), _full((1, D)), _full((1, D)), _full((8, D, 256)), _full((1, 2 * D))],
        out_specs=[_tile(), _tile(2 * D), _tile()],
        out_shape=[SDS((T, D), f32), SDS((T, 2 * D), f32), SDS((T, D), f32)],
        compiler_params=_cp(dimension_semantics=("arbitrary",)),
    )(ys, xp, g, d, wglu, bglu)


def s5_post_bwd(dh, y, z, wglu, bg=()):
    def body(dh_ref, y_ref, z_ref, w_ref, dy_ref, dw_ref, db_ref, ygs, dzs):
        i = pl.program_id(0)
        rows = pl.ds(pl.multiple_of(i * TM, TM), TM)

        @pl.when(i == 0)
        def _():
            db_ref[...] = jnp.zeros_like(db_ref)

        dh_ = dh_ref[...]
        y = y_ref[...]
        ygs[rows, :] = jax.nn.gelu(y).astype(bf16)
        dyg = jnp.zeros((TM, D), f32)
        for j in range(4):
            cv = slice(j * 256, (j + 1) * 256)
            cg = slice(1024 + j * 256, 1024 + (j + 1) * 256)
            val = z_ref[:, cv]
            sg = jax.nn.sigmoid(z_ref[:, cg])
            dval = dh_[:, cv] * sg
            dgate = dh_[:, cv] * val * sg * (1.0 - sg)
            db_ref[:, cv] += _colsum8(dval)
            db_ref[:, cg] += _colsum8(dgate)
            dvb = dval.astype(bf16)
            dgb = dgate.astype(bf16)
            dzs[rows, cv] = dvb
            dzs[rows, cg] = dgb
            dyg = dyg + _dot_nt(dvb, w_ref[j]) + _dot_nt(dgb, w_ref[j + 4])
        dy_ref[...] = dyg * _gelu_grad(y)

        @pl.when(i == NT - 1)
        def _():
            for half in range(2):
                dw = _dot_tn(ygs[...], dzs[:, half * D:(half + 1) * D])
                for j in range(4):
                    dw_ref[4 * half + j] = dw[:, j * 256:(j + 1) * 256].astype(bf16)

    return _call(
        bg, body, name="s5_post_bwd", grid=(NT,),
        in_specs=[_tile(), _tile(), _tile(2 * D), _full((8, D, 256))],
        out_specs=[_tile(), _full((8, D, 256)), _full((8, 2 * D))],
        out_shape=[SDS((T, D), f32), SDS((8, D, 256), bf16), SDS((8, 2 * D), f32)],
        scratch_shapes=[pltpu.VMEM((T, D), bf16), pltpu.VMEM((T, 2 * D), bf16)],
        compiler_params=_cp(dimension_semantics=("arbitrary",)),
    )(dh, y, z, wglu)


def s5_pre_bwd(xp, g, du, dy, d, dh, bg=()):
    def body(x_ref, g_ref, du_ref, dy_ref, d_ref, dh_ref, dx_ref, dg_ref, dd_ref):
        i = pl.program_id(0)

        @pl.when(i == 0)
        def _():
            dg_ref[...] = jnp.zeros_like(dg_ref)
            dd_ref[...] = jnp.zeros_like(dd_ref)

        x = x_ref[...]
        g = g_ref[...]
        dy = dy_ref[...]
        hn, _ = _rms(x, g)
        dhn = du_ref[...] + d_ref[...] * dy
        dx, dgt = _rms_bwd(x, g, dhn)
        dx_ref[...] = dh_ref[...] + dx
        dg_ref[...] += _colsum8(dgt)
        dd_ref[...] += _colsum8(dy * hn)

    return _call(
        bg, body, name="s5_pre_bwd", grid=(NT,),
        in_specs=[_tile(), _full((1, D)), _tile(), _tile(), _full((1, D)), _tile()],
        out_specs=[_tile(), _full((8, D)), _full((8, D))],
        out_shape=[SDS((T, D), f32), SDS((8, D), f32), SDS((8, D), f32)],
        compiler_params=_cp(dimension_semantics=("arbitrary",)),
    )(xp, g, du, dy, d, dh)


TMF = 1024


def mlp_fwd(h, g, w_in, w_out, layer, bg=()):
    def body(h_ref, g_ref, wi_ref, wo_ref, hm_ref, r_ref, out_ref, acc):
        j = pl.program_id(1)

        @pl.when(j == 0)
        def _():
            hm, _ = _rms(h_ref[...], g_ref[...])
            hm_ref[...] = hm.astype(bf16)
            acc[...] = jnp.zeros_like(acc)

        a = jnp.maximum(_dot(hm_ref[...], wi_ref[...]), 0.0)
        r_ref[...] = a.astype(bf16)
        acc[...] += _dot((a * a).astype(bf16), wo_ref[...])

        @pl.when(j == NDEV - 1)
        def _():
            out_ref[...] = h_ref[...] + acc[...]

    return _call(
        bg, body, name=f"mlp_fwd{layer}", grid=(T // TMF, NDEV),
        in_specs=[pl.BlockSpec((TMF, D), lambda i, j: (i, 0)),
                  pl.BlockSpec((1, D), lambda i, j: (0, 0)),
                  pl.BlockSpec((None, D, D_FF_SHARD), lambda i, j: (j, 0, 0)),
                  pl.BlockSpec((None, D_FF_SHARD, D), lambda i, j: (j, 0, 0))],
        out_specs=[pl.BlockSpec((TMF, D), lambda i, j: (i, 0)), pl.BlockSpec((TMF, D_FF_SHARD), lambda i, j: (i, j)),
                   pl.BlockSpec((TMF, D), lambda i, j: (i, 0))],
        out_shape=[SDS((T, D), bf16), SDS((T, NDEV * D_FF_SHARD), bf16), SDS((T, D), f32)],
        scratch_shapes=[pltpu.VMEM((TMF, D), f32)],
        compiler_params=_cp(dimension_semantics=("arbitrary", "arbitrary")),
    )(h, g, w_in, w_out)


def mlp_bwd(h, hm, r, g, dout, dout_b, w_in, w_out, layer, bg=()):
    def body(h_ref, hm_ref, r_ref, g_ref, do_ref, dob_ref, wi_ref, wo_ref, dh_ref, dwi_ref, dwo_ref, dg_ref,
             dhm, dzs):
        s = pl.program_id(0)

        @pl.when(s == 0)
        def _():
            dhm[...] = jnp.zeros_like(dhm)

        @pl.when(s < NDEV)
        def _():
            for c in range(NT):
                rows = pl.ds(c * TM, TM)
                dz = (_dot_nt(dob_ref[rows, :], wo_ref[...]) * (2.0 * r_ref[rows, :].astype(f32))).astype(bf16)
                dzs[rows, :] = dz
                dhm[rows, :] += _dot_nt(dz, wi_ref[...])
            rb = r_ref[...]
            dwo_ref[...] = _dot_tn(rb * rb, dob_ref[...]).astype(bf16)
            dwi_ref[...] = _dot_tn(hm_ref[...], dzs[...]).astype(bf16)

        @pl.when(s >= NDEV)
        def _():
            @pl.when(s == NDEV)
            def _():
                dg_ref[...] = jnp.zeros_like(dg_ref)
            rows = pl.ds(pl.multiple_of((s - NDEV) * TM, TM), TM)
            dx, dgt = _rms_bwd(h_ref[...], g_ref[...], dhm[rows, :])
            dh_ref[...] = do_ref[...] + dx
            dg_ref[...] += _colsum8(dgt)

    shard = lambda s: (jnp.minimum(s, NDEV - 1), 0, 0)
    tile = lambda s: (jnp.maximum(s - NDEV, 0), 0)
    return _call(
        bg, body, name=f"mlp_bwd{layer}", grid=(NDEV + NT,),
        in_specs=[pl.BlockSpec((TM, D), tile),
                  _full((T, D)),
                  pl.BlockSpec((T, D_FF_SHARD), lambda s: (0, jnp.minimum(s, NDEV - 1))),
                  _full((1, D)),
                  pl.BlockSpec((TM, D), tile),
                  _full((T, D)),
                  pl.BlockSpec((None, D, D_FF_SHARD), shard),
                  pl.BlockSpec((None, D_FF_SHARD, D), shard)],
        out_specs=[pl.BlockSpec((TM, D), tile),
                   pl.BlockSpec((None, D, D_FF_SHARD), shard),
                   pl.BlockSpec((None, D_FF_SHARD, D), shard),
                   pl.BlockSpec((8, D), lambda s: (0, 0))],
        out_shape=[SDS((T, D), f32), SDS((NDEV, D, D_FF_SHARD), bf16), SDS((NDEV, D_FF_SHARD, D), bf16),
                   SDS((8, D), f32)],
        scratch_shapes=[pltpu.VMEM((T, D), f32), pltpu.VMEM((T, D_FF_SHARD), bf16)],
        compiler_params=_cp(dimension_semantics=("arbitrary",)),
    )(h, hm, r, g, dout, dout_b, w_in, w_out)


def _spread4():
    r = lax.broadcasted_iota(jnp.int32, (256, D), 0)
    c = lax.broadcasted_iota(jnp.int32, (256, D), 1)
    return ((c // 256 == r // HEAD_DIM) & (c % HEAD_DIM == r % HEAD_DIM)).astype(bf16)


def attn_pre(h, g_kv, g_mix, wkv, bkv, spread, wq, bq):
    def body(h_ref, gkv_ref, gm_ref, wkv_ref, bkv_ref, sp_ref, wq_ref, bq_ref, kvn_ref, hn_ref, k_ref, v_ref, q_ref):
        h_ = h_ref[...]
        kvn = _rms(h_, gkv_ref[...])[0].astype(bf16)
        hn = _rms(h_, gm_ref[...])[0].astype(bf16)
        kvn_ref[...] = kvn
        hn_ref[...] = hn
        kv = (_dot(kvn, wkv_ref[...]) + bkv_ref[...]).astype(bf16)
        k_ref[...] = _dot(kv[:, :256], sp_ref[...]).astype(bf16)
        v_ref[...] = _dot(kv[:, 256:], sp_ref[...]).astype(bf16)
        q_ref[...] = (_dot(hn, wq_ref[...]) + bq_ref[...]).astype(bf16)

    return pl.pallas_call(
        body, name="attn_pre", grid=(NT,),
        in_specs=[_tile(), _full((1, D)), _full((1, D)), _full((D, 512)), _full((1, 512)), _full((256, D)),
                  _full((D, D)), _full((1, D))],
        out_specs=[_tile()] * 5,
        out_shape=[SDS((T, D), bf16)] * 5,
        compiler_params=_cp(dimension_semantics=("arbitrary",)),
    )(h, g_kv, g_mix, wkv, bkv, spread, wq, bq)


def _attn_specs():
    cur = pl.BlockSpec((TM, 256), lambda j, n: (n, j))
    prev = pl.BlockSpec((BLK, 256), lambda j, n: (jnp.maximum(n * (TM // BLK) - 1, 0), j))
    return cur, prev


def _head_mask(g):
    lane = lax.broadcasted_iota(jnp.int32, (1, 256), 1)
    return (lane >= g * HEAD_DIM) & (lane < (g + 1) * HEAD_DIM)


def _stack_heads(t):
    return jnp.concatenate([jnp.where(_head_mask(g), t, 0) for g in range(Q_PER_KV)], axis=0)


def _unstack_heads(t):
    out = jnp.where(_head_mask(0), t[0:BLK], 0.0)
    for g in range(1, Q_PER_KV):
        out = out + jnp.where(_head_mask(g), t[g * BLK:(g + 1) * BLK], 0.0)
    return out


def _attn_probs(qs, k2, sinks, first):
    rows = Q_PER_KV * BLK
    s = _dot_nt(qs, k2) * (1.0 / math.sqrt(HEAD_DIM))
    qi = jnp.bitwise_and(lax.broadcasted_iota(jnp.int32, (rows, 2 * BLK), 0), BLK - 1)
    kj = lax.broadcasted_iota(jnp.int32, (rows, 2 * BLK), 1)
    diff = qi + BLK - kj
    valid = (diff >= 0) & (diff < BLK) & (jnp.logical_not(first) | (kj >= BLK))
    s = jnp.where(valid, s, -jnp.inf)
    rb = lax.broadcasted_iota(jnp.int32, (rows, 1), 0)
    sink = jnp.where(rb < BLK, sinks[0], jnp.where(rb < 2 * BLK, sinks[1], jnp.where(rb < 3 * BLK, sinks[2], sinks[3])))
    m = jnp.maximum(jnp.max(s, axis=-1, keepdims=True), sink)
    p = jnp.exp(s - m)
    ps = jnp.exp(sink - m)
    denom = jnp.sum(p, axis=-1, keepdims=True) + ps
    return p / denom, ps / denom


def _window_blocks(b, n, kc_ref, kp_ref, vc_ref, vp_ref):
    if b == 0:
        return (jnp.concatenate([kp_ref[...], kc_ref[0:BLK, :]], axis=0),
                jnp.concatenate([vp_ref[...], vc_ref[0:BLK, :]], axis=0), n == 0)
    rows = pl.ds((b - 1) * BLK, 2 * BLK)
    return kc_ref[rows, :], vc_ref[rows, :], False


def attn_core_fwd(q, k4, v4, sinks, bg=()):
    nb = TM // BLK

    def body(sink_ref, q_ref, kc_ref, kp_ref, vc_ref, vp_ref, o_ref, a_ref, as_ref):
        j = pl.program_id(0)
        n = pl.program_id(1)
        sk = [sink_ref[j * Q_PER_KV + g] for g in range(Q_PER_KV)]
        for b in range(nb):
            qb = q_ref[b * BLK:(b + 1) * BLK, :]
            k2, v2, first = _window_blocks(b, n, kc_ref, kp_ref, vc_ref, vp_ref)
            a, asink = _attn_probs(_stack_heads(qb), k2, sk, first)
            ab = a.astype(bf16)
            a_ref[b] = ab
            as_ref[b] = jnp.broadcast_to(asink, (Q_PER_KV * BLK, 128)).astype(bf16)
            o_ref[b * BLK:(b + 1) * BLK, :] = _unstack_heads(_dot(ab, v2)).astype(bf16)

    cur, prev = _attn_specs()
    rows = Q_PER_KV * BLK
    return _call(
        bg, body, name="attn_core_fwd", grid=(N_KV, NT),
        in_specs=[pl.BlockSpec(memory_space=pltpu.SMEM), cur, cur, prev, cur, prev],
        out_specs=[cur, pl.BlockSpec((None, nb, rows, 2 * BLK), lambda j, n: (j, n, 0, 0)),
                   pl.BlockSpec((None, nb, rows, 128), lambda j, n: (j, n, 0, 0))],
        out_shape=[SDS((T, D), bf16), SDS((N_KV, T // BLK, rows, 2 * BLK), bf16), SDS((N_KV, T // BLK, rows, 128), bf16)],
        compiler_params=_cp(dimension_semantics=("arbitrary", "arbitrary")),
    )(sinks, q, k4, k4, v4, v4)


def attn_post(h, o, wo, bo):
    def body(h_ref, o_ref, w_ref, b_ref, out_ref):
        out_ref[...] = h_ref[...] + _dot(o_ref[...], w_ref[...]) + b_ref[...]

    return pl.pallas_call(
        body, name="attn_post", grid=(NT,), in_specs=[_tile(), _tile(), _full((D, D)), _full((1, D))],
        out_specs=_tile(), out_shape=SDS((T, D), f32), compiler_params=_cp(dimension_semantics=("arbitrary",)),
    )(h, o, wo, bo)


def attn_bwd_pre(dh, o, wo, bg=()):
    def body(dh_ref, o_ref, w_ref, do_ref, dw_ref, db_ref, acc):
        i = pl.program_id(0)

        @pl.when(i == 0)
        def _():
            acc[...] = jnp.zeros_like(acc)
            db_ref[...] = jnp.zeros_like(db_ref)

        dh_ = dh_ref[...]
        dhb = dh_.astype(bf16)
        do_ref[...] = _dot_nt(dhb, w_ref[...]).astype(bf16)
        acc[...] += _dot_tn(o_ref[...], dhb)
        db_ref[...] += _colsum8(dh_)

        @pl.when(i == NT - 1)
        def _():
            dw_ref[...] = acc[...].astype(bf16)

    return _call(
        bg, body, name="attn_bwd_pre", grid=(NT,), in_specs=[_tile(), _tile(), _full((D, D))],
        out_specs=[_tile(), _full((D, D)), _full((8, D))],
        out_shape=[SDS((T, D), bf16), SDS((D, D), bf16), SDS((8, D), f32)],
        scratch_shapes=[pltpu.VMEM((D, D), f32)],
        compiler_params=_cp(dimension_semantics=("arbitrary",)),
    )(dh, o, wo)


def attn_core_bwd(q, do, k4, v4, probs, sink_w, bg=()):
    nb = TM // BLK

    def body(q_ref, do_ref, kc_ref, kp_ref, vc_ref, vp_ref, a_ref, as_ref, dq_ref, dk_ref, dv_ref, ds_ref):
        j = pl.program_id(0)
        n = pl.program_id(1)

        @pl.when(n == 0)
        def _():
            dk_ref[...] = jnp.zeros_like(dk_ref)
            dv_ref[...] = jnp.zeros_like(dv_ref)
            ds_ref[...] = jnp.zeros_like(ds_ref)

        lane8 = lax.broadcasted_iota(jnp.int32, (8, 128), 1)
        row8 = lax.broadcasted_iota(jnp.int32, (8, 128), 0)
        for b in range(nb):
            qs = _stack_heads(q_ref[b * BLK:(b + 1) * BLK, :])
            dos = _stack_heads(do_ref[b * BLK:(b + 1) * BLK, :])
            k2, v2, _ = _window_blocks(b, n, kc_ref, kp_ref, vc_ref, vp_ref)
            ab = a_ref[b]
            a = ab.astype(f32)
            asink = as_ref[b][:, 0:1].astype(f32)
            dp = _dot_nt(dos, v2)
            dd = jnp.sum(a * dp, axis=-1, keepdims=True)
            dsc = (a * (dp - dd) * (1.0 / math.sqrt(HEAD_DIM))).astype(bf16)
            t = asink * dd
            for g in range(Q_PER_KV):
                dsink = -jnp.sum(t[g * BLK:(g + 1) * BLK], axis=0, keepdims=True)
                ds_ref[...] += jnp.where((lane8 == g) & (row8 == 0), jnp.broadcast_to(dsink, (8, 128)), 0.0)
            dq_ref[b * BLK:(b + 1) * BLK, :] = _unstack_heads(_dot(dsc, k2))
            dk2 = _dot_tn(dsc, qs)
            dv2 = _dot_tn(ab, dos)
            cur = pl.ds(pl.multiple_of(n * TM + b * BLK, BLK), BLK)
            dk_ref[cur, :] += dk2[BLK:, :]
            dv_ref[cur, :] += dv2[BLK:, :]
            if b == 0:
                @pl.when(n > 0)
                def _():
                    prv = pl.ds(pl.multiple_of(n * TM - BLK, BLK), BLK)
                    dk_ref[prv, :] += dk2[:BLK, :]
                    dv_ref[prv, :] += dv2[:BLK, :]
            else:
                prv = pl.ds(pl.multiple_of(n * TM + (b - 1) * BLK, BLK), BLK)
                dk_ref[prv, :] += dk2[:BLK, :]
                dv_ref[prv, :] += dv2[:BLK, :]

    cur, prev = _attn_specs()
    col = pl.BlockSpec((T, 256), lambda j, n: (0, j))
    rows = Q_PER_KV * BLK
    return _call(
        bg, body, name="attn_core_bwd", grid=(N_KV, NT),
        in_specs=[cur, cur, cur, prev, cur, prev,
                  pl.BlockSpec((None, nb, rows, 2 * BLK), lambda j, n: (j, n, 0, 0)),
                  pl.BlockSpec((None, nb, rows, 128), lambda j, n: (j, n, 0, 0))],
        out_specs=[cur, col, col, pl.BlockSpec((None, 8, 128), lambda j, n: (j, 0, 0))],
        out_shape=[SDS((T, D), f32), SDS((T, D), f32), SDS((T, D), f32), SDS((N_KV, 8, 128), f32)],
        compiler_params=_cp(dimension_semantics=("arbitrary", "arbitrary")),
    )(q, do, k4, k4, v4, v4, probs, sink_w)


def attn_bwd_q(h, dh, dq, hn, g_mix, wq):
    def body(h_ref, dh_ref, dq_ref, hn_ref, gm_ref, wq_ref, out_ref, dwq_ref, dbq_ref, dgm_ref, aq):
        i = pl.program_id(0)

        @pl.when(i == 0)
        def _():
            aq[...] = jnp.zeros_like(aq)
            dbq_ref[...] = jnp.zeros_like(dbq_ref)
            dgm_ref[...] = jnp.zeros_like(dgm_ref)

        dq_ = dq_ref[...]
        dqb = dq_.astype(bf16)
        aq[...] += _dot_tn(hn_ref[...], dqb)
        dbq_ref[...] += _colsum8(dq_)
        dx, dg = _rms_bwd(h_ref[...], gm_ref[...], _dot_nt(dqb, wq_ref[...]))
        out_ref[...] = dh_ref[...] + dx
        dgm_ref[...] += _colsum8(dg)

        @pl.when(i == NT - 1)
        def _():
            dwq_ref[...] = aq[...].astype(bf16)

    vec = _full((8, D))
    mat = _full((D, D))
    return pl.pallas_call(
        body, name="attn_bwd_q", grid=(NT,),
        in_specs=[_tile()] * 4 + [_full((1, D)), mat],
        out_specs=[_tile(), mat, vec, vec],
        out_shape=[SDS((T, D), f32), SDS((D, D), bf16), SDS((8, D), f32), SDS((8, D), f32)],
        scratch_shapes=[pltpu.VMEM((D, D), f32)],
        compiler_params=_cp(dimension_semantics=("arbitrary",)),
    )(h, dh, dq, hn, g_mix, wq)


def attn_bwd_kv(h, dh, dk4, dv4, kvn, g_kv, wkv, spread):
    def body(h_ref, dh_ref, dk_ref, dv_ref, kvn_ref, gkv_ref, wkv_ref, sp_ref, out_ref, outb_ref, dw_ref, db_ref,
             dgkv_ref, acc):
        i = pl.program_id(0)

        @pl.when(i == 0)
        def _():
            for r in (acc, db_ref, dgkv_ref):
                r[...] = jnp.zeros_like(r)

        dkv = jnp.concatenate([_dot_nt(dk_ref[...].astype(bf16), sp_ref[...]),
                               _dot_nt(dv_ref[...].astype(bf16), sp_ref[...])], axis=1)
        dkvb = dkv.astype(bf16)
        acc[...] += _dot_tn(kvn_ref[...], dkvb)
        db_ref[...] += _colsum8(dkv)
        dx, dg = _rms_bwd(h_ref[...], gkv_ref[...], _dot_nt(dkvb, wkv_ref[...]))
        out = dh_ref[...] + dx
        out_ref[...] = out
        outb_ref[...] = out.astype(bf16)
        dgkv_ref[...] += _colsum8(dg)

        @pl.when(i == NT - 1)
        def _():
            dw_ref[...] = acc[...].astype(bf16)

    return pl.pallas_call(
        body, name="attn_bwd_kv", grid=(NT,),
        in_specs=[_tile()] * 5 + [_full((1, D)), _full((D, 512)), _full((256, D))],
        out_specs=[_tile(), _tile(), _full((D, 512)), _full((8, 512)), _full((8, D))],
        out_shape=[SDS((T, D), f32), SDS((T, D), bf16), SDS((D, 512), bf16), SDS((8, 512), f32), SDS((8, D), f32)],
        scratch_shapes=[pltpu.VMEM((D, 512), f32)],
        compiler_params=_cp(dimension_semantics=("arbitrary",)),
    )(h, dh, dk4, dv4, kvn, g_kv, wkv, spread)


def final_loss(h, g, target):
    def body(h_ref, g_ref, t_ref, loss_ref, dh_ref, dhb_ref, dg_ref):
        i = pl.program_id(0)

        @pl.when(i == 0)
        def _():
            loss_ref[...] = jnp.zeros_like(loss_ref)
            dg_ref[...] = jnp.zeros_like(dg_ref)

        h_ = h_ref[...]
        g_ = g_ref[...]
        y, _ = _rms(h_, g_)
        diff = y - t_ref[...]
        per_tok = jnp.mean(diff * diff, axis=-1, keepdims=True)
        tot = 0.5 * jnp.sum(per_tok, axis=0, keepdims=True)
        lane = lax.broadcasted_iota(jnp.int32, (8, 128), 1)
        row = lax.broadcasted_iota(jnp.int32, (8, 128), 0)
        loss_ref[...] += jnp.where((lane == 0) & (row == 0), jnp.broadcast_to(tot, (8, 128)), 0.0)
        dx, dgt = _rms_bwd(h_, g_, diff * (1.0 / D))
        dh_ref[...] = dx
        dhb_ref[...] = dx.astype(bf16)
        dg_ref[...] += _colsum8(dgt)

    return pl.pallas_call(
        body, name="final_loss", grid=(NT,), in_specs=[_tile(), _full((1, D)), _tile()],
        out_specs=[_full((8, 128)), _tile(), _tile(), _full((8, D))],
        out_shape=[SDS((8, 128), f32), SDS((T, D), f32), SDS((T, D), bf16), SDS((8, D), f32)],
        compiler_params=_cp(dimension_semantics=("arbitrary",)),
    )(h, g, target)


def fwd_bwd(x, target, p, shards, opt, core, chip):
    row = lambda v: v.reshape(1, -1)
    (lam, bm, cm), prep_vjp = jax.vjp(s5_discretize, p["s5_a_re"][0], p["s5_a_im"][0], p["s5_log_dt"][0],
                                      p["s5_b_re"][0], p["s5_b_im"][0], p["s5_c_re"][0], p["s5_c_im"][0])
    bmb, cmb = bm.astype(bf16), cm.astype(bf16)
    lam = jnp.concatenate([lam, lam * jnp.array([1.0, -1.0], f32).reshape(1, 2, 1, 1)], axis=1)
    g_mix0, g_mix1 = row(p["norm_mix"][0]), row(p["norm_mix"][1])
    g_mlp0, g_mlp1 = row(p["norm_mlp"][0]), row(p["norm_mlp"][1])
    g_kv, g_fin = row(p["norm_kv"]), row(p["norm_final"])
    bq, bo = p["b_q"], p["b_o"]
    bkv = row(p["b_kv"])
    spread = _spread4()
    sinks = p["sinks"].reshape(16)

    wglu, gvec = sc_gather([shards["s5_w_glu"], shards["vecs"]], 3, "sc_gather_s5")
    win0, wout0 = sc_gather([shards["w_in0"], shards["w_out0"]], 14, "sc_gather_mlp0")
    wkv, wq, wo = sc_gather([shards["w_kv"], shards["w_q"], shards["w_o"]], 4, "sc_gather_attn")
    win1, wout1 = sc_gather([shards["w_in1"], shards["w_out1"]], 5, "sc_gather_mlp1")
    xp = x
    hn0 = s5_pre(xp, g_mix0)
    ys, xs = s5_core_fwd(hn0, bmb, lam, cmb)
    d_skip = gvec[:, 0, :128].reshape(1, D)
    bglu = gvec[:, 0, 128:].reshape(1, 2 * D)
    y, z, h1 = s5_post(ys, xp, g_mix0, d_skip, wglu, bglu)
    hm0, r0, h2p = mlp_fwd(h1, g_mlp0, win0, wout0, 0)
    wkv, wq, wo = wkv.reshape(D, 512), wq.reshape(D, D), wo.reshape(D, D)
    h2 = h2p
    kvn, hn1, k4, v4, q = attn_pre(h2, g_kv, g_mix1, wkv, bkv, spread, wq, bq)
    o, probs, sink_w = attn_core_fwd(q, k4, v4, sinks)
    h3 = attn_post(h2, o, wo, bo)
    hm1, r1, h4 = mlp_fwd(h3, g_mlp1, win1, wout1, 1)
    loss, dh4, dh4b, dg_fin = final_loss(h4, g_fin, target)

    def pair_sums(names, grads, cid, before):
        r1 = sc_comm(BgPair(grads), cid, "sc_pair_" + names[0])
        parts = add_pairs(grads, r1, core, "add_pairs_" + names[0])
        before, parts = lax.optimization_barrier((before, parts))
        return before, parts

    def across_chips(names, parts, cid):
        return list(zip(parts, sc_comm(BgChips(parts), cid, "sc_chips_" + names[0])))

    dh3, dwin1, dwout1, dg_mlp1 = mlp_bwd(h3, hm1, r1, g_mlp1, dh4, dh4b, win1, wout1, 1)
    do, dwo, dbo = attn_bwd_pre(dh3, o, wo)
    do, parts = pair_sums(["w_in1", "w_out1"], [dwin1, dwout1], 6, do)
    rs_in1, rs_out1 = across_chips(["w_in1", "w_out1"], parts, 7)
    dq, dk4, dv4, dsink = attn_core_bwd(q, do, k4, v4, probs, sink_w)
    dh2, dwq, dbq, dg_mix1 = attn_bwd_q(h2, dh3, dq, hn1, g_mix1, wq)
    dh2, dh2b, dwkv, dbkv, dg_kv = attn_bwd_kv(h2, dh2, dk4, dv4, kvn, g_kv, wkv, spread)
    dh2p, dh2pb = dh2, dh2b
    big = {}
    def adam_group(group, rss, name, **kw):
        return adam_big(*zip(*[opt[n] for n in group]), *zip(*rss), chip, name, **kw)

    mlp = ["w_mlp_in", "w_mlp_out"]
    a_in1, a_out1 = adam_group(mlp, [rs_in1, rs_out1], "adam_w_mlp1", layer=1)
    dh2p, a_in1, a_out1 = lax.optimization_barrier((dh2p, a_in1, a_out1))
    names = ["w_kv", "w_q", "w_o"]
    dh2p, parts = pair_sums(names, [dwkv.reshape(NDEV, 128, 512), dwq.reshape(NDEV, 128, D),
                                    dwo.reshape(NDEV, 128, D)], 8, dh2p)
    rs_attn = across_chips(names, parts, 9)
    dh1, dwin0, dwout0, dg_mlp0 = mlp_bwd(h1, hm0, r0, g_mlp0, dh2p, dh2pb, win0, wout0, 0)
    a_attn = adam_group(names, rs_attn, "adam_attn")
    dh1, a_attn = lax.optimization_barrier((dh1, a_attn))
    big.update(zip(names, a_attn))
    dy, dwglu, dbglu = s5_post_bwd(dh1, y, z, wglu)
    dy, parts = pair_sums(["w_in0", "w_out0"], [dwin0, dwout0], 10, dy)
    rs_in0, rs_out0 = across_chips(["w_in0", "w_out0"], parts, 11)
    du, dbm, dcmt, dlam = s5_core_bwd(hn0, dy, xs, bmb, lam, cmb)
    du, parts = pair_sums(["s5_w_glu"], [dwglu], 12, du)
    rs_glu, = across_chips(["s5_w_glu"], parts, 13)
    dxp, dg_mix0, dd = s5_pre_bwd(xp, g_mix0, du, dy, d_skip, dh1)
    big["w_mlp_in"], big["w_mlp_out"] = adam_group(mlp, [rs_in0, rs_out0], "adam_w_mlp0", layer=0,
                                                   prev=[a_in1, a_out1])
    big["s5_w_glu"], = adam_group(["s5_w_glu"], [rs_glu], "adam_s5_w_glu")
    grad_x = dxp
    da_re, da_im, dlog_dt, db_re, db_im, dc_re, dc_im = prep_vjp((dlam, dbm, dcmt.transpose(0, 2, 1)))

    def lanes(v_):
        v_ = v_.reshape(1, -1)
        return jnp.pad(v_, ((0, 0), (0, D - v_.shape[1])))

    small = dict(
        rows8=[dg_mix0, dg_mix1, dg_mlp0, dg_mlp1, dg_kv, dg_fin, dd, dbq, dbo], b_glu=dbglu, b_kv=dbkv,
        misc=jnp.concatenate([lanes(dsink[:, 0, :Q_PER_KV]), lanes(dlog_dt), lanes(loss[0:1, 0:1])], axis=0),
        s5=[da_re.reshape(4, D), da_im.reshape(4, D),
            db_re.transpose(0, 2, 1).reshape(64, D), db_im.transpose(0, 2, 1).reshape(64, D),
            dc_re.reshape(64, D), dc_im.reshape(64, D)])
    small, big["w_mlp_in"], big["w_mlp_out"] = lax.optimization_barrier((small, big["w_mlp_in"], big["w_mlp_out"]))
    return loss, grad_x, small, big


def add_pairs(gs, r1s, core, name):
    n = len(gs)

    def body(core_ref, *refs):
        for g_ref, r_ref, o_ref in zip(refs[:n], refs[n:2 * n], refs[2 * n:]):
            o_ref[...] = (g_ref[...].astype(f32) + r_ref[...].astype(f32)).astype(bf16)

    mine = [pl.BlockSpec((None,) + g.shape[1:], lambda k, core: (2 * k + core[0], 0, 0)) for g in gs]
    slot = [pl.BlockSpec((None,) + g.shape[1:], lambda k, core: (k, 0, 0)) for g in gs]
    return pl.pallas_call(
        body, name=name, out_shape=[SDS((4,) + g.shape[1:], bf16) for g in gs],
        grid_spec=pltpu.PrefetchScalarGridSpec(num_scalar_prefetch=1, grid=(4,), in_specs=mine + slot, out_specs=slot),
        compiler_params=_cp(dimension_semantics=("arbitrary",)),
    )(core, *gs, *r1s)


def _adamw(w, g, m, v):
    m = ADAM_B1 * m + (1.0 - ADAM_B1) * g
    v = ADAM_B2 * v + (1.0 - ADAM_B2) * (g * g)
    m_hat = m / (1.0 - ADAM_B1 ** ADAM_STEP)
    v_hat = v / (1.0 - ADAM_B2 ** ADAM_STEP)
    delta = -ADAM_LR * (m_hat / (jnp.sqrt(v_hat) + ADAM_EPS) + ADAM_WD * w)
    return delta, m, v


ADAM_STEPS = 4


def adam_big(ws, ms, vs, parts, r2s, chip, name, layer=0, prev=None):
    n = len(ws)

    def body(chip_ref, *refs):
        outs = refs[len(refs) - 4 * n:]
        for a in range(n):
            w_ref, m_ref, v_ref, p_ref, r_ref = (refs[k * n + a] for k in range(5))
            g = p_ref[...].astype(f32) + r_ref[0].astype(f32) + r_ref[1].astype(f32) + r_ref[2].astype(f32)
            d, m_, v_ = _adamw(w_ref[...], g, m_ref[...], v_ref[...])
            for o_ref, val in zip(outs[4 * a:4 * a + 4], (g, d, m_, v_)):
                o_ref[...] = val

    tiles = [(w.shape[1] // ADAM_STEPS, w.shape[2]) for w in ws]
    blk = [pl.BlockSpec((None,) + t, lambda i, chip: (layer, i, 0)) for t in tiles]
    extra = [] if prev is None else [arr for four in prev for arr in four]
    res = pl.pallas_call(
        body, name=name, out_shape=[SDS(w.shape, f32) for w in ws for _ in range(4)],
        grid_spec=pltpu.PrefetchScalarGridSpec(
            num_scalar_prefetch=1, grid=(ADAM_STEPS,),
            in_specs=blk * 3 + [pl.BlockSpec((None,) + t, lambda i, chip: (chip[0], i, 0)) for t in tiles]
            + [pl.BlockSpec((3,) + t, lambda i, chip: (0, i, 0)) for t in tiles] + [_ANY] * len(extra),
            out_specs=[b for b in blk for _ in range(4)]),
        input_output_aliases={1 + 5 * n + k: k for k in range(len(extra))},
        compiler_params=_cp(dimension_semantics=("arbitrary",)),
    )(chip, *ws, *ms, *vs, *parts, *r2s, *extra)
    return [tuple(res[4 * a:4 * a + 4]) for a in range(n)]


SMALL_BUF_ROWS = 288


def allreduce_small(rows8, b_glu, b_kv, misc, s5):
    R = SMALL_BUF_ROWS
    half, quarter = R // 2, R // 4
    pieces = [*rows8, b_glu, b_kv, misc, *s5]

    def body(*refs):
        ins, (out_ref, in_ref, acc1, acc2, r0, r1, r2, send_sems, recv_sems) = refs[:len(pieces)], refs[len(pieces):]
        in_ref[8:16, :] = jnp.zeros((8, D), f32)
        in_ref[R - 8:R, :] = jnp.zeros((8, D), f32)
        for k in range(len(rows8)):
            in_ref[k:k + 1, :] = ins[k][0:1, :]
        glu_ref, kv_ref, misc_ref = ins[len(rows8):len(rows8) + 3]
        in_ref[9:10, :] = glu_ref[0:1, 0:D]
        in_ref[10:11, :] = glu_ref[0:1, D:2 * D]
        in_ref[11:12, 0:kv_ref.shape[1]] = kv_ref[0:1, :]
        in_ref[12:15, :] = misc_ref[...]
        row = 16
        for a in ins[len(rows8) + 3:]:
            in_ref[row:row + a.shape[0], :] = a[...]
            row += a.shape[0]
        x, y, c = _pos()
        sibling, over_x, over_y = (x, y, 1 - c), (1 - x, y, c), (x, 1 - y, c)
        first = pl.multiple_of(c * half, 8)
        mine = pl.ds(first, half)
        theirs = pl.ds(pl.multiple_of((1 - c) * half, 8), half)
        qa = pl.ds(first, quarter)
        qb = pl.ds(pl.multiple_of(first + quarter, 8), quarter)

        def exchange(copies):
            cps = [pltpu.make_async_remote_copy(
                src_ref=src.at[rows], dst_ref=dst.at[rows], send_sem=send_sems.at[k], recv_sem=recv_sems.at[k],
                device_id=peer, device_id_type=MESH) for k, src, dst, rows, peer in copies]
            for cp in cps:
                cp.start()
            for cp in cps:
                cp.wait()

        exchange([(0, in_ref, r0, theirs, sibling)])
        acc1[mine, :] = in_ref[mine, :] + r0[mine, :]
        exchange([(1, acc1, r1, qa, over_x), (2, acc1, r1, qb, over_y)])
        acc2[mine, :] = acc1[mine, :] + r1[mine, :]
        exchange([(3, acc2, r2, qa, over_y), (4, acc2, r2, qb, over_x)])
        out_ref[mine, :] = acc2[mine, :] + r2[mine, :]
        exchange([(5, out_ref, out_ref, mine, sibling)])

    vm = pl.BlockSpec(memory_space=pltpu.VMEM)
    return pl.pallas_call(
        body, name="allreduce_small", in_specs=[vm] * len(pieces), out_specs=vm, out_shape=SDS((R, D), f32),
        scratch_shapes=[pltpu.VMEM((R, D), f32)] * 6 + [pltpu.SemaphoreType.DMA((6,)), pltpu.SemaphoreType.DMA((6,))],
    )(*pieces)


SMALL_ROWS = {'norm_mix': (0, 2, D), 'norm_mlp': (2, 2, D), 'norm_kv': (4, 1, D), 'norm_final': (5, 1, D),
              's5_d': (6, 1, D), 'b_q': (7, 1, D), 'b_o': (8, 1, D), 's5_b_glu': (9, 2, D), 'b_kv': (11, 1, 512),
              'sinks': (12, 1, 16), 's5_log_dt': (13, 1, 64), 's5_a_re': (16, 4, D), 's5_a_im': (20, 4, D),
              's5_b_re': (24, 64, D), 's5_b_im': (88, 64, D), 's5_c_re': (152, 64, D), 's5_c_im': (216, 64, D)}
LOSS_ROW = 14
ROW_PARAMS = ['norm_mix', 'norm_mlp', 'norm_kv', 'norm_final', 'b_q', 'b_o', 'b_kv', 'sinks', 's5_log_dt']
SHARD_PARAMS = ['s5_d', 's5_b_glu']
S5_PARAMS = ['s5_a_re', 's5_a_im', 's5_b_re', 's5_b_im', 's5_c_re', 's5_c_im']


def adam_small(dev, gsum, s5_grads, w, m, v):
    names = ROW_PARAMS + SHARD_PARAMS + S5_PARAMS
    n_g = len(ROW_PARAMS) + len(SHARD_PARAMS)

    def body(dev_ref, gs_ref, *refs):
        pos = [0]

        def take(k):
            r = refs[pos[0]:pos[0] + k]
            pos[0] += k
            return r

        g5 = take(len(S5_PARAMS))
        wr, mr, vr = take(len(names)), take(len(names)), take(len(names))
        g_out = take(n_g)
        d_out, m_out, v_out = take(len(names)), take(len(names)), take(len(names))
        dv = dev_ref[0]
        for i, n in enumerate(names):
            if n in S5_PARAMS:
                g = g5[S5_PARAMS.index(n)][...]
            elif n in SHARD_PARAMS:
                r0, _, _ = SMALL_ROWS[n]
                ln = wr[i].shape[1]
                g = jnp.zeros((1, ln), f32)
                for k in range(NDEV):
                    off = k * ln
                    piece = gs_ref[r0 + off // D:r0 + off // D + 1, off % D:off % D + ln]
                    g = g + jnp.where(dv == k, piece, 0.0)
                g_out[i][...] = g
            else:
                r0, nr, nl = SMALL_ROWS[n]
                g = gs_ref[r0:r0 + nr, 0:nl]
                g_out[i][...] = g
            d, m_, v_ = _adamw(wr[i][...], g, mr[i][...], vr[i][...])
            d_out[i][...] = d
            m_out[i][...] = m_
            v_out[i][...] = v_

    vm = pl.BlockSpec(memory_space=pltpu.VMEM)
    ins = [s5_grads[n] for n in S5_PARAMS] + [d[n] for d in (w, m, v) for n in names]
    shapes = [SDS(w[n].shape, f32) for n in names]
    res = pl.pallas_call(
        body, name="adam_small", in_specs=[pl.BlockSpec(memory_space=pltpu.SMEM)] + [vm] * (1 + len(ins)),
        out_specs=[vm] * (n_g + 3 * len(names)), out_shape=shapes[:n_g] + shapes * 3,
        compiler_params=_cp(),
    )(dev, gsum, *ins)
    g_o = dict(zip(names[:n_g], res[:n_g]))
    rest = res[n_g:]
    k = len(names)
    return g_o, dict(zip(names, rest[:k])), dict(zip(names, rest[k:2 * k])), dict(zip(names, rest[2 * k:]))


WEIGHTS = ['norm_mix', 'norm_mlp', 'norm_kv', 'norm_final', 's5_a_re', 's5_a_im', 's5_log_dt', 's5_b_re', 's5_b_im',
           's5_c_re', 's5_c_im', 's5_d', 's5_w_glu', 's5_b_glu', 'w_kv', 'b_kv', 'w_q', 'b_q', 'sinks', 'w_o', 'b_o',
           'w_mlp_in', 'w_mlp_out']
BIG = ['s5_w_glu', 'w_kv', 'w_q', 'w_o', 'w_mlp_in', 'w_mlp_out']
BIG_2D = {'s5_w_glu': (D, 256), 'w_kv': (128, 512), 'w_q': (128, D), 'w_o': (128, D), 'w_mlp_in': (2 * D, 512),
          'w_mlp_out': (2 * 512, D)}
SMALL = [n for n in WEIGHTS if n not in BIG]


def kernel(x, norm_mix, norm_mlp, norm_kv, norm_final, s5_a_re, s5_a_im, s5_log_dt, s5_b_re, s5_b_im, s5_c_re, s5_c_im, s5_d, s5_w_glu, s5_b_glu, w_kv, b_kv, w_q, b_q, sinks, w_o, b_o, w_mlp_in, w_mlp_out, loss_target, m_norm_mix, m_norm_mlp, m_norm_kv, m_norm_final, m_s5_a_re, m_s5_a_im, m_s5_log_dt, m_s5_b_re, m_s5_b_im, m_s5_c_re, m_s5_c_im, m_s5_d, m_s5_w_glu, m_s5_b_glu, m_w_kv, m_b_kv, m_w_q, m_b_q, m_sinks, m_w_o, m_b_o, m_w_mlp_in, m_w_mlp_out, v_norm_mix, v_norm_mlp, v_norm_kv, v_norm_final, v_s5_a_re, v_s5_a_im, v_s5_log_dt, v_s5_b_re, v_s5_b_im, v_s5_c_re, v_s5_c_im, v_s5_d, v_s5_w_glu, v_s5_b_glu, v_w_kv, v_b_kv, v_w_q, v_b_q, v_sinks, v_w_o, v_b_o, v_w_mlp_in, v_w_mlp_out):
    w = dict(norm_mix=norm_mix, norm_mlp=norm_mlp, norm_kv=norm_kv, norm_final=norm_final, s5_a_re=s5_a_re,
             s5_a_im=s5_a_im, s5_log_dt=s5_log_dt, s5_b_re=s5_b_re, s5_b_im=s5_b_im, s5_c_re=s5_c_re, s5_c_im=s5_c_im,
             s5_d=s5_d, s5_w_glu=s5_w_glu, s5_b_glu=s5_b_glu, w_kv=w_kv, b_kv=b_kv, w_q=w_q, b_q=b_q, sinks=sinks,
             w_o=w_o, b_o=b_o, w_mlp_in=w_mlp_in, w_mlp_out=w_mlp_out)
    m = dict(norm_mix=m_norm_mix, norm_mlp=m_norm_mlp, norm_kv=m_norm_kv, norm_final=m_norm_final, s5_a_re=m_s5_a_re,
             s5_a_im=m_s5_a_im, s5_log_dt=m_s5_log_dt, s5_b_re=m_s5_b_re, s5_b_im=m_s5_b_im, s5_c_re=m_s5_c_re,
             s5_c_im=m_s5_c_im, s5_d=m_s5_d, s5_w_glu=m_s5_w_glu, s5_b_glu=m_s5_b_glu, w_kv=m_w_kv, b_kv=m_b_kv,
             w_q=m_w_q, b_q=m_b_q, sinks=m_sinks, w_o=m_w_o, b_o=m_b_o, w_mlp_in=m_w_mlp_in, w_mlp_out=m_w_mlp_out)
    v = dict(norm_mix=v_norm_mix, norm_mlp=v_norm_mlp, norm_kv=v_norm_kv, norm_final=v_norm_final, s5_a_re=v_s5_a_re,
             s5_a_im=v_s5_a_im, s5_log_dt=v_s5_log_dt, s5_b_re=v_s5_b_re, s5_b_im=v_s5_b_im, s5_c_re=v_s5_c_re,
             s5_c_im=v_s5_c_im, s5_d=v_s5_d, s5_w_glu=v_s5_w_glu, s5_b_glu=v_s5_b_glu, w_kv=v_w_kv, b_kv=v_b_kv,
             w_q=v_w_q, b_q=v_b_q, sinks=v_sinks, w_o=v_w_o, b_o=v_b_o, w_mlp_in=v_w_mlp_in, w_mlp_out=v_w_mlp_out)
    xi, yi, ci = _pos()
    dev = 4 * xi + 2 * yi + ci
    core = ci.reshape(1).astype(jnp.int32)
    chip = (2 * xi + yi).reshape(1).astype(jnp.int32)

    shards = {
        "s5_w_glu": s5_w_glu[0].astype(bf16), "w_kv": w_kv.astype(bf16), "w_q": w_q[0].astype(bf16),
        "w_o": w_o[0].astype(bf16), "w_in0": w_mlp_in[0].astype(bf16), "w_in1": w_mlp_in[1].astype(bf16),
        "w_out0": w_mlp_out[0].astype(bf16), "w_out1": w_mlp_out[1].astype(bf16),
        "vecs": jnp.broadcast_to(jnp.concatenate([s5_d, s5_b_glu], axis=1), (8, 384)),
    }
    as3d = lambda a, n: a if a.ndim == 3 and a.shape[0] == 2 else a.reshape((1,) + BIG_2D[n])
    opt = {n: (as3d(w[n], n), as3d(m[n], n), as3d(v[n], n)) for n in BIG}
    _, grad_x, grads, big = fwd_bwd(x[0], loss_target[0], {n: w[n] for n in SMALL}, shards, opt, core, chip)

    gsum = allreduce_small(**grads)

    out_g, out_d, out_m, out_v = {}, {}, {}, {}
    for n in BIG:
        out_g[n], out_d[n], out_m[n], out_v[n] = [r.reshape(w[n].shape) for r in big[n]]

    loss = gsum[LOSS_ROW, 0]
    swapped = ("s5_b_re", "s5_b_im")
    swap = lambda a: a.transpose(0, 1, 3, 2)

    def kernel_side(d):
        d = {n: (d[n].reshape(1, -1) if d[n].ndim == 1 else d[n]) for n in SMALL}
        d.update({n: swap(d[n]) for n in swapped})
        return d

    s5_g = {}
    for n in S5_PARAMS:
        r0, nr, _ = SMALL_ROWS[n]
        s5_g[n] = gsum[r0:r0 + nr].reshape((1, 64, 16, 64) if n in swapped else w[n].shape)
        out_g[n] = s5_g[n]
    g_s, d_s, m_s, v_s = adam_small(dev.reshape(1).astype(jnp.int32), gsum, s5_g, kernel_side(w), kernel_side(m),
                                    kernel_side(v))
    for src, dst in ((g_s, out_g), (d_s, out_d), (m_s, out_m), (v_s, out_v)):
        dst.update(src)
    for dst in (out_g, out_d, out_m, out_v):
        for n in SMALL:
            dst[n] = (swap(dst[n]) if n in swapped else dst[n]).reshape(w[n].shape)

    return (loss, grad_x[None], *[out_g[n] for n in WEIGHTS], *[out_d[n] for n in WEIGHTS],
            *[out_m[n] for n in WEIGHTS], *[out_v[n] for n in WEIGHTS])
```

```python
import math

import jax
import jax.numpy as jnp
from jax import lax
from jax.experimental import pallas as pl
from jax.experimental.pallas import tpu as pltpu
from jax.experimental.pallas import tpu_sc as plsc

f32 = jnp.float32
bf16 = jnp.bfloat16
SDS = jax.ShapeDtypeStruct

T = 2048
D = 1024
NDEV = 8
NORM_EPS = 1e-5
S5_G, S5_C, S5_P = 64, 16, 64
S5_SUB = 8
S5_CH = 8
S5_STEPS = T // S5_CH
DT_MIN_LAMBDA = -1e-4
HEAD_DIM = 64
N_KV = 4
Q_PER_KV = 4
BLK = 128
D_FF_SHARD = 512
ADAM_LR, ADAM_B1, ADAM_B2, ADAM_EPS, ADAM_WD, ADAM_STEP = 0.001, 0.9, 0.999, 1e-08, 0.01, 10
VMEM_LIMIT = 56 * 1024 * 1024
MESH = pl.DeviceIdType.MESH


def _cp(**kw):
    return pltpu.CompilerParams(vmem_limit_bytes=VMEM_LIMIT, **kw)


def _dot(a, b):
    return jnp.dot(a, b, preferred_element_type=f32)


def _dot_nt(a, b):
    return lax.dot_general(a, b, (((1,), (1,)), ((), ())), preferred_element_type=f32)


def _dot_tn(a, b):
    return lax.dot_general(a, b, (((0,), (0,)), ((), ())), preferred_element_type=f32)


def _rms(x, g):
    r = lax.rsqrt(jnp.mean(x * x, axis=-1, keepdims=True) + NORM_EPS)
    return x * r * g, r


def _rms_bwd(x, g, dy):
    r = lax.rsqrt(jnp.mean(x * x, axis=-1, keepdims=True) + NORM_EPS)
    u = dy * g
    dx = r * u - (r * r * r) * x * jnp.mean(u * x, axis=-1, keepdims=True)
    return dx, dy * x * r


def _colsum8(v):
    s = jnp.sum(v, axis=0, keepdims=True)
    row = lax.broadcasted_iota(jnp.int32, (8, v.shape[1]), 0)
    return jnp.where(row == 0, jnp.broadcast_to(s, (8, v.shape[1])), 0.0)


def _full(shape):
    nd = len(shape)
    return pl.BlockSpec(shape, lambda *_: (0,) * nd, pipeline_mode=pl.Buffered(1))


_ANY = pl.BlockSpec(memory_space=pl.ANY)


def _pos():
    return lax.axis_index("x"), lax.axis_index("y"), lax.axis_index("c")


def _other_chips(x, y):
    return [(1 - x, y), (x, 1 - y), (1 - x, 1 - y)]


class BgGather:
    SIB, XN, YN, FWD_Y, FWD_X, SIB_X, SIB_Y, SIB_D = range(8)

    def __init__(self, arrs, mids=(0.5, 0.75)):
        n = len(arrs)
        self.arrs = list(arrs)
        self.out_shape = [SDS((NDEV,) + a.shape, a.dtype) for a in arrs]
        self.scratch = [pltpu.SemaphoreType.DMA((n, 8)), pltpu.SemaphoreType.DMA((n, 8)),
                        pltpu.SemaphoreType.DMA((n,))]
        self.mids = mids
        self.result = None

    @staticmethod
    def peers(x, y, c):
        return [(x, y, 1 - c), (1 - x, y, c), (x, 1 - y, c)]

    def mid_steps(self, nsteps):
        at = lambda f: min(nsteps - 1, max(0, int(f * nsteps) - 1))
        return [(at(self.mids[0]), self.mid), (max(at(self.mids[0]), at(self.mids[1])), self.mid2)]

    def _halves(self, a):
        rows = self.arrs[a].shape[0]
        cut = (rows // 32) * 16 if rows >= 32 else rows
        return (0, cut), (cut, rows - cut)

    def _copy(self, ins, outs, sems, a, k, block, to, own=False, part=None):
        slot = 4 * block[0] + 2 * block[1] + block[2]
        rows = pl.ds(0, self.arrs[a].shape[0]) if part is None else pl.ds(*self._halves(a)[part])
        dst = outs[a].at[slot, rows]
        return pltpu.make_async_remote_copy(
            src_ref=ins[a].at[rows] if own else dst, dst_ref=dst, send_sem=sems[0].at[a, k],
            recv_sem=sems[1].at[a, k], device_id=to, device_id_type=MESH)

    def _mine(self, ins, outs, sems):
        x, y, c = _pos()
        return [pltpu.make_async_copy(ins[a], outs[a].at[4 * x + 2 * y + c], sems[2].at[a])
                for a in range(len(self.arrs))]

    def _split(self, a):
        return self._halves(a)[1][1] > 0

    def _sends(self, ins, outs, sems, phase):
        x, y, c = _pos()
        me, sib, xn, yn, dg = (x, y, c), (x, y, 1 - c), (1 - x, y, c), (x, 1 - y, c), (1 - x, 1 - y, c)
        cps = []
        for a in range(len(self.arrs)):
            cp = lambda k, block, to, **kw: self._copy(ins, outs, sems, a, k, block, to, **kw)
            if phase == 0:
                cps += [cp(self.SIB, me, sib, own=True), cp(self.XN, me, xn, own=True), cp(self.YN, me, yn, own=True)]
            elif phase == 1:
                cps.append(cp(self.FWD_Y, xn, yn, part=0))
                if self._split(a):
                    cps.append(cp(self.FWD_X, yn, xn, part=1))
                cps += [cp(self.SIB_X, xn, sib), cp(self.SIB_Y, yn, sib)]
            else:
                cps.append(cp(self.SIB_D, dg, sib))
        return cps

    def _arrivals(self, ins, outs, sems, phase):
        x, y, c = _pos()
        me, xn, yn, dg = (x, y, c), (1 - x, y, c), (x, 1 - y, c), (1 - x, 1 - y, c)
        cps = []
        for a in range(len(self.arrs)):
            cp = lambda k, block, **kw: self._copy(ins, outs, sems, a, k, block, me, **kw)
            if phase == 1:
                cps += [cp(self.XN, xn), cp(self.YN, yn)]
            elif phase == 2:
                cps.append(cp(self.FWD_Y, dg, part=0))
                if self._split(a):
                    cps.append(cp(self.FWD_X, dg, part=1))
            else:
                cps += [cp(self.SIB, (x, y, 1 - c)), cp(self.SIB_X, (1 - x, y, 1 - c)),
                        cp(self.SIB_Y, (x, 1 - y, 1 - c)), cp(self.SIB_D, (1 - x, 1 - y, 1 - c))]
        return cps

    def start(self, ins, outs, sems):
        for cp in self._mine(ins, outs, sems) + self._sends(ins, outs, sems, 0):
            cp.start()

    def mid(self, ins, outs, sems):
        for cp in self._arrivals(ins, outs, sems, 1):
            cp.wait_recv()
        for cp in self._sends(ins, outs, sems, 1):
            cp.start()

    def mid2(self, ins, outs, sems):
        for cp in self._arrivals(ins, outs, sems, 2):
            cp.wait_recv()
        for cp in self._sends(ins, outs, sems, 2):
            cp.start()

    def finish(self, ins, outs, sems):
        for cp in self._arrivals(ins, outs, sems, 3):
            cp.wait_recv()
        for ph in range(3):
            for cp in self._sends(ins, outs, sems, ph):
                cp.wait_send()
        for cp in self._mine(ins, outs, sems):
            cp.wait()


def sc_comm(g, collective_id, name):
    srcs = [jax.new_ref(a, memory_space=pltpu.MemorySpace.HBM) for a in g.arrs]
    dsts = [jax.empty_ref(s, memory_space=pltpu.MemorySpace.HBM) for s in g.out_shape]

    @pl.kernel(mesh=plsc.ScalarSubcoreMesh(axis_name="sequencer", num_cores=1), name=name,
               scratch_types=tuple(g.scratch), compiler_params=pltpu.CompilerParams(collective_id=collective_id))
    def launch(*sems):
        peers = g.peers(*_pos())
        barrier = pltpu.get_barrier_semaphore()
        for peer in peers:
            pl.semaphore_signal(barrier, inc=1, device_id=peer, device_id_type=MESH)
        pl.semaphore_wait(barrier, len(peers))
        g.start(srcs, dsts, sems)
        for _, phase in g.mid_steps(1):
            phase(srcs, dsts, sems)
        g.finish(srcs, dsts, sems)

    launch()
    return [d[...] for d in dsts]


def sc_gather(arrs, collective_id, name):
    return sc_comm(BgGather(arrs), collective_id, name)


class BgPair:
    def __init__(self, arrs):
        n = len(arrs)
        self.arrs = list(arrs)
        self.out_shape = [SDS((4,) + a.shape[1:], a.dtype) for a in arrs]
        self.scratch = [pltpu.SemaphoreType.DMA((n, 4)), pltpu.SemaphoreType.DMA((n, 4))]
        self.result = None

    @staticmethod
    def peers(x, y, c):
        return [(x, y, 1 - c)]

    def mid_steps(self, nsteps):
        return []

    def _copies(self, ins, outs, sems):
        x, y, c = _pos()
        return [pltpu.make_async_remote_copy(
            src_ref=ins[a].at[2 * k + 1 - c], dst_ref=outs[a].at[k], send_sem=sems[0].at[a, k],
            recv_sem=sems[1].at[a, k], device_id=(x, y, 1 - c), device_id_type=MESH)
            for a in range(len(self.arrs)) for k in range(4)]

    def start(self, ins, outs, sems):
        for cp in self._copies(ins, outs, sems):
            cp.start()

    def finish(self, ins, outs, sems):
        cps = self._copies(ins, outs, sems)
        for cp in cps:
            cp.wait_recv()
        for cp in cps:
            cp.wait_send()


class BgChips(BgPair):
    def __init__(self, arrs):
        n = len(arrs)
        self.arrs = list(arrs)
        self.out_shape = [SDS((3,) + a.shape[1:], a.dtype) for a in arrs]
        self.scratch = [pltpu.SemaphoreType.DMA((n, 3)), pltpu.SemaphoreType.DMA((n, 3))]
        self.result = None

    @staticmethod
    def peers(x, y, c):
        return [(px, py, c) for px, py in _other_chips(x, y)]

    def _copies(self, ins, outs, sems):
        x, y, c = _pos()
        return [pltpu.make_async_remote_copy(
            src_ref=ins[a].at[2 * px + py], dst_ref=outs[a].at[r], send_sem=sems[0].at[a, r],
            recv_sem=sems[1].at[a, r], device_id=(px, py, c), device_id_type=MESH)
            for a in range(len(self.arrs)) for r, (px, py) in enumerate(_other_chips(x, y))]


def s5_discretize(a_re, a_im, log_dt, b_re, b_im, c_re, c_im):
    lam_r = jnp.minimum(a_re, DT_MIN_LAMBDA)
    lam_i = a_im
    dt = jnp.exp(log_dt)[:, None]
    e = jnp.exp(lam_r * dt)
    lbr = e * jnp.cos(lam_i * dt)
    lbi = e * jnp.sin(lam_i * dt)
    den = lam_r * lam_r + lam_i * lam_i
    cf_r = ((lbr - 1.0) * lam_r + lbi * lam_i) / den
    cf_i = (lbi * lam_r - (lbr - 1.0) * lam_i) / den
    bb_r = cf_r[:, :, None] * b_re - cf_i[:, :, None] * b_im
    bb_i = cf_r[:, :, None] * b_im + cf_i[:, :, None] * b_re
    eye = jnp.eye(8, dtype=f32)

    def blk_b(m):
        return jnp.einsum('bgpc,gh->bgchp', m.reshape(8, 8, S5_P, S5_C), eye).reshape(8, 128, 512)

    def blk_c(m):
        return jnp.einsum('bgcp,gh->bgphc', m.reshape(8, 8, S5_C, S5_P), eye).reshape(8, 512, 128)

    bm = jnp.concatenate([blk_b(bb_r), blk_b(bb_i)], axis=-1)
    cm = jnp.concatenate([blk_c(c_re), -blk_c(c_im)], axis=1)
    lam = jnp.stack([lbr.reshape(8, 512), lbi.reshape(8, 512)], axis=1)
    lam = jnp.broadcast_to(lam[:, :, None, :], (8, 2, 8, 512))
    return lam, bm, cm


def _cmul(ar, ai, br, bi):
    return ar * br - ai * bi, ar * bi + ai * br


def _shift_rows(v, k, up):
    row = lax.broadcasted_iota(jnp.int32, v.shape, 0)
    if up:
        return jnp.where(row < 8 - k, pltpu.roll(v, 8 - k, 0), 0.0)
    return jnp.where(row >= k, pltpu.roll(v, k, 0), 0.0)


_ROWS = 256


def s5_core_fwd(hn, bm, lam, cm):
    nt = T // _ROWS

    def body(u_ref, b_ref, lam_ref, c_ref, ys_ref, S):
        lr, li = lam_ref[0], lam_ref[1]
        z = jnp.zeros((8, 512), f32)
        tile = lambda k: pl.ds(k * _ROWS, _ROWS)
        c = (z, z)
        for k in range(nt):
            S[tile(k), :] = _dot(_rows_in(u_ref, k).astype(bf16), b_ref[...])
            if k >= 1:
                c = _scan_tile(S, lr, li, k - 1, c, False, False)
        c = _scan_tile(S, lr, li, nt - 1, c, False, False)
        c = _chunk_starts(c[0], c[1], lr, li, False)
        for k in range(nt):
            c = _scan_tile(S, lr, li, k, c, False, True)
            if k >= 1:
                _rows_out(ys_ref, k - 1, _dot(S[tile(k - 1), :].astype(bf16), c_ref[...]))
        _rows_out(ys_ref, nt - 1, _dot(S[tile(nt - 1), :].astype(bf16), c_ref[...]))

    return pl.pallas_call(
        body, name="s5_core_fwd", grid=(S5_SUB,),
        in_specs=[pl.BlockSpec((T, 128), lambda b: (0, b)),
                  pl.BlockSpec((None, 128, 1024), lambda b: (b, 0, 0)),
                  pl.BlockSpec((None, 4, 8, 512), lambda b: (b, 0, 0, 0)),
                  pl.BlockSpec((None, 1024, 128), lambda b: (b, 0, 0))],
        out_specs=[pl.BlockSpec((T, 128), lambda b: (0, b)), pl.BlockSpec((T, 1024), lambda b: (0, b))],
        out_shape=[SDS((T, D), f32), SDS((T, S5_SUB * 1024), f32)],
        compiler_params=_cp(dimension_semantics=("arbitrary",)),
    )(hn, bm, lam, cm)


_SEG = _ROWS // S5_CH


def _rows_in(ref, k):
    return jnp.concatenate([ref[pl.ds(s, S5_CH, stride=S5_STEPS), :] for s in range(k * _SEG, (k + 1) * _SEG)], axis=0)


def _rows_out(ref, k, val):
    for j, s in enumerate(range(k * _SEG, (k + 1) * _SEG)):
        ref[pl.ds(s, S5_CH, stride=S5_STEPS), :] = val[j * S5_CH:(j + 1) * S5_CH, :]


def _scan_tile(S, lr, li, k, carry, reverse, store, aux=None):
    steps = range(k * _SEG, (k + 1) * _SEG)
    for s in (reversed(steps) if reverse else steps):
        row = pl.ds(s * 8, 8)
        xr, xi = carry[0], carry[1]
        nr = lr * xr - li * xi + S[row, 0:512]
        ni = lr * xi + li * xr + S[row, 512:1024]
        if store:
            S[row, 0:512] = nr
            S[row, 512:1024] = ni
        if aux is not None and s >= 1:
            prow = pl.ds((s - 1) * 8, 8)
            pr, pi_ = aux[prow, 0:512], aux[prow, 512:1024]
            carry = (nr, ni, carry[2] + nr * pr + ni * pi_, carry[3] + ni * pr - nr * pi_)
        elif aux is not None:
            carry = (nr, ni, carry[2], carry[3])
        else:
            carry = (nr, ni)
    return carry


def _chunk_starts(er, ei, lr, li, reverse):
    ar, ai = lr, li
    for _ in range(8):
        ar, ai = _cmul(ar, ai, ar, ai)
    cr, ci = _shift_rows(er, 1, reverse), _shift_rows(ei, 1, reverse)
    for k in (1, 2, 4):
        sr, si = _shift_rows(cr, k, reverse), _shift_rows(ci, k, reverse)
        pr, pi_ = _cmul(ar, ai, sr, si)
        cr, ci = cr + pr, ci + pi_
        ar, ai = _cmul(ar, ai, ar, ai)
    return cr, ci


def s5_core_bwd(hn, dy, xs, bm, lam, cm):
    nt = T // _ROWS

    def body(u_ref, dy_ref, S1, b_ref, lam_ref, c_ref, du_ref, db_ref, dct_ref, dlam_ref, S2):
        lcr, lci = lam_ref[2], lam_ref[3]
        z = jnp.zeros((8, 512), f32)
        tile = lambda k: pl.ds(k * _ROWS, _ROWS)

        def dx(k):
            dyb = _rows_in(dy_ref, k).astype(bf16)
            S2[tile(k), :] = _dot_nt(dyb, c_ref[...])
            dct_ref[...] += _dot_tn(dyb, S1[tile(k), :].astype(bf16))

        dct_ref[...] = jnp.zeros_like(dct_ref)
        dx(nt - 1)
        c = (z, z)
        for k in range(nt - 1, -1, -1):
            if k >= 1:
                dx(k - 1)
            c = _scan_tile(S2, lcr, lci, k, c, True, False)

        def dbu(k):
            gb = S2[tile(k), :].astype(bf16)
            db_ref[...] += _dot_tn(_rows_in(u_ref, k).astype(bf16), gb)
            _rows_out(du_ref, k, _dot_nt(gb, b_ref[...]))

        c = _chunk_starts(c[0], c[1], lcr, lci, True) + (z, z)
        db_ref[...] = jnp.zeros_like(db_ref)
        for k in range(nt - 1, -1, -1):
            c = _scan_tile(S2, lcr, lci, k, c, True, True, aux=S1)
            if k + 1 < nt:
                dbu(k + 1)
        dbu(0)
        gr, gi, dr, di = c
        last = pl.ds((S5_STEPS - 1) * 8, 8)
        xr = _shift_rows(S1[last, 0:512], 1, False)
        xi = _shift_rows(S1[last, 512:1024], 1, False)
        dlam_ref[0] = dr + gr * xr + gi * xi
        dlam_ref[1] = di + gi * xr - gr * xi

    return pl.pallas_call(
        body, name="s5_core_bwd", grid=(S5_SUB,),
        in_specs=[pl.BlockSpec((T, 128), lambda b: (0, b)),
                  pl.BlockSpec((T, 128), lambda b: (0, b)),
                  pl.BlockSpec((T, 1024), lambda b: (0, b)),
                  pl.BlockSpec((None, 128, 1024), lambda b: (b, 0, 0)),
                  pl.BlockSpec((None, 4, 8, 512), lambda b: (b, 0, 0, 0)),
                  pl.BlockSpec((None, 1024, 128), lambda b: (b, 0, 0))],
        out_specs=[pl.BlockSpec((T, 128), lambda b: (0, b)),
                   pl.BlockSpec((None, 128, 1024), lambda b: (b, 0, 0)),
                   pl.BlockSpec((None, 128, 1024), lambda b: (b, 0, 0)),
                   pl.BlockSpec((None, 2, 8, 512), lambda b: (b, 0, 0, 0))],
        out_shape=[SDS((T, D), f32), SDS((8, 128, 1024), f32), SDS((8, 128, 1024), f32), SDS((8, 2, 8, 512), f32)],
        scratch_shapes=[pltpu.VMEM((T, 1024), f32)],
        compiler_params=_cp(dimension_semantics=("arbitrary",)),
    )(hn, dy, xs, bm, lam, cm)


TM = 512
NT = T // TM


def _tile(n=D):
    return pl.BlockSpec((TM, n), lambda i: (i, 0))


def s5_pre(xp, g):
    def body(x_ref, g_ref, hn_ref):
        hn_ref[...] = _rms(x_ref[...], g_ref[...])[0]

    return pl.pallas_call(
        body, name="s5_pre", grid=(NT,), in_specs=[_tile(), _full((1, D))], out_specs=_tile(),
        out_shape=SDS((T, D), f32), compiler_params=_cp(dimension_semantics=("arbitrary",)),
    )(xp, g)


def _gelu_grad(y):
    c = math.sqrt(2.0 / math.pi)
    t = jnp.tanh(c * (y + 0.044715 * y * y * y))
    return 0.5 * (1.0 + t) + 0.5 * y * (1.0 - t * t) * c * (1.0 + 3.0 * 0.044715 * y * y)


def s5_post(ys, xp, g, d, wglu, bglu):
    def body(ys_ref, x_ref, g_ref, d_ref, w_ref, b_ref, y_ref, z_ref, h_ref):
        x = x_ref[...]
        hn, _ = _rms(x, g_ref[...])
        y = ys_ref[...] + d_ref[...] * hn
        y_ref[...] = y
        yg = jax.nn.gelu(y).astype(bf16)
        for j in range(4):
            cv = slice(j * 256, (j + 1) * 256)
            cg = slice(1024 + j * 256, 1024 + (j + 1) * 256)
            val = _dot(yg, w_ref[j]) + b_ref[:, cv]
            gate = _dot(yg, w_ref[j + 4]) + b_ref[:, cg]
            z_ref[:, cv] = val
            z_ref[:, cg] = gate
            h_ref[:, cv] = x[:, cv] + val * jax.nn.sigmoid(gate)

    return pl.pallas_call(
        body, name="s5_post", grid=(NT,),
        in_specs=[_tile(), _tile(), _full((1, D)), _full((1, D)), _full((8, D, 256)), _full((1, 2 * D))],
        out_specs=[_tile(), _tile(2 * D), _tile()],
        out_shape=[SDS((T, D), f32), SDS((T, 2 * D), f32), SDS((T, D), f32)],
        compiler_params=_cp(dimension_semantics=("arbitrary",)),
    )(ys, xp, g, d, wglu, bglu)


def s5_post_bwd(dh, y, z, wglu):
    def body(dh_ref, y_ref, z_ref, w_ref, dy_ref, dw_ref, db_ref, ygs, dzs):
        i = pl.program_id(0)
        rows = pl.ds(pl.multiple_of(i * TM, TM), TM)

        @pl.when(i == 0)
        def _():
            db_ref[...] = jnp.zeros_like(db_ref)

        dh_ = dh_ref[...]
        y = y_ref[...]
        ygs[rows, :] = jax.nn.gelu(y).astype(bf16)
        dyg = jnp.zeros((TM, D), f32)
        for j in range(4):
            cv = slice(j * 256, (j + 1) * 256)
            cg = slice(1024 + j * 256, 1024 + (j + 1) * 256)
            val = z_ref[:, cv]
            sg = jax.nn.sigmoid(z_ref[:, cg])
            dval = dh_[:, cv] * sg
            dgate = dh_[:, cv] * val * sg * (1.0 - sg)
            db_ref[:, cv] += _colsum8(dval)
            db_ref[:, cg] += _colsum8(dgate)
            dvb = dval.astype(bf16)
            dgb = dgate.astype(bf16)
            dzs[rows, cv] = dvb
            dzs[rows, cg] = dgb
            dyg = dyg + _dot_nt(dvb, w_ref[j]) + _dot_nt(dgb, w_ref[j + 4])
        dy_ref[...] = dyg * _gelu_grad(y)

        @pl.when(i == NT - 1)
        def _():
            for half in range(2):
                dw = _dot_tn(ygs[...], dzs[:, half * D:(half + 1) * D])
                for j in range(4):
                    dw_ref[4 * half + j] = dw[:, j * 256:(j + 1) * 256].astype(bf16)

    return pl.pallas_call(
        body, name="s5_post_bwd", grid=(NT,),
        in_specs=[_tile(), _tile(), _tile(2 * D), _full((8, D, 256))],
        out_specs=[_tile(), _full((8, D, 256)), _full((8, 2 * D))],
        out_shape=[SDS((T, D), f32), SDS((8, D, 256), bf16), SDS((8, 2 * D), f32)],
        scratch_shapes=[pltpu.VMEM((T, D), bf16), pltpu.VMEM((T, 2 * D), bf16)],
        compiler_params=_cp(dimension_semantics=("arbitrary",)),
    )(dh, y, z, wglu)


def s5_pre_bwd(xp, g, du, dy, d, dh):
    def body(x_ref, g_ref, du_ref, dy_ref, d_ref, dh_ref, dx_ref, dg_ref, dd_ref):
        i = pl.program_id(0)

        @pl.when(i == 0)
        def _():
            dg_ref[...] = jnp.zeros_like(dg_ref)
            dd_ref[...] = jnp.zeros_like(dd_ref)

        x = x_ref[...]
        g = g_ref[...]
        dy = dy_ref[...]
        hn, _ = _rms(x, g)
        dhn = du_ref[...] + d_ref[...] * dy
        dx, dgt = _rms_bwd(x, g, dhn)
        dx_ref[...] = dh_ref[...] + dx
        dg_ref[...] += _colsum8(dgt)
        dd_ref[...] += _colsum8(dy * hn)

    return pl.pallas_call(
        body, name="s5_pre_bwd", grid=(NT,),
        in_specs=[_tile(), _full((1, D)), _tile(), _tile(), _full((1, D)), _tile()],
        out_specs=[_tile(), _full((8, D)), _full((8, D))],
        out_shape=[SDS((T, D), f32), SDS((8, D), f32), SDS((8, D), f32)],
        compiler_params=_cp(dimension_semantics=("arbitrary",)),
    )(xp, g, du, dy, d, dh)


TMF = 1024


def mlp_fwd(h, g, w_in, w_out, layer):
    def body(h_ref, g_ref, wi_ref, wo_ref, hm_ref, r_ref, out_ref, acc):
        j = pl.program_id(1)

        @pl.when(j == 0)
        def _():
            hm, _ = _rms(h_ref[...], g_ref[...])
            hm_ref[...] = hm.astype(bf16)
            acc[...] = jnp.zeros_like(acc)

        a = jnp.maximum(_dot(hm_ref[...], wi_ref[...]), 0.0)
        r_ref[...] = a.astype(bf16)
        acc[...] += _dot((a * a).astype(bf16), wo_ref[...])

        @pl.when(j == NDEV - 1)
        def _():
            out_ref[...] = h_ref[...] + acc[...]

    return pl.pallas_call(
        body, name=f"mlp_fwd{layer}", grid=(T // TMF, NDEV),
        in_specs=[pl.BlockSpec((TMF, D), lambda i, j: (i, 0)),
                  pl.BlockSpec((1, D), lambda i, j: (0, 0)),
                  pl.BlockSpec((None, D, D_FF_SHARD), lambda i, j: (j, 0, 0)),
                  pl.BlockSpec((None, D_FF_SHARD, D), lambda i, j: (j, 0, 0))],
        out_specs=[pl.BlockSpec((TMF, D), lambda i, j: (i, 0)), pl.BlockSpec((TMF, D_FF_SHARD), lambda i, j: (i, j)),
                   pl.BlockSpec((TMF, D), lambda i, j: (i, 0))],
        out_shape=[SDS((T, D), bf16), SDS((T, NDEV * D_FF_SHARD), bf16), SDS((T, D), f32)],
        scratch_shapes=[pltpu.VMEM((TMF, D), f32)],
        compiler_params=_cp(dimension_semantics=("arbitrary", "arbitrary")),
    )(h, g, w_in, w_out)


def mlp_bwd(h, hm, r, g, dout, dout_b, w_in, w_out, layer):
    def body(h_ref, hm_ref, r_ref, g_ref, do_ref, dob_ref, wi_ref, wo_ref, dh_ref, dwi_ref, dwo_ref, dg_ref,
             dhm, dzs):
        s = pl.program_id(0)

        @pl.when(s == 0)
        def _():
            dhm[...] = jnp.zeros_like(dhm)

        @pl.when(s < NDEV)
        def _():
            for c in range(NT):
                rows = pl.ds(c * TM, TM)
                dz = (_dot_nt(dob_ref[rows, :], wo_ref[...]) * (2.0 * r_ref[rows, :].astype(f32))).astype(bf16)
                dzs[rows, :] = dz
                dhm[rows, :] += _dot_nt(dz, wi_ref[...])
            rb = r_ref[...]
            dwo_ref[...] = _dot_tn(rb * rb, dob_ref[...]).astype(bf16)
            dwi_ref[...] = _dot_tn(hm_ref[...], dzs[...]).astype(bf16)

        @pl.when(s >= NDEV)
        def _():
            @pl.when(s == NDEV)
            def _():
                dg_ref[...] = jnp.zeros_like(dg_ref)
            rows = pl.ds(pl.multiple_of((s - NDEV) * TM, TM), TM)
            dx, dgt = _rms_bwd(h_ref[...], g_ref[...], dhm[rows, :])
            dh_ref[...] = do_ref[...] + dx
            dg_ref[...] += _colsum8(dgt)

    shard = lambda s: (jnp.minimum(s, NDEV - 1), 0, 0)
    tile = lambda s: (jnp.maximum(s - NDEV, 0), 0)
    return pl.pallas_call(
        body, name=f"mlp_bwd{layer}", grid=(NDEV + NT,),
        in_specs=[pl.BlockSpec((TM, D), tile),
                  _full((T, D)),
                  pl.BlockSpec((T, D_FF_SHARD), lambda s: (0, jnp.minimum(s, NDEV - 1))),
                  _full((1, D)),
                  pl.BlockSpec((TM, D), tile),
                  _full((T, D)),
                  pl.BlockSpec((None, D, D_FF_SHARD), shard),
                  pl.BlockSpec((None, D_FF_SHARD, D), shard)],
        out_specs=[pl.BlockSpec((TM, D), tile),
                   pl.BlockSpec((None, D, D_FF_SHARD), shard),
                   pl.BlockSpec((None, D_FF_SHARD, D), shard),
                   pl.BlockSpec((8, D), lambda s: (0, 0))],
        out_shape=[SDS((T, D), f32), SDS((NDEV, D, D_FF_SHARD), bf16), SDS((NDEV, D_FF_SHARD, D), bf16),
                   SDS((8, D), f32)],
        scratch_shapes=[pltpu.VMEM((T, D), f32), pltpu.VMEM((T, D_FF_SHARD), bf16)],
        compiler_params=_cp(dimension_semantics=("arbitrary",)),
    )(h, hm, r, g, dout, dout_b, w_in, w_out)


def _spread4():
    r = lax.broadcasted_iota(jnp.int32, (256, D), 0)
    c = lax.broadcasted_iota(jnp.int32, (256, D), 1)
    return ((c // 256 == r // HEAD_DIM) & (c % HEAD_DIM == r % HEAD_DIM)).astype(bf16)


def attn_pre(h, g_kv, g_mix, wkv, bkv, spread, wq, bq):
    def body(h_ref, gkv_ref, gm_ref, wkv_ref, bkv_ref, sp_ref, wq_ref, bq_ref, kvn_ref, hn_ref, k_ref, v_ref, q_ref):
        h_ = h_ref[...]
        kvn = _rms(h_, gkv_ref[...])[0].astype(bf16)
        hn = _rms(h_, gm_ref[...])[0].astype(bf16)
        kvn_ref[...] = kvn
        hn_ref[...] = hn
        kv = (_dot(kvn, wkv_ref[...]) + bkv_ref[...]).astype(bf16)
        k_ref[...] = _dot(kv[:, :256], sp_ref[...]).astype(bf16)
        v_ref[...] = _dot(kv[:, 256:], sp_ref[...]).astype(bf16)
        q_ref[...] = (_dot(hn, wq_ref[...]) + bq_ref[...]).astype(bf16)

    return pl.pallas_call(
        body, name="attn_pre", grid=(NT,),
        in_specs=[_tile(), _full((1, D)), _full((1, D)), _full((D, 512)), _full((1, 512)), _full((256, D)),
                  _full((D, D)), _full((1, D))],
        out_specs=[_tile()] * 5,
        out_shape=[SDS((T, D), bf16)] * 5,
        compiler_params=_cp(dimension_semantics=("arbitrary",)),
    )(h, g_kv, g_mix, wkv, bkv, spread, wq, bq)


def _attn_specs():
    cur = pl.BlockSpec((TM, 256), lambda j, n: (n, j))
    prev = pl.BlockSpec((BLK, 256), lambda j, n: (jnp.maximum(n * (TM // BLK) - 1, 0), j))
    return cur, prev


def _head_mask(g):
    lane = lax.broadcasted_iota(jnp.int32, (1, 256), 1)
    return (lane >= g * HEAD_DIM) & (lane < (g + 1) * HEAD_DIM)


def _stack_heads(t):
    return jnp.concatenate([jnp.where(_head_mask(g), t, 0) for g in range(Q_PER_KV)], axis=0)


def _unstack_heads(t):
    out = jnp.where(_head_mask(0), t[0:BLK], 0.0)
    for g in range(1, Q_PER_KV):
        out = out + jnp.where(_head_mask(g), t[g * BLK:(g + 1) * BLK], 0.0)
    return out


def _attn_probs(qs, k2, sinks, first):
    rows = Q_PER_KV * BLK
    s = _dot_nt(qs, k2) * (1.0 / math.sqrt(HEAD_DIM))
    qi = jnp.bitwise_and(lax.broadcasted_iota(jnp.int32, (rows, 2 * BLK), 0), BLK - 1)
    kj = lax.broadcasted_iota(jnp.int32, (rows, 2 * BLK), 1)
    diff = qi + BLK - kj
    valid = (diff >= 0) & (diff < BLK) & (jnp.logical_not(first) | (kj >= BLK))
    s = jnp.where(valid, s, -jnp.inf)
    rb = lax.broadcasted_iota(jnp.int32, (rows, 1), 0)
    sink = jnp.where(rb < BLK, sinks[0], jnp.where(rb < 2 * BLK, sinks[1], jnp.where(rb < 3 * BLK, sinks[2], sinks[3])))
    m = jnp.maximum(jnp.max(s, axis=-1, keepdims=True), sink)
    p = jnp.exp(s - m)
    ps = jnp.exp(sink - m)
    denom = jnp.sum(p, axis=-1, keepdims=True) + ps
    return p / denom, ps / denom


def _window_blocks(b, n, kc_ref, kp_ref, vc_ref, vp_ref):
    if b == 0:
        return (jnp.concatenate([kp_ref[...], kc_ref[0:BLK, :]], axis=0),
                jnp.concatenate([vp_ref[...], vc_ref[0:BLK, :]], axis=0), n == 0)
    rows = pl.ds((b - 1) * BLK, 2 * BLK)
    return kc_ref[rows, :], vc_ref[rows, :], False


def attn_core_fwd(q, k4, v4, sinks):
    nb = TM // BLK

    def body(sink_ref, q_ref, kc_ref, kp_ref, vc_ref, vp_ref, o_ref, a_ref, as_ref):
        j = pl.program_id(0)
        n = pl.program_id(1)
        sk = [sink_ref[j * Q_PER_KV + g] for g in range(Q_PER_KV)]
        for b in range(nb):
            qb = q_ref[b * BLK:(b + 1) * BLK, :]
            k2, v2, first = _window_blocks(b, n, kc_ref, kp_ref, vc_ref, vp_ref)
            a, asink = _attn_probs(_stack_heads(qb), k2, sk, first)
            ab = a.astype(bf16)
            a_ref[b] = ab
            as_ref[b] = jnp.broadcast_to(asink, (Q_PER_KV * BLK, 128)).astype(bf16)
            o_ref[b * BLK:(b + 1) * BLK, :] = _unstack_heads(_dot(ab, v2)).astype(bf16)

    cur, prev = _attn_specs()
    rows = Q_PER_KV * BLK
    return pl.pallas_call(
        body, name="attn_core_fwd", grid=(N_KV, NT),
        in_specs=[pl.BlockSpec(memory_space=pltpu.SMEM), cur, cur, prev, cur, prev],
        out_specs=[cur, pl.BlockSpec((None, nb, rows, 2 * BLK), lambda j, n: (j, n, 0, 0)),
                   pl.BlockSpec((None, nb, rows, 128), lambda j, n: (j, n, 0, 0))],
        out_shape=[SDS((T, D), bf16), SDS((N_KV, T // BLK, rows, 2 * BLK), bf16), SDS((N_KV, T // BLK, rows, 128), bf16)],
        compiler_params=_cp(dimension_semantics=("arbitrary", "arbitrary")),
    )(sinks, q, k4, k4, v4, v4)


def attn_post(h, o, wo, bo):
    def body(h_ref, o_ref, w_ref, b_ref, out_ref):
        out_ref[...] = h_ref[...] + _dot(o_ref[...], w_ref[...]) + b_ref[...]

    return pl.pallas_call(
        body, name="attn_post", grid=(NT,), in_specs=[_tile(), _tile(), _full((D, D)), _full((1, D))],
        out_specs=_tile(), out_shape=SDS((T, D), f32), compiler_params=_cp(dimension_semantics=("arbitrary",)),
    )(h, o, wo, bo)


def attn_bwd_pre(dh, o, wo):
    def body(dh_ref, o_ref, w_ref, do_ref, dw_ref, db_ref, acc):
        i = pl.program_id(0)

        @pl.when(i == 0)
        def _():
            acc[...] = jnp.zeros_like(acc)
            db_ref[...] = jnp.zeros_like(db_ref)

        dh_ = dh_ref[...]
        dhb = dh_.astype(bf16)
        do_ref[...] = _dot_nt(dhb, w_ref[...]).astype(bf16)
        acc[...] += _dot_tn(o_ref[...], dhb)
        db_ref[...] += _colsum8(dh_)

        @pl.when(i == NT - 1)
        def _():
            dw_ref[...] = acc[...].astype(bf16)

    return pl.pallas_call(
        body, name="attn_bwd_pre", grid=(NT,), in_specs=[_tile(), _tile(), _full((D, D))],
        out_specs=[_tile(), _full((D, D)), _full((8, D))],
        out_shape=[SDS((T, D), bf16), SDS((D, D), bf16), SDS((8, D), f32)],
        scratch_shapes=[pltpu.VMEM((D, D), f32)],
        compiler_params=_cp(dimension_semantics=("arbitrary",)),
    )(dh, o, wo)


def attn_core_bwd(q, do, k4, v4, probs, sink_w):
    nb = TM // BLK

    def body(q_ref, do_ref, kc_ref, kp_ref, vc_ref, vp_ref, a_ref, as_ref, dq_ref, dk_ref, dv_ref, ds_ref):
        j = pl.program_id(0)
        n = pl.program_id(1)

        @pl.when(n == 0)
        def _():
            dk_ref[...] = jnp.zeros_like(dk_ref)
            dv_ref[...] = jnp.zeros_like(dv_ref)
            ds_ref[...] = jnp.zeros_like(ds_ref)

        lane8 = lax.broadcasted_iota(jnp.int32, (8, 128), 1)
        row8 = lax.broadcasted_iota(jnp.int32, (8, 128), 0)
        for b in range(nb):
            qs = _stack_heads(q_ref[b * BLK:(b + 1) * BLK, :])
            dos = _stack_heads(do_ref[b * BLK:(b + 1) * BLK, :])
            k2, v2, _ = _window_blocks(b, n, kc_ref, kp_ref, vc_ref, vp_ref)
            ab = a_ref[b]
            a = ab.astype(f32)
            asink = as_ref[b][:, 0:1].astype(f32)
            dp = _dot_nt(dos, v2)
            dd = jnp.sum(a * dp, axis=-1, keepdims=True)
            dsc = (a * (dp - dd) * (1.0 / math.sqrt(HEAD_DIM))).astype(bf16)
            t = asink * dd
            for g in range(Q_PER_KV):
                dsink = -jnp.sum(t[g * BLK:(g + 1) * BLK], axis=0, keepdims=True)
                ds_ref[...] += jnp.where((lane8 == g) & (row8 == 0), jnp.broadcast_to(dsink, (8, 128)), 0.0)
            dq_ref[b * BLK:(b + 1) * BLK, :] = _unstack_heads(_dot(dsc, k2))
            dk2 = _dot_tn(dsc, qs)
            dv2 = _dot_tn(ab, dos)
            cur = pl.ds(pl.multiple_of(n * TM + b * BLK, BLK), BLK)
            dk_ref[cur, :] += dk2[BLK:, :]
            dv_ref[cur, :] += dv2[BLK:, :]
            prv = pl.ds(pl.multiple_of(jnp.maximum(n * TM + (b - 1) * BLK, 0), BLK), BLK)
            dk_ref[prv, :] += dk2[:BLK, :]
            dv_ref[prv, :] += dv2[:BLK, :]

    cur, prev = _attn_specs()
    col = pl.BlockSpec((T, 256), lambda j, n: (0, j))
    rows = Q_PER_KV * BLK
    return pl.pallas_call(
        body, name="attn_core_bwd", grid=(N_KV, NT),
        in_specs=[cur, cur, cur, prev, cur, prev,
                  pl.BlockSpec((None, nb, rows, 2 * BLK), lambda j, n: (j, n, 0, 0)),
                  pl.BlockSpec((None, nb, rows, 128), lambda j, n: (j, n, 0, 0))],
        out_specs=[cur, col, col, pl.BlockSpec((None, 8, 128), lambda j, n: (j, 0, 0))],
        out_shape=[SDS((T, D), f32), SDS((T, D), f32), SDS((T, D), f32), SDS((N_KV, 8, 128), f32)],
        compiler_params=_cp(dimension_semantics=("arbitrary", "arbitrary")),
    )(q, do, k4, k4, v4, v4, probs, sink_w)


def attn_bwd_q(h, dh, dq, hn, g_mix, wq):
    def body(h_ref, dh_ref, dq_ref, hn_ref, gm_ref, wq_ref, out_ref, dwq_ref, dbq_ref, dgm_ref, aq):
        i = pl.program_id(0)

        @pl.when(i == 0)
        def _():
            aq[...] = jnp.zeros_like(aq)
            dbq_ref[...] = jnp.zeros_like(dbq_ref)
            dgm_ref[...] = jnp.zeros_like(dgm_ref)

        dq_ = dq_ref[...]
        dqb = dq_.astype(bf16)
        aq[...] += _dot_tn(hn_ref[...], dqb)
        dbq_ref[...] += _colsum8(dq_)
        dx, dg = _rms_bwd(h_ref[...], gm_ref[...], _dot_nt(dqb, wq_ref[...]))
        out_ref[...] = dh_ref[...] + dx
        dgm_ref[...] += _colsum8(dg)

        @pl.when(i == NT - 1)
        def _():
            dwq_ref[...] = aq[...].astype(bf16)

    vec = _full((8, D))
    mat = _full((D, D))
    return pl.pallas_call(
        body, name="attn_bwd_q", grid=(NT,),
        in_specs=[_tile()] * 4 + [_full((1, D)), mat],
        out_specs=[_tile(), mat, vec, vec],
        out_shape=[SDS((T, D), f32), SDS((D, D), bf16), SDS((8, D), f32), SDS((8, D), f32)],
        scratch_shapes=[pltpu.VMEM((D, D), f32)],
        compiler_params=_cp(dimension_semantics=("arbitrary",)),
    )(h, dh, dq, hn, g_mix, wq)


def attn_bwd_kv(h, dh, dk4, dv4, kvn, g_kv, wkv, spread):
    def body(h_ref, dh_ref, dk_ref, dv_ref, kvn_ref, gkv_ref, wkv_ref, sp_ref, out_ref, outb_ref, dw_ref, db_ref,
             dgkv_ref, acc):
        i = pl.program_id(0)

        @pl.when(i == 0)
        def _():
            for r in (acc, db_ref, dgkv_ref):
                r[...] = jnp.zeros_like(r)

        dkv = jnp.concatenate([_dot_nt(dk_ref[...].astype(bf16), sp_ref[...]),
                               _dot_nt(dv_ref[...].astype(bf16), sp_ref[...])], axis=1)
        dkvb = dkv.astype(bf16)
        acc[...] += _dot_tn(kvn_ref[...], dkvb)
        db_ref[...] += _colsum8(dkv)
        dx, dg = _rms_bwd(h_ref[...], gkv_ref[...], _dot_nt(dkvb, wkv_ref[...]))
        out = dh_ref[...] + dx
        out_ref[...] = out
        outb_ref[...] = out.astype(bf16)
        dgkv_ref[...] += _colsum8(dg)

        @pl.when(i == NT - 1)
        def _():
            dw_ref[...] = acc[...].astype(bf16)

    return pl.pallas_call(
        body, name="attn_bwd_kv", grid=(NT,),
        in_specs=[_tile()] * 5 + [_full((1, D)), _full((D, 512)), _full((256, D))],
        out_specs=[_tile(), _tile(), _full((D, 512)), _full((8, 512)), _full((8, D))],
        out_shape=[SDS((T, D), f32), SDS((T, D), bf16), SDS((D, 512), bf16), SDS((8, 512), f32), SDS((8, D), f32)],
        scratch_shapes=[pltpu.VMEM((D, 512), f32)],
        compiler_params=_cp(dimension_semantics=("arbitrary",)),
    )(h, dh, dk4, dv4, kvn, g_kv, wkv, spread)


def final_loss(h, g, target):
    def body(h_ref, g_ref, t_ref, loss_ref, dh_ref, dhb_ref, dg_ref):
        i = pl.program_id(0)

        @pl.when(i == 0)
        def _():
            loss_ref[...] = jnp.zeros_like(loss_ref)
            dg_ref[...] = jnp.zeros_like(dg_ref)

        h_ = h_ref[...]
        g_ = g_ref[...]
        y, _ = _rms(h_, g_)
        diff = y - t_ref[...]
        per_tok = jnp.mean(diff * diff, axis=-1, keepdims=True)
        tot = 0.5 * jnp.sum(per_tok, axis=0, keepdims=True)
        lane = lax.broadcasted_iota(jnp.int32, (8, 128), 1)
        row = lax.broadcasted_iota(jnp.int32, (8, 128), 0)
        loss_ref[...] += jnp.where((lane == 0) & (row == 0), jnp.broadcast_to(tot, (8, 128)), 0.0)
        dx, dgt = _rms_bwd(h_, g_, diff * (1.0 / D))
        dh_ref[...] = dx
        dhb_ref[...] = dx.astype(bf16)
        dg_ref[...] += _colsum8(dgt)

    return pl.pallas_call(
        body, name="final_loss", grid=(NT,), in_specs=[_tile(), _full((1, D)), _tile()],
        out_specs=[_full((8, 128)), _tile(), _tile(), _full((8, D))],
        out_shape=[SDS((8, 128), f32), SDS((T, D), f32), SDS((T, D), bf16), SDS((8, D), f32)],
        compiler_params=_cp(dimension_semantics=("arbitrary",)),
    )(h, g, target)


def fwd_bwd(x, target, p, shards, opt, core, chip):
    row = lambda v: v.reshape(1, -1)
    (lam, bm, cm), prep_vjp = jax.vjp(s5_discretize, p["s5_a_re"][0], p["s5_a_im"][0], p["s5_log_dt"][0],
                                      p["s5_b_re"][0], p["s5_b_im"][0], p["s5_c_re"][0], p["s5_c_im"][0])
    bmb, cmb = bm.astype(bf16), cm.astype(bf16)
    lam = jnp.concatenate([lam, lam * jnp.array([1.0, -1.0], f32).reshape(1, 2, 1, 1)], axis=1)
    g_mix0, g_mix1 = row(p["norm_mix"][0]), row(p["norm_mix"][1])
    g_mlp0, g_mlp1 = row(p["norm_mlp"][0]), row(p["norm_mlp"][1])
    g_kv, g_fin = row(p["norm_kv"]), row(p["norm_final"])
    bq, bo = p["b_q"], p["b_o"]
    bkv = row(p["b_kv"])
    spread = _spread4()
    sinks = p["sinks"].reshape(16)

    wglu, gvec = sc_gather([shards["s5_w_glu"], shards["vecs"]], 3, "sc_gather_s5")
    win0, wout0 = sc_gather([shards["w_in0"], shards["w_out0"]], 14, "sc_gather_mlp0")
    wkv, wq, wo = sc_gather([shards["w_kv"], shards["w_q"], shards["w_o"]], 4, "sc_gather_attn")
    win1, wout1 = sc_gather([shards["w_in1"], shards["w_out1"]], 5, "sc_gather_mlp1")
    xp = x
    hn0 = s5_pre(xp, g_mix0)
    ys, xs = s5_core_fwd(hn0, bmb, lam, cmb)
    d_skip = gvec[:, 0, :128].reshape(1, D)
    bglu = gvec[:, 0, 128:].reshape(1, 2 * D)
    y, z, h1 = s5_post(ys, xp, g_mix0, d_skip, wglu, bglu)
    hm0, r0, h2p = mlp_fwd(h1, g_mlp0, win0, wout0, 0)
    wkv, wq, wo = wkv.reshape(D, 512), wq.reshape(D, D), wo.reshape(D, D)
    h2 = h2p
    kvn, hn1, k4, v4, q = attn_pre(h2, g_kv, g_mix1, wkv, bkv, spread, wq, bq)
    o, probs, sink_w = attn_core_fwd(q, k4, v4, sinks)
    h3 = attn_post(h2, o, wo, bo)
    hm1, r1, h4 = mlp_fwd(h3, g_mlp1, win1, wout1, 1)
    loss, dh4, dh4b, dg_fin = final_loss(h4, g_fin, target)

    def pair_sums(names, grads, cid, before):
        r1 = sc_comm(BgPair(grads), cid, "sc_pair_" + names[0])
        parts = add_pairs(grads, r1, core, "add_pairs_" + names[0])
        before, parts = lax.optimization_barrier((before, parts))
        return before, parts

    def across_chips(names, parts, cid):
        return list(zip(parts, sc_comm(BgChips(parts), cid, "sc_chips_" + names[0])))

    dh3, dwin1, dwout1, dg_mlp1 = mlp_bwd(h3, hm1, r1, g_mlp1, dh4, dh4b, win1, wout1, 1)
    do, dwo, dbo = attn_bwd_pre(dh3, o, wo)
    do, parts = pair_sums(["w_in1", "w_out1"], [dwin1, dwout1], 6, do)
    rs_in1, rs_out1 = across_chips(["w_in1", "w_out1"], parts, 7)
    dq, dk4, dv4, dsink = attn_core_bwd(q, do, k4, v4, probs, sink_w)
    dh2, dwq, dbq, dg_mix1 = attn_bwd_q(h2, dh3, dq, hn1, g_mix1, wq)
    dh2, dh2b, dwkv, dbkv, dg_kv = attn_bwd_kv(h2, dh2, dk4, dv4, kvn, g_kv, wkv, spread)
    dh2p, dh2pb = dh2, dh2b
    big = {}
    def adam_group(group, rss, name, **kw):
        return adam_big(*zip(*[opt[n] for n in group]), *zip(*rss), chip, name, **kw)

    mlp = ["w_mlp_in", "w_mlp_out"]
    a_in1, a_out1 = adam_group(mlp, [rs_in1, rs_out1], "adam_w_mlp1", layer=1)
    dh2p, a_in1, a_out1 = lax.optimization_barrier((dh2p, a_in1, a_out1))
    names = ["w_kv", "w_q", "w_o"]
    dh2p, parts = pair_sums(names, [dwkv.reshape(NDEV, 128, 512), dwq.reshape(NDEV, 128, D),
                                    dwo.reshape(NDEV, 128, D)], 8, dh2p)
    rs_attn = across_chips(names, parts, 9)
    dh1, dwin0, dwout0, dg_mlp0 = mlp_bwd(h1, hm0, r0, g_mlp0, dh2p, dh2pb, win0, wout0, 0)
    a_attn = adam_group(names, rs_attn, "adam_attn")
    dh1, a_attn = lax.optimization_barrier((dh1, a_attn))
    big.update(zip(names, a_attn))
    dy, dwglu, dbglu = s5_post_bwd(dh1, y, z, wglu)
    dy, parts = pair_sums(["w_in0", "w_out0"], [dwin0, dwout0], 10, dy)
    rs_in0, rs_out0 = across_chips(["w_in0", "w_out0"], parts, 11)
    du, dbm, dcmt, dlam = s5_core_bwd(hn0, dy, xs, bmb, lam, cmb)
    du, parts = pair_sums(["s5_w_glu"], [dwglu], 12, du)
    rs_glu, = across_chips(["s5_w_glu"], parts, 13)
    dxp, dg_mix0, dd = s5_pre_bwd(xp, g_mix0, du, dy, d_skip, dh1)
    big["w_mlp_in"], big["w_mlp_out"] = adam_group(mlp, [rs_in0, rs_out0], "adam_w_mlp0", layer=0,
                                                   prev=[a_in1, a_out1])
    big["s5_w_glu"], = adam_group(["s5_w_glu"], [rs_glu], "adam_s5_w_glu")
    grad_x = dxp
    da_re, da_im, dlog_dt, db_re, db_im, dc_re, dc_im = prep_vjp((dlam, dbm, dcmt.transpose(0, 2, 1)))

    def lanes(v_):
        v_ = v_.reshape(1, -1)
        return jnp.pad(v_, ((0, 0), (0, D - v_.shape[1])))

    small = dict(
        rows8=[dg_mix0, dg_mix1, dg_mlp0, dg_mlp1, dg_kv, dg_fin, dd, dbq, dbo], b_glu=dbglu, b_kv=dbkv,
        misc=jnp.concatenate([lanes(dsink[:, 0, :Q_PER_KV]), lanes(dlog_dt), lanes(loss[0:1, 0:1])], axis=0),
        s5=[da_re.reshape(4, D), da_im.reshape(4, D),
            db_re.transpose(0, 2, 1).reshape(64, D), db_im.transpose(0, 2, 1).reshape(64, D),
            dc_re.reshape(64, D), dc_im.reshape(64, D)])
    small, big["w_mlp_in"], big["w_mlp_out"] = lax.optimization_barrier((small, big["w_mlp_in"], big["w_mlp_out"]))
    return loss, grad_x, small, big


def add_pairs(gs, r1s, core, name):
    n = len(gs)

    def body(core_ref, *refs):
        for g_ref, r_ref, o_ref in zip(refs[:n], refs[n:2 * n], refs[2 * n:]):
            o_ref[...] = (g_ref[...].astype(f32) + r_ref[...].astype(f32)).astype(bf16)

    mine = [pl.BlockSpec((None,) + g.shape[1:], lambda k, core: (2 * k + core[0], 0, 0)) for g in gs]
    slot = [pl.BlockSpec((None,) + g.shape[1:], lambda k, core: (k, 0, 0)) for g in gs]
    return pl.pallas_call(
        body, name=name, out_shape=[SDS((4,) + g.shape[1:], bf16) for g in gs],
        grid_spec=pltpu.PrefetchScalarGridSpec(num_scalar_prefetch=1, grid=(4,), in_specs=mine + slot, out_specs=slot),
        compiler_params=_cp(dimension_semantics=("arbitrary",)),
    )(core, *gs, *r1s)


def _adamw(w, g, m, v):
    m = ADAM_B1 * m + (1.0 - ADAM_B1) * g
    v = ADAM_B2 * v + (1.0 - ADAM_B2) * (g * g)
    m_hat = m / (1.0 - ADAM_B1 ** ADAM_STEP)
    v_hat = v / (1.0 - ADAM_B2 ** ADAM_STEP)
    delta = -ADAM_LR * (m_hat / (jnp.sqrt(v_hat) + ADAM_EPS) + ADAM_WD * w)
    return delta, m, v


ADAM_STEPS = 4


def adam_big(ws, ms, vs, parts, r2s, chip, name, layer=0, prev=None):
    n = len(ws)

    def body(chip_ref, *refs):
        outs = refs[len(refs) - 4 * n:]
        for a in range(n):
            w_ref, m_ref, v_ref, p_ref, r_ref = (refs[k * n + a] for k in range(5))
            g = p_ref[...].astype(f32) + r_ref[0].astype(f32) + r_ref[1].astype(f32) + r_ref[2].astype(f32)
            d, m_, v_ = _adamw(w_ref[...], g, m_ref[...], v_ref[...])
            for o_ref, val in zip(outs[4 * a:4 * a + 4], (g, d, m_, v_)):
                o_ref[...] = val

    tiles = [(w.shape[1] // ADAM_STEPS, w.shape[2]) for w in ws]
    blk = [pl.BlockSpec((None,) + t, lambda i, chip: (layer, i, 0)) for t in tiles]
    extra = [] if prev is None else [arr for four in prev for arr in four]
    res = pl.pallas_call(
        body, name=name, out_shape=[SDS(w.shape, f32) for w in ws for _ in range(4)],
        grid_spec=pltpu.PrefetchScalarGridSpec(
            num_scalar_prefetch=1, grid=(ADAM_STEPS,),
            in_specs=blk * 3 + [pl.BlockSpec((None,) + t, lambda i, chip: (chip[0], i, 0)) for t in tiles]
            + [pl.BlockSpec((3,) + t, lambda i, chip: (0, i, 0)) for t in tiles] + [_ANY] * len(extra),
            out_specs=[b for b in blk for _ in range(4)]),
        input_output_aliases={1 + 5 * n + k: k for k in range(len(extra))},
        compiler_params=_cp(dimension_semantics=("arbitrary",)),
    )(chip, *ws, *ms, *vs, *parts, *r2s, *extra)
    return [tuple(res[4 * a:4 * a + 4]) for a in range(n)]


SMALL_BUF_ROWS = 288


def allreduce_small(rows8, b_glu, b_kv, misc, s5):
    R = SMALL_BUF_ROWS
    half, quarter = R // 2, R // 4
    pieces = [*rows8, b_glu, b_kv, misc, *s5]

    def body(*refs):
        ins, (out_ref, in_ref, acc1, acc2, r0, r1, r2, send_sems, recv_sems) = refs[:len(pieces)], refs[len(pieces):]
        in_ref[8:16, :] = jnp.zeros((8, D), f32)
        in_ref[R - 8:R, :] = jnp.zeros((8, D), f32)
        for k in range(len(rows8)):
            in_ref[k:k + 1, :] = ins[k][0:1, :]
        glu_ref, kv_ref, misc_ref = ins[len(rows8):len(rows8) + 3]
        in_ref[9:10, :] = glu_ref[0:1, 0:D]
        in_ref[10:11, :] = glu_ref[0:1, D:2 * D]
        in_ref[11:12, 0:kv_ref.shape[1]] = kv_ref[0:1, :]
        in_ref[12:15, :] = misc_ref[...]
        row = 16
        for a in ins[len(rows8) + 3:]:
            in_ref[row:row + a.shape[0], :] = a[...]
            row += a.shape[0]
        x, y, c = _pos()
        sibling, over_x, over_y = (x, y, 1 - c), (1 - x, y, c), (x, 1 - y, c)
        first = pl.multiple_of(c * half, 8)
        mine = pl.ds(first, half)
        theirs = pl.ds(pl.multiple_of((1 - c) * half, 8), half)
        qa = pl.ds(first, quarter)
        qb = pl.ds(pl.multiple_of(first + quarter, 8), quarter)

        def exchange(copies):
            cps = [pltpu.make_async_remote_copy(
                src_ref=src.at[rows], dst_ref=dst.at[rows], send_sem=send_sems.at[k], recv_sem=recv_sems.at[k],
                device_id=peer, device_id_type=MESH) for k, src, dst, rows, peer in copies]
            for cp in cps:
                cp.start()
            for cp in cps:
                cp.wait()

        exchange([(0, in_ref, r0, theirs, sibling)])
        acc1[mine, :] = in_ref[mine, :] + r0[mine, :]
        exchange([(1, acc1, r1, qa, over_x), (2, acc1, r1, qb, over_y)])
        acc2[mine, :] = acc1[mine, :] + r1[mine, :]
        exchange([(3, acc2, r2, qa, over_y), (4, acc2, r2, qb, over_x)])
        out_ref[mine, :] = acc2[mine, :] + r2[mine, :]
        exchange([(5, out_ref, out_ref, mine, sibling)])

    vm = pl.BlockSpec(memory_space=pltpu.VMEM)
    return pl.pallas_call(
        body, name="allreduce_small", in_specs=[vm] * len(pieces), out_specs=vm, out_shape=SDS((R, D), f32),
        scratch_shapes=[pltpu.VMEM((R, D), f32)] * 6 + [pltpu.SemaphoreType.DMA((6,)), pltpu.SemaphoreType.DMA((6,))],
    )(*pieces)


SMALL_ROWS = {'norm_mix': (0, 2, D), 'norm_mlp': (2, 2, D), 'norm_kv': (4, 1, D), 'norm_final': (5, 1, D),
              's5_d': (6, 1, D), 'b_q': (7, 1, D), 'b_o': (8, 1, D), 's5_b_glu': (9, 2, D), 'b_kv': (11, 1, 512),
              'sinks': (12, 1, 16), 's5_log_dt': (13, 1, 64), 's5_a_re': (16, 4, D), 's5_a_im': (20, 4, D),
              's5_b_re': (24, 64, D), 's5_b_im': (88, 64, D), 's5_c_re': (152, 64, D), 's5_c_im': (216, 64, D)}
LOSS_ROW = 14
ROW_PARAMS = ['norm_mix', 'norm_mlp', 'norm_kv', 'norm_final', 'b_q', 'b_o', 'b_kv', 'sinks', 's5_log_dt']
SHARD_PARAMS = ['s5_d', 's5_b_glu']
S5_PARAMS = ['s5_a_re', 's5_a_im', 's5_b_re', 's5_b_im', 's5_c_re', 's5_c_im']


def adam_small(dev, gsum, s5_grads, w, m, v):
    names = ROW_PARAMS + SHARD_PARAMS + S5_PARAMS
    n_g = len(ROW_PARAMS) + len(SHARD_PARAMS)

    def body(dev_ref, gs_ref, *refs):
        pos = [0]

        def take(k):
            r = refs[pos[0]:pos[0] + k]
            pos[0] += k
            return r

        g5 = take(len(S5_PARAMS))
        wr, mr, vr = take(len(names)), take(len(names)), take(len(names))
        g_out = take(n_g)
        d_out, m_out, v_out = take(len(names)), take(len(names)), take(len(names))
        dv = dev_ref[0]
        for i, n in enumerate(names):
            if n in S5_PARAMS:
                g = g5[S5_PARAMS.index(n)][...]
            elif n in SHARD_PARAMS:
                r0, _, _ = SMALL_ROWS[n]
                ln = wr[i].shape[1]
                g = jnp.zeros((1, ln), f32)
                for k in range(NDEV):
                    off = k * ln
                    piece = gs_ref[r0 + off // D:r0 + off // D + 1, off % D:off % D + ln]
                    g = g + jnp.where(dv == k, piece, 0.0)
                g_out[i][...] = g
            else:
                r0, nr, nl = SMALL_ROWS[n]
                g = gs_ref[r0:r0 + nr, 0:nl]
                g_out[i][...] = g
            d, m_, v_ = _adamw(wr[i][...], g, mr[i][...], vr[i][...])
            d_out[i][...] = d
            m_out[i][...] = m_
            v_out[i][...] = v_

    vm = pl.BlockSpec(memory_space=pltpu.VMEM)
    ins = [s5_grads[n] for n in S5_PARAMS] + [d[n] for d in (w, m, v) for n in names]
    shapes = [SDS(w[n].shape, f32) for n in names]
    res = pl.pallas_call(
        body, name="adam_small", in_specs=[pl.BlockSpec(memory_space=pltpu.SMEM)] + [vm] * (1 + len(ins)),
        out_specs=[vm] * (n_g + 3 * len(names)), out_shape=shapes[:n_g] + shapes * 3,
        compiler_params=_cp(),
    )(dev, gsum, *ins)
    g_o = dict(zip(names[:n_g], res[:n_g]))
    rest = res[n_g:]
    k = len(names)
    return g_o, dict(zip(names, rest[:k])), dict(zip(names, rest[k:2 * k])), dict(zip(names, rest[2 * k:]))


WEIGHTS = ['norm_mix', 'norm_mlp', 'norm_kv', 'norm_final', 's5_a_re', 's5_a_im', 's5_log_dt', 's5_b_re', 's5_b_im',
           's5_c_re', 's5_c_im', 's5_d', 's5_w_glu', 's5_b_glu', 'w_kv', 'b_kv', 'w_q', 'b_q', 'sinks', 'w_o', 'b_o',
           'w_mlp_in', 'w_mlp_out']
BIG = ['s5_w_glu', 'w_kv', 'w_q', 'w_o', 'w_mlp_in', 'w_mlp_out']
BIG_2D = {'s5_w_glu': (D, 256), 'w_kv': (128, 512), 'w_q': (128, D), 'w_o': (128, D), 'w_mlp_in': (2 * D, 512),
          'w_mlp_out': (2 * 512, D)}
SMALL = [n for n in WEIGHTS if n not in BIG]


def kernel(x, norm_mix, norm_mlp, norm_kv, norm_final, s5_a_re, s5_a_im, s5_log_dt, s5_b_re, s5_b_im, s5_c_re, s5_c_im, s5_d, s5_w_glu, s5_b_glu, w_kv, b_kv, w_q, b_q, sinks, w_o, b_o, w_mlp_in, w_mlp_out, loss_target, m_norm_mix, m_norm_mlp, m_norm_kv, m_norm_final, m_s5_a_re, m_s5_a_im, m_s5_log_dt, m_s5_b_re, m_s5_b_im, m_s5_c_re, m_s5_c_im, m_s5_d, m_s5_w_glu, m_s5_b_glu, m_w_kv, m_b_kv, m_w_q, m_b_q, m_sinks, m_w_o, m_b_o, m_w_mlp_in, m_w_mlp_out, v_norm_mix, v_norm_mlp, v_norm_kv, v_norm_final, v_s5_a_re, v_s5_a_im, v_s5_log_dt, v_s5_b_re, v_s5_b_im, v_s5_c_re, v_s5_c_im, v_s5_d, v_s5_w_glu, v_s5_b_glu, v_w_kv, v_b_kv, v_w_q, v_b_q, v_sinks, v_w_o, v_b_o, v_w_mlp_in, v_w_mlp_out):
    w = dict(norm_mix=norm_mix, norm_mlp=norm_mlp, norm_kv=norm_kv, norm_final=norm_final, s5_a_re=s5_a_re,
             s5_a_im=s5_a_im, s5_log_dt=s5_log_dt, s5_b_re=s5_b_re, s5_b_im=s5_b_im, s5_c_re=s5_c_re, s5_c_im=s5_c_im,
             s5_d=s5_d, s5_w_glu=s5_w_glu, s5_b_glu=s5_b_glu, w_kv=w_kv, b_kv=b_kv, w_q=w_q, b_q=b_q, sinks=sinks,
             w_o=w_o, b_o=b_o, w_mlp_in=w_mlp_in, w_mlp_out=w_mlp_out)
    m = dict(norm_mix=m_norm_mix, norm_mlp=m_norm_mlp, norm_kv=m_norm_kv, norm_final=m_norm_final, s5_a_re=m_s5_a_re,
             s5_a_im=m_s5_a_im, s5_log_dt=m_s5_log_dt, s5_b_re=m_s5_b_re, s5_b_im=m_s5_b_im, s5_c_re=m_s5_c_re,
             s5_c_im=m_s5_c_im, s5_d=m_s5_d, s5_w_glu=m_s5_w_glu, s5_b_glu=m_s5_b_glu, w_kv=m_w_kv, b_kv=m_b_kv,
             w_q=m_w_q, b_q=m_b_q, sinks=m_sinks, w_o=m_w_o, b_o=m_b_o, w_mlp_in=m_w_mlp_in, w_mlp_out=m_w_mlp_out)
    v = dict(norm_mix=v_norm_mix, norm_mlp=v_norm_mlp, norm_kv=v_norm_kv, norm_final=v_norm_final, s5_a_re=v_s5_a_re,
             s5_a_im=v_s5_a_im, s5_log_dt=v_s5_log_dt, s5_b_re=v_s5_b_re, s5_b_im=v_s5_b_im, s5_c_re=v_s5_c_re,
             s5_c_im=v_s5_c_im, s5_d=v_s5_d, s5_w_glu=v_s5_w_glu, s5_b_glu=v_s5_b_glu, w_kv=v_w_kv, b_kv=v_b_kv,
             w_q=v_w_q, b_q=v_b_q, sinks=v_sinks, w_o=v_w_o, b_o=v_b_o, w_mlp_in=v_w_mlp_in, w_mlp_out=v_w_mlp_out)
    xi, yi, ci = _pos()
    dev = 4 * xi + 2 * yi + ci
    core = ci.reshape(1).astype(jnp.int32)
    chip = (2 * xi + yi).reshape(1).astype(jnp.int32)

    shards = {
        "s5_w_glu": s5_w_glu[0].astype(bf16), "w_kv": w_kv.astype(bf16), "w_q": w_q[0].astype(bf16),
        "w_o": w_o[0].astype(bf16), "w_in0": w_mlp_in[0].astype(bf16), "w_in1": w_mlp_in[1].astype(bf16),
        "w_out0": w_mlp_out[0].astype(bf16), "w_out1": w_mlp_out[1].astype(bf16),
        "vecs": jnp.broadcast_to(jnp.concatenate([s5_d, s5_b_glu], axis=1), (8, 384)),
    }
    as3d = lambda a, n: a if a.ndim == 3 and a.shape[0] == 2 else a.reshape((1,) + BIG_2D[n])
    opt = {n: (as3d(w[n], n), as3d(m[n], n), as3d(v[n], n)) for n in BIG}
    _, grad_x, grads, big = fwd_bwd(x[0], loss_target[0], {n: w[n] for n in SMALL}, shards, opt, core, chip)

    gsum = allreduce_small(**grads)

    out_g, out_d, out_m, out_v = {}, {}, {}, {}
    for n in BIG:
        out_g[n], out_d[n], out_m[n], out_v[n] = [r.reshape(w[n].shape) for r in big[n]]

    loss = gsum[LOSS_ROW, 0]
    swapped = ("s5_b_re", "s5_b_im")
    swap = lambda a: a.transpose(0, 1, 3, 2)

    def kernel_side(d):
        d = {n: (d[n].reshape(1, -1) if d[n].ndim == 1 else d[n]) for n in SMALL}
        d.update({n: swap(d[n]) for n in swapped})
        return d

    s5_g = {}
    for n in S5_PARAMS:
        r0, nr, _ = SMALL_ROWS[n]
        s5_g[n] = gsum[r0:r0 + nr].reshape((1, 64, 16, 64) if n in swapped else w[n].shape)
        out_g[n] = s5_g[n]
    g_s, d_s, m_s, v_s = adam_small(dev.reshape(1).astype(jnp.int32), gsum, s5_g, kernel_side(w), kernel_side(m),
                                    kernel_side(v))
    for src, dst in ((g_s, out_g), (d_s, out_d), (m_s, out_m), (v_s, out_v)):
        dst.update(src)
    for dst in (out_g, out_d, out_m, out_v):
        for n in SMALL:
            dst[n] = (swap(dst[n]) if n in swapped else dst[n]).reshape(w[n].shape)

    return (loss, grad_x[None], *[out_g[n] for n in WEIGHTS], *[out_d[n] for n in WEIGHTS],
            *[out_m[n] for n in WEIGHTS], *[out_v[n] for n in WEIGHTS])
```

```python
import math

import jax
import jax.numpy as jnp
from jax import lax
from jax.experimental import pallas as pl
from jax.experimental.pallas import tpu as pltpu
from jax.experimental.pallas import tpu_sc as plsc

f32 = jnp.float32
bf16 = jnp.bfloat16
SDS = jax.ShapeDtypeStruct

T = 2048
D = 1024
NDEV = 8
NORM_EPS = 1e-5
S5_G, S5_C, S5_P = 64, 16, 64
S5_SUB = 8
S5_CH = 8
S5_STEPS = T // S5_CH
DT_MIN_LAMBDA = -1e-4
HEAD_DIM = 64
N_KV = 4
Q_PER_KV = 4
BLK = 128
D_FF_SHARD = 512
ADAM_LR, ADAM_B1, ADAM_B2, ADAM_EPS, ADAM_WD, ADAM_STEP = 0.001, 0.9, 0.999, 1e-08, 0.01, 10
VMEM_LIMIT = 56 * 1024 * 1024
MESH = pl.DeviceIdType.MESH


def _cp(**kw):
    return pltpu.CompilerParams(vmem_limit_bytes=VMEM_LIMIT, **kw)


def _dot(a, b):
    return jnp.dot(a, b, preferred_element_type=f32)


def _dot_nt(a, b):
    return lax.dot_general(a, b, (((1,), (1,)), ((), ())), preferred_element_type=f32)


def _dot_tn(a, b):
    return lax.dot_general(a, b, (((0,), (0,)), ((), ())), preferred_element_type=f32)


def _rms(x, g):
    r = lax.rsqrt(jnp.mean(x * x, axis=-1, keepdims=True) + NORM_EPS)
    return x * r * g, r


def _rms_bwd(x, g, dy):
    r = lax.rsqrt(jnp.mean(x * x, axis=-1, keepdims=True) + NORM_EPS)
    u = dy * g
    dx = r * u - (r * r * r) * x * jnp.mean(u * x, axis=-1, keepdims=True)
    return dx, dy * x * r


def _colsum8(v):
    s = jnp.sum(v, axis=0, keepdims=True)
    row = lax.broadcasted_iota(jnp.int32, (8, v.shape[1]), 0)
    return jnp.where(row == 0, jnp.broadcast_to(s, (8, v.shape[1])), 0.0)


def _full(shape):
    nd = len(shape)
    return pl.BlockSpec(shape, lambda *_: (0,) * nd, pipeline_mode=pl.Buffered(1))


_ANY = pl.BlockSpec(memory_space=pl.ANY)


def _pos():
    return lax.axis_index("x"), lax.axis_index("y"), lax.axis_index("c")


def _other_chips(x, y):
    return [(1 - x, y), (x, 1 - y), (1 - x, 1 - y)]


class BgGather:
    SIB, XN, YN, FWD_Y, FWD_X, SIB_X, SIB_Y, SIB_D = range(8)

    def __init__(self, arrs, mids=(0.5, 0.75)):
        n = len(arrs)
        self.arrs = list(arrs)
        self.out_shape = [SDS((NDEV,) + a.shape, a.dtype) for a in arrs]
        self.scratch = [pltpu.SemaphoreType.DMA((n, 8)), pltpu.SemaphoreType.DMA((n, 8)),
                        pltpu.SemaphoreType.DMA((n,))]
        self.mids = mids
        self.result = None

    @staticmethod
    def peers(x, y, c):
        return [(x, y, 1 - c), (1 - x, y, c), (x, 1 - y, c)]

    def mid_steps(self, nsteps):
        at = lambda f: min(nsteps - 1, max(0, int(f * nsteps) - 1))
        return [(at(self.mids[0]), self.mid), (max(at(self.mids[0]), at(self.mids[1])), self.mid2)]

    def _halves(self, a):
        rows = self.arrs[a].shape[0]
        cut = (rows // 32) * 16 if rows >= 32 else rows
        return (0, cut), (cut, rows - cut)

    def _copy(self, ins, outs, sems, a, k, block, to, own=False, part=None):
        slot = 4 * block[0] + 2 * block[1] + block[2]
        rows = pl.ds(0, self.arrs[a].shape[0]) if part is None else pl.ds(*self._halves(a)[part])
        dst = outs[a].at[slot, rows]
        return pltpu.make_async_remote_copy(
            src_ref=ins[a].at[rows] if own else dst, dst_ref=dst, send_sem=sems[0].at[a, k],
            recv_sem=sems[1].at[a, k], device_id=to, device_id_type=MESH)

    def _mine(self, ins, outs, sems):
        x, y, c = _pos()
        return [pltpu.make_async_copy(ins[a], outs[a].at[4 * x + 2 * y + c], sems[2].at[a])
                for a in range(len(self.arrs))]

    def _split(self, a):
        return self._halves(a)[1][1] > 0

    def _sends(self, ins, outs, sems, phase):
        x, y, c = _pos()
        me, sib, xn, yn, dg = (x, y, c), (x, y, 1 - c), (1 - x, y, c), (x, 1 - y, c), (1 - x, 1 - y, c)
        cps = []
        for a in range(len(self.arrs)):
            cp = lambda k, block, to, **kw: self._copy(ins, outs, sems, a, k, block, to, **kw)
            if phase == 0:
                cps += [cp(self.SIB, me, sib, own=True), cp(self.XN, me, xn, own=True), cp(self.YN, me, yn, own=True)]
            elif phase == 1:
                cps.append(cp(self.FWD_Y, xn, yn, part=0))
                if self._split(a):
                    cps.append(cp(self.FWD_X, yn, xn, part=1))
                cps += [cp(self.SIB_X, xn, sib), cp(self.SIB_Y, yn, sib)]
            else:
                cps.append(cp(self.SIB_D, dg, sib))
        return cps

    def _arrivals(self, ins, outs, sems, phase):
        x, y, c = _pos()
        me, xn, yn, dg = (x, y, c), (1 - x, y, c), (x, 1 - y, c), (1 - x, 1 - y, c)
        cps = []
        for a in range(len(self.arrs)):
            cp = lambda k, block, **kw: self._copy(ins, outs, sems, a, k, block, me, **kw)
            if phase == 1:
                cps += [cp(self.XN, xn), cp(self.YN, yn)]
            elif phase == 2:
                cps.append(cp(self.FWD_Y, dg, part=0))
                if self._split(a):
                    cps.append(cp(self.FWD_X, dg, part=1))
            else:
                cps += [cp(self.SIB, (x, y, 1 - c)), cp(self.SIB_X, (1 - x, y, 1 - c)),
                        cp(self.SIB_Y, (x, 1 - y, 1 - c)), cp(self.SIB_D, (1 - x, 1 - y, 1 - c))]
        return cps

    def start(self, ins, outs, sems):
        for cp in self._mine(ins, outs, sems) + self._sends(ins, outs, sems, 0):
            cp.start()

    def mid(self, ins, outs, sems):
        for cp in self._arrivals(ins, outs, sems, 1):
            cp.wait_recv()
        for cp in self._sends(ins, outs, sems, 1):
            cp.start()

    def mid2(self, ins, outs, sems):
        for cp in self._arrivals(ins, outs, sems, 2):
            cp.wait_recv()
        for cp in self._sends(ins, outs, sems, 2):
            cp.start()

    def finish(self, ins, outs, sems):
        for cp in self._arrivals(ins, outs, sems, 3):
            cp.wait_recv()
        for ph in range(3):
            for cp in self._sends(ins, outs, sems, ph):
                cp.wait_send()
        for cp in self._mine(ins, outs, sems):
            cp.wait()


def sc_comm(g, collective_id, name):
    srcs = [jax.new_ref(a, memory_space=pltpu.MemorySpace.HBM) for a in g.arrs]
    dsts = [jax.empty_ref(s, memory_space=pltpu.MemorySpace.HBM) for s in g.out_shape]

    @pl.kernel(mesh=plsc.ScalarSubcoreMesh(axis_name="sequencer", num_cores=1), name=name,
               scratch_types=tuple(g.scratch), compiler_params=pltpu.CompilerParams(collective_id=collective_id))
    def launch(*sems):
        peers = g.peers(*_pos())
        barrier = pltpu.get_barrier_semaphore()
        for peer in peers:
            pl.semaphore_signal(barrier, inc=1, device_id=peer, device_id_type=MESH)
        pl.semaphore_wait(barrier, len(peers))
        g.start(srcs, dsts, sems)
        for _, phase in g.mid_steps(1):
            phase(srcs, dsts, sems)
        g.finish(srcs, dsts, sems)

    launch()
    return [d[...] for d in dsts]


def sc_gather(arrs, collective_id, name):
    return sc_comm(BgGather(arrs), collective_id, name)


class BgPair:
    def __init__(self, arrs):
        n = len(arrs)
        self.arrs = list(arrs)
        self.out_shape = [SDS((4,) + a.shape[1:], a.dtype) for a in arrs]
        self.scratch = [pltpu.SemaphoreType.DMA((n, 4)), pltpu.SemaphoreType.DMA((n, 4))]
        self.result = None

    @staticmethod
    def peers(x, y, c):
        return [(x, y, 1 - c)]

    def mid_steps(self, nsteps):
        return []

    def _copies(self, ins, outs, sems):
        x, y, c = _pos()
        return [pltpu.make_async_remote_copy(
            src_ref=ins[a].at[2 * k + 1 - c], dst_ref=outs[a].at[k], send_sem=sems[0].at[a, k],
            recv_sem=sems[1].at[a, k], device_id=(x, y, 1 - c), device_id_type=MESH)
            for a in range(len(self.arrs)) for k in range(4)]

    def start(self, ins, outs, sems):
        for cp in self._copies(ins, outs, sems):
            cp.start()

    def finish(self, ins, outs, sems):
        cps = self._copies(ins, outs, sems)
        for cp in cps:
            cp.wait_recv()
        for cp in cps:
            cp.wait_send()


class BgChips(BgPair):
    def __init__(self, arrs):
        n = len(arrs)
        self.arrs = list(arrs)
        self.out_shape = [SDS((3,) + a.shape[1:], a.dtype) for a in arrs]
        self.scratch = [pltpu.SemaphoreType.DMA((n, 3)), pltpu.SemaphoreType.DMA((n, 3))]
        self.result = None

    @staticmethod
    def peers(x, y, c):
        return [(px, py, c) for px, py in _other_chips(x, y)]

    def _copies(self, ins, outs, sems):
        x, y, c = _pos()
        return [pltpu.make_async_remote_copy(
            src_ref=ins[a].at[2 * px + py], dst_ref=outs[a].at[r], send_sem=sems[0].at[a, r],
            recv_sem=sems[1].at[a, r], device_id=(px, py, c), device_id_type=MESH)
            for a in range(len(self.arrs)) for r, (px, py) in enumerate(_other_chips(x, y))]


def s5_discretize(a_re, a_im, log_dt, b_re, b_im, c_re, c_im):
    lam_r = jnp.minimum(a_re, DT_MIN_LAMBDA)
    lam_i = a_im
    dt = jnp.exp(log_dt)[:, None]
    e = jnp.exp(lam_r * dt)
    lbr = e * jnp.cos(lam_i * dt)
    lbi = e * jnp.sin(lam_i * dt)
    den = lam_r * lam_r + lam_i * lam_i
    cf_r = ((lbr - 1.0) * lam_r + lbi * lam_i) / den
    cf_i = (lbi * lam_r - (lbr - 1.0) * lam_i) / den
    bb_r = cf_r[:, :, None] * b_re - cf_i[:, :, None] * b_im
    bb_i = cf_r[:, :, None] * b_im + cf_i[:, :, None] * b_re
    eye = jnp.eye(8, dtype=f32)

    def blk_b(m):
        return jnp.einsum('bgpc,gh->bgchp', m.reshape(8, 8, S5_P, S5_C), eye).reshape(8, 128, 512)

    def blk_c(m):
        return jnp.einsum('bgcp,gh->bgphc', m.reshape(8, 8, S5_C, S5_P), eye).reshape(8, 512, 128)

    bm = jnp.concatenate([blk_b(bb_r), blk_b(bb_i)], axis=-1)
    cm = jnp.concatenate([blk_c(c_re), -blk_c(c_im)], axis=1)
    lam = jnp.stack([lbr.reshape(8, 512), lbi.reshape(8, 512)], axis=1)
    lam = jnp.broadcast_to(lam[:, :, None, :], (8, 2, 8, 512))
    return lam, bm, cm


def _cmul(ar, ai, br, bi):
    return ar * br - ai * bi, ar * bi + ai * br


def _shift_rows(v, k, up):
    row = lax.broadcasted_iota(jnp.int32, v.shape, 0)
    if up:
        return jnp.where(row < 8 - k, pltpu.roll(v, 8 - k, 0), 0.0)
    return jnp.where(row >= k, pltpu.roll(v, k, 0), 0.0)


_ROWS = 256


def s5_core_fwd(hn, bm, lam, cm):
    nt = T // _ROWS

    def body(u_ref, b_ref, lam_ref, c_ref, ys_ref, S):
        lr, li = lam_ref[0], lam_ref[1]
        z = jnp.zeros((8, 512), f32)
        tile = lambda k: pl.ds(k * _ROWS, _ROWS)
        c = (z, z)
        for k in range(nt):
            S[tile(k), :] = _dot(_rows_in(u_ref, k).astype(bf16), b_ref[...])
            if k >= 1:
                c = _scan_tile(S, lr, li, k - 1, c, False, False)
        c = _scan_tile(S, lr, li, nt - 1, c, False, False)
        c = _chunk_starts(c[0], c[1], lr, li, False)
        for k in range(nt):
            c = _scan_tile(S, lr, li, k, c, False, True)
            if k >= 1:
                _rows_out(ys_ref, k - 1, _dot(S[tile(k - 1), :].astype(bf16), c_ref[...]))
        _rows_out(ys_ref, nt - 1, _dot(S[tile(nt - 1), :].astype(bf16), c_ref[...]))

    return pl.pallas_call(
        body, name="s5_core_fwd", grid=(S5_SUB,),
        in_specs=[pl.BlockSpec((T, 128), lambda b: (0, b)),
                  pl.BlockSpec((None, 128, 1024), lambda b: (b, 0, 0)),
                  pl.BlockSpec((None, 4, 8, 512), lambda b: (b, 0, 0, 0)),
                  pl.BlockSpec((None, 1024, 128), lambda b: (b, 0, 0))],
        out_specs=[pl.BlockSpec((T, 128), lambda b: (0, b)), pl.BlockSpec((T, 1024), lambda b: (0, b))],
        out_shape=[SDS((T, D), f32), SDS((T, S5_SUB * 1024), f32)],
        compiler_params=_cp(dimension_semantics=("arbitrary",)),
    )(hn, bm, lam, cm)


_SEG = _ROWS // S5_CH


def _rows_in(ref, k):
    return jnp.concatenate([ref[pl.ds(s, S5_CH, stride=S5_STEPS), :] for s in range(k * _SEG, (k + 1) * _SEG)], axis=0)


def _rows_out(ref, k, val):
    for j, s in enumerate(range(k * _SEG, (k + 1) * _SEG)):
        ref[pl.ds(s, S5_CH, stride=S5_STEPS), :] = val[j * S5_CH:(j + 1) * S5_CH, :]


def _scan_tile(S, lr, li, k, carry, reverse, store, aux=None):
    steps = range(k * _SEG, (k + 1) * _SEG)
    for s in (reversed(steps) if reverse else steps):
        row = pl.ds(s * 8, 8)
        xr, xi = carry[0], carry[1]
        nr = lr * xr - li * xi + S[row, 0:512]
        ni = lr * xi + li * xr + S[row, 512:1024]
        if store:
            S[row, 0:512] = nr
            S[row, 512:1024] = ni
        if aux is not None and s >= 1:
            prow = pl.ds((s - 1) * 8, 8)
            pr, pi_ = aux[prow, 0:512], aux[prow, 512:1024]
            carry = (nr, ni, carry[2] + nr * pr + ni * pi_, carry[3] + ni * pr - nr * pi_)
        elif aux is not None:
            carry = (nr, ni, carry[2], carry[3])
        else:
            carry = (nr, ni)
    return carry


def _chunk_starts(er, ei, lr, li, reverse):
    ar, ai = lr, li
    for _ in range(8):
        ar, ai = _cmul(ar, ai, ar, ai)
    cr, ci = _shift_rows(er, 1, reverse), _shift_rows(ei, 1, reverse)
    for k in (1, 2, 4):
        sr, si = _shift_rows(cr, k, reverse), _shift_rows(ci, k, reverse)
        pr, pi_ = _cmul(ar, ai, sr, si)
        cr, ci = cr + pr, ci + pi_
        ar, ai = _cmul(ar, ai, ar, ai)
    return cr, ci


def s5_core_bwd(hn, dy, xs, bm, lam, cm):
    nt = T // _ROWS

    def body(u_ref, dy_ref, S1, b_ref, lam_ref, c_ref, du_ref, db_ref, dct_ref, dlam_ref, S2):
        lcr, lci = lam_ref[2], lam_ref[3]
        z = jnp.zeros((8, 512), f32)
        tile = lambda k: pl.ds(k * _ROWS, _ROWS)

        def dx(k):
            dyb = _rows_in(dy_ref, k).astype(bf16)
            S2[tile(k), :] = _dot_nt(dyb, c_ref[...])
            dct_ref[...] += _dot_tn(dyb, S1[tile(k), :].astype(bf16))

        dct_ref[...] = jnp.zeros_like(dct_ref)
        dx(nt - 1)
        c = (z, z)
        for k in range(nt - 1, -1, -1):
            if k >= 1:
                dx(k - 1)
            c = _scan_tile(S2, lcr, lci, k, c, True, False)

        def dbu(k):
            gb = S2[tile(k), :].astype(bf16)
            db_ref[...] += _dot_tn(_rows_in(u_ref, k).astype(bf16), gb)
            _rows_out(du_ref, k, _dot_nt(gb, b_ref[...]))

        c = _chunk_starts(c[0], c[1], lcr, lci, True) + (z, z)
        db_ref[...] = jnp.zeros_like(db_ref)
        for k in range(nt - 1, -1, -1):
            c = _scan_tile(S2, lcr, lci, k, c, True, True, aux=S1)
            if k + 1 < nt:
                dbu(k + 1)
        dbu(0)
        gr, gi, dr, di = c
        last = pl.ds((S5_STEPS - 1) * 8, 8)
        xr = _shift_rows(S1[last, 0:512], 1, False)
        xi = _shift_rows(S1[last, 512:1024], 1, False)
        dlam_ref[0] = dr + gr * xr + gi * xi
        dlam_ref[1] = di + gi * xr - gr * xi

    return pl.pallas_call(
        body, name="s5_core_bwd", grid=(S5_SUB,),
        in_specs=[pl.BlockSpec((T, 128), lambda b: (0, b)),
                  pl.BlockSpec((T, 128), lambda b: (0, b)),
                  pl.BlockSpec((T, 1024), lambda b: (0, b)),
                  pl.BlockSpec((None, 128, 1024), lambda b: (b, 0, 0)),
                  pl.BlockSpec((None, 4, 8, 512), lambda b: (b, 0, 0, 0)),
                  pl.BlockSpec((None, 1024, 128), lambda b: (b, 0, 0))],
        out_specs=[pl.BlockSpec((T, 128), lambda b: (0, b)),
                   pl.BlockSpec((None, 128, 1024), lambda b: (b, 0, 0)),
                   pl.BlockSpec((None, 128, 1024), lambda b: (b, 0, 0)),
                   pl.BlockSpec((None, 2, 8, 512), lambda b: (b, 0, 0, 0))],
        out_shape=[SDS((T, D), f32), SDS((8, 128, 1024), f32), SDS((8, 128, 1024), f32), SDS((8, 2, 8, 512), f32)],
        scratch_shapes=[pltpu.VMEM((T, 1024), f32)],
        compiler_params=_cp(dimension_semantics=("arbitrary",)),
    )(hn, dy, xs, bm, lam, cm)


TM = 512
NT = T // TM


def _tile(n=D):
    return pl.BlockSpec((TM, n), lambda i: (i, 0))


TS = 256


def _stream(bufs=None):
    mode = {} if bufs is None else dict(pipeline_mode=pl.Buffered(bufs))
    return pl.BlockSpec((TS, D), lambda i: (i, 0), **mode)


def s5_pre(xp, g):
    def body(x_hbm, g_ref, hn_hbm):
        def step(x_ref, hn_ref):
            hn_ref[...] = _rms(x_ref[...], g_ref[...])[0]

        pltpu.emit_pipeline(step, grid=(T // TS,), in_specs=[_stream(3)], out_specs=[_stream()])(x_hbm, hn_hbm)

    return pl.pallas_call(
        body, name="s5_pre", in_specs=[_ANY, pl.BlockSpec(memory_space=pltpu.VMEM)], out_specs=_ANY,
        out_shape=SDS((T, D), f32), compiler_params=_cp(),
    )(xp, g)


def _gelu_grad(y):
    c = math.sqrt(2.0 / math.pi)
    t = jnp.tanh(c * (y + 0.044715 * y * y * y))
    return 0.5 * (1.0 + t) + 0.5 * y * (1.0 - t * t) * c * (1.0 + 3.0 * 0.044715 * y * y)


def s5_post(ys, xp, g, d, wglu, bglu):
    def body(ys_ref, x_ref, g_ref, d_ref, w_ref, b_ref, y_ref, z_ref, h_ref):
        x = x_ref[...]
        hn, _ = _rms(x, g_ref[...])
        y = ys_ref[...] + d_ref[...] * hn
        y_ref[...] = y
        yg = jax.nn.gelu(y).astype(bf16)
        for j in range(4):
            cv = slice(j * 256, (j + 1) * 256)
            cg = slice(1024 + j * 256, 1024 + (j + 1) * 256)
            val = _dot(yg, w_ref[j]) + b_ref[:, cv]
            gate = _dot(yg, w_ref[j + 4]) + b_ref[:, cg]
            z_ref[:, cv] = val
            z_ref[:, cg] = gate
            h_ref[:, cv] = x[:, cv] + val * jax.nn.sigmoid(gate)

    return pl.pallas_call(
        body, name="s5_post", grid=(NT,),
        in_specs=[_tile(), _tile(), _full((1, D)), _full((1, D)), _full((8, D, 256)), _full((1, 2 * D))],
        out_specs=[_tile(), _tile(2 * D), _tile()],
        out_shape=[SDS((T, D), f32), SDS((T, 2 * D), f32), SDS((T, D), f32)],
        compiler_params=_cp(dimension_semantics=("arbitrary",)),
    )(ys, xp, g, d, wglu, bglu)


def s5_post_bwd(dh, y, z, wglu):
    def body(dh_ref, y_ref, z_ref, w_ref, dy_ref, dw_ref, db_ref, ygs, dzs):
        i = pl.program_id(0)
        rows = pl.ds(pl.multiple_of(i * TM, TM), TM)

        @pl.when(i == 0)
        def _():
            db_ref[...] = jnp.zeros_like(db_ref)

        dh_ = dh_ref[...]
        y = y_ref[...]
        ygs[rows, :] = jax.nn.gelu(y).astype(bf16)
        dyg = jnp.zeros((TM, D), f32)
        for j in range(4):
            cv = slice(j * 256, (j + 1) * 256)
            cg = slice(1024 + j * 256, 1024 + (j + 1) * 256)
            val = z_ref[:, cv]
            sg = jax.nn.sigmoid(z_ref[:, cg])
            dval = dh_[:, cv] * sg
            dgate = dh_[:, cv] * val * sg * (1.0 - sg)
            db_ref[:, cv] += _colsum8(dval)
            db_ref[:, cg] += _colsum8(dgate)
            dvb = dval.astype(bf16)
            dgb = dgate.astype(bf16)
            dzs[rows, cv] = dvb
            dzs[rows, cg] = dgb
            dyg = dyg + _dot_nt(dvb, w_ref[j]) + _dot_nt(dgb, w_ref[j + 4])
        dy_ref[...] = dyg * _gelu_grad(y)

        @pl.when(i == NT - 1)
        def _():
            for half in range(2):
                dw = _dot_tn(ygs[...], dzs[:, half * D:(half + 1) * D])
                for j in range(4):
                    dw_ref[4 * half + j] = dw[:, j * 256:(j + 1) * 256].astype(bf16)

    return pl.pallas_call(
        body, name="s5_post_bwd", grid=(NT,),
        in_specs=[_tile(), _tile(), _tile(2 * D), _full((8, D, 256))],
        out_specs=[_tile(), _full((8, D, 256)), _full((8, 2 * D))],
        out_shape=[SDS((T, D), f32), SDS((8, D, 256), bf16), SDS((8, 2 * D), f32)],
        scratch_shapes=[pltpu.VMEM((T, D), bf16), pltpu.VMEM((T, 2 * D), bf16)],
        compiler_params=_cp(dimension_semantics=("arbitrary",)),
    )(dh, y, z, wglu)


def s5_pre_bwd(xp, g, du, dy, d, dh):
    def body(x_ref, g_ref, du_ref, dy_ref, d_ref, dh_ref, dx_ref, dg_ref, dd_ref):
        i = pl.program_id(0)

        @pl.when(i == 0)
        def _():
            dg_ref[...] = jnp.zeros_like(dg_ref)
            dd_ref[...] = jnp.zeros_like(dd_ref)

        x = x_ref[...]
        g = g_ref[...]
        dy = dy_ref[...]
        hn, _ = _rms(x, g)
        dhn = du_ref[...] + d_ref[...] * dy
        dx, dgt = _rms_bwd(x, g, dhn)
        dx_ref[...] = dh_ref[...] + dx
        dg_ref[...] += _colsum8(dgt)
        dd_ref[...] += _colsum8(dy * hn)

    return pl.pallas_call(
        body, name="s5_pre_bwd", grid=(T // TS,),
        in_specs=[_stream(), _full((1, D)), _stream(), _stream(), _full((1, D)), _stream()],
        out_specs=[_stream(), _full((8, D)), _full((8, D))],
        out_shape=[SDS((T, D), f32), SDS((8, D), f32), SDS((8, D), f32)],
        compiler_params=_cp(dimension_semantics=("arbitrary",)),
    )(xp, g, du, dy, d, dh)


TMF = 1024


def mlp_fwd(h, g, w_in, w_out, layer):
    def body(h_ref, g_ref, wi_ref, wo_ref, hm_ref, r_ref, out_ref, acc):
        j = pl.program_id(1)

        @pl.when(j == 0)
        def _():
            hm, _ = _rms(h_ref[...], g_ref[...])
            hm_ref[...] = hm.astype(bf16)
            acc[...] = jnp.zeros_like(acc)

        a = jnp.maximum(_dot(hm_ref[...], wi_ref[...]), 0.0)
        r_ref[...] = a.astype(bf16)
        acc[...] += _dot((a * a).astype(bf16), wo_ref[...])

        @pl.when(j == NDEV - 1)
        def _():
            out_ref[...] = h_ref[...] + acc[...]

    return pl.pallas_call(
        body, name=f"mlp_fwd{layer}", grid=(T // TMF, NDEV),
        in_specs=[pl.BlockSpec((TMF, D), lambda i, j: (i, 0)),
                  pl.BlockSpec((1, D), lambda i, j: (0, 0)),
                  pl.BlockSpec((None, D, D_FF_SHARD), lambda i, j: (j, 0, 0)),
                  pl.BlockSpec((None, D_FF_SHARD, D), lambda i, j: (j, 0, 0))],
        out_specs=[pl.BlockSpec((TMF, D), lambda i, j: (i, 0)), pl.BlockSpec((TMF, D_FF_SHARD), lambda i, j: (i, j)),
                   pl.BlockSpec((TMF, D), lambda i, j: (i, 0))],
        out_shape=[SDS((T, D), bf16), SDS((T, NDEV * D_FF_SHARD), bf16), SDS((T, D), f32)],
        scratch_shapes=[pltpu.VMEM((TMF, D), f32)],
        compiler_params=_cp(dimension_semantics=("arbitrary", "arbitrary")),
    )(h, g, w_in, w_out)


def mlp_bwd(h, hm, r, g, dout, dout_b, w_in, w_out, layer):
    def body(h_ref, hm_ref, r_ref, g_ref, do_ref, dob_ref, wi_ref, wo_ref, dh_ref, dwi_ref, dwo_ref, dg_ref,
             dhm, dzs):
        s = pl.program_id(0)

        @pl.when(s == 0)
        def _():
            dhm[...] = jnp.zeros_like(dhm)

        @pl.when(s < NDEV)
        def _():
            for c in range(NT):
                rows = pl.ds(c * TM, TM)
                dz = (_dot_nt(dob_ref[rows, :], wo_ref[...]) * (2.0 * r_ref[rows, :].astype(f32))).astype(bf16)
                dzs[rows, :] = dz
                dhm[rows, :] += _dot_nt(dz, wi_ref[...])
            rb = r_ref[...]
            dwo_ref[...] = _dot_tn(rb * rb, dob_ref[...]).astype(bf16)
            dwi_ref[...] = _dot_tn(hm_ref[...], dzs[...]).astype(bf16)

        @pl.when(s >= NDEV)
        def _():
            @pl.when(s == NDEV)
            def _():
                dg_ref[...] = jnp.zeros_like(dg_ref)
            rows = pl.ds(pl.multiple_of((s - NDEV) * TM, TM), TM)
            dx, dgt = _rms_bwd(h_ref[...], g_ref[...], dhm[rows, :])
            dh_ref[...] = do_ref[...] + dx
            dg_ref[...] += _colsum8(dgt)

    shard = lambda s: (jnp.minimum(s, NDEV - 1), 0, 0)
    tile = lambda s: (jnp.maximum(s - NDEV, 0), 0)
    return pl.pallas_call(
        body, name=f"mlp_bwd{layer}", grid=(NDEV + NT,),
        in_specs=[pl.BlockSpec((TM, D), tile),
                  _full((T, D)),
                  pl.BlockSpec((T, D_FF_SHARD), lambda s: (0, jnp.minimum(s, NDEV - 1))),
                  _full((1, D)),
                  pl.BlockSpec((TM, D), tile),
                  _full((T, D)),
                  pl.BlockSpec((None, D, D_FF_SHARD), shard),
                  pl.BlockSpec((None, D_FF_SHARD, D), shard)],
        out_specs=[pl.BlockSpec((TM, D), tile),
                   pl.BlockSpec((None, D, D_FF_SHARD), shard),
                   pl.BlockSpec((None, D_FF_SHARD, D), shard),
                   pl.BlockSpec((8, D), lambda s: (0, 0))],
        out_shape=[SDS((T, D), f32), SDS((NDEV, D, D_FF_SHARD), bf16), SDS((NDEV, D_FF_SHARD, D), bf16),
                   SDS((8, D), f32)],
        scratch_shapes=[pltpu.VMEM((T, D), f32), pltpu.VMEM((T, D_FF_SHARD), bf16)],
        compiler_params=_cp(dimension_semantics=("arbitrary",)),
    )(h, hm, r, g, dout, dout_b, w_in, w_out)


def _spread4():
    r = lax.broadcasted_iota(jnp.int32, (256, D), 0)
    c = lax.broadcasted_iota(jnp.int32, (256, D), 1)
    return ((c // 256 == r // HEAD_DIM) & (c % HEAD_DIM == r % HEAD_DIM)).astype(bf16)


def attn_pre(h, g_kv, g_mix, wkv, bkv, spread, wq, bq):
    def body(h_ref, gkv_ref, gm_ref, wkv_ref, bkv_ref, sp_ref, wq_ref, bq_ref, kvn_ref, hn_ref, k_ref, v_ref, q_ref):
        h_ = h_ref[...]
        kvn = _rms(h_, gkv_ref[...])[0].astype(bf16)
        hn = _rms(h_, gm_ref[...])[0].astype(bf16)
        kvn_ref[...] = kvn
        hn_ref[...] = hn
        kv = (_dot(kvn, wkv_ref[...]) + bkv_ref[...]).astype(bf16)
        k_ref[...] = _dot(kv[:, :256], sp_ref[...]).astype(bf16)
        v_ref[...] = _dot(kv[:, 256:], sp_ref[...]).astype(bf16)
        q_ref[...] = (_dot(hn, wq_ref[...]) + bq_ref[...]).astype(bf16)

    return pl.pallas_call(
        body, name="attn_pre", grid=(NT,),
        in_specs=[_tile(), _full((1, D)), _full((1, D)), _full((D, 512)), _full((1, 512)), _full((256, D)),
                  _full((D, D)), _full((1, D))],
        out_specs=[_tile()] * 5,
        out_shape=[SDS((T, D), bf16)] * 5,
        compiler_params=_cp(dimension_semantics=("arbitrary",)),
    )(h, g_kv, g_mix, wkv, bkv, spread, wq, bq)


def _attn_specs():
    cur = pl.BlockSpec((TM, 256), lambda j, n: (n, j))
    prev = pl.BlockSpec((BLK, 256), lambda j, n: (jnp.maximum(n * (TM // BLK) - 1, 0), j))
    return cur, prev


def _head_mask(g):
    lane = lax.broadcasted_iota(jnp.int32, (1, 256), 1)
    return (lane >= g * HEAD_DIM) & (lane < (g + 1) * HEAD_DIM)


def _stack_heads(t):
    return jnp.concatenate([jnp.where(_head_mask(g), t, 0) for g in range(Q_PER_KV)], axis=0)


def _unstack_heads(t):
    out = jnp.where(_head_mask(0), t[0:BLK], 0.0)
    for g in range(1, Q_PER_KV):
        out = out + jnp.where(_head_mask(g), t[g * BLK:(g + 1) * BLK], 0.0)
    return out


def _attn_probs(qs, k2, sinks, first):
    rows = Q_PER_KV * BLK
    s = _dot_nt(qs, k2) * (1.0 / math.sqrt(HEAD_DIM))
    qi = jnp.bitwise_and(lax.broadcasted_iota(jnp.int32, (rows, 2 * BLK), 0), BLK - 1)
    kj = lax.broadcasted_iota(jnp.int32, (rows, 2 * BLK), 1)
    diff = qi + BLK - kj
    valid = (diff >= 0) & (diff < BLK) & (jnp.logical_not(first) | (kj >= BLK))
    s = jnp.where(valid, s, -jnp.inf)
    rb = lax.broadcasted_iota(jnp.int32, (rows, 1), 0)
    sink = jnp.where(rb < BLK, sinks[0], jnp.where(rb < 2 * BLK, sinks[1], jnp.where(rb < 3 * BLK, sinks[2], sinks[3])))
    m = jnp.maximum(jnp.max(s, axis=-1, keepdims=True), sink)
    p = jnp.exp(s - m)
    ps = jnp.exp(sink - m)
    denom = jnp.sum(p, axis=-1, keepdims=True) + ps
    return p / denom, ps / denom


def _window_blocks(b, n, kc_ref, kp_ref, vc_ref, vp_ref):
    if b == 0:
        return (jnp.concatenate([kp_ref[...], kc_ref[0:BLK, :]], axis=0),
                jnp.concatenate([vp_ref[...], vc_ref[0:BLK, :]], axis=0), n == 0)
    rows = pl.ds((b - 1) * BLK, 2 * BLK)
    return kc_ref[rows, :], vc_ref[rows, :], False


def attn_core_fwd(q, k4, v4, sinks):
    nb = TM // BLK

    def body(sink_ref, q_ref, kc_ref, kp_ref, vc_ref, vp_ref, o_ref, a_ref, as_ref):
        j = pl.program_id(0)
        n = pl.program_id(1)
        sk = [sink_ref[j * Q_PER_KV + g] for g in range(Q_PER_KV)]
        for b in range(nb):
            qb = q_ref[b * BLK:(b + 1) * BLK, :]
            k2, v2, first = _window_blocks(b, n, kc_ref, kp_ref, vc_ref, vp_ref)
            a, asink = _attn_probs(_stack_heads(qb), k2, sk, first)
            ab = a.astype(bf16)
            a_ref[b] = ab
            as_ref[b] = jnp.broadcast_to(asink, (Q_PER_KV * BLK, 128)).astype(bf16)
            o_ref[b * BLK:(b + 1) * BLK, :] = _unstack_heads(_dot(ab, v2)).astype(bf16)

    cur, prev = _attn_specs()
    rows = Q_PER_KV * BLK
    return pl.pallas_call(
        body, name="attn_core_fwd", grid=(N_KV, NT),
        in_specs=[pl.BlockSpec(memory_space=pltpu.SMEM), cur, cur, prev, cur, prev],
        out_specs=[cur, pl.BlockSpec((None, nb, rows, 2 * BLK), lambda j, n: (j, n, 0, 0)),
                   pl.BlockSpec((None, nb, rows, 128), lambda j, n: (j, n, 0, 0))],
        out_shape=[SDS((T, D), bf16), SDS((N_KV, T // BLK, rows, 2 * BLK), bf16), SDS((N_KV, T // BLK, rows, 128), bf16)],
        compiler_params=_cp(dimension_semantics=("arbitrary", "arbitrary")),
    )(sinks, q, k4, k4, v4, v4)


def attn_post(h, o, wo, bo):
    def body(h_ref, o_ref, w_ref, b_ref, out_ref):
        out_ref[...] = h_ref[...] + _dot(o_ref[...], w_ref[...]) + b_ref[...]

    return pl.pallas_call(
        body, name="attn_post", grid=(NT,), in_specs=[_tile(), _tile(), _full((D, D)), _full((1, D))],
        out_specs=_tile(), out_shape=SDS((T, D), f32), compiler_params=_cp(dimension_semantics=("arbitrary",)),
    )(h, o, wo, bo)


def attn_bwd_pre(dh, o, wo):
    def body(dh_ref, o_ref, w_ref, do_ref, dw_ref, db_ref, acc):
        i = pl.program_id(0)

        @pl.when(i == 0)
        def _():
            acc[...] = jnp.zeros_like(acc)
            db_ref[...] = jnp.zeros_like(db_ref)

        dh_ = dh_ref[...]
        dhb = dh_.astype(bf16)
        do_ref[...] = _dot_nt(dhb, w_ref[...]).astype(bf16)
        acc[...] += _dot_tn(o_ref[...], dhb)
        db_ref[...] += _colsum8(dh_)

        @pl.when(i == NT - 1)
        def _():
            dw_ref[...] = acc[...].astype(bf16)

    return pl.pallas_call(
        body, name="attn_bwd_pre", grid=(NT,), in_specs=[_tile(), _tile(), _full((D, D))],
        out_specs=[_tile(), _full((D, D)), _full((8, D))],
        out_shape=[SDS((T, D), bf16), SDS((D, D), bf16), SDS((8, D), f32)],
        scratch_shapes=[pltpu.VMEM((D, D), f32)],
        compiler_params=_cp(dimension_semantics=("arbitrary",)),
    )(dh, o, wo)


def attn_core_bwd(q, do, k4, v4, probs, sink_w):
    nb = TM // BLK

    def body(q_ref, do_ref, kc_ref, kp_ref, vc_ref, vp_ref, a_ref, as_ref, dq_ref, dk_ref, dv_ref, ds_ref):
        j = pl.program_id(0)
        n = pl.program_id(1)

        @pl.when(n == 0)
        def _():
            dk_ref[...] = jnp.zeros_like(dk_ref)
            dv_ref[...] = jnp.zeros_like(dv_ref)
            ds_ref[...] = jnp.zeros_like(ds_ref)

        lane8 = lax.broadcasted_iota(jnp.int32, (8, 128), 1)
        row8 = lax.broadcasted_iota(jnp.int32, (8, 128), 0)
        for b in range(nb):
            qs = _stack_heads(q_ref[b * BLK:(b + 1) * BLK, :])
            dos = _stack_heads(do_ref[b * BLK:(b + 1) * BLK, :])
            k2, v2, _ = _window_blocks(b, n, kc_ref, kp_ref, vc_ref, vp_ref)
            ab = a_ref[b]
            a = ab.astype(f32)
            asink = as_ref[b][:, 0:1].astype(f32)
            dp = _dot_nt(dos, v2)
            dd = jnp.sum(a * dp, axis=-1, keepdims=True)
            dsc = (a * (dp - dd) * (1.0 / math.sqrt(HEAD_DIM))).astype(bf16)
            t = asink * dd
            for g in range(Q_PER_KV):
                dsink = -jnp.sum(t[g * BLK:(g + 1) * BLK], axis=0, keepdims=True)
                ds_ref[...] += jnp.where((lane8 == g) & (row8 == 0), jnp.broadcast_to(dsink, (8, 128)), 0.0)
            dq_ref[b * BLK:(b + 1) * BLK, :] = _unstack_heads(_dot(dsc, k2))
            dk2 = _dot_tn(dsc, qs)
            dv2 = _dot_tn(ab, dos)
            cur = pl.ds(pl.multiple_of(n * TM + b * BLK, BLK), BLK)
            dk_ref[cur, :] += dk2[BLK:, :]
            dv_ref[cur, :] += dv2[BLK:, :]
            prv = pl.ds(pl.multiple_of(jnp.maximum(n * TM + (b - 1) * BLK, 0), BLK), BLK)
            dk_ref[prv, :] += dk2[:BLK, :]
            dv_ref[prv, :] += dv2[:BLK, :]

    cur, prev = _attn_specs()
    col = pl.BlockSpec((T, 256), lambda j, n: (0, j))
    rows = Q_PER_KV * BLK
    return pl.pallas_call(
        body, name="attn_core_bwd", grid=(N_KV, NT),
        in_specs=[cur, cur, cur, prev, cur, prev,
                  pl.BlockSpec((None, nb, rows, 2 * BLK), lambda j, n: (j, n, 0, 0)),
                  pl.BlockSpec((None, nb, rows, 128), lambda j, n: (j, n, 0, 0))],
        out_specs=[cur, col, col, pl.BlockSpec((None, 8, 128), lambda j, n: (j, 0, 0))],
        out_shape=[SDS((T, D), f32), SDS((T, D), f32), SDS((T, D), f32), SDS((N_KV, 8, 128), f32)],
        compiler_params=_cp(dimension_semantics=("arbitrary", "arbitrary")),
    )(q, do, k4, k4, v4, v4, probs, sink_w)


def attn_bwd_q(h, dh, dq, hn, g_mix, wq):
    def body(h_ref, dh_ref, dq_ref, hn_ref, gm_ref, wq_ref, out_ref, dwq_ref, dbq_ref, dgm_ref, aq):
        i = pl.program_id(0)

        @pl.when(i == 0)
        def _():
            aq[...] = jnp.zeros_like(aq)
            dbq_ref[...] = jnp.zeros_like(dbq_ref)
            dgm_ref[...] = jnp.zeros_like(dgm_ref)

        dq_ = dq_ref[...]
        dqb = dq_.astype(bf16)
        aq[...] += _dot_tn(hn_ref[...], dqb)
        dbq_ref[...] += _colsum8(dq_)
        dx, dg = _rms_bwd(h_ref[...], gm_ref[...], _dot_nt(dqb, wq_ref[...]))
        out_ref[...] = dh_ref[...] + dx
        dgm_ref[...] += _colsum8(dg)

        @pl.when(i == NT - 1)
        def _():
            dwq_ref[...] = aq[...].astype(bf16)

    vec = _full((8, D))
    mat = _full((D, D))
    return pl.pallas_call(
        body, name="attn_bwd_q", grid=(NT,),
        in_specs=[_tile()] * 4 + [_full((1, D)), mat],
        out_specs=[_tile(), mat, vec, vec],
        out_shape=[SDS((T, D), f32), SDS((D, D), bf16), SDS((8, D), f32), SDS((8, D), f32)],
        scratch_shapes=[pltpu.VMEM((D, D), f32)],
        compiler_params=_cp(dimension_semantics=("arbitrary",)),
    )(h, dh, dq, hn, g_mix, wq)


def attn_bwd_kv(h, dh, dk4, dv4, kvn, g_kv, wkv, spread):
    def body(h_ref, dh_ref, dk_ref, dv_ref, kvn_ref, gkv_ref, wkv_ref, sp_ref, out_ref, outb_ref, dw_ref, db_ref,
             dgkv_ref, acc):
        i = pl.program_id(0)

        @pl.when(i == 0)
        def _():
            for r in (acc, db_ref, dgkv_ref):
                r[...] = jnp.zeros_like(r)

        dkv = jnp.concatenate([_dot_nt(dk_ref[...].astype(bf16), sp_ref[...]),
                               _dot_nt(dv_ref[...].astype(bf16), sp_ref[...])], axis=1)
        dkvb = dkv.astype(bf16)
        acc[...] += _dot_tn(kvn_ref[...], dkvb)
        db_ref[...] += _colsum8(dkv)
        dx, dg = _rms_bwd(h_ref[...], gkv_ref[...], _dot_nt(dkvb, wkv_ref[...]))
        out = dh_ref[...] + dx
        out_ref[...] = out
        outb_ref[...] = out.astype(bf16)
        dgkv_ref[...] += _colsum8(dg)

        @pl.when(i == NT - 1)
        def _():
            dw_ref[...] = acc[...].astype(bf16)

    return pl.pallas_call(
        body, name="attn_bwd_kv", grid=(NT,),
        in_specs=[_tile()] * 5 + [_full((1, D)), _full((D, 512)), _full((256, D))],
        out_specs=[_tile(), _tile(), _full((D, 512)), _full((8, 512)), _full((8, D))],
        out_shape=[SDS((T, D), f32), SDS((T, D), bf16), SDS((D, 512), bf16), SDS((8, 512), f32), SDS((8, D), f32)],
        scratch_shapes=[pltpu.VMEM((D, 512), f32)],
        compiler_params=_cp(dimension_semantics=("arbitrary",)),
    )(h, dh, dk4, dv4, kvn, g_kv, wkv, spread)


def final_loss(h, g, target):
    def body(h_ref, g_ref, t_ref, loss_ref, dh_ref, dhb_ref, dg_ref):
        i = pl.program_id(0)

        @pl.when(i == 0)
        def _():
            loss_ref[...] = jnp.zeros_like(loss_ref)
            dg_ref[...] = jnp.zeros_like(dg_ref)

        h_ = h_ref[...]
        g_ = g_ref[...]
        y, _ = _rms(h_, g_)
        diff = y - t_ref[...]
        per_tok = jnp.mean(diff * diff, axis=-1, keepdims=True)
        tot = 0.5 * jnp.sum(per_tok, axis=0, keepdims=True)
        lane = lax.broadcasted_iota(jnp.int32, (8, 128), 1)
        row = lax.broadcasted_iota(jnp.int32, (8, 128), 0)
        loss_ref[...] += jnp.where((lane == 0) & (row == 0), jnp.broadcast_to(tot, (8, 128)), 0.0)
        dx, dgt = _rms_bwd(h_, g_, diff * (1.0 / D))
        dh_ref[...] = dx
        dhb_ref[...] = dx.astype(bf16)
        dg_ref[...] += _colsum8(dgt)

    return pl.pallas_call(
        body, name="final_loss", grid=(T // TS,), in_specs=[_stream(), _full((1, D)), _stream()],
        out_specs=[_full((8, 128)), _stream(), _stream(), _full((8, D))],
        out_shape=[SDS((8, 128), f32), SDS((T, D), f32), SDS((T, D), bf16), SDS((8, D), f32)],
        compiler_params=_cp(dimension_semantics=("arbitrary",)),
    )(h, g, target)


def fwd_bwd(x, target, p, shards, opt, core, chip):
    row = lambda v: v.reshape(1, -1)
    (lam, bm, cm), prep_vjp = jax.vjp(s5_discretize, p["s5_a_re"][0], p["s5_a_im"][0], p["s5_log_dt"][0],
                                      p["s5_b_re"][0], p["s5_b_im"][0], p["s5_c_re"][0], p["s5_c_im"][0])
    bmb, cmb = bm.astype(bf16), cm.astype(bf16)
    lam = jnp.concatenate([lam, lam * jnp.array([1.0, -1.0], f32).reshape(1, 2, 1, 1)], axis=1)
    g_mix0, g_mix1 = row(p["norm_mix"][0]), row(p["norm_mix"][1])
    g_mlp0, g_mlp1 = row(p["norm_mlp"][0]), row(p["norm_mlp"][1])
    g_kv, g_fin = row(p["norm_kv"]), row(p["norm_final"])
    bq, bo = p["b_q"], p["b_o"]
    bkv = row(p["b_kv"])
    spread = _spread4()
    sinks = p["sinks"].reshape(16)

    wglu, gvec = sc_gather([shards["s5_w_glu"], shards["vecs"]], 3, "sc_gather_s5")
    win0, wout0 = sc_gather([shards["w_in0"], shards["w_out0"]], 14, "sc_gather_mlp0")
    wkv, wq, wo = sc_gather([shards["w_kv"], shards["w_q"], shards["w_o"]], 4, "sc_gather_attn")
    win1, wout1 = sc_gather([shards["w_in1"], shards["w_out1"]], 5, "sc_gather_mlp1")
    xp = x
    hn0 = s5_pre(xp, g_mix0)
    ys, xs = s5_core_fwd(hn0, bmb, lam, cmb)
    d_skip = gvec[:, 0, :128].reshape(1, D)
    bglu = gvec[:, 0, 128:].reshape(1, 2 * D)
    y, z, h1 = s5_post(ys, xp, g_mix0, d_skip, wglu, bglu)
    hm0, r0, h2p = mlp_fwd(h1, g_mlp0, win0, wout0, 0)
    wkv, wq, wo = wkv.reshape(D, 512), wq.reshape(D, D), wo.reshape(D, D)
    h2 = h2p
    kvn, hn1, k4, v4, q = attn_pre(h2, g_kv, g_mix1, wkv, bkv, spread, wq, bq)
    o, probs, sink_w = attn_core_fwd(q, k4, v4, sinks)
    h3 = attn_post(h2, o, wo, bo)
    hm1, r1, h4 = mlp_fwd(h3, g_mlp1, win1, wout1, 1)
    loss, dh4, dh4b, dg_fin = final_loss(h4, g_fin, target)

    def pair_sums(names, grads, cid, before):
        r1 = sc_comm(BgPair(grads), cid, "sc_pair_" + names[0])
        parts = add_pairs(grads, r1, core, "add_pairs_" + names[0])
        before, parts = lax.optimization_barrier((before, parts))
        return before, parts

    def across_chips(names, parts, cid):
        return list(zip(parts, sc_comm(BgChips(parts), cid, "sc_chips_" + names[0])))

    dh3, dwin1, dwout1, dg_mlp1 = mlp_bwd(h3, hm1, r1, g_mlp1, dh4, dh4b, win1, wout1, 1)
    do, dwo, dbo = attn_bwd_pre(dh3, o, wo)
    do, parts = pair_sums(["w_in1", "w_out1"], [dwin1, dwout1], 6, do)
    rs_in1, rs_out1 = across_chips(["w_in1", "w_out1"], parts, 7)
    dq, dk4, dv4, dsink = attn_core_bwd(q, do, k4, v4, probs, sink_w)
    dh2, dwq, dbq, dg_mix1 = attn_bwd_q(h2, dh3, dq, hn1, g_mix1, wq)
    dh2, dh2b, dwkv, dbkv, dg_kv = attn_bwd_kv(h2, dh2, dk4, dv4, kvn, g_kv, wkv, spread)
    dh2p, dh2pb = dh2, dh2b
    big = {}
    def adam_group(group, rss, name, **kw):
        return adam_big(*zip(*[opt[n] for n in group]), *zip(*rss), chip, name, **kw)

    mlp = ["w_mlp_in", "w_mlp_out"]
    a_in1, a_out1 = adam_group(mlp, [rs_in1, rs_out1], "adam_w_mlp1", layer=1)
    dh2p, a_in1, a_out1 = lax.optimization_barrier((dh2p, a_in1, a_out1))
    names = ["w_kv", "w_q", "w_o"]
    dh2p, parts = pair_sums(names, [dwkv.reshape(NDEV, 128, 512), dwq.reshape(NDEV, 128, D),
                                    dwo.reshape(NDEV, 128, D)], 8, dh2p)
    rs_attn = across_chips(names, parts, 9)
    dh1, dwin0, dwout0, dg_mlp0 = mlp_bwd(h1, hm0, r0, g_mlp0, dh2p, dh2pb, win0, wout0, 0)
    a_attn = adam_group(names, rs_attn, "adam_attn")
    dh1, a_attn = lax.optimization_barrier((dh1, a_attn))
    big.update(zip(names, a_attn))
    dy, dwglu, dbglu = s5_post_bwd(dh1, y, z, wglu)
    dy, parts = pair_sums(["w_in0", "w_out0"], [dwin0, dwout0], 10, dy)
    rs_in0, rs_out0 = across_chips(["w_in0", "w_out0"], parts, 11)
    du, dbm, dcmt, dlam = s5_core_bwd(hn0, dy, xs, bmb, lam, cmb)
    du, parts = pair_sums(["s5_w_glu"], [dwglu], 12, du)
    rs_glu, = across_chips(["s5_w_glu"], parts, 13)
    dxp, dg_mix0, dd = s5_pre_bwd(xp, g_mix0, du, dy, d_skip, dh1)
    big["w_mlp_in"], big["w_mlp_out"] = adam_group(mlp, [rs_in0, rs_out0], "adam_w_mlp0", layer=0,
                                                   prev=[a_in1, a_out1])
    big["s5_w_glu"], = adam_group(["s5_w_glu"], [rs_glu], "adam_s5_w_glu")
    grad_x = dxp
    da_re, da_im, dlog_dt, db_re, db_im, dc_re, dc_im = prep_vjp((dlam, dbm, dcmt.transpose(0, 2, 1)))

    def lanes(v_):
        v_ = v_.reshape(1, -1)
        return jnp.pad(v_, ((0, 0), (0, D - v_.shape[1])))

    small = dict(
        rows8=[dg_mix0, dg_mix1, dg_mlp0, dg_mlp1, dg_kv, dg_fin, dd, dbq, dbo], b_glu=dbglu, b_kv=dbkv,
        misc=jnp.concatenate([lanes(dsink[:, 0, :Q_PER_KV]), lanes(dlog_dt), lanes(loss[0:1, 0:1])], axis=0),
        s5=[da_re.reshape(4, D), da_im.reshape(4, D),
            db_re.transpose(0, 2, 1).reshape(64, D), db_im.transpose(0, 2, 1).reshape(64, D),
            dc_re.reshape(64, D), dc_im.reshape(64, D)])
    small, big["w_mlp_in"], big["w_mlp_out"] = lax.optimization_barrier((small, big["w_mlp_in"], big["w_mlp_out"]))
    return loss, grad_x, small, big


def add_pairs(gs, r1s, core, name):
    n = len(gs)

    def body(core_ref, *refs):
        for g_ref, r_ref, o_ref in zip(refs[:n], refs[n:2 * n], refs[2 * n:]):
            o_ref[...] = (g_ref[...].astype(f32) + r_ref[...].astype(f32)).astype(bf16)

    mine = [pl.BlockSpec((None,) + g.shape[1:], lambda k, core: (2 * k + core[0], 0, 0)) for g in gs]
    slot = [pl.BlockSpec((None,) + g.shape[1:], lambda k, core: (k, 0, 0)) for g in gs]
    return pl.pallas_call(
        body, name=name, out_shape=[SDS((4,) + g.shape[1:], bf16) for g in gs],
        grid_spec=pltpu.PrefetchScalarGridSpec(num_scalar_prefetch=1, grid=(4,), in_specs=mine + slot, out_specs=slot),
        compiler_params=_cp(dimension_semantics=("arbitrary",)),
    )(core, *gs, *r1s)


def _adamw(w, g, m, v):
    m = ADAM_B1 * m + (1.0 - ADAM_B1) * g
    v = ADAM_B2 * v + (1.0 - ADAM_B2) * (g * g)
    m_hat = m / (1.0 - ADAM_B1 ** ADAM_STEP)
    v_hat = v / (1.0 - ADAM_B2 ** ADAM_STEP)
    delta = -ADAM_LR * (m_hat / (jnp.sqrt(v_hat) + ADAM_EPS) + ADAM_WD * w)
    return delta, m, v


ADAM_STEPS = 4


def adam_big(ws, ms, vs, parts, r2s, chip, name, layer=0, prev=None):
    n = len(ws)

    def body(chip_ref, *refs):
        outs = refs[len(refs) - 4 * n:]
        for a in range(n):
            w_ref, m_ref, v_ref, p_ref, r_ref = (refs[k * n + a] for k in range(5))
            g = p_ref[...].astype(f32) + r_ref[0].astype(f32) + r_ref[1].astype(f32) + r_ref[2].astype(f32)
            d, m_, v_ = _adamw(w_ref[...], g, m_ref[...], v_ref[...])
            for o_ref, val in zip(outs[4 * a:4 * a + 4], (g, d, m_, v_)):
                o_ref[...] = val

    tiles = [(w.shape[1] // ADAM_STEPS, w.shape[2]) for w in ws]
    blk = [pl.BlockSpec((None,) + t, lambda i, chip: (layer, i, 0)) for t in tiles]
    extra = [] if prev is None else [arr for four in prev for arr in four]
    res = pl.pallas_call(
        body, name=name, out_shape=[SDS(w.shape, f32) for w in ws for _ in range(4)],
        grid_spec=pltpu.PrefetchScalarGridSpec(
            num_scalar_prefetch=1, grid=(ADAM_STEPS,),
            in_specs=blk * 3 + [pl.BlockSpec((None,) + t, lambda i, chip: (chip[0], i, 0)) for t in tiles]
            + [pl.BlockSpec((3,) + t, lambda i, chip: (0, i, 0)) for t in tiles] + [_ANY] * len(extra),
            out_specs=[b for b in blk for _ in range(4)]),
        input_output_aliases={1 + 5 * n + k: k for k in range(len(extra))},
        compiler_params=_cp(dimension_semantics=("arbitrary",)),
    )(chip, *ws, *ms, *vs, *parts, *r2s, *extra)
    return [tuple(res[4 * a:4 * a + 4]) for a in range(n)]


SMALL_BUF_ROWS = 288


def allreduce_small(rows8, b_glu, b_kv, misc, s5):
    R = SMALL_BUF_ROWS
    half, quarter = R // 2, R // 4
    pieces = [*rows8, b_glu, b_kv, misc, *s5]

    def body(*refs):
        ins, (out_ref, in_ref, acc1, acc2, r0, r1, r2, send_sems, recv_sems) = refs[:len(pieces)], refs[len(pieces):]
        in_ref[8:16, :] = jnp.zeros((8, D), f32)
        in_ref[R - 8:R, :] = jnp.zeros((8, D), f32)
        for k in range(len(rows8)):
            in_ref[k:k + 1, :] = ins[k][0:1, :]
        glu_ref, kv_ref, misc_ref = ins[len(rows8):len(rows8) + 3]
        in_ref[9:10, :] = glu_ref[0:1, 0:D]
        in_ref[10:11, :] = glu_ref[0:1, D:2 * D]
        in_ref[11:12, 0:kv_ref.shape[1]] = kv_ref[0:1, :]
        in_ref[12:15, :] = misc_ref[...]
        row = 16
        for a in ins[len(rows8) + 3:]:
            in_ref[row:row + a.shape[0], :] = a[...]
            row += a.shape[0]
        x, y, c = _pos()
        sibling, over_x, over_y = (x, y, 1 - c), (1 - x, y, c), (x, 1 - y, c)
        first = pl.multiple_of(c * half, 8)
        mine = pl.ds(first, half)
        theirs = pl.ds(pl.multiple_of((1 - c) * half, 8), half)
        qa = pl.ds(first, quarter)
        qb = pl.ds(pl.multiple_of(first + quarter, 8), quarter)

        def exchange(copies):
            cps = [pltpu.make_async_remote_copy(
                src_ref=src.at[rows], dst_ref=dst.at[rows], send_sem=send_sems.at[k], recv_sem=recv_sems.at[k],
                device_id=peer, device_id_type=MESH) for k, src, dst, rows, peer in copies]
            for cp in cps:
                cp.start()
            for cp in cps:
                cp.wait()

        exchange([(0, in_ref, r0, theirs, sibling)])
        acc1[mine, :] = in_ref[mine, :] + r0[mine, :]
        exchange([(1, acc1, r1, qa, over_x), (2, acc1, r1, qb, over_y)])
        acc2[mine, :] = acc1[mine, :] + r1[mine, :]
        exchange([(3, acc2, r2, qa, over_y), (4, acc2, r2, qb, over_x)])
        out_ref[mine, :] = acc2[mine, :] + r2[mine, :]
        exchange([(5, out_ref, out_ref, mine, sibling)])

    vm = pl.BlockSpec(memory_space=pltpu.VMEM)
    return pl.pallas_call(
        body, name="allreduce_small", in_specs=[vm] * len(pieces), out_specs=vm, out_shape=SDS((R, D), f32),
        scratch_shapes=[pltpu.VMEM((R, D), f32)] * 6 + [pltpu.SemaphoreType.DMA((6,)), pltpu.SemaphoreType.DMA((6,))],
    )(*pieces)


SMALL_ROWS = {'norm_mix': (0, 2, D), 'norm_mlp': (2, 2, D), 'norm_kv': (4, 1, D), 'norm_final': (5, 1, D),
              's5_d': (6, 1, D), 'b_q': (7, 1, D), 'b_o': (8, 1, D), 's5_b_glu': (9, 2, D), 'b_kv': (11, 1, 512),
              'sinks': (12, 1, 16), 's5_log_dt': (13, 1, 64), 's5_a_re': (16, 4, D), 's5_a_im': (20, 4, D),
              's5_b_re': (24, 64, D), 's5_b_im': (88, 64, D), 's5_c_re': (152, 64, D), 's5_c_im': (216, 64, D)}
LOSS_ROW = 14
ROW_PARAMS = ['norm_mix', 'norm_mlp', 'norm_kv', 'norm_final', 'b_q', 'b_o', 'b_kv', 'sinks', 's5_log_dt']
SHARD_PARAMS = ['s5_d', 's5_b_glu']
S5_PARAMS = ['s5_a_re', 's5_a_im', 's5_b_re', 's5_b_im', 's5_c_re', 's5_c_im']


def adam_small(dev, gsum, s5_grads, w, m, v):
    names = ROW_PARAMS + SHARD_PARAMS + S5_PARAMS
    n_g = len(ROW_PARAMS) + len(SHARD_PARAMS)

    def body(dev_ref, gs_ref, *refs):
        pos = [0]

        def take(k):
            r = refs[pos[0]:pos[0] + k]
            pos[0] += k
            return r

        g5 = take(len(S5_PARAMS))
        wr, mr, vr = take(len(names)), take(len(names)), take(len(names))
        g_out = take(n_g)
        d_out, m_out, v_out = take(len(names)), take(len(names)), take(len(names))
        dv = dev_ref[0]
        for i, n in enumerate(names):
            if n in S5_PARAMS:
                g = g5[S5_PARAMS.index(n)][...]
            elif n in SHARD_PARAMS:
                r0, _, _ = SMALL_ROWS[n]
                ln = wr[i].shape[1]
                g = jnp.zeros((1, ln), f32)
                for k in range(NDEV):
                    off = k * ln
                    piece = gs_ref[r0 + off // D:r0 + off // D + 1, off % D:off % D + ln]
                    g = g + jnp.where(dv == k, piece, 0.0)
                g_out[i][...] = g
            else:
                r0, nr, nl = SMALL_ROWS[n]
                g = gs_ref[r0:r0 + nr, 0:nl]
                g_out[i][...] = g
            d, m_, v_ = _adamw(wr[i][...], g, mr[i][...], vr[i][...])
            d_out[i][...] = d
            m_out[i][...] = m_
            v_out[i][...] = v_

    vm = pl.BlockSpec(memory_space=pltpu.VMEM)
    ins = [s5_grads[n] for n in S5_PARAMS] + [d[n] for d in (w, m, v) for n in names]
    shapes = [SDS(w[n].shape, f32) for n in names]
    res = pl.pallas_call(
        body, name="adam_small", in_specs=[pl.BlockSpec(memory_space=pltpu.SMEM)] + [vm] * (1 + len(ins)),
        out_specs=[vm] * (n_g + 3 * len(names)), out_shape=shapes[:n_g] + shapes * 3,
        compiler_params=_cp(),
    )(dev, gsum, *ins)
    g_o = dict(zip(names[:n_g], res[:n_g]))
    rest = res[n_g:]
    k = len(names)
    return g_o, dict(zip(names, rest[:k])), dict(zip(names, rest[k:2 * k])), dict(zip(names, rest[2 * k:]))


WEIGHTS = ['norm_mix', 'norm_mlp', 'norm_kv', 'norm_final', 's5_a_re', 's5_a_im', 's5_log_dt', 's5_b_re', 's5_b_im',
           's5_c_re', 's5_c_im', 's5_d', 's5_w_glu', 's5_b_glu', 'w_kv', 'b_kv', 'w_q', 'b_q', 'sinks', 'w_o', 'b_o',
           'w_mlp_in', 'w_mlp_out']
BIG = ['s5_w_glu', 'w_kv', 'w_q', 'w_o', 'w_mlp_in', 'w_mlp_out']
BIG_2D = {'s5_w_glu': (D, 256), 'w_kv': (128, 512), 'w_q': (128, D), 'w_o': (128, D), 'w_mlp_in': (2 * D, 512),
          'w_mlp_out': (2 * 512, D)}
SMALL = [n for n in WEIGHTS if n not in BIG]


def kernel(x, norm_mix, norm_mlp, norm_kv, norm_final, s5_a_re, s5_a_im, s5_log_dt, s5_b_re, s5_b_im, s5_c_re, s5_c_im, s5_d, s5_w_glu, s5_b_glu, w_kv, b_kv, w_q, b_q, sinks, w_o, b_o, w_mlp_in, w_mlp_out, loss_target, m_norm_mix, m_norm_mlp, m_norm_kv, m_norm_final, m_s5_a_re, m_s5_a_im, m_s5_log_dt, m_s5_b_re, m_s5_b_im, m_s5_c_re, m_s5_c_im, m_s5_d, m_s5_w_glu, m_s5_b_glu, m_w_kv, m_b_kv, m_w_q, m_b_q, m_sinks, m_w_o, m_b_o, m_w_mlp_in, m_w_mlp_out, v_norm_mix, v_norm_mlp, v_norm_kv, v_norm_final, v_s5_a_re, v_s5_a_im, v_s5_log_dt, v_s5_b_re, v_s5_b_im, v_s5_c_re, v_s5_c_im, v_s5_d, v_s5_w_glu, v_s5_b_glu, v_w_kv, v_b_kv, v_w_q, v_b_q, v_sinks, v_w_o, v_b_o, v_w_mlp_in, v_w_mlp_out):
    w = dict(norm_mix=norm_mix, norm_mlp=norm_mlp, norm_kv=norm_kv, norm_final=norm_final, s5_a_re=s5_a_re,
             s5_a_im=s5_a_im, s5_log_dt=s5_log_dt, s5_b_re=s5_b_re, s5_b_im=s5_b_im, s5_c_re=s5_c_re, s5_c_im=s5_c_im,
             s5_d=s5_d, s5_w_glu=s5_w_glu, s5_b_glu=s5_b_glu, w_kv=w_kv, b_kv=b_kv, w_q=w_q, b_q=b_q, sinks=sinks,
             w_o=w_o, b_o=b_o, w_mlp_in=w_mlp_in, w_mlp_out=w_mlp_out)
    m = dict(norm_mix=m_norm_mix, norm_mlp=m_norm_mlp, norm_kv=m_norm_kv, norm_final=m_norm_final, s5_a_re=m_s5_a_re,
             s5_a_im=m_s5_a_im, s5_log_dt=m_s5_log_dt, s5_b_re=m_s5_b_re, s5_b_im=m_s5_b_im, s5_c_re=m_s5_c_re,
             s5_c_im=m_s5_c_im, s5_d=m_s5_d, s5_w_glu=m_s5_w_glu, s5_b_glu=m_s5_b_glu, w_kv=m_w_kv, b_kv=m_b_kv,
             w_q=m_w_q, b_q=m_b_q, sinks=m_sinks, w_o=m_w_o, b_o=m_b_o, w_mlp_in=m_w_mlp_in, w_mlp_out=m_w_mlp_out)
    v = dict(norm_mix=v_norm_mix, norm_mlp=v_norm_mlp, norm_kv=v_norm_kv, norm_final=v_norm_final, s5_a_re=v_s5_a_re,
             s5_a_im=v_s5_a_im, s5_log_dt=v_s5_log_dt, s5_b_re=v_s5_b_re, s5_b_im=v_s5_b_im, s5_c_re=v_s5_c_re,
             s5_c_im=v_s5_c_im, s5_d=v_s5_d, s5_w_glu=v_s5_w_glu, s5_b_glu=v_s5_b_glu, w_kv=v_w_kv, b_kv=v_b_kv,
             w_q=v_w_q, b_q=v_b_q, sinks=v_sinks, w_o=v_w_o, b_o=v_b_o, w_mlp_in=v_w_mlp_in, w_mlp_out=v_w_mlp_out)
    xi, yi, ci = _pos()
    dev = 4 * xi + 2 * yi + ci
    core = ci.reshape(1).astype(jnp.int32)
    chip = (2 * xi + yi).reshape(1).astype(jnp.int32)

    shards = {
        "s5_w_glu": s5_w_glu[0].astype(bf16), "w_kv": w_kv.astype(bf16), "w_q": w_q[0].astype(bf16),
        "w_o": w_o[0].astype(bf16), "w_in0": w_mlp_in[0].astype(bf16), "w_in1": w_mlp_in[1].astype(bf16),
        "w_out0": w_mlp_out[0].astype(bf16), "w_out1": w_mlp_out[1].astype(bf16),
        "vecs": jnp.broadcast_to(jnp.concatenate([s5_d, s5_b_glu], axis=1), (8, 384)),
    }
    as3d = lambda a, n: a if a.ndim == 3 and a.shape[0] == 2 else a.reshape((1,) + BIG_2D[n])
    opt = {n: (as3d(w[n], n), as3d(m[n], n), as3d(v[n], n)) for n in BIG}
    _, grad_x, grads, big = fwd_bwd(x[0], loss_target[0], {n: w[n] for n in SMALL}, shards, opt, core, chip)

    gsum = allreduce_small(**grads)

    out_g, out_d, out_m, out_v = {}, {}, {}, {}
    for n in BIG:
        out_g[n], out_d[n], out_m[n], out_v[n] = [r.reshape(w[n].shape) for r in big[n]]

    loss = gsum[LOSS_ROW, 0]
    swapped = ("s5_b_re", "s5_b_im")
    swap = lambda a: a.transpose(0, 1, 3, 2)

    def kernel_side(d):
        d = {n: (d[n].reshape(1, -1) if d[n].ndim == 1 else d[n]) for n in SMALL}
        d.update({n: swap(d[n]) for n in swapped})
        return d

    s5_g = {}
    for n in S5_PARAMS:
        r0, nr, _ = SMALL_ROWS[n]
        s5_g[n] = gsum[r0:r0 + nr].reshape((1, 64, 16, 64) if n in swapped else w[n].shape)
        out_g[n] = s5_g[n]
    g_s, d_s, m_s, v_s = adam_small(dev.reshape(1).astype(jnp.int32), gsum, s5_g, kernel_side(w), kernel_side(m),
                                    kernel_side(v))
    for src, dst in ((g_s, out_g), (d_s, out_d), (m_s, out_m), (v_s, out_v)):
        dst.update(src)
    for dst in (out_g, out_d, out_m, out_v):
        for n in SMALL:
            dst[n] = (swap(dst[n]) if n in swapped else dst[n]).reshape(w[n].shape)

    return (loss, grad_x[None], *[out_g[n] for n in WEIGHTS], *[out_d[n] for n in WEIGHTS],
            *[out_m[n] for n in WEIGHTS], *[out_v[n] for n in WEIGHTS])
```
